```python
import jax, jax.numpy as jnp
from jax import lax
import numpy as np

D_MODEL = 1024
BATCH = 8
SEQ = 2048
DEPTH = 2

EPS = 1e-6
N_EVEN = (DEPTH + 1) // 2
N_ODD = DEPTH // 2

A_HEADS = 8
A_HEAD_DIM = 64
A_WIDTH = A_HEADS * A_HEAD_DIM
CONV_WIDTH = 3
POOL_WINDOWS = (2, 4, 8, 16)
B_GROUPS = len(POOL_WINDOWS)
B_GROUP_DIM = 128
B_WIDTH = B_GROUPS * B_GROUP_DIM
AB_IN = 3 * A_WIDTH + B_WIDTH
AB_OUT = A_WIDTH + B_WIDTH

C_HEADS = 8
C_NOPE = 64
C_ROPE = 32
C_V = 64
C_Q_RANK = 256
C_KV_RANK = 128
C_WIDTH = C_HEADS * C_V
ROPE_THETA = 10000.0
ATTN_BLOCK = 128
D_GROUPS = 4
D_GROUP_DIM = 128
D_WIDTH = D_GROUPS * D_GROUP_DIM
D_CHUNK = 128
CD_IN = C_Q_RANK + C_KV_RANK + C_ROPE + 2 * D_WIDTH
CD_OUT = C_WIDTH + D_WIDTH

D_FF = 2816
N_MOD = 6

kernel_name = "hybrid_conv_pool_mla_gmlp_block"


def rms_norm(x, g):
    xf = x.astype(jnp.float32)
    y = xf * lax.rsqrt(jnp.mean(xf * xf, axis=-1, keepdims=True) + EPS)
    return (y * g.astype(jnp.float32)).astype(x.dtype)


def layer_norm(x, g, b):
    xf = x.astype(jnp.float32)
    mu = jnp.mean(xf, axis=-1, keepdims=True)
    xc = xf - mu
    y = xc * lax.rsqrt(jnp.mean(xc * xc, axis=-1, keepdims=True) + EPS)
    return (y * g.astype(jnp.float32) + b.astype(jnp.float32)).astype(x.dtype)


def causal_dwconv(x, w):
    K, C = w.shape
    return lax.conv_general_dilated(
        x, w[:, None, :].astype(x.dtype), window_strides=(1,),
        padding=((K - 1, 0),), dimension_numbers=("NWC", "WIO", "NWC"),
        feature_group_count=C)


def rope_tables(positions):
    half = C_ROPE // 2
    inv_freq = ROPE_THETA ** (-jnp.arange(half, dtype=jnp.float32) / half)
    ang = positions.astype(jnp.float32)[..., None] * inv_freq
    return jnp.cos(ang), jnp.sin(ang)


def apply_rope(x, cos, sin):
    if x.ndim == 4:
        cos, sin = cos[:, :, None, :], sin[:, :, None, :]
    xf = x.astype(jnp.float32)
    x1, x2 = jnp.split(xf, 2, axis=-1)
    return jnp.concatenate([x1 * cos - x2 * sin, x2 * cos + x1 * sin], axis=-1).astype(x.dtype)


def short_gated_conv(b_gate, c_gate, h, conv_w):
    return b_gate * causal_dwconv(c_gate * h, conv_w)


def multiscale_pool(p, mix_w, scale):
    Bn, S, _ = p.shape
    pg = p.reshape(Bn, S, B_GROUPS, B_GROUP_DIM)
    cs = jnp.cumsum(pg.astype(jnp.float32), axis=1)
    t = jnp.arange(S)
    pooled = []
    for g, w in enumerate(POOL_WINDOWS):
        csg = cs[:, :, g]
        lag = jnp.pad(csg[:, :S - w], ((0, 0), (w, 0), (0, 0)))
        cnt = jnp.minimum(t + 1, w).astype(jnp.float32)[None, :, None]
        pooled.append((csg - lag) / cnt)
    pooled = jnp.stack(pooled, axis=2).astype(p.dtype) - pg
    y = jnp.einsum("bsgc,gcd->bsgd", pooled, mix_w)
    return y.reshape(Bn, S, B_WIDTH) * scale


def latent_attention(q_lat, kv_lat, k_pe, q_norm_g, w_uq, kv_norm_g, w_ukv, cos, sin):
    Bn, S, _ = q_lat.shape
    q = (rms_norm(q_lat, q_norm_g) @ w_uq).reshape(Bn, S, C_HEADS, C_NOPE + C_ROPE)
    q_nope, q_pe = q[..., :C_NOPE], apply_rope(q[..., C_NOPE:], cos, sin)
    kv = (rms_norm(kv_lat, kv_norm_g) @ w_ukv).reshape(Bn, S, C_HEADS, C_NOPE + C_V)
    k_nope, v = kv[..., :C_NOPE], kv[..., C_NOPE:]
    k_pe = apply_rope(k_pe, cos, sin)
    scale = (C_NOPE + C_ROPE) ** -0.5
    nb = S // ATTN_BLOCK
    qn_blocks = q_nope.reshape(Bn, nb, ATTN_BLOCK, C_HEADS, C_NOPE).transpose(1, 0, 2, 3, 4)
    qp_blocks = q_pe.reshape(Bn, nb, ATTN_BLOCK, C_HEADS, C_ROPE).transpose(1, 0, 2, 3, 4)
    key_pos = jnp.arange(S)

    def one_block(args):
        i, qn, qp = args
        s = (jnp.einsum("bqhd,bkhd->bhqk", qn, k_nope, preferred_element_type=jnp.float32)
             + jnp.einsum("bqhr,bkr->bhqk", qp, k_pe, preferred_element_type=jnp.float32)) * scale
        q_pos = i * ATTN_BLOCK + jnp.arange(ATTN_BLOCK)
        s = jnp.where(key_pos[None, :] <= q_pos[:, None], s, -jnp.inf)
        prob = jax.nn.softmax(s, axis=-1).astype(v.dtype)
        return jnp.einsum("bhqk,bkhd->bqhd", prob, v)

    out = lax.map(one_block, (jnp.arange(nb), qn_blocks, qp_blocks))
    return out.transpose(1, 0, 2, 3, 4).reshape(Bn, S, C_WIDTH)


def spatial_gating(u, v, ln_g, ln_b, w_s, b_s):
    Bn, S, _ = v.shape
    v = layer_norm(v, ln_g, ln_b)
    vc = v.reshape(Bn, S // D_CHUNK, D_CHUNK, D_GROUPS, D_GROUP_DIM)
    mask = jnp.tril(jnp.ones((D_CHUNK, D_CHUNK), dtype=bool))
    w = jnp.where(mask[None], w_s, 0)
    mixed = jnp.einsum("gts,bnsgc->bntgc", w, vc) + b_s.T[None, None, :, :, None]
    return u * mixed.reshape(Bn, S, D_WIDTH)


def conv_ffn(h, w_up, conv_w, w_down):
    z = causal_dwconv(h @ w_up, conv_w)
    g, u = jnp.split(z, 2, axis=-1)
    return (jax.nn.silu(g) * u) @ w_down


def _fwd_setup_inputs(seed: int = 0) -> dict:
    key = jax.random.key(seed)
    ks = iter(jax.random.split(key, 32))
    nrm = lambda shape, s: jax.random.normal(next(ks), shape, jnp.float32) * s
    gain = lambda shape: 1.0 + nrm(shape, 0.05)
    D = D_MODEL
    offsets = jax.random.randint(next(ks), (BATCH, 1), 0, 4096, dtype=jnp.int32)
    positions = (jnp.arange(SEQ, dtype=jnp.int32)[None, :] + offsets).astype(jnp.int32)
    return {
        "x": nrm((BATCH, SEQ, D), 1.0),
        "c": nrm((BATCH, D), 1.0),
        "positions": positions,
        "ada_w": nrm((DEPTH, D, N_MOD * D), 0.5 * D ** -0.5),
        "ada_b": nrm((DEPTH, N_MOD * D), 0.02),
        "norm1_g": gain((DEPTH, D)),
        "norm2_g": gain((DEPTH, D)),
        "ab_w_in": nrm((N_EVEN, D, AB_IN), D ** -0.5),
        "a_conv_w": nrm((N_EVEN, CONV_WIDTH, A_WIDTH), CONV_WIDTH ** -0.5),
        "b_mix_w": nrm((N_EVEN, B_GROUPS, B_GROUP_DIM, B_GROUP_DIM), B_GROUP_DIM ** -0.5),
        "b_scale": 1.0 + nrm((N_EVEN, B_WIDTH), 0.1),
        "ab_w_out": nrm((N_EVEN, AB_OUT, D), AB_OUT ** -0.5),
        "cd_w_in": nrm((N_ODD, D, CD_IN), D ** -0.5),
        "c_q_norm_g": gain((N_ODD, C_Q_RANK)),
        "c_w_uq": nrm((N_ODD, C_Q_RANK, C_HEADS * (C_NOPE + C_ROPE)), C_Q_RANK ** -0.5),
        "c_kv_norm_g": gain((N_ODD, C_KV_RANK)),
        "c_w_ukv": nrm((N_ODD, C_KV_RANK, C_HEADS * (C_NOPE + C_V)), C_KV_RANK ** -0.5),
        "d_ln_g": gain((N_ODD, D_WIDTH)),
        "d_ln_b": nrm((N_ODD, D_WIDTH), 0.02),
        "d_w_s": nrm((N_ODD, D_GROUPS, D_CHUNK, D_CHUNK), 0.5 * D_CHUNK ** -0.5),
        "d_b_s": 1.0 + nrm((N_ODD, D_GROUPS, D_CHUNK), 0.02),
        "cd_w_out": nrm((N_ODD, CD_OUT, D), CD_OUT ** -0.5),
        "ffn_w_up": nrm((DEPTH, D, 2 * D_FF), D ** -0.5),
        "ffn_conv_w": nrm((DEPTH, CONV_WIDTH, 2 * D_FF), CONV_WIDTH ** -0.5),
        "ffn_w_down": nrm((DEPTH, D_FF, D), D_FF ** -0.5),
        "final_norm_g": gain((D,)),
    }


def _fwd_reference(x, c, positions, ada_w, ada_b, norm1_g, norm2_g,
              ab_w_in, a_conv_w, b_mix_w, b_scale, ab_w_out,
              cd_w_in, c_q_norm_g, c_w_uq, c_kv_norm_g, c_w_ukv,
              d_ln_g, d_ln_b, d_w_s, d_b_s, cd_w_out,
              ffn_w_up, ffn_conv_w, ffn_w_down, final_norm_g):
    cos, sin = rope_tables(positions)
    c_act = jax.nn.silu(c)
    for l in range(DEPTH):
        mod = c_act @ ada_w[l] + ada_b[l]
        sh1, sc1, g1, sh2, sc2, g2 = [m[:, None, :] for m in jnp.split(mod, N_MOD, axis=-1)]
        h = rms_norm(x, norm1_g[l]) * (1 + sc1) + sh1
        i = l // 2
        if l % 2 == 0:
            z = h @ ab_w_in[i]
            b_gate, c_gate, a_in, p = jnp.split(z, [A_WIDTH, 2 * A_WIDTH, 3 * A_WIDTH], axis=-1)
            y_a = short_gated_conv(b_gate, c_gate, a_in, a_conv_w[i])
            y_b = multiscale_pool(p, b_mix_w[i], b_scale[i])
            y = jnp.concatenate([y_a, y_b], axis=-1) @ ab_w_out[i]
        else:
            z = h @ cd_w_in[i]
            q_lat, kv_lat, k_pe, uv = jnp.split(
                z, [C_Q_RANK, C_Q_RANK + C_KV_RANK, C_Q_RANK + C_KV_RANK + C_ROPE], axis=-1)
            y_c = latent_attention(q_lat, kv_lat, k_pe, c_q_norm_g[i], c_w_uq[i],
                                   c_kv_norm_g[i], c_w_ukv[i], cos, sin)
            u, v = jnp.split(jax.nn.gelu(uv), 2, axis=-1)
            y_d = spatial_gating(u, v, d_ln_g[i], d_ln_b[i], d_w_s[i], d_b_s[i])
            y = jnp.concatenate([y_c, y_d], axis=-1) @ cd_w_out[i]
        x = x + g1 * y
        h = rms_norm(x, norm2_g[l]) * (1 + sc2) + sh2
        x = x + g2 * conv_ffn(h, ffn_w_up[l], ffn_conv_w[l], ffn_w_down[l])
    return rms_norm(x, final_norm_g)


import jax as _jax
import jax.numpy as _jnp

TWIN_FORMAT = 'train_step'
FWD_PARAMS = ['x', 'c', 'positions', 'ada_w', 'ada_b', 'norm1_g', 'norm2_g', 'ab_w_in', 'a_conv_w', 'b_mix_w', 'b_scale', 'ab_w_out', 'cd_w_in', 'c_q_norm_g', 'c_w_uq', 'c_kv_norm_g', 'c_w_ukv', 'd_ln_g', 'd_ln_b', 'd_w_s', 'd_b_s', 'cd_w_out', 'ffn_w_up', 'ffn_conv_w', 'ffn_w_down', 'final_norm_g']
TWIN_WEIGHTS = ['ada_w', 'ada_b', 'norm1_g', 'norm2_g', 'ab_w_in', 'a_conv_w', 'b_mix_w', 'b_scale', 'ab_w_out', 'cd_w_in', 'c_q_norm_g', 'c_w_uq', 'c_kv_norm_g', 'c_w_ukv', 'd_ln_g', 'd_ln_b', 'd_w_s', 'd_b_s', 'cd_w_out', 'ffn_w_up', 'ffn_conv_w', 'ffn_w_down', 'final_norm_g']
TWIN_DIFF_INPUT = 'x'
TWIN_INPUTS = ['x', 'c', 'positions', 'ada_w', 'ada_b', 'norm1_g', 'norm2_g', 'ab_w_in', 'a_conv_w', 'b_mix_w', 'b_scale', 'ab_w_out', 'cd_w_in', 'c_q_norm_g', 'c_w_uq', 'c_kv_norm_g', 'c_w_ukv', 'd_ln_g', 'd_ln_b', 'd_w_s', 'd_b_s', 'cd_w_out', 'ffn_w_up', 'ffn_conv_w', 'ffn_w_down', 'final_norm_g', 'loss_target', 'm_ada_w', 'm_ada_b', 'm_norm1_g', 'm_norm2_g', 'm_ab_w_in', 'm_a_conv_w', 'm_b_mix_w', 'm_b_scale', 'm_ab_w_out', 'm_cd_w_in', 'm_c_q_norm_g', 'm_c_w_uq', 'm_c_kv_norm_g', 'm_c_w_ukv', 'm_d_ln_g', 'm_d_ln_b', 'm_d_w_s', 'm_d_b_s', 'm_cd_w_out', 'm_ffn_w_up', 'm_ffn_conv_w', 'm_ffn_w_down', 'm_final_norm_g', 'v_ada_w', 'v_ada_b', 'v_norm1_g', 'v_norm2_g', 'v_ab_w_in', 'v_a_conv_w', 'v_b_mix_w', 'v_b_scale', 'v_ab_w_out', 'v_cd_w_in', 'v_c_q_norm_g', 'v_c_w_uq', 'v_c_kv_norm_g', 'v_c_w_ukv', 'v_d_ln_g', 'v_d_ln_b', 'v_d_w_s', 'v_d_b_s', 'v_cd_w_out', 'v_ffn_w_up', 'v_ffn_conv_w', 'v_ffn_w_down', 'v_final_norm_g']
TWIN_OUTPUTS = ['loss', 'grad_x', 'grad_ada_w', 'grad_ada_b', 'grad_norm1_g', 'grad_norm2_g', 'grad_ab_w_in', 'grad_a_conv_w', 'grad_b_mix_w', 'grad_b_scale', 'grad_ab_w_out', 'grad_cd_w_in', 'grad_c_q_norm_g', 'grad_c_w_uq', 'grad_c_kv_norm_g', 'grad_c_w_ukv', 'grad_d_ln_g', 'grad_d_ln_b', 'grad_d_w_s', 'grad_d_b_s', 'grad_cd_w_out', 'grad_ffn_w_up', 'grad_ffn_conv_w', 'grad_ffn_w_down', 'grad_final_norm_g', 'delta_ada_w', 'delta_ada_b', 'delta_norm1_g', 'delta_norm2_g', 'delta_ab_w_in', 'delta_a_conv_w', 'delta_b_mix_w', 'delta_b_scale', 'delta_ab_w_out', 'delta_cd_w_in', 'delta_c_q_norm_g', 'delta_c_w_uq', 'delta_c_kv_norm_g', 'delta_c_w_ukv', 'delta_d_ln_g', 'delta_d_ln_b', 'delta_d_w_s', 'delta_d_b_s', 'delta_cd_w_out', 'delta_ffn_w_up', 'delta_ffn_conv_w', 'delta_ffn_w_down', 'delta_final_norm_g', 'new_m_ada_w', 'new_m_ada_b', 'new_m_norm1_g', 'new_m_norm2_g', 'new_m_ab_w_in', 'new_m_a_conv_w', 'new_m_b_mix_w', 'new_m_b_scale', 'new_m_ab_w_out', 'new_m_cd_w_in', 'new_m_c_q_norm_g', 'new_m_c_w_uq', 'new_m_c_kv_norm_g', 'new_m_c_w_ukv', 'new_m_d_ln_g', 'new_m_d_ln_b', 'new_m_d_w_s', 'new_m_d_b_s', 'new_m_cd_w_out', 'new_m_ffn_w_up', 'new_m_ffn_conv_w', 'new_m_ffn_w_down', 'new_m_final_norm_g', 'new_v_ada_w', 'new_v_ada_b', 'new_v_norm1_g', 'new_v_norm2_g', 'new_v_ab_w_in', 'new_v_a_conv_w', 'new_v_b_mix_w', 'new_v_b_scale', 'new_v_ab_w_out', 'new_v_cd_w_in', 'new_v_c_q_norm_g', 'new_v_c_w_uq', 'new_v_c_kv_norm_g', 'new_v_c_w_ukv', 'new_v_d_ln_g', 'new_v_d_ln_b', 'new_v_d_w_s', 'new_v_d_b_s', 'new_v_cd_w_out', 'new_v_ffn_w_up', 'new_v_ffn_conv_w', 'new_v_ffn_w_down', 'new_v_final_norm_g']
TWIN_LEAF_KINDS = {'loss': 'loss', 'grad_x': 'grad_x', 'grad_ada_w': 'grad_w', 'grad_ada_b': 'grad_w', 'grad_norm1_g': 'grad_w', 'grad_norm2_g': 'grad_w', 'grad_ab_w_in': 'grad_w', 'grad_a_conv_w': 'grad_w', 'grad_b_mix_w': 'grad_w', 'grad_b_scale': 'grad_w', 'grad_ab_w_out': 'grad_w', 'grad_cd_w_in': 'grad_w', 'grad_c_q_norm_g': 'grad_w', 'grad_c_w_uq': 'grad_w', 'grad_c_kv_norm_g': 'grad_w', 'grad_c_w_ukv': 'grad_w', 'grad_d_ln_g': 'grad_w', 'grad_d_ln_b': 'grad_w', 'grad_d_w_s': 'grad_w', 'grad_d_b_s': 'grad_w', 'grad_cd_w_out': 'grad_w', 'grad_ffn_w_up': 'grad_w', 'grad_ffn_conv_w': 'grad_w', 'grad_ffn_w_down': 'grad_w', 'grad_final_norm_g': 'grad_w', 'delta_ada_w': 'delta_w', 'delta_ada_b': 'delta_w', 'delta_norm1_g': 'delta_w', 'delta_norm2_g': 'delta_w', 'delta_ab_w_in': 'delta_w', 'delta_a_conv_w': 'delta_w', 'delta_b_mix_w': 'delta_w', 'delta_b_scale': 'delta_w', 'delta_ab_w_out': 'delta_w', 'delta_cd_w_in': 'delta_w', 'delta_c_q_norm_g': 'delta_w', 'delta_c_w_uq': 'delta_w', 'delta_c_kv_norm_g': 'delta_w', 'delta_c_w_ukv': 'delta_w', 'delta_d_ln_g': 'delta_w', 'delta_d_ln_b': 'delta_w', 'delta_d_w_s': 'delta_w', 'delta_d_b_s': 'delta_w', 'delta_cd_w_out': 'delta_w', 'delta_ffn_w_up': 'delta_w', 'delta_ffn_conv_w': 'delta_w', 'delta_ffn_w_down': 'delta_w', 'delta_final_norm_g': 'delta_w', 'new_m_ada_w': 'new_m', 'new_m_ada_b': 'new_m', 'new_m_norm1_g': 'new_m', 'new_m_norm2_g': 'new_m', 'new_m_ab_w_in': 'new_m', 'new_m_a_conv_w': 'new_m', 'new_m_b_mix_w': 'new_m', 'new_m_b_scale': 'new_m', 'new_m_ab_w_out': 'new_m', 'new_m_cd_w_in': 'new_m', 'new_m_c_q_norm_g': 'new_m', 'new_m_c_w_uq': 'new_m', 'new_m_c_kv_norm_g': 'new_m', 'new_m_c_w_ukv': 'new_m', 'new_m_d_ln_g': 'new_m', 'new_m_d_ln_b': 'new_m', 'new_m_d_w_s': 'new_m', 'new_m_d_b_s': 'new_m', 'new_m_cd_w_out': 'new_m', 'new_m_ffn_w_up': 'new_m', 'new_m_ffn_conv_w': 'new_m', 'new_m_ffn_w_down': 'new_m', 'new_m_final_norm_g': 'new_m', 'new_v_ada_w': 'new_v', 'new_v_ada_b': 'new_v', 'new_v_norm1_g': 'new_v', 'new_v_norm2_g': 'new_v', 'new_v_ab_w_in': 'new_v', 'new_v_a_conv_w': 'new_v', 'new_v_b_mix_w': 'new_v', 'new_v_b_scale': 'new_v', 'new_v_ab_w_out': 'new_v', 'new_v_cd_w_in': 'new_v', 'new_v_c_q_norm_g': 'new_v', 'new_v_c_w_uq': 'new_v', 'new_v_c_kv_norm_g': 'new_v', 'new_v_c_w_ukv': 'new_v', 'new_v_d_ln_g': 'new_v', 'new_v_d_ln_b': 'new_v', 'new_v_d_w_s': 'new_v', 'new_v_d_b_s': 'new_v', 'new_v_cd_w_out': 'new_v', 'new_v_ffn_w_up': 'new_v', 'new_v_ffn_conv_w': 'new_v', 'new_v_ffn_w_down': 'new_v', 'new_v_final_norm_g': 'new_v'}


def _forward(args):
    return _fwd_reference(*[args[k] for k in FWD_PARAMS])


def _output_shape():
    out = _jax.eval_shape(lambda: _forward(_fwd_setup_inputs(0)))
    return out.shape, out.dtype

N_MICROBATCH = 1
ADAM_LR = 0.001
ADAM_B1 = 0.9
ADAM_B2 = 0.999
ADAM_EPS = 1e-08
ADAM_WD = 0.01
ADAM_STEP = 10
PER_EXAMPLE_BATCH_AXIS = {'x': 0, 'c': 0, 'positions': 0, 'loss_target': 0}
SHARED_INPUTS = []
_WEIGHT_DTYPES = {'ada_w': _jnp.float32, 'ada_b': _jnp.float32, 'norm1_g': _jnp.float32, 'norm2_g': _jnp.float32, 'ab_w_in': _jnp.float32, 'a_conv_w': _jnp.float32, 'b_mix_w': _jnp.float32, 'b_scale': _jnp.float32, 'ab_w_out': _jnp.float32, 'cd_w_in': _jnp.float32, 'c_q_norm_g': _jnp.float32, 'c_w_uq': _jnp.float32, 'c_kv_norm_g': _jnp.float32, 'c_w_ukv': _jnp.float32, 'd_ln_g': _jnp.float32, 'd_ln_b': _jnp.float32, 'd_w_s': _jnp.float32, 'd_b_s': _jnp.float32, 'cd_w_out': _jnp.float32, 'ffn_w_up': _jnp.float32, 'ffn_conv_w': _jnp.float32, 'ffn_w_down': _jnp.float32, 'final_norm_g': _jnp.float32}
MOMENT_SCALE = {'ada_w': 5.644376e-02, 'ada_b': 9.796634e-02, 'norm1_g': 4.914881e-02, 'norm2_g': 3.620544e-02, 'ab_w_in': 4.546434e-02, 'a_conv_w': 5.064079e-02, 'b_mix_w': 3.583415e-02, 'b_scale': 3.821590e-02, 'ab_w_out': 4.255923e-02, 'cd_w_in': 2.042105e-02, 'c_q_norm_g': 8.904977e-03, 'c_w_uq': 5.252290e-03, 'c_kv_norm_g': 2.887230e-02, 'c_w_ukv': 9.782261e-03, 'd_ln_g': 9.016989e-03, 'd_ln_b': 9.359236e-03, 'd_w_s': 1.860670e-02, 'd_b_s': 2.556277e-02, 'cd_w_out': 2.221157e-02, 'ffn_w_up': 1.617312e-02, 'ffn_conv_w': 1.610389e-02, 'ffn_w_down': 2.642997e-02, 'final_norm_g': 1.606754e+01}


def _to_microbatches(a, axis):
    t = _jnp.moveaxis(a, axis, 0)
    t = t.reshape((N_MICROBATCH, t.shape[0] // N_MICROBATCH) + t.shape[1:])
    return _jnp.moveaxis(t, 1, axis + 1)


def setup_inputs(seed: int = 0) -> dict:
    inp = _fwd_setup_inputs(seed)
    key = _jax.random.fold_in(_jax.random.key(seed), 7919)
    shape, _ = _output_shape()
    out = dict(inp)
    out["loss_target"] = _jax.random.normal(_jax.random.fold_in(key, 0), shape, _jnp.float32)
    for i, name in enumerate(TWIN_WEIGHTS):
        w = inp[name].astype(_jnp.float32)
        if MOMENT_SCALE is None:
            s = _jnp.sqrt(_jnp.mean(_jnp.square(w)) + 1e-30)
        else:
            s = MOMENT_SCALE[name]
        km, kv = _jax.random.split(_jax.random.fold_in(key, i + 1))
        out[name] = w
        out["m_" + name] = s * _jax.random.normal(km, w.shape, _jnp.float32)
        out["v_" + name] = (s * s) * _jax.random.uniform(kv, w.shape, _jnp.float32, 0.5, 1.5)
    if N_MICROBATCH > 1:
        for name, axis in PER_EXAMPLE_BATCH_AXIS.items():
            out[name] = _to_microbatches(out[name], axis)
    return {'x': out['x'], 'c': out['c'], 'positions': out['positions'], 'ada_w': out['ada_w'], 'ada_b': out['ada_b'], 'norm1_g': out['norm1_g'], 'norm2_g': out['norm2_g'], 'ab_w_in': out['ab_w_in'], 'a_conv_w': out['a_conv_w'], 'b_mix_w': out['b_mix_w'], 'b_scale': out['b_scale'], 'ab_w_out': out['ab_w_out'], 'cd_w_in': out['cd_w_in'], 'c_q_norm_g': out['c_q_norm_g'], 'c_w_uq': out['c_w_uq'], 'c_kv_norm_g': out['c_kv_norm_g'], 'c_w_ukv': out['c_w_ukv'], 'd_ln_g': out['d_ln_g'], 'd_ln_b': out['d_ln_b'], 'd_w_s': out['d_w_s'], 'd_b_s': out['d_b_s'], 'cd_w_out': out['cd_w_out'], 'ffn_w_up': out['ffn_w_up'], 'ffn_conv_w': out['ffn_conv_w'], 'ffn_w_down': out['ffn_w_down'], 'final_norm_g': out['final_norm_g'], 'loss_target': out['loss_target'], 'm_ada_w': out['m_ada_w'], 'm_ada_b': out['m_ada_b'], 'm_norm1_g': out['m_norm1_g'], 'm_norm2_g': out['m_norm2_g'], 'm_ab_w_in': out['m_ab_w_in'], 'm_a_conv_w': out['m_a_conv_w'], 'm_b_mix_w': out['m_b_mix_w'], 'm_b_scale': out['m_b_scale'], 'm_ab_w_out': out['m_ab_w_out'], 'm_cd_w_in': out['m_cd_w_in'], 'm_c_q_norm_g': out['m_c_q_norm_g'], 'm_c_w_uq': out['m_c_w_uq'], 'm_c_kv_norm_g': out['m_c_kv_norm_g'], 'm_c_w_ukv': out['m_c_w_ukv'], 'm_d_ln_g': out['m_d_ln_g'], 'm_d_ln_b': out['m_d_ln_b'], 'm_d_w_s': out['m_d_w_s'], 'm_d_b_s': out['m_d_b_s'], 'm_cd_w_out': out['m_cd_w_out'], 'm_ffn_w_up': out['m_ffn_w_up'], 'm_ffn_conv_w': out['m_ffn_conv_w'], 'm_ffn_w_down': out['m_ffn_w_down'], 'm_final_norm_g': out['m_final_norm_g'], 'v_ada_w': out['v_ada_w'], 'v_ada_b': out['v_ada_b'], 'v_norm1_g': out['v_norm1_g'], 'v_norm2_g': out['v_norm2_g'], 'v_ab_w_in': out['v_ab_w_in'], 'v_a_conv_w': out['v_a_conv_w'], 'v_b_mix_w': out['v_b_mix_w'], 'v_b_scale': out['v_b_scale'], 'v_ab_w_out': out['v_ab_w_out'], 'v_cd_w_in': out['v_cd_w_in'], 'v_c_q_norm_g': out['v_c_q_norm_g'], 'v_c_w_uq': out['v_c_w_uq'], 'v_c_kv_norm_g': out['v_c_kv_norm_g'], 'v_c_w_ukv': out['v_c_w_ukv'], 'v_d_ln_g': out['v_d_ln_g'], 'v_d_ln_b': out['v_d_ln_b'], 'v_d_w_s': out['v_d_w_s'], 'v_d_b_s': out['v_d_b_s'], 'v_cd_w_out': out['v_cd_w_out'], 'v_ffn_w_up': out['v_ffn_w_up'], 'v_ffn_conv_w': out['v_ffn_conv_w'], 'v_ffn_w_down': out['v_ffn_w_down'], 'v_final_norm_g': out['v_final_norm_g']}


def _loss(weights, diff, rest, loss_target):
    with _jax.named_scope("forward"):
        args = {**rest, TWIN_DIFF_INPUT: diff, **{k: w.astype(_WEIGHT_DTYPES[k]) for k, w in weights.items()}}
        y = _forward(args)
    with _jax.named_scope("loss_head"):
        err = _jnp.square(y.astype(_jnp.float32) - loss_target)
        return 0.5 * _jnp.sum(_jnp.mean(err, axis=-1)) if err.ndim else 0.5 * err


def _adamw(w, g, m, v):
    m = ADAM_B1 * m + (1.0 - ADAM_B1) * g
    v = ADAM_B2 * v + (1.0 - ADAM_B2) * _jnp.square(g)
    m_hat = m / (1.0 - ADAM_B1 ** ADAM_STEP)
    v_hat = v / (1.0 - ADAM_B2 ** ADAM_STEP)
    delta = -ADAM_LR * (m_hat / (_jnp.sqrt(v_hat) + ADAM_EPS) + ADAM_WD * w)
    return delta, m, v


def reference(x, c, positions, ada_w, ada_b, norm1_g, norm2_g, ab_w_in, a_conv_w, b_mix_w, b_scale, ab_w_out, cd_w_in, c_q_norm_g, c_w_uq, c_kv_norm_g, c_w_ukv, d_ln_g, d_ln_b, d_w_s, d_b_s, cd_w_out, ffn_w_up, ffn_conv_w, ffn_w_down, final_norm_g, loss_target, m_ada_w, m_ada_b, m_norm1_g, m_norm2_g, m_ab_w_in, m_a_conv_w, m_b_mix_w, m_b_scale, m_ab_w_out, m_cd_w_in, m_c_q_norm_g, m_c_w_uq, m_c_kv_norm_g, m_c_w_ukv, m_d_ln_g, m_d_ln_b, m_d_w_s, m_d_b_s, m_cd_w_out, m_ffn_w_up, m_ffn_conv_w, m_ffn_w_down, m_final_norm_g, v_ada_w, v_ada_b, v_norm1_g, v_norm2_g, v_ab_w_in, v_a_conv_w, v_b_mix_w, v_b_scale, v_ab_w_out, v_cd_w_in, v_c_q_norm_g, v_c_w_uq, v_c_kv_norm_g, v_c_w_ukv, v_d_ln_g, v_d_ln_b, v_d_w_s, v_d_b_s, v_cd_w_out, v_ffn_w_up, v_ffn_conv_w, v_ffn_w_down, v_final_norm_g):
    given = dict(x=x, c=c, positions=positions, ada_w=ada_w, ada_b=ada_b, norm1_g=norm1_g, norm2_g=norm2_g, ab_w_in=ab_w_in, a_conv_w=a_conv_w, b_mix_w=b_mix_w, b_scale=b_scale, ab_w_out=ab_w_out, cd_w_in=cd_w_in, c_q_norm_g=c_q_norm_g, c_w_uq=c_w_uq, c_kv_norm_g=c_kv_norm_g, c_w_ukv=c_w_ukv, d_ln_g=d_ln_g, d_ln_b=d_ln_b, d_w_s=d_w_s, d_b_s=d_b_s, cd_w_out=cd_w_out, ffn_w_up=ffn_w_up, ffn_conv_w=ffn_conv_w, ffn_w_down=ffn_w_down, final_norm_g=final_norm_g, loss_target=loss_target, m_ada_w=m_ada_w, m_ada_b=m_ada_b, m_norm1_g=m_norm1_g, m_norm2_g=m_norm2_g, m_ab_w_in=m_ab_w_in, m_a_conv_w=m_a_conv_w, m_b_mix_w=m_b_mix_w, m_b_scale=m_b_scale, m_ab_w_out=m_ab_w_out, m_cd_w_in=m_cd_w_in, m_c_q_norm_g=m_c_q_norm_g, m_c_w_uq=m_c_w_uq, m_c_kv_norm_g=m_c_kv_norm_g, m_c_w_ukv=m_c_w_ukv, m_d_ln_g=m_d_ln_g, m_d_ln_b=m_d_ln_b, m_d_w_s=m_d_w_s, m_d_b_s=m_d_b_s, m_cd_w_out=m_cd_w_out, m_ffn_w_up=m_ffn_w_up, m_ffn_conv_w=m_ffn_conv_w, m_ffn_w_down=m_ffn_w_down, m_final_norm_g=m_final_norm_g, v_ada_w=v_ada_w, v_ada_b=v_ada_b, v_norm1_g=v_norm1_g, v_norm2_g=v_norm2_g, v_ab_w_in=v_ab_w_in, v_a_conv_w=v_a_conv_w, v_b_mix_w=v_b_mix_w, v_b_scale=v_b_scale, v_ab_w_out=v_ab_w_out, v_cd_w_in=v_cd_w_in, v_c_q_norm_g=v_c_q_norm_g, v_c_w_uq=v_c_w_uq, v_c_kv_norm_g=v_c_kv_norm_g, v_c_w_ukv=v_c_w_ukv, v_d_ln_g=v_d_ln_g, v_d_ln_b=v_d_ln_b, v_d_w_s=v_d_w_s, v_d_b_s=v_d_b_s, v_cd_w_out=v_cd_w_out, v_ffn_w_up=v_ffn_w_up, v_ffn_conv_w=v_ffn_conv_w, v_ffn_w_down=v_ffn_w_down, v_final_norm_g=v_final_norm_g)
    weights = {n: given[n] for n in TWIN_WEIGHTS}
    shared = {n: given[n] for n in SHARED_INPUTS}
    per_example = {n: given[n] for n in ['x', 'c', 'positions']}
    grad_fn = _jax.value_and_grad(_loss, argnums=(0, 1))

    def one_microbatch(ex, loss_target):
        ex = dict(ex)
        diff = ex.pop(TWIN_DIFF_INPUT)
        return grad_fn(weights, diff, {**shared, **ex}, loss_target)

    if N_MICROBATCH == 1:
        loss, (grad_w, grad_x) = one_microbatch(per_example, given["loss_target"])
    else:
        def body(carry, xs):
            loss_sum, grad_sum = carry
            l_k, (gw_k, gx_k) = one_microbatch(xs[0], xs[1])
            with _jax.named_scope("update"):
                return (loss_sum + l_k, _jax.tree.map(_jnp.add, grad_sum, gw_k)), gx_k

        init = (_jnp.zeros((), _jnp.float32), _jax.tree.map(_jnp.zeros_like, weights))
        (loss, grad_w), grad_x = _jax.lax.scan(body, init, (per_example, given["loss_target"]))
    with _jax.named_scope("update"):
        delta_w, new_m, new_v = {}, {}, {}
        for n in TWIN_WEIGHTS:
            delta_w[n], new_m[n], new_v[n] = _adamw(weights[n], grad_w[n], given["m_" + n], given["v_" + n])
    return (loss, grad_x, *[grad_w[n] for n in TWIN_WEIGHTS], *[delta_w[n] for n in TWIN_WEIGHTS],
            *[new_m[n] for n in TWIN_WEIGHTS], *[new_v[n] for n in TWIN_WEIGHTS])
```

```python
import functools

import jax
import jax.numpy as jnp
from jax import lax
from jax.experimental import pallas as pl
from jax.experimental.pallas import tpu as pltpu

F32 = jnp.float32
BF16 = jnp.bfloat16
EPS = 1e-6
D_MODEL = 1024
N_MOD = 6
A_WIDTH = 512
B_GROUPS = 4
POOL_WINDOWS = (2, 4, 8, 16)
C_HEADS = 8
C_NOPE = 64
C_ROPE = 32
C_V = 64
C_Q_RANK = 256
C_KV_RANK = 128
HEAD_PAD = 128
ROPE_THETA = 10000.0
D_GROUPS = 4
D_CHUNK = 128
D_FF = 2816
FF_UNIT = 128
ADAM_LR = 0.001
ADAM_B1 = 0.9
ADAM_B2 = 0.999
ADAM_EPS = 1e-08
ADAM_WD = 0.01
ADAM_STEP = 10
N_CHIPS = 4
N_DEV = 8
LANES = 128
VMEM_BIG = 56 * 1024 * 1024
MESH = pl.DeviceIdType.MESH


def _sds(shape, dtype=F32):
    return jax.ShapeDtypeStruct(tuple(shape), dtype)


def _tile(n, cap, mult=128):
    if n <= cap:
        return n
    best = None
    for t in range(mult, cap + 1, mult):
        if n % t == 0:
            best = t
    assert best is not None, (n, cap, mult)
    return best


def _params(dims=None, vmem=None):
    return pltpu.CompilerParams(dimension_semantics=dims, vmem_limit_bytes=vmem)


def _shift_down(v, k):
    r = pltpu.roll(v, k, axis=0)
    t = lax.broadcasted_iota(jnp.int32, v.shape, 0)
    return jnp.where(t >= k, r, 0.0)


def _shift_up(v, k):
    n = v.shape[0]
    r = pltpu.roll(v, n - k, axis=0)
    t = lax.broadcasted_iota(jnp.int32, v.shape, 0)
    return jnp.where(t < n - k, r, 0.0)


def _sigmoid(v):
    return 1.0 / (1.0 + jnp.exp(-v))


_GELU_C = 0.7978845608028654
_GELU_A = 0.044715


def _gelu(v):
    return 0.5 * v * (1.0 + jnp.tanh(_GELU_C * (v + _GELU_A * v * v * v)))


def _gelu_grad(v):
    th = jnp.tanh(_GELU_C * (v + _GELU_A * v * v * v))
    return 0.5 * (1.0 + th) + 0.5 * v * (1.0 - th * th) * _GELU_C * (1.0 + 3.0 * _GELU_A * v * v)


_NN = (((1,), (0,)), ((), ()))
_NT = (((1,), (1,)), ((), ()))
_TN = (((0,), (0,)), ((), ()))


def _dot(a, b, dims=_NN):
    return lax.dot_general(a, b, dims, preferred_element_type=F32)


def matmul(a, b, mode, out_dtype, name):
    if mode == "nn":
        m, k = a.shape
        n = b.shape[1]
    elif mode == "nt":
        m, k = a.shape
        n = b.shape[0]
    else:
        k, m = a.shape
        n = b.shape[1]
    tm, tn, tk = _tile(m, 512), _tile(n, 512), _tile(k, 2048)
    nk = k // tk
    if mode == "nn":
        a_spec = pl.BlockSpec((tm, tk), lambda i, j, s: (i, s))
        b_spec = pl.BlockSpec((tk, tn), lambda i, j, s: (s, j))
        dims = _NN
    elif mode == "nt":
        a_spec = pl.BlockSpec((tm, tk), lambda i, j, s: (i, s))
        b_spec = pl.BlockSpec((tn, tk), lambda i, j, s: (j, s))
        dims = _NT
    else:
        a_spec = pl.BlockSpec((tk, tm), lambda i, j, s: (s, i))
        b_spec = pl.BlockSpec((tk, tn), lambda i, j, s: (s, j))
        dims = _TN

    def body(a_ref, b_ref, o_ref, acc_ref):
        s = pl.program_id(2)

        @pl.when(s == 0)
        def _():
            acc_ref[...] = jnp.zeros_like(acc_ref)

        acc_ref[...] += _dot(a_ref[...], b_ref[...], dims)

        @pl.when(s == nk - 1)
        def _():
            o_ref[...] = acc_ref[...].astype(o_ref.dtype)

    return pl.pallas_call(
        body, name=name, out_shape=_sds((m, n), out_dtype), grid=(m // tm, n // tn, nk),
        in_specs=[a_spec, b_spec], out_specs=pl.BlockSpec((tm, tn), lambda i, j, s: (i, j)),
        scratch_shapes=[pltpu.VMEM((tm, tn), F32)],
        compiler_params=_params(("parallel", "parallel", "arbitrary"), VMEM_BIG),
    )(a, b)


def _rows(tm, n):
    return pl.BlockSpec((tm, n), lambda i: (i, 0))


def _vec(n):
    return pl.BlockSpec((1, n), lambda i: (0, 0))


def modnorm_fwd(x, g, sc, sh, name):
    s, d = x.shape
    tm = _tile(s, 256, 8)

    def body(x_ref, g_ref, sc_ref, sh_ref, o_ref):
        xv = x_ref[...]
        r = lax.rsqrt(jnp.mean(xv * xv, axis=-1, keepdims=True) + EPS)
        o_ref[...] = ((xv * r) * g_ref[...] * (1.0 + sc_ref[...]) + sh_ref[...]).astype(BF16)

    return pl.pallas_call(
        body, name=name, out_shape=_sds((s, d), BF16), grid=(s // tm,),
        in_specs=[_rows(tm, d), _vec(d), _vec(d), _vec(d)], out_specs=_rows(tm, d),
        compiler_params=_params(("parallel",)),
    )(x, g, sc, sh)


def resid_fwd(x, y, gate, name):
    s, d = x.shape
    tm = _tile(s, 256, 8)

    def body(x_ref, y_ref, g_ref, o_ref):
        o_ref[...] = x_ref[...] + g_ref[...] * y_ref[...]

    return pl.pallas_call(
        body, name=name, out_shape=_sds((s, d)), grid=(s // tm,),
        in_specs=[_rows(tm, d), _rows(tm, d), _vec(d)], out_specs=_rows(tm, d),
        compiler_params=_params(("parallel",)),
    )(x, y, gate)


def gate_bwd(dres, y, gate, name):
    s, d = dres.shape
    tm = _tile(s, 256, 8)

    def body(dr_ref, y_ref, g_ref, dy_ref, dg_ref):
        @pl.when(pl.program_id(0) == 0)
        def _():
            dg_ref[...] = jnp.zeros_like(dg_ref)

        dr = dr_ref[...]
        dy_ref[...] = (dr * g_ref[...]).astype(BF16)
        dg_ref[...] += jnp.sum(dr * y_ref[...], axis=0, keepdims=True)

    return pl.pallas_call(
        body, name=name, out_shape=(_sds((s, d), BF16), _sds((1, d))), grid=(s // tm,),
        in_specs=[_rows(tm, d), _rows(tm, d), _vec(d)], out_specs=(_rows(tm, d), _vec(d)),
        compiler_params=_params(("arbitrary",)),
    )(dres, y, gate)


def norm_bwd(x, dh, g, sc, dres, name):
    s, d = x.shape
    tm = _tile(s, 256, 8)
    nsteps = s // tm

    def body(x_ref, dh_ref, g_ref, sc_ref, dr_ref, dx_ref, dsh_ref, dsc_ref, dg_ref, a2_ref):
        i = pl.program_id(0)

        @pl.when(i == 0)
        def _():
            dsh_ref[...] = jnp.zeros_like(dsh_ref)
            a2_ref[...] = jnp.zeros_like(a2_ref)

        xv = x_ref[...]
        dh = dh_ref[...]
        r = lax.rsqrt(jnp.mean(xv * xv, axis=-1, keepdims=True) + EPS)
        xh = xv * r
        dsh_ref[...] += jnp.sum(dh, axis=0, keepdims=True)
        a2_ref[...] += jnp.sum(dh * xh, axis=0, keepdims=True)
        dxh = dh * (g_ref[...] * (1.0 + sc_ref[...]))
        dx = r * (dxh - xh * jnp.mean(dxh * xh, axis=-1, keepdims=True))
        dx_ref[...] = dr_ref[...] + dx

        @pl.when(i == nsteps - 1)
        def _():
            dsc_ref[...] = a2_ref[...] * g_ref[...]
            dg_ref[...] = a2_ref[...] * (1.0 + sc_ref[...])

    return pl.pallas_call(
        body, name=name, out_shape=(_sds((s, d)), _sds((1, d)), _sds((1, d)), _sds((1, d))), grid=(nsteps,),
        in_specs=[_rows(tm, d), _rows(tm, d), _vec(d), _vec(d), _rows(tm, d)],
        out_specs=(_rows(tm, d), _vec(d), _vec(d), _vec(d)),
        scratch_shapes=[pltpu.VMEM((1, d), F32)],
        compiler_params=_params(("arbitrary",)),
    )(x, dh, g, sc, dres)


def final_fwd_bwd(x, g, tgt):
    s, d = x.shape
    tm = _tile(s, 256, 8)

    def body(x_ref, g_ref, t_ref, dx_ref, dg_ref, loss_ref):
        @pl.when(pl.program_id(0) == 0)
        def _():
            dg_ref[...] = jnp.zeros_like(dg_ref)
            loss_ref[...] = jnp.zeros_like(loss_ref)

        xv = x_ref[...]
        gv = g_ref[...]
        r = lax.rsqrt(jnp.mean(xv * xv, axis=-1, keepdims=True) + EPS)
        xh = xv * r
        e = xh * gv - t_ref[...]
        row = jnp.sum(e * e, axis=-1, keepdims=True) * (0.5 / d)
        loss_ref[...] += jnp.sum(row, axis=0, keepdims=True)
        dy = e * (1.0 / d)
        dg_ref[...] += jnp.sum(dy * xh, axis=0, keepdims=True)
        dxh = dy * gv
        dx_ref[...] = r * (dxh - xh * jnp.mean(dxh * xh, axis=-1, keepdims=True))

    return pl.pallas_call(
        body, name="final_fwd_bwd", out_shape=(_sds((s, d)), _sds((1, d)), _sds((1, LANES))), grid=(s // tm,),
        in_specs=[_rows(tm, d), _vec(d), _rows(tm, d)], out_specs=(_rows(tm, d), _vec(d), _vec(LANES)),
        compiler_params=_params(("arbitrary",)),
    )(x, g, tgt)


def _conv3(v, w):
    return w[0:1, :] * _shift_down(v, 2) + w[1:2, :] * _shift_down(v, 1) + w[2:3, :] * v


def _conv3_t(dv, w):
    return w[0:1, :] * _shift_up(dv, 2) + w[1:2, :] * _shift_up(dv, 1) + w[2:3, :] * dv


def _conv3_dw(dv, v):
    return jnp.concatenate([
        jnp.sum(dv * _shift_down(v, 2), axis=0, keepdims=True),
        jnp.sum(dv * _shift_down(v, 1), axis=0, keepdims=True),
        jnp.sum(dv * v, axis=0, keepdims=True)], axis=0)


def gconv_fwd(z, conv_w):
    s = z.shape[0]
    nb = A_WIDTH // LANES

    def body(b_ref, c_ref, a_ref, w_ref, o_ref):
        o_ref[...] = (b_ref[...] * _conv3(c_ref[...] * a_ref[...], w_ref[...])).astype(BF16)

    col = lambda off: pl.BlockSpec((s, LANES), lambda j: (0, off + j))
    return pl.pallas_call(
        body, name="gconv_fwd", out_shape=_sds((s, A_WIDTH), BF16), grid=(nb,),
        in_specs=[col(0), col(nb), col(2 * nb), pl.BlockSpec((3, LANES), lambda j: (0, j))],
        out_specs=pl.BlockSpec((s, LANES), lambda j: (0, j)),
        compiler_params=_params(("parallel",), VMEM_BIG),
    )(z, z, z, conv_w)


def gconv_bwd(z, conv_w, dycat):
    s = z.shape[0]
    nb = A_WIDTH // LANES

    def body(b_ref, c_ref, a_ref, w_ref, dy_ref, db_ref, dc_ref, da_ref, dw_ref):
        c, a, w, dy = c_ref[...], a_ref[...], w_ref[...], dy_ref[...]
        ca = c * a
        db_ref[...] = (dy * _conv3(ca, w)).astype(BF16)
        dconv = dy * b_ref[...]
        dw_ref[...] = _conv3_dw(dconv, ca)
        dca = _conv3_t(dconv, w)
        dc_ref[...] = (dca * a).astype(BF16)
        da_ref[...] = (dca * c).astype(BF16)

    col = lambda off: pl.BlockSpec((s, LANES), lambda j: (0, off + j))
    wspec = pl.BlockSpec((3, LANES), lambda j: (0, j))
    part = _sds((s, A_WIDTH), BF16)
    return pl.pallas_call(
        body, name="gconv_bwd", out_shape=(part, part, part, _sds((3, A_WIDTH))), grid=(nb,),
        in_specs=[col(0), col(nb), col(2 * nb), wspec, col(0)],
        out_specs=(col(0), col(0), col(0), wspec),
        compiler_params=_params(("parallel",), VMEM_BIG),
    )(z, z, z, conv_w, dycat)


def _pool_counts(s, w):
    t = lax.broadcasted_iota(jnp.int32, (s, 1), 0)
    return jnp.minimum(t + 1, w).astype(F32)


def _pooled(p, levels):
    acc = p
    for lv in range(levels):
        acc = acc + _shift_down(acc, 2 ** lv)
    return acc / _pool_counts(p.shape[0], 2 ** levels) - p


def pool_fwd(z, mix_w, scale):
    s = z.shape[0]

    def make(g):
        def body_g(p_ref, m_ref, sc_ref, o_ref):
            pooled = _pooled(p_ref[...], g + 1)
            y = _dot(pooled.astype(BF16), m_ref[...].astype(BF16))
            o_ref[...] = (y * sc_ref[...]).astype(BF16)
        return body_g

    outs = []
    for g in range(B_GROUPS):
        outs.append(pl.pallas_call(
            make(g), name=f"pool_fwd{g}", out_shape=_sds((s, LANES), BF16), grid=(1,),
            in_specs=[pl.BlockSpec((s, LANES), lambda i, g=g: (0, 3 * (A_WIDTH // LANES) + g)),
                      pl.BlockSpec((None, LANES, LANES), lambda i, g=g: (g, 0, 0)),
                      pl.BlockSpec((1, LANES), lambda i, g=g: (0, g))],
            out_specs=pl.BlockSpec((s, LANES), lambda i: (0, 0)),
            compiler_params=_params(("arbitrary",), VMEM_BIG),
        )(z, mix_w, scale))
    return outs


def pool_bwd(z, mix_w, scale, dycat):
    s = z.shape[0]

    def make(g):
        w = 2 ** (g + 1)

        def body_g(p_ref, m_ref, sc_ref, dy_ref, dp_ref, dm_ref, dsc_ref):
            pooled = _pooled(p_ref[...], g + 1)
            mw = m_ref[...].astype(BF16)
            pb = pooled.astype(BF16)
            dy = dy_ref[...]
            dsc_ref[...] = jnp.sum(dy * _dot(pb, mw), axis=0, keepdims=True)
            dmix = (dy * sc_ref[...]).astype(BF16)
            dm_ref[...] = _dot(pb, dmix, _TN)
            dpool = _dot(dmix, mw, _NT)
            acc = dpool / _pool_counts(s, w)
            for lv in range(g + 1):
                acc = acc + _shift_up(acc, 2 ** lv)
            dp_ref[...] = (acc - dpool).astype(BF16)
        return body_g

    outs = []
    for g in range(B_GROUPS):
        outs.append(pl.pallas_call(
            make(g), name=f"pool_bwd{g}",
            out_shape=(_sds((s, LANES), BF16), _sds((LANES, LANES)), _sds((1, LANES))), grid=(1,),
            in_specs=[pl.BlockSpec((s, LANES), lambda i, g=g: (0, 3 * (A_WIDTH // LANES) + g)),
                      pl.BlockSpec((None, LANES, LANES), lambda i, g=g: (g, 0, 0)),
                      pl.BlockSpec((1, LANES), lambda i, g=g: (0, g)),
                      pl.BlockSpec((s, LANES), lambda i, g=g: (0, A_WIDTH // LANES + g))],
            out_specs=(pl.BlockSpec((s, LANES), lambda i: (0, 0)), pl.BlockSpec((LANES, LANES), lambda i: (0, 0)),
                       pl.BlockSpec((1, LANES), lambda i: (0, 0))),
            compiler_params=_params(("arbitrary",), VMEM_BIG),
        )(z, mix_w, scale, dycat))
    return outs


def ffn_act_fwd(zf, conv_w, name):
    s = zf.shape[0]
    nb = D_FF // FF_UNIT

    def body(z_ref, w_ref, o_ref):
        zc = _conv3(z_ref[...], w_ref[...])
        g, u = zc[:, :FF_UNIT], zc[:, FF_UNIT:]
        o_ref[...] = (g * _sigmoid(g) * u).astype(BF16)

    return pl.pallas_call(
        body, name=name, out_shape=_sds((s, D_FF), BF16), grid=(nb,),
        in_specs=[pl.BlockSpec((s, 2 * FF_UNIT), lambda j: (0, j)), pl.BlockSpec((3, 2 * FF_UNIT), lambda j: (0, j))],
        out_specs=pl.BlockSpec((s, FF_UNIT), lambda j: (0, j)),
        compiler_params=_params(("parallel",), VMEM_BIG),
    )(zf, conv_w)


def ffn_act_bwd(zf, conv_w, da, name):
    s = zf.shape[0]
    nb = D_FF // FF_UNIT

    def body(z_ref, w_ref, da_ref, dz_ref, dw_ref):
        zv, w, dav = z_ref[...], w_ref[...], da_ref[...]
        zc = _conv3(zv, w)
        g, u = zc[:, :FF_UNIT], zc[:, FF_UNIT:]
        sg = _sigmoid(g)
        dg = dav * u * (sg * (1.0 + g * (1.0 - sg)))
        du = dav * (g * sg)
        dzc = jnp.concatenate([dg, du], axis=1)
        dw_ref[...] = _conv3_dw(dzc, zv)
        dz_ref[...] = _conv3_t(dzc, w).astype(BF16)

    zspec = pl.BlockSpec((s, 2 * FF_UNIT), lambda j: (0, j))
    wspec = pl.BlockSpec((3, 2 * FF_UNIT), lambda j: (0, j))
    return pl.pallas_call(
        body, name=name, out_shape=(_sds((s, 2 * D_FF), BF16), _sds((3, 2 * D_FF))), grid=(nb,),
        in_specs=[zspec, wspec, pl.BlockSpec((s, FF_UNIT), lambda j: (0, j))], out_specs=(zspec, wspec),
        compiler_params=_params(("parallel",), VMEM_BIG),
    )(zf, conv_w, da)


def _rope(v, cs, s1, s2):
    return v * cs + pltpu.roll(v, LANES - C_ROPE // 2, axis=1) * s1 + pltpu.roll(v, C_ROPE // 2, axis=1) * s2


def _rope_t(dv, cs, s1, s2):
    return dv * cs + pltpu.roll(dv * s1, C_ROPE // 2, axis=1) + pltpu.roll(dv * s2, LANES - C_ROPE // 2, axis=1)


def _kpe_mask(shape):
    lane = lax.broadcasted_iota(jnp.int32, shape, 1)
    return (lane >= C_NOPE) & (lane < C_NOPE + C_ROPE)


def _rms(v, g):
    r = lax.rsqrt(jnp.mean(v * v, axis=-1, keepdims=True) + EPS)
    return v * r, r


def _rms_bwd(dn, xh, r, g):
    dxh = dn * g
    return r * (dxh - xh * jnp.mean(dxh * xh, axis=-1, keepdims=True)), jnp.sum(dn * xh, axis=0, keepdims=True)


_ZQ = C_Q_RANK + C_KV_RANK + HEAD_PAD
_HW = C_HEADS * HEAD_PAD


def mla_pre_fwd(z, gq, gkv, wq, wk, wv, cs, s1, s2):
    s = z.shape[0]
    tm = _tile(s, 256, 8)

    def body(z_ref, gq_ref, gkv_ref, wq_ref, wk_ref, wv_ref, cs_ref, s1_ref, s2_ref, q_ref, k_ref, v_ref):
        zv = z_ref[...]
        cst, s1t, s2t = cs_ref[...], s1_ref[...], s2_ref[...]
        qh, _ = _rms(zv[:, :C_Q_RANK], None)
        qn = (qh * gq_ref[...]).astype(BF16)
        q = _dot(qn, wq_ref[...])
        kh, _ = _rms(zv[:, C_Q_RANK:C_Q_RANK + C_KV_RANK], None)
        kvn = (kh * gkv_ref[...]).astype(BF16)
        k = _dot(kvn, wk_ref[...])
        v_ref[...] = _dot(kvn, wv_ref[...]).astype(BF16)
        kpe = _rope(zv[:, C_Q_RANK + C_KV_RANK:], cst, s1t, s2t)
        for h in range(C_HEADS):
            sl = slice(h * HEAD_PAD, (h + 1) * HEAD_PAD)
            q_ref[:, sl] = _rope(q[:, sl], cst, s1t, s2t).astype(BF16)
            k_ref[:, sl] = (k[:, sl] + kpe).astype(BF16)

    full = lambda r, c: pl.BlockSpec((r, c), lambda i: (0, 0))
    hw = _sds((s, _HW), BF16)
    return pl.pallas_call(
        body, name="mla_pre_fwd", out_shape=(hw, hw, hw), grid=(s // tm,),
        in_specs=[_rows(tm, _ZQ), _vec(C_Q_RANK), _vec(C_KV_RANK), full(C_Q_RANK, _HW), full(C_KV_RANK, _HW),
                  full(C_KV_RANK, _HW), _rows(tm, LANES), _rows(tm, LANES), _rows(tm, LANES)],
        out_specs=(_rows(tm, _HW), _rows(tm, _HW), _rows(tm, _HW)),
        compiler_params=_params(("parallel",), VMEM_BIG),
    )(z, gq, gkv, wq, wk, wv, cs, s1, s2)


def mla_pre_bwd(z, gq, gkv, wq, wk, wv, cs, s1, s2, dq, dk, dv):
    s = z.shape[0]
    tm = _tile(s, 256, 8)

    def body(z_ref, gq_ref, gkv_ref, wq_ref, wk_ref, wv_ref, cs_ref, s1_ref, s2_ref, dq_ref, dk_ref, dv_ref,
             dz_ref, dwq_ref, dwk_ref, dwv_ref, dgq_ref, dgkv_ref):
        @pl.when(pl.program_id(0) == 0)
        def _():
            dwq_ref[...] = jnp.zeros_like(dwq_ref)
            dwk_ref[...] = jnp.zeros_like(dwk_ref)
            dwv_ref[...] = jnp.zeros_like(dwv_ref)
            dgq_ref[...] = jnp.zeros_like(dgq_ref)
            dgkv_ref[...] = jnp.zeros_like(dgkv_ref)

        zv = z_ref[...]
        cst, s1t, s2t = cs_ref[...], s1_ref[...], s2_ref[...]
        gqv, gkvv = gq_ref[...], gkv_ref[...]
        qh, rq = _rms(zv[:, :C_Q_RANK], None)
        qn = (qh * gqv).astype(BF16)
        kh, rk = _rms(zv[:, C_Q_RANK:C_Q_RANK + C_KV_RANK], None)
        kvn = (kh * gkvv).astype(BF16)

        dqv = dq_ref[...].astype(F32)
        dqp = jnp.concatenate(
            [_rope_t(dqv[:, h * HEAD_PAD:(h + 1) * HEAD_PAD], cst, s1t, s2t) for h in range(C_HEADS)], axis=1
        ).astype(BF16)
        dwq_ref[...] += _dot(qn, dqp, _TN)
        dqn = _dot(dqp, wq_ref[...], _NT)
        dql, dgq = _rms_bwd(dqn, qh, rq, gqv)
        dgq_ref[...] += dgq

        dkv = dk_ref[...]
        dkb = dkv.astype(BF16)
        dvb = dv_ref[...].astype(BF16)
        dwk_ref[...] += _dot(kvn, dkb, _TN)
        dwv_ref[...] += _dot(kvn, dvb, _TN)
        dkvn = _dot(dkb, wk_ref[...], _NT) + _dot(dvb, wv_ref[...], _NT)
        dkl, dgkv = _rms_bwd(dkvn, kh, rk, gkvv)
        dgkv_ref[...] += dgkv

        dkpe = dkv[:, :HEAD_PAD]
        for h in range(1, C_HEADS):
            dkpe = dkpe + dkv[:, h * HEAD_PAD:(h + 1) * HEAD_PAD]
        dkpe = _rope_t(jnp.where(_kpe_mask(dkpe.shape), dkpe, 0.0), cst, s1t, s2t)
        dz_ref[...] = jnp.concatenate([dql, dkl, dkpe], axis=1).astype(BF16)

    full = lambda r, c: pl.BlockSpec((r, c), lambda i: (0, 0))
    return pl.pallas_call(
        body, name="mla_pre_bwd",
        out_shape=(_sds((s, _ZQ), BF16), _sds((C_Q_RANK, _HW)), _sds((C_KV_RANK, _HW)), _sds((C_KV_RANK, _HW)),
                   _sds((1, C_Q_RANK)), _sds((1, C_KV_RANK))),
        grid=(s // tm,),
        in_specs=[_rows(tm, _ZQ), _vec(C_Q_RANK), _vec(C_KV_RANK), full(C_Q_RANK, _HW), full(C_KV_RANK, _HW),
                  full(C_KV_RANK, _HW), _rows(tm, LANES), _rows(tm, LANES), _rows(tm, LANES),
                  _rows(tm, _HW), _rows(tm, _HW), _rows(tm, _HW)],
        out_specs=(_rows(tm, _ZQ), full(C_Q_RANK, _HW), full(C_KV_RANK, _HW), full(C_KV_RANK, _HW),
                   _vec(C_Q_RANK), _vec(C_KV_RANK)),
        compiler_params=_params(("arbitrary",), VMEM_BIG),
    )(z, gq, gkv, wq, wk, wv, cs, s1, s2, dq, dk, dv)


_ATT_SCALE = (C_NOPE + C_ROPE) ** -0.5
_NEG = -1e30


def _att_probs(q, k, row0):
    sc = _dot(q, k, _NT) * _ATT_SCALE
    qpos = row0 + lax.broadcasted_iota(jnp.int32, sc.shape, 0)
    kpos = lax.broadcasted_iota(jnp.int32, sc.shape, 1)
    sc = jnp.where(kpos <= qpos, sc, _NEG)
    e = jnp.exp(sc - jnp.max(sc, axis=-1, keepdims=True))
    return e / jnp.sum(e, axis=-1, keepdims=True)


def attn_fwd(q, k, v):
    s = q.shape[0]
    tq = _tile(s, 256, 8)

    def body(q_ref, k_ref, v_ref, o_ref):
        p = _att_probs(q_ref[...], k_ref[...], pl.program_id(1) * tq)
        o_ref[...] = _dot(p.astype(BF16), v_ref[...]).astype(BF16)

    qspec = pl.BlockSpec((tq, HEAD_PAD), lambda h, i: (i, h))
    kspec = pl.BlockSpec((s, HEAD_PAD), lambda h, i: (0, h))
    return pl.pallas_call(
        body, name="attn_fwd", out_shape=_sds((s, _HW), BF16), grid=(C_HEADS, s // tq),
        in_specs=[qspec, kspec, kspec], out_specs=qspec,
        compiler_params=_params(("parallel", "parallel"), VMEM_BIG),
    )(q, k, v)


def attn_bwd(q, k, v, o, do_all, do_col0):
    s = q.shape[0]
    tq = _tile(s, 256, 8)

    def body(q_ref, k_ref, v_ref, o_ref, do_ref, dq_ref, dk_ref, dv_ref):
        i = pl.program_id(1)

        @pl.when(i == 0)
        def _():
            dk_ref[...] = jnp.zeros_like(dk_ref)
            dv_ref[...] = jnp.zeros_like(dv_ref)

        qv, kv, vv, dov = q_ref[...], k_ref[...], v_ref[...], do_ref[...]
        p = _att_probs(qv, kv, i * tq)
        dp = _dot(dov, vv, _NT)
        delta = jnp.sum(dov.astype(F32) * o_ref[...].astype(F32), axis=-1, keepdims=True)
        ds = (p * (dp - delta) * _ATT_SCALE).astype(BF16)
        dq_ref[...] = _dot(ds, kv).astype(BF16)
        dk_ref[...] += _dot(ds, qv, _TN)
        dv_ref[...] += _dot(p.astype(BF16), dov, _TN)

    qspec = pl.BlockSpec((tq, HEAD_PAD), lambda h, i: (i, h))
    dospec = pl.BlockSpec((tq, HEAD_PAD), lambda h, i: (i, do_col0 + h))
    kspec = pl.BlockSpec((s, HEAD_PAD), lambda h, i: (0, h))
    return pl.pallas_call(
        body, name="attn_bwd", out_shape=(_sds((s, _HW), BF16), _sds((s, _HW)), _sds((s, _HW))),
        grid=(C_HEADS, s // tq),
        in_specs=[qspec, kspec, kspec, qspec, dospec], out_specs=(qspec, kspec, kspec),
        compiler_params=_params(("parallel", "arbitrary"), VMEM_BIG),
    )(q, k, v, o, do_all)


_DW = D_GROUPS * LANES


def _tril_bf16(w):
    r = lax.broadcasted_iota(jnp.int32, w.shape, 0)
    c = lax.broadcasted_iota(jnp.int32, w.shape, 1)
    return jnp.where(c <= r, w, 0.0).astype(BF16)


def _sgu_forward(zu, zv, lg, lb, ws_ref, bs):
    u = _gelu(zu)
    v = _gelu(zv)
    mu = jnp.mean(v, axis=-1, keepdims=True)
    vc = v - mu
    rstd = lax.rsqrt(jnp.mean(vc * vc, axis=-1, keepdims=True) + EPS)
    xh = vc * rstd
    vln = (xh * lg + lb).astype(BF16)
    mixed = []
    for g in range(D_GROUPS):
        wg = _tril_bf16(ws_ref[g])
        mixed.append(_dot(wg, vln[:, g * LANES:(g + 1) * LANES]) + bs[:, g:g + 1])
    return u, xh, rstd, vln, jnp.concatenate(mixed, axis=1)


def sgu_fwd(z, lg, lb, ws, bs_t):
    s = z.shape[0]
    nchunk = s // D_CHUNK

    def body(zu_ref, zv_ref, lg_ref, lb_ref, ws_ref, bs_ref, o_ref):
        u, _, _, _, mixed = _sgu_forward(zu_ref[...], zv_ref[...], lg_ref[...], lb_ref[...], ws_ref, bs_ref[...])
        o_ref[...] = (u * mixed).astype(BF16)

    return pl.pallas_call(
        body, name="sgu_fwd", out_shape=_sds((s, _DW), BF16), grid=(nchunk,),
        in_specs=[pl.BlockSpec((D_CHUNK, _DW), lambda n: (n, 1)), pl.BlockSpec((D_CHUNK, _DW), lambda n: (n, 2)),
                  _vec(_DW), _vec(_DW), pl.BlockSpec((D_GROUPS, D_CHUNK, D_CHUNK), lambda n: (0, 0, 0)),
                  pl.BlockSpec((D_CHUNK, LANES), lambda n: (0, 0))],
        out_specs=pl.BlockSpec((D_CHUNK, _DW), lambda n: (n, 0)),
        compiler_params=_params(("parallel",)),
    )(z, z, lg, lb, ws, bs_t)


def sgu_bwd(z, lg, lb, ws, bs_t, dycat, dy_col):
    s = z.shape[0]
    nchunk = s // D_CHUNK

    def body(zu_ref, zv_ref, lg_ref, lb_ref, ws_ref, bs_ref, dy_ref, dzu_ref, dzv_ref, dws_ref, dbs_ref, dlg_ref,
             dlb_ref):
        @pl.when(pl.program_id(0) == 0)
        def _():
            dws_ref[...] = jnp.zeros_like(dws_ref)
            dbs_ref[...] = jnp.zeros_like(dbs_ref)
            dlg_ref[...] = jnp.zeros_like(dlg_ref)
            dlb_ref[...] = jnp.zeros_like(dlb_ref)

        zu, zv, lg = zu_ref[...], zv_ref[...], lg_ref[...]
        u, xh, rstd, vln, mixed = _sgu_forward(zu, zv, lg, lb_ref[...], ws_ref, bs_ref[...])
        dy = dy_ref[...].astype(F32)
        dzu_ref[...] = (dy * mixed * _gelu_grad(zu)).astype(BF16)
        dmix = dy * u
        lane = lax.broadcasted_iota(jnp.int32, (D_CHUNK, LANES), 1)
        row = lax.broadcasted_iota(jnp.int32, (D_CHUNK, D_CHUNK), 0)
        colm = lax.broadcasted_iota(jnp.int32, (D_CHUNK, D_CHUNK), 1)
        dvln = []
        dbs = jnp.zeros((D_CHUNK, LANES), F32)
        for g in range(D_GROUPS):
            sl = slice(g * LANES, (g + 1) * LANES)
            dmg = dmix[:, sl]
            dbs = dbs + jnp.where(lane == g, jnp.sum(dmg, axis=-1, keepdims=True), 0.0)
            dmb = dmg.astype(BF16)
            dws_ref[g] += jnp.where(colm <= row, _dot(dmb, vln[:, sl], _NT), 0.0)
            dvln.append(_dot(_tril_bf16(ws_ref[g]), dmb, _TN))
        dbs_ref[...] += dbs
        dvln = jnp.concatenate(dvln, axis=1)
        dlg_ref[...] += jnp.sum(dvln * xh, axis=0, keepdims=True)
        dlb_ref[...] += jnp.sum(dvln, axis=0, keepdims=True)
        dxh = dvln * lg
        dvv = rstd * (dxh - jnp.mean(dxh, axis=-1, keepdims=True) - xh * jnp.mean(dxh * xh, axis=-1, keepdims=True))
        dzv_ref[...] = (dvv * _gelu_grad(zv)).astype(BF16)

    wsspec = pl.BlockSpec((D_GROUPS, D_CHUNK, D_CHUNK), lambda n: (0, 0, 0))
    chunk = lambda cidx: pl.BlockSpec((D_CHUNK, _DW), lambda n: (n, cidx))
    return pl.pallas_call(
        body, name="sgu_bwd",
        out_shape=(_sds((s, _DW), BF16), _sds((s, _DW), BF16), _sds((D_GROUPS, D_CHUNK, D_CHUNK)),
                   _sds((D_CHUNK, LANES)), _sds((1, _DW)), _sds((1, _DW))),
        grid=(nchunk,),
        in_specs=[chunk(1), chunk(2), _vec(_DW), _vec(_DW), wsspec, pl.BlockSpec((D_CHUNK, LANES), lambda n: (0, 0)),
                  chunk(dy_col)],
        out_specs=(chunk(0), chunk(0), wsspec, pl.BlockSpec((D_CHUNK, LANES), lambda n: (0, 0)), _vec(_DW), _vec(_DW)),
        compiler_params=_params(("arbitrary",)),
    )(z, z, lg, lb, ws, bs_t, dycat)


def ada_mod(c_all, ada_w, ada_b):
    nl, d, n = ada_w.shape
    nb = c_all.shape[0]
    tn = _tile(n, 512)

    def body(c_ref, w_ref, b_ref, o_ref):
        cv = c_ref[...]
        ca = (cv * _sigmoid(cv)).astype(BF16)
        o_ref[...] = _dot(ca, w_ref[...].astype(BF16)) + b_ref[...]

    return pl.pallas_call(
        body, name="ada_mod", out_shape=_sds((nl, nb, n)), grid=(nl, n // tn),
        in_specs=[pl.BlockSpec((nb, d), lambda l, j: (0, 0)), pl.BlockSpec((None, d, tn), lambda l, j: (l, 0, j)),
                  pl.BlockSpec((None, 1, tn), lambda l, j: (l, 0, j))],
        out_specs=pl.BlockSpec((None, nb, tn), lambda l, j: (l, 0, j)),
        compiler_params=_params(("parallel", "parallel")),
    )(c_all, ada_w, ada_b.reshape(nl, 1, n))


def ada_grad(c_all_t, dmod):
    d, nb = c_all_t.shape
    nl, _, n = dmod.shape
    tn = _tile(n, 512)
    tr = _tile(d, 256, 8)

    def body(c_ref, dm_ref, o_ref):
        cv = c_ref[...]
        ca = cv * _sigmoid(cv)
        dm = dm_ref[...]
        acc = ca[:, 0:1] * dm[0:1, :]
        for b in range(1, nb):
            acc = acc + ca[:, b:b + 1] * dm[b:b + 1, :]
        o_ref[...] = acc

    return pl.pallas_call(
        body, name="ada_grad", out_shape=_sds((nl, d, n)), grid=(nl, n // tn, d // tr),
        in_specs=[pl.BlockSpec((tr, nb), lambda l, j, r: (r, 0)), pl.BlockSpec((None, nb, tn), lambda l, j, r: (l, 0, j))],
        out_specs=pl.BlockSpec((None, tr, tn), lambda l, j, r: (l, r, j)),
        compiler_params=_params(("parallel", "parallel", "parallel")),
    )(c_all_t, dmod)


def adamw(w, g, m, v, name):
    shape = w.shape
    cols = shape[-1]
    rows = w.size // cols
    budget = 256 * 1024
    if rows * cols <= budget or rows % 8:
        tr = rows
    else:
        tr = _tile(rows, max(8, budget // cols), 8)
    bc1 = 1.0 - ADAM_B1 ** ADAM_STEP
    bc2 = 1.0 - ADAM_B2 ** ADAM_STEP

    def body(w_ref, g_ref, m_ref, v_ref, d_ref, nm_ref, nv_ref):
        gv = g_ref[...]
        nm = ADAM_B1 * m_ref[...] + (1.0 - ADAM_B1) * gv
        nv = ADAM_B2 * v_ref[...] + (1.0 - ADAM_B2) * (gv * gv)
        nm_ref[...] = nm
        nv_ref[...] = nv
        d_ref[...] = -ADAM_LR * ((nm / bc1) / (jnp.sqrt(nv / bc2) + ADAM_EPS) + ADAM_WD * w_ref[...])

    spec = pl.BlockSpec((tr, cols), lambda i: (i, 0))
    out = _sds((rows, cols))
    r2 = lambda t: t.reshape(rows, cols)
    d, nm, nv = pl.pallas_call(
        body, name=name, out_shape=(out, out, out), grid=(rows // tr,),
        in_specs=[spec] * 4, out_specs=(spec,) * 3, compiler_params=_params(("parallel",)),
    )(r2(w), r2(g), r2(m), r2(v))
    return d.reshape(shape), nm.reshape(shape), nv.reshape(shape)


def sum8(gathered):
    _, r, _ = gathered.shape
    tr = _tile(r, 512, 8)

    def body(g_ref, o_ref):
        acc = g_ref[0]
        for dev in range(1, N_DEV):
            acc = acc + g_ref[dev]
        o_ref[...] = acc

    return pl.pallas_call(
        body, name="sum8", out_shape=_sds((r, LANES)), grid=(r // tr,),
        in_specs=[pl.BlockSpec((N_DEV, tr, LANES), lambda i: (0, i, 0))], out_specs=pl.BlockSpec((tr, LANES), lambda i: (i, 0)),
        compiler_params=_params(("parallel",)),
    )(gathered)


def pair_sum(gp, recv, core):
    _, _, rh, _ = gp.shape
    tr = _tile(rh, 4096, 512)

    def body(c_ref, a_ref, b_ref, o_ref):
        del c_ref
        o_ref[...] = (a_ref[...].astype(F32) + b_ref[...].astype(F32)).astype(BF16)

    grid_spec = pltpu.PrefetchScalarGridSpec(
        num_scalar_prefetch=1, grid=(N_CHIPS, rh // tr),
        in_specs=[pl.BlockSpec((None, None, tr, LANES), lambda k, i, c: (k, c[0], i, 0)),
                  pl.BlockSpec((None, tr, LANES), lambda k, i, c: (k, i, 0))],
        out_specs=pl.BlockSpec((None, tr, LANES), lambda k, i, c: (k, i, 0)))
    return pl.pallas_call(
        body, name="pair_sum", out_shape=_sds((N_CHIPS, rh, LANES), BF16), grid_spec=grid_spec,
        compiler_params=_params(("parallel", "parallel")),
    )(core.reshape(1).astype(jnp.int32), gp, recv)


def chip_sum(t):
    _, rh, _ = t.shape
    tr = _tile(rh, 4096, 512)

    def body(t_ref, o_ref):
        acc = t_ref[0].astype(F32)
        for k in range(1, N_CHIPS):
            acc = acc + t_ref[k].astype(F32)
        o_ref[...] = acc

    return pl.pallas_call(
        body, name="chip_sum", out_shape=_sds((rh, LANES)), grid=(rh // tr,),
        in_specs=[pl.BlockSpec((N_CHIPS, tr, LANES), lambda i: (0, i, 0))], out_specs=pl.BlockSpec((tr, LANES), lambda i: (i, 0)),
        compiler_params=_params(("parallel",)),
    )(t)


def _place():
    return lax.axis_index("x"), lax.axis_index("y"), lax.axis_index("c")


def _other_chips(x, y):
    return [(x, 1 - y), (1 - x, y), (1 - x, 1 - y)]


_HBM = pl.BlockSpec(memory_space=pltpu.HBM)


def all_gather8(v, name):
    m, n = v.shape

    def body(x_ref, out_ref, send_sems, recv_sems, local_sem):
        x, y, c = _place()
        me, sibling = (x, y, c), (x, y, 1 - c)
        chips = _other_chips(x, y)

        def rows(px, py, pc):
            return out_ref.at[pl.ds((4 * px + 2 * py + pc) * m, m), :]

        def copy(k, block, to, src=None):
            return pltpu.make_async_remote_copy(
                src_ref=rows(*block) if src is None else src, dst_ref=rows(*block),
                send_sem=send_sems.at[k], recv_sem=recv_sems.at[k], device_id=to, device_id_type=MESH)

        mine = pltpu.make_async_copy(x_ref, rows(*me), local_sem)
        mine.start()
        first = [copy(0, me, sibling, src=x_ref)]
        first += [copy(1 + j, me, (*chip, c), src=x_ref) for j, chip in enumerate(chips)]
        for cp in first:
            cp.start()
        passed = [copy(4 + j, (*chip, c), sibling) for j, chip in enumerate(chips)]
        for j, chip in enumerate(chips):
            copy(1 + j, (*chip, c), me).wait_recv()
            passed[j].start()
        copy(0, sibling, me).wait_recv()
        for j, chip in enumerate(chips):
            copy(4 + j, (*chip, 1 - c), me).wait_recv()
        for cp in first + passed:
            cp.wait_send()
        mine.wait()

    return pl.pallas_call(
        body, name=name, out_shape=_sds((N_DEV * m, n), v.dtype),
        in_specs=[pl.BlockSpec(memory_space=pltpu.VMEM)], out_specs=pl.BlockSpec(memory_space=pltpu.VMEM),
        scratch_shapes=[pltpu.SemaphoreType.DMA((7,)), pltpu.SemaphoreType.DMA((7,)), pltpu.SemaphoreType.DMA],
        compiler_params=_params(None, VMEM_BIG),
    )(v)


def all_gather_chips(p):
    r, n = p.shape
    rh = r // 2

    def body(p_ref, out_ref, send_sems, recv_sems, local_sem):
        x, y, c = _place()
        me = 2 * x + y
        sibling = (x, y, 1 - c)
        chips = _other_chips(x, y)
        my_half = pl.ds(pl.multiple_of(c * rh, 16), rh)
        other_half = pl.ds(pl.multiple_of((1 - c) * rh, 16), rh)

        def copy(k, src, dst, to):
            return pltpu.make_async_remote_copy(src_ref=src, dst_ref=dst, send_sem=send_sems.at[k],
                                                recv_sem=recv_sems.at[k], device_id=to, device_id_type=MESH)

        mine = pltpu.make_async_copy(p_ref, out_ref.at[me], local_sem)
        mine.start()
        first = [copy(j, p_ref.at[my_half], out_ref.at[me, my_half], (*chip, c)) for j, chip in enumerate(chips)]
        for cp in first:
            cp.start()
        passed = []
        for j, (px, py) in enumerate(chips):
            landed = out_ref.at[2 * px + py, my_half]
            copy(j, landed, landed, (px, py, c)).wait_recv()
            fw = copy(3 + j, landed, landed, sibling)
            fw.start()
            passed.append(fw)
        for j, (px, py) in enumerate(chips):
            landed = out_ref.at[2 * px + py, other_half]
            copy(3 + j, landed, landed, sibling).wait_recv()
        for cp in first + passed:
            cp.wait_send()
        mine.wait()

    return pl.pallas_call(
        body, name="all_gather_chips", out_shape=_sds((N_CHIPS, r, n), p.dtype),
        in_specs=[_HBM], out_specs=_HBM,
        scratch_shapes=[pltpu.SemaphoreType.DMA((6,)), pltpu.SemaphoreType.DMA((6,)), pltpu.SemaphoreType.DMA],
    )(p)


def swap_halves(gp):
    nk, _, rh, n = gp.shape

    def body(g_ref, out_ref, send_sem, recv_sem):
        x, y, c = _place()
        cp = pltpu.make_async_remote_copy(src_ref=g_ref.at[:, 1 - c], dst_ref=out_ref, send_sem=send_sem,
                                          recv_sem=recv_sem, device_id=(x, y, 1 - c), device_id_type=MESH)
        cp.start()
        cp.wait()

    return pl.pallas_call(
        body, name="swap_halves", out_shape=_sds((nk, rh, n), gp.dtype), in_specs=[_HBM], out_specs=_HBM,
        scratch_shapes=[pltpu.SemaphoreType.DMA, pltpu.SemaphoreType.DMA],
    )(gp)


def chip_exchange(t):
    nk, rh, n = t.shape

    def body(t_ref, out_ref, send_sems, recv_sems, local_sem):
        x, y, c = _place()
        me = 2 * x + y
        chips = _other_chips(x, y)

        def copy(j, src, dst, to):
            return pltpu.make_async_remote_copy(src_ref=src, dst_ref=dst, send_sem=send_sems.at[j],
                                                recv_sem=recv_sems.at[j], device_id=to, device_id_type=MESH)

        mine = pltpu.make_async_copy(t_ref.at[me], out_ref.at[me], local_sem)
        mine.start()
        sends = [copy(j, t_ref.at[2 * px + py], out_ref.at[me], (px, py, c)) for j, (px, py) in enumerate(chips)]
        for cp in sends:
            cp.start()
        for j, (px, py) in enumerate(chips):
            landed = out_ref.at[2 * px + py]
            copy(j, landed, landed, (px, py, c)).wait_recv()
        for cp in sends:
            cp.wait_send()
        mine.wait()

    return pl.pallas_call(
        body, name="chip_exchange", out_shape=_sds((nk, rh, n), t.dtype), in_specs=[_HBM], out_specs=_HBM,
        scratch_shapes=[pltpu.SemaphoreType.DMA((3,)), pltpu.SemaphoreType.DMA((3,)), pltpu.SemaphoreType.DMA],
    )(t)


def join_halves(rd):
    rh, n = rd.shape

    def body(r_ref, out_ref, send_sem, recv_sem, local_sem):
        x, y, c = _place()
        mine = pltpu.make_async_copy(r_ref, out_ref.at[c], local_sem)
        mine.start()
        cp = pltpu.make_async_remote_copy(src_ref=r_ref, dst_ref=out_ref.at[c], send_sem=send_sem, recv_sem=recv_sem,
                                          device_id=(x, y, 1 - c), device_id_type=MESH)
        cp.start()
        theirs = out_ref.at[1 - c]
        pltpu.make_async_remote_copy(src_ref=theirs, dst_ref=theirs, send_sem=send_sem, recv_sem=recv_sem,
                                     device_id=(x, y, 1 - c), device_id_type=MESH).wait_recv()
        cp.wait_send()
        mine.wait()

    return pl.pallas_call(
        body, name="join_halves", out_shape=_sds((2, rh, n), rd.dtype), in_specs=[_HBM], out_specs=_HBM,
        scratch_shapes=[pltpu.SemaphoreType.DMA, pltpu.SemaphoreType.DMA, pltpu.SemaphoreType.DMA],
    )(rd)


_CD_PAD = C_Q_RANK + C_KV_RANK + HEAD_PAD + 2 * _DW


def _ff_interleave(w):
    lead = w.shape[:-1]
    nd = len(lead)
    w = w.reshape(*lead, 2, D_FF // FF_UNIT, FF_UNIT)
    return jnp.swapaxes(w, nd, nd + 1).reshape(*lead, 2 * D_FF)


def _ff_deinterleave(w):
    lead = w.shape[:-1]
    nd = len(lead)
    w = w.reshape(*lead, D_FF // FF_UNIT, 2, FF_UNIT)
    return jnp.swapaxes(w, nd, nd + 1).reshape(*lead, 2 * D_FF)


def _cd_in_pad(w):
    a = C_Q_RANK + C_KV_RANK
    z = lambda n: jnp.zeros((w.shape[0], n), w.dtype)
    return jnp.concatenate([w[:, :a], z(C_NOPE), w[:, a:a + C_ROPE], z(HEAD_PAD - C_NOPE - C_ROPE), w[:, a + C_ROPE:]], axis=1)


def _cd_in_unpad(w):
    a = C_Q_RANK + C_KV_RANK
    return jnp.concatenate([w[:, :a], w[:, a + C_NOPE:a + C_NOPE + C_ROPE], w[:, a + HEAD_PAD:]], axis=1)


def _pad_heads(w, width):
    r = w.shape[0]
    w = w.reshape(r, C_HEADS, width)
    return jnp.pad(w, ((0, 0), (0, 0), (0, HEAD_PAD - width))).reshape(r, _HW)


def _unpad_heads(w, width):
    r = w.shape[0]
    return w.reshape(r, C_HEADS, HEAD_PAD)[:, :, :width].reshape(r, C_HEADS * width)


_MATMUL_WEIGHTS = ("ab_w_in", "ab_w_out", "cd_w_in", "c_w_uq", "c_w_ukv", "cd_w_out", "ffn_w_up", "ffn_w_down")
_LAYER_STACKED = ("norm1_g", "norm2_g", "ffn_w_up", "ffn_conv_w", "ffn_w_down")
_ROW_VECTORS = ("b_scale", "c_q_norm_g", "c_kv_norm_g", "d_ln_g", "d_ln_b")


def full_to_local(p):
    q = {}
    for k, v in p.items():
        if k == "final_norm_g":
            v = v.reshape(1, -1)
        elif k not in _LAYER_STACKED and k not in _ROW_VECTORS:
            v = v[0]
        q[k] = v.astype(BF16) if k in _MATMUL_WEIGHTS else v
    return q


def local_to_full(g):
    q = {}
    for k, v in g.items():
        if k == "final_norm_g":
            q[k] = v.reshape(-1)
        elif k not in _LAYER_STACKED and k not in _ROW_VECTORS:
            q[k] = v[None]
        else:
            q[k] = v
    return q


def prepare_weights(p):
    q = dict(p)
    q["cd_w_in"] = _cd_in_pad(p["cd_w_in"])
    q["c_w_uq"] = _pad_heads(p["c_w_uq"], C_NOPE + C_ROPE)
    ukv = p["c_w_ukv"].reshape(C_KV_RANK, C_HEADS, C_NOPE + C_V)
    q["c_w_uk"] = _pad_heads(ukv[:, :, :C_NOPE].reshape(C_KV_RANK, -1), C_NOPE)
    q["c_w_uv"] = _pad_heads(ukv[:, :, C_NOPE:].reshape(C_KV_RANK, -1), C_V)
    wo = p["cd_w_out"]
    att_rows = jnp.pad(wo[:C_HEADS * C_V].reshape(C_HEADS, C_V, D_MODEL), ((0, 0), (0, HEAD_PAD - C_V), (0, 0)))
    q["cd_w_out"] = jnp.concatenate([att_rows.reshape(_HW, D_MODEL), wo[C_HEADS * C_V:]], axis=0)
    q["ffn_w_up"] = _ff_interleave(p["ffn_w_up"])
    q["ffn_conv_w"] = _ff_interleave(p["ffn_conv_w"])
    return q


def unprepare_grads(g):
    q = dict(g)
    q["cd_w_in"] = _cd_in_unpad(g["cd_w_in"])
    q["c_w_uq"] = _unpad_heads(g["c_w_uq"], C_NOPE + C_ROPE)
    uk = g.pop("c_w_uk").reshape(C_KV_RANK, C_HEADS, HEAD_PAD)[:, :, :C_NOPE]
    uv = g.pop("c_w_uv").reshape(C_KV_RANK, C_HEADS, HEAD_PAD)[:, :, :C_V]
    q.pop("c_w_uk", None)
    q.pop("c_w_uv", None)
    q["c_w_ukv"] = jnp.concatenate([uk, uv], axis=-1).reshape(C_KV_RANK, C_HEADS * (C_NOPE + C_V))
    wo = g["cd_w_out"]
    att = wo[:_HW].reshape(C_HEADS, HEAD_PAD, D_MODEL)[:, :C_V].reshape(C_HEADS * C_V, D_MODEL)
    q["cd_w_out"] = jnp.concatenate([att, wo[_HW:]], axis=0)
    q["ffn_w_up"] = _ff_deinterleave(g["ffn_w_up"])
    q["ffn_conv_w"] = _ff_deinterleave(g["ffn_conv_w"])
    return q


def rope_tables(positions):
    half = C_ROPE // 2
    inv_freq = ROPE_THETA ** (-jnp.arange(half, dtype=F32) / half)
    ang = positions.astype(F32)[:, None] * inv_freq
    cos, sin = jnp.cos(ang), jnp.sin(ang)
    s = positions.shape[0]
    z = lambda n: jnp.zeros((s, n), F32)
    cs = jnp.concatenate([jnp.ones((s, C_NOPE), F32), cos, cos, z(HEAD_PAD - C_NOPE - C_ROPE)], axis=1)
    s1 = jnp.concatenate([z(C_NOPE), -sin, z(HEAD_PAD - C_NOPE - half)], axis=1)
    s2 = jnp.concatenate([z(C_NOPE + half), sin, z(HEAD_PAD - C_NOPE - C_ROPE)], axis=1)
    return cs, s1, s2


def _mods(mod_l):
    return [mod_l[:, i * D_MODEL:(i + 1) * D_MODEL] for i in range(N_MOD)]


def _ffn_fwd(x1, w, l, sc2, sh2, g2):
    n2 = w["norm2_g"][l:l + 1]
    h2 = modnorm_fwd(x1, n2, sc2, sh2, f"modnorm2_fwd{l}")
    zf = matmul(h2, w["ffn_w_up"][l], "nn", F32, f"ffn_up{l}")
    a = ffn_act_fwd(zf, w["ffn_conv_w"][l], f"ffn_act_fwd{l}")
    f = matmul(a, w["ffn_w_down"][l], "nn", F32, f"ffn_down{l}")
    x2 = resid_fwd(x1, f, g2, f"resid2_fwd{l}")
    return x2, (h2, zf, a, f)


def _ffn_bwd(dres, x1, saved, w, l, sc2, g2):
    h2, zf, a, f = saved
    n2 = w["norm2_g"][l:l + 1]
    df, dg2 = gate_bwd(dres, f, g2, f"gate2_bwd{l}")
    da = matmul(df, w["ffn_w_down"][l], "nt", F32, f"ffn_down_dx{l}")
    d_down = matmul(a, df, "tn", F32, f"ffn_down_dw{l}")
    dzf, d_conv = ffn_act_bwd(zf, w["ffn_conv_w"][l], da, f"ffn_act_bwd{l}")
    dh2 = matmul(dzf, w["ffn_w_up"][l], "nt", F32, f"ffn_up_dx{l}")
    d_up = matmul(h2, dzf, "tn", F32, f"ffn_up_dw{l}")
    dres, dsh2, dsc2, dn2 = norm_bwd(x1, dh2, n2, sc2, dres, f"norm2_bwd{l}")
    return dres, dict(ffn_w_down=d_down, ffn_conv_w=d_conv, ffn_w_up=d_up, norm2_g=dn2), (dsh2, dsc2, dg2)


def local_step(x, tgt, mod, ropes, w):
    cs, s1, s2 = ropes
    grads = {}

    sh1, sc1, g1, sh2, sc2, g2 = _mods(mod[0:1])
    x0 = x
    h_0 = modnorm_fwd(x0, w["norm1_g"][0:1], sc1, sh1, "modnorm1_fwd0")
    z_0 = matmul(h_0, w["ab_w_in"], "nn", F32, "ab_in")
    ya = gconv_fwd(z_0, w["a_conv_w"])
    yb = pool_fwd(z_0, w["b_mix_w"], w["b_scale"])
    ycat_0 = jnp.concatenate([ya] + yb, axis=1)
    y_0 = matmul(ycat_0, w["ab_w_out"], "nn", F32, "ab_out")
    x1_0 = resid_fwd(x0, y_0, g1, "resid1_fwd0")
    x2_0, ffn_saved_0 = _ffn_fwd(x1_0, w, 0, sc2, sh2, g2)
    mods_0 = (sc1, g1, sc2, g2)

    sh1, sc1, g1, sh2, sc2, g2 = _mods(mod[1:2])
    h_1 = modnorm_fwd(x2_0, w["norm1_g"][1:2], sc1, sh1, "modnorm1_fwd1")
    z_1 = matmul(h_1, w["cd_w_in"], "nn", F32, "cd_in")
    bs_t = jnp.pad(w["d_b_s"].T, ((0, 0), (0, LANES - D_GROUPS)))
    qh, kh, vh = mla_pre_fwd(z_1, w["c_q_norm_g"], w["c_kv_norm_g"], w["c_w_uq"], w["c_w_uk"], w["c_w_uv"], cs, s1, s2)
    oh = attn_fwd(qh, kh, vh)
    yd = sgu_fwd(z_1, w["d_ln_g"], w["d_ln_b"], w["d_w_s"], bs_t)
    ycat_1 = jnp.concatenate([oh, yd], axis=1)
    y_1 = matmul(ycat_1, w["cd_w_out"], "nn", F32, "cd_out")
    x1_1 = resid_fwd(x2_0, y_1, g1, "resid1_fwd1")
    x2_1, ffn_saved_1 = _ffn_fwd(x1_1, w, 1, sc2, sh2, g2)

    dres, d_final, loss = final_fwd_bwd(x2_1, w["final_norm_g"], tgt)
    grads["final_norm_g"] = d_final

    dres, gf1, (dsh2_1, dsc2_1, dg2_1) = _ffn_bwd(dres, x1_1, ffn_saved_1, w, 1, sc2, g2)
    dy, dg1_1 = gate_bwd(dres, y_1, g1, "gate1_bwd1")
    dycat = matmul(dy, w["cd_w_out"], "nt", BF16, "cd_out_dx")
    grads["cd_w_out"] = matmul(ycat_1, dy, "tn", F32, "cd_out_dw")
    dqh, dkh, dvh = attn_bwd(qh, kh, vh, oh, dycat, 0)
    dzq, d_uq, d_uk, d_uv, d_gq, d_gkv = mla_pre_bwd(
        z_1, w["c_q_norm_g"], w["c_kv_norm_g"], w["c_w_uq"], w["c_w_uk"], w["c_w_uv"], cs, s1, s2, dqh, dkh, dvh)
    dzu, dzv, d_ws, d_bs, d_lg, d_lb = sgu_bwd(z_1, w["d_ln_g"], w["d_ln_b"], w["d_w_s"], bs_t, dycat, _HW // _DW)
    dz = jnp.concatenate([dzq, dzu, dzv], axis=1)
    dh = matmul(dz, w["cd_w_in"], "nt", F32, "cd_in_dx")
    grads["cd_w_in"] = matmul(h_1, dz, "tn", F32, "cd_in_dw")
    dres, dsh1_1, dsc1_1, dn1_1 = norm_bwd(x2_0, dh, w["norm1_g"][1:2], sc1, dres, "norm1_bwd1")
    grads.update(c_w_uq=d_uq, c_w_uk=d_uk, c_w_uv=d_uv, c_q_norm_g=d_gq, c_kv_norm_g=d_gkv, d_w_s=d_ws,
                 d_b_s=d_bs[:, :D_GROUPS].T, d_ln_g=d_lg, d_ln_b=d_lb)
    dmod_1 = jnp.concatenate([dsh1_1, dsc1_1, dg1_1, dsh2_1, dsc2_1, dg2_1], axis=1)

    sc1, g1, sc2, g2 = mods_0
    dres, gf0, (dsh2_0, dsc2_0, dg2_0) = _ffn_bwd(dres, x1_0, ffn_saved_0, w, 0, sc2, g2)
    dy, dg1_0 = gate_bwd(dres, y_0, g1, "gate1_bwd0")
    dycat = matmul(dy, w["ab_w_out"], "nt", F32, "ab_out_dx")
    grads["ab_w_out"] = matmul(ycat_0, dy, "tn", F32, "ab_out_dw")
    db, dc, da, d_conv = gconv_bwd(z_0, w["a_conv_w"], dycat)
    pb = pool_bwd(z_0, w["b_mix_w"], w["b_scale"], dycat)
    dz = jnp.concatenate([db, dc, da] + [t[0] for t in pb], axis=1)
    dh = matmul(dz, w["ab_w_in"], "nt", F32, "ab_in_dx")
    grads["ab_w_in"] = matmul(h_0, dz, "tn", F32, "ab_in_dw")
    dres, dsh1_0, dsc1_0, dn1_0 = norm_bwd(x0, dh, w["norm1_g"][0:1], sc1, dres, "norm1_bwd0")
    grads.update(a_conv_w=d_conv, b_mix_w=jnp.stack([t[1] for t in pb]), b_scale=jnp.concatenate([t[2] for t in pb], axis=1))
    dmod_0 = jnp.concatenate([dsh1_0, dsc1_0, dg1_0, dsh2_0, dsc2_0, dg2_0], axis=1)

    for k in ("ffn_w_down", "ffn_conv_w", "ffn_w_up", "norm2_g"):
        grads[k] = jnp.stack([gf0[k], gf1[k]]) if k != "norm2_g" else jnp.concatenate([gf0[k], gf1[k]], axis=0)
    grads["norm1_g"] = jnp.concatenate([dn1_0, dn1_1], axis=0)
    return loss, dres, jnp.concatenate([dmod_0, dmod_1], axis=0), grads


_WEIGHTS = ("ada_w", "ada_b", "norm1_g", "norm2_g", "ab_w_in", "a_conv_w", "b_mix_w", "b_scale", "ab_w_out", "cd_w_in",
            "c_q_norm_g", "c_w_uq", "c_kv_norm_g", "c_w_ukv", "d_ln_g", "d_ln_b", "d_w_s", "d_b_s", "cd_w_out",
            "ffn_w_up", "ffn_conv_w", "ffn_w_down", "final_norm_g")
_INPUTS = ("x", "c", "positions") + _WEIGHTS + ("loss_target",) + tuple("m_" + n for n in _WEIGHTS) + tuple(
    "v_" + n for n in _WEIGHTS)

_BIG = (
    ("ab_w_in", None, 1, (D_MODEL, 2048)), ("ab_w_out", None, 0, (1024, D_MODEL)),
    ("ffn_w_up", 0, 1, (D_MODEL, 2 * D_FF)), ("ffn_w_down", 0, 0, (D_FF, D_MODEL)),
    ("cd_w_in", None, 1, (D_MODEL, 1440)), ("c_w_uq", None, 1, (C_Q_RANK, 768)), ("c_w_ukv", None, 1, (C_KV_RANK, 1024)),
    ("cd_w_out", None, 0, (1024, D_MODEL)),
    ("ffn_w_up", 1, 1, (D_MODEL, 2 * D_FF)), ("ffn_w_down", 1, 0, (D_FF, D_MODEL)),
)
_PACK_ROWS = 1024


def _big_rows():
    n = sum(r * c for _, _, _, (r, c) in _BIG) // N_CHIPS
    rows = -(-n // LANES)
    return -(-rows // _PACK_ROWS) * _PACK_ROWS


def _shard_2d(a, layer):
    return a[layer] if layer is not None else a[0]


def _pack_rows(parts, rows, dtype):
    flat = jnp.concatenate([p.reshape(-1).astype(dtype) for p in parts])
    return jnp.pad(flat, (0, rows * LANES - flat.shape[0])).reshape(rows, LANES)


def _split_shards(g, axis):
    r, c = g.shape
    if axis == 0:
        return g.reshape(N_CHIPS, (r // N_CHIPS) * c)
    return g.reshape(r, N_CHIPS, c // N_CHIPS).transpose(1, 0, 2).reshape(N_CHIPS, r * (c // N_CHIPS))


def _join_shards(s, axis, shape):
    r, c = shape
    if axis == 0:
        return s.reshape(r, c)
    return s.reshape(N_CHIPS, r, c // N_CHIPS).transpose(1, 0, 2).reshape(r, c)


_SMALL_SHARDED = (("a_conv_w", (3, 128), 1), ("c_q_norm_g", (1, 64), 1), ("d_ln_g", (1, 128), 1), ("d_ln_b", (1, 128), 1),
                  ("ffn_conv_w", (2, 3, 2 * D_FF // N_CHIPS), 2))
_SMALL_GRADS = (("norm1_g", (2, D_MODEL)), ("norm2_g", (2, D_MODEL)), ("b_mix_w", (4, 128, 128)), ("b_scale", (1, 512)),
                ("c_kv_norm_g", (1, 128)), ("d_w_s", (4, 128, 128)), ("d_b_s", (4, 128)), ("final_norm_g", (1, D_MODEL)),
                ("a_conv_w", (3, 512)), ("c_q_norm_g", (1, 256)), ("d_ln_g", (1, 512)), ("d_ln_b", (1, 512)),
                ("ffn_conv_w", (2, 3, 2 * D_FF)))


def _size(shape):
    n = 1
    for d in shape:
        n *= d
    return n


def kernel(x, c, positions, ada_w, ada_b, norm1_g, norm2_g, ab_w_in, a_conv_w, b_mix_w, b_scale, ab_w_out, cd_w_in, c_q_norm_g, c_w_uq, c_kv_norm_g, c_w_ukv, d_ln_g, d_ln_b, d_w_s, d_b_s, cd_w_out, ffn_w_up, ffn_conv_w, ffn_w_down, final_norm_g, loss_target, m_ada_w, m_ada_b, m_norm1_g, m_norm2_g, m_ab_w_in, m_a_conv_w, m_b_mix_w, m_b_scale, m_ab_w_out, m_cd_w_in, m_c_q_norm_g, m_c_w_uq, m_c_kv_norm_g, m_c_w_ukv, m_d_ln_g, m_d_ln_b, m_d_w_s, m_d_b_s, m_cd_w_out, m_ffn_w_up, m_ffn_conv_w, m_ffn_w_down, m_final_norm_g, v_ada_w, v_ada_b, v_norm1_g, v_norm2_g, v_ab_w_in, v_a_conv_w, v_b_mix_w, v_b_scale, v_ab_w_out, v_cd_w_in, v_c_q_norm_g, v_c_w_uq, v_c_kv_norm_g, v_c_w_ukv, v_d_ln_g, v_d_ln_b, v_d_w_s, v_d_b_s, v_cd_w_out, v_ffn_w_up, v_ffn_conv_w, v_ffn_w_down, v_final_norm_g):
    args = (x, c, positions, ada_w, ada_b, norm1_g, norm2_g, ab_w_in, a_conv_w, b_mix_w, b_scale, ab_w_out, cd_w_in, c_q_norm_g, c_w_uq, c_kv_norm_g, c_w_ukv, d_ln_g, d_ln_b, d_w_s, d_b_s, cd_w_out, ffn_w_up, ffn_conv_w, ffn_w_down, final_norm_g, loss_target, m_ada_w, m_ada_b, m_norm1_g, m_norm2_g, m_ab_w_in, m_a_conv_w, m_b_mix_w, m_b_scale, m_ab_w_out, m_cd_w_in, m_c_q_norm_g, m_c_w_uq, m_c_kv_norm_g, m_c_w_ukv, m_d_ln_g, m_d_ln_b, m_d_w_s, m_d_b_s, m_cd_w_out, m_ffn_w_up, m_ffn_conv_w, m_ffn_w_down, m_final_norm_g, v_ada_w, v_ada_b, v_norm1_g, v_norm2_g, v_ab_w_in, v_a_conv_w, v_b_mix_w, v_b_scale, v_ab_w_out, v_cd_w_in, v_c_q_norm_g, v_c_w_uq, v_c_kv_norm_g, v_c_w_ukv, v_d_ln_g, v_d_ln_b, v_d_w_s, v_d_b_s, v_cd_w_out, v_ffn_w_up, v_ffn_conv_w, v_ffn_w_down, v_final_norm_g)
    a = dict(zip(_INPUTS, args, strict=True))
    xi, yi, ci = _place()
    chip = 2 * xi + yi
    dev = 4 * xi + 2 * yi + ci
    x = a["x"][0]
    tgt = a["loss_target"][0]
    seq = x.shape[0]

    small_parts = [a["c"]] + [a[n] for n, _, _ in _SMALL_SHARDED]
    rows1 = -(-sum(p.size for p in small_parts) // LANES // 8) * 8
    g1 = all_gather8(_pack_rows(small_parts, rows1, F32), "gather_small").reshape(N_DEV, rows1 * LANES)
    c_all = g1[:, :D_MODEL]
    per_chip = g1[0::2]
    small_full = {}
    off = D_MODEL
    for n, shp, axis in _SMALL_SHARDED:
        piece = per_chip[:, off:off + _size(shp)].reshape((N_CHIPS,) + shp)
        small_full[n] = jnp.concatenate([piece[k] for k in range(N_CHIPS)], axis=axis)
        off += _size(shp)

    rows = _big_rows()
    packed = _pack_rows([_shard_2d(a[n], layer) for n, layer, _, _ in _BIG], rows, BF16)
    gathered = all_gather_chips(packed).reshape(N_CHIPS, rows * LANES)
    full = {}
    off = 0
    for n, layer, axis, shp in _BIG:
        cnt = _size(shp) // N_CHIPS
        wfull = _join_shards(gathered[:, off:off + cnt], axis, shp)
        off += cnt
        if layer is None:
            full[n] = wfull
        else:
            full.setdefault(n, [None, None])[layer] = wfull
    for n in ("ffn_w_up", "ffn_w_down"):
        full[n] = jnp.stack(full[n])
    full.update(norm1_g=a["norm1_g"], norm2_g=a["norm2_g"], b_mix_w=a["b_mix_w"][0], b_scale=a["b_scale"],
                c_kv_norm_g=a["c_kv_norm_g"], d_w_s=a["d_w_s"][0], d_b_s=a["d_b_s"][0],
                final_norm_g=a["final_norm_g"].reshape(1, D_MODEL), **small_full)
    w = prepare_weights(full)

    ncol = N_MOD * D_MODEL // N_CHIPS
    ada_b_mine = lax.dynamic_slice_in_dim(a["ada_b"], chip * ncol, ncol, axis=1)
    mod_cols = ada_mod(c_all, a["ada_w"], ada_b_mine)
    g2 = all_gather8(mod_cols.reshape(-1, LANES), "gather_mod").reshape(N_DEV, 2, N_DEV, ncol)
    mod = lax.dynamic_index_in_dim(g2[0::2], dev, axis=2, keepdims=False)
    mod = mod.transpose(1, 0, 2).reshape(2, N_MOD * D_MODEL)

    loss, grad_x, dmod, grads = local_step(x, tgt, mod, rope_tables(a["positions"][0]), w)
    grads = unprepare_grads(grads)
    loss = lax.psum(loss[0, 0], ("x", "y", "c"))

    parts3 = [dmod] + [grads[n] for n, _ in _SMALL_GRADS]
    rows3 = -(-sum(p.size for p in parts3) // LANES // 8) * 8
    g3 = all_gather8(_pack_rows(parts3, rows3, F32), "gather_grads").reshape(N_DEV, rows3, LANES)
    summed = sum8(g3).reshape(-1)
    nmod = 2 * N_MOD * D_MODEL
    out_grads = {"ada_b": summed[:nmod].reshape(2, N_MOD * D_MODEL)}
    off = nmod
    for n, shp in _SMALL_GRADS:
        out_grads[n] = summed[off:off + _size(shp)].reshape(shp)
        off += _size(shp)
    for n, shp, axis in _SMALL_SHARDED:
        width = out_grads[n].shape[-1] // N_CHIPS
        out_grads[n] = lax.dynamic_slice_in_dim(out_grads[n], chip * width, width, axis=out_grads[n].ndim - 1)
    dmod_all = g3.reshape(N_DEV, rows3 * LANES)[:, :nmod].reshape(N_DEV, 2, N_MOD * D_MODEL)
    dmod_mine = lax.dynamic_slice_in_dim(dmod_all, chip * ncol, ncol, axis=2).transpose(1, 0, 2)
    out_grads["ada_w"] = ada_grad(c_all.T, dmod_mine)

    pieces = []
    for n, layer, axis, _ in _BIG:
        g = grads[n] if layer is None else grads[n][layer]
        pieces.append(_split_shards(g.astype(BF16), axis))
    gp = jnp.concatenate(pieces, axis=1)
    gp = jnp.pad(gp, ((0, 0), (0, rows * LANES - gp.shape[1]))).reshape(N_CHIPS, 2, rows // 2, LANES)
    pair = pair_sum(gp, swap_halves(gp), ci)
    reduced_half = chip_sum(chip_exchange(pair))
    reduced = join_halves(reduced_half).reshape(-1)
    off = 0
    stacked = {}
    for n, layer, axis, shp in _BIG:
        cnt = _size(shp) // N_CHIPS
        shard_shape = (shp[0] // N_CHIPS, shp[1]) if axis == 0 else (shp[0], shp[1] // N_CHIPS)
        g = reduced[off:off + cnt].reshape(shard_shape)
        off += cnt
        if layer is None:
            out_grads[n] = g
        else:
            stacked.setdefault(n, [None, None])[layer] = g
    for n, v in stacked.items():
        out_grads[n] = jnp.stack(v)

    g_out, d_out, m_out, v_out = [], [], [], []
    for n in _WEIGHTS:
        g = out_grads[n].reshape(a[n].shape)
        d, nm, nv = adamw(a[n], g, a["m_" + n], a["v_" + n], "adamw_" + n)
        g_out.append(g)
        d_out.append(d)
        m_out.append(nm)
        v_out.append(nv)
    return (loss, grad_x[None], *g_out, *d_out, *m_out, *v_out)
```

```python
import functools

import jax
import jax.numpy as jnp
from jax import lax
from jax.experimental import pallas as pl
from jax.experimental.pallas import tpu as pltpu

F32 = jnp.float32
BF16 = jnp.bfloat16
EPS = 1e-6
D_MODEL = 1024
N_MOD = 6
A_WIDTH = 512
B_GROUPS = 4
POOL_WINDOWS = (2, 4, 8, 16)
C_HEADS = 8
C_NOPE = 64
C_ROPE = 32
C_V = 64
C_Q_RANK = 256
C_KV_RANK = 128
HEAD_PAD = 128
ROPE_THETA = 10000.0
D_GROUPS = 4
D_CHUNK = 128
D_FF = 2816
FF_UNIT = 128
ADAM_LR = 0.001
ADAM_B1 = 0.9
ADAM_B2 = 0.999
ADAM_EPS = 1e-08
ADAM_WD = 0.01
ADAM_STEP = 10
N_CHIPS = 4
N_DEV = 8
LANES = 128
VMEM_BIG = 56 * 1024 * 1024
MESH = pl.DeviceIdType.MESH


def _sds(shape, dtype=F32):
    return jax.ShapeDtypeStruct(tuple(shape), dtype)


def _tile(n, cap, mult=128):
    if n <= cap:
        return n
    best = None
    for t in range(mult, cap + 1, mult):
        if n % t == 0:
            best = t
    assert best is not None, (n, cap, mult)
    return best


def _params(dims=None, vmem=None):
    return pltpu.CompilerParams(dimension_semantics=dims, vmem_limit_bytes=vmem)


def _shift_down(v, k):
    r = pltpu.roll(v, k, axis=0)
    t = lax.broadcasted_iota(jnp.int32, v.shape, 0)
    return jnp.where(t >= k, r, 0.0)


def _shift_up(v, k):
    n = v.shape[0]
    r = pltpu.roll(v, n - k, axis=0)
    t = lax.broadcasted_iota(jnp.int32, v.shape, 0)
    return jnp.where(t < n - k, r, 0.0)


def _sigmoid(v):
    return 1.0 / (1.0 + jnp.exp(-v))


_GELU_C = 0.7978845608028654
_GELU_A = 0.044715


def _gelu(v):
    return 0.5 * v * (1.0 + jnp.tanh(_GELU_C * (v + _GELU_A * v * v * v)))


def _gelu_grad(v):
    th = jnp.tanh(_GELU_C * (v + _GELU_A * v * v * v))
    return 0.5 * (1.0 + th) + 0.5 * v * (1.0 - th * th) * _GELU_C * (1.0 + 3.0 * _GELU_A * v * v)


_NN = (((1,), (0,)), ((), ()))
_NT = (((1,), (1,)), ((), ()))
_TN = (((0,), (0,)), ((), ()))


def _dot(a, b, dims=_NN):
    return lax.dot_general(a, b, dims, preferred_element_type=F32)


def _logical(t, groups):
    return (t.shape[-2], t.shape[-1] * groups)


def _block(tr, tc, groups, cols, where):
    if groups == 1:
        return pl.BlockSpec((tr, tc), where)
    per = cols // groups // tc

    def index(i, j, s):
        r, c = where(i, j, s)
        return (c // per, r, c % per)

    return pl.BlockSpec((None, tr, tc), index)


def matmul(a, b, mode, out_dtype, name, ga=1, gb=1, go=1, tm=None, tn=None, tk=None):
    (ar, ac), (br, bc) = _logical(a, ga), _logical(b, gb)
    if mode == "nn":
        m, k, n = ar, ac, bc
        a_col, b_col = "k", "n"
    elif mode == "nt":
        m, k, n = ar, ac, br
        a_col, b_col = "k", "k"
    else:
        k, m, n = ar, ac, bc
        a_col, b_col = "m", "n"
    limit = {"m": m, "n": n // go, "k": k}
    limit[a_col] = min(limit[a_col], ac // ga)
    limit[b_col] = min(limit[b_col], bc // gb)
    tm = tm or _tile(limit["m"], 1024, 128 if mode == "tn" else 16)
    tn = tn or _tile(limit["n"], 512)
    tk = tk or _tile(limit["k"], 2048, 16 if mode == "tn" else 128)
    nk = k // tk
    if mode == "nn":
        a_spec = _block(tm, tk, ga, ac, lambda i, j, s: (i, s))
        b_spec = _block(tk, tn, gb, bc, lambda i, j, s: (s, j))
        dims = _NN
    elif mode == "nt":
        a_spec = _block(tm, tk, ga, ac, lambda i, j, s: (i, s))
        b_spec = _block(tn, tk, gb, bc, lambda i, j, s: (j, s))
        dims = _NT
    else:
        a_spec = _block(tk, tm, ga, ac, lambda i, j, s: (s, i))
        b_spec = _block(tk, tn, gb, bc, lambda i, j, s: (s, j))
        dims = _TN
    o_spec = _block(tm, tn, go, n, lambda i, j, s: (i, j))
    out_shape = _sds((m, n), out_dtype) if go == 1 else _sds((go, m, n // go), out_dtype)

    def body(a_ref, b_ref, o_ref, acc_ref):
        s = pl.program_id(2)

        @pl.when(s == 0)
        def _():
            acc_ref[...] = jnp.zeros_like(acc_ref)

        acc_ref[...] += _dot(a_ref[...], b_ref[...], dims)

        @pl.when(s == nk - 1)
        def _():
            o_ref[...] = acc_ref[...].astype(o_ref.dtype)

    return pl.pallas_call(
        body, name=name, out_shape=out_shape, grid=(m // tm, n // tn, nk),
        in_specs=[a_spec, b_spec], out_specs=o_spec,
        scratch_shapes=[pltpu.VMEM((tm, tn), F32)],
        compiler_params=_params(("parallel", "parallel", "arbitrary"), VMEM_BIG),
    )(a, b)


def _rows(tm, n):
    return pl.BlockSpec((tm, n), lambda i: (i, 0))


def _vec(n):
    return pl.BlockSpec((1, n), lambda i: (0, 0))


def modnorm_fwd(x, g, sc, sh, name):
    s, d = x.shape
    tm = _tile(s, 256, 8)

    def body(x_ref, g_ref, sc_ref, sh_ref, o_ref):
        xv = x_ref[...]
        r = lax.rsqrt(jnp.mean(xv * xv, axis=-1, keepdims=True) + EPS)
        o_ref[...] = ((xv * r) * g_ref[...] * (1.0 + sc_ref[...]) + sh_ref[...]).astype(BF16)

    return pl.pallas_call(
        body, name=name, out_shape=_sds((s, d), BF16), grid=(s // tm,),
        in_specs=[_rows(tm, d), _vec(d), _vec(d), _vec(d)], out_specs=_rows(tm, d),
        compiler_params=_params(("parallel",)),
    )(x, g, sc, sh)


def resid_fwd(x, y, gate, name):
    s, d = x.shape
    tm = _tile(s, 256, 8)

    def body(x_ref, y_ref, g_ref, o_ref):
        o_ref[...] = x_ref[...] + g_ref[...] * y_ref[...]

    return pl.pallas_call(
        body, name=name, out_shape=_sds((s, d)), grid=(s // tm,),
        in_specs=[_rows(tm, d), _rows(tm, d), _vec(d)], out_specs=_rows(tm, d),
        compiler_params=_params(("parallel",)),
    )(x, y, gate)


def gate_bwd(dres, y, gate, name):
    s, d = dres.shape
    tm = _tile(s, 256, 8)

    def body(dr_ref, y_ref, g_ref, dy_ref, dg_ref):
        @pl.when(pl.program_id(0) == 0)
        def _():
            dg_ref[...] = jnp.zeros_like(dg_ref)

        dr = dr_ref[...]
        dy_ref[...] = (dr * g_ref[...]).astype(BF16)
        dg_ref[...] += jnp.sum(dr * y_ref[...], axis=0, keepdims=True)

    return pl.pallas_call(
        body, name=name, out_shape=(_sds((s, d), BF16), _sds((1, d))), grid=(s // tm,),
        in_specs=[_rows(tm, d), _rows(tm, d), _vec(d)], out_specs=(_rows(tm, d), _vec(d)),
        compiler_params=_params(("arbitrary",)),
    )(dres, y, gate)


def norm_bwd(x, dh, g, sc, dres, name):
    s, d = x.shape
    tm = _tile(s, 256, 8)
    nsteps = s // tm

    def body(x_ref, dh_ref, g_ref, sc_ref, dr_ref, dx_ref, dsh_ref, dsc_ref, dg_ref, a2_ref):
        i = pl.program_id(0)

        @pl.when(i == 0)
        def _():
            dsh_ref[...] = jnp.zeros_like(dsh_ref)
            a2_ref[...] = jnp.zeros_like(a2_ref)

        xv = x_ref[...]
        dh = dh_ref[...]
        r = lax.rsqrt(jnp.mean(xv * xv, axis=-1, keepdims=True) + EPS)
        xh = xv * r
        dsh_ref[...] += jnp.sum(dh, axis=0, keepdims=True)
        a2_ref[...] += jnp.sum(dh * xh, axis=0, keepdims=True)
        dxh = dh * (g_ref[...] * (1.0 + sc_ref[...]))
        dx = r * (dxh - xh * jnp.mean(dxh * xh, axis=-1, keepdims=True))
        dx_ref[...] = dr_ref[...] + dx

        @pl.when(i == nsteps - 1)
        def _():
            dsc_ref[...] = a2_ref[...] * g_ref[...]
            dg_ref[...] = a2_ref[...] * (1.0 + sc_ref[...])

    return pl.pallas_call(
        body, name=name, out_shape=(_sds((s, d)), _sds((1, d)), _sds((1, d)), _sds((1, d))), grid=(nsteps,),
        in_specs=[_rows(tm, d), _rows(tm, d), _vec(d), _vec(d), _rows(tm, d)],
        out_specs=(_rows(tm, d), _vec(d), _vec(d), _vec(d)),
        scratch_shapes=[pltpu.VMEM((1, d), F32)],
        compiler_params=_params(("arbitrary",)),
    )(x, dh, g, sc, dres)


def final_fwd_bwd(x, g, tgt):
    s, d = x.shape
    tm = _tile(s, 256, 8)

    def body(x_ref, g_ref, t_ref, dx_ref, dg_ref, loss_ref):
        @pl.when(pl.program_id(0) == 0)
        def _():
            dg_ref[...] = jnp.zeros_like(dg_ref)
            loss_ref[...] = jnp.zeros_like(loss_ref)

        xv = x_ref[...]
        gv = g_ref[...]
        r = lax.rsqrt(jnp.mean(xv * xv, axis=-1, keepdims=True) + EPS)
        xh = xv * r
        e = xh * gv - t_ref[...]
        row = jnp.sum(e * e, axis=-1, keepdims=True) * (0.5 / d)
        loss_ref[...] += jnp.sum(row, axis=0, keepdims=True)
        dy = e * (1.0 / d)
        dg_ref[...] += jnp.sum(dy * xh, axis=0, keepdims=True)
        dxh = dy * gv
        dx_ref[...] = r * (dxh - xh * jnp.mean(dxh * xh, axis=-1, keepdims=True))

    return pl.pallas_call(
        body, name="final_fwd_bwd", out_shape=(_sds((s, d)), _sds((1, d)), _sds((1, LANES))), grid=(s // tm,),
        in_specs=[_rows(tm, d), _vec(d), _rows(tm, d)], out_specs=(_rows(tm, d), _vec(d), _vec(LANES)),
        compiler_params=_params(("arbitrary",)),
    )(x, g, tgt)


def _conv3(v, w):
    return w[0:1, :] * _shift_down(v, 2) + w[1:2, :] * _shift_down(v, 1) + w[2:3, :] * v


def _conv3_t(dv, w):
    return w[0:1, :] * _shift_up(dv, 2) + w[1:2, :] * _shift_up(dv, 1) + w[2:3, :] * dv


def _conv3_dw(dv, v):
    return jnp.concatenate([
        jnp.sum(dv * _shift_down(v, 2), axis=0, keepdims=True),
        jnp.sum(dv * _shift_down(v, 1), axis=0, keepdims=True),
        jnp.sum(dv * v, axis=0, keepdims=True)], axis=0)


def gconv_fwd(z, conv_w):
    s = z.shape[0]
    nb = A_WIDTH // LANES

    def body(b_ref, c_ref, a_ref, w_ref, o_ref):
        b, c, a = b_ref[...].astype(F32), c_ref[...].astype(F32), a_ref[...].astype(F32)
        o_ref[...] = (b * _conv3(c * a, w_ref[...])).astype(BF16)

    col = lambda off: pl.BlockSpec((s, LANES), lambda j: (0, off + j))
    return pl.pallas_call(
        body, name="gconv_fwd", out_shape=_sds((s, A_WIDTH), BF16), grid=(nb,),
        in_specs=[col(0), col(nb), col(2 * nb), pl.BlockSpec((3, LANES), lambda j: (0, j))],
        out_specs=pl.BlockSpec((s, LANES), lambda j: (0, j)),
        compiler_params=_params(("parallel",), VMEM_BIG),
    )(z, z, z, conv_w)


def gconv_bwd(z, conv_w, dycat):
    s = z.shape[0]
    nb = A_WIDTH // LANES

    def body(b_ref, c_ref, a_ref, w_ref, dy_ref, db_ref, dc_ref, da_ref, dw_ref):
        c, a, w, dy = c_ref[...].astype(F32), a_ref[...].astype(F32), w_ref[...], dy_ref[...].astype(F32)
        ca = c * a
        db_ref[...] = (dy * _conv3(ca, w)).astype(BF16)
        dconv = dy * b_ref[...].astype(F32)
        dw_ref[...] = _conv3_dw(dconv, ca)
        dca = _conv3_t(dconv, w)
        dc_ref[...] = (dca * a).astype(BF16)
        da_ref[...] = (dca * c).astype(BF16)

    col = lambda off: pl.BlockSpec((s, LANES), lambda j: (0, off + j))
    wspec = pl.BlockSpec((3, LANES), lambda j: (0, j))
    part = _sds((s, A_WIDTH), BF16)
    return pl.pallas_call(
        body, name="gconv_bwd", out_shape=(part, part, part, _sds((3, A_WIDTH))), grid=(nb,),
        in_specs=[col(0), col(nb), col(2 * nb), wspec, col(0)],
        out_specs=(col(0), col(0), col(0), wspec),
        compiler_params=_params(("parallel",), VMEM_BIG),
    )(z, z, z, conv_w, dycat)


def _pool_counts(s, w):
    t = lax.broadcasted_iota(jnp.int32, (s, 1), 0)
    return jnp.minimum(t + 1, w).astype(F32)


def _pooled(p, levels):
    acc = p
    for lv in range(levels):
        acc = acc + _shift_down(acc, 2 ** lv)
    return acc / _pool_counts(p.shape[0], 2 ** levels) - p


def pool_fwd(z, mix_w, scale):
    s = z.shape[0]

    def make(g):
        def body_g(p_ref, m_ref, sc_ref, o_ref):
            pooled = _pooled(p_ref[...].astype(F32), g + 1)
            y = _dot(pooled.astype(BF16), m_ref[...].astype(BF16))
            o_ref[...] = (y * sc_ref[...]).astype(BF16)
        return body_g

    outs = []
    for g in range(B_GROUPS):
        outs.append(pl.pallas_call(
            make(g), name=f"pool_fwd{g}", out_shape=_sds((s, LANES), BF16), grid=(1,),
            in_specs=[pl.BlockSpec((s, LANES), lambda i, g=g: (0, 3 * (A_WIDTH // LANES) + g)),
                      pl.BlockSpec((None, LANES, LANES), lambda i, g=g: (g, 0, 0)),
                      pl.BlockSpec((1, LANES), lambda i, g=g: (0, g))],
            out_specs=pl.BlockSpec((s, LANES), lambda i: (0, 0)),
            compiler_params=_params(("arbitrary",), VMEM_BIG),
        )(z, mix_w, scale))
    return outs


def pool_bwd(z, mix_w, scale, dycat):
    s = z.shape[0]

    def make(g):
        w = 2 ** (g + 1)

        def body_g(p_ref, m_ref, sc_ref, dy_ref, dp_ref, dm_ref, dsc_ref):
            pooled = _pooled(p_ref[...].astype(F32), g + 1)
            mw = m_ref[...].astype(BF16)
            pb = pooled.astype(BF16)
            dy = dy_ref[...].astype(F32)
            dsc_ref[...] = jnp.sum(dy * _dot(pb, mw), axis=0, keepdims=True)
            dmix = (dy * sc_ref[...]).astype(BF16)
            dm_ref[...] = _dot(pb, dmix, _TN)
            dpool = _dot(dmix, mw, _NT)
            acc = dpool / _pool_counts(s, w)
            for lv in range(g + 1):
                acc = acc + _shift_up(acc, 2 ** lv)
            dp_ref[...] = (acc - dpool).astype(BF16)
        return body_g

    outs = []
    for g in range(B_GROUPS):
        outs.append(pl.pallas_call(
            make(g), name=f"pool_bwd{g}",
            out_shape=(_sds((s, LANES), BF16), _sds((LANES, LANES)), _sds((1, LANES))), grid=(1,),
            in_specs=[pl.BlockSpec((s, LANES), lambda i, g=g: (0, 3 * (A_WIDTH // LANES) + g)),
                      pl.BlockSpec((None, LANES, LANES), lambda i, g=g: (g, 0, 0)),
                      pl.BlockSpec((1, LANES), lambda i, g=g: (0, g)),
                      pl.BlockSpec((s, LANES), lambda i, g=g: (0, A_WIDTH // LANES + g))],
            out_specs=(pl.BlockSpec((s, LANES), lambda i: (0, 0)), pl.BlockSpec((LANES, LANES), lambda i: (0, 0)),
                       pl.BlockSpec((1, LANES), lambda i: (0, 0))),
            compiler_params=_params(("arbitrary",), VMEM_BIG),
        )(z, mix_w, scale, dycat))
    return outs


_FF_BLOCKS = D_FF // FF_UNIT


def _ff_spec(s):
    return pl.BlockSpec((2, s, FF_UNIT), lambda j: (0, 0, j))


def _ff_wspecs():
    return [pl.BlockSpec((3, FF_UNIT), lambda j: (0, j)), pl.BlockSpec((3, FF_UNIT), lambda j: (0, _FF_BLOCKS + j))]


def ffn_act_fwd(zf, conv_w, name):
    s = zf.shape[1]

    def body(z_ref, wg_ref, wu_ref, o_ref):
        g = _conv3(z_ref[0].astype(F32), wg_ref[...])
        u = _conv3(z_ref[1].astype(F32), wu_ref[...])
        o_ref[...] = (g * _sigmoid(g) * u).astype(BF16)

    return pl.pallas_call(
        body, name=name, out_shape=_sds((s, D_FF), BF16), grid=(_FF_BLOCKS,),
        in_specs=[_ff_spec(s)] + _ff_wspecs(), out_specs=pl.BlockSpec((s, FF_UNIT), lambda j: (0, j)),
        compiler_params=_params(("parallel",), VMEM_BIG),
    )(zf, conv_w, conv_w)


def ffn_act_bwd(zf, conv_w, da, name):
    s = zf.shape[1]

    def body(z_ref, wg_ref, wu_ref, da_ref, dz_ref, dw_ref):
        zg, zu = z_ref[0].astype(F32), z_ref[1].astype(F32)
        wg, wu = wg_ref[...], wu_ref[...]
        dav = da_ref[...].astype(F32)
        g = _conv3(zg, wg)
        u = _conv3(zu, wu)
        sg = _sigmoid(g)
        dg = dav * u * (sg * (1.0 + g * (1.0 - sg)))
        du = dav * (g * sg)
        dw_ref[0] = _conv3_dw(dg, zg)
        dw_ref[1] = _conv3_dw(du, zu)
        dz_ref[0] = _conv3_t(dg, wg).astype(BF16)
        dz_ref[1] = _conv3_t(du, wu).astype(BF16)

    return pl.pallas_call(
        body, name=name, out_shape=(_sds((2, s, D_FF), BF16), _sds((2, 3, D_FF))), grid=(_FF_BLOCKS,),
        in_specs=[_ff_spec(s)] + _ff_wspecs() + [pl.BlockSpec((s, FF_UNIT), lambda j: (0, j))],
        out_specs=(_ff_spec(s), pl.BlockSpec((2, 3, FF_UNIT), lambda j: (0, 0, j))),
        compiler_params=_params(("parallel",), VMEM_BIG),
    )(zf, conv_w, conv_w, da)


def _rope(v, cs, s1, s2):
    return v * cs + pltpu.roll(v, LANES - C_ROPE // 2, axis=1) * s1 + pltpu.roll(v, C_ROPE // 2, axis=1) * s2


def _rope_t(dv, cs, s1, s2):
    return dv * cs + pltpu.roll(dv * s1, C_ROPE // 2, axis=1) + pltpu.roll(dv * s2, LANES - C_ROPE // 2, axis=1)


def _kpe_mask(shape):
    lane = lax.broadcasted_iota(jnp.int32, shape, 1)
    return (lane >= C_NOPE) & (lane < C_NOPE + C_ROPE)


def _rms(v, g):
    r = lax.rsqrt(jnp.mean(v * v, axis=-1, keepdims=True) + EPS)
    return v * r, r


def _rms_bwd(dn, xh, r, g):
    dxh = dn * g
    return r * (dxh - xh * jnp.mean(dxh * xh, axis=-1, keepdims=True)), jnp.sum(dn * xh, axis=0, keepdims=True)


_ZQ = C_Q_RANK + C_KV_RANK + HEAD_PAD
_HW = C_HEADS * HEAD_PAD


def mla_pre_fwd(z, gq, gkv, wq, wk, wv, cs, s1, s2):
    s = z.shape[0]
    tm = _tile(s, 256, 8)

    def body(z_ref, gq_ref, gkv_ref, wq_ref, wk_ref, wv_ref, cs_ref, s1_ref, s2_ref, q_ref, k_ref, v_ref):
        zv = z_ref[...].astype(F32)
        cst, s1t, s2t = cs_ref[...], s1_ref[...], s2_ref[...]
        qh, _ = _rms(zv[:, :C_Q_RANK], None)
        qn = (qh * gq_ref[...]).astype(BF16)
        q = _dot(qn, wq_ref[...])
        kh, _ = _rms(zv[:, C_Q_RANK:C_Q_RANK + C_KV_RANK], None)
        kvn = (kh * gkv_ref[...]).astype(BF16)
        k = _dot(kvn, wk_ref[...])
        v_ref[...] = _dot(kvn, wv_ref[...]).astype(BF16)
        kpe = _rope(zv[:, C_Q_RANK + C_KV_RANK:], cst, s1t, s2t)
        for h in range(C_HEADS):
            sl = slice(h * HEAD_PAD, (h + 1) * HEAD_PAD)
            q_ref[:, sl] = _rope(q[:, sl], cst, s1t, s2t).astype(BF16)
            k_ref[:, sl] = (k[:, sl] + kpe).astype(BF16)

    full = lambda r, c: pl.BlockSpec((r, c), lambda i: (0, 0))
    hw = _sds((s, _HW), BF16)
    return pl.pallas_call(
        body, name="mla_pre_fwd", out_shape=(hw, hw, hw), grid=(s // tm,),
        in_specs=[_rows(tm, _ZQ), _vec(C_Q_RANK), _vec(C_KV_RANK), full(C_Q_RANK, _HW), full(C_KV_RANK, _HW),
                  full(C_KV_RANK, _HW), _rows(tm, LANES), _rows(tm, LANES), _rows(tm, LANES)],
        out_specs=(_rows(tm, _HW), _rows(tm, _HW), _rows(tm, _HW)),
        compiler_params=_params(("parallel",), VMEM_BIG),
    )(z, gq, gkv, wq, wk, wv, cs, s1, s2)


def mla_pre_bwd(z, gq, gkv, wq, wk, wv, cs, s1, s2, dq, dk, dv):
    s = z.shape[0]
    tm = _tile(s, 256, 8)

    def body(z_ref, gq_ref, gkv_ref, wq_ref, wk_ref, wv_ref, cs_ref, s1_ref, s2_ref, dq_ref, dk_ref, dv_ref,
             dz_ref, dwq_ref, dwk_ref, dwv_ref, dgq_ref, dgkv_ref):
        @pl.when(pl.program_id(0) == 0)
        def _():
            dwq_ref[...] = jnp.zeros_like(dwq_ref)
            dwk_ref[...] = jnp.zeros_like(dwk_ref)
            dwv_ref[...] = jnp.zeros_like(dwv_ref)
            dgq_ref[...] = jnp.zeros_like(dgq_ref)
            dgkv_ref[...] = jnp.zeros_like(dgkv_ref)

        zv = z_ref[...].astype(F32)
        cst, s1t, s2t = cs_ref[...], s1_ref[...], s2_ref[...]
        gqv, gkvv = gq_ref[...], gkv_ref[...]
        qh, rq = _rms(zv[:, :C_Q_RANK], None)
        qn = (qh * gqv).astype(BF16)
        kh, rk = _rms(zv[:, C_Q_RANK:C_Q_RANK + C_KV_RANK], None)
        kvn = (kh * gkvv).astype(BF16)

        dqv = dq_ref[...].astype(F32)
        dqp = jnp.concatenate(
            [_rope_t(dqv[:, h * HEAD_PAD:(h + 1) * HEAD_PAD], cst, s1t, s2t) for h in range(C_HEADS)], axis=1
        ).astype(BF16)
        dwq_ref[...] += _dot(qn, dqp, _TN)
        dqn = _dot(dqp, wq_ref[...], _NT)
        dql, dgq = _rms_bwd(dqn, qh, rq, gqv)
        dgq_ref[...] += dgq

        dkv = dk_ref[...]
        dkb = dkv.astype(BF16)
        dvb = dv_ref[...].astype(BF16)
        dwk_ref[...] += _dot(kvn, dkb, _TN)
        dwv_ref[...] += _dot(kvn, dvb, _TN)
        dkvn = _dot(dkb, wk_ref[...], _NT) + _dot(dvb, wv_ref[...], _NT)
        dkl, dgkv = _rms_bwd(dkvn, kh, rk, gkvv)
        dgkv_ref[...] += dgkv

        dkpe = dkv[:, :HEAD_PAD]
        for h in range(1, C_HEADS):
            dkpe = dkpe + dkv[:, h * HEAD_PAD:(h + 1) * HEAD_PAD]
        dkpe = _rope_t(jnp.where(_kpe_mask(dkpe.shape), dkpe, 0.0), cst, s1t, s2t)
        dz_ref[...] = jnp.concatenate([dql, dkl, dkpe], axis=1).astype(BF16)

    full = lambda r, c: pl.BlockSpec((r, c), lambda i: (0, 0))
    return pl.pallas_call(
        body, name="mla_pre_bwd",
        out_shape=(_sds((s, _ZQ), BF16), _sds((C_Q_RANK, _HW)), _sds((C_KV_RANK, _HW)), _sds((C_KV_RANK, _HW)),
                   _sds((1, C_Q_RANK)), _sds((1, C_KV_RANK))),
        grid=(s // tm,),
        in_specs=[_rows(tm, _ZQ), _vec(C_Q_RANK), _vec(C_KV_RANK), full(C_Q_RANK, _HW), full(C_KV_RANK, _HW),
                  full(C_KV_RANK, _HW), _rows(tm, LANES), _rows(tm, LANES), _rows(tm, LANES),
                  _rows(tm, _HW), _rows(tm, _HW), _rows(tm, _HW)],
        out_specs=(_rows(tm, _ZQ), full(C_Q_RANK, _HW), full(C_KV_RANK, _HW), full(C_KV_RANK, _HW),
                   _vec(C_Q_RANK), _vec(C_KV_RANK)),
        compiler_params=_params(("arbitrary",), VMEM_BIG),
    )(z, gq, gkv, wq, wk, wv, cs, s1, s2, dq, dk, dv)


_ATT_SCALE = (C_NOPE + C_ROPE) ** -0.5
_NEG = -1e30


def _att_probs(q, k, row0):
    sc = _dot(q, k, _NT) * _ATT_SCALE
    qpos = row0 + lax.broadcasted_iota(jnp.int32, sc.shape, 0)
    kpos = lax.broadcasted_iota(jnp.int32, sc.shape, 1)
    sc = jnp.where(kpos <= qpos, sc, _NEG)
    e = jnp.exp(sc - jnp.max(sc, axis=-1, keepdims=True))
    return e / jnp.sum(e, axis=-1, keepdims=True)


def attn_fwd(q, k, v):
    s = q.shape[0]
    tq = _tile(s, 256, 8)

    def body(q_ref, k_ref, v_ref, o_ref):
        p = _att_probs(q_ref[...], k_ref[...], pl.program_id(1) * tq)
        o_ref[...] = _dot(p.astype(BF16), v_ref[...]).astype(BF16)

    qspec = pl.BlockSpec((tq, HEAD_PAD), lambda h, i: (i, h))
    kspec = pl.BlockSpec((s, HEAD_PAD), lambda h, i: (0, h))
    return pl.pallas_call(
        body, name="attn_fwd", out_shape=_sds((s, _HW), BF16), grid=(C_HEADS, s // tq),
        in_specs=[qspec, kspec, kspec], out_specs=qspec,
        compiler_params=_params(("parallel", "parallel"), VMEM_BIG),
    )(q, k, v)


def attn_bwd(q, k, v, o, do_all, do_col0):
    s = q.shape[0]
    tq = _tile(s, 256, 8)

    def body(q_ref, k_ref, v_ref, o_ref, do_ref, dq_ref, dk_ref, dv_ref):
        i = pl.program_id(1)

        @pl.when(i == 0)
        def _():
            dk_ref[...] = jnp.zeros_like(dk_ref)
            dv_ref[...] = jnp.zeros_like(dv_ref)

        qv, kv, vv, dov = q_ref[...], k_ref[...], v_ref[...], do_ref[...]
        p = _att_probs(qv, kv, i * tq)
        dp = _dot(dov, vv, _NT)
        delta = jnp.sum(dov.astype(F32) * o_ref[...].astype(F32), axis=-1, keepdims=True)
        ds = (p * (dp - delta) * _ATT_SCALE).astype(BF16)
        dq_ref[...] = _dot(ds, kv).astype(BF16)
        dk_ref[...] += _dot(ds, qv, _TN)
        dv_ref[...] += _dot(p.astype(BF16), dov, _TN)

    qspec = pl.BlockSpec((tq, HEAD_PAD), lambda h, i: (i, h))
    dospec = pl.BlockSpec((tq, HEAD_PAD), lambda h, i: (i, do_col0 + h))
    kspec = pl.BlockSpec((s, HEAD_PAD), lambda h, i: (0, h))
    return pl.pallas_call(
        body, name="attn_bwd", out_shape=(_sds((s, _HW), BF16), _sds((s, _HW)), _sds((s, _HW))),
        grid=(C_HEADS, s // tq),
        in_specs=[qspec, kspec, kspec, qspec, dospec], out_specs=(qspec, kspec, kspec),
        compiler_params=_params(("parallel", "arbitrary"), VMEM_BIG),
    )(q, k, v, o, do_all)


_DW = D_GROUPS * LANES


def _tril_bf16(w):
    r = lax.broadcasted_iota(jnp.int32, w.shape, 0)
    c = lax.broadcasted_iota(jnp.int32, w.shape, 1)
    return jnp.where(c <= r, w, 0.0).astype(BF16)


def _sgu_forward(zu, zv, lg, lb, ws_ref, bs):
    u = _gelu(zu)
    v = _gelu(zv)
    mu = jnp.mean(v, axis=-1, keepdims=True)
    vc = v - mu
    rstd = lax.rsqrt(jnp.mean(vc * vc, axis=-1, keepdims=True) + EPS)
    xh = vc * rstd
    vln = (xh * lg + lb).astype(BF16)
    mixed = []
    for g in range(D_GROUPS):
        wg = _tril_bf16(ws_ref[g])
        mixed.append(_dot(wg, vln[:, g * LANES:(g + 1) * LANES]) + bs[:, g:g + 1])
    return u, xh, rstd, vln, jnp.concatenate(mixed, axis=1)


def sgu_fwd(z, lg, lb, ws, bs_t):
    s = z.shape[0]
    nchunk = s // D_CHUNK

    def body(zu_ref, zv_ref, lg_ref, lb_ref, ws_ref, bs_ref, o_ref):
        u, _, _, _, mixed = _sgu_forward(zu_ref[...].astype(F32), zv_ref[...].astype(F32), lg_ref[...], lb_ref[...],
                                         ws_ref, bs_ref[...])
        o_ref[...] = (u * mixed).astype(BF16)

    return pl.pallas_call(
        body, name="sgu_fwd", out_shape=_sds((s, _DW), BF16), grid=(nchunk,),
        in_specs=[pl.BlockSpec((D_CHUNK, _DW), lambda n: (n, 1)), pl.BlockSpec((D_CHUNK, _DW), lambda n: (n, 2)),
                  _vec(_DW), _vec(_DW), pl.BlockSpec((D_GROUPS, D_CHUNK, D_CHUNK), lambda n: (0, 0, 0)),
                  pl.BlockSpec((D_CHUNK, LANES), lambda n: (0, 0))],
        out_specs=pl.BlockSpec((D_CHUNK, _DW), lambda n: (n, 0)),
        compiler_params=_params(("parallel",)),
    )(z, z, lg, lb, ws, bs_t)


def sgu_bwd(z, lg, lb, ws, bs_t, dycat, dy_col):
    s = z.shape[0]
    nchunk = s // D_CHUNK

    def body(zu_ref, zv_ref, lg_ref, lb_ref, ws_ref, bs_ref, dy_ref, dzu_ref, dzv_ref, dws_ref, dbs_ref, dlg_ref,
             dlb_ref):
        @pl.when(pl.program_id(0) == 0)
        def _():
            dws_ref[...] = jnp.zeros_like(dws_ref)
            dbs_ref[...] = jnp.zeros_like(dbs_ref)
            dlg_ref[...] = jnp.zeros_like(dlg_ref)
            dlb_ref[...] = jnp.zeros_like(dlb_ref)

        zu, zv, lg = zu_ref[...].astype(F32), zv_ref[...].astype(F32), lg_ref[...]
        u, xh, rstd, vln, mixed = _sgu_forward(zu, zv, lg, lb_ref[...], ws_ref, bs_ref[...])
        dy = dy_ref[...].astype(F32)
        dzu_ref[...] = (dy * mixed * _gelu_grad(zu)).astype(BF16)
        dmix = dy * u
        lane = lax.broadcasted_iota(jnp.int32, (D_CHUNK, LANES), 1)
        row = lax.broadcasted_iota(jnp.int32, (D_CHUNK, D_CHUNK), 0)
        colm = lax.broadcasted_iota(jnp.int32, (D_CHUNK, D_CHUNK), 1)
        dvln = []
        dbs = jnp.zeros((D_CHUNK, LANES), F32)
        for g in range(D_GROUPS):
            sl = slice(g * LANES, (g + 1) * LANES)
            dmg = dmix[:, sl]
            dbs = dbs + jnp.where(lane == g, jnp.sum(dmg, axis=-1, keepdims=True), 0.0)
            dmb = dmg.astype(BF16)
            dws_ref[g] += jnp.where(colm <= row, _dot(dmb, vln[:, sl], _NT), 0.0)
            dvln.append(_dot(_tril_bf16(ws_ref[g]), dmb, _TN))
        dbs_ref[...] += dbs
        dvln = jnp.concatenate(dvln, axis=1)
        dlg_ref[...] += jnp.sum(dvln * xh, axis=0, keepdims=True)
        dlb_ref[...] += jnp.sum(dvln, axis=0, keepdims=True)
        dxh = dvln * lg
        dvv = rstd * (dxh - jnp.mean(dxh, axis=-1, keepdims=True) - xh * jnp.mean(dxh * xh, axis=-1, keepdims=True))
        dzv_ref[...] = (dvv * _gelu_grad(zv)).astype(BF16)

    wsspec = pl.BlockSpec((D_GROUPS, D_CHUNK, D_CHUNK), lambda n: (0, 0, 0))
    chunk = lambda cidx: pl.BlockSpec((D_CHUNK, _DW), lambda n: (n, cidx))
    return pl.pallas_call(
        body, name="sgu_bwd",
        out_shape=(_sds((s, _DW), BF16), _sds((s, _DW), BF16), _sds((D_GROUPS, D_CHUNK, D_CHUNK)),
                   _sds((D_CHUNK, LANES)), _sds((1, _DW)), _sds((1, _DW))),
        grid=(nchunk,),
        in_specs=[chunk(1), chunk(2), _vec(_DW), _vec(_DW), wsspec, pl.BlockSpec((D_CHUNK, LANES), lambda n: (0, 0)),
                  chunk(dy_col)],
        out_specs=(chunk(0), chunk(0), wsspec, pl.BlockSpec((D_CHUNK, LANES), lambda n: (0, 0)), _vec(_DW), _vec(_DW)),
        compiler_params=_params(("arbitrary",)),
    )(z, z, lg, lb, ws, bs_t, dycat)


def ada_mod(c_all, ada_w, ada_b):
    nl, d, n = ada_w.shape
    nb = c_all.shape[0]
    tn = _tile(n, 512)

    def body(c_ref, w_ref, b_ref, o_ref):
        cv = c_ref[...]
        ca = (cv * _sigmoid(cv)).astype(BF16)
        o_ref[...] = _dot(ca, w_ref[...].astype(BF16)) + b_ref[...]

    return pl.pallas_call(
        body, name="ada_mod", out_shape=_sds((nl, nb, n)), grid=(nl, n // tn),
        in_specs=[pl.BlockSpec((nb, d), lambda l, j: (0, 0)), pl.BlockSpec((None, d, tn), lambda l, j: (l, 0, j)),
                  pl.BlockSpec((None, 1, tn), lambda l, j: (l, 0, j))],
        out_specs=pl.BlockSpec((None, nb, tn), lambda l, j: (l, 0, j)),
        compiler_params=_params(("parallel", "parallel")),
    )(c_all, ada_w, ada_b.reshape(nl, 1, n))


def ada_grad(c_all_t, dmod):
    d, nb = c_all_t.shape
    nl, _, n = dmod.shape
    tn = _tile(n, 512)
    tr = _tile(d, 256, 8)

    def body(c_ref, dm_ref, o_ref):
        cv = c_ref[...]
        ca = cv * _sigmoid(cv)
        dm = dm_ref[...]
        acc = ca[:, 0:1] * dm[0:1, :]
        for b in range(1, nb):
            acc = acc + ca[:, b:b + 1] * dm[b:b + 1, :]
        o_ref[...] = acc

    return pl.pallas_call(
        body, name="ada_grad", out_shape=_sds((nl, d, n)), grid=(nl, n // tn, d // tr),
        in_specs=[pl.BlockSpec((tr, nb), lambda l, j, r: (r, 0)), pl.BlockSpec((None, nb, tn), lambda l, j, r: (l, 0, j))],
        out_specs=pl.BlockSpec((None, tr, tn), lambda l, j, r: (l, r, j)),
        compiler_params=_params(("parallel", "parallel", "parallel")),
    )(c_all_t, dmod)


_ADAM_BLOCK = 256 * 1024


def _adam_rows(rows, cols):
    if rows * cols <= _ADAM_BLOCK or rows % 8:
        return rows
    return _tile(rows, max(8, _ADAM_BLOCK // cols), 8)


def _adam_update(w, gv, m, v):
    bc1 = 1.0 - ADAM_B1 ** ADAM_STEP
    bc2 = 1.0 - ADAM_B2 ** ADAM_STEP
    nm = ADAM_B1 * m + (1.0 - ADAM_B1) * gv
    nv = ADAM_B2 * v + (1.0 - ADAM_B2) * (gv * gv)
    return -ADAM_LR * ((nm / bc1) / (jnp.sqrt(nv / bc2) + ADAM_EPS) + ADAM_WD * w), nm, nv


def adamw(w, g, m, v, name):
    shape = w.shape
    cols = shape[-1]
    rows = w.size // cols
    tr = _adam_rows(rows, cols)

    def body(w_ref, g_ref, m_ref, v_ref, d_ref, nm_ref, nv_ref):
        d_ref[...], nm_ref[...], nv_ref[...] = _adam_update(w_ref[...], g_ref[...], m_ref[...], v_ref[...])

    spec = pl.BlockSpec((tr, cols), lambda i: (i, 0))
    out = _sds((rows, cols))
    r2 = lambda t: t.reshape(rows, cols)
    d, nm, nv = pl.pallas_call(
        body, name=name, out_shape=(out, out, out), grid=(rows // tr,),
        in_specs=[spec] * 4, out_specs=(spec,) * 3, compiler_params=_params(("parallel",)),
    )(r2(w), r2(g), r2(m), r2(v))
    return g.reshape(shape), d.reshape(shape), nm.reshape(shape), nv.reshape(shape)


def adamw_layers(w, g0, g1, m, v, name):
    _, rows, cols = w.shape
    tr = _adam_rows(rows, cols)

    def body(w_ref, g0_ref, g1_ref, m_ref, v_ref, g_ref, d_ref, nm_ref, nv_ref):
        gv = jnp.where(pl.program_id(0) == 0, g0_ref[...], g1_ref[...])
        g_ref[...] = gv
        d_ref[...], nm_ref[...], nv_ref[...] = _adam_update(w_ref[...], gv, m_ref[...], v_ref[...])

    spec = pl.BlockSpec((None, tr, cols), lambda l, i: (l, i, 0))
    gspec = pl.BlockSpec((tr, cols), lambda l, i: (i, 0))
    out = _sds((2, rows, cols))
    return pl.pallas_call(
        body, name=name, out_shape=(out, out, out, out), grid=(2, rows // tr),
        in_specs=[spec, gspec, gspec, spec, spec], out_specs=(spec,) * 4, compiler_params=_params(("parallel", "parallel")),
    )(w, g0, g1, m, v)


def sum8(gathered):
    _, r, _ = gathered.shape
    tr = _tile(r, 512, 8)

    def body(g_ref, o_ref):
        acc = g_ref[0]
        for dev in range(1, N_DEV):
            acc = acc + g_ref[dev]
        o_ref[...] = acc

    return pl.pallas_call(
        body, name="sum8", out_shape=_sds((r, LANES)), grid=(r // tr,),
        in_specs=[pl.BlockSpec((N_DEV, tr, LANES), lambda i: (0, i, 0))], out_specs=pl.BlockSpec((tr, LANES), lambda i: (i, 0)),
        compiler_params=_params(("parallel",)),
    )(gathered)


_SUM_BLOCK = 512 * 1024


def _sum_rows(rh, cols):
    return rh if rh * cols <= _SUM_BLOCK else _tile(rh, max(16, _SUM_BLOCK // cols), 16)


def pair_sum(g, recv, core, name):
    _, r, cols = g.shape
    rh = r // 2
    tr = _sum_rows(rh, cols)
    per = rh // tr

    def body(c_ref, a_ref, b_ref, o_ref):
        del c_ref
        o_ref[...] = (a_ref[...].astype(F32) + b_ref[...].astype(F32)).astype(BF16)

    grid_spec = pltpu.PrefetchScalarGridSpec(
        num_scalar_prefetch=1, grid=(N_CHIPS, per),
        in_specs=[pl.BlockSpec((None, tr, cols), lambda k, i, c: (k, c[0] * per + i, 0)),
                  pl.BlockSpec((None, tr, cols), lambda k, i, c: (k, i, 0))],
        out_specs=pl.BlockSpec((None, tr, cols), lambda k, i, c: (k, i, 0)))
    return pl.pallas_call(
        body, name=name, out_shape=_sds((N_CHIPS, rh, cols), BF16), grid_spec=grid_spec,
        compiler_params=_params(("parallel", "parallel")),
    )(core.reshape(1).astype(jnp.int32), g, recv)


def chip_sum(pair, recv, chip, core, name):
    _, rh, cols = pair.shape
    tr = _sum_rows(rh, cols)

    def body(p_ref, own_ref, r_ref, o_ref):
        del p_ref
        acc = own_ref[...].astype(F32)
        for j in range(N_CHIPS - 1):
            acc = acc + r_ref[j].astype(F32)
        o_ref[...] = acc

    grid_spec = pltpu.PrefetchScalarGridSpec(
        num_scalar_prefetch=1, grid=(rh // tr,),
        in_specs=[pl.BlockSpec((None, tr, cols), lambda i, p: (p[0], i, 0)),
                  pl.BlockSpec((N_CHIPS - 1, tr, cols), lambda i, p: (0, i, 0))],
        out_specs=pl.BlockSpec((None, tr, cols), lambda i, p: (p[1], i, 0)))
    return pl.pallas_call(
        body, name=name, out_shape=_sds((2, rh, cols)), grid_spec=grid_spec,
        compiler_params=_params(("parallel",)),
    )(jnp.stack([chip, core]).astype(jnp.int32), pair, recv)


def _place():
    return lax.axis_index("x"), lax.axis_index("y"), lax.axis_index("c")


def _other_chips(x, y):
    return [(x, 1 - y), (1 - x, y), (1 - x, 1 - y)]


_HBM = pl.BlockSpec(memory_space=pltpu.HBM)


def all_gather8(v, name):
    m, n = v.shape

    def body(x_ref, out_ref, send_sems, recv_sems, local_sem):
        x, y, c = _place()
        me, sibling = (x, y, c), (x, y, 1 - c)
        chips = _other_chips(x, y)

        def rows(px, py, pc):
            return out_ref.at[pl.ds((4 * px + 2 * py + pc) * m, m), :]

        def copy(k, block, to, src=None):
            return pltpu.make_async_remote_copy(
                src_ref=rows(*block) if src is None else src, dst_ref=rows(*block),
                send_sem=send_sems.at[k], recv_sem=recv_sems.at[k], device_id=to, device_id_type=MESH)

        mine = pltpu.make_async_copy(x_ref, rows(*me), local_sem)
        mine.start()
        first = [copy(0, me, sibling, src=x_ref)]
        first += [copy(1 + j, me, (*chip, c), src=x_ref) for j, chip in enumerate(chips)]
        for cp in first:
            cp.start()
        passed = [copy(4 + j, (*chip, c), sibling) for j, chip in enumerate(chips)]
        for j, chip in enumerate(chips):
            copy(1 + j, (*chip, c), me).wait_recv()
            passed[j].start()
        copy(0, sibling, me).wait_recv()
        for j, chip in enumerate(chips):
            copy(4 + j, (*chip, 1 - c), me).wait_recv()
        for cp in first + passed:
            cp.wait_send()
        mine.wait()

    return pl.pallas_call(
        body, name=name, out_shape=_sds((N_DEV * m, n), v.dtype),
        in_specs=[pl.BlockSpec(memory_space=pltpu.VMEM)], out_specs=pl.BlockSpec(memory_space=pltpu.VMEM),
        scratch_shapes=[pltpu.SemaphoreType.DMA((7,)), pltpu.SemaphoreType.DMA((7,)), pltpu.SemaphoreType.DMA],
        compiler_params=_params(None, VMEM_BIG),
    )(v)


def _comm_call(body, name, ins, out_shapes, nsem, aliases=None):
    return pl.pallas_call(
        body, name=name, out_shape=tuple(out_shapes), in_specs=[_HBM] * len(ins), out_specs=tuple([_HBM] * len(out_shapes)),
        scratch_shapes=[pltpu.SemaphoreType.DMA((nsem,)), pltpu.SemaphoreType.DMA((nsem,))],
        input_output_aliases=aliases or {},
    )(*ins)


def _remote(src, dst, send_sems, recv_sems, k, to):
    return pltpu.make_async_remote_copy(src_ref=src, dst_ref=dst, send_sem=send_sems.at[k], recv_sem=recv_sems.at[k],
                                        device_id=to, device_id_type=MESH)


def _half(core, rh):
    return pl.ds(pl.multiple_of(core * rh, 16), rh)


def gather_chips(shards, name):
    n = len(shards)

    def body(*refs):
        ins, outs, (send_sems, recv_sems) = refs[:n], refs[n:2 * n], refs[2 * n:]
        x, y, c = _place()
        me = 2 * x + y
        sibling = (x, y, 1 - c)
        chips = _other_chips(x, y)
        first, passed = [], []
        for i in range(n):
            mine = _half(c, ins[i].shape[0] // 2)
            for j, chip in enumerate(chips):
                cp = _remote(ins[i].at[mine], outs[i].at[me, mine], send_sems, recv_sems, 6 * i + j, (*chip, c))
                cp.start()
                first.append(cp)
        for i in range(n):
            mine = _half(c, ins[i].shape[0] // 2)
            for j, (px, py) in enumerate(chips):
                landed = outs[i].at[2 * px + py, mine]
                _remote(landed, landed, send_sems, recv_sems, 6 * i + j, (px, py, c)).wait_recv()
                fw = _remote(landed, landed, send_sems, recv_sems, 6 * i + 3 + j, sibling)
                fw.start()
                passed.append(fw)
        for i in range(n):
            theirs = _half(1 - c, ins[i].shape[0] // 2)
            for j, (px, py) in enumerate(chips):
                landed = outs[i].at[2 * px + py, theirs]
                _remote(landed, landed, send_sems, recv_sems, 6 * i + 3 + j, sibling).wait_recv()
        for cp in first + passed:
            cp.wait_send()

    return _comm_call(body, name, shards, [_sds((N_CHIPS,) + s.shape, s.dtype) for s in shards], 6 * n)


def swap_halves(gs, name):
    n = len(gs)

    def body(*refs):
        ins, outs, (send_sems, recv_sems) = refs[:n], refs[n:2 * n], refs[2 * n:]
        x, y, c = _place()
        copies = []
        for i in range(n):
            theirs = _half(1 - c, ins[i].shape[1] // 2)
            cp = _remote(ins[i].at[:, theirs], outs[i], send_sems, recv_sems, i, (x, y, 1 - c))
            cp.start()
            copies.append(cp)
        for cp in copies:
            cp.wait()

    return _comm_call(body, name, gs, [_sds((g.shape[0], g.shape[1] // 2, g.shape[2]), g.dtype) for g in gs], n)


def chip_exchange(ts, name):
    n = len(ts)

    def body(*refs):
        ins, outs, (send_sems, recv_sems) = refs[:n], refs[n:2 * n], refs[2 * n:]
        x, y, c = _place()
        copies = []
        for i in range(n):
            for j, (px, py) in enumerate(_other_chips(x, y)):
                cp = _remote(ins[i].at[2 * px + py], outs[i].at[j], send_sems, recv_sems, 3 * i + j, (px, py, c))
                cp.start()
                copies.append(cp)
        for cp in copies:
            cp.wait()

    return _comm_call(body, name, ts, [_sds((N_CHIPS - 1,) + t.shape[1:], t.dtype) for t in ts], 3 * n)


def join_halves(bufs, name):
    n = len(bufs)

    def body(*refs):
        ins, outs, (send_sems, recv_sems) = refs[:n], refs[n:2 * n], refs[2 * n:]
        x, y, c = _place()
        copies = []
        for i in range(n):
            cp = _remote(ins[i].at[c], outs[i].at[c], send_sems, recv_sems, i, (x, y, 1 - c))
            cp.start()
            copies.append(cp)
        for i in range(n):
            theirs = outs[i].at[1 - c]
            _remote(theirs, theirs, send_sems, recv_sems, i, (x, y, 1 - c)).wait_recv()
        for cp in copies:
            cp.wait_send()

    return _comm_call(body, name, bufs, [_sds(b.shape, b.dtype) for b in bufs], n, {i: i for i in range(n)})


_CD_PAD = C_Q_RANK + C_KV_RANK + HEAD_PAD + 2 * _DW


def chip_major(w, groups=N_CHIPS):
    r, c = w.shape
    return w.reshape(r, groups, c // groups).transpose(1, 0, 2)


def from_chip_major(w):
    g, r, c = w.shape
    return w.transpose(1, 0, 2).reshape(r, g * c)


def _cd_in_pad(w):
    a = C_Q_RANK + C_KV_RANK
    z = lambda n: jnp.zeros((w.shape[0], n), w.dtype)
    return jnp.concatenate([w[:, :a], z(C_NOPE), w[:, a:a + C_ROPE], z(HEAD_PAD - C_NOPE - C_ROPE), w[:, a + C_ROPE:]], axis=1)


def _cd_in_unpad(w):
    a = C_Q_RANK + C_KV_RANK
    return jnp.concatenate([w[:, :a], w[:, a + C_NOPE:a + C_NOPE + C_ROPE], w[:, a + HEAD_PAD:]], axis=1)


def _pad_heads(w, width):
    r = w.shape[0]
    w = w.reshape(r, C_HEADS, width)
    return jnp.pad(w, ((0, 0), (0, 0), (0, HEAD_PAD - width))).reshape(r, _HW)


def _unpad_heads(w, width):
    r = w.shape[0]
    return w.reshape(r, C_HEADS, HEAD_PAD)[:, :, :width].reshape(r, C_HEADS * width)


_MATMUL_WEIGHTS = ("ab_w_in", "ab_w_out", "cd_w_in", "c_w_uq", "c_w_ukv", "cd_w_out", "ffn_w_up", "ffn_w_down")
_LAYER_STACKED = ("norm1_g", "norm2_g", "ffn_w_up", "ffn_conv_w", "ffn_w_down")
_ROW_VECTORS = ("b_scale", "c_q_norm_g", "c_kv_norm_g", "d_ln_g", "d_ln_b")


def full_to_local(p):
    q = {}
    for k, v in p.items():
        if k == "final_norm_g":
            v = v.reshape(1, -1)
        elif k not in _LAYER_STACKED and k not in _ROW_VECTORS:
            v = v[0]
        q[k] = v.astype(BF16) if k in _MATMUL_WEIGHTS else v
    return q


def local_to_full(g):
    q = {}
    for k, v in g.items():
        if k == "final_norm_g":
            q[k] = v.reshape(-1)
        elif k not in _LAYER_STACKED and k not in _ROW_VECTORS:
            q[k] = v[None]
        else:
            q[k] = v
    return q


def prepare_weights(p):
    q = dict(p)
    q["cd_w_in"] = _cd_in_pad(p["cd_w_in"])
    q["c_w_uq"] = _pad_heads(p["c_w_uq"], C_NOPE + C_ROPE)
    ukv = p["c_w_ukv"].reshape(C_KV_RANK, C_HEADS, C_NOPE + C_V)
    q["c_w_uk"] = _pad_heads(ukv[:, :, :C_NOPE].reshape(C_KV_RANK, -1), C_NOPE)
    q["c_w_uv"] = _pad_heads(ukv[:, :, C_NOPE:].reshape(C_KV_RANK, -1), C_V)
    wo = p["cd_w_out"]
    att_rows = jnp.pad(wo[:C_HEADS * C_V].reshape(C_HEADS, C_V, D_MODEL), ((0, 0), (0, HEAD_PAD - C_V), (0, 0)))
    q["cd_w_out"] = jnp.concatenate([att_rows.reshape(_HW, D_MODEL), wo[C_HEADS * C_V:]], axis=0)
    return q


def unprepare_grads(g):
    q = dict(g)
    q["cd_w_in"] = _cd_in_unpad(g["cd_w_in"])
    q["c_w_uq"] = _unpad_heads(g["c_w_uq"], C_NOPE + C_ROPE)
    uk = g.pop("c_w_uk").reshape(C_KV_RANK, C_HEADS, HEAD_PAD)[:, :, :C_NOPE]
    uv = g.pop("c_w_uv").reshape(C_KV_RANK, C_HEADS, HEAD_PAD)[:, :, :C_V]
    q.pop("c_w_uk", None)
    q.pop("c_w_uv", None)
    q["c_w_ukv"] = jnp.concatenate([uk, uv], axis=-1).reshape(C_KV_RANK, C_HEADS * (C_NOPE + C_V))
    wo = g["cd_w_out"]
    att = wo[:_HW].reshape(C_HEADS, HEAD_PAD, D_MODEL)[:, :C_V].reshape(C_HEADS * C_V, D_MODEL)
    q["cd_w_out"] = jnp.concatenate([att, wo[_HW:]], axis=0)
    return q


def rope_tables(positions):
    half = C_ROPE // 2
    inv_freq = ROPE_THETA ** (-jnp.arange(half, dtype=F32) / half)
    ang = positions.astype(F32)[:, None] * inv_freq
    cos, sin = jnp.cos(ang), jnp.sin(ang)
    s = positions.shape[0]
    z = lambda n: jnp.zeros((s, n), F32)
    cs = jnp.concatenate([jnp.ones((s, C_NOPE), F32), cos, cos, z(HEAD_PAD - C_NOPE - C_ROPE)], axis=1)
    s1 = jnp.concatenate([z(C_NOPE), -sin, z(HEAD_PAD - C_NOPE - half)], axis=1)
    s2 = jnp.concatenate([z(C_NOPE + half), sin, z(HEAD_PAD - C_NOPE - C_ROPE)], axis=1)
    return cs, s1, s2


def _mods(mod_l):
    return [mod_l[:, i * D_MODEL:(i + 1) * D_MODEL] for i in range(N_MOD)]


def _ffn_fwd(x1, w, l, sc2, sh2, g2):
    n2 = w["norm2_g"][l:l + 1]
    h2 = modnorm_fwd(x1, n2, sc2, sh2, f"modnorm2_fwd{l}")
    up_cols = 2 * D_FF // N_CHIPS
    zf = matmul(h2, w["ffn_w_up"][l], "nn", BF16, f"ffn_up{l}", gb=N_CHIPS, go=2, tn=up_cols)
    a = ffn_act_fwd(zf, w["ffn_conv_w"][l], f"ffn_act_fwd{l}")
    f = matmul(a, w["ffn_w_down"][l], "nn", F32, f"ffn_down{l}", tk=D_FF)
    x2 = resid_fwd(x1, f, g2, f"resid2_fwd{l}")
    return x2, (h2, zf, a, f)


def _ffn_bwd(dres, x1, saved, w, l, sc2, g2):
    h2, zf, a, f = saved
    n2 = w["norm2_g"][l:l + 1]
    df, dg2 = gate_bwd(dres, f, g2, f"gate2_bwd{l}")
    up_cols = 2 * D_FF // N_CHIPS
    da = matmul(df, w["ffn_w_down"][l], "nt", BF16, f"ffn_down_dx{l}")
    d_down = matmul(a, df, "tn", BF16, f"ffn_down_dw{l}", tm=D_FF // 2)
    dzf, d_conv = ffn_act_bwd(zf, w["ffn_conv_w"][l], da, f"ffn_act_bwd{l}")
    dh2 = matmul(dzf, w["ffn_w_up"][l], "nt", F32, f"ffn_up_dx{l}", ga=2, gb=N_CHIPS, tk=up_cols)
    d_up = matmul(h2, dzf, "tn", BF16, f"ffn_up_dw{l}", gb=2, go=N_CHIPS, tn=up_cols)
    dres, dsh2, dsc2, dn2 = norm_bwd(x1, dh2, n2, sc2, dres, f"norm2_bwd{l}")
    d_conv = d_conv.transpose(1, 0, 2).reshape(3, 2 * D_FF)
    return dres, dict(ffn_w_down=d_down, ffn_conv_w=d_conv, ffn_w_up=d_up, norm2_g=dn2), (dsh2, dsc2, dg2)


def layer0_fwd(x0, mod0, w):
    sh1, sc1, g1, sh2, sc2, g2 = _mods(mod0)
    h = modnorm_fwd(x0, w["norm1_g"][0:1], sc1, sh1, "modnorm1_fwd0")
    z = matmul(h, w["ab_w_in"], "nn", BF16, "ab_in", gb=N_CHIPS)
    ya = gconv_fwd(z, w["a_conv_w"])
    yb = pool_fwd(z, w["b_mix_w"], w["b_scale"])
    ycat = jnp.concatenate([ya] + yb, axis=1)
    y = matmul(ycat, w["ab_w_out"], "nn", F32, "ab_out")
    x1 = resid_fwd(x0, y, g1, "resid1_fwd0")
    x2, ffn_saved = _ffn_fwd(x1, w, 0, sc2, sh2, g2)
    return x2, (x0, h, z, ycat, y, x1, ffn_saved, (sc1, g1, sc2, g2))


def layer0_bwd(dres, saved, w):
    x0, h, z, ycat, y, x1, ffn_saved, (sc1, g1, sc2, g2) = saved
    dres, grads, (dsh2, dsc2, dg2) = _ffn_bwd(dres, x1, ffn_saved, w, 0, sc2, g2)
    dy, dg1 = gate_bwd(dres, y, g1, "gate1_bwd0")
    dycat = matmul(dy, w["ab_w_out"], "nt", BF16, "ab_out_dx")
    grads["ab_w_out"] = matmul(ycat, dy, "tn", BF16, "ab_out_dw")
    db, dc, da, d_conv = gconv_bwd(z, w["a_conv_w"], dycat)
    pb = pool_bwd(z, w["b_mix_w"], w["b_scale"], dycat)
    dz = jnp.concatenate([db, dc, da] + [t[0] for t in pb], axis=1)
    dh = matmul(dz, w["ab_w_in"], "nt", F32, "ab_in_dx", gb=N_CHIPS)
    grads["ab_w_in"] = matmul(h, dz, "tn", BF16, "ab_in_dw", go=N_CHIPS)
    dres, dsh1, dsc1, dn1 = norm_bwd(x0, dh, w["norm1_g"][0:1], sc1, dres, "norm1_bwd0")
    grads.update(a_conv_w=d_conv, b_mix_w=jnp.stack([t[1] for t in pb]),
                 b_scale=jnp.concatenate([t[2] for t in pb], axis=1), norm1_g=dn1)
    return dres, grads, jnp.concatenate([dsh1, dsc1, dg1, dsh2, dsc2, dg2], axis=1)


def layer1_fwd(x0, mod1, ropes, w):
    cs, s1, s2 = ropes
    sh1, sc1, g1, sh2, sc2, g2 = _mods(mod1)
    h = modnorm_fwd(x0, w["norm1_g"][1:2], sc1, sh1, "modnorm1_fwd1")
    z = matmul(h, w["cd_w_in"], "nn", BF16, "cd_in")
    bs_t = jnp.pad(w["d_b_s"].T, ((0, 0), (0, LANES - D_GROUPS)))
    qh, kh, vh = mla_pre_fwd(z, w["c_q_norm_g"], w["c_kv_norm_g"], w["c_w_uq"], w["c_w_uk"], w["c_w_uv"], cs, s1, s2)
    oh = attn_fwd(qh, kh, vh)
    yd = sgu_fwd(z, w["d_ln_g"], w["d_ln_b"], w["d_w_s"], bs_t)
    ycat = jnp.concatenate([oh, yd], axis=1)
    y = matmul(ycat, w["cd_w_out"], "nn", F32, "cd_out")
    x1 = resid_fwd(x0, y, g1, "resid1_fwd1")
    x2, ffn_saved = _ffn_fwd(x1, w, 1, sc2, sh2, g2)
    return x2, (x0, h, z, bs_t, qh, kh, vh, oh, ycat, y, x1, ffn_saved, (sc1, g1, sc2, g2))


def layer1_bwd(dres, saved, ropes, w):
    cs, s1, s2 = ropes
    x0, h, z, bs_t, qh, kh, vh, oh, ycat, y, x1, ffn_saved, (sc1, g1, sc2, g2) = saved
    dres, grads, (dsh2, dsc2, dg2) = _ffn_bwd(dres, x1, ffn_saved, w, 1, sc2, g2)
    dy, dg1 = gate_bwd(dres, y, g1, "gate1_bwd1")
    dycat = matmul(dy, w["cd_w_out"], "nt", BF16, "cd_out_dx")
    grads["cd_w_out"] = matmul(ycat, dy, "tn", F32, "cd_out_dw")
    dqh, dkh, dvh = attn_bwd(qh, kh, vh, oh, dycat, 0)
    dzq, d_uq, d_uk, d_uv, d_gq, d_gkv = mla_pre_bwd(
        z, w["c_q_norm_g"], w["c_kv_norm_g"], w["c_w_uq"], w["c_w_uk"], w["c_w_uv"], cs, s1, s2, dqh, dkh, dvh)
    dzu, dzv, d_ws, d_bs, d_lg, d_lb = sgu_bwd(z, w["d_ln_g"], w["d_ln_b"], w["d_w_s"], bs_t, dycat, _HW // _DW)
    dz = jnp.concatenate([dzq, dzu, dzv], axis=1)
    dh = matmul(dz, w["cd_w_in"], "nt", F32, "cd_in_dx")
    grads["cd_w_in"] = matmul(h, dz, "tn", F32, "cd_in_dw")
    dres, dsh1, dsc1, dn1 = norm_bwd(x0, dh, w["norm1_g"][1:2], sc1, dres, "norm1_bwd1")
    grads.update(c_w_uq=d_uq, c_w_uk=d_uk, c_w_uv=d_uv, c_q_norm_g=d_gq, c_kv_norm_g=d_gkv, d_w_s=d_ws,
                 d_b_s=d_bs[:, :D_GROUPS].T, d_ln_g=d_lg, d_ln_b=d_lb, norm1_g=dn1)
    return dres, grads, jnp.concatenate([dsh1, dsc1, dg1, dsh2, dsc2, dg2], axis=1)


def _merge_layer_grads(g0, g1):
    grads = {k: v for k, v in g0.items() if k not in ("ffn_w_down", "ffn_conv_w", "ffn_w_up", "norm2_g", "norm1_g")}
    grads.update({k: v for k, v in g1.items() if k not in ("ffn_w_down", "ffn_conv_w", "ffn_w_up", "norm2_g", "norm1_g")})
    for k in ("ffn_w_down", "ffn_w_up"):
        grads[k] = [g0[k], g1[k]]
    grads["ffn_conv_w"] = jnp.stack([g0["ffn_conv_w"], g1["ffn_conv_w"]])
    grads["norm1_g"] = jnp.concatenate([g0["norm1_g"], g1["norm1_g"]], axis=0)
    grads["norm2_g"] = jnp.concatenate([g0["norm2_g"], g1["norm2_g"]], axis=0)
    return grads


def local_step(x, tgt, mod, ropes, w):
    x1, saved0 = layer0_fwd(x, mod[0:1], w)
    x2, saved1 = layer1_fwd(x1, mod[1:2], ropes, w)
    dres, d_final, loss = final_fwd_bwd(x2, w["final_norm_g"], tgt)
    dres, g1, dmod1 = layer1_bwd(dres, saved1, ropes, w)
    dres, g0, dmod0 = layer0_bwd(dres, saved0, w)
    grads = _merge_layer_grads(g0, g1)
    grads["final_norm_g"] = d_final
    return loss, dres, jnp.concatenate([dmod0, dmod1], axis=0), grads


_WEIGHTS = ("ada_w", "ada_b", "norm1_g", "norm2_g", "ab_w_in", "a_conv_w", "b_mix_w", "b_scale", "ab_w_out", "cd_w_in",
            "c_q_norm_g", "c_w_uq", "c_kv_norm_g", "c_w_ukv", "d_ln_g", "d_ln_b", "d_w_s", "d_b_s", "cd_w_out",
            "ffn_w_up", "ffn_conv_w", "ffn_w_down", "final_norm_g")
_INPUTS = ("x", "c", "positions") + _WEIGHTS + ("loss_target",) + tuple("m_" + n for n in _WEIGHTS) + tuple(
    "v_" + n for n in _WEIGHTS)

def _pack_rows(parts, rows, dtype):
    flat = jnp.concatenate([p.reshape(-1).astype(dtype) for p in parts])
    return jnp.pad(flat, (0, rows * LANES - flat.shape[0])).reshape(rows, LANES)


def _rows_major(w):
    r, c = w.shape
    return w.reshape(N_CHIPS, r // N_CHIPS, c)


def gather_weights(shards, chip, name):
    outs = gather_chips(shards, name)
    return [lax.dynamic_update_index_in_dim(o, s, chip, 0) for o, s in zip(outs, shards)]


def reduce_grads(gs, chip, core, tag):
    recv = swap_halves(gs, "swap_halves_" + tag)
    pairs = [pair_sum(g, r, core, f"pair_sum_{tag}{i}") for i, (g, r) in enumerate(zip(gs, recv))]
    others = chip_exchange(pairs, "chip_exchange_" + tag)
    halves = [chip_sum(p, o, chip, core, f"chip_sum_{tag}{i}") for i, (p, o) in enumerate(zip(pairs, others))]
    full = join_halves(halves, "join_halves_" + tag)
    return [f.reshape(f.shape[1] * 2, f.shape[2]) for f in full]


_SMALL_SHARDED = (("a_conv_w", (3, 128), 1), ("c_q_norm_g", (1, 64), 1), ("d_ln_g", (1, 128), 1), ("d_ln_b", (1, 128), 1),
                  ("ffn_conv_w", (2, 3, 2 * D_FF // N_CHIPS), 2))
_SMALL_GRADS = (("norm1_g", (2, D_MODEL)), ("norm2_g", (2, D_MODEL)), ("b_mix_w", (4, 128, 128)), ("b_scale", (1, 512)),
                ("c_kv_norm_g", (1, 128)), ("d_w_s", (4, 128, 128)), ("d_b_s", (4, 128)), ("final_norm_g", (1, D_MODEL)),
                ("a_conv_w", (3, 512)), ("c_q_norm_g", (1, 256)), ("d_ln_g", (1, 512)), ("d_ln_b", (1, 512)),
                ("ffn_conv_w", (2, 3, 2 * D_FF)))


def _size(shape):
    n = 1
    for d in shape:
        n *= d
    return n


def kernel(x, c, positions, ada_w, ada_b, norm1_g, norm2_g, ab_w_in, a_conv_w, b_mix_w, b_scale, ab_w_out, cd_w_in, c_q_norm_g, c_w_uq, c_kv_norm_g, c_w_ukv, d_ln_g, d_ln_b, d_w_s, d_b_s, cd_w_out, ffn_w_up, ffn_conv_w, ffn_w_down, final_norm_g, loss_target, m_ada_w, m_ada_b, m_norm1_g, m_norm2_g, m_ab_w_in, m_a_conv_w, m_b_mix_w, m_b_scale, m_ab_w_out, m_cd_w_in, m_c_q_norm_g, m_c_w_uq, m_c_kv_norm_g, m_c_w_ukv, m_d_ln_g, m_d_ln_b, m_d_w_s, m_d_b_s, m_cd_w_out, m_ffn_w_up, m_ffn_conv_w, m_ffn_w_down, m_final_norm_g, v_ada_w, v_ada_b, v_norm1_g, v_norm2_g, v_ab_w_in, v_a_conv_w, v_b_mix_w, v_b_scale, v_ab_w_out, v_cd_w_in, v_c_q_norm_g, v_c_w_uq, v_c_kv_norm_g, v_c_w_ukv, v_d_ln_g, v_d_ln_b, v_d_w_s, v_d_b_s, v_cd_w_out, v_ffn_w_up, v_ffn_conv_w, v_ffn_w_down, v_final_norm_g):
    args = (x, c, positions, ada_w, ada_b, norm1_g, norm2_g, ab_w_in, a_conv_w, b_mix_w, b_scale, ab_w_out, cd_w_in, c_q_norm_g, c_w_uq, c_kv_norm_g, c_w_ukv, d_ln_g, d_ln_b, d_w_s, d_b_s, cd_w_out, ffn_w_up, ffn_conv_w, ffn_w_down, final_norm_g, loss_target, m_ada_w, m_ada_b, m_norm1_g, m_norm2_g, m_ab_w_in, m_a_conv_w, m_b_mix_w, m_b_scale, m_ab_w_out, m_cd_w_in, m_c_q_norm_g, m_c_w_uq, m_c_kv_norm_g, m_c_w_ukv, m_d_ln_g, m_d_ln_b, m_d_w_s, m_d_b_s, m_cd_w_out, m_ffn_w_up, m_ffn_conv_w, m_ffn_w_down, m_final_norm_g, v_ada_w, v_ada_b, v_norm1_g, v_norm2_g, v_ab_w_in, v_a_conv_w, v_b_mix_w, v_b_scale, v_ab_w_out, v_cd_w_in, v_c_q_norm_g, v_c_w_uq, v_c_kv_norm_g, v_c_w_ukv, v_d_ln_g, v_d_ln_b, v_d_w_s, v_d_b_s, v_cd_w_out, v_ffn_w_up, v_ffn_conv_w, v_ffn_w_down, v_final_norm_g)
    a = dict(zip(_INPUTS, args, strict=True))
    xi, yi, ci = _place()
    chip = 2 * xi + yi
    dev = 4 * xi + 2 * yi + ci
    x = a["x"][0]
    tgt = a["loss_target"][0]
    seq = x.shape[0]

    small_parts = [a["c"]] + [a[n] for n, _, _ in _SMALL_SHARDED]
    rows1 = -(-sum(p.size for p in small_parts) // LANES // 8) * 8
    g1 = all_gather8(_pack_rows(small_parts, rows1, F32), "gather_small").reshape(N_DEV, rows1 * LANES)
    c_all = g1[:, :D_MODEL]
    per_chip = g1[0::2]
    small_full = {}
    off = D_MODEL
    for n, shp, axis in _SMALL_SHARDED:
        piece = per_chip[:, off:off + _size(shp)].reshape((N_CHIPS,) + shp)
        small_full[n] = jnp.concatenate([piece[k] for k in range(N_CHIPS)], axis=axis)
        off += _size(shp)

    bf = lambda t: t.astype(BF16)
    merge = lambda t: t.reshape(t.shape[0] * t.shape[1], t.shape[2])
    w_in0, w_out0, up0, down0 = gather_weights(
        [bf(a["ab_w_in"][0]), bf(a["ab_w_out"][0]), bf(a["ffn_w_up"][0]), bf(a["ffn_w_down"][0])], chip, "gather_layer0")
    cd_in, uq, ukv, cd_out, up1, down1 = gather_weights(
        [bf(a["cd_w_in"][0]), bf(a["c_w_uq"][0]), bf(a["c_w_ukv"][0]), bf(a["cd_w_out"][0]), bf(a["ffn_w_up"][1]),
         bf(a["ffn_w_down"][1])], chip, "gather_layer1")
    w = prepare_weights(dict(
        ab_w_in=w_in0, ab_w_out=merge(w_out0), ffn_w_up=[up0, up1], ffn_w_down=[merge(down0), merge(down1)],
        cd_w_in=from_chip_major(cd_in), c_w_uq=from_chip_major(uq), c_w_ukv=from_chip_major(ukv), cd_w_out=merge(cd_out),
        norm1_g=a["norm1_g"], norm2_g=a["norm2_g"], b_mix_w=a["b_mix_w"][0], b_scale=a["b_scale"],
        c_kv_norm_g=a["c_kv_norm_g"], d_w_s=a["d_w_s"][0], d_b_s=a["d_b_s"][0],
        final_norm_g=a["final_norm_g"].reshape(1, D_MODEL), **small_full))

    ncol = N_MOD * D_MODEL // N_CHIPS
    ada_b_mine = lax.dynamic_slice_in_dim(a["ada_b"], chip * ncol, ncol, axis=1)
    mod_cols = ada_mod(c_all, a["ada_w"], ada_b_mine)
    g2 = all_gather8(mod_cols.reshape(-1, LANES), "gather_mod").reshape(N_DEV, 2, N_DEV, ncol)
    mod = lax.dynamic_index_in_dim(g2[0::2], dev, axis=2, keepdims=False)
    mod = mod.transpose(1, 0, 2).reshape(2, N_MOD * D_MODEL)

    ropes = rope_tables(a["positions"][0])
    cm16 = lambda t: chip_major(t).astype(BF16)
    x_mid, saved0 = layer0_fwd(x, mod[0:1], w)
    x_end, saved1 = layer1_fwd(x_mid, mod[1:2], ropes, w)
    dres, d_final, loss = final_fwd_bwd(x_end, w["final_norm_g"], tgt)
    loss = lax.psum(loss[0, 0], ("x", "y", "c"))
    dres, g1, dmod1 = layer1_bwd(dres, saved1, ropes, w)
    g1 = unprepare_grads(g1)
    red_cd_in, red_uq, red_ukv, red_cd_out, red_up1, red_down1 = reduce_grads(
        [cm16(g1["cd_w_in"]), cm16(g1["c_w_uq"]), cm16(g1["c_w_ukv"]), _rows_major(g1["cd_w_out"]).astype(BF16),
         g1["ffn_w_up"], _rows_major(g1["ffn_w_down"])], chip, ci, "l1")
    grad_x, g0, dmod0 = layer0_bwd(dres, saved0, w)
    red_in0, red_out0, red_up0, red_down0 = reduce_grads(
        [g0["ab_w_in"], _rows_major(g0["ab_w_out"]), g0["ffn_w_up"], _rows_major(g0["ffn_w_down"])], chip, ci, "l0")
    grads = _merge_layer_grads(g0, g1)
    grads["final_norm_g"] = d_final
    dmod = jnp.concatenate([dmod0, dmod1], axis=0)

    parts3 = [dmod] + [grads[n] for n, _ in _SMALL_GRADS]
    rows3 = -(-sum(p.size for p in parts3) // LANES // 8) * 8
    g3 = all_gather8(_pack_rows(parts3, rows3, F32), "gather_grads").reshape(N_DEV, rows3, LANES)
    summed = sum8(g3).reshape(-1)
    nmod = 2 * N_MOD * D_MODEL
    out_grads = {"ada_b": summed[:nmod].reshape(2, N_MOD * D_MODEL)}
    off = nmod
    for n, shp in _SMALL_GRADS:
        out_grads[n] = summed[off:off + _size(shp)].reshape(shp)
        off += _size(shp)
    for n, shp, axis in _SMALL_SHARDED:
        width = out_grads[n].shape[-1] // N_CHIPS
        out_grads[n] = lax.dynamic_slice_in_dim(out_grads[n], chip * width, width, axis=out_grads[n].ndim - 1)
    dmod_all = g3.reshape(N_DEV, rows3 * LANES)[:, :nmod].reshape(N_DEV, 2, N_MOD * D_MODEL)
    dmod_mine = lax.dynamic_slice_in_dim(dmod_all, chip * ncol, ncol, axis=2).transpose(1, 0, 2)
    out_grads["ada_w"] = ada_grad(c_all.T, dmod_mine)

    out_grads.update(ab_w_in=red_in0, ab_w_out=red_out0, cd_w_in=red_cd_in, c_w_uq=red_uq, c_w_ukv=red_ukv,
                     cd_w_out=red_cd_out)
    per_layer = dict(ffn_w_up=(red_up0, red_up1), ffn_w_down=(red_down0, red_down1))

    g_out, d_out, m_out, v_out = [], [], [], []
    for n in _WEIGHTS:
        if n in per_layer:
            g, d, nm, nv = adamw_layers(a[n], *per_layer[n], a["m_" + n], a["v_" + n], "adamw_" + n)
        else:
            g, d, nm, nv = adamw(a[n], out_grads[n].reshape(a[n].shape), a["m_" + n], a["v_" + n], "adamw_" + n)
        g_out.append(g)
        d_out.append(d)
        m_out.append(nm)
        v_out.append(nv)
    return (loss, grad_x[None], *g_out, *d_out, *m_out, *v_out)
```

```python
import functools

import jax
import jax.numpy as jnp
from jax import lax
from jax.experimental import pallas as pl
from jax.experimental.pallas import tpu as pltpu

F32 = jnp.float32
BF16 = jnp.bfloat16
EPS = 1e-6
D_MODEL = 1024
N_MOD = 6
A_WIDTH = 512
B_GROUPS = 4
POOL_WINDOWS = (2, 4, 8, 16)
C_HEADS = 8
C_NOPE = 64
C_ROPE = 32
C_V = 64
C_Q_RANK = 256
C_KV_RANK = 128
HEAD_PAD = 128
ROPE_THETA = 10000.0
D_GROUPS = 4
D_CHUNK = 128
D_FF = 2816
FF_UNIT = 128
ADAM_LR = 0.001
ADAM_B1 = 0.9
ADAM_B2 = 0.999
ADAM_EPS = 1e-08
ADAM_WD = 0.01
ADAM_STEP = 10
N_CHIPS = 4
N_DEV = 8
LANES = 128
VMEM_BIG = 56 * 1024 * 1024
MESH = pl.DeviceIdType.MESH


def _sds(shape, dtype=F32):
    return jax.ShapeDtypeStruct(tuple(shape), dtype)


def _tile(n, cap, mult=128):
    if n <= cap:
        return n
    best = None
    for t in range(mult, cap + 1, mult):
        if n % t == 0:
            best = t
    assert best is not None, (n, cap, mult)
    return best


def _params(dims=None, vmem=None):
    return pltpu.CompilerParams(dimension_semantics=dims, vmem_limit_bytes=vmem)


def _shift_down(v, k):
    r = pltpu.roll(v, k, axis=0)
    t = lax.broadcasted_iota(jnp.int32, v.shape, 0)
    return jnp.where(t >= k, r, 0.0)


def _shift_up(v, k):
    n = v.shape[0]
    r = pltpu.roll(v, n - k, axis=0)
    t = lax.broadcasted_iota(jnp.int32, v.shape, 0)
    return jnp.where(t < n - k, r, 0.0)


def _sigmoid(v):
    return 1.0 / (1.0 + jnp.exp(-v))


_GELU_C = 0.7978845608028654
_GELU_A = 0.044715


def _gelu(v):
    return 0.5 * v * (1.0 + jnp.tanh(_GELU_C * (v + _GELU_A * v * v * v)))


def _gelu_grad(v):
    th = jnp.tanh(_GELU_C * (v + _GELU_A * v * v * v))
    return 0.5 * (1.0 + th) + 0.5 * v * (1.0 - th * th) * _GELU_C * (1.0 + 3.0 * _GELU_A * v * v)


_NN = (((1,), (0,)), ((), ()))
_NT = (((1,), (1,)), ((), ()))
_TN = (((0,), (0,)), ((), ()))


def _dot(a, b, dims=_NN):
    return lax.dot_general(a, b, dims, preferred_element_type=F32)


def _logical(t, groups):
    return (t.shape[-2], t.shape[-1] * groups)


def _block(tr, tc, groups, cols, where):
    if groups == 1:
        return pl.BlockSpec((tr, tc), where)
    per = cols // groups // tc

    def index(i, j, s):
        r, c = where(i, j, s)
        return (c // per, r, c % per)

    return pl.BlockSpec((None, tr, tc), index)


def matmul(a, b, mode, out_dtype, name, ga=1, gb=1, go=1, tm=None, tn=None, tk=None):
    (ar, ac), (br, bc) = _logical(a, ga), _logical(b, gb)
    if mode == "nn":
        m, k, n = ar, ac, bc
        a_col, b_col = "k", "n"
    elif mode == "nt":
        m, k, n = ar, ac, br
        a_col, b_col = "k", "k"
    else:
        k, m, n = ar, ac, bc
        a_col, b_col = "m", "n"
    limit = {"m": m, "n": n // go, "k": k}
    limit[a_col] = min(limit[a_col], ac // ga)
    limit[b_col] = min(limit[b_col], bc // gb)
    tm = tm or _tile(limit["m"], 1024, 128 if mode == "tn" else 16)
    tn = tn or _tile(limit["n"], 512)
    tk = tk or _tile(limit["k"], 2048, 16 if mode == "tn" else 128)
    nk = k // tk
    if mode == "nn":
        a_spec = _block(tm, tk, ga, ac, lambda i, j, s: (i, s))
        b_spec = _block(tk, tn, gb, bc, lambda i, j, s: (s, j))
        dims = _NN
    elif mode == "nt":
        a_spec = _block(tm, tk, ga, ac, lambda i, j, s: (i, s))
        b_spec = _block(tn, tk, gb, bc, lambda i, j, s: (j, s))
        dims = _NT
    else:
        a_spec = _block(tk, tm, ga, ac, lambda i, j, s: (s, i))
        b_spec = _block(tk, tn, gb, bc, lambda i, j, s: (s, j))
        dims = _TN
    o_spec = _block(tm, tn, go, n, lambda i, j, s: (i, j))
    out_shape = _sds((m, n), out_dtype) if go == 1 else _sds((go, m, n // go), out_dtype)

    def body(a_ref, b_ref, o_ref, acc_ref):
        s = pl.program_id(2)

        @pl.when(s == 0)
        def _():
            acc_ref[...] = jnp.zeros_like(acc_ref)

        acc_ref[...] += _dot(a_ref[...], b_ref[...], dims)

        @pl.when(s == nk - 1)
        def _():
            o_ref[...] = acc_ref[...].astype(o_ref.dtype)

    return pl.pallas_call(
        body, name=name, out_shape=out_shape, grid=(m // tm, n // tn, nk),
        in_specs=[a_spec, b_spec], out_specs=o_spec,
        scratch_shapes=[pltpu.VMEM((tm, tn), F32)],
        compiler_params=_params(("parallel", "parallel", "arbitrary"), VMEM_BIG),
    )(a, b)


def _rows(tm, n):
    return pl.BlockSpec((tm, n), lambda i: (i, 0))


def _vec(n):
    return pl.BlockSpec((1, n), lambda i: (0, 0))


def modnorm_fwd(x, g, sc, sh, name):
    s, d = x.shape
    tm = _tile(s, 256, 8)

    def body(x_ref, g_ref, sc_ref, sh_ref, o_ref):
        xv = x_ref[...]
        r = lax.rsqrt(jnp.mean(xv * xv, axis=-1, keepdims=True) + EPS)
        o_ref[...] = ((xv * r) * g_ref[...] * (1.0 + sc_ref[...]) + sh_ref[...]).astype(BF16)

    return pl.pallas_call(
        body, name=name, out_shape=_sds((s, d), BF16), grid=(s // tm,),
        in_specs=[_rows(tm, d), _vec(d), _vec(d), _vec(d)], out_specs=_rows(tm, d),
        compiler_params=_params(("parallel",)),
    )(x, g, sc, sh)


def resid_fwd(x, y, gate, name):
    s, d = x.shape
    tm = _tile(s, 256, 8)

    def body(x_ref, y_ref, g_ref, o_ref):
        o_ref[...] = x_ref[...] + g_ref[...] * y_ref[...]

    return pl.pallas_call(
        body, name=name, out_shape=_sds((s, d)), grid=(s // tm,),
        in_specs=[_rows(tm, d), _rows(tm, d), _vec(d)], out_specs=_rows(tm, d),
        compiler_params=_params(("parallel",)),
    )(x, y, gate)


def gate_bwd(dres, y, gate, name):
    s, d = dres.shape
    tm = _tile(s, 256, 8)

    def body(dr_ref, y_ref, g_ref, dy_ref, dg_ref):
        @pl.when(pl.program_id(0) == 0)
        def _():
            dg_ref[...] = jnp.zeros_like(dg_ref)

        dr = dr_ref[...]
        dy_ref[...] = (dr * g_ref[...]).astype(BF16)
        dg_ref[...] += jnp.sum(dr * y_ref[...], axis=0, keepdims=True)

    return pl.pallas_call(
        body, name=name, out_shape=(_sds((s, d), BF16), _sds((1, d))), grid=(s // tm,),
        in_specs=[_rows(tm, d), _rows(tm, d), _vec(d)], out_specs=(_rows(tm, d), _vec(d)),
        compiler_params=_params(("arbitrary",)),
    )(dres, y, gate)


def norm_bwd(x, dh, g, sc, dres, name):
    s, d = x.shape
    tm = _tile(s, 256, 8)
    nsteps = s // tm

    def body(x_ref, dh_ref, g_ref, sc_ref, dr_ref, dx_ref, dsh_ref, dsc_ref, dg_ref, a2_ref):
        i = pl.program_id(0)

        @pl.when(i == 0)
        def _():
            dsh_ref[...] = jnp.zeros_like(dsh_ref)
            a2_ref[...] = jnp.zeros_like(a2_ref)

        xv = x_ref[...]
        dh = dh_ref[...]
        r = lax.rsqrt(jnp.mean(xv * xv, axis=-1, keepdims=True) + EPS)
        xh = xv * r
        dsh_ref[...] += jnp.sum(dh, axis=0, keepdims=True)
        a2_ref[...] += jnp.sum(dh * xh, axis=0, keepdims=True)
        dxh = dh * (g_ref[...] * (1.0 + sc_ref[...]))
        dx = r * (dxh - xh * jnp.mean(dxh * xh, axis=-1, keepdims=True))
        dx_ref[...] = dr_ref[...] + dx

        @pl.when(i == nsteps - 1)
        def _():
            dsc_ref[...] = a2_ref[...] * g_ref[...]
            dg_ref[...] = a2_ref[...] * (1.0 + sc_ref[...])

    return pl.pallas_call(
        body, name=name, out_shape=(_sds((s, d)), _sds((1, d)), _sds((1, d)), _sds((1, d))), grid=(nsteps,),
        in_specs=[_rows(tm, d), _rows(tm, d), _vec(d), _vec(d), _rows(tm, d)],
        out_specs=(_rows(tm, d), _vec(d), _vec(d), _vec(d)),
        scratch_shapes=[pltpu.VMEM((1, d), F32)],
        compiler_params=_params(("arbitrary",)),
    )(x, dh, g, sc, dres)


def final_fwd_bwd(x, g, tgt):
    s, d = x.shape
    tm = _tile(s, 256, 8)

    def body(x_ref, g_ref, t_ref, dx_ref, dg_ref, loss_ref):
        @pl.when(pl.program_id(0) == 0)
        def _():
            dg_ref[...] = jnp.zeros_like(dg_ref)
            loss_ref[...] = jnp.zeros_like(loss_ref)

        xv = x_ref[...]
        gv = g_ref[...]
        r = lax.rsqrt(jnp.mean(xv * xv, axis=-1, keepdims=True) + EPS)
        xh = xv * r
        e = xh * gv - t_ref[...]
        row = jnp.sum(e * e, axis=-1, keepdims=True) * (0.5 / d)
        loss_ref[...] += jnp.sum(row, axis=0, keepdims=True)
        dy = e * (1.0 / d)
        dg_ref[...] += jnp.sum(dy * xh, axis=0, keepdims=True)
        dxh = dy * gv
        dx_ref[...] = r * (dxh - xh * jnp.mean(dxh * xh, axis=-1, keepdims=True))

    return pl.pallas_call(
        body, name="final_fwd_bwd", out_shape=(_sds((s, d)), _sds((1, d)), _sds((1, LANES))), grid=(s // tm,),
        in_specs=[_rows(tm, d), _vec(d), _rows(tm, d)], out_specs=(_rows(tm, d), _vec(d), _vec(LANES)),
        compiler_params=_params(("arbitrary",)),
    )(x, g, tgt)


def _conv3(v, w):
    return w[0:1, :] * _shift_down(v, 2) + w[1:2, :] * _shift_down(v, 1) + w[2:3, :] * v


def _conv3_t(dv, w):
    return w[0:1, :] * _shift_up(dv, 2) + w[1:2, :] * _shift_up(dv, 1) + w[2:3, :] * dv


def _conv3_dw(dv, v):
    return jnp.concatenate([
        jnp.sum(dv * _shift_down(v, 2), axis=0, keepdims=True),
        jnp.sum(dv * _shift_down(v, 1), axis=0, keepdims=True),
        jnp.sum(dv * v, axis=0, keepdims=True)], axis=0)


def gconv_fwd(z, conv_w):
    s = z.shape[0]
    nb = A_WIDTH // LANES

    def body(b_ref, c_ref, a_ref, w_ref, o_ref):
        b, c, a = b_ref[...].astype(F32), c_ref[...].astype(F32), a_ref[...].astype(F32)
        o_ref[...] = (b * _conv3(c * a, w_ref[...])).astype(BF16)

    col = lambda off: pl.BlockSpec((s, LANES), lambda j: (0, off + j))
    return pl.pallas_call(
        body, name="gconv_fwd", out_shape=_sds((s, A_WIDTH), BF16), grid=(nb,),
        in_specs=[col(0), col(nb), col(2 * nb), pl.BlockSpec((3, LANES), lambda j: (0, j))],
        out_specs=pl.BlockSpec((s, LANES), lambda j: (0, j)),
        compiler_params=_params(("parallel",), VMEM_BIG),
    )(z, z, z, conv_w)


def gconv_bwd(z, conv_w, dycat):
    s = z.shape[0]
    nb = A_WIDTH // LANES

    def body(b_ref, c_ref, a_ref, w_ref, dy_ref, db_ref, dc_ref, da_ref, dw_ref):
        c, a, w, dy = c_ref[...].astype(F32), a_ref[...].astype(F32), w_ref[...], dy_ref[...].astype(F32)
        ca = c * a
        db_ref[...] = (dy * _conv3(ca, w)).astype(BF16)
        dconv = dy * b_ref[...].astype(F32)
        dw_ref[...] = _conv3_dw(dconv, ca)
        dca = _conv3_t(dconv, w)
        dc_ref[...] = (dca * a).astype(BF16)
        da_ref[...] = (dca * c).astype(BF16)

    col = lambda off: pl.BlockSpec((s, LANES), lambda j: (0, off + j))
    wspec = pl.BlockSpec((3, LANES), lambda j: (0, j))
    part = _sds((s, A_WIDTH), BF16)
    return pl.pallas_call(
        body, name="gconv_bwd", out_shape=(part, part, part, _sds((3, A_WIDTH))), grid=(nb,),
        in_specs=[col(0), col(nb), col(2 * nb), wspec, col(0)],
        out_specs=(col(0), col(0), col(0), wspec),
        compiler_params=_params(("parallel",), VMEM_BIG),
    )(z, z, z, conv_w, dycat)


def _pool_counts(s, w):
    t = lax.broadcasted_iota(jnp.int32, (s, 1), 0)
    return jnp.minimum(t + 1, w).astype(F32)


def _pooled(p, levels):
    acc = p
    for lv in range(levels):
        acc = acc + _shift_down(acc, 2 ** lv)
    return acc / _pool_counts(p.shape[0], 2 ** levels) - p


def pool_fwd(z, mix_w, scale):
    s = z.shape[0]

    def make(g):
        def body_g(p_ref, m_ref, sc_ref, o_ref):
            pooled = _pooled(p_ref[...].astype(F32), g + 1)
            y = _dot(pooled.astype(BF16), m_ref[...].astype(BF16))
            o_ref[...] = (y * sc_ref[...]).astype(BF16)
        return body_g

    outs = []
    for g in range(B_GROUPS):
        outs.append(pl.pallas_call(
            make(g), name=f"pool_fwd{g}", out_shape=_sds((s, LANES), BF16), grid=(1,),
            in_specs=[pl.BlockSpec((s, LANES), lambda i, g=g: (0, 3 * (A_WIDTH // LANES) + g)),
                      pl.BlockSpec((None, LANES, LANES), lambda i, g=g: (g, 0, 0)),
                      pl.BlockSpec((1, LANES), lambda i, g=g: (0, g))],
            out_specs=pl.BlockSpec((s, LANES), lambda i: (0, 0)),
            compiler_params=_params(("arbitrary",), VMEM_BIG),
        )(z, mix_w, scale))
    return outs


def pool_bwd(z, mix_w, scale, dycat):
    s = z.shape[0]

    def make(g):
        w = 2 ** (g + 1)

        def body_g(p_ref, m_ref, sc_ref, dy_ref, dp_ref, dm_ref, dsc_ref):
            pooled = _pooled(p_ref[...].astype(F32), g + 1)
            mw = m_ref[...].astype(BF16)
            pb = pooled.astype(BF16)
            dy = dy_ref[...].astype(F32)
            dsc_ref[...] = jnp.sum(dy * _dot(pb, mw), axis=0, keepdims=True)
            dmix = (dy * sc_ref[...]).astype(BF16)
            dm_ref[...] = _dot(pb, dmix, _TN)
            dpool = _dot(dmix, mw, _NT)
            acc = dpool / _pool_counts(s, w)
            for lv in range(g + 1):
                acc = acc + _shift_up(acc, 2 ** lv)
            dp_ref[...] = (acc - dpool).astype(BF16)
        return body_g

    outs = []
    for g in range(B_GROUPS):
        outs.append(pl.pallas_call(
            make(g), name=f"pool_bwd{g}",
            out_shape=(_sds((s, LANES), BF16), _sds((LANES, LANES)), _sds((1, LANES))), grid=(1,),
            in_specs=[pl.BlockSpec((s, LANES), lambda i, g=g: (0, 3 * (A_WIDTH // LANES) + g)),
                      pl.BlockSpec((None, LANES, LANES), lambda i, g=g: (g, 0, 0)),
                      pl.BlockSpec((1, LANES), lambda i, g=g: (0, g)),
                      pl.BlockSpec((s, LANES), lambda i, g=g: (0, A_WIDTH // LANES + g))],
            out_specs=(pl.BlockSpec((s, LANES), lambda i: (0, 0)), pl.BlockSpec((LANES, LANES), lambda i: (0, 0)),
                       pl.BlockSpec((1, LANES), lambda i: (0, 0))),
            compiler_params=_params(("arbitrary",), VMEM_BIG),
        )(z, mix_w, scale, dycat))
    return outs


_FF_BLOCKS = D_FF // FF_UNIT


def _ff_spec(s):
    return pl.BlockSpec((2, s, FF_UNIT), lambda j: (0, 0, j))


def _ff_wspecs():
    return [pl.BlockSpec((3, FF_UNIT), lambda j: (0, j)), pl.BlockSpec((3, FF_UNIT), lambda j: (0, _FF_BLOCKS + j))]


def ffn_act_fwd(zf, conv_w, name):
    s = zf.shape[1]

    def body(z_ref, wg_ref, wu_ref, o_ref):
        g = _conv3(z_ref[0].astype(F32), wg_ref[...])
        u = _conv3(z_ref[1].astype(F32), wu_ref[...])
        o_ref[...] = (g * _sigmoid(g) * u).astype(BF16)

    return pl.pallas_call(
        body, name=name, out_shape=_sds((s, D_FF), BF16), grid=(_FF_BLOCKS,),
        in_specs=[_ff_spec(s)] + _ff_wspecs(), out_specs=pl.BlockSpec((s, FF_UNIT), lambda j: (0, j)),
        compiler_params=_params(("parallel",), VMEM_BIG),
    )(zf, conv_w, conv_w)


def ffn_act_bwd(zf, conv_w, da, name):
    s = zf.shape[1]

    def body(z_ref, wg_ref, wu_ref, da_ref, dz_ref, dw_ref):
        zg, zu = z_ref[0].astype(F32), z_ref[1].astype(F32)
        wg, wu = wg_ref[...], wu_ref[...]
        dav = da_ref[...].astype(F32)
        g = _conv3(zg, wg)
        u = _conv3(zu, wu)
        sg = _sigmoid(g)
        dg = dav * u * (sg * (1.0 + g * (1.0 - sg)))
        du = dav * (g * sg)
        dw_ref[0] = _conv3_dw(dg, zg)
        dw_ref[1] = _conv3_dw(du, zu)
        dz_ref[0] = _conv3_t(dg, wg).astype(BF16)
        dz_ref[1] = _conv3_t(du, wu).astype(BF16)

    return pl.pallas_call(
        body, name=name, out_shape=(_sds((2, s, D_FF), BF16), _sds((2, 3, D_FF))), grid=(_FF_BLOCKS,),
        in_specs=[_ff_spec(s)] + _ff_wspecs() + [pl.BlockSpec((s, FF_UNIT), lambda j: (0, j))],
        out_specs=(_ff_spec(s), pl.BlockSpec((2, 3, FF_UNIT), lambda j: (0, 0, j))),
        compiler_params=_params(("parallel",), VMEM_BIG),
    )(zf, conv_w, conv_w, da)


def _rope(v, cs, s1, s2):
    return v * cs + pltpu.roll(v, LANES - C_ROPE // 2, axis=1) * s1 + pltpu.roll(v, C_ROPE // 2, axis=1) * s2


def _rope_t(dv, cs, s1, s2):
    return dv * cs + pltpu.roll(dv * s1, C_ROPE // 2, axis=1) + pltpu.roll(dv * s2, LANES - C_ROPE // 2, axis=1)


def _kpe_mask(shape):
    lane = lax.broadcasted_iota(jnp.int32, shape, 1)
    return (lane >= C_NOPE) & (lane < C_NOPE + C_ROPE)


def _rms(v, g):
    r = lax.rsqrt(jnp.mean(v * v, axis=-1, keepdims=True) + EPS)
    return v * r, r


def _rms_bwd(dn, xh, r, g):
    dxh = dn * g
    return r * (dxh - xh * jnp.mean(dxh * xh, axis=-1, keepdims=True)), jnp.sum(dn * xh, axis=0, keepdims=True)


_ZQ = C_Q_RANK + C_KV_RANK + HEAD_PAD
_HW = C_HEADS * HEAD_PAD


def mla_pre_fwd(z, gq, gkv, wq, wk, wv, cs, s1, s2):
    s = z.shape[0]
    tm = _tile(s, 256, 8)

    def body(z_ref, gq_ref, gkv_ref, wq_ref, wk_ref, wv_ref, cs_ref, s1_ref, s2_ref, q_ref, k_ref, v_ref):
        zv = z_ref[...].astype(F32)
        cst, s1t, s2t = cs_ref[...], s1_ref[...], s2_ref[...]
        qh, _ = _rms(zv[:, :C_Q_RANK], None)
        qn = (qh * gq_ref[...]).astype(BF16)
        q = _dot(qn, wq_ref[...])
        kh, _ = _rms(zv[:, C_Q_RANK:C_Q_RANK + C_KV_RANK], None)
        kvn = (kh * gkv_ref[...]).astype(BF16)
        k = _dot(kvn, wk_ref[...])
        v_ref[...] = _dot(kvn, wv_ref[...]).astype(BF16)
        kpe = _rope(zv[:, C_Q_RANK + C_KV_RANK:], cst, s1t, s2t)
        for h in range(C_HEADS):
            sl = slice(h * HEAD_PAD, (h + 1) * HEAD_PAD)
            q_ref[:, sl] = _rope(q[:, sl], cst, s1t, s2t).astype(BF16)
            k_ref[:, sl] = (k[:, sl] + kpe).astype(BF16)

    full = lambda r, c: pl.BlockSpec((r, c), lambda i: (0, 0))
    hw = _sds((s, _HW), BF16)
    return pl.pallas_call(
        body, name="mla_pre_fwd", out_shape=(hw, hw, hw), grid=(s // tm,),
        in_specs=[_rows(tm, _ZQ), _vec(C_Q_RANK), _vec(C_KV_RANK), full(C_Q_RANK, _HW), full(C_KV_RANK, _HW),
                  full(C_KV_RANK, _HW), _rows(tm, LANES), _rows(tm, LANES), _rows(tm, LANES)],
        out_specs=(_rows(tm, _HW), _rows(tm, _HW), _rows(tm, _HW)),
        compiler_params=_params(("parallel",), VMEM_BIG),
    )(z, gq, gkv, wq, wk, wv, cs, s1, s2)


def mla_pre_bwd(z, gq, gkv, wq, wk, wv, cs, s1, s2, dq, dk, dv):
    s = z.shape[0]
    tm = _tile(s, 256, 8)

    def body(z_ref, gq_ref, gkv_ref, wq_ref, wk_ref, wv_ref, cs_ref, s1_ref, s2_ref, dq_ref, dk_ref, dv_ref,
             dz_ref, dwq_ref, dwk_ref, dwv_ref, dgq_ref, dgkv_ref):
        @pl.when(pl.program_id(0) == 0)
        def _():
            dwq_ref[...] = jnp.zeros_like(dwq_ref)
            dwk_ref[...] = jnp.zeros_like(dwk_ref)
            dwv_ref[...] = jnp.zeros_like(dwv_ref)
            dgq_ref[...] = jnp.zeros_like(dgq_ref)
            dgkv_ref[...] = jnp.zeros_like(dgkv_ref)

        zv = z_ref[...].astype(F32)
        cst, s1t, s2t = cs_ref[...], s1_ref[...], s2_ref[...]
        gqv, gkvv = gq_ref[...], gkv_ref[...]
        qh, rq = _rms(zv[:, :C_Q_RANK], None)
        qn = (qh * gqv).astype(BF16)
        kh, rk = _rms(zv[:, C_Q_RANK:C_Q_RANK + C_KV_RANK], None)
        kvn = (kh * gkvv).astype(BF16)

        dqv = dq_ref[...].astype(F32)
        dqp = jnp.concatenate(
            [_rope_t(dqv[:, h * HEAD_PAD:(h + 1) * HEAD_PAD], cst, s1t, s2t) for h in range(C_HEADS)], axis=1
        ).astype(BF16)
        dwq_ref[...] += _dot(qn, dqp, _TN)
        dqn = _dot(dqp, wq_ref[...], _NT)
        dql, dgq = _rms_bwd(dqn, qh, rq, gqv)
        dgq_ref[...] += dgq

        dkv = dk_ref[...]
        dkb = dkv.astype(BF16)
        dvb = dv_ref[...].astype(BF16)
        dwk_ref[...] += _dot(kvn, dkb, _TN)
        dwv_ref[...] += _dot(kvn, dvb, _TN)
        dkvn = _dot(dkb, wk_ref[...], _NT) + _dot(dvb, wv_ref[...], _NT)
        dkl, dgkv = _rms_bwd(dkvn, kh, rk, gkvv)
        dgkv_ref[...] += dgkv

        dkpe = dkv[:, :HEAD_PAD]
        for h in range(1, C_HEADS):
            dkpe = dkpe + dkv[:, h * HEAD_PAD:(h + 1) * HEAD_PAD]
        dkpe = _rope_t(jnp.where(_kpe_mask(dkpe.shape), dkpe, 0.0), cst, s1t, s2t)
        dz_ref[...] = jnp.concatenate([dql, dkl, dkpe], axis=1).astype(BF16)

    full = lambda r, c: pl.BlockSpec((r, c), lambda i: (0, 0))
    return pl.pallas_call(
        body, name="mla_pre_bwd",
        out_shape=(_sds((s, _ZQ), BF16), _sds((C_Q_RANK, _HW)), _sds((C_KV_RANK, _HW)), _sds((C_KV_RANK, _HW)),
                   _sds((1, C_Q_RANK)), _sds((1, C_KV_RANK))),
        grid=(s // tm,),
        in_specs=[_rows(tm, _ZQ), _vec(C_Q_RANK), _vec(C_KV_RANK), full(C_Q_RANK, _HW), full(C_KV_RANK, _HW),
                  full(C_KV_RANK, _HW), _rows(tm, LANES), _rows(tm, LANES), _rows(tm, LANES),
                  _rows(tm, _HW), _rows(tm, _HW), _rows(tm, _HW)],
        out_specs=(_rows(tm, _ZQ), full(C_Q_RANK, _HW), full(C_KV_RANK, _HW), full(C_KV_RANK, _HW),
                   _vec(C_Q_RANK), _vec(C_KV_RANK)),
        compiler_params=_params(("arbitrary",), VMEM_BIG),
    )(z, gq, gkv, wq, wk, wv, cs, s1, s2, dq, dk, dv)


_ATT_SCALE = (C_NOPE + C_ROPE) ** -0.5
_NEG = -1e30


def _att_probs(q, k, row0):
    sc = _dot(q, k, _NT) * _ATT_SCALE
    qpos = row0 + lax.broadcasted_iota(jnp.int32, sc.shape, 0)
    kpos = lax.broadcasted_iota(jnp.int32, sc.shape, 1)
    sc = jnp.where(kpos <= qpos, sc, _NEG)
    e = jnp.exp(sc - jnp.max(sc, axis=-1, keepdims=True))
    return e / jnp.sum(e, axis=-1, keepdims=True)


def attn_fwd(q, k, v):
    s = q.shape[0]
    tq = _tile(s, 256, 8)

    def body(q_ref, k_ref, v_ref, o_ref):
        p = _att_probs(q_ref[...], k_ref[...], pl.program_id(1) * tq)
        o_ref[...] = _dot(p.astype(BF16), v_ref[...]).astype(BF16)

    qspec = pl.BlockSpec((tq, HEAD_PAD), lambda h, i: (i, h))
    kspec = pl.BlockSpec((s, HEAD_PAD), lambda h, i: (0, h))
    return pl.pallas_call(
        body, name="attn_fwd", out_shape=_sds((s, _HW), BF16), grid=(C_HEADS, s // tq),
        in_specs=[qspec, kspec, kspec], out_specs=qspec,
        compiler_params=_params(("parallel", "parallel"), VMEM_BIG),
    )(q, k, v)


def attn_bwd(q, k, v, o, do_all, do_col0):
    s = q.shape[0]
    tq = _tile(s, 256, 8)

    def body(q_ref, k_ref, v_ref, o_ref, do_ref, dq_ref, dk_ref, dv_ref):
        i = pl.program_id(1)

        @pl.when(i == 0)
        def _():
            dk_ref[...] = jnp.zeros_like(dk_ref)
            dv_ref[...] = jnp.zeros_like(dv_ref)

        qv, kv, vv, dov = q_ref[...], k_ref[...], v_ref[...], do_ref[...]
        p = _att_probs(qv, kv, i * tq)
        dp = _dot(dov, vv, _NT)
        delta = jnp.sum(dov.astype(F32) * o_ref[...].astype(F32), axis=-1, keepdims=True)
        ds = (p * (dp - delta) * _ATT_SCALE).astype(BF16)
        dq_ref[...] = _dot(ds, kv).astype(BF16)
        dk_ref[...] += _dot(ds, qv, _TN)
        dv_ref[...] += _dot(p.astype(BF16), dov, _TN)

    qspec = pl.BlockSpec((tq, HEAD_PAD), lambda h, i: (i, h))
    dospec = pl.BlockSpec((tq, HEAD_PAD), lambda h, i: (i, do_col0 + h))
    kspec = pl.BlockSpec((s, HEAD_PAD), lambda h, i: (0, h))
    return pl.pallas_call(
        body, name="attn_bwd", out_shape=(_sds((s, _HW), BF16), _sds((s, _HW)), _sds((s, _HW))),
        grid=(C_HEADS, s // tq),
        in_specs=[qspec, kspec, kspec, qspec, dospec], out_specs=(qspec, kspec, kspec),
        compiler_params=_params(("parallel", "arbitrary"), VMEM_BIG),
    )(q, k, v, o, do_all)


_DW = D_GROUPS * LANES


def _tril_bf16(w):
    r = lax.broadcasted_iota(jnp.int32, w.shape, 0)
    c = lax.broadcasted_iota(jnp.int32, w.shape, 1)
    return jnp.where(c <= r, w, 0.0).astype(BF16)


def _sgu_forward(zu, zv, lg, lb, ws_ref, bs):
    u = _gelu(zu)
    v = _gelu(zv)
    mu = jnp.mean(v, axis=-1, keepdims=True)
    vc = v - mu
    rstd = lax.rsqrt(jnp.mean(vc * vc, axis=-1, keepdims=True) + EPS)
    xh = vc * rstd
    vln = (xh * lg + lb).astype(BF16)
    mixed = []
    for g in range(D_GROUPS):
        wg = _tril_bf16(ws_ref[g])
        mixed.append(_dot(wg, vln[:, g * LANES:(g + 1) * LANES]) + bs[:, g:g + 1])
    return u, xh, rstd, vln, jnp.concatenate(mixed, axis=1)


def sgu_fwd(z, lg, lb, ws, bs_t):
    s = z.shape[0]
    nchunk = s // D_CHUNK

    def body(zu_ref, zv_ref, lg_ref, lb_ref, ws_ref, bs_ref, o_ref):
        u, _, _, _, mixed = _sgu_forward(zu_ref[...].astype(F32), zv_ref[...].astype(F32), lg_ref[...], lb_ref[...],
                                         ws_ref, bs_ref[...])
        o_ref[...] = (u * mixed).astype(BF16)

    return pl.pallas_call(
        body, name="sgu_fwd", out_shape=_sds((s, _DW), BF16), grid=(nchunk,),
        in_specs=[pl.BlockSpec((D_CHUNK, _DW), lambda n: (n, 1)), pl.BlockSpec((D_CHUNK, _DW), lambda n: (n, 2)),
                  _vec(_DW), _vec(_DW), pl.BlockSpec((D_GROUPS, D_CHUNK, D_CHUNK), lambda n: (0, 0, 0)),
                  pl.BlockSpec((D_CHUNK, LANES), lambda n: (0, 0))],
        out_specs=pl.BlockSpec((D_CHUNK, _DW), lambda n: (n, 0)),
        compiler_params=_params(("parallel",)),
    )(z, z, lg, lb, ws, bs_t)


def sgu_bwd(z, lg, lb, ws, bs_t, dycat, dy_col):
    s = z.shape[0]
    nchunk = s // D_CHUNK

    def body(zu_ref, zv_ref, lg_ref, lb_ref, ws_ref, bs_ref, dy_ref, dzu_ref, dzv_ref, dws_ref, dbs_ref, dlg_ref,
             dlb_ref):
        @pl.when(pl.program_id(0) == 0)
        def _():
            dws_ref[...] = jnp.zeros_like(dws_ref)
            dbs_ref[...] = jnp.zeros_like(dbs_ref)
            dlg_ref[...] = jnp.zeros_like(dlg_ref)
            dlb_ref[...] = jnp.zeros_like(dlb_ref)

        zu, zv, lg = zu_ref[...].astype(F32), zv_ref[...].astype(F32), lg_ref[...]
        u, xh, rstd, vln, mixed = _sgu_forward(zu, zv, lg, lb_ref[...], ws_ref, bs_ref[...])
        dy = dy_ref[...].astype(F32)
        dzu_ref[...] = (dy * mixed * _gelu_grad(zu)).astype(BF16)
        dmix = dy * u
        lane = lax.broadcasted_iota(jnp.int32, (D_CHUNK, LANES), 1)
        row = lax.broadcasted_iota(jnp.int32, (D_CHUNK, D_CHUNK), 0)
        colm = lax.broadcasted_iota(jnp.int32, (D_CHUNK, D_CHUNK), 1)
        dvln = []
        dbs = jnp.zeros((D_CHUNK, LANES), F32)
        for g in range(D_GROUPS):
            sl = slice(g * LANES, (g + 1) * LANES)
            dmg = dmix[:, sl]
            dbs = dbs + jnp.where(lane == g, jnp.sum(dmg, axis=-1, keepdims=True), 0.0)
            dmb = dmg.astype(BF16)
            dws_ref[g] += jnp.where(colm <= row, _dot(dmb, vln[:, sl], _NT), 0.0)
            dvln.append(_dot(_tril_bf16(ws_ref[g]), dmb, _TN))
        dbs_ref[...] += dbs
        dvln = jnp.concatenate(dvln, axis=1)
        dlg_ref[...] += jnp.sum(dvln * xh, axis=0, keepdims=True)
        dlb_ref[...] += jnp.sum(dvln, axis=0, keepdims=True)
        dxh = dvln * lg
        dvv = rstd * (dxh - jnp.mean(dxh, axis=-1, keepdims=True) - xh * jnp.mean(dxh * xh, axis=-1, keepdims=True))
        dzv_ref[...] = (dvv * _gelu_grad(zv)).astype(BF16)

    wsspec = pl.BlockSpec((D_GROUPS, D_CHUNK, D_CHUNK), lambda n: (0, 0, 0))
    chunk = lambda cidx: pl.BlockSpec((D_CHUNK, _DW), lambda n: (n, cidx))
    return pl.pallas_call(
        body, name="sgu_bwd",
        out_shape=(_sds((s, _DW), BF16), _sds((s, _DW), BF16), _sds((D_GROUPS, D_CHUNK, D_CHUNK)),
                   _sds((D_CHUNK, LANES)), _sds((1, _DW)), _sds((1, _DW))),
        grid=(nchunk,),
        in_specs=[chunk(1), chunk(2), _vec(_DW), _vec(_DW), wsspec, pl.BlockSpec((D_CHUNK, LANES), lambda n: (0, 0)),
                  chunk(dy_col)],
        out_specs=(chunk(0), chunk(0), wsspec, pl.BlockSpec((D_CHUNK, LANES), lambda n: (0, 0)), _vec(_DW), _vec(_DW)),
        compiler_params=_params(("arbitrary",)),
    )(z, z, lg, lb, ws, bs_t, dycat)


def ada_mod(c_all, ada_w, ada_b):
    nl, d, n = ada_w.shape
    nb = c_all.shape[0]
    tn = _tile(n, 512)

    def body(c_ref, w_ref, b_ref, o_ref):
        cv = c_ref[...]
        ca = (cv * _sigmoid(cv)).astype(BF16)
        o_ref[...] = _dot(ca, w_ref[...].astype(BF16)) + b_ref[...]

    return pl.pallas_call(
        body, name="ada_mod", out_shape=_sds((nl, nb, n)), grid=(nl, n // tn),
        in_specs=[pl.BlockSpec((nb, d), lambda l, j: (0, 0)), pl.BlockSpec((None, d, tn), lambda l, j: (l, 0, j)),
                  pl.BlockSpec((None, 1, tn), lambda l, j: (l, 0, j))],
        out_specs=pl.BlockSpec((None, nb, tn), lambda l, j: (l, 0, j)),
        compiler_params=_params(("parallel", "parallel")),
    )(c_all, ada_w, ada_b.reshape(nl, 1, n))


def ada_grad(c_all_t, dmod):
    d, nb = c_all_t.shape
    nl, _, n = dmod.shape
    tn = _tile(n, 512)
    tr = _tile(d, 256, 8)

    def body(c_ref, dm_ref, o_ref):
        cv = c_ref[...]
        ca = cv * _sigmoid(cv)
        dm = dm_ref[...]
        acc = ca[:, 0:1] * dm[0:1, :]
        for b in range(1, nb):
            acc = acc + ca[:, b:b + 1] * dm[b:b + 1, :]
        o_ref[...] = acc

    return pl.pallas_call(
        body, name="ada_grad", out_shape=_sds((nl, d, n)), grid=(nl, n // tn, d // tr),
        in_specs=[pl.BlockSpec((tr, nb), lambda l, j, r: (r, 0)), pl.BlockSpec((None, nb, tn), lambda l, j, r: (l, 0, j))],
        out_specs=pl.BlockSpec((None, tr, tn), lambda l, j, r: (l, r, j)),
        compiler_params=_params(("parallel", "parallel", "parallel")),
    )(c_all_t, dmod)


_ADAM_BLOCK = 256 * 1024


def _adam_rows(rows, cols):
    if rows * cols <= _ADAM_BLOCK or rows % 8:
        return rows
    return _tile(rows, max(8, _ADAM_BLOCK // cols), 8)


def _adam_update(w, gv, m, v):
    bc1 = 1.0 - ADAM_B1 ** ADAM_STEP
    bc2 = 1.0 - ADAM_B2 ** ADAM_STEP
    nm = ADAM_B1 * m + (1.0 - ADAM_B1) * gv
    nv = ADAM_B2 * v + (1.0 - ADAM_B2) * (gv * gv)
    return -ADAM_LR * ((nm / bc1) / (jnp.sqrt(nv / bc2) + ADAM_EPS) + ADAM_WD * w), nm, nv


def adamw(w, g, m, v, name):
    shape = w.shape
    cols = shape[-1]
    rows = w.size // cols
    tr = _adam_rows(rows, cols)

    def body(w_ref, g_ref, m_ref, v_ref, d_ref, nm_ref, nv_ref):
        d_ref[...], nm_ref[...], nv_ref[...] = _adam_update(w_ref[...], g_ref[...], m_ref[...], v_ref[...])

    spec = pl.BlockSpec((tr, cols), lambda i: (i, 0))
    out = _sds((rows, cols))
    r2 = lambda t: t.reshape(rows, cols)
    d, nm, nv = pl.pallas_call(
        body, name=name, out_shape=(out, out, out), grid=(rows // tr,),
        in_specs=[spec] * 4, out_specs=(spec,) * 3, compiler_params=_params(("parallel",)),
    )(r2(w), r2(g), r2(m), r2(v))
    return g.reshape(shape), d.reshape(shape), nm.reshape(shape), nv.reshape(shape)


def adamw_layers(w, g0, g1, m, v, name):
    _, rows, cols = w.shape
    tr = _adam_rows(rows, cols)

    def body(w_ref, g0_ref, g1_ref, m_ref, v_ref, g_ref, d_ref, nm_ref, nv_ref):
        gv = jnp.where(pl.program_id(0) == 0, g0_ref[...], g1_ref[...])
        g_ref[...] = gv
        d_ref[...], nm_ref[...], nv_ref[...] = _adam_update(w_ref[...], gv, m_ref[...], v_ref[...])

    spec = pl.BlockSpec((None, tr, cols), lambda l, i: (l, i, 0))
    gspec = pl.BlockSpec((tr, cols), lambda l, i: (i, 0))
    out = _sds((2, rows, cols))
    return pl.pallas_call(
        body, name=name, out_shape=(out, out, out, out), grid=(2, rows // tr),
        in_specs=[spec, gspec, gspec, spec, spec], out_specs=(spec,) * 4, compiler_params=_params(("parallel", "parallel")),
    )(w, g0, g1, m, v)


def sum8(gathered):
    _, r, _ = gathered.shape
    tr = _tile(r, 512, 8)

    def body(g_ref, o_ref):
        acc = g_ref[0]
        for dev in range(1, N_DEV):
            acc = acc + g_ref[dev]
        o_ref[...] = acc

    return pl.pallas_call(
        body, name="sum8", out_shape=_sds((r, LANES)), grid=(r // tr,),
        in_specs=[pl.BlockSpec((N_DEV, tr, LANES), lambda i: (0, i, 0))], out_specs=pl.BlockSpec((tr, LANES), lambda i: (i, 0)),
        compiler_params=_params(("parallel",)),
    )(gathered)


_SUM_BLOCK = 512 * 1024


def _sum_rows(rh, cols):
    return rh if rh * cols <= _SUM_BLOCK else _tile(rh, max(16, _SUM_BLOCK // cols), 16)


def pair_sum(g, recv, core, name):
    _, r, cols = g.shape
    rh = r // 2
    tr = _sum_rows(rh, cols)
    per = rh // tr

    def body(c_ref, a_ref, b_ref, o_ref):
        del c_ref
        o_ref[...] = (a_ref[...].astype(F32) + b_ref[...].astype(F32)).astype(BF16)

    grid_spec = pltpu.PrefetchScalarGridSpec(
        num_scalar_prefetch=1, grid=(N_CHIPS, per),
        in_specs=[pl.BlockSpec((None, tr, cols), lambda k, i, c: (k, c[0] * per + i, 0)),
                  pl.BlockSpec((None, tr, cols), lambda k, i, c: (k, i, 0))],
        out_specs=pl.BlockSpec((None, tr, cols), lambda k, i, c: (k, i, 0)))
    return pl.pallas_call(
        body, name=name, out_shape=_sds((N_CHIPS, rh, cols), BF16), grid_spec=grid_spec,
        compiler_params=_params(("parallel", "parallel")),
    )(core.reshape(1).astype(jnp.int32), g, recv)


def chip_sum(pair, recv, chip, core, name):
    _, rh, cols = pair.shape
    tr = _sum_rows(rh, cols)

    def body(p_ref, own_ref, r_ref, o_ref):
        del p_ref
        acc = own_ref[...].astype(F32)
        for j in range(N_CHIPS - 1):
            acc = acc + r_ref[j].astype(F32)
        o_ref[...] = acc

    grid_spec = pltpu.PrefetchScalarGridSpec(
        num_scalar_prefetch=1, grid=(rh // tr,),
        in_specs=[pl.BlockSpec((None, tr, cols), lambda i, p: (p[0], i, 0)),
                  pl.BlockSpec((N_CHIPS - 1, tr, cols), lambda i, p: (0, i, 0))],
        out_specs=pl.BlockSpec((None, tr, cols), lambda i, p: (p[1], i, 0)))
    return pl.pallas_call(
        body, name=name, out_shape=_sds((2, rh, cols)), grid_spec=grid_spec,
        compiler_params=_params(("parallel",)),
    )(jnp.stack([chip, core]).astype(jnp.int32), pair, recv)


def _place():
    return lax.axis_index("x"), lax.axis_index("y"), lax.axis_index("c")


def _other_chips(x, y):
    return [(x, 1 - y), (1 - x, y), (1 - x, 1 - y)]


_HBM = pl.BlockSpec(memory_space=pltpu.HBM)


def all_gather8(v, name):
    m, n = v.shape

    def body(x_ref, out_ref, send_sems, recv_sems, local_sem):
        x, y, c = _place()
        me, sibling = (x, y, c), (x, y, 1 - c)
        chips = _other_chips(x, y)

        def rows(px, py, pc):
            return out_ref.at[pl.ds((4 * px + 2 * py + pc) * m, m), :]

        def copy(k, block, to, src=None):
            return pltpu.make_async_remote_copy(
                src_ref=rows(*block) if src is None else src, dst_ref=rows(*block),
                send_sem=send_sems.at[k], recv_sem=recv_sems.at[k], device_id=to, device_id_type=MESH)

        mine = pltpu.make_async_copy(x_ref, rows(*me), local_sem)
        mine.start()
        first = [copy(0, me, sibling, src=x_ref)]
        first += [copy(1 + j, me, (*chip, c), src=x_ref) for j, chip in enumerate(chips)]
        for cp in first:
            cp.start()
        passed = [copy(4 + j, (*chip, c), sibling) for j, chip in enumerate(chips)]
        for j, chip in enumerate(chips):
            copy(1 + j, (*chip, c), me).wait_recv()
            passed[j].start()
        copy(0, sibling, me).wait_recv()
        for j, chip in enumerate(chips):
            copy(4 + j, (*chip, 1 - c), me).wait_recv()
        for cp in first + passed:
            cp.wait_send()
        mine.wait()

    return pl.pallas_call(
        body, name=name, out_shape=_sds((N_DEV * m, n), v.dtype),
        in_specs=[pl.BlockSpec(memory_space=pltpu.VMEM)], out_specs=pl.BlockSpec(memory_space=pltpu.VMEM),
        scratch_shapes=[pltpu.SemaphoreType.DMA((7,)), pltpu.SemaphoreType.DMA((7,)), pltpu.SemaphoreType.DMA],
        compiler_params=_params(None, VMEM_BIG),
    )(v)


def _comm_call(body, name, ins, out_shapes, nsem, aliases=None):
    return pl.pallas_call(
        body, name=name, out_shape=tuple(out_shapes), in_specs=[_HBM] * len(ins), out_specs=tuple([_HBM] * len(out_shapes)),
        scratch_shapes=[pltpu.SemaphoreType.DMA((nsem,)), pltpu.SemaphoreType.DMA((nsem,))],
        input_output_aliases=aliases or {},
    )(*ins)


def _remote(src, dst, send_sems, recv_sems, k, to):
    return pltpu.make_async_remote_copy(src_ref=src, dst_ref=dst, send_sem=send_sems.at[k], recv_sem=recv_sems.at[k],
                                        device_id=to, device_id_type=MESH)


def _half(core, rh):
    return pl.ds(pl.multiple_of(core * rh, 16), rh)


def gather_chips(shards, name):
    n = len(shards)

    def body(*refs):
        ins, outs, (send_sems, recv_sems) = refs[:n], refs[n:2 * n], refs[2 * n:]
        x, y, c = _place()
        me = 2 * x + y
        sibling = (x, y, 1 - c)
        chips = _other_chips(x, y)
        first, passed = [], []
        for i in range(n):
            mine = _half(c, ins[i].shape[0] // 2)
            for j, chip in enumerate(chips):
                cp = _remote(ins[i].at[mine], outs[i].at[me, mine], send_sems, recv_sems, 6 * i + j, (*chip, c))
                cp.start()
                first.append(cp)
        for i in range(n):
            mine = _half(c, ins[i].shape[0] // 2)
            for j, (px, py) in enumerate(chips):
                landed = outs[i].at[2 * px + py, mine]
                _remote(landed, landed, send_sems, recv_sems, 6 * i + j, (px, py, c)).wait_recv()
                fw = _remote(landed, landed, send_sems, recv_sems, 6 * i + 3 + j, sibling)
                fw.start()
                passed.append(fw)
        for i in range(n):
            theirs = _half(1 - c, ins[i].shape[0] // 2)
            for j, (px, py) in enumerate(chips):
                landed = outs[i].at[2 * px + py, theirs]
                _remote(landed, landed, send_sems, recv_sems, 6 * i + 3 + j, sibling).wait_recv()
        for cp in first + passed:
            cp.wait_send()

    return _comm_call(body, name, shards, [_sds((N_CHIPS,) + s.shape, s.dtype) for s in shards], 6 * n)


def swap_halves(gs, name):
    n = len(gs)

    def body(*refs):
        ins, outs, (send_sems, recv_sems) = refs[:n], refs[n:2 * n], refs[2 * n:]
        x, y, c = _place()
        copies = []
        for i in range(n):
            theirs = _half(1 - c, ins[i].shape[1] // 2)
            cp = _remote(ins[i].at[:, theirs], outs[i], send_sems, recv_sems, i, (x, y, 1 - c))
            cp.start()
            copies.append(cp)
        for cp in copies:
            cp.wait()

    return _comm_call(body, name, gs, [_sds((g.shape[0], g.shape[1] // 2, g.shape[2]), g.dtype) for g in gs], n)


def chip_exchange(ts, name):
    n = len(ts)

    def body(*refs):
        ins, outs, (send_sems, recv_sems) = refs[:n], refs[n:2 * n], refs[2 * n:]
        x, y, c = _place()
        copies = []
        for i in range(n):
            for j, (px, py) in enumerate(_other_chips(x, y)):
                cp = _remote(ins[i].at[2 * px + py], outs[i].at[j], send_sems, recv_sems, 3 * i + j, (px, py, c))
                cp.start()
                copies.append(cp)
        for cp in copies:
            cp.wait()

    return _comm_call(body, name, ts, [_sds((N_CHIPS - 1,) + t.shape[1:], t.dtype) for t in ts], 3 * n)


def join_halves(bufs, name):
    n = len(bufs)

    def body(*refs):
        ins, outs, (send_sems, recv_sems) = refs[:n], refs[n:2 * n], refs[2 * n:]
        x, y, c = _place()
        copies = []
        for i in range(n):
            cp = _remote(ins[i].at[c], outs[i].at[c], send_sems, recv_sems, i, (x, y, 1 - c))
            cp.start()
            copies.append(cp)
        for i in range(n):
            theirs = outs[i].at[1 - c]
            _remote(theirs, theirs, send_sems, recv_sems, i, (x, y, 1 - c)).wait_recv()
        for cp in copies:
            cp.wait_send()

    return _comm_call(body, name, bufs, [_sds(b.shape, b.dtype) for b in bufs], n, {i: i for i in range(n)})


def forward_halves(lands, name):
    n = len(lands)

    def body(*refs):
        ins, outs, (send_sems, recv_sems) = refs[:n], refs[n:2 * n], refs[2 * n:]
        x, y, c = _place()
        sibling = (x, y, 1 - c)
        chips = _other_chips(x, y)
        copies = []
        for i in range(n):
            mine = _half(c, ins[i].shape[1] // 2)
            for j, (px, py) in enumerate(chips):
                cp = _remote(ins[i].at[2 * px + py, mine], outs[i].at[2 * px + py, mine], send_sems, recv_sems, 3 * i + j, sibling)
                cp.start()
                copies.append(cp)
        for i in range(n):
            theirs = _half(1 - c, ins[i].shape[1] // 2)
            for j, (px, py) in enumerate(chips):
                landed = outs[i].at[2 * px + py, theirs]
                _remote(landed, landed, send_sems, recv_sems, 3 * i + j, sibling).wait_recv()
        for cp in copies:
            cp.wait_send()

    return _comm_call(body, name, lands, [_sds(b.shape, b.dtype) for b in lands], 3 * n, {i: i for i in range(n)})


_SEM = pl.BlockSpec(memory_space=pltpu.SEMAPHORE)
_EFFECT = pltpu.SideEffectType.DATAFLOW_SIDE_EFFECTING


def _gather_copies(srcs, lands, send_sems, recv_sems):
    x, y, c = _place()
    copies = []
    for i in range(len(srcs)):
        mine = _half(c, srcs[i].shape[0] // 2)
        for j, chip in enumerate(_other_chips(x, y)):
            copies.append(_remote(srcs[i].at[mine], lands[i].at[2 * x + y, mine], send_sems, recv_sems, 3 * i + j, (*chip, c)))
    return copies


def _exchange_copies(srcs, lands, send_sems, recv_sems):
    x, y, c = _place()
    copies = []
    for i in range(len(srcs)):
        for j, (px, py) in enumerate(_other_chips(x, y)):
            copies.append(_remote(srcs[i].at[2 * px + py], lands[i].at[j], send_sems, recv_sems, 3 * i + j, (px, py, c)))
    return copies


def split_start(name, copies_fn, srcs, land_shapes):
    n, m = len(srcs), len(land_shapes)
    ncopies = 3 * n

    def body(*refs):
        src_refs, land_refs = refs[:n], refs[n:n + m]
        send_sems, recv_sems = refs[n + m], refs[n + m + 1]
        token = refs[-1]
        for cp in copies_fn(src_refs, land_refs, send_sems, recv_sems):
            cp.start()
        token[...] = jnp.zeros_like(token)

    hbm = lambda s: pltpu.HBM(tuple(s.shape), s.dtype)
    outs = pl.pallas_call(
        body, name=name,
        out_shape=(pltpu.SemaphoreType.DMA((ncopies,)), pltpu.SemaphoreType.DMA((ncopies,)), *[hbm(s) for s in srcs],
                   *[hbm(s) for s in land_shapes], _sds((8, LANES))),
        in_specs=[_HBM] * (n + m),
        out_specs=(_SEM, _SEM, *([_HBM] * (n + m)), pl.BlockSpec(memory_space=pltpu.VMEM)),
        input_output_aliases={i: 2 + i for i in range(n + m)},
        compiler_params=pltpu.CompilerParams(has_side_effects=_EFFECT),
    )(*[pltpu.with_memory_space_constraint(s, pltpu.HBM) for s in srcs],
      *[pltpu.with_memory_space_constraint(lax.empty(tuple(s.shape), s.dtype), pltpu.HBM) for s in land_shapes])
    handle = (outs[0], outs[1], list(outs[2:2 + n]), list(outs[2 + n:2 + n + m]))
    return handle, outs[-1][0, 0]


def split_wait(name, copies_fn, handle, after):
    send_sems, recv_sems, srcs, lands = handle
    n, m = len(srcs), len(lands)

    def body(*refs):
        src_refs, land_refs = refs[:n], refs[n:n + m]
        for cp in copies_fn(src_refs, land_refs, refs[n + m], refs[n + m + 1]):
            cp.wait_send()
            cp.wait_recv()

    hbm = lambda s: pltpu.HBM(tuple(s.shape), s.dtype)
    outs = pl.pallas_call(
        body, name=name, out_shape=tuple(hbm(s) for s in srcs + lands),
        in_specs=[_HBM] * (n + m) + [_SEM, _SEM, pl.BlockSpec(memory_space=pl.ANY)], out_specs=tuple([_HBM] * (n + m)),
        input_output_aliases={i: i for i in range(n + m)},
        compiler_params=pltpu.CompilerParams(has_side_effects=_EFFECT),
    )(*srcs, *lands, send_sems, recv_sems, after)
    return list(outs[:n]), list(outs[n:])


_CD_PAD = C_Q_RANK + C_KV_RANK + HEAD_PAD + 2 * _DW


def chip_major(w, groups=N_CHIPS):
    r, c = w.shape
    return w.reshape(r, groups, c // groups).transpose(1, 0, 2)


def from_chip_major(w):
    g, r, c = w.shape
    return w.transpose(1, 0, 2).reshape(r, g * c)


def _cd_in_pad(w):
    a = C_Q_RANK + C_KV_RANK
    z = lambda n: jnp.zeros((w.shape[0], n), w.dtype)
    return jnp.concatenate([w[:, :a], z(C_NOPE), w[:, a:a + C_ROPE], z(HEAD_PAD - C_NOPE - C_ROPE), w[:, a + C_ROPE:]], axis=1)


def _cd_in_unpad(w):
    a = C_Q_RANK + C_KV_RANK
    return jnp.concatenate([w[:, :a], w[:, a + C_NOPE:a + C_NOPE + C_ROPE], w[:, a + HEAD_PAD:]], axis=1)


def _pad_heads(w, width):
    r = w.shape[0]
    w = w.reshape(r, C_HEADS, width)
    return jnp.pad(w, ((0, 0), (0, 0), (0, HEAD_PAD - width))).reshape(r, _HW)


def _unpad_heads(w, width):
    r = w.shape[0]
    return w.reshape(r, C_HEADS, HEAD_PAD)[:, :, :width].reshape(r, C_HEADS * width)


_MATMUL_WEIGHTS = ("ab_w_in", "ab_w_out", "cd_w_in", "c_w_uq", "c_w_ukv", "cd_w_out", "ffn_w_up", "ffn_w_down")
_LAYER_STACKED = ("norm1_g", "norm2_g", "ffn_w_up", "ffn_conv_w", "ffn_w_down")
_ROW_VECTORS = ("b_scale", "c_q_norm_g", "c_kv_norm_g", "d_ln_g", "d_ln_b")


def full_to_local(p):
    q = {}
    for k, v in p.items():
        if k == "final_norm_g":
            v = v.reshape(1, -1)
        elif k not in _LAYER_STACKED and k not in _ROW_VECTORS:
            v = v[0]
        q[k] = v.astype(BF16) if k in _MATMUL_WEIGHTS else v
    return q


def local_to_full(g):
    q = {}
    for k, v in g.items():
        if k == "final_norm_g":
            q[k] = v.reshape(-1)
        elif k not in _LAYER_STACKED and k not in _ROW_VECTORS:
            q[k] = v[None]
        else:
            q[k] = v
    return q


def prepare_weights(p):
    q = dict(p)
    q["cd_w_in"] = _cd_in_pad(p["cd_w_in"])
    q["c_w_uq"] = _pad_heads(p["c_w_uq"], C_NOPE + C_ROPE)
    ukv = p["c_w_ukv"].reshape(C_KV_RANK, C_HEADS, C_NOPE + C_V)
    q["c_w_uk"] = _pad_heads(ukv[:, :, :C_NOPE].reshape(C_KV_RANK, -1), C_NOPE)
    q["c_w_uv"] = _pad_heads(ukv[:, :, C_NOPE:].reshape(C_KV_RANK, -1), C_V)
    wo = p["cd_w_out"]
    att_rows = jnp.pad(wo[:C_HEADS * C_V].reshape(C_HEADS, C_V, D_MODEL), ((0, 0), (0, HEAD_PAD - C_V), (0, 0)))
    q["cd_w_out"] = jnp.concatenate([att_rows.reshape(_HW, D_MODEL), wo[C_HEADS * C_V:]], axis=0)
    return q


def unprepare_grads(g):
    q = dict(g)
    q["cd_w_in"] = _cd_in_unpad(g["cd_w_in"])
    q["c_w_uq"] = _unpad_heads(g["c_w_uq"], C_NOPE + C_ROPE)
    uk = g.pop("c_w_uk").reshape(C_KV_RANK, C_HEADS, HEAD_PAD)[:, :, :C_NOPE]
    uv = g.pop("c_w_uv").reshape(C_KV_RANK, C_HEADS, HEAD_PAD)[:, :, :C_V]
    q.pop("c_w_uk", None)
    q.pop("c_w_uv", None)
    q["c_w_ukv"] = jnp.concatenate([uk, uv], axis=-1).reshape(C_KV_RANK, C_HEADS * (C_NOPE + C_V))
    wo = g["cd_w_out"]
    att = wo[:_HW].reshape(C_HEADS, HEAD_PAD, D_MODEL)[:, :C_V].reshape(C_HEADS * C_V, D_MODEL)
    q["cd_w_out"] = jnp.concatenate([att, wo[_HW:]], axis=0)
    return q


def rope_tables(positions):
    half = C_ROPE // 2
    inv_freq = ROPE_THETA ** (-jnp.arange(half, dtype=F32) / half)
    ang = positions.astype(F32)[:, None] * inv_freq
    cos, sin = jnp.cos(ang), jnp.sin(ang)
    s = positions.shape[0]
    z = lambda n: jnp.zeros((s, n), F32)
    cs = jnp.concatenate([jnp.ones((s, C_NOPE), F32), cos, cos, z(HEAD_PAD - C_NOPE - C_ROPE)], axis=1)
    s1 = jnp.concatenate([z(C_NOPE), -sin, z(HEAD_PAD - C_NOPE - half)], axis=1)
    s2 = jnp.concatenate([z(C_NOPE + half), sin, z(HEAD_PAD - C_NOPE - C_ROPE)], axis=1)
    return cs, s1, s2


def _mods(mod_l):
    return [mod_l[:, i * D_MODEL:(i + 1) * D_MODEL] for i in range(N_MOD)]


def _ffn_fwd(x1, w, l, sc2, sh2, g2):
    n2 = w["norm2_g"][l:l + 1]
    h2 = modnorm_fwd(x1, n2, sc2, sh2, f"modnorm2_fwd{l}")
    up_cols = 2 * D_FF // N_CHIPS
    zf = matmul(h2, w["ffn_w_up"][l], "nn", BF16, f"ffn_up{l}", gb=N_CHIPS, go=2, tn=up_cols)
    a = ffn_act_fwd(zf, w["ffn_conv_w"][l], f"ffn_act_fwd{l}")
    f = matmul(a, w["ffn_w_down"][l], "nn", F32, f"ffn_down{l}", tk=D_FF)
    x2 = resid_fwd(x1, f, g2, f"resid2_fwd{l}")
    return x2, (h2, zf, a, f)


def _ffn_bwd(dres, x1, saved, w, l, sc2, g2):
    h2, zf, a, f = saved
    n2 = w["norm2_g"][l:l + 1]
    df, dg2 = gate_bwd(dres, f, g2, f"gate2_bwd{l}")
    up_cols = 2 * D_FF // N_CHIPS
    da = matmul(df, w["ffn_w_down"][l], "nt", BF16, f"ffn_down_dx{l}")
    d_down = matmul(a, df, "tn", BF16, f"ffn_down_dw{l}", tm=D_FF // 2)
    dzf, d_conv = ffn_act_bwd(zf, w["ffn_conv_w"][l], da, f"ffn_act_bwd{l}")
    dh2 = matmul(dzf, w["ffn_w_up"][l], "nt", F32, f"ffn_up_dx{l}", ga=2, gb=N_CHIPS, tk=up_cols)
    d_up = matmul(h2, dzf, "tn", BF16, f"ffn_up_dw{l}", gb=2, go=N_CHIPS, tn=up_cols)
    dres, dsh2, dsc2, dn2 = norm_bwd(x1, dh2, n2, sc2, dres, f"norm2_bwd{l}")
    d_conv = d_conv.transpose(1, 0, 2).reshape(3, 2 * D_FF)
    return dres, dict(ffn_w_down=d_down, ffn_conv_w=d_conv, ffn_w_up=d_up, norm2_g=dn2), (dsh2, dsc2, dg2)


def mixer0_fwd(x0, sh1, sc1, g1, w):
    h = modnorm_fwd(x0, w["norm1_g"][0:1], sc1, sh1, "modnorm1_fwd0")
    z = matmul(h, w["ab_w_in"], "nn", BF16, "ab_in", gb=N_CHIPS)
    ya = gconv_fwd(z, w["a_conv_w"])
    yb = pool_fwd(z, w["b_mix_w"], w["b_scale"])
    ycat = jnp.concatenate([ya] + yb, axis=1)
    y = matmul(ycat, w["ab_w_out"], "nn", F32, "ab_out")
    x1 = resid_fwd(x0, y, g1, "resid1_fwd0")
    return x1, (x0, h, z, ycat, y, sc1, g1)


def mixer0_bwd(dres, saved, w):
    x0, h, z, ycat, y, sc1, g1 = saved
    grads = {}
    dy, dg1 = gate_bwd(dres, y, g1, "gate1_bwd0")
    dycat = matmul(dy, w["ab_w_out"], "nt", BF16, "ab_out_dx")
    grads["ab_w_out"] = matmul(ycat, dy, "tn", BF16, "ab_out_dw")
    db, dc, da, d_conv = gconv_bwd(z, w["a_conv_w"], dycat)
    pb = pool_bwd(z, w["b_mix_w"], w["b_scale"], dycat)
    dz = jnp.concatenate([db, dc, da] + [t[0] for t in pb], axis=1)
    dh = matmul(dz, w["ab_w_in"], "nt", F32, "ab_in_dx", gb=N_CHIPS)
    grads["ab_w_in"] = matmul(h, dz, "tn", BF16, "ab_in_dw", go=N_CHIPS)
    dres, dsh1, dsc1, dn1 = norm_bwd(x0, dh, w["norm1_g"][0:1], sc1, dres, "norm1_bwd0")
    grads.update(a_conv_w=d_conv, b_mix_w=jnp.stack([t[1] for t in pb]),
                 b_scale=jnp.concatenate([t[2] for t in pb], axis=1), norm1_g=dn1)
    return dres, grads, (dsh1, dsc1, dg1)


def mixer1_fwd(x0, sh1, sc1, g1, ropes, w):
    cs, s1, s2 = ropes
    h = modnorm_fwd(x0, w["norm1_g"][1:2], sc1, sh1, "modnorm1_fwd1")
    z = matmul(h, w["cd_w_in"], "nn", BF16, "cd_in")
    bs_t = jnp.pad(w["d_b_s"].T, ((0, 0), (0, LANES - D_GROUPS)))
    qh, kh, vh = mla_pre_fwd(z, w["c_q_norm_g"], w["c_kv_norm_g"], w["c_w_uq"], w["c_w_uk"], w["c_w_uv"], cs, s1, s2)
    oh = attn_fwd(qh, kh, vh)
    yd = sgu_fwd(z, w["d_ln_g"], w["d_ln_b"], w["d_w_s"], bs_t)
    ycat = jnp.concatenate([oh, yd], axis=1)
    y = matmul(ycat, w["cd_w_out"], "nn", F32, "cd_out")
    x1 = resid_fwd(x0, y, g1, "resid1_fwd1")
    return x1, (x0, h, z, bs_t, qh, kh, vh, oh, ycat, y, sc1, g1)


def mixer1_bwd(dres, saved, ropes, w):
    cs, s1, s2 = ropes
    x0, h, z, bs_t, qh, kh, vh, oh, ycat, y, sc1, g1 = saved
    grads = {}
    dy, dg1 = gate_bwd(dres, y, g1, "gate1_bwd1")
    dycat = matmul(dy, w["cd_w_out"], "nt", BF16, "cd_out_dx")
    grads["cd_w_out"] = matmul(ycat, dy, "tn", F32, "cd_out_dw")
    dqh, dkh, dvh = attn_bwd(qh, kh, vh, oh, dycat, 0)
    dzq, d_uq, d_uk, d_uv, d_gq, d_gkv = mla_pre_bwd(
        z, w["c_q_norm_g"], w["c_kv_norm_g"], w["c_w_uq"], w["c_w_uk"], w["c_w_uv"], cs, s1, s2, dqh, dkh, dvh)
    dzu, dzv, d_ws, d_bs, d_lg, d_lb = sgu_bwd(z, w["d_ln_g"], w["d_ln_b"], w["d_w_s"], bs_t, dycat, _HW // _DW)
    dz = jnp.concatenate([dzq, dzu, dzv], axis=1)
    dh = matmul(dz, w["cd_w_in"], "nt", F32, "cd_in_dx")
    grads["cd_w_in"] = matmul(h, dz, "tn", F32, "cd_in_dw")
    dres, dsh1, dsc1, dn1 = norm_bwd(x0, dh, w["norm1_g"][1:2], sc1, dres, "norm1_bwd1")
    grads.update(c_w_uq=d_uq, c_w_uk=d_uk, c_w_uv=d_uv, c_q_norm_g=d_gq, c_kv_norm_g=d_gkv, d_w_s=d_ws,
                 d_b_s=d_bs[:, :D_GROUPS].T, d_ln_g=d_lg, d_ln_b=d_lb, norm1_g=dn1)
    return dres, grads, (dsh1, dsc1, dg1)


_PER_LAYER = ("ffn_w_down", "ffn_conv_w", "ffn_w_up", "norm2_g", "norm1_g")


def _merge_layer_grads(g0, g1):
    grads = {k: v for k, v in g0.items() if k not in _PER_LAYER}
    grads.update({k: v for k, v in g1.items() if k not in _PER_LAYER})
    for k in ("ffn_w_down", "ffn_w_up"):
        grads[k] = [g0[k], g1[k]]
    grads["ffn_conv_w"] = jnp.stack([g0["ffn_conv_w"], g1["ffn_conv_w"]])
    grads["norm1_g"] = jnp.concatenate([g0["norm1_g"], g1["norm1_g"]], axis=0)
    grads["norm2_g"] = jnp.concatenate([g0["norm2_g"], g1["norm2_g"]], axis=0)
    return grads


def local_step(x, tgt, mod, ropes, w):
    sh1a, sc1a, g1a, sh2a, sc2a, g2a = _mods(mod[0:1])
    sh1b, sc1b, g1b, sh2b, sc2b, g2b = _mods(mod[1:2])
    x1, mix0 = mixer0_fwd(x, sh1a, sc1a, g1a, w)
    x2, ffn0 = _ffn_fwd(x1, w, 0, sc2a, sh2a, g2a)
    x3, mix1 = mixer1_fwd(x2, sh1b, sc1b, g1b, ropes, w)
    x4, ffn1 = _ffn_fwd(x3, w, 1, sc2b, sh2b, g2b)
    dres, d_final, loss = final_fwd_bwd(x4, w["final_norm_g"], tgt)
    dres, gf1, dm2b = _ffn_bwd(dres, x3, ffn1, w, 1, sc2b, g2b)
    dres, gm1, dm1b = mixer1_bwd(dres, mix1, ropes, w)
    dres, gf0, dm2a = _ffn_bwd(dres, x1, ffn0, w, 0, sc2a, g2a)
    dres, gm0, dm1a = mixer0_bwd(dres, mix0, w)
    grads = _merge_layer_grads({**gf0, **gm0}, {**gf1, **gm1})
    grads["final_norm_g"] = d_final
    dmod = jnp.concatenate([jnp.concatenate(dm1a + dm2a, axis=1), jnp.concatenate(dm1b + dm2b, axis=1)], axis=0)
    return loss, dres, dmod, grads


_WEIGHTS = ("ada_w", "ada_b", "norm1_g", "norm2_g", "ab_w_in", "a_conv_w", "b_mix_w", "b_scale", "ab_w_out", "cd_w_in",
            "c_q_norm_g", "c_w_uq", "c_kv_norm_g", "c_w_ukv", "d_ln_g", "d_ln_b", "d_w_s", "d_b_s", "cd_w_out",
            "ffn_w_up", "ffn_conv_w", "ffn_w_down", "final_norm_g")
_INPUTS = ("x", "c", "positions") + _WEIGHTS + ("loss_target",) + tuple("m_" + n for n in _WEIGHTS) + tuple(
    "v_" + n for n in _WEIGHTS)

def _pack_rows(parts, rows, dtype):
    flat = jnp.concatenate([p.reshape(-1).astype(dtype) for p in parts])
    return jnp.pad(flat, (0, rows * LANES - flat.shape[0])).reshape(rows, LANES)


def _rows_major(w):
    r, c = w.shape
    return w.reshape(N_CHIPS, r // N_CHIPS, c)


def start_gather(shards, tag):
    lands = [_sds((N_CHIPS,) + s.shape, s.dtype) for s in shards]
    return split_start("gather_start_" + tag, _gather_copies, shards, lands)


def finish_gather(handle, chip, tag, after):
    shards, lands = split_wait("gather_wait_" + tag, _gather_copies, handle, after)
    lands = forward_halves(lands, "gather_forward_" + tag)
    return [lax.dynamic_update_index_in_dim(o, s, chip, 0) for o, s in zip(lands, shards)]


def start_reduce(gs, core, tag):
    recv = swap_halves(gs, "swap_halves_" + tag)
    pairs = [pair_sum(g, r, core, f"pair_sum_{tag}{i}") for i, (g, r) in enumerate(zip(gs, recv))]
    lands = [_sds((N_CHIPS - 1,) + p.shape[1:], p.dtype) for p in pairs]
    return split_start("exchange_start_" + tag, _exchange_copies, pairs, lands)


def finish_reduce(handle, chip, core, tag, after):
    pairs, others = split_wait("exchange_wait_" + tag, _exchange_copies, handle, after)
    halves = [chip_sum(p, o, chip, core, f"chip_sum_{tag}{i}") for i, (p, o) in enumerate(zip(pairs, others))]
    full = join_halves(halves, "join_halves_" + tag)
    return [f.reshape(f.shape[1] * 2, f.shape[2]) for f in full]


_SMALL_SHARDED = (("a_conv_w", (3, 128), 1), ("c_q_norm_g", (1, 64), 1), ("d_ln_g", (1, 128), 1), ("d_ln_b", (1, 128), 1),
                  ("ffn_conv_w", (2, 3, 2 * D_FF // N_CHIPS), 2))
_SMALL_GRADS = (("norm1_g", (2, D_MODEL)), ("norm2_g", (2, D_MODEL)), ("b_mix_w", (4, 128, 128)), ("b_scale", (1, 512)),
                ("c_kv_norm_g", (1, 128)), ("d_w_s", (4, 128, 128)), ("d_b_s", (4, 128)), ("final_norm_g", (1, D_MODEL)),
                ("a_conv_w", (3, 512)), ("c_q_norm_g", (1, 256)), ("d_ln_g", (1, 512)), ("d_ln_b", (1, 512)),
                ("ffn_conv_w", (2, 3, 2 * D_FF)))


def _size(shape):
    n = 1
    for d in shape:
        n *= d
    return n


def kernel(x, c, positions, ada_w, ada_b, norm1_g, norm2_g, ab_w_in, a_conv_w, b_mix_w, b_scale, ab_w_out, cd_w_in, c_q_norm_g, c_w_uq, c_kv_norm_g, c_w_ukv, d_ln_g, d_ln_b, d_w_s, d_b_s, cd_w_out, ffn_w_up, ffn_conv_w, ffn_w_down, final_norm_g, loss_target, m_ada_w, m_ada_b, m_norm1_g, m_norm2_g, m_ab_w_in, m_a_conv_w, m_b_mix_w, m_b_scale, m_ab_w_out, m_cd_w_in, m_c_q_norm_g, m_c_w_uq, m_c_kv_norm_g, m_c_w_ukv, m_d_ln_g, m_d_ln_b, m_d_w_s, m_d_b_s, m_cd_w_out, m_ffn_w_up, m_ffn_conv_w, m_ffn_w_down, m_final_norm_g, v_ada_w, v_ada_b, v_norm1_g, v_norm2_g, v_ab_w_in, v_a_conv_w, v_b_mix_w, v_b_scale, v_ab_w_out, v_cd_w_in, v_c_q_norm_g, v_c_w_uq, v_c_kv_norm_g, v_c_w_ukv, v_d_ln_g, v_d_ln_b, v_d_w_s, v_d_b_s, v_cd_w_out, v_ffn_w_up, v_ffn_conv_w, v_ffn_w_down, v_final_norm_g):
    args = (x, c, positions, ada_w, ada_b, norm1_g, norm2_g, ab_w_in, a_conv_w, b_mix_w, b_scale, ab_w_out, cd_w_in, c_q_norm_g, c_w_uq, c_kv_norm_g, c_w_ukv, d_ln_g, d_ln_b, d_w_s, d_b_s, cd_w_out, ffn_w_up, ffn_conv_w, ffn_w_down, final_norm_g, loss_target, m_ada_w, m_ada_b, m_norm1_g, m_norm2_g, m_ab_w_in, m_a_conv_w, m_b_mix_w, m_b_scale, m_ab_w_out, m_cd_w_in, m_c_q_norm_g, m_c_w_uq, m_c_kv_norm_g, m_c_w_ukv, m_d_ln_g, m_d_ln_b, m_d_w_s, m_d_b_s, m_cd_w_out, m_ffn_w_up, m_ffn_conv_w, m_ffn_w_down, m_final_norm_g, v_ada_w, v_ada_b, v_norm1_g, v_norm2_g, v_ab_w_in, v_a_conv_w, v_b_mix_w, v_b_scale, v_ab_w_out, v_cd_w_in, v_c_q_norm_g, v_c_w_uq, v_c_kv_norm_g, v_c_w_ukv, v_d_ln_g, v_d_ln_b, v_d_w_s, v_d_b_s, v_cd_w_out, v_ffn_w_up, v_ffn_conv_w, v_ffn_w_down, v_final_norm_g)
    a = dict(zip(_INPUTS, args, strict=True))
    xi, yi, ci = _place()
    chip = 2 * xi + yi
    dev = 4 * xi + 2 * yi + ci
    x = a["x"][0]
    tgt = a["loss_target"][0]

    bf = lambda t: t.astype(BF16)
    mix0_shards = [bf(a["ab_w_in"][0]), bf(a["ab_w_out"][0])]
    ffn0_shards = [bf(a["ffn_w_up"][0]), bf(a["ffn_w_down"][0])]
    lay1_shards = [bf(a["cd_w_in"][0]), bf(a["c_w_uq"][0]), bf(a["c_w_ukv"][0]), bf(a["cd_w_out"][0]),
                   bf(a["ffn_w_up"][1]), bf(a["ffn_w_down"][1])]
    mix0_handle, tok_a = start_gather(mix0_shards, "mix0")
    ffn0_handle, tok_b = start_gather(ffn0_shards, "ffn0")
    lay1_handle, tok_c = start_gather(lay1_shards, "lay1")

    small_parts = [a["c"] + (tok_a + tok_b + tok_c)] + [a[n] for n, _, _ in _SMALL_SHARDED]
    rows1 = -(-sum(p.size for p in small_parts) // LANES // 8) * 8
    g1 = all_gather8(_pack_rows(small_parts, rows1, F32), "gather_small").reshape(N_DEV, rows1 * LANES)
    c_all = g1[:, :D_MODEL]
    per_chip = g1[0::2]
    small_full = {}
    off = D_MODEL
    for n, shp, axis in _SMALL_SHARDED:
        piece = per_chip[:, off:off + _size(shp)].reshape((N_CHIPS,) + shp)
        small_full[n] = jnp.concatenate([piece[k] for k in range(N_CHIPS)], axis=axis)
        off += _size(shp)

    merge = lambda t: t.reshape(t.shape[0] * t.shape[1], t.shape[2])
    w = dict(norm1_g=a["norm1_g"], norm2_g=a["norm2_g"], b_mix_w=a["b_mix_w"][0], b_scale=a["b_scale"],
             c_kv_norm_g=a["c_kv_norm_g"], d_w_s=a["d_w_s"][0], d_b_s=a["d_b_s"][0],
             final_norm_g=a["final_norm_g"].reshape(1, D_MODEL), **small_full)

    ncol = N_MOD * D_MODEL // N_CHIPS
    ada_b_mine = lax.dynamic_slice_in_dim(a["ada_b"], chip * ncol, ncol, axis=1)
    mod_cols = ada_mod(c_all, a["ada_w"], ada_b_mine)
    g2 = all_gather8(mod_cols.reshape(-1, LANES), "gather_mod").reshape(N_DEV, 2, N_DEV, ncol)
    mod = lax.dynamic_index_in_dim(g2[0::2], dev, axis=2, keepdims=False)
    mod = mod.transpose(1, 0, 2).reshape(2, N_MOD * D_MODEL)

    ropes = rope_tables(a["positions"][0])
    cm16 = lambda t: chip_major(t).astype(BF16)
    sh1a, sc1a, g1a, sh2a, sc2a, g2a = _mods(mod[0:1])
    sh1b, sc1b, g1b, sh2b, sc2b, g2b = _mods(mod[1:2])

    w_in0, w_out0 = finish_gather(mix0_handle, chip, "mix0", mod)
    w.update(ab_w_in=w_in0, ab_w_out=merge(w_out0))
    x1, mix0 = mixer0_fwd(x, sh1a, sc1a, g1a, w)
    up0, down0 = finish_gather(ffn0_handle, chip, "ffn0", x1)
    w.update(ffn_w_up=[up0, None], ffn_w_down=[merge(down0), None])
    x2, ffn0 = _ffn_fwd(x1, w, 0, sc2a, sh2a, g2a)
    cd_in, uq, ukv, cd_out, up1, down1 = finish_gather(lay1_handle, chip, "lay1", x2)
    w.update(prepare_weights(dict(cd_w_in=from_chip_major(cd_in), c_w_uq=from_chip_major(uq), c_w_ukv=from_chip_major(ukv),
                                  cd_w_out=merge(cd_out))))
    w.update(ffn_w_up=[up0, up1], ffn_w_down=[merge(down0), merge(down1)])
    x3, mix1 = mixer1_fwd(x2, sh1b, sc1b, g1b, ropes, w)
    x4, ffn1 = _ffn_fwd(x3, w, 1, sc2b, sh2b, g2b)
    dres, d_final, loss = final_fwd_bwd(x4, w["final_norm_g"], tgt)
    loss = lax.psum(loss[0, 0], ("x", "y", "c"))

    dres, gf1, dm2b = _ffn_bwd(dres, x3, ffn1, w, 1, sc2b, g2b)
    ffn1_red, tok = start_reduce([gf1["ffn_w_up"], _rows_major(gf1["ffn_w_down"])], ci, "ffn1")
    dres, gm1, dm1b = mixer1_bwd(dres, mix1[:-1] + (mix1[-1] + tok,), ropes, w)
    gm1 = unprepare_grads(gm1)
    mix1_red, tok = start_reduce([cm16(gm1["cd_w_in"]), cm16(gm1["c_w_uq"]), cm16(gm1["c_w_ukv"]),
                                  _rows_major(gm1["cd_w_out"]).astype(BF16)], ci, "mix1")
    red_up1, red_down1 = finish_reduce(ffn1_red, chip, ci, "ffn1", dres)
    dres, gf0, dm2a = _ffn_bwd(dres, x1, ffn0, w, 0, sc2a, g2a + tok)
    ffn0_red, tok = start_reduce([gf0["ffn_w_up"], _rows_major(gf0["ffn_w_down"])], ci, "ffn0")
    red_cd_in, red_uq, red_ukv, red_cd_out = finish_reduce(mix1_red, chip, ci, "mix1", dres)
    grad_x, gm0, dm1a = mixer0_bwd(dres, mix0[:-1] + (mix0[-1] + tok,), w)
    mix0_red, tok = start_reduce([gm0["ab_w_in"], _rows_major(gm0["ab_w_out"])], ci, "mix0")
    grads = _merge_layer_grads({**gf0, **gm0}, {**gf1, **gm1})
    grads["final_norm_g"] = d_final
    dmod = jnp.concatenate([jnp.concatenate(dm1a + dm2a, axis=1), jnp.concatenate(dm1b + dm2b, axis=1)], axis=0) + tok

    parts3 = [dmod] + [grads[n] for n, _ in _SMALL_GRADS]
    rows3 = -(-sum(p.size for p in parts3) // LANES // 8) * 8
    g3 = all_gather8(_pack_rows(parts3, rows3, F32), "gather_grads").reshape(N_DEV, rows3, LANES)
    summed = sum8(g3).reshape(-1)
    nmod = 2 * N_MOD * D_MODEL
    out_grads = {"ada_b": summed[:nmod].reshape(2, N_MOD * D_MODEL)}
    off = nmod
    for n, shp in _SMALL_GRADS:
        out_grads[n] = summed[off:off + _size(shp)].reshape(shp)
        off += _size(shp)
    for n, shp, axis in _SMALL_SHARDED:
        width = out_grads[n].shape[-1] // N_CHIPS
        out_grads[n] = lax.dynamic_slice_in_dim(out_grads[n], chip * width, width, axis=out_grads[n].ndim - 1)
    dmod_all = g3.reshape(N_DEV, rows3 * LANES)[:, :nmod].reshape(N_DEV, 2, N_MOD * D_MODEL)
    dmod_mine = lax.dynamic_slice_in_dim(dmod_all, chip * ncol, ncol, axis=2).transpose(1, 0, 2)
    out_grads["ada_w"] = ada_grad(c_all.T, dmod_mine)

    red_up0, red_down0 = finish_reduce(ffn0_red, chip, ci, "ffn0", summed)
    red_in0, red_out0 = finish_reduce(mix0_red, chip, ci, "mix0", out_grads["ada_w"])
    out_grads.update(ab_w_in=red_in0, ab_w_out=red_out0, cd_w_in=red_cd_in, c_w_uq=red_uq, c_w_ukv=red_ukv,
                     cd_w_out=red_cd_out)
    per_layer = dict(ffn_w_up=(red_up0, red_up1), ffn_w_down=(red_down0, red_down1))

    g_out, d_out, m_out, v_out = [], [], [], []
    for n in _WEIGHTS:
        if n in per_layer:
            g, d, nm, nv = adamw_layers(a[n], *per_layer[n], a["m_" + n], a["v_" + n], "adamw_" + n)
        else:
            g, d, nm, nv = adamw(a[n], out_grads[n].reshape(a[n].shape), a["m_" + n], a["v_" + n], "adamw_" + n)
        g_out.append(g)
        d_out.append(d)
        m_out.append(nm)
        v_out.append(nv)
    return (loss, grad_x[None], *g_out, *d_out, *m_out, *v_out)
```

```python
import functools

import jax
import jax.numpy as jnp
from jax import lax
from jax.experimental import pallas as pl
from jax.experimental.pallas import tpu as pltpu

F32 = jnp.float32
BF16 = jnp.bfloat16
EPS = 1e-6
D_MODEL = 1024
N_MOD = 6
A_WIDTH = 512
B_GROUPS = 4
POOL_WINDOWS = (2, 4, 8, 16)
C_HEADS = 8
C_NOPE = 64
C_ROPE = 32
C_V = 64
C_Q_RANK = 256
C_KV_RANK = 128
HEAD_PAD = 128
ROPE_THETA = 10000.0
D_GROUPS = 4
D_CHUNK = 128
D_FF = 2816
FF_UNIT = 128
ADAM_LR = 0.001
ADAM_B1 = 0.9
ADAM_B2 = 0.999
ADAM_EPS = 1e-08
ADAM_WD = 0.01
ADAM_STEP = 10
N_CHIPS = 4
N_DEV = 8
LANES = 128
VMEM_BIG = 56 * 1024 * 1024
MESH = pl.DeviceIdType.MESH


def _sds(shape, dtype=F32):
    return jax.ShapeDtypeStruct(tuple(shape), dtype)


def _tile(n, cap, mult=128):
    if n <= cap:
        return n
    best = None
    for t in range(mult, cap + 1, mult):
        if n % t == 0:
            best = t
    assert best is not None, (n, cap, mult)
    return best


def _params(dims=None, vmem=None):
    return pltpu.CompilerParams(dimension_semantics=dims, vmem_limit_bytes=vmem)


def _shift_down(v, k):
    r = pltpu.roll(v, k, axis=0)
    t = lax.broadcasted_iota(jnp.int32, v.shape, 0)
    return jnp.where(t >= k, r, 0.0)


def _shift_up(v, k):
    n = v.shape[0]
    r = pltpu.roll(v, n - k, axis=0)
    t = lax.broadcasted_iota(jnp.int32, v.shape, 0)
    return jnp.where(t < n - k, r, 0.0)


def _sigmoid(v):
    return 1.0 / (1.0 + jnp.exp(-v))


_GELU_C = 0.7978845608028654
_GELU_A = 0.044715


def _gelu(v):
    return 0.5 * v * (1.0 + jnp.tanh(_GELU_C * (v + _GELU_A * v * v * v)))


def _gelu_grad(v):
    th = jnp.tanh(_GELU_C * (v + _GELU_A * v * v * v))
    return 0.5 * (1.0 + th) + 0.5 * v * (1.0 - th * th) * _GELU_C * (1.0 + 3.0 * _GELU_A * v * v)


_NN = (((1,), (0,)), ((), ()))
_NT = (((1,), (1,)), ((), ()))
_TN = (((0,), (0,)), ((), ()))


def _dot(a, b, dims=_NN):
    return lax.dot_general(a, b, dims, preferred_element_type=F32)


def _logical(t, groups):
    return (t.shape[-2], t.shape[-1] * groups)


def _block(tr, tc, groups, cols, where):
    if groups == 1:
        return pl.BlockSpec((tr, tc), where)
    per = cols // groups // tc

    def index(i, j, s):
        r, c = where(i, j, s)
        return (c // per, r, c % per)

    return pl.BlockSpec((None, tr, tc), index)


def matmul(a, b, mode, out_dtype, name, ga=1, gb=1, go=1, tm=None, tn=None, tk=None):
    (ar, ac), (br, bc) = _logical(a, ga), _logical(b, gb)
    if mode == "nn":
        m, k, n = ar, ac, bc
        a_col, b_col = "k", "n"
    elif mode == "nt":
        m, k, n = ar, ac, br
        a_col, b_col = "k", "k"
    else:
        k, m, n = ar, ac, bc
        a_col, b_col = "m", "n"
    limit = {"m": m, "n": n // go, "k": k}
    limit[a_col] = min(limit[a_col], ac // ga)
    limit[b_col] = min(limit[b_col], bc // gb)
    tm = tm or _tile(limit["m"], 1024, 128 if mode == "tn" else 16)
    tn = tn or _tile(limit["n"], 512)
    tk = tk or _tile(limit["k"], 2048, 16 if mode == "tn" else 128)
    nk = k // tk
    if mode == "nn":
        a_spec = _block(tm, tk, ga, ac, lambda i, j, s: (i, s))
        b_spec = _block(tk, tn, gb, bc, lambda i, j, s: (s, j))
        dims = _NN
    elif mode == "nt":
        a_spec = _block(tm, tk, ga, ac, lambda i, j, s: (i, s))
        b_spec = _block(tn, tk, gb, bc, lambda i, j, s: (j, s))
        dims = _NT
    else:
        a_spec = _block(tk, tm, ga, ac, lambda i, j, s: (s, i))
        b_spec = _block(tk, tn, gb, bc, lambda i, j, s: (s, j))
        dims = _TN
    o_spec = _block(tm, tn, go, n, lambda i, j, s: (i, j))
    out_shape = _sds((m, n), out_dtype) if go == 1 else _sds((go, m, n // go), out_dtype)

    def body(a_ref, b_ref, o_ref, acc_ref):
        s = pl.program_id(2)

        @pl.when(s == 0)
        def _():
            acc_ref[...] = jnp.zeros_like(acc_ref)

        acc_ref[...] += _dot(a_ref[...], b_ref[...], dims)

        @pl.when(s == nk - 1)
        def _():
            o_ref[...] = acc_ref[...].astype(o_ref.dtype)

    return pl.pallas_call(
        body, name=name, out_shape=out_shape, grid=(m // tm, n // tn, nk),
        in_specs=[a_spec, b_spec], out_specs=o_spec,
        scratch_shapes=[pltpu.VMEM((tm, tn), F32)],
        compiler_params=_params(("parallel", "parallel", "arbitrary"), VMEM_BIG),
    )(a, b)


def _rows(tm, n):
    return pl.BlockSpec((tm, n), lambda i: (i, 0))


def _vec(n):
    return pl.BlockSpec((1, n), lambda i: (0, 0))


def modnorm_fwd(x, g, sc, sh, name):
    s, d = x.shape
    tm = _tile(s, 256, 8)

    def body(x_ref, g_ref, sc_ref, sh_ref, o_ref):
        xv = x_ref[...]
        r = lax.rsqrt(jnp.mean(xv * xv, axis=-1, keepdims=True) + EPS)
        o_ref[...] = ((xv * r) * g_ref[...] * (1.0 + sc_ref[...]) + sh_ref[...]).astype(BF16)

    return pl.pallas_call(
        body, name=name, out_shape=_sds((s, d), BF16), grid=(s // tm,),
        in_specs=[_rows(tm, d), _vec(d), _vec(d), _vec(d)], out_specs=_rows(tm, d),
        compiler_params=_params(("parallel",)),
    )(x, g, sc, sh)


def resid_fwd(x, y, gate, name):
    s, d = x.shape
    tm = _tile(s, 256, 8)

    def body(x_ref, y_ref, g_ref, o_ref):
        o_ref[...] = x_ref[...] + g_ref[...] * y_ref[...]

    return pl.pallas_call(
        body, name=name, out_shape=_sds((s, d)), grid=(s // tm,),
        in_specs=[_rows(tm, d), _rows(tm, d), _vec(d)], out_specs=_rows(tm, d),
        compiler_params=_params(("parallel",)),
    )(x, y, gate)


def gate_bwd(dres, y, gate, name):
    s, d = dres.shape
    tm = _tile(s, 256, 8)

    def body(dr_ref, y_ref, g_ref, dy_ref, dg_ref):
        @pl.when(pl.program_id(0) == 0)
        def _():
            dg_ref[...] = jnp.zeros_like(dg_ref)

        dr = dr_ref[...]
        dy_ref[...] = (dr * g_ref[...]).astype(BF16)
        dg_ref[...] += jnp.sum(dr * y_ref[...], axis=0, keepdims=True)

    return pl.pallas_call(
        body, name=name, out_shape=(_sds((s, d), BF16), _sds((1, d))), grid=(s // tm,),
        in_specs=[_rows(tm, d), _rows(tm, d), _vec(d)], out_specs=(_rows(tm, d), _vec(d)),
        compiler_params=_params(("arbitrary",)),
    )(dres, y, gate)


def norm_bwd(x, dh, g, sc, dres, name):
    s, d = x.shape
    tm = _tile(s, 256, 8)
    nsteps = s // tm

    def body(x_ref, dh_ref, g_ref, sc_ref, dr_ref, dx_ref, dsh_ref, dsc_ref, dg_ref, a2_ref):
        i = pl.program_id(0)

        @pl.when(i == 0)
        def _():
            dsh_ref[...] = jnp.zeros_like(dsh_ref)
            a2_ref[...] = jnp.zeros_like(a2_ref)

        xv = x_ref[...]
        dh = dh_ref[...]
        r = lax.rsqrt(jnp.mean(xv * xv, axis=-1, keepdims=True) + EPS)
        xh = xv * r
        dsh_ref[...] += jnp.sum(dh, axis=0, keepdims=True)
        a2_ref[...] += jnp.sum(dh * xh, axis=0, keepdims=True)
        dxh = dh * (g_ref[...] * (1.0 + sc_ref[...]))
        dx = r * (dxh - xh * jnp.mean(dxh * xh, axis=-1, keepdims=True))
        dx_ref[...] = dr_ref[...] + dx

        @pl.when(i == nsteps - 1)
        def _():
            dsc_ref[...] = a2_ref[...] * g_ref[...]
            dg_ref[...] = a2_ref[...] * (1.0 + sc_ref[...])

    return pl.pallas_call(
        body, name=name, out_shape=(_sds((s, d)), _sds((1, d)), _sds((1, d)), _sds((1, d))), grid=(nsteps,),
        in_specs=[_rows(tm, d), _rows(tm, d), _vec(d), _vec(d), _rows(tm, d)],
        out_specs=(_rows(tm, d), _vec(d), _vec(d), _vec(d)),
        scratch_shapes=[pltpu.VMEM((1, d), F32)],
        compiler_params=_params(("arbitrary",)),
    )(x, dh, g, sc, dres)


def final_fwd_bwd(x, g, tgt):
    s, d = x.shape
    tm = _tile(s, 256, 8)

    def body(x_ref, g_ref, t_ref, dx_ref, dg_ref, loss_ref):
        @pl.when(pl.program_id(0) == 0)
        def _():
            dg_ref[...] = jnp.zeros_like(dg_ref)
            loss_ref[...] = jnp.zeros_like(loss_ref)

        xv = x_ref[...]
        gv = g_ref[...]
        r = lax.rsqrt(jnp.mean(xv * xv, axis=-1, keepdims=True) + EPS)
        xh = xv * r
        e = xh * gv - t_ref[...]
        row = jnp.sum(e * e, axis=-1, keepdims=True) * (0.5 / d)
        loss_ref[...] += jnp.sum(row, axis=0, keepdims=True)
        dy = e * (1.0 / d)
        dg_ref[...] += jnp.sum(dy * xh, axis=0, keepdims=True)
        dxh = dy * gv
        dx_ref[...] = r * (dxh - xh * jnp.mean(dxh * xh, axis=-1, keepdims=True))

    return pl.pallas_call(
        body, name="final_fwd_bwd", out_shape=(_sds((s, d)), _sds((1, d)), _sds((1, LANES))), grid=(s // tm,),
        in_specs=[_rows(tm, d), _vec(d), _rows(tm, d)], out_specs=(_rows(tm, d), _vec(d), _vec(LANES)),
        compiler_params=_params(("arbitrary",)),
    )(x, g, tgt)


def _taps(v):
    return _shift_down(v, 2), _shift_down(v, 1), v


def _conv3_taps(taps, w):
    return w[0:1, :] * taps[0] + w[1:2, :] * taps[1] + w[2:3, :] * taps[2]


def _conv3(v, w):
    return _conv3_taps(_taps(v), w)


def _conv3_t(dv, w):
    return w[0:1, :] * _shift_up(dv, 2) + w[1:2, :] * _shift_up(dv, 1) + w[2:3, :] * dv


def _conv3_dw_taps(dv, taps):
    return jnp.concatenate([jnp.sum(dv * t, axis=0, keepdims=True) for t in taps], axis=0)


def _conv3_dw(dv, v):
    return _conv3_dw_taps(dv, _taps(v))


def gconv_fwd(z, conv_w):
    s = z.shape[0]
    nb = A_WIDTH // LANES

    def body(b_ref, c_ref, a_ref, w_ref, o_ref):
        b, c, a = b_ref[...].astype(F32), c_ref[...].astype(F32), a_ref[...].astype(F32)
        o_ref[...] = (b * _conv3(c * a, w_ref[...])).astype(BF16)

    col = lambda off: pl.BlockSpec((s, LANES), lambda j: (0, off + j))
    return pl.pallas_call(
        body, name="gconv_fwd", out_shape=_sds((s, A_WIDTH), BF16), grid=(nb,),
        in_specs=[col(0), col(nb), col(2 * nb), pl.BlockSpec((3, LANES), lambda j: (0, j))],
        out_specs=pl.BlockSpec((s, LANES), lambda j: (0, j)),
        compiler_params=_params(("parallel",), VMEM_BIG),
    )(z, z, z, conv_w)


def gconv_bwd(z, conv_w, dycat):
    s = z.shape[0]
    nb = A_WIDTH // LANES

    def body(b_ref, c_ref, a_ref, w_ref, dy_ref, db_ref, dc_ref, da_ref, dw_ref):
        c, a, w, dy = c_ref[...].astype(F32), a_ref[...].astype(F32), w_ref[...], dy_ref[...].astype(F32)
        ca = c * a
        db_ref[...] = (dy * _conv3(ca, w)).astype(BF16)
        dconv = dy * b_ref[...].astype(F32)
        dw_ref[...] = _conv3_dw(dconv, ca)
        dca = _conv3_t(dconv, w)
        dc_ref[...] = (dca * a).astype(BF16)
        da_ref[...] = (dca * c).astype(BF16)

    col = lambda off: pl.BlockSpec((s, LANES), lambda j: (0, off + j))
    wspec = pl.BlockSpec((3, LANES), lambda j: (0, j))
    part = _sds((s, A_WIDTH), BF16)
    return pl.pallas_call(
        body, name="gconv_bwd", out_shape=(part, part, part, _sds((3, A_WIDTH))), grid=(nb,),
        in_specs=[col(0), col(nb), col(2 * nb), wspec, col(0)],
        out_specs=(col(0), col(0), col(0), wspec),
        compiler_params=_params(("parallel",), VMEM_BIG),
    )(z, z, z, conv_w, dycat)


def _pool_counts(s, w):
    t = lax.broadcasted_iota(jnp.int32, (s, 1), 0)
    return jnp.minimum(t + 1, w).astype(F32)


def _pooled(p, levels):
    acc = p
    for lv in range(levels):
        acc = acc + _shift_down(acc, 2 ** lv)
    return acc / _pool_counts(p.shape[0], 2 ** levels) - p


def pool_fwd(z, mix_w, scale):
    s = z.shape[0]

    def make(g):
        def body_g(p_ref, m_ref, sc_ref, o_ref):
            pooled = _pooled(p_ref[...].astype(F32), g + 1)
            y = _dot(pooled.astype(BF16), m_ref[...].astype(BF16))
            o_ref[...] = (y * sc_ref[...]).astype(BF16)
        return body_g

    outs = []
    for g in range(B_GROUPS):
        outs.append(pl.pallas_call(
            make(g), name=f"pool_fwd{g}", out_shape=_sds((s, LANES), BF16), grid=(1,),
            in_specs=[pl.BlockSpec((s, LANES), lambda i, g=g: (0, 3 * (A_WIDTH // LANES) + g)),
                      pl.BlockSpec((None, LANES, LANES), lambda i, g=g: (g, 0, 0)),
                      pl.BlockSpec((1, LANES), lambda i, g=g: (0, g))],
            out_specs=pl.BlockSpec((s, LANES), lambda i: (0, 0)),
            compiler_params=_params(("arbitrary",), VMEM_BIG),
        )(z, mix_w, scale))
    return outs


def pool_bwd(z, mix_w, scale, dycat):
    s = z.shape[0]

    def make(g):
        w = 2 ** (g + 1)

        def body_g(p_ref, m_ref, sc_ref, dy_ref, dp_ref, dm_ref, dsc_ref):
            pooled = _pooled(p_ref[...].astype(F32), g + 1)
            mw = m_ref[...].astype(BF16)
            pb = pooled.astype(BF16)
            dy = dy_ref[...].astype(F32)
            dsc_ref[...] = jnp.sum(dy * _dot(pb, mw), axis=0, keepdims=True)
            dmix = (dy * sc_ref[...]).astype(BF16)
            dm_ref[...] = _dot(pb, dmix, _TN)
            dpool = _dot(dmix, mw, _NT)
            acc = dpool / _pool_counts(s, w)
            for lv in range(g + 1):
                acc = acc + _shift_up(acc, 2 ** lv)
            dp_ref[...] = (acc - dpool).astype(BF16)
        return body_g

    outs = []
    for g in range(B_GROUPS):
        outs.append(pl.pallas_call(
            make(g), name=f"pool_bwd{g}",
            out_shape=(_sds((s, LANES), BF16), _sds((LANES, LANES)), _sds((1, LANES))), grid=(1,),
            in_specs=[pl.BlockSpec((s, LANES), lambda i, g=g: (0, 3 * (A_WIDTH // LANES) + g)),
                      pl.BlockSpec((None, LANES, LANES), lambda i, g=g: (g, 0, 0)),
                      pl.BlockSpec((1, LANES), lambda i, g=g: (0, g)),
                      pl.BlockSpec((s, LANES), lambda i, g=g: (0, A_WIDTH // LANES + g))],
            out_specs=(pl.BlockSpec((s, LANES), lambda i: (0, 0)), pl.BlockSpec((LANES, LANES), lambda i: (0, 0)),
                       pl.BlockSpec((1, LANES), lambda i: (0, 0))),
            compiler_params=_params(("arbitrary",), VMEM_BIG),
        )(z, mix_w, scale, dycat))
    return outs


_FF_BLOCKS = D_FF // FF_UNIT


def _ff_spec(s):
    return pl.BlockSpec((2, s, FF_UNIT), lambda j: (0, 0, j))


def _ff_wspecs():
    return [pl.BlockSpec((3, FF_UNIT), lambda j: (0, j)), pl.BlockSpec((3, FF_UNIT), lambda j: (0, _FF_BLOCKS + j))]


def ffn_act_fwd(zf, conv_w, name):
    s = zf.shape[1]

    def body(z_ref, wg_ref, wu_ref, o_ref):
        g = _conv3(z_ref[0].astype(F32), wg_ref[...])
        u = _conv3(z_ref[1].astype(F32), wu_ref[...])
        o_ref[...] = (g * _sigmoid(g) * u).astype(BF16)

    return pl.pallas_call(
        body, name=name, out_shape=_sds((s, D_FF), BF16), grid=(_FF_BLOCKS,),
        in_specs=[_ff_spec(s)] + _ff_wspecs(), out_specs=pl.BlockSpec((s, FF_UNIT), lambda j: (0, j)),
        compiler_params=_params(("parallel",), VMEM_BIG),
    )(zf, conv_w, conv_w)


def ffn_act_bwd(zf, conv_w, da, name):
    s = zf.shape[1]

    def body(z_ref, wg_ref, wu_ref, da_ref, dz_ref, dw_ref):
        tg, tu = _taps(z_ref[0].astype(F32)), _taps(z_ref[1].astype(F32))
        wg, wu = wg_ref[...], wu_ref[...]
        dav = da_ref[...].astype(F32)
        g = _conv3_taps(tg, wg)
        u = _conv3_taps(tu, wu)
        sg = _sigmoid(g)
        dg = dav * u * (sg * (1.0 + g * (1.0 - sg)))
        du = dav * (g * sg)
        dw_ref[0] = _conv3_dw_taps(dg, tg)
        dw_ref[1] = _conv3_dw_taps(du, tu)
        dz_ref[0] = _conv3_t(dg, wg).astype(BF16)
        dz_ref[1] = _conv3_t(du, wu).astype(BF16)

    return pl.pallas_call(
        body, name=name, out_shape=(_sds((2, s, D_FF), BF16), _sds((2, 3, D_FF))), grid=(_FF_BLOCKS,),
        in_specs=[_ff_spec(s)] + _ff_wspecs() + [pl.BlockSpec((s, FF_UNIT), lambda j: (0, j))],
        out_specs=(_ff_spec(s), pl.BlockSpec((2, 3, FF_UNIT), lambda j: (0, 0, j))),
        compiler_params=_params(("parallel",), VMEM_BIG),
    )(zf, conv_w, conv_w, da)


def _rope(v, cs, s1, s2):
    return v * cs + pltpu.roll(v, LANES - C_ROPE // 2, axis=1) * s1 + pltpu.roll(v, C_ROPE // 2, axis=1) * s2


def _rope_t(dv, cs, s1, s2):
    return dv * cs + pltpu.roll(dv * s1, C_ROPE // 2, axis=1) + pltpu.roll(dv * s2, LANES - C_ROPE // 2, axis=1)


def _kpe_mask(shape):
    lane = lax.broadcasted_iota(jnp.int32, shape, 1)
    return (lane >= C_NOPE) & (lane < C_NOPE + C_ROPE)


def _rms(v, g):
    r = lax.rsqrt(jnp.mean(v * v, axis=-1, keepdims=True) + EPS)
    return v * r, r


def _rms_bwd(dn, xh, r, g):
    dxh = dn * g
    return r * (dxh - xh * jnp.mean(dxh * xh, axis=-1, keepdims=True)), jnp.sum(dn * xh, axis=0, keepdims=True)


_ZQ = C_Q_RANK + C_KV_RANK + HEAD_PAD
_HW = C_HEADS * HEAD_PAD


def mla_pre_fwd(z, gq, gkv, wq, wk, wv, cs, s1, s2):
    s = z.shape[0]
    tm = _tile(s, 256, 8)

    def body(z_ref, gq_ref, gkv_ref, wq_ref, wk_ref, wv_ref, cs_ref, s1_ref, s2_ref, q_ref, k_ref, v_ref):
        zv = z_ref[...].astype(F32)
        cst, s1t, s2t = cs_ref[...], s1_ref[...], s2_ref[...]
        qh, _ = _rms(zv[:, :C_Q_RANK], None)
        qn = (qh * gq_ref[...]).astype(BF16)
        q = _dot(qn, wq_ref[...])
        kh, _ = _rms(zv[:, C_Q_RANK:C_Q_RANK + C_KV_RANK], None)
        kvn = (kh * gkv_ref[...]).astype(BF16)
        k = _dot(kvn, wk_ref[...])
        v_ref[...] = _dot(kvn, wv_ref[...]).astype(BF16)
        kpe = _rope(zv[:, C_Q_RANK + C_KV_RANK:], cst, s1t, s2t)
        for h in range(C_HEADS):
            sl = slice(h * HEAD_PAD, (h + 1) * HEAD_PAD)
            q_ref[:, sl] = _rope(q[:, sl], cst, s1t, s2t).astype(BF16)
            k_ref[:, sl] = (k[:, sl] + kpe).astype(BF16)

    full = lambda r, c: pl.BlockSpec((r, c), lambda i: (0, 0))
    hw = _sds((s, _HW), BF16)
    return pl.pallas_call(
        body, name="mla_pre_fwd", out_shape=(hw, hw, hw), grid=(s // tm,),
        in_specs=[_rows(tm, _ZQ), _vec(C_Q_RANK), _vec(C_KV_RANK), full(C_Q_RANK, _HW), full(C_KV_RANK, _HW),
                  full(C_KV_RANK, _HW), _rows(tm, LANES), _rows(tm, LANES), _rows(tm, LANES)],
        out_specs=(_rows(tm, _HW), _rows(tm, _HW), _rows(tm, _HW)),
        compiler_params=_params(("parallel",), VMEM_BIG),
    )(z, gq, gkv, wq, wk, wv, cs, s1, s2)


def mla_pre_bwd(z, gq, gkv, wq, wk, wv, cs, s1, s2, dq, dk, dv):
    s = z.shape[0]
    tm = _tile(s, 256, 8)

    def body(z_ref, gq_ref, gkv_ref, wq_ref, wk_ref, wv_ref, cs_ref, s1_ref, s2_ref, dq_ref, dk_ref, dv_ref,
             dz_ref, dwq_ref, dwk_ref, dwv_ref, dgq_ref, dgkv_ref):
        @pl.when(pl.program_id(0) == 0)
        def _():
            dwq_ref[...] = jnp.zeros_like(dwq_ref)
            dwk_ref[...] = jnp.zeros_like(dwk_ref)
            dwv_ref[...] = jnp.zeros_like(dwv_ref)
            dgq_ref[...] = jnp.zeros_like(dgq_ref)
            dgkv_ref[...] = jnp.zeros_like(dgkv_ref)

        zv = z_ref[...].astype(F32)
        cst, s1t, s2t = cs_ref[...], s1_ref[...], s2_ref[...]
        gqv, gkvv = gq_ref[...], gkv_ref[...]
        qh, rq = _rms(zv[:, :C_Q_RANK], None)
        qn = (qh * gqv).astype(BF16)
        kh, rk = _rms(zv[:, C_Q_RANK:C_Q_RANK + C_KV_RANK], None)
        kvn = (kh * gkvv).astype(BF16)

        dqv = dq_ref[...].astype(F32)
        dqp = jnp.concatenate(
            [_rope_t(dqv[:, h * HEAD_PAD:(h + 1) * HEAD_PAD], cst, s1t, s2t) for h in range(C_HEADS)], axis=1
        ).astype(BF16)
        dwq_ref[...] += _dot(qn, dqp, _TN)
        dqn = _dot(dqp, wq_ref[...], _NT)
        dql, dgq = _rms_bwd(dqn, qh, rq, gqv)
        dgq_ref[...] += dgq

        dkv = dk_ref[...]
        dkb = dkv.astype(BF16)
        dvb = dv_ref[...].astype(BF16)
        dwk_ref[...] += _dot(kvn, dkb, _TN)
        dwv_ref[...] += _dot(kvn, dvb, _TN)
        dkvn = _dot(dkb, wk_ref[...], _NT) + _dot(dvb, wv_ref[...], _NT)
        dkl, dgkv = _rms_bwd(dkvn, kh, rk, gkvv)
        dgkv_ref[...] += dgkv

        dkpe = dkv[:, :HEAD_PAD]
        for h in range(1, C_HEADS):
            dkpe = dkpe + dkv[:, h * HEAD_PAD:(h + 1) * HEAD_PAD]
        dkpe = _rope_t(jnp.where(_kpe_mask(dkpe.shape), dkpe, 0.0), cst, s1t, s2t)
        dz_ref[...] = jnp.concatenate([dql, dkl, dkpe], axis=1).astype(BF16)

    full = lambda r, c: pl.BlockSpec((r, c), lambda i: (0, 0))
    return pl.pallas_call(
        body, name="mla_pre_bwd",
        out_shape=(_sds((s, _ZQ), BF16), _sds((C_Q_RANK, _HW)), _sds((C_KV_RANK, _HW)), _sds((C_KV_RANK, _HW)),
                   _sds((1, C_Q_RANK)), _sds((1, C_KV_RANK))),
        grid=(s // tm,),
        in_specs=[_rows(tm, _ZQ), _vec(C_Q_RANK), _vec(C_KV_RANK), full(C_Q_RANK, _HW), full(C_KV_RANK, _HW),
                  full(C_KV_RANK, _HW), _rows(tm, LANES), _rows(tm, LANES), _rows(tm, LANES),
                  _rows(tm, _HW), _rows(tm, _HW), _rows(tm, _HW)],
        out_specs=(_rows(tm, _ZQ), full(C_Q_RANK, _HW), full(C_KV_RANK, _HW), full(C_KV_RANK, _HW),
                   _vec(C_Q_RANK), _vec(C_KV_RANK)),
        compiler_params=_params(("arbitrary",), VMEM_BIG),
    )(z, gq, gkv, wq, wk, wv, cs, s1, s2, dq, dk, dv)


_ATT_SCALE = (C_NOPE + C_ROPE) ** -0.5
_NEG = -1e30


def _att_probs(q, k, row0):
    sc = _dot(q, k, _NT) * _ATT_SCALE
    qpos = row0 + lax.broadcasted_iota(jnp.int32, sc.shape, 0)
    kpos = lax.broadcasted_iota(jnp.int32, sc.shape, 1)
    sc = jnp.where(kpos <= qpos, sc, _NEG)
    e = jnp.exp(sc - jnp.max(sc, axis=-1, keepdims=True))
    return e / jnp.sum(e, axis=-1, keepdims=True)


def _causal_cases(i, nq, tq, fn):
    if nq > 8:
        fn(nq * tq)
        return
    for blk in range(nq):
        pl.when(i == blk)(functools.partial(fn, (blk + 1) * tq))


def attn_fwd(q, k, v):
    s = q.shape[0]
    tq = _tile(s, 256, 8)
    nq = s // tq

    def body(q_ref, k_ref, v_ref, o_ref):
        i = pl.program_id(1)

        def case(nk):
            p = _att_probs(q_ref[...], k_ref[:nk, :], i * tq)
            o_ref[...] = _dot(p.astype(BF16), v_ref[:nk, :]).astype(BF16)

        _causal_cases(i, nq, tq, case)

    qspec = pl.BlockSpec((tq, HEAD_PAD), lambda h, i: (i, h))
    kspec = pl.BlockSpec((s, HEAD_PAD), lambda h, i: (0, h))
    return pl.pallas_call(
        body, name="attn_fwd", out_shape=_sds((s, _HW), BF16), grid=(C_HEADS, s // tq),
        in_specs=[qspec, kspec, kspec], out_specs=qspec,
        compiler_params=_params(("parallel", "parallel"), VMEM_BIG),
    )(q, k, v)


def attn_bwd(q, k, v, o, do_all, do_col0):
    s = q.shape[0]
    tq = _tile(s, 256, 8)

    def body(q_ref, k_ref, v_ref, o_ref, do_ref, dq_ref, dk_ref, dv_ref):
        i = pl.program_id(1)

        @pl.when(i == 0)
        def _():
            dk_ref[...] = jnp.zeros_like(dk_ref)
            dv_ref[...] = jnp.zeros_like(dv_ref)

        def case(nk):
            qv, kv, vv, dov = q_ref[...], k_ref[:nk, :], v_ref[:nk, :], do_ref[...]
            p = _att_probs(qv, kv, i * tq)
            dp = _dot(dov, vv, _NT)
            delta = jnp.sum(dov.astype(F32) * o_ref[...].astype(F32), axis=-1, keepdims=True)
            ds = (p * (dp - delta) * _ATT_SCALE).astype(BF16)
            dq_ref[...] = _dot(ds, kv).astype(BF16)
            dk_ref[:nk, :] += _dot(ds, qv, _TN)
            dv_ref[:nk, :] += _dot(p.astype(BF16), dov, _TN)

        _causal_cases(i, s // tq, tq, case)

    qspec = pl.BlockSpec((tq, HEAD_PAD), lambda h, i: (i, h))
    dospec = pl.BlockSpec((tq, HEAD_PAD), lambda h, i: (i, do_col0 + h))
    kspec = pl.BlockSpec((s, HEAD_PAD), lambda h, i: (0, h))
    return pl.pallas_call(
        body, name="attn_bwd", out_shape=(_sds((s, _HW), BF16), _sds((s, _HW)), _sds((s, _HW))),
        grid=(C_HEADS, s // tq),
        in_specs=[qspec, kspec, kspec, qspec, dospec], out_specs=(qspec, kspec, kspec),
        compiler_params=_params(("parallel", "arbitrary"), VMEM_BIG),
    )(q, k, v, o, do_all)


_DW = D_GROUPS * LANES


def _tril_bf16(w):
    r = lax.broadcasted_iota(jnp.int32, w.shape, 0)
    c = lax.broadcasted_iota(jnp.int32, w.shape, 1)
    return jnp.where(c <= r, w, 0.0).astype(BF16)


def _sgu_forward(zu, zv, lg, lb, ws_ref, bs):
    u = _gelu(zu)
    v = _gelu(zv)
    mu = jnp.mean(v, axis=-1, keepdims=True)
    vc = v - mu
    rstd = lax.rsqrt(jnp.mean(vc * vc, axis=-1, keepdims=True) + EPS)
    xh = vc * rstd
    vln = (xh * lg + lb).astype(BF16)
    mixed = []
    for g in range(D_GROUPS):
        wg = _tril_bf16(ws_ref[g])
        mixed.append(_dot(wg, vln[:, g * LANES:(g + 1) * LANES]) + bs[:, g:g + 1])
    return u, xh, rstd, vln, jnp.concatenate(mixed, axis=1)


def sgu_fwd(z, lg, lb, ws, bs_t):
    s = z.shape[0]
    nchunk = s // D_CHUNK

    def body(zu_ref, zv_ref, lg_ref, lb_ref, ws_ref, bs_ref, o_ref):
        u, _, _, _, mixed = _sgu_forward(zu_ref[...].astype(F32), zv_ref[...].astype(F32), lg_ref[...], lb_ref[...],
                                         ws_ref, bs_ref[...])
        o_ref[...] = (u * mixed).astype(BF16)

    return pl.pallas_call(
        body, name="sgu_fwd", out_shape=_sds((s, _DW), BF16), grid=(nchunk,),
        in_specs=[pl.BlockSpec((D_CHUNK, _DW), lambda n: (n, 1)), pl.BlockSpec((D_CHUNK, _DW), lambda n: (n, 2)),
                  _vec(_DW), _vec(_DW), pl.BlockSpec((D_GROUPS, D_CHUNK, D_CHUNK), lambda n: (0, 0, 0)),
                  pl.BlockSpec((D_CHUNK, LANES), lambda n: (0, 0))],
        out_specs=pl.BlockSpec((D_CHUNK, _DW), lambda n: (n, 0)),
        compiler_params=_params(("parallel",)),
    )(z, z, lg, lb, ws, bs_t)


def sgu_bwd(z, lg, lb, ws, bs_t, dycat, dy_col):
    s = z.shape[0]
    nchunk = s // D_CHUNK

    def body(zu_ref, zv_ref, lg_ref, lb_ref, ws_ref, bs_ref, dy_ref, dzu_ref, dzv_ref, dws_ref, dbs_ref, dlg_ref,
             dlb_ref):
        @pl.when(pl.program_id(0) == 0)
        def _():
            dws_ref[...] = jnp.zeros_like(dws_ref)
            dbs_ref[...] = jnp.zeros_like(dbs_ref)
            dlg_ref[...] = jnp.zeros_like(dlg_ref)
            dlb_ref[...] = jnp.zeros_like(dlb_ref)

        zu, zv, lg = zu_ref[...].astype(F32), zv_ref[...].astype(F32), lg_ref[...]
        u, xh, rstd, vln, mixed = _sgu_forward(zu, zv, lg, lb_ref[...], ws_ref, bs_ref[...])
        dy = dy_ref[...].astype(F32)
        dzu_ref[...] = (dy * mixed * _gelu_grad(zu)).astype(BF16)
        dmix = dy * u
        lane = lax.broadcasted_iota(jnp.int32, (D_CHUNK, LANES), 1)
        row = lax.broadcasted_iota(jnp.int32, (D_CHUNK, D_CHUNK), 0)
        colm = lax.broadcasted_iota(jnp.int32, (D_CHUNK, D_CHUNK), 1)
        dvln = []
        dbs = jnp.zeros((D_CHUNK, LANES), F32)
        for g in range(D_GROUPS):
            sl = slice(g * LANES, (g + 1) * LANES)
            dmg = dmix[:, sl]
            dbs = dbs + jnp.where(lane == g, jnp.sum(dmg, axis=-1, keepdims=True), 0.0)
            dmb = dmg.astype(BF16)
            dws_ref[g] += jnp.where(colm <= row, _dot(dmb, vln[:, sl], _NT), 0.0)
            dvln.append(_dot(_tril_bf16(ws_ref[g]), dmb, _TN))
        dbs_ref[...] += dbs
        dvln = jnp.concatenate(dvln, axis=1)
        dlg_ref[...] += jnp.sum(dvln * xh, axis=0, keepdims=True)
        dlb_ref[...] += jnp.sum(dvln, axis=0, keepdims=True)
        dxh = dvln * lg
        dvv = rstd * (dxh - jnp.mean(dxh, axis=-1, keepdims=True) - xh * jnp.mean(dxh * xh, axis=-1, keepdims=True))
        dzv_ref[...] = (dvv * _gelu_grad(zv)).astype(BF16)

    wsspec = pl.BlockSpec((D_GROUPS, D_CHUNK, D_CHUNK), lambda n: (0, 0, 0))
    chunk = lambda cidx: pl.BlockSpec((D_CHUNK, _DW), lambda n: (n, cidx))
    return pl.pallas_call(
        body, name="sgu_bwd",
        out_shape=(_sds((s, _DW), BF16), _sds((s, _DW), BF16), _sds((D_GROUPS, D_CHUNK, D_CHUNK)),
                   _sds((D_CHUNK, LANES)), _sds((1, _DW)), _sds((1, _DW))),
        grid=(nchunk,),
        in_specs=[chunk(1), chunk(2), _vec(_DW), _vec(_DW), wsspec, pl.BlockSpec((D_CHUNK, LANES), lambda n: (0, 0)),
                  chunk(dy_col)],
        out_specs=(chunk(0), chunk(0), wsspec, pl.BlockSpec((D_CHUNK, LANES), lambda n: (0, 0)), _vec(_DW), _vec(_DW)),
        compiler_params=_params(("arbitrary",)),
    )(z, z, lg, lb, ws, bs_t, dycat)


def ada_mod(c_all, ada_w, ada_b):
    nl, d, n = ada_w.shape
    nb = c_all.shape[0]
    tn = _tile(n, 512)

    def body(c_ref, w_ref, b_ref, o_ref):
        cv = c_ref[...]
        ca = (cv * _sigmoid(cv)).astype(BF16)
        o_ref[...] = _dot(ca, w_ref[...].astype(BF16)) + b_ref[...]

    return pl.pallas_call(
        body, name="ada_mod", out_shape=_sds((nl, nb, n)), grid=(nl, n // tn),
        in_specs=[pl.BlockSpec((nb, d), lambda l, j: (0, 0)), pl.BlockSpec((None, d, tn), lambda l, j: (l, 0, j)),
                  pl.BlockSpec((None, 1, tn), lambda l, j: (l, 0, j))],
        out_specs=pl.BlockSpec((None, nb, tn), lambda l, j: (l, 0, j)),
        compiler_params=_params(("parallel", "parallel")),
    )(c_all, ada_w, ada_b.reshape(nl, 1, n))


def ada_grad(c_all_t, dmod):
    d, nb = c_all_t.shape
    nl, _, n = dmod.shape
    tn = _tile(n, 512)
    tr = _tile(d, 256, 8)

    def body(c_ref, dm_ref, o_ref):
        cv = c_ref[...]
        ca = cv * _sigmoid(cv)
        dm = dm_ref[...]
        acc = ca[:, 0:1] * dm[0:1, :]
        for b in range(1, nb):
            acc = acc + ca[:, b:b + 1] * dm[b:b + 1, :]
        o_ref[...] = acc

    return pl.pallas_call(
        body, name="ada_grad", out_shape=_sds((nl, d, n)), grid=(nl, n // tn, d // tr),
        in_specs=[pl.BlockSpec((tr, nb), lambda l, j, r: (r, 0)), pl.BlockSpec((None, nb, tn), lambda l, j, r: (l, 0, j))],
        out_specs=pl.BlockSpec((None, tr, tn), lambda l, j, r: (l, r, j)),
        compiler_params=_params(("parallel", "parallel", "parallel")),
    )(c_all_t, dmod)


_ADAM_BLOCK = 256 * 1024


def _adam_rows(rows, cols):
    if rows * cols <= _ADAM_BLOCK or rows % 8:
        return rows
    return _tile(rows, max(8, _ADAM_BLOCK // cols), 8)


def _adam_update(w, gv, m, v):
    inv_bc1 = 1.0 / (1.0 - ADAM_B1 ** ADAM_STEP)
    inv_bc2 = 1.0 / (1.0 - ADAM_B2 ** ADAM_STEP)
    nm = ADAM_B1 * m + (1.0 - ADAM_B1) * gv
    nv = ADAM_B2 * v + (1.0 - ADAM_B2) * (gv * gv)
    return -ADAM_LR * ((nm * inv_bc1) / (jnp.sqrt(nv * inv_bc2) + ADAM_EPS) + ADAM_WD * w), nm, nv


def adamw(w, g, m, v, name):
    shape = w.shape
    cols = shape[-1]
    rows = w.size // cols
    tr = _adam_rows(rows, cols)

    def body(w_ref, g_ref, m_ref, v_ref, d_ref, nm_ref, nv_ref):
        d_ref[...], nm_ref[...], nv_ref[...] = _adam_update(w_ref[...], g_ref[...], m_ref[...], v_ref[...])

    spec = pl.BlockSpec((tr, cols), lambda i: (i, 0))
    out = _sds((rows, cols))
    r2 = lambda t: t.reshape(rows, cols)
    d, nm, nv = pl.pallas_call(
        body, name=name, out_shape=(out, out, out), grid=(rows // tr,),
        in_specs=[spec] * 4, out_specs=(spec,) * 3, compiler_params=_params(("parallel",)),
    )(r2(w), r2(g), r2(m), r2(v))
    return g.reshape(shape), d.reshape(shape), nm.reshape(shape), nv.reshape(shape)


def adamw_layers(w, g0, g1, m, v, name):
    _, rows, cols = w.shape
    tr = _adam_rows(rows, cols)

    def body(w_ref, g0_ref, g1_ref, m_ref, v_ref, g_ref, d_ref, nm_ref, nv_ref):
        gv = jnp.where(pl.program_id(0) == 0, g0_ref[...], g1_ref[...])
        g_ref[...] = gv
        d_ref[...], nm_ref[...], nv_ref[...] = _adam_update(w_ref[...], gv, m_ref[...], v_ref[...])

    spec = pl.BlockSpec((None, tr, cols), lambda l, i: (l, i, 0))
    gspec = pl.BlockSpec((tr, cols), lambda l, i: (i, 0))
    out = _sds((2, rows, cols))
    return pl.pallas_call(
        body, name=name, out_shape=(out, out, out, out), grid=(2, rows // tr),
        in_specs=[spec, gspec, gspec, spec, spec], out_specs=(spec,) * 4, compiler_params=_params(("parallel", "parallel")),
    )(w, g0, g1, m, v)


def sum8(gathered):
    _, r, _ = gathered.shape
    tr = _tile(r, 512, 8)

    def body(g_ref, o_ref):
        acc = g_ref[0]
        for dev in range(1, N_DEV):
            acc = acc + g_ref[dev]
        o_ref[...] = acc

    return pl.pallas_call(
        body, name="sum8", out_shape=_sds((r, LANES)), grid=(r // tr,),
        in_specs=[pl.BlockSpec((N_DEV, tr, LANES), lambda i: (0, i, 0))], out_specs=pl.BlockSpec((tr, LANES), lambda i: (i, 0)),
        compiler_params=_params(("parallel",)),
    )(gathered)


_SUM_BLOCK = 512 * 1024


def _sum_rows(rh, cols):
    return rh if rh * cols <= _SUM_BLOCK else _tile(rh, max(16, _SUM_BLOCK // cols), 16)


def pair_sum(g, recv, core, name):
    _, r, cols = g.shape
    rh = r // 2
    tr = _sum_rows(rh, cols)
    per = rh // tr

    def body(c_ref, a_ref, b_ref, o_ref):
        del c_ref
        o_ref[...] = (a_ref[...].astype(F32) + b_ref[...].astype(F32)).astype(BF16)

    grid_spec = pltpu.PrefetchScalarGridSpec(
        num_scalar_prefetch=1, grid=(N_CHIPS, per),
        in_specs=[pl.BlockSpec((None, tr, cols), lambda k, i, c: (k, c[0] * per + i, 0)),
                  pl.BlockSpec((None, tr, cols), lambda k, i, c: (k, i, 0))],
        out_specs=pl.BlockSpec((None, tr, cols), lambda k, i, c: (k, i, 0)))
    return pl.pallas_call(
        body, name=name, out_shape=_sds((N_CHIPS, rh, cols), BF16), grid_spec=grid_spec,
        compiler_params=_params(("parallel", "parallel")),
    )(core.reshape(1).astype(jnp.int32), g, recv)


def chip_sum(pair, recv, chip, core, name):
    _, rh, cols = pair.shape
    tr = _sum_rows(rh, cols)

    def body(p_ref, own_ref, r_ref, o_ref):
        del p_ref
        acc = own_ref[...].astype(F32)
        for j in range(N_CHIPS - 1):
            acc = acc + r_ref[j].astype(F32)
        o_ref[...] = acc

    grid_spec = pltpu.PrefetchScalarGridSpec(
        num_scalar_prefetch=1, grid=(rh // tr,),
        in_specs=[pl.BlockSpec((None, tr, cols), lambda i, p: (p[0], i, 0)),
                  pl.BlockSpec((N_CHIPS - 1, tr, cols), lambda i, p: (0, i, 0))],
        out_specs=pl.BlockSpec((None, tr, cols), lambda i, p: (p[1], i, 0)))
    return pl.pallas_call(
        body, name=name, out_shape=_sds((2, rh, cols)), grid_spec=grid_spec,
        compiler_params=_params(("parallel",)),
    )(jnp.stack([chip, core]).astype(jnp.int32), pair, recv)


def _place():
    return lax.axis_index("x"), lax.axis_index("y"), lax.axis_index("c")


def _other_chips(x, y):
    return [(x, 1 - y), (1 - x, y), (1 - x, 1 - y)]


_HBM = pl.BlockSpec(memory_space=pltpu.HBM)


def all_gather8(v, name):
    m, n = v.shape

    def body(x_ref, out_ref, send_sems, recv_sems, local_sem):
        x, y, c = _place()
        me, sibling = (x, y, c), (x, y, 1 - c)
        chips = _other_chips(x, y)

        def rows(px, py, pc):
            return out_ref.at[pl.ds((4 * px + 2 * py + pc) * m, m), :]

        def copy(k, block, to, src=None):
            return pltpu.make_async_remote_copy(
                src_ref=rows(*block) if src is None else src, dst_ref=rows(*block),
                send_sem=send_sems.at[k], recv_sem=recv_sems.at[k], device_id=to, device_id_type=MESH)

        mine = pltpu.make_async_copy(x_ref, rows(*me), local_sem)
        mine.start()
        first = [copy(0, me, sibling, src=x_ref)]
        first += [copy(1 + j, me, (*chip, c), src=x_ref) for j, chip in enumerate(chips)]
        for cp in first:
            cp.start()
        passed = [copy(4 + j, (*chip, c), sibling) for j, chip in enumerate(chips)]
        for j, chip in enumerate(chips):
            copy(1 + j, (*chip, c), me).wait_recv()
            passed[j].start()
        copy(0, sibling, me).wait_recv()
        for j, chip in enumerate(chips):
            copy(4 + j, (*chip, 1 - c), me).wait_recv()
        for cp in first + passed:
            cp.wait_send()
        mine.wait()

    return pl.pallas_call(
        body, name=name, out_shape=_sds((N_DEV * m, n), v.dtype),
        in_specs=[pl.BlockSpec(memory_space=pltpu.VMEM)], out_specs=pl.BlockSpec(memory_space=pltpu.VMEM),
        scratch_shapes=[pltpu.SemaphoreType.DMA((7,)), pltpu.SemaphoreType.DMA((7,)), pltpu.SemaphoreType.DMA],
        compiler_params=_params(None, VMEM_BIG),
    )(v)


def _comm_call(body, name, ins, out_shapes, nsem, aliases=None):
    return pl.pallas_call(
        body, name=name, out_shape=tuple(out_shapes), in_specs=[_HBM] * len(ins), out_specs=tuple([_HBM] * len(out_shapes)),
        scratch_shapes=[pltpu.SemaphoreType.DMA((nsem,)), pltpu.SemaphoreType.DMA((nsem,))],
        input_output_aliases=aliases or {},
    )(*ins)


def _remote(src, dst, send_sems, recv_sems, k, to):
    return pltpu.make_async_remote_copy(src_ref=src, dst_ref=dst, send_sem=send_sems.at[k], recv_sem=recv_sems.at[k],
                                        device_id=to, device_id_type=MESH)


def _half(core, rh):
    return pl.ds(pl.multiple_of(core * rh, 16), rh)


def swap_halves(gs, name):
    n = len(gs)

    def body(*refs):
        ins, outs, (send_sems, recv_sems) = refs[:n], refs[n:2 * n], refs[2 * n:]
        x, y, c = _place()
        copies = []
        for i in range(n):
            theirs = _half(1 - c, ins[i].shape[1] // 2)
            cp = _remote(ins[i].at[:, theirs], outs[i], send_sems, recv_sems, i, (x, y, 1 - c))
            cp.start()
            copies.append(cp)
        for cp in copies:
            cp.wait()

    return _comm_call(body, name, gs, [_sds((g.shape[0], g.shape[1] // 2, g.shape[2]), g.dtype) for g in gs], n)


def join_halves(bufs, name):
    n = len(bufs)

    def body(*refs):
        ins, outs, (send_sems, recv_sems) = refs[:n], refs[n:2 * n], refs[2 * n:]
        x, y, c = _place()
        copies = []
        for i in range(n):
            cp = _remote(ins[i].at[c], outs[i].at[c], send_sems, recv_sems, i, (x, y, 1 - c))
            cp.start()
            copies.append(cp)
        for i in range(n):
            theirs = outs[i].at[1 - c]
            _remote(theirs, theirs, send_sems, recv_sems, i, (x, y, 1 - c)).wait_recv()
        for cp in copies:
            cp.wait_send()

    return _comm_call(body, name, bufs, [_sds(b.shape, b.dtype) for b in bufs], n, {i: i for i in range(n)})


def forward_halves(lands, name):
    n = len(lands)

    def body(*refs):
        ins, outs, (send_sems, recv_sems) = refs[:n], refs[n:2 * n], refs[2 * n:]
        x, y, c = _place()
        sibling = (x, y, 1 - c)
        chips = _other_chips(x, y)
        copies = []
        for i in range(n):
            mine = _half(c, ins[i].shape[1] // 2)
            for j, (px, py) in enumerate(chips):
                cp = _remote(ins[i].at[2 * px + py, mine], outs[i].at[2 * px + py, mine], send_sems, recv_sems, 3 * i + j, sibling)
                cp.start()
                copies.append(cp)
        for i in range(n):
            theirs = _half(1 - c, ins[i].shape[1] // 2)
            for j, (px, py) in enumerate(chips):
                landed = outs[i].at[2 * px + py, theirs]
                _remote(landed, landed, send_sems, recv_sems, 3 * i + j, sibling).wait_recv()
        for cp in copies:
            cp.wait_send()

    return _comm_call(body, name, lands, [_sds(b.shape, b.dtype) for b in lands], 3 * n, {i: i for i in range(n)})


_SEM = pl.BlockSpec(memory_space=pltpu.SEMAPHORE)
_EFFECT = pltpu.SideEffectType.DATAFLOW_SIDE_EFFECTING


def _gather_copies(srcs, lands, send_sems, recv_sems):
    x, y, c = _place()
    copies = []
    for i in range(len(srcs)):
        mine = _half(c, srcs[i].shape[0] // 2)
        for j, chip in enumerate(_other_chips(x, y)):
            copies.append(_remote(srcs[i].at[mine], lands[i].at[2 * x + y, mine], send_sems, recv_sems, 3 * i + j, (*chip, c)))
    return copies


def _exchange_copies(srcs, lands, send_sems, recv_sems):
    x, y, c = _place()
    copies = []
    for i in range(len(srcs)):
        for j, (px, py) in enumerate(_other_chips(x, y)):
            copies.append(_remote(srcs[i].at[2 * px + py], lands[i].at[j], send_sems, recv_sems, 3 * i + j, (px, py, c)))
    return copies


def split_start(name, copies_fn, srcs, land_shapes):
    n, m = len(srcs), len(land_shapes)
    ncopies = 3 * n

    def body(*refs):
        src_refs, land_refs = refs[:n], refs[n:n + m]
        send_sems, recv_sems = refs[n + m], refs[n + m + 1]
        token = refs[-1]
        for cp in copies_fn(src_refs, land_refs, send_sems, recv_sems):
            cp.start()
        token[...] = jnp.zeros_like(token)

    hbm = lambda s: pltpu.HBM(tuple(s.shape), s.dtype)
    outs = pl.pallas_call(
        body, name=name,
        out_shape=(pltpu.SemaphoreType.DMA((ncopies,)), pltpu.SemaphoreType.DMA((ncopies,)), *[hbm(s) for s in srcs],
                   *[hbm(s) for s in land_shapes], _sds((8, LANES))),
        in_specs=[_HBM] * (n + m),
        out_specs=(_SEM, _SEM, *([_HBM] * (n + m)), pl.BlockSpec(memory_space=pltpu.VMEM)),
        input_output_aliases={i: 2 + i for i in range(n + m)},
        compiler_params=pltpu.CompilerParams(has_side_effects=_EFFECT),
    )(*[pltpu.with_memory_space_constraint(s, pltpu.HBM) for s in srcs],
      *[pltpu.with_memory_space_constraint(lax.empty(tuple(s.shape), s.dtype), pltpu.HBM) for s in land_shapes])
    handle = (outs[0], outs[1], list(outs[2:2 + n]), list(outs[2 + n:2 + n + m]))
    return handle, outs[-1][0, 0]


def split_wait(name, copies_fn, handle, after):
    send_sems, recv_sems, srcs, lands = handle
    n, m = len(srcs), len(lands)

    def body(*refs):
        src_refs, land_refs = refs[:n], refs[n:n + m]
        for cp in copies_fn(src_refs, land_refs, refs[n + m], refs[n + m + 1]):
            cp.wait_send()
            cp.wait_recv()

    hbm = lambda s: pltpu.HBM(tuple(s.shape), s.dtype)
    outs = pl.pallas_call(
        body, name=name, out_shape=tuple(hbm(s) for s in srcs + lands),
        in_specs=[_HBM] * (n + m) + [_SEM, _SEM, pl.BlockSpec(memory_space=pl.ANY)], out_specs=tuple([_HBM] * (n + m)),
        input_output_aliases={i: i for i in range(n + m)},
        compiler_params=pltpu.CompilerParams(has_side_effects=_EFFECT),
    )(*srcs, *lands, send_sems, recv_sems, after)
    return list(outs[:n]), list(outs[n:])


_CD_PAD = C_Q_RANK + C_KV_RANK + HEAD_PAD + 2 * _DW


def chip_major(w, groups=N_CHIPS):
    r, c = w.shape
    return w.reshape(r, groups, c // groups).transpose(1, 0, 2)


def from_chip_major(w):
    g, r, c = w.shape
    return w.transpose(1, 0, 2).reshape(r, g * c)


def _cd_in_pad(w):
    a = C_Q_RANK + C_KV_RANK
    z = lambda n: jnp.zeros((w.shape[0], n), w.dtype)
    return jnp.concatenate([w[:, :a], z(C_NOPE), w[:, a:a + C_ROPE], z(HEAD_PAD - C_NOPE - C_ROPE), w[:, a + C_ROPE:]], axis=1)


def _cd_in_unpad(w):
    a = C_Q_RANK + C_KV_RANK
    return jnp.concatenate([w[:, :a], w[:, a + C_NOPE:a + C_NOPE + C_ROPE], w[:, a + HEAD_PAD:]], axis=1)


def _pad_heads(w, width):
    r = w.shape[0]
    w = w.reshape(r, C_HEADS, width)
    return jnp.pad(w, ((0, 0), (0, 0), (0, HEAD_PAD - width))).reshape(r, _HW)


def _unpad_heads(w, width):
    r = w.shape[0]
    return w.reshape(r, C_HEADS, HEAD_PAD)[:, :, :width].reshape(r, C_HEADS * width)


_MATMUL_WEIGHTS = ("ab_w_in", "ab_w_out", "cd_w_in", "c_w_uq", "c_w_ukv", "cd_w_out", "ffn_w_up", "ffn_w_down")
_LAYER_STACKED = ("norm1_g", "norm2_g", "ffn_w_up", "ffn_conv_w", "ffn_w_down")
_ROW_VECTORS = ("b_scale", "c_q_norm_g", "c_kv_norm_g", "d_ln_g", "d_ln_b")


def full_to_local(p):
    q = {}
    for k, v in p.items():
        if k == "final_norm_g":
            v = v.reshape(1, -1)
        elif k not in _LAYER_STACKED and k not in _ROW_VECTORS:
            v = v[0]
        q[k] = v.astype(BF16) if k in _MATMUL_WEIGHTS else v
    return q


def local_to_full(g):
    q = {}
    for k, v in g.items():
        if k == "final_norm_g":
            q[k] = v.reshape(-1)
        elif k not in _LAYER_STACKED and k not in _ROW_VECTORS:
            q[k] = v[None]
        else:
            q[k] = v
    return q


def prepare_weights(p):
    q = dict(p)
    q["cd_w_in"] = _cd_in_pad(p["cd_w_in"])
    q["c_w_uq"] = _pad_heads(p["c_w_uq"], C_NOPE + C_ROPE)
    ukv = p["c_w_ukv"].reshape(C_KV_RANK, C_HEADS, C_NOPE + C_V)
    q["c_w_uk"] = _pad_heads(ukv[:, :, :C_NOPE].reshape(C_KV_RANK, -1), C_NOPE)
    q["c_w_uv"] = _pad_heads(ukv[:, :, C_NOPE:].reshape(C_KV_RANK, -1), C_V)
    wo = p["cd_w_out"]
    att_rows = jnp.pad(wo[:C_HEADS * C_V].reshape(C_HEADS, C_V, D_MODEL), ((0, 0), (0, HEAD_PAD - C_V), (0, 0)))
    q["cd_w_out"] = jnp.concatenate([att_rows.reshape(_HW, D_MODEL), wo[C_HEADS * C_V:]], axis=0)
    return q


def unprepare_grads(g):
    q = dict(g)
    q["cd_w_in"] = _cd_in_unpad(g["cd_w_in"])
    q["c_w_uq"] = _unpad_heads(g["c_w_uq"], C_NOPE + C_ROPE)
    uk = g.pop("c_w_uk").reshape(C_KV_RANK, C_HEADS, HEAD_PAD)[:, :, :C_NOPE]
    uv = g.pop("c_w_uv").reshape(C_KV_RANK, C_HEADS, HEAD_PAD)[:, :, :C_V]
    q.pop("c_w_uk", None)
    q.pop("c_w_uv", None)
    q["c_w_ukv"] = jnp.concatenate([uk, uv], axis=-1).reshape(C_KV_RANK, C_HEADS * (C_NOPE + C_V))
    wo = g["cd_w_out"]
    att = wo[:_HW].reshape(C_HEADS, HEAD_PAD, D_MODEL)[:, :C_V].reshape(C_HEADS * C_V, D_MODEL)
    q["cd_w_out"] = jnp.concatenate([att, wo[_HW:]], axis=0)
    return q


def rope_tables(positions):
    half = C_ROPE // 2
    inv_freq = ROPE_THETA ** (-jnp.arange(half, dtype=F32) / half)
    ang = positions.astype(F32)[:, None] * inv_freq
    cos, sin = jnp.cos(ang), jnp.sin(ang)
    s = positions.shape[0]
    z = lambda n: jnp.zeros((s, n), F32)
    cs = jnp.concatenate([jnp.ones((s, C_NOPE), F32), cos, cos, z(HEAD_PAD - C_NOPE - C_ROPE)], axis=1)
    s1 = jnp.concatenate([z(C_NOPE), -sin, z(HEAD_PAD - C_NOPE - half)], axis=1)
    s2 = jnp.concatenate([z(C_NOPE + half), sin, z(HEAD_PAD - C_NOPE - C_ROPE)], axis=1)
    return cs, s1, s2


def _mods(mod_l):
    return [mod_l[:, i * D_MODEL:(i + 1) * D_MODEL] for i in range(N_MOD)]


def _ffn_fwd(x1, w, l, sc2, sh2, g2):
    n2 = w["norm2_g"][l:l + 1]
    h2 = modnorm_fwd(x1, n2, sc2, sh2, f"modnorm2_fwd{l}")
    up_cols = 2 * D_FF // N_CHIPS
    zf = matmul(h2, w["ffn_w_up"][l], "nn", BF16, f"ffn_up{l}", gb=N_CHIPS, go=2, tn=up_cols)
    a = ffn_act_fwd(zf, w["ffn_conv_w"][l], f"ffn_act_fwd{l}")
    f = matmul(a, w["ffn_w_down"][l], "nn", F32, f"ffn_down{l}", tk=D_FF)
    x2 = resid_fwd(x1, f, g2, f"resid2_fwd{l}")
    return x2, (h2, zf, a, f)


def _ffn_bwd(dres, x1, saved, w, l, sc2, g2):
    h2, zf, a, f = saved
    n2 = w["norm2_g"][l:l + 1]
    df, dg2 = gate_bwd(dres, f, g2, f"gate2_bwd{l}")
    up_cols = 2 * D_FF // N_CHIPS
    da = matmul(df, w["ffn_w_down"][l], "nt", BF16, f"ffn_down_dx{l}")
    d_down = matmul(a, df, "tn", BF16, f"ffn_down_dw{l}", tm=D_FF // 2)
    dzf, d_conv = ffn_act_bwd(zf, w["ffn_conv_w"][l], da, f"ffn_act_bwd{l}")
    dh2 = matmul(dzf, w["ffn_w_up"][l], "nt", F32, f"ffn_up_dx{l}", ga=2, gb=N_CHIPS, tk=up_cols)
    d_up = matmul(h2, dzf, "tn", BF16, f"ffn_up_dw{l}", gb=2, go=N_CHIPS, tn=up_cols)
    dres, dsh2, dsc2, dn2 = norm_bwd(x1, dh2, n2, sc2, dres, f"norm2_bwd{l}")
    d_conv = d_conv.transpose(1, 0, 2).reshape(3, 2 * D_FF)
    return dres, dict(ffn_w_down=d_down, ffn_conv_w=d_conv, ffn_w_up=d_up, norm2_g=dn2), (dsh2, dsc2, dg2)


def mixer0_fwd(x0, sh1, sc1, g1, w):
    h = modnorm_fwd(x0, w["norm1_g"][0:1], sc1, sh1, "modnorm1_fwd0")
    z = matmul(h, w["ab_w_in"], "nn", BF16, "ab_in", gb=N_CHIPS)
    ya = gconv_fwd(z, w["a_conv_w"])
    yb = pool_fwd(z, w["b_mix_w"], w["b_scale"])
    ycat = jnp.concatenate([ya] + yb, axis=1)
    y = matmul(ycat, w["ab_w_out"], "nn", F32, "ab_out")
    x1 = resid_fwd(x0, y, g1, "resid1_fwd0")
    return x1, (x0, h, z, ycat, y, sc1, g1)


def mixer0_bwd(dres, saved, w):
    x0, h, z, ycat, y, sc1, g1 = saved
    grads = {}
    dy, dg1 = gate_bwd(dres, y, g1, "gate1_bwd0")
    dycat = matmul(dy, w["ab_w_out"], "nt", BF16, "ab_out_dx")
    grads["ab_w_out"] = matmul(ycat, dy, "tn", BF16, "ab_out_dw")
    db, dc, da, d_conv = gconv_bwd(z, w["a_conv_w"], dycat)
    pb = pool_bwd(z, w["b_mix_w"], w["b_scale"], dycat)
    dz = jnp.concatenate([db, dc, da] + [t[0] for t in pb], axis=1)
    dh = matmul(dz, w["ab_w_in"], "nt", F32, "ab_in_dx", gb=N_CHIPS)
    grads["ab_w_in"] = matmul(h, dz, "tn", BF16, "ab_in_dw", go=N_CHIPS)
    dres, dsh1, dsc1, dn1 = norm_bwd(x0, dh, w["norm1_g"][0:1], sc1, dres, "norm1_bwd0")
    grads.update(a_conv_w=d_conv, b_mix_w=jnp.stack([t[1] for t in pb]),
                 b_scale=jnp.concatenate([t[2] for t in pb], axis=1), norm1_g=dn1)
    return dres, grads, (dsh1, dsc1, dg1)


def mixer1_fwd(x0, sh1, sc1, g1, ropes, w):
    cs, s1, s2 = ropes
    h = modnorm_fwd(x0, w["norm1_g"][1:2], sc1, sh1, "modnorm1_fwd1")
    z = matmul(h, w["cd_w_in"], "nn", BF16, "cd_in")
    bs_t = jnp.pad(w["d_b_s"].T, ((0, 0), (0, LANES - D_GROUPS)))
    qh, kh, vh = mla_pre_fwd(z, w["c_q_norm_g"], w["c_kv_norm_g"], w["c_w_uq"], w["c_w_uk"], w["c_w_uv"], cs, s1, s2)
    oh = attn_fwd(qh, kh, vh)
    yd = sgu_fwd(z, w["d_ln_g"], w["d_ln_b"], w["d_w_s"], bs_t)
    ycat = jnp.concatenate([oh, yd], axis=1)
    y = matmul(ycat, w["cd_w_out"], "nn", F32, "cd_out")
    x1 = resid_fwd(x0, y, g1, "resid1_fwd1")
    return x1, (x0, h, z, bs_t, qh, kh, vh, oh, ycat, y, sc1, g1)


def mixer1_bwd(dres, saved, ropes, w):
    cs, s1, s2 = ropes
    x0, h, z, bs_t, qh, kh, vh, oh, ycat, y, sc1, g1 = saved
    grads = {}
    dy, dg1 = gate_bwd(dres, y, g1, "gate1_bwd1")
    dycat = matmul(dy, w["cd_w_out"], "nt", BF16, "cd_out_dx")
    grads["cd_w_out"] = matmul(ycat, dy, "tn", F32, "cd_out_dw")
    dqh, dkh, dvh = attn_bwd(qh, kh, vh, oh, dycat, 0)
    dzq, d_uq, d_uk, d_uv, d_gq, d_gkv = mla_pre_bwd(
        z, w["c_q_norm_g"], w["c_kv_norm_g"], w["c_w_uq"], w["c_w_uk"], w["c_w_uv"], cs, s1, s2, dqh, dkh, dvh)
    dzu, dzv, d_ws, d_bs, d_lg, d_lb = sgu_bwd(z, w["d_ln_g"], w["d_ln_b"], w["d_w_s"], bs_t, dycat, _HW // _DW)
    dz = jnp.concatenate([dzq, dzu, dzv], axis=1)
    dh = matmul(dz, w["cd_w_in"], "nt", F32, "cd_in_dx")
    grads["cd_w_in"] = matmul(h, dz, "tn", F32, "cd_in_dw")
    dres, dsh1, dsc1, dn1 = norm_bwd(x0, dh, w["norm1_g"][1:2], sc1, dres, "norm1_bwd1")
    grads.update(c_w_uq=d_uq, c_w_uk=d_uk, c_w_uv=d_uv, c_q_norm_g=d_gq, c_kv_norm_g=d_gkv, d_w_s=d_ws,
                 d_b_s=d_bs[:, :D_GROUPS].T, d_ln_g=d_lg, d_ln_b=d_lb, norm1_g=dn1)
    return dres, grads, (dsh1, dsc1, dg1)


_PER_LAYER = ("ffn_w_down", "ffn_conv_w", "ffn_w_up", "norm2_g", "norm1_g")


def _merge_layer_grads(g0, g1):
    grads = {k: v for k, v in g0.items() if k not in _PER_LAYER}
    grads.update({k: v for k, v in g1.items() if k not in _PER_LAYER})
    for k in ("ffn_w_down", "ffn_w_up"):
        grads[k] = [g0[k], g1[k]]
    grads["ffn_conv_w"] = jnp.stack([g0["ffn_conv_w"], g1["ffn_conv_w"]])
    grads["norm1_g"] = jnp.concatenate([g0["norm1_g"], g1["norm1_g"]], axis=0)
    grads["norm2_g"] = jnp.concatenate([g0["norm2_g"], g1["norm2_g"]], axis=0)
    return grads


def local_step(x, tgt, mod, ropes, w):
    sh1a, sc1a, g1a, sh2a, sc2a, g2a = _mods(mod[0:1])
    sh1b, sc1b, g1b, sh2b, sc2b, g2b = _mods(mod[1:2])
    x1, mix0 = mixer0_fwd(x, sh1a, sc1a, g1a, w)
    x2, ffn0 = _ffn_fwd(x1, w, 0, sc2a, sh2a, g2a)
    x3, mix1 = mixer1_fwd(x2, sh1b, sc1b, g1b, ropes, w)
    x4, ffn1 = _ffn_fwd(x3, w, 1, sc2b, sh2b, g2b)
    dres, d_final, loss = final_fwd_bwd(x4, w["final_norm_g"], tgt)
    dres, gf1, dm2b = _ffn_bwd(dres, x3, ffn1, w, 1, sc2b, g2b)
    dres, gm1, dm1b = mixer1_bwd(dres, mix1, ropes, w)
    dres, gf0, dm2a = _ffn_bwd(dres, x1, ffn0, w, 0, sc2a, g2a)
    dres, gm0, dm1a = mixer0_bwd(dres, mix0, w)
    grads = _merge_layer_grads({**gf0, **gm0}, {**gf1, **gm1})
    grads["final_norm_g"] = d_final
    dmod = jnp.concatenate([jnp.concatenate(dm1a + dm2a, axis=1), jnp.concatenate(dm1b + dm2b, axis=1)], axis=0)
    return loss, dres, dmod, grads


_WEIGHTS = ("ada_w", "ada_b", "norm1_g", "norm2_g", "ab_w_in", "a_conv_w", "b_mix_w", "b_scale", "ab_w_out", "cd_w_in",
            "c_q_norm_g", "c_w_uq", "c_kv_norm_g", "c_w_ukv", "d_ln_g", "d_ln_b", "d_w_s", "d_b_s", "cd_w_out",
            "ffn_w_up", "ffn_conv_w", "ffn_w_down", "final_norm_g")
_INPUTS = ("x", "c", "positions") + _WEIGHTS + ("loss_target",) + tuple("m_" + n for n in _WEIGHTS) + tuple(
    "v_" + n for n in _WEIGHTS)

def _pack_rows(parts, rows, dtype):
    flat = jnp.concatenate([p.reshape(-1).astype(dtype) for p in parts])
    return jnp.pad(flat, (0, rows * LANES - flat.shape[0])).reshape(rows, LANES)


def _rows_major(w):
    r, c = w.shape
    return w.reshape(N_CHIPS, r // N_CHIPS, c)


def order_after(v, name):
    def body(v_ref, o_ref):
        del v_ref
        o_ref[...] = jnp.zeros_like(o_ref)

    return pl.pallas_call(
        body, name=name, out_shape=_sds((8, LANES)), in_specs=[pl.BlockSpec(memory_space=pl.ANY)],
        out_specs=pl.BlockSpec(memory_space=pltpu.VMEM),
    )(v)[0, 0]


def start_gather(shards, tag):
    lands = [_sds((N_CHIPS,) + s.shape, s.dtype) for s in shards]
    return split_start("gather_start_" + tag, _gather_copies, shards, lands)


def finish_gather(handle, chip, tag, after):
    shards, lands = split_wait("gather_wait_" + tag, _gather_copies, handle, after)
    lands = forward_halves(lands, "gather_forward_" + tag)
    return [lax.dynamic_update_index_in_dim(o, s, chip, 0) for o, s in zip(lands, shards)]


def start_reduce(gs, core, tag):
    recv = swap_halves(gs, "swap_halves_" + tag)
    pairs = [pair_sum(g, r, core, f"pair_sum_{tag}{i}") for i, (g, r) in enumerate(zip(gs, recv))]
    lands = [_sds((N_CHIPS - 1,) + p.shape[1:], p.dtype) for p in pairs]
    return split_start("exchange_start_" + tag, _exchange_copies, pairs, lands)


def finish_reduce(handle, chip, core, tag, after):
    pairs, others = split_wait("exchange_wait_" + tag, _exchange_copies, handle, after)
    halves = [chip_sum(p, o, chip, core, f"chip_sum_{tag}{i}") for i, (p, o) in enumerate(zip(pairs, others))]
    full = join_halves(halves, "join_halves_" + tag)
    return [f.reshape(f.shape[1] * 2, f.shape[2]) for f in full]


_SMALL_SHARDED = (("a_conv_w", (3, 128), 1), ("c_q_norm_g", (1, 64), 1), ("d_ln_g", (1, 128), 1), ("d_ln_b", (1, 128), 1),
                  ("ffn_conv_w", (2, 3, 2 * D_FF // N_CHIPS), 2))
_SMALL_GRADS = (("norm1_g", (2, D_MODEL)), ("norm2_g", (2, D_MODEL)), ("b_mix_w", (4, 128, 128)), ("b_scale", (1, 512)),
                ("c_kv_norm_g", (1, 128)), ("d_w_s", (4, 128, 128)), ("d_b_s", (4, 128)), ("final_norm_g", (1, D_MODEL)),
                ("a_conv_w", (3, 512)), ("c_q_norm_g", (1, 256)), ("d_ln_g", (1, 512)), ("d_ln_b", (1, 512)),
                ("ffn_conv_w", (2, 3, 2 * D_FF)))


def _size(shape):
    n = 1
    for d in shape:
        n *= d
    return n


def kernel(x, c, positions, ada_w, ada_b, norm1_g, norm2_g, ab_w_in, a_conv_w, b_mix_w, b_scale, ab_w_out, cd_w_in, c_q_norm_g, c_w_uq, c_kv_norm_g, c_w_ukv, d_ln_g, d_ln_b, d_w_s, d_b_s, cd_w_out, ffn_w_up, ffn_conv_w, ffn_w_down, final_norm_g, loss_target, m_ada_w, m_ada_b, m_norm1_g, m_norm2_g, m_ab_w_in, m_a_conv_w, m_b_mix_w, m_b_scale, m_ab_w_out, m_cd_w_in, m_c_q_norm_g, m_c_w_uq, m_c_kv_norm_g, m_c_w_ukv, m_d_ln_g, m_d_ln_b, m_d_w_s, m_d_b_s, m_cd_w_out, m_ffn_w_up, m_ffn_conv_w, m_ffn_w_down, m_final_norm_g, v_ada_w, v_ada_b, v_norm1_g, v_norm2_g, v_ab_w_in, v_a_conv_w, v_b_mix_w, v_b_scale, v_ab_w_out, v_cd_w_in, v_c_q_norm_g, v_c_w_uq, v_c_kv_norm_g, v_c_w_ukv, v_d_ln_g, v_d_ln_b, v_d_w_s, v_d_b_s, v_cd_w_out, v_ffn_w_up, v_ffn_conv_w, v_ffn_w_down, v_final_norm_g):
    args = (x, c, positions, ada_w, ada_b, norm1_g, norm2_g, ab_w_in, a_conv_w, b_mix_w, b_scale, ab_w_out, cd_w_in, c_q_norm_g, c_w_uq, c_kv_norm_g, c_w_ukv, d_ln_g, d_ln_b, d_w_s, d_b_s, cd_w_out, ffn_w_up, ffn_conv_w, ffn_w_down, final_norm_g, loss_target, m_ada_w, m_ada_b, m_norm1_g, m_norm2_g, m_ab_w_in, m_a_conv_w, m_b_mix_w, m_b_scale, m_ab_w_out, m_cd_w_in, m_c_q_norm_g, m_c_w_uq, m_c_kv_norm_g, m_c_w_ukv, m_d_ln_g, m_d_ln_b, m_d_w_s, m_d_b_s, m_cd_w_out, m_ffn_w_up, m_ffn_conv_w, m_ffn_w_down, m_final_norm_g, v_ada_w, v_ada_b, v_norm1_g, v_norm2_g, v_ab_w_in, v_a_conv_w, v_b_mix_w, v_b_scale, v_ab_w_out, v_cd_w_in, v_c_q_norm_g, v_c_w_uq, v_c_kv_norm_g, v_c_w_ukv, v_d_ln_g, v_d_ln_b, v_d_w_s, v_d_b_s, v_cd_w_out, v_ffn_w_up, v_ffn_conv_w, v_ffn_w_down, v_final_norm_g)
    a = dict(zip(_INPUTS, args, strict=True))
    xi, yi, ci = _place()
    chip = 2 * xi + yi
    dev = 4 * xi + 2 * yi + ci
    x = a["x"][0]
    tgt = a["loss_target"][0]

    mix0_handle, tok = start_gather([a["ab_w_in"][0].astype(BF16), a["ab_w_out"][0].astype(BF16)], "mix0")

    small_parts = [a["c"] + tok] + [a[n] for n, _, _ in _SMALL_SHARDED]
    rows1 = -(-sum(p.size for p in small_parts) // LANES // 8) * 8
    g1 = all_gather8(_pack_rows(small_parts, rows1, F32), "gather_small").reshape(N_DEV, rows1 * LANES)
    c_all = g1[:, :D_MODEL]
    per_chip = g1[0::2]
    small_full = {}
    off = D_MODEL
    for n, shp, axis in _SMALL_SHARDED:
        piece = per_chip[:, off:off + _size(shp)].reshape((N_CHIPS,) + shp)
        small_full[n] = jnp.concatenate([piece[k] for k in range(N_CHIPS)], axis=axis)
        off += _size(shp)

    merge = lambda t: t.reshape(t.shape[0] * t.shape[1], t.shape[2])
    w = dict(norm1_g=a["norm1_g"], norm2_g=a["norm2_g"], b_mix_w=a["b_mix_w"][0], b_scale=a["b_scale"],
             c_kv_norm_g=a["c_kv_norm_g"], d_w_s=a["d_w_s"][0], d_b_s=a["d_b_s"][0],
             final_norm_g=a["final_norm_g"].reshape(1, D_MODEL), **small_full)

    ncol = N_MOD * D_MODEL // N_CHIPS
    ada_b_mine = lax.dynamic_slice_in_dim(a["ada_b"], chip * ncol, ncol, axis=1)
    mod_cols = ada_mod(c_all, a["ada_w"], ada_b_mine)
    g2 = all_gather8(mod_cols.reshape(-1, LANES), "gather_mod").reshape(N_DEV, 2, N_DEV, ncol)
    mod = lax.dynamic_index_in_dim(g2[0::2], dev, axis=2, keepdims=False)
    mod = mod.transpose(1, 0, 2).reshape(2, N_MOD * D_MODEL)

    late = order_after(mod, "after_mod")
    bf_late = lambda t: (t + late).astype(BF16)
    ffn0_handle, tok_b = start_gather([bf_late(a["ffn_w_up"][0]), bf_late(a["ffn_w_down"][0])], "ffn0")
    lay1_handle, tok_c = start_gather(
        [bf_late(a["cd_w_in"][0]), bf_late(a["c_w_uq"][0]), bf_late(a["c_w_ukv"][0]), bf_late(a["cd_w_out"][0]),
         bf_late(a["ffn_w_up"][1]), bf_late(a["ffn_w_down"][1])], "lay1")
    mod = mod + (tok_b + tok_c)

    ropes = rope_tables(a["positions"][0])
    cm16 = lambda t: chip_major(t).astype(BF16)
    sh1a, sc1a, g1a, sh2a, sc2a, g2a = _mods(mod[0:1])
    sh1b, sc1b, g1b, sh2b, sc2b, g2b = _mods(mod[1:2])

    w_in0, w_out0 = finish_gather(mix0_handle, chip, "mix0", mod)
    w.update(ab_w_in=w_in0, ab_w_out=merge(w_out0))
    x1, mix0 = mixer0_fwd(x, sh1a, sc1a, g1a, w)
    up0, down0 = finish_gather(ffn0_handle, chip, "ffn0", x1)
    w.update(ffn_w_up=[up0, None], ffn_w_down=[merge(down0), None])
    x2, ffn0 = _ffn_fwd(x1, w, 0, sc2a, sh2a, g2a)
    cd_in, uq, ukv, cd_out, up1, down1 = finish_gather(lay1_handle, chip, "lay1", x2)
    w.update(prepare_weights(dict(cd_w_in=from_chip_major(cd_in), c_w_uq=from_chip_major(uq), c_w_ukv=from_chip_major(ukv),
                                  cd_w_out=merge(cd_out))))
    w.update(ffn_w_up=[up0, up1], ffn_w_down=[merge(down0), merge(down1)])
    x3, mix1 = mixer1_fwd(x2, sh1b, sc1b, g1b, ropes, w)
    x4, ffn1 = _ffn_fwd(x3, w, 1, sc2b, sh2b, g2b)
    dres, d_final, loss = final_fwd_bwd(x4, w["final_norm_g"], tgt)
    loss = lax.psum(loss[0, 0], ("x", "y", "c"))

    dres, gf1, dm2b = _ffn_bwd(dres, x3, ffn1, w, 1, sc2b, g2b)
    ffn1_red, tok = start_reduce([gf1["ffn_w_up"], _rows_major(gf1["ffn_w_down"])], ci, "ffn1")
    dres, gm1, dm1b = mixer1_bwd(dres, mix1[:-1] + (mix1[-1] + tok,), ropes, w)
    gm1 = unprepare_grads(gm1)
    mix1_red, tok = start_reduce([cm16(gm1["cd_w_in"]), cm16(gm1["c_w_uq"]), cm16(gm1["c_w_ukv"]),
                                  _rows_major(gm1["cd_w_out"]).astype(BF16)], ci, "mix1")
    red_up1, red_down1 = finish_reduce(ffn1_red, chip, ci, "ffn1", dres)
    dres, gf0, dm2a = _ffn_bwd(dres, x1, ffn0, w, 0, sc2a, g2a + tok)
    ffn0_red, tok = start_reduce([gf0["ffn_w_up"], _rows_major(gf0["ffn_w_down"])], ci, "ffn0")
    red_cd_in, red_uq, red_ukv, red_cd_out = finish_reduce(mix1_red, chip, ci, "mix1", dres)
    grad_x, gm0, dm1a = mixer0_bwd(dres, mix0[:-1] + (mix0[-1] + tok,), w)
    grads = _merge_layer_grads({**gf0, **gm0}, {**gf1, **gm1})
    grads["final_norm_g"] = d_final
    dmod = jnp.concatenate([jnp.concatenate(dm1a + dm2a, axis=1), jnp.concatenate(dm1b + dm2b, axis=1)], axis=0)

    parts3 = [dmod] + [grads[n] for n, _ in _SMALL_GRADS]
    rows3 = -(-sum(p.size for p in parts3) // LANES // 8) * 8
    g3 = all_gather8(_pack_rows(parts3, rows3, F32), "gather_grads").reshape(N_DEV, rows3, LANES)
    late = order_after(g3, "after_gather_grads").astype(BF16)
    mix0_red, _ = start_reduce([gm0["ab_w_in"], _rows_major(gm0["ab_w_out"]) + late], ci, "mix0")
    summed = sum8(g3).reshape(-1)
    nmod = 2 * N_MOD * D_MODEL
    out_grads = {"ada_b": summed[:nmod].reshape(2, N_MOD * D_MODEL)}
    off = nmod
    for n, shp in _SMALL_GRADS:
        out_grads[n] = summed[off:off + _size(shp)].reshape(shp)
        off += _size(shp)
    for n, shp, axis in _SMALL_SHARDED:
        width = out_grads[n].shape[-1] // N_CHIPS
        out_grads[n] = lax.dynamic_slice_in_dim(out_grads[n], chip * width, width, axis=out_grads[n].ndim - 1)
    dmod_all = g3.reshape(N_DEV, rows3 * LANES)[:, :nmod].reshape(N_DEV, 2, N_MOD * D_MODEL)
    dmod_mine = lax.dynamic_slice_in_dim(dmod_all, chip * ncol, ncol, axis=2).transpose(1, 0, 2)
    out_grads["ada_w"] = ada_grad(c_all.T, dmod_mine)

    red_up0, red_down0 = finish_reduce(ffn0_red, chip, ci, "ffn0", summed)
    red_in0, red_out0 = finish_reduce(mix0_red, chip, ci, "mix0", out_grads["ada_w"])
    out_grads.update(ab_w_in=red_in0, ab_w_out=red_out0, cd_w_in=red_cd_in, c_w_uq=red_uq, c_w_ukv=red_ukv,
                     cd_w_out=red_cd_out)
    per_layer = dict(ffn_w_up=(red_up0, red_up1), ffn_w_down=(red_down0, red_down1))

    g_out, d_out, m_out, v_out = [], [], [], []
    for n in _WEIGHTS:
        if n in per_layer:
            g, d, nm, nv = adamw_layers(a[n], *per_layer[n], a["m_" + n], a["v_" + n], "adamw_" + n)
        else:
            g, d, nm, nv = adamw(a[n], out_grads[n].reshape(a[n].shape), a["m_" + n], a["v_" + n], "adamw_" + n)
        g_out.append(g)
        d_out.append(d)
        m_out.append(nm)
        v_out.append(nv)
    return (loss, grad_x[None], *g_out, *d_out, *m_out, *v_out)
```

```python
import functools

import jax
import jax.numpy as jnp
from jax import lax
from jax.experimental import pallas as pl
from jax.experimental.pallas import tpu as pltpu

F32 = jnp.float32
BF16 = jnp.bfloat16
EPS = 1e-6
D_MODEL = 1024
N_MOD = 6
A_WIDTH = 512
B_GROUPS = 4
POOL_WINDOWS = (2, 4, 8, 16)
C_HEADS = 8
C_NOPE = 64
C_ROPE = 32
C_V = 64
C_Q_RANK = 256
C_KV_RANK = 128
HEAD_PAD = 128
ROPE_THETA = 10000.0
D_GROUPS = 4
D_CHUNK = 128
D_FF = 2816
FF_UNIT = 128
ADAM_LR = 0.001
ADAM_B1 = 0.9
ADAM_B2 = 0.999
ADAM_EPS = 1e-08
ADAM_WD = 0.01
ADAM_STEP = 10
N_CHIPS = 4
N_DEV = 8
LANES = 128
VMEM_BIG = 56 * 1024 * 1024
MESH = pl.DeviceIdType.MESH


def _sds(shape, dtype=F32):
    return jax.ShapeDtypeStruct(tuple(shape), dtype)


def _tile(n, cap, mult=128):
    if n <= cap:
        return n
    best = None
    for t in range(mult, cap + 1, mult):
        if n % t == 0:
            best = t
    assert best is not None, (n, cap, mult)
    return best


def _params(dims=None, vmem=None):
    return pltpu.CompilerParams(dimension_semantics=dims, vmem_limit_bytes=vmem)


def _shift_down(v, k):
    r = pltpu.roll(v, k, axis=0)
    t = lax.broadcasted_iota(jnp.int32, v.shape, 0)
    return jnp.where(t >= k, r, 0.0)


def _shift_up(v, k):
    n = v.shape[0]
    r = pltpu.roll(v, n - k, axis=0)
    t = lax.broadcasted_iota(jnp.int32, v.shape, 0)
    return jnp.where(t < n - k, r, 0.0)


def _sigmoid(v):
    return 1.0 / (1.0 + jnp.exp(-v))


_GELU_C = 0.7978845608028654
_GELU_A = 0.044715


def _gelu(v):
    return 0.5 * v * (1.0 + jnp.tanh(_GELU_C * (v + _GELU_A * v * v * v)))


def _gelu_grad(v):
    th = jnp.tanh(_GELU_C * (v + _GELU_A * v * v * v))
    return 0.5 * (1.0 + th) + 0.5 * v * (1.0 - th * th) * _GELU_C * (1.0 + 3.0 * _GELU_A * v * v)


_NN = (((1,), (0,)), ((), ()))
_NT = (((1,), (1,)), ((), ()))
_TN = (((0,), (0,)), ((), ()))


def _dot(a, b, dims=_NN):
    return lax.dot_general(a, b, dims, preferred_element_type=F32)


def _logical(t, groups):
    return (t.shape[-2], t.shape[-1] * groups)


def _block(tr, tc, groups, cols, where):
    if groups == 1:
        return pl.BlockSpec((tr, tc), where)
    per = cols // groups // tc

    def index(i, j, s):
        r, c = where(i, j, s)
        return (c // per, r, c % per)

    return pl.BlockSpec((None, tr, tc), index)


def matmul(a, b, mode, out_dtype, name, ga=1, gb=1, go=1, tm=None, tn=None, tk=None):
    (ar, ac), (br, bc) = _logical(a, ga), _logical(b, gb)
    if mode == "nn":
        m, k, n = ar, ac, bc
        a_col, b_col = "k", "n"
    elif mode == "nt":
        m, k, n = ar, ac, br
        a_col, b_col = "k", "k"
    else:
        k, m, n = ar, ac, bc
        a_col, b_col = "m", "n"
    limit = {"m": m, "n": n // go, "k": k}
    limit[a_col] = min(limit[a_col], ac // ga)
    limit[b_col] = min(limit[b_col], bc // gb)
    tm = tm or _tile(limit["m"], 1024, 128 if mode == "tn" else 16)
    tn = tn or _tile(limit["n"], 512)
    tk = tk or _tile(limit["k"], 2048, 16 if mode == "tn" else 128)
    nk = k // tk
    if mode == "nn":
        a_spec = _block(tm, tk, ga, ac, lambda i, j, s: (i, s))
        b_spec = _block(tk, tn, gb, bc, lambda i, j, s: (s, j))
        dims = _NN
    elif mode == "nt":
        a_spec = _block(tm, tk, ga, ac, lambda i, j, s: (i, s))
        b_spec = _block(tn, tk, gb, bc, lambda i, j, s: (j, s))
        dims = _NT
    else:
        a_spec = _block(tk, tm, ga, ac, lambda i, j, s: (s, i))
        b_spec = _block(tk, tn, gb, bc, lambda i, j, s: (s, j))
        dims = _TN
    o_spec = _block(tm, tn, go, n, lambda i, j, s: (i, j))
    out_shape = _sds((m, n), out_dtype) if go == 1 else _sds((go, m, n // go), out_dtype)

    def body(a_ref, b_ref, o_ref, acc_ref):
        s = pl.program_id(2)

        @pl.when(s == 0)
        def _():
            acc_ref[...] = jnp.zeros_like(acc_ref)

        acc_ref[...] += _dot(a_ref[...], b_ref[...], dims)

        @pl.when(s == nk - 1)
        def _():
            o_ref[...] = acc_ref[...].astype(o_ref.dtype)

    return pl.pallas_call(
        body, name=name, out_shape=out_shape, grid=(m // tm, n // tn, nk),
        in_specs=[a_spec, b_spec], out_specs=o_spec,
        scratch_shapes=[pltpu.VMEM((tm, tn), F32)],
        compiler_params=_params(("parallel", "parallel", "arbitrary"), VMEM_BIG),
    )(a, b)


def _rows(tm, n):
    return pl.BlockSpec((tm, n), lambda i: (i, 0))


def _vec(n):
    return pl.BlockSpec((1, n), lambda i: (0, 0))


def modnorm_fwd(x, g, sc, sh, name):
    s, d = x.shape
    tm = _tile(s, 256, 8)

    def body(x_ref, g_ref, sc_ref, sh_ref, o_ref):
        xv = x_ref[...]
        r = lax.rsqrt(jnp.mean(xv * xv, axis=-1, keepdims=True) + EPS)
        o_ref[...] = ((xv * r) * g_ref[...] * (1.0 + sc_ref[...]) + sh_ref[...]).astype(BF16)

    return pl.pallas_call(
        body, name=name, out_shape=_sds((s, d), BF16), grid=(s // tm,),
        in_specs=[_rows(tm, d), _vec(d), _vec(d), _vec(d)], out_specs=_rows(tm, d),
        compiler_params=_params(("parallel",)),
    )(x, g, sc, sh)


def resid_fwd(x, y, gate, name):
    s, d = x.shape
    tm = _tile(s, 256, 8)

    def body(x_ref, y_ref, g_ref, o_ref):
        o_ref[...] = x_ref[...] + g_ref[...] * y_ref[...]

    return pl.pallas_call(
        body, name=name, out_shape=_sds((s, d)), grid=(s // tm,),
        in_specs=[_rows(tm, d), _rows(tm, d), _vec(d)], out_specs=_rows(tm, d),
        compiler_params=_params(("parallel",)),
    )(x, y, gate)


def gate_bwd(dres, y, gate, name):
    s, d = dres.shape
    tm = _tile(s, 256, 8)

    def body(dr_ref, y_ref, g_ref, dy_ref, dg_ref):
        @pl.when(pl.program_id(0) == 0)
        def _():
            dg_ref[...] = jnp.zeros_like(dg_ref)

        dr = dr_ref[...]
        dy_ref[...] = (dr * g_ref[...]).astype(BF16)
        dg_ref[...] += jnp.sum(dr * y_ref[...], axis=0, keepdims=True)

    return pl.pallas_call(
        body, name=name, out_shape=(_sds((s, d), BF16), _sds((1, d))), grid=(s // tm,),
        in_specs=[_rows(tm, d), _rows(tm, d), _vec(d)], out_specs=(_rows(tm, d), _vec(d)),
        compiler_params=_params(("arbitrary",)),
    )(dres, y, gate)


def norm_bwd(x, dh, g, sc, dres, name):
    s, d = x.shape
    tm = _tile(s, 256, 8)
    nsteps = s // tm

    def body(x_ref, dh_ref, g_ref, sc_ref, dr_ref, dx_ref, dsh_ref, dsc_ref, dg_ref, a2_ref):
        i = pl.program_id(0)

        @pl.when(i == 0)
        def _():
            dsh_ref[...] = jnp.zeros_like(dsh_ref)
            a2_ref[...] = jnp.zeros_like(a2_ref)

        xv = x_ref[...]
        dh = dh_ref[...]
        r = lax.rsqrt(jnp.mean(xv * xv, axis=-1, keepdims=True) + EPS)
        xh = xv * r
        dsh_ref[...] += jnp.sum(dh, axis=0, keepdims=True)
        a2_ref[...] += jnp.sum(dh * xh, axis=0, keepdims=True)
        dxh = dh * (g_ref[...] * (1.0 + sc_ref[...]))
        dx = r * (dxh - xh * jnp.mean(dxh * xh, axis=-1, keepdims=True))
        dx_ref[...] = dr_ref[...] + dx

        @pl.when(i == nsteps - 1)
        def _():
            dsc_ref[...] = a2_ref[...] * g_ref[...]
            dg_ref[...] = a2_ref[...] * (1.0 + sc_ref[...])

    return pl.pallas_call(
        body, name=name, out_shape=(_sds((s, d)), _sds((1, d)), _sds((1, d)), _sds((1, d))), grid=(nsteps,),
        in_specs=[_rows(tm, d), _rows(tm, d), _vec(d), _vec(d), _rows(tm, d)],
        out_specs=(_rows(tm, d), _vec(d), _vec(d), _vec(d)),
        scratch_shapes=[pltpu.VMEM((1, d), F32)],
        compiler_params=_params(("arbitrary",)),
    )(x, dh, g, sc, dres)


def final_fwd_bwd(x, g, tgt):
    s, d = x.shape
    tm = _tile(s, 256, 8)

    def body(x_ref, g_ref, t_ref, dx_ref, dg_ref, loss_ref):
        @pl.when(pl.program_id(0) == 0)
        def _():
            dg_ref[...] = jnp.zeros_like(dg_ref)
            loss_ref[...] = jnp.zeros_like(loss_ref)

        xv = x_ref[...]
        gv = g_ref[...]
        r = lax.rsqrt(jnp.mean(xv * xv, axis=-1, keepdims=True) + EPS)
        xh = xv * r
        e = xh * gv - t_ref[...]
        row = jnp.sum(e * e, axis=-1, keepdims=True) * (0.5 / d)
        loss_ref[...] += jnp.sum(row, axis=0, keepdims=True)
        dy = e * (1.0 / d)
        dg_ref[...] += jnp.sum(dy * xh, axis=0, keepdims=True)
        dxh = dy * gv
        dx_ref[...] = r * (dxh - xh * jnp.mean(dxh * xh, axis=-1, keepdims=True))

    return pl.pallas_call(
        body, name="final_fwd_bwd", out_shape=(_sds((s, d)), _sds((1, d)), _sds((1, LANES))), grid=(s // tm,),
        in_specs=[_rows(tm, d), _vec(d), _rows(tm, d)], out_specs=(_rows(tm, d), _vec(d), _vec(LANES)),
        compiler_params=_params(("arbitrary",)),
    )(x, g, tgt)


def _taps(v):
    return _shift_down(v, 2), _shift_down(v, 1), v


def _conv3_taps(taps, w):
    return w[0:1, :] * taps[0] + w[1:2, :] * taps[1] + w[2:3, :] * taps[2]


def _conv3(v, w):
    return _conv3_taps(_taps(v), w)


def _conv3_t(dv, w):
    return w[0:1, :] * _shift_up(dv, 2) + w[1:2, :] * _shift_up(dv, 1) + w[2:3, :] * dv


def _conv3_dw_taps(dv, taps):
    return jnp.concatenate([jnp.sum(dv * t, axis=0, keepdims=True) for t in taps], axis=0)


def _conv3_dw(dv, v):
    return _conv3_dw_taps(dv, _taps(v))


def gconv_fwd(z, conv_w):
    s = z.shape[0]
    nb = A_WIDTH // LANES

    def body(b_ref, c_ref, a_ref, w_ref, o_ref):
        b, c, a = b_ref[...].astype(F32), c_ref[...].astype(F32), a_ref[...].astype(F32)
        o_ref[...] = (b * _conv3(c * a, w_ref[...])).astype(BF16)

    col = lambda off: pl.BlockSpec((s, LANES), lambda j: (0, off + j))
    return pl.pallas_call(
        body, name="gconv_fwd", out_shape=_sds((s, A_WIDTH), BF16), grid=(nb,),
        in_specs=[col(0), col(nb), col(2 * nb), pl.BlockSpec((3, LANES), lambda j: (0, j))],
        out_specs=pl.BlockSpec((s, LANES), lambda j: (0, j)),
        compiler_params=_params(("parallel",), VMEM_BIG),
    )(z, z, z, conv_w)


def gconv_bwd(z, conv_w, dycat):
    s = z.shape[0]
    nb = A_WIDTH // LANES

    def body(b_ref, c_ref, a_ref, w_ref, dy_ref, db_ref, dc_ref, da_ref, dw_ref):
        c, a, w, dy = c_ref[...].astype(F32), a_ref[...].astype(F32), w_ref[...], dy_ref[...].astype(F32)
        ca = c * a
        db_ref[...] = (dy * _conv3(ca, w)).astype(BF16)
        dconv = dy * b_ref[...].astype(F32)
        dw_ref[...] = _conv3_dw(dconv, ca)
        dca = _conv3_t(dconv, w)
        dc_ref[...] = (dca * a).astype(BF16)
        da_ref[...] = (dca * c).astype(BF16)

    col = lambda off: pl.BlockSpec((s, LANES), lambda j: (0, off + j))
    wspec = pl.BlockSpec((3, LANES), lambda j: (0, j))
    part = _sds((s, A_WIDTH), BF16)
    return pl.pallas_call(
        body, name="gconv_bwd", out_shape=(part, part, part, _sds((3, A_WIDTH))), grid=(nb,),
        in_specs=[col(0), col(nb), col(2 * nb), wspec, col(0)],
        out_specs=(col(0), col(0), col(0), wspec),
        compiler_params=_params(("parallel",), VMEM_BIG),
    )(z, z, z, conv_w, dycat)


def _pool_counts(s, w):
    t = lax.broadcasted_iota(jnp.int32, (s, 1), 0)
    return jnp.minimum(t + 1, w).astype(F32)


def _pooled(p, levels):
    acc = p
    for lv in range(levels):
        acc = acc + _shift_down(acc, 2 ** lv)
    return acc / _pool_counts(p.shape[0], 2 ** levels) - p


def pool_fwd(z, mix_w, scale):
    s = z.shape[0]

    def make(g):
        def body_g(p_ref, m_ref, sc_ref, o_ref):
            pooled = _pooled(p_ref[...].astype(F32), g + 1)
            y = _dot(pooled.astype(BF16), m_ref[...].astype(BF16))
            o_ref[...] = (y * sc_ref[...]).astype(BF16)
        return body_g

    outs = []
    for g in range(B_GROUPS):
        outs.append(pl.pallas_call(
            make(g), name=f"pool_fwd{g}", out_shape=_sds((s, LANES), BF16), grid=(1,),
            in_specs=[pl.BlockSpec((s, LANES), lambda i, g=g: (0, 3 * (A_WIDTH // LANES) + g)),
                      pl.BlockSpec((None, LANES, LANES), lambda i, g=g: (g, 0, 0)),
                      pl.BlockSpec((1, LANES), lambda i, g=g: (0, g))],
            out_specs=pl.BlockSpec((s, LANES), lambda i: (0, 0)),
            compiler_params=_params(("arbitrary",), VMEM_BIG),
        )(z, mix_w, scale))
    return outs


def pool_bwd(z, mix_w, scale, dycat):
    s = z.shape[0]

    def make(g):
        w = 2 ** (g + 1)

        def body_g(p_ref, m_ref, sc_ref, dy_ref, dp_ref, dm_ref, dsc_ref):
            pooled = _pooled(p_ref[...].astype(F32), g + 1)
            mw = m_ref[...].astype(BF16)
            pb = pooled.astype(BF16)
            dy = dy_ref[...].astype(F32)
            dsc_ref[...] = jnp.sum(dy * _dot(pb, mw), axis=0, keepdims=True)
            dmix = (dy * sc_ref[...]).astype(BF16)
            dm_ref[...] = _dot(pb, dmix, _TN)
            dpool = _dot(dmix, mw, _NT)
            acc = dpool / _pool_counts(s, w)
            for lv in range(g + 1):
                acc = acc + _shift_up(acc, 2 ** lv)
            dp_ref[...] = (acc - dpool).astype(BF16)
        return body_g

    outs = []
    for g in range(B_GROUPS):
        outs.append(pl.pallas_call(
            make(g), name=f"pool_bwd{g}",
            out_shape=(_sds((s, LANES), BF16), _sds((LANES, LANES)), _sds((1, LANES))), grid=(1,),
            in_specs=[pl.BlockSpec((s, LANES), lambda i, g=g: (0, 3 * (A_WIDTH // LANES) + g)),
                      pl.BlockSpec((None, LANES, LANES), lambda i, g=g: (g, 0, 0)),
                      pl.BlockSpec((1, LANES), lambda i, g=g: (0, g)),
                      pl.BlockSpec((s, LANES), lambda i, g=g: (0, A_WIDTH // LANES + g))],
            out_specs=(pl.BlockSpec((s, LANES), lambda i: (0, 0)), pl.BlockSpec((LANES, LANES), lambda i: (0, 0)),
                       pl.BlockSpec((1, LANES), lambda i: (0, 0))),
            compiler_params=_params(("arbitrary",), VMEM_BIG),
        )(z, mix_w, scale, dycat))
    return outs


_FF_BLOCKS = D_FF // FF_UNIT


def _ff_spec(s):
    return pl.BlockSpec((2, s, FF_UNIT), lambda j: (0, 0, j))


def _ff_wspecs():
    return [pl.BlockSpec((3, FF_UNIT), lambda j: (0, j)), pl.BlockSpec((3, FF_UNIT), lambda j: (0, _FF_BLOCKS + j))]


def ffn_act_fwd(zf, conv_w, name):
    s = zf.shape[1]

    def body(z_ref, wg_ref, wu_ref, o_ref):
        g = _conv3(z_ref[0].astype(F32), wg_ref[...])
        u = _conv3(z_ref[1].astype(F32), wu_ref[...])
        o_ref[...] = (g * _sigmoid(g) * u).astype(BF16)

    return pl.pallas_call(
        body, name=name, out_shape=_sds((s, D_FF), BF16), grid=(_FF_BLOCKS,),
        in_specs=[_ff_spec(s)] + _ff_wspecs(), out_specs=pl.BlockSpec((s, FF_UNIT), lambda j: (0, j)),
        compiler_params=_params(("parallel",), VMEM_BIG),
    )(zf, conv_w, conv_w)


def ffn_act_bwd(zf, conv_w, da, name):
    s = zf.shape[1]

    def body(z_ref, wg_ref, wu_ref, da_ref, dz_ref, dw_ref):
        tg, tu = _taps(z_ref[0].astype(F32)), _taps(z_ref[1].astype(F32))
        wg, wu = wg_ref[...], wu_ref[...]
        dav = da_ref[...].astype(F32)
        g = _conv3_taps(tg, wg)
        u = _conv3_taps(tu, wu)
        sg = _sigmoid(g)
        dg = dav * u * (sg * (1.0 + g * (1.0 - sg)))
        du = dav * (g * sg)
        dw_ref[0] = _conv3_dw_taps(dg, tg)
        dw_ref[1] = _conv3_dw_taps(du, tu)
        dz_ref[0] = _conv3_t(dg, wg).astype(BF16)
        dz_ref[1] = _conv3_t(du, wu).astype(BF16)

    return pl.pallas_call(
        body, name=name, out_shape=(_sds((2, s, D_FF), BF16), _sds((2, 3, D_FF))), grid=(_FF_BLOCKS,),
        in_specs=[_ff_spec(s)] + _ff_wspecs() + [pl.BlockSpec((s, FF_UNIT), lambda j: (0, j))],
        out_specs=(_ff_spec(s), pl.BlockSpec((2, 3, FF_UNIT), lambda j: (0, 0, j))),
        compiler_params=_params(("parallel",), VMEM_BIG),
    )(zf, conv_w, conv_w, da)


def _rope(v, cs, s1, s2):
    return v * cs + pltpu.roll(v, LANES - C_ROPE // 2, axis=1) * s1 + pltpu.roll(v, C_ROPE // 2, axis=1) * s2


def _rope_t(dv, cs, s1, s2):
    return dv * cs + pltpu.roll(dv * s1, C_ROPE // 2, axis=1) + pltpu.roll(dv * s2, LANES - C_ROPE // 2, axis=1)


def _kpe_mask(shape):
    lane = lax.broadcasted_iota(jnp.int32, shape, 1)
    return (lane >= C_NOPE) & (lane < C_NOPE + C_ROPE)


def _rms(v, g):
    r = lax.rsqrt(jnp.mean(v * v, axis=-1, keepdims=True) + EPS)
    return v * r, r


def _rms_bwd(dn, xh, r, g):
    dxh = dn * g
    return r * (dxh - xh * jnp.mean(dxh * xh, axis=-1, keepdims=True)), jnp.sum(dn * xh, axis=0, keepdims=True)


_ZQ = C_Q_RANK + C_KV_RANK + HEAD_PAD
_HW = C_HEADS * HEAD_PAD


def mla_pre_fwd(z, gq, gkv, wq, wk, wv, cs, s1, s2):
    s = z.shape[0]
    tm = _tile(s, 256, 8)

    def body(z_ref, gq_ref, gkv_ref, wq_ref, wk_ref, wv_ref, cs_ref, s1_ref, s2_ref, q_ref, k_ref, v_ref):
        zv = z_ref[...].astype(F32)
        cst, s1t, s2t = cs_ref[...], s1_ref[...], s2_ref[...]
        qh, _ = _rms(zv[:, :C_Q_RANK], None)
        qn = (qh * gq_ref[...]).astype(BF16)
        q = _dot(qn, wq_ref[...])
        kh, _ = _rms(zv[:, C_Q_RANK:C_Q_RANK + C_KV_RANK], None)
        kvn = (kh * gkv_ref[...]).astype(BF16)
        k = _dot(kvn, wk_ref[...])
        v_ref[...] = _dot(kvn, wv_ref[...]).astype(BF16)
        kpe = _rope(zv[:, C_Q_RANK + C_KV_RANK:], cst, s1t, s2t)
        for h in range(C_HEADS):
            sl = slice(h * HEAD_PAD, (h + 1) * HEAD_PAD)
            q_ref[:, sl] = _rope(q[:, sl], cst, s1t, s2t).astype(BF16)
            k_ref[:, sl] = (k[:, sl] + kpe).astype(BF16)

    full = lambda r, c: pl.BlockSpec((r, c), lambda i: (0, 0))
    hw = _sds((s, _HW), BF16)
    return pl.pallas_call(
        body, name="mla_pre_fwd", out_shape=(hw, hw, hw), grid=(s // tm,),
        in_specs=[_rows(tm, _ZQ), _vec(C_Q_RANK), _vec(C_KV_RANK), full(C_Q_RANK, _HW), full(C_KV_RANK, _HW),
                  full(C_KV_RANK, _HW), _rows(tm, LANES), _rows(tm, LANES), _rows(tm, LANES)],
        out_specs=(_rows(tm, _HW), _rows(tm, _HW), _rows(tm, _HW)),
        compiler_params=_params(("parallel",), VMEM_BIG),
    )(z, gq, gkv, wq, wk, wv, cs, s1, s2)


def mla_pre_bwd(z, gq, gkv, wq, wk, wv, cs, s1, s2, dq, dk, dv):
    s = z.shape[0]
    tm = _tile(s, 256, 8)

    def body(z_ref, gq_ref, gkv_ref, wq_ref, wk_ref, wv_ref, cs_ref, s1_ref, s2_ref, dq_ref, dk_ref, dv_ref,
             dz_ref, dwq_ref, dwk_ref, dwv_ref, dgq_ref, dgkv_ref):
        @pl.when(pl.program_id(0) == 0)
        def _():
            dwq_ref[...] = jnp.zeros_like(dwq_ref)
            dwk_ref[...] = jnp.zeros_like(dwk_ref)
            dwv_ref[...] = jnp.zeros_like(dwv_ref)
            dgq_ref[...] = jnp.zeros_like(dgq_ref)
            dgkv_ref[...] = jnp.zeros_like(dgkv_ref)

        zv = z_ref[...].astype(F32)
        cst, s1t, s2t = cs_ref[...], s1_ref[...], s2_ref[...]
        gqv, gkvv = gq_ref[...], gkv_ref[...]
        qh, rq = _rms(zv[:, :C_Q_RANK], None)
        qn = (qh * gqv).astype(BF16)
        kh, rk = _rms(zv[:, C_Q_RANK:C_Q_RANK + C_KV_RANK], None)
        kvn = (kh * gkvv).astype(BF16)

        dqv = dq_ref[...].astype(F32)
        dqp = jnp.concatenate(
            [_rope_t(dqv[:, h * HEAD_PAD:(h + 1) * HEAD_PAD], cst, s1t, s2t) for h in range(C_HEADS)], axis=1
        ).astype(BF16)
        dwq_ref[...] += _dot(qn, dqp, _TN)
        dqn = _dot(dqp, wq_ref[...], _NT)
        dql, dgq = _rms_bwd(dqn, qh, rq, gqv)
        dgq_ref[...] += dgq

        dkv = dk_ref[...]
        dkb = dkv.astype(BF16)
        dvb = dv_ref[...].astype(BF16)
        dwk_ref[...] += _dot(kvn, dkb, _TN)
        dwv_ref[...] += _dot(kvn, dvb, _TN)
        dkvn = _dot(dkb, wk_ref[...], _NT) + _dot(dvb, wv_ref[...], _NT)
        dkl, dgkv = _rms_bwd(dkvn, kh, rk, gkvv)
        dgkv_ref[...] += dgkv

        dkpe = dkv[:, :HEAD_PAD]
        for h in range(1, C_HEADS):
            dkpe = dkpe + dkv[:, h * HEAD_PAD:(h + 1) * HEAD_PAD]
        dkpe = _rope_t(jnp.where(_kpe_mask(dkpe.shape), dkpe, 0.0), cst, s1t, s2t)
        dz_ref[...] = jnp.concatenate([dql, dkl, dkpe], axis=1).astype(BF16)

    full = lambda r, c: pl.BlockSpec((r, c), lambda i: (0, 0))
    return pl.pallas_call(
        body, name="mla_pre_bwd",
        out_shape=(_sds((s, _ZQ), BF16), _sds((C_Q_RANK, _HW)), _sds((C_KV_RANK, _HW)), _sds((C_KV_RANK, _HW)),
                   _sds((1, C_Q_RANK)), _sds((1, C_KV_RANK))),
        grid=(s // tm,),
        in_specs=[_rows(tm, _ZQ), _vec(C_Q_RANK), _vec(C_KV_RANK), full(C_Q_RANK, _HW), full(C_KV_RANK, _HW),
                  full(C_KV_RANK, _HW), _rows(tm, LANES), _rows(tm, LANES), _rows(tm, LANES),
                  _rows(tm, _HW), _rows(tm, _HW), _rows(tm, _HW)],
        out_specs=(_rows(tm, _ZQ), full(C_Q_RANK, _HW), full(C_KV_RANK, _HW), full(C_KV_RANK, _HW),
                   _vec(C_Q_RANK), _vec(C_KV_RANK)),
        compiler_params=_params(("arbitrary",), VMEM_BIG),
    )(z, gq, gkv, wq, wk, wv, cs, s1, s2, dq, dk, dv)


_ATT_SCALE = (C_NOPE + C_ROPE) ** -0.5
_NEG = -1e30


def _att_probs(q, k, row0):
    sc = _dot(q, k, _NT) * _ATT_SCALE
    qpos = row0 + lax.broadcasted_iota(jnp.int32, sc.shape, 0)
    kpos = lax.broadcasted_iota(jnp.int32, sc.shape, 1)
    sc = jnp.where(kpos <= qpos, sc, _NEG)
    e = jnp.exp(sc - jnp.max(sc, axis=-1, keepdims=True))
    return e / jnp.sum(e, axis=-1, keepdims=True)


def _causal_cases(i, nq, tq, fn):
    if nq > 8:
        fn(nq * tq)
        return
    for blk in range(nq):
        pl.when(i == blk)(functools.partial(fn, (blk + 1) * tq))


def attn_fwd(q, k, v):
    s = q.shape[0]
    tq = _tile(s, 256, 8)
    nq = s // tq

    def body(q_ref, k_ref, v_ref, o_ref):
        i = pl.program_id(1)

        def case(nk):
            p = _att_probs(q_ref[...], k_ref[:nk, :], i * tq)
            o_ref[...] = _dot(p.astype(BF16), v_ref[:nk, :]).astype(BF16)

        _causal_cases(i, nq, tq, case)

    qspec = pl.BlockSpec((tq, HEAD_PAD), lambda h, i: (i, h))
    kspec = pl.BlockSpec((s, HEAD_PAD), lambda h, i: (0, h))
    return pl.pallas_call(
        body, name="attn_fwd", out_shape=_sds((s, _HW), BF16), grid=(C_HEADS, s // tq),
        in_specs=[qspec, kspec, kspec], out_specs=qspec,
        compiler_params=_params(("parallel", "parallel"), VMEM_BIG),
    )(q, k, v)


def attn_bwd(q, k, v, o, do_all, do_col0):
    s = q.shape[0]
    tq = _tile(s, 256, 8)

    def body(q_ref, k_ref, v_ref, o_ref, do_ref, dq_ref, dk_ref, dv_ref):
        i = pl.program_id(1)

        @pl.when(i == 0)
        def _():
            dk_ref[...] = jnp.zeros_like(dk_ref)
            dv_ref[...] = jnp.zeros_like(dv_ref)

        def case(nk):
            qv, kv, vv, dov = q_ref[...], k_ref[:nk, :], v_ref[:nk, :], do_ref[...]
            p = _att_probs(qv, kv, i * tq)
            dp = _dot(dov, vv, _NT)
            delta = jnp.sum(dov.astype(F32) * o_ref[...].astype(F32), axis=-1, keepdims=True)
            ds = (p * (dp - delta) * _ATT_SCALE).astype(BF16)
            dq_ref[...] = _dot(ds, kv).astype(BF16)
            dk_ref[:nk, :] += _dot(ds, qv, _TN)
            dv_ref[:nk, :] += _dot(p.astype(BF16), dov, _TN)

        _causal_cases(i, s // tq, tq, case)

    qspec = pl.BlockSpec((tq, HEAD_PAD), lambda h, i: (i, h))
    dospec = pl.BlockSpec((tq, HEAD_PAD), lambda h, i: (i, do_col0 + h))
    kspec = pl.BlockSpec((s, HEAD_PAD), lambda h, i: (0, h))
    return pl.pallas_call(
        body, name="attn_bwd", out_shape=(_sds((s, _HW), BF16), _sds((s, _HW)), _sds((s, _HW))),
        grid=(C_HEADS, s // tq),
        in_specs=[qspec, kspec, kspec, qspec, dospec], out_specs=(qspec, kspec, kspec),
        compiler_params=_params(("parallel", "arbitrary"), VMEM_BIG),
    )(q, k, v, o, do_all)


_DW = D_GROUPS * LANES


def _tril_bf16(w):
    r = lax.broadcasted_iota(jnp.int32, w.shape, 0)
    c = lax.broadcasted_iota(jnp.int32, w.shape, 1)
    return jnp.where(c <= r, w, 0.0).astype(BF16)


def _sgu_forward(zu, zv, lg, lb, ws_ref, bs):
    u = _gelu(zu)
    v = _gelu(zv)
    mu = jnp.mean(v, axis=-1, keepdims=True)
    vc = v - mu
    rstd = lax.rsqrt(jnp.mean(vc * vc, axis=-1, keepdims=True) + EPS)
    xh = vc * rstd
    vln = (xh * lg + lb).astype(BF16)
    mixed = []
    for g in range(D_GROUPS):
        wg = _tril_bf16(ws_ref[g])
        mixed.append(_dot(wg, vln[:, g * LANES:(g + 1) * LANES]) + bs[:, g:g + 1])
    return u, xh, rstd, vln, jnp.concatenate(mixed, axis=1)


def sgu_fwd(z, lg, lb, ws, bs_t):
    s = z.shape[0]
    nchunk = s // D_CHUNK

    def body(zu_ref, zv_ref, lg_ref, lb_ref, ws_ref, bs_ref, o_ref):
        u, _, _, _, mixed = _sgu_forward(zu_ref[...].astype(F32), zv_ref[...].astype(F32), lg_ref[...], lb_ref[...],
                                         ws_ref, bs_ref[...])
        o_ref[...] = (u * mixed).astype(BF16)

    return pl.pallas_call(
        body, name="sgu_fwd", out_shape=_sds((s, _DW), BF16), grid=(nchunk,),
        in_specs=[pl.BlockSpec((D_CHUNK, _DW), lambda n: (n, 1)), pl.BlockSpec((D_CHUNK, _DW), lambda n: (n, 2)),
                  _vec(_DW), _vec(_DW), pl.BlockSpec((D_GROUPS, D_CHUNK, D_CHUNK), lambda n: (0, 0, 0)),
                  pl.BlockSpec((D_CHUNK, LANES), lambda n: (0, 0))],
        out_specs=pl.BlockSpec((D_CHUNK, _DW), lambda n: (n, 0)),
        compiler_params=_params(("parallel",)),
    )(z, z, lg, lb, ws, bs_t)


def sgu_bwd(z, lg, lb, ws, bs_t, dycat, dy_col):
    s = z.shape[0]
    nchunk = s // D_CHUNK

    def body(zu_ref, zv_ref, lg_ref, lb_ref, ws_ref, bs_ref, dy_ref, dzu_ref, dzv_ref, dws_ref, dbs_ref, dlg_ref,
             dlb_ref):
        @pl.when(pl.program_id(0) == 0)
        def _():
            dws_ref[...] = jnp.zeros_like(dws_ref)
            dbs_ref[...] = jnp.zeros_like(dbs_ref)
            dlg_ref[...] = jnp.zeros_like(dlg_ref)
            dlb_ref[...] = jnp.zeros_like(dlb_ref)

        zu, zv, lg = zu_ref[...].astype(F32), zv_ref[...].astype(F32), lg_ref[...]
        u, xh, rstd, vln, mixed = _sgu_forward(zu, zv, lg, lb_ref[...], ws_ref, bs_ref[...])
        dy = dy_ref[...].astype(F32)
        dzu_ref[...] = (dy * mixed * _gelu_grad(zu)).astype(BF16)
        dmix = dy * u
        lane = lax.broadcasted_iota(jnp.int32, (D_CHUNK, LANES), 1)
        row = lax.broadcasted_iota(jnp.int32, (D_CHUNK, D_CHUNK), 0)
        colm = lax.broadcasted_iota(jnp.int32, (D_CHUNK, D_CHUNK), 1)
        dvln = []
        dbs = jnp.zeros((D_CHUNK, LANES), F32)
        for g in range(D_GROUPS):
            sl = slice(g * LANES, (g + 1) * LANES)
            dmg = dmix[:, sl]
            dbs = dbs + jnp.where(lane == g, jnp.sum(dmg, axis=-1, keepdims=True), 0.0)
            dmb = dmg.astype(BF16)
            dws_ref[g] += jnp.where(colm <= row, _dot(dmb, vln[:, sl], _NT), 0.0)
            dvln.append(_dot(_tril_bf16(ws_ref[g]), dmb, _TN))
        dbs_ref[...] += dbs
        dvln = jnp.concatenate(dvln, axis=1)
        dlg_ref[...] += jnp.sum(dvln * xh, axis=0, keepdims=True)
        dlb_ref[...] += jnp.sum(dvln, axis=0, keepdims=True)
        dxh = dvln * lg
        dvv = rstd * (dxh - jnp.mean(dxh, axis=-1, keepdims=True) - xh * jnp.mean(dxh * xh, axis=-1, keepdims=True))
        dzv_ref[...] = (dvv * _gelu_grad(zv)).astype(BF16)

    wsspec = pl.BlockSpec((D_GROUPS, D_CHUNK, D_CHUNK), lambda n: (0, 0, 0))
    chunk = lambda cidx: pl.BlockSpec((D_CHUNK, _DW), lambda n: (n, cidx))
    return pl.pallas_call(
        body, name="sgu_bwd",
        out_shape=(_sds((s, _DW), BF16), _sds((s, _DW), BF16), _sds((D_GROUPS, D_CHUNK, D_CHUNK)),
                   _sds((D_CHUNK, LANES)), _sds((1, _DW)), _sds((1, _DW))),
        grid=(nchunk,),
        in_specs=[chunk(1), chunk(2), _vec(_DW), _vec(_DW), wsspec, pl.BlockSpec((D_CHUNK, LANES), lambda n: (0, 0)),
                  chunk(dy_col)],
        out_specs=(chunk(0), chunk(0), wsspec, pl.BlockSpec((D_CHUNK, LANES), lambda n: (0, 0)), _vec(_DW), _vec(_DW)),
        compiler_params=_params(("arbitrary",)),
    )(z, z, lg, lb, ws, bs_t, dycat)


def ada_mod(c_all, ada_w, ada_b):
    nl, d, n = ada_w.shape
    nb = c_all.shape[0]
    tn = _tile(n, 512)

    def body(c_ref, w_ref, b_ref, o_ref):
        cv = c_ref[...]
        ca = (cv * _sigmoid(cv)).astype(BF16)
        o_ref[...] = _dot(ca, w_ref[...].astype(BF16)) + b_ref[...]

    return pl.pallas_call(
        body, name="ada_mod", out_shape=_sds((nl, nb, n)), grid=(nl, n // tn),
        in_specs=[pl.BlockSpec((nb, d), lambda l, j: (0, 0)), pl.BlockSpec((None, d, tn), lambda l, j: (l, 0, j)),
                  pl.BlockSpec((None, 1, tn), lambda l, j: (l, 0, j))],
        out_specs=pl.BlockSpec((None, nb, tn), lambda l, j: (l, 0, j)),
        compiler_params=_params(("parallel", "parallel")),
    )(c_all, ada_w, ada_b.reshape(nl, 1, n))


def ada_grad(c_all_t, dmod):
    d, nb = c_all_t.shape
    nl, _, n = dmod.shape
    tn = _tile(n, 512)
    tr = _tile(d, 256, 8)

    def body(c_ref, dm_ref, o_ref):
        cv = c_ref[...]
        ca = cv * _sigmoid(cv)
        dm = dm_ref[...]
        acc = ca[:, 0:1] * dm[0:1, :]
        for b in range(1, nb):
            acc = acc + ca[:, b:b + 1] * dm[b:b + 1, :]
        o_ref[...] = acc

    return pl.pallas_call(
        body, name="ada_grad", out_shape=_sds((nl, d, n)), grid=(nl, n // tn, d // tr),
        in_specs=[pl.BlockSpec((tr, nb), lambda l, j, r: (r, 0)), pl.BlockSpec((None, nb, tn), lambda l, j, r: (l, 0, j))],
        out_specs=pl.BlockSpec((None, tr, tn), lambda l, j, r: (l, r, j)),
        compiler_params=_params(("parallel", "parallel", "parallel")),
    )(c_all_t, dmod)


_ADAM_BLOCK = 256 * 1024


def _adam_rows(rows, cols):
    if rows * cols <= _ADAM_BLOCK or rows % 8:
        return rows
    return _tile(rows, max(8, _ADAM_BLOCK // cols), 8)


def _adam_update(w, gv, m, v):
    inv_bc1 = 1.0 / (1.0 - ADAM_B1 ** ADAM_STEP)
    inv_bc2 = 1.0 / (1.0 - ADAM_B2 ** ADAM_STEP)
    nm = ADAM_B1 * m + (1.0 - ADAM_B1) * gv
    nv = ADAM_B2 * v + (1.0 - ADAM_B2) * (gv * gv)
    return -ADAM_LR * ((nm * inv_bc1) / (jnp.sqrt(nv * inv_bc2) + ADAM_EPS) + ADAM_WD * w), nm, nv


def adamw(w, g, m, v, name):
    shape = w.shape
    cols = shape[-1]
    rows = w.size // cols
    tr = _adam_rows(rows, cols)

    def body(w_ref, g_ref, m_ref, v_ref, d_ref, nm_ref, nv_ref):
        d_ref[...], nm_ref[...], nv_ref[...] = _adam_update(w_ref[...], g_ref[...], m_ref[...], v_ref[...])

    spec = pl.BlockSpec((tr, cols), lambda i: (i, 0))
    out = _sds((rows, cols))
    r2 = lambda t: t.reshape(rows, cols)
    d, nm, nv = pl.pallas_call(
        body, name=name, out_shape=(out, out, out), grid=(rows // tr,),
        in_specs=[spec] * 4, out_specs=(spec,) * 3, compiler_params=_params(("parallel",)),
    )(r2(w), r2(g), r2(m), r2(v))
    return g.reshape(shape), d.reshape(shape), nm.reshape(shape), nv.reshape(shape)


def adamw_layers(w, g0, g1, m, v, name):
    _, rows, cols = w.shape
    tr = _adam_rows(rows, cols)

    def body(w_ref, g0_ref, g1_ref, m_ref, v_ref, g_ref, d_ref, nm_ref, nv_ref):
        gv = jnp.where(pl.program_id(0) == 0, g0_ref[...], g1_ref[...])
        g_ref[...] = gv
        d_ref[...], nm_ref[...], nv_ref[...] = _adam_update(w_ref[...], gv, m_ref[...], v_ref[...])

    spec = pl.BlockSpec((None, tr, cols), lambda l, i: (l, i, 0))
    gspec = pl.BlockSpec((tr, cols), lambda l, i: (i, 0))
    out = _sds((2, rows, cols))
    return pl.pallas_call(
        body, name=name, out_shape=(out, out, out, out), grid=(2, rows // tr),
        in_specs=[spec, gspec, gspec, spec, spec], out_specs=(spec,) * 4, compiler_params=_params(("parallel", "parallel")),
    )(w, g0, g1, m, v)


def sum8(gathered):
    _, r, _ = gathered.shape
    tr = _tile(r, 512, 8)

    def body(g_ref, o_ref):
        acc = g_ref[0]
        for dev in range(1, N_DEV):
            acc = acc + g_ref[dev]
        o_ref[...] = acc

    return pl.pallas_call(
        body, name="sum8", out_shape=_sds((r, LANES)), grid=(r // tr,),
        in_specs=[pl.BlockSpec((N_DEV, tr, LANES), lambda i: (0, i, 0))], out_specs=pl.BlockSpec((tr, LANES), lambda i: (i, 0)),
        compiler_params=_params(("parallel",)),
    )(gathered)


_SUM_BLOCK = 512 * 1024


def _sum_rows(rh, cols):
    return rh if rh * cols <= _SUM_BLOCK else _tile(rh, max(16, _SUM_BLOCK // cols), 16)


def pair_sum(g, recv, core, name):
    _, r, cols = g.shape
    rh = r // 2
    tr = _sum_rows(rh, cols)
    per = rh // tr

    def body(c_ref, a_ref, b_ref, o_ref):
        del c_ref
        o_ref[...] = (a_ref[...].astype(F32) + b_ref[...].astype(F32)).astype(BF16)

    grid_spec = pltpu.PrefetchScalarGridSpec(
        num_scalar_prefetch=1, grid=(N_CHIPS, per),
        in_specs=[pl.BlockSpec((None, tr, cols), lambda k, i, c: (k, c[0] * per + i, 0)),
                  pl.BlockSpec((None, tr, cols), lambda k, i, c: (k, i, 0))],
        out_specs=pl.BlockSpec((None, tr, cols), lambda k, i, c: (k, i, 0)))
    return pl.pallas_call(
        body, name=name, out_shape=_sds((N_CHIPS, rh, cols), BF16), grid_spec=grid_spec,
        compiler_params=_params(("parallel", "parallel")),
    )(core.reshape(1).astype(jnp.int32), g, recv)


def chip_sum(pair, recv, chip, core, name):
    _, rh, cols = pair.shape
    tr = _sum_rows(rh, cols)

    def body(p_ref, own_ref, r_ref, o_ref):
        del p_ref
        acc = own_ref[...].astype(F32)
        for j in range(N_CHIPS - 1):
            acc = acc + r_ref[j].astype(F32)
        o_ref[...] = acc

    grid_spec = pltpu.PrefetchScalarGridSpec(
        num_scalar_prefetch=1, grid=(rh // tr,),
        in_specs=[pl.BlockSpec((None, tr, cols), lambda i, p: (p[0], i, 0)),
                  pl.BlockSpec((N_CHIPS - 1, tr, cols), lambda i, p: (0, i, 0))],
        out_specs=pl.BlockSpec((None, tr, cols), lambda i, p: (p[1], i, 0)))
    return pl.pallas_call(
        body, name=name, out_shape=_sds((2, rh, cols)), grid_spec=grid_spec,
        compiler_params=_params(("parallel",)),
    )(jnp.stack([chip, core]).astype(jnp.int32), pair, recv)


def _place():
    return lax.axis_index("x"), lax.axis_index("y"), lax.axis_index("c")


def _other_chips(x, y):
    return [(x, 1 - y), (1 - x, y), (1 - x, 1 - y)]


_HBM = pl.BlockSpec(memory_space=pltpu.HBM)


def all_gather8(v, name):
    m, n = v.shape

    def body(x_ref, out_ref, send_sems, recv_sems, local_sem):
        x, y, c = _place()
        me, sibling = (x, y, c), (x, y, 1 - c)
        chips = _other_chips(x, y)

        def rows(px, py, pc):
            return out_ref.at[pl.ds((4 * px + 2 * py + pc) * m, m), :]

        def copy(k, block, to, src=None):
            return pltpu.make_async_remote_copy(
                src_ref=rows(*block) if src is None else src, dst_ref=rows(*block),
                send_sem=send_sems.at[k], recv_sem=recv_sems.at[k], device_id=to, device_id_type=MESH)

        mine = pltpu.make_async_copy(x_ref, rows(*me), local_sem)
        mine.start()
        first = [copy(0, me, sibling, src=x_ref)]
        first += [copy(1 + j, me, (*chip, c), src=x_ref) for j, chip in enumerate(chips)]
        for cp in first:
            cp.start()
        passed = [copy(4 + j, (*chip, c), sibling) for j, chip in enumerate(chips)]
        for j, chip in enumerate(chips):
            copy(1 + j, (*chip, c), me).wait_recv()
            passed[j].start()
        copy(0, sibling, me).wait_recv()
        for j, chip in enumerate(chips):
            copy(4 + j, (*chip, 1 - c), me).wait_recv()
        for cp in first + passed:
            cp.wait_send()
        mine.wait()

    return pl.pallas_call(
        body, name=name, out_shape=_sds((N_DEV * m, n), v.dtype),
        in_specs=[pl.BlockSpec(memory_space=pltpu.VMEM)], out_specs=pl.BlockSpec(memory_space=pltpu.VMEM),
        scratch_shapes=[pltpu.SemaphoreType.DMA((7,)), pltpu.SemaphoreType.DMA((7,)), pltpu.SemaphoreType.DMA],
        compiler_params=_params(None, VMEM_BIG),
    )(v)


def _comm_call(body, name, ins, out_shapes, nsem, aliases=None):
    return pl.pallas_call(
        body, name=name, out_shape=tuple(out_shapes), in_specs=[_HBM] * len(ins), out_specs=tuple([_HBM] * len(out_shapes)),
        scratch_shapes=[pltpu.SemaphoreType.DMA((nsem,)), pltpu.SemaphoreType.DMA((nsem,))],
        input_output_aliases=aliases or {},
    )(*ins)


def _remote(src, dst, send_sems, recv_sems, k, to):
    return pltpu.make_async_remote_copy(src_ref=src, dst_ref=dst, send_sem=send_sems.at[k], recv_sem=recv_sems.at[k],
                                        device_id=to, device_id_type=MESH)


def _half(core, rh):
    return pl.ds(pl.multiple_of(core * rh, 16), rh)


def swap_halves(gs, name):
    n = len(gs)

    def body(*refs):
        ins, outs, (send_sems, recv_sems) = refs[:n], refs[n:2 * n], refs[2 * n:]
        x, y, c = _place()
        copies = []
        for i in range(n):
            theirs = _half(1 - c, ins[i].shape[1] // 2)
            cp = _remote(ins[i].at[:, theirs], outs[i], send_sems, recv_sems, i, (x, y, 1 - c))
            cp.start()
            copies.append(cp)
        for cp in copies:
            cp.wait()

    return _comm_call(body, name, gs, [_sds((g.shape[0], g.shape[1] // 2, g.shape[2]), g.dtype) for g in gs], n)


def join_halves(bufs, name):
    n = len(bufs)

    def body(*refs):
        ins, outs, (send_sems, recv_sems) = refs[:n], refs[n:2 * n], refs[2 * n:]
        x, y, c = _place()
        copies = []
        for i in range(n):
            cp = _remote(ins[i].at[c], outs[i].at[c], send_sems, recv_sems, i, (x, y, 1 - c))
            cp.start()
            copies.append(cp)
        for i in range(n):
            theirs = outs[i].at[1 - c]
            _remote(theirs, theirs, send_sems, recv_sems, i, (x, y, 1 - c)).wait_recv()
        for cp in copies:
            cp.wait_send()

    return _comm_call(body, name, bufs, [_sds(b.shape, b.dtype) for b in bufs], n, {i: i for i in range(n)})


def forward_halves(lands, name):
    n = len(lands)

    def body(*refs):
        ins, outs, (send_sems, recv_sems) = refs[:n], refs[n:2 * n], refs[2 * n:]
        x, y, c = _place()
        sibling = (x, y, 1 - c)
        chips = _other_chips(x, y)
        copies = []
        for i in range(n):
            mine = _half(c, ins[i].shape[1] // 2)
            for j, (px, py) in enumerate(chips):
                cp = _remote(ins[i].at[2 * px + py, mine], outs[i].at[2 * px + py, mine], send_sems, recv_sems, 3 * i + j, sibling)
                cp.start()
                copies.append(cp)
        for i in range(n):
            theirs = _half(1 - c, ins[i].shape[1] // 2)
            for j, (px, py) in enumerate(chips):
                landed = outs[i].at[2 * px + py, theirs]
                _remote(landed, landed, send_sems, recv_sems, 3 * i + j, sibling).wait_recv()
        for cp in copies:
            cp.wait_send()

    return _comm_call(body, name, lands, [_sds(b.shape, b.dtype) for b in lands], 3 * n, {i: i for i in range(n)})


_SEM = pl.BlockSpec(memory_space=pltpu.SEMAPHORE)
_EFFECT = pltpu.SideEffectType.DATAFLOW_SIDE_EFFECTING


def _gather_copies(srcs, lands, send_sems, recv_sems):
    x, y, c = _place()
    copies = []
    for i in range(len(srcs)):
        mine = _half(c, srcs[i].shape[0] // 2)
        for j, chip in enumerate(_other_chips(x, y)):
            copies.append(_remote(srcs[i].at[mine], lands[i].at[2 * x + y, mine], send_sems, recv_sems, 3 * i + j, (*chip, c)))
    return copies


def _exchange_copies(srcs, lands, send_sems, recv_sems):
    x, y, c = _place()
    copies = []
    for i in range(len(srcs)):
        for j, (px, py) in enumerate(_other_chips(x, y)):
            copies.append(_remote(srcs[i].at[2 * px + py], lands[i].at[j], send_sems, recv_sems, 3 * i + j, (px, py, c)))
    return copies


def split_start(name, copies_fn, srcs, land_shapes):
    n, m = len(srcs), len(land_shapes)
    ncopies = 3 * n

    def body(*refs):
        src_refs, land_refs = refs[:n], refs[n:n + m]
        send_sems, recv_sems = refs[n + m], refs[n + m + 1]
        token = refs[-1]
        for cp in copies_fn(src_refs, land_refs, send_sems, recv_sems):
            cp.start()
        token[...] = jnp.zeros_like(token)

    hbm = lambda s: pltpu.HBM(tuple(s.shape), s.dtype)
    outs = pl.pallas_call(
        body, name=name,
        out_shape=(pltpu.SemaphoreType.DMA((ncopies,)), pltpu.SemaphoreType.DMA((ncopies,)), *[hbm(s) for s in srcs],
                   *[hbm(s) for s in land_shapes], _sds((8, LANES))),
        in_specs=[_HBM] * (n + m),
        out_specs=(_SEM, _SEM, *([_HBM] * (n + m)), pl.BlockSpec(memory_space=pltpu.VMEM)),
        input_output_aliases={i: 2 + i for i in range(n + m)},
        compiler_params=pltpu.CompilerParams(has_side_effects=_EFFECT),
    )(*[pltpu.with_memory_space_constraint(s, pltpu.HBM) for s in srcs],
      *[pltpu.with_memory_space_constraint(lax.empty(tuple(s.shape), s.dtype), pltpu.HBM) for s in land_shapes])
    handle = (outs[0], outs[1], list(outs[2:2 + n]), list(outs[2 + n:2 + n + m]))
    return handle, outs[-1][0, 0]


def split_wait(name, copies_fn, handle, after):
    send_sems, recv_sems, srcs, lands = handle
    n, m = len(srcs), len(lands)

    def body(*refs):
        src_refs, land_refs = refs[:n], refs[n:n + m]
        for cp in copies_fn(src_refs, land_refs, refs[n + m], refs[n + m + 1]):
            cp.wait_send()
            cp.wait_recv()

    hbm = lambda s: pltpu.HBM(tuple(s.shape), s.dtype)
    outs = pl.pallas_call(
        body, name=name, out_shape=tuple(hbm(s) for s in srcs + lands),
        in_specs=[_HBM] * (n + m) + [_SEM, _SEM, pl.BlockSpec(memory_space=pl.ANY)], out_specs=tuple([_HBM] * (n + m)),
        input_output_aliases={i: i for i in range(n + m)},
        compiler_params=pltpu.CompilerParams(has_side_effects=_EFFECT),
    )(*srcs, *lands, send_sems, recv_sems, after)
    return list(outs[:n]), list(outs[n:])


_CD_PAD = C_Q_RANK + C_KV_RANK + HEAD_PAD + 2 * _DW


def chip_major(w, groups=N_CHIPS):
    r, c = w.shape
    return w.reshape(r, groups, c // groups).transpose(1, 0, 2)


def from_chip_major(w):
    g, r, c = w.shape
    return w.transpose(1, 0, 2).reshape(r, g * c)


def _cd_in_pad(w):
    a = C_Q_RANK + C_KV_RANK
    z = lambda n: jnp.zeros((w.shape[0], n), w.dtype)
    return jnp.concatenate([w[:, :a], z(C_NOPE), w[:, a:a + C_ROPE], z(HEAD_PAD - C_NOPE - C_ROPE), w[:, a + C_ROPE:]], axis=1)


def _cd_in_unpad(w):
    a = C_Q_RANK + C_KV_RANK
    return jnp.concatenate([w[:, :a], w[:, a + C_NOPE:a + C_NOPE + C_ROPE], w[:, a + HEAD_PAD:]], axis=1)


def _pad_heads(w, width):
    r = w.shape[0]
    w = w.reshape(r, C_HEADS, width)
    return jnp.pad(w, ((0, 0), (0, 0), (0, HEAD_PAD - width))).reshape(r, _HW)


def _unpad_heads(w, width):
    r = w.shape[0]
    return w.reshape(r, C_HEADS, HEAD_PAD)[:, :, :width].reshape(r, C_HEADS * width)


_MATMUL_WEIGHTS = ("ab_w_in", "ab_w_out", "cd_w_in", "c_w_uq", "c_w_ukv", "cd_w_out", "ffn_w_up", "ffn_w_down")
_LAYER_STACKED = ("norm1_g", "norm2_g", "ffn_w_up", "ffn_conv_w", "ffn_w_down")
_ROW_VECTORS = ("b_scale", "c_q_norm_g", "c_kv_norm_g", "d_ln_g", "d_ln_b")


def full_to_local(p):
    q = {}
    for k, v in p.items():
        if k == "final_norm_g":
            v = v.reshape(1, -1)
        elif k not in _LAYER_STACKED and k not in _ROW_VECTORS:
            v = v[0]
        q[k] = v.astype(BF16) if k in _MATMUL_WEIGHTS else v
    return q


def local_to_full(g):
    q = {}
    for k, v in g.items():
        if k == "final_norm_g":
            q[k] = v.reshape(-1)
        elif k not in _LAYER_STACKED and k not in _ROW_VECTORS:
            q[k] = v[None]
        else:
            q[k] = v
    return q


def prepare_weights(p):
    q = dict(p)
    q["cd_w_in"] = _cd_in_pad(p["cd_w_in"])
    q["c_w_uq"] = _pad_heads(p["c_w_uq"], C_NOPE + C_ROPE)
    ukv = p["c_w_ukv"].reshape(C_KV_RANK, C_HEADS, C_NOPE + C_V)
    q["c_w_uk"] = _pad_heads(ukv[:, :, :C_NOPE].reshape(C_KV_RANK, -1), C_NOPE)
    q["c_w_uv"] = _pad_heads(ukv[:, :, C_NOPE:].reshape(C_KV_RANK, -1), C_V)
    wo = p["cd_w_out"]
    att_rows = jnp.pad(wo[:C_HEADS * C_V].reshape(C_HEADS, C_V, D_MODEL), ((0, 0), (0, HEAD_PAD - C_V), (0, 0)))
    q["cd_w_out"] = jnp.concatenate([att_rows.reshape(_HW, D_MODEL), wo[C_HEADS * C_V:]], axis=0)
    return q


def unprepare_grads(g):
    q = dict(g)
    q["cd_w_in"] = _cd_in_unpad(g["cd_w_in"])
    q["c_w_uq"] = _unpad_heads(g["c_w_uq"], C_NOPE + C_ROPE)
    uk = g.pop("c_w_uk").reshape(C_KV_RANK, C_HEADS, HEAD_PAD)[:, :, :C_NOPE]
    uv = g.pop("c_w_uv").reshape(C_KV_RANK, C_HEADS, HEAD_PAD)[:, :, :C_V]
    q.pop("c_w_uk", None)
    q.pop("c_w_uv", None)
    q["c_w_ukv"] = jnp.concatenate([uk, uv], axis=-1).reshape(C_KV_RANK, C_HEADS * (C_NOPE + C_V))
    wo = g["cd_w_out"]
    att = wo[:_HW].reshape(C_HEADS, HEAD_PAD, D_MODEL)[:, :C_V].reshape(C_HEADS * C_V, D_MODEL)
    q["cd_w_out"] = jnp.concatenate([att, wo[_HW:]], axis=0)
    return q


def rope_tables(positions):
    half = C_ROPE // 2
    inv_freq = ROPE_THETA ** (-jnp.arange(half, dtype=F32) / half)
    ang = positions.astype(F32)[:, None] * inv_freq
    cos, sin = jnp.cos(ang), jnp.sin(ang)
    s = positions.shape[0]
    z = lambda n: jnp.zeros((s, n), F32)
    cs = jnp.concatenate([jnp.ones((s, C_NOPE), F32), cos, cos, z(HEAD_PAD - C_NOPE - C_ROPE)], axis=1)
    s1 = jnp.concatenate([z(C_NOPE), -sin, z(HEAD_PAD - C_NOPE - half)], axis=1)
    s2 = jnp.concatenate([z(C_NOPE + half), sin, z(HEAD_PAD - C_NOPE - C_ROPE)], axis=1)
    return cs, s1, s2


def _mods(mod_l):
    return [mod_l[:, i * D_MODEL:(i + 1) * D_MODEL] for i in range(N_MOD)]


def _ffn_fwd(x1, w, l, sc2, sh2, g2):
    n2 = w["norm2_g"][l:l + 1]
    h2 = modnorm_fwd(x1, n2, sc2, sh2, f"modnorm2_fwd{l}")
    up_cols = 2 * D_FF // N_CHIPS
    zf = matmul(h2, w["ffn_w_up"][l], "nn", BF16, f"ffn_up{l}", gb=N_CHIPS, go=2, tn=up_cols)
    a = ffn_act_fwd(zf, w["ffn_conv_w"][l], f"ffn_act_fwd{l}")
    f = matmul(a, w["ffn_w_down"][l], "nn", F32, f"ffn_down{l}", tk=D_FF)
    x2 = resid_fwd(x1, f, g2, f"resid2_fwd{l}")
    return x2, (h2, zf, a, f)


def _ffn_bwd(dres, x1, saved, w, l, sc2, g2):
    h2, zf, a, f = saved
    n2 = w["norm2_g"][l:l + 1]
    df, dg2 = gate_bwd(dres, f, g2, f"gate2_bwd{l}")
    up_cols = 2 * D_FF // N_CHIPS
    da = matmul(df, w["ffn_w_down"][l], "nt", BF16, f"ffn_down_dx{l}")
    d_down = matmul(a, df, "tn", BF16, f"ffn_down_dw{l}", tm=D_FF // 2)
    dzf, d_conv = ffn_act_bwd(zf, w["ffn_conv_w"][l], da, f"ffn_act_bwd{l}")
    dh2 = matmul(dzf, w["ffn_w_up"][l], "nt", F32, f"ffn_up_dx{l}", ga=2, gb=N_CHIPS, tk=up_cols)
    d_up = matmul(h2, dzf, "tn", BF16, f"ffn_up_dw{l}", gb=2, go=N_CHIPS, tn=up_cols)
    dres, dsh2, dsc2, dn2 = norm_bwd(x1, dh2, n2, sc2, dres, f"norm2_bwd{l}")
    d_conv = d_conv.transpose(1, 0, 2).reshape(3, 2 * D_FF)
    return dres, dict(ffn_w_down=d_down, ffn_conv_w=d_conv, ffn_w_up=d_up, norm2_g=dn2), (dsh2, dsc2, dg2)


def mixer0_fwd(x0, sh1, sc1, g1, w):
    h = modnorm_fwd(x0, w["norm1_g"][0:1], sc1, sh1, "modnorm1_fwd0")
    z = matmul(h, w["ab_w_in"], "nn", BF16, "ab_in", gb=N_CHIPS)
    ya = gconv_fwd(z, w["a_conv_w"])
    yb = pool_fwd(z, w["b_mix_w"], w["b_scale"])
    ycat = jnp.concatenate([ya] + yb, axis=1)
    y = matmul(ycat, w["ab_w_out"], "nn", F32, "ab_out")
    x1 = resid_fwd(x0, y, g1, "resid1_fwd0")
    return x1, (x0, h, z, ycat, y, sc1, g1)


def mixer0_bwd(dres, saved, w):
    x0, h, z, ycat, y, sc1, g1 = saved
    grads = {}
    dy, dg1 = gate_bwd(dres, y, g1, "gate1_bwd0")
    dycat = matmul(dy, w["ab_w_out"], "nt", BF16, "ab_out_dx")
    grads["ab_w_out"] = matmul(ycat, dy, "tn", BF16, "ab_out_dw")
    db, dc, da, d_conv = gconv_bwd(z, w["a_conv_w"], dycat)
    pb = pool_bwd(z, w["b_mix_w"], w["b_scale"], dycat)
    dz = jnp.concatenate([db, dc, da] + [t[0] for t in pb], axis=1)
    dh = matmul(dz, w["ab_w_in"], "nt", F32, "ab_in_dx", gb=N_CHIPS)
    grads["ab_w_in"] = matmul(h, dz, "tn", BF16, "ab_in_dw", go=N_CHIPS)
    dres, dsh1, dsc1, dn1 = norm_bwd(x0, dh, w["norm1_g"][0:1], sc1, dres, "norm1_bwd0")
    grads.update(a_conv_w=d_conv, b_mix_w=jnp.stack([t[1] for t in pb]),
                 b_scale=jnp.concatenate([t[2] for t in pb], axis=1), norm1_g=dn1)
    return dres, grads, (dsh1, dsc1, dg1)


def mixer1_fwd(x0, sh1, sc1, g1, ropes, w):
    cs, s1, s2 = ropes
    h = modnorm_fwd(x0, w["norm1_g"][1:2], sc1, sh1, "modnorm1_fwd1")
    z = matmul(h, w["cd_w_in"], "nn", BF16, "cd_in")
    bs_t = jnp.pad(w["d_b_s"].T, ((0, 0), (0, LANES - D_GROUPS)))
    qh, kh, vh = mla_pre_fwd(z, w["c_q_norm_g"], w["c_kv_norm_g"], w["c_w_uq"], w["c_w_uk"], w["c_w_uv"], cs, s1, s2)
    oh = attn_fwd(qh, kh, vh)
    yd = sgu_fwd(z, w["d_ln_g"], w["d_ln_b"], w["d_w_s"], bs_t)
    ycat = jnp.concatenate([oh, yd], axis=1)
    y = matmul(ycat, w["cd_w_out"], "nn", F32, "cd_out")
    x1 = resid_fwd(x0, y, g1, "resid1_fwd1")
    return x1, (x0, h, z, bs_t, qh, kh, vh, oh, ycat, y, sc1, g1)


def mixer1_bwd(dres, saved, ropes, w):
    cs, s1, s2 = ropes
    x0, h, z, bs_t, qh, kh, vh, oh, ycat, y, sc1, g1 = saved
    grads = {}
    dy, dg1 = gate_bwd(dres, y, g1, "gate1_bwd1")
    dycat = matmul(dy, w["cd_w_out"], "nt", BF16, "cd_out_dx")
    grads["cd_w_out"] = matmul(ycat, dy, "tn", F32, "cd_out_dw")
    dqh, dkh, dvh = attn_bwd(qh, kh, vh, oh, dycat, 0)
    dzq, d_uq, d_uk, d_uv, d_gq, d_gkv = mla_pre_bwd(
        z, w["c_q_norm_g"], w["c_kv_norm_g"], w["c_w_uq"], w["c_w_uk"], w["c_w_uv"], cs, s1, s2, dqh, dkh, dvh)
    dzu, dzv, d_ws, d_bs, d_lg, d_lb = sgu_bwd(z, w["d_ln_g"], w["d_ln_b"], w["d_w_s"], bs_t, dycat, _HW // _DW)
    dz = jnp.concatenate([dzq, dzu, dzv], axis=1)
    dh = matmul(dz, w["cd_w_in"], "nt", F32, "cd_in_dx")
    grads["cd_w_in"] = matmul(h, dz, "tn", F32, "cd_in_dw")
    dres, dsh1, dsc1, dn1 = norm_bwd(x0, dh, w["norm1_g"][1:2], sc1, dres, "norm1_bwd1")
    grads.update(c_w_uq=d_uq, c_w_uk=d_uk, c_w_uv=d_uv, c_q_norm_g=d_gq, c_kv_norm_g=d_gkv, d_w_s=d_ws,
                 d_b_s=d_bs[:, :D_GROUPS].T, d_ln_g=d_lg, d_ln_b=d_lb, norm1_g=dn1)
    return dres, grads, (dsh1, dsc1, dg1)


_PER_LAYER = ("ffn_w_down", "ffn_conv_w", "ffn_w_up", "norm2_g", "norm1_g")


def _merge_layer_grads(g0, g1):
    grads = {k: v for k, v in g0.items() if k not in _PER_LAYER}
    grads.update({k: v for k, v in g1.items() if k not in _PER_LAYER})
    for k in ("ffn_w_down", "ffn_w_up"):
        grads[k] = [g0[k], g1[k]]
    grads["ffn_conv_w"] = jnp.stack([g0["ffn_conv_w"], g1["ffn_conv_w"]])
    grads["norm1_g"] = jnp.concatenate([g0["norm1_g"], g1["norm1_g"]], axis=0)
    grads["norm2_g"] = jnp.concatenate([g0["norm2_g"], g1["norm2_g"]], axis=0)
    return grads


def local_step(x, tgt, mod, ropes, w):
    sh1a, sc1a, g1a, sh2a, sc2a, g2a = _mods(mod[0:1])
    sh1b, sc1b, g1b, sh2b, sc2b, g2b = _mods(mod[1:2])
    x1, mix0 = mixer0_fwd(x, sh1a, sc1a, g1a, w)
    x2, ffn0 = _ffn_fwd(x1, w, 0, sc2a, sh2a, g2a)
    x3, mix1 = mixer1_fwd(x2, sh1b, sc1b, g1b, ropes, w)
    x4, ffn1 = _ffn_fwd(x3, w, 1, sc2b, sh2b, g2b)
    dres, d_final, loss = final_fwd_bwd(x4, w["final_norm_g"], tgt)
    dres, gf1, dm2b = _ffn_bwd(dres, x3, ffn1, w, 1, sc2b, g2b)
    dres, gm1, dm1b = mixer1_bwd(dres, mix1, ropes, w)
    dres, gf0, dm2a = _ffn_bwd(dres, x1, ffn0, w, 0, sc2a, g2a)
    dres, gm0, dm1a = mixer0_bwd(dres, mix0, w)
    grads = _merge_layer_grads({**gf0, **gm0}, {**gf1, **gm1})
    grads["final_norm_g"] = d_final
    dmod = jnp.concatenate([jnp.concatenate(dm1a + dm2a, axis=1), jnp.concatenate(dm1b + dm2b, axis=1)], axis=0)
    return loss, dres, dmod, grads


_WEIGHTS = ("ada_w", "ada_b", "norm1_g", "norm2_g", "ab_w_in", "a_conv_w", "b_mix_w", "b_scale", "ab_w_out", "cd_w_in",
            "c_q_norm_g", "c_w_uq", "c_kv_norm_g", "c_w_ukv", "d_ln_g", "d_ln_b", "d_w_s", "d_b_s", "cd_w_out",
            "ffn_w_up", "ffn_conv_w", "ffn_w_down", "final_norm_g")
_INPUTS = ("x", "c", "positions") + _WEIGHTS + ("loss_target",) + tuple("m_" + n for n in _WEIGHTS) + tuple(
    "v_" + n for n in _WEIGHTS)

def _pack_rows(parts, rows, dtype):
    flat = jnp.concatenate([p.reshape(-1).astype(dtype) for p in parts])
    return jnp.pad(flat, (0, rows * LANES - flat.shape[0])).reshape(rows, LANES)


def _rows_major(w):
    r, c = w.shape
    return w.reshape(N_CHIPS, r // N_CHIPS, c)


def order_after(v, name):
    def body(v_ref, o_ref):
        del v_ref
        o_ref[...] = jnp.zeros_like(o_ref)

    return pl.pallas_call(
        body, name=name, out_shape=_sds((8, LANES)), in_specs=[pl.BlockSpec(memory_space=pl.ANY)],
        out_specs=pl.BlockSpec(memory_space=pltpu.VMEM),
    )(v)[0, 0]


def start_gather(shards, tag):
    lands = [_sds((N_CHIPS,) + s.shape, s.dtype) for s in shards]
    return split_start("gather_start_" + tag, _gather_copies, shards, lands)


def finish_gather(handle, chip, tag, after):
    shards, lands = split_wait("gather_wait_" + tag, _gather_copies, handle, after)
    lands = forward_halves(lands, "gather_forward_" + tag)
    return [lax.dynamic_update_index_in_dim(o, s, chip, 0) for o, s in zip(lands, shards)]


def start_reduce(gs, core, tag):
    recv = swap_halves(gs, "swap_halves_" + tag)
    pairs = [pair_sum(g, r, core, f"pair_sum_{tag}{i}") for i, (g, r) in enumerate(zip(gs, recv))]
    lands = [_sds((N_CHIPS - 1,) + p.shape[1:], p.dtype) for p in pairs]
    return split_start("exchange_start_" + tag, _exchange_copies, pairs, lands)


def finish_reduce(handle, chip, core, tag, after):
    pairs, others = split_wait("exchange_wait_" + tag, _exchange_copies, handle, after)
    halves = [chip_sum(p, o, chip, core, f"chip_sum_{tag}{i}") for i, (p, o) in enumerate(zip(pairs, others))]
    full = join_halves(halves, "join_halves_" + tag)
    return [f.reshape(f.shape[1] * 2, f.shape[2]) for f in full]


_SMALL_SHARDED = (("a_conv_w", (3, 128), 1), ("c_q_norm_g", (1, 64), 1), ("d_ln_g", (1, 128), 1), ("d_ln_b", (1, 128), 1),
                  ("ffn_conv_w", (2, 3, 2 * D_FF // N_CHIPS), 2))
_SMALL_GRADS = (("norm1_g", (2, D_MODEL)), ("norm2_g", (2, D_MODEL)), ("b_mix_w", (4, 128, 128)), ("b_scale", (1, 512)),
                ("c_kv_norm_g", (1, 128)), ("d_w_s", (4, 128, 128)), ("d_b_s", (4, 128)), ("final_norm_g", (1, D_MODEL)),
                ("a_conv_w", (3, 512)), ("c_q_norm_g", (1, 256)), ("d_ln_g", (1, 512)), ("d_ln_b", (1, 512)),
                ("ffn_conv_w", (2, 3, 2 * D_FF)))


def _size(shape):
    n = 1
    for d in shape:
        n *= d
    return n


def kernel(x, c, positions, ada_w, ada_b, norm1_g, norm2_g, ab_w_in, a_conv_w, b_mix_w, b_scale, ab_w_out, cd_w_in, c_q_norm_g, c_w_uq, c_kv_norm_g, c_w_ukv, d_ln_g, d_ln_b, d_w_s, d_b_s, cd_w_out, ffn_w_up, ffn_conv_w, ffn_w_down, final_norm_g, loss_target, m_ada_w, m_ada_b, m_norm1_g, m_norm2_g, m_ab_w_in, m_a_conv_w, m_b_mix_w, m_b_scale, m_ab_w_out, m_cd_w_in, m_c_q_norm_g, m_c_w_uq, m_c_kv_norm_g, m_c_w_ukv, m_d_ln_g, m_d_ln_b, m_d_w_s, m_d_b_s, m_cd_w_out, m_ffn_w_up, m_ffn_conv_w, m_ffn_w_down, m_final_norm_g, v_ada_w, v_ada_b, v_norm1_g, v_norm2_g, v_ab_w_in, v_a_conv_w, v_b_mix_w, v_b_scale, v_ab_w_out, v_cd_w_in, v_c_q_norm_g, v_c_w_uq, v_c_kv_norm_g, v_c_w_ukv, v_d_ln_g, v_d_ln_b, v_d_w_s, v_d_b_s, v_cd_w_out, v_ffn_w_up, v_ffn_conv_w, v_ffn_w_down, v_final_norm_g):
    args = (x, c, positions, ada_w, ada_b, norm1_g, norm2_g, ab_w_in, a_conv_w, b_mix_w, b_scale, ab_w_out, cd_w_in, c_q_norm_g, c_w_uq, c_kv_norm_g, c_w_ukv, d_ln_g, d_ln_b, d_w_s, d_b_s, cd_w_out, ffn_w_up, ffn_conv_w, ffn_w_down, final_norm_g, loss_target, m_ada_w, m_ada_b, m_norm1_g, m_norm2_g, m_ab_w_in, m_a_conv_w, m_b_mix_w, m_b_scale, m_ab_w_out, m_cd_w_in, m_c_q_norm_g, m_c_w_uq, m_c_kv_norm_g, m_c_w_ukv, m_d_ln_g, m_d_ln_b, m_d_w_s, m_d_b_s, m_cd_w_out, m_ffn_w_up, m_ffn_conv_w, m_ffn_w_down, m_final_norm_g, v_ada_w, v_ada_b, v_norm1_g, v_norm2_g, v_ab_w_in, v_a_conv_w, v_b_mix_w, v_b_scale, v_ab_w_out, v_cd_w_in, v_c_q_norm_g, v_c_w_uq, v_c_kv_norm_g, v_c_w_ukv, v_d_ln_g, v_d_ln_b, v_d_w_s, v_d_b_s, v_cd_w_out, v_ffn_w_up, v_ffn_conv_w, v_ffn_w_down, v_final_norm_g)
    a = dict(zip(_INPUTS, args, strict=True))
    xi, yi, ci = _place()
    chip = 2 * xi + yi
    dev = 4 * xi + 2 * yi + ci
    x = a["x"][0]
    tgt = a["loss_target"][0]

    bf = lambda t: t.astype(BF16)
    mix0_handle, tok = start_gather([bf(a["ab_w_in"][0]), bf(a["ab_w_out"][0]), bf(a["ffn_w_up"][0])], "mix0")

    small_parts = [a["c"] + tok] + [a[n] for n, _, _ in _SMALL_SHARDED]
    rows1 = -(-sum(p.size for p in small_parts) // LANES // 8) * 8
    g1 = all_gather8(_pack_rows(small_parts, rows1, F32), "gather_small").reshape(N_DEV, rows1 * LANES)
    c_all = g1[:, :D_MODEL]
    per_chip = g1[0::2]
    small_full = {}
    off = D_MODEL
    for n, shp, axis in _SMALL_SHARDED:
        piece = per_chip[:, off:off + _size(shp)].reshape((N_CHIPS,) + shp)
        small_full[n] = jnp.concatenate([piece[k] for k in range(N_CHIPS)], axis=axis)
        off += _size(shp)

    merge = lambda t: t.reshape(t.shape[0] * t.shape[1], t.shape[2])
    w = dict(norm1_g=a["norm1_g"], norm2_g=a["norm2_g"], b_mix_w=a["b_mix_w"][0], b_scale=a["b_scale"],
             c_kv_norm_g=a["c_kv_norm_g"], d_w_s=a["d_w_s"][0], d_b_s=a["d_b_s"][0],
             final_norm_g=a["final_norm_g"].reshape(1, D_MODEL), **small_full)

    ncol = N_MOD * D_MODEL // N_CHIPS
    ada_b_mine = lax.dynamic_slice_in_dim(a["ada_b"], chip * ncol, ncol, axis=1)
    mod_cols = ada_mod(c_all, a["ada_w"], ada_b_mine)
    g2 = all_gather8(mod_cols.reshape(-1, LANES), "gather_mod").reshape(N_DEV, 2, N_DEV, ncol)
    mod = lax.dynamic_index_in_dim(g2[0::2], dev, axis=2, keepdims=False)
    mod = mod.transpose(1, 0, 2).reshape(2, N_MOD * D_MODEL)

    late = order_after(mod, "after_mod")
    bf_late = lambda t: (t + late).astype(BF16)
    ffn0_handle, tok_b = start_gather([bf_late(a["ffn_w_down"][0])], "ffn0")
    mix1_handle, tok_c = start_gather(
        [bf_late(a["cd_w_in"][0]), bf_late(a["c_w_uq"][0]), bf_late(a["c_w_ukv"][0]), bf_late(a["cd_w_out"][0])], "mix1")
    ffn1_handle, tok_d = start_gather([bf_late(a["ffn_w_up"][1]), bf_late(a["ffn_w_down"][1])], "ffn1")
    mod = mod + (tok_b + tok_c + tok_d)

    ropes = rope_tables(a["positions"][0])
    cm16 = lambda t: chip_major(t).astype(BF16)
    sh1a, sc1a, g1a, sh2a, sc2a, g2a = _mods(mod[0:1])
    sh1b, sc1b, g1b, sh2b, sc2b, g2b = _mods(mod[1:2])

    w_in0, w_out0, up0 = finish_gather(mix0_handle, chip, "mix0", mod)
    w.update(ab_w_in=w_in0, ab_w_out=merge(w_out0))
    x1, mix0 = mixer0_fwd(x, sh1a, sc1a, g1a, w)
    down0, = finish_gather(ffn0_handle, chip, "ffn0", x1)
    w.update(ffn_w_up=[up0, None], ffn_w_down=[merge(down0), None])
    x2, ffn0 = _ffn_fwd(x1, w, 0, sc2a, sh2a, g2a)
    cd_in, uq, ukv, cd_out = finish_gather(mix1_handle, chip, "mix1", x2)
    w.update(prepare_weights(dict(cd_w_in=from_chip_major(cd_in), c_w_uq=from_chip_major(uq), c_w_ukv=from_chip_major(ukv),
                                  cd_w_out=merge(cd_out))))
    x3, mix1 = mixer1_fwd(x2, sh1b, sc1b, g1b, ropes, w)
    up1, down1 = finish_gather(ffn1_handle, chip, "ffn1", x3)
    w.update(ffn_w_up=[up0, up1], ffn_w_down=[merge(down0), merge(down1)])
    x4, ffn1 = _ffn_fwd(x3, w, 1, sc2b, sh2b, g2b)
    dres, d_final, loss = final_fwd_bwd(x4, w["final_norm_g"], tgt)

    dres, gf1, dm2b = _ffn_bwd(dres, x3, ffn1, w, 1, sc2b, g2b)
    ffn1_red, tok = start_reduce([gf1["ffn_w_up"], _rows_major(gf1["ffn_w_down"])], ci, "ffn1")
    dres, gm1, dm1b = mixer1_bwd(dres, mix1[:-1] + (mix1[-1] + tok,), ropes, w)
    gm1 = unprepare_grads(gm1)
    mix1_red, tok = start_reduce([cm16(gm1["cd_w_in"]), cm16(gm1["c_w_uq"]), cm16(gm1["c_w_ukv"]),
                                  _rows_major(gm1["cd_w_out"]).astype(BF16)], ci, "mix1")
    red_up1, red_down1 = finish_reduce(ffn1_red, chip, ci, "ffn1", dres)
    dres, gf0, dm2a = _ffn_bwd(dres, x1, ffn0, w, 0, sc2a, g2a + tok)
    ffn0_red, tok = start_reduce([gf0["ffn_w_up"], _rows_major(gf0["ffn_w_down"])], ci, "ffn0")
    red_cd_in, red_uq, red_ukv, red_cd_out = finish_reduce(mix1_red, chip, ci, "mix1", dres)
    grad_x, gm0, dm1a = mixer0_bwd(dres, mix0[:-1] + (mix0[-1] + tok,), w)
    grads = _merge_layer_grads({**gf0, **gm0}, {**gf1, **gm1})
    grads["final_norm_g"] = d_final
    dmod = jnp.concatenate([jnp.concatenate(dm1a + dm2a, axis=1), jnp.concatenate(dm1b + dm2b, axis=1)], axis=0)

    parts3 = [dmod] + [grads[n] for n, _ in _SMALL_GRADS] + [loss[0, 0]]
    rows3 = -(-sum(p.size for p in parts3) // LANES // 8) * 8
    g3 = all_gather8(_pack_rows(parts3, rows3, F32), "gather_grads").reshape(N_DEV, rows3, LANES)
    late = order_after(g3, "after_gather_grads").astype(BF16)
    mix0_red, _ = start_reduce([gm0["ab_w_in"], _rows_major(gm0["ab_w_out"]) + late], ci, "mix0")
    summed = sum8(g3).reshape(-1)
    nmod = 2 * N_MOD * D_MODEL
    out_grads = {"ada_b": summed[:nmod].reshape(2, N_MOD * D_MODEL)}
    off = nmod
    for n, shp in _SMALL_GRADS:
        out_grads[n] = summed[off:off + _size(shp)].reshape(shp)
        off += _size(shp)
    loss = summed[off]
    for n, shp, axis in _SMALL_SHARDED:
        width = out_grads[n].shape[-1] // N_CHIPS
        out_grads[n] = lax.dynamic_slice_in_dim(out_grads[n], chip * width, width, axis=out_grads[n].ndim - 1)
    dmod_all = g3.reshape(N_DEV, rows3 * LANES)[:, :nmod].reshape(N_DEV, 2, N_MOD * D_MODEL)
    dmod_mine = lax.dynamic_slice_in_dim(dmod_all, chip * ncol, ncol, axis=2).transpose(1, 0, 2)
    out_grads["ada_w"] = ada_grad(c_all.T, dmod_mine)

    red_up0, red_down0 = finish_reduce(ffn0_red, chip, ci, "ffn0", summed)
    red_in0, red_out0 = finish_reduce(mix0_red, chip, ci, "mix0", out_grads["ada_w"])
    out_grads.update(ab_w_in=red_in0, ab_w_out=red_out0, cd_w_in=red_cd_in, c_w_uq=red_uq, c_w_ukv=red_ukv,
                     cd_w_out=red_cd_out)
    per_layer = dict(ffn_w_up=(red_up0, red_up1), ffn_w_down=(red_down0, red_down1))

    g_out, d_out, m_out, v_out = [], [], [], []
    for n in _WEIGHTS:
        if n in per_layer:
            g, d, nm, nv = adamw_layers(a[n], *per_layer[n], a["m_" + n], a["v_" + n], "adamw_" + n)
        else:
            g, d, nm, nv = adamw(a[n], out_grads[n].reshape(a[n].shape), a["m_" + n], a["v_" + n], "adamw_" + n)
        g_out.append(g)
        d_out.append(d)
        m_out.append(nm)
        v_out.append(nv)
    return (loss, grad_x[None], *g_out, *d_out, *m_out, *v_out)
```

```python
import functools

import jax
import jax.numpy as jnp
from jax import lax
from jax.experimental import pallas as pl
from jax.experimental.pallas import tpu as pltpu

F32 = jnp.float32
BF16 = jnp.bfloat16
EPS = 1e-6
D_MODEL = 1024
N_MOD = 6
A_WIDTH = 512
B_GROUPS = 4
POOL_WINDOWS = (2, 4, 8, 16)
C_HEADS = 8
C_NOPE = 64
C_ROPE = 32
C_V = 64
C_Q_RANK = 256
C_KV_RANK = 128
HEAD_PAD = 128
ROPE_THETA = 10000.0
D_GROUPS = 4
D_CHUNK = 128
D_FF = 2816
FF_UNIT = 128
ADAM_LR = 0.001
ADAM_B1 = 0.9
ADAM_B2 = 0.999
ADAM_EPS = 1e-08
ADAM_WD = 0.01
ADAM_STEP = 10
N_CHIPS = 4
N_DEV = 8
LANES = 128
VMEM_BIG = 56 * 1024 * 1024
MESH = pl.DeviceIdType.MESH


def _sds(shape, dtype=F32):
    return jax.ShapeDtypeStruct(tuple(shape), dtype)


def _tile(n, cap, mult=128):
    if n <= cap:
        return n
    best = None
    for t in range(mult, cap + 1, mult):
        if n % t == 0:
            best = t
    assert best is not None, (n, cap, mult)
    return best


def _params(dims=None, vmem=None):
    return pltpu.CompilerParams(dimension_semantics=dims, vmem_limit_bytes=vmem)


def _shift_down(v, k):
    r = pltpu.roll(v, k, axis=0)
    t = lax.broadcasted_iota(jnp.int32, v.shape, 0)
    return jnp.where(t >= k, r, 0.0)


def _shift_up(v, k):
    n = v.shape[0]
    r = pltpu.roll(v, n - k, axis=0)
    t = lax.broadcasted_iota(jnp.int32, v.shape, 0)
    return jnp.where(t < n - k, r, 0.0)


def _sigmoid(v):
    return 1.0 / (1.0 + jnp.exp(-v))


_GELU_C = 0.7978845608028654
_GELU_A = 0.044715


def _gelu(v):
    return 0.5 * v * (1.0 + jnp.tanh(_GELU_C * (v + _GELU_A * v * v * v)))


def _gelu_grad(v):
    th = jnp.tanh(_GELU_C * (v + _GELU_A * v * v * v))
    return 0.5 * (1.0 + th) + 0.5 * v * (1.0 - th * th) * _GELU_C * (1.0 + 3.0 * _GELU_A * v * v)


_NN = (((1,), (0,)), ((), ()))
_NT = (((1,), (1,)), ((), ()))
_TN = (((0,), (0,)), ((), ()))


def _dot(a, b, dims=_NN):
    return lax.dot_general(a, b, dims, preferred_element_type=F32)


def _logical(t, groups):
    return (t.shape[-2], t.shape[-1] * groups)


def _block(tr, tc, groups, cols, where):
    if groups == 1:
        return pl.BlockSpec((tr, tc), where)
    per = cols // groups // tc

    def index(i, j, s):
        r, c = where(i, j, s)
        return (c // per, r, c % per)

    return pl.BlockSpec((None, tr, tc), index)


def matmul(a, b, mode, out_dtype, name, ga=1, gb=1, go=1, tm=None, tn=None, tk=None):
    (ar, ac), (br, bc) = _logical(a, ga), _logical(b, gb)
    if mode == "nn":
        m, k, n = ar, ac, bc
        a_col, b_col = "k", "n"
    elif mode == "nt":
        m, k, n = ar, ac, br
        a_col, b_col = "k", "k"
    else:
        k, m, n = ar, ac, bc
        a_col, b_col = "m", "n"
    limit = {"m": m, "n": n // go, "k": k}
    limit[a_col] = min(limit[a_col], ac // ga)
    limit[b_col] = min(limit[b_col], bc // gb)
    tm = tm or _tile(limit["m"], 1024, 128 if mode == "tn" else 16)
    tn = tn or _tile(limit["n"], 512)
    tk = tk or _tile(limit["k"], 2048, 16 if mode == "tn" else 128)
    nk = k // tk
    if mode == "nn":
        a_spec = _block(tm, tk, ga, ac, lambda i, j, s: (i, s))
        b_spec = _block(tk, tn, gb, bc, lambda i, j, s: (s, j))
        dims = _NN
    elif mode == "nt":
        a_spec = _block(tm, tk, ga, ac, lambda i, j, s: (i, s))
        b_spec = _block(tn, tk, gb, bc, lambda i, j, s: (j, s))
        dims = _NT
    else:
        a_spec = _block(tk, tm, ga, ac, lambda i, j, s: (s, i))
        b_spec = _block(tk, tn, gb, bc, lambda i, j, s: (s, j))
        dims = _TN
    o_spec = _block(tm, tn, go, n, lambda i, j, s: (i, j))
    out_shape = _sds((m, n), out_dtype) if go == 1 else _sds((go, m, n // go), out_dtype)

    def body(a_ref, b_ref, o_ref, acc_ref):
        s = pl.program_id(2)

        @pl.when(s == 0)
        def _():
            acc_ref[...] = jnp.zeros_like(acc_ref)

        acc_ref[...] += _dot(a_ref[...], b_ref[...], dims)

        @pl.when(s == nk - 1)
        def _():
            o_ref[...] = acc_ref[...].astype(o_ref.dtype)

    return pl.pallas_call(
        body, name=name, out_shape=out_shape, grid=(m // tm, n // tn, nk),
        in_specs=[a_spec, b_spec], out_specs=o_spec,
        scratch_shapes=[pltpu.VMEM((tm, tn), F32)],
        compiler_params=_params(("parallel", "parallel", "arbitrary"), VMEM_BIG),
    )(a, b)


def _rows(tm, n):
    return pl.BlockSpec((tm, n), lambda i: (i, 0))


def _vec(n):
    return pl.BlockSpec((1, n), lambda i: (0, 0))


def modnorm_fwd(x, g, sc, sh, name):
    s, d = x.shape
    tm = _tile(s, 256, 8)

    def body(x_ref, g_ref, sc_ref, sh_ref, o_ref):
        xv = x_ref[...]
        r = lax.rsqrt(jnp.mean(xv * xv, axis=-1, keepdims=True) + EPS)
        o_ref[...] = ((xv * r) * g_ref[...] * (1.0 + sc_ref[...]) + sh_ref[...]).astype(BF16)

    return pl.pallas_call(
        body, name=name, out_shape=_sds((s, d), BF16), grid=(s // tm,),
        in_specs=[_rows(tm, d), _vec(d), _vec(d), _vec(d)], out_specs=_rows(tm, d),
        compiler_params=_params(("parallel",)),
    )(x, g, sc, sh)


def resid_fwd(x, y, gate, name):
    s, d = x.shape
    tm = _tile(s, 256, 8)

    def body(x_ref, y_ref, g_ref, o_ref):
        o_ref[...] = x_ref[...] + g_ref[...] * y_ref[...]

    return pl.pallas_call(
        body, name=name, out_shape=_sds((s, d)), grid=(s // tm,),
        in_specs=[_rows(tm, d), _rows(tm, d), _vec(d)], out_specs=_rows(tm, d),
        compiler_params=_params(("parallel",)),
    )(x, y, gate)


def gate_bwd(dres, y, gate, name):
    s, d = dres.shape
    tm = _tile(s, 256, 8)

    def body(dr_ref, y_ref, g_ref, dy_ref, dg_ref):
        @pl.when(pl.program_id(0) == 0)
        def _():
            dg_ref[...] = jnp.zeros_like(dg_ref)

        dr = dr_ref[...]
        dy_ref[...] = (dr * g_ref[...]).astype(BF16)
        dg_ref[...] += jnp.sum(dr * y_ref[...], axis=0, keepdims=True)

    return pl.pallas_call(
        body, name=name, out_shape=(_sds((s, d), BF16), _sds((1, d))), grid=(s // tm,),
        in_specs=[_rows(tm, d), _rows(tm, d), _vec(d)], out_specs=(_rows(tm, d), _vec(d)),
        compiler_params=_params(("arbitrary",)),
    )(dres, y, gate)


def norm_bwd(x, dh, g, sc, dres, name):
    s, d = x.shape
    tm = _tile(s, 256, 8)
    nsteps = s // tm

    def body(x_ref, dh_ref, g_ref, sc_ref, dr_ref, dx_ref, dsh_ref, dsc_ref, dg_ref, a2_ref):
        i = pl.program_id(0)

        @pl.when(i == 0)
        def _():
            dsh_ref[...] = jnp.zeros_like(dsh_ref)
            a2_ref[...] = jnp.zeros_like(a2_ref)

        xv = x_ref[...]
        dh = dh_ref[...]
        r = lax.rsqrt(jnp.mean(xv * xv, axis=-1, keepdims=True) + EPS)
        xh = xv * r
        dsh_ref[...] += jnp.sum(dh, axis=0, keepdims=True)
        a2_ref[...] += jnp.sum(dh * xh, axis=0, keepdims=True)
        dxh = dh * (g_ref[...] * (1.0 + sc_ref[...]))
        dx = r * (dxh - xh * jnp.mean(dxh * xh, axis=-1, keepdims=True))
        dx_ref[...] = dr_ref[...] + dx

        @pl.when(i == nsteps - 1)
        def _():
            dsc_ref[...] = a2_ref[...] * g_ref[...]
            dg_ref[...] = a2_ref[...] * (1.0 + sc_ref[...])

    return pl.pallas_call(
        body, name=name, out_shape=(_sds((s, d)), _sds((1, d)), _sds((1, d)), _sds((1, d))), grid=(nsteps,),
        in_specs=[_rows(tm, d), _rows(tm, d), _vec(d), _vec(d), _rows(tm, d)],
        out_specs=(_rows(tm, d), _vec(d), _vec(d), _vec(d)),
        scratch_shapes=[pltpu.VMEM((1, d), F32)],
        compiler_params=_params(("arbitrary",)),
    )(x, dh, g, sc, dres)


def final_fwd_bwd(x, g, tgt):
    s, d = x.shape
    tm = _tile(s, 256, 8)

    def body(x_ref, g_ref, t_ref, dx_ref, dg_ref, loss_ref):
        @pl.when(pl.program_id(0) == 0)
        def _():
            dg_ref[...] = jnp.zeros_like(dg_ref)
            loss_ref[...] = jnp.zeros_like(loss_ref)

        xv = x_ref[...]
        gv = g_ref[...]
        r = lax.rsqrt(jnp.mean(xv * xv, axis=-1, keepdims=True) + EPS)
        xh = xv * r
        e = xh * gv - t_ref[...]
        row = jnp.sum(e * e, axis=-1, keepdims=True) * (0.5 / d)
        loss_ref[...] += jnp.sum(row, axis=0, keepdims=True)
        dy = e * (1.0 / d)
        dg_ref[...] += jnp.sum(dy * xh, axis=0, keepdims=True)
        dxh = dy * gv
        dx_ref[...] = r * (dxh - xh * jnp.mean(dxh * xh, axis=-1, keepdims=True))

    return pl.pallas_call(
        body, name="final_fwd_bwd", out_shape=(_sds((s, d)), _sds((1, d)), _sds((1, LANES))), grid=(s // tm,),
        in_specs=[_rows(tm, d), _vec(d), _rows(tm, d)], out_specs=(_rows(tm, d), _vec(d), _vec(LANES)),
        compiler_params=_params(("arbitrary",)),
    )(x, g, tgt)


def _taps(v):
    return _shift_down(v, 2), _shift_down(v, 1), v


def _conv3_taps(taps, w):
    return w[0:1, :] * taps[0] + w[1:2, :] * taps[1] + w[2:3, :] * taps[2]


def _conv3(v, w):
    return _conv3_taps(_taps(v), w)


def _conv3_t(dv, w):
    return w[0:1, :] * _shift_up(dv, 2) + w[1:2, :] * _shift_up(dv, 1) + w[2:3, :] * dv


def _conv3_dw_taps(dv, taps):
    return jnp.concatenate([jnp.sum(dv * t, axis=0, keepdims=True) for t in taps], axis=0)


def _conv3_dw(dv, v):
    return _conv3_dw_taps(dv, _taps(v))


def gconv_fwd(z, conv_w):
    s = z.shape[0]
    nb = A_WIDTH // LANES

    def body(b_ref, c_ref, a_ref, w_ref, o_ref):
        b, c, a = b_ref[...].astype(F32), c_ref[...].astype(F32), a_ref[...].astype(F32)
        o_ref[...] = (b * _conv3(c * a, w_ref[...])).astype(BF16)

    col = lambda off: pl.BlockSpec((s, LANES), lambda j: (0, off + j))
    return pl.pallas_call(
        body, name="gconv_fwd", out_shape=_sds((s, A_WIDTH), BF16), grid=(nb,),
        in_specs=[col(0), col(nb), col(2 * nb), pl.BlockSpec((3, LANES), lambda j: (0, j))],
        out_specs=pl.BlockSpec((s, LANES), lambda j: (0, j)),
        compiler_params=_params(("parallel",), VMEM_BIG),
    )(z, z, z, conv_w)


def gconv_bwd(z, conv_w, dycat):
    s = z.shape[0]
    nb = A_WIDTH // LANES

    def body(b_ref, c_ref, a_ref, w_ref, dy_ref, db_ref, dc_ref, da_ref, dw_ref):
        c, a, w, dy = c_ref[...].astype(F32), a_ref[...].astype(F32), w_ref[...], dy_ref[...].astype(F32)
        ca = c * a
        db_ref[...] = (dy * _conv3(ca, w)).astype(BF16)
        dconv = dy * b_ref[...].astype(F32)
        dw_ref[...] = _conv3_dw(dconv, ca)
        dca = _conv3_t(dconv, w)
        dc_ref[...] = (dca * a).astype(BF16)
        da_ref[...] = (dca * c).astype(BF16)

    col = lambda off: pl.BlockSpec((s, LANES), lambda j: (0, off + j))
    wspec = pl.BlockSpec((3, LANES), lambda j: (0, j))
    part = _sds((s, A_WIDTH), BF16)
    return pl.pallas_call(
        body, name="gconv_bwd", out_shape=(part, part, part, _sds((3, A_WIDTH))), grid=(nb,),
        in_specs=[col(0), col(nb), col(2 * nb), wspec, col(0)],
        out_specs=(col(0), col(0), col(0), wspec),
        compiler_params=_params(("parallel",), VMEM_BIG),
    )(z, z, z, conv_w, dycat)


def _pool_counts(s, w):
    t = lax.broadcasted_iota(jnp.int32, (s, 1), 0)
    return jnp.minimum(t + 1, w).astype(F32)


def _pooled(p, levels):
    acc = p
    for lv in range(levels):
        acc = acc + _shift_down(acc, 2 ** lv)
    return acc / _pool_counts(p.shape[0], 2 ** levels) - p


def pool_fwd(z, mix_w, scale):
    s = z.shape[0]

    def make(g):
        def body_g(p_ref, m_ref, sc_ref, o_ref):
            pooled = _pooled(p_ref[...].astype(F32), g + 1)
            y = _dot(pooled.astype(BF16), m_ref[...].astype(BF16))
            o_ref[...] = (y * sc_ref[...]).astype(BF16)
        return body_g

    outs = []
    for g in range(B_GROUPS):
        outs.append(pl.pallas_call(
            make(g), name=f"pool_fwd{g}", out_shape=_sds((s, LANES), BF16), grid=(1,),
            in_specs=[pl.BlockSpec((s, LANES), lambda i, g=g: (0, 3 * (A_WIDTH // LANES) + g)),
                      pl.BlockSpec((None, LANES, LANES), lambda i, g=g: (g, 0, 0)),
                      pl.BlockSpec((1, LANES), lambda i, g=g: (0, g))],
            out_specs=pl.BlockSpec((s, LANES), lambda i: (0, 0)),
            compiler_params=_params(("arbitrary",), VMEM_BIG),
        )(z, mix_w, scale))
    return outs


def pool_bwd(z, mix_w, scale, dycat):
    s = z.shape[0]

    def make(g):
        w = 2 ** (g + 1)

        def body_g(p_ref, m_ref, sc_ref, dy_ref, dp_ref, dm_ref, dsc_ref):
            pooled = _pooled(p_ref[...].astype(F32), g + 1)
            mw = m_ref[...].astype(BF16)
            pb = pooled.astype(BF16)
            dy = dy_ref[...].astype(F32)
            dsc_ref[...] = jnp.sum(dy * _dot(pb, mw), axis=0, keepdims=True)
            dmix = (dy * sc_ref[...]).astype(BF16)
            dm_ref[...] = _dot(pb, dmix, _TN)
            dpool = _dot(dmix, mw, _NT)
            acc = dpool / _pool_counts(s, w)
            for lv in range(g + 1):
                acc = acc + _shift_up(acc, 2 ** lv)
            dp_ref[...] = (acc - dpool).astype(BF16)
        return body_g

    outs = []
    for g in range(B_GROUPS):
        outs.append(pl.pallas_call(
            make(g), name=f"pool_bwd{g}",
            out_shape=(_sds((s, LANES), BF16), _sds((LANES, LANES)), _sds((1, LANES))), grid=(1,),
            in_specs=[pl.BlockSpec((s, LANES), lambda i, g=g: (0, 3 * (A_WIDTH // LANES) + g)),
                      pl.BlockSpec((None, LANES, LANES), lambda i, g=g: (g, 0, 0)),
                      pl.BlockSpec((1, LANES), lambda i, g=g: (0, g)),
                      pl.BlockSpec((s, LANES), lambda i, g=g: (0, A_WIDTH // LANES + g))],
            out_specs=(pl.BlockSpec((s, LANES), lambda i: (0, 0)), pl.BlockSpec((LANES, LANES), lambda i: (0, 0)),
                       pl.BlockSpec((1, LANES), lambda i: (0, 0))),
            compiler_params=_params(("arbitrary",), VMEM_BIG),
        )(z, mix_w, scale, dycat))
    return outs


_FF_BLOCKS = D_FF // FF_UNIT


def _ff_spec(s):
    return pl.BlockSpec((2, s, FF_UNIT), lambda j: (0, 0, j))


def _ff_wspecs():
    return [pl.BlockSpec((3, FF_UNIT), lambda j: (0, j)), pl.BlockSpec((3, FF_UNIT), lambda j: (0, _FF_BLOCKS + j))]


def ffn_act_fwd(zf, conv_w, name):
    s = zf.shape[1]

    def body(z_ref, wg_ref, wu_ref, o_ref):
        g = _conv3(z_ref[0].astype(F32), wg_ref[...])
        u = _conv3(z_ref[1].astype(F32), wu_ref[...])
        o_ref[...] = (g * _sigmoid(g) * u).astype(BF16)

    return pl.pallas_call(
        body, name=name, out_shape=_sds((s, D_FF), BF16), grid=(_FF_BLOCKS,),
        in_specs=[_ff_spec(s)] + _ff_wspecs(), out_specs=pl.BlockSpec((s, FF_UNIT), lambda j: (0, j)),
        compiler_params=_params(("parallel",), VMEM_BIG),
    )(zf, conv_w, conv_w)


def ffn_act_bwd(zf, conv_w, da, name):
    s = zf.shape[1]

    def body(z_ref, wg_ref, wu_ref, da_ref, dz_ref, dw_ref):
        tg, tu = _taps(z_ref[0].astype(F32)), _taps(z_ref[1].astype(F32))
        wg, wu = wg_ref[...], wu_ref[...]
        dav = da_ref[...].astype(F32)
        g = _conv3_taps(tg, wg)
        u = _conv3_taps(tu, wu)
        sg = _sigmoid(g)
        dg = dav * u * (sg * (1.0 + g * (1.0 - sg)))
        du = dav * (g * sg)
        dw_ref[0] = _conv3_dw_taps(dg, tg)
        dw_ref[1] = _conv3_dw_taps(du, tu)
        dz_ref[0] = _conv3_t(dg, wg).astype(BF16)
        dz_ref[1] = _conv3_t(du, wu).astype(BF16)

    return pl.pallas_call(
        body, name=name, out_shape=(_sds((2, s, D_FF), BF16), _sds((2, 3, D_FF))), grid=(_FF_BLOCKS,),
        in_specs=[_ff_spec(s)] + _ff_wspecs() + [pl.BlockSpec((s, FF_UNIT), lambda j: (0, j))],
        out_specs=(_ff_spec(s), pl.BlockSpec((2, 3, FF_UNIT), lambda j: (0, 0, j))),
        compiler_params=_params(("parallel",), VMEM_BIG),
    )(zf, conv_w, conv_w, da)


def _rope(v, cs, s1, s2):
    return v * cs + pltpu.roll(v, LANES - C_ROPE // 2, axis=1) * s1 + pltpu.roll(v, C_ROPE // 2, axis=1) * s2


def _rope_t(dv, cs, s1, s2):
    return dv * cs + pltpu.roll(dv * s1, C_ROPE // 2, axis=1) + pltpu.roll(dv * s2, LANES - C_ROPE // 2, axis=1)


def _kpe_mask(shape):
    lane = lax.broadcasted_iota(jnp.int32, shape, 1)
    return (lane >= C_NOPE) & (lane < C_NOPE + C_ROPE)


def _rms(v, g):
    r = lax.rsqrt(jnp.mean(v * v, axis=-1, keepdims=True) + EPS)
    return v * r, r


def _rms_bwd(dn, xh, r, g):
    dxh = dn * g
    return r * (dxh - xh * jnp.mean(dxh * xh, axis=-1, keepdims=True)), jnp.sum(dn * xh, axis=0, keepdims=True)


_ZQ = C_Q_RANK + C_KV_RANK + HEAD_PAD
_HW = C_HEADS * HEAD_PAD


def mla_pre_fwd(z, gq, gkv, wq, wk, wv, cs, s1, s2):
    s = z.shape[0]
    tm = _tile(s, 256, 8)

    def body(z_ref, gq_ref, gkv_ref, wq_ref, wk_ref, wv_ref, cs_ref, s1_ref, s2_ref, q_ref, k_ref, v_ref):
        zv = z_ref[...].astype(F32)
        cst, s1t, s2t = cs_ref[...], s1_ref[...], s2_ref[...]
        qh, _ = _rms(zv[:, :C_Q_RANK], None)
        qn = (qh * gq_ref[...]).astype(BF16)
        q = _dot(qn, wq_ref[...])
        kh, _ = _rms(zv[:, C_Q_RANK:C_Q_RANK + C_KV_RANK], None)
        kvn = (kh * gkv_ref[...]).astype(BF16)
        k = _dot(kvn, wk_ref[...])
        v_ref[...] = _dot(kvn, wv_ref[...]).astype(BF16)
        kpe = _rope(zv[:, C_Q_RANK + C_KV_RANK:], cst, s1t, s2t)
        for h in range(C_HEADS):
            sl = slice(h * HEAD_PAD, (h + 1) * HEAD_PAD)
            q_ref[:, sl] = _rope(q[:, sl], cst, s1t, s2t).astype(BF16)
            k_ref[:, sl] = (k[:, sl] + kpe).astype(BF16)

    full = lambda r, c: pl.BlockSpec((r, c), lambda i: (0, 0))
    hw = _sds((s, _HW), BF16)
    return pl.pallas_call(
        body, name="mla_pre_fwd", out_shape=(hw, hw, hw), grid=(s // tm,),
        in_specs=[_rows(tm, _ZQ), _vec(C_Q_RANK), _vec(C_KV_RANK), full(C_Q_RANK, _HW), full(C_KV_RANK, _HW),
                  full(C_KV_RANK, _HW), _rows(tm, LANES), _rows(tm, LANES), _rows(tm, LANES)],
        out_specs=(_rows(tm, _HW), _rows(tm, _HW), _rows(tm, _HW)),
        compiler_params=_params(("parallel",), VMEM_BIG),
    )(z, gq, gkv, wq, wk, wv, cs, s1, s2)


def mla_pre_bwd(z, gq, gkv, wq, wk, wv, cs, s1, s2, dq, dk, dv):
    s = z.shape[0]
    tm = _tile(s, 256, 8)

    def body(z_ref, gq_ref, gkv_ref, wq_ref, wk_ref, wv_ref, cs_ref, s1_ref, s2_ref, dq_ref, dk_ref, dv_ref,
             dz_ref, dwq_ref, dwk_ref, dwv_ref, dgq_ref, dgkv_ref):
        @pl.when(pl.program_id(0) == 0)
        def _():
            dwq_ref[...] = jnp.zeros_like(dwq_ref)
            dwk_ref[...] = jnp.zeros_like(dwk_ref)
            dwv_ref[...] = jnp.zeros_like(dwv_ref)
            dgq_ref[...] = jnp.zeros_like(dgq_ref)
            dgkv_ref[...] = jnp.zeros_like(dgkv_ref)

        zv = z_ref[...].astype(F32)
        cst, s1t, s2t = cs_ref[...], s1_ref[...], s2_ref[...]
        gqv, gkvv = gq_ref[...], gkv_ref[...]
        qh, rq = _rms(zv[:, :C_Q_RANK], None)
        qn = (qh * gqv).astype(BF16)
        kh, rk = _rms(zv[:, C_Q_RANK:C_Q_RANK + C_KV_RANK], None)
        kvn = (kh * gkvv).astype(BF16)

        dqv = dq_ref[...].astype(F32)
        dqp = jnp.concatenate(
            [_rope_t(dqv[:, h * HEAD_PAD:(h + 1) * HEAD_PAD], cst, s1t, s2t) for h in range(C_HEADS)], axis=1
        ).astype(BF16)
        dwq_ref[...] += _dot(qn, dqp, _TN)
        dqn = _dot(dqp, wq_ref[...], _NT)
        dql, dgq = _rms_bwd(dqn, qh, rq, gqv)
        dgq_ref[...] += dgq

        dkv = dk_ref[...]
        dkb = dkv.astype(BF16)
        dvb = dv_ref[...].astype(BF16)
        dwk_ref[...] += _dot(kvn, dkb, _TN)
        dwv_ref[...] += _dot(kvn, dvb, _TN)
        dkvn = _dot(dkb, wk_ref[...], _NT) + _dot(dvb, wv_ref[...], _NT)
        dkl, dgkv = _rms_bwd(dkvn, kh, rk, gkvv)
        dgkv_ref[...] += dgkv

        dkpe = dkv[:, :HEAD_PAD]
        for h in range(1, C_HEADS):
            dkpe = dkpe + dkv[:, h * HEAD_PAD:(h + 1) * HEAD_PAD]
        dkpe = _rope_t(jnp.where(_kpe_mask(dkpe.shape), dkpe, 0.0), cst, s1t, s2t)
        dz_ref[...] = jnp.concatenate([dql, dkl, dkpe], axis=1).astype(BF16)

    full = lambda r, c: pl.BlockSpec((r, c), lambda i: (0, 0))
    return pl.pallas_call(
        body, name="mla_pre_bwd",
        out_shape=(_sds((s, _ZQ), BF16), _sds((C_Q_RANK, _HW)), _sds((C_KV_RANK, _HW)), _sds((C_KV_RANK, _HW)),
                   _sds((1, C_Q_RANK)), _sds((1, C_KV_RANK))),
        grid=(s // tm,),
        in_specs=[_rows(tm, _ZQ), _vec(C_Q_RANK), _vec(C_KV_RANK), full(C_Q_RANK, _HW), full(C_KV_RANK, _HW),
                  full(C_KV_RANK, _HW), _rows(tm, LANES), _rows(tm, LANES), _rows(tm, LANES),
                  _rows(tm, _HW), _rows(tm, _HW), _rows(tm, _HW)],
        out_specs=(_rows(tm, _ZQ), full(C_Q_RANK, _HW), full(C_KV_RANK, _HW), full(C_KV_RANK, _HW),
                   _vec(C_Q_RANK), _vec(C_KV_RANK)),
        compiler_params=_params(("arbitrary",), VMEM_BIG),
    )(z, gq, gkv, wq, wk, wv, cs, s1, s2, dq, dk, dv)


_ATT_SCALE = (C_NOPE + C_ROPE) ** -0.5
_NEG = -1e30


def _att_probs(q, k, row0):
    sc = _dot(q, k, _NT) * _ATT_SCALE
    qpos = row0 + lax.broadcasted_iota(jnp.int32, sc.shape, 0)
    kpos = lax.broadcasted_iota(jnp.int32, sc.shape, 1)
    sc = jnp.where(kpos <= qpos, sc, _NEG)
    e = jnp.exp(sc - jnp.max(sc, axis=-1, keepdims=True))
    return e / jnp.sum(e, axis=-1, keepdims=True)


def _causal_cases(i, nq, tq, fn):
    if nq > 8:
        fn(nq * tq)
        return
    for blk in range(nq):
        pl.when(i == blk)(functools.partial(fn, (blk + 1) * tq))


def attn_fwd(q, k, v):
    s = q.shape[0]
    tq = _tile(s, 256, 8)
    nq = s // tq

    def body(q_ref, k_ref, v_ref, o_ref):
        i = pl.program_id(1)

        def case(nk):
            p = _att_probs(q_ref[...], k_ref[:nk, :], i * tq)
            o_ref[...] = _dot(p.astype(BF16), v_ref[:nk, :]).astype(BF16)

        _causal_cases(i, nq, tq, case)

    qspec = pl.BlockSpec((tq, HEAD_PAD), lambda h, i: (i, h))
    kspec = pl.BlockSpec((s, HEAD_PAD), lambda h, i: (0, h))
    return pl.pallas_call(
        body, name="attn_fwd", out_shape=_sds((s, _HW), BF16), grid=(C_HEADS, s // tq),
        in_specs=[qspec, kspec, kspec], out_specs=qspec,
        compiler_params=_params(("parallel", "parallel"), VMEM_BIG),
    )(q, k, v)


def attn_bwd(q, k, v, o, do_all, do_col0):
    s = q.shape[0]
    tq = _tile(s, 256, 8)

    def body(q_ref, k_ref, v_ref, o_ref, do_ref, dq_ref, dk_ref, dv_ref):
        i = pl.program_id(1)

        @pl.when(i == 0)
        def _():
            dk_ref[...] = jnp.zeros_like(dk_ref)
            dv_ref[...] = jnp.zeros_like(dv_ref)

        def case(nk):
            qv, kv, vv, dov = q_ref[...], k_ref[:nk, :], v_ref[:nk, :], do_ref[...]
            p = _att_probs(qv, kv, i * tq)
            dp = _dot(dov, vv, _NT)
            delta = jnp.sum(dov.astype(F32) * o_ref[...].astype(F32), axis=-1, keepdims=True)
            ds = (p * (dp - delta) * _ATT_SCALE).astype(BF16)
            dq_ref[...] = _dot(ds, kv).astype(BF16)
            dk_ref[:nk, :] += _dot(ds, qv, _TN)
            dv_ref[:nk, :] += _dot(p.astype(BF16), dov, _TN)

        _causal_cases(i, s // tq, tq, case)

    qspec = pl.BlockSpec((tq, HEAD_PAD), lambda h, i: (i, h))
    dospec = pl.BlockSpec((tq, HEAD_PAD), lambda h, i: (i, do_col0 + h))
    kspec = pl.BlockSpec((s, HEAD_PAD), lambda h, i: (0, h))
    return pl.pallas_call(
        body, name="attn_bwd", out_shape=(_sds((s, _HW), BF16), _sds((s, _HW)), _sds((s, _HW))),
        grid=(C_HEADS, s // tq),
        in_specs=[qspec, kspec, kspec, qspec, dospec], out_specs=(qspec, kspec, kspec),
        compiler_params=_params(("parallel", "arbitrary"), VMEM_BIG),
    )(q, k, v, o, do_all)


_DW = D_GROUPS * LANES


def _tril_bf16(w):
    r = lax.broadcasted_iota(jnp.int32, w.shape, 0)
    c = lax.broadcasted_iota(jnp.int32, w.shape, 1)
    return jnp.where(c <= r, w, 0.0).astype(BF16)


def _sgu_forward(zu, zv, lg, lb, ws_ref, bs):
    u = _gelu(zu)
    v = _gelu(zv)
    mu = jnp.mean(v, axis=-1, keepdims=True)
    vc = v - mu
    rstd = lax.rsqrt(jnp.mean(vc * vc, axis=-1, keepdims=True) + EPS)
    xh = vc * rstd
    vln = (xh * lg + lb).astype(BF16)
    mixed = []
    for g in range(D_GROUPS):
        wg = _tril_bf16(ws_ref[g])
        mixed.append(_dot(wg, vln[:, g * LANES:(g + 1) * LANES]) + bs[:, g:g + 1])
    return u, xh, rstd, vln, jnp.concatenate(mixed, axis=1)


def sgu_fwd(z, lg, lb, ws, bs_t):
    s = z.shape[0]
    nchunk = s // D_CHUNK

    def body(zu_ref, zv_ref, lg_ref, lb_ref, ws_ref, bs_ref, o_ref):
        u, _, _, _, mixed = _sgu_forward(zu_ref[...].astype(F32), zv_ref[...].astype(F32), lg_ref[...], lb_ref[...],
                                         ws_ref, bs_ref[...])
        o_ref[...] = (u * mixed).astype(BF16)

    return pl.pallas_call(
        body, name="sgu_fwd", out_shape=_sds((s, _DW), BF16), grid=(nchunk,),
        in_specs=[pl.BlockSpec((D_CHUNK, _DW), lambda n: (n, 1)), pl.BlockSpec((D_CHUNK, _DW), lambda n: (n, 2)),
                  _vec(_DW), _vec(_DW), pl.BlockSpec((D_GROUPS, D_CHUNK, D_CHUNK), lambda n: (0, 0, 0)),
                  pl.BlockSpec((D_CHUNK, LANES), lambda n: (0, 0))],
        out_specs=pl.BlockSpec((D_CHUNK, _DW), lambda n: (n, 0)),
        compiler_params=_params(("parallel",)),
    )(z, z, lg, lb, ws, bs_t)


def sgu_bwd(z, lg, lb, ws, bs_t, dycat, dy_col):
    s = z.shape[0]
    nchunk = s // D_CHUNK

    def body(zu_ref, zv_ref, lg_ref, lb_ref, ws_ref, bs_ref, dy_ref, dzu_ref, dzv_ref, dws_ref, dbs_ref, dlg_ref,
             dlb_ref):
        @pl.when(pl.program_id(0) == 0)
        def _():
            dws_ref[...] = jnp.zeros_like(dws_ref)
            dbs_ref[...] = jnp.zeros_like(dbs_ref)
            dlg_ref[...] = jnp.zeros_like(dlg_ref)
            dlb_ref[...] = jnp.zeros_like(dlb_ref)

        zu, zv, lg = zu_ref[...].astype(F32), zv_ref[...].astype(F32), lg_ref[...]
        u, xh, rstd, vln, mixed = _sgu_forward(zu, zv, lg, lb_ref[...], ws_ref, bs_ref[...])
        dy = dy_ref[...].astype(F32)
        dzu_ref[...] = (dy * mixed * _gelu_grad(zu)).astype(BF16)
        dmix = dy * u
        lane = lax.broadcasted_iota(jnp.int32, (D_CHUNK, LANES), 1)
        row = lax.broadcasted_iota(jnp.int32, (D_CHUNK, D_CHUNK), 0)
        colm = lax.broadcasted_iota(jnp.int32, (D_CHUNK, D_CHUNK), 1)
        dvln = []
        dbs = jnp.zeros((D_CHUNK, LANES), F32)
        for g in range(D_GROUPS):
            sl = slice(g * LANES, (g + 1) * LANES)
            dmg = dmix[:, sl]
            dbs = dbs + jnp.where(lane == g, jnp.sum(dmg, axis=-1, keepdims=True), 0.0)
            dmb = dmg.astype(BF16)
            dws_ref[g] += jnp.where(colm <= row, _dot(dmb, vln[:, sl], _NT), 0.0)
            dvln.append(_dot(_tril_bf16(ws_ref[g]), dmb, _TN))
        dbs_ref[...] += dbs
        dvln = jnp.concatenate(dvln, axis=1)
        dlg_ref[...] += jnp.sum(dvln * xh, axis=0, keepdims=True)
        dlb_ref[...] += jnp.sum(dvln, axis=0, keepdims=True)
        dxh = dvln * lg
        dvv = rstd * (dxh - jnp.mean(dxh, axis=-1, keepdims=True) - xh * jnp.mean(dxh * xh, axis=-1, keepdims=True))
        dzv_ref[...] = (dvv * _gelu_grad(zv)).astype(BF16)

    wsspec = pl.BlockSpec((D_GROUPS, D_CHUNK, D_CHUNK), lambda n: (0, 0, 0))
    chunk = lambda cidx: pl.BlockSpec((D_CHUNK, _DW), lambda n: (n, cidx))
    return pl.pallas_call(
        body, name="sgu_bwd",
        out_shape=(_sds((s, _DW), BF16), _sds((s, _DW), BF16), _sds((D_GROUPS, D_CHUNK, D_CHUNK)),
                   _sds((D_CHUNK, LANES)), _sds((1, _DW)), _sds((1, _DW))),
        grid=(nchunk,),
        in_specs=[chunk(1), chunk(2), _vec(_DW), _vec(_DW), wsspec, pl.BlockSpec((D_CHUNK, LANES), lambda n: (0, 0)),
                  chunk(dy_col)],
        out_specs=(chunk(0), chunk(0), wsspec, pl.BlockSpec((D_CHUNK, LANES), lambda n: (0, 0)), _vec(_DW), _vec(_DW)),
        compiler_params=_params(("arbitrary",)),
    )(z, z, lg, lb, ws, bs_t, dycat)


def ada_mod(c_all, ada_w, ada_b):
    nl, d, n = ada_w.shape
    nb = c_all.shape[0]
    tn = _tile(n, 512)

    def body(c_ref, w_ref, b_ref, o_ref):
        cv = c_ref[...]
        ca = (cv * _sigmoid(cv)).astype(BF16)
        o_ref[...] = _dot(ca, w_ref[...].astype(BF16)) + b_ref[...]

    return pl.pallas_call(
        body, name="ada_mod", out_shape=_sds((nl, nb, n)), grid=(nl, n // tn),
        in_specs=[pl.BlockSpec((nb, d), lambda l, j: (0, 0)), pl.BlockSpec((None, d, tn), lambda l, j: (l, 0, j)),
                  pl.BlockSpec((None, 1, tn), lambda l, j: (l, 0, j))],
        out_specs=pl.BlockSpec((None, nb, tn), lambda l, j: (l, 0, j)),
        compiler_params=_params(("parallel", "parallel")),
    )(c_all, ada_w, ada_b.reshape(nl, 1, n))


def ada_grad(c_all_t, dmod):
    d, nb = c_all_t.shape
    nl, _, n = dmod.shape
    tn = _tile(n, 512)
    tr = _tile(d, 256, 8)

    def body(c_ref, dm_ref, o_ref):
        cv = c_ref[...]
        ca = cv * _sigmoid(cv)
        dm = dm_ref[...]
        acc = ca[:, 0:1] * dm[0:1, :]
        for b in range(1, nb):
            acc = acc + ca[:, b:b + 1] * dm[b:b + 1, :]
        o_ref[...] = acc

    return pl.pallas_call(
        body, name="ada_grad", out_shape=_sds((nl, d, n)), grid=(nl, n // tn, d // tr),
        in_specs=[pl.BlockSpec((tr, nb), lambda l, j, r: (r, 0)), pl.BlockSpec((None, nb, tn), lambda l, j, r: (l, 0, j))],
        out_specs=pl.BlockSpec((None, tr, tn), lambda l, j, r: (l, r, j)),
        compiler_params=_params(("parallel", "parallel", "parallel")),
    )(c_all_t, dmod)


_ADAM_BLOCK = 256 * 1024


def _adam_rows(rows, cols):
    if rows * cols <= _ADAM_BLOCK or rows % 8:
        return rows
    return _tile(rows, max(8, _ADAM_BLOCK // cols), 8)


def _adam_update(w, gv, m, v):
    inv_bc1 = 1.0 / (1.0 - ADAM_B1 ** ADAM_STEP)
    inv_bc2 = 1.0 / (1.0 - ADAM_B2 ** ADAM_STEP)
    nm = ADAM_B1 * m + (1.0 - ADAM_B1) * gv
    nv = ADAM_B2 * v + (1.0 - ADAM_B2) * (gv * gv)
    return -ADAM_LR * ((nm * inv_bc1) / (jnp.sqrt(nv * inv_bc2) + ADAM_EPS) + ADAM_WD * w), nm, nv


def adamw(w, g, m, v, name):
    shape = w.shape
    cols = shape[-1]
    rows = w.size // cols
    tr = _adam_rows(rows, cols)

    def body(w_ref, g_ref, m_ref, v_ref, d_ref, nm_ref, nv_ref):
        d_ref[...], nm_ref[...], nv_ref[...] = _adam_update(w_ref[...], g_ref[...], m_ref[...], v_ref[...])

    spec = pl.BlockSpec((tr, cols), lambda i: (i, 0))
    out = _sds((rows, cols))
    r2 = lambda t: t.reshape(rows, cols)
    d, nm, nv = pl.pallas_call(
        body, name=name, out_shape=(out, out, out), grid=(rows // tr,),
        in_specs=[spec] * 4, out_specs=(spec,) * 3, compiler_params=_params(("parallel",)),
    )(r2(w), r2(g), r2(m), r2(v))
    return g.reshape(shape), d.reshape(shape), nm.reshape(shape), nv.reshape(shape)


def adamw_layers(w, g0, g1, m, v, name):
    _, rows, cols = w.shape
    tr = _adam_rows(rows, cols)

    def body(w_ref, g0_ref, g1_ref, m_ref, v_ref, g_ref, d_ref, nm_ref, nv_ref):
        gv = jnp.where(pl.program_id(0) == 0, g0_ref[...], g1_ref[...])
        g_ref[...] = gv
        d_ref[...], nm_ref[...], nv_ref[...] = _adam_update(w_ref[...], gv, m_ref[...], v_ref[...])

    spec = pl.BlockSpec((None, tr, cols), lambda l, i: (l, i, 0))
    gspec = pl.BlockSpec((tr, cols), lambda l, i: (i, 0))
    out = _sds((2, rows, cols))
    return pl.pallas_call(
        body, name=name, out_shape=(out, out, out, out), grid=(2, rows // tr),
        in_specs=[spec, gspec, gspec, spec, spec], out_specs=(spec,) * 4, compiler_params=_params(("parallel", "parallel")),
    )(w, g0, g1, m, v)


def sum8(gathered):
    _, r, _ = gathered.shape
    tr = _tile(r, 512, 8)

    def body(g_ref, o_ref):
        acc = g_ref[0]
        for dev in range(1, N_DEV):
            acc = acc + g_ref[dev]
        o_ref[...] = acc

    return pl.pallas_call(
        body, name="sum8", out_shape=_sds((r, LANES)), grid=(r // tr,),
        in_specs=[pl.BlockSpec((N_DEV, tr, LANES), lambda i: (0, i, 0))], out_specs=pl.BlockSpec((tr, LANES), lambda i: (i, 0)),
        compiler_params=_params(("parallel",)),
    )(gathered)


_SUM_BLOCK = 512 * 1024


def _sum_rows(rh, cols):
    return rh if rh * cols <= _SUM_BLOCK else _tile(rh, max(16, _SUM_BLOCK // cols), 16)


def pair_sum(g, recv, core, name):
    _, r, cols = g.shape
    rh = r // 2
    tr = _sum_rows(rh, cols)
    per = rh // tr

    def body(c_ref, a_ref, b_ref, o_ref):
        del c_ref
        o_ref[...] = (a_ref[...].astype(F32) + b_ref[...].astype(F32)).astype(BF16)

    grid_spec = pltpu.PrefetchScalarGridSpec(
        num_scalar_prefetch=1, grid=(N_CHIPS, per),
        in_specs=[pl.BlockSpec((None, tr, cols), lambda k, i, c: (k, c[0] * per + i, 0)),
                  pl.BlockSpec((None, tr, cols), lambda k, i, c: (k, i, 0))],
        out_specs=pl.BlockSpec((None, tr, cols), lambda k, i, c: (k, i, 0)))
    return pl.pallas_call(
        body, name=name, out_shape=_sds((N_CHIPS, rh, cols), BF16), grid_spec=grid_spec,
        compiler_params=_params(("parallel", "parallel")),
    )(core.reshape(1).astype(jnp.int32), g, recv)


def chip_sum(pair, recv, chip, core, name):
    _, rh, cols = pair.shape
    tr = _sum_rows(rh, cols)

    def body(p_ref, own_ref, r_ref, o_ref):
        del p_ref
        acc = own_ref[...].astype(F32)
        for j in range(N_CHIPS - 1):
            acc = acc + r_ref[j].astype(F32)
        o_ref[...] = acc

    grid_spec = pltpu.PrefetchScalarGridSpec(
        num_scalar_prefetch=1, grid=(rh // tr,),
        in_specs=[pl.BlockSpec((None, tr, cols), lambda i, p: (p[0], i, 0)),
                  pl.BlockSpec((N_CHIPS - 1, tr, cols), lambda i, p: (0, i, 0))],
        out_specs=pl.BlockSpec((None, tr, cols), lambda i, p: (p[1], i, 0)))
    return pl.pallas_call(
        body, name=name, out_shape=_sds((2, rh, cols)), grid_spec=grid_spec,
        compiler_params=_params(("parallel",)),
    )(jnp.stack([chip, core]).astype(jnp.int32), pair, recv)


def _place():
    return lax.axis_index("x"), lax.axis_index("y"), lax.axis_index("c")


def _other_chips(x, y):
    return [(x, 1 - y), (1 - x, y), (1 - x, 1 - y)]


_HBM = pl.BlockSpec(memory_space=pltpu.HBM)


def all_gather8(v, name):
    m, n = v.shape

    def body(x_ref, out_ref, send_sems, recv_sems, local_sem):
        x, y, c = _place()
        me, sibling = (x, y, c), (x, y, 1 - c)
        chips = _other_chips(x, y)

        def rows(px, py, pc):
            return out_ref.at[pl.ds((4 * px + 2 * py + pc) * m, m), :]

        def copy(k, block, to, src=None):
            return pltpu.make_async_remote_copy(
                src_ref=rows(*block) if src is None else src, dst_ref=rows(*block),
                send_sem=send_sems.at[k], recv_sem=recv_sems.at[k], device_id=to, device_id_type=MESH)

        mine = pltpu.make_async_copy(x_ref, rows(*me), local_sem)
        mine.start()
        first = [copy(0, me, sibling, src=x_ref)]
        first += [copy(1 + j, me, (*chip, c), src=x_ref) for j, chip in enumerate(chips)]
        for cp in first:
            cp.start()
        passed = [copy(4 + j, (*chip, c), sibling) for j, chip in enumerate(chips)]
        for j, chip in enumerate(chips):
            copy(1 + j, (*chip, c), me).wait_recv()
            passed[j].start()
        copy(0, sibling, me).wait_recv()
        for j, chip in enumerate(chips):
            copy(4 + j, (*chip, 1 - c), me).wait_recv()
        for cp in first + passed:
            cp.wait_send()
        mine.wait()

    return pl.pallas_call(
        body, name=name, out_shape=_sds((N_DEV * m, n), v.dtype),
        in_specs=[pl.BlockSpec(memory_space=pltpu.VMEM)], out_specs=pl.BlockSpec(memory_space=pltpu.VMEM),
        scratch_shapes=[pltpu.SemaphoreType.DMA((7,)), pltpu.SemaphoreType.DMA((7,)), pltpu.SemaphoreType.DMA],
        compiler_params=_params(None, VMEM_BIG),
    )(v)


def _comm_call(body, name, ins, out_shapes, nsem, aliases=None):
    return pl.pallas_call(
        body, name=name, out_shape=tuple(out_shapes), in_specs=[_HBM] * len(ins), out_specs=tuple([_HBM] * len(out_shapes)),
        scratch_shapes=[pltpu.SemaphoreType.DMA((nsem,)), pltpu.SemaphoreType.DMA((nsem,))],
        input_output_aliases=aliases or {},
    )(*ins)


def _remote(src, dst, send_sems, recv_sems, k, to):
    return pltpu.make_async_remote_copy(src_ref=src, dst_ref=dst, send_sem=send_sems.at[k], recv_sem=recv_sems.at[k],
                                        device_id=to, device_id_type=MESH)


def _half(core, rh):
    return pl.ds(pl.multiple_of(core * rh, 16), rh)


def swap_halves(gs, name):
    n = len(gs)

    def body(*refs):
        ins, outs, (send_sems, recv_sems) = refs[:n], refs[n:2 * n], refs[2 * n:]
        x, y, c = _place()
        copies = []
        for i in range(n):
            theirs = _half(1 - c, ins[i].shape[1] // 2)
            cp = _remote(ins[i].at[:, theirs], outs[i], send_sems, recv_sems, i, (x, y, 1 - c))
            cp.start()
            copies.append(cp)
        for cp in copies:
            cp.wait()

    return _comm_call(body, name, gs, [_sds((g.shape[0], g.shape[1] // 2, g.shape[2]), g.dtype) for g in gs], n)


def join_halves(bufs, name):
    n = len(bufs)

    def body(*refs):
        ins, outs, (send_sems, recv_sems) = refs[:n], refs[n:2 * n], refs[2 * n:]
        x, y, c = _place()
        copies = []
        for i in range(n):
            cp = _remote(ins[i].at[c], outs[i].at[c], send_sems, recv_sems, i, (x, y, 1 - c))
            cp.start()
            copies.append(cp)
        for i in range(n):
            theirs = outs[i].at[1 - c]
            _remote(theirs, theirs, send_sems, recv_sems, i, (x, y, 1 - c)).wait_recv()
        for cp in copies:
            cp.wait_send()

    return _comm_call(body, name, bufs, [_sds(b.shape, b.dtype) for b in bufs], n, {i: i for i in range(n)})


def forward_halves(lands, name):
    n = len(lands)

    def body(*refs):
        ins, outs, (send_sems, recv_sems) = refs[:n], refs[n:2 * n], refs[2 * n:]
        x, y, c = _place()
        sibling = (x, y, 1 - c)
        chips = _other_chips(x, y)
        copies = []
        for i in range(n):
            mine = _half(c, ins[i].shape[1] // 2)
            for j, (px, py) in enumerate(chips):
                cp = _remote(ins[i].at[2 * px + py, mine], outs[i].at[2 * px + py, mine], send_sems, recv_sems, 3 * i + j, sibling)
                cp.start()
                copies.append(cp)
        for i in range(n):
            theirs = _half(1 - c, ins[i].shape[1] // 2)
            for j, (px, py) in enumerate(chips):
                landed = outs[i].at[2 * px + py, theirs]
                _remote(landed, landed, send_sems, recv_sems, 3 * i + j, sibling).wait_recv()
        for cp in copies:
            cp.wait_send()

    return _comm_call(body, name, lands, [_sds(b.shape, b.dtype) for b in lands], 3 * n, {i: i for i in range(n)})


_SEM = pl.BlockSpec(memory_space=pltpu.SEMAPHORE)
_EFFECT = pltpu.SideEffectType.DATAFLOW_SIDE_EFFECTING


def _gather_copies(srcs, lands, send_sems, recv_sems):
    x, y, c = _place()
    copies = []
    for i in range(len(srcs)):
        mine = _half(c, srcs[i].shape[0] // 2)
        for j, chip in enumerate(_other_chips(x, y)):
            copies.append(_remote(srcs[i].at[mine], lands[i].at[2 * x + y, mine], send_sems, recv_sems, 3 * i + j, (*chip, c)))
    return copies


def _exchange_copies(srcs, lands, send_sems, recv_sems):
    x, y, c = _place()
    copies = []
    for i in range(len(srcs)):
        for j, (px, py) in enumerate(_other_chips(x, y)):
            copies.append(_remote(srcs[i].at[2 * px + py], lands[i].at[j], send_sems, recv_sems, 3 * i + j, (px, py, c)))
    return copies


def _swap_copies(srcs, lands, send_sems, recv_sems):
    x, y, c = _place()
    return [_remote(srcs[i].at[:, _half(1 - c, srcs[i].shape[1] // 2)], lands[i], send_sems, recv_sems, i, (x, y, 1 - c))
            for i in range(len(srcs))]


def _join_copies(srcs, lands, send_sems, recv_sems):
    del lands
    x, y, c = _place()
    return [_remote(srcs[i].at[c], srcs[i].at[c], send_sems, recv_sems, i, (x, y, 1 - c)) for i in range(len(srcs))]


def _everyone_copies(srcs, lands, send_sems, recv_sems):
    x, y, c = _place()
    flip = lambda v, b: 1 - v if b else v
    dst = lands[0].at[4 * x + 2 * y + c]
    return [_remote(srcs[0], dst, send_sems, recv_sems, j - 1, (flip(x, j & 4), flip(y, j & 2), flip(c, j & 1)))
            for j in range(1, N_DEV)]


GATHER = (_gather_copies, 3)
EXCHANGE = (_exchange_copies, 3)
SWAP = (_swap_copies, 1)
JOIN = (_join_copies, 1)
EVERYONE = (_everyone_copies, N_DEV - 1)


def split_start(name, plan, srcs, land_shapes):
    copies_fn, per_source = plan
    n, m = len(srcs), len(land_shapes)
    ncopies = per_source * n

    def body(*refs):
        src_refs, land_refs = refs[:n], refs[n:n + m]
        send_sems, recv_sems = refs[n + m], refs[n + m + 1]
        token = refs[-1]
        for cp in copies_fn(src_refs, land_refs, send_sems, recv_sems):
            cp.start()
        token[...] = jnp.zeros_like(token)

    hbm = lambda s: pltpu.HBM(tuple(s.shape), s.dtype)
    outs = pl.pallas_call(
        body, name=name,
        out_shape=(pltpu.SemaphoreType.DMA((ncopies,)), pltpu.SemaphoreType.DMA((ncopies,)), *[hbm(s) for s in srcs],
                   *[hbm(s) for s in land_shapes], _sds((8, LANES))),
        in_specs=[_HBM] * (n + m),
        out_specs=(_SEM, _SEM, *([_HBM] * (n + m)), pl.BlockSpec(memory_space=pltpu.VMEM)),
        input_output_aliases={i: 2 + i for i in range(n + m)},
        compiler_params=pltpu.CompilerParams(has_side_effects=_EFFECT),
    )(*[pltpu.with_memory_space_constraint(s, pltpu.HBM) for s in srcs],
      *[pltpu.with_memory_space_constraint(lax.empty(tuple(s.shape), s.dtype), pltpu.HBM) for s in land_shapes])
    handle = (outs[0], outs[1], list(outs[2:2 + n]), list(outs[2 + n:2 + n + m]))
    return handle, outs[-1][0, 0]


def split_wait(name, plan, handle, after):
    copies_fn, _ = plan
    send_sems, recv_sems, srcs, lands = handle
    n, m = len(srcs), len(lands)

    def body(*refs):
        src_refs, land_refs = refs[:n], refs[n:n + m]
        for cp in copies_fn(src_refs, land_refs, refs[n + m], refs[n + m + 1]):
            cp.wait_send()
            cp.wait_recv()

    hbm = lambda s: pltpu.HBM(tuple(s.shape), s.dtype)
    outs = pl.pallas_call(
        body, name=name, out_shape=tuple(hbm(s) for s in srcs + lands),
        in_specs=[_HBM] * (n + m) + [_SEM, _SEM, pl.BlockSpec(memory_space=pl.ANY)], out_specs=tuple([_HBM] * (n + m)),
        input_output_aliases={i: i for i in range(n + m)},
        compiler_params=pltpu.CompilerParams(has_side_effects=_EFFECT),
    )(*srcs, *lands, send_sems, recv_sems, after)
    return list(outs[:n]), list(outs[n:])


_CD_PAD = C_Q_RANK + C_KV_RANK + HEAD_PAD + 2 * _DW


def chip_major(w, groups=N_CHIPS):
    r, c = w.shape
    return w.reshape(r, groups, c // groups).transpose(1, 0, 2)


def from_chip_major(w):
    g, r, c = w.shape
    return w.transpose(1, 0, 2).reshape(r, g * c)


def _cd_in_pad(w):
    a = C_Q_RANK + C_KV_RANK
    z = lambda n: jnp.zeros((w.shape[0], n), w.dtype)
    return jnp.concatenate([w[:, :a], z(C_NOPE), w[:, a:a + C_ROPE], z(HEAD_PAD - C_NOPE - C_ROPE), w[:, a + C_ROPE:]], axis=1)


def _cd_in_unpad(w):
    a = C_Q_RANK + C_KV_RANK
    return jnp.concatenate([w[:, :a], w[:, a + C_NOPE:a + C_NOPE + C_ROPE], w[:, a + HEAD_PAD:]], axis=1)


def _pad_heads(w, width):
    r = w.shape[0]
    w = w.reshape(r, C_HEADS, width)
    return jnp.pad(w, ((0, 0), (0, 0), (0, HEAD_PAD - width))).reshape(r, _HW)


def _unpad_heads(w, width):
    r = w.shape[0]
    return w.reshape(r, C_HEADS, HEAD_PAD)[:, :, :width].reshape(r, C_HEADS * width)


_MATMUL_WEIGHTS = ("ab_w_in", "ab_w_out", "cd_w_in", "c_w_uq", "c_w_ukv", "cd_w_out", "ffn_w_up", "ffn_w_down")
_LAYER_STACKED = ("norm1_g", "norm2_g", "ffn_w_up", "ffn_conv_w", "ffn_w_down")
_ROW_VECTORS = ("b_scale", "c_q_norm_g", "c_kv_norm_g", "d_ln_g", "d_ln_b")


def full_to_local(p):
    q = {}
    for k, v in p.items():
        if k == "final_norm_g":
            v = v.reshape(1, -1)
        elif k not in _LAYER_STACKED and k not in _ROW_VECTORS:
            v = v[0]
        q[k] = v.astype(BF16) if k in _MATMUL_WEIGHTS else v
    return q


def local_to_full(g):
    q = {}
    for k, v in g.items():
        if k == "final_norm_g":
            q[k] = v.reshape(-1)
        elif k not in _LAYER_STACKED and k not in _ROW_VECTORS:
            q[k] = v[None]
        else:
            q[k] = v
    return q


def prepare_weights(p):
    q = dict(p)
    q["cd_w_in"] = _cd_in_pad(p["cd_w_in"])
    q["c_w_uq"] = _pad_heads(p["c_w_uq"], C_NOPE + C_ROPE)
    ukv = p["c_w_ukv"].reshape(C_KV_RANK, C_HEADS, C_NOPE + C_V)
    q["c_w_uk"] = _pad_heads(ukv[:, :, :C_NOPE].reshape(C_KV_RANK, -1), C_NOPE)
    q["c_w_uv"] = _pad_heads(ukv[:, :, C_NOPE:].reshape(C_KV_RANK, -1), C_V)
    wo = p["cd_w_out"]
    att_rows = jnp.pad(wo[:C_HEADS * C_V].reshape(C_HEADS, C_V, D_MODEL), ((0, 0), (0, HEAD_PAD - C_V), (0, 0)))
    q["cd_w_out"] = jnp.concatenate([att_rows.reshape(_HW, D_MODEL), wo[C_HEADS * C_V:]], axis=0)
    return q


def unprepare_grads(g):
    q = dict(g)
    q["cd_w_in"] = _cd_in_unpad(g["cd_w_in"])
    q["c_w_uq"] = _unpad_heads(g["c_w_uq"], C_NOPE + C_ROPE)
    uk = g.pop("c_w_uk").reshape(C_KV_RANK, C_HEADS, HEAD_PAD)[:, :, :C_NOPE]
    uv = g.pop("c_w_uv").reshape(C_KV_RANK, C_HEADS, HEAD_PAD)[:, :, :C_V]
    q.pop("c_w_uk", None)
    q.pop("c_w_uv", None)
    q["c_w_ukv"] = jnp.concatenate([uk, uv], axis=-1).reshape(C_KV_RANK, C_HEADS * (C_NOPE + C_V))
    wo = g["cd_w_out"]
    att = wo[:_HW].reshape(C_HEADS, HEAD_PAD, D_MODEL)[:, :C_V].reshape(C_HEADS * C_V, D_MODEL)
    q["cd_w_out"] = jnp.concatenate([att, wo[_HW:]], axis=0)
    return q


def rope_tables(positions):
    half = C_ROPE // 2
    inv_freq = ROPE_THETA ** (-jnp.arange(half, dtype=F32) / half)
    ang = positions.astype(F32)[:, None] * inv_freq
    cos, sin = jnp.cos(ang), jnp.sin(ang)
    s = positions.shape[0]
    z = lambda n: jnp.zeros((s, n), F32)
    cs = jnp.concatenate([jnp.ones((s, C_NOPE), F32), cos, cos, z(HEAD_PAD - C_NOPE - C_ROPE)], axis=1)
    s1 = jnp.concatenate([z(C_NOPE), -sin, z(HEAD_PAD - C_NOPE - half)], axis=1)
    s2 = jnp.concatenate([z(C_NOPE + half), sin, z(HEAD_PAD - C_NOPE - C_ROPE)], axis=1)
    return cs, s1, s2


def _mods(mod_l):
    return [mod_l[:, i * D_MODEL:(i + 1) * D_MODEL] for i in range(N_MOD)]


def _ffn_fwd(x1, w, l, sc2, sh2, g2, late_down=None):
    n2 = w["norm2_g"][l:l + 1]
    h2 = modnorm_fwd(x1, n2, sc2, sh2, f"modnorm2_fwd{l}")
    up_cols = 2 * D_FF // N_CHIPS
    zf = matmul(h2, w["ffn_w_up"][l], "nn", BF16, f"ffn_up{l}", gb=N_CHIPS, go=2, tn=up_cols)
    a = ffn_act_fwd(zf, w["ffn_conv_w"][l], f"ffn_act_fwd{l}")
    if late_down is not None:
        late_down(a)
    f = matmul(a, w["ffn_w_down"][l], "nn", F32, f"ffn_down{l}", tk=D_FF)
    x2 = resid_fwd(x1, f, g2, f"resid2_fwd{l}")
    return x2, (h2, zf, a, f)


def _ffn_bwd(dres, x1, saved, w, l, sc2, g2):
    h2, zf, a, f = saved
    n2 = w["norm2_g"][l:l + 1]
    df, dg2 = gate_bwd(dres, f, g2, f"gate2_bwd{l}")
    up_cols = 2 * D_FF // N_CHIPS
    da = matmul(df, w["ffn_w_down"][l], "nt", BF16, f"ffn_down_dx{l}")
    d_down = matmul(a, df, "tn", BF16, f"ffn_down_dw{l}", tm=D_FF // 2)
    dzf, d_conv = ffn_act_bwd(zf, w["ffn_conv_w"][l], da, f"ffn_act_bwd{l}")
    dh2 = matmul(dzf, w["ffn_w_up"][l], "nt", F32, f"ffn_up_dx{l}", ga=2, gb=N_CHIPS, tk=up_cols)
    d_up = matmul(h2, dzf, "tn", BF16, f"ffn_up_dw{l}", gb=2, go=N_CHIPS, tn=up_cols)
    dres, dsh2, dsc2, dn2 = norm_bwd(x1, dh2, n2, sc2, dres, f"norm2_bwd{l}")
    d_conv = d_conv.transpose(1, 0, 2).reshape(3, 2 * D_FF)
    return dres, dict(ffn_w_down=d_down, ffn_conv_w=d_conv, ffn_w_up=d_up, norm2_g=dn2), (dsh2, dsc2, dg2)


def mixer0_fwd(x0, sh1, sc1, g1, w):
    h = modnorm_fwd(x0, w["norm1_g"][0:1], sc1, sh1, "modnorm1_fwd0")
    z = matmul(h, w["ab_w_in"], "nn", BF16, "ab_in", gb=N_CHIPS)
    ya = gconv_fwd(z, w["a_conv_w"])
    yb = pool_fwd(z, w["b_mix_w"], w["b_scale"])
    ycat = jnp.concatenate([ya] + yb, axis=1)
    y = matmul(ycat, w["ab_w_out"], "nn", F32, "ab_out")
    x1 = resid_fwd(x0, y, g1, "resid1_fwd0")
    return x1, (x0, h, z, ycat, y, sc1, g1)


def mixer0_bwd(dres, saved, w):
    x0, h, z, ycat, y, sc1, g1 = saved
    grads = {}
    dy, dg1 = gate_bwd(dres, y, g1, "gate1_bwd0")
    dycat = matmul(dy, w["ab_w_out"], "nt", BF16, "ab_out_dx")
    grads["ab_w_out"] = matmul(ycat, dy, "tn", BF16, "ab_out_dw")
    db, dc, da, d_conv = gconv_bwd(z, w["a_conv_w"], dycat)
    pb = pool_bwd(z, w["b_mix_w"], w["b_scale"], dycat)
    dz = jnp.concatenate([db, dc, da] + [t[0] for t in pb], axis=1)
    dh = matmul(dz, w["ab_w_in"], "nt", F32, "ab_in_dx", gb=N_CHIPS)
    grads["ab_w_in"] = matmul(h, dz, "tn", BF16, "ab_in_dw", go=N_CHIPS)
    dres, dsh1, dsc1, dn1 = norm_bwd(x0, dh, w["norm1_g"][0:1], sc1, dres, "norm1_bwd0")
    grads.update(a_conv_w=d_conv, b_mix_w=jnp.stack([t[1] for t in pb]),
                 b_scale=jnp.concatenate([t[2] for t in pb], axis=1), norm1_g=dn1)
    return dres, grads, (dsh1, dsc1, dg1)


def mixer1_fwd(x0, sh1, sc1, g1, ropes, w):
    cs, s1, s2 = ropes
    h = modnorm_fwd(x0, w["norm1_g"][1:2], sc1, sh1, "modnorm1_fwd1")
    z = matmul(h, w["cd_w_in"], "nn", BF16, "cd_in")
    bs_t = jnp.pad(w["d_b_s"].T, ((0, 0), (0, LANES - D_GROUPS)))
    qh, kh, vh = mla_pre_fwd(z, w["c_q_norm_g"], w["c_kv_norm_g"], w["c_w_uq"], w["c_w_uk"], w["c_w_uv"], cs, s1, s2)
    oh = attn_fwd(qh, kh, vh)
    yd = sgu_fwd(z, w["d_ln_g"], w["d_ln_b"], w["d_w_s"], bs_t)
    ycat = jnp.concatenate([oh, yd], axis=1)
    y = matmul(ycat, w["cd_w_out"], "nn", F32, "cd_out")
    x1 = resid_fwd(x0, y, g1, "resid1_fwd1")
    return x1, (x0, h, z, bs_t, qh, kh, vh, oh, ycat, y, sc1, g1)


def mixer1_bwd(dres, saved, ropes, w):
    cs, s1, s2 = ropes
    x0, h, z, bs_t, qh, kh, vh, oh, ycat, y, sc1, g1 = saved
    grads = {}
    dy, dg1 = gate_bwd(dres, y, g1, "gate1_bwd1")
    dycat = matmul(dy, w["cd_w_out"], "nt", BF16, "cd_out_dx")
    grads["cd_w_out"] = matmul(ycat, dy, "tn", F32, "cd_out_dw")
    dqh, dkh, dvh = attn_bwd(qh, kh, vh, oh, dycat, 0)
    dzq, d_uq, d_uk, d_uv, d_gq, d_gkv = mla_pre_bwd(
        z, w["c_q_norm_g"], w["c_kv_norm_g"], w["c_w_uq"], w["c_w_uk"], w["c_w_uv"], cs, s1, s2, dqh, dkh, dvh)
    dzu, dzv, d_ws, d_bs, d_lg, d_lb = sgu_bwd(z, w["d_ln_g"], w["d_ln_b"], w["d_w_s"], bs_t, dycat, _HW // _DW)
    dz = jnp.concatenate([dzq, dzu, dzv], axis=1)
    dh = matmul(dz, w["cd_w_in"], "nt", F32, "cd_in_dx")
    grads["cd_w_in"] = matmul(h, dz, "tn", F32, "cd_in_dw")
    dres, dsh1, dsc1, dn1 = norm_bwd(x0, dh, w["norm1_g"][1:2], sc1, dres, "norm1_bwd1")
    grads.update(c_w_uq=d_uq, c_w_uk=d_uk, c_w_uv=d_uv, c_q_norm_g=d_gq, c_kv_norm_g=d_gkv, d_w_s=d_ws,
                 d_b_s=d_bs[:, :D_GROUPS].T, d_ln_g=d_lg, d_ln_b=d_lb, norm1_g=dn1)
    return dres, grads, (dsh1, dsc1, dg1)


_PER_LAYER = ("ffn_w_down", "ffn_conv_w", "ffn_w_up", "norm2_g", "norm1_g")


def _merge_layer_grads(g0, g1):
    grads = {k: v for k, v in g0.items() if k not in _PER_LAYER}
    grads.update({k: v for k, v in g1.items() if k not in _PER_LAYER})
    for k in ("ffn_w_down", "ffn_w_up"):
        grads[k] = [g0[k], g1[k]]
    grads["ffn_conv_w"] = jnp.stack([g0["ffn_conv_w"], g1["ffn_conv_w"]])
    grads["norm1_g"] = jnp.concatenate([g0["norm1_g"], g1["norm1_g"]], axis=0)
    grads["norm2_g"] = jnp.concatenate([g0["norm2_g"], g1["norm2_g"]], axis=0)
    return grads


def local_step(x, tgt, mod, ropes, w):
    sh1a, sc1a, g1a, sh2a, sc2a, g2a = _mods(mod[0:1])
    sh1b, sc1b, g1b, sh2b, sc2b, g2b = _mods(mod[1:2])
    x1, mix0 = mixer0_fwd(x, sh1a, sc1a, g1a, w)
    x2, ffn0 = _ffn_fwd(x1, w, 0, sc2a, sh2a, g2a)
    x3, mix1 = mixer1_fwd(x2, sh1b, sc1b, g1b, ropes, w)
    x4, ffn1 = _ffn_fwd(x3, w, 1, sc2b, sh2b, g2b)
    dres, d_final, loss = final_fwd_bwd(x4, w["final_norm_g"], tgt)
    dres, gf1, dm2b = _ffn_bwd(dres, x3, ffn1, w, 1, sc2b, g2b)
    dres, gm1, dm1b = mixer1_bwd(dres, mix1, ropes, w)
    dres, gf0, dm2a = _ffn_bwd(dres, x1, ffn0, w, 0, sc2a, g2a)
    dres, gm0, dm1a = mixer0_bwd(dres, mix0, w)
    grads = _merge_layer_grads({**gf0, **gm0}, {**gf1, **gm1})
    grads["final_norm_g"] = d_final
    dmod = jnp.concatenate([jnp.concatenate(dm1a + dm2a, axis=1), jnp.concatenate(dm1b + dm2b, axis=1)], axis=0)
    return loss, dres, dmod, grads


_WEIGHTS = ("ada_w", "ada_b", "norm1_g", "norm2_g", "ab_w_in", "a_conv_w", "b_mix_w", "b_scale", "ab_w_out", "cd_w_in",
            "c_q_norm_g", "c_w_uq", "c_kv_norm_g", "c_w_ukv", "d_ln_g", "d_ln_b", "d_w_s", "d_b_s", "cd_w_out",
            "ffn_w_up", "ffn_conv_w", "ffn_w_down", "final_norm_g")
_INPUTS = ("x", "c", "positions") + _WEIGHTS + ("loss_target",) + tuple("m_" + n for n in _WEIGHTS) + tuple(
    "v_" + n for n in _WEIGHTS)

def _pack_rows(parts, rows, dtype):
    flat = jnp.concatenate([p.reshape(-1).astype(dtype) for p in parts])
    return jnp.pad(flat, (0, rows * LANES - flat.shape[0])).reshape(rows, LANES)


def _rows_major(w):
    r, c = w.shape
    return w.reshape(N_CHIPS, r // N_CHIPS, c)


def order_after(v, name):
    def body(v_ref, o_ref):
        del v_ref
        o_ref[...] = jnp.zeros_like(o_ref)

    return pl.pallas_call(
        body, name=name, out_shape=_sds((8, LANES)), in_specs=[pl.BlockSpec(memory_space=pl.ANY)],
        out_specs=pl.BlockSpec(memory_space=pltpu.VMEM),
    )(v)[0, 0]


def start_gather(shards, tag):
    lands = [_sds((N_CHIPS,) + s.shape, s.dtype) for s in shards]
    return split_start("gather_start_" + tag, GATHER, shards, lands)


def finish_gather(handle, chip, tag, after):
    shards, lands = split_wait("gather_wait_" + tag, GATHER, handle, after)
    lands = forward_halves(lands, "gather_forward_" + tag)
    return [lax.dynamic_update_index_in_dim(o, s, chip, 0) for o, s in zip(lands, shards)]


def start_reduce(gs, core, tag):
    recv = swap_halves(gs, "swap_halves_" + tag)
    pairs = [pair_sum(g, r, core, f"pair_sum_{tag}{i}") for i, (g, r) in enumerate(zip(gs, recv))]
    lands = [_sds((N_CHIPS - 1,) + p.shape[1:], p.dtype) for p in pairs]
    return split_start("exchange_start_" + tag, EXCHANGE, pairs, lands)


def finish_reduce(handle, chip, core, tag, after):
    pairs, others = split_wait("exchange_wait_" + tag, EXCHANGE, handle, after)
    halves = [chip_sum(p, o, chip, core, f"chip_sum_{tag}{i}") for i, (p, o) in enumerate(zip(pairs, others))]
    full = join_halves(halves, "join_halves_" + tag)
    return [f.reshape(f.shape[1] * 2, f.shape[2]) for f in full]


_SMALL_SHARDED = (("a_conv_w", (3, 128), 1), ("c_q_norm_g", (1, 64), 1), ("d_ln_g", (1, 128), 1), ("d_ln_b", (1, 128), 1),
                  ("ffn_conv_w", (2, 3, 2 * D_FF // N_CHIPS), 2))
_SMALL_GRADS = (("norm1_g", (2, D_MODEL)), ("norm2_g", (2, D_MODEL)), ("b_mix_w", (4, 128, 128)), ("b_scale", (1, 512)),
                ("c_kv_norm_g", (1, 128)), ("d_w_s", (4, 128, 128)), ("d_b_s", (4, 128)), ("final_norm_g", (1, D_MODEL)),
                ("a_conv_w", (3, 512)), ("c_q_norm_g", (1, 256)), ("d_ln_g", (1, 512)), ("d_ln_b", (1, 512)),
                ("ffn_conv_w", (2, 3, 2 * D_FF)))


def _size(shape):
    n = 1
    for d in shape:
        n *= d
    return n


def kernel(x, c, positions, ada_w, ada_b, norm1_g, norm2_g, ab_w_in, a_conv_w, b_mix_w, b_scale, ab_w_out, cd_w_in, c_q_norm_g, c_w_uq, c_kv_norm_g, c_w_ukv, d_ln_g, d_ln_b, d_w_s, d_b_s, cd_w_out, ffn_w_up, ffn_conv_w, ffn_w_down, final_norm_g, loss_target, m_ada_w, m_ada_b, m_norm1_g, m_norm2_g, m_ab_w_in, m_a_conv_w, m_b_mix_w, m_b_scale, m_ab_w_out, m_cd_w_in, m_c_q_norm_g, m_c_w_uq, m_c_kv_norm_g, m_c_w_ukv, m_d_ln_g, m_d_ln_b, m_d_w_s, m_d_b_s, m_cd_w_out, m_ffn_w_up, m_ffn_conv_w, m_ffn_w_down, m_final_norm_g, v_ada_w, v_ada_b, v_norm1_g, v_norm2_g, v_ab_w_in, v_a_conv_w, v_b_mix_w, v_b_scale, v_ab_w_out, v_cd_w_in, v_c_q_norm_g, v_c_w_uq, v_c_kv_norm_g, v_c_w_ukv, v_d_ln_g, v_d_ln_b, v_d_w_s, v_d_b_s, v_cd_w_out, v_ffn_w_up, v_ffn_conv_w, v_ffn_w_down, v_final_norm_g):
    args = (x, c, positions, ada_w, ada_b, norm1_g, norm2_g, ab_w_in, a_conv_w, b_mix_w, b_scale, ab_w_out, cd_w_in, c_q_norm_g, c_w_uq, c_kv_norm_g, c_w_ukv, d_ln_g, d_ln_b, d_w_s, d_b_s, cd_w_out, ffn_w_up, ffn_conv_w, ffn_w_down, final_norm_g, loss_target, m_ada_w, m_ada_b, m_norm1_g, m_norm2_g, m_ab_w_in, m_a_conv_w, m_b_mix_w, m_b_scale, m_ab_w_out, m_cd_w_in, m_c_q_norm_g, m_c_w_uq, m_c_kv_norm_g, m_c_w_ukv, m_d_ln_g, m_d_ln_b, m_d_w_s, m_d_b_s, m_cd_w_out, m_ffn_w_up, m_ffn_conv_w, m_ffn_w_down, m_final_norm_g, v_ada_w, v_ada_b, v_norm1_g, v_norm2_g, v_ab_w_in, v_a_conv_w, v_b_mix_w, v_b_scale, v_ab_w_out, v_cd_w_in, v_c_q_norm_g, v_c_w_uq, v_c_kv_norm_g, v_c_w_ukv, v_d_ln_g, v_d_ln_b, v_d_w_s, v_d_b_s, v_cd_w_out, v_ffn_w_up, v_ffn_conv_w, v_ffn_w_down, v_final_norm_g)
    a = dict(zip(_INPUTS, args, strict=True))
    xi, yi, ci = _place()
    chip = 2 * xi + yi
    dev = 4 * xi + 2 * yi + ci
    x = a["x"][0]
    tgt = a["loss_target"][0]

    bf = lambda t: t.astype(BF16)
    mix0_handle, tok = start_gather([bf(a["ab_w_in"][0]), bf(a["ab_w_out"][0])], "mix0")

    small_parts = [a["c"] + tok] + [a[n] for n, _, _ in _SMALL_SHARDED]
    rows1 = -(-sum(p.size for p in small_parts) // LANES // 8) * 8
    g1 = all_gather8(_pack_rows(small_parts, rows1, F32), "gather_small").reshape(N_DEV, rows1 * LANES)
    c_all = g1[:, :D_MODEL]
    per_chip = g1[0::2]
    small_full = {}
    off = D_MODEL
    for n, shp, axis in _SMALL_SHARDED:
        piece = per_chip[:, off:off + _size(shp)].reshape((N_CHIPS,) + shp)
        small_full[n] = jnp.concatenate([piece[k] for k in range(N_CHIPS)], axis=axis)
        off += _size(shp)

    merge = lambda t: t.reshape(t.shape[0] * t.shape[1], t.shape[2])
    w = dict(norm1_g=a["norm1_g"], norm2_g=a["norm2_g"], b_mix_w=a["b_mix_w"][0], b_scale=a["b_scale"],
             c_kv_norm_g=a["c_kv_norm_g"], d_w_s=a["d_w_s"][0], d_b_s=a["d_b_s"][0],
             final_norm_g=a["final_norm_g"].reshape(1, D_MODEL), **small_full)

    ncol = N_MOD * D_MODEL // N_CHIPS
    ada_b_mine = lax.dynamic_slice_in_dim(a["ada_b"], chip * ncol, ncol, axis=1)
    mod_cols = ada_mod(c_all, a["ada_w"], ada_b_mine)
    g2 = all_gather8(mod_cols.reshape(-1, LANES), "gather_mod").reshape(N_DEV, 2, N_DEV, ncol)
    mod = lax.dynamic_index_in_dim(g2[0::2], dev, axis=2, keepdims=False)
    mod = mod.transpose(1, 0, 2).reshape(2, N_MOD * D_MODEL)

    late = order_after(mod, "after_mod")
    bf_late = lambda t: (t + late).astype(BF16)
    up0_handle, tok_a = start_gather([bf_late(a["ffn_w_up"][0])], "up0")
    down0_handle, tok_b = start_gather([bf_late(a["ffn_w_down"][0])], "down0")
    mix1_handle, tok_c = start_gather(
        [bf_late(a["cd_w_in"][0]), bf_late(a["c_w_uq"][0]), bf_late(a["c_w_ukv"][0]), bf_late(a["cd_w_out"][0])], "mix1")
    ffn1_handle, tok_d = start_gather([bf_late(a["ffn_w_up"][1]), bf_late(a["ffn_w_down"][1])], "ffn1")
    mod = mod + (tok_a + tok_b + tok_c + tok_d)

    ropes = rope_tables(a["positions"][0])
    cm16 = lambda t: chip_major(t).astype(BF16)
    sh1a, sc1a, g1a, sh2a, sc2a, g2a = _mods(mod[0:1])
    sh1b, sc1b, g1b, sh2b, sc2b, g2b = _mods(mod[1:2])

    w_in0, w_out0 = finish_gather(mix0_handle, chip, "mix0", mod)
    w.update(ab_w_in=w_in0, ab_w_out=merge(w_out0))
    x1, mix0 = mixer0_fwd(x, sh1a, sc1a, g1a, w)
    up0, = finish_gather(up0_handle, chip, "up0", x1)
    w.update(ffn_w_up=[up0, None], ffn_w_down=[None, None])

    def fetch_down0(act):
        down0, = finish_gather(down0_handle, chip, "down0", act)
        w["ffn_w_down"][0] = merge(down0)

    x2, ffn0 = _ffn_fwd(x1, w, 0, sc2a, sh2a, g2a, late_down=fetch_down0)
    cd_in, uq, ukv, cd_out = finish_gather(mix1_handle, chip, "mix1", x2)
    w.update(prepare_weights(dict(cd_w_in=from_chip_major(cd_in), c_w_uq=from_chip_major(uq), c_w_ukv=from_chip_major(ukv),
                                  cd_w_out=merge(cd_out))))
    x3, mix1 = mixer1_fwd(x2, sh1b, sc1b, g1b, ropes, w)
    up1, down1 = finish_gather(ffn1_handle, chip, "ffn1", x3)
    w.update(ffn_w_up=[up0, up1], ffn_w_down=[w["ffn_w_down"][0], merge(down1)])
    x4, ffn1 = _ffn_fwd(x3, w, 1, sc2b, sh2b, g2b)
    dres, d_final, loss = final_fwd_bwd(x4, w["final_norm_g"], tgt)

    dres, gf1, dm2b = _ffn_bwd(dres, x3, ffn1, w, 1, sc2b, g2b)
    ffn1_red, tok = start_reduce([gf1["ffn_w_up"], _rows_major(gf1["ffn_w_down"])], ci, "ffn1")
    dres, gm1, dm1b = mixer1_bwd(dres, mix1[:-1] + (mix1[-1] + tok,), ropes, w)
    gm1 = unprepare_grads(gm1)
    mix1_red, tok = start_reduce([cm16(gm1["cd_w_in"]), cm16(gm1["c_w_uq"]), cm16(gm1["c_w_ukv"]),
                                  _rows_major(gm1["cd_w_out"]).astype(BF16)], ci, "mix1")
    red_up1, red_down1 = finish_reduce(ffn1_red, chip, ci, "ffn1", dres)
    dres, gf0, dm2a = _ffn_bwd(dres, x1, ffn0, w, 0, sc2a, g2a + tok)
    ffn0_red, tok = start_reduce([gf0["ffn_w_up"], _rows_major(gf0["ffn_w_down"])], ci, "ffn0")
    red_cd_in, red_uq, red_ukv, red_cd_out = finish_reduce(mix1_red, chip, ci, "mix1", dres)
    grad_x, gm0, dm1a = mixer0_bwd(dres, mix0[:-1] + (mix0[-1] + tok,), w)
    grads = _merge_layer_grads({**gf0, **gm0}, {**gf1, **gm1})
    grads["final_norm_g"] = d_final
    dmod = jnp.concatenate([jnp.concatenate(dm1a + dm2a, axis=1), jnp.concatenate(dm1b + dm2b, axis=1)], axis=0)

    parts3 = [dmod] + [grads[n] for n, _ in _SMALL_GRADS] + [loss[0, 0]]
    rows3 = -(-sum(p.size for p in parts3) // LANES // 8) * 8
    small_handle, tok = split_start("small_grads_start", EVERYONE, [_pack_rows(parts3, rows3, F32)],
                                    [_sds((N_DEV, rows3, LANES))])
    mix0_red, _ = start_reduce([gm0["ab_w_in"], _rows_major(gm0["ab_w_out"]) + tok.astype(BF16)], ci, "mix0")
    red_up0, red_down0 = finish_reduce(ffn0_red, chip, ci, "ffn0", grad_x)
    out_grads = dict(cd_w_in=red_cd_in, c_w_uq=red_uq, c_w_ukv=red_ukv, cd_w_out=red_cd_out)
    per_layer = dict(ffn_w_up=(red_up0, red_up1), ffn_w_down=(red_down0, red_down1))
    updates = {}

    def update(n):
        if n in per_layer:
            updates[n] = adamw_layers(a[n], *per_layer[n], a["m_" + n], a["v_" + n], "adamw_" + n)
        else:
            updates[n] = adamw(a[n], out_grads[n].reshape(a[n].shape), a["m_" + n], a["v_" + n], "adamw_" + n)

    for n in ("ffn_w_up", "ffn_w_down", "cd_w_in", "c_w_uq", "c_w_ukv", "cd_w_out"):
        update(n)
    (mine,), (landed,) = split_wait("small_grads_wait", EVERYONE, small_handle, updates["ffn_w_down"][1])
    g3 = lax.dynamic_update_index_in_dim(landed, mine, dev, 0)
    summed = sum8(g3).reshape(-1)
    nmod = 2 * N_MOD * D_MODEL
    out_grads["ada_b"] = summed[:nmod].reshape(2, N_MOD * D_MODEL)
    off = nmod
    for n, shp in _SMALL_GRADS:
        out_grads[n] = summed[off:off + _size(shp)].reshape(shp)
        off += _size(shp)
    loss = summed[off]
    for n, shp, axis in _SMALL_SHARDED:
        width = out_grads[n].shape[-1] // N_CHIPS
        out_grads[n] = lax.dynamic_slice_in_dim(out_grads[n], chip * width, width, axis=out_grads[n].ndim - 1)
    dmod_all = g3.reshape(N_DEV, rows3 * LANES)[:, :nmod].reshape(N_DEV, 2, N_MOD * D_MODEL)
    dmod_mine = lax.dynamic_slice_in_dim(dmod_all, chip * ncol, ncol, axis=2).transpose(1, 0, 2)
    out_grads["ada_w"] = ada_grad(c_all.T, dmod_mine)

    red_in0, red_out0 = finish_reduce(mix0_red, chip, ci, "mix0", out_grads["ada_w"])
    out_grads.update(ab_w_in=red_in0, ab_w_out=red_out0)

    for n in _WEIGHTS:
        if n not in updates:
            update(n)
    return (loss, grad_x[None], *[updates[n][i] for i in range(4) for n in _WEIGHTS])
```

```python
import functools

import jax
import jax.numpy as jnp
from jax import lax
from jax.experimental import pallas as pl
from jax.experimental.pallas import tpu as pltpu

F32 = jnp.float32
BF16 = jnp.bfloat16
EPS = 1e-6
D_MODEL = 1024
N_MOD = 6
A_WIDTH = 512
B_GROUPS = 4
POOL_WINDOWS = (2, 4, 8, 16)
C_HEADS = 8
C_NOPE = 64
C_ROPE = 32
C_V = 64
C_Q_RANK = 256
C_KV_RANK = 128
HEAD_PAD = 128
ROPE_THETA = 10000.0
D_GROUPS = 4
D_CHUNK = 128
D_FF = 2816
FF_UNIT = 128
ADAM_LR = 0.001
ADAM_B1 = 0.9
ADAM_B2 = 0.999
ADAM_EPS = 1e-08
ADAM_WD = 0.01
ADAM_STEP = 10
N_CHIPS = 4
N_DEV = 8
LANES = 128
VMEM_BIG = 56 * 1024 * 1024
MESH = pl.DeviceIdType.MESH


def _sds(shape, dtype=F32):
    return jax.ShapeDtypeStruct(tuple(shape), dtype)


def _tile(n, cap, mult=128):
    if n <= cap:
        return n
    best = None
    for t in range(mult, cap + 1, mult):
        if n % t == 0:
            best = t
    assert best is not None, (n, cap, mult)
    return best


def _params(dims=None, vmem=None):
    return pltpu.CompilerParams(dimension_semantics=dims, vmem_limit_bytes=vmem)


def _shift_down(v, k):
    r = pltpu.roll(v, k, axis=0)
    t = lax.broadcasted_iota(jnp.int32, v.shape, 0)
    return jnp.where(t >= k, r, 0.0)


def _shift_up(v, k):
    n = v.shape[0]
    r = pltpu.roll(v, n - k, axis=0)
    t = lax.broadcasted_iota(jnp.int32, v.shape, 0)
    return jnp.where(t < n - k, r, 0.0)


def _sigmoid(v):
    return 1.0 / (1.0 + jnp.exp(-v))


_GELU_C = 0.7978845608028654
_GELU_A = 0.044715


def _gelu(v):
    return 0.5 * v * (1.0 + jnp.tanh(_GELU_C * (v + _GELU_A * v * v * v)))


def _gelu_grad(v):
    th = jnp.tanh(_GELU_C * (v + _GELU_A * v * v * v))
    return 0.5 * (1.0 + th) + 0.5 * v * (1.0 - th * th) * _GELU_C * (1.0 + 3.0 * _GELU_A * v * v)


_NN = (((1,), (0,)), ((), ()))
_NT = (((1,), (1,)), ((), ()))
_TN = (((0,), (0,)), ((), ()))


def _dot(a, b, dims=_NN):
    return lax.dot_general(a, b, dims, preferred_element_type=F32)


def _logical(t, groups):
    return (t.shape[-2], t.shape[-1] * groups)


def _block(tr, tc, groups, cols, where):
    if groups == 1:
        return pl.BlockSpec((tr, tc), where)
    per = cols // groups // tc

    def index(i, j, s):
        r, c = where(i, j, s)
        return (c // per, r, c % per)

    return pl.BlockSpec((None, tr, tc), index)


def matmul(a, b, mode, out_dtype, name, ga=1, gb=1, go=1, tm=None, tn=None, tk=None):
    (ar, ac), (br, bc) = _logical(a, ga), _logical(b, gb)
    if mode == "nn":
        m, k, n = ar, ac, bc
        a_col, b_col = "k", "n"
    elif mode == "nt":
        m, k, n = ar, ac, br
        a_col, b_col = "k", "k"
    else:
        k, m, n = ar, ac, bc
        a_col, b_col = "m", "n"
    limit = {"m": m, "n": n // go, "k": k}
    limit[a_col] = min(limit[a_col], ac // ga)
    limit[b_col] = min(limit[b_col], bc // gb)
    tm = tm or _tile(limit["m"], 1024, 128 if mode == "tn" else 16)
    tn = tn or _tile(limit["n"], 512)
    tk = tk or _tile(limit["k"], 2048, 16 if mode == "tn" else 128)
    nk = k // tk
    if mode == "nn":
        a_spec = _block(tm, tk, ga, ac, lambda i, j, s: (i, s))
        b_spec = _block(tk, tn, gb, bc, lambda i, j, s: (s, j))
        dims = _NN
    elif mode == "nt":
        a_spec = _block(tm, tk, ga, ac, lambda i, j, s: (i, s))
        b_spec = _block(tn, tk, gb, bc, lambda i, j, s: (j, s))
        dims = _NT
    else:
        a_spec = _block(tk, tm, ga, ac, lambda i, j, s: (s, i))
        b_spec = _block(tk, tn, gb, bc, lambda i, j, s: (s, j))
        dims = _TN
    o_spec = _block(tm, tn, go, n, lambda i, j, s: (i, j))
    out_shape = _sds((m, n), out_dtype) if go == 1 else _sds((go, m, n // go), out_dtype)

    def body(a_ref, b_ref, o_ref, acc_ref):
        s = pl.program_id(2)

        @pl.when(s == 0)
        def _():
            acc_ref[...] = jnp.zeros_like(acc_ref)

        acc_ref[...] += _dot(a_ref[...], b_ref[...], dims)

        @pl.when(s == nk - 1)
        def _():
            o_ref[...] = acc_ref[...].astype(o_ref.dtype)

    return pl.pallas_call(
        body, name=name, out_shape=out_shape, grid=(m // tm, n // tn, nk),
        in_specs=[a_spec, b_spec], out_specs=o_spec,
        scratch_shapes=[pltpu.VMEM((tm, tn), F32)],
        compiler_params=_params(("parallel", "parallel", "arbitrary"), VMEM_BIG),
    )(a, b)


def _rows(tm, n):
    return pl.BlockSpec((tm, n), lambda i: (i, 0))


def _vec(n):
    return pl.BlockSpec((1, n), lambda i: (0, 0))


def modnorm_fwd(x, g, sc, sh, name):
    s, d = x.shape
    tm = _tile(s, 256, 8)

    def body(x_ref, g_ref, sc_ref, sh_ref, o_ref):
        xv = x_ref[...]
        r = lax.rsqrt(jnp.mean(xv * xv, axis=-1, keepdims=True) + EPS)
        o_ref[...] = ((xv * r) * g_ref[...] * (1.0 + sc_ref[...]) + sh_ref[...]).astype(BF16)

    return pl.pallas_call(
        body, name=name, out_shape=_sds((s, d), BF16), grid=(s // tm,),
        in_specs=[_rows(tm, d), _vec(d), _vec(d), _vec(d)], out_specs=_rows(tm, d),
        compiler_params=_params(("parallel",)),
    )(x, g, sc, sh)


def resid_fwd(x, y, gate, name):
    s, d = x.shape
    tm = _tile(s, 256, 8)

    def body(x_ref, y_ref, g_ref, o_ref):
        o_ref[...] = x_ref[...] + g_ref[...] * y_ref[...]

    return pl.pallas_call(
        body, name=name, out_shape=_sds((s, d)), grid=(s // tm,),
        in_specs=[_rows(tm, d), _rows(tm, d), _vec(d)], out_specs=_rows(tm, d),
        compiler_params=_params(("parallel",)),
    )(x, y, gate)


def gate_bwd(dres, y, gate, name):
    s, d = dres.shape
    tm = _tile(s, 256, 8)

    def body(dr_ref, y_ref, g_ref, dy_ref, dg_ref):
        @pl.when(pl.program_id(0) == 0)
        def _():
            dg_ref[...] = jnp.zeros_like(dg_ref)

        dr = dr_ref[...]
        dy_ref[...] = (dr * g_ref[...]).astype(BF16)
        dg_ref[...] += jnp.sum(dr * y_ref[...], axis=0, keepdims=True)

    return pl.pallas_call(
        body, name=name, out_shape=(_sds((s, d), BF16), _sds((1, d))), grid=(s // tm,),
        in_specs=[_rows(tm, d), _rows(tm, d), _vec(d)], out_specs=(_rows(tm, d), _vec(d)),
        compiler_params=_params(("arbitrary",)),
    )(dres, y, gate)


def norm_bwd(x, dh, g, sc, dres, name):
    s, d = x.shape
    tm = _tile(s, 256, 8)
    nsteps = s // tm

    def body(x_ref, dh_ref, g_ref, sc_ref, dr_ref, dx_ref, dsh_ref, dsc_ref, dg_ref, a2_ref):
        i = pl.program_id(0)

        @pl.when(i == 0)
        def _():
            dsh_ref[...] = jnp.zeros_like(dsh_ref)
            a2_ref[...] = jnp.zeros_like(a2_ref)

        xv = x_ref[...]
        dh = dh_ref[...]
        r = lax.rsqrt(jnp.mean(xv * xv, axis=-1, keepdims=True) + EPS)
        xh = xv * r
        dsh_ref[...] += jnp.sum(dh, axis=0, keepdims=True)
        a2_ref[...] += jnp.sum(dh * xh, axis=0, keepdims=True)
        dxh = dh * (g_ref[...] * (1.0 + sc_ref[...]))
        dx = r * (dxh - xh * jnp.mean(dxh * xh, axis=-1, keepdims=True))
        dx_ref[...] = dr_ref[...] + dx

        @pl.when(i == nsteps - 1)
        def _():
            dsc_ref[...] = a2_ref[...] * g_ref[...]
            dg_ref[...] = a2_ref[...] * (1.0 + sc_ref[...])

    return pl.pallas_call(
        body, name=name, out_shape=(_sds((s, d)), _sds((1, d)), _sds((1, d)), _sds((1, d))), grid=(nsteps,),
        in_specs=[_rows(tm, d), _rows(tm, d), _vec(d), _vec(d), _rows(tm, d)],
        out_specs=(_rows(tm, d), _vec(d), _vec(d), _vec(d)),
        scratch_shapes=[pltpu.VMEM((1, d), F32)],
        compiler_params=_params(("arbitrary",)),
    )(x, dh, g, sc, dres)


def final_fwd_bwd(x, g, tgt):
    s, d = x.shape
    tm = _tile(s, 256, 8)

    def body(x_ref, g_ref, t_ref, dx_ref, dg_ref, loss_ref):
        @pl.when(pl.program_id(0) == 0)
        def _():
            dg_ref[...] = jnp.zeros_like(dg_ref)
            loss_ref[...] = jnp.zeros_like(loss_ref)

        xv = x_ref[...]
        gv = g_ref[...]
        r = lax.rsqrt(jnp.mean(xv * xv, axis=-1, keepdims=True) + EPS)
        xh = xv * r
        e = xh * gv - t_ref[...]
        row = jnp.sum(e * e, axis=-1, keepdims=True) * (0.5 / d)
        loss_ref[...] += jnp.sum(row, axis=0, keepdims=True)
        dy = e * (1.0 / d)
        dg_ref[...] += jnp.sum(dy * xh, axis=0, keepdims=True)
        dxh = dy * gv
        dx_ref[...] = r * (dxh - xh * jnp.mean(dxh * xh, axis=-1, keepdims=True))

    return pl.pallas_call(
        body, name="final_fwd_bwd", out_shape=(_sds((s, d)), _sds((1, d)), _sds((1, LANES))), grid=(s // tm,),
        in_specs=[_rows(tm, d), _vec(d), _rows(tm, d)], out_specs=(_rows(tm, d), _vec(d), _vec(LANES)),
        compiler_params=_params(("arbitrary",)),
    )(x, g, tgt)


def _taps(v):
    return _shift_down(v, 2), _shift_down(v, 1), v


def _conv3_taps(taps, w):
    return w[0:1, :] * taps[0] + w[1:2, :] * taps[1] + w[2:3, :] * taps[2]


def _conv3(v, w):
    return _conv3_taps(_taps(v), w)


def _conv3_t(dv, w):
    return w[0:1, :] * _shift_up(dv, 2) + w[1:2, :] * _shift_up(dv, 1) + w[2:3, :] * dv


def _conv3_dw_taps(dv, taps):
    return jnp.concatenate([jnp.sum(dv * t, axis=0, keepdims=True) for t in taps], axis=0)


def _conv3_dw(dv, v):
    return _conv3_dw_taps(dv, _taps(v))


def gconv_fwd(z, conv_w):
    s = z.shape[0]
    nb = A_WIDTH // LANES

    def body(b_ref, c_ref, a_ref, w_ref, o_ref):
        b, c, a = b_ref[...].astype(F32), c_ref[...].astype(F32), a_ref[...].astype(F32)
        o_ref[...] = (b * _conv3(c * a, w_ref[...])).astype(BF16)

    col = lambda off: pl.BlockSpec((s, LANES), lambda j: (0, off + j))
    return pl.pallas_call(
        body, name="gconv_fwd", out_shape=_sds((s, A_WIDTH), BF16), grid=(nb,),
        in_specs=[col(0), col(nb), col(2 * nb), pl.BlockSpec((3, LANES), lambda j: (0, j))],
        out_specs=pl.BlockSpec((s, LANES), lambda j: (0, j)),
        compiler_params=_params(("parallel",), VMEM_BIG),
    )(z, z, z, conv_w)


def gconv_bwd(z, conv_w, dycat):
    s = z.shape[0]
    nb = A_WIDTH // LANES

    def body(b_ref, c_ref, a_ref, w_ref, dy_ref, db_ref, dc_ref, da_ref, dw_ref):
        c, a, w, dy = c_ref[...].astype(F32), a_ref[...].astype(F32), w_ref[...], dy_ref[...].astype(F32)
        ca = c * a
        db_ref[...] = (dy * _conv3(ca, w)).astype(BF16)
        dconv = dy * b_ref[...].astype(F32)
        dw_ref[...] = _conv3_dw(dconv, ca)
        dca = _conv3_t(dconv, w)
        dc_ref[...] = (dca * a).astype(BF16)
        da_ref[...] = (dca * c).astype(BF16)

    col = lambda off: pl.BlockSpec((s, LANES), lambda j: (0, off + j))
    wspec = pl.BlockSpec((3, LANES), lambda j: (0, j))
    part = _sds((s, A_WIDTH), BF16)
    return pl.pallas_call(
        body, name="gconv_bwd", out_shape=(part, part, part, _sds((3, A_WIDTH))), grid=(nb,),
        in_specs=[col(0), col(nb), col(2 * nb), wspec, col(0)],
        out_specs=(col(0), col(0), col(0), wspec),
        compiler_params=_params(("parallel",), VMEM_BIG),
    )(z, z, z, conv_w, dycat)


def _pool_counts(s, w):
    t = lax.broadcasted_iota(jnp.int32, (s, 1), 0)
    return jnp.minimum(t + 1, w).astype(F32)


def _pooled(p, levels):
    acc = p
    for lv in range(levels):
        acc = acc + _shift_down(acc, 2 ** lv)
    return acc / _pool_counts(p.shape[0], 2 ** levels) - p


def pool_fwd(z, mix_w, scale):
    s = z.shape[0]

    def make(g):
        def body_g(p_ref, m_ref, sc_ref, o_ref):
            pooled = _pooled(p_ref[...].astype(F32), g + 1)
            y = _dot(pooled.astype(BF16), m_ref[...].astype(BF16))
            o_ref[...] = (y * sc_ref[...]).astype(BF16)
        return body_g

    outs = []
    for g in range(B_GROUPS):
        outs.append(pl.pallas_call(
            make(g), name=f"pool_fwd{g}", out_shape=_sds((s, LANES), BF16), grid=(1,),
            in_specs=[pl.BlockSpec((s, LANES), lambda i, g=g: (0, 3 * (A_WIDTH // LANES) + g)),
                      pl.BlockSpec((None, LANES, LANES), lambda i, g=g: (g, 0, 0)),
                      pl.BlockSpec((1, LANES), lambda i, g=g: (0, g))],
            out_specs=pl.BlockSpec((s, LANES), lambda i: (0, 0)),
            compiler_params=_params(("arbitrary",), VMEM_BIG),
        )(z, mix_w, scale))
    return outs


def pool_bwd(z, mix_w, scale, dycat):
    s = z.shape[0]

    def make(g):
        w = 2 ** (g + 1)

        def body_g(p_ref, m_ref, sc_ref, dy_ref, dp_ref, dm_ref, dsc_ref):
            pooled = _pooled(p_ref[...].astype(F32), g + 1)
            mw = m_ref[...].astype(BF16)
            pb = pooled.astype(BF16)
            dy = dy_ref[...].astype(F32)
            dsc_ref[...] = jnp.sum(dy * _dot(pb, mw), axis=0, keepdims=True)
            dmix = (dy * sc_ref[...]).astype(BF16)
            dm_ref[...] = _dot(pb, dmix, _TN)
            dpool = _dot(dmix, mw, _NT)
            acc = dpool / _pool_counts(s, w)
            for lv in range(g + 1):
                acc = acc + _shift_up(acc, 2 ** lv)
            dp_ref[...] = (acc - dpool).astype(BF16)
        return body_g

    outs = []
    for g in range(B_GROUPS):
        outs.append(pl.pallas_call(
            make(g), name=f"pool_bwd{g}",
            out_shape=(_sds((s, LANES), BF16), _sds((LANES, LANES)), _sds((1, LANES))), grid=(1,),
            in_specs=[pl.BlockSpec((s, LANES), lambda i, g=g: (0, 3 * (A_WIDTH // LANES) + g)),
                      pl.BlockSpec((None, LANES, LANES), lambda i, g=g: (g, 0, 0)),
                      pl.BlockSpec((1, LANES), lambda i, g=g: (0, g)),
                      pl.BlockSpec((s, LANES), lambda i, g=g: (0, A_WIDTH // LANES + g))],
            out_specs=(pl.BlockSpec((s, LANES), lambda i: (0, 0)), pl.BlockSpec((LANES, LANES), lambda i: (0, 0)),
                       pl.BlockSpec((1, LANES), lambda i: (0, 0))),
            compiler_params=_params(("arbitrary",), VMEM_BIG),
        )(z, mix_w, scale, dycat))
    return outs


_FF_BLOCKS = D_FF // FF_UNIT


def _ff_spec(s):
    return pl.BlockSpec((2, s, FF_UNIT), lambda j: (0, 0, j))


def _ff_wspecs():
    return [pl.BlockSpec((3, FF_UNIT), lambda j: (0, j)), pl.BlockSpec((3, FF_UNIT), lambda j: (0, _FF_BLOCKS + j))]


_FF_ROWS = 64
_FF_HALO = 16


def _chunk_taps(z_ref, half, c):
    start = pl.multiple_of(c * _FF_ROWS, _FF_ROWS)
    before = pl.multiple_of(jnp.maximum(c * _FF_ROWS - _FF_HALO, 0), _FF_HALO)
    halo = z_ref[half, pl.ds(before, _FF_HALO), :].astype(F32)
    halo = jnp.where(c > 0, halo, 0.0)
    win = jnp.concatenate([halo, z_ref[half, pl.ds(start, _FF_ROWS), :].astype(F32)], axis=0)
    return tuple(pltpu.roll(win, k, axis=0)[_FF_HALO:] for k in (2, 1)) + (win[_FF_HALO:],)


def _fold8(v):
    acc = v[0:8]
    for r in range(8, v.shape[0], 8):
        acc = acc + v[r:r + 8]
    return acc


def ffn_act_fwd(zf, conv_w, name):
    s = zf.shape[1]
    assert s % _FF_ROWS == 0

    def body(z_ref, wg_ref, wu_ref, o_ref):
        wg, wu = wg_ref[...], wu_ref[...]

        def chunk(c, carry):
            g = _conv3_taps(_chunk_taps(z_ref, 0, c), wg)
            u = _conv3_taps(_chunk_taps(z_ref, 1, c), wu)
            o_ref[pl.ds(pl.multiple_of(c * _FF_ROWS, _FF_ROWS), _FF_ROWS), :] = (g * _sigmoid(g) * u).astype(BF16)
            return carry

        lax.fori_loop(0, s // _FF_ROWS, chunk, 0)

    return pl.pallas_call(
        body, name=name, out_shape=_sds((s, D_FF), BF16), grid=(_FF_BLOCKS,),
        in_specs=[_ff_spec(s)] + _ff_wspecs(), out_specs=pl.BlockSpec((s, FF_UNIT), lambda j: (0, j)),
        compiler_params=_params(("parallel",), VMEM_BIG),
    )(zf, conv_w, conv_w)


def ffn_act_bwd(zf, conv_w, da, name):
    s = zf.shape[1]
    assert s % _FF_ROWS == 0
    nchunks = s // _FF_ROWS

    def body(z_ref, wg_ref, wu_ref, da_ref, dz_ref, dw_ref, dg_ref, du_ref):
        wg, wu = wg_ref[...], wu_ref[...]

        def first(c, acc):
            rows = pl.ds(pl.multiple_of(c * _FF_ROWS, _FF_ROWS), _FF_ROWS)
            tg, tu = _chunk_taps(z_ref, 0, c), _chunk_taps(z_ref, 1, c)
            g = _conv3_taps(tg, wg)
            u = _conv3_taps(tu, wu)
            dav = da_ref[rows, :].astype(F32)
            sg = _sigmoid(g)
            dg = dav * u * (sg * (1.0 + g * (1.0 - sg)))
            du = dav * (g * sg)
            dg_ref[rows, :] = dg
            du_ref[rows, :] = du
            return tuple(a + _fold8(d * t) for a, (d, t) in zip(acc, [(dg, t) for t in tg] + [(du, t) for t in tu]))

        zero = jnp.zeros((8, FF_UNIT), F32)
        acc = lax.fori_loop(0, nchunks, first, (zero,) * 6)
        sums = [jnp.sum(a, axis=0, keepdims=True) for a in acc]
        dw_ref[0] = jnp.concatenate(sums[:3], axis=0)
        dw_ref[1] = jnp.concatenate(sums[3:], axis=0)

        tail = pl.ds(s, _FF_HALO)
        dg_ref[tail, :] = jnp.zeros((_FF_HALO, FF_UNIT), F32)
        du_ref[tail, :] = jnp.zeros((_FF_HALO, FF_UNIT), F32)
        span = _FF_ROWS + _FF_HALO

        def second(c, carry):
            start = pl.multiple_of(c * _FF_ROWS, _FF_ROWS)
            for half, (d_ref, w) in enumerate(((dg_ref, wg), (du_ref, wu))):
                win = d_ref[pl.ds(start, span), :]
                dz = (w[0:1, :] * pltpu.roll(win, span - 2, axis=0)[:_FF_ROWS]
                      + w[1:2, :] * pltpu.roll(win, span - 1, axis=0)[:_FF_ROWS] + w[2:3, :] * win[:_FF_ROWS])
                dz_ref[half, pl.ds(start, _FF_ROWS), :] = dz.astype(BF16)
            return carry

        lax.fori_loop(0, nchunks, second, 0)

    return pl.pallas_call(
        body, name=name, out_shape=(_sds((2, s, D_FF), BF16), _sds((2, 3, D_FF))), grid=(_FF_BLOCKS,),
        in_specs=[_ff_spec(s)] + _ff_wspecs() + [pl.BlockSpec((s, FF_UNIT), lambda j: (0, j))],
        out_specs=(_ff_spec(s), pl.BlockSpec((2, 3, FF_UNIT), lambda j: (0, 0, j))),
        scratch_shapes=[pltpu.VMEM((s + _FF_HALO, FF_UNIT), F32), pltpu.VMEM((s + _FF_HALO, FF_UNIT), F32)],
        compiler_params=_params(("parallel",), VMEM_BIG),
    )(zf, conv_w, conv_w, da)


def _rope(v, cs, s1, s2):
    return v * cs + pltpu.roll(v, LANES - C_ROPE // 2, axis=1) * s1 + pltpu.roll(v, C_ROPE // 2, axis=1) * s2


def _rope_t(dv, cs, s1, s2):
    return dv * cs + pltpu.roll(dv * s1, C_ROPE // 2, axis=1) + pltpu.roll(dv * s2, LANES - C_ROPE // 2, axis=1)


def _kpe_mask(shape):
    lane = lax.broadcasted_iota(jnp.int32, shape, 1)
    return (lane >= C_NOPE) & (lane < C_NOPE + C_ROPE)


def _rms(v, g):
    r = lax.rsqrt(jnp.mean(v * v, axis=-1, keepdims=True) + EPS)
    return v * r, r


def _rms_bwd(dn, xh, r, g):
    dxh = dn * g
    return r * (dxh - xh * jnp.mean(dxh * xh, axis=-1, keepdims=True)), jnp.sum(dn * xh, axis=0, keepdims=True)


_ZQ = C_Q_RANK + C_KV_RANK + HEAD_PAD
_HW = C_HEADS * HEAD_PAD


def mla_pre_fwd(z, gq, gkv, wq, wk, wv, cs, s1, s2):
    s = z.shape[0]
    tm = _tile(s, 256, 8)

    def body(z_ref, gq_ref, gkv_ref, wq_ref, wk_ref, wv_ref, cs_ref, s1_ref, s2_ref, q_ref, k_ref, v_ref):
        zv = z_ref[...].astype(F32)
        cst, s1t, s2t = cs_ref[...], s1_ref[...], s2_ref[...]
        qh, _ = _rms(zv[:, :C_Q_RANK], None)
        qn = (qh * gq_ref[...]).astype(BF16)
        q = _dot(qn, wq_ref[...])
        kh, _ = _rms(zv[:, C_Q_RANK:C_Q_RANK + C_KV_RANK], None)
        kvn = (kh * gkv_ref[...]).astype(BF16)
        k = _dot(kvn, wk_ref[...])
        v_ref[...] = _dot(kvn, wv_ref[...]).astype(BF16)
        kpe = _rope(zv[:, C_Q_RANK + C_KV_RANK:], cst, s1t, s2t)
        for h in range(C_HEADS):
            sl = slice(h * HEAD_PAD, (h + 1) * HEAD_PAD)
            q_ref[:, sl] = _rope(q[:, sl], cst, s1t, s2t).astype(BF16)
            k_ref[:, sl] = (k[:, sl] + kpe).astype(BF16)

    full = lambda r, c: pl.BlockSpec((r, c), lambda i: (0, 0))
    hw = _sds((s, _HW), BF16)
    return pl.pallas_call(
        body, name="mla_pre_fwd", out_shape=(hw, hw, hw), grid=(s // tm,),
        in_specs=[_rows(tm, _ZQ), _vec(C_Q_RANK), _vec(C_KV_RANK), full(C_Q_RANK, _HW), full(C_KV_RANK, _HW),
                  full(C_KV_RANK, _HW), _rows(tm, LANES), _rows(tm, LANES), _rows(tm, LANES)],
        out_specs=(_rows(tm, _HW), _rows(tm, _HW), _rows(tm, _HW)),
        compiler_params=_params(("parallel",), VMEM_BIG),
    )(z, gq, gkv, wq, wk, wv, cs, s1, s2)


def mla_pre_bwd(z, gq, gkv, wq, wk, wv, cs, s1, s2, dq, dk, dv):
    s = z.shape[0]
    tm = _tile(s, 256, 8)

    def body(z_ref, gq_ref, gkv_ref, wq_ref, wk_ref, wv_ref, cs_ref, s1_ref, s2_ref, dq_ref, dk_ref, dv_ref,
             dz_ref, dwq_ref, dwk_ref, dwv_ref, dgq_ref, dgkv_ref):
        @pl.when(pl.program_id(0) == 0)
        def _():
            dwq_ref[...] = jnp.zeros_like(dwq_ref)
            dwk_ref[...] = jnp.zeros_like(dwk_ref)
            dwv_ref[...] = jnp.zeros_like(dwv_ref)
            dgq_ref[...] = jnp.zeros_like(dgq_ref)
            dgkv_ref[...] = jnp.zeros_like(dgkv_ref)

        zv = z_ref[...].astype(F32)
        cst, s1t, s2t = cs_ref[...], s1_ref[...], s2_ref[...]
        gqv, gkvv = gq_ref[...], gkv_ref[...]
        qh, rq = _rms(zv[:, :C_Q_RANK], None)
        qn = (qh * gqv).astype(BF16)
        kh, rk = _rms(zv[:, C_Q_RANK:C_Q_RANK + C_KV_RANK], None)
        kvn = (kh * gkvv).astype(BF16)

        dqv = dq_ref[...].astype(F32)
        dqp = jnp.concatenate(
            [_rope_t(dqv[:, h * HEAD_PAD:(h + 1) * HEAD_PAD], cst, s1t, s2t) for h in range(C_HEADS)], axis=1
        ).astype(BF16)
        dwq_ref[...] += _dot(qn, dqp, _TN)
        dqn = _dot(dqp, wq_ref[...], _NT)
        dql, dgq = _rms_bwd(dqn, qh, rq, gqv)
        dgq_ref[...] += dgq

        dkv = dk_ref[...]
        dkb = dkv.astype(BF16)
        dvb = dv_ref[...].astype(BF16)
        dwk_ref[...] += _dot(kvn, dkb, _TN)
        dwv_ref[...] += _dot(kvn, dvb, _TN)
        dkvn = _dot(dkb, wk_ref[...], _NT) + _dot(dvb, wv_ref[...], _NT)
        dkl, dgkv = _rms_bwd(dkvn, kh, rk, gkvv)
        dgkv_ref[...] += dgkv

        dkpe = dkv[:, :HEAD_PAD]
        for h in range(1, C_HEADS):
            dkpe = dkpe + dkv[:, h * HEAD_PAD:(h + 1) * HEAD_PAD]
        dkpe = _rope_t(jnp.where(_kpe_mask(dkpe.shape), dkpe, 0.0), cst, s1t, s2t)
        dz_ref[...] = jnp.concatenate([dql, dkl, dkpe], axis=1).astype(BF16)

    full = lambda r, c: pl.BlockSpec((r, c), lambda i: (0, 0))
    return pl.pallas_call(
        body, name="mla_pre_bwd",
        out_shape=(_sds((s, _ZQ), BF16), _sds((C_Q_RANK, _HW)), _sds((C_KV_RANK, _HW)), _sds((C_KV_RANK, _HW)),
                   _sds((1, C_Q_RANK)), _sds((1, C_KV_RANK))),
        grid=(s // tm,),
        in_specs=[_rows(tm, _ZQ), _vec(C_Q_RANK), _vec(C_KV_RANK), full(C_Q_RANK, _HW), full(C_KV_RANK, _HW),
                  full(C_KV_RANK, _HW), _rows(tm, LANES), _rows(tm, LANES), _rows(tm, LANES),
                  _rows(tm, _HW), _rows(tm, _HW), _rows(tm, _HW)],
        out_specs=(_rows(tm, _ZQ), full(C_Q_RANK, _HW), full(C_KV_RANK, _HW), full(C_KV_RANK, _HW),
                   _vec(C_Q_RANK), _vec(C_KV_RANK)),
        compiler_params=_params(("arbitrary",), VMEM_BIG),
    )(z, gq, gkv, wq, wk, wv, cs, s1, s2, dq, dk, dv)


_ATT_SCALE = (C_NOPE + C_ROPE) ** -0.5
_NEG = -1e30


def _att_probs(q, k, row0):
    sc = _dot(q, k, _NT) * _ATT_SCALE
    qpos = row0 + lax.broadcasted_iota(jnp.int32, sc.shape, 0)
    kpos = lax.broadcasted_iota(jnp.int32, sc.shape, 1)
    sc = jnp.where(kpos <= qpos, sc, _NEG)
    e = jnp.exp(sc - jnp.max(sc, axis=-1, keepdims=True))
    return e / jnp.sum(e, axis=-1, keepdims=True)


def _causal_cases(i, nq, tq, fn):
    if nq > 8:
        fn(nq * tq)
        return
    for blk in range(nq):
        pl.when(i == blk)(functools.partial(fn, (blk + 1) * tq))


def attn_fwd(q, k, v):
    s = q.shape[0]
    tq = _tile(s, 256, 8)
    nq = s // tq

    def body(q_ref, k_ref, v_ref, o_ref):
        i = pl.program_id(1)

        def case(nk):
            p = _att_probs(q_ref[...], k_ref[:nk, :], i * tq)
            o_ref[...] = _dot(p.astype(BF16), v_ref[:nk, :]).astype(BF16)

        _causal_cases(i, nq, tq, case)

    qspec = pl.BlockSpec((tq, HEAD_PAD), lambda h, i: (i, h))
    kspec = pl.BlockSpec((s, HEAD_PAD), lambda h, i: (0, h))
    return pl.pallas_call(
        body, name="attn_fwd", out_shape=_sds((s, _HW), BF16), grid=(C_HEADS, s // tq),
        in_specs=[qspec, kspec, kspec], out_specs=qspec,
        compiler_params=_params(("parallel", "parallel"), VMEM_BIG),
    )(q, k, v)


def attn_bwd(q, k, v, o, do_all, do_col0):
    s = q.shape[0]
    tq = _tile(s, 256, 8)

    def body(q_ref, k_ref, v_ref, o_ref, do_ref, dq_ref, dk_ref, dv_ref):
        i = pl.program_id(1)

        @pl.when(i == 0)
        def _():
            dk_ref[...] = jnp.zeros_like(dk_ref)
            dv_ref[...] = jnp.zeros_like(dv_ref)

        def case(nk):
            qv, kv, vv, dov = q_ref[...], k_ref[:nk, :], v_ref[:nk, :], do_ref[...]
            p = _att_probs(qv, kv, i * tq)
            dp = _dot(dov, vv, _NT)
            delta = jnp.sum(dov.astype(F32) * o_ref[...].astype(F32), axis=-1, keepdims=True)
            ds = (p * (dp - delta) * _ATT_SCALE).astype(BF16)
            dq_ref[...] = _dot(ds, kv).astype(BF16)
            dk_ref[:nk, :] += _dot(ds, qv, _TN)
            dv_ref[:nk, :] += _dot(p.astype(BF16), dov, _TN)

        _causal_cases(i, s // tq, tq, case)

    qspec = pl.BlockSpec((tq, HEAD_PAD), lambda h, i: (i, h))
    dospec = pl.BlockSpec((tq, HEAD_PAD), lambda h, i: (i, do_col0 + h))
    kspec = pl.BlockSpec((s, HEAD_PAD), lambda h, i: (0, h))
    return pl.pallas_call(
        body, name="attn_bwd", out_shape=(_sds((s, _HW), BF16), _sds((s, _HW)), _sds((s, _HW))),
        grid=(C_HEADS, s // tq),
        in_specs=[qspec, kspec, kspec, qspec, dospec], out_specs=(qspec, kspec, kspec),
        compiler_params=_params(("parallel", "arbitrary"), VMEM_BIG),
    )(q, k, v, o, do_all)


_DW = D_GROUPS * LANES


def _tril_bf16(w):
    r = lax.broadcasted_iota(jnp.int32, w.shape, 0)
    c = lax.broadcasted_iota(jnp.int32, w.shape, 1)
    return jnp.where(c <= r, w, 0.0).astype(BF16)


def _sgu_forward(zu, zv, lg, lb, ws_ref, bs):
    u = _gelu(zu)
    v = _gelu(zv)
    mu = jnp.mean(v, axis=-1, keepdims=True)
    vc = v - mu
    rstd = lax.rsqrt(jnp.mean(vc * vc, axis=-1, keepdims=True) + EPS)
    xh = vc * rstd
    vln = (xh * lg + lb).astype(BF16)
    mixed = []
    for g in range(D_GROUPS):
        wg = _tril_bf16(ws_ref[g])
        mixed.append(_dot(wg, vln[:, g * LANES:(g + 1) * LANES]) + bs[:, g:g + 1])
    return u, xh, rstd, vln, jnp.concatenate(mixed, axis=1)


def sgu_fwd(z, lg, lb, ws, bs_t):
    s = z.shape[0]
    nchunk = s // D_CHUNK

    def body(zu_ref, zv_ref, lg_ref, lb_ref, ws_ref, bs_ref, o_ref):
        u, _, _, _, mixed = _sgu_forward(zu_ref[...].astype(F32), zv_ref[...].astype(F32), lg_ref[...], lb_ref[...],
                                         ws_ref, bs_ref[...])
        o_ref[...] = (u * mixed).astype(BF16)

    return pl.pallas_call(
        body, name="sgu_fwd", out_shape=_sds((s, _DW), BF16), grid=(nchunk,),
        in_specs=[pl.BlockSpec((D_CHUNK, _DW), lambda n: (n, 1)), pl.BlockSpec((D_CHUNK, _DW), lambda n: (n, 2)),
                  _vec(_DW), _vec(_DW), pl.BlockSpec((D_GROUPS, D_CHUNK, D_CHUNK), lambda n: (0, 0, 0)),
                  pl.BlockSpec((D_CHUNK, LANES), lambda n: (0, 0))],
        out_specs=pl.BlockSpec((D_CHUNK, _DW), lambda n: (n, 0)),
        compiler_params=_params(("parallel",)),
    )(z, z, lg, lb, ws, bs_t)


def sgu_bwd(z, lg, lb, ws, bs_t, dycat, dy_col):
    s = z.shape[0]
    nchunk = s // D_CHUNK

    def body(zu_ref, zv_ref, lg_ref, lb_ref, ws_ref, bs_ref, dy_ref, dzu_ref, dzv_ref, dws_ref, dbs_ref, dlg_ref,
             dlb_ref):
        @pl.when(pl.program_id(0) == 0)
        def _():
            dws_ref[...] = jnp.zeros_like(dws_ref)
            dbs_ref[...] = jnp.zeros_like(dbs_ref)
            dlg_ref[...] = jnp.zeros_like(dlg_ref)
            dlb_ref[...] = jnp.zeros_like(dlb_ref)

        zu, zv, lg = zu_ref[...].astype(F32), zv_ref[...].astype(F32), lg_ref[...]
        u, xh, rstd, vln, mixed = _sgu_forward(zu, zv, lg, lb_ref[...], ws_ref, bs_ref[...])
        dy = dy_ref[...].astype(F32)
        dzu_ref[...] = (dy * mixed * _gelu_grad(zu)).astype(BF16)
        dmix = dy * u
        lane = lax.broadcasted_iota(jnp.int32, (D_CHUNK, LANES), 1)
        row = lax.broadcasted_iota(jnp.int32, (D_CHUNK, D_CHUNK), 0)
        colm = lax.broadcasted_iota(jnp.int32, (D_CHUNK, D_CHUNK), 1)
        dvln = []
        dbs = jnp.zeros((D_CHUNK, LANES), F32)
        for g in range(D_GROUPS):
            sl = slice(g * LANES, (g + 1) * LANES)
            dmg = dmix[:, sl]
            dbs = dbs + jnp.where(lane == g, jnp.sum(dmg, axis=-1, keepdims=True), 0.0)
            dmb = dmg.astype(BF16)
            dws_ref[g] += jnp.where(colm <= row, _dot(dmb, vln[:, sl], _NT), 0.0)
            dvln.append(_dot(_tril_bf16(ws_ref[g]), dmb, _TN))
        dbs_ref[...] += dbs
        dvln = jnp.concatenate(dvln, axis=1)
        dlg_ref[...] += jnp.sum(dvln * xh, axis=0, keepdims=True)
        dlb_ref[...] += jnp.sum(dvln, axis=0, keepdims=True)
        dxh = dvln * lg
        dvv = rstd * (dxh - jnp.mean(dxh, axis=-1, keepdims=True) - xh * jnp.mean(dxh * xh, axis=-1, keepdims=True))
        dzv_ref[...] = (dvv * _gelu_grad(zv)).astype(BF16)

    wsspec = pl.BlockSpec((D_GROUPS, D_CHUNK, D_CHUNK), lambda n: (0, 0, 0))
    chunk = lambda cidx: pl.BlockSpec((D_CHUNK, _DW), lambda n: (n, cidx))
    return pl.pallas_call(
        body, name="sgu_bwd",
        out_shape=(_sds((s, _DW), BF16), _sds((s, _DW), BF16), _sds((D_GROUPS, D_CHUNK, D_CHUNK)),
                   _sds((D_CHUNK, LANES)), _sds((1, _DW)), _sds((1, _DW))),
        grid=(nchunk,),
        in_specs=[chunk(1), chunk(2), _vec(_DW), _vec(_DW), wsspec, pl.BlockSpec((D_CHUNK, LANES), lambda n: (0, 0)),
                  chunk(dy_col)],
        out_specs=(chunk(0), chunk(0), wsspec, pl.BlockSpec((D_CHUNK, LANES), lambda n: (0, 0)), _vec(_DW), _vec(_DW)),
        compiler_params=_params(("arbitrary",)),
    )(z, z, lg, lb, ws, bs_t, dycat)


def ada_mod(c_all, ada_w, ada_b):
    nl, d, n = ada_w.shape
    nb = c_all.shape[0]
    tn = _tile(n, 512)

    def body(c_ref, w_ref, b_ref, o_ref):
        cv = c_ref[...]
        ca = (cv * _sigmoid(cv)).astype(BF16)
        o_ref[...] = _dot(ca, w_ref[...].astype(BF16)) + b_ref[...]

    return pl.pallas_call(
        body, name="ada_mod", out_shape=_sds((nl, nb, n)), grid=(nl, n // tn),
        in_specs=[pl.BlockSpec((nb, d), lambda l, j: (0, 0)), pl.BlockSpec((None, d, tn), lambda l, j: (l, 0, j)),
                  pl.BlockSpec((None, 1, tn), lambda l, j: (l, 0, j))],
        out_specs=pl.BlockSpec((None, nb, tn), lambda l, j: (l, 0, j)),
        compiler_params=_params(("parallel", "parallel")),
    )(c_all, ada_w, ada_b.reshape(nl, 1, n))


def ada_grad(c_all_t, dmod):
    d, nb = c_all_t.shape
    nl, _, n = dmod.shape
    tn = _tile(n, 512)
    tr = _tile(d, 256, 8)

    def body(c_ref, dm_ref, o_ref):
        cv = c_ref[...]
        ca = cv * _sigmoid(cv)
        dm = dm_ref[...]
        acc = ca[:, 0:1] * dm[0:1, :]
        for b in range(1, nb):
            acc = acc + ca[:, b:b + 1] * dm[b:b + 1, :]
        o_ref[...] = acc

    return pl.pallas_call(
        body, name="ada_grad", out_shape=_sds((nl, d, n)), grid=(nl, n // tn, d // tr),
        in_specs=[pl.BlockSpec((tr, nb), lambda l, j, r: (r, 0)), pl.BlockSpec((None, nb, tn), lambda l, j, r: (l, 0, j))],
        out_specs=pl.BlockSpec((None, tr, tn), lambda l, j, r: (l, r, j)),
        compiler_params=_params(("parallel", "parallel", "parallel")),
    )(c_all_t, dmod)


_ADAM_BLOCK = 256 * 1024


def _adam_rows(rows, cols):
    if rows * cols <= _ADAM_BLOCK or rows % 8:
        return rows
    return _tile(rows, max(8, _ADAM_BLOCK // cols), 8)


def _adam_update(w, gv, m, v):
    inv_bc1 = 1.0 / (1.0 - ADAM_B1 ** ADAM_STEP)
    inv_bc2 = 1.0 / (1.0 - ADAM_B2 ** ADAM_STEP)
    nm = ADAM_B1 * m + (1.0 - ADAM_B1) * gv
    nv = ADAM_B2 * v + (1.0 - ADAM_B2) * (gv * gv)
    return -ADAM_LR * ((nm * inv_bc1) / (jnp.sqrt(nv * inv_bc2) + ADAM_EPS) + ADAM_WD * w), nm, nv


def adamw(w, g, m, v, name):
    shape = w.shape
    cols = shape[-1]
    rows = w.size // cols
    tr = _adam_rows(rows, cols)

    def body(w_ref, g_ref, m_ref, v_ref, d_ref, nm_ref, nv_ref):
        d_ref[...], nm_ref[...], nv_ref[...] = _adam_update(w_ref[...], g_ref[...], m_ref[...], v_ref[...])

    spec = pl.BlockSpec((tr, cols), lambda i: (i, 0))
    out = _sds((rows, cols))
    r2 = lambda t: t.reshape(rows, cols)
    d, nm, nv = pl.pallas_call(
        body, name=name, out_shape=(out, out, out), grid=(rows // tr,),
        in_specs=[spec] * 4, out_specs=(spec,) * 3, compiler_params=_params(("parallel",)),
    )(r2(w), r2(g), r2(m), r2(v))
    return g.reshape(shape), d.reshape(shape), nm.reshape(shape), nv.reshape(shape)


def adamw_layers(w, g0, g1, m, v, name):
    _, rows, cols = w.shape
    tr = _adam_rows(rows, cols)

    def body(w_ref, g0_ref, g1_ref, m_ref, v_ref, g_ref, d_ref, nm_ref, nv_ref):
        gv = jnp.where(pl.program_id(0) == 0, g0_ref[...], g1_ref[...])
        g_ref[...] = gv
        d_ref[...], nm_ref[...], nv_ref[...] = _adam_update(w_ref[...], gv, m_ref[...], v_ref[...])

    spec = pl.BlockSpec((None, tr, cols), lambda l, i: (l, i, 0))
    gspec = pl.BlockSpec((tr, cols), lambda l, i: (i, 0))
    out = _sds((2, rows, cols))
    return pl.pallas_call(
        body, name=name, out_shape=(out, out, out, out), grid=(2, rows // tr),
        in_specs=[spec, gspec, gspec, spec, spec], out_specs=(spec,) * 4, compiler_params=_params(("parallel", "parallel")),
    )(w, g0, g1, m, v)


def sum8(gathered):
    _, r, _ = gathered.shape
    tr = _tile(r, 512, 8)

    def body(g_ref, o_ref):
        acc = g_ref[0]
        for dev in range(1, N_DEV):
            acc = acc + g_ref[dev]
        o_ref[...] = acc

    return pl.pallas_call(
        body, name="sum8", out_shape=_sds((r, LANES)), grid=(r // tr,),
        in_specs=[pl.BlockSpec((N_DEV, tr, LANES), lambda i: (0, i, 0))], out_specs=pl.BlockSpec((tr, LANES), lambda i: (i, 0)),
        compiler_params=_params(("parallel",)),
    )(gathered)


_SUM_BLOCK = 512 * 1024


def _sum_rows(rh, cols):
    return rh if rh * cols <= _SUM_BLOCK else _tile(rh, max(16, _SUM_BLOCK // cols), 16)


def pair_sum(g, recv, core, name):
    _, r, cols = g.shape
    rh = r // 2
    tr = _sum_rows(rh, cols)
    per = rh // tr

    def body(c_ref, a_ref, b_ref, o_ref):
        del c_ref
        o_ref[...] = (a_ref[...].astype(F32) + b_ref[...].astype(F32)).astype(BF16)

    grid_spec = pltpu.PrefetchScalarGridSpec(
        num_scalar_prefetch=1, grid=(N_CHIPS, per),
        in_specs=[pl.BlockSpec((None, tr, cols), lambda k, i, c: (k, c[0] * per + i, 0)),
                  pl.BlockSpec((None, tr, cols), lambda k, i, c: (k, i, 0))],
        out_specs=pl.BlockSpec((None, tr, cols), lambda k, i, c: (k, i, 0)))
    return pl.pallas_call(
        body, name=name, out_shape=_sds((N_CHIPS, rh, cols), BF16), grid_spec=grid_spec,
        compiler_params=_params(("parallel", "parallel")),
    )(core.reshape(1).astype(jnp.int32), g, recv)


def chip_sum(pair, recv, chip, core, name):
    _, rh, cols = pair.shape
    tr = _sum_rows(rh, cols)

    def body(p_ref, own_ref, r_ref, o_ref):
        del p_ref
        acc = own_ref[...].astype(F32)
        for j in range(N_CHIPS - 1):
            acc = acc + r_ref[j].astype(F32)
        o_ref[...] = acc

    grid_spec = pltpu.PrefetchScalarGridSpec(
        num_scalar_prefetch=1, grid=(rh // tr,),
        in_specs=[pl.BlockSpec((None, tr, cols), lambda i, p: (p[0], i, 0)),
                  pl.BlockSpec((N_CHIPS - 1, tr, cols), lambda i, p: (0, i, 0))],
        out_specs=pl.BlockSpec((None, tr, cols), lambda i, p: (p[1], i, 0)))
    return pl.pallas_call(
        body, name=name, out_shape=_sds((2, rh, cols)), grid_spec=grid_spec,
        compiler_params=_params(("parallel",)),
    )(jnp.stack([chip, core]).astype(jnp.int32), pair, recv)


def _place():
    return lax.axis_index("x"), lax.axis_index("y"), lax.axis_index("c")


def _other_chips(x, y):
    return [(x, 1 - y), (1 - x, y), (1 - x, 1 - y)]


_HBM = pl.BlockSpec(memory_space=pltpu.HBM)


def all_gather8(v, name):
    m, n = v.shape

    def body(x_ref, out_ref, send_sems, recv_sems, local_sem):
        x, y, c = _place()
        me, sibling = (x, y, c), (x, y, 1 - c)
        chips = _other_chips(x, y)

        def rows(px, py, pc):
            return out_ref.at[pl.ds((4 * px + 2 * py + pc) * m, m), :]

        def copy(k, block, to, src=None):
            return pltpu.make_async_remote_copy(
                src_ref=rows(*block) if src is None else src, dst_ref=rows(*block),
                send_sem=send_sems.at[k], recv_sem=recv_sems.at[k], device_id=to, device_id_type=MESH)

        mine = pltpu.make_async_copy(x_ref, rows(*me), local_sem)
        mine.start()
        first = [copy(0, me, sibling, src=x_ref)]
        first += [copy(1 + j, me, (*chip, c), src=x_ref) for j, chip in enumerate(chips)]
        for cp in first:
            cp.start()
        passed = [copy(4 + j, (*chip, c), sibling) for j, chip in enumerate(chips)]
        for j, chip in enumerate(chips):
            copy(1 + j, (*chip, c), me).wait_recv()
            passed[j].start()
        copy(0, sibling, me).wait_recv()
        for j, chip in enumerate(chips):
            copy(4 + j, (*chip, 1 - c), me).wait_recv()
        for cp in first + passed:
            cp.wait_send()
        mine.wait()

    return pl.pallas_call(
        body, name=name, out_shape=_sds((N_DEV * m, n), v.dtype),
        in_specs=[pl.BlockSpec(memory_space=pltpu.VMEM)], out_specs=pl.BlockSpec(memory_space=pltpu.VMEM),
        scratch_shapes=[pltpu.SemaphoreType.DMA((7,)), pltpu.SemaphoreType.DMA((7,)), pltpu.SemaphoreType.DMA],
        compiler_params=_params(None, VMEM_BIG),
    )(v)


def _comm_call(body, name, ins, out_shapes, nsem, aliases=None):
    return pl.pallas_call(
        body, name=name, out_shape=tuple(out_shapes), in_specs=[_HBM] * len(ins), out_specs=tuple([_HBM] * len(out_shapes)),
        scratch_shapes=[pltpu.SemaphoreType.DMA((nsem,)), pltpu.SemaphoreType.DMA((nsem,))],
        input_output_aliases=aliases or {},
    )(*ins)


def _remote(src, dst, send_sems, recv_sems, k, to):
    return pltpu.make_async_remote_copy(src_ref=src, dst_ref=dst, send_sem=send_sems.at[k], recv_sem=recv_sems.at[k],
                                        device_id=to, device_id_type=MESH)


def _half(core, rh):
    return pl.ds(pl.multiple_of(core * rh, 16), rh)


def swap_halves(gs, name):
    n = len(gs)

    def body(*refs):
        ins, outs, (send_sems, recv_sems) = refs[:n], refs[n:2 * n], refs[2 * n:]
        x, y, c = _place()
        copies = []
        for i in range(n):
            theirs = _half(1 - c, ins[i].shape[1] // 2)
            cp = _remote(ins[i].at[:, theirs], outs[i], send_sems, recv_sems, i, (x, y, 1 - c))
            cp.start()
            copies.append(cp)
        for cp in copies:
            cp.wait()

    return _comm_call(body, name, gs, [_sds((g.shape[0], g.shape[1] // 2, g.shape[2]), g.dtype) for g in gs], n)


def join_halves(bufs, name):
    n = len(bufs)

    def body(*refs):
        ins, outs, (send_sems, recv_sems) = refs[:n], refs[n:2 * n], refs[2 * n:]
        x, y, c = _place()
        copies = []
        for i in range(n):
            cp = _remote(ins[i].at[c], outs[i].at[c], send_sems, recv_sems, i, (x, y, 1 - c))
            cp.start()
            copies.append(cp)
        for i in range(n):
            theirs = outs[i].at[1 - c]
            _remote(theirs, theirs, send_sems, recv_sems, i, (x, y, 1 - c)).wait_recv()
        for cp in copies:
            cp.wait_send()

    return _comm_call(body, name, bufs, [_sds(b.shape, b.dtype) for b in bufs], n, {i: i for i in range(n)})


def forward_halves(lands, name):
    n = len(lands)

    def body(*refs):
        ins, outs, (send_sems, recv_sems) = refs[:n], refs[n:2 * n], refs[2 * n:]
        x, y, c = _place()
        sibling = (x, y, 1 - c)
        chips = _other_chips(x, y)
        copies = []
        for i in range(n):
            mine = _half(c, ins[i].shape[1] // 2)
            for j, (px, py) in enumerate(chips):
                cp = _remote(ins[i].at[2 * px + py, mine], outs[i].at[2 * px + py, mine], send_sems, recv_sems, 3 * i + j, sibling)
                cp.start()
                copies.append(cp)
        for i in range(n):
            theirs = _half(1 - c, ins[i].shape[1] // 2)
            for j, (px, py) in enumerate(chips):
                landed = outs[i].at[2 * px + py, theirs]
                _remote(landed, landed, send_sems, recv_sems, 3 * i + j, sibling).wait_recv()
        for cp in copies:
            cp.wait_send()

    return _comm_call(body, name, lands, [_sds(b.shape, b.dtype) for b in lands], 3 * n, {i: i for i in range(n)})


_SEM = pl.BlockSpec(memory_space=pltpu.SEMAPHORE)
_EFFECT = pltpu.SideEffectType.DATAFLOW_SIDE_EFFECTING


def _gather_copies(srcs, lands, send_sems, recv_sems):
    x, y, c = _place()
    copies = []
    for i in range(len(srcs)):
        mine = _half(c, srcs[i].shape[0] // 2)
        for j, chip in enumerate(_other_chips(x, y)):
            copies.append(_remote(srcs[i].at[mine], lands[i].at[2 * x + y, mine], send_sems, recv_sems, 3 * i + j, (*chip, c)))
    return copies


def _exchange_copies(srcs, lands, send_sems, recv_sems):
    x, y, c = _place()
    copies = []
    for i in range(len(srcs)):
        for j, (px, py) in enumerate(_other_chips(x, y)):
            copies.append(_remote(srcs[i].at[2 * px + py], lands[i].at[j], send_sems, recv_sems, 3 * i + j, (px, py, c)))
    return copies


def _swap_copies(srcs, lands, send_sems, recv_sems):
    x, y, c = _place()
    return [_remote(srcs[i].at[:, _half(1 - c, srcs[i].shape[1] // 2)], lands[i], send_sems, recv_sems, i, (x, y, 1 - c))
            for i in range(len(srcs))]


def _join_copies(srcs, lands, send_sems, recv_sems):
    del lands
    x, y, c = _place()
    return [_remote(srcs[i].at[c], srcs[i].at[c], send_sems, recv_sems, i, (x, y, 1 - c)) for i in range(len(srcs))]


def _everyone_copies(srcs, lands, send_sems, recv_sems):
    x, y, c = _place()
    flip = lambda v, b: 1 - v if b else v
    dst = lands[0].at[4 * x + 2 * y + c]
    return [_remote(srcs[0], dst, send_sems, recv_sems, j - 1, (flip(x, j & 4), flip(y, j & 2), flip(c, j & 1)))
            for j in range(1, N_DEV)]


GATHER = (_gather_copies, 3)
EXCHANGE = (_exchange_copies, 3)
SWAP = (_swap_copies, 1)
JOIN = (_join_copies, 1)
EVERYONE = (_everyone_copies, N_DEV - 1)


def split_start(name, plan, srcs, land_shapes):
    copies_fn, per_source = plan
    n, m = len(srcs), len(land_shapes)
    ncopies = per_source * n

    def body(*refs):
        src_refs, land_refs = refs[:n], refs[n:n + m]
        send_sems, recv_sems = refs[n + m], refs[n + m + 1]
        token = refs[-1]
        for cp in copies_fn(src_refs, land_refs, send_sems, recv_sems):
            cp.start()
        token[...] = jnp.zeros_like(token)

    hbm = lambda s: pltpu.HBM(tuple(s.shape), s.dtype)
    outs = pl.pallas_call(
        body, name=name,
        out_shape=(pltpu.SemaphoreType.DMA((ncopies,)), pltpu.SemaphoreType.DMA((ncopies,)), *[hbm(s) for s in srcs],
                   *[hbm(s) for s in land_shapes], _sds((8, LANES))),
        in_specs=[_HBM] * (n + m),
        out_specs=(_SEM, _SEM, *([_HBM] * (n + m)), pl.BlockSpec(memory_space=pltpu.VMEM)),
        input_output_aliases={i: 2 + i for i in range(n + m)},
        compiler_params=pltpu.CompilerParams(has_side_effects=_EFFECT),
    )(*[pltpu.with_memory_space_constraint(s, pltpu.HBM) for s in srcs],
      *[pltpu.with_memory_space_constraint(lax.empty(tuple(s.shape), s.dtype), pltpu.HBM) for s in land_shapes])
    handle = (outs[0], outs[1], list(outs[2:2 + n]), list(outs[2 + n:2 + n + m]))
    return handle, outs[-1][0, 0]


def split_wait(name, plan, handle, after):
    copies_fn, _ = plan
    send_sems, recv_sems, srcs, lands = handle
    n, m = len(srcs), len(lands)
    after = list(after) if isinstance(after, (list, tuple)) else [after]

    def body(*refs):
        src_refs, land_refs = refs[:n], refs[n:n + m]
        for cp in copies_fn(src_refs, land_refs, refs[n + m], refs[n + m + 1]):
            cp.wait_send()
            cp.wait_recv()

    hbm = lambda s: pltpu.HBM(tuple(s.shape), s.dtype)
    outs = pl.pallas_call(
        body, name=name, out_shape=tuple(hbm(s) for s in srcs + lands),
        in_specs=[_HBM] * (n + m) + [_SEM, _SEM] + [pl.BlockSpec(memory_space=pl.ANY)] * len(after),
        out_specs=tuple([_HBM] * (n + m)), input_output_aliases={i: i for i in range(n + m)},
        compiler_params=pltpu.CompilerParams(has_side_effects=_EFFECT),
    )(*srcs, *lands, send_sems, recv_sems, *after)
    return list(outs[:n]), list(outs[n:])


_CD_PAD = C_Q_RANK + C_KV_RANK + HEAD_PAD + 2 * _DW


def chip_major(w, groups=N_CHIPS):
    r, c = w.shape
    return w.reshape(r, groups, c // groups).transpose(1, 0, 2)


def from_chip_major(w):
    g, r, c = w.shape
    return w.transpose(1, 0, 2).reshape(r, g * c)


def _cd_in_pad(w):
    a = C_Q_RANK + C_KV_RANK
    z = lambda n: jnp.zeros((w.shape[0], n), w.dtype)
    return jnp.concatenate([w[:, :a], z(C_NOPE), w[:, a:a + C_ROPE], z(HEAD_PAD - C_NOPE - C_ROPE), w[:, a + C_ROPE:]], axis=1)


def _cd_in_unpad(w):
    a = C_Q_RANK + C_KV_RANK
    return jnp.concatenate([w[:, :a], w[:, a + C_NOPE:a + C_NOPE + C_ROPE], w[:, a + HEAD_PAD:]], axis=1)


def _pad_heads(w, width):
    r = w.shape[0]
    w = w.reshape(r, C_HEADS, width)
    return jnp.pad(w, ((0, 0), (0, 0), (0, HEAD_PAD - width))).reshape(r, _HW)


def _unpad_heads(w, width):
    r = w.shape[0]
    return w.reshape(r, C_HEADS, HEAD_PAD)[:, :, :width].reshape(r, C_HEADS * width)


_MATMUL_WEIGHTS = ("ab_w_in", "ab_w_out", "cd_w_in", "c_w_uq", "c_w_ukv", "cd_w_out", "ffn_w_up", "ffn_w_down")
_LAYER_STACKED = ("norm1_g", "norm2_g", "ffn_w_up", "ffn_conv_w", "ffn_w_down")
_ROW_VECTORS = ("b_scale", "c_q_norm_g", "c_kv_norm_g", "d_ln_g", "d_ln_b")


def full_to_local(p):
    q = {}
    for k, v in p.items():
        if k == "final_norm_g":
            v = v.reshape(1, -1)
        elif k not in _LAYER_STACKED and k not in _ROW_VECTORS:
            v = v[0]
        q[k] = v.astype(BF16) if k in _MATMUL_WEIGHTS else v
    return q


def local_to_full(g):
    q = {}
    for k, v in g.items():
        if k == "final_norm_g":
            q[k] = v.reshape(-1)
        elif k not in _LAYER_STACKED and k not in _ROW_VECTORS:
            q[k] = v[None]
        else:
            q[k] = v
    return q


def prepare_weights(p):
    q = dict(p)
    q["cd_w_in"] = _cd_in_pad(p["cd_w_in"])
    q["c_w_uq"] = _pad_heads(p["c_w_uq"], C_NOPE + C_ROPE)
    ukv = p["c_w_ukv"].reshape(C_KV_RANK, C_HEADS, C_NOPE + C_V)
    q["c_w_uk"] = _pad_heads(ukv[:, :, :C_NOPE].reshape(C_KV_RANK, -1), C_NOPE)
    q["c_w_uv"] = _pad_heads(ukv[:, :, C_NOPE:].reshape(C_KV_RANK, -1), C_V)
    wo = p["cd_w_out"]
    att_rows = jnp.pad(wo[:C_HEADS * C_V].reshape(C_HEADS, C_V, D_MODEL), ((0, 0), (0, HEAD_PAD - C_V), (0, 0)))
    q["cd_w_out"] = jnp.concatenate([att_rows.reshape(_HW, D_MODEL), wo[C_HEADS * C_V:]], axis=0)
    return q


def unprepare_grads(g):
    q = dict(g)
    q["cd_w_in"] = _cd_in_unpad(g["cd_w_in"])
    q["c_w_uq"] = _unpad_heads(g["c_w_uq"], C_NOPE + C_ROPE)
    uk = g.pop("c_w_uk").reshape(C_KV_RANK, C_HEADS, HEAD_PAD)[:, :, :C_NOPE]
    uv = g.pop("c_w_uv").reshape(C_KV_RANK, C_HEADS, HEAD_PAD)[:, :, :C_V]
    q.pop("c_w_uk", None)
    q.pop("c_w_uv", None)
    q["c_w_ukv"] = jnp.concatenate([uk, uv], axis=-1).reshape(C_KV_RANK, C_HEADS * (C_NOPE + C_V))
    wo = g["cd_w_out"]
    att = wo[:_HW].reshape(C_HEADS, HEAD_PAD, D_MODEL)[:, :C_V].reshape(C_HEADS * C_V, D_MODEL)
    q["cd_w_out"] = jnp.concatenate([att, wo[_HW:]], axis=0)
    return q


def rope_tables(positions):
    half = C_ROPE // 2
    inv_freq = ROPE_THETA ** (-jnp.arange(half, dtype=F32) / half)
    ang = positions.astype(F32)[:, None] * inv_freq
    cos, sin = jnp.cos(ang), jnp.sin(ang)
    s = positions.shape[0]
    z = lambda n: jnp.zeros((s, n), F32)
    cs = jnp.concatenate([jnp.ones((s, C_NOPE), F32), cos, cos, z(HEAD_PAD - C_NOPE - C_ROPE)], axis=1)
    s1 = jnp.concatenate([z(C_NOPE), -sin, z(HEAD_PAD - C_NOPE - half)], axis=1)
    s2 = jnp.concatenate([z(C_NOPE + half), sin, z(HEAD_PAD - C_NOPE - C_ROPE)], axis=1)
    return cs, s1, s2


def _mods(mod_l):
    return [mod_l[:, i * D_MODEL:(i + 1) * D_MODEL] for i in range(N_MOD)]


def _ffn_fwd(x1, w, l, sc2, sh2, g2, late_down=None):
    n2 = w["norm2_g"][l:l + 1]
    h2 = modnorm_fwd(x1, n2, sc2, sh2, f"modnorm2_fwd{l}")
    up_cols = 2 * D_FF // N_CHIPS
    zf = matmul(h2, w["ffn_w_up"][l], "nn", BF16, f"ffn_up{l}", gb=N_CHIPS, go=2, tn=up_cols)
    a = ffn_act_fwd(zf, w["ffn_conv_w"][l], f"ffn_act_fwd{l}")
    if late_down is not None:
        late_down(a)
    f = matmul(a, w["ffn_w_down"][l], "nn", F32, f"ffn_down{l}", tk=D_FF)
    x2 = resid_fwd(x1, f, g2, f"resid2_fwd{l}")
    return x2, (h2, zf, a, f)


def _ffn_bwd(dres, x1, saved, w, l, sc2, g2):
    h2, zf, a, f = saved
    n2 = w["norm2_g"][l:l + 1]
    df, dg2 = gate_bwd(dres, f, g2, f"gate2_bwd{l}")
    up_cols = 2 * D_FF // N_CHIPS
    da = matmul(df, w["ffn_w_down"][l], "nt", BF16, f"ffn_down_dx{l}")
    d_down = matmul(a, df, "tn", BF16, f"ffn_down_dw{l}", tm=D_FF // 2)
    dzf, d_conv = ffn_act_bwd(zf, w["ffn_conv_w"][l], da, f"ffn_act_bwd{l}")
    dh2 = matmul(dzf, w["ffn_w_up"][l], "nt", F32, f"ffn_up_dx{l}", ga=2, gb=N_CHIPS, tk=up_cols)
    d_up = matmul(h2, dzf, "tn", BF16, f"ffn_up_dw{l}", gb=2, go=N_CHIPS, tn=up_cols)
    dres, dsh2, dsc2, dn2 = norm_bwd(x1, dh2, n2, sc2, dres, f"norm2_bwd{l}")
    d_conv = d_conv.transpose(1, 0, 2).reshape(3, 2 * D_FF)
    return dres, dict(ffn_w_down=d_down, ffn_conv_w=d_conv, ffn_w_up=d_up, norm2_g=dn2), (dsh2, dsc2, dg2)


def mixer0_fwd(x0, sh1, sc1, g1, w):
    h = modnorm_fwd(x0, w["norm1_g"][0:1], sc1, sh1, "modnorm1_fwd0")
    z = matmul(h, w["ab_w_in"], "nn", BF16, "ab_in", gb=N_CHIPS)
    ya = gconv_fwd(z, w["a_conv_w"])
    yb = pool_fwd(z, w["b_mix_w"], w["b_scale"])
    ycat = jnp.concatenate([ya] + yb, axis=1)
    y = matmul(ycat, w["ab_w_out"], "nn", F32, "ab_out")
    x1 = resid_fwd(x0, y, g1, "resid1_fwd0")
    return x1, (x0, h, z, ycat, y, sc1, g1)


def mixer0_bwd(dres, saved, w):
    x0, h, z, ycat, y, sc1, g1 = saved
    grads = {}
    dy, dg1 = gate_bwd(dres, y, g1, "gate1_bwd0")
    dycat = matmul(dy, w["ab_w_out"], "nt", BF16, "ab_out_dx")
    grads["ab_w_out"] = matmul(ycat, dy, "tn", BF16, "ab_out_dw")
    db, dc, da, d_conv = gconv_bwd(z, w["a_conv_w"], dycat)
    pb = pool_bwd(z, w["b_mix_w"], w["b_scale"], dycat)
    dz = jnp.concatenate([db, dc, da] + [t[0] for t in pb], axis=1)
    dh = matmul(dz, w["ab_w_in"], "nt", F32, "ab_in_dx", gb=N_CHIPS)
    grads["ab_w_in"] = matmul(h, dz, "tn", BF16, "ab_in_dw", go=N_CHIPS)
    dres, dsh1, dsc1, dn1 = norm_bwd(x0, dh, w["norm1_g"][0:1], sc1, dres, "norm1_bwd0")
    grads.update(a_conv_w=d_conv, b_mix_w=jnp.stack([t[1] for t in pb]),
                 b_scale=jnp.concatenate([t[2] for t in pb], axis=1), norm1_g=dn1)
    return dres, grads, (dsh1, dsc1, dg1)


def mixer1_fwd(x0, sh1, sc1, g1, ropes, w):
    cs, s1, s2 = ropes
    h = modnorm_fwd(x0, w["norm1_g"][1:2], sc1, sh1, "modnorm1_fwd1")
    z = matmul(h, w["cd_w_in"], "nn", BF16, "cd_in")
    bs_t = jnp.pad(w["d_b_s"].T, ((0, 0), (0, LANES - D_GROUPS)))
    qh, kh, vh = mla_pre_fwd(z, w["c_q_norm_g"], w["c_kv_norm_g"], w["c_w_uq"], w["c_w_uk"], w["c_w_uv"], cs, s1, s2)
    oh = attn_fwd(qh, kh, vh)
    yd = sgu_fwd(z, w["d_ln_g"], w["d_ln_b"], w["d_w_s"], bs_t)
    ycat = jnp.concatenate([oh, yd], axis=1)
    y = matmul(ycat, w["cd_w_out"], "nn", F32, "cd_out")
    x1 = resid_fwd(x0, y, g1, "resid1_fwd1")
    return x1, (x0, h, z, bs_t, qh, kh, vh, oh, ycat, y, sc1, g1)


def mixer1_bwd(dres, saved, ropes, w):
    cs, s1, s2 = ropes
    x0, h, z, bs_t, qh, kh, vh, oh, ycat, y, sc1, g1 = saved
    grads = {}
    dy, dg1 = gate_bwd(dres, y, g1, "gate1_bwd1")
    dycat = matmul(dy, w["cd_w_out"], "nt", BF16, "cd_out_dx")
    grads["cd_w_out"] = matmul(ycat, dy, "tn", F32, "cd_out_dw")
    dqh, dkh, dvh = attn_bwd(qh, kh, vh, oh, dycat, 0)
    dzq, d_uq, d_uk, d_uv, d_gq, d_gkv = mla_pre_bwd(
        z, w["c_q_norm_g"], w["c_kv_norm_g"], w["c_w_uq"], w["c_w_uk"], w["c_w_uv"], cs, s1, s2, dqh, dkh, dvh)
    dzu, dzv, d_ws, d_bs, d_lg, d_lb = sgu_bwd(z, w["d_ln_g"], w["d_ln_b"], w["d_w_s"], bs_t, dycat, _HW // _DW)
    dz = jnp.concatenate([dzq, dzu, dzv], axis=1)
    dh = matmul(dz, w["cd_w_in"], "nt", F32, "cd_in_dx")
    grads["cd_w_in"] = matmul(h, dz, "tn", F32, "cd_in_dw")
    dres, dsh1, dsc1, dn1 = norm_bwd(x0, dh, w["norm1_g"][1:2], sc1, dres, "norm1_bwd1")
    grads.update(c_w_uq=d_uq, c_w_uk=d_uk, c_w_uv=d_uv, c_q_norm_g=d_gq, c_kv_norm_g=d_gkv, d_w_s=d_ws,
                 d_b_s=d_bs[:, :D_GROUPS].T, d_ln_g=d_lg, d_ln_b=d_lb, norm1_g=dn1)
    return dres, grads, (dsh1, dsc1, dg1)


_PER_LAYER = ("ffn_w_down", "ffn_conv_w", "ffn_w_up", "norm2_g", "norm1_g")


def _merge_layer_grads(g0, g1):
    grads = {k: v for k, v in g0.items() if k not in _PER_LAYER}
    grads.update({k: v for k, v in g1.items() if k not in _PER_LAYER})
    for k in ("ffn_w_down", "ffn_w_up"):
        grads[k] = [g0[k], g1[k]]
    grads["ffn_conv_w"] = jnp.stack([g0["ffn_conv_w"], g1["ffn_conv_w"]])
    grads["norm1_g"] = jnp.concatenate([g0["norm1_g"], g1["norm1_g"]], axis=0)
    grads["norm2_g"] = jnp.concatenate([g0["norm2_g"], g1["norm2_g"]], axis=0)
    return grads


def local_step(x, tgt, mod, ropes, w):
    sh1a, sc1a, g1a, sh2a, sc2a, g2a = _mods(mod[0:1])
    sh1b, sc1b, g1b, sh2b, sc2b, g2b = _mods(mod[1:2])
    x1, mix0 = mixer0_fwd(x, sh1a, sc1a, g1a, w)
    x2, ffn0 = _ffn_fwd(x1, w, 0, sc2a, sh2a, g2a)
    x3, mix1 = mixer1_fwd(x2, sh1b, sc1b, g1b, ropes, w)
    x4, ffn1 = _ffn_fwd(x3, w, 1, sc2b, sh2b, g2b)
    dres, d_final, loss = final_fwd_bwd(x4, w["final_norm_g"], tgt)
    dres, gf1, dm2b = _ffn_bwd(dres, x3, ffn1, w, 1, sc2b, g2b)
    dres, gm1, dm1b = mixer1_bwd(dres, mix1, ropes, w)
    dres, gf0, dm2a = _ffn_bwd(dres, x1, ffn0, w, 0, sc2a, g2a)
    dres, gm0, dm1a = mixer0_bwd(dres, mix0, w)
    grads = _merge_layer_grads({**gf0, **gm0}, {**gf1, **gm1})
    grads["final_norm_g"] = d_final
    dmod = jnp.concatenate([jnp.concatenate(dm1a + dm2a, axis=1), jnp.concatenate(dm1b + dm2b, axis=1)], axis=0)
    return loss, dres, dmod, grads


_WEIGHTS = ("ada_w", "ada_b", "norm1_g", "norm2_g", "ab_w_in", "a_conv_w", "b_mix_w", "b_scale", "ab_w_out", "cd_w_in",
            "c_q_norm_g", "c_w_uq", "c_kv_norm_g", "c_w_ukv", "d_ln_g", "d_ln_b", "d_w_s", "d_b_s", "cd_w_out",
            "ffn_w_up", "ffn_conv_w", "ffn_w_down", "final_norm_g")
_INPUTS = ("x", "c", "positions") + _WEIGHTS + ("loss_target",) + tuple("m_" + n for n in _WEIGHTS) + tuple(
    "v_" + n for n in _WEIGHTS)

def _pack_rows(parts, rows, dtype):
    flat = jnp.concatenate([p.reshape(-1).astype(dtype) for p in parts])
    return jnp.pad(flat, (0, rows * LANES - flat.shape[0])).reshape(rows, LANES)


def _rows_major(w):
    r, c = w.shape
    return w.reshape(N_CHIPS, r // N_CHIPS, c)


def order_after(vs, name):
    def body(*refs):
        refs[-1][...] = jnp.zeros_like(refs[-1])

    return pl.pallas_call(
        body, name=name, out_shape=_sds((8, LANES)), in_specs=[pl.BlockSpec(memory_space=pl.ANY)] * len(vs),
        out_specs=pl.BlockSpec(memory_space=pltpu.VMEM),
    )(*vs)[0, 0]


def start_gather(shards, tag):
    lands = [_sds((N_CHIPS,) + s.shape, s.dtype) for s in shards]
    return split_start("gather_start_" + tag, GATHER, shards, lands)


def finish_gather(handle, chip, tag, after):
    shards, lands = split_wait("gather_wait_" + tag, GATHER, handle, after)
    lands = forward_halves(lands, "gather_forward_" + tag)
    return [lax.dynamic_update_index_in_dim(o, s, chip, 0) for o, s in zip(lands, shards)]


def start_reduce(gs, core, tag):
    recv = swap_halves(gs, "swap_halves_" + tag)
    pairs = [pair_sum(g, r, core, f"pair_sum_{tag}{i}") for i, (g, r) in enumerate(zip(gs, recv))]
    lands = [_sds((N_CHIPS - 1,) + p.shape[1:], p.dtype) for p in pairs]
    return split_start("exchange_start_" + tag, EXCHANGE, pairs, lands)


def finish_reduce(handle, chip, core, tag, after):
    pairs, others = split_wait("exchange_wait_" + tag, EXCHANGE, handle, after)
    halves = [chip_sum(p, o, chip, core, f"chip_sum_{tag}{i}") for i, (p, o) in enumerate(zip(pairs, others))]
    full = join_halves(halves, "join_halves_" + tag)
    return [f.reshape(f.shape[1] * 2, f.shape[2]) for f in full]


_SMALL_SHARDED = (("a_conv_w", (3, 128), 1), ("c_q_norm_g", (1, 64), 1), ("d_ln_g", (1, 128), 1), ("d_ln_b", (1, 128), 1),
                  ("ffn_conv_w", (2, 3, 2 * D_FF // N_CHIPS), 2))
_SMALL_GRADS = (("norm1_g", (2, D_MODEL)), ("norm2_g", (2, D_MODEL)), ("b_mix_w", (4, 128, 128)), ("b_scale", (1, 512)),
                ("c_kv_norm_g", (1, 128)), ("d_w_s", (4, 128, 128)), ("d_b_s", (4, 128)), ("final_norm_g", (1, D_MODEL)),
                ("a_conv_w", (3, 512)), ("c_q_norm_g", (1, 256)), ("d_ln_g", (1, 512)), ("d_ln_b", (1, 512)),
                ("ffn_conv_w", (2, 3, 2 * D_FF)))


def _size(shape):
    n = 1
    for d in shape:
        n *= d
    return n


def kernel(x, c, positions, ada_w, ada_b, norm1_g, norm2_g, ab_w_in, a_conv_w, b_mix_w, b_scale, ab_w_out, cd_w_in, c_q_norm_g, c_w_uq, c_kv_norm_g, c_w_ukv, d_ln_g, d_ln_b, d_w_s, d_b_s, cd_w_out, ffn_w_up, ffn_conv_w, ffn_w_down, final_norm_g, loss_target, m_ada_w, m_ada_b, m_norm1_g, m_norm2_g, m_ab_w_in, m_a_conv_w, m_b_mix_w, m_b_scale, m_ab_w_out, m_cd_w_in, m_c_q_norm_g, m_c_w_uq, m_c_kv_norm_g, m_c_w_ukv, m_d_ln_g, m_d_ln_b, m_d_w_s, m_d_b_s, m_cd_w_out, m_ffn_w_up, m_ffn_conv_w, m_ffn_w_down, m_final_norm_g, v_ada_w, v_ada_b, v_norm1_g, v_norm2_g, v_ab_w_in, v_a_conv_w, v_b_mix_w, v_b_scale, v_ab_w_out, v_cd_w_in, v_c_q_norm_g, v_c_w_uq, v_c_kv_norm_g, v_c_w_ukv, v_d_ln_g, v_d_ln_b, v_d_w_s, v_d_b_s, v_cd_w_out, v_ffn_w_up, v_ffn_conv_w, v_ffn_w_down, v_final_norm_g):
    args = (x, c, positions, ada_w, ada_b, norm1_g, norm2_g, ab_w_in, a_conv_w, b_mix_w, b_scale, ab_w_out, cd_w_in, c_q_norm_g, c_w_uq, c_kv_norm_g, c_w_ukv, d_ln_g, d_ln_b, d_w_s, d_b_s, cd_w_out, ffn_w_up, ffn_conv_w, ffn_w_down, final_norm_g, loss_target, m_ada_w, m_ada_b, m_norm1_g, m_norm2_g, m_ab_w_in, m_a_conv_w, m_b_mix_w, m_b_scale, m_ab_w_out, m_cd_w_in, m_c_q_norm_g, m_c_w_uq, m_c_kv_norm_g, m_c_w_ukv, m_d_ln_g, m_d_ln_b, m_d_w_s, m_d_b_s, m_cd_w_out, m_ffn_w_up, m_ffn_conv_w, m_ffn_w_down, m_final_norm_g, v_ada_w, v_ada_b, v_norm1_g, v_norm2_g, v_ab_w_in, v_a_conv_w, v_b_mix_w, v_b_scale, v_ab_w_out, v_cd_w_in, v_c_q_norm_g, v_c_w_uq, v_c_kv_norm_g, v_c_w_ukv, v_d_ln_g, v_d_ln_b, v_d_w_s, v_d_b_s, v_cd_w_out, v_ffn_w_up, v_ffn_conv_w, v_ffn_w_down, v_final_norm_g)
    a = dict(zip(_INPUTS, args, strict=True))
    xi, yi, ci = _place()
    chip = 2 * xi + yi
    dev = 4 * xi + 2 * yi + ci
    x = a["x"][0]
    tgt = a["loss_target"][0]

    bf = lambda t: t.astype(BF16)
    mix0_handle, tok = start_gather([bf(a["ab_w_in"][0]), bf(a["ab_w_out"][0])], "mix0")

    small_parts = [a["c"] + tok] + [a[n] for n, _, _ in _SMALL_SHARDED]
    rows1 = -(-sum(p.size for p in small_parts) // LANES // 8) * 8
    g1 = all_gather8(_pack_rows(small_parts, rows1, F32), "gather_small").reshape(N_DEV, rows1 * LANES)
    c_all = g1[:, :D_MODEL]
    per_chip = g1[0::2]
    small_full = {}
    off = D_MODEL
    for n, shp, axis in _SMALL_SHARDED:
        piece = per_chip[:, off:off + _size(shp)].reshape((N_CHIPS,) + shp)
        small_full[n] = jnp.concatenate([piece[k] for k in range(N_CHIPS)], axis=axis)
        off += _size(shp)

    merge = lambda t: t.reshape(t.shape[0] * t.shape[1], t.shape[2])
    w = dict(norm1_g=a["norm1_g"], norm2_g=a["norm2_g"], b_mix_w=a["b_mix_w"][0], b_scale=a["b_scale"],
             c_kv_norm_g=a["c_kv_norm_g"], d_w_s=a["d_w_s"][0], d_b_s=a["d_b_s"][0],
             final_norm_g=a["final_norm_g"].reshape(1, D_MODEL), **small_full)

    ncol = N_MOD * D_MODEL // N_CHIPS
    ada_b_mine = lax.dynamic_slice_in_dim(a["ada_b"], chip * ncol, ncol, axis=1)
    mod_cols = ada_mod(c_all, a["ada_w"], ada_b_mine)
    g2 = all_gather8(mod_cols.reshape(-1, LANES), "gather_mod").reshape(N_DEV, 2, N_DEV, ncol)
    mod = lax.dynamic_index_in_dim(g2[0::2], dev, axis=2, keepdims=False)
    mod = mod.transpose(1, 0, 2).reshape(2, N_MOD * D_MODEL)

    late = order_after([mod], "after_mod")
    bf_late = lambda t: (t + late).astype(BF16)
    up0_handle, tok_a = start_gather([bf_late(a["ffn_w_up"][0])], "up0")
    down0_handle, tok_b = start_gather([bf_late(a["ffn_w_down"][0])], "down0")
    mix1_handle, tok_c = start_gather(
        [bf_late(a["cd_w_in"][0]), bf_late(a["c_w_uq"][0]), bf_late(a["c_w_ukv"][0]), bf_late(a["cd_w_out"][0])], "mix1")
    ffn1_handle, tok_d = start_gather([bf_late(a["ffn_w_up"][1]), bf_late(a["ffn_w_down"][1])], "ffn1")
    mod = mod + (tok_a + tok_b + tok_c + tok_d)

    ropes = rope_tables(a["positions"][0])
    cm16 = lambda t: chip_major(t).astype(BF16)
    sh1a, sc1a, g1a, sh2a, sc2a, g2a = _mods(mod[0:1])
    sh1b, sc1b, g1b, sh2b, sc2b, g2b = _mods(mod[1:2])

    w_in0, w_out0 = finish_gather(mix0_handle, chip, "mix0", mod)
    w.update(ab_w_in=w_in0, ab_w_out=merge(w_out0))
    x1, mix0 = mixer0_fwd(x, sh1a, sc1a, g1a, w)
    up0, = finish_gather(up0_handle, chip, "up0", x1)
    w.update(ffn_w_up=[up0, None], ffn_w_down=[None, None])

    def fetch_down0(act):
        down0, = finish_gather(down0_handle, chip, "down0", act)
        w["ffn_w_down"][0] = merge(down0)

    x2, ffn0 = _ffn_fwd(x1, w, 0, sc2a, sh2a, g2a, late_down=fetch_down0)
    cd_in, uq, ukv, cd_out = finish_gather(mix1_handle, chip, "mix1", x2)
    w.update(prepare_weights(dict(cd_w_in=from_chip_major(cd_in), c_w_uq=from_chip_major(uq), c_w_ukv=from_chip_major(ukv),
                                  cd_w_out=merge(cd_out))))
    x3, mix1 = mixer1_fwd(x2, sh1b, sc1b, g1b, ropes, w)
    up1, down1 = finish_gather(ffn1_handle, chip, "ffn1", x3)
    w.update(ffn_w_up=[up0, up1], ffn_w_down=[w["ffn_w_down"][0], merge(down1)])
    x4, ffn1 = _ffn_fwd(x3, w, 1, sc2b, sh2b, g2b)
    dres, d_final, loss = final_fwd_bwd(x4, w["final_norm_g"], tgt)

    dres, gf1, dm2b = _ffn_bwd(dres, x3, ffn1, w, 1, sc2b, g2b)
    ffn1_red, tok = start_reduce([gf1["ffn_w_up"], _rows_major(gf1["ffn_w_down"])], ci, "ffn1")
    dres, gm1, dm1b = mixer1_bwd(dres, mix1[:-1] + (mix1[-1] + tok,), ropes, w)
    gm1 = unprepare_grads(gm1)
    mix1_red, tok = start_reduce([cm16(gm1["cd_w_in"]), cm16(gm1["c_w_uq"]), cm16(gm1["c_w_ukv"]),
                                  _rows_major(gm1["cd_w_out"]).astype(BF16)], ci, "mix1")
    red_up1, red_down1 = finish_reduce(ffn1_red, chip, ci, "ffn1", dres)
    dres, gf0, dm2a = _ffn_bwd(dres, x1, ffn0, w, 0, sc2a, g2a + tok)
    ffn0_red, tok = start_reduce([gf0["ffn_w_up"], _rows_major(gf0["ffn_w_down"])], ci, "ffn0")
    red_cd_in, red_uq, red_ukv, red_cd_out = finish_reduce(mix1_red, chip, ci, "mix1", dres)
    grad_x, gm0, dm1a = mixer0_bwd(dres, mix0[:-1] + (mix0[-1] + tok,), w)
    grads = _merge_layer_grads({**gf0, **gm0}, {**gf1, **gm1})
    grads["final_norm_g"] = d_final
    dmod = jnp.concatenate([jnp.concatenate(dm1a + dm2a, axis=1), jnp.concatenate(dm1b + dm2b, axis=1)], axis=0)

    parts3 = [dmod] + [grads[n] for n, _ in _SMALL_GRADS] + [loss[0, 0]]
    rows3 = -(-sum(p.size for p in parts3) // LANES // 8) * 8
    small_handle, tok = split_start("small_grads_start", EVERYONE, [_pack_rows(parts3, rows3, F32)],
                                    [_sds((N_DEV, rows3, LANES))])
    mix0_red, _ = start_reduce([gm0["ab_w_in"], _rows_major(gm0["ab_w_out"]) + tok.astype(BF16)], ci, "mix0")
    red_up0, red_down0 = finish_reduce(ffn0_red, chip, ci, "ffn0", grad_x)
    out_grads = dict(cd_w_in=red_cd_in, c_w_uq=red_uq, c_w_ukv=red_ukv, cd_w_out=red_cd_out)
    per_layer = dict(ffn_w_up=(red_up0, red_up1), ffn_w_down=(red_down0, red_down1))
    updates = {}

    def update(n):
        if n in per_layer:
            updates[n] = adamw_layers(a[n], *per_layer[n], a["m_" + n], a["v_" + n], "adamw_" + n)
        else:
            updates[n] = adamw(a[n], out_grads[n].reshape(a[n].shape), a["m_" + n], a["v_" + n], "adamw_" + n)

    early = ("ffn_w_up", "ffn_w_down", "cd_w_in", "c_w_uq", "c_w_ukv", "cd_w_out")
    for n in early:
        update(n)
    (mine,), (landed,) = split_wait("small_grads_wait", EVERYONE, small_handle, [updates[n][1] for n in early])
    g3 = lax.dynamic_update_index_in_dim(landed, mine, dev, 0)
    summed = sum8(g3).reshape(-1)
    nmod = 2 * N_MOD * D_MODEL
    out_grads["ada_b"] = summed[:nmod].reshape(2, N_MOD * D_MODEL)
    off = nmod
    for n, shp in _SMALL_GRADS:
        out_grads[n] = summed[off:off + _size(shp)].reshape(shp)
        off += _size(shp)
    loss = summed[off]
    for n, shp, axis in _SMALL_SHARDED:
        width = out_grads[n].shape[-1] // N_CHIPS
        out_grads[n] = lax.dynamic_slice_in_dim(out_grads[n], chip * width, width, axis=out_grads[n].ndim - 1)
    dmod_all = g3.reshape(N_DEV, rows3 * LANES)[:, :nmod].reshape(N_DEV, 2, N_MOD * D_MODEL)
    dmod_mine = lax.dynamic_slice_in_dim(dmod_all, chip * ncol, ncol, axis=2).transpose(1, 0, 2)
    out_grads["ada_w"] = ada_grad(c_all.T, dmod_mine)

    red_in0, red_out0 = finish_reduce(mix0_red, chip, ci, "mix0", out_grads["ada_w"])
    out_grads.update(ab_w_in=red_in0, ab_w_out=red_out0)

    for n in _WEIGHTS:
        if n not in updates:
            update(n)
    return (loss, grad_x[None], *[updates[n][i] for i in range(4) for n in _WEIGHTS])
```

```python
import functools

import jax
import jax.numpy as jnp
from jax import lax
from jax.experimental import pallas as pl
from jax.experimental.pallas import tpu as pltpu

F32 = jnp.float32
BF16 = jnp.bfloat16
EPS = 1e-6
D_MODEL = 1024
N_MOD = 6
A_WIDTH = 512
B_GROUPS = 4
POOL_WINDOWS = (2, 4, 8, 16)
C_HEADS = 8
C_NOPE = 64
C_ROPE = 32
C_V = 64
C_Q_RANK = 256
C_KV_RANK = 128
HEAD_PAD = 128
ROPE_THETA = 10000.0
D_GROUPS = 4
D_CHUNK = 128
D_FF = 2816
FF_UNIT = 128
ADAM_LR = 0.001
ADAM_B1 = 0.9
ADAM_B2 = 0.999
ADAM_EPS = 1e-08
ADAM_WD = 0.01
ADAM_STEP = 10
N_CHIPS = 4
N_DEV = 8
LANES = 128
VMEM_BIG = 56 * 1024 * 1024
MESH = pl.DeviceIdType.MESH


def _sds(shape, dtype=F32):
    return jax.ShapeDtypeStruct(tuple(shape), dtype)


def _tile(n, cap, mult=128):
    if n <= cap:
        return n
    best = None
    for t in range(mult, cap + 1, mult):
        if n % t == 0:
            best = t
    assert best is not None, (n, cap, mult)
    return best


def _params(dims=None, vmem=None):
    return pltpu.CompilerParams(dimension_semantics=dims, vmem_limit_bytes=vmem)


def _shift_down(v, k):
    r = pltpu.roll(v, k, axis=0)
    t = lax.broadcasted_iota(jnp.int32, v.shape, 0)
    return jnp.where(t >= k, r, 0.0)


def _shift_up(v, k):
    n = v.shape[0]
    r = pltpu.roll(v, n - k, axis=0)
    t = lax.broadcasted_iota(jnp.int32, v.shape, 0)
    return jnp.where(t < n - k, r, 0.0)


def _sigmoid(v):
    return 1.0 / (1.0 + jnp.exp(-v))


_GELU_C = 0.7978845608028654
_GELU_A = 0.044715


def _gelu(v):
    return 0.5 * v * (1.0 + jnp.tanh(_GELU_C * (v + _GELU_A * v * v * v)))


def _gelu_grad(v):
    th = jnp.tanh(_GELU_C * (v + _GELU_A * v * v * v))
    return 0.5 * (1.0 + th) + 0.5 * v * (1.0 - th * th) * _GELU_C * (1.0 + 3.0 * _GELU_A * v * v)


_NN = (((1,), (0,)), ((), ()))
_NT = (((1,), (1,)), ((), ()))
_TN = (((0,), (0,)), ((), ()))


def _dot(a, b, dims=_NN):
    return lax.dot_general(a, b, dims, preferred_element_type=F32)


def _logical(t, groups):
    return (t.shape[-2], t.shape[-1] * groups)


def _block(tr, tc, groups, cols, where):
    if groups == 1:
        return pl.BlockSpec((tr, tc), where)
    per = cols // groups // tc

    def index(i, j, s):
        r, c = where(i, j, s)
        return (c // per, r, c % per)

    return pl.BlockSpec((None, tr, tc), index)


def matmul(a, b, mode, out_dtype, name, ga=1, gb=1, go=1, tm=None, tn=None, tk=None):
    (ar, ac), (br, bc) = _logical(a, ga), _logical(b, gb)
    if mode == "nn":
        m, k, n = ar, ac, bc
        a_col, b_col = "k", "n"
    elif mode == "nt":
        m, k, n = ar, ac, br
        a_col, b_col = "k", "k"
    else:
        k, m, n = ar, ac, bc
        a_col, b_col = "m", "n"
    limit = {"m": m, "n": n // go, "k": k}
    limit[a_col] = min(limit[a_col], ac // ga)
    limit[b_col] = min(limit[b_col], bc // gb)
    tm = tm or _tile(limit["m"], 1024, 128 if mode == "tn" else 16)
    tn = tn or _tile(limit["n"], 512)
    tk = tk or _tile(limit["k"], 2048, 16 if mode == "tn" else 128)
    nk = k // tk
    if mode == "nn":
        a_spec = _block(tm, tk, ga, ac, lambda i, j, s: (i, s))
        b_spec = _block(tk, tn, gb, bc, lambda i, j, s: (s, j))
        dims = _NN
    elif mode == "nt":
        a_spec = _block(tm, tk, ga, ac, lambda i, j, s: (i, s))
        b_spec = _block(tn, tk, gb, bc, lambda i, j, s: (j, s))
        dims = _NT
    else:
        a_spec = _block(tk, tm, ga, ac, lambda i, j, s: (s, i))
        b_spec = _block(tk, tn, gb, bc, lambda i, j, s: (s, j))
        dims = _TN
    o_spec = _block(tm, tn, go, n, lambda i, j, s: (i, j))
    out_shape = _sds((m, n), out_dtype) if go == 1 else _sds((go, m, n // go), out_dtype)

    def body(a_ref, b_ref, o_ref, acc_ref):
        s = pl.program_id(2)

        @pl.when(s == 0)
        def _():
            acc_ref[...] = jnp.zeros_like(acc_ref)

        acc_ref[...] += _dot(a_ref[...], b_ref[...], dims)

        @pl.when(s == nk - 1)
        def _():
            o_ref[...] = acc_ref[...].astype(o_ref.dtype)

    return pl.pallas_call(
        body, name=name, out_shape=out_shape, grid=(m // tm, n // tn, nk),
        in_specs=[a_spec, b_spec], out_specs=o_spec,
        scratch_shapes=[pltpu.VMEM((tm, tn), F32)],
        compiler_params=_params(("parallel", "parallel", "arbitrary"), VMEM_BIG),
    )(a, b)


def _rows(tm, n):
    return pl.BlockSpec((tm, n), lambda i: (i, 0))


def _vec(n):
    return pl.BlockSpec((1, n), lambda i: (0, 0))


def modnorm_fwd(x, g, sc, sh, name):
    s, d = x.shape
    tm = _tile(s, 256, 8)

    def body(x_ref, g_ref, sc_ref, sh_ref, o_ref):
        xv = x_ref[...]
        r = lax.rsqrt(jnp.mean(xv * xv, axis=-1, keepdims=True) + EPS)
        o_ref[...] = ((xv * r) * g_ref[...] * (1.0 + sc_ref[...]) + sh_ref[...]).astype(BF16)

    return pl.pallas_call(
        body, name=name, out_shape=_sds((s, d), BF16), grid=(s // tm,),
        in_specs=[_rows(tm, d), _vec(d), _vec(d), _vec(d)], out_specs=_rows(tm, d),
        compiler_params=_params(("parallel",)),
    )(x, g, sc, sh)


def resid_fwd(x, y, gate, name):
    s, d = x.shape
    tm = _tile(s, 256, 8)

    def body(x_ref, y_ref, g_ref, o_ref):
        o_ref[...] = x_ref[...] + g_ref[...] * y_ref[...]

    return pl.pallas_call(
        body, name=name, out_shape=_sds((s, d)), grid=(s // tm,),
        in_specs=[_rows(tm, d), _rows(tm, d), _vec(d)], out_specs=_rows(tm, d),
        compiler_params=_params(("parallel",)),
    )(x, y, gate)


def gate_bwd(dres, y, gate, name):
    s, d = dres.shape
    tm = _tile(s, 256, 8)

    def body(dr_ref, y_ref, g_ref, dy_ref, dg_ref):
        @pl.when(pl.program_id(0) == 0)
        def _():
            dg_ref[...] = jnp.zeros_like(dg_ref)

        dr = dr_ref[...]
        dy_ref[...] = (dr * g_ref[...]).astype(BF16)
        dg_ref[...] += jnp.sum(dr * y_ref[...], axis=0, keepdims=True)

    return pl.pallas_call(
        body, name=name, out_shape=(_sds((s, d), BF16), _sds((1, d))), grid=(s // tm,),
        in_specs=[_rows(tm, d), _rows(tm, d), _vec(d)], out_specs=(_rows(tm, d), _vec(d)),
        compiler_params=_params(("arbitrary",)),
    )(dres, y, gate)


def norm_bwd(x, dh, g, sc, dres, name):
    s, d = x.shape
    tm = _tile(s, 256, 8)
    nsteps = s // tm

    def body(x_ref, dh_ref, g_ref, sc_ref, dr_ref, dx_ref, dsh_ref, dsc_ref, dg_ref, a2_ref):
        i = pl.program_id(0)

        @pl.when(i == 0)
        def _():
            dsh_ref[...] = jnp.zeros_like(dsh_ref)
            a2_ref[...] = jnp.zeros_like(a2_ref)

        xv = x_ref[...]
        dh = dh_ref[...]
        r = lax.rsqrt(jnp.mean(xv * xv, axis=-1, keepdims=True) + EPS)
        xh = xv * r
        dsh_ref[...] += jnp.sum(dh, axis=0, keepdims=True)
        a2_ref[...] += jnp.sum(dh * xh, axis=0, keepdims=True)
        dxh = dh * (g_ref[...] * (1.0 + sc_ref[...]))
        dx = r * (dxh - xh * jnp.mean(dxh * xh, axis=-1, keepdims=True))
        dx_ref[...] = dr_ref[...] + dx

        @pl.when(i == nsteps - 1)
        def _():
            dsc_ref[...] = a2_ref[...] * g_ref[...]
            dg_ref[...] = a2_ref[...] * (1.0 + sc_ref[...])

    return pl.pallas_call(
        body, name=name, out_shape=(_sds((s, d)), _sds((1, d)), _sds((1, d)), _sds((1, d))), grid=(nsteps,),
        in_specs=[_rows(tm, d), _rows(tm, d), _vec(d), _vec(d), _rows(tm, d)],
        out_specs=(_rows(tm, d), _vec(d), _vec(d), _vec(d)),
        scratch_shapes=[pltpu.VMEM((1, d), F32)],
        compiler_params=_params(("arbitrary",)),
    )(x, dh, g, sc, dres)


def final_fwd_bwd(x, g, tgt):
    s, d = x.shape
    tm = _tile(s, 256, 8)

    def body(x_ref, g_ref, t_ref, dx_ref, dg_ref, loss_ref):
        @pl.when(pl.program_id(0) == 0)
        def _():
            dg_ref[...] = jnp.zeros_like(dg_ref)
            loss_ref[...] = jnp.zeros_like(loss_ref)

        xv = x_ref[...]
        gv = g_ref[...]
        r = lax.rsqrt(jnp.mean(xv * xv, axis=-1, keepdims=True) + EPS)
        xh = xv * r
        e = xh * gv - t_ref[...]
        row = jnp.sum(e * e, axis=-1, keepdims=True) * (0.5 / d)
        loss_ref[...] += jnp.sum(row, axis=0, keepdims=True)
        dy = e * (1.0 / d)
        dg_ref[...] += jnp.sum(dy * xh, axis=0, keepdims=True)
        dxh = dy * gv
        dx_ref[...] = r * (dxh - xh * jnp.mean(dxh * xh, axis=-1, keepdims=True))

    return pl.pallas_call(
        body, name="final_fwd_bwd", out_shape=(_sds((s, d)), _sds((1, d)), _sds((1, LANES))), grid=(s // tm,),
        in_specs=[_rows(tm, d), _vec(d), _rows(tm, d)], out_specs=(_rows(tm, d), _vec(d), _vec(LANES)),
        compiler_params=_params(("arbitrary",)),
    )(x, g, tgt)


def _taps(v):
    return _shift_down(v, 2), _shift_down(v, 1), v


def _conv3_taps(taps, w):
    return w[0:1, :] * taps[0] + w[1:2, :] * taps[1] + w[2:3, :] * taps[2]


def _conv3(v, w):
    return _conv3_taps(_taps(v), w)


def _conv3_t(dv, w):
    return w[0:1, :] * _shift_up(dv, 2) + w[1:2, :] * _shift_up(dv, 1) + w[2:3, :] * dv


def _conv3_dw_taps(dv, taps):
    return jnp.concatenate([jnp.sum(dv * t, axis=0, keepdims=True) for t in taps], axis=0)


def _conv3_dw(dv, v):
    return _conv3_dw_taps(dv, _taps(v))


def gconv_fwd(z, conv_w):
    s = z.shape[0]
    nb = A_WIDTH // LANES

    def body(b_ref, c_ref, a_ref, w_ref, o_ref):
        b, c, a = b_ref[...].astype(F32), c_ref[...].astype(F32), a_ref[...].astype(F32)
        o_ref[...] = (b * _conv3(c * a, w_ref[...])).astype(BF16)

    col = lambda off: pl.BlockSpec((s, LANES), lambda j: (0, off + j))
    return pl.pallas_call(
        body, name="gconv_fwd", out_shape=_sds((s, A_WIDTH), BF16), grid=(nb,),
        in_specs=[col(0), col(nb), col(2 * nb), pl.BlockSpec((3, LANES), lambda j: (0, j))],
        out_specs=pl.BlockSpec((s, LANES), lambda j: (0, j)),
        compiler_params=_params(("parallel",), VMEM_BIG),
    )(z, z, z, conv_w)


def gconv_bwd(z, conv_w, dycat):
    s = z.shape[0]
    nb = A_WIDTH // LANES

    def body(b_ref, c_ref, a_ref, w_ref, dy_ref, db_ref, dc_ref, da_ref, dw_ref):
        c, a, w, dy = c_ref[...].astype(F32), a_ref[...].astype(F32), w_ref[...], dy_ref[...].astype(F32)
        ca = c * a
        db_ref[...] = (dy * _conv3(ca, w)).astype(BF16)
        dconv = dy * b_ref[...].astype(F32)
        dw_ref[...] = _conv3_dw(dconv, ca)
        dca = _conv3_t(dconv, w)
        dc_ref[...] = (dca * a).astype(BF16)
        da_ref[...] = (dca * c).astype(BF16)

    col = lambda off: pl.BlockSpec((s, LANES), lambda j: (0, off + j))
    wspec = pl.BlockSpec((3, LANES), lambda j: (0, j))
    part = _sds((s, A_WIDTH), BF16)
    return pl.pallas_call(
        body, name="gconv_bwd", out_shape=(part, part, part, _sds((3, A_WIDTH))), grid=(nb,),
        in_specs=[col(0), col(nb), col(2 * nb), wspec, col(0)],
        out_specs=(col(0), col(0), col(0), wspec),
        compiler_params=_params(("parallel",), VMEM_BIG),
    )(z, z, z, conv_w, dycat)


def _pool_counts(s, w):
    t = lax.broadcasted_iota(jnp.int32, (s, 1), 0)
    return jnp.minimum(t + 1, w).astype(F32)


def _pooled(p, levels):
    acc = p
    for lv in range(levels):
        acc = acc + _shift_down(acc, 2 ** lv)
    return acc / _pool_counts(p.shape[0], 2 ** levels) - p


def pool_fwd(z, mix_w, scale):
    s = z.shape[0]

    def make(g):
        def body_g(p_ref, m_ref, sc_ref, o_ref):
            pooled = _pooled(p_ref[...].astype(F32), g + 1)
            y = _dot(pooled.astype(BF16), m_ref[...].astype(BF16))
            o_ref[...] = (y * sc_ref[...]).astype(BF16)
        return body_g

    outs = []
    for g in range(B_GROUPS):
        outs.append(pl.pallas_call(
            make(g), name=f"pool_fwd{g}", out_shape=_sds((s, LANES), BF16), grid=(1,),
            in_specs=[pl.BlockSpec((s, LANES), lambda i, g=g: (0, 3 * (A_WIDTH // LANES) + g)),
                      pl.BlockSpec((None, LANES, LANES), lambda i, g=g: (g, 0, 0)),
                      pl.BlockSpec((1, LANES), lambda i, g=g: (0, g))],
            out_specs=pl.BlockSpec((s, LANES), lambda i: (0, 0)),
            compiler_params=_params(("arbitrary",), VMEM_BIG),
        )(z, mix_w, scale))
    return outs


def pool_bwd(z, mix_w, scale, dycat):
    s = z.shape[0]

    def make(g):
        w = 2 ** (g + 1)

        def body_g(p_ref, m_ref, sc_ref, dy_ref, dp_ref, dm_ref, dsc_ref):
            pooled = _pooled(p_ref[...].astype(F32), g + 1)
            mw = m_ref[...].astype(BF16)
            pb = pooled.astype(BF16)
            dy = dy_ref[...].astype(F32)
            dsc_ref[...] = jnp.sum(dy * _dot(pb, mw), axis=0, keepdims=True)
            dmix = (dy * sc_ref[...]).astype(BF16)
            dm_ref[...] = _dot(pb, dmix, _TN)
            dpool = _dot(dmix, mw, _NT)
            acc = dpool / _pool_counts(s, w)
            for lv in range(g + 1):
                acc = acc + _shift_up(acc, 2 ** lv)
            dp_ref[...] = (acc - dpool).astype(BF16)
        return body_g

    outs = []
    for g in range(B_GROUPS):
        outs.append(pl.pallas_call(
            make(g), name=f"pool_bwd{g}",
            out_shape=(_sds((s, LANES), BF16), _sds((LANES, LANES)), _sds((1, LANES))), grid=(1,),
            in_specs=[pl.BlockSpec((s, LANES), lambda i, g=g: (0, 3 * (A_WIDTH // LANES) + g)),
                      pl.BlockSpec((None, LANES, LANES), lambda i, g=g: (g, 0, 0)),
                      pl.BlockSpec((1, LANES), lambda i, g=g: (0, g)),
                      pl.BlockSpec((s, LANES), lambda i, g=g: (0, A_WIDTH // LANES + g))],
            out_specs=(pl.BlockSpec((s, LANES), lambda i: (0, 0)), pl.BlockSpec((LANES, LANES), lambda i: (0, 0)),
                       pl.BlockSpec((1, LANES), lambda i: (0, 0))),
            compiler_params=_params(("arbitrary",), VMEM_BIG),
        )(z, mix_w, scale, dycat))
    return outs


_FF_BLOCKS = D_FF // FF_UNIT


def _ff_spec(s):
    return pl.BlockSpec((2, s, FF_UNIT), lambda j: (0, 0, j))


def _ff_wspecs():
    return [pl.BlockSpec((3, FF_UNIT), lambda j: (0, j)), pl.BlockSpec((3, FF_UNIT), lambda j: (0, _FF_BLOCKS + j))]


_FF_ROWS = 64
_FF_HALO = 16


def _chunk_taps(z_ref, half, c):
    start = pl.multiple_of(c * _FF_ROWS, _FF_ROWS)
    before = pl.multiple_of(jnp.maximum(c * _FF_ROWS - _FF_HALO, 0), _FF_HALO)
    halo = z_ref[half, pl.ds(before, _FF_HALO), :].astype(F32)
    halo = jnp.where(c > 0, halo, 0.0)
    win = jnp.concatenate([halo, z_ref[half, pl.ds(start, _FF_ROWS), :].astype(F32)], axis=0)
    return tuple(pltpu.roll(win, k, axis=0)[_FF_HALO:] for k in (2, 1)) + (win[_FF_HALO:],)


def _fold8(v):
    acc = v[0:8]
    for r in range(8, v.shape[0], 8):
        acc = acc + v[r:r + 8]
    return acc


def ffn_act_fwd(zf, conv_w, name):
    s = zf.shape[1]
    assert s % _FF_ROWS == 0

    def body(z_ref, wg_ref, wu_ref, o_ref):
        g = _conv3(z_ref[0].astype(F32), wg_ref[...])
        u = _conv3(z_ref[1].astype(F32), wu_ref[...])
        o_ref[...] = (g * _sigmoid(g) * u).astype(BF16)

    return pl.pallas_call(
        body, name=name, out_shape=_sds((s, D_FF), BF16), grid=(_FF_BLOCKS,),
        in_specs=[_ff_spec(s)] + _ff_wspecs(), out_specs=pl.BlockSpec((s, FF_UNIT), lambda j: (0, j)),
        compiler_params=_params(("parallel",), VMEM_BIG),
    )(zf, conv_w, conv_w)


def ffn_act_bwd(zf, conv_w, da, name):
    s = zf.shape[1]
    assert s % _FF_ROWS == 0
    nchunks = s // _FF_ROWS

    def body(z_ref, wg_ref, wu_ref, da_ref, dz_ref, dw_ref, dg_ref, du_ref):
        wg, wu = wg_ref[...], wu_ref[...]

        def first(c, acc):
            rows = pl.ds(pl.multiple_of(c * _FF_ROWS, _FF_ROWS), _FF_ROWS)
            tg, tu = _chunk_taps(z_ref, 0, c), _chunk_taps(z_ref, 1, c)
            g = _conv3_taps(tg, wg)
            u = _conv3_taps(tu, wu)
            dav = da_ref[rows, :].astype(F32)
            sg = _sigmoid(g)
            dg = dav * u * (sg * (1.0 + g * (1.0 - sg)))
            du = dav * (g * sg)
            dg_ref[rows, :] = dg
            du_ref[rows, :] = du
            return tuple(a + _fold8(d * t) for a, (d, t) in zip(acc, [(dg, t) for t in tg] + [(du, t) for t in tu]))

        zero = jnp.zeros((8, FF_UNIT), F32)
        acc = lax.fori_loop(0, nchunks, first, (zero,) * 6)
        sums = [jnp.sum(a, axis=0, keepdims=True) for a in acc]
        dw_ref[0] = jnp.concatenate(sums[:3], axis=0)
        dw_ref[1] = jnp.concatenate(sums[3:], axis=0)

        tail = pl.ds(s, _FF_HALO)
        dg_ref[tail, :] = jnp.zeros((_FF_HALO, FF_UNIT), F32)
        du_ref[tail, :] = jnp.zeros((_FF_HALO, FF_UNIT), F32)
        span = _FF_ROWS + _FF_HALO

        def second(c, carry):
            start = pl.multiple_of(c * _FF_ROWS, _FF_ROWS)
            for half, (d_ref, w) in enumerate(((dg_ref, wg), (du_ref, wu))):
                win = d_ref[pl.ds(start, span), :]
                dz = (w[0:1, :] * pltpu.roll(win, span - 2, axis=0)[:_FF_ROWS]
                      + w[1:2, :] * pltpu.roll(win, span - 1, axis=0)[:_FF_ROWS] + w[2:3, :] * win[:_FF_ROWS])
                dz_ref[half, pl.ds(start, _FF_ROWS), :] = dz.astype(BF16)
            return carry

        lax.fori_loop(0, nchunks, second, 0)

    return pl.pallas_call(
        body, name=name, out_shape=(_sds((2, s, D_FF), BF16), _sds((2, 3, D_FF))), grid=(_FF_BLOCKS,),
        in_specs=[_ff_spec(s)] + _ff_wspecs() + [pl.BlockSpec((s, FF_UNIT), lambda j: (0, j))],
        out_specs=(_ff_spec(s), pl.BlockSpec((2, 3, FF_UNIT), lambda j: (0, 0, j))),
        scratch_shapes=[pltpu.VMEM((s + _FF_HALO, FF_UNIT), F32), pltpu.VMEM((s + _FF_HALO, FF_UNIT), F32)],
        compiler_params=_params(("parallel",), VMEM_BIG),
    )(zf, conv_w, conv_w, da)


def _rope(v, cs, s1, s2):
    return v * cs + pltpu.roll(v, LANES - C_ROPE // 2, axis=1) * s1 + pltpu.roll(v, C_ROPE // 2, axis=1) * s2


def _rope_t(dv, cs, s1, s2):
    return dv * cs + pltpu.roll(dv * s1, C_ROPE // 2, axis=1) + pltpu.roll(dv * s2, LANES - C_ROPE // 2, axis=1)


def _kpe_mask(shape):
    lane = lax.broadcasted_iota(jnp.int32, shape, 1)
    return (lane >= C_NOPE) & (lane < C_NOPE + C_ROPE)


def _rms(v, g):
    r = lax.rsqrt(jnp.mean(v * v, axis=-1, keepdims=True) + EPS)
    return v * r, r


def _rms_bwd(dn, xh, r, g):
    dxh = dn * g
    return r * (dxh - xh * jnp.mean(dxh * xh, axis=-1, keepdims=True)), jnp.sum(dn * xh, axis=0, keepdims=True)


_ZQ = C_Q_RANK + C_KV_RANK + HEAD_PAD
_HW = C_HEADS * HEAD_PAD


def mla_pre_fwd(z, gq, gkv, wq, wk, wv, cs, s1, s2):
    s = z.shape[0]
    tm = _tile(s, 256, 8)

    def body(z_ref, gq_ref, gkv_ref, wq_ref, wk_ref, wv_ref, cs_ref, s1_ref, s2_ref, q_ref, k_ref, v_ref):
        zv = z_ref[...].astype(F32)
        cst, s1t, s2t = cs_ref[...], s1_ref[...], s2_ref[...]
        qh, _ = _rms(zv[:, :C_Q_RANK], None)
        qn = (qh * gq_ref[...]).astype(BF16)
        q = _dot(qn, wq_ref[...])
        kh, _ = _rms(zv[:, C_Q_RANK:C_Q_RANK + C_KV_RANK], None)
        kvn = (kh * gkv_ref[...]).astype(BF16)
        k = _dot(kvn, wk_ref[...])
        v_ref[...] = _dot(kvn, wv_ref[...]).astype(BF16)
        kpe = _rope(zv[:, C_Q_RANK + C_KV_RANK:], cst, s1t, s2t)
        for h in range(C_HEADS):
            sl = slice(h * HEAD_PAD, (h + 1) * HEAD_PAD)
            q_ref[:, sl] = _rope(q[:, sl], cst, s1t, s2t).astype(BF16)
            k_ref[:, sl] = (k[:, sl] + kpe).astype(BF16)

    full = lambda r, c: pl.BlockSpec((r, c), lambda i: (0, 0))
    hw = _sds((s, _HW), BF16)
    return pl.pallas_call(
        body, name="mla_pre_fwd", out_shape=(hw, hw, hw), grid=(s // tm,),
        in_specs=[_rows(tm, _ZQ), _vec(C_Q_RANK), _vec(C_KV_RANK), full(C_Q_RANK, _HW), full(C_KV_RANK, _HW),
                  full(C_KV_RANK, _HW), _rows(tm, LANES), _rows(tm, LANES), _rows(tm, LANES)],
        out_specs=(_rows(tm, _HW), _rows(tm, _HW), _rows(tm, _HW)),
        compiler_params=_params(("parallel",), VMEM_BIG),
    )(z, gq, gkv, wq, wk, wv, cs, s1, s2)


def mla_pre_bwd(z, gq, gkv, wq, wk, wv, cs, s1, s2, dq, dk, dv):
    s = z.shape[0]
    tm = _tile(s, 256, 8)

    def body(z_ref, gq_ref, gkv_ref, wq_ref, wk_ref, wv_ref, cs_ref, s1_ref, s2_ref, dq_ref, dk_ref, dv_ref,
             dz_ref, dwq_ref, dwk_ref, dwv_ref, dgq_ref, dgkv_ref):
        @pl.when(pl.program_id(0) == 0)
        def _():
            dwq_ref[...] = jnp.zeros_like(dwq_ref)
            dwk_ref[...] = jnp.zeros_like(dwk_ref)
            dwv_ref[...] = jnp.zeros_like(dwv_ref)
            dgq_ref[...] = jnp.zeros_like(dgq_ref)
            dgkv_ref[...] = jnp.zeros_like(dgkv_ref)

        zv = z_ref[...].astype(F32)
        cst, s1t, s2t = cs_ref[...], s1_ref[...], s2_ref[...]
        gqv, gkvv = gq_ref[...], gkv_ref[...]
        qh, rq = _rms(zv[:, :C_Q_RANK], None)
        qn = (qh * gqv).astype(BF16)
        kh, rk = _rms(zv[:, C_Q_RANK:C_Q_RANK + C_KV_RANK], None)
        kvn = (kh * gkvv).astype(BF16)

        dqv = dq_ref[...].astype(F32)
        dqp = jnp.concatenate(
            [_rope_t(dqv[:, h * HEAD_PAD:(h + 1) * HEAD_PAD], cst, s1t, s2t) for h in range(C_HEADS)], axis=1
        ).astype(BF16)
        dwq_ref[...] += _dot(qn, dqp, _TN)
        dqn = _dot(dqp, wq_ref[...], _NT)
        dql, dgq = _rms_bwd(dqn, qh, rq, gqv)
        dgq_ref[...] += dgq

        dkv = dk_ref[...]
        dkb = dkv.astype(BF16)
        dvb = dv_ref[...].astype(BF16)
        dwk_ref[...] += _dot(kvn, dkb, _TN)
        dwv_ref[...] += _dot(kvn, dvb, _TN)
        dkvn = _dot(dkb, wk_ref[...], _NT) + _dot(dvb, wv_ref[...], _NT)
        dkl, dgkv = _rms_bwd(dkvn, kh, rk, gkvv)
        dgkv_ref[...] += dgkv

        dkpe = dkv[:, :HEAD_PAD]
        for h in range(1, C_HEADS):
            dkpe = dkpe + dkv[:, h * HEAD_PAD:(h + 1) * HEAD_PAD]
        dkpe = _rope_t(jnp.where(_kpe_mask(dkpe.shape), dkpe, 0.0), cst, s1t, s2t)
        dz_ref[...] = jnp.concatenate([dql, dkl, dkpe], axis=1).astype(BF16)

    full = lambda r, c: pl.BlockSpec((r, c), lambda i: (0, 0))
    return pl.pallas_call(
        body, name="mla_pre_bwd",
        out_shape=(_sds((s, _ZQ), BF16), _sds((C_Q_RANK, _HW)), _sds((C_KV_RANK, _HW)), _sds((C_KV_RANK, _HW)),
                   _sds((1, C_Q_RANK)), _sds((1, C_KV_RANK))),
        grid=(s // tm,),
        in_specs=[_rows(tm, _ZQ), _vec(C_Q_RANK), _vec(C_KV_RANK), full(C_Q_RANK, _HW), full(C_KV_RANK, _HW),
                  full(C_KV_RANK, _HW), _rows(tm, LANES), _rows(tm, LANES), _rows(tm, LANES),
                  _rows(tm, _HW), _rows(tm, _HW), _rows(tm, _HW)],
        out_specs=(_rows(tm, _ZQ), full(C_Q_RANK, _HW), full(C_KV_RANK, _HW), full(C_KV_RANK, _HW),
                   _vec(C_Q_RANK), _vec(C_KV_RANK)),
        compiler_params=_params(("arbitrary",), VMEM_BIG),
    )(z, gq, gkv, wq, wk, wv, cs, s1, s2, dq, dk, dv)


_ATT_SCALE = (C_NOPE + C_ROPE) ** -0.5
_NEG = -1e30


def _att_probs(q, k, row0):
    sc = _dot(q, k, _NT) * _ATT_SCALE
    qpos = row0 + lax.broadcasted_iota(jnp.int32, sc.shape, 0)
    kpos = lax.broadcasted_iota(jnp.int32, sc.shape, 1)
    sc = jnp.where(kpos <= qpos, sc, _NEG)
    e = jnp.exp(sc - jnp.max(sc, axis=-1, keepdims=True))
    return e / jnp.sum(e, axis=-1, keepdims=True)


def _causal_cases(i, nq, tq, fn):
    if nq > 8:
        fn(nq * tq)
        return
    for blk in range(nq):
        pl.when(i == blk)(functools.partial(fn, (blk + 1) * tq))


def attn_fwd(q, k, v):
    s = q.shape[0]
    tq = _tile(s, 256, 8)
    nq = s // tq

    def body(q_ref, k_ref, v_ref, o_ref):
        i = pl.program_id(1)

        def case(nk):
            p = _att_probs(q_ref[...], k_ref[:nk, :], i * tq)
            o_ref[...] = _dot(p.astype(BF16), v_ref[:nk, :]).astype(BF16)

        _causal_cases(i, nq, tq, case)

    qspec = pl.BlockSpec((tq, HEAD_PAD), lambda h, i: (i, h))
    kspec = pl.BlockSpec((s, HEAD_PAD), lambda h, i: (0, h))
    return pl.pallas_call(
        body, name="attn_fwd", out_shape=_sds((s, _HW), BF16), grid=(C_HEADS, s // tq),
        in_specs=[qspec, kspec, kspec], out_specs=qspec,
        compiler_params=_params(("parallel", "parallel"), VMEM_BIG),
    )(q, k, v)


def attn_bwd(q, k, v, o, do_all, do_col0):
    s = q.shape[0]
    tq = _tile(s, 256, 8)

    def body(q_ref, k_ref, v_ref, o_ref, do_ref, dq_ref, dk_ref, dv_ref):
        i = pl.program_id(1)

        @pl.when(i == 0)
        def _():
            dk_ref[...] = jnp.zeros_like(dk_ref)
            dv_ref[...] = jnp.zeros_like(dv_ref)

        def case(nk):
            qv, kv, vv, dov = q_ref[...], k_ref[:nk, :], v_ref[:nk, :], do_ref[...]
            p = _att_probs(qv, kv, i * tq)
            dp = _dot(dov, vv, _NT)
            delta = jnp.sum(dov.astype(F32) * o_ref[...].astype(F32), axis=-1, keepdims=True)
            ds = (p * (dp - delta) * _ATT_SCALE).astype(BF16)
            dq_ref[...] = _dot(ds, kv).astype(BF16)
            dk_ref[:nk, :] += _dot(ds, qv, _TN)
            dv_ref[:nk, :] += _dot(p.astype(BF16), dov, _TN)

        _causal_cases(i, s // tq, tq, case)

    qspec = pl.BlockSpec((tq, HEAD_PAD), lambda h, i: (i, h))
    dospec = pl.BlockSpec((tq, HEAD_PAD), lambda h, i: (i, do_col0 + h))
    kspec = pl.BlockSpec((s, HEAD_PAD), lambda h, i: (0, h))
    return pl.pallas_call(
        body, name="attn_bwd", out_shape=(_sds((s, _HW), BF16), _sds((s, _HW)), _sds((s, _HW))),
        grid=(C_HEADS, s // tq),
        in_specs=[qspec, kspec, kspec, qspec, dospec], out_specs=(qspec, kspec, kspec),
        compiler_params=_params(("parallel", "arbitrary"), VMEM_BIG),
    )(q, k, v, o, do_all)


_DW = D_GROUPS * LANES


def _tril_bf16(w):
    r = lax.broadcasted_iota(jnp.int32, w.shape, 0)
    c = lax.broadcasted_iota(jnp.int32, w.shape, 1)
    return jnp.where(c <= r, w, 0.0).astype(BF16)


def _sgu_forward(zu, zv, lg, lb, ws_ref, bs):
    u = _gelu(zu)
    v = _gelu(zv)
    mu = jnp.mean(v, axis=-1, keepdims=True)
    vc = v - mu
    rstd = lax.rsqrt(jnp.mean(vc * vc, axis=-1, keepdims=True) + EPS)
    xh = vc * rstd
    vln = (xh * lg + lb).astype(BF16)
    mixed = []
    for g in range(D_GROUPS):
        wg = _tril_bf16(ws_ref[g])
        mixed.append(_dot(wg, vln[:, g * LANES:(g + 1) * LANES]) + bs[:, g:g + 1])
    return u, xh, rstd, vln, jnp.concatenate(mixed, axis=1)


def sgu_fwd(z, lg, lb, ws, bs_t):
    s = z.shape[0]
    nchunk = s // D_CHUNK

    def body(zu_ref, zv_ref, lg_ref, lb_ref, ws_ref, bs_ref, o_ref):
        u, _, _, _, mixed = _sgu_forward(zu_ref[...].astype(F32), zv_ref[...].astype(F32), lg_ref[...], lb_ref[...],
                                         ws_ref, bs_ref[...])
        o_ref[...] = (u * mixed).astype(BF16)

    return pl.pallas_call(
        body, name="sgu_fwd", out_shape=_sds((s, _DW), BF16), grid=(nchunk,),
        in_specs=[pl.BlockSpec((D_CHUNK, _DW), lambda n: (n, 1)), pl.BlockSpec((D_CHUNK, _DW), lambda n: (n, 2)),
                  _vec(_DW), _vec(_DW), pl.BlockSpec((D_GROUPS, D_CHUNK, D_CHUNK), lambda n: (0, 0, 0)),
                  pl.BlockSpec((D_CHUNK, LANES), lambda n: (0, 0))],
        out_specs=pl.BlockSpec((D_CHUNK, _DW), lambda n: (n, 0)),
        compiler_params=_params(("parallel",)),
    )(z, z, lg, lb, ws, bs_t)


def sgu_bwd(z, lg, lb, ws, bs_t, dycat, dy_col):
    s = z.shape[0]
    nchunk = s // D_CHUNK

    def body(zu_ref, zv_ref, lg_ref, lb_ref, ws_ref, bs_ref, dy_ref, dzu_ref, dzv_ref, dws_ref, dbs_ref, dlg_ref,
             dlb_ref):
        @pl.when(pl.program_id(0) == 0)
        def _():
            dws_ref[...] = jnp.zeros_like(dws_ref)
            dbs_ref[...] = jnp.zeros_like(dbs_ref)
            dlg_ref[...] = jnp.zeros_like(dlg_ref)
            dlb_ref[...] = jnp.zeros_like(dlb_ref)

        zu, zv, lg = zu_ref[...].astype(F32), zv_ref[...].astype(F32), lg_ref[...]
        u, xh, rstd, vln, mixed = _sgu_forward(zu, zv, lg, lb_ref[...], ws_ref, bs_ref[...])
        dy = dy_ref[...].astype(F32)
        dzu_ref[...] = (dy * mixed * _gelu_grad(zu)).astype(BF16)
        dmix = dy * u
        lane = lax.broadcasted_iota(jnp.int32, (D_CHUNK, LANES), 1)
        row = lax.broadcasted_iota(jnp.int32, (D_CHUNK, D_CHUNK), 0)
        colm = lax.broadcasted_iota(jnp.int32, (D_CHUNK, D_CHUNK), 1)
        dvln = []
        dbs = jnp.zeros((D_CHUNK, LANES), F32)
        for g in range(D_GROUPS):
            sl = slice(g * LANES, (g + 1) * LANES)
            dmg = dmix[:, sl]
            dbs = dbs + jnp.where(lane == g, jnp.sum(dmg, axis=-1, keepdims=True), 0.0)
            dmb = dmg.astype(BF16)
            dws_ref[g] += jnp.where(colm <= row, _dot(dmb, vln[:, sl], _NT), 0.0)
            dvln.append(_dot(_tril_bf16(ws_ref[g]), dmb, _TN))
        dbs_ref[...] += dbs
        dvln = jnp.concatenate(dvln, axis=1)
        dlg_ref[...] += jnp.sum(dvln * xh, axis=0, keepdims=True)
        dlb_ref[...] += jnp.sum(dvln, axis=0, keepdims=True)
        dxh = dvln * lg
        dvv = rstd * (dxh - jnp.mean(dxh, axis=-1, keepdims=True) - xh * jnp.mean(dxh * xh, axis=-1, keepdims=True))
        dzv_ref[...] = (dvv * _gelu_grad(zv)).astype(BF16)

    wsspec = pl.BlockSpec((D_GROUPS, D_CHUNK, D_CHUNK), lambda n: (0, 0, 0))
    chunk = lambda cidx: pl.BlockSpec((D_CHUNK, _DW), lambda n: (n, cidx))
    return pl.pallas_call(
        body, name="sgu_bwd",
        out_shape=(_sds((s, _DW), BF16), _sds((s, _DW), BF16), _sds((D_GROUPS, D_CHUNK, D_CHUNK)),
                   _sds((D_CHUNK, LANES)), _sds((1, _DW)), _sds((1, _DW))),
        grid=(nchunk,),
        in_specs=[chunk(1), chunk(2), _vec(_DW), _vec(_DW), wsspec, pl.BlockSpec((D_CHUNK, LANES), lambda n: (0, 0)),
                  chunk(dy_col)],
        out_specs=(chunk(0), chunk(0), wsspec, pl.BlockSpec((D_CHUNK, LANES), lambda n: (0, 0)), _vec(_DW), _vec(_DW)),
        compiler_params=_params(("arbitrary",)),
    )(z, z, lg, lb, ws, bs_t, dycat)


def ada_mod(c_all, ada_w, ada_b):
    nl, d, n = ada_w.shape
    nb = c_all.shape[0]
    tn = _tile(n, 512)

    def body(c_ref, w_ref, b_ref, o_ref):
        cv = c_ref[...]
        ca = (cv * _sigmoid(cv)).astype(BF16)
        o_ref[...] = _dot(ca, w_ref[...].astype(BF16)) + b_ref[...]

    return pl.pallas_call(
        body, name="ada_mod", out_shape=_sds((nl, nb, n)), grid=(nl, n // tn),
        in_specs=[pl.BlockSpec((nb, d), lambda l, j: (0, 0)), pl.BlockSpec((None, d, tn), lambda l, j: (l, 0, j)),
                  pl.BlockSpec((None, 1, tn), lambda l, j: (l, 0, j))],
        out_specs=pl.BlockSpec((None, nb, tn), lambda l, j: (l, 0, j)),
        compiler_params=_params(("parallel", "parallel")),
    )(c_all, ada_w, ada_b.reshape(nl, 1, n))


def ada_grad(c_all_t, dmod):
    d, nb = c_all_t.shape
    nl, _, n = dmod.shape
    tn = _tile(n, 512)
    tr = _tile(d, 256, 8)

    def body(c_ref, dm_ref, o_ref):
        cv = c_ref[...]
        ca = cv * _sigmoid(cv)
        dm = dm_ref[...]
        acc = ca[:, 0:1] * dm[0:1, :]
        for b in range(1, nb):
            acc = acc + ca[:, b:b + 1] * dm[b:b + 1, :]
        o_ref[...] = acc

    return pl.pallas_call(
        body, name="ada_grad", out_shape=_sds((nl, d, n)), grid=(nl, n // tn, d // tr),
        in_specs=[pl.BlockSpec((tr, nb), lambda l, j, r: (r, 0)), pl.BlockSpec((None, nb, tn), lambda l, j, r: (l, 0, j))],
        out_specs=pl.BlockSpec((None, tr, tn), lambda l, j, r: (l, r, j)),
        compiler_params=_params(("parallel", "parallel", "parallel")),
    )(c_all_t, dmod)


_ADAM_BLOCK = 256 * 1024


def _adam_rows(rows, cols):
    if rows * cols <= _ADAM_BLOCK or rows % 8:
        return rows
    return _tile(rows, max(8, _ADAM_BLOCK // cols), 8)


def _adam_update(w, gv, m, v):
    inv_bc1 = 1.0 / (1.0 - ADAM_B1 ** ADAM_STEP)
    inv_bc2 = 1.0 / (1.0 - ADAM_B2 ** ADAM_STEP)
    nm = ADAM_B1 * m + (1.0 - ADAM_B1) * gv
    nv = ADAM_B2 * v + (1.0 - ADAM_B2) * (gv * gv)
    return -ADAM_LR * ((nm * inv_bc1) / (jnp.sqrt(nv * inv_bc2) + ADAM_EPS) + ADAM_WD * w), nm, nv


def adamw(w, g, m, v, name, copy_grad=False):
    shape = w.shape
    cols = shape[-1]
    rows = w.size // cols
    tr = _adam_rows(rows, cols)

    def body(w_ref, g_ref, m_ref, v_ref, d_ref, nm_ref, nv_ref, *go_ref):
        gv = g_ref[...]
        d_ref[...], nm_ref[...], nv_ref[...] = _adam_update(w_ref[...], gv, m_ref[...], v_ref[...])
        if copy_grad:
            go_ref[0][...] = gv

    spec = pl.BlockSpec((tr, cols), lambda i: (i, 0))
    out = _sds((rows, cols))
    r2 = lambda t: t.reshape(rows, cols)
    nout = 4 if copy_grad else 3
    res = pl.pallas_call(
        body, name=name, out_shape=(out,) * nout, grid=(rows // tr,),
        in_specs=[spec] * 4, out_specs=(spec,) * nout, compiler_params=_params(("parallel",)),
    )(r2(w), r2(g), r2(m), r2(v))
    grad = res[3] if copy_grad else g
    return grad.reshape(shape), res[0].reshape(shape), res[1].reshape(shape), res[2].reshape(shape)


def adamw_small(ws, gs, ms, vs):
    n = len(ws)
    flat = lambda t: t.reshape(-1, t.shape[-1])

    def body(*refs):
        ins, outs = refs[:4 * n], refs[4 * n:]
        for i in range(n):
            w_ref, g_ref, m_ref, v_ref = ins[4 * i:4 * i + 4]
            outs[3 * i][...], outs[3 * i + 1][...], outs[3 * i + 2][...] = _adam_update(
                w_ref[...], g_ref[...], m_ref[...], v_ref[...])

    operands = [flat(t) for quad in zip(ws, gs, ms, vs) for t in quad]
    res = pl.pallas_call(
        body, name="adamw_small", out_shape=tuple(_sds(flat(w).shape) for w in ws for _ in range(3)),
    )(*operands)
    return [(g, res[3 * i].reshape(w.shape), res[3 * i + 1].reshape(w.shape), res[3 * i + 2].reshape(w.shape))
            for i, (w, g) in enumerate(zip(ws, gs))]


def adamw_layers(w, g0, g1, m, v, name):
    _, rows, cols = w.shape
    tr = _adam_rows(rows, cols)

    def body(w_ref, g0_ref, g1_ref, m_ref, v_ref, g_ref, d_ref, nm_ref, nv_ref):
        gv = jnp.where(pl.program_id(0) == 0, g0_ref[...], g1_ref[...])
        g_ref[...] = gv
        d_ref[...], nm_ref[...], nv_ref[...] = _adam_update(w_ref[...], gv, m_ref[...], v_ref[...])

    spec = pl.BlockSpec((None, tr, cols), lambda l, i: (l, i, 0))
    gspec = pl.BlockSpec((tr, cols), lambda l, i: (i, 0))
    out = _sds((2, rows, cols))
    return pl.pallas_call(
        body, name=name, out_shape=(out, out, out, out), grid=(2, rows // tr),
        in_specs=[spec, gspec, gspec, spec, spec], out_specs=(spec,) * 4, compiler_params=_params(("parallel", "parallel")),
    )(w, g0, g1, m, v)


def sum8(gathered):
    _, r, _ = gathered.shape
    tr = _tile(r, 512, 8)

    def body(g_ref, o_ref):
        acc = g_ref[0]
        for dev in range(1, N_DEV):
            acc = acc + g_ref[dev]
        o_ref[...] = acc

    return pl.pallas_call(
        body, name="sum8", out_shape=_sds((r, LANES)), grid=(r // tr,),
        in_specs=[pl.BlockSpec((N_DEV, tr, LANES), lambda i: (0, i, 0))], out_specs=pl.BlockSpec((tr, LANES), lambda i: (i, 0)),
        compiler_params=_params(("parallel",)),
    )(gathered)


_SUM_BLOCK = 512 * 1024


def _sum_rows(rh, cols):
    return rh if rh * cols <= _SUM_BLOCK else _tile(rh, max(16, _SUM_BLOCK // cols), 16)


def pair_sum(g, recv, core, name):
    _, r, cols = g.shape
    rh = r // 2
    tr = _sum_rows(rh, cols)
    per = rh // tr

    def body(c_ref, a_ref, b_ref, o_ref):
        del c_ref
        o_ref[...] = (a_ref[...].astype(F32) + b_ref[...].astype(F32)).astype(BF16)

    grid_spec = pltpu.PrefetchScalarGridSpec(
        num_scalar_prefetch=1, grid=(N_CHIPS, per),
        in_specs=[pl.BlockSpec((None, tr, cols), lambda k, i, c: (k, c[0] * per + i, 0)),
                  pl.BlockSpec((None, tr, cols), lambda k, i, c: (k, i, 0))],
        out_specs=pl.BlockSpec((None, tr, cols), lambda k, i, c: (k, i, 0)))
    return pl.pallas_call(
        body, name=name, out_shape=_sds((N_CHIPS, rh, cols), BF16), grid_spec=grid_spec,
        compiler_params=_params(("parallel", "parallel")),
    )(core.reshape(1).astype(jnp.int32), g, recv)


def chip_sum(pair, recv, chip, core, name):
    _, rh, cols = pair.shape
    tr = _sum_rows(rh, cols)

    def body(p_ref, own_ref, r_ref, o_ref):
        del p_ref
        acc = own_ref[...].astype(F32)
        for j in range(N_CHIPS - 1):
            acc = acc + r_ref[j].astype(F32)
        o_ref[...] = acc

    grid_spec = pltpu.PrefetchScalarGridSpec(
        num_scalar_prefetch=1, grid=(rh // tr,),
        in_specs=[pl.BlockSpec((None, tr, cols), lambda i, p: (p[0], i, 0)),
                  pl.BlockSpec((N_CHIPS - 1, tr, cols), lambda i, p: (0, i, 0))],
        out_specs=pl.BlockSpec((None, tr, cols), lambda i, p: (p[1], i, 0)))
    return pl.pallas_call(
        body, name=name, out_shape=_sds((2, rh, cols)), grid_spec=grid_spec,
        compiler_params=_params(("parallel",)),
    )(jnp.stack([chip, core]).astype(jnp.int32), pair, recv)


def _place():
    return lax.axis_index("x"), lax.axis_index("y"), lax.axis_index("c")


def _other_chips(x, y):
    return [(x, 1 - y), (1 - x, y), (1 - x, 1 - y)]


_HBM = pl.BlockSpec(memory_space=pltpu.HBM)


def all_gather8(v, name):
    m, n = v.shape

    def body(x_ref, out_ref, send_sems, recv_sems, local_sem):
        x, y, c = _place()
        me, sibling = (x, y, c), (x, y, 1 - c)
        chips = _other_chips(x, y)

        def rows(px, py, pc):
            return out_ref.at[pl.ds((4 * px + 2 * py + pc) * m, m), :]

        def copy(k, block, to, src=None):
            return pltpu.make_async_remote_copy(
                src_ref=rows(*block) if src is None else src, dst_ref=rows(*block),
                send_sem=send_sems.at[k], recv_sem=recv_sems.at[k], device_id=to, device_id_type=MESH)

        mine = pltpu.make_async_copy(x_ref, rows(*me), local_sem)
        mine.start()
        first = [copy(0, me, sibling, src=x_ref)]
        first += [copy(1 + j, me, (*chip, c), src=x_ref) for j, chip in enumerate(chips)]
        for cp in first:
            cp.start()
        passed = [copy(4 + j, (*chip, c), sibling) for j, chip in enumerate(chips)]
        for j, chip in enumerate(chips):
            copy(1 + j, (*chip, c), me).wait_recv()
            passed[j].start()
        copy(0, sibling, me).wait_recv()
        for j, chip in enumerate(chips):
            copy(4 + j, (*chip, 1 - c), me).wait_recv()
        for cp in first + passed:
            cp.wait_send()
        mine.wait()

    return pl.pallas_call(
        body, name=name, out_shape=_sds((N_DEV * m, n), v.dtype),
        in_specs=[pl.BlockSpec(memory_space=pltpu.VMEM)], out_specs=pl.BlockSpec(memory_space=pltpu.VMEM),
        scratch_shapes=[pltpu.SemaphoreType.DMA((7,)), pltpu.SemaphoreType.DMA((7,)), pltpu.SemaphoreType.DMA],
        compiler_params=_params(None, VMEM_BIG),
    )(v)


def _comm_call(body, name, ins, out_shapes, nsem, aliases=None):
    return pl.pallas_call(
        body, name=name, out_shape=tuple(out_shapes), in_specs=[_HBM] * len(ins), out_specs=tuple([_HBM] * len(out_shapes)),
        scratch_shapes=[pltpu.SemaphoreType.DMA((nsem,)), pltpu.SemaphoreType.DMA((nsem,))],
        input_output_aliases=aliases or {},
    )(*ins)


def _remote(src, dst, send_sems, recv_sems, k, to):
    return pltpu.make_async_remote_copy(src_ref=src, dst_ref=dst, send_sem=send_sems.at[k], recv_sem=recv_sems.at[k],
                                        device_id=to, device_id_type=MESH)


def _half(core, rh):
    return pl.ds(pl.multiple_of(core * rh, 16), rh)


def swap_halves(gs, name):
    n = len(gs)

    def body(*refs):
        ins, outs, (send_sems, recv_sems) = refs[:n], refs[n:2 * n], refs[2 * n:]
        x, y, c = _place()
        copies = []
        for i in range(n):
            theirs = _half(1 - c, ins[i].shape[1] // 2)
            cp = _remote(ins[i].at[:, theirs], outs[i], send_sems, recv_sems, i, (x, y, 1 - c))
            cp.start()
            copies.append(cp)
        for cp in copies:
            cp.wait()

    return _comm_call(body, name, gs, [_sds((g.shape[0], g.shape[1] // 2, g.shape[2]), g.dtype) for g in gs], n)


def join_halves(bufs, name):
    n = len(bufs)

    def body(*refs):
        ins, outs, (send_sems, recv_sems) = refs[:n], refs[n:2 * n], refs[2 * n:]
        x, y, c = _place()
        copies = []
        for i in range(n):
            cp = _remote(ins[i].at[c], outs[i].at[c], send_sems, recv_sems, i, (x, y, 1 - c))
            cp.start()
            copies.append(cp)
        for i in range(n):
            theirs = outs[i].at[1 - c]
            _remote(theirs, theirs, send_sems, recv_sems, i, (x, y, 1 - c)).wait_recv()
        for cp in copies:
            cp.wait_send()

    return _comm_call(body, name, bufs, [_sds(b.shape, b.dtype) for b in bufs], n, {i: i for i in range(n)})


def forward_halves(lands, name):
    n = len(lands)

    def body(*refs):
        ins, outs, (send_sems, recv_sems) = refs[:n], refs[n:2 * n], refs[2 * n:]
        x, y, c = _place()
        sibling = (x, y, 1 - c)
        chips = _other_chips(x, y)
        copies = []
        for i in range(n):
            mine = _half(c, ins[i].shape[1] // 2)
            for j, (px, py) in enumerate(chips):
                cp = _remote(ins[i].at[2 * px + py, mine], outs[i].at[2 * px + py, mine], send_sems, recv_sems, 3 * i + j, sibling)
                cp.start()
                copies.append(cp)
        for i in range(n):
            theirs = _half(1 - c, ins[i].shape[1] // 2)
            for j, (px, py) in enumerate(chips):
                landed = outs[i].at[2 * px + py, theirs]
                _remote(landed, landed, send_sems, recv_sems, 3 * i + j, sibling).wait_recv()
        for cp in copies:
            cp.wait_send()

    return _comm_call(body, name, lands, [_sds(b.shape, b.dtype) for b in lands], 3 * n, {i: i for i in range(n)})


_SEM = pl.BlockSpec(memory_space=pltpu.SEMAPHORE)
_EFFECT = pltpu.SideEffectType.DATAFLOW_SIDE_EFFECTING


def _gather_copies(srcs, lands, send_sems, recv_sems):
    x, y, c = _place()
    copies = []
    for i in range(len(srcs)):
        mine = _half(c, srcs[i].shape[0] // 2)
        for j, chip in enumerate(_other_chips(x, y)):
            copies.append(_remote(srcs[i].at[mine], lands[i].at[2 * x + y, mine], send_sems, recv_sems, 3 * i + j, (*chip, c)))
    return copies


def _exchange_copies(srcs, lands, send_sems, recv_sems):
    x, y, c = _place()
    copies = []
    for i in range(len(srcs)):
        for j, (px, py) in enumerate(_other_chips(x, y)):
            copies.append(_remote(srcs[i].at[2 * px + py], lands[i].at[j], send_sems, recv_sems, 3 * i + j, (px, py, c)))
    return copies


def _everyone_copies(srcs, lands, send_sems, recv_sems):
    x, y, c = _place()
    flip = lambda v, b: 1 - v if b else v
    dst = lands[0].at[4 * x + 2 * y + c]
    return [_remote(srcs[0], dst, send_sems, recv_sems, j - 1, (flip(x, j & 4), flip(y, j & 2), flip(c, j & 1)))
            for j in range(1, N_DEV)]


GATHER = (_gather_copies, 3)
EXCHANGE = (_exchange_copies, 3)
EVERYONE = (_everyone_copies, N_DEV - 1)


def split_start(name, plan, srcs, land_shapes, after=()):
    copies_fn, per_source = plan
    n, m, k = len(srcs), len(land_shapes), len(after)
    ncopies = per_source * n

    def body(*refs):
        src_refs, land_refs = refs[:n], refs[n:n + m]
        send_sems, recv_sems = refs[n + m + k], refs[n + m + k + 1]
        token = refs[-1]
        for cp in copies_fn(src_refs, land_refs, send_sems, recv_sems):
            cp.start()
        token[...] = jnp.zeros_like(token)

    hbm = lambda s: pltpu.HBM(tuple(s.shape), s.dtype)
    outs = pl.pallas_call(
        body, name=name,
        out_shape=(pltpu.SemaphoreType.DMA((ncopies,)), pltpu.SemaphoreType.DMA((ncopies,)), *[hbm(s) for s in srcs],
                   *[hbm(s) for s in land_shapes], _sds((8, LANES))),
        in_specs=[_HBM] * (n + m) + [pl.BlockSpec(memory_space=pl.ANY)] * k,
        out_specs=(_SEM, _SEM, *([_HBM] * (n + m)), pl.BlockSpec(memory_space=pltpu.VMEM)),
        input_output_aliases={i: 2 + i for i in range(n + m)},
        compiler_params=pltpu.CompilerParams(has_side_effects=_EFFECT),
    )(*[pltpu.with_memory_space_constraint(s, pltpu.HBM) for s in srcs],
      *[pltpu.with_memory_space_constraint(lax.empty(tuple(s.shape), s.dtype), pltpu.HBM) for s in land_shapes], *after)
    handle = (outs[0], outs[1], list(outs[2:2 + n]), list(outs[2 + n:2 + n + m]))
    return handle, outs[-1][0, 0]


def split_wait(name, plan, handle, after):
    copies_fn, _ = plan
    send_sems, recv_sems, srcs, lands = handle
    n, m = len(srcs), len(lands)
    after = list(after) if isinstance(after, (list, tuple)) else [after]

    def body(*refs):
        src_refs, land_refs = refs[:n], refs[n:n + m]
        for cp in copies_fn(src_refs, land_refs, refs[n + m], refs[n + m + 1]):
            cp.wait_send()
            cp.wait_recv()

    hbm = lambda s: pltpu.HBM(tuple(s.shape), s.dtype)
    outs = pl.pallas_call(
        body, name=name, out_shape=tuple(hbm(s) for s in srcs + lands),
        in_specs=[_HBM] * (n + m) + [_SEM, _SEM] + [pl.BlockSpec(memory_space=pl.ANY)] * len(after),
        out_specs=tuple([_HBM] * (n + m)), input_output_aliases={i: i for i in range(n + m)},
        compiler_params=pltpu.CompilerParams(has_side_effects=_EFFECT),
    )(*srcs, *lands, send_sems, recv_sems, *after)
    return list(outs[:n]), list(outs[n:])


_CD_PAD = C_Q_RANK + C_KV_RANK + HEAD_PAD + 2 * _DW


def chip_major(w, groups=N_CHIPS):
    r, c = w.shape
    return w.reshape(r, groups, c // groups).transpose(1, 0, 2)


def from_chip_major(w):
    g, r, c = w.shape
    return w.transpose(1, 0, 2).reshape(r, g * c)


def _cd_in_pad(w):
    a = C_Q_RANK + C_KV_RANK
    z = lambda n: jnp.zeros((w.shape[0], n), w.dtype)
    return jnp.concatenate([w[:, :a], z(C_NOPE), w[:, a:a + C_ROPE], z(HEAD_PAD - C_NOPE - C_ROPE), w[:, a + C_ROPE:]], axis=1)


def _cd_in_unpad(w):
    a = C_Q_RANK + C_KV_RANK
    return jnp.concatenate([w[:, :a], w[:, a + C_NOPE:a + C_NOPE + C_ROPE], w[:, a + HEAD_PAD:]], axis=1)


def _pad_heads(w, width):
    r = w.shape[0]
    w = w.reshape(r, C_HEADS, width)
    return jnp.pad(w, ((0, 0), (0, 0), (0, HEAD_PAD - width))).reshape(r, _HW)


def _unpad_heads(w, width):
    r = w.shape[0]
    return w.reshape(r, C_HEADS, HEAD_PAD)[:, :, :width].reshape(r, C_HEADS * width)


_MATMUL_WEIGHTS = ("ab_w_in", "ab_w_out", "cd_w_in", "c_w_uq", "c_w_ukv", "cd_w_out", "ffn_w_up", "ffn_w_down")
_LAYER_STACKED = ("norm1_g", "norm2_g", "ffn_w_up", "ffn_conv_w", "ffn_w_down")
_ROW_VECTORS = ("b_scale", "c_q_norm_g", "c_kv_norm_g", "d_ln_g", "d_ln_b")


def full_to_local(p):
    q = {}
    for k, v in p.items():
        if k == "final_norm_g":
            v = v.reshape(1, -1)
        elif k not in _LAYER_STACKED and k not in _ROW_VECTORS:
            v = v[0]
        q[k] = v.astype(BF16) if k in _MATMUL_WEIGHTS else v
    return q


def local_to_full(g):
    q = {}
    for k, v in g.items():
        if k == "final_norm_g":
            q[k] = v.reshape(-1)
        elif k not in _LAYER_STACKED and k not in _ROW_VECTORS:
            q[k] = v[None]
        else:
            q[k] = v
    return q


def prepare_weights(p):
    q = dict(p)
    q["cd_w_in"] = _cd_in_pad(p["cd_w_in"])
    q["c_w_uq"] = _pad_heads(p["c_w_uq"], C_NOPE + C_ROPE)
    ukv = p["c_w_ukv"].reshape(C_KV_RANK, C_HEADS, C_NOPE + C_V)
    q["c_w_uk"] = _pad_heads(ukv[:, :, :C_NOPE].reshape(C_KV_RANK, -1), C_NOPE)
    q["c_w_uv"] = _pad_heads(ukv[:, :, C_NOPE:].reshape(C_KV_RANK, -1), C_V)
    wo = p["cd_w_out"]
    att_rows = jnp.pad(wo[:C_HEADS * C_V].reshape(C_HEADS, C_V, D_MODEL), ((0, 0), (0, HEAD_PAD - C_V), (0, 0)))
    q["cd_w_out"] = jnp.concatenate([att_rows.reshape(_HW, D_MODEL), wo[C_HEADS * C_V:]], axis=0)
    return q


def unprepare_grads(g):
    q = dict(g)
    q["cd_w_in"] = _cd_in_unpad(g["cd_w_in"])
    q["c_w_uq"] = _unpad_heads(g["c_w_uq"], C_NOPE + C_ROPE)
    uk = g.pop("c_w_uk").reshape(C_KV_RANK, C_HEADS, HEAD_PAD)[:, :, :C_NOPE]
    uv = g.pop("c_w_uv").reshape(C_KV_RANK, C_HEADS, HEAD_PAD)[:, :, :C_V]
    q.pop("c_w_uk", None)
    q.pop("c_w_uv", None)
    q["c_w_ukv"] = jnp.concatenate([uk, uv], axis=-1).reshape(C_KV_RANK, C_HEADS * (C_NOPE + C_V))
    wo = g["cd_w_out"]
    att = wo[:_HW].reshape(C_HEADS, HEAD_PAD, D_MODEL)[:, :C_V].reshape(C_HEADS * C_V, D_MODEL)
    q["cd_w_out"] = jnp.concatenate([att, wo[_HW:]], axis=0)
    return q


def rope_tables(positions):
    half = C_ROPE // 2
    inv_freq = ROPE_THETA ** (-jnp.arange(half, dtype=F32) / half)
    ang = positions.astype(F32)[:, None] * inv_freq
    cos, sin = jnp.cos(ang), jnp.sin(ang)
    s = positions.shape[0]
    z = lambda n: jnp.zeros((s, n), F32)
    cs = jnp.concatenate([jnp.ones((s, C_NOPE), F32), cos, cos, z(HEAD_PAD - C_NOPE - C_ROPE)], axis=1)
    s1 = jnp.concatenate([z(C_NOPE), -sin, z(HEAD_PAD - C_NOPE - half)], axis=1)
    s2 = jnp.concatenate([z(C_NOPE + half), sin, z(HEAD_PAD - C_NOPE - C_ROPE)], axis=1)
    return cs, s1, s2


def _mods(mod_l):
    return [mod_l[:, i * D_MODEL:(i + 1) * D_MODEL] for i in range(N_MOD)]


def _ffn_fwd(x1, w, l, sc2, sh2, g2, late_down=None):
    n2 = w["norm2_g"][l:l + 1]
    h2 = modnorm_fwd(x1, n2, sc2, sh2, f"modnorm2_fwd{l}")
    up_cols = 2 * D_FF // N_CHIPS
    zf = matmul(h2, w["ffn_w_up"][l], "nn", BF16, f"ffn_up{l}", gb=N_CHIPS, go=2, tn=up_cols)
    a = ffn_act_fwd(zf, w["ffn_conv_w"][l], f"ffn_act_fwd{l}")
    if late_down is not None:
        late_down(a)
    f = matmul(a, w["ffn_w_down"][l], "nn", F32, f"ffn_down{l}", tk=D_FF)
    x2 = resid_fwd(x1, f, g2, f"resid2_fwd{l}")
    return x2, (h2, zf, a, f)


def _ffn_bwd(dres, x1, saved, w, l, sc2, g2):
    h2, zf, a, f = saved
    n2 = w["norm2_g"][l:l + 1]
    df, dg2 = gate_bwd(dres, f, g2, f"gate2_bwd{l}")
    up_cols = 2 * D_FF // N_CHIPS
    da = matmul(df, w["ffn_w_down"][l], "nt", BF16, f"ffn_down_dx{l}")
    d_down = matmul(a, df, "tn", BF16, f"ffn_down_dw{l}", tm=D_FF // 2)
    dzf, d_conv = ffn_act_bwd(zf, w["ffn_conv_w"][l], da, f"ffn_act_bwd{l}")
    dh2 = matmul(dzf, w["ffn_w_up"][l], "nt", F32, f"ffn_up_dx{l}", ga=2, gb=N_CHIPS, tk=up_cols)
    d_up = matmul(h2, dzf, "tn", BF16, f"ffn_up_dw{l}", gb=2, go=N_CHIPS, tn=up_cols)
    dres, dsh2, dsc2, dn2 = norm_bwd(x1, dh2, n2, sc2, dres, f"norm2_bwd{l}")
    d_conv = d_conv.transpose(1, 0, 2).reshape(3, 2 * D_FF)
    return dres, dict(ffn_w_down=d_down, ffn_conv_w=d_conv, ffn_w_up=d_up, norm2_g=dn2), (dsh2, dsc2, dg2)


def mixer0_fwd(x0, sh1, sc1, g1, w):
    h = modnorm_fwd(x0, w["norm1_g"][0:1], sc1, sh1, "modnorm1_fwd0")
    z = matmul(h, w["ab_w_in"], "nn", BF16, "ab_in", gb=N_CHIPS)
    ya = gconv_fwd(z, w["a_conv_w"])
    yb = pool_fwd(z, w["b_mix_w"], w["b_scale"])
    ycat = jnp.concatenate([ya] + yb, axis=1)
    y = matmul(ycat, w["ab_w_out"], "nn", F32, "ab_out")
    x1 = resid_fwd(x0, y, g1, "resid1_fwd0")
    return x1, (x0, h, z, ycat, y, sc1, g1)


def mixer0_bwd(dres, saved, w):
    x0, h, z, ycat, y, sc1, g1 = saved
    grads = {}
    dy, dg1 = gate_bwd(dres, y, g1, "gate1_bwd0")
    dycat = matmul(dy, w["ab_w_out"], "nt", BF16, "ab_out_dx")
    grads["ab_w_out"] = matmul(ycat, dy, "tn", BF16, "ab_out_dw")
    db, dc, da, d_conv = gconv_bwd(z, w["a_conv_w"], dycat)
    pb = pool_bwd(z, w["b_mix_w"], w["b_scale"], dycat)
    dz = jnp.concatenate([db, dc, da] + [t[0] for t in pb], axis=1)
    dh = matmul(dz, w["ab_w_in"], "nt", F32, "ab_in_dx", gb=N_CHIPS)
    grads["ab_w_in"] = matmul(h, dz, "tn", BF16, "ab_in_dw", go=N_CHIPS)
    dres, dsh1, dsc1, dn1 = norm_bwd(x0, dh, w["norm1_g"][0:1], sc1, dres, "norm1_bwd0")
    grads.update(a_conv_w=d_conv, b_mix_w=jnp.stack([t[1] for t in pb]),
                 b_scale=jnp.concatenate([t[2] for t in pb], axis=1), norm1_g=dn1)
    return dres, grads, (dsh1, dsc1, dg1)


def mixer1_fwd(x0, sh1, sc1, g1, ropes, w):
    cs, s1, s2 = ropes
    h = modnorm_fwd(x0, w["norm1_g"][1:2], sc1, sh1, "modnorm1_fwd1")
    z = matmul(h, w["cd_w_in"], "nn", BF16, "cd_in")
    bs_t = jnp.pad(w["d_b_s"].T, ((0, 0), (0, LANES - D_GROUPS)))
    qh, kh, vh = mla_pre_fwd(z, w["c_q_norm_g"], w["c_kv_norm_g"], w["c_w_uq"], w["c_w_uk"], w["c_w_uv"], cs, s1, s2)
    oh = attn_fwd(qh, kh, vh)
    yd = sgu_fwd(z, w["d_ln_g"], w["d_ln_b"], w["d_w_s"], bs_t)
    ycat = jnp.concatenate([oh, yd], axis=1)
    y = matmul(ycat, w["cd_w_out"], "nn", F32, "cd_out")
    x1 = resid_fwd(x0, y, g1, "resid1_fwd1")
    return x1, (x0, h, z, bs_t, qh, kh, vh, oh, ycat, y, sc1, g1)


def mixer1_bwd(dres, saved, ropes, w):
    cs, s1, s2 = ropes
    x0, h, z, bs_t, qh, kh, vh, oh, ycat, y, sc1, g1 = saved
    grads = {}
    dy, dg1 = gate_bwd(dres, y, g1, "gate1_bwd1")
    dycat = matmul(dy, w["cd_w_out"], "nt", BF16, "cd_out_dx")
    grads["cd_w_out"] = matmul(ycat, dy, "tn", F32, "cd_out_dw")
    dqh, dkh, dvh = attn_bwd(qh, kh, vh, oh, dycat, 0)
    dzq, d_uq, d_uk, d_uv, d_gq, d_gkv = mla_pre_bwd(
        z, w["c_q_norm_g"], w["c_kv_norm_g"], w["c_w_uq"], w["c_w_uk"], w["c_w_uv"], cs, s1, s2, dqh, dkh, dvh)
    dzu, dzv, d_ws, d_bs, d_lg, d_lb = sgu_bwd(z, w["d_ln_g"], w["d_ln_b"], w["d_w_s"], bs_t, dycat, _HW // _DW)
    dz = jnp.concatenate([dzq, dzu, dzv], axis=1)
    dh = matmul(dz, w["cd_w_in"], "nt", F32, "cd_in_dx")
    grads["cd_w_in"] = matmul(h, dz, "tn", F32, "cd_in_dw")
    dres, dsh1, dsc1, dn1 = norm_bwd(x0, dh, w["norm1_g"][1:2], sc1, dres, "norm1_bwd1")
    grads.update(c_w_uq=d_uq, c_w_uk=d_uk, c_w_uv=d_uv, c_q_norm_g=d_gq, c_kv_norm_g=d_gkv, d_w_s=d_ws,
                 d_b_s=d_bs[:, :D_GROUPS].T, d_ln_g=d_lg, d_ln_b=d_lb, norm1_g=dn1)
    return dres, grads, (dsh1, dsc1, dg1)


_PER_LAYER = ("ffn_w_down", "ffn_conv_w", "ffn_w_up", "norm2_g", "norm1_g")


def _merge_layer_grads(g0, g1):
    grads = {k: v for k, v in g0.items() if k not in _PER_LAYER}
    grads.update({k: v for k, v in g1.items() if k not in _PER_LAYER})
    for k in ("ffn_w_down", "ffn_w_up"):
        grads[k] = [g0[k], g1[k]]
    grads["ffn_conv_w"] = jnp.stack([g0["ffn_conv_w"], g1["ffn_conv_w"]])
    grads["norm1_g"] = jnp.concatenate([g0["norm1_g"], g1["norm1_g"]], axis=0)
    grads["norm2_g"] = jnp.concatenate([g0["norm2_g"], g1["norm2_g"]], axis=0)
    return grads


def local_step(x, tgt, mod, ropes, w):
    sh1a, sc1a, g1a, sh2a, sc2a, g2a = _mods(mod[0:1])
    sh1b, sc1b, g1b, sh2b, sc2b, g2b = _mods(mod[1:2])
    x1, mix0 = mixer0_fwd(x, sh1a, sc1a, g1a, w)
    x2, ffn0 = _ffn_fwd(x1, w, 0, sc2a, sh2a, g2a)
    x3, mix1 = mixer1_fwd(x2, sh1b, sc1b, g1b, ropes, w)
    x4, ffn1 = _ffn_fwd(x3, w, 1, sc2b, sh2b, g2b)
    dres, d_final, loss = final_fwd_bwd(x4, w["final_norm_g"], tgt)
    dres, gf1, dm2b = _ffn_bwd(dres, x3, ffn1, w, 1, sc2b, g2b)
    dres, gm1, dm1b = mixer1_bwd(dres, mix1, ropes, w)
    dres, gf0, dm2a = _ffn_bwd(dres, x1, ffn0, w, 0, sc2a, g2a)
    dres, gm0, dm1a = mixer0_bwd(dres, mix0, w)
    grads = _merge_layer_grads({**gf0, **gm0}, {**gf1, **gm1})
    grads["final_norm_g"] = d_final
    dmod = jnp.concatenate([jnp.concatenate(dm1a + dm2a, axis=1), jnp.concatenate(dm1b + dm2b, axis=1)], axis=0)
    return loss, dres, dmod, grads


_WEIGHTS = ("ada_w", "ada_b", "norm1_g", "norm2_g", "ab_w_in", "a_conv_w", "b_mix_w", "b_scale", "ab_w_out", "cd_w_in",
            "c_q_norm_g", "c_w_uq", "c_kv_norm_g", "c_w_ukv", "d_ln_g", "d_ln_b", "d_w_s", "d_b_s", "cd_w_out",
            "ffn_w_up", "ffn_conv_w", "ffn_w_down", "final_norm_g")
_INPUTS = ("x", "c", "positions") + _WEIGHTS + ("loss_target",) + tuple("m_" + n for n in _WEIGHTS) + tuple(
    "v_" + n for n in _WEIGHTS)

def _pack_rows(parts, rows, dtype):
    flat = jnp.concatenate([p.reshape(-1).astype(dtype) for p in parts])
    return jnp.pad(flat, (0, rows * LANES - flat.shape[0])).reshape(rows, LANES)


def _rows_major(w):
    r, c = w.shape
    return w.reshape(N_CHIPS, r // N_CHIPS, c)


def start_gather(shards, tag, after=()):
    lands = [_sds((N_CHIPS,) + s.shape, s.dtype) for s in shards]
    return split_start("gather_start_" + tag, GATHER, shards, lands, after)


def finish_gather(handle, chip, tag, after):
    shards, lands = split_wait("gather_wait_" + tag, GATHER, handle, after)
    lands = forward_halves(lands, "gather_forward_" + tag)
    return [lax.dynamic_update_index_in_dim(o, s, chip, 0) for o, s in zip(lands, shards)]


def start_reduce(gs, core, tag):
    recv = swap_halves(gs, "swap_halves_" + tag)
    pairs = [pair_sum(g, r, core, f"pair_sum_{tag}{i}") for i, (g, r) in enumerate(zip(gs, recv))]
    lands = [_sds((N_CHIPS - 1,) + p.shape[1:], p.dtype) for p in pairs]
    return split_start("exchange_start_" + tag, EXCHANGE, pairs, lands)


def finish_reduce(handle, chip, core, tag, after):
    pairs, others = split_wait("exchange_wait_" + tag, EXCHANGE, handle, after)
    halves = [chip_sum(p, o, chip, core, f"chip_sum_{tag}{i}") for i, (p, o) in enumerate(zip(pairs, others))]
    full = join_halves(halves, "join_halves_" + tag)
    return [f.reshape(f.shape[1] * 2, f.shape[2]) for f in full]


_SMALL_SHARDED = (("a_conv_w", (3, 128), 1), ("c_q_norm_g", (1, 64), 1), ("d_ln_g", (1, 128), 1), ("d_ln_b", (1, 128), 1),
                  ("ffn_conv_w", (2, 3, 2 * D_FF // N_CHIPS), 2))
_SMALL_GRADS = (("norm1_g", (2, D_MODEL)), ("norm2_g", (2, D_MODEL)), ("b_mix_w", (4, 128, 128)), ("b_scale", (1, 512)),
                ("c_kv_norm_g", (1, 128)), ("d_w_s", (4, 128, 128)), ("d_b_s", (4, 128)), ("final_norm_g", (1, D_MODEL)),
                ("a_conv_w", (3, 512)), ("c_q_norm_g", (1, 256)), ("d_ln_g", (1, 512)), ("d_ln_b", (1, 512)),
                ("ffn_conv_w", (2, 3, 2 * D_FF)))


def _size(shape):
    n = 1
    for d in shape:
        n *= d
    return n


def kernel(x, c, positions, ada_w, ada_b, norm1_g, norm2_g, ab_w_in, a_conv_w, b_mix_w, b_scale, ab_w_out, cd_w_in, c_q_norm_g, c_w_uq, c_kv_norm_g, c_w_ukv, d_ln_g, d_ln_b, d_w_s, d_b_s, cd_w_out, ffn_w_up, ffn_conv_w, ffn_w_down, final_norm_g, loss_target, m_ada_w, m_ada_b, m_norm1_g, m_norm2_g, m_ab_w_in, m_a_conv_w, m_b_mix_w, m_b_scale, m_ab_w_out, m_cd_w_in, m_c_q_norm_g, m_c_w_uq, m_c_kv_norm_g, m_c_w_ukv, m_d_ln_g, m_d_ln_b, m_d_w_s, m_d_b_s, m_cd_w_out, m_ffn_w_up, m_ffn_conv_w, m_ffn_w_down, m_final_norm_g, v_ada_w, v_ada_b, v_norm1_g, v_norm2_g, v_ab_w_in, v_a_conv_w, v_b_mix_w, v_b_scale, v_ab_w_out, v_cd_w_in, v_c_q_norm_g, v_c_w_uq, v_c_kv_norm_g, v_c_w_ukv, v_d_ln_g, v_d_ln_b, v_d_w_s, v_d_b_s, v_cd_w_out, v_ffn_w_up, v_ffn_conv_w, v_ffn_w_down, v_final_norm_g):
    args = (x, c, positions, ada_w, ada_b, norm1_g, norm2_g, ab_w_in, a_conv_w, b_mix_w, b_scale, ab_w_out, cd_w_in, c_q_norm_g, c_w_uq, c_kv_norm_g, c_w_ukv, d_ln_g, d_ln_b, d_w_s, d_b_s, cd_w_out, ffn_w_up, ffn_conv_w, ffn_w_down, final_norm_g, loss_target, m_ada_w, m_ada_b, m_norm1_g, m_norm2_g, m_ab_w_in, m_a_conv_w, m_b_mix_w, m_b_scale, m_ab_w_out, m_cd_w_in, m_c_q_norm_g, m_c_w_uq, m_c_kv_norm_g, m_c_w_ukv, m_d_ln_g, m_d_ln_b, m_d_w_s, m_d_b_s, m_cd_w_out, m_ffn_w_up, m_ffn_conv_w, m_ffn_w_down, m_final_norm_g, v_ada_w, v_ada_b, v_norm1_g, v_norm2_g, v_ab_w_in, v_a_conv_w, v_b_mix_w, v_b_scale, v_ab_w_out, v_cd_w_in, v_c_q_norm_g, v_c_w_uq, v_c_kv_norm_g, v_c_w_ukv, v_d_ln_g, v_d_ln_b, v_d_w_s, v_d_b_s, v_cd_w_out, v_ffn_w_up, v_ffn_conv_w, v_ffn_w_down, v_final_norm_g)
    a = dict(zip(_INPUTS, args, strict=True))
    xi, yi, ci = _place()
    chip = 2 * xi + yi
    dev = 4 * xi + 2 * yi + ci
    x = a["x"][0]
    tgt = a["loss_target"][0]

    bf = lambda t: t.astype(BF16)
    mix0_handle, tok = start_gather([bf(a["ab_w_in"][0]), bf(a["ab_w_out"][0])], "mix0")

    small_parts = [a["c"] + tok] + [a[n] for n, _, _ in _SMALL_SHARDED]
    rows1 = -(-sum(p.size for p in small_parts) // LANES // 8) * 8
    g1 = all_gather8(_pack_rows(small_parts, rows1, F32), "gather_small").reshape(N_DEV, rows1 * LANES)
    c_all = g1[:, :D_MODEL]
    per_chip = g1[0::2]
    small_full = {}
    off = D_MODEL
    for n, shp, axis in _SMALL_SHARDED:
        piece = per_chip[:, off:off + _size(shp)].reshape((N_CHIPS,) + shp)
        small_full[n] = jnp.concatenate([piece[k] for k in range(N_CHIPS)], axis=axis)
        off += _size(shp)

    merge = lambda t: t.reshape(t.shape[0] * t.shape[1], t.shape[2])
    w = dict(norm1_g=a["norm1_g"], norm2_g=a["norm2_g"], b_mix_w=a["b_mix_w"][0], b_scale=a["b_scale"],
             c_kv_norm_g=a["c_kv_norm_g"], d_w_s=a["d_w_s"][0], d_b_s=a["d_b_s"][0],
             final_norm_g=a["final_norm_g"].reshape(1, D_MODEL), **small_full)

    ncol = N_MOD * D_MODEL // N_CHIPS
    ada_b_mine = lax.dynamic_slice_in_dim(a["ada_b"], chip * ncol, ncol, axis=1)
    mod_cols = ada_mod(c_all, a["ada_w"], ada_b_mine)
    g2_rows = all_gather8(mod_cols.reshape(-1, LANES), "gather_mod")
    g2 = g2_rows.reshape(N_DEV, 2, N_DEV, ncol)
    mod = lax.dynamic_index_in_dim(g2[0::2], dev, axis=2, keepdims=False)
    mod = mod.transpose(1, 0, 2).reshape(2, N_MOD * D_MODEL)

    late = [g2_rows]
    up0_handle, tok_a = start_gather([bf(a["ffn_w_up"][0])], "up0", late)
    down0_handle, tok_b = start_gather([bf(a["ffn_w_down"][0])], "down0", late)
    mix1_handle, tok_c = start_gather(
        [bf(a["cd_w_in"][0]), bf(a["c_w_uq"][0]), bf(a["c_w_ukv"][0]), bf(a["cd_w_out"][0])], "mix1", late)
    ffn1_handle, tok_d = start_gather([bf(a["ffn_w_up"][1]), bf(a["ffn_w_down"][1])], "ffn1", late)
    mod = mod + (tok_a + tok_b + tok_c + tok_d)

    ropes = rope_tables(a["positions"][0])
    cm16 = lambda t: chip_major(t).astype(BF16)
    sh1a, sc1a, g1a, sh2a, sc2a, g2a = _mods(mod[0:1])
    sh1b, sc1b, g1b, sh2b, sc2b, g2b = _mods(mod[1:2])

    w_in0, w_out0 = finish_gather(mix0_handle, chip, "mix0", mod)
    w.update(ab_w_in=w_in0, ab_w_out=merge(w_out0))
    x1, mix0 = mixer0_fwd(x, sh1a, sc1a, g1a, w)
    up0, = finish_gather(up0_handle, chip, "up0", x1)
    w.update(ffn_w_up=[up0, None], ffn_w_down=[None, None])

    def fetch_down0(act):
        down0, = finish_gather(down0_handle, chip, "down0", act)
        w["ffn_w_down"][0] = merge(down0)

    x2, ffn0 = _ffn_fwd(x1, w, 0, sc2a, sh2a, g2a, late_down=fetch_down0)
    cd_in, uq, ukv, cd_out = finish_gather(mix1_handle, chip, "mix1", x2)
    w.update(prepare_weights(dict(cd_w_in=from_chip_major(cd_in), c_w_uq=from_chip_major(uq), c_w_ukv=from_chip_major(ukv),
                                  cd_w_out=merge(cd_out))))
    x3, mix1 = mixer1_fwd(x2, sh1b, sc1b, g1b, ropes, w)
    up1, down1 = finish_gather(ffn1_handle, chip, "ffn1", x3)
    w.update(ffn_w_up=[up0, up1], ffn_w_down=[w["ffn_w_down"][0], merge(down1)])
    x4, ffn1 = _ffn_fwd(x3, w, 1, sc2b, sh2b, g2b)
    dres, d_final, loss = final_fwd_bwd(x4, w["final_norm_g"], tgt)

    dres, gf1, dm2b = _ffn_bwd(dres, x3, ffn1, w, 1, sc2b, g2b)
    ffn1_red, tok = start_reduce([gf1["ffn_w_up"], _rows_major(gf1["ffn_w_down"])], ci, "ffn1")
    dres, gm1, dm1b = mixer1_bwd(dres, mix1[:-1] + (mix1[-1] + tok,), ropes, w)
    gm1 = unprepare_grads(gm1)
    mix1_red, tok = start_reduce([cm16(gm1["cd_w_in"]), cm16(gm1["c_w_uq"]), cm16(gm1["c_w_ukv"]),
                                  _rows_major(gm1["cd_w_out"]).astype(BF16)], ci, "mix1")
    red_up1, red_down1 = finish_reduce(ffn1_red, chip, ci, "ffn1", dres)
    dres, gf0, dm2a = _ffn_bwd(dres, x1, ffn0, w, 0, sc2a, g2a + tok)
    ffn0_red, tok = start_reduce([gf0["ffn_w_up"], _rows_major(gf0["ffn_w_down"])], ci, "ffn0")
    red_cd_in, red_uq, red_ukv, red_cd_out = finish_reduce(mix1_red, chip, ci, "mix1", dres)
    grad_x, gm0, dm1a = mixer0_bwd(dres, mix0[:-1] + (mix0[-1] + tok,), w)
    grads = _merge_layer_grads({**gf0, **gm0}, {**gf1, **gm1})
    grads["final_norm_g"] = d_final
    dmod = jnp.concatenate([jnp.concatenate(dm1a + dm2a, axis=1), jnp.concatenate(dm1b + dm2b, axis=1)], axis=0)

    parts3 = [dmod] + [grads[n] for n, _ in _SMALL_GRADS] + [loss[0, 0]]
    rows3 = -(-sum(p.size for p in parts3) // LANES // 8) * 8
    small_handle, tok = split_start("small_grads_start", EVERYONE, [_pack_rows(parts3, rows3, F32)],
                                    [_sds((N_DEV, rows3, LANES))])
    mix0_red, _ = start_reduce([gm0["ab_w_in"], _rows_major(gm0["ab_w_out"]) + tok.astype(BF16)], ci, "mix0")
    red_up0, red_down0 = finish_reduce(ffn0_red, chip, ci, "ffn0", grad_x)
    out_grads = dict(cd_w_in=red_cd_in, c_w_uq=red_uq, c_w_ukv=red_ukv, cd_w_out=red_cd_out)
    per_layer = dict(ffn_w_up=(red_up0, red_up1), ffn_w_down=(red_down0, red_down1))
    updates = {}

    def update(n):
        if n in per_layer:
            updates[n] = adamw_layers(a[n], *per_layer[n], a["m_" + n], a["v_" + n], "adamw_" + n)
        else:
            updates[n] = adamw(a[n], out_grads[n].reshape(a[n].shape), a["m_" + n], a["v_" + n], "adamw_" + n,
                               copy_grad=n != "ada_w")

    early =("ffn_w_up", "ffn_w_down", "cd_w_in", "c_w_uq", "c_w_ukv", "cd_w_out")
    for n in early:
        update(n)
    (mine,), (landed,) = split_wait("small_grads_wait", EVERYONE, small_handle, [updates[n][1] for n in early])
    g3 = lax.dynamic_update_index_in_dim(landed, mine, dev, 0)
    summed = sum8(g3).reshape(-1)
    nmod = 2 * N_MOD * D_MODEL
    out_grads["ada_b"] = summed[:nmod].reshape(2, N_MOD * D_MODEL)
    off = nmod
    for n, shp in _SMALL_GRADS:
        out_grads[n] = summed[off:off + _size(shp)].reshape(shp)
        off += _size(shp)
    loss = summed[off]
    for n, shp, axis in _SMALL_SHARDED:
        width = out_grads[n].shape[-1] // N_CHIPS
        out_grads[n] = lax.dynamic_slice_in_dim(out_grads[n], chip * width, width, axis=out_grads[n].ndim - 1)
    dmod_all = g3.reshape(N_DEV, rows3 * LANES)[:, :nmod].reshape(N_DEV, 2, N_MOD * D_MODEL)
    dmod_mine = lax.dynamic_slice_in_dim(dmod_all, chip * ncol, ncol, axis=2).transpose(1, 0, 2)
    out_grads["ada_w"] = ada_grad(c_all.T, dmod_mine)

    red_in0, red_out0 = finish_reduce(mix0_red, chip, ci, "mix0", out_grads["ada_w"])
    out_grads.update(ab_w_in=red_in0, ab_w_out=red_out0)

    for n in ("ada_w", "ab_w_in", "ab_w_out"):
        update(n)
    small = [n for n in _WEIGHTS if n not in updates]
    for n, res in zip(small, adamw_small([a[n] for n in small], [out_grads[n].reshape(a[n].shape) for n in small],
                                         [a["m_" + n] for n in small], [a["v_" + n] for n in small])):
        updates[n] = res
    return (loss, grad_x[None], *[updates[n][i] for i in range(4) for n in _WEIGHTS])
```

```python
import functools

import jax
import jax.numpy as jnp
from jax import lax
from jax.experimental import pallas as pl
from jax.experimental.pallas import tpu as pltpu

F32 = jnp.float32
BF16 = jnp.bfloat16
EPS = 1e-6
D_MODEL = 1024
N_MOD = 6
A_WIDTH = 512
B_GROUPS = 4
POOL_WINDOWS = (2, 4, 8, 16)
C_HEADS = 8
C_NOPE = 64
C_ROPE = 32
C_V = 64
C_Q_RANK = 256
C_KV_RANK = 128
HEAD_PAD = 128
ROPE_THETA = 10000.0
D_GROUPS = 4
D_CHUNK = 128
D_FF = 2816
FF_UNIT = 128
ADAM_LR = 0.001
ADAM_B1 = 0.9
ADAM_B2 = 0.999
ADAM_EPS = 1e-08
ADAM_WD = 0.01
ADAM_STEP = 10
N_CHIPS = 4
N_DEV = 8
LANES = 128
VMEM_BIG = 56 * 1024 * 1024
MESH = pl.DeviceIdType.MESH


def _sds(shape, dtype=F32):
    return jax.ShapeDtypeStruct(tuple(shape), dtype)


def _tile(n, cap, mult=128):
    if n <= cap:
        return n
    best = None
    for t in range(mult, cap + 1, mult):
        if n % t == 0:
            best = t
    assert best is not None, (n, cap, mult)
    return best


def _params(dims=None, vmem=None):
    return pltpu.CompilerParams(dimension_semantics=dims, vmem_limit_bytes=vmem)


def _shift_down(v, k):
    r = pltpu.roll(v, k, axis=0)
    t = lax.broadcasted_iota(jnp.int32, v.shape, 0)
    return jnp.where(t >= k, r, 0.0)


def _shift_up(v, k):
    n = v.shape[0]
    r = pltpu.roll(v, n - k, axis=0)
    t = lax.broadcasted_iota(jnp.int32, v.shape, 0)
    return jnp.where(t < n - k, r, 0.0)


def _sigmoid(v):
    return 1.0 / (1.0 + jnp.exp(-v))


_GELU_C = 0.7978845608028654
_GELU_A = 0.044715


def _gelu(v):
    return 0.5 * v * (1.0 + jnp.tanh(_GELU_C * (v + _GELU_A * v * v * v)))


def _gelu_grad(v):
    th = jnp.tanh(_GELU_C * (v + _GELU_A * v * v * v))
    return 0.5 * (1.0 + th) + 0.5 * v * (1.0 - th * th) * _GELU_C * (1.0 + 3.0 * _GELU_A * v * v)


_NN = (((1,), (0,)), ((), ()))
_NT = (((1,), (1,)), ((), ()))
_TN = (((0,), (0,)), ((), ()))


def _dot(a, b, dims=_NN):
    return lax.dot_general(a, b, dims, preferred_element_type=F32)


def _logical(t, groups):
    return (t.shape[-2], t.shape[-1] * groups)


def _block(tr, tc, groups, cols, where):
    if groups == 1:
        return pl.BlockSpec((tr, tc), where)
    per = cols // groups // tc

    def index(i, j, s):
        r, c = where(i, j, s)
        return (c // per, r, c % per)

    return pl.BlockSpec((None, tr, tc), index)


def matmul(a, b, mode, out_dtype, name, ga=1, gb=1, go=1, tm=None, tn=None, tk=None):
    (ar, ac), (br, bc) = _logical(a, ga), _logical(b, gb)
    if mode == "nn":
        m, k, n = ar, ac, bc
        a_col, b_col = "k", "n"
    elif mode == "nt":
        m, k, n = ar, ac, br
        a_col, b_col = "k", "k"
    else:
        k, m, n = ar, ac, bc
        a_col, b_col = "m", "n"
    limit = {"m": m, "n": n // go, "k": k}
    limit[a_col] = min(limit[a_col], ac // ga)
    limit[b_col] = min(limit[b_col], bc // gb)
    tm = tm or _tile(limit["m"], 1024, 128 if mode == "tn" else 16)
    tn = tn or _tile(limit["n"], 512)
    tk = tk or _tile(limit["k"], 2048, 16 if mode == "tn" else 128)
    nk = k // tk
    if mode == "nn":
        a_spec = _block(tm, tk, ga, ac, lambda i, j, s: (i, s))
        b_spec = _block(tk, tn, gb, bc, lambda i, j, s: (s, j))
        dims = _NN
    elif mode == "nt":
        a_spec = _block(tm, tk, ga, ac, lambda i, j, s: (i, s))
        b_spec = _block(tn, tk, gb, bc, lambda i, j, s: (j, s))
        dims = _NT
    else:
        a_spec = _block(tk, tm, ga, ac, lambda i, j, s: (s, i))
        b_spec = _block(tk, tn, gb, bc, lambda i, j, s: (s, j))
        dims = _TN
    o_spec = _block(tm, tn, go, n, lambda i, j, s: (i, j))
    out_shape = _sds((m, n), out_dtype) if go == 1 else _sds((go, m, n // go), out_dtype)

    def body(a_ref, b_ref, o_ref, acc_ref):
        s = pl.program_id(2)

        @pl.when(s == 0)
        def _():
            acc_ref[...] = jnp.zeros_like(acc_ref)

        acc_ref[...] += _dot(a_ref[...], b_ref[...], dims)

        @pl.when(s == nk - 1)
        def _():
            o_ref[...] = acc_ref[...].astype(o_ref.dtype)

    return pl.pallas_call(
        body, name=name, out_shape=out_shape, grid=(m // tm, n // tn, nk),
        in_specs=[a_spec, b_spec], out_specs=o_spec,
        scratch_shapes=[pltpu.VMEM((tm, tn), F32)],
        compiler_params=_params(("parallel", "parallel", "arbitrary"), VMEM_BIG),
    )(a, b)


def _rows(tm, n):
    return pl.BlockSpec((tm, n), lambda i: (i, 0))


def _vec(n):
    return pl.BlockSpec((1, n), lambda i: (0, 0))


def modnorm_fwd(x, g, sc, sh, name):
    s, d = x.shape
    tm = _tile(s, 256, 8)

    def body(x_ref, g_ref, sc_ref, sh_ref, o_ref):
        xv = x_ref[...]
        r = lax.rsqrt(jnp.mean(xv * xv, axis=-1, keepdims=True) + EPS)
        o_ref[...] = ((xv * r) * g_ref[...] * (1.0 + sc_ref[...]) + sh_ref[...]).astype(BF16)

    return pl.pallas_call(
        body, name=name, out_shape=_sds((s, d), BF16), grid=(s // tm,),
        in_specs=[_rows(tm, d), _vec(d), _vec(d), _vec(d)], out_specs=_rows(tm, d),
        compiler_params=_params(("parallel",)),
    )(x, g, sc, sh)


def resid_fwd(x, y, gate, name):
    s, d = x.shape
    tm = _tile(s, 256, 8)

    def body(x_ref, y_ref, g_ref, o_ref):
        o_ref[...] = x_ref[...] + g_ref[...] * y_ref[...]

    return pl.pallas_call(
        body, name=name, out_shape=_sds((s, d)), grid=(s // tm,),
        in_specs=[_rows(tm, d), _rows(tm, d), _vec(d)], out_specs=_rows(tm, d),
        compiler_params=_params(("parallel",)),
    )(x, y, gate)


def gate_bwd(dres, y, gate, name):
    s, d = dres.shape
    tm = _tile(s, 256, 8)

    def body(dr_ref, y_ref, g_ref, dy_ref, dg_ref):
        @pl.when(pl.program_id(0) == 0)
        def _():
            dg_ref[...] = jnp.zeros_like(dg_ref)

        dr = dr_ref[...]
        dy_ref[...] = (dr * g_ref[...]).astype(BF16)
        dg_ref[...] += jnp.sum(dr * y_ref[...], axis=0, keepdims=True)

    return pl.pallas_call(
        body, name=name, out_shape=(_sds((s, d), BF16), _sds((1, d))), grid=(s // tm,),
        in_specs=[_rows(tm, d), _rows(tm, d), _vec(d)], out_specs=(_rows(tm, d), _vec(d)),
        compiler_params=_params(("arbitrary",)),
    )(dres, y, gate)


def norm_bwd(x, dh, g, sc, dres, name):
    s, d = x.shape
    tm = _tile(s, 256, 8)
    nsteps = s // tm

    def body(x_ref, dh_ref, g_ref, sc_ref, dr_ref, dx_ref, dsh_ref, dsc_ref, dg_ref, a2_ref):
        i = pl.program_id(0)

        @pl.when(i == 0)
        def _():
            dsh_ref[...] = jnp.zeros_like(dsh_ref)
            a2_ref[...] = jnp.zeros_like(a2_ref)

        xv = x_ref[...]
        dh = dh_ref[...]
        r = lax.rsqrt(jnp.mean(xv * xv, axis=-1, keepdims=True) + EPS)
        xh = xv * r
        dsh_ref[...] += jnp.sum(dh, axis=0, keepdims=True)
        a2_ref[...] += jnp.sum(dh * xh, axis=0, keepdims=True)
        dxh = dh * (g_ref[...] * (1.0 + sc_ref[...]))
        dx = r * (dxh - xh * jnp.mean(dxh * xh, axis=-1, keepdims=True))
        dx_ref[...] = dr_ref[...] + dx

        @pl.when(i == nsteps - 1)
        def _():
            dsc_ref[...] = a2_ref[...] * g_ref[...]
            dg_ref[...] = a2_ref[...] * (1.0 + sc_ref[...])

    return pl.pallas_call(
        body, name=name, out_shape=(_sds((s, d)), _sds((1, d)), _sds((1, d)), _sds((1, d))), grid=(nsteps,),
        in_specs=[_rows(tm, d), _rows(tm, d), _vec(d), _vec(d), _rows(tm, d)],
        out_specs=(_rows(tm, d), _vec(d), _vec(d), _vec(d)),
        scratch_shapes=[pltpu.VMEM((1, d), F32)],
        compiler_params=_params(("arbitrary",)),
    )(x, dh, g, sc, dres)


def final_fwd_bwd(x, g, tgt):
    s, d = x.shape
    tm = _tile(s, 256, 8)

    def body(x_ref, g_ref, t_ref, dx_ref, dg_ref, loss_ref):
        @pl.when(pl.program_id(0) == 0)
        def _():
            dg_ref[...] = jnp.zeros_like(dg_ref)
            loss_ref[...] = jnp.zeros_like(loss_ref)

        xv = x_ref[...]
        gv = g_ref[...]
        r = lax.rsqrt(jnp.mean(xv * xv, axis=-1, keepdims=True) + EPS)
        xh = xv * r
        e = xh * gv - t_ref[...]
        row = jnp.sum(e * e, axis=-1, keepdims=True) * (0.5 / d)
        loss_ref[...] += jnp.sum(row, axis=0, keepdims=True)
        dy = e * (1.0 / d)
        dg_ref[...] += jnp.sum(dy * xh, axis=0, keepdims=True)
        dxh = dy * gv
        dx_ref[...] = r * (dxh - xh * jnp.mean(dxh * xh, axis=-1, keepdims=True))

    return pl.pallas_call(
        body, name="final_fwd_bwd", out_shape=(_sds((s, d)), _sds((1, d)), _sds((1, LANES))), grid=(s // tm,),
        in_specs=[_rows(tm, d), _vec(d), _rows(tm, d)], out_specs=(_rows(tm, d), _vec(d), _vec(LANES)),
        compiler_params=_params(("arbitrary",)),
    )(x, g, tgt)


def _taps(v):
    return _shift_down(v, 2), _shift_down(v, 1), v


def _conv3_taps(taps, w):
    return w[0:1, :] * taps[0] + w[1:2, :] * taps[1] + w[2:3, :] * taps[2]


def _conv3(v, w):
    return _conv3_taps(_taps(v), w)


def _conv3_t(dv, w):
    return w[0:1, :] * _shift_up(dv, 2) + w[1:2, :] * _shift_up(dv, 1) + w[2:3, :] * dv


def _conv3_dw_taps(dv, taps):
    return jnp.concatenate([jnp.sum(dv * t, axis=0, keepdims=True) for t in taps], axis=0)


def _conv3_dw(dv, v):
    return _conv3_dw_taps(dv, _taps(v))


def gconv_fwd(z, conv_w):
    s = z.shape[0]
    nb = A_WIDTH // LANES

    def body(b_ref, c_ref, a_ref, w_ref, o_ref):
        b, c, a = b_ref[...].astype(F32), c_ref[...].astype(F32), a_ref[...].astype(F32)
        o_ref[...] = (b * _conv3(c * a, w_ref[...])).astype(BF16)

    col = lambda off: pl.BlockSpec((s, LANES), lambda j: (0, off + j))
    return pl.pallas_call(
        body, name="gconv_fwd", out_shape=_sds((s, A_WIDTH), BF16), grid=(nb,),
        in_specs=[col(0), col(nb), col(2 * nb), pl.BlockSpec((3, LANES), lambda j: (0, j))],
        out_specs=pl.BlockSpec((s, LANES), lambda j: (0, j)),
        compiler_params=_params(("parallel",), VMEM_BIG),
    )(z, z, z, conv_w)


def gconv_bwd(z, conv_w, dycat):
    s = z.shape[0]
    nb = A_WIDTH // LANES

    def body(b_ref, c_ref, a_ref, w_ref, dy_ref, db_ref, dc_ref, da_ref, dw_ref):
        c, a, w, dy = c_ref[...].astype(F32), a_ref[...].astype(F32), w_ref[...], dy_ref[...].astype(F32)
        ca = c * a
        db_ref[...] = (dy * _conv3(ca, w)).astype(BF16)
        dconv = dy * b_ref[...].astype(F32)
        dw_ref[...] = _conv3_dw(dconv, ca)
        dca = _conv3_t(dconv, w)
        dc_ref[...] = (dca * a).astype(BF16)
        da_ref[...] = (dca * c).astype(BF16)

    col = lambda off: pl.BlockSpec((s, LANES), lambda j: (0, off + j))
    wspec = pl.BlockSpec((3, LANES), lambda j: (0, j))
    part = _sds((s, A_WIDTH), BF16)
    return pl.pallas_call(
        body, name="gconv_bwd", out_shape=(part, part, part, _sds((3, A_WIDTH))), grid=(nb,),
        in_specs=[col(0), col(nb), col(2 * nb), wspec, col(0)],
        out_specs=(col(0), col(0), col(0), wspec),
        compiler_params=_params(("parallel",), VMEM_BIG),
    )(z, z, z, conv_w, dycat)


def _pool_counts(s, w):
    t = lax.broadcasted_iota(jnp.int32, (s, 1), 0)
    return jnp.minimum(t + 1, w).astype(F32)


def _pooled(p, levels):
    acc = p
    for lv in range(levels):
        acc = acc + _shift_down(acc, 2 ** lv)
    return acc / _pool_counts(p.shape[0], 2 ** levels) - p


def pool_fwd(z, mix_w, scale):
    s = z.shape[0]

    def make(g):
        def body_g(p_ref, m_ref, sc_ref, o_ref):
            pooled = _pooled(p_ref[...].astype(F32), g + 1)
            y = _dot(pooled.astype(BF16), m_ref[...].astype(BF16))
            o_ref[...] = (y * sc_ref[...]).astype(BF16)
        return body_g

    outs = []
    for g in range(B_GROUPS):
        outs.append(pl.pallas_call(
            make(g), name=f"pool_fwd{g}", out_shape=_sds((s, LANES), BF16), grid=(1,),
            in_specs=[pl.BlockSpec((s, LANES), lambda i, g=g: (0, 3 * (A_WIDTH // LANES) + g)),
                      pl.BlockSpec((None, LANES, LANES), lambda i, g=g: (g, 0, 0)),
                      pl.BlockSpec((1, LANES), lambda i, g=g: (0, g))],
            out_specs=pl.BlockSpec((s, LANES), lambda i: (0, 0)),
            compiler_params=_params(("arbitrary",), VMEM_BIG),
        )(z, mix_w, scale))
    return outs


def pool_bwd(z, mix_w, scale, dycat):
    s = z.shape[0]

    def make(g):
        w = 2 ** (g + 1)

        def body_g(p_ref, m_ref, sc_ref, dy_ref, dp_ref, dm_ref, dsc_ref):
            pooled = _pooled(p_ref[...].astype(F32), g + 1)
            mw = m_ref[...].astype(BF16)
            pb = pooled.astype(BF16)
            dy = dy_ref[...].astype(F32)
            dsc_ref[...] = jnp.sum(dy * _dot(pb, mw), axis=0, keepdims=True)
            dmix = (dy * sc_ref[...]).astype(BF16)
            dm_ref[...] = _dot(pb, dmix, _TN)
            dpool = _dot(dmix, mw, _NT)
            acc = dpool / _pool_counts(s, w)
            for lv in range(g + 1):
                acc = acc + _shift_up(acc, 2 ** lv)
            dp_ref[...] = (acc - dpool).astype(BF16)
        return body_g

    outs = []
    for g in range(B_GROUPS):
        outs.append(pl.pallas_call(
            make(g), name=f"pool_bwd{g}",
            out_shape=(_sds((s, LANES), BF16), _sds((LANES, LANES)), _sds((1, LANES))), grid=(1,),
            in_specs=[pl.BlockSpec((s, LANES), lambda i, g=g: (0, 3 * (A_WIDTH // LANES) + g)),
                      pl.BlockSpec((None, LANES, LANES), lambda i, g=g: (g, 0, 0)),
                      pl.BlockSpec((1, LANES), lambda i, g=g: (0, g)),
                      pl.BlockSpec((s, LANES), lambda i, g=g: (0, A_WIDTH // LANES + g))],
            out_specs=(pl.BlockSpec((s, LANES), lambda i: (0, 0)), pl.BlockSpec((LANES, LANES), lambda i: (0, 0)),
                       pl.BlockSpec((1, LANES), lambda i: (0, 0))),
            compiler_params=_params(("arbitrary",), VMEM_BIG),
        )(z, mix_w, scale, dycat))
    return outs


_FF_BLOCKS = D_FF // FF_UNIT


def _ff_spec(s):
    return pl.BlockSpec((2, s, FF_UNIT), lambda j: (0, 0, j))


def _ff_wspecs():
    return [pl.BlockSpec((3, FF_UNIT), lambda j: (0, j)), pl.BlockSpec((3, FF_UNIT), lambda j: (0, _FF_BLOCKS + j))]


_FF_ROWS = 64
_FF_HALO = 16


def _chunk_taps(z_ref, half, c):
    start = pl.multiple_of(c * _FF_ROWS, _FF_ROWS)
    before = pl.multiple_of(jnp.maximum(c * _FF_ROWS - _FF_HALO, 0), _FF_HALO)
    halo = z_ref[half, pl.ds(before, _FF_HALO), :].astype(F32)
    halo = jnp.where(c > 0, halo, 0.0)
    win = jnp.concatenate([halo, z_ref[half, pl.ds(start, _FF_ROWS), :].astype(F32)], axis=0)
    return tuple(pltpu.roll(win, k, axis=0)[_FF_HALO:] for k in (2, 1)) + (win[_FF_HALO:],)


def _fold8(v):
    acc = v[0:8]
    for r in range(8, v.shape[0], 8):
        acc = acc + v[r:r + 8]
    return acc


def ffn_act_fwd(zf, conv_w, name):
    s = zf.shape[1]
    assert s % _FF_ROWS == 0

    def body(z_ref, wg_ref, wu_ref, o_ref):
        g = _conv3(z_ref[0].astype(F32), wg_ref[...])
        u = _conv3(z_ref[1].astype(F32), wu_ref[...])
        o_ref[...] = (g * _sigmoid(g) * u).astype(BF16)

    return pl.pallas_call(
        body, name=name, out_shape=_sds((s, D_FF), BF16), grid=(_FF_BLOCKS,),
        in_specs=[_ff_spec(s)] + _ff_wspecs(), out_specs=pl.BlockSpec((s, FF_UNIT), lambda j: (0, j)),
        compiler_params=_params(("parallel",), VMEM_BIG),
    )(zf, conv_w, conv_w)


def ffn_act_bwd(zf, conv_w, da, name):
    s = zf.shape[1]
    assert s % _FF_ROWS == 0
    nchunks = s // _FF_ROWS

    def body(z_ref, wg_ref, wu_ref, da_ref, dz_ref, dw_ref, dg_ref, du_ref):
        wg, wu = wg_ref[...], wu_ref[...]

        def first(c, acc):
            rows = pl.ds(pl.multiple_of(c * _FF_ROWS, _FF_ROWS), _FF_ROWS)
            tg, tu = _chunk_taps(z_ref, 0, c), _chunk_taps(z_ref, 1, c)
            g = _conv3_taps(tg, wg)
            u = _conv3_taps(tu, wu)
            dav = da_ref[rows, :].astype(F32)
            sg = _sigmoid(g)
            dg = dav * u * (sg * (1.0 + g * (1.0 - sg)))
            du = dav * (g * sg)
            dg_ref[rows, :] = dg
            du_ref[rows, :] = du
            return tuple(a + _fold8(d * t) for a, (d, t) in zip(acc, [(dg, t) for t in tg] + [(du, t) for t in tu]))

        zero = jnp.zeros((8, FF_UNIT), F32)
        acc = lax.fori_loop(0, nchunks, first, (zero,) * 6)
        sums = [jnp.sum(a, axis=0, keepdims=True) for a in acc]
        dw_ref[0] = jnp.concatenate(sums[:3], axis=0)
        dw_ref[1] = jnp.concatenate(sums[3:], axis=0)

        tail = pl.ds(s, _FF_HALO)
        dg_ref[tail, :] = jnp.zeros((_FF_HALO, FF_UNIT), F32)
        du_ref[tail, :] = jnp.zeros((_FF_HALO, FF_UNIT), F32)
        span = _FF_ROWS + _FF_HALO

        def second(c, carry):
            start = pl.multiple_of(c * _FF_ROWS, _FF_ROWS)
            for half, (d_ref, w) in enumerate(((dg_ref, wg), (du_ref, wu))):
                win = d_ref[pl.ds(start, span), :]
                dz = (w[0:1, :] * pltpu.roll(win, span - 2, axis=0)[:_FF_ROWS]
                      + w[1:2, :] * pltpu.roll(win, span - 1, axis=0)[:_FF_ROWS] + w[2:3, :] * win[:_FF_ROWS])
                dz_ref[half, pl.ds(start, _FF_ROWS), :] = dz.astype(BF16)
            return carry

        lax.fori_loop(0, nchunks, second, 0)

    return pl.pallas_call(
        body, name=name, out_shape=(_sds((2, s, D_FF), BF16), _sds((2, 3, D_FF))), grid=(_FF_BLOCKS,),
        in_specs=[_ff_spec(s)] + _ff_wspecs() + [pl.BlockSpec((s, FF_UNIT), lambda j: (0, j))],
        out_specs=(_ff_spec(s), pl.BlockSpec((2, 3, FF_UNIT), lambda j: (0, 0, j))),
        scratch_shapes=[pltpu.VMEM((s + _FF_HALO, FF_UNIT), F32), pltpu.VMEM((s + _FF_HALO, FF_UNIT), F32)],
        compiler_params=_params(("parallel",), VMEM_BIG),
    )(zf, conv_w, conv_w, da)


def _rope(v, cs, s1, s2):
    return v * cs + pltpu.roll(v, LANES - C_ROPE // 2, axis=1) * s1 + pltpu.roll(v, C_ROPE // 2, axis=1) * s2


def _rope_t(dv, cs, s1, s2):
    return dv * cs + pltpu.roll(dv * s1, C_ROPE // 2, axis=1) + pltpu.roll(dv * s2, LANES - C_ROPE // 2, axis=1)


def _kpe_mask(shape):
    lane = lax.broadcasted_iota(jnp.int32, shape, 1)
    return (lane >= C_NOPE) & (lane < C_NOPE + C_ROPE)


def _rms(v, g):
    r = lax.rsqrt(jnp.mean(v * v, axis=-1, keepdims=True) + EPS)
    return v * r, r


def _rms_bwd(dn, xh, r, g):
    dxh = dn * g
    return r * (dxh - xh * jnp.mean(dxh * xh, axis=-1, keepdims=True)), jnp.sum(dn * xh, axis=0, keepdims=True)


_ZQ = C_Q_RANK + C_KV_RANK + HEAD_PAD
_HW = C_HEADS * HEAD_PAD


def mla_pre_fwd(z, gq, gkv, wq, wk, wv, cs, s1, s2):
    s = z.shape[0]
    tm = _tile(s, 256, 8)

    def body(z_ref, gq_ref, gkv_ref, wq_ref, wk_ref, wv_ref, cs_ref, s1_ref, s2_ref, q_ref, k_ref, v_ref):
        zv = z_ref[...].astype(F32)
        cst, s1t, s2t = cs_ref[...], s1_ref[...], s2_ref[...]
        qh, _ = _rms(zv[:, :C_Q_RANK], None)
        qn = (qh * gq_ref[...]).astype(BF16)
        q = _dot(qn, wq_ref[...])
        kh, _ = _rms(zv[:, C_Q_RANK:C_Q_RANK + C_KV_RANK], None)
        kvn = (kh * gkv_ref[...]).astype(BF16)
        k = _dot(kvn, wk_ref[...])
        v_ref[...] = _dot(kvn, wv_ref[...]).astype(BF16)
        kpe = _rope(zv[:, C_Q_RANK + C_KV_RANK:], cst, s1t, s2t)
        for h in range(C_HEADS):
            sl = slice(h * HEAD_PAD, (h + 1) * HEAD_PAD)
            q_ref[:, sl] = _rope(q[:, sl], cst, s1t, s2t).astype(BF16)
            k_ref[:, sl] = (k[:, sl] + kpe).astype(BF16)

    full = lambda r, c: pl.BlockSpec((r, c), lambda i: (0, 0))
    hw = _sds((s, _HW), BF16)
    return pl.pallas_call(
        body, name="mla_pre_fwd", out_shape=(hw, hw, hw), grid=(s // tm,),
        in_specs=[_rows(tm, _ZQ), _vec(C_Q_RANK), _vec(C_KV_RANK), full(C_Q_RANK, _HW), full(C_KV_RANK, _HW),
                  full(C_KV_RANK, _HW), _rows(tm, LANES), _rows(tm, LANES), _rows(tm, LANES)],
        out_specs=(_rows(tm, _HW), _rows(tm, _HW), _rows(tm, _HW)),
        compiler_params=_params(("parallel",), VMEM_BIG),
    )(z, gq, gkv, wq, wk, wv, cs, s1, s2)


def mla_pre_bwd(z, gq, gkv, wq, wk, wv, cs, s1, s2, dq, dk, dv):
    s = z.shape[0]
    tm = _tile(s, 256, 8)

    def body(z_ref, gq_ref, gkv_ref, wq_ref, wk_ref, wv_ref, cs_ref, s1_ref, s2_ref, dq_ref, dk_ref, dv_ref,
             dz_ref, dwq_ref, dwk_ref, dwv_ref, dgq_ref, dgkv_ref):
        @pl.when(pl.program_id(0) == 0)
        def _():
            dwq_ref[...] = jnp.zeros_like(dwq_ref)
            dwk_ref[...] = jnp.zeros_like(dwk_ref)
            dwv_ref[...] = jnp.zeros_like(dwv_ref)
            dgq_ref[...] = jnp.zeros_like(dgq_ref)
            dgkv_ref[...] = jnp.zeros_like(dgkv_ref)

        zv = z_ref[...].astype(F32)
        cst, s1t, s2t = cs_ref[...], s1_ref[...], s2_ref[...]
        gqv, gkvv = gq_ref[...], gkv_ref[...]
        qh, rq = _rms(zv[:, :C_Q_RANK], None)
        qn = (qh * gqv).astype(BF16)
        kh, rk = _rms(zv[:, C_Q_RANK:C_Q_RANK + C_KV_RANK], None)
        kvn = (kh * gkvv).astype(BF16)

        dqv = dq_ref[...].astype(F32)
        dqp = jnp.concatenate(
            [_rope_t(dqv[:, h * HEAD_PAD:(h + 1) * HEAD_PAD], cst, s1t, s2t) for h in range(C_HEADS)], axis=1
        ).astype(BF16)
        dwq_ref[...] += _dot(qn, dqp, _TN)
        dqn = _dot(dqp, wq_ref[...], _NT)
        dql, dgq = _rms_bwd(dqn, qh, rq, gqv)
        dgq_ref[...] += dgq

        dkv = dk_ref[...]
        dkb = dkv.astype(BF16)
        dvb = dv_ref[...].astype(BF16)
        dwk_ref[...] += _dot(kvn, dkb, _TN)
        dwv_ref[...] += _dot(kvn, dvb, _TN)
        dkvn = _dot(dkb, wk_ref[...], _NT) + _dot(dvb, wv_ref[...], _NT)
        dkl, dgkv = _rms_bwd(dkvn, kh, rk, gkvv)
        dgkv_ref[...] += dgkv

        dkpe = dkv[:, :HEAD_PAD]
        for h in range(1, C_HEADS):
            dkpe = dkpe + dkv[:, h * HEAD_PAD:(h + 1) * HEAD_PAD]
        dkpe = _rope_t(jnp.where(_kpe_mask(dkpe.shape), dkpe, 0.0), cst, s1t, s2t)
        dz_ref[...] = jnp.concatenate([dql, dkl, dkpe], axis=1).astype(BF16)

    full = lambda r, c: pl.BlockSpec((r, c), lambda i: (0, 0))
    return pl.pallas_call(
        body, name="mla_pre_bwd",
        out_shape=(_sds((s, _ZQ), BF16), _sds((C_Q_RANK, _HW)), _sds((C_KV_RANK, _HW)), _sds((C_KV_RANK, _HW)),
                   _sds((1, C_Q_RANK)), _sds((1, C_KV_RANK))),
        grid=(s // tm,),
        in_specs=[_rows(tm, _ZQ), _vec(C_Q_RANK), _vec(C_KV_RANK), full(C_Q_RANK, _HW), full(C_KV_RANK, _HW),
                  full(C_KV_RANK, _HW), _rows(tm, LANES), _rows(tm, LANES), _rows(tm, LANES),
                  _rows(tm, _HW), _rows(tm, _HW), _rows(tm, _HW)],
        out_specs=(_rows(tm, _ZQ), full(C_Q_RANK, _HW), full(C_KV_RANK, _HW), full(C_KV_RANK, _HW),
                   _vec(C_Q_RANK), _vec(C_KV_RANK)),
        compiler_params=_params(("arbitrary",), VMEM_BIG),
    )(z, gq, gkv, wq, wk, wv, cs, s1, s2, dq, dk, dv)


_ATT_SCALE = (C_NOPE + C_ROPE) ** -0.5
_NEG = -1e30


def _att_exp(q, k, row0, ends_here):
    sc = _dot(q, k, _NT) * _ATT_SCALE
    tq, nk = sc.shape
    if ends_here:
        last = sc[:, nk - tq:]
        row = lax.broadcasted_iota(jnp.int32, last.shape, 0)
        col = lax.broadcasted_iota(jnp.int32, last.shape, 1)
        last = jnp.where(col <= row, last, _NEG)
        sc = last if nk == tq else jnp.concatenate([sc[:, :nk - tq], last], axis=1)
    else:
        qpos = row0 + lax.broadcasted_iota(jnp.int32, sc.shape, 0)
        kpos = lax.broadcasted_iota(jnp.int32, sc.shape, 1)
        sc = jnp.where(kpos <= qpos, sc, _NEG)
    e = jnp.exp(sc - jnp.max(sc, axis=-1, keepdims=True))
    return e, 1.0 / jnp.sum(e, axis=-1, keepdims=True)


def _causal_cases(i, nq, tq, fn):
    if nq > 8:
        fn(nq * tq, False)
        return
    for blk in range(nq):
        pl.when(i == blk)(functools.partial(fn, (blk + 1) * tq, True))


def attn_fwd(q, k, v):
    s = q.shape[0]
    tq = _tile(s, 256, 8)
    nq = s // tq

    def body(q_ref, k_ref, v_ref, o_ref):
        i = pl.program_id(1)

        def case(nk, ends_here):
            e, inv = _att_exp(q_ref[...], k_ref[:nk, :], i * tq, ends_here)
            o_ref[...] = (_dot(e.astype(BF16), v_ref[:nk, :]) * inv).astype(BF16)

        _causal_cases(i, nq, tq, case)

    qspec = pl.BlockSpec((tq, HEAD_PAD), lambda h, i: (i, h))
    kspec = pl.BlockSpec((s, HEAD_PAD), lambda h, i: (0, h))
    return pl.pallas_call(
        body, name="attn_fwd", out_shape=_sds((s, _HW), BF16), grid=(C_HEADS, s // tq),
        in_specs=[qspec, kspec, kspec], out_specs=qspec,
        compiler_params=_params(("parallel", "parallel"), VMEM_BIG),
    )(q, k, v)


def attn_bwd(q, k, v, o, do_all, do_col0):
    s = q.shape[0]
    tq = _tile(s, 256, 8)

    def body(q_ref, k_ref, v_ref, o_ref, do_ref, dq_ref, dk_ref, dv_ref):
        i = pl.program_id(1)

        @pl.when(i == 0)
        def _():
            dk_ref[...] = jnp.zeros_like(dk_ref)
            dv_ref[...] = jnp.zeros_like(dv_ref)

        def case(nk, ends_here):
            qv, kv, vv, dov = q_ref[...], k_ref[:nk, :], v_ref[:nk, :], do_ref[...]
            e, inv = _att_exp(qv, kv, i * tq, ends_here)
            p = e * inv
            dp = _dot(dov, vv, _NT)
            delta = jnp.sum(dov.astype(F32) * o_ref[...].astype(F32), axis=-1, keepdims=True)
            ds = (p * (dp - delta) * _ATT_SCALE).astype(BF16)
            dq_ref[...] = _dot(ds, kv).astype(BF16)
            dk_ref[:nk, :] += _dot(ds, qv, _TN)
            dv_ref[:nk, :] += _dot(p.astype(BF16), dov, _TN)

        _causal_cases(i, s // tq, tq, case)

    qspec = pl.BlockSpec((tq, HEAD_PAD), lambda h, i: (i, h))
    dospec = pl.BlockSpec((tq, HEAD_PAD), lambda h, i: (i, do_col0 + h))
    kspec = pl.BlockSpec((s, HEAD_PAD), lambda h, i: (0, h))
    return pl.pallas_call(
        body, name="attn_bwd", out_shape=(_sds((s, _HW), BF16), _sds((s, _HW)), _sds((s, _HW))),
        grid=(C_HEADS, s // tq),
        in_specs=[qspec, kspec, kspec, qspec, dospec], out_specs=(qspec, kspec, kspec),
        compiler_params=_params(("parallel", "arbitrary"), VMEM_BIG),
    )(q, k, v, o, do_all)


_DW = D_GROUPS * LANES


def _tril_bf16(w):
    r = lax.broadcasted_iota(jnp.int32, w.shape, 0)
    c = lax.broadcasted_iota(jnp.int32, w.shape, 1)
    return jnp.where(c <= r, w, 0.0).astype(BF16)


def _sgu_forward(zu, zv, lg, lb, ws_ref, bs):
    u = _gelu(zu)
    v = _gelu(zv)
    mu = jnp.mean(v, axis=-1, keepdims=True)
    vc = v - mu
    rstd = lax.rsqrt(jnp.mean(vc * vc, axis=-1, keepdims=True) + EPS)
    xh = vc * rstd
    vln = (xh * lg + lb).astype(BF16)
    mixed = []
    for g in range(D_GROUPS):
        wg = _tril_bf16(ws_ref[g])
        mixed.append(_dot(wg, vln[:, g * LANES:(g + 1) * LANES]) + bs[:, g:g + 1])
    return u, xh, rstd, vln, jnp.concatenate(mixed, axis=1)


def sgu_fwd(z, lg, lb, ws, bs_t):
    s = z.shape[0]
    nchunk = s // D_CHUNK

    def body(zu_ref, zv_ref, lg_ref, lb_ref, ws_ref, bs_ref, o_ref):
        u, _, _, _, mixed = _sgu_forward(zu_ref[...].astype(F32), zv_ref[...].astype(F32), lg_ref[...], lb_ref[...],
                                         ws_ref, bs_ref[...])
        o_ref[...] = (u * mixed).astype(BF16)

    return pl.pallas_call(
        body, name="sgu_fwd", out_shape=_sds((s, _DW), BF16), grid=(nchunk,),
        in_specs=[pl.BlockSpec((D_CHUNK, _DW), lambda n: (n, 1)), pl.BlockSpec((D_CHUNK, _DW), lambda n: (n, 2)),
                  _vec(_DW), _vec(_DW), pl.BlockSpec((D_GROUPS, D_CHUNK, D_CHUNK), lambda n: (0, 0, 0)),
                  pl.BlockSpec((D_CHUNK, LANES), lambda n: (0, 0))],
        out_specs=pl.BlockSpec((D_CHUNK, _DW), lambda n: (n, 0)),
        compiler_params=_params(("parallel",)),
    )(z, z, lg, lb, ws, bs_t)


def sgu_bwd(z, lg, lb, ws, bs_t, dycat, dy_col):
    s = z.shape[0]
    nchunk = s // D_CHUNK

    def body(zu_ref, zv_ref, lg_ref, lb_ref, ws_ref, bs_ref, dy_ref, dzu_ref, dzv_ref, dws_ref, dbs_ref, dlg_ref,
             dlb_ref):
        @pl.when(pl.program_id(0) == 0)
        def _():
            dws_ref[...] = jnp.zeros_like(dws_ref)
            dbs_ref[...] = jnp.zeros_like(dbs_ref)
            dlg_ref[...] = jnp.zeros_like(dlg_ref)
            dlb_ref[...] = jnp.zeros_like(dlb_ref)

        zu, zv, lg = zu_ref[...].astype(F32), zv_ref[...].astype(F32), lg_ref[...]
        u, xh, rstd, vln, mixed = _sgu_forward(zu, zv, lg, lb_ref[...], ws_ref, bs_ref[...])
        dy = dy_ref[...].astype(F32)
        dzu_ref[...] = (dy * mixed * _gelu_grad(zu)).astype(BF16)
        dmix = dy * u
        lane = lax.broadcasted_iota(jnp.int32, (D_CHUNK, LANES), 1)
        row = lax.broadcasted_iota(jnp.int32, (D_CHUNK, D_CHUNK), 0)
        colm = lax.broadcasted_iota(jnp.int32, (D_CHUNK, D_CHUNK), 1)
        dvln = []
        dbs = jnp.zeros((D_CHUNK, LANES), F32)
        for g in range(D_GROUPS):
            sl = slice(g * LANES, (g + 1) * LANES)
            dmg = dmix[:, sl]
            dbs = dbs + jnp.where(lane == g, jnp.sum(dmg, axis=-1, keepdims=True), 0.0)
            dmb = dmg.astype(BF16)
            dws_ref[g] += jnp.where(colm <= row, _dot(dmb, vln[:, sl], _NT), 0.0)
            dvln.append(_dot(_tril_bf16(ws_ref[g]), dmb, _TN))
        dbs_ref[...] += dbs
        dvln = jnp.concatenate(dvln, axis=1)
        dlg_ref[...] += jnp.sum(dvln * xh, axis=0, keepdims=True)
        dlb_ref[...] += jnp.sum(dvln, axis=0, keepdims=True)
        dxh = dvln * lg
        dvv = rstd * (dxh - jnp.mean(dxh, axis=-1, keepdims=True) - xh * jnp.mean(dxh * xh, axis=-1, keepdims=True))
        dzv_ref[...] = (dvv * _gelu_grad(zv)).astype(BF16)

    wsspec = pl.BlockSpec((D_GROUPS, D_CHUNK, D_CHUNK), lambda n: (0, 0, 0))
    chunk = lambda cidx: pl.BlockSpec((D_CHUNK, _DW), lambda n: (n, cidx))
    return pl.pallas_call(
        body, name="sgu_bwd",
        out_shape=(_sds((s, _DW), BF16), _sds((s, _DW), BF16), _sds((D_GROUPS, D_CHUNK, D_CHUNK)),
                   _sds((D_CHUNK, LANES)), _sds((1, _DW)), _sds((1, _DW))),
        grid=(nchunk,),
        in_specs=[chunk(1), chunk(2), _vec(_DW), _vec(_DW), wsspec, pl.BlockSpec((D_CHUNK, LANES), lambda n: (0, 0)),
                  chunk(dy_col)],
        out_specs=(chunk(0), chunk(0), wsspec, pl.BlockSpec((D_CHUNK, LANES), lambda n: (0, 0)), _vec(_DW), _vec(_DW)),
        compiler_params=_params(("arbitrary",)),
    )(z, z, lg, lb, ws, bs_t, dycat)


def ada_mod(c_all, ada_w, ada_b):
    nl, d, n = ada_w.shape
    nb = c_all.shape[0]
    tn = _tile(n, 512)

    def body(c_ref, w_ref, b_ref, o_ref):
        cv = c_ref[...]
        ca = (cv * _sigmoid(cv)).astype(BF16)
        o_ref[...] = _dot(ca, w_ref[...].astype(BF16)) + b_ref[...]

    return pl.pallas_call(
        body, name="ada_mod", out_shape=_sds((nl, nb, n)), grid=(nl, n // tn),
        in_specs=[pl.BlockSpec((nb, d), lambda l, j: (0, 0)), pl.BlockSpec((None, d, tn), lambda l, j: (l, 0, j)),
                  pl.BlockSpec((None, 1, tn), lambda l, j: (l, 0, j))],
        out_specs=pl.BlockSpec((None, nb, tn), lambda l, j: (l, 0, j)),
        compiler_params=_params(("parallel", "parallel")),
    )(c_all, ada_w, ada_b.reshape(nl, 1, n))


def ada_grad(c_all_t, dmod):
    d, nb = c_all_t.shape
    nl, _, n = dmod.shape
    tn = _tile(n, 512)
    tr = _tile(d, 256, 8)

    def body(c_ref, dm_ref, o_ref):
        cv = c_ref[...]
        ca = cv * _sigmoid(cv)
        dm = dm_ref[...]
        acc = ca[:, 0:1] * dm[0:1, :]
        for b in range(1, nb):
            acc = acc + ca[:, b:b + 1] * dm[b:b + 1, :]
        o_ref[...] = acc

    return pl.pallas_call(
        body, name="ada_grad", out_shape=_sds((nl, d, n)), grid=(nl, n // tn, d // tr),
        in_specs=[pl.BlockSpec((tr, nb), lambda l, j, r: (r, 0)), pl.BlockSpec((None, nb, tn), lambda l, j, r: (l, 0, j))],
        out_specs=pl.BlockSpec((None, tr, tn), lambda l, j, r: (l, r, j)),
        compiler_params=_params(("parallel", "parallel", "parallel")),
    )(c_all_t, dmod)


_ADAM_BLOCK = 256 * 1024


def _adam_rows(rows, cols):
    if rows * cols <= _ADAM_BLOCK or rows % 8:
        return rows
    return _tile(rows, max(8, _ADAM_BLOCK // cols), 8)


def _adam_update(w, gv, m, v):
    inv_bc1 = 1.0 / (1.0 - ADAM_B1 ** ADAM_STEP)
    inv_bc2 = 1.0 / (1.0 - ADAM_B2 ** ADAM_STEP)
    nm = ADAM_B1 * m + (1.0 - ADAM_B1) * gv
    nv = ADAM_B2 * v + (1.0 - ADAM_B2) * (gv * gv)
    return -ADAM_LR * ((nm * inv_bc1) / (jnp.sqrt(nv * inv_bc2) + ADAM_EPS) + ADAM_WD * w), nm, nv


def adamw(w, g, m, v, name, copy_grad=False):
    shape = w.shape
    cols = shape[-1]
    rows = w.size // cols
    tr = _adam_rows(rows, cols)

    def body(w_ref, g_ref, m_ref, v_ref, d_ref, nm_ref, nv_ref, *go_ref):
        gv = g_ref[...]
        d_ref[...], nm_ref[...], nv_ref[...] = _adam_update(w_ref[...], gv, m_ref[...], v_ref[...])
        if copy_grad:
            go_ref[0][...] = gv

    spec = pl.BlockSpec((tr, cols), lambda i: (i, 0))
    out = _sds((rows, cols))
    r2 = lambda t: t.reshape(rows, cols)
    nout = 4 if copy_grad else 3
    res = pl.pallas_call(
        body, name=name, out_shape=(out,) * nout, grid=(rows // tr,),
        in_specs=[spec] * 4, out_specs=(spec,) * nout, compiler_params=_params(("parallel",)),
    )(r2(w), r2(g), r2(m), r2(v))
    grad = res[3] if copy_grad else g
    return grad.reshape(shape), res[0].reshape(shape), res[1].reshape(shape), res[2].reshape(shape)


def adamw_small(ws, gs, ms, vs):
    n = len(ws)
    flat = lambda t: t.reshape(-1, t.shape[-1])

    def body(*refs):
        ins, outs = refs[:4 * n], refs[4 * n:]
        for i in range(n):
            w_ref, g_ref, m_ref, v_ref = ins[4 * i:4 * i + 4]
            outs[3 * i][...], outs[3 * i + 1][...], outs[3 * i + 2][...] = _adam_update(
                w_ref[...], g_ref[...], m_ref[...], v_ref[...])

    operands = [flat(t) for quad in zip(ws, gs, ms, vs) for t in quad]
    res = pl.pallas_call(
        body, name="adamw_small", out_shape=tuple(_sds(flat(w).shape) for w in ws for _ in range(3)),
    )(*operands)
    return [(g, res[3 * i].reshape(w.shape), res[3 * i + 1].reshape(w.shape), res[3 * i + 2].reshape(w.shape))
            for i, (w, g) in enumerate(zip(ws, gs))]


def adamw_layers(w, g0, g1, m, v, name):
    _, rows, cols = w.shape
    tr = _adam_rows(rows, cols)

    def body(w_ref, g0_ref, g1_ref, m_ref, v_ref, g_ref, d_ref, nm_ref, nv_ref):
        gv = jnp.where(pl.program_id(0) == 0, g0_ref[...], g1_ref[...])
        g_ref[...] = gv
        d_ref[...], nm_ref[...], nv_ref[...] = _adam_update(w_ref[...], gv, m_ref[...], v_ref[...])

    spec = pl.BlockSpec((None, tr, cols), lambda l, i: (l, i, 0))
    gspec = pl.BlockSpec((tr, cols), lambda l, i: (i, 0))
    out = _sds((2, rows, cols))
    return pl.pallas_call(
        body, name=name, out_shape=(out, out, out, out), grid=(2, rows // tr),
        in_specs=[spec, gspec, gspec, spec, spec], out_specs=(spec,) * 4, compiler_params=_params(("parallel", "parallel")),
    )(w, g0, g1, m, v)


def sum8(gathered):
    _, r, _ = gathered.shape
    tr = _tile(r, 512, 8)

    def body(g_ref, o_ref):
        acc = g_ref[0]
        for dev in range(1, N_DEV):
            acc = acc + g_ref[dev]
        o_ref[...] = acc

    return pl.pallas_call(
        body, name="sum8", out_shape=_sds((r, LANES)), grid=(r // tr,),
        in_specs=[pl.BlockSpec((N_DEV, tr, LANES), lambda i: (0, i, 0))], out_specs=pl.BlockSpec((tr, LANES), lambda i: (i, 0)),
        compiler_params=_params(("parallel",)),
    )(gathered)


_SUM_BLOCK = 512 * 1024


def _sum_rows(rh, cols):
    return rh if rh * cols <= _SUM_BLOCK else _tile(rh, max(16, _SUM_BLOCK // cols), 16)


def pair_sum(g, recv, core, name):
    _, r, cols = g.shape
    rh = r // 2
    tr = _sum_rows(rh, cols)
    per = rh // tr

    def body(c_ref, a_ref, b_ref, o_ref):
        del c_ref
        o_ref[...] = (a_ref[...].astype(F32) + b_ref[...].astype(F32)).astype(BF16)

    grid_spec = pltpu.PrefetchScalarGridSpec(
        num_scalar_prefetch=1, grid=(N_CHIPS, per),
        in_specs=[pl.BlockSpec((None, tr, cols), lambda k, i, c: (k, c[0] * per + i, 0)),
                  pl.BlockSpec((None, tr, cols), lambda k, i, c: (k, i, 0))],
        out_specs=pl.BlockSpec((None, tr, cols), lambda k, i, c: (k, i, 0)))
    return pl.pallas_call(
        body, name=name, out_shape=_sds((N_CHIPS, rh, cols), BF16), grid_spec=grid_spec,
        compiler_params=_params(("parallel", "parallel")),
    )(core.reshape(1).astype(jnp.int32), g, recv)


def chip_sum(pair, recv, chip, core, name):
    _, rh, cols = pair.shape
    tr = _sum_rows(rh, cols)

    def body(p_ref, own_ref, r_ref, o_ref):
        del p_ref
        acc = own_ref[...].astype(F32)
        for j in range(N_CHIPS - 1):
            acc = acc + r_ref[j].astype(F32)
        o_ref[...] = acc

    grid_spec = pltpu.PrefetchScalarGridSpec(
        num_scalar_prefetch=1, grid=(rh // tr,),
        in_specs=[pl.BlockSpec((None, tr, cols), lambda i, p: (p[0], i, 0)),
                  pl.BlockSpec((N_CHIPS - 1, tr, cols), lambda i, p: (0, i, 0))],
        out_specs=pl.BlockSpec((None, tr, cols), lambda i, p: (p[1], i, 0)))
    return pl.pallas_call(
        body, name=name, out_shape=_sds((2, rh, cols)), grid_spec=grid_spec,
        compiler_params=_params(("parallel",)),
    )(jnp.stack([chip, core]).astype(jnp.int32), pair, recv)


def _place():
    return lax.axis_index("x"), lax.axis_index("y"), lax.axis_index("c")


def _other_chips(x, y):
    return [(x, 1 - y), (1 - x, y), (1 - x, 1 - y)]


_HBM = pl.BlockSpec(memory_space=pltpu.HBM)


def all_gather8(v, name):
    m, n = v.shape

    def body(x_ref, out_ref, send_sems, recv_sems, local_sem):
        x, y, c = _place()
        me, sibling = (x, y, c), (x, y, 1 - c)
        chips = _other_chips(x, y)

        def rows(px, py, pc):
            return out_ref.at[pl.ds((4 * px + 2 * py + pc) * m, m), :]

        def copy(k, block, to, src=None):
            return pltpu.make_async_remote_copy(
                src_ref=rows(*block) if src is None else src, dst_ref=rows(*block),
                send_sem=send_sems.at[k], recv_sem=recv_sems.at[k], device_id=to, device_id_type=MESH)

        mine = pltpu.make_async_copy(x_ref, rows(*me), local_sem)
        mine.start()
        first = [copy(0, me, sibling, src=x_ref)]
        first += [copy(1 + j, me, (*chip, c), src=x_ref) for j, chip in enumerate(chips)]
        for cp in first:
            cp.start()
        passed = [copy(4 + j, (*chip, c), sibling) for j, chip in enumerate(chips)]
        for j, chip in enumerate(chips):
            copy(1 + j, (*chip, c), me).wait_recv()
            passed[j].start()
        copy(0, sibling, me).wait_recv()
        for j, chip in enumerate(chips):
            copy(4 + j, (*chip, 1 - c), me).wait_recv()
        for cp in first + passed:
            cp.wait_send()
        mine.wait()

    return pl.pallas_call(
        body, name=name, out_shape=_sds((N_DEV * m, n), v.dtype),
        in_specs=[pl.BlockSpec(memory_space=pltpu.VMEM)], out_specs=pl.BlockSpec(memory_space=pltpu.VMEM),
        scratch_shapes=[pltpu.SemaphoreType.DMA((7,)), pltpu.SemaphoreType.DMA((7,)), pltpu.SemaphoreType.DMA],
        compiler_params=_params(None, VMEM_BIG),
    )(v)


def _comm_call(body, name, ins, out_shapes, nsem, aliases=None):
    return pl.pallas_call(
        body, name=name, out_shape=tuple(out_shapes), in_specs=[_HBM] * len(ins), out_specs=tuple([_HBM] * len(out_shapes)),
        scratch_shapes=[pltpu.SemaphoreType.DMA((nsem,)), pltpu.SemaphoreType.DMA((nsem,))],
        input_output_aliases=aliases or {},
    )(*ins)


def _remote(src, dst, send_sems, recv_sems, k, to):
    return pltpu.make_async_remote_copy(src_ref=src, dst_ref=dst, send_sem=send_sems.at[k], recv_sem=recv_sems.at[k],
                                        device_id=to, device_id_type=MESH)


def _half(core, rh):
    return pl.ds(pl.multiple_of(core * rh, 16), rh)


def swap_halves(gs, name):
    n = len(gs)

    def body(*refs):
        ins, outs, (send_sems, recv_sems) = refs[:n], refs[n:2 * n], refs[2 * n:]
        x, y, c = _place()
        copies = []
        for i in range(n):
            theirs = _half(1 - c, ins[i].shape[1] // 2)
            cp = _remote(ins[i].at[:, theirs], outs[i], send_sems, recv_sems, i, (x, y, 1 - c))
            cp.start()
            copies.append(cp)
        for cp in copies:
            cp.wait()

    return _comm_call(body, name, gs, [_sds((g.shape[0], g.shape[1] // 2, g.shape[2]), g.dtype) for g in gs], n)


def join_halves(bufs, name):
    n = len(bufs)

    def body(*refs):
        ins, outs, (send_sems, recv_sems) = refs[:n], refs[n:2 * n], refs[2 * n:]
        x, y, c = _place()
        copies = []
        for i in range(n):
            cp = _remote(ins[i].at[c], outs[i].at[c], send_sems, recv_sems, i, (x, y, 1 - c))
            cp.start()
            copies.append(cp)
        for i in range(n):
            theirs = outs[i].at[1 - c]
            _remote(theirs, theirs, send_sems, recv_sems, i, (x, y, 1 - c)).wait_recv()
        for cp in copies:
            cp.wait_send()

    return _comm_call(body, name, bufs, [_sds(b.shape, b.dtype) for b in bufs], n, {i: i for i in range(n)})


def forward_halves(lands, name):
    n = len(lands)

    def body(*refs):
        ins, outs, (send_sems, recv_sems) = refs[:n], refs[n:2 * n], refs[2 * n:]
        x, y, c = _place()
        sibling = (x, y, 1 - c)
        chips = _other_chips(x, y)
        copies = []
        for i in range(n):
            mine = _half(c, ins[i].shape[1] // 2)
            for j, (px, py) in enumerate(chips):
                cp = _remote(ins[i].at[2 * px + py, mine], outs[i].at[2 * px + py, mine], send_sems, recv_sems, 3 * i + j, sibling)
                cp.start()
                copies.append(cp)
        for i in range(n):
            theirs = _half(1 - c, ins[i].shape[1] // 2)
            for j, (px, py) in enumerate(chips):
                landed = outs[i].at[2 * px + py, theirs]
                _remote(landed, landed, send_sems, recv_sems, 3 * i + j, sibling).wait_recv()
        for cp in copies:
            cp.wait_send()

    return _comm_call(body, name, lands, [_sds(b.shape, b.dtype) for b in lands], 3 * n, {i: i for i in range(n)})


_SEM = pl.BlockSpec(memory_space=pltpu.SEMAPHORE)
_EFFECT = pltpu.SideEffectType.DATAFLOW_SIDE_EFFECTING


def _gather_copies(srcs, lands, send_sems, recv_sems):
    x, y, c = _place()
    copies = []
    for i in range(len(srcs)):
        mine = _half(c, srcs[i].shape[0] // 2)
        for j, chip in enumerate(_other_chips(x, y)):
            copies.append(_remote(srcs[i].at[mine], lands[i].at[2 * x + y, mine], send_sems, recv_sems, 3 * i + j, (*chip, c)))
    return copies


def _exchange_copies(srcs, lands, send_sems, recv_sems):
    x, y, c = _place()
    copies = []
    for i in range(len(srcs)):
        for j, (px, py) in enumerate(_other_chips(x, y)):
            copies.append(_remote(srcs[i].at[2 * px + py], lands[i].at[j], send_sems, recv_sems, 3 * i + j, (px, py, c)))
    return copies


def _everyone_copies(srcs, lands, send_sems, recv_sems):
    x, y, c = _place()
    flip = lambda v, b: 1 - v if b else v
    dst = lands[0].at[4 * x + 2 * y + c]
    return [_remote(srcs[0], dst, send_sems, recv_sems, j - 1, (flip(x, j & 4), flip(y, j & 2), flip(c, j & 1)))
            for j in range(1, N_DEV)]


GATHER = (_gather_copies, 3)
EXCHANGE = (_exchange_copies, 3)
EVERYONE = (_everyone_copies, N_DEV - 1)


def split_start(name, plan, srcs, land_shapes, after=()):
    copies_fn, per_source = plan
    n, m, k = len(srcs), len(land_shapes), len(after)
    ncopies = per_source * n

    def body(*refs):
        src_refs, land_refs = refs[:n], refs[n:n + m]
        send_sems, recv_sems = refs[n + m + k], refs[n + m + k + 1]
        token = refs[-1]
        for cp in copies_fn(src_refs, land_refs, send_sems, recv_sems):
            cp.start()
        token[...] = jnp.zeros_like(token)

    hbm = lambda s: pltpu.HBM(tuple(s.shape), s.dtype)
    outs = pl.pallas_call(
        body, name=name,
        out_shape=(pltpu.SemaphoreType.DMA((ncopies,)), pltpu.SemaphoreType.DMA((ncopies,)), *[hbm(s) for s in srcs],
                   *[hbm(s) for s in land_shapes], _sds((8, LANES))),
        in_specs=[_HBM] * (n + m) + [pl.BlockSpec(memory_space=pl.ANY)] * k,
        out_specs=(_SEM, _SEM, *([_HBM] * (n + m)), pl.BlockSpec(memory_space=pltpu.VMEM)),
        input_output_aliases={i: 2 + i for i in range(n + m)},
        compiler_params=pltpu.CompilerParams(has_side_effects=_EFFECT),
    )(*[pltpu.with_memory_space_constraint(s, pltpu.HBM) for s in srcs],
      *[pltpu.with_memory_space_constraint(lax.empty(tuple(s.shape), s.dtype), pltpu.HBM) for s in land_shapes], *after)
    handle = (outs[0], outs[1], list(outs[2:2 + n]), list(outs[2 + n:2 + n + m]))
    return handle, outs[-1][0, 0]


def split_wait(name, plan, handle, after):
    copies_fn, _ = plan
    send_sems, recv_sems, srcs, lands = handle
    n, m = len(srcs), len(lands)
    after = list(after) if isinstance(after, (list, tuple)) else [after]

    def body(*refs):
        src_refs, land_refs = refs[:n], refs[n:n + m]
        for cp in copies_fn(src_refs, land_refs, refs[n + m], refs[n + m + 1]):
            cp.wait_send()
            cp.wait_recv()

    hbm = lambda s: pltpu.HBM(tuple(s.shape), s.dtype)
    outs = pl.pallas_call(
        body, name=name, out_shape=tuple(hbm(s) for s in srcs + lands),
        in_specs=[_HBM] * (n + m) + [_SEM, _SEM] + [pl.BlockSpec(memory_space=pl.ANY)] * len(after),
        out_specs=tuple([_HBM] * (n + m)), input_output_aliases={i: i for i in range(n + m)},
        compiler_params=pltpu.CompilerParams(has_side_effects=_EFFECT),
    )(*srcs, *lands, send_sems, recv_sems, *after)
    return list(outs[:n]), list(outs[n:])


_CD_PAD = C_Q_RANK + C_KV_RANK + HEAD_PAD + 2 * _DW


def chip_major(w, groups=N_CHIPS):
    r, c = w.shape
    return w.reshape(r, groups, c // groups).transpose(1, 0, 2)


def from_chip_major(w):
    g, r, c = w.shape
    return w.transpose(1, 0, 2).reshape(r, g * c)


def _cd_in_pad(w):
    a = C_Q_RANK + C_KV_RANK
    z = lambda n: jnp.zeros((w.shape[0], n), w.dtype)
    return jnp.concatenate([w[:, :a], z(C_NOPE), w[:, a:a + C_ROPE], z(HEAD_PAD - C_NOPE - C_ROPE), w[:, a + C_ROPE:]], axis=1)


def _cd_in_unpad(w):
    a = C_Q_RANK + C_KV_RANK
    return jnp.concatenate([w[:, :a], w[:, a + C_NOPE:a + C_NOPE + C_ROPE], w[:, a + HEAD_PAD:]], axis=1)


def _pad_heads(w, width):
    r = w.shape[0]
    w = w.reshape(r, C_HEADS, width)
    return jnp.pad(w, ((0, 0), (0, 0), (0, HEAD_PAD - width))).reshape(r, _HW)


def _unpad_heads(w, width):
    r = w.shape[0]
    return w.reshape(r, C_HEADS, HEAD_PAD)[:, :, :width].reshape(r, C_HEADS * width)


_MATMUL_WEIGHTS = ("ab_w_in", "ab_w_out", "cd_w_in", "c_w_uq", "c_w_ukv", "cd_w_out", "ffn_w_up", "ffn_w_down")
_LAYER_STACKED = ("norm1_g", "norm2_g", "ffn_w_up", "ffn_conv_w", "ffn_w_down")
_ROW_VECTORS = ("b_scale", "c_q_norm_g", "c_kv_norm_g", "d_ln_g", "d_ln_b")


def full_to_local(p):
    q = {}
    for k, v in p.items():
        if k == "final_norm_g":
            v = v.reshape(1, -1)
        elif k not in _LAYER_STACKED and k not in _ROW_VECTORS:
            v = v[0]
        q[k] = v.astype(BF16) if k in _MATMUL_WEIGHTS else v
    return q


def local_to_full(g):
    q = {}
    for k, v in g.items():
        if k == "final_norm_g":
            q[k] = v.reshape(-1)
        elif k not in _LAYER_STACKED and k not in _ROW_VECTORS:
            q[k] = v[None]
        else:
            q[k] = v
    return q


def prepare_weights(p):
    q = dict(p)
    q["cd_w_in"] = _cd_in_pad(p["cd_w_in"])
    q["c_w_uq"] = _pad_heads(p["c_w_uq"], C_NOPE + C_ROPE)
    ukv = p["c_w_ukv"].reshape(C_KV_RANK, C_HEADS, C_NOPE + C_V)
    q["c_w_uk"] = _pad_heads(ukv[:, :, :C_NOPE].reshape(C_KV_RANK, -1), C_NOPE)
    q["c_w_uv"] = _pad_heads(ukv[:, :, C_NOPE:].reshape(C_KV_RANK, -1), C_V)
    wo = p["cd_w_out"]
    att_rows = jnp.pad(wo[:C_HEADS * C_V].reshape(C_HEADS, C_V, D_MODEL), ((0, 0), (0, HEAD_PAD - C_V), (0, 0)))
    q["cd_w_out"] = jnp.concatenate([att_rows.reshape(_HW, D_MODEL), wo[C_HEADS * C_V:]], axis=0)
    return q


def unprepare_grads(g):
    q = dict(g)
    q["cd_w_in"] = _cd_in_unpad(g["cd_w_in"])
    q["c_w_uq"] = _unpad_heads(g["c_w_uq"], C_NOPE + C_ROPE)
    uk = g.pop("c_w_uk").reshape(C_KV_RANK, C_HEADS, HEAD_PAD)[:, :, :C_NOPE]
    uv = g.pop("c_w_uv").reshape(C_KV_RANK, C_HEADS, HEAD_PAD)[:, :, :C_V]
    q.pop("c_w_uk", None)
    q.pop("c_w_uv", None)
    q["c_w_ukv"] = jnp.concatenate([uk, uv], axis=-1).reshape(C_KV_RANK, C_HEADS * (C_NOPE + C_V))
    wo = g["cd_w_out"]
    att = wo[:_HW].reshape(C_HEADS, HEAD_PAD, D_MODEL)[:, :C_V].reshape(C_HEADS * C_V, D_MODEL)
    q["cd_w_out"] = jnp.concatenate([att, wo[_HW:]], axis=0)
    return q


def rope_tables(positions):
    half = C_ROPE // 2
    inv_freq = ROPE_THETA ** (-jnp.arange(half, dtype=F32) / half)
    ang = positions.astype(F32)[:, None] * inv_freq
    cos, sin = jnp.cos(ang), jnp.sin(ang)
    s = positions.shape[0]
    z = lambda n: jnp.zeros((s, n), F32)
    cs = jnp.concatenate([jnp.ones((s, C_NOPE), F32), cos, cos, z(HEAD_PAD - C_NOPE - C_ROPE)], axis=1)
    s1 = jnp.concatenate([z(C_NOPE), -sin, z(HEAD_PAD - C_NOPE - half)], axis=1)
    s2 = jnp.concatenate([z(C_NOPE + half), sin, z(HEAD_PAD - C_NOPE - C_ROPE)], axis=1)
    return cs, s1, s2


def _mods(mod_l):
    return [mod_l[:, i * D_MODEL:(i + 1) * D_MODEL] for i in range(N_MOD)]


def _ffn_fwd(x1, w, l, sc2, sh2, g2, late_down=None):
    n2 = w["norm2_g"][l:l + 1]
    h2 = modnorm_fwd(x1, n2, sc2, sh2, f"modnorm2_fwd{l}")
    up_cols = 2 * D_FF // N_CHIPS
    zf = matmul(h2, w["ffn_w_up"][l], "nn", BF16, f"ffn_up{l}", gb=N_CHIPS, go=2, tn=up_cols)
    a = ffn_act_fwd(zf, w["ffn_conv_w"][l], f"ffn_act_fwd{l}")
    if late_down is not None:
        late_down(a)
    f = matmul(a, w["ffn_w_down"][l], "nn", F32, f"ffn_down{l}", tk=D_FF)
    x2 = resid_fwd(x1, f, g2, f"resid2_fwd{l}")
    return x2, (h2, zf, a, f)


def _ffn_bwd(dres, x1, saved, w, l, sc2, g2):
    h2, zf, a, f = saved
    n2 = w["norm2_g"][l:l + 1]
    df, dg2 = gate_bwd(dres, f, g2, f"gate2_bwd{l}")
    up_cols = 2 * D_FF // N_CHIPS
    da = matmul(df, w["ffn_w_down"][l], "nt", BF16, f"ffn_down_dx{l}", tn=D_FF // 2)
    d_down = matmul(a, df, "tn", BF16, f"ffn_down_dw{l}", tm=D_FF // 2)
    dzf, d_conv = ffn_act_bwd(zf, w["ffn_conv_w"][l], da, f"ffn_act_bwd{l}")
    dh2 = matmul(dzf, w["ffn_w_up"][l], "nt", F32, f"ffn_up_dx{l}", ga=2, gb=N_CHIPS, tk=up_cols, tn=D_MODEL)
    d_up = matmul(h2, dzf, "tn", BF16, f"ffn_up_dw{l}", gb=2, go=N_CHIPS, tn=up_cols)
    dres, dsh2, dsc2, dn2 = norm_bwd(x1, dh2, n2, sc2, dres, f"norm2_bwd{l}")
    d_conv = d_conv.transpose(1, 0, 2).reshape(3, 2 * D_FF)
    return dres, dict(ffn_w_down=d_down, ffn_conv_w=d_conv, ffn_w_up=d_up, norm2_g=dn2), (dsh2, dsc2, dg2)


def mixer0_fwd(x0, sh1, sc1, g1, w):
    h = modnorm_fwd(x0, w["norm1_g"][0:1], sc1, sh1, "modnorm1_fwd0")
    z = matmul(h, w["ab_w_in"], "nn", BF16, "ab_in", gb=N_CHIPS)
    ya = gconv_fwd(z, w["a_conv_w"])
    yb = pool_fwd(z, w["b_mix_w"], w["b_scale"])
    ycat = jnp.concatenate([ya] + yb, axis=1)
    y = matmul(ycat, w["ab_w_out"], "nn", F32, "ab_out", tn=D_MODEL)
    x1 = resid_fwd(x0, y, g1, "resid1_fwd0")
    return x1, (x0, h, z, ycat, y, sc1, g1)


def mixer0_bwd(dres, saved, w):
    x0, h, z, ycat, y, sc1, g1 = saved
    grads = {}
    dy, dg1 = gate_bwd(dres, y, g1, "gate1_bwd0")
    dycat = matmul(dy, w["ab_w_out"], "nt", BF16, "ab_out_dx")
    grads["ab_w_out"] = matmul(ycat, dy, "tn", BF16, "ab_out_dw")
    db, dc, da, d_conv = gconv_bwd(z, w["a_conv_w"], dycat)
    pb = pool_bwd(z, w["b_mix_w"], w["b_scale"], dycat)
    dz = jnp.concatenate([db, dc, da] + [t[0] for t in pb], axis=1)
    dh = matmul(dz, w["ab_w_in"], "nt", F32, "ab_in_dx", gb=N_CHIPS, tn=D_MODEL)
    grads["ab_w_in"] = matmul(h, dz, "tn", BF16, "ab_in_dw", go=N_CHIPS)
    dres, dsh1, dsc1, dn1 = norm_bwd(x0, dh, w["norm1_g"][0:1], sc1, dres, "norm1_bwd0")
    grads.update(a_conv_w=d_conv, b_mix_w=jnp.stack([t[1] for t in pb]),
                 b_scale=jnp.concatenate([t[2] for t in pb], axis=1), norm1_g=dn1)
    return dres, grads, (dsh1, dsc1, dg1)


def mixer1_fwd(x0, sh1, sc1, g1, ropes, w):
    cs, s1, s2 = ropes
    h = modnorm_fwd(x0, w["norm1_g"][1:2], sc1, sh1, "modnorm1_fwd1")
    z = matmul(h, w["cd_w_in"], "nn", BF16, "cd_in")
    bs_t = jnp.pad(w["d_b_s"].T, ((0, 0), (0, LANES - D_GROUPS)))
    qh, kh, vh = mla_pre_fwd(z, w["c_q_norm_g"], w["c_kv_norm_g"], w["c_w_uq"], w["c_w_uk"], w["c_w_uv"], cs, s1, s2)
    oh = attn_fwd(qh, kh, vh)
    yd = sgu_fwd(z, w["d_ln_g"], w["d_ln_b"], w["d_w_s"], bs_t)
    ycat = jnp.concatenate([oh, yd], axis=1)
    y = matmul(ycat, w["cd_w_out"], "nn", F32, "cd_out", tn=D_MODEL)
    x1 = resid_fwd(x0, y, g1, "resid1_fwd1")
    return x1, (x0, h, z, bs_t, qh, kh, vh, oh, ycat, y, sc1, g1)


def mixer1_bwd(dres, saved, ropes, w):
    cs, s1, s2 = ropes
    x0, h, z, bs_t, qh, kh, vh, oh, ycat, y, sc1, g1 = saved
    grads = {}
    dy, dg1 = gate_bwd(dres, y, g1, "gate1_bwd1")
    dycat = matmul(dy, w["cd_w_out"], "nt", BF16, "cd_out_dx")
    grads["cd_w_out"] = matmul(ycat, dy, "tn", F32, "cd_out_dw")
    dqh, dkh, dvh = attn_bwd(qh, kh, vh, oh, dycat, 0)
    dzq, d_uq, d_uk, d_uv, d_gq, d_gkv = mla_pre_bwd(
        z, w["c_q_norm_g"], w["c_kv_norm_g"], w["c_w_uq"], w["c_w_uk"], w["c_w_uv"], cs, s1, s2, dqh, dkh, dvh)
    dzu, dzv, d_ws, d_bs, d_lg, d_lb = sgu_bwd(z, w["d_ln_g"], w["d_ln_b"], w["d_w_s"], bs_t, dycat, _HW // _DW)
    dz = jnp.concatenate([dzq, dzu, dzv], axis=1)
    dh = matmul(dz, w["cd_w_in"], "nt", F32, "cd_in_dx", tn=D_MODEL)
    grads["cd_w_in"] = matmul(h, dz, "tn", F32, "cd_in_dw")
    dres, dsh1, dsc1, dn1 = norm_bwd(x0, dh, w["norm1_g"][1:2], sc1, dres, "norm1_bwd1")
    grads.update(c_w_uq=d_uq, c_w_uk=d_uk, c_w_uv=d_uv, c_q_norm_g=d_gq, c_kv_norm_g=d_gkv, d_w_s=d_ws,
                 d_b_s=d_bs[:, :D_GROUPS].T, d_ln_g=d_lg, d_ln_b=d_lb, norm1_g=dn1)
    return dres, grads, (dsh1, dsc1, dg1)


_PER_LAYER = ("ffn_w_down", "ffn_conv_w", "ffn_w_up", "norm2_g", "norm1_g")


def _merge_layer_grads(g0, g1):
    grads = {k: v for k, v in g0.items() if k not in _PER_LAYER}
    grads.update({k: v for k, v in g1.items() if k not in _PER_LAYER})
    for k in ("ffn_w_down", "ffn_w_up"):
        grads[k] = [g0[k], g1[k]]
    grads["ffn_conv_w"] = jnp.stack([g0["ffn_conv_w"], g1["ffn_conv_w"]])
    grads["norm1_g"] = jnp.concatenate([g0["norm1_g"], g1["norm1_g"]], axis=0)
    grads["norm2_g"] = jnp.concatenate([g0["norm2_g"], g1["norm2_g"]], axis=0)
    return grads


def local_step(x, tgt, mod, ropes, w):
    sh1a, sc1a, g1a, sh2a, sc2a, g2a = _mods(mod[0:1])
    sh1b, sc1b, g1b, sh2b, sc2b, g2b = _mods(mod[1:2])
    x1, mix0 = mixer0_fwd(x, sh1a, sc1a, g1a, w)
    x2, ffn0 = _ffn_fwd(x1, w, 0, sc2a, sh2a, g2a)
    x3, mix1 = mixer1_fwd(x2, sh1b, sc1b, g1b, ropes, w)
    x4, ffn1 = _ffn_fwd(x3, w, 1, sc2b, sh2b, g2b)
    dres, d_final, loss = final_fwd_bwd(x4, w["final_norm_g"], tgt)
    dres, gf1, dm2b = _ffn_bwd(dres, x3, ffn1, w, 1, sc2b, g2b)
    dres, gm1, dm1b = mixer1_bwd(dres, mix1, ropes, w)
    dres, gf0, dm2a = _ffn_bwd(dres, x1, ffn0, w, 0, sc2a, g2a)
    dres, gm0, dm1a = mixer0_bwd(dres, mix0, w)
    grads = _merge_layer_grads({**gf0, **gm0}, {**gf1, **gm1})
    grads["final_norm_g"] = d_final
    dmod = jnp.concatenate([jnp.concatenate(dm1a + dm2a, axis=1), jnp.concatenate(dm1b + dm2b, axis=1)], axis=0)
    return loss, dres, dmod, grads


_WEIGHTS = ("ada_w", "ada_b", "norm1_g", "norm2_g", "ab_w_in", "a_conv_w", "b_mix_w", "b_scale", "ab_w_out", "cd_w_in",
            "c_q_norm_g", "c_w_uq", "c_kv_norm_g", "c_w_ukv", "d_ln_g", "d_ln_b", "d_w_s", "d_b_s", "cd_w_out",
            "ffn_w_up", "ffn_conv_w", "ffn_w_down", "final_norm_g")
_INPUTS = ("x", "c", "positions") + _WEIGHTS + ("loss_target",) + tuple("m_" + n for n in _WEIGHTS) + tuple(
    "v_" + n for n in _WEIGHTS)

def _pack_rows(parts, rows, dtype):
    flat = jnp.concatenate([p.reshape(-1).astype(dtype) for p in parts])
    return jnp.pad(flat, (0, rows * LANES - flat.shape[0])).reshape(rows, LANES)


def _rows_major(w):
    r, c = w.shape
    return w.reshape(N_CHIPS, r // N_CHIPS, c)


def start_gather(shards, tag, after=()):
    lands = [_sds((N_CHIPS,) + s.shape, s.dtype) for s in shards]
    return split_start("gather_start_" + tag, GATHER, shards, lands, after)


def finish_gather(handle, chip, tag, after):
    shards, lands = split_wait("gather_wait_" + tag, GATHER, handle, after)
    lands = forward_halves(lands, "gather_forward_" + tag)
    return [lax.dynamic_update_index_in_dim(o, s, chip, 0) for o, s in zip(lands, shards)]


def start_reduce(gs, core, tag):
    recv = swap_halves(gs, "swap_halves_" + tag)
    pairs = [pair_sum(g, r, core, f"pair_sum_{tag}{i}") for i, (g, r) in enumerate(zip(gs, recv))]
    lands = [_sds((N_CHIPS - 1,) + p.shape[1:], p.dtype) for p in pairs]
    return split_start("exchange_start_" + tag, EXCHANGE, pairs, lands)


def finish_reduce(handle, chip, core, tag, after):
    pairs, others = split_wait("exchange_wait_" + tag, EXCHANGE, handle, after)
    halves = [chip_sum(p, o, chip, core, f"chip_sum_{tag}{i}") for i, (p, o) in enumerate(zip(pairs, others))]
    full = join_halves(halves, "join_halves_" + tag)
    return [f.reshape(f.shape[1] * 2, f.shape[2]) for f in full]


_SMALL_SHARDED = (("a_conv_w", (3, 128), 1), ("c_q_norm_g", (1, 64), 1), ("d_ln_g", (1, 128), 1), ("d_ln_b", (1, 128), 1),
                  ("ffn_conv_w", (2, 3, 2 * D_FF // N_CHIPS), 2))
_SMALL_GRADS = (("norm1_g", (2, D_MODEL)), ("norm2_g", (2, D_MODEL)), ("b_mix_w", (4, 128, 128)), ("b_scale", (1, 512)),
                ("c_kv_norm_g", (1, 128)), ("d_w_s", (4, 128, 128)), ("d_b_s", (4, 128)), ("final_norm_g", (1, D_MODEL)),
                ("a_conv_w", (3, 512)), ("c_q_norm_g", (1, 256)), ("d_ln_g", (1, 512)), ("d_ln_b", (1, 512)),
                ("ffn_conv_w", (2, 3, 2 * D_FF)))


def _size(shape):
    n = 1
    for d in shape:
        n *= d
    return n


def kernel(x, c, positions, ada_w, ada_b, norm1_g, norm2_g, ab_w_in, a_conv_w, b_mix_w, b_scale, ab_w_out, cd_w_in, c_q_norm_g, c_w_uq, c_kv_norm_g, c_w_ukv, d_ln_g, d_ln_b, d_w_s, d_b_s, cd_w_out, ffn_w_up, ffn_conv_w, ffn_w_down, final_norm_g, loss_target, m_ada_w, m_ada_b, m_norm1_g, m_norm2_g, m_ab_w_in, m_a_conv_w, m_b_mix_w, m_b_scale, m_ab_w_out, m_cd_w_in, m_c_q_norm_g, m_c_w_uq, m_c_kv_norm_g, m_c_w_ukv, m_d_ln_g, m_d_ln_b, m_d_w_s, m_d_b_s, m_cd_w_out, m_ffn_w_up, m_ffn_conv_w, m_ffn_w_down, m_final_norm_g, v_ada_w, v_ada_b, v_norm1_g, v_norm2_g, v_ab_w_in, v_a_conv_w, v_b_mix_w, v_b_scale, v_ab_w_out, v_cd_w_in, v_c_q_norm_g, v_c_w_uq, v_c_kv_norm_g, v_c_w_ukv, v_d_ln_g, v_d_ln_b, v_d_w_s, v_d_b_s, v_cd_w_out, v_ffn_w_up, v_ffn_conv_w, v_ffn_w_down, v_final_norm_g):
    args = (x, c, positions, ada_w, ada_b, norm1_g, norm2_g, ab_w_in, a_conv_w, b_mix_w, b_scale, ab_w_out, cd_w_in, c_q_norm_g, c_w_uq, c_kv_norm_g, c_w_ukv, d_ln_g, d_ln_b, d_w_s, d_b_s, cd_w_out, ffn_w_up, ffn_conv_w, ffn_w_down, final_norm_g, loss_target, m_ada_w, m_ada_b, m_norm1_g, m_norm2_g, m_ab_w_in, m_a_conv_w, m_b_mix_w, m_b_scale, m_ab_w_out, m_cd_w_in, m_c_q_norm_g, m_c_w_uq, m_c_kv_norm_g, m_c_w_ukv, m_d_ln_g, m_d_ln_b, m_d_w_s, m_d_b_s, m_cd_w_out, m_ffn_w_up, m_ffn_conv_w, m_ffn_w_down, m_final_norm_g, v_ada_w, v_ada_b, v_norm1_g, v_norm2_g, v_ab_w_in, v_a_conv_w, v_b_mix_w, v_b_scale, v_ab_w_out, v_cd_w_in, v_c_q_norm_g, v_c_w_uq, v_c_kv_norm_g, v_c_w_ukv, v_d_ln_g, v_d_ln_b, v_d_w_s, v_d_b_s, v_cd_w_out, v_ffn_w_up, v_ffn_conv_w, v_ffn_w_down, v_final_norm_g)
    a = dict(zip(_INPUTS, args, strict=True))
    xi, yi, ci = _place()
    chip = 2 * xi + yi
    dev = 4 * xi + 2 * yi + ci
    x = a["x"][0]
    tgt = a["loss_target"][0]

    bf = lambda t: t.astype(BF16)
    mix0_handle, tok = start_gather([bf(a["ab_w_in"][0]), bf(a["ab_w_out"][0])], "mix0")

    small_parts = [a["c"] + tok] + [a[n] for n, _, _ in _SMALL_SHARDED]
    rows1 = -(-sum(p.size for p in small_parts) // LANES // 8) * 8
    g1 = all_gather8(_pack_rows(small_parts, rows1, F32), "gather_small").reshape(N_DEV, rows1 * LANES)
    c_all = g1[:, :D_MODEL]
    per_chip = g1[0::2]
    small_full = {}
    off = D_MODEL
    for n, shp, axis in _SMALL_SHARDED:
        piece = per_chip[:, off:off + _size(shp)].reshape((N_CHIPS,) + shp)
        small_full[n] = jnp.concatenate([piece[k] for k in range(N_CHIPS)], axis=axis)
        off += _size(shp)

    merge = lambda t: t.reshape(t.shape[0] * t.shape[1], t.shape[2])
    w = dict(norm1_g=a["norm1_g"], norm2_g=a["norm2_g"], b_mix_w=a["b_mix_w"][0], b_scale=a["b_scale"],
             c_kv_norm_g=a["c_kv_norm_g"], d_w_s=a["d_w_s"][0], d_b_s=a["d_b_s"][0],
             final_norm_g=a["final_norm_g"].reshape(1, D_MODEL), **small_full)

    ncol = N_MOD * D_MODEL // N_CHIPS
    ada_b_mine = lax.dynamic_slice_in_dim(a["ada_b"], chip * ncol, ncol, axis=1)
    mod_cols = ada_mod(c_all, a["ada_w"], ada_b_mine)
    g2_rows = all_gather8(mod_cols.reshape(-1, LANES), "gather_mod")
    g2 = g2_rows.reshape(N_DEV, 2, N_DEV, ncol)
    mod = lax.dynamic_index_in_dim(g2[0::2], dev, axis=2, keepdims=False)
    mod = mod.transpose(1, 0, 2).reshape(2, N_MOD * D_MODEL)

    late = [g2_rows]
    up0_handle, tok_a = start_gather([bf(a["ffn_w_up"][0])], "up0", late)
    down0_handle, tok_b = start_gather([bf(a["ffn_w_down"][0])], "down0", late)
    mix1_handle, tok_c = start_gather(
        [bf(a["cd_w_in"][0]), bf(a["c_w_uq"][0]), bf(a["c_w_ukv"][0]), bf(a["cd_w_out"][0])], "mix1", late)
    ffn1_handle, tok_d = start_gather([bf(a["ffn_w_up"][1]), bf(a["ffn_w_down"][1])], "ffn1", late)
    mod = mod + (tok_a + tok_b + tok_c + tok_d)

    ropes = rope_tables(a["positions"][0])
    cm16 = lambda t: chip_major(t).astype(BF16)
    sh1a, sc1a, g1a, sh2a, sc2a, g2a = _mods(mod[0:1])
    sh1b, sc1b, g1b, sh2b, sc2b, g2b = _mods(mod[1:2])

    w_in0, w_out0 = finish_gather(mix0_handle, chip, "mix0", mod)
    w.update(ab_w_in=w_in0, ab_w_out=merge(w_out0))
    x1, mix0 = mixer0_fwd(x, sh1a, sc1a, g1a, w)
    up0, = finish_gather(up0_handle, chip, "up0", x1)
    w.update(ffn_w_up=[up0, None], ffn_w_down=[None, None])

    def fetch_down0(act):
        down0, = finish_gather(down0_handle, chip, "down0", act)
        w["ffn_w_down"][0] = merge(down0)

    x2, ffn0 = _ffn_fwd(x1, w, 0, sc2a, sh2a, g2a, late_down=fetch_down0)
    cd_in, uq, ukv, cd_out = finish_gather(mix1_handle, chip, "mix1", x2)
    w.update(prepare_weights(dict(cd_w_in=from_chip_major(cd_in), c_w_uq=from_chip_major(uq), c_w_ukv=from_chip_major(ukv),
                                  cd_w_out=merge(cd_out))))
    x3, mix1 = mixer1_fwd(x2, sh1b, sc1b, g1b, ropes, w)
    up1, down1 = finish_gather(ffn1_handle, chip, "ffn1", x3)
    w.update(ffn_w_up=[up0, up1], ffn_w_down=[w["ffn_w_down"][0], merge(down1)])
    x4, ffn1 = _ffn_fwd(x3, w, 1, sc2b, sh2b, g2b)
    dres, d_final, loss = final_fwd_bwd(x4, w["final_norm_g"], tgt)

    dres, gf1, dm2b = _ffn_bwd(dres, x3, ffn1, w, 1, sc2b, g2b)
    ffn1_red, tok = start_reduce([gf1["ffn_w_up"], _rows_major(gf1["ffn_w_down"])], ci, "ffn1")
    dres, gm1, dm1b = mixer1_bwd(dres, mix1[:-1] + (mix1[-1] + tok,), ropes, w)
    gm1 = unprepare_grads(gm1)
    mix1_red, tok = start_reduce([cm16(gm1["cd_w_in"]), cm16(gm1["c_w_uq"]), cm16(gm1["c_w_ukv"]),
                                  _rows_major(gm1["cd_w_out"]).astype(BF16)], ci, "mix1")
    red_up1, red_down1 = finish_reduce(ffn1_red, chip, ci, "ffn1", dres)
    dres, gf0, dm2a = _ffn_bwd(dres, x1, ffn0, w, 0, sc2a, g2a + tok)
    ffn0_red, tok = start_reduce([gf0["ffn_w_up"], _rows_major(gf0["ffn_w_down"])], ci, "ffn0")
    red_cd_in, red_uq, red_ukv, red_cd_out = finish_reduce(mix1_red, chip, ci, "mix1", dres)
    grad_x, gm0, dm1a = mixer0_bwd(dres, mix0[:-1] + (mix0[-1] + tok,), w)
    grads = _merge_layer_grads({**gf0, **gm0}, {**gf1, **gm1})
    grads["final_norm_g"] = d_final
    dmod = jnp.concatenate([jnp.concatenate(dm1a + dm2a, axis=1), jnp.concatenate(dm1b + dm2b, axis=1)], axis=0)

    parts3 = [dmod] + [grads[n] for n, _ in _SMALL_GRADS] + [loss[0, 0]]
    rows3 = -(-sum(p.size for p in parts3) // LANES // 8) * 8
    small_handle, tok = split_start("small_grads_start", EVERYONE, [_pack_rows(parts3, rows3, F32)],
                                    [_sds((N_DEV, rows3, LANES))])
    mix0_red, _ = start_reduce([gm0["ab_w_in"], _rows_major(gm0["ab_w_out"]) + tok.astype(BF16)], ci, "mix0")
    red_up0, red_down0 = finish_reduce(ffn0_red, chip, ci, "ffn0", grad_x)
    out_grads = dict(cd_w_in=red_cd_in, c_w_uq=red_uq, c_w_ukv=red_ukv, cd_w_out=red_cd_out)
    per_layer = dict(ffn_w_up=(red_up0, red_up1), ffn_w_down=(red_down0, red_down1))
    updates = {}

    def update(n):
        if n in per_layer:
            updates[n] = adamw_layers(a[n], *per_layer[n], a["m_" + n], a["v_" + n], "adamw_" + n)
        else:
            updates[n] = adamw(a[n], out_grads[n].reshape(a[n].shape), a["m_" + n], a["v_" + n], "adamw_" + n,
                               copy_grad=n != "ada_w")

    early =("ffn_w_up", "ffn_w_down", "cd_w_in", "c_w_uq", "c_w_ukv", "cd_w_out")
    for n in early:
        update(n)
    (mine,), (landed,) = split_wait("small_grads_wait", EVERYONE, small_handle, [updates[n][1] for n in early])
    g3 = lax.dynamic_update_index_in_dim(landed, mine, dev, 0)
    summed = sum8(g3).reshape(-1)
    nmod = 2 * N_MOD * D_MODEL
    out_grads["ada_b"] = summed[:nmod].reshape(2, N_MOD * D_MODEL)
    off = nmod
    for n, shp in _SMALL_GRADS:
        out_grads[n] = summed[off:off + _size(shp)].reshape(shp)
        off += _size(shp)
    loss = summed[off]
    for n, shp, axis in _SMALL_SHARDED:
        width = out_grads[n].shape[-1] // N_CHIPS
        out_grads[n] = lax.dynamic_slice_in_dim(out_grads[n], chip * width, width, axis=out_grads[n].ndim - 1)
    dmod_all = g3.reshape(N_DEV, rows3 * LANES)[:, :nmod].reshape(N_DEV, 2, N_MOD * D_MODEL)
    dmod_mine = lax.dynamic_slice_in_dim(dmod_all, chip * ncol, ncol, axis=2).transpose(1, 0, 2)
    out_grads["ada_w"] = ada_grad(c_all.T, dmod_mine)

    red_in0, red_out0 = finish_reduce(mix0_red, chip, ci, "mix0", out_grads["ada_w"])
    out_grads.update(ab_w_in=red_in0, ab_w_out=red_out0)

    for n in ("ada_w", "ab_w_in", "ab_w_out"):
        update(n)
    small = [n for n in _WEIGHTS if n not in updates]
    for n, res in zip(small, adamw_small([a[n] for n in small], [out_grads[n].reshape(a[n].shape) for n in small],
                                         [a["m_" + n] for n in small], [a["v_" + n] for n in small])):
        updates[n] = res
    return (loss, grad_x[None], *[updates[n][i] for i in range(4) for n in _WEIGHTS])
```

```python
import functools

import jax
import jax.numpy as jnp
from jax import lax
from jax.experimental import pallas as pl
from jax.experimental.pallas import tpu as pltpu

F32 = jnp.float32
BF16 = jnp.bfloat16
EPS = 1e-6
D_MODEL = 1024
N_MOD = 6
A_WIDTH = 512
B_GROUPS = 4
POOL_WINDOWS = (2, 4, 8, 16)
C_HEADS = 8
C_NOPE = 64
C_ROPE = 32
C_V = 64
C_Q_RANK = 256
C_KV_RANK = 128
HEAD_PAD = 128
ROPE_THETA = 10000.0
D_GROUPS = 4
D_CHUNK = 128
D_FF = 2816
FF_UNIT = 128
ADAM_LR = 0.001
ADAM_B1 = 0.9
ADAM_B2 = 0.999
ADAM_EPS = 1e-08
ADAM_WD = 0.01
ADAM_STEP = 10
N_CHIPS = 4
N_DEV = 8
LANES = 128
VMEM_BIG = 56 * 1024 * 1024
MESH = pl.DeviceIdType.MESH


def _sds(shape, dtype=F32):
    return jax.ShapeDtypeStruct(tuple(shape), dtype)


def _tile(n, cap, mult=128):
    if n <= cap:
        return n
    best = None
    for t in range(mult, cap + 1, mult):
        if n % t == 0:
            best = t
    assert best is not None, (n, cap, mult)
    return best


def _params(dims=None, vmem=None):
    return pltpu.CompilerParams(dimension_semantics=dims, vmem_limit_bytes=vmem)


def _shift_down(v, k):
    r = pltpu.roll(v, k, axis=0)
    t = lax.broadcasted_iota(jnp.int32, v.shape, 0)
    return jnp.where(t >= k, r, 0.0)


def _shift_up(v, k):
    n = v.shape[0]
    r = pltpu.roll(v, n - k, axis=0)
    t = lax.broadcasted_iota(jnp.int32, v.shape, 0)
    return jnp.where(t < n - k, r, 0.0)


def _sigmoid(v):
    return 1.0 / (1.0 + jnp.exp(-v))


_GELU_C = 0.7978845608028654
_GELU_A = 0.044715


def _gelu(v):
    return 0.5 * v * (1.0 + jnp.tanh(_GELU_C * (v + _GELU_A * v * v * v)))


def _gelu_grad(v):
    th = jnp.tanh(_GELU_C * (v + _GELU_A * v * v * v))
    return 0.5 * (1.0 + th) + 0.5 * v * (1.0 - th * th) * _GELU_C * (1.0 + 3.0 * _GELU_A * v * v)


_NN = (((1,), (0,)), ((), ()))
_NT = (((1,), (1,)), ((), ()))
_TN = (((0,), (0,)), ((), ()))


def _dot(a, b, dims=_NN):
    return lax.dot_general(a, b, dims, preferred_element_type=F32)


def _logical(t, groups):
    return (t.shape[-2], t.shape[-1] * groups)


def _block(tr, tc, groups, cols, where):
    if groups == 1:
        return pl.BlockSpec((tr, tc), where)
    per = cols // groups // tc

    def index(i, j, s):
        r, c = where(i, j, s)
        return (c // per, r, c % per)

    return pl.BlockSpec((None, tr, tc), index)


def matmul(a, b, mode, out_dtype, name, ga=1, gb=1, go=1, tm=None, tn=None, tk=None):
    (ar, ac), (br, bc) = _logical(a, ga), _logical(b, gb)
    if mode == "nn":
        m, k, n = ar, ac, bc
        a_col, b_col = "k", "n"
    elif mode == "nt":
        m, k, n = ar, ac, br
        a_col, b_col = "k", "k"
    else:
        k, m, n = ar, ac, bc
        a_col, b_col = "m", "n"
    limit = {"m": m, "n": n // go, "k": k}
    limit[a_col] = min(limit[a_col], ac // ga)
    limit[b_col] = min(limit[b_col], bc // gb)
    tm = tm or _tile(limit["m"], 1024, 128 if mode == "tn" else 16)
    tn = tn or _tile(limit["n"], 512)
    tk = tk or _tile(limit["k"], 2048, 16 if mode == "tn" else 128)
    nk = k // tk
    if mode == "nn":
        a_spec = _block(tm, tk, ga, ac, lambda i, j, s: (i, s))
        b_spec = _block(tk, tn, gb, bc, lambda i, j, s: (s, j))
        dims = _NN
    elif mode == "nt":
        a_spec = _block(tm, tk, ga, ac, lambda i, j, s: (i, s))
        b_spec = _block(tn, tk, gb, bc, lambda i, j, s: (j, s))
        dims = _NT
    else:
        a_spec = _block(tk, tm, ga, ac, lambda i, j, s: (s, i))
        b_spec = _block(tk, tn, gb, bc, lambda i, j, s: (s, j))
        dims = _TN
    o_spec = _block(tm, tn, go, n, lambda i, j, s: (i, j))
    out_shape = _sds((m, n), out_dtype) if go == 1 else _sds((go, m, n // go), out_dtype)

    def body(a_ref, b_ref, o_ref, acc_ref):
        s = pl.program_id(2)

        @pl.when(s == 0)
        def _():
            acc_ref[...] = jnp.zeros_like(acc_ref)

        acc_ref[...] += _dot(a_ref[...], b_ref[...], dims)

        @pl.when(s == nk - 1)
        def _():
            o_ref[...] = acc_ref[...].astype(o_ref.dtype)

    return pl.pallas_call(
        body, name=name, out_shape=out_shape, grid=(m // tm, n // tn, nk),
        in_specs=[a_spec, b_spec], out_specs=o_spec,
        scratch_shapes=[pltpu.VMEM((tm, tn), F32)],
        compiler_params=_params(("parallel", "parallel", "arbitrary"), VMEM_BIG),
    )(a, b)


def _rows(tm, n):
    return pl.BlockSpec((tm, n), lambda i: (i, 0))


def _vec(n):
    return pl.BlockSpec((1, n), lambda i: (0, 0))


def modnorm_fwd(x, g, sc, sh, name):
    s, d = x.shape
    tm = _tile(s, 256, 8)

    def body(x_ref, g_ref, sc_ref, sh_ref, o_ref):
        xv = x_ref[...]
        r = lax.rsqrt(jnp.mean(xv * xv, axis=-1, keepdims=True) + EPS)
        o_ref[...] = ((xv * r) * g_ref[...] * (1.0 + sc_ref[...]) + sh_ref[...]).astype(BF16)

    return pl.pallas_call(
        body, name=name, out_shape=_sds((s, d), BF16), grid=(s // tm,),
        in_specs=[_rows(tm, d), _vec(d), _vec(d), _vec(d)], out_specs=_rows(tm, d),
        compiler_params=_params(("parallel",)),
    )(x, g, sc, sh)


def norm_bwd(x, dh, g, sc, dres, name):
    s, d = x.shape
    tm = _tile(s, 256, 8)
    nsteps = s // tm

    def body(x_ref, dh_ref, g_ref, sc_ref, dr_ref, dx_ref, dsh_ref, dsc_ref, dg_ref, a2_ref):
        i = pl.program_id(0)

        @pl.when(i == 0)
        def _():
            dsh_ref[...] = jnp.zeros_like(dsh_ref)
            a2_ref[...] = jnp.zeros_like(a2_ref)

        xv = x_ref[...]
        dh = dh_ref[...]
        r = lax.rsqrt(jnp.mean(xv * xv, axis=-1, keepdims=True) + EPS)
        xh = xv * r
        dsh_ref[...] += jnp.sum(dh, axis=0, keepdims=True)
        a2_ref[...] += jnp.sum(dh * xh, axis=0, keepdims=True)
        dxh = dh * (g_ref[...] * (1.0 + sc_ref[...]))
        dx = r * (dxh - xh * jnp.mean(dxh * xh, axis=-1, keepdims=True))
        dx_ref[...] = dr_ref[...] + dx

        @pl.when(i == nsteps - 1)
        def _():
            dsc_ref[...] = a2_ref[...] * g_ref[...]
            dg_ref[...] = a2_ref[...] * (1.0 + sc_ref[...])

    return pl.pallas_call(
        body, name=name, out_shape=(_sds((s, d)), _sds((1, d)), _sds((1, d)), _sds((1, d))), grid=(nsteps,),
        in_specs=[_rows(tm, d), _rows(tm, d), _vec(d), _vec(d), _rows(tm, d)],
        out_specs=(_rows(tm, d), _vec(d), _vec(d), _vec(d)),
        scratch_shapes=[pltpu.VMEM((1, d), F32)],
        compiler_params=_params(("arbitrary",)),
    )(x, dh, g, sc, dres)


def resid_modnorm_fwd(x, y, gate, g, sc, sh, name):
    s, d = x.shape
    tm = _tile(s, 256, 8)

    def body(x_ref, y_ref, gate_ref, g_ref, sc_ref, sh_ref, xo_ref, h_ref):
        xv = x_ref[...] + gate_ref[...] * y_ref[...]
        xo_ref[...] = xv
        r = lax.rsqrt(jnp.mean(xv * xv, axis=-1, keepdims=True) + EPS)
        h_ref[...] = ((xv * r) * g_ref[...] * (1.0 + sc_ref[...]) + sh_ref[...]).astype(BF16)

    return pl.pallas_call(
        body, name=name, out_shape=(_sds((s, d)), _sds((s, d), BF16)), grid=(s // tm,),
        in_specs=[_rows(tm, d), _rows(tm, d), _vec(d), _vec(d), _vec(d), _vec(d)], out_specs=(_rows(tm, d), _rows(tm, d)),
        compiler_params=_params(("parallel",)),
    )(x, y, gate, g, sc, sh)


def norm_gate_bwd(x, dh, g, sc, dres, y, gate, name):
    s, d = x.shape
    tm = _tile(s, 256, 8)
    nsteps = s // tm

    def body(x_ref, dh_ref, g_ref, sc_ref, dr_ref, y_ref, gate_ref, dx_ref, dsh_ref, dsc_ref, dg_ref, dy_ref,
             dgate_ref, a2_ref):
        i = pl.program_id(0)

        @pl.when(i == 0)
        def _():
            dsh_ref[...] = jnp.zeros_like(dsh_ref)
            a2_ref[...] = jnp.zeros_like(a2_ref)
            dgate_ref[...] = jnp.zeros_like(dgate_ref)

        xv = x_ref[...]
        dh = dh_ref[...]
        r = lax.rsqrt(jnp.mean(xv * xv, axis=-1, keepdims=True) + EPS)
        xh = xv * r
        dsh_ref[...] += jnp.sum(dh, axis=0, keepdims=True)
        a2_ref[...] += jnp.sum(dh * xh, axis=0, keepdims=True)
        dxh = dh * (g_ref[...] * (1.0 + sc_ref[...]))
        dr = dr_ref[...] + r * (dxh - xh * jnp.mean(dxh * xh, axis=-1, keepdims=True))
        dx_ref[...] = dr
        dy_ref[...] = (dr * gate_ref[...]).astype(BF16)
        dgate_ref[...] += jnp.sum(dr * y_ref[...], axis=0, keepdims=True)

        @pl.when(i == nsteps - 1)
        def _():
            dsc_ref[...] = a2_ref[...] * g_ref[...]
            dg_ref[...] = a2_ref[...] * (1.0 + sc_ref[...])

    vec = _sds((1, d))
    return pl.pallas_call(
        body, name=name, out_shape=(_sds((s, d)), vec, vec, vec, _sds((s, d), BF16), vec), grid=(nsteps,),
        in_specs=[_rows(tm, d), _rows(tm, d), _vec(d), _vec(d), _rows(tm, d), _rows(tm, d), _vec(d)],
        out_specs=(_rows(tm, d), _vec(d), _vec(d), _vec(d), _rows(tm, d), _vec(d)),
        scratch_shapes=[pltpu.VMEM((1, d), F32)],
        compiler_params=_params(("arbitrary",)),
    )(x, dh, g, sc, dres, y, gate)


def final_fused(x, f, gate, g, tgt):
    s, d = x.shape
    tm = _tile(s, 256, 8)

    def body(x_ref, f_ref, gate_ref, g_ref, t_ref, dx_ref, dg_ref, loss_ref, df_ref, dgate_ref):
        @pl.when(pl.program_id(0) == 0)
        def _():
            dg_ref[...] = jnp.zeros_like(dg_ref)
            loss_ref[...] = jnp.zeros_like(loss_ref)
            dgate_ref[...] = jnp.zeros_like(dgate_ref)

        fv, gatev, gv = f_ref[...], gate_ref[...], g_ref[...]
        xv = x_ref[...] + gatev * fv
        r = lax.rsqrt(jnp.mean(xv * xv, axis=-1, keepdims=True) + EPS)
        xh = xv * r
        e = xh * gv - t_ref[...]
        row = jnp.sum(e * e, axis=-1, keepdims=True) * (0.5 / d)
        loss_ref[...] += jnp.sum(row, axis=0, keepdims=True)
        dy = e * (1.0 / d)
        dg_ref[...] += jnp.sum(dy * xh, axis=0, keepdims=True)
        dxh = dy * gv
        dx = r * (dxh - xh * jnp.mean(dxh * xh, axis=-1, keepdims=True))
        dx_ref[...] = dx
        df_ref[...] = (dx * gatev).astype(BF16)
        dgate_ref[...] += jnp.sum(dx * fv, axis=0, keepdims=True)

    vec = _sds((1, d))
    return pl.pallas_call(
        body, name="final_fused", out_shape=(_sds((s, d)), vec, _sds((1, LANES)), _sds((s, d), BF16), vec),
        grid=(s // tm,),
        in_specs=[_rows(tm, d), _rows(tm, d), _vec(d), _vec(d), _rows(tm, d)],
        out_specs=(_rows(tm, d), _vec(d), _vec(LANES), _rows(tm, d), _vec(d)),
        compiler_params=_params(("arbitrary",)),
    )(x, f, gate, g, tgt)


def _taps(v):
    return _shift_down(v, 2), _shift_down(v, 1), v


def _conv3_taps(taps, w):
    return w[0:1, :] * taps[0] + w[1:2, :] * taps[1] + w[2:3, :] * taps[2]


def _conv3(v, w):
    return _conv3_taps(_taps(v), w)


def _conv3_t(dv, w):
    return w[0:1, :] * _shift_up(dv, 2) + w[1:2, :] * _shift_up(dv, 1) + w[2:3, :] * dv


def _conv3_dw_taps(dv, taps):
    return jnp.concatenate([jnp.sum(dv * t, axis=0, keepdims=True) for t in taps], axis=0)


def _conv3_dw(dv, v):
    return _conv3_dw_taps(dv, _taps(v))


def gconv_fwd(z, conv_w):
    s = z.shape[0]
    nb = A_WIDTH // LANES

    def body(b_ref, c_ref, a_ref, w_ref, o_ref):
        b, c, a = b_ref[...].astype(F32), c_ref[...].astype(F32), a_ref[...].astype(F32)
        o_ref[...] = (b * _conv3(c * a, w_ref[...])).astype(BF16)

    col = lambda off: pl.BlockSpec((s, LANES), lambda j: (0, off + j))
    return pl.pallas_call(
        body, name="gconv_fwd", out_shape=_sds((s, A_WIDTH), BF16), grid=(nb,),
        in_specs=[col(0), col(nb), col(2 * nb), pl.BlockSpec((3, LANES), lambda j: (0, j))],
        out_specs=pl.BlockSpec((s, LANES), lambda j: (0, j)),
        compiler_params=_params(("parallel",), VMEM_BIG),
    )(z, z, z, conv_w)


def gconv_bwd(z, conv_w, dycat):
    s = z.shape[0]
    nb = A_WIDTH // LANES

    def body(b_ref, c_ref, a_ref, w_ref, dy_ref, db_ref, dc_ref, da_ref, dw_ref):
        c, a, w, dy = c_ref[...].astype(F32), a_ref[...].astype(F32), w_ref[...], dy_ref[...].astype(F32)
        ca = c * a
        db_ref[...] = (dy * _conv3(ca, w)).astype(BF16)
        dconv = dy * b_ref[...].astype(F32)
        dw_ref[...] = _conv3_dw(dconv, ca)
        dca = _conv3_t(dconv, w)
        dc_ref[...] = (dca * a).astype(BF16)
        da_ref[...] = (dca * c).astype(BF16)

    col = lambda off: pl.BlockSpec((s, LANES), lambda j: (0, off + j))
    wspec = pl.BlockSpec((3, LANES), lambda j: (0, j))
    part = _sds((s, A_WIDTH), BF16)
    return pl.pallas_call(
        body, name="gconv_bwd", out_shape=(part, part, part, _sds((3, A_WIDTH))), grid=(nb,),
        in_specs=[col(0), col(nb), col(2 * nb), wspec, col(0)],
        out_specs=(col(0), col(0), col(0), wspec),
        compiler_params=_params(("parallel",), VMEM_BIG),
    )(z, z, z, conv_w, dycat)


def _pool_counts(s, w):
    t = lax.broadcasted_iota(jnp.int32, (s, 1), 0)
    return jnp.minimum(t + 1, w).astype(F32)


def _pooled(p, levels):
    acc = p
    for lv in range(levels):
        acc = acc + _shift_down(acc, 2 ** lv)
    return acc / _pool_counts(p.shape[0], 2 ** levels) - p


def pool_fwd(z, mix_w, scale):
    s = z.shape[0]

    def make(g):
        def body_g(p_ref, m_ref, sc_ref, o_ref):
            pooled = _pooled(p_ref[...].astype(F32), g + 1)
            y = _dot(pooled.astype(BF16), m_ref[...].astype(BF16))
            o_ref[...] = (y * sc_ref[...]).astype(BF16)
        return body_g

    outs = []
    for g in range(B_GROUPS):
        outs.append(pl.pallas_call(
            make(g), name=f"pool_fwd{g}", out_shape=_sds((s, LANES), BF16), grid=(1,),
            in_specs=[pl.BlockSpec((s, LANES), lambda i, g=g: (0, 3 * (A_WIDTH // LANES) + g)),
                      pl.BlockSpec((None, LANES, LANES), lambda i, g=g: (g, 0, 0)),
                      pl.BlockSpec((1, LANES), lambda i, g=g: (0, g))],
            out_specs=pl.BlockSpec((s, LANES), lambda i: (0, 0)),
            compiler_params=_params(("arbitrary",), VMEM_BIG),
        )(z, mix_w, scale))
    return outs


def pool_bwd(z, mix_w, scale, dycat):
    s = z.shape[0]

    def make(g):
        w = 2 ** (g + 1)

        def body_g(p_ref, m_ref, sc_ref, dy_ref, dp_ref, dm_ref, dsc_ref):
            pooled = _pooled(p_ref[...].astype(F32), g + 1)
            mw = m_ref[...].astype(BF16)
            pb = pooled.astype(BF16)
            dy = dy_ref[...].astype(F32)
            dsc_ref[...] = jnp.sum(dy * _dot(pb, mw), axis=0, keepdims=True)
            dmix = (dy * sc_ref[...]).astype(BF16)
            dm_ref[...] = _dot(pb, dmix, _TN)
            dpool = _dot(dmix, mw, _NT)
            acc = dpool / _pool_counts(s, w)
            for lv in range(g + 1):
                acc = acc + _shift_up(acc, 2 ** lv)
            dp_ref[...] = (acc - dpool).astype(BF16)
        return body_g

    outs = []
    for g in range(B_GROUPS):
        outs.append(pl.pallas_call(
            make(g), name=f"pool_bwd{g}",
            out_shape=(_sds((s, LANES), BF16), _sds((LANES, LANES)), _sds((1, LANES))), grid=(1,),
            in_specs=[pl.BlockSpec((s, LANES), lambda i, g=g: (0, 3 * (A_WIDTH // LANES) + g)),
                      pl.BlockSpec((None, LANES, LANES), lambda i, g=g: (g, 0, 0)),
                      pl.BlockSpec((1, LANES), lambda i, g=g: (0, g)),
                      pl.BlockSpec((s, LANES), lambda i, g=g: (0, A_WIDTH // LANES + g))],
            out_specs=(pl.BlockSpec((s, LANES), lambda i: (0, 0)), pl.BlockSpec((LANES, LANES), lambda i: (0, 0)),
                       pl.BlockSpec((1, LANES), lambda i: (0, 0))),
            compiler_params=_params(("arbitrary",), VMEM_BIG),
        )(z, mix_w, scale, dycat))
    return outs


_FF_BLOCKS = D_FF // FF_UNIT


def _ff_spec(s):
    return pl.BlockSpec((2, s, FF_UNIT), lambda j: (0, 0, j))


def _ff_wspecs():
    return [pl.BlockSpec((3, FF_UNIT), lambda j: (0, j)), pl.BlockSpec((3, FF_UNIT), lambda j: (0, _FF_BLOCKS + j))]


_FF_ROWS = 64
_FF_HALO = 16


def _chunk_taps(z_ref, half, c):
    start = pl.multiple_of(c * _FF_ROWS, _FF_ROWS)
    before = pl.multiple_of(jnp.maximum(c * _FF_ROWS - _FF_HALO, 0), _FF_HALO)
    halo = z_ref[half, pl.ds(before, _FF_HALO), :].astype(F32)
    halo = jnp.where(c > 0, halo, 0.0)
    win = jnp.concatenate([halo, z_ref[half, pl.ds(start, _FF_ROWS), :].astype(F32)], axis=0)
    return tuple(pltpu.roll(win, k, axis=0)[_FF_HALO:] for k in (2, 1)) + (win[_FF_HALO:],)


def _fold8(v):
    acc = v[0:8]
    for r in range(8, v.shape[0], 8):
        acc = acc + v[r:r + 8]
    return acc


def ffn_act_fwd(zf, conv_w, name):
    s = zf.shape[1]
    assert s % _FF_ROWS == 0

    def body(z_ref, wg_ref, wu_ref, o_ref):
        g = _conv3(z_ref[0].astype(F32), wg_ref[...])
        u = _conv3(z_ref[1].astype(F32), wu_ref[...])
        o_ref[...] = (g * _sigmoid(g) * u).astype(BF16)

    return pl.pallas_call(
        body, name=name, out_shape=_sds((s, D_FF), BF16), grid=(_FF_BLOCKS,),
        in_specs=[_ff_spec(s)] + _ff_wspecs(), out_specs=pl.BlockSpec((s, FF_UNIT), lambda j: (0, j)),
        compiler_params=_params(("parallel",), VMEM_BIG),
    )(zf, conv_w, conv_w)


def ffn_act_bwd(zf, conv_w, da, name):
    s = zf.shape[1]
    assert s % _FF_ROWS == 0
    nchunks = s // _FF_ROWS

    def body(z_ref, wg_ref, wu_ref, da_ref, dz_ref, dw_ref, dg_ref, du_ref):
        wg, wu = wg_ref[...], wu_ref[...]

        def first(c, acc):
            rows = pl.ds(pl.multiple_of(c * _FF_ROWS, _FF_ROWS), _FF_ROWS)
            tg, tu = _chunk_taps(z_ref, 0, c), _chunk_taps(z_ref, 1, c)
            g = _conv3_taps(tg, wg)
            u = _conv3_taps(tu, wu)
            dav = da_ref[rows, :].astype(F32)
            sg = _sigmoid(g)
            dg = dav * u * (sg * (1.0 + g * (1.0 - sg)))
            du = dav * (g * sg)
            dg_ref[rows, :] = dg
            du_ref[rows, :] = du
            return tuple(a + _fold8(d * t) for a, (d, t) in zip(acc, [(dg, t) for t in tg] + [(du, t) for t in tu]))

        zero = jnp.zeros((8, FF_UNIT), F32)
        acc = lax.fori_loop(0, nchunks, first, (zero,) * 6)
        sums = [jnp.sum(a, axis=0, keepdims=True) for a in acc]
        dw_ref[0] = jnp.concatenate(sums[:3], axis=0)
        dw_ref[1] = jnp.concatenate(sums[3:], axis=0)

        tail = pl.ds(s, _FF_HALO)
        dg_ref[tail, :] = jnp.zeros((_FF_HALO, FF_UNIT), F32)
        du_ref[tail, :] = jnp.zeros((_FF_HALO, FF_UNIT), F32)
        span = _FF_ROWS + _FF_HALO

        def second(c, carry):
            start = pl.multiple_of(c * _FF_ROWS, _FF_ROWS)
            for half, (d_ref, w) in enumerate(((dg_ref, wg), (du_ref, wu))):
                win = d_ref[pl.ds(start, span), :]
                dz = (w[0:1, :] * pltpu.roll(win, span - 2, axis=0)[:_FF_ROWS]
                      + w[1:2, :] * pltpu.roll(win, span - 1, axis=0)[:_FF_ROWS] + w[2:3, :] * win[:_FF_ROWS])
                dz_ref[half, pl.ds(start, _FF_ROWS), :] = dz.astype(BF16)
            return carry

        lax.fori_loop(0, nchunks, second, 0)

    return pl.pallas_call(
        body, name=name, out_shape=(_sds((2, s, D_FF), BF16), _sds((2, 3, D_FF))), grid=(_FF_BLOCKS,),
        in_specs=[_ff_spec(s)] + _ff_wspecs() + [pl.BlockSpec((s, FF_UNIT), lambda j: (0, j))],
        out_specs=(_ff_spec(s), pl.BlockSpec((2, 3, FF_UNIT), lambda j: (0, 0, j))),
        scratch_shapes=[pltpu.VMEM((s + _FF_HALO, FF_UNIT), F32), pltpu.VMEM((s + _FF_HALO, FF_UNIT), F32)],
        compiler_params=_params(("parallel",), VMEM_BIG),
    )(zf, conv_w, conv_w, da)


def _rope(v, cs, s1, s2):
    return v * cs + pltpu.roll(v, LANES - C_ROPE // 2, axis=1) * s1 + pltpu.roll(v, C_ROPE // 2, axis=1) * s2


def _rope_t(dv, cs, s1, s2):
    return dv * cs + pltpu.roll(dv * s1, C_ROPE // 2, axis=1) + pltpu.roll(dv * s2, LANES - C_ROPE // 2, axis=1)


def _kpe_mask(shape):
    lane = lax.broadcasted_iota(jnp.int32, shape, 1)
    return (lane >= C_NOPE) & (lane < C_NOPE + C_ROPE)


def _rms(v, g):
    r = lax.rsqrt(jnp.mean(v * v, axis=-1, keepdims=True) + EPS)
    return v * r, r


def _rms_bwd(dn, xh, r, g):
    dxh = dn * g
    return r * (dxh - xh * jnp.mean(dxh * xh, axis=-1, keepdims=True)), jnp.sum(dn * xh, axis=0, keepdims=True)


_ZQ = C_Q_RANK + C_KV_RANK + HEAD_PAD
_HW = C_HEADS * HEAD_PAD


def mla_pre_fwd(z, gq, gkv, wq, wk, wv, cs, s1, s2):
    s = z.shape[0]
    tm = _tile(s, 256, 8)

    def body(z_ref, gq_ref, gkv_ref, wq_ref, wk_ref, wv_ref, cs_ref, s1_ref, s2_ref, q_ref, k_ref, v_ref):
        zv = z_ref[...].astype(F32)
        cst, s1t, s2t = cs_ref[...], s1_ref[...], s2_ref[...]
        qh, _ = _rms(zv[:, :C_Q_RANK], None)
        qn = (qh * gq_ref[...]).astype(BF16)
        q = _dot(qn, wq_ref[...])
        kh, _ = _rms(zv[:, C_Q_RANK:C_Q_RANK + C_KV_RANK], None)
        kvn = (kh * gkv_ref[...]).astype(BF16)
        k = _dot(kvn, wk_ref[...])
        v_ref[...] = _dot(kvn, wv_ref[...]).astype(BF16)
        kpe = _rope(zv[:, C_Q_RANK + C_KV_RANK:], cst, s1t, s2t)
        for h in range(C_HEADS):
            sl = slice(h * HEAD_PAD, (h + 1) * HEAD_PAD)
            q_ref[:, sl] = _rope(q[:, sl], cst, s1t, s2t).astype(BF16)
            k_ref[:, sl] = (k[:, sl] + kpe).astype(BF16)

    full = lambda r, c: pl.BlockSpec((r, c), lambda i: (0, 0))
    hw = _sds((s, _HW), BF16)
    return pl.pallas_call(
        body, name="mla_pre_fwd", out_shape=(hw, hw, hw), grid=(s // tm,),
        in_specs=[_rows(tm, _ZQ), _vec(C_Q_RANK), _vec(C_KV_RANK), full(C_Q_RANK, _HW), full(C_KV_RANK, _HW),
                  full(C_KV_RANK, _HW), _rows(tm, LANES), _rows(tm, LANES), _rows(tm, LANES)],
        out_specs=(_rows(tm, _HW), _rows(tm, _HW), _rows(tm, _HW)),
        compiler_params=_params(("parallel",), VMEM_BIG),
    )(z, gq, gkv, wq, wk, wv, cs, s1, s2)


def mla_pre_bwd(z, gq, gkv, wq, wk, wv, cs, s1, s2, dq, dk, dv):
    s = z.shape[0]
    tm = _tile(s, 256, 8)

    def body(z_ref, gq_ref, gkv_ref, wq_ref, wk_ref, wv_ref, cs_ref, s1_ref, s2_ref, dq_ref, dk_ref, dv_ref,
             dz_ref, dwq_ref, dwk_ref, dwv_ref, dgq_ref, dgkv_ref):
        @pl.when(pl.program_id(0) == 0)
        def _():
            dwq_ref[...] = jnp.zeros_like(dwq_ref)
            dwk_ref[...] = jnp.zeros_like(dwk_ref)
            dwv_ref[...] = jnp.zeros_like(dwv_ref)
            dgq_ref[...] = jnp.zeros_like(dgq_ref)
            dgkv_ref[...] = jnp.zeros_like(dgkv_ref)

        zv = z_ref[...].astype(F32)
        cst, s1t, s2t = cs_ref[...], s1_ref[...], s2_ref[...]
        gqv, gkvv = gq_ref[...], gkv_ref[...]
        qh, rq = _rms(zv[:, :C_Q_RANK], None)
        qn = (qh * gqv).astype(BF16)
        kh, rk = _rms(zv[:, C_Q_RANK:C_Q_RANK + C_KV_RANK], None)
        kvn = (kh * gkvv).astype(BF16)

        dqv = dq_ref[...].astype(F32)
        dqp = jnp.concatenate(
            [_rope_t(dqv[:, h * HEAD_PAD:(h + 1) * HEAD_PAD], cst, s1t, s2t) for h in range(C_HEADS)], axis=1
        ).astype(BF16)
        dwq_ref[...] += _dot(qn, dqp, _TN)
        dqn = _dot(dqp, wq_ref[...], _NT)
        dql, dgq = _rms_bwd(dqn, qh, rq, gqv)
        dgq_ref[...] += dgq

        dkv = dk_ref[...]
        dkb = dkv.astype(BF16)
        dvb = dv_ref[...].astype(BF16)
        dwk_ref[...] += _dot(kvn, dkb, _TN)
        dwv_ref[...] += _dot(kvn, dvb, _TN)
        dkvn = _dot(dkb, wk_ref[...], _NT) + _dot(dvb, wv_ref[...], _NT)
        dkl, dgkv = _rms_bwd(dkvn, kh, rk, gkvv)
        dgkv_ref[...] += dgkv

        dkpe = dkv[:, :HEAD_PAD]
        for h in range(1, C_HEADS):
            dkpe = dkpe + dkv[:, h * HEAD_PAD:(h + 1) * HEAD_PAD]
        dkpe = _rope_t(jnp.where(_kpe_mask(dkpe.shape), dkpe, 0.0), cst, s1t, s2t)
        dz_ref[...] = jnp.concatenate([dql, dkl, dkpe], axis=1).astype(BF16)

    full = lambda r, c: pl.BlockSpec((r, c), lambda i: (0, 0))
    return pl.pallas_call(
        body, name="mla_pre_bwd",
        out_shape=(_sds((s, _ZQ), BF16), _sds((C_Q_RANK, _HW)), _sds((C_KV_RANK, _HW)), _sds((C_KV_RANK, _HW)),
                   _sds((1, C_Q_RANK)), _sds((1, C_KV_RANK))),
        grid=(s // tm,),
        in_specs=[_rows(tm, _ZQ), _vec(C_Q_RANK), _vec(C_KV_RANK), full(C_Q_RANK, _HW), full(C_KV_RANK, _HW),
                  full(C_KV_RANK, _HW), _rows(tm, LANES), _rows(tm, LANES), _rows(tm, LANES),
                  _rows(tm, _HW), _rows(tm, _HW), _rows(tm, _HW)],
        out_specs=(_rows(tm, _ZQ), full(C_Q_RANK, _HW), full(C_KV_RANK, _HW), full(C_KV_RANK, _HW),
                   _vec(C_Q_RANK), _vec(C_KV_RANK)),
        compiler_params=_params(("arbitrary",), VMEM_BIG),
    )(z, gq, gkv, wq, wk, wv, cs, s1, s2, dq, dk, dv)


_ATT_SCALE = (C_NOPE + C_ROPE) ** -0.5
_NEG = -1e30


def _att_exp(q, k, row0, ends_here):
    sc = _dot(q, k, _NT) * _ATT_SCALE
    tq, nk = sc.shape
    if ends_here:
        last = sc[:, nk - tq:]
        row = lax.broadcasted_iota(jnp.int32, last.shape, 0)
        col = lax.broadcasted_iota(jnp.int32, last.shape, 1)
        last = jnp.where(col <= row, last, _NEG)
        sc = last if nk == tq else jnp.concatenate([sc[:, :nk - tq], last], axis=1)
    else:
        qpos = row0 + lax.broadcasted_iota(jnp.int32, sc.shape, 0)
        kpos = lax.broadcasted_iota(jnp.int32, sc.shape, 1)
        sc = jnp.where(kpos <= qpos, sc, _NEG)
    e = jnp.exp(sc - jnp.max(sc, axis=-1, keepdims=True))
    return e, 1.0 / jnp.sum(e, axis=-1, keepdims=True)


def _causal_cases(i, nq, tq, fn):
    if nq > 8:
        fn(nq * tq, False)
        return
    for blk in range(nq):
        pl.when(i == blk)(functools.partial(fn, (blk + 1) * tq, True))


def attn_fwd(q, k, v):
    s = q.shape[0]
    tq = _tile(s, 256, 8)
    nq = s // tq

    def body(q_ref, k_ref, v_ref, o_ref):
        i = pl.program_id(1)

        def case(nk, ends_here):
            e, inv = _att_exp(q_ref[...], k_ref[:nk, :], i * tq, ends_here)
            o_ref[...] = (_dot(e.astype(BF16), v_ref[:nk, :]) * inv).astype(BF16)

        _causal_cases(i, nq, tq, case)

    qspec = pl.BlockSpec((tq, HEAD_PAD), lambda h, i: (i, h))
    kspec = pl.BlockSpec((s, HEAD_PAD), lambda h, i: (0, h))
    return pl.pallas_call(
        body, name="attn_fwd", out_shape=_sds((s, _HW), BF16), grid=(C_HEADS, s // tq),
        in_specs=[qspec, kspec, kspec], out_specs=qspec,
        compiler_params=_params(("parallel", "parallel"), VMEM_BIG),
    )(q, k, v)


def attn_bwd(q, k, v, o, do_all, do_col0):
    s = q.shape[0]
    tq = _tile(s, 256, 8)

    def body(q_ref, k_ref, v_ref, o_ref, do_ref, dq_ref, dk_ref, dv_ref):
        i = pl.program_id(1)

        @pl.when(i == 0)
        def _():
            dk_ref[...] = jnp.zeros_like(dk_ref)
            dv_ref[...] = jnp.zeros_like(dv_ref)

        def case(nk, ends_here):
            qv, kv, vv, dov = q_ref[...], k_ref[:nk, :], v_ref[:nk, :], do_ref[...]
            e, inv = _att_exp(qv, kv, i * tq, ends_here)
            p = e * inv
            dp = _dot(dov, vv, _NT)
            delta = jnp.sum(dov.astype(F32) * o_ref[...].astype(F32), axis=-1, keepdims=True)
            ds = (p * (dp - delta) * _ATT_SCALE).astype(BF16)
            dq_ref[...] = _dot(ds, kv).astype(BF16)
            dk_ref[:nk, :] += _dot(ds, qv, _TN)
            dv_ref[:nk, :] += _dot(p.astype(BF16), dov, _TN)

        _causal_cases(i, s // tq, tq, case)

    qspec = pl.BlockSpec((tq, HEAD_PAD), lambda h, i: (i, h))
    dospec = pl.BlockSpec((tq, HEAD_PAD), lambda h, i: (i, do_col0 + h))
    kspec = pl.BlockSpec((s, HEAD_PAD), lambda h, i: (0, h))
    return pl.pallas_call(
        body, name="attn_bwd", out_shape=(_sds((s, _HW), BF16), _sds((s, _HW)), _sds((s, _HW))),
        grid=(C_HEADS, s // tq),
        in_specs=[qspec, kspec, kspec, qspec, dospec], out_specs=(qspec, kspec, kspec),
        compiler_params=_params(("parallel", "arbitrary"), VMEM_BIG),
    )(q, k, v, o, do_all)


_DW = D_GROUPS * LANES


def _tril_bf16(w):
    r = lax.broadcasted_iota(jnp.int32, w.shape, 0)
    c = lax.broadcasted_iota(jnp.int32, w.shape, 1)
    return jnp.where(c <= r, w, 0.0).astype(BF16)


def _sgu_forward(zu, zv, lg, lb, ws_ref, bs):
    u = _gelu(zu)
    v = _gelu(zv)
    mu = jnp.mean(v, axis=-1, keepdims=True)
    vc = v - mu
    rstd = lax.rsqrt(jnp.mean(vc * vc, axis=-1, keepdims=True) + EPS)
    xh = vc * rstd
    vln = (xh * lg + lb).astype(BF16)
    mixed = []
    for g in range(D_GROUPS):
        wg = _tril_bf16(ws_ref[g])
        mixed.append(_dot(wg, vln[:, g * LANES:(g + 1) * LANES]) + bs[:, g:g + 1])
    return u, xh, rstd, vln, jnp.concatenate(mixed, axis=1)


def sgu_fwd(z, lg, lb, ws, bs_t):
    s = z.shape[0]
    nchunk = s // D_CHUNK

    def body(zu_ref, zv_ref, lg_ref, lb_ref, ws_ref, bs_ref, o_ref):
        u, _, _, _, mixed = _sgu_forward(zu_ref[...].astype(F32), zv_ref[...].astype(F32), lg_ref[...], lb_ref[...],
                                         ws_ref, bs_ref[...])
        o_ref[...] = (u * mixed).astype(BF16)

    return pl.pallas_call(
        body, name="sgu_fwd", out_shape=_sds((s, _DW), BF16), grid=(nchunk,),
        in_specs=[pl.BlockSpec((D_CHUNK, _DW), lambda n: (n, 1)), pl.BlockSpec((D_CHUNK, _DW), lambda n: (n, 2)),
                  _vec(_DW), _vec(_DW), pl.BlockSpec((D_GROUPS, D_CHUNK, D_CHUNK), lambda n: (0, 0, 0)),
                  pl.BlockSpec((D_CHUNK, LANES), lambda n: (0, 0))],
        out_specs=pl.BlockSpec((D_CHUNK, _DW), lambda n: (n, 0)),
        compiler_params=_params(("parallel",)),
    )(z, z, lg, lb, ws, bs_t)


def sgu_bwd(z, lg, lb, ws, bs_t, dycat, dy_col):
    s = z.shape[0]
    nchunk = s // D_CHUNK

    def body(zu_ref, zv_ref, lg_ref, lb_ref, ws_ref, bs_ref, dy_ref, dzu_ref, dzv_ref, dws_ref, dbs_ref, dlg_ref,
             dlb_ref):
        @pl.when(pl.program_id(0) == 0)
        def _():
            dws_ref[...] = jnp.zeros_like(dws_ref)
            dbs_ref[...] = jnp.zeros_like(dbs_ref)
            dlg_ref[...] = jnp.zeros_like(dlg_ref)
            dlb_ref[...] = jnp.zeros_like(dlb_ref)

        zu, zv, lg = zu_ref[...].astype(F32), zv_ref[...].astype(F32), lg_ref[...]
        u, xh, rstd, vln, mixed = _sgu_forward(zu, zv, lg, lb_ref[...], ws_ref, bs_ref[...])
        dy = dy_ref[...].astype(F32)
        dzu_ref[...] = (dy * mixed * _gelu_grad(zu)).astype(BF16)
        dmix = dy * u
        lane = lax.broadcasted_iota(jnp.int32, (D_CHUNK, LANES), 1)
        row = lax.broadcasted_iota(jnp.int32, (D_CHUNK, D_CHUNK), 0)
        colm = lax.broadcasted_iota(jnp.int32, (D_CHUNK, D_CHUNK), 1)
        dvln = []
        dbs = jnp.zeros((D_CHUNK, LANES), F32)
        for g in range(D_GROUPS):
            sl = slice(g * LANES, (g + 1) * LANES)
            dmg = dmix[:, sl]
            dbs = dbs + jnp.where(lane == g, jnp.sum(dmg, axis=-1, keepdims=True), 0.0)
            dmb = dmg.astype(BF16)
            dws_ref[g] += jnp.where(colm <= row, _dot(dmb, vln[:, sl], _NT), 0.0)
            dvln.append(_dot(_tril_bf16(ws_ref[g]), dmb, _TN))
        dbs_ref[...] += dbs
        dvln = jnp.concatenate(dvln, axis=1)
        dlg_ref[...] += jnp.sum(dvln * xh, axis=0, keepdims=True)
        dlb_ref[...] += jnp.sum(dvln, axis=0, keepdims=True)
        dxh = dvln * lg
        dvv = rstd * (dxh - jnp.mean(dxh, axis=-1, keepdims=True) - xh * jnp.mean(dxh * xh, axis=-1, keepdims=True))
        dzv_ref[...] = (dvv * _gelu_grad(zv)).astype(BF16)

    wsspec = pl.BlockSpec((D_GROUPS, D_CHUNK, D_CHUNK), lambda n: (0, 0, 0))
    chunk = lambda cidx: pl.BlockSpec((D_CHUNK, _DW), lambda n: (n, cidx))
    return pl.pallas_call(
        body, name="sgu_bwd",
        out_shape=(_sds((s, _DW), BF16), _sds((s, _DW), BF16), _sds((D_GROUPS, D_CHUNK, D_CHUNK)),
                   _sds((D_CHUNK, LANES)), _sds((1, _DW)), _sds((1, _DW))),
        grid=(nchunk,),
        in_specs=[chunk(1), chunk(2), _vec(_DW), _vec(_DW), wsspec, pl.BlockSpec((D_CHUNK, LANES), lambda n: (0, 0)),
                  chunk(dy_col)],
        out_specs=(chunk(0), chunk(0), wsspec, pl.BlockSpec((D_CHUNK, LANES), lambda n: (0, 0)), _vec(_DW), _vec(_DW)),
        compiler_params=_params(("arbitrary",)),
    )(z, z, lg, lb, ws, bs_t, dycat)


def ada_mod(c_all, ada_w, ada_b):
    nl, d, n = ada_w.shape
    nb = c_all.shape[0]
    tn = _tile(n, 512)

    def body(c_ref, w_ref, b_ref, o_ref):
        cv = c_ref[...]
        ca = (cv * _sigmoid(cv)).astype(BF16)
        o_ref[...] = _dot(ca, w_ref[...].astype(BF16)) + b_ref[...]

    return pl.pallas_call(
        body, name="ada_mod", out_shape=_sds((nl, nb, n)), grid=(nl, n // tn),
        in_specs=[pl.BlockSpec((nb, d), lambda l, j: (0, 0)), pl.BlockSpec((None, d, tn), lambda l, j: (l, 0, j)),
                  pl.BlockSpec((None, 1, tn), lambda l, j: (l, 0, j))],
        out_specs=pl.BlockSpec((None, nb, tn), lambda l, j: (l, 0, j)),
        compiler_params=_params(("parallel", "parallel")),
    )(c_all, ada_w, ada_b.reshape(nl, 1, n))


def ada_grad(c_all_t, dmod):
    d, nb = c_all_t.shape
    nl, _, n = dmod.shape
    tn = _tile(n, 512)
    tr = _tile(d, 256, 8)

    def body(c_ref, dm_ref, o_ref):
        cv = c_ref[...]
        ca = cv * _sigmoid(cv)
        dm = dm_ref[...]
        acc = ca[:, 0:1] * dm[0:1, :]
        for b in range(1, nb):
            acc = acc + ca[:, b:b + 1] * dm[b:b + 1, :]
        o_ref[...] = acc

    return pl.pallas_call(
        body, name="ada_grad", out_shape=_sds((nl, d, n)), grid=(nl, n // tn, d // tr),
        in_specs=[pl.BlockSpec((tr, nb), lambda l, j, r: (r, 0)), pl.BlockSpec((None, nb, tn), lambda l, j, r: (l, 0, j))],
        out_specs=pl.BlockSpec((None, tr, tn), lambda l, j, r: (l, r, j)),
        compiler_params=_params(("parallel", "parallel", "parallel")),
    )(c_all_t, dmod)


_ADAM_BLOCK = 256 * 1024


def _adam_rows(rows, cols):
    if rows * cols <= _ADAM_BLOCK or rows % 8:
        return rows
    return _tile(rows, max(8, _ADAM_BLOCK // cols), 8)


def _adam_update(w, gv, m, v):
    inv_bc1 = 1.0 / (1.0 - ADAM_B1 ** ADAM_STEP)
    inv_bc2 = 1.0 / (1.0 - ADAM_B2 ** ADAM_STEP)
    nm = ADAM_B1 * m + (1.0 - ADAM_B1) * gv
    nv = ADAM_B2 * v + (1.0 - ADAM_B2) * (gv * gv)
    return -ADAM_LR * ((nm * inv_bc1) / (jnp.sqrt(nv * inv_bc2) + ADAM_EPS) + ADAM_WD * w), nm, nv


def adamw(w, g, m, v, name, copy_grad=False):
    shape = w.shape
    cols = shape[-1]
    rows = w.size // cols
    tr = _adam_rows(rows, cols)

    def body(w_ref, g_ref, m_ref, v_ref, d_ref, nm_ref, nv_ref, *go_ref):
        gv = g_ref[...]
        d_ref[...], nm_ref[...], nv_ref[...] = _adam_update(w_ref[...], gv, m_ref[...], v_ref[...])
        if copy_grad:
            go_ref[0][...] = gv

    spec = pl.BlockSpec((tr, cols), lambda i: (i, 0))
    out = _sds((rows, cols))
    r2 = lambda t: t.reshape(rows, cols)
    nout = 4 if copy_grad else 3
    res = pl.pallas_call(
        body, name=name, out_shape=(out,) * nout, grid=(rows // tr,),
        in_specs=[spec] * 4, out_specs=(spec,) * nout, compiler_params=_params(("parallel",)),
    )(r2(w), r2(g), r2(m), r2(v))
    grad = res[3] if copy_grad else g
    return grad.reshape(shape), res[0].reshape(shape), res[1].reshape(shape), res[2].reshape(shape)


def adamw_small(ws, gs, ms, vs):
    n = len(ws)
    flat = lambda t: t.reshape(-1, t.shape[-1])

    def body(*refs):
        ins, outs = refs[:4 * n], refs[4 * n:]
        for i in range(n):
            w_ref, g_ref, m_ref, v_ref = ins[4 * i:4 * i + 4]
            outs[3 * i][...], outs[3 * i + 1][...], outs[3 * i + 2][...] = _adam_update(
                w_ref[...], g_ref[...], m_ref[...], v_ref[...])

    operands = [flat(t) for quad in zip(ws, gs, ms, vs) for t in quad]
    res = pl.pallas_call(
        body, name="adamw_small", out_shape=tuple(_sds(flat(w).shape) for w in ws for _ in range(3)),
    )(*operands)
    return [(g, res[3 * i].reshape(w.shape), res[3 * i + 1].reshape(w.shape), res[3 * i + 2].reshape(w.shape))
            for i, (w, g) in enumerate(zip(ws, gs))]


def adamw_layers(w, g0, g1, m, v, name):
    _, rows, cols = w.shape
    tr = _adam_rows(rows, cols)

    def body(w_ref, g0_ref, g1_ref, m_ref, v_ref, g_ref, d_ref, nm_ref, nv_ref):
        gv = jnp.where(pl.program_id(0) == 0, g0_ref[...], g1_ref[...])
        g_ref[...] = gv
        d_ref[...], nm_ref[...], nv_ref[...] = _adam_update(w_ref[...], gv, m_ref[...], v_ref[...])

    spec = pl.BlockSpec((None, tr, cols), lambda l, i: (l, i, 0))
    gspec = pl.BlockSpec((tr, cols), lambda l, i: (i, 0))
    out = _sds((2, rows, cols))
    return pl.pallas_call(
        body, name=name, out_shape=(out, out, out, out), grid=(2, rows // tr),
        in_specs=[spec, gspec, gspec, spec, spec], out_specs=(spec,) * 4, compiler_params=_params(("parallel", "parallel")),
    )(w, g0, g1, m, v)


def sum8(gathered):
    _, r, _ = gathered.shape
    tr = _tile(r, 512, 8)

    def body(g_ref, o_ref):
        acc = g_ref[0]
        for dev in range(1, N_DEV):
            acc = acc + g_ref[dev]
        o_ref[...] = acc

    return pl.pallas_call(
        body, name="sum8", out_shape=_sds((r, LANES)), grid=(r // tr,),
        in_specs=[pl.BlockSpec((N_DEV, tr, LANES), lambda i: (0, i, 0))], out_specs=pl.BlockSpec((tr, LANES), lambda i: (i, 0)),
        compiler_params=_params(("parallel",)),
    )(gathered)


_SUM_BLOCK = 512 * 1024


def _sum_rows(rh, cols):
    return rh if rh * cols <= _SUM_BLOCK else _tile(rh, max(16, _SUM_BLOCK // cols), 16)


def pair_sum(g, recv, core, name):
    _, r, cols = g.shape
    rh = r // 2
    tr = _sum_rows(rh, cols)
    per = rh // tr

    def body(c_ref, a_ref, b_ref, o_ref):
        del c_ref
        o_ref[...] = (a_ref[...].astype(F32) + b_ref[...].astype(F32)).astype(BF16)

    grid_spec = pltpu.PrefetchScalarGridSpec(
        num_scalar_prefetch=1, grid=(N_CHIPS, per),
        in_specs=[pl.BlockSpec((None, tr, cols), lambda k, i, c: (k, c[0] * per + i, 0)),
                  pl.BlockSpec((None, tr, cols), lambda k, i, c: (k, i, 0))],
        out_specs=pl.BlockSpec((None, tr, cols), lambda k, i, c: (k, i, 0)))
    return pl.pallas_call(
        body, name=name, out_shape=_sds((N_CHIPS, rh, cols), BF16), grid_spec=grid_spec,
        compiler_params=_params(("parallel", "parallel")),
    )(core.reshape(1).astype(jnp.int32), g, recv)


def chip_sum(pair, recv, chip, core, name):
    _, rh, cols = pair.shape
    tr = _sum_rows(rh, cols)

    def body(p_ref, own_ref, r_ref, o_ref):
        del p_ref
        acc = own_ref[...].astype(F32)
        for j in range(N_CHIPS - 1):
            acc = acc + r_ref[j].astype(F32)
        o_ref[...] = acc

    grid_spec = pltpu.PrefetchScalarGridSpec(
        num_scalar_prefetch=1, grid=(rh // tr,),
        in_specs=[pl.BlockSpec((None, tr, cols), lambda i, p: (p[0], i, 0)),
                  pl.BlockSpec((N_CHIPS - 1, tr, cols), lambda i, p: (0, i, 0))],
        out_specs=pl.BlockSpec((None, tr, cols), lambda i, p: (p[1], i, 0)))
    return pl.pallas_call(
        body, name=name, out_shape=_sds((2, rh, cols)), grid_spec=grid_spec,
        compiler_params=_params(("parallel",)),
    )(jnp.stack([chip, core]).astype(jnp.int32), pair, recv)


def _place():
    return lax.axis_index("x"), lax.axis_index("y"), lax.axis_index("c")


def _other_chips(x, y):
    return [(x, 1 - y), (1 - x, y), (1 - x, 1 - y)]


_HBM = pl.BlockSpec(memory_space=pltpu.HBM)


def all_gather8(v, name):
    m, n = v.shape

    def body(x_ref, out_ref, send_sems, recv_sems, local_sem):
        x, y, c = _place()
        me, sibling = (x, y, c), (x, y, 1 - c)
        chips = _other_chips(x, y)

        def rows(px, py, pc):
            return out_ref.at[pl.ds((4 * px + 2 * py + pc) * m, m), :]

        def copy(k, block, to, src=None):
            return pltpu.make_async_remote_copy(
                src_ref=rows(*block) if src is None else src, dst_ref=rows(*block),
                send_sem=send_sems.at[k], recv_sem=recv_sems.at[k], device_id=to, device_id_type=MESH)

        mine = pltpu.make_async_copy(x_ref, rows(*me), local_sem)
        mine.start()
        first = [copy(0, me, sibling, src=x_ref)]
        first += [copy(1 + j, me, (*chip, c), src=x_ref) for j, chip in enumerate(chips)]
        for cp in first:
            cp.start()
        passed = [copy(4 + j, (*chip, c), sibling) for j, chip in enumerate(chips)]
        for j, chip in enumerate(chips):
            copy(1 + j, (*chip, c), me).wait_recv()
            passed[j].start()
        copy(0, sibling, me).wait_recv()
        for j, chip in enumerate(chips):
            copy(4 + j, (*chip, 1 - c), me).wait_recv()
        for cp in first + passed:
            cp.wait_send()
        mine.wait()

    return pl.pallas_call(
        body, name=name, out_shape=_sds((N_DEV * m, n), v.dtype),
        in_specs=[pl.BlockSpec(memory_space=pltpu.VMEM)], out_specs=pl.BlockSpec(memory_space=pltpu.VMEM),
        scratch_shapes=[pltpu.SemaphoreType.DMA((7,)), pltpu.SemaphoreType.DMA((7,)), pltpu.SemaphoreType.DMA],
        compiler_params=_params(None, VMEM_BIG),
    )(v)


def _comm_call(body, name, ins, out_shapes, nsem, aliases=None):
    return pl.pallas_call(
        body, name=name, out_shape=tuple(out_shapes), in_specs=[_HBM] * len(ins), out_specs=tuple([_HBM] * len(out_shapes)),
        scratch_shapes=[pltpu.SemaphoreType.DMA((nsem,)), pltpu.SemaphoreType.DMA((nsem,))],
        input_output_aliases=aliases or {},
    )(*ins)


def _remote(src, dst, send_sems, recv_sems, k, to):
    return pltpu.make_async_remote_copy(src_ref=src, dst_ref=dst, send_sem=send_sems.at[k], recv_sem=recv_sems.at[k],
                                        device_id=to, device_id_type=MESH)


def _half(core, rh):
    return pl.ds(pl.multiple_of(core * rh, 16), rh)


def swap_halves(gs, name):
    n = len(gs)

    def body(*refs):
        ins, outs, (send_sems, recv_sems) = refs[:n], refs[n:2 * n], refs[2 * n:]
        x, y, c = _place()
        copies = []
        for i in range(n):
            theirs = _half(1 - c, ins[i].shape[1] // 2)
            cp = _remote(ins[i].at[:, theirs], outs[i], send_sems, recv_sems, i, (x, y, 1 - c))
            cp.start()
            copies.append(cp)
        for cp in copies:
            cp.wait()

    return _comm_call(body, name, gs, [_sds((g.shape[0], g.shape[1] // 2, g.shape[2]), g.dtype) for g in gs], n)


def join_halves(bufs, name):
    n = len(bufs)

    def body(*refs):
        ins, outs, (send_sems, recv_sems) = refs[:n], refs[n:2 * n], refs[2 * n:]
        x, y, c = _place()
        copies = []
        for i in range(n):
            cp = _remote(ins[i].at[c], outs[i].at[c], send_sems, recv_sems, i, (x, y, 1 - c))
            cp.start()
            copies.append(cp)
        for i in range(n):
            theirs = outs[i].at[1 - c]
            _remote(theirs, theirs, send_sems, recv_sems, i, (x, y, 1 - c)).wait_recv()
        for cp in copies:
            cp.wait_send()

    return _comm_call(body, name, bufs, [_sds(b.shape, b.dtype) for b in bufs], n, {i: i for i in range(n)})


def forward_halves(lands, name):
    n = len(lands)

    def body(*refs):
        ins, outs, (send_sems, recv_sems) = refs[:n], refs[n:2 * n], refs[2 * n:]
        x, y, c = _place()
        sibling = (x, y, 1 - c)
        chips = _other_chips(x, y)
        copies = []
        for i in range(n):
            mine = _half(c, ins[i].shape[1] // 2)
            for j, (px, py) in enumerate(chips):
                cp = _remote(ins[i].at[2 * px + py, mine], outs[i].at[2 * px + py, mine], send_sems, recv_sems, 3 * i + j, sibling)
                cp.start()
                copies.append(cp)
        for i in range(n):
            theirs = _half(1 - c, ins[i].shape[1] // 2)
            for j, (px, py) in enumerate(chips):
                landed = outs[i].at[2 * px + py, theirs]
                _remote(landed, landed, send_sems, recv_sems, 3 * i + j, sibling).wait_recv()
        for cp in copies:
            cp.wait_send()

    return _comm_call(body, name, lands, [_sds(b.shape, b.dtype) for b in lands], 3 * n, {i: i for i in range(n)})


_SEM = pl.BlockSpec(memory_space=pltpu.SEMAPHORE)
_EFFECT = pltpu.SideEffectType.DATAFLOW_SIDE_EFFECTING


def _gather_copies(srcs, lands, send_sems, recv_sems):
    x, y, c = _place()
    copies = []
    for i in range(len(srcs)):
        mine = _half(c, srcs[i].shape[0] // 2)
        for j, chip in enumerate(_other_chips(x, y)):
            copies.append(_remote(srcs[i].at[mine], lands[i].at[2 * x + y, mine], send_sems, recv_sems, 3 * i + j, (*chip, c)))
    return copies


def _exchange_copies(srcs, lands, send_sems, recv_sems):
    x, y, c = _place()
    copies = []
    for i in range(len(srcs)):
        for j, (px, py) in enumerate(_other_chips(x, y)):
            copies.append(_remote(srcs[i].at[2 * px + py], lands[i].at[j], send_sems, recv_sems, 3 * i + j, (px, py, c)))
    return copies


def _everyone_copies(srcs, lands, send_sems, recv_sems):
    x, y, c = _place()
    flip = lambda v, b: 1 - v if b else v
    dst = lands[0].at[4 * x + 2 * y + c]
    return [_remote(srcs[0], dst, send_sems, recv_sems, j - 1, (flip(x, j & 4), flip(y, j & 2), flip(c, j & 1)))
            for j in range(1, N_DEV)]


GATHER = (_gather_copies, 3)
EXCHANGE = (_exchange_copies, 3)
EVERYONE = (_everyone_copies, N_DEV - 1)


def split_start(name, plan, srcs, land_shapes, after=()):
    copies_fn, per_source = plan
    n, m, k = len(srcs), len(land_shapes), len(after)
    ncopies = per_source * n

    def body(*refs):
        src_refs, land_refs = refs[:n], refs[n:n + m]
        send_sems, recv_sems = refs[n + m + k], refs[n + m + k + 1]
        token = refs[-1]
        for cp in copies_fn(src_refs, land_refs, send_sems, recv_sems):
            cp.start()
        token[...] = jnp.zeros_like(token)

    hbm = lambda s: pltpu.HBM(tuple(s.shape), s.dtype)
    outs = pl.pallas_call(
        body, name=name,
        out_shape=(pltpu.SemaphoreType.DMA((ncopies,)), pltpu.SemaphoreType.DMA((ncopies,)), *[hbm(s) for s in srcs],
                   *[hbm(s) for s in land_shapes], _sds((8, LANES))),
        in_specs=[_HBM] * (n + m) + [pl.BlockSpec(memory_space=pl.ANY)] * k,
        out_specs=(_SEM, _SEM, *([_HBM] * (n + m)), pl.BlockSpec(memory_space=pltpu.VMEM)),
        input_output_aliases={i: 2 + i for i in range(n + m)},
        compiler_params=pltpu.CompilerParams(has_side_effects=_EFFECT),
    )(*[pltpu.with_memory_space_constraint(s, pltpu.HBM) for s in srcs],
      *[pltpu.with_memory_space_constraint(lax.empty(tuple(s.shape), s.dtype), pltpu.HBM) for s in land_shapes], *after)
    handle = (outs[0], outs[1], list(outs[2:2 + n]), list(outs[2 + n:2 + n + m]))
    return handle, outs[-1][0, 0]


def split_wait(name, plan, handle, after):
    copies_fn, _ = plan
    send_sems, recv_sems, srcs, lands = handle
    n, m = len(srcs), len(lands)
    after = list(after) if isinstance(after, (list, tuple)) else [after]

    def body(*refs):
        src_refs, land_refs = refs[:n], refs[n:n + m]
        for cp in copies_fn(src_refs, land_refs, refs[n + m], refs[n + m + 1]):
            cp.wait_send()
            cp.wait_recv()

    hbm = lambda s: pltpu.HBM(tuple(s.shape), s.dtype)
    outs = pl.pallas_call(
        body, name=name, out_shape=tuple(hbm(s) for s in srcs + lands),
        in_specs=[_HBM] * (n + m) + [_SEM, _SEM] + [pl.BlockSpec(memory_space=pl.ANY)] * len(after),
        out_specs=tuple([_HBM] * (n + m)), input_output_aliases={i: i for i in range(n + m)},
        compiler_params=pltpu.CompilerParams(has_side_effects=_EFFECT),
    )(*srcs, *lands, send_sems, recv_sems, *after)
    return list(outs[:n]), list(outs[n:])


_CD_PAD = C_Q_RANK + C_KV_RANK + HEAD_PAD + 2 * _DW


def chip_major(w, groups=N_CHIPS):
    r, c = w.shape
    return w.reshape(r, groups, c // groups).transpose(1, 0, 2)


def from_chip_major(w):
    g, r, c = w.shape
    return w.transpose(1, 0, 2).reshape(r, g * c)


def _cd_in_pad(w):
    a = C_Q_RANK + C_KV_RANK
    z = lambda n: jnp.zeros((w.shape[0], n), w.dtype)
    return jnp.concatenate([w[:, :a], z(C_NOPE), w[:, a:a + C_ROPE], z(HEAD_PAD - C_NOPE - C_ROPE), w[:, a + C_ROPE:]], axis=1)


def _cd_in_unpad(w):
    a = C_Q_RANK + C_KV_RANK
    return jnp.concatenate([w[:, :a], w[:, a + C_NOPE:a + C_NOPE + C_ROPE], w[:, a + HEAD_PAD:]], axis=1)


def _pad_heads(w, width):
    r = w.shape[0]
    w = w.reshape(r, C_HEADS, width)
    return jnp.pad(w, ((0, 0), (0, 0), (0, HEAD_PAD - width))).reshape(r, _HW)


def _unpad_heads(w, width):
    r = w.shape[0]
    return w.reshape(r, C_HEADS, HEAD_PAD)[:, :, :width].reshape(r, C_HEADS * width)


_MATMUL_WEIGHTS = ("ab_w_in", "ab_w_out", "cd_w_in", "c_w_uq", "c_w_ukv", "cd_w_out", "ffn_w_up", "ffn_w_down")
_LAYER_STACKED = ("norm1_g", "norm2_g", "ffn_w_up", "ffn_conv_w", "ffn_w_down")
_ROW_VECTORS = ("b_scale", "c_q_norm_g", "c_kv_norm_g", "d_ln_g", "d_ln_b")


def full_to_local(p):
    q = {}
    for k, v in p.items():
        if k == "final_norm_g":
            v = v.reshape(1, -1)
        elif k not in _LAYER_STACKED and k not in _ROW_VECTORS:
            v = v[0]
        q[k] = v.astype(BF16) if k in _MATMUL_WEIGHTS else v
    return q


def local_to_full(g):
    q = {}
    for k, v in g.items():
        if k == "final_norm_g":
            q[k] = v.reshape(-1)
        elif k not in _LAYER_STACKED and k not in _ROW_VECTORS:
            q[k] = v[None]
        else:
            q[k] = v
    return q


def prepare_weights(p):
    q = dict(p)
    q["cd_w_in"] = _cd_in_pad(p["cd_w_in"])
    q["c_w_uq"] = _pad_heads(p["c_w_uq"], C_NOPE + C_ROPE)
    ukv = p["c_w_ukv"].reshape(C_KV_RANK, C_HEADS, C_NOPE + C_V)
    q["c_w_uk"] = _pad_heads(ukv[:, :, :C_NOPE].reshape(C_KV_RANK, -1), C_NOPE)
    q["c_w_uv"] = _pad_heads(ukv[:, :, C_NOPE:].reshape(C_KV_RANK, -1), C_V)
    wo = p["cd_w_out"]
    att_rows = jnp.pad(wo[:C_HEADS * C_V].reshape(C_HEADS, C_V, D_MODEL), ((0, 0), (0, HEAD_PAD - C_V), (0, 0)))
    q["cd_w_out"] = jnp.concatenate([att_rows.reshape(_HW, D_MODEL), wo[C_HEADS * C_V:]], axis=0)
    return q


def unprepare_grads(g):
    q = dict(g)
    q["cd_w_in"] = _cd_in_unpad(g["cd_w_in"])
    q["c_w_uq"] = _unpad_heads(g["c_w_uq"], C_NOPE + C_ROPE)
    uk = g.pop("c_w_uk").reshape(C_KV_RANK, C_HEADS, HEAD_PAD)[:, :, :C_NOPE]
    uv = g.pop("c_w_uv").reshape(C_KV_RANK, C_HEADS, HEAD_PAD)[:, :, :C_V]
    q.pop("c_w_uk", None)
    q.pop("c_w_uv", None)
    q["c_w_ukv"] = jnp.concatenate([uk, uv], axis=-1).reshape(C_KV_RANK, C_HEADS * (C_NOPE + C_V))
    wo = g["cd_w_out"]
    att = wo[:_HW].reshape(C_HEADS, HEAD_PAD, D_MODEL)[:, :C_V].reshape(C_HEADS * C_V, D_MODEL)
    q["cd_w_out"] = jnp.concatenate([att, wo[_HW:]], axis=0)
    return q


def rope_tables(positions):
    half = C_ROPE // 2
    inv_freq = ROPE_THETA ** (-jnp.arange(half, dtype=F32) / half)
    ang = positions.astype(F32)[:, None] * inv_freq
    cos, sin = jnp.cos(ang), jnp.sin(ang)
    s = positions.shape[0]
    z = lambda n: jnp.zeros((s, n), F32)
    cs = jnp.concatenate([jnp.ones((s, C_NOPE), F32), cos, cos, z(HEAD_PAD - C_NOPE - C_ROPE)], axis=1)
    s1 = jnp.concatenate([z(C_NOPE), -sin, z(HEAD_PAD - C_NOPE - half)], axis=1)
    s2 = jnp.concatenate([z(C_NOPE + half), sin, z(HEAD_PAD - C_NOPE - C_ROPE)], axis=1)
    return cs, s1, s2


def _mods(mod_l):
    return [mod_l[:, i * D_MODEL:(i + 1) * D_MODEL] for i in range(N_MOD)]


_UP_COLS = 2 * D_FF // N_CHIPS


def ffn_fwd(h2, w, l, late_down=None):
    zf = matmul(h2, w["ffn_w_up"][l], "nn", BF16, f"ffn_up{l}", gb=N_CHIPS, go=2, tn=_UP_COLS)
    a = ffn_act_fwd(zf, w["ffn_conv_w"][l], f"ffn_act_fwd{l}")
    if late_down is not None:
        late_down(a)
    f = matmul(a, w["ffn_w_down"][l], "nn", F32, f"ffn_down{l}", tk=D_FF)
    return f, (zf, a)


def ffn_bwd(df, h2, saved, w, l):
    zf, a = saved
    da = matmul(df, w["ffn_w_down"][l], "nt", BF16, f"ffn_down_dx{l}", tn=D_FF // 2)
    d_down = matmul(a, df, "tn", BF16, f"ffn_down_dw{l}", tm=D_FF // 2)
    dzf, d_conv = ffn_act_bwd(zf, w["ffn_conv_w"][l], da, f"ffn_act_bwd{l}")
    dh2 = matmul(dzf, w["ffn_w_up"][l], "nt", F32, f"ffn_up_dx{l}", ga=2, gb=N_CHIPS, tk=_UP_COLS, tn=D_MODEL)
    d_up = matmul(h2, dzf, "tn", BF16, f"ffn_up_dw{l}", gb=2, go=N_CHIPS, tn=_UP_COLS)
    d_conv = d_conv.transpose(1, 0, 2).reshape(3, 2 * D_FF)
    return dh2, dict(ffn_w_down=d_down, ffn_conv_w=d_conv, ffn_w_up=d_up)


def mixer0_fwd(h, w):
    z = matmul(h, w["ab_w_in"], "nn", BF16, "ab_in", gb=N_CHIPS)
    ya = gconv_fwd(z, w["a_conv_w"])
    yb = pool_fwd(z, w["b_mix_w"], w["b_scale"])
    ycat = jnp.concatenate([ya] + yb, axis=1)
    y = matmul(ycat, w["ab_w_out"], "nn", F32, "ab_out", tn=D_MODEL)
    return y, (z, ycat)


def mixer0_bwd(dy, h, saved, w):
    z, ycat = saved
    grads = {}
    dycat = matmul(dy, w["ab_w_out"], "nt", BF16, "ab_out_dx")
    grads["ab_w_out"] = matmul(ycat, dy, "tn", BF16, "ab_out_dw")
    db, dc, da, d_conv = gconv_bwd(z, w["a_conv_w"], dycat)
    pb = pool_bwd(z, w["b_mix_w"], w["b_scale"], dycat)
    dz = jnp.concatenate([db, dc, da] + [t[0] for t in pb], axis=1)
    dh = matmul(dz, w["ab_w_in"], "nt", F32, "ab_in_dx", gb=N_CHIPS, tn=D_MODEL)
    grads["ab_w_in"] = matmul(h, dz, "tn", BF16, "ab_in_dw", go=N_CHIPS)
    grads.update(a_conv_w=d_conv, b_mix_w=jnp.stack([t[1] for t in pb]),
                 b_scale=jnp.concatenate([t[2] for t in pb], axis=1))
    return dh, grads


def mixer1_fwd(h, ropes, w):
    cs, s1, s2 = ropes
    z = matmul(h, w["cd_w_in"], "nn", BF16, "cd_in")
    bs_t = jnp.pad(w["d_b_s"].T, ((0, 0), (0, LANES - D_GROUPS)))
    qh, kh, vh = mla_pre_fwd(z, w["c_q_norm_g"], w["c_kv_norm_g"], w["c_w_uq"], w["c_w_uk"], w["c_w_uv"], cs, s1, s2)
    oh = attn_fwd(qh, kh, vh)
    yd = sgu_fwd(z, w["d_ln_g"], w["d_ln_b"], w["d_w_s"], bs_t)
    ycat = jnp.concatenate([oh, yd], axis=1)
    y = matmul(ycat, w["cd_w_out"], "nn", F32, "cd_out", tn=D_MODEL)
    return y, (z, bs_t, qh, kh, vh, oh, ycat)


def mixer1_bwd(dy, h, saved, ropes, w):
    cs, s1, s2 = ropes
    z, bs_t, qh, kh, vh, oh, ycat = saved
    grads = {}
    dycat = matmul(dy, w["cd_w_out"], "nt", BF16, "cd_out_dx")
    grads["cd_w_out"] = matmul(ycat, dy, "tn", F32, "cd_out_dw")
    dqh, dkh, dvh = attn_bwd(qh, kh, vh, oh, dycat, 0)
    dzq, d_uq, d_uk, d_uv, d_gq, d_gkv = mla_pre_bwd(
        z, w["c_q_norm_g"], w["c_kv_norm_g"], w["c_w_uq"], w["c_w_uk"], w["c_w_uv"], cs, s1, s2, dqh, dkh, dvh)
    dzu, dzv, d_ws, d_bs, d_lg, d_lb = sgu_bwd(z, w["d_ln_g"], w["d_ln_b"], w["d_w_s"], bs_t, dycat, _HW // _DW)
    dz = jnp.concatenate([dzq, dzu, dzv], axis=1)
    dh = matmul(dz, w["cd_w_in"], "nt", F32, "cd_in_dx", tn=D_MODEL)
    grads["cd_w_in"] = matmul(h, dz, "tn", F32, "cd_in_dw")
    grads.update(c_w_uq=d_uq, c_w_uk=d_uk, c_w_uv=d_uv, c_q_norm_g=d_gq, c_kv_norm_g=d_gkv, d_w_s=d_ws,
                 d_b_s=d_bs[:, :D_GROUPS].T, d_ln_g=d_lg, d_ln_b=d_lb)
    return dh, grads


class StepHooks:
    def weights(self, stage, after):
        pass

    def gradients(self, stage, grads, after):
        return 0.0


def run_step(x, tgt, mod, ropes, w, hooks):
    sh1a, sc1a, g1a, sh2a, sc2a, g2a = _mods(mod[0:1])
    sh1b, sc1b, g1b, sh2b, sc2b, g2b = _mods(mod[1:2])
    n1, n2 = w["norm1_g"], w["norm2_g"]

    hooks.weights("mix0", mod)
    h0 = modnorm_fwd(x, n1[0:1], sc1a, sh1a, "modnorm_0")
    y0, mix0 = mixer0_fwd(h0, w)
    x1, h1 = resid_modnorm_fwd(x, y0, g1a, n2[0:1], sc2a, sh2a, "resid_modnorm_1")
    hooks.weights("up0", x1)
    f0, ffn0 = ffn_fwd(h1, w, 0, lambda act: hooks.weights("down0", act))
    x2, h2 = resid_modnorm_fwd(x1, f0, g2a, n1[1:2], sc1b, sh1b, "resid_modnorm_2")
    hooks.weights("mix1", x2)
    y1, mix1 = mixer1_fwd(h2, ropes, w)
    x3, h3 = resid_modnorm_fwd(x2, y1, g1b, n2[1:2], sc2b, sh2b, "resid_modnorm_3")
    hooks.weights("ffn1", x3)
    f1, ffn1 = ffn_fwd(h3, w, 1)
    dres, d_final, loss, df1, dg2b = final_fused(x3, f1, g2b, w["final_norm_g"], tgt)

    dh3, gf1 = ffn_bwd(df1, h3, ffn1, w, 1)
    tok = hooks.gradients("ffn1", gf1, dh3)
    dres, dsh2b, dsc2b, dn2b, dy1, dg1b = norm_gate_bwd(x3, dh3, n2[1:2], sc2b, dres, y1, g1b + tok, "norm_gate_bwd_3")
    dh2, gm1 = mixer1_bwd(dy1, h2, mix1, ropes, w)
    tok = hooks.gradients("mix1", gm1, dh2)
    dres, dsh1b, dsc1b, dn1b, df0, dg2a = norm_gate_bwd(x2, dh2, n1[1:2], sc1b, dres, f0, g2a + tok, "norm_gate_bwd_2")
    dh1, gf0 = ffn_bwd(df0, h1, ffn0, w, 0)
    tok = hooks.gradients("ffn0", gf0, dh1)
    dres, dsh2a, dsc2a, dn2a, dy0, dg1a = norm_gate_bwd(x1, dh1, n2[0:1], sc2a, dres, y0, g1a + tok, "norm_gate_bwd_1")
    dh0, gm0 = mixer0_bwd(dy0, h0, mix0, w)
    hooks.gradients("mix0", gm0, dh0)
    grad_x, dsh1a, dsc1a, dn1a = norm_bwd(x, dh0, n1[0:1], sc1a, dres, "norm_bwd_0")

    dmod = jnp.concatenate([jnp.concatenate([dsh1a, dsc1a, dg1a, dsh2a, dsc2a, dg2a], axis=1),
                            jnp.concatenate([dsh1b, dsc1b, dg1b, dsh2b, dsc2b, dg2b], axis=1)], axis=0)
    norms = dict(norm1_g=jnp.concatenate([dn1a, dn1b], axis=0), norm2_g=jnp.concatenate([dn2a, dn2b], axis=0),
                 final_norm_g=d_final)
    return loss, grad_x, dmod, dict(mix0=gm0, ffn0=gf0, mix1=gm1, ffn1=gf1, norms=norms)


def merge_grads(by_stage):
    grads = {**by_stage["mix0"], **by_stage["mix1"], **by_stage["norms"]}
    for k in ("ffn_w_down", "ffn_w_up"):
        grads[k] = [by_stage["ffn0"][k], by_stage["ffn1"][k]]
    grads["ffn_conv_w"] = jnp.stack([by_stage["ffn0"]["ffn_conv_w"], by_stage["ffn1"]["ffn_conv_w"]])
    return grads


def local_step(x, tgt, mod, ropes, w):
    loss, grad_x, dmod, by_stage = run_step(x, tgt, mod, ropes, w, StepHooks())
    return loss, grad_x, dmod, merge_grads(by_stage)


_WEIGHTS = ("ada_w", "ada_b", "norm1_g", "norm2_g", "ab_w_in", "a_conv_w", "b_mix_w", "b_scale", "ab_w_out", "cd_w_in",
            "c_q_norm_g", "c_w_uq", "c_kv_norm_g", "c_w_ukv", "d_ln_g", "d_ln_b", "d_w_s", "d_b_s", "cd_w_out",
            "ffn_w_up", "ffn_conv_w", "ffn_w_down", "final_norm_g")
_INPUTS = ("x", "c", "positions") + _WEIGHTS + ("loss_target",) + tuple("m_" + n for n in _WEIGHTS) + tuple(
    "v_" + n for n in _WEIGHTS)

def _pack_rows(parts, rows, dtype):
    flat = jnp.concatenate([p.reshape(-1).astype(dtype) for p in parts])
    return jnp.pad(flat, (0, rows * LANES - flat.shape[0])).reshape(rows, LANES)


def _rows_major(w):
    r, c = w.shape
    return w.reshape(N_CHIPS, r // N_CHIPS, c)


def start_gather(shards, tag, after=()):
    lands = [_sds((N_CHIPS,) + s.shape, s.dtype) for s in shards]
    return split_start("gather_start_" + tag, GATHER, shards, lands, after)


def finish_gather(handle, chip, tag, after):
    shards, lands = split_wait("gather_wait_" + tag, GATHER, handle, after)
    lands = forward_halves(lands, "gather_forward_" + tag)
    return [lax.dynamic_update_index_in_dim(o, s, chip, 0) for o, s in zip(lands, shards)]


def start_reduce(gs, core, tag):
    recv = swap_halves(gs, "swap_halves_" + tag)
    pairs = [pair_sum(g, r, core, f"pair_sum_{tag}{i}") for i, (g, r) in enumerate(zip(gs, recv))]
    lands = [_sds((N_CHIPS - 1,) + p.shape[1:], p.dtype) for p in pairs]
    return split_start("exchange_start_" + tag, EXCHANGE, pairs, lands)


def finish_reduce(handle, chip, core, tag, after):
    pairs, others = split_wait("exchange_wait_" + tag, EXCHANGE, handle, after)
    halves = [chip_sum(p, o, chip, core, f"chip_sum_{tag}{i}") for i, (p, o) in enumerate(zip(pairs, others))]
    full = join_halves(halves, "join_halves_" + tag)
    return [f.reshape(f.shape[1] * 2, f.shape[2]) for f in full]


_SMALL_SHARDED = (("a_conv_w", (3, 128), 1), ("c_q_norm_g", (1, 64), 1), ("d_ln_g", (1, 128), 1), ("d_ln_b", (1, 128), 1),
                  ("ffn_conv_w", (2, 3, 2 * D_FF // N_CHIPS), 2))
_SMALL_GRADS = (("norm1_g", (2, D_MODEL)), ("norm2_g", (2, D_MODEL)), ("b_mix_w", (4, 128, 128)), ("b_scale", (1, 512)),
                ("c_kv_norm_g", (1, 128)), ("d_w_s", (4, 128, 128)), ("d_b_s", (4, 128)), ("final_norm_g", (1, D_MODEL)),
                ("a_conv_w", (3, 512)), ("c_q_norm_g", (1, 256)), ("d_ln_g", (1, 512)), ("d_ln_b", (1, 512)),
                ("ffn_conv_w", (2, 3, 2 * D_FF)))


def _size(shape):
    n = 1
    for d in shape:
        n *= d
    return n


def kernel(x, c, positions, ada_w, ada_b, norm1_g, norm2_g, ab_w_in, a_conv_w, b_mix_w, b_scale, ab_w_out, cd_w_in, c_q_norm_g, c_w_uq, c_kv_norm_g, c_w_ukv, d_ln_g, d_ln_b, d_w_s, d_b_s, cd_w_out, ffn_w_up, ffn_conv_w, ffn_w_down, final_norm_g, loss_target, m_ada_w, m_ada_b, m_norm1_g, m_norm2_g, m_ab_w_in, m_a_conv_w, m_b_mix_w, m_b_scale, m_ab_w_out, m_cd_w_in, m_c_q_norm_g, m_c_w_uq, m_c_kv_norm_g, m_c_w_ukv, m_d_ln_g, m_d_ln_b, m_d_w_s, m_d_b_s, m_cd_w_out, m_ffn_w_up, m_ffn_conv_w, m_ffn_w_down, m_final_norm_g, v_ada_w, v_ada_b, v_norm1_g, v_norm2_g, v_ab_w_in, v_a_conv_w, v_b_mix_w, v_b_scale, v_ab_w_out, v_cd_w_in, v_c_q_norm_g, v_c_w_uq, v_c_kv_norm_g, v_c_w_ukv, v_d_ln_g, v_d_ln_b, v_d_w_s, v_d_b_s, v_cd_w_out, v_ffn_w_up, v_ffn_conv_w, v_ffn_w_down, v_final_norm_g):
    args = (x, c, positions, ada_w, ada_b, norm1_g, norm2_g, ab_w_in, a_conv_w, b_mix_w, b_scale, ab_w_out, cd_w_in, c_q_norm_g, c_w_uq, c_kv_norm_g, c_w_ukv, d_ln_g, d_ln_b, d_w_s, d_b_s, cd_w_out, ffn_w_up, ffn_conv_w, ffn_w_down, final_norm_g, loss_target, m_ada_w, m_ada_b, m_norm1_g, m_norm2_g, m_ab_w_in, m_a_conv_w, m_b_mix_w, m_b_scale, m_ab_w_out, m_cd_w_in, m_c_q_norm_g, m_c_w_uq, m_c_kv_norm_g, m_c_w_ukv, m_d_ln_g, m_d_ln_b, m_d_w_s, m_d_b_s, m_cd_w_out, m_ffn_w_up, m_ffn_conv_w, m_ffn_w_down, m_final_norm_g, v_ada_w, v_ada_b, v_norm1_g, v_norm2_g, v_ab_w_in, v_a_conv_w, v_b_mix_w, v_b_scale, v_ab_w_out, v_cd_w_in, v_c_q_norm_g, v_c_w_uq, v_c_kv_norm_g, v_c_w_ukv, v_d_ln_g, v_d_ln_b, v_d_w_s, v_d_b_s, v_cd_w_out, v_ffn_w_up, v_ffn_conv_w, v_ffn_w_down, v_final_norm_g)
    a = dict(zip(_INPUTS, args, strict=True))
    xi, yi, ci = _place()
    chip = 2 * xi + yi
    dev = 4 * xi + 2 * yi + ci
    x = a["x"][0]
    tgt = a["loss_target"][0]

    bf = lambda t: t.astype(BF16)
    mix0_handle, tok = start_gather([bf(a["ab_w_in"][0]), bf(a["ab_w_out"][0])], "mix0")

    small_parts = [a["c"] + tok] + [a[n] for n, _, _ in _SMALL_SHARDED]
    rows1 = -(-sum(p.size for p in small_parts) // LANES // 8) * 8
    g1 = all_gather8(_pack_rows(small_parts, rows1, F32), "gather_small").reshape(N_DEV, rows1 * LANES)
    c_all = g1[:, :D_MODEL]
    per_chip = g1[0::2]
    small_full = {}
    off = D_MODEL
    for n, shp, axis in _SMALL_SHARDED:
        piece = per_chip[:, off:off + _size(shp)].reshape((N_CHIPS,) + shp)
        small_full[n] = jnp.concatenate([piece[k] for k in range(N_CHIPS)], axis=axis)
        off += _size(shp)

    merge = lambda t: t.reshape(t.shape[0] * t.shape[1], t.shape[2])
    w = dict(norm1_g=a["norm1_g"], norm2_g=a["norm2_g"], b_mix_w=a["b_mix_w"][0], b_scale=a["b_scale"],
             c_kv_norm_g=a["c_kv_norm_g"], d_w_s=a["d_w_s"][0], d_b_s=a["d_b_s"][0],
             final_norm_g=a["final_norm_g"].reshape(1, D_MODEL), **small_full)

    ncol = N_MOD * D_MODEL // N_CHIPS
    ada_b_mine = lax.dynamic_slice_in_dim(a["ada_b"], chip * ncol, ncol, axis=1)
    mod_cols = ada_mod(c_all, a["ada_w"], ada_b_mine)
    g2_rows = all_gather8(mod_cols.reshape(-1, LANES), "gather_mod")
    g2 = g2_rows.reshape(N_DEV, 2, N_DEV, ncol)
    mod = lax.dynamic_index_in_dim(g2[0::2], dev, axis=2, keepdims=False)
    mod = mod.transpose(1, 0, 2).reshape(2, N_MOD * D_MODEL)

    late = [g2_rows]
    up0_handle, tok_a = start_gather([bf(a["ffn_w_up"][0])], "up0", late)
    down0_handle, tok_b = start_gather([bf(a["ffn_w_down"][0])], "down0", late)
    mix1_handle, tok_c = start_gather(
        [bf(a["cd_w_in"][0]), bf(a["c_w_uq"][0]), bf(a["c_w_ukv"][0]), bf(a["cd_w_out"][0])], "mix1", late)
    ffn1_handle, tok_d = start_gather([bf(a["ffn_w_up"][1]), bf(a["ffn_w_down"][1])], "ffn1", late)
    mod = mod + (tok_a + tok_b + tok_c + tok_d)

    ropes = rope_tables(a["positions"][0])
    cm16 = lambda t: chip_major(t).astype(BF16)
    w.update(ffn_w_up=[None, None], ffn_w_down=[None, None])
    handles = dict(mix0=mix0_handle, up0=up0_handle, down0=down0_handle, mix1=mix1_handle, ffn1=ffn1_handle)
    reducing, reduced = {}, {}

    class Hooks(StepHooks):
        def weights(self, stage, after):
            got = finish_gather(handles[stage], chip, stage, after)
            if stage == "mix0":
                w.update(ab_w_in=got[0], ab_w_out=merge(got[1]))
            elif stage == "up0":
                w["ffn_w_up"][0] = got[0]
            elif stage == "down0":
                w["ffn_w_down"][0] = merge(got[0])
            elif stage == "mix1":
                cd_in, uq, ukv, cd_out = got
                w.update(prepare_weights(dict(cd_w_in=from_chip_major(cd_in), c_w_uq=from_chip_major(uq),
                                              c_w_ukv=from_chip_major(ukv), cd_w_out=merge(cd_out))))
            else:
                w["ffn_w_up"][1], w["ffn_w_down"][1] = got[0], merge(got[1])

        def gradients(self, stage, grads, after):
            if stage in ("ffn0", "ffn1"):
                parts = [grads["ffn_w_up"], _rows_major(grads["ffn_w_down"])]
            elif stage == "mix1":
                grads.update(unprepare_grads(grads))
                parts = [cm16(grads["cd_w_in"]), cm16(grads["c_w_uq"]), cm16(grads["c_w_ukv"]),
                         _rows_major(grads["cd_w_out"]).astype(BF16)]
            else:
                parts = [grads["ab_w_in"], _rows_major(grads["ab_w_out"])]
            reducing[stage], tok = start_reduce(parts, ci, stage)
            before = {"mix1": "ffn1", "ffn0": "mix1", "mix0": "ffn0"}.get(stage)
            if before is not None:
                reduced[before] = finish_reduce(reducing[before], chip, ci, before, after)
            return tok

    loss, grad_x, dmod, by_stage = run_step(x, tgt, mod, ropes, w, Hooks())
    grads = merge_grads(by_stage)

    parts3 = [dmod] + [grads[n] for n, _ in _SMALL_GRADS] + [loss[0, 0]]
    rows3 = -(-sum(p.size for p in parts3) // LANES // 8) * 8
    small_handle, _ = split_start("small_grads_start", EVERYONE, [_pack_rows(parts3, rows3, F32)],
                                  [_sds((N_DEV, rows3, LANES))])
    red_up1, red_down1 = reduced["ffn1"]
    red_cd_in, red_uq, red_ukv, red_cd_out = reduced["mix1"]
    red_up0, red_down0 = reduced["ffn0"]
    out_grads = dict(cd_w_in=red_cd_in, c_w_uq=red_uq, c_w_ukv=red_ukv, cd_w_out=red_cd_out)
    per_layer = dict(ffn_w_up=(red_up0, red_up1), ffn_w_down=(red_down0, red_down1))
    updates = {}

    def update(n):
        if n in per_layer:
            updates[n] = adamw_layers(a[n], *per_layer[n], a["m_" + n], a["v_" + n], "adamw_" + n)
        else:
            updates[n] = adamw(a[n], out_grads[n].reshape(a[n].shape), a["m_" + n], a["v_" + n], "adamw_" + n,
                               copy_grad=n != "ada_w")

    early =("ffn_w_up", "ffn_w_down", "cd_w_in", "c_w_uq", "c_w_ukv", "cd_w_out")
    for n in early:
        update(n)
    (mine,), (landed,) = split_wait("small_grads_wait", EVERYONE, small_handle, [updates[n][1] for n in early])
    g3 = lax.dynamic_update_index_in_dim(landed, mine, dev, 0)
    summed = sum8(g3).reshape(-1)
    nmod = 2 * N_MOD * D_MODEL
    out_grads["ada_b"] = summed[:nmod].reshape(2, N_MOD * D_MODEL)
    off = nmod
    for n, shp in _SMALL_GRADS:
        out_grads[n] = summed[off:off + _size(shp)].reshape(shp)
        off += _size(shp)
    loss = summed[off]
    for n, shp, axis in _SMALL_SHARDED:
        width = out_grads[n].shape[-1] // N_CHIPS
        out_grads[n] = lax.dynamic_slice_in_dim(out_grads[n], chip * width, width, axis=out_grads[n].ndim - 1)
    dmod_all = g3.reshape(N_DEV, rows3 * LANES)[:, :nmod].reshape(N_DEV, 2, N_MOD * D_MODEL)
    dmod_mine = lax.dynamic_slice_in_dim(dmod_all, chip * ncol, ncol, axis=2).transpose(1, 0, 2)
    out_grads["ada_w"] = ada_grad(c_all.T, dmod_mine)

    red_in0, red_out0 = finish_reduce(reducing["mix0"], chip, ci, "mix0", out_grads["ada_w"])
    out_grads.update(ab_w_in=red_in0, ab_w_out=red_out0)

    for n in ("ada_w", "ab_w_in", "ab_w_out"):
        update(n)
    small = [n for n in _WEIGHTS if n not in updates]
    for n, res in zip(small, adamw_small([a[n] for n in small], [out_grads[n].reshape(a[n].shape) for n in small],
                                         [a["m_" + n] for n in small], [a["v_" + n] for n in small])):
        updates[n] = res
    return (loss, grad_x[None], *[updates[n][i] for i in range(4) for n in _WEIGHTS])
```

```python
import functools

import jax
import jax.numpy as jnp
from jax import lax
from jax.experimental import pallas as pl
from jax.experimental.pallas import tpu as pltpu

F32 = jnp.float32
BF16 = jnp.bfloat16
EPS = 1e-6
D_MODEL = 1024
N_MOD = 6
A_WIDTH = 512
B_GROUPS = 4
POOL_WINDOWS = (2, 4, 8, 16)
C_HEADS = 8
C_NOPE = 64
C_ROPE = 32
C_V = 64
C_Q_RANK = 256
C_KV_RANK = 128
HEAD_PAD = 128
ROPE_THETA = 10000.0
D_GROUPS = 4
D_CHUNK = 128
D_FF = 2816
FF_UNIT = 128
ADAM_LR = 0.001
ADAM_B1 = 0.9
ADAM_B2 = 0.999
ADAM_EPS = 1e-08
ADAM_WD = 0.01
ADAM_STEP = 10
N_CHIPS = 4
N_DEV = 8
LANES = 128
VMEM_BIG = 56 * 1024 * 1024
MESH = pl.DeviceIdType.MESH


def _sds(shape, dtype=F32):
    return jax.ShapeDtypeStruct(tuple(shape), dtype)


def _tile(n, cap, mult=128):
    if n <= cap:
        return n
    best = None
    for t in range(mult, cap + 1, mult):
        if n % t == 0:
            best = t
    assert best is not None, (n, cap, mult)
    return best


def _params(dims=None, vmem=None):
    return pltpu.CompilerParams(dimension_semantics=dims, vmem_limit_bytes=vmem)


def _shift_down(v, k):
    r = pltpu.roll(v, k, axis=0)
    t = lax.broadcasted_iota(jnp.int32, v.shape, 0)
    return jnp.where(t >= k, r, 0.0)


def _shift_up(v, k):
    n = v.shape[0]
    r = pltpu.roll(v, n - k, axis=0)
    t = lax.broadcasted_iota(jnp.int32, v.shape, 0)
    return jnp.where(t < n - k, r, 0.0)


def _sigmoid(v):
    return 1.0 / (1.0 + jnp.exp(-v))


_GELU_C = 0.7978845608028654
_GELU_A = 0.044715


def _gelu(v):
    return 0.5 * v * (1.0 + jnp.tanh(_GELU_C * (v + _GELU_A * v * v * v)))


def _gelu_grad(v):
    th = jnp.tanh(_GELU_C * (v + _GELU_A * v * v * v))
    return 0.5 * (1.0 + th) + 0.5 * v * (1.0 - th * th) * _GELU_C * (1.0 + 3.0 * _GELU_A * v * v)


_NN = (((1,), (0,)), ((), ()))
_NT = (((1,), (1,)), ((), ()))
_TN = (((0,), (0,)), ((), ()))


def _dot(a, b, dims=_NN):
    return lax.dot_general(a, b, dims, preferred_element_type=F32)


def _logical(t, groups):
    return (t.shape[-2], t.shape[-1] * groups)


def _block(tr, tc, groups, cols, where):
    if groups == 1:
        return pl.BlockSpec((tr, tc), where)
    per = cols // groups // tc

    def index(i, j, s):
        r, c = where(i, j, s)
        return (c // per, r, c % per)

    return pl.BlockSpec((None, tr, tc), index)


def matmul(a, b, mode, out_dtype, name, ga=1, gb=1, go=1, tm=None, tn=None, tk=None):
    (ar, ac), (br, bc) = _logical(a, ga), _logical(b, gb)
    if mode == "nn":
        m, k, n = ar, ac, bc
        a_col, b_col = "k", "n"
    elif mode == "nt":
        m, k, n = ar, ac, br
        a_col, b_col = "k", "k"
    else:
        k, m, n = ar, ac, bc
        a_col, b_col = "m", "n"
    limit = {"m": m, "n": n // go, "k": k}
    limit[a_col] = min(limit[a_col], ac // ga)
    limit[b_col] = min(limit[b_col], bc // gb)
    tm = tm or _tile(limit["m"], 1024, 128 if mode == "tn" else 16)
    tn = tn or _tile(limit["n"], 512)
    tk = tk or _tile(limit["k"], 2048, 16 if mode == "tn" else 128)
    nk = k // tk
    if mode == "nn":
        a_spec = _block(tm, tk, ga, ac, lambda i, j, s: (i, s))
        b_spec = _block(tk, tn, gb, bc, lambda i, j, s: (s, j))
        dims = _NN
    elif mode == "nt":
        a_spec = _block(tm, tk, ga, ac, lambda i, j, s: (i, s))
        b_spec = _block(tn, tk, gb, bc, lambda i, j, s: (j, s))
        dims = _NT
    else:
        a_spec = _block(tk, tm, ga, ac, lambda i, j, s: (s, i))
        b_spec = _block(tk, tn, gb, bc, lambda i, j, s: (s, j))
        dims = _TN
    o_spec = _block(tm, tn, go, n, lambda i, j, s: (i, j))
    out_shape = _sds((m, n), out_dtype) if go == 1 else _sds((go, m, n // go), out_dtype)

    def body(a_ref, b_ref, o_ref, acc_ref):
        s = pl.program_id(2)

        @pl.when(s == 0)
        def _():
            acc_ref[...] = jnp.zeros_like(acc_ref)

        acc_ref[...] += _dot(a_ref[...], b_ref[...], dims)

        @pl.when(s == nk - 1)
        def _():
            o_ref[...] = acc_ref[...].astype(o_ref.dtype)

    return pl.pallas_call(
        body, name=name, out_shape=out_shape, grid=(m // tm, n // tn, nk),
        in_specs=[a_spec, b_spec], out_specs=o_spec,
        scratch_shapes=[pltpu.VMEM((tm, tn), F32)],
        compiler_params=_params(("parallel", "parallel", "arbitrary"), VMEM_BIG),
    )(a, b)


def _rows(tm, n):
    return pl.BlockSpec((tm, n), lambda i: (i, 0))


def _vec(n):
    return pl.BlockSpec((1, n), lambda i: (0, 0))


def modnorm_fwd(x, g, sc, sh, name):
    s, d = x.shape
    tm = _tile(s, 256, 8)

    def body(x_ref, g_ref, sc_ref, sh_ref, o_ref):
        xv = x_ref[...]
        r = lax.rsqrt(jnp.mean(xv * xv, axis=-1, keepdims=True) + EPS)
        o_ref[...] = ((xv * r) * g_ref[...] * (1.0 + sc_ref[...]) + sh_ref[...]).astype(BF16)

    return pl.pallas_call(
        body, name=name, out_shape=_sds((s, d), BF16), grid=(s // tm,),
        in_specs=[_rows(tm, d), _vec(d), _vec(d), _vec(d)], out_specs=_rows(tm, d),
        compiler_params=_params(("parallel",)),
    )(x, g, sc, sh)


def norm_bwd(x, dh, g, sc, dres, name):
    s, d = x.shape
    tm = _tile(s, 256, 8)
    nsteps = s // tm

    def body(x_ref, dh_ref, g_ref, sc_ref, dr_ref, dx_ref, dsh_ref, dsc_ref, dg_ref, a2_ref):
        i = pl.program_id(0)

        @pl.when(i == 0)
        def _():
            dsh_ref[...] = jnp.zeros_like(dsh_ref)
            a2_ref[...] = jnp.zeros_like(a2_ref)

        xv = x_ref[...]
        dh = dh_ref[...]
        r = lax.rsqrt(jnp.mean(xv * xv, axis=-1, keepdims=True) + EPS)
        xh = xv * r
        dsh_ref[...] += jnp.sum(dh, axis=0, keepdims=True)
        a2_ref[...] += jnp.sum(dh * xh, axis=0, keepdims=True)
        dxh = dh * (g_ref[...] * (1.0 + sc_ref[...]))
        dx = r * (dxh - xh * jnp.mean(dxh * xh, axis=-1, keepdims=True))
        dx_ref[...] = dr_ref[...] + dx

        @pl.when(i == nsteps - 1)
        def _():
            dsc_ref[...] = a2_ref[...] * g_ref[...]
            dg_ref[...] = a2_ref[...] * (1.0 + sc_ref[...])

    return pl.pallas_call(
        body, name=name, out_shape=(_sds((s, d)), _sds((1, d)), _sds((1, d)), _sds((1, d))), grid=(nsteps,),
        in_specs=[_rows(tm, d), _rows(tm, d), _vec(d), _vec(d), _rows(tm, d)],
        out_specs=(_rows(tm, d), _vec(d), _vec(d), _vec(d)),
        scratch_shapes=[pltpu.VMEM((1, d), F32)],
        compiler_params=_params(("arbitrary",)),
    )(x, dh, g, sc, dres)


def resid_modnorm_fwd(x, y, gate, g, sc, sh, name):
    s, d = x.shape
    tm = _tile(s, 256, 8)

    def body(x_ref, y_ref, gate_ref, g_ref, sc_ref, sh_ref, xo_ref, h_ref):
        xv = x_ref[...] + gate_ref[...] * y_ref[...]
        xo_ref[...] = xv
        r = lax.rsqrt(jnp.mean(xv * xv, axis=-1, keepdims=True) + EPS)
        h_ref[...] = ((xv * r) * g_ref[...] * (1.0 + sc_ref[...]) + sh_ref[...]).astype(BF16)

    return pl.pallas_call(
        body, name=name, out_shape=(_sds((s, d)), _sds((s, d), BF16)), grid=(s // tm,),
        in_specs=[_rows(tm, d), _rows(tm, d), _vec(d), _vec(d), _vec(d), _vec(d)], out_specs=(_rows(tm, d), _rows(tm, d)),
        compiler_params=_params(("parallel",)),
    )(x, y, gate, g, sc, sh)


def norm_gate_bwd(x, dh, g, sc, dres, y, gate, name):
    s, d = x.shape
    tm = _tile(s, 256, 8)
    nsteps = s // tm

    def body(x_ref, dh_ref, g_ref, sc_ref, dr_ref, y_ref, gate_ref, dx_ref, dsh_ref, dsc_ref, dg_ref, dy_ref,
             dgate_ref, a2_ref):
        i = pl.program_id(0)

        @pl.when(i == 0)
        def _():
            dsh_ref[...] = jnp.zeros_like(dsh_ref)
            a2_ref[...] = jnp.zeros_like(a2_ref)
            dgate_ref[...] = jnp.zeros_like(dgate_ref)

        xv = x_ref[...]
        dh = dh_ref[...]
        r = lax.rsqrt(jnp.mean(xv * xv, axis=-1, keepdims=True) + EPS)
        xh = xv * r
        dsh_ref[...] += jnp.sum(dh, axis=0, keepdims=True)
        a2_ref[...] += jnp.sum(dh * xh, axis=0, keepdims=True)
        dxh = dh * (g_ref[...] * (1.0 + sc_ref[...]))
        dr = dr_ref[...] + r * (dxh - xh * jnp.mean(dxh * xh, axis=-1, keepdims=True))
        dx_ref[...] = dr
        dy_ref[...] = (dr * gate_ref[...]).astype(BF16)
        dgate_ref[...] += jnp.sum(dr * y_ref[...], axis=0, keepdims=True)

        @pl.when(i == nsteps - 1)
        def _():
            dsc_ref[...] = a2_ref[...] * g_ref[...]
            dg_ref[...] = a2_ref[...] * (1.0 + sc_ref[...])

    vec = _sds((1, d))
    return pl.pallas_call(
        body, name=name, out_shape=(_sds((s, d)), vec, vec, vec, _sds((s, d), BF16), vec), grid=(nsteps,),
        in_specs=[_rows(tm, d), _rows(tm, d), _vec(d), _vec(d), _rows(tm, d), _rows(tm, d), _vec(d)],
        out_specs=(_rows(tm, d), _vec(d), _vec(d), _vec(d), _rows(tm, d), _vec(d)),
        scratch_shapes=[pltpu.VMEM((1, d), F32)],
        compiler_params=_params(("arbitrary",)),
    )(x, dh, g, sc, dres, y, gate)


def final_fused(x, f, gate, g, tgt):
    s, d = x.shape
    tm = _tile(s, 256, 8)

    def body(x_ref, f_ref, gate_ref, g_ref, t_ref, dx_ref, dg_ref, loss_ref, df_ref, dgate_ref):
        @pl.when(pl.program_id(0) == 0)
        def _():
            dg_ref[...] = jnp.zeros_like(dg_ref)
            loss_ref[...] = jnp.zeros_like(loss_ref)
            dgate_ref[...] = jnp.zeros_like(dgate_ref)

        fv, gatev, gv = f_ref[...], gate_ref[...], g_ref[...]
        xv = x_ref[...] + gatev * fv
        r = lax.rsqrt(jnp.mean(xv * xv, axis=-1, keepdims=True) + EPS)
        xh = xv * r
        e = xh * gv - t_ref[...]
        row = jnp.sum(e * e, axis=-1, keepdims=True) * (0.5 / d)
        loss_ref[...] += jnp.sum(row, axis=0, keepdims=True)
        dy = e * (1.0 / d)
        dg_ref[...] += jnp.sum(dy * xh, axis=0, keepdims=True)
        dxh = dy * gv
        dx = r * (dxh - xh * jnp.mean(dxh * xh, axis=-1, keepdims=True))
        dx_ref[...] = dx
        df_ref[...] = (dx * gatev).astype(BF16)
        dgate_ref[...] += jnp.sum(dx * fv, axis=0, keepdims=True)

    vec = _sds((1, d))
    return pl.pallas_call(
        body, name="final_fused", out_shape=(_sds((s, d)), vec, _sds((1, LANES)), _sds((s, d), BF16), vec),
        grid=(s // tm,),
        in_specs=[_rows(tm, d), _rows(tm, d), _vec(d), _vec(d), _rows(tm, d)],
        out_specs=(_rows(tm, d), _vec(d), _vec(LANES), _rows(tm, d), _vec(d)),
        compiler_params=_params(("arbitrary",)),
    )(x, f, gate, g, tgt)


def _taps(v):
    return _shift_down(v, 2), _shift_down(v, 1), v


def _conv3_taps(taps, w):
    return w[0:1, :] * taps[0] + w[1:2, :] * taps[1] + w[2:3, :] * taps[2]


def _conv3(v, w):
    return _conv3_taps(_taps(v), w)


def _conv3_t(dv, w):
    return w[0:1, :] * _shift_up(dv, 2) + w[1:2, :] * _shift_up(dv, 1) + w[2:3, :] * dv


def _conv3_dw_taps(dv, taps):
    return jnp.concatenate([jnp.sum(dv * t, axis=0, keepdims=True) for t in taps], axis=0)


def _conv3_dw(dv, v):
    return _conv3_dw_taps(dv, _taps(v))


def gconv_fwd(z, conv_w):
    s = z.shape[0]
    nb = A_WIDTH // LANES

    def body(b_ref, c_ref, a_ref, w_ref, o_ref):
        b, c, a = b_ref[...].astype(F32), c_ref[...].astype(F32), a_ref[...].astype(F32)
        o_ref[...] = (b * _conv3(c * a, w_ref[...])).astype(BF16)

    col = lambda off: pl.BlockSpec((s, LANES), lambda j: (0, off + j))
    return pl.pallas_call(
        body, name="gconv_fwd", out_shape=_sds((s, A_WIDTH), BF16), grid=(nb,),
        in_specs=[col(0), col(nb), col(2 * nb), pl.BlockSpec((3, LANES), lambda j: (0, j))],
        out_specs=pl.BlockSpec((s, LANES), lambda j: (0, j)),
        compiler_params=_params(("parallel",), VMEM_BIG),
    )(z, z, z, conv_w)


def gconv_bwd(z, conv_w, dycat):
    s = z.shape[0]
    nb = A_WIDTH // LANES

    def body(b_ref, c_ref, a_ref, w_ref, dy_ref, db_ref, dc_ref, da_ref, dw_ref):
        c, a, w, dy = c_ref[...].astype(F32), a_ref[...].astype(F32), w_ref[...], dy_ref[...].astype(F32)
        ca = c * a
        db_ref[...] = (dy * _conv3(ca, w)).astype(BF16)
        dconv = dy * b_ref[...].astype(F32)
        dw_ref[...] = _conv3_dw(dconv, ca)
        dca = _conv3_t(dconv, w)
        dc_ref[...] = (dca * a).astype(BF16)
        da_ref[...] = (dca * c).astype(BF16)

    col = lambda off: pl.BlockSpec((s, LANES), lambda j: (0, off + j))
    wspec = pl.BlockSpec((3, LANES), lambda j: (0, j))
    part = _sds((s, A_WIDTH), BF16)
    return pl.pallas_call(
        body, name="gconv_bwd", out_shape=(part, part, part, _sds((3, A_WIDTH))), grid=(nb,),
        in_specs=[col(0), col(nb), col(2 * nb), wspec, col(0)],
        out_specs=(col(0), col(0), col(0), wspec),
        compiler_params=_params(("parallel",), VMEM_BIG),
    )(z, z, z, conv_w, dycat)


def _pool_counts(s, w):
    t = lax.broadcasted_iota(jnp.int32, (s, 1), 0)
    return jnp.minimum(t + 1, w).astype(F32)


def _pooled(p, levels):
    acc = p
    for lv in range(levels):
        acc = acc + _shift_down(acc, 2 ** lv)
    return acc / _pool_counts(p.shape[0], 2 ** levels) - p


def pool_fwd(z, mix_w, scale):
    s = z.shape[0]

    def make(g):
        def body_g(p_ref, m_ref, sc_ref, o_ref):
            pooled = _pooled(p_ref[...].astype(F32), g + 1)
            y = _dot(pooled.astype(BF16), m_ref[...].astype(BF16))
            o_ref[...] = (y * sc_ref[...]).astype(BF16)
        return body_g

    outs = []
    for g in range(B_GROUPS):
        outs.append(pl.pallas_call(
            make(g), name=f"pool_fwd{g}", out_shape=_sds((s, LANES), BF16), grid=(1,),
            in_specs=[pl.BlockSpec((s, LANES), lambda i, g=g: (0, 3 * (A_WIDTH // LANES) + g)),
                      pl.BlockSpec((None, LANES, LANES), lambda i, g=g: (g, 0, 0)),
                      pl.BlockSpec((1, LANES), lambda i, g=g: (0, g))],
            out_specs=pl.BlockSpec((s, LANES), lambda i: (0, 0)),
            compiler_params=_params(("arbitrary",), VMEM_BIG),
        )(z, mix_w, scale))
    return outs


def pool_bwd(z, mix_w, scale, dycat):
    s = z.shape[0]

    def make(g):
        w = 2 ** (g + 1)

        def body_g(p_ref, m_ref, sc_ref, dy_ref, dp_ref, dm_ref, dsc_ref):
            pooled = _pooled(p_ref[...].astype(F32), g + 1)
            mw = m_ref[...].astype(BF16)
            pb = pooled.astype(BF16)
            dy = dy_ref[...].astype(F32)
            dsc_ref[...] = jnp.sum(dy * _dot(pb, mw), axis=0, keepdims=True)
            dmix = (dy * sc_ref[...]).astype(BF16)
            dm_ref[...] = _dot(pb, dmix, _TN)
            dpool = _dot(dmix, mw, _NT)
            acc = dpool / _pool_counts(s, w)
            for lv in range(g + 1):
                acc = acc + _shift_up(acc, 2 ** lv)
            dp_ref[...] = (acc - dpool).astype(BF16)
        return body_g

    outs = []
    for g in range(B_GROUPS):
        outs.append(pl.pallas_call(
            make(g), name=f"pool_bwd{g}",
            out_shape=(_sds((s, LANES), BF16), _sds((LANES, LANES)), _sds((1, LANES))), grid=(1,),
            in_specs=[pl.BlockSpec((s, LANES), lambda i, g=g: (0, 3 * (A_WIDTH // LANES) + g)),
                      pl.BlockSpec((None, LANES, LANES), lambda i, g=g: (g, 0, 0)),
                      pl.BlockSpec((1, LANES), lambda i, g=g: (0, g)),
                      pl.BlockSpec((s, LANES), lambda i, g=g: (0, A_WIDTH // LANES + g))],
            out_specs=(pl.BlockSpec((s, LANES), lambda i: (0, 0)), pl.BlockSpec((LANES, LANES), lambda i: (0, 0)),
                       pl.BlockSpec((1, LANES), lambda i: (0, 0))),
            compiler_params=_params(("arbitrary",), VMEM_BIG),
        )(z, mix_w, scale, dycat))
    return outs


_FF_BLOCKS = D_FF // FF_UNIT


def _ff_spec(s):
    return pl.BlockSpec((2, s, FF_UNIT), lambda j: (0, 0, j))


def _ff_wspecs():
    return [pl.BlockSpec((3, FF_UNIT), lambda j: (0, j)), pl.BlockSpec((3, FF_UNIT), lambda j: (0, _FF_BLOCKS + j))]


_FF_ROWS = 64
_FF_HALO = 16


def _chunk_taps(z_ref, half, c):
    start = pl.multiple_of(c * _FF_ROWS, _FF_ROWS)
    before = pl.multiple_of(jnp.maximum(c * _FF_ROWS - _FF_HALO, 0), _FF_HALO)
    halo = z_ref[half, pl.ds(before, _FF_HALO), :].astype(F32)
    halo = jnp.where(c > 0, halo, 0.0)
    win = jnp.concatenate([halo, z_ref[half, pl.ds(start, _FF_ROWS), :].astype(F32)], axis=0)
    return tuple(pltpu.roll(win, k, axis=0)[_FF_HALO:] for k in (2, 1)) + (win[_FF_HALO:],)


def _fold8(v):
    acc = v[0:8]
    for r in range(8, v.shape[0], 8):
        acc = acc + v[r:r + 8]
    return acc


def ffn_act_fwd(zf, conv_w, name):
    s = zf.shape[1]
    assert s % _FF_ROWS == 0

    def body(z_ref, wg_ref, wu_ref, o_ref):
        g = _conv3(z_ref[0].astype(F32), wg_ref[...])
        u = _conv3(z_ref[1].astype(F32), wu_ref[...])
        o_ref[...] = (g * _sigmoid(g) * u).astype(BF16)

    return pl.pallas_call(
        body, name=name, out_shape=_sds((s, D_FF), BF16), grid=(_FF_BLOCKS,),
        in_specs=[_ff_spec(s)] + _ff_wspecs(), out_specs=pl.BlockSpec((s, FF_UNIT), lambda j: (0, j)),
        compiler_params=_params(("parallel",), VMEM_BIG),
    )(zf, conv_w, conv_w)


def ffn_act_bwd(zf, conv_w, da, name):
    s = zf.shape[1]
    assert s % _FF_ROWS == 0
    nchunks = s // _FF_ROWS

    def body(z_ref, wg_ref, wu_ref, da_ref, dz_ref, dw_ref, dg_ref, du_ref):
        wg, wu = wg_ref[...], wu_ref[...]

        def first(c, acc):
            rows = pl.ds(pl.multiple_of(c * _FF_ROWS, _FF_ROWS), _FF_ROWS)
            tg, tu = _chunk_taps(z_ref, 0, c), _chunk_taps(z_ref, 1, c)
            g = _conv3_taps(tg, wg)
            u = _conv3_taps(tu, wu)
            dav = da_ref[rows, :].astype(F32)
            sg = _sigmoid(g)
            dg = dav * u * (sg * (1.0 + g * (1.0 - sg)))
            du = dav * (g * sg)
            dg_ref[rows, :] = dg
            du_ref[rows, :] = du
            return tuple(a + _fold8(d * t) for a, (d, t) in zip(acc, [(dg, t) for t in tg] + [(du, t) for t in tu]))

        zero = jnp.zeros((8, FF_UNIT), F32)
        acc = lax.fori_loop(0, nchunks, first, (zero,) * 6)
        sums = [jnp.sum(a, axis=0, keepdims=True) for a in acc]
        dw_ref[0] = jnp.concatenate(sums[:3], axis=0)
        dw_ref[1] = jnp.concatenate(sums[3:], axis=0)

        tail = pl.ds(s, _FF_HALO)
        dg_ref[tail, :] = jnp.zeros((_FF_HALO, FF_UNIT), F32)
        du_ref[tail, :] = jnp.zeros((_FF_HALO, FF_UNIT), F32)
        span = _FF_ROWS + _FF_HALO

        def second(c, carry):
            start = pl.multiple_of(c * _FF_ROWS, _FF_ROWS)
            for half, (d_ref, w) in enumerate(((dg_ref, wg), (du_ref, wu))):
                win = d_ref[pl.ds(start, span), :]
                dz = (w[0:1, :] * pltpu.roll(win, span - 2, axis=0)[:_FF_ROWS]
                      + w[1:2, :] * pltpu.roll(win, span - 1, axis=0)[:_FF_ROWS] + w[2:3, :] * win[:_FF_ROWS])
                dz_ref[half, pl.ds(start, _FF_ROWS), :] = dz.astype(BF16)
            return carry

        lax.fori_loop(0, nchunks, second, 0)

    return pl.pallas_call(
        body, name=name, out_shape=(_sds((2, s, D_FF), BF16), _sds((2, 3, D_FF))), grid=(_FF_BLOCKS,),
        in_specs=[_ff_spec(s)] + _ff_wspecs() + [pl.BlockSpec((s, FF_UNIT), lambda j: (0, j))],
        out_specs=(_ff_spec(s), pl.BlockSpec((2, 3, FF_UNIT), lambda j: (0, 0, j))),
        scratch_shapes=[pltpu.VMEM((s + _FF_HALO, FF_UNIT), F32), pltpu.VMEM((s + _FF_HALO, FF_UNIT), F32)],
        compiler_params=_params(("parallel",), VMEM_BIG),
    )(zf, conv_w, conv_w, da)


def _rope(v, cs, s1, s2):
    return v * cs + pltpu.roll(v, LANES - C_ROPE // 2, axis=1) * s1 + pltpu.roll(v, C_ROPE // 2, axis=1) * s2


def _rope_t(dv, cs, s1, s2):
    return dv * cs + pltpu.roll(dv * s1, C_ROPE // 2, axis=1) + pltpu.roll(dv * s2, LANES - C_ROPE // 2, axis=1)


def _kpe_mask(shape):
    lane = lax.broadcasted_iota(jnp.int32, shape, 1)
    return (lane >= C_NOPE) & (lane < C_NOPE + C_ROPE)


def _rms(v, g):
    r = lax.rsqrt(jnp.mean(v * v, axis=-1, keepdims=True) + EPS)
    return v * r, r


def _rms_bwd(dn, xh, r, g):
    dxh = dn * g
    return r * (dxh - xh * jnp.mean(dxh * xh, axis=-1, keepdims=True)), jnp.sum(dn * xh, axis=0, keepdims=True)


_ZQ = C_Q_RANK + C_KV_RANK + HEAD_PAD
_HW = C_HEADS * HEAD_PAD


def mla_pre_fwd(z, gq, gkv, wq, wk, wv, cs, s1, s2):
    s = z.shape[0]
    tm = _tile(s, 256, 8)

    def body(z_ref, gq_ref, gkv_ref, wq_ref, wk_ref, wv_ref, cs_ref, s1_ref, s2_ref, q_ref, k_ref, v_ref):
        zv = z_ref[...].astype(F32)
        cst, s1t, s2t = cs_ref[...], s1_ref[...], s2_ref[...]
        qh, _ = _rms(zv[:, :C_Q_RANK], None)
        qn = (qh * gq_ref[...]).astype(BF16)
        q = _dot(qn, wq_ref[...])
        kh, _ = _rms(zv[:, C_Q_RANK:C_Q_RANK + C_KV_RANK], None)
        kvn = (kh * gkv_ref[...]).astype(BF16)
        k = _dot(kvn, wk_ref[...])
        v_ref[...] = _dot(kvn, wv_ref[...]).astype(BF16)
        kpe = _rope(zv[:, C_Q_RANK + C_KV_RANK:], cst, s1t, s2t)
        for h in range(C_HEADS):
            sl = slice(h * HEAD_PAD, (h + 1) * HEAD_PAD)
            q_ref[:, sl] = _rope(q[:, sl], cst, s1t, s2t).astype(BF16)
            k_ref[:, sl] = (k[:, sl] + kpe).astype(BF16)

    full = lambda r, c: pl.BlockSpec((r, c), lambda i: (0, 0))
    hw = _sds((s, _HW), BF16)
    return pl.pallas_call(
        body, name="mla_pre_fwd", out_shape=(hw, hw, hw), grid=(s // tm,),
        in_specs=[_rows(tm, _ZQ), _vec(C_Q_RANK), _vec(C_KV_RANK), full(C_Q_RANK, _HW), full(C_KV_RANK, _HW),
                  full(C_KV_RANK, _HW), _rows(tm, LANES), _rows(tm, LANES), _rows(tm, LANES)],
        out_specs=(_rows(tm, _HW), _rows(tm, _HW), _rows(tm, _HW)),
        compiler_params=_params(("parallel",), VMEM_BIG),
    )(z, gq, gkv, wq, wk, wv, cs, s1, s2)


def mla_pre_bwd(z, gq, gkv, wq, wk, wv, cs, s1, s2, dq, dk, dv):
    s = z.shape[0]
    tm = _tile(s, 256, 8)

    def body(z_ref, gq_ref, gkv_ref, wq_ref, wk_ref, wv_ref, cs_ref, s1_ref, s2_ref, dq_ref, dk_ref, dv_ref,
             dz_ref, dwq_ref, dwk_ref, dwv_ref, dgq_ref, dgkv_ref):
        @pl.when(pl.program_id(0) == 0)
        def _():
            dwq_ref[...] = jnp.zeros_like(dwq_ref)
            dwk_ref[...] = jnp.zeros_like(dwk_ref)
            dwv_ref[...] = jnp.zeros_like(dwv_ref)
            dgq_ref[...] = jnp.zeros_like(dgq_ref)
            dgkv_ref[...] = jnp.zeros_like(dgkv_ref)

        zv = z_ref[...].astype(F32)
        cst, s1t, s2t = cs_ref[...], s1_ref[...], s2_ref[...]
        gqv, gkvv = gq_ref[...], gkv_ref[...]
        qh, rq = _rms(zv[:, :C_Q_RANK], None)
        qn = (qh * gqv).astype(BF16)
        kh, rk = _rms(zv[:, C_Q_RANK:C_Q_RANK + C_KV_RANK], None)
        kvn = (kh * gkvv).astype(BF16)

        dqv = dq_ref[...].astype(F32)
        dqp = jnp.concatenate(
            [_rope_t(dqv[:, h * HEAD_PAD:(h + 1) * HEAD_PAD], cst, s1t, s2t) for h in range(C_HEADS)], axis=1
        ).astype(BF16)
        dwq_ref[...] += _dot(qn, dqp, _TN)
        dqn = _dot(dqp, wq_ref[...], _NT)
        dql, dgq = _rms_bwd(dqn, qh, rq, gqv)
        dgq_ref[...] += dgq

        dkv = dk_ref[...]
        dkb = dkv.astype(BF16)
        dvb = dv_ref[...].astype(BF16)
        dwk_ref[...] += _dot(kvn, dkb, _TN)
        dwv_ref[...] += _dot(kvn, dvb, _TN)
        dkvn = _dot(dkb, wk_ref[...], _NT) + _dot(dvb, wv_ref[...], _NT)
        dkl, dgkv = _rms_bwd(dkvn, kh, rk, gkvv)
        dgkv_ref[...] += dgkv

        dkpe = dkv[:, :HEAD_PAD]
        for h in range(1, C_HEADS):
            dkpe = dkpe + dkv[:, h * HEAD_PAD:(h + 1) * HEAD_PAD]
        dkpe = _rope_t(jnp.where(_kpe_mask(dkpe.shape), dkpe, 0.0), cst, s1t, s2t)
        dz_ref[...] = jnp.concatenate([dql, dkl, dkpe], axis=1).astype(BF16)

    full = lambda r, c: pl.BlockSpec((r, c), lambda i: (0, 0))
    return pl.pallas_call(
        body, name="mla_pre_bwd",
        out_shape=(_sds((s, _ZQ), BF16), _sds((C_Q_RANK, _HW)), _sds((C_KV_RANK, _HW)), _sds((C_KV_RANK, _HW)),
                   _sds((1, C_Q_RANK)), _sds((1, C_KV_RANK))),
        grid=(s // tm,),
        in_specs=[_rows(tm, _ZQ), _vec(C_Q_RANK), _vec(C_KV_RANK), full(C_Q_RANK, _HW), full(C_KV_RANK, _HW),
                  full(C_KV_RANK, _HW), _rows(tm, LANES), _rows(tm, LANES), _rows(tm, LANES),
                  _rows(tm, _HW), _rows(tm, _HW), _rows(tm, _HW)],
        out_specs=(_rows(tm, _ZQ), full(C_Q_RANK, _HW), full(C_KV_RANK, _HW), full(C_KV_RANK, _HW),
                   _vec(C_Q_RANK), _vec(C_KV_RANK)),
        compiler_params=_params(("arbitrary",), VMEM_BIG),
    )(z, gq, gkv, wq, wk, wv, cs, s1, s2, dq, dk, dv)


_ATT_SCALE = (C_NOPE + C_ROPE) ** -0.5
_NEG = -1e30


def _att_exp(q, k, row0, ends_here):
    sc = _dot(q, k, _NT) * _ATT_SCALE
    tq, nk = sc.shape
    if ends_here:
        last = sc[:, nk - tq:]
        row = lax.broadcasted_iota(jnp.int32, last.shape, 0)
        col = lax.broadcasted_iota(jnp.int32, last.shape, 1)
        last = jnp.where(col <= row, last, _NEG)
        sc = last if nk == tq else jnp.concatenate([sc[:, :nk - tq], last], axis=1)
    else:
        qpos = row0 + lax.broadcasted_iota(jnp.int32, sc.shape, 0)
        kpos = lax.broadcasted_iota(jnp.int32, sc.shape, 1)
        sc = jnp.where(kpos <= qpos, sc, _NEG)
    e = jnp.exp(sc - jnp.max(sc, axis=-1, keepdims=True))
    return e, 1.0 / jnp.sum(e, axis=-1, keepdims=True)


def _causal_cases(i, nq, tq, fn):
    if nq > 8:
        fn(nq * tq, False)
        return
    for blk in range(nq):
        pl.when(i == blk)(functools.partial(fn, (blk + 1) * tq, True))


def attn_fwd(q, k, v):
    s = q.shape[0]
    tq = _tile(s, 256, 8)
    nq = s // tq

    def body(q_ref, k_ref, v_ref, o_ref):
        i = pl.program_id(1)

        def case(nk, ends_here):
            e, inv = _att_exp(q_ref[...], k_ref[:nk, :], i * tq, ends_here)
            o_ref[...] = (_dot(e.astype(BF16), v_ref[:nk, :]) * inv).astype(BF16)

        _causal_cases(i, nq, tq, case)

    qspec = pl.BlockSpec((tq, HEAD_PAD), lambda h, i: (i, h))
    kspec = pl.BlockSpec((s, HEAD_PAD), lambda h, i: (0, h))
    return pl.pallas_call(
        body, name="attn_fwd", out_shape=_sds((s, _HW), BF16), grid=(C_HEADS, s // tq),
        in_specs=[qspec, kspec, kspec], out_specs=qspec,
        compiler_params=_params(("parallel", "parallel"), VMEM_BIG),
    )(q, k, v)


def attn_bwd(q, k, v, o, do_all, do_col0):
    s = q.shape[0]
    tq = _tile(s, 256, 8)

    def body(q_ref, k_ref, v_ref, o_ref, do_ref, dq_ref, dk_ref, dv_ref):
        i = pl.program_id(1)

        @pl.when(i == 0)
        def _():
            dk_ref[...] = jnp.zeros_like(dk_ref)
            dv_ref[...] = jnp.zeros_like(dv_ref)

        def case(nk, ends_here):
            qv, kv, vv, dov = q_ref[...], k_ref[:nk, :], v_ref[:nk, :], do_ref[...]
            e, inv = _att_exp(qv, kv, i * tq, ends_here)
            p = e * inv
            dp = _dot(dov, vv, _NT)
            delta = jnp.sum(dov.astype(F32) * o_ref[...].astype(F32), axis=-1, keepdims=True)
            ds = (p * (dp - delta) * _ATT_SCALE).astype(BF16)
            dq_ref[...] = _dot(ds, kv).astype(BF16)
            dk_ref[:nk, :] += _dot(ds, qv, _TN)
            dv_ref[:nk, :] += _dot(p.astype(BF16), dov, _TN)

        _causal_cases(i, s // tq, tq, case)

    qspec = pl.BlockSpec((tq, HEAD_PAD), lambda h, i: (i, h))
    dospec = pl.BlockSpec((tq, HEAD_PAD), lambda h, i: (i, do_col0 + h))
    kspec = pl.BlockSpec((s, HEAD_PAD), lambda h, i: (0, h))
    return pl.pallas_call(
        body, name="attn_bwd", out_shape=(_sds((s, _HW), BF16), _sds((s, _HW)), _sds((s, _HW))),
        grid=(C_HEADS, s // tq),
        in_specs=[qspec, kspec, kspec, qspec, dospec], out_specs=(qspec, kspec, kspec),
        compiler_params=_params(("parallel", "arbitrary"), VMEM_BIG),
    )(q, k, v, o, do_all)


_DW = D_GROUPS * LANES


def _tril_bf16(w):
    r = lax.broadcasted_iota(jnp.int32, w.shape, 0)
    c = lax.broadcasted_iota(jnp.int32, w.shape, 1)
    return jnp.where(c <= r, w, 0.0).astype(BF16)


def _sgu_forward(zu, zv, lg, lb, ws_ref, bs):
    u = _gelu(zu)
    v = _gelu(zv)
    mu = jnp.mean(v, axis=-1, keepdims=True)
    vc = v - mu
    rstd = lax.rsqrt(jnp.mean(vc * vc, axis=-1, keepdims=True) + EPS)
    xh = vc * rstd
    vln = (xh * lg + lb).astype(BF16)
    mixed = []
    for g in range(D_GROUPS):
        wg = _tril_bf16(ws_ref[g])
        mixed.append(_dot(wg, vln[:, g * LANES:(g + 1) * LANES]) + bs[:, g:g + 1])
    return u, xh, rstd, vln, jnp.concatenate(mixed, axis=1)


def sgu_fwd(z, lg, lb, ws, bs_t):
    s = z.shape[0]
    nchunk = s // D_CHUNK

    def body(zu_ref, zv_ref, lg_ref, lb_ref, ws_ref, bs_ref, o_ref):
        u, _, _, _, mixed = _sgu_forward(zu_ref[...].astype(F32), zv_ref[...].astype(F32), lg_ref[...], lb_ref[...],
                                         ws_ref, bs_ref[...])
        o_ref[...] = (u * mixed).astype(BF16)

    return pl.pallas_call(
        body, name="sgu_fwd", out_shape=_sds((s, _DW), BF16), grid=(nchunk,),
        in_specs=[pl.BlockSpec((D_CHUNK, _DW), lambda n: (n, 1)), pl.BlockSpec((D_CHUNK, _DW), lambda n: (n, 2)),
                  _vec(_DW), _vec(_DW), pl.BlockSpec((D_GROUPS, D_CHUNK, D_CHUNK), lambda n: (0, 0, 0)),
                  pl.BlockSpec((D_CHUNK, LANES), lambda n: (0, 0))],
        out_specs=pl.BlockSpec((D_CHUNK, _DW), lambda n: (n, 0)),
        compiler_params=_params(("parallel",)),
    )(z, z, lg, lb, ws, bs_t)


def sgu_bwd(z, lg, lb, ws, bs_t, dycat, dy_col):
    s = z.shape[0]
    nchunk = s // D_CHUNK

    def body(zu_ref, zv_ref, lg_ref, lb_ref, ws_ref, bs_ref, dy_ref, dzu_ref, dzv_ref, dws_ref, dbs_ref, dlg_ref,
             dlb_ref):
        @pl.when(pl.program_id(0) == 0)
        def _():
            dws_ref[...] = jnp.zeros_like(dws_ref)
            dbs_ref[...] = jnp.zeros_like(dbs_ref)
            dlg_ref[...] = jnp.zeros_like(dlg_ref)
            dlb_ref[...] = jnp.zeros_like(dlb_ref)

        zu, zv, lg = zu_ref[...].astype(F32), zv_ref[...].astype(F32), lg_ref[...]
        u, xh, rstd, vln, mixed = _sgu_forward(zu, zv, lg, lb_ref[...], ws_ref, bs_ref[...])
        dy = dy_ref[...].astype(F32)
        dzu_ref[...] = (dy * mixed * _gelu_grad(zu)).astype(BF16)
        dmix = dy * u
        lane = lax.broadcasted_iota(jnp.int32, (D_CHUNK, LANES), 1)
        row = lax.broadcasted_iota(jnp.int32, (D_CHUNK, D_CHUNK), 0)
        colm = lax.broadcasted_iota(jnp.int32, (D_CHUNK, D_CHUNK), 1)
        dvln = []
        dbs = jnp.zeros((D_CHUNK, LANES), F32)
        for g in range(D_GROUPS):
            sl = slice(g * LANES, (g + 1) * LANES)
            dmg = dmix[:, sl]
            dbs = dbs + jnp.where(lane == g, jnp.sum(dmg, axis=-1, keepdims=True), 0.0)
            dmb = dmg.astype(BF16)
            dws_ref[g] += jnp.where(colm <= row, _dot(dmb, vln[:, sl], _NT), 0.0)
            dvln.append(_dot(_tril_bf16(ws_ref[g]), dmb, _TN))
        dbs_ref[...] += dbs
        dvln = jnp.concatenate(dvln, axis=1)
        dlg_ref[...] += jnp.sum(dvln * xh, axis=0, keepdims=True)
        dlb_ref[...] += jnp.sum(dvln, axis=0, keepdims=True)
        dxh = dvln * lg
        dvv = rstd * (dxh - jnp.mean(dxh, axis=-1, keepdims=True) - xh * jnp.mean(dxh * xh, axis=-1, keepdims=True))
        dzv_ref[...] = (dvv * _gelu_grad(zv)).astype(BF16)

    wsspec = pl.BlockSpec((D_GROUPS, D_CHUNK, D_CHUNK), lambda n: (0, 0, 0))
    chunk = lambda cidx: pl.BlockSpec((D_CHUNK, _DW), lambda n: (n, cidx))
    return pl.pallas_call(
        body, name="sgu_bwd",
        out_shape=(_sds((s, _DW), BF16), _sds((s, _DW), BF16), _sds((D_GROUPS, D_CHUNK, D_CHUNK)),
                   _sds((D_CHUNK, LANES)), _sds((1, _DW)), _sds((1, _DW))),
        grid=(nchunk,),
        in_specs=[chunk(1), chunk(2), _vec(_DW), _vec(_DW), wsspec, pl.BlockSpec((D_CHUNK, LANES), lambda n: (0, 0)),
                  chunk(dy_col)],
        out_specs=(chunk(0), chunk(0), wsspec, pl.BlockSpec((D_CHUNK, LANES), lambda n: (0, 0)), _vec(_DW), _vec(_DW)),
        compiler_params=_params(("arbitrary",)),
    )(z, z, lg, lb, ws, bs_t, dycat)


def ada_mod(c_all, ada_w, ada_b):
    nl, d, n = ada_w.shape
    nb = c_all.shape[0]
    tn = _tile(n, 512)

    def body(c_ref, w_ref, b_ref, o_ref):
        cv = c_ref[...]
        ca = (cv * _sigmoid(cv)).astype(BF16)
        o_ref[...] = _dot(ca, w_ref[...].astype(BF16)) + b_ref[...]

    return pl.pallas_call(
        body, name="ada_mod", out_shape=_sds((nl, nb, n)), grid=(nl, n // tn),
        in_specs=[pl.BlockSpec((nb, d), lambda l, j: (0, 0)), pl.BlockSpec((None, d, tn), lambda l, j: (l, 0, j)),
                  pl.BlockSpec((None, 1, tn), lambda l, j: (l, 0, j))],
        out_specs=pl.BlockSpec((None, nb, tn), lambda l, j: (l, 0, j)),
        compiler_params=_params(("parallel", "parallel")),
    )(c_all, ada_w, ada_b.reshape(nl, 1, n))


def ada_grad(c_all_t, dmod):
    d, nb = c_all_t.shape
    nl, _, n = dmod.shape
    tn = _tile(n, 512)
    tr = _tile(d, 256, 8)

    def body(c_ref, dm_ref, o_ref):
        cv = c_ref[...]
        ca = cv * _sigmoid(cv)
        dm = dm_ref[...]
        acc = ca[:, 0:1] * dm[0:1, :]
        for b in range(1, nb):
            acc = acc + ca[:, b:b + 1] * dm[b:b + 1, :]
        o_ref[...] = acc

    return pl.pallas_call(
        body, name="ada_grad", out_shape=_sds((nl, d, n)), grid=(nl, n // tn, d // tr),
        in_specs=[pl.BlockSpec((tr, nb), lambda l, j, r: (r, 0)), pl.BlockSpec((None, nb, tn), lambda l, j, r: (l, 0, j))],
        out_specs=pl.BlockSpec((None, tr, tn), lambda l, j, r: (l, r, j)),
        compiler_params=_params(("parallel", "parallel", "parallel")),
    )(c_all_t, dmod)


_ADAM_BLOCK = 256 * 1024


def _adam_rows(rows, cols):
    if rows * cols <= _ADAM_BLOCK or rows % 8:
        return rows
    return _tile(rows, max(8, _ADAM_BLOCK // cols), 8)


def _adam_update(w, gv, m, v):
    inv_bc1 = 1.0 / (1.0 - ADAM_B1 ** ADAM_STEP)
    inv_bc2 = 1.0 / (1.0 - ADAM_B2 ** ADAM_STEP)
    nm = ADAM_B1 * m + (1.0 - ADAM_B1) * gv
    nv = ADAM_B2 * v + (1.0 - ADAM_B2) * (gv * gv)
    return -ADAM_LR * ((nm * inv_bc1) / (jnp.sqrt(nv * inv_bc2) + ADAM_EPS) + ADAM_WD * w), nm, nv


def adamw(w, g, m, v, name, copy_grad=False):
    shape = w.shape
    cols = shape[-1]
    rows = w.size // cols
    tr = _adam_rows(rows, cols)

    def body(w_ref, g_ref, m_ref, v_ref, d_ref, nm_ref, nv_ref, *go_ref):
        gv = g_ref[...]
        d_ref[...], nm_ref[...], nv_ref[...] = _adam_update(w_ref[...], gv, m_ref[...], v_ref[...])
        if copy_grad:
            go_ref[0][...] = gv

    spec = pl.BlockSpec((tr, cols), lambda i: (i, 0))
    out = _sds((rows, cols))
    r2 = lambda t: t.reshape(rows, cols)
    nout = 4 if copy_grad else 3
    res = pl.pallas_call(
        body, name=name, out_shape=(out,) * nout, grid=(rows // tr,),
        in_specs=[spec] * 4, out_specs=(spec,) * nout, compiler_params=_params(("parallel",)),
    )(r2(w), r2(g), r2(m), r2(v))
    grad = res[3] if copy_grad else g
    return grad.reshape(shape), res[0].reshape(shape), res[1].reshape(shape), res[2].reshape(shape)


def adamw_small(ws, gs, ms, vs):
    n = len(ws)
    flat = lambda t: t.reshape(-1, t.shape[-1])

    def body(*refs):
        ins, outs = refs[:4 * n], refs[4 * n:]
        for i in range(n):
            w_ref, g_ref, m_ref, v_ref = ins[4 * i:4 * i + 4]
            outs[3 * i][...], outs[3 * i + 1][...], outs[3 * i + 2][...] = _adam_update(
                w_ref[...], g_ref[...], m_ref[...], v_ref[...])

    operands = [flat(t) for quad in zip(ws, gs, ms, vs) for t in quad]
    res = pl.pallas_call(
        body, name="adamw_small", out_shape=tuple(_sds(flat(w).shape) for w in ws for _ in range(3)),
    )(*operands)
    return [(g, res[3 * i].reshape(w.shape), res[3 * i + 1].reshape(w.shape), res[3 * i + 2].reshape(w.shape))
            for i, (w, g) in enumerate(zip(ws, gs))]


def adamw_layers(w, g0, g1, m, v, name):
    _, rows, cols = w.shape
    tr = _adam_rows(rows, cols)

    def body(w_ref, g0_ref, g1_ref, m_ref, v_ref, g_ref, d_ref, nm_ref, nv_ref):
        gv = jnp.where(pl.program_id(0) == 0, g0_ref[...], g1_ref[...])
        g_ref[...] = gv
        d_ref[...], nm_ref[...], nv_ref[...] = _adam_update(w_ref[...], gv, m_ref[...], v_ref[...])

    spec = pl.BlockSpec((None, tr, cols), lambda l, i: (l, i, 0))
    gspec = pl.BlockSpec((tr, cols), lambda l, i: (i, 0))
    out = _sds((2, rows, cols))
    return pl.pallas_call(
        body, name=name, out_shape=(out, out, out, out), grid=(2, rows // tr),
        in_specs=[spec, gspec, gspec, spec, spec], out_specs=(spec,) * 4, compiler_params=_params(("parallel", "parallel")),
    )(w, g0, g1, m, v)


def sum8(gathered):
    _, r, _ = gathered.shape
    tr = _tile(r, 512, 8)

    def body(g_ref, o_ref):
        acc = g_ref[0]
        for dev in range(1, N_DEV):
            acc = acc + g_ref[dev]
        o_ref[...] = acc

    return pl.pallas_call(
        body, name="sum8", out_shape=_sds((r, LANES)), grid=(r // tr,),
        in_specs=[pl.BlockSpec((N_DEV, tr, LANES), lambda i: (0, i, 0))], out_specs=pl.BlockSpec((tr, LANES), lambda i: (i, 0)),
        compiler_params=_params(("parallel",)),
    )(gathered)


_SUM_STEPS = 2


def pair_sums(gs, recvs, core, name):
    n = len(gs)
    trs = [g.shape[1] // 2 // _SUM_STEPS for g in gs]

    def body(c_ref, *refs):
        del c_ref
        for i in range(n):
            a_ref, b_ref, o_ref = refs[2 * i], refs[2 * i + 1], refs[2 * n + i]
            o_ref[...] = (a_ref[...].astype(F32) + b_ref[...].astype(F32)).astype(BF16)

    in_specs, out_specs = [], []
    for g, tr in zip(gs, trs):
        cols = g.shape[2]
        in_specs.append(pl.BlockSpec((None, tr, cols), lambda k, s, c: (k, c[0] * _SUM_STEPS + s, 0)))
        in_specs.append(pl.BlockSpec((None, tr, cols), lambda k, s, c: (k, s, 0)))
        out_specs.append(pl.BlockSpec((None, tr, cols), lambda k, s, c: (k, s, 0)))
    grid_spec = pltpu.PrefetchScalarGridSpec(num_scalar_prefetch=1, grid=(N_CHIPS, _SUM_STEPS), in_specs=in_specs,
                                             out_specs=tuple(out_specs))
    return list(pl.pallas_call(
        body, name=name, out_shape=tuple(_sds((N_CHIPS, g.shape[1] // 2, g.shape[2]), BF16) for g in gs),
        grid_spec=grid_spec, compiler_params=_params(("parallel", "parallel")),
    )(core.reshape(1).astype(jnp.int32), *[t for pair in zip(gs, recvs) for t in pair]))


def chip_sums(pairs, recvs, chip, core, name):
    n = len(pairs)
    trs = [p.shape[1] // _SUM_STEPS for p in pairs]

    def body(p_ref, *refs):
        del p_ref
        for i in range(n):
            own_ref, r_ref, o_ref = refs[2 * i], refs[2 * i + 1], refs[2 * n + i]
            acc = own_ref[...].astype(F32)
            for j in range(N_CHIPS - 1):
                acc = acc + r_ref[j].astype(F32)
            o_ref[...] = acc

    in_specs, out_specs = [], []
    for p, tr in zip(pairs, trs):
        cols = p.shape[2]
        in_specs.append(pl.BlockSpec((None, tr, cols), lambda s, q: (q[0], s, 0)))
        in_specs.append(pl.BlockSpec((N_CHIPS - 1, tr, cols), lambda s, q: (0, s, 0)))
        out_specs.append(pl.BlockSpec((None, tr, cols), lambda s, q: (q[1], s, 0)))
    grid_spec = pltpu.PrefetchScalarGridSpec(num_scalar_prefetch=1, grid=(_SUM_STEPS,), in_specs=in_specs,
                                             out_specs=tuple(out_specs))
    return list(pl.pallas_call(
        body, name=name, out_shape=tuple(_sds((2,) + p.shape[1:]) for p in pairs), grid_spec=grid_spec,
        compiler_params=_params(("parallel",)),
    )(jnp.stack([chip, core]).astype(jnp.int32), *[t for pair in zip(pairs, recvs) for t in pair]))


def _place():
    return lax.axis_index("x"), lax.axis_index("y"), lax.axis_index("c")


def _other_chips(x, y):
    return [(x, 1 - y), (1 - x, y), (1 - x, 1 - y)]


_HBM = pl.BlockSpec(memory_space=pltpu.HBM)


def all_gather8(v, name):
    m, n = v.shape

    def body(x_ref, out_ref, send_sems, recv_sems, local_sem):
        x, y, c = _place()
        me, sibling = (x, y, c), (x, y, 1 - c)
        chips = _other_chips(x, y)

        def rows(px, py, pc):
            return out_ref.at[pl.ds((4 * px + 2 * py + pc) * m, m), :]

        def copy(k, block, to, src=None):
            return pltpu.make_async_remote_copy(
                src_ref=rows(*block) if src is None else src, dst_ref=rows(*block),
                send_sem=send_sems.at[k], recv_sem=recv_sems.at[k], device_id=to, device_id_type=MESH)

        mine = pltpu.make_async_copy(x_ref, rows(*me), local_sem)
        mine.start()
        first = [copy(0, me, sibling, src=x_ref)]
        first += [copy(1 + j, me, (*chip, c), src=x_ref) for j, chip in enumerate(chips)]
        for cp in first:
            cp.start()
        passed = [copy(4 + j, (*chip, c), sibling) for j, chip in enumerate(chips)]
        for j, chip in enumerate(chips):
            copy(1 + j, (*chip, c), me).wait_recv()
            passed[j].start()
        copy(0, sibling, me).wait_recv()
        for j, chip in enumerate(chips):
            copy(4 + j, (*chip, 1 - c), me).wait_recv()
        for cp in first + passed:
            cp.wait_send()
        mine.wait()

    return pl.pallas_call(
        body, name=name, out_shape=_sds((N_DEV * m, n), v.dtype),
        in_specs=[pl.BlockSpec(memory_space=pltpu.VMEM)], out_specs=pl.BlockSpec(memory_space=pltpu.VMEM),
        scratch_shapes=[pltpu.SemaphoreType.DMA((7,)), pltpu.SemaphoreType.DMA((7,)), pltpu.SemaphoreType.DMA],
        compiler_params=_params(None, VMEM_BIG),
    )(v)


def _comm_call(body, name, ins, out_shapes, nsem, aliases=None):
    return pl.pallas_call(
        body, name=name, out_shape=tuple(out_shapes), in_specs=[_HBM] * len(ins), out_specs=tuple([_HBM] * len(out_shapes)),
        scratch_shapes=[pltpu.SemaphoreType.DMA((nsem,)), pltpu.SemaphoreType.DMA((nsem,))],
        input_output_aliases=aliases or {},
    )(*ins)


def _remote(src, dst, send_sems, recv_sems, k, to):
    return pltpu.make_async_remote_copy(src_ref=src, dst_ref=dst, send_sem=send_sems.at[k], recv_sem=recv_sems.at[k],
                                        device_id=to, device_id_type=MESH)


def _half(core, rh):
    return pl.ds(pl.multiple_of(core * rh, 16), rh)


def swap_halves(gs, name):
    n = len(gs)

    def body(*refs):
        ins, outs, (send_sems, recv_sems) = refs[:n], refs[n:2 * n], refs[2 * n:]
        x, y, c = _place()
        copies = []
        for i in range(n):
            theirs = _half(1 - c, ins[i].shape[1] // 2)
            cp = _remote(ins[i].at[:, theirs], outs[i], send_sems, recv_sems, i, (x, y, 1 - c))
            cp.start()
            copies.append(cp)
        for cp in copies:
            cp.wait()

    return _comm_call(body, name, gs, [_sds((g.shape[0], g.shape[1] // 2, g.shape[2]), g.dtype) for g in gs], n)


def join_halves(bufs, name):
    n = len(bufs)

    def body(*refs):
        ins, outs, (send_sems, recv_sems) = refs[:n], refs[n:2 * n], refs[2 * n:]
        x, y, c = _place()
        copies = []
        for i in range(n):
            cp = _remote(ins[i].at[c], outs[i].at[c], send_sems, recv_sems, i, (x, y, 1 - c))
            cp.start()
            copies.append(cp)
        for i in range(n):
            theirs = outs[i].at[1 - c]
            _remote(theirs, theirs, send_sems, recv_sems, i, (x, y, 1 - c)).wait_recv()
        for cp in copies:
            cp.wait_send()

    return _comm_call(body, name, bufs, [_sds(b.shape, b.dtype) for b in bufs], n, {i: i for i in range(n)})


def forward_halves(lands, name):
    n = len(lands)

    def body(*refs):
        ins, outs, (send_sems, recv_sems) = refs[:n], refs[n:2 * n], refs[2 * n:]
        x, y, c = _place()
        sibling = (x, y, 1 - c)
        chips = _other_chips(x, y)
        copies = []
        for i in range(n):
            mine = _half(c, ins[i].shape[1] // 2)
            for j, (px, py) in enumerate(chips):
                cp = _remote(ins[i].at[2 * px + py, mine], outs[i].at[2 * px + py, mine], send_sems, recv_sems, 3 * i + j, sibling)
                cp.start()
                copies.append(cp)
        for i in range(n):
            theirs = _half(1 - c, ins[i].shape[1] // 2)
            for j, (px, py) in enumerate(chips):
                landed = outs[i].at[2 * px + py, theirs]
                _remote(landed, landed, send_sems, recv_sems, 3 * i + j, sibling).wait_recv()
        for cp in copies:
            cp.wait_send()

    return _comm_call(body, name, lands, [_sds(b.shape, b.dtype) for b in lands], 3 * n, {i: i for i in range(n)})


_SEM = pl.BlockSpec(memory_space=pltpu.SEMAPHORE)
_EFFECT = pltpu.SideEffectType.DATAFLOW_SIDE_EFFECTING


def _gather_copies(srcs, lands, send_sems, recv_sems):
    x, y, c = _place()
    copies = []
    for i in range(len(srcs)):
        mine = _half(c, srcs[i].shape[0] // 2)
        for j, chip in enumerate(_other_chips(x, y)):
            copies.append(_remote(srcs[i].at[mine], lands[i].at[2 * x + y, mine], send_sems, recv_sems, 3 * i + j, (*chip, c)))
    return copies


def _exchange_copies(srcs, lands, send_sems, recv_sems):
    x, y, c = _place()
    copies = []
    for i in range(len(srcs)):
        for j, (px, py) in enumerate(_other_chips(x, y)):
            copies.append(_remote(srcs[i].at[2 * px + py], lands[i].at[j], send_sems, recv_sems, 3 * i + j, (px, py, c)))
    return copies


def _everyone_copies(srcs, lands, send_sems, recv_sems):
    x, y, c = _place()
    flip = lambda v, b: 1 - v if b else v
    dst = lands[0].at[4 * x + 2 * y + c]
    return [_remote(srcs[0], dst, send_sems, recv_sems, j - 1, (flip(x, j & 4), flip(y, j & 2), flip(c, j & 1)))
            for j in range(1, N_DEV)]


GATHER = (_gather_copies, 3)
EXCHANGE = (_exchange_copies, 3)
EVERYONE = (_everyone_copies, N_DEV - 1)


def split_start(name, plan, srcs, land_shapes, after=()):
    copies_fn, per_source = plan
    n, m, k = len(srcs), len(land_shapes), len(after)
    ncopies = per_source * n

    def body(*refs):
        src_refs, land_refs = refs[:n], refs[n:n + m]
        send_sems, recv_sems = refs[n + m + k], refs[n + m + k + 1]
        token = refs[-1]
        for cp in copies_fn(src_refs, land_refs, send_sems, recv_sems):
            cp.start()
        token[...] = jnp.zeros_like(token)

    hbm = lambda s: pltpu.HBM(tuple(s.shape), s.dtype)
    outs = pl.pallas_call(
        body, name=name,
        out_shape=(pltpu.SemaphoreType.DMA((ncopies,)), pltpu.SemaphoreType.DMA((ncopies,)), *[hbm(s) for s in srcs],
                   *[hbm(s) for s in land_shapes], _sds((8, LANES))),
        in_specs=[_HBM] * (n + m) + [pl.BlockSpec(memory_space=pl.ANY)] * k,
        out_specs=(_SEM, _SEM, *([_HBM] * (n + m)), pl.BlockSpec(memory_space=pltpu.VMEM)),
        input_output_aliases={i: 2 + i for i in range(n + m)},
        compiler_params=pltpu.CompilerParams(has_side_effects=_EFFECT),
    )(*[pltpu.with_memory_space_constraint(s, pltpu.HBM) for s in srcs],
      *[pltpu.with_memory_space_constraint(lax.empty(tuple(s.shape), s.dtype), pltpu.HBM) for s in land_shapes], *after)
    handle = (outs[0], outs[1], list(outs[2:2 + n]), list(outs[2 + n:2 + n + m]))
    return handle, outs[-1][0, 0]


def split_wait(name, plan, handle, after):
    copies_fn, _ = plan
    send_sems, recv_sems, srcs, lands = handle
    n, m = len(srcs), len(lands)
    after = list(after) if isinstance(after, (list, tuple)) else [after]

    def body(*refs):
        src_refs, land_refs = refs[:n], refs[n:n + m]
        for cp in copies_fn(src_refs, land_refs, refs[n + m], refs[n + m + 1]):
            cp.wait_send()
            cp.wait_recv()

    hbm = lambda s: pltpu.HBM(tuple(s.shape), s.dtype)
    outs = pl.pallas_call(
        body, name=name, out_shape=tuple(hbm(s) for s in srcs + lands),
        in_specs=[_HBM] * (n + m) + [_SEM, _SEM] + [pl.BlockSpec(memory_space=pl.ANY)] * len(after),
        out_specs=tuple([_HBM] * (n + m)), input_output_aliases={i: i for i in range(n + m)},
        compiler_params=pltpu.CompilerParams(has_side_effects=_EFFECT),
    )(*srcs, *lands, send_sems, recv_sems, *after)
    return list(outs[:n]), list(outs[n:])


_CD_PAD = C_Q_RANK + C_KV_RANK + HEAD_PAD + 2 * _DW


def chip_major(w, groups=N_CHIPS):
    r, c = w.shape
    return w.reshape(r, groups, c // groups).transpose(1, 0, 2)


def from_chip_major(w):
    g, r, c = w.shape
    return w.transpose(1, 0, 2).reshape(r, g * c)


def _cd_in_pad(w):
    a = C_Q_RANK + C_KV_RANK
    z = lambda n: jnp.zeros((w.shape[0], n), w.dtype)
    return jnp.concatenate([w[:, :a], z(C_NOPE), w[:, a:a + C_ROPE], z(HEAD_PAD - C_NOPE - C_ROPE), w[:, a + C_ROPE:]], axis=1)


def _cd_in_unpad(w):
    a = C_Q_RANK + C_KV_RANK
    return jnp.concatenate([w[:, :a], w[:, a + C_NOPE:a + C_NOPE + C_ROPE], w[:, a + HEAD_PAD:]], axis=1)


def _pad_heads(w, width):
    r = w.shape[0]
    w = w.reshape(r, C_HEADS, width)
    return jnp.pad(w, ((0, 0), (0, 0), (0, HEAD_PAD - width))).reshape(r, _HW)


def _unpad_heads(w, width):
    r = w.shape[0]
    return w.reshape(r, C_HEADS, HEAD_PAD)[:, :, :width].reshape(r, C_HEADS * width)


_MATMUL_WEIGHTS = ("ab_w_in", "ab_w_out", "cd_w_in", "c_w_uq", "c_w_ukv", "cd_w_out", "ffn_w_up", "ffn_w_down")
_LAYER_STACKED = ("norm1_g", "norm2_g", "ffn_w_up", "ffn_conv_w", "ffn_w_down")
_ROW_VECTORS = ("b_scale", "c_q_norm_g", "c_kv_norm_g", "d_ln_g", "d_ln_b")


def full_to_local(p):
    q = {}
    for k, v in p.items():
        if k == "final_norm_g":
            v = v.reshape(1, -1)
        elif k not in _LAYER_STACKED and k not in _ROW_VECTORS:
            v = v[0]
        q[k] = v.astype(BF16) if k in _MATMUL_WEIGHTS else v
    return q


def local_to_full(g):
    q = {}
    for k, v in g.items():
        if k == "final_norm_g":
            q[k] = v.reshape(-1)
        elif k not in _LAYER_STACKED and k not in _ROW_VECTORS:
            q[k] = v[None]
        else:
            q[k] = v
    return q


def prepare_weights(p):
    q = dict(p)
    q["cd_w_in"] = _cd_in_pad(p["cd_w_in"])
    q["c_w_uq"] = _pad_heads(p["c_w_uq"], C_NOPE + C_ROPE)
    ukv = p["c_w_ukv"].reshape(C_KV_RANK, C_HEADS, C_NOPE + C_V)
    q["c_w_uk"] = _pad_heads(ukv[:, :, :C_NOPE].reshape(C_KV_RANK, -1), C_NOPE)
    q["c_w_uv"] = _pad_heads(ukv[:, :, C_NOPE:].reshape(C_KV_RANK, -1), C_V)
    wo = p["cd_w_out"]
    att_rows = jnp.pad(wo[:C_HEADS * C_V].reshape(C_HEADS, C_V, D_MODEL), ((0, 0), (0, HEAD_PAD - C_V), (0, 0)))
    q["cd_w_out"] = jnp.concatenate([att_rows.reshape(_HW, D_MODEL), wo[C_HEADS * C_V:]], axis=0)
    return q


def unprepare_grads(g):
    q = dict(g)
    q["cd_w_in"] = _cd_in_unpad(g["cd_w_in"])
    q["c_w_uq"] = _unpad_heads(g["c_w_uq"], C_NOPE + C_ROPE)
    uk = g.pop("c_w_uk").reshape(C_KV_RANK, C_HEADS, HEAD_PAD)[:, :, :C_NOPE]
    uv = g.pop("c_w_uv").reshape(C_KV_RANK, C_HEADS, HEAD_PAD)[:, :, :C_V]
    q.pop("c_w_uk", None)
    q.pop("c_w_uv", None)
    q["c_w_ukv"] = jnp.concatenate([uk, uv], axis=-1).reshape(C_KV_RANK, C_HEADS * (C_NOPE + C_V))
    wo = g["cd_w_out"]
    att = wo[:_HW].reshape(C_HEADS, HEAD_PAD, D_MODEL)[:, :C_V].reshape(C_HEADS * C_V, D_MODEL)
    q["cd_w_out"] = jnp.concatenate([att, wo[_HW:]], axis=0)
    return q


def rope_tables(positions):
    half = C_ROPE // 2
    inv_freq = ROPE_THETA ** (-jnp.arange(half, dtype=F32) / half)
    ang = positions.astype(F32)[:, None] * inv_freq
    cos, sin = jnp.cos(ang), jnp.sin(ang)
    s = positions.shape[0]
    z = lambda n: jnp.zeros((s, n), F32)
    cs = jnp.concatenate([jnp.ones((s, C_NOPE), F32), cos, cos, z(HEAD_PAD - C_NOPE - C_ROPE)], axis=1)
    s1 = jnp.concatenate([z(C_NOPE), -sin, z(HEAD_PAD - C_NOPE - half)], axis=1)
    s2 = jnp.concatenate([z(C_NOPE + half), sin, z(HEAD_PAD - C_NOPE - C_ROPE)], axis=1)
    return cs, s1, s2


def _mods(mod_l):
    return [mod_l[:, i * D_MODEL:(i + 1) * D_MODEL] for i in range(N_MOD)]


_UP_COLS = 2 * D_FF // N_CHIPS


def ffn_fwd(h2, w, l, late_down=None):
    zf = matmul(h2, w["ffn_w_up"][l], "nn", BF16, f"ffn_up{l}", gb=N_CHIPS, go=2, tn=_UP_COLS)
    a = ffn_act_fwd(zf, w["ffn_conv_w"][l], f"ffn_act_fwd{l}")
    if late_down is not None:
        late_down(a)
    f = matmul(a, w["ffn_w_down"][l], "nn", F32, f"ffn_down{l}", tk=D_FF)
    return f, (zf, a)


def ffn_bwd(df, h2, saved, w, l):
    zf, a = saved
    da = matmul(df, w["ffn_w_down"][l], "nt", BF16, f"ffn_down_dx{l}", tn=D_FF // 2)
    d_down = matmul(a, df, "tn", BF16, f"ffn_down_dw{l}", tm=D_FF // 2)
    dzf, d_conv = ffn_act_bwd(zf, w["ffn_conv_w"][l], da, f"ffn_act_bwd{l}")
    dh2 = matmul(dzf, w["ffn_w_up"][l], "nt", F32, f"ffn_up_dx{l}", ga=2, gb=N_CHIPS, tk=_UP_COLS, tn=D_MODEL)
    d_up = matmul(h2, dzf, "tn", BF16, f"ffn_up_dw{l}", gb=2, go=N_CHIPS, tn=_UP_COLS)
    d_conv = d_conv.transpose(1, 0, 2).reshape(3, 2 * D_FF)
    return dh2, dict(ffn_w_down=d_down, ffn_conv_w=d_conv, ffn_w_up=d_up)


def mixer0_fwd(h, w):
    z = matmul(h, w["ab_w_in"], "nn", BF16, "ab_in", gb=N_CHIPS)
    ya = gconv_fwd(z, w["a_conv_w"])
    yb = pool_fwd(z, w["b_mix_w"], w["b_scale"])
    ycat = jnp.concatenate([ya] + yb, axis=1)
    y = matmul(ycat, w["ab_w_out"], "nn", F32, "ab_out", tn=D_MODEL)
    return y, (z, ycat)


def mixer0_bwd(dy, h, saved, w):
    z, ycat = saved
    grads = {}
    dycat = matmul(dy, w["ab_w_out"], "nt", BF16, "ab_out_dx")
    grads["ab_w_out"] = matmul(ycat, dy, "tn", BF16, "ab_out_dw")
    db, dc, da, d_conv = gconv_bwd(z, w["a_conv_w"], dycat)
    pb = pool_bwd(z, w["b_mix_w"], w["b_scale"], dycat)
    dz = jnp.concatenate([db, dc, da] + [t[0] for t in pb], axis=1)
    dh = matmul(dz, w["ab_w_in"], "nt", F32, "ab_in_dx", gb=N_CHIPS, tn=D_MODEL)
    grads["ab_w_in"] = matmul(h, dz, "tn", BF16, "ab_in_dw", go=N_CHIPS)
    grads.update(a_conv_w=d_conv, b_mix_w=jnp.stack([t[1] for t in pb]),
                 b_scale=jnp.concatenate([t[2] for t in pb], axis=1))
    return dh, grads


def mixer1_fwd(h, ropes, w):
    cs, s1, s2 = ropes
    z = matmul(h, w["cd_w_in"], "nn", BF16, "cd_in")
    bs_t = jnp.pad(w["d_b_s"].T, ((0, 0), (0, LANES - D_GROUPS)))
    qh, kh, vh = mla_pre_fwd(z, w["c_q_norm_g"], w["c_kv_norm_g"], w["c_w_uq"], w["c_w_uk"], w["c_w_uv"], cs, s1, s2)
    oh = attn_fwd(qh, kh, vh)
    yd = sgu_fwd(z, w["d_ln_g"], w["d_ln_b"], w["d_w_s"], bs_t)
    ycat = jnp.concatenate([oh, yd], axis=1)
    y = matmul(ycat, w["cd_w_out"], "nn", F32, "cd_out", tn=D_MODEL)
    return y, (z, bs_t, qh, kh, vh, oh, ycat)


def mixer1_bwd(dy, h, saved, ropes, w):
    cs, s1, s2 = ropes
    z, bs_t, qh, kh, vh, oh, ycat = saved
    grads = {}
    dycat = matmul(dy, w["cd_w_out"], "nt", BF16, "cd_out_dx")
    grads["cd_w_out"] = matmul(ycat, dy, "tn", F32, "cd_out_dw")
    dqh, dkh, dvh = attn_bwd(qh, kh, vh, oh, dycat, 0)
    dzq, d_uq, d_uk, d_uv, d_gq, d_gkv = mla_pre_bwd(
        z, w["c_q_norm_g"], w["c_kv_norm_g"], w["c_w_uq"], w["c_w_uk"], w["c_w_uv"], cs, s1, s2, dqh, dkh, dvh)
    dzu, dzv, d_ws, d_bs, d_lg, d_lb = sgu_bwd(z, w["d_ln_g"], w["d_ln_b"], w["d_w_s"], bs_t, dycat, _HW // _DW)
    dz = jnp.concatenate([dzq, dzu, dzv], axis=1)
    dh = matmul(dz, w["cd_w_in"], "nt", F32, "cd_in_dx", tn=D_MODEL)
    grads["cd_w_in"] = matmul(h, dz, "tn", F32, "cd_in_dw")
    grads.update(c_w_uq=d_uq, c_w_uk=d_uk, c_w_uv=d_uv, c_q_norm_g=d_gq, c_kv_norm_g=d_gkv, d_w_s=d_ws,
                 d_b_s=d_bs[:, :D_GROUPS].T, d_ln_g=d_lg, d_ln_b=d_lb)
    return dh, grads


class StepHooks:
    def weights(self, stage, after):
        pass

    def gradients(self, stage, grads, after):
        return 0.0


def run_step(x, tgt, mod, ropes, w, hooks):
    sh1a, sc1a, g1a, sh2a, sc2a, g2a = _mods(mod[0:1])
    sh1b, sc1b, g1b, sh2b, sc2b, g2b = _mods(mod[1:2])
    n1, n2 = w["norm1_g"], w["norm2_g"]

    hooks.weights("mix0", mod)
    h0 = modnorm_fwd(x, n1[0:1], sc1a, sh1a, "modnorm_0")
    y0, mix0 = mixer0_fwd(h0, w)
    x1, h1 = resid_modnorm_fwd(x, y0, g1a, n2[0:1], sc2a, sh2a, "resid_modnorm_1")
    hooks.weights("up0", x1)
    f0, ffn0 = ffn_fwd(h1, w, 0, lambda act: hooks.weights("down0", act))
    x2, h2 = resid_modnorm_fwd(x1, f0, g2a, n1[1:2], sc1b, sh1b, "resid_modnorm_2")
    hooks.weights("mix1", x2)
    y1, mix1 = mixer1_fwd(h2, ropes, w)
    x3, h3 = resid_modnorm_fwd(x2, y1, g1b, n2[1:2], sc2b, sh2b, "resid_modnorm_3")
    hooks.weights("ffn1", x3)
    f1, ffn1 = ffn_fwd(h3, w, 1)
    dres, d_final, loss, df1, dg2b = final_fused(x3, f1, g2b, w["final_norm_g"], tgt)

    dh3, gf1 = ffn_bwd(df1, h3, ffn1, w, 1)
    tok = hooks.gradients("ffn1", gf1, dh3)
    dres, dsh2b, dsc2b, dn2b, dy1, dg1b = norm_gate_bwd(x3, dh3, n2[1:2], sc2b, dres, y1, g1b + tok, "norm_gate_bwd_3")
    dh2, gm1 = mixer1_bwd(dy1, h2, mix1, ropes, w)
    tok = hooks.gradients("mix1", gm1, dh2)
    dres, dsh1b, dsc1b, dn1b, df0, dg2a = norm_gate_bwd(x2, dh2, n1[1:2], sc1b, dres, f0, g2a + tok, "norm_gate_bwd_2")
    dh1, gf0 = ffn_bwd(df0, h1, ffn0, w, 0)
    tok = hooks.gradients("ffn0", gf0, dh1)
    dres, dsh2a, dsc2a, dn2a, dy0, dg1a = norm_gate_bwd(x1, dh1, n2[0:1], sc2a, dres, y0, g1a + tok, "norm_gate_bwd_1")
    dh0, gm0 = mixer0_bwd(dy0, h0, mix0, w)
    tok = hooks.gradients("mix0", gm0, dh0)
    grad_x, dsh1a, dsc1a, dn1a = norm_bwd(x, dh0, n1[0:1], sc1a + tok, dres, "norm_bwd_0")

    dmod = jnp.concatenate([jnp.concatenate([dsh1a, dsc1a, dg1a, dsh2a, dsc2a, dg2a], axis=1),
                            jnp.concatenate([dsh1b, dsc1b, dg1b, dsh2b, dsc2b, dg2b], axis=1)], axis=0)
    norms = dict(norm1_g=jnp.concatenate([dn1a, dn1b], axis=0), norm2_g=jnp.concatenate([dn2a, dn2b], axis=0),
                 final_norm_g=d_final)
    return loss, grad_x, dmod, dict(mix0=gm0, ffn0=gf0, mix1=gm1, ffn1=gf1, norms=norms)


def merge_grads(by_stage):
    grads = {**by_stage["mix0"], **by_stage["mix1"], **by_stage["norms"]}
    for k in ("ffn_w_down", "ffn_w_up"):
        grads[k] = [by_stage["ffn0"][k], by_stage["ffn1"][k]]
    grads["ffn_conv_w"] = jnp.stack([by_stage["ffn0"]["ffn_conv_w"], by_stage["ffn1"]["ffn_conv_w"]])
    return grads


def local_step(x, tgt, mod, ropes, w):
    loss, grad_x, dmod, by_stage = run_step(x, tgt, mod, ropes, w, StepHooks())
    return loss, grad_x, dmod, merge_grads(by_stage)


_WEIGHTS = ("ada_w", "ada_b", "norm1_g", "norm2_g", "ab_w_in", "a_conv_w", "b_mix_w", "b_scale", "ab_w_out", "cd_w_in",
            "c_q_norm_g", "c_w_uq", "c_kv_norm_g", "c_w_ukv", "d_ln_g", "d_ln_b", "d_w_s", "d_b_s", "cd_w_out",
            "ffn_w_up", "ffn_conv_w", "ffn_w_down", "final_norm_g")
_INPUTS = ("x", "c", "positions") + _WEIGHTS + ("loss_target",) + tuple("m_" + n for n in _WEIGHTS) + tuple(
    "v_" + n for n in _WEIGHTS)

def _pack_rows(parts, rows, dtype):
    flat = jnp.concatenate([p.reshape(-1).astype(dtype) for p in parts])
    return jnp.pad(flat, (0, rows * LANES - flat.shape[0])).reshape(rows, LANES)


def _rows_major(w):
    r, c = w.shape
    return w.reshape(N_CHIPS, r // N_CHIPS, c)


def start_gather(shards, tag, after=()):
    lands = [_sds((N_CHIPS,) + s.shape, s.dtype) for s in shards]
    return split_start("gather_start_" + tag, GATHER, shards, lands, after)


def finish_gather(handle, chip, tag, after):
    shards, lands = split_wait("gather_wait_" + tag, GATHER, handle, after)
    lands = forward_halves(lands, "gather_forward_" + tag)
    return [lax.dynamic_update_index_in_dim(o, s, chip, 0) for o, s in zip(lands, shards)]


def start_reduce(gs, core, tag):
    recv = swap_halves(gs, "swap_halves_" + tag)
    pairs = pair_sums(gs, recv, core, "pair_sums_" + tag)
    lands = [_sds((N_CHIPS - 1,) + p.shape[1:], p.dtype) for p in pairs]
    return split_start("exchange_start_" + tag, EXCHANGE, pairs, lands)


def finish_reduce(handle, chip, core, tag, after):
    pairs, others = split_wait("exchange_wait_" + tag, EXCHANGE, handle, after)
    halves = chip_sums(pairs, others, chip, core, "chip_sums_" + tag)
    full = join_halves(halves, "join_halves_" + tag)
    return [f.reshape(f.shape[1] * 2, f.shape[2]) for f in full]


_SMALL_SHARDED = (("a_conv_w", (3, 128), 1), ("c_q_norm_g", (1, 64), 1), ("d_ln_g", (1, 128), 1), ("d_ln_b", (1, 128), 1),
                  ("ffn_conv_w", (2, 3, 2 * D_FF // N_CHIPS), 2))
_SMALL_GRADS = (("norm1_g", (2, D_MODEL)), ("norm2_g", (2, D_MODEL)), ("b_mix_w", (4, 128, 128)), ("b_scale", (1, 512)),
                ("c_kv_norm_g", (1, 128)), ("d_w_s", (4, 128, 128)), ("d_b_s", (4, 128)), ("final_norm_g", (1, D_MODEL)),
                ("a_conv_w", (3, 512)), ("c_q_norm_g", (1, 256)), ("d_ln_g", (1, 512)), ("d_ln_b", (1, 512)),
                ("ffn_conv_w", (2, 3, 2 * D_FF)))


def _size(shape):
    n = 1
    for d in shape:
        n *= d
    return n


def kernel(x, c, positions, ada_w, ada_b, norm1_g, norm2_g, ab_w_in, a_conv_w, b_mix_w, b_scale, ab_w_out, cd_w_in, c_q_norm_g, c_w_uq, c_kv_norm_g, c_w_ukv, d_ln_g, d_ln_b, d_w_s, d_b_s, cd_w_out, ffn_w_up, ffn_conv_w, ffn_w_down, final_norm_g, loss_target, m_ada_w, m_ada_b, m_norm1_g, m_norm2_g, m_ab_w_in, m_a_conv_w, m_b_mix_w, m_b_scale, m_ab_w_out, m_cd_w_in, m_c_q_norm_g, m_c_w_uq, m_c_kv_norm_g, m_c_w_ukv, m_d_ln_g, m_d_ln_b, m_d_w_s, m_d_b_s, m_cd_w_out, m_ffn_w_up, m_ffn_conv_w, m_ffn_w_down, m_final_norm_g, v_ada_w, v_ada_b, v_norm1_g, v_norm2_g, v_ab_w_in, v_a_conv_w, v_b_mix_w, v_b_scale, v_ab_w_out, v_cd_w_in, v_c_q_norm_g, v_c_w_uq, v_c_kv_norm_g, v_c_w_ukv, v_d_ln_g, v_d_ln_b, v_d_w_s, v_d_b_s, v_cd_w_out, v_ffn_w_up, v_ffn_conv_w, v_ffn_w_down, v_final_norm_g):
    args = (x, c, positions, ada_w, ada_b, norm1_g, norm2_g, ab_w_in, a_conv_w, b_mix_w, b_scale, ab_w_out, cd_w_in, c_q_norm_g, c_w_uq, c_kv_norm_g, c_w_ukv, d_ln_g, d_ln_b, d_w_s, d_b_s, cd_w_out, ffn_w_up, ffn_conv_w, ffn_w_down, final_norm_g, loss_target, m_ada_w, m_ada_b, m_norm1_g, m_norm2_g, m_ab_w_in, m_a_conv_w, m_b_mix_w, m_b_scale, m_ab_w_out, m_cd_w_in, m_c_q_norm_g, m_c_w_uq, m_c_kv_norm_g, m_c_w_ukv, m_d_ln_g, m_d_ln_b, m_d_w_s, m_d_b_s, m_cd_w_out, m_ffn_w_up, m_ffn_conv_w, m_ffn_w_down, m_final_norm_g, v_ada_w, v_ada_b, v_norm1_g, v_norm2_g, v_ab_w_in, v_a_conv_w, v_b_mix_w, v_b_scale, v_ab_w_out, v_cd_w_in, v_c_q_norm_g, v_c_w_uq, v_c_kv_norm_g, v_c_w_ukv, v_d_ln_g, v_d_ln_b, v_d_w_s, v_d_b_s, v_cd_w_out, v_ffn_w_up, v_ffn_conv_w, v_ffn_w_down, v_final_norm_g)
    a = dict(zip(_INPUTS, args, strict=True))
    xi, yi, ci = _place()
    chip = 2 * xi + yi
    dev = 4 * xi + 2 * yi + ci
    x = a["x"][0]
    tgt = a["loss_target"][0]

    bf = lambda t: t.astype(BF16)
    mix0_handle, tok = start_gather([bf(a["ab_w_in"][0]), bf(a["ab_w_out"][0])], "mix0")

    small_parts = [a["c"] + tok] + [a[n] for n, _, _ in _SMALL_SHARDED]
    rows1 = -(-sum(p.size for p in small_parts) // LANES // 8) * 8
    g1 = all_gather8(_pack_rows(small_parts, rows1, F32), "gather_small").reshape(N_DEV, rows1 * LANES)
    c_all = g1[:, :D_MODEL]
    per_chip = g1[0::2]
    small_full = {}
    off = D_MODEL
    for n, shp, axis in _SMALL_SHARDED:
        piece = per_chip[:, off:off + _size(shp)].reshape((N_CHIPS,) + shp)
        small_full[n] = jnp.concatenate([piece[k] for k in range(N_CHIPS)], axis=axis)
        off += _size(shp)

    merge = lambda t: t.reshape(t.shape[0] * t.shape[1], t.shape[2])
    w = dict(norm1_g=a["norm1_g"], norm2_g=a["norm2_g"], b_mix_w=a["b_mix_w"][0], b_scale=a["b_scale"],
             c_kv_norm_g=a["c_kv_norm_g"], d_w_s=a["d_w_s"][0], d_b_s=a["d_b_s"][0],
             final_norm_g=a["final_norm_g"].reshape(1, D_MODEL), **small_full)

    ncol = N_MOD * D_MODEL // N_CHIPS
    ada_b_mine = lax.dynamic_slice_in_dim(a["ada_b"], chip * ncol, ncol, axis=1)
    mod_cols = ada_mod(c_all, a["ada_w"], ada_b_mine)
    g2_rows = all_gather8(mod_cols.reshape(-1, LANES), "gather_mod")
    g2 = g2_rows.reshape(N_DEV, 2, N_DEV, ncol)
    mod = lax.dynamic_index_in_dim(g2[0::2], dev, axis=2, keepdims=False)
    mod = mod.transpose(1, 0, 2).reshape(2, N_MOD * D_MODEL)

    late = [g2_rows]
    up0_handle, tok_a = start_gather([bf(a["ffn_w_up"][0])], "up0", late)
    down0_handle, tok_b = start_gather([bf(a["ffn_w_down"][0])], "down0", late)
    mix1_handle, tok_c = start_gather(
        [bf(a["cd_w_in"][0]), bf(a["c_w_uq"][0]), bf(a["c_w_ukv"][0]), bf(a["cd_w_out"][0])], "mix1", late)
    ffn1_handle, tok_d = start_gather([bf(a["ffn_w_up"][1]), bf(a["ffn_w_down"][1])], "ffn1", late)
    mod = mod + (tok_a + tok_b + tok_c + tok_d)

    ropes = rope_tables(a["positions"][0])
    cm16 = lambda t: chip_major(t).astype(BF16)
    w.update(ffn_w_up=[None, None], ffn_w_down=[None, None])
    handles = dict(mix0=mix0_handle, up0=up0_handle, down0=down0_handle, mix1=mix1_handle, ffn1=ffn1_handle)
    reducing, reduced = {}, {}

    class Hooks(StepHooks):
        def weights(self, stage, after):
            got = finish_gather(handles[stage], chip, stage, after)
            if stage == "mix0":
                w.update(ab_w_in=got[0], ab_w_out=merge(got[1]))
            elif stage == "up0":
                w["ffn_w_up"][0] = got[0]
            elif stage == "down0":
                w["ffn_w_down"][0] = merge(got[0])
            elif stage == "mix1":
                cd_in, uq, ukv, cd_out = got
                w.update(prepare_weights(dict(cd_w_in=from_chip_major(cd_in), c_w_uq=from_chip_major(uq),
                                              c_w_ukv=from_chip_major(ukv), cd_w_out=merge(cd_out))))
            else:
                w["ffn_w_up"][1], w["ffn_w_down"][1] = got[0], merge(got[1])

        def gradients(self, stage, grads, after):
            if stage in ("ffn0", "ffn1"):
                parts = [grads["ffn_w_up"], _rows_major(grads["ffn_w_down"])]
            elif stage == "mix1":
                grads.update(unprepare_grads(grads))
                parts = [cm16(grads["cd_w_in"]), cm16(grads["c_w_uq"]), cm16(grads["c_w_ukv"]),
                         _rows_major(grads["cd_w_out"]).astype(BF16)]
            else:
                parts = [grads["ab_w_in"], _rows_major(grads["ab_w_out"])]
            reducing[stage], tok = start_reduce(parts, ci, stage)
            before = {"mix1": "ffn1", "ffn0": "mix1", "mix0": "ffn0"}.get(stage)
            if before is not None:
                reduced[before] = finish_reduce(reducing[before], chip, ci, before, after)
            return tok

    loss, grad_x, dmod, by_stage = run_step(x, tgt, mod, ropes, w, Hooks())
    grads = merge_grads(by_stage)

    parts3 = [dmod] + [grads[n] for n, _ in _SMALL_GRADS] + [loss[0, 0]]
    rows3 = -(-sum(p.size for p in parts3) // LANES // 8) * 8
    small_handle, _ = split_start("small_grads_start", EVERYONE, [_pack_rows(parts3, rows3, F32)],
                                  [_sds((N_DEV, rows3, LANES))])
    red_up1, red_down1 = reduced["ffn1"]
    red_cd_in, red_uq, red_ukv, red_cd_out = reduced["mix1"]
    red_up0, red_down0 = reduced["ffn0"]
    out_grads = dict(cd_w_in=red_cd_in, c_w_uq=red_uq, c_w_ukv=red_ukv, cd_w_out=red_cd_out)
    per_layer = dict(ffn_w_up=(red_up0, red_up1), ffn_w_down=(red_down0, red_down1))
    updates = {}

    def update(n):
        if n in per_layer:
            updates[n] = adamw_layers(a[n], *per_layer[n], a["m_" + n], a["v_" + n], "adamw_" + n)
        else:
            updates[n] = adamw(a[n], out_grads[n].reshape(a[n].shape), a["m_" + n], a["v_" + n], "adamw_" + n,
                               copy_grad=n != "ada_w")

    early =("ffn_w_up", "ffn_w_down", "cd_w_in", "c_w_uq", "c_w_ukv", "cd_w_out")
    for n in early:
        update(n)
    (mine,), (landed,) = split_wait("small_grads_wait", EVERYONE, small_handle, [updates[n][1] for n in early])
    g3 = lax.dynamic_update_index_in_dim(landed, mine, dev, 0)
    summed = sum8(g3).reshape(-1)
    nmod = 2 * N_MOD * D_MODEL
    out_grads["ada_b"] = summed[:nmod].reshape(2, N_MOD * D_MODEL)
    off = nmod
    for n, shp in _SMALL_GRADS:
        out_grads[n] = summed[off:off + _size(shp)].reshape(shp)
        off += _size(shp)
    loss = summed[off]
    for n, shp, axis in _SMALL_SHARDED:
        width = out_grads[n].shape[-1] // N_CHIPS
        out_grads[n] = lax.dynamic_slice_in_dim(out_grads[n], chip * width, width, axis=out_grads[n].ndim - 1)
    dmod_all = g3.reshape(N_DEV, rows3 * LANES)[:, :nmod].reshape(N_DEV, 2, N_MOD * D_MODEL)
    dmod_mine = lax.dynamic_slice_in_dim(dmod_all, chip * ncol, ncol, axis=2).transpose(1, 0, 2)
    out_grads["ada_w"] = ada_grad(c_all.T, dmod_mine)

    red_in0, red_out0 = finish_reduce(reducing["mix0"], chip, ci, "mix0", out_grads["ada_w"])
    out_grads.update(ab_w_in=red_in0, ab_w_out=red_out0)

    for n in ("ada_w", "ab_w_in", "ab_w_out"):
        update(n)
    small = [n for n in _WEIGHTS if n not in updates]
    for n, res in zip(small, adamw_small([a[n] for n in small], [out_grads[n].reshape(a[n].shape) for n in small],
                                         [a["m_" + n] for n in small], [a["v_" + n] for n in small])):
        updates[n] = res
    return (loss, grad_x[None], *[updates[n][i] for i in range(4) for n in _WEIGHTS])
```

```python
import functools

import jax
import jax.numpy as jnp
from jax import lax
from jax.experimental import pallas as pl
from jax.experimental.pallas import tpu as pltpu

F32 = jnp.float32
BF16 = jnp.bfloat16
EPS = 1e-6
D_MODEL = 1024
N_MOD = 6
A_WIDTH = 512
B_GROUPS = 4
POOL_WINDOWS = (2, 4, 8, 16)
C_HEADS = 8
C_NOPE = 64
C_ROPE = 32
C_V = 64
C_Q_RANK = 256
C_KV_RANK = 128
HEAD_PAD = 128
ROPE_THETA = 10000.0
D_GROUPS = 4
D_CHUNK = 128
D_FF = 2816
FF_UNIT = 128
ADAM_LR = 0.001
ADAM_B1 = 0.9
ADAM_B2 = 0.999
ADAM_EPS = 1e-08
ADAM_WD = 0.01
ADAM_STEP = 10
N_CHIPS = 4
N_DEV = 8
LANES = 128
VMEM_BIG = 56 * 1024 * 1024
MESH = pl.DeviceIdType.MESH


def _sds(shape, dtype=F32):
    return jax.ShapeDtypeStruct(tuple(shape), dtype)


def _tile(n, cap, mult=128):
    if n <= cap:
        return n
    best = None
    for t in range(mult, cap + 1, mult):
        if n % t == 0:
            best = t
    assert best is not None, (n, cap, mult)
    return best


def _params(dims=None, vmem=None):
    return pltpu.CompilerParams(dimension_semantics=dims, vmem_limit_bytes=vmem)


def _shift_down(v, k):
    r = pltpu.roll(v, k, axis=0)
    t = lax.broadcasted_iota(jnp.int32, v.shape, 0)
    return jnp.where(t >= k, r, 0.0)


def _shift_up(v, k):
    n = v.shape[0]
    r = pltpu.roll(v, n - k, axis=0)
    t = lax.broadcasted_iota(jnp.int32, v.shape, 0)
    return jnp.where(t < n - k, r, 0.0)


def _sigmoid(v):
    return 1.0 / (1.0 + jnp.exp(-v))


_GELU_C = 0.7978845608028654
_GELU_A = 0.044715


def _gelu(v):
    return 0.5 * v * (1.0 + jnp.tanh(_GELU_C * (v + _GELU_A * v * v * v)))


def _gelu_grad(v):
    th = jnp.tanh(_GELU_C * (v + _GELU_A * v * v * v))
    return 0.5 * (1.0 + th) + 0.5 * v * (1.0 - th * th) * _GELU_C * (1.0 + 3.0 * _GELU_A * v * v)


_NN = (((1,), (0,)), ((), ()))
_NT = (((1,), (1,)), ((), ()))
_TN = (((0,), (0,)), ((), ()))


def _dot(a, b, dims=_NN):
    return lax.dot_general(a, b, dims, preferred_element_type=F32)


def _logical(t, groups):
    return (t.shape[-2], t.shape[-1] * groups)


def _block(tr, tc, groups, cols, where):
    if groups == 1:
        return pl.BlockSpec((tr, tc), where)
    per = cols // groups // tc

    def index(i, j, s):
        r, c = where(i, j, s)
        return (c // per, r, c % per)

    return pl.BlockSpec((None, tr, tc), index)


def matmul(a, b, mode, out_dtype, name, ga=1, gb=1, go=1, tm=None, tn=None, tk=None):
    (ar, ac), (br, bc) = _logical(a, ga), _logical(b, gb)
    if mode == "nn":
        m, k, n = ar, ac, bc
        a_col, b_col = "k", "n"
    elif mode == "nt":
        m, k, n = ar, ac, br
        a_col, b_col = "k", "k"
    else:
        k, m, n = ar, ac, bc
        a_col, b_col = "m", "n"
    limit = {"m": m, "n": n // go, "k": k}
    limit[a_col] = min(limit[a_col], ac // ga)
    limit[b_col] = min(limit[b_col], bc // gb)
    tm = tm or _tile(limit["m"], 1024, 128 if mode == "tn" else 16)
    tn = tn or _tile(limit["n"], 512)
    tk = tk or _tile(limit["k"], 2048, 16 if mode == "tn" else 128)
    nk = k // tk
    if mode == "nn":
        a_spec = _block(tm, tk, ga, ac, lambda i, j, s: (i, s))
        b_spec = _block(tk, tn, gb, bc, lambda i, j, s: (s, j))
        dims = _NN
    elif mode == "nt":
        a_spec = _block(tm, tk, ga, ac, lambda i, j, s: (i, s))
        b_spec = _block(tn, tk, gb, bc, lambda i, j, s: (j, s))
        dims = _NT
    else:
        a_spec = _block(tk, tm, ga, ac, lambda i, j, s: (s, i))
        b_spec = _block(tk, tn, gb, bc, lambda i, j, s: (s, j))
        dims = _TN
    o_spec = _block(tm, tn, go, n, lambda i, j, s: (i, j))
    out_shape = _sds((m, n), out_dtype) if go == 1 else _sds((go, m, n // go), out_dtype)

    def body(a_ref, b_ref, o_ref, acc_ref):
        s = pl.program_id(2)

        @pl.when(s == 0)
        def _():
            acc_ref[...] = jnp.zeros_like(acc_ref)

        acc_ref[...] += _dot(a_ref[...], b_ref[...], dims)

        @pl.when(s == nk - 1)
        def _():
            o_ref[...] = acc_ref[...].astype(o_ref.dtype)

    return pl.pallas_call(
        body, name=name, out_shape=out_shape, grid=(m // tm, n // tn, nk),
        in_specs=[a_spec, b_spec], out_specs=o_spec,
        scratch_shapes=[pltpu.VMEM((tm, tn), F32)],
        compiler_params=_params(("parallel", "parallel", "arbitrary"), VMEM_BIG),
    )(a, b)


def _rows(tm, n):
    return pl.BlockSpec((tm, n), lambda i: (i, 0))


def _vec(n):
    return pl.BlockSpec((1, n), lambda i: (0, 0))


def modnorm_fwd(x, g, sc, sh, name):
    s, d = x.shape
    tm = _tile(s, 256, 8)

    def body(x_ref, g_ref, sc_ref, sh_ref, o_ref):
        xv = x_ref[...]
        r = lax.rsqrt(jnp.mean(xv * xv, axis=-1, keepdims=True) + EPS)
        o_ref[...] = ((xv * r) * g_ref[...] * (1.0 + sc_ref[...]) + sh_ref[...]).astype(BF16)

    return pl.pallas_call(
        body, name=name, out_shape=_sds((s, d), BF16), grid=(s // tm,),
        in_specs=[_rows(tm, d), _vec(d), _vec(d), _vec(d)], out_specs=_rows(tm, d),
        compiler_params=_params(("parallel",)),
    )(x, g, sc, sh)


def norm_bwd(x, dh, g, sc, dres, name):
    s, d = x.shape
    tm = _tile(s, 256, 8)
    nsteps = s // tm

    def body(x_ref, dh_ref, g_ref, sc_ref, dr_ref, dx_ref, dsh_ref, dsc_ref, dg_ref, a2_ref):
        i = pl.program_id(0)

        @pl.when(i == 0)
        def _():
            dsh_ref[...] = jnp.zeros_like(dsh_ref)
            a2_ref[...] = jnp.zeros_like(a2_ref)

        xv = x_ref[...]
        dh = dh_ref[...]
        r = lax.rsqrt(jnp.mean(xv * xv, axis=-1, keepdims=True) + EPS)
        xh = xv * r
        dsh_ref[...] += jnp.sum(dh, axis=0, keepdims=True)
        a2_ref[...] += jnp.sum(dh * xh, axis=0, keepdims=True)
        dxh = dh * (g_ref[...] * (1.0 + sc_ref[...]))
        dx = r * (dxh - xh * jnp.mean(dxh * xh, axis=-1, keepdims=True))
        dx_ref[...] = dr_ref[...] + dx

        @pl.when(i == nsteps - 1)
        def _():
            dsc_ref[...] = a2_ref[...] * g_ref[...]
            dg_ref[...] = a2_ref[...] * (1.0 + sc_ref[...])

    return pl.pallas_call(
        body, name=name, out_shape=(_sds((s, d)), _sds((1, d)), _sds((1, d)), _sds((1, d))), grid=(nsteps,),
        in_specs=[_rows(tm, d), _rows(tm, d), _vec(d), _vec(d), _rows(tm, d)],
        out_specs=(_rows(tm, d), _vec(d), _vec(d), _vec(d)),
        scratch_shapes=[pltpu.VMEM((1, d), F32)],
        compiler_params=_params(("arbitrary",)),
    )(x, dh, g, sc, dres)


def resid_modnorm_fwd(x, y, gate, g, sc, sh, name):
    s, d = x.shape
    tm = _tile(s, 256, 8)

    def body(x_ref, y_ref, gate_ref, g_ref, sc_ref, sh_ref, xo_ref, h_ref):
        xv = x_ref[...] + gate_ref[...] * y_ref[...]
        xo_ref[...] = xv
        r = lax.rsqrt(jnp.mean(xv * xv, axis=-1, keepdims=True) + EPS)
        h_ref[...] = ((xv * r) * g_ref[...] * (1.0 + sc_ref[...]) + sh_ref[...]).astype(BF16)

    return pl.pallas_call(
        body, name=name, out_shape=(_sds((s, d)), _sds((s, d), BF16)), grid=(s // tm,),
        in_specs=[_rows(tm, d), _rows(tm, d), _vec(d), _vec(d), _vec(d), _vec(d)], out_specs=(_rows(tm, d), _rows(tm, d)),
        compiler_params=_params(("parallel",)),
    )(x, y, gate, g, sc, sh)


def norm_gate_bwd(x, dh, g, sc, dres, y, gate, name):
    s, d = x.shape
    tm = _tile(s, 256, 8)
    nsteps = s // tm

    def body(x_ref, dh_ref, g_ref, sc_ref, dr_ref, y_ref, gate_ref, dx_ref, dsh_ref, dsc_ref, dg_ref, dy_ref,
             dgate_ref, a2_ref):
        i = pl.program_id(0)

        @pl.when(i == 0)
        def _():
            dsh_ref[...] = jnp.zeros_like(dsh_ref)
            a2_ref[...] = jnp.zeros_like(a2_ref)
            dgate_ref[...] = jnp.zeros_like(dgate_ref)

        xv = x_ref[...]
        dh = dh_ref[...]
        r = lax.rsqrt(jnp.mean(xv * xv, axis=-1, keepdims=True) + EPS)
        xh = xv * r
        dsh_ref[...] += jnp.sum(dh, axis=0, keepdims=True)
        a2_ref[...] += jnp.sum(dh * xh, axis=0, keepdims=True)
        dxh = dh * (g_ref[...] * (1.0 + sc_ref[...]))
        dr = dr_ref[...] + r * (dxh - xh * jnp.mean(dxh * xh, axis=-1, keepdims=True))
        dx_ref[...] = dr
        dy_ref[...] = (dr * gate_ref[...]).astype(BF16)
        dgate_ref[...] += jnp.sum(dr * y_ref[...], axis=0, keepdims=True)

        @pl.when(i == nsteps - 1)
        def _():
            dsc_ref[...] = a2_ref[...] * g_ref[...]
            dg_ref[...] = a2_ref[...] * (1.0 + sc_ref[...])

    vec = _sds((1, d))
    return pl.pallas_call(
        body, name=name, out_shape=(_sds((s, d)), vec, vec, vec, _sds((s, d), BF16), vec), grid=(nsteps,),
        in_specs=[_rows(tm, d), _rows(tm, d), _vec(d), _vec(d), _rows(tm, d), _rows(tm, d), _vec(d)],
        out_specs=(_rows(tm, d), _vec(d), _vec(d), _vec(d), _rows(tm, d), _vec(d)),
        scratch_shapes=[pltpu.VMEM((1, d), F32)],
        compiler_params=_params(("arbitrary",)),
    )(x, dh, g, sc, dres, y, gate)


def final_fused(x, f, gate, g, tgt):
    s, d = x.shape
    tm = _tile(s, 256, 8)

    def body(x_ref, f_ref, gate_ref, g_ref, t_ref, dx_ref, dg_ref, loss_ref, df_ref, dgate_ref):
        @pl.when(pl.program_id(0) == 0)
        def _():
            dg_ref[...] = jnp.zeros_like(dg_ref)
            loss_ref[...] = jnp.zeros_like(loss_ref)
            dgate_ref[...] = jnp.zeros_like(dgate_ref)

        fv, gatev, gv = f_ref[...], gate_ref[...], g_ref[...]
        xv = x_ref[...] + gatev * fv
        r = lax.rsqrt(jnp.mean(xv * xv, axis=-1, keepdims=True) + EPS)
        xh = xv * r
        e = xh * gv - t_ref[...]
        row = jnp.sum(e * e, axis=-1, keepdims=True) * (0.5 / d)
        loss_ref[...] += jnp.sum(row, axis=0, keepdims=True)
        dy = e * (1.0 / d)
        dg_ref[...] += jnp.sum(dy * xh, axis=0, keepdims=True)
        dxh = dy * gv
        dx = r * (dxh - xh * jnp.mean(dxh * xh, axis=-1, keepdims=True))
        dx_ref[...] = dx
        df_ref[...] = (dx * gatev).astype(BF16)
        dgate_ref[...] += jnp.sum(dx * fv, axis=0, keepdims=True)

    vec = _sds((1, d))
    return pl.pallas_call(
        body, name="final_fused", out_shape=(_sds((s, d)), vec, _sds((1, LANES)), _sds((s, d), BF16), vec),
        grid=(s // tm,),
        in_specs=[_rows(tm, d), _rows(tm, d), _vec(d), _vec(d), _rows(tm, d)],
        out_specs=(_rows(tm, d), _vec(d), _vec(LANES), _rows(tm, d), _vec(d)),
        compiler_params=_params(("arbitrary",)),
    )(x, f, gate, g, tgt)


def _taps(v):
    return _shift_down(v, 2), _shift_down(v, 1), v


def _conv3_taps(taps, w):
    return w[0:1, :] * taps[0] + w[1:2, :] * taps[1] + w[2:3, :] * taps[2]


def _conv3(v, w):
    return _conv3_taps(_taps(v), w)


def _conv3_t(dv, w):
    return w[0:1, :] * _shift_up(dv, 2) + w[1:2, :] * _shift_up(dv, 1) + w[2:3, :] * dv


def _conv3_dw_taps(dv, taps):
    return jnp.concatenate([jnp.sum(dv * t, axis=0, keepdims=True) for t in taps], axis=0)


def _conv3_dw(dv, v):
    return _conv3_dw_taps(dv, _taps(v))


def gconv_fwd(z, conv_w):
    s = z.shape[0]
    nb = A_WIDTH // LANES

    def body(b_ref, c_ref, a_ref, w_ref, o_ref):
        b, c, a = b_ref[...].astype(F32), c_ref[...].astype(F32), a_ref[...].astype(F32)
        o_ref[...] = (b * _conv3(c * a, w_ref[...])).astype(BF16)

    col = lambda off: pl.BlockSpec((s, LANES), lambda j: (0, off + j))
    return pl.pallas_call(
        body, name="gconv_fwd", out_shape=_sds((s, A_WIDTH), BF16), grid=(nb,),
        in_specs=[col(0), col(nb), col(2 * nb), pl.BlockSpec((3, LANES), lambda j: (0, j))],
        out_specs=pl.BlockSpec((s, LANES), lambda j: (0, j)),
        compiler_params=_params(("parallel",), VMEM_BIG),
    )(z, z, z, conv_w)


def gconv_bwd(z, conv_w, dycat):
    s = z.shape[0]
    nb = A_WIDTH // LANES

    def body(b_ref, c_ref, a_ref, w_ref, dy_ref, db_ref, dc_ref, da_ref, dw_ref):
        c, a, w, dy = c_ref[...].astype(F32), a_ref[...].astype(F32), w_ref[...], dy_ref[...].astype(F32)
        ca = c * a
        db_ref[...] = (dy * _conv3(ca, w)).astype(BF16)
        dconv = dy * b_ref[...].astype(F32)
        dw_ref[...] = _conv3_dw(dconv, ca)
        dca = _conv3_t(dconv, w)
        dc_ref[...] = (dca * a).astype(BF16)
        da_ref[...] = (dca * c).astype(BF16)

    col = lambda off: pl.BlockSpec((s, LANES), lambda j: (0, off + j))
    wspec = pl.BlockSpec((3, LANES), lambda j: (0, j))
    part = _sds((s, A_WIDTH), BF16)
    return pl.pallas_call(
        body, name="gconv_bwd", out_shape=(part, part, part, _sds((3, A_WIDTH))), grid=(nb,),
        in_specs=[col(0), col(nb), col(2 * nb), wspec, col(0)],
        out_specs=(col(0), col(0), col(0), wspec),
        compiler_params=_params(("parallel",), VMEM_BIG),
    )(z, z, z, conv_w, dycat)


def _pool_counts(s, w):
    t = lax.broadcasted_iota(jnp.int32, (s, 1), 0)
    return jnp.minimum(t + 1, w).astype(F32)


def _pooled(p, levels):
    acc = p
    for lv in range(levels):
        acc = acc + _shift_down(acc, 2 ** lv)
    return acc / _pool_counts(p.shape[0], 2 ** levels) - p


def pool_fwd(z, mix_w, scale):
    s = z.shape[0]

    def make(g):
        def body_g(p_ref, m_ref, sc_ref, o_ref):
            pooled = _pooled(p_ref[...].astype(F32), g + 1)
            y = _dot(pooled.astype(BF16), m_ref[...].astype(BF16))
            o_ref[...] = (y * sc_ref[...]).astype(BF16)
        return body_g

    outs = []
    for g in range(B_GROUPS):
        outs.append(pl.pallas_call(
            make(g), name=f"pool_fwd{g}", out_shape=_sds((s, LANES), BF16), grid=(1,),
            in_specs=[pl.BlockSpec((s, LANES), lambda i, g=g: (0, 3 * (A_WIDTH // LANES) + g)),
                      pl.BlockSpec((None, LANES, LANES), lambda i, g=g: (g, 0, 0)),
                      pl.BlockSpec((1, LANES), lambda i, g=g: (0, g))],
            out_specs=pl.BlockSpec((s, LANES), lambda i: (0, 0)),
            compiler_params=_params(("arbitrary",), VMEM_BIG),
        )(z, mix_w, scale))
    return outs


def pool_bwd(z, mix_w, scale, dycat):
    s = z.shape[0]

    def make(g):
        w = 2 ** (g + 1)

        def body_g(p_ref, m_ref, sc_ref, dy_ref, dp_ref, dm_ref, dsc_ref):
            pooled = _pooled(p_ref[...].astype(F32), g + 1)
            mw = m_ref[...].astype(BF16)
            pb = pooled.astype(BF16)
            dy = dy_ref[...].astype(F32)
            dsc_ref[...] = jnp.sum(dy * _dot(pb, mw), axis=0, keepdims=True)
            dmix = (dy * sc_ref[...]).astype(BF16)
            dm_ref[...] = _dot(pb, dmix, _TN)
            dpool = _dot(dmix, mw, _NT)
            acc = dpool / _pool_counts(s, w)
            for lv in range(g + 1):
                acc = acc + _shift_up(acc, 2 ** lv)
            dp_ref[...] = (acc - dpool).astype(BF16)
        return body_g

    outs = []
    for g in range(B_GROUPS):
        outs.append(pl.pallas_call(
            make(g), name=f"pool_bwd{g}",
            out_shape=(_sds((s, LANES), BF16), _sds((LANES, LANES)), _sds((1, LANES))), grid=(1,),
            in_specs=[pl.BlockSpec((s, LANES), lambda i, g=g: (0, 3 * (A_WIDTH // LANES) + g)),
                      pl.BlockSpec((None, LANES, LANES), lambda i, g=g: (g, 0, 0)),
                      pl.BlockSpec((1, LANES), lambda i, g=g: (0, g)),
                      pl.BlockSpec((s, LANES), lambda i, g=g: (0, A_WIDTH // LANES + g))],
            out_specs=(pl.BlockSpec((s, LANES), lambda i: (0, 0)), pl.BlockSpec((LANES, LANES), lambda i: (0, 0)),
                       pl.BlockSpec((1, LANES), lambda i: (0, 0))),
            compiler_params=_params(("arbitrary",), VMEM_BIG),
        )(z, mix_w, scale, dycat))
    return outs


_FF_BLOCKS = D_FF // FF_UNIT


def _ff_spec(s):
    return pl.BlockSpec((2, s, FF_UNIT), lambda j: (0, 0, j))


def _ff_wspecs():
    return [pl.BlockSpec((3, FF_UNIT), lambda j: (0, j)), pl.BlockSpec((3, FF_UNIT), lambda j: (0, _FF_BLOCKS + j))]


_FF_ROWS = 64
_FF_HALO = 16


def _chunk_taps(z_ref, half, c):
    start = pl.multiple_of(c * _FF_ROWS, _FF_ROWS)
    before = pl.multiple_of(jnp.maximum(c * _FF_ROWS - _FF_HALO, 0), _FF_HALO)
    halo = z_ref[half, pl.ds(before, _FF_HALO), :].astype(F32)
    halo = jnp.where(c > 0, halo, 0.0)
    win = jnp.concatenate([halo, z_ref[half, pl.ds(start, _FF_ROWS), :].astype(F32)], axis=0)
    return tuple(pltpu.roll(win, k, axis=0)[_FF_HALO:] for k in (2, 1)) + (win[_FF_HALO:],)


def _fold8(v):
    acc = v[0:8]
    for r in range(8, v.shape[0], 8):
        acc = acc + v[r:r + 8]
    return acc


def ffn_act_fwd(zf, conv_w, name):
    s = zf.shape[1]
    assert s % _FF_ROWS == 0

    def body(z_ref, wg_ref, wu_ref, o_ref):
        g = _conv3(z_ref[0].astype(F32), wg_ref[...])
        u = _conv3(z_ref[1].astype(F32), wu_ref[...])
        o_ref[...] = (g * _sigmoid(g) * u).astype(BF16)

    return pl.pallas_call(
        body, name=name, out_shape=_sds((s, D_FF), BF16), grid=(_FF_BLOCKS,),
        in_specs=[_ff_spec(s)] + _ff_wspecs(), out_specs=pl.BlockSpec((s, FF_UNIT), lambda j: (0, j)),
        compiler_params=_params(("parallel",), VMEM_BIG),
    )(zf, conv_w, conv_w)


def ffn_act_bwd(zf, conv_w, da, name):
    s = zf.shape[1]
    assert s % _FF_ROWS == 0
    nchunks = s // _FF_ROWS

    def body(z_ref, wg_ref, wu_ref, da_ref, dz_ref, dw_ref, dg_ref, du_ref):
        wg, wu = wg_ref[...], wu_ref[...]

        def first(c, acc):
            rows = pl.ds(pl.multiple_of(c * _FF_ROWS, _FF_ROWS), _FF_ROWS)
            tg, tu = _chunk_taps(z_ref, 0, c), _chunk_taps(z_ref, 1, c)
            g = _conv3_taps(tg, wg)
            u = _conv3_taps(tu, wu)
            dav = da_ref[rows, :].astype(F32)
            sg = _sigmoid(g)
            dg = dav * u * (sg * (1.0 + g * (1.0 - sg)))
            du = dav * (g * sg)
            dg_ref[rows, :] = dg
            du_ref[rows, :] = du
            return tuple(a + _fold8(d * t) for a, (d, t) in zip(acc, [(dg, t) for t in tg] + [(du, t) for t in tu]))

        zero = jnp.zeros((8, FF_UNIT), F32)
        acc = lax.fori_loop(0, nchunks, first, (zero,) * 6)
        sums = [jnp.sum(a, axis=0, keepdims=True) for a in acc]
        dw_ref[0] = jnp.concatenate(sums[:3], axis=0)
        dw_ref[1] = jnp.concatenate(sums[3:], axis=0)

        tail = pl.ds(s, _FF_HALO)
        dg_ref[tail, :] = jnp.zeros((_FF_HALO, FF_UNIT), F32)
        du_ref[tail, :] = jnp.zeros((_FF_HALO, FF_UNIT), F32)
        span = _FF_ROWS + _FF_HALO

        def second(c, carry):
            start = pl.multiple_of(c * _FF_ROWS, _FF_ROWS)
            for half, (d_ref, w) in enumerate(((dg_ref, wg), (du_ref, wu))):
                win = d_ref[pl.ds(start, span), :]
                dz = (w[0:1, :] * pltpu.roll(win, span - 2, axis=0)[:_FF_ROWS]
                      + w[1:2, :] * pltpu.roll(win, span - 1, axis=0)[:_FF_ROWS] + w[2:3, :] * win[:_FF_ROWS])
                dz_ref[half, pl.ds(start, _FF_ROWS), :] = dz.astype(BF16)
            return carry

        lax.fori_loop(0, nchunks, second, 0)

    return pl.pallas_call(
        body, name=name, out_shape=(_sds((2, s, D_FF), BF16), _sds((2, 3, D_FF))), grid=(_FF_BLOCKS,),
        in_specs=[_ff_spec(s)] + _ff_wspecs() + [pl.BlockSpec((s, FF_UNIT), lambda j: (0, j))],
        out_specs=(_ff_spec(s), pl.BlockSpec((2, 3, FF_UNIT), lambda j: (0, 0, j))),
        scratch_shapes=[pltpu.VMEM((s + _FF_HALO, FF_UNIT), F32), pltpu.VMEM((s + _FF_HALO, FF_UNIT), F32)],
        compiler_params=_params(("parallel",), VMEM_BIG),
    )(zf, conv_w, conv_w, da)


def _rope(v, cs, s1, s2):
    return v * cs + pltpu.roll(v, LANES - C_ROPE // 2, axis=1) * s1 + pltpu.roll(v, C_ROPE // 2, axis=1) * s2


def _rope_t(dv, cs, s1, s2):
    return dv * cs + pltpu.roll(dv * s1, C_ROPE // 2, axis=1) + pltpu.roll(dv * s2, LANES - C_ROPE // 2, axis=1)


def _kpe_mask(shape):
    lane = lax.broadcasted_iota(jnp.int32, shape, 1)
    return (lane >= C_NOPE) & (lane < C_NOPE + C_ROPE)


def _rms(v, g):
    r = lax.rsqrt(jnp.mean(v * v, axis=-1, keepdims=True) + EPS)
    return v * r, r


def _rms_bwd(dn, xh, r, g):
    dxh = dn * g
    return r * (dxh - xh * jnp.mean(dxh * xh, axis=-1, keepdims=True)), jnp.sum(dn * xh, axis=0, keepdims=True)


_ZQ = C_Q_RANK + C_KV_RANK + HEAD_PAD
_HW = C_HEADS * HEAD_PAD


def mla_pre_fwd(z, gq, gkv, wq, wk, wv, cs, s1, s2):
    s = z.shape[0]
    tm = _tile(s, 256, 8)

    def body(z_ref, gq_ref, gkv_ref, wq_ref, wk_ref, wv_ref, cs_ref, s1_ref, s2_ref, q_ref, k_ref, v_ref):
        zv = z_ref[...].astype(F32)
        cst, s1t, s2t = cs_ref[...], s1_ref[...], s2_ref[...]
        qh, _ = _rms(zv[:, :C_Q_RANK], None)
        qn = (qh * gq_ref[...]).astype(BF16)
        q = _dot(qn, wq_ref[...])
        kh, _ = _rms(zv[:, C_Q_RANK:C_Q_RANK + C_KV_RANK], None)
        kvn = (kh * gkv_ref[...]).astype(BF16)
        k = _dot(kvn, wk_ref[...])
        v_ref[...] = _dot(kvn, wv_ref[...]).astype(BF16)
        kpe = _rope(zv[:, C_Q_RANK + C_KV_RANK:], cst, s1t, s2t)
        for h in range(C_HEADS):
            sl = slice(h * HEAD_PAD, (h + 1) * HEAD_PAD)
            q_ref[:, sl] = _rope(q[:, sl], cst, s1t, s2t).astype(BF16)
            k_ref[:, sl] = (k[:, sl] + kpe).astype(BF16)

    full = lambda r, c: pl.BlockSpec((r, c), lambda i: (0, 0))
    hw = _sds((s, _HW), BF16)
    return pl.pallas_call(
        body, name="mla_pre_fwd", out_shape=(hw, hw, hw), grid=(s // tm,),
        in_specs=[_rows(tm, _ZQ), _vec(C_Q_RANK), _vec(C_KV_RANK), full(C_Q_RANK, _HW), full(C_KV_RANK, _HW),
                  full(C_KV_RANK, _HW), _rows(tm, LANES), _rows(tm, LANES), _rows(tm, LANES)],
        out_specs=(_rows(tm, _HW), _rows(tm, _HW), _rows(tm, _HW)),
        compiler_params=_params(("parallel",), VMEM_BIG),
    )(z, gq, gkv, wq, wk, wv, cs, s1, s2)


def mla_pre_bwd(z, gq, gkv, wq, wk, wv, cs, s1, s2, dq, dk, dv):
    s = z.shape[0]
    tm = _tile(s, 256, 8)

    def body(z_ref, gq_ref, gkv_ref, wq_ref, wk_ref, wv_ref, cs_ref, s1_ref, s2_ref, dq_ref, dk_ref, dv_ref,
             dz_ref, dwq_ref, dwk_ref, dwv_ref, dgq_ref, dgkv_ref):
        @pl.when(pl.program_id(0) == 0)
        def _():
            dwq_ref[...] = jnp.zeros_like(dwq_ref)
            dwk_ref[...] = jnp.zeros_like(dwk_ref)
            dwv_ref[...] = jnp.zeros_like(dwv_ref)
            dgq_ref[...] = jnp.zeros_like(dgq_ref)
            dgkv_ref[...] = jnp.zeros_like(dgkv_ref)

        zv = z_ref[...].astype(F32)
        cst, s1t, s2t = cs_ref[...], s1_ref[...], s2_ref[...]
        gqv, gkvv = gq_ref[...], gkv_ref[...]
        qh, rq = _rms(zv[:, :C_Q_RANK], None)
        qn = (qh * gqv).astype(BF16)
        kh, rk = _rms(zv[:, C_Q_RANK:C_Q_RANK + C_KV_RANK], None)
        kvn = (kh * gkvv).astype(BF16)

        dqv = dq_ref[...].astype(F32)
        dqp = jnp.concatenate(
            [_rope_t(dqv[:, h * HEAD_PAD:(h + 1) * HEAD_PAD], cst, s1t, s2t) for h in range(C_HEADS)], axis=1
        ).astype(BF16)
        dwq_ref[...] += _dot(qn, dqp, _TN)
        dqn = _dot(dqp, wq_ref[...], _NT)
        dql, dgq = _rms_bwd(dqn, qh, rq, gqv)
        dgq_ref[...] += dgq

        dkv = dk_ref[...]
        dkb = dkv.astype(BF16)
        dvb = dv_ref[...].astype(BF16)
        dwk_ref[...] += _dot(kvn, dkb, _TN)
        dwv_ref[...] += _dot(kvn, dvb, _TN)
        dkvn = _dot(dkb, wk_ref[...], _NT) + _dot(dvb, wv_ref[...], _NT)
        dkl, dgkv = _rms_bwd(dkvn, kh, rk, gkvv)
        dgkv_ref[...] += dgkv

        dkpe = dkv[:, :HEAD_PAD]
        for h in range(1, C_HEADS):
            dkpe = dkpe + dkv[:, h * HEAD_PAD:(h + 1) * HEAD_PAD]
        dkpe = _rope_t(jnp.where(_kpe_mask(dkpe.shape), dkpe, 0.0), cst, s1t, s2t)
        dz_ref[...] = jnp.concatenate([dql, dkl, dkpe], axis=1).astype(BF16)

    full = lambda r, c: pl.BlockSpec((r, c), lambda i: (0, 0))
    return pl.pallas_call(
        body, name="mla_pre_bwd",
        out_shape=(_sds((s, _ZQ), BF16), _sds((C_Q_RANK, _HW)), _sds((C_KV_RANK, _HW)), _sds((C_KV_RANK, _HW)),
                   _sds((1, C_Q_RANK)), _sds((1, C_KV_RANK))),
        grid=(s // tm,),
        in_specs=[_rows(tm, _ZQ), _vec(C_Q_RANK), _vec(C_KV_RANK), full(C_Q_RANK, _HW), full(C_KV_RANK, _HW),
                  full(C_KV_RANK, _HW), _rows(tm, LANES), _rows(tm, LANES), _rows(tm, LANES),
                  _rows(tm, _HW), _rows(tm, _HW), _rows(tm, _HW)],
        out_specs=(_rows(tm, _ZQ), full(C_Q_RANK, _HW), full(C_KV_RANK, _HW), full(C_KV_RANK, _HW),
                   _vec(C_Q_RANK), _vec(C_KV_RANK)),
        compiler_params=_params(("arbitrary",), VMEM_BIG),
    )(z, gq, gkv, wq, wk, wv, cs, s1, s2, dq, dk, dv)


_ATT_SCALE = (C_NOPE + C_ROPE) ** -0.5
_NEG = -1e30


def _att_exp(q, k, row0, ends_here):
    sc = _dot(q, k, _NT) * _ATT_SCALE
    tq, nk = sc.shape
    if ends_here:
        last = sc[:, nk - tq:]
        row = lax.broadcasted_iota(jnp.int32, last.shape, 0)
        col = lax.broadcasted_iota(jnp.int32, last.shape, 1)
        last = jnp.where(col <= row, last, _NEG)
        sc = last if nk == tq else jnp.concatenate([sc[:, :nk - tq], last], axis=1)
    else:
        qpos = row0 + lax.broadcasted_iota(jnp.int32, sc.shape, 0)
        kpos = lax.broadcasted_iota(jnp.int32, sc.shape, 1)
        sc = jnp.where(kpos <= qpos, sc, _NEG)
    e = jnp.exp(sc - jnp.max(sc, axis=-1, keepdims=True))
    return e, 1.0 / jnp.sum(e, axis=-1, keepdims=True)


def _causal_cases(i, nq, tq, fn):
    if nq > 8:
        fn(nq * tq, False)
        return
    for blk in range(nq):
        pl.when(i == blk)(functools.partial(fn, (blk + 1) * tq, True))


def attn_fwd(q, k, v):
    s = q.shape[0]
    tq = _tile(s, 256, 8)
    nq = s // tq

    def body(q_ref, k_ref, v_ref, o_ref):
        i = pl.program_id(1)

        def case(nk, ends_here):
            e, inv = _att_exp(q_ref[...], k_ref[:nk, :], i * tq, ends_here)
            o_ref[...] = (_dot(e.astype(BF16), v_ref[:nk, :]) * inv).astype(BF16)

        _causal_cases(i, nq, tq, case)

    qspec = pl.BlockSpec((tq, HEAD_PAD), lambda h, i: (i, h))
    kspec = pl.BlockSpec((s, HEAD_PAD), lambda h, i: (0, h))
    return pl.pallas_call(
        body, name="attn_fwd", out_shape=_sds((s, _HW), BF16), grid=(C_HEADS, s // tq),
        in_specs=[qspec, kspec, kspec], out_specs=qspec,
        compiler_params=_params(("parallel", "parallel"), VMEM_BIG),
    )(q, k, v)


def attn_bwd(q, k, v, o, do_all, do_col0):
    s = q.shape[0]
    tq = _tile(s, 256, 8)

    def body(q_ref, k_ref, v_ref, o_ref, do_ref, dq_ref, dk_ref, dv_ref):
        i = pl.program_id(1)

        @pl.when(i == 0)
        def _():
            dk_ref[...] = jnp.zeros_like(dk_ref)
            dv_ref[...] = jnp.zeros_like(dv_ref)

        def case(nk, ends_here):
            qv, kv, vv, dov = q_ref[...], k_ref[:nk, :], v_ref[:nk, :], do_ref[...]
            e, inv = _att_exp(qv, kv, i * tq, ends_here)
            p = e * inv
            dp = _dot(dov, vv, _NT)
            delta = jnp.sum(dov.astype(F32) * o_ref[...].astype(F32), axis=-1, keepdims=True)
            ds = (p * (dp - delta) * _ATT_SCALE).astype(BF16)
            dq_ref[...] = _dot(ds, kv).astype(BF16)
            dk_ref[:nk, :] += _dot(ds, qv, _TN)
            dv_ref[:nk, :] += _dot(p.astype(BF16), dov, _TN)

        _causal_cases(i, s // tq, tq, case)

    qspec = pl.BlockSpec((tq, HEAD_PAD), lambda h, i: (i, h))
    dospec = pl.BlockSpec((tq, HEAD_PAD), lambda h, i: (i, do_col0 + h))
    kspec = pl.BlockSpec((s, HEAD_PAD), lambda h, i: (0, h))
    return pl.pallas_call(
        body, name="attn_bwd", out_shape=(_sds((s, _HW), BF16), _sds((s, _HW)), _sds((s, _HW))),
        grid=(C_HEADS, s // tq),
        in_specs=[qspec, kspec, kspec, qspec, dospec], out_specs=(qspec, kspec, kspec),
        compiler_params=_params(("parallel", "arbitrary"), VMEM_BIG),
    )(q, k, v, o, do_all)


_DW = D_GROUPS * LANES


def _tril_bf16(w):
    r = lax.broadcasted_iota(jnp.int32, w.shape, 0)
    c = lax.broadcasted_iota(jnp.int32, w.shape, 1)
    return jnp.where(c <= r, w, 0.0).astype(BF16)


def _sgu_forward(zu, zv, lg, lb, ws_ref, bs):
    u = _gelu(zu)
    v = _gelu(zv)
    mu = jnp.mean(v, axis=-1, keepdims=True)
    vc = v - mu
    rstd = lax.rsqrt(jnp.mean(vc * vc, axis=-1, keepdims=True) + EPS)
    xh = vc * rstd
    vln = (xh * lg + lb).astype(BF16)
    mixed = []
    for g in range(D_GROUPS):
        wg = _tril_bf16(ws_ref[g])
        mixed.append(_dot(wg, vln[:, g * LANES:(g + 1) * LANES]) + bs[:, g:g + 1])
    return u, xh, rstd, vln, jnp.concatenate(mixed, axis=1)


def sgu_fwd(z, lg, lb, ws, bs_t):
    s = z.shape[0]
    nchunk = s // D_CHUNK

    def body(zu_ref, zv_ref, lg_ref, lb_ref, ws_ref, bs_ref, o_ref):
        u, _, _, _, mixed = _sgu_forward(zu_ref[...].astype(F32), zv_ref[...].astype(F32), lg_ref[...], lb_ref[...],
                                         ws_ref, bs_ref[...])
        o_ref[...] = (u * mixed).astype(BF16)

    return pl.pallas_call(
        body, name="sgu_fwd", out_shape=_sds((s, _DW), BF16), grid=(nchunk,),
        in_specs=[pl.BlockSpec((D_CHUNK, _DW), lambda n: (n, 1)), pl.BlockSpec((D_CHUNK, _DW), lambda n: (n, 2)),
                  _vec(_DW), _vec(_DW), pl.BlockSpec((D_GROUPS, D_CHUNK, D_CHUNK), lambda n: (0, 0, 0)),
                  pl.BlockSpec((D_CHUNK, LANES), lambda n: (0, 0))],
        out_specs=pl.BlockSpec((D_CHUNK, _DW), lambda n: (n, 0)),
        compiler_params=_params(("parallel",)),
    )(z, z, lg, lb, ws, bs_t)


def sgu_bwd(z, lg, lb, ws, bs_t, dycat, dy_col):
    s = z.shape[0]
    nchunk = s // D_CHUNK

    def body(zu_ref, zv_ref, lg_ref, lb_ref, ws_ref, bs_ref, dy_ref, dzu_ref, dzv_ref, dws_ref, dbs_ref, dlg_ref,
             dlb_ref):
        @pl.when(pl.program_id(0) == 0)
        def _():
            dws_ref[...] = jnp.zeros_like(dws_ref)
            dbs_ref[...] = jnp.zeros_like(dbs_ref)
            dlg_ref[...] = jnp.zeros_like(dlg_ref)
            dlb_ref[...] = jnp.zeros_like(dlb_ref)

        zu, zv, lg = zu_ref[...].astype(F32), zv_ref[...].astype(F32), lg_ref[...]
        u, xh, rstd, vln, mixed = _sgu_forward(zu, zv, lg, lb_ref[...], ws_ref, bs_ref[...])
        dy = dy_ref[...].astype(F32)
        dzu_ref[...] = (dy * mixed * _gelu_grad(zu)).astype(BF16)
        dmix = dy * u
        lane = lax.broadcasted_iota(jnp.int32, (D_CHUNK, LANES), 1)
        row = lax.broadcasted_iota(jnp.int32, (D_CHUNK, D_CHUNK), 0)
        colm = lax.broadcasted_iota(jnp.int32, (D_CHUNK, D_CHUNK), 1)
        dvln = []
        dbs = jnp.zeros((D_CHUNK, LANES), F32)
        for g in range(D_GROUPS):
            sl = slice(g * LANES, (g + 1) * LANES)
            dmg = dmix[:, sl]
            dbs = dbs + jnp.where(lane == g, jnp.sum(dmg, axis=-1, keepdims=True), 0.0)
            dmb = dmg.astype(BF16)
            dws_ref[g] += jnp.where(colm <= row, _dot(dmb, vln[:, sl], _NT), 0.0)
            dvln.append(_dot(_tril_bf16(ws_ref[g]), dmb, _TN))
        dbs_ref[...] += dbs
        dvln = jnp.concatenate(dvln, axis=1)
        dlg_ref[...] += jnp.sum(dvln * xh, axis=0, keepdims=True)
        dlb_ref[...] += jnp.sum(dvln, axis=0, keepdims=True)
        dxh = dvln * lg
        dvv = rstd * (dxh - jnp.mean(dxh, axis=-1, keepdims=True) - xh * jnp.mean(dxh * xh, axis=-1, keepdims=True))
        dzv_ref[...] = (dvv * _gelu_grad(zv)).astype(BF16)

    wsspec = pl.BlockSpec((D_GROUPS, D_CHUNK, D_CHUNK), lambda n: (0, 0, 0))
    chunk = lambda cidx: pl.BlockSpec((D_CHUNK, _DW), lambda n: (n, cidx))
    return pl.pallas_call(
        body, name="sgu_bwd",
        out_shape=(_sds((s, _DW), BF16), _sds((s, _DW), BF16), _sds((D_GROUPS, D_CHUNK, D_CHUNK)),
                   _sds((D_CHUNK, LANES)), _sds((1, _DW)), _sds((1, _DW))),
        grid=(nchunk,),
        in_specs=[chunk(1), chunk(2), _vec(_DW), _vec(_DW), wsspec, pl.BlockSpec((D_CHUNK, LANES), lambda n: (0, 0)),
                  chunk(dy_col)],
        out_specs=(chunk(0), chunk(0), wsspec, pl.BlockSpec((D_CHUNK, LANES), lambda n: (0, 0)), _vec(_DW), _vec(_DW)),
        compiler_params=_params(("arbitrary",)),
    )(z, z, lg, lb, ws, bs_t, dycat)


def ada_mod(c_all, ada_w, ada_b):
    nl, d, n = ada_w.shape
    nb = c_all.shape[0]
    tn = _tile(n, 512)

    def body(c_ref, w_ref, b_ref, o_ref):
        cv = c_ref[...]
        ca = (cv * _sigmoid(cv)).astype(BF16)
        o_ref[...] = _dot(ca, w_ref[...].astype(BF16)) + b_ref[...]

    return pl.pallas_call(
        body, name="ada_mod", out_shape=_sds((nl, nb, n)), grid=(nl, n // tn),
        in_specs=[pl.BlockSpec((nb, d), lambda l, j: (0, 0)), pl.BlockSpec((None, d, tn), lambda l, j: (l, 0, j)),
                  pl.BlockSpec((None, 1, tn), lambda l, j: (l, 0, j))],
        out_specs=pl.BlockSpec((None, nb, tn), lambda l, j: (l, 0, j)),
        compiler_params=_params(("parallel", "parallel")),
    )(c_all, ada_w, ada_b.reshape(nl, 1, n))


def ada_grad(c_all_t, dmod):
    d, nb = c_all_t.shape
    nl, _, n = dmod.shape
    tn = _tile(n, 512)
    tr = _tile(d, 256, 8)

    def body(c_ref, dm_ref, o_ref):
        cv = c_ref[...]
        ca = cv * _sigmoid(cv)
        dm = dm_ref[...]
        acc = ca[:, 0:1] * dm[0:1, :]
        for b in range(1, nb):
            acc = acc + ca[:, b:b + 1] * dm[b:b + 1, :]
        o_ref[...] = acc

    return pl.pallas_call(
        body, name="ada_grad", out_shape=_sds((nl, d, n)), grid=(nl, n // tn, d // tr),
        in_specs=[pl.BlockSpec((tr, nb), lambda l, j, r: (r, 0)), pl.BlockSpec((None, nb, tn), lambda l, j, r: (l, 0, j))],
        out_specs=pl.BlockSpec((None, tr, tn), lambda l, j, r: (l, r, j)),
        compiler_params=_params(("parallel", "parallel", "parallel")),
    )(c_all_t, dmod)


_ADAM_BLOCK = 256 * 1024


def _adam_rows(rows, cols):
    if rows * cols <= _ADAM_BLOCK or rows % 8:
        return rows
    return _tile(rows, max(8, _ADAM_BLOCK // cols), 8)


def _adam_update(w, gv, m, v):
    inv_bc1 = 1.0 / (1.0 - ADAM_B1 ** ADAM_STEP)
    inv_bc2 = 1.0 / (1.0 - ADAM_B2 ** ADAM_STEP)
    nm = ADAM_B1 * m + (1.0 - ADAM_B1) * gv
    nv = ADAM_B2 * v + (1.0 - ADAM_B2) * (gv * gv)
    return -ADAM_LR * ((nm * inv_bc1) / (jnp.sqrt(nv * inv_bc2) + ADAM_EPS) + ADAM_WD * w), nm, nv


def adamw(w, g, m, v, name, copy_grad=False):
    shape = w.shape
    cols = shape[-1]
    rows = w.size // cols
    tr = _adam_rows(rows, cols)

    def body(w_ref, g_ref, m_ref, v_ref, d_ref, nm_ref, nv_ref, *go_ref):
        gv = g_ref[...]
        d_ref[...], nm_ref[...], nv_ref[...] = _adam_update(w_ref[...], gv, m_ref[...], v_ref[...])
        if copy_grad:
            go_ref[0][...] = gv

    spec = pl.BlockSpec((tr, cols), lambda i: (i, 0))
    out = _sds((rows, cols))
    r2 = lambda t: t.reshape(rows, cols)
    nout = 4 if copy_grad else 3
    res = pl.pallas_call(
        body, name=name, out_shape=(out,) * nout, grid=(rows // tr,),
        in_specs=[spec] * 4, out_specs=(spec,) * nout, compiler_params=_params(("parallel",)),
    )(r2(w), r2(g), r2(m), r2(v))
    grad = res[3] if copy_grad else g
    return grad.reshape(shape), res[0].reshape(shape), res[1].reshape(shape), res[2].reshape(shape)


def adamw_small(ws, gs, ms, vs):
    n = len(ws)
    flat = lambda t: t.reshape(-1, t.shape[-1])

    def body(*refs):
        ins, outs = refs[:4 * n], refs[4 * n:]
        for i in range(n):
            w_ref, g_ref, m_ref, v_ref = ins[4 * i:4 * i + 4]
            outs[3 * i][...], outs[3 * i + 1][...], outs[3 * i + 2][...] = _adam_update(
                w_ref[...], g_ref[...], m_ref[...], v_ref[...])

    operands = [flat(t) for quad in zip(ws, gs, ms, vs) for t in quad]
    res = pl.pallas_call(
        body, name="adamw_small", out_shape=tuple(_sds(flat(w).shape) for w in ws for _ in range(3)),
    )(*operands)
    return [(g, res[3 * i].reshape(w.shape), res[3 * i + 1].reshape(w.shape), res[3 * i + 2].reshape(w.shape))
            for i, (w, g) in enumerate(zip(ws, gs))]


def adamw_layers(w, g0, g1, m, v, name):
    _, rows, cols = w.shape
    tr = _adam_rows(rows, cols)

    def body(w_ref, g0_ref, g1_ref, m_ref, v_ref, g_ref, d_ref, nm_ref, nv_ref):
        gv = jnp.where(pl.program_id(0) == 0, g0_ref[...], g1_ref[...])
        g_ref[...] = gv
        d_ref[...], nm_ref[...], nv_ref[...] = _adam_update(w_ref[...], gv, m_ref[...], v_ref[...])

    spec = pl.BlockSpec((None, tr, cols), lambda l, i: (l, i, 0))
    gspec = pl.BlockSpec((tr, cols), lambda l, i: (i, 0))
    out = _sds((2, rows, cols))
    return pl.pallas_call(
        body, name=name, out_shape=(out, out, out, out), grid=(2, rows // tr),
        in_specs=[spec, gspec, gspec, spec, spec], out_specs=(spec,) * 4, compiler_params=_params(("parallel", "parallel")),
    )(w, g0, g1, m, v)


def sum8(gathered):
    _, r, _ = gathered.shape
    tr = _tile(r, 512, 8)

    def body(g_ref, o_ref):
        acc = g_ref[0]
        for dev in range(1, N_DEV):
            acc = acc + g_ref[dev]
        o_ref[...] = acc

    return pl.pallas_call(
        body, name="sum8", out_shape=_sds((r, LANES)), grid=(r // tr,),
        in_specs=[pl.BlockSpec((N_DEV, tr, LANES), lambda i: (0, i, 0))], out_specs=pl.BlockSpec((tr, LANES), lambda i: (i, 0)),
        compiler_params=_params(("parallel",)),
    )(gathered)


_SUM_STEPS = 2


def pair_sums(gs, recvs, core, name):
    n = len(gs)
    trs = [g.shape[1] // 2 // _SUM_STEPS for g in gs]

    def body(c_ref, *refs):
        del c_ref
        for i in range(n):
            a_ref, b_ref, o_ref = refs[2 * i], refs[2 * i + 1], refs[2 * n + i]
            o_ref[...] = (a_ref[...].astype(F32) + b_ref[...].astype(F32)).astype(BF16)

    in_specs, out_specs = [], []
    for g, tr in zip(gs, trs):
        cols = g.shape[2]
        in_specs.append(pl.BlockSpec((None, tr, cols), lambda k, s, c: (k, c[0] * _SUM_STEPS + s, 0)))
        in_specs.append(pl.BlockSpec((None, tr, cols), lambda k, s, c: (k, s, 0)))
        out_specs.append(pl.BlockSpec((None, tr, cols), lambda k, s, c: (k, s, 0)))
    grid_spec = pltpu.PrefetchScalarGridSpec(num_scalar_prefetch=1, grid=(N_CHIPS, _SUM_STEPS), in_specs=in_specs,
                                             out_specs=tuple(out_specs))
    return list(pl.pallas_call(
        body, name=name, out_shape=tuple(_sds((N_CHIPS, g.shape[1] // 2, g.shape[2]), BF16) for g in gs),
        grid_spec=grid_spec, compiler_params=_params(("parallel", "parallel")),
    )(core.reshape(1).astype(jnp.int32), *[t for pair in zip(gs, recvs) for t in pair]))


def chip_sums(pairs, recvs, chip, core, name):
    n = len(pairs)
    trs = [p.shape[1] // _SUM_STEPS for p in pairs]

    def body(p_ref, *refs):
        del p_ref
        for i in range(n):
            own_ref, r_ref, o_ref = refs[2 * i], refs[2 * i + 1], refs[2 * n + i]
            acc = own_ref[...].astype(F32)
            for j in range(N_CHIPS - 1):
                acc = acc + r_ref[j].astype(F32)
            o_ref[...] = acc

    in_specs, out_specs = [], []
    for p, tr in zip(pairs, trs):
        cols = p.shape[2]
        in_specs.append(pl.BlockSpec((None, tr, cols), lambda s, q: (q[0], s, 0)))
        in_specs.append(pl.BlockSpec((N_CHIPS - 1, tr, cols), lambda s, q: (0, s, 0)))
        out_specs.append(pl.BlockSpec((None, tr, cols), lambda s, q: (q[1], s, 0)))
    grid_spec = pltpu.PrefetchScalarGridSpec(num_scalar_prefetch=1, grid=(_SUM_STEPS,), in_specs=in_specs,
                                             out_specs=tuple(out_specs))
    return list(pl.pallas_call(
        body, name=name, out_shape=tuple(_sds((2,) + p.shape[1:]) for p in pairs), grid_spec=grid_spec,
        compiler_params=_params(("parallel",)),
    )(jnp.stack([chip, core]).astype(jnp.int32), *[t for pair in zip(pairs, recvs) for t in pair]))


def _place():
    return lax.axis_index("x"), lax.axis_index("y"), lax.axis_index("c")


def _other_chips(x, y):
    return [(x, 1 - y), (1 - x, y), (1 - x, 1 - y)]


_HBM = pl.BlockSpec(memory_space=pltpu.HBM)


def all_gather8(v, name, after=()):
    m, n = v.shape

    def body(x_ref, *refs):
        out_ref, send_sems, recv_sems, local_sem = refs[len(after):]
        x, y, c = _place()
        me, sibling = (x, y, c), (x, y, 1 - c)
        chips = _other_chips(x, y)

        def rows(px, py, pc):
            return out_ref.at[pl.ds((4 * px + 2 * py + pc) * m, m), :]

        def copy(k, block, to, src=None):
            return pltpu.make_async_remote_copy(
                src_ref=rows(*block) if src is None else src, dst_ref=rows(*block),
                send_sem=send_sems.at[k], recv_sem=recv_sems.at[k], device_id=to, device_id_type=MESH)

        mine = pltpu.make_async_copy(x_ref, rows(*me), local_sem)
        mine.start()
        first = [copy(0, me, sibling, src=x_ref)]
        first += [copy(1 + j, me, (*chip, c), src=x_ref) for j, chip in enumerate(chips)]
        for cp in first:
            cp.start()
        passed = [copy(4 + j, (*chip, c), sibling) for j, chip in enumerate(chips)]
        for j, chip in enumerate(chips):
            copy(1 + j, (*chip, c), me).wait_recv()
            passed[j].start()
        copy(0, sibling, me).wait_recv()
        for j, chip in enumerate(chips):
            copy(4 + j, (*chip, 1 - c), me).wait_recv()
        for cp in first + passed:
            cp.wait_send()
        mine.wait()

    return pl.pallas_call(
        body, name=name, out_shape=_sds((N_DEV * m, n), v.dtype),
        in_specs=[pl.BlockSpec(memory_space=pltpu.VMEM)] + [pl.BlockSpec(memory_space=pl.ANY)] * len(after),
        out_specs=pl.BlockSpec(memory_space=pltpu.VMEM),
        scratch_shapes=[pltpu.SemaphoreType.DMA((7,)), pltpu.SemaphoreType.DMA((7,)), pltpu.SemaphoreType.DMA],
        compiler_params=_params(None, VMEM_BIG),
    )(v, *after)


def _comm_call(body, name, ins, out_shapes, nsem, aliases=None):
    return pl.pallas_call(
        body, name=name, out_shape=tuple(out_shapes), in_specs=[_HBM] * len(ins), out_specs=tuple([_HBM] * len(out_shapes)),
        scratch_shapes=[pltpu.SemaphoreType.DMA((nsem,)), pltpu.SemaphoreType.DMA((nsem,))],
        input_output_aliases=aliases or {},
    )(*ins)


def _remote(src, dst, send_sems, recv_sems, k, to):
    return pltpu.make_async_remote_copy(src_ref=src, dst_ref=dst, send_sem=send_sems.at[k], recv_sem=recv_sems.at[k],
                                        device_id=to, device_id_type=MESH)


def _half(core, rh):
    return pl.ds(pl.multiple_of(core * rh, 16), rh)


def swap_halves(gs, name):
    n = len(gs)

    def body(*refs):
        ins, outs, (send_sems, recv_sems) = refs[:n], refs[n:2 * n], refs[2 * n:]
        x, y, c = _place()
        copies = []
        for i in range(n):
            theirs = _half(1 - c, ins[i].shape[1] // 2)
            cp = _remote(ins[i].at[:, theirs], outs[i], send_sems, recv_sems, i, (x, y, 1 - c))
            cp.start()
            copies.append(cp)
        for cp in copies:
            cp.wait()

    return _comm_call(body, name, gs, [_sds((g.shape[0], g.shape[1] // 2, g.shape[2]), g.dtype) for g in gs], n)


def join_halves(bufs, name):
    n = len(bufs)

    def body(*refs):
        ins, outs, (send_sems, recv_sems) = refs[:n], refs[n:2 * n], refs[2 * n:]
        x, y, c = _place()
        copies = []
        for i in range(n):
            cp = _remote(ins[i].at[c], outs[i].at[c], send_sems, recv_sems, i, (x, y, 1 - c))
            cp.start()
            copies.append(cp)
        for i in range(n):
            theirs = outs[i].at[1 - c]
            _remote(theirs, theirs, send_sems, recv_sems, i, (x, y, 1 - c)).wait_recv()
        for cp in copies:
            cp.wait_send()

    return _comm_call(body, name, bufs, [_sds(b.shape, b.dtype) for b in bufs], n, {i: i for i in range(n)})


def forward_halves(lands, name):
    n = len(lands)

    def body(*refs):
        ins, outs, (send_sems, recv_sems) = refs[:n], refs[n:2 * n], refs[2 * n:]
        x, y, c = _place()
        sibling = (x, y, 1 - c)
        chips = _other_chips(x, y)
        copies = []
        for i in range(n):
            mine = _half(c, ins[i].shape[1] // 2)
            for j, (px, py) in enumerate(chips):
                cp = _remote(ins[i].at[2 * px + py, mine], outs[i].at[2 * px + py, mine], send_sems, recv_sems, 3 * i + j, sibling)
                cp.start()
                copies.append(cp)
        for i in range(n):
            theirs = _half(1 - c, ins[i].shape[1] // 2)
            for j, (px, py) in enumerate(chips):
                landed = outs[i].at[2 * px + py, theirs]
                _remote(landed, landed, send_sems, recv_sems, 3 * i + j, sibling).wait_recv()
        for cp in copies:
            cp.wait_send()

    return _comm_call(body, name, lands, [_sds(b.shape, b.dtype) for b in lands], 3 * n, {i: i for i in range(n)})


_SEM = pl.BlockSpec(memory_space=pltpu.SEMAPHORE)
_EFFECT = pltpu.SideEffectType.DATAFLOW_SIDE_EFFECTING


def _gather_copies(srcs, lands, send_sems, recv_sems):
    x, y, c = _place()
    copies = []
    for i in range(len(srcs)):
        mine = _half(c, srcs[i].shape[0] // 2)
        for j, chip in enumerate(_other_chips(x, y)):
            copies.append(_remote(srcs[i].at[mine], lands[i].at[2 * x + y, mine], send_sems, recv_sems, 3 * i + j, (*chip, c)))
    return copies


def _exchange_copies(srcs, lands, send_sems, recv_sems):
    x, y, c = _place()
    copies = []
    for i in range(len(srcs)):
        for j, (px, py) in enumerate(_other_chips(x, y)):
            copies.append(_remote(srcs[i].at[2 * px + py], lands[i].at[j], send_sems, recv_sems, 3 * i + j, (px, py, c)))
    return copies


def _everyone_copies(srcs, lands, send_sems, recv_sems):
    x, y, c = _place()
    flip = lambda v, b: 1 - v if b else v
    dst = lands[0].at[4 * x + 2 * y + c]
    return [_remote(srcs[0], dst, send_sems, recv_sems, j - 1, (flip(x, j & 4), flip(y, j & 2), flip(c, j & 1)))
            for j in range(1, N_DEV)]


GATHER = (_gather_copies, 3)
EXCHANGE = (_exchange_copies, 3)
EVERYONE = (_everyone_copies, N_DEV - 1)


def split_start(name, plan, srcs, land_shapes, after=()):
    copies_fn, per_source = plan
    n, m, k = len(srcs), len(land_shapes), len(after)
    ncopies = per_source * n

    def body(*refs):
        src_refs, land_refs = refs[:n], refs[n:n + m]
        send_sems, recv_sems = refs[n + m + k], refs[n + m + k + 1]
        token = refs[-1]
        for cp in copies_fn(src_refs, land_refs, send_sems, recv_sems):
            cp.start()
        token[...] = jnp.zeros_like(token)

    hbm = lambda s: pltpu.HBM(tuple(s.shape), s.dtype)
    outs = pl.pallas_call(
        body, name=name,
        out_shape=(pltpu.SemaphoreType.DMA((ncopies,)), pltpu.SemaphoreType.DMA((ncopies,)), *[hbm(s) for s in srcs],
                   *[hbm(s) for s in land_shapes], _sds((8, LANES))),
        in_specs=[_HBM] * (n + m) + [pl.BlockSpec(memory_space=pl.ANY)] * k,
        out_specs=(_SEM, _SEM, *([_HBM] * (n + m)), pl.BlockSpec(memory_space=pltpu.VMEM)),
        input_output_aliases={i: 2 + i for i in range(n + m)},
        compiler_params=pltpu.CompilerParams(has_side_effects=_EFFECT),
    )(*[pltpu.with_memory_space_constraint(s, pltpu.HBM) for s in srcs],
      *[pltpu.with_memory_space_constraint(lax.empty(tuple(s.shape), s.dtype), pltpu.HBM) for s in land_shapes], *after)
    handle = (outs[0], outs[1], list(outs[2:2 + n]), list(outs[2 + n:2 + n + m]))
    return handle, outs[-1][0, 0]


def split_wait(name, plan, handle, after):
    copies_fn, _ = plan
    send_sems, recv_sems, srcs, lands = handle
    n, m = len(srcs), len(lands)
    after = list(after) if isinstance(after, (list, tuple)) else [after]

    def body(*refs):
        src_refs, land_refs = refs[:n], refs[n:n + m]
        for cp in copies_fn(src_refs, land_refs, refs[n + m], refs[n + m + 1]):
            cp.wait_send()
            cp.wait_recv()

    hbm = lambda s: pltpu.HBM(tuple(s.shape), s.dtype)
    outs = pl.pallas_call(
        body, name=name, out_shape=tuple(hbm(s) for s in srcs + lands),
        in_specs=[_HBM] * (n + m) + [_SEM, _SEM] + [pl.BlockSpec(memory_space=pl.ANY)] * len(after),
        out_specs=tuple([_HBM] * (n + m)), input_output_aliases={i: i for i in range(n + m)},
        compiler_params=pltpu.CompilerParams(has_side_effects=_EFFECT),
    )(*srcs, *lands, send_sems, recv_sems, *after)
    return list(outs[:n]), list(outs[n:])


_CD_PAD = C_Q_RANK + C_KV_RANK + HEAD_PAD + 2 * _DW


def chip_major(w, groups=N_CHIPS):
    r, c = w.shape
    return w.reshape(r, groups, c // groups).transpose(1, 0, 2)


def from_chip_major(w):
    g, r, c = w.shape
    return w.transpose(1, 0, 2).reshape(r, g * c)


def _cd_in_pad(w):
    a = C_Q_RANK + C_KV_RANK
    z = lambda n: jnp.zeros((w.shape[0], n), w.dtype)
    return jnp.concatenate([w[:, :a], z(C_NOPE), w[:, a:a + C_ROPE], z(HEAD_PAD - C_NOPE - C_ROPE), w[:, a + C_ROPE:]], axis=1)


def _cd_in_unpad(w):
    a = C_Q_RANK + C_KV_RANK
    return jnp.concatenate([w[:, :a], w[:, a + C_NOPE:a + C_NOPE + C_ROPE], w[:, a + HEAD_PAD:]], axis=1)


def _pad_heads(w, width):
    r = w.shape[0]
    w = w.reshape(r, C_HEADS, width)
    return jnp.pad(w, ((0, 0), (0, 0), (0, HEAD_PAD - width))).reshape(r, _HW)


def _unpad_heads(w, width):
    r = w.shape[0]
    return w.reshape(r, C_HEADS, HEAD_PAD)[:, :, :width].reshape(r, C_HEADS * width)


_MATMUL_WEIGHTS = ("ab_w_in", "ab_w_out", "cd_w_in", "c_w_uq", "c_w_ukv", "cd_w_out", "ffn_w_up", "ffn_w_down")
_LAYER_STACKED = ("norm1_g", "norm2_g", "ffn_w_up", "ffn_conv_w", "ffn_w_down")
_ROW_VECTORS = ("b_scale", "c_q_norm_g", "c_kv_norm_g", "d_ln_g", "d_ln_b")


def full_to_local(p):
    q = {}
    for k, v in p.items():
        if k == "final_norm_g":
            v = v.reshape(1, -1)
        elif k not in _LAYER_STACKED and k not in _ROW_VECTORS:
            v = v[0]
        q[k] = v.astype(BF16) if k in _MATMUL_WEIGHTS else v
    return q


def local_to_full(g):
    q = {}
    for k, v in g.items():
        if k == "final_norm_g":
            q[k] = v.reshape(-1)
        elif k not in _LAYER_STACKED and k not in _ROW_VECTORS:
            q[k] = v[None]
        else:
            q[k] = v
    return q


def prepare_weights(p):
    q = dict(p)
    q["cd_w_in"] = _cd_in_pad(p["cd_w_in"])
    q["c_w_uq"] = _pad_heads(p["c_w_uq"], C_NOPE + C_ROPE)
    ukv = p["c_w_ukv"].reshape(C_KV_RANK, C_HEADS, C_NOPE + C_V)
    q["c_w_uk"] = _pad_heads(ukv[:, :, :C_NOPE].reshape(C_KV_RANK, -1), C_NOPE)
    q["c_w_uv"] = _pad_heads(ukv[:, :, C_NOPE:].reshape(C_KV_RANK, -1), C_V)
    wo = p["cd_w_out"]
    att_rows = jnp.pad(wo[:C_HEADS * C_V].reshape(C_HEADS, C_V, D_MODEL), ((0, 0), (0, HEAD_PAD - C_V), (0, 0)))
    q["cd_w_out"] = jnp.concatenate([att_rows.reshape(_HW, D_MODEL), wo[C_HEADS * C_V:]], axis=0)
    return q


def unprepare_grads(g):
    q = dict(g)
    q["cd_w_in"] = _cd_in_unpad(g["cd_w_in"])
    q["c_w_uq"] = _unpad_heads(g["c_w_uq"], C_NOPE + C_ROPE)
    uk = g.pop("c_w_uk").reshape(C_KV_RANK, C_HEADS, HEAD_PAD)[:, :, :C_NOPE]
    uv = g.pop("c_w_uv").reshape(C_KV_RANK, C_HEADS, HEAD_PAD)[:, :, :C_V]
    q.pop("c_w_uk", None)
    q.pop("c_w_uv", None)
    q["c_w_ukv"] = jnp.concatenate([uk, uv], axis=-1).reshape(C_KV_RANK, C_HEADS * (C_NOPE + C_V))
    wo = g["cd_w_out"]
    att = wo[:_HW].reshape(C_HEADS, HEAD_PAD, D_MODEL)[:, :C_V].reshape(C_HEADS * C_V, D_MODEL)
    q["cd_w_out"] = jnp.concatenate([att, wo[_HW:]], axis=0)
    return q


def rope_tables(positions):
    half = C_ROPE // 2
    inv_freq = ROPE_THETA ** (-jnp.arange(half, dtype=F32) / half)
    ang = positions.astype(F32)[:, None] * inv_freq
    cos, sin = jnp.cos(ang), jnp.sin(ang)
    s = positions.shape[0]
    z = lambda n: jnp.zeros((s, n), F32)
    cs = jnp.concatenate([jnp.ones((s, C_NOPE), F32), cos, cos, z(HEAD_PAD - C_NOPE - C_ROPE)], axis=1)
    s1 = jnp.concatenate([z(C_NOPE), -sin, z(HEAD_PAD - C_NOPE - half)], axis=1)
    s2 = jnp.concatenate([z(C_NOPE + half), sin, z(HEAD_PAD - C_NOPE - C_ROPE)], axis=1)
    return cs, s1, s2


def _mods(mod_l):
    return [mod_l[:, i * D_MODEL:(i + 1) * D_MODEL] for i in range(N_MOD)]


_UP_COLS = 2 * D_FF // N_CHIPS


def ffn_fwd(h2, w, l, late_down=None):
    zf = matmul(h2, w["ffn_w_up"][l], "nn", BF16, f"ffn_up{l}", gb=N_CHIPS, go=2, tn=_UP_COLS)
    a = ffn_act_fwd(zf, w["ffn_conv_w"][l], f"ffn_act_fwd{l}")
    if late_down is not None:
        late_down(a)
    f = matmul(a, w["ffn_w_down"][l], "nn", F32, f"ffn_down{l}", tk=D_FF)
    return f, (zf, a)


def ffn_bwd(df, h2, saved, w, l):
    zf, a = saved
    da = matmul(df, w["ffn_w_down"][l], "nt", BF16, f"ffn_down_dx{l}", tn=D_FF // 2)
    d_down = matmul(a, df, "tn", BF16, f"ffn_down_dw{l}", tm=D_FF // 2)
    dzf, d_conv = ffn_act_bwd(zf, w["ffn_conv_w"][l], da, f"ffn_act_bwd{l}")
    dh2 = matmul(dzf, w["ffn_w_up"][l], "nt", F32, f"ffn_up_dx{l}", ga=2, gb=N_CHIPS, tk=_UP_COLS, tn=D_MODEL)
    d_up = matmul(h2, dzf, "tn", BF16, f"ffn_up_dw{l}", gb=2, go=N_CHIPS, tn=_UP_COLS)
    d_conv = d_conv.transpose(1, 0, 2).reshape(3, 2 * D_FF)
    return dh2, dict(ffn_w_down=d_down, ffn_conv_w=d_conv, ffn_w_up=d_up)


def mixer0_fwd(h, w):
    z = matmul(h, w["ab_w_in"], "nn", BF16, "ab_in", gb=N_CHIPS)
    ya = gconv_fwd(z, w["a_conv_w"])
    yb = pool_fwd(z, w["b_mix_w"], w["b_scale"])
    ycat = jnp.concatenate([ya] + yb, axis=1)
    y = matmul(ycat, w["ab_w_out"], "nn", F32, "ab_out", tn=D_MODEL)
    return y, (z, ycat)


def mixer0_bwd(dy, h, saved, w):
    z, ycat = saved
    grads = {}
    dycat = matmul(dy, w["ab_w_out"], "nt", BF16, "ab_out_dx")
    grads["ab_w_out"] = matmul(ycat, dy, "tn", BF16, "ab_out_dw")
    db, dc, da, d_conv = gconv_bwd(z, w["a_conv_w"], dycat)
    pb = pool_bwd(z, w["b_mix_w"], w["b_scale"], dycat)
    dz = jnp.concatenate([db, dc, da] + [t[0] for t in pb], axis=1)
    dh = matmul(dz, w["ab_w_in"], "nt", F32, "ab_in_dx", gb=N_CHIPS, tn=D_MODEL)
    grads["ab_w_in"] = matmul(h, dz, "tn", BF16, "ab_in_dw", go=N_CHIPS)
    grads.update(a_conv_w=d_conv, b_mix_w=jnp.stack([t[1] for t in pb]),
                 b_scale=jnp.concatenate([t[2] for t in pb], axis=1))
    return dh, grads


def mixer1_fwd(h, ropes, w):
    cs, s1, s2 = ropes
    z = matmul(h, w["cd_w_in"], "nn", BF16, "cd_in")
    bs_t = jnp.pad(w["d_b_s"].T, ((0, 0), (0, LANES - D_GROUPS)))
    qh, kh, vh = mla_pre_fwd(z, w["c_q_norm_g"], w["c_kv_norm_g"], w["c_w_uq"], w["c_w_uk"], w["c_w_uv"], cs, s1, s2)
    oh = attn_fwd(qh, kh, vh)
    yd = sgu_fwd(z, w["d_ln_g"], w["d_ln_b"], w["d_w_s"], bs_t)
    ycat = jnp.concatenate([oh, yd], axis=1)
    y = matmul(ycat, w["cd_w_out"], "nn", F32, "cd_out", tn=D_MODEL)
    return y, (z, bs_t, qh, kh, vh, oh, ycat)


def mixer1_bwd(dy, h, saved, ropes, w):
    cs, s1, s2 = ropes
    z, bs_t, qh, kh, vh, oh, ycat = saved
    grads = {}
    dycat = matmul(dy, w["cd_w_out"], "nt", BF16, "cd_out_dx")
    grads["cd_w_out"] = matmul(ycat, dy, "tn", F32, "cd_out_dw")
    dqh, dkh, dvh = attn_bwd(qh, kh, vh, oh, dycat, 0)
    dzq, d_uq, d_uk, d_uv, d_gq, d_gkv = mla_pre_bwd(
        z, w["c_q_norm_g"], w["c_kv_norm_g"], w["c_w_uq"], w["c_w_uk"], w["c_w_uv"], cs, s1, s2, dqh, dkh, dvh)
    dzu, dzv, d_ws, d_bs, d_lg, d_lb = sgu_bwd(z, w["d_ln_g"], w["d_ln_b"], w["d_w_s"], bs_t, dycat, _HW // _DW)
    dz = jnp.concatenate([dzq, dzu, dzv], axis=1)
    dh = matmul(dz, w["cd_w_in"], "nt", F32, "cd_in_dx", tn=D_MODEL)
    grads["cd_w_in"] = matmul(h, dz, "tn", F32, "cd_in_dw")
    grads.update(c_w_uq=d_uq, c_w_uk=d_uk, c_w_uv=d_uv, c_q_norm_g=d_gq, c_kv_norm_g=d_gkv, d_w_s=d_ws,
                 d_b_s=d_bs[:, :D_GROUPS].T, d_ln_g=d_lg, d_ln_b=d_lb)
    return dh, grads


class StepHooks:
    def weights(self, stage, after):
        pass

    def gradients(self, stage, grads, after):
        return 0.0


def run_step(x, tgt, mod, ropes, w, hooks):
    sh1a, sc1a, g1a, sh2a, sc2a, g2a = _mods(mod[0:1])
    sh1b, sc1b, g1b, sh2b, sc2b, g2b = _mods(mod[1:2])
    n1, n2 = w["norm1_g"], w["norm2_g"]

    hooks.weights("mix0", mod)
    h0 = modnorm_fwd(x, n1[0:1], sc1a, sh1a, "modnorm_0")
    y0, mix0 = mixer0_fwd(h0, w)
    x1, h1 = resid_modnorm_fwd(x, y0, g1a, n2[0:1], sc2a, sh2a, "resid_modnorm_1")
    hooks.weights("up0", x1)
    f0, ffn0 = ffn_fwd(h1, w, 0, lambda act: hooks.weights("down0", act))
    x2, h2 = resid_modnorm_fwd(x1, f0, g2a, n1[1:2], sc1b, sh1b, "resid_modnorm_2")
    hooks.weights("mix1", x2)
    y1, mix1 = mixer1_fwd(h2, ropes, w)
    x3, h3 = resid_modnorm_fwd(x2, y1, g1b, n2[1:2], sc2b, sh2b, "resid_modnorm_3")
    hooks.weights("ffn1", x3)
    f1, ffn1 = ffn_fwd(h3, w, 1)
    dres, d_final, loss, df1, dg2b = final_fused(x3, f1, g2b, w["final_norm_g"], tgt)

    dh3, gf1 = ffn_bwd(df1, h3, ffn1, w, 1)
    tok = hooks.gradients("ffn1", gf1, dh3)
    dres, dsh2b, dsc2b, dn2b, dy1, dg1b = norm_gate_bwd(x3, dh3, n2[1:2], sc2b, dres, y1, g1b + tok, "norm_gate_bwd_3")
    dh2, gm1 = mixer1_bwd(dy1, h2, mix1, ropes, w)
    tok = hooks.gradients("mix1", gm1, dh2)
    dres, dsh1b, dsc1b, dn1b, df0, dg2a = norm_gate_bwd(x2, dh2, n1[1:2], sc1b, dres, f0, g2a + tok, "norm_gate_bwd_2")
    dh1, gf0 = ffn_bwd(df0, h1, ffn0, w, 0)
    tok = hooks.gradients("ffn0", gf0, dh1)
    dres, dsh2a, dsc2a, dn2a, dy0, dg1a = norm_gate_bwd(x1, dh1, n2[0:1], sc2a, dres, y0, g1a + tok, "norm_gate_bwd_1")
    dh0, gm0 = mixer0_bwd(dy0, h0, mix0, w)
    tok = hooks.gradients("mix0", gm0, dh0)
    grad_x, dsh1a, dsc1a, dn1a = norm_bwd(x, dh0, n1[0:1], sc1a + tok, dres, "norm_bwd_0")

    dmod = jnp.concatenate([jnp.concatenate([dsh1a, dsc1a, dg1a, dsh2a, dsc2a, dg2a], axis=1),
                            jnp.concatenate([dsh1b, dsc1b, dg1b, dsh2b, dsc2b, dg2b], axis=1)], axis=0)
    norms = dict(norm1_g=jnp.concatenate([dn1a, dn1b], axis=0), norm2_g=jnp.concatenate([dn2a, dn2b], axis=0),
                 final_norm_g=d_final)
    return loss, grad_x, dmod, dict(mix0=gm0, ffn0=gf0, mix1=gm1, ffn1=gf1, norms=norms)


def merge_grads(by_stage):
    grads = {**by_stage["mix0"], **by_stage["mix1"], **by_stage["norms"]}
    for k in ("ffn_w_down", "ffn_w_up"):
        grads[k] = [by_stage["ffn0"][k], by_stage["ffn1"][k]]
    grads["ffn_conv_w"] = jnp.stack([by_stage["ffn0"]["ffn_conv_w"], by_stage["ffn1"]["ffn_conv_w"]])
    return grads


def local_step(x, tgt, mod, ropes, w):
    loss, grad_x, dmod, by_stage = run_step(x, tgt, mod, ropes, w, StepHooks())
    return loss, grad_x, dmod, merge_grads(by_stage)


_WEIGHTS = ("ada_w", "ada_b", "norm1_g", "norm2_g", "ab_w_in", "a_conv_w", "b_mix_w", "b_scale", "ab_w_out", "cd_w_in",
            "c_q_norm_g", "c_w_uq", "c_kv_norm_g", "c_w_ukv", "d_ln_g", "d_ln_b", "d_w_s", "d_b_s", "cd_w_out",
            "ffn_w_up", "ffn_conv_w", "ffn_w_down", "final_norm_g")
_INPUTS = ("x", "c", "positions") + _WEIGHTS + ("loss_target",) + tuple("m_" + n for n in _WEIGHTS) + tuple(
    "v_" + n for n in _WEIGHTS)

def _pack_rows(parts, rows, dtype):
    flat = jnp.concatenate([p.reshape(-1).astype(dtype) for p in parts])
    return jnp.pad(flat, (0, rows * LANES - flat.shape[0])).reshape(rows, LANES)


def _rows_major(w):
    r, c = w.shape
    return w.reshape(N_CHIPS, r // N_CHIPS, c)


def start_gather(shards, tag, after=()):
    lands = [_sds((N_CHIPS,) + s.shape, s.dtype) for s in shards]
    return split_start("gather_start_" + tag, GATHER, shards, lands, after)


def finish_gather(handle, chip, tag, after):
    shards, lands = split_wait("gather_wait_" + tag, GATHER, handle, after)
    lands = forward_halves(lands, "gather_forward_" + tag)
    return [lax.dynamic_update_index_in_dim(o, s, chip, 0) for o, s in zip(lands, shards)]


def start_reduce(gs, core, tag):
    recv = swap_halves(gs, "swap_halves_" + tag)
    pairs = pair_sums(gs, recv, core, "pair_sums_" + tag)
    lands = [_sds((N_CHIPS - 1,) + p.shape[1:], p.dtype) for p in pairs]
    return split_start("exchange_start_" + tag, EXCHANGE, pairs, lands)


def finish_reduce(handle, chip, core, tag, after):
    pairs, others = split_wait("exchange_wait_" + tag, EXCHANGE, handle, after)
    halves = chip_sums(pairs, others, chip, core, "chip_sums_" + tag)
    full = join_halves(halves, "join_halves_" + tag)
    return [f.reshape(f.shape[1] * 2, f.shape[2]) for f in full]


_SMALL_SHARDED = (("a_conv_w", (3, 128), 1), ("c_q_norm_g", (1, 64), 1), ("d_ln_g", (1, 128), 1), ("d_ln_b", (1, 128), 1),
                  ("ffn_conv_w", (2, 3, 2 * D_FF // N_CHIPS), 2))
_SMALL_GRADS = (("norm1_g", (2, D_MODEL)), ("norm2_g", (2, D_MODEL)), ("b_mix_w", (4, 128, 128)), ("b_scale", (1, 512)),
                ("c_kv_norm_g", (1, 128)), ("d_w_s", (4, 128, 128)), ("d_b_s", (4, 128)), ("final_norm_g", (1, D_MODEL)),
                ("a_conv_w", (3, 512)), ("c_q_norm_g", (1, 256)), ("d_ln_g", (1, 512)), ("d_ln_b", (1, 512)),
                ("ffn_conv_w", (2, 3, 2 * D_FF)))


def _size(shape):
    n = 1
    for d in shape:
        n *= d
    return n


def kernel(x, c, positions, ada_w, ada_b, norm1_g, norm2_g, ab_w_in, a_conv_w, b_mix_w, b_scale, ab_w_out, cd_w_in, c_q_norm_g, c_w_uq, c_kv_norm_g, c_w_ukv, d_ln_g, d_ln_b, d_w_s, d_b_s, cd_w_out, ffn_w_up, ffn_conv_w, ffn_w_down, final_norm_g, loss_target, m_ada_w, m_ada_b, m_norm1_g, m_norm2_g, m_ab_w_in, m_a_conv_w, m_b_mix_w, m_b_scale, m_ab_w_out, m_cd_w_in, m_c_q_norm_g, m_c_w_uq, m_c_kv_norm_g, m_c_w_ukv, m_d_ln_g, m_d_ln_b, m_d_w_s, m_d_b_s, m_cd_w_out, m_ffn_w_up, m_ffn_conv_w, m_ffn_w_down, m_final_norm_g, v_ada_w, v_ada_b, v_norm1_g, v_norm2_g, v_ab_w_in, v_a_conv_w, v_b_mix_w, v_b_scale, v_ab_w_out, v_cd_w_in, v_c_q_norm_g, v_c_w_uq, v_c_kv_norm_g, v_c_w_ukv, v_d_ln_g, v_d_ln_b, v_d_w_s, v_d_b_s, v_cd_w_out, v_ffn_w_up, v_ffn_conv_w, v_ffn_w_down, v_final_norm_g):
    args = (x, c, positions, ada_w, ada_b, norm1_g, norm2_g, ab_w_in, a_conv_w, b_mix_w, b_scale, ab_w_out, cd_w_in, c_q_norm_g, c_w_uq, c_kv_norm_g, c_w_ukv, d_ln_g, d_ln_b, d_w_s, d_b_s, cd_w_out, ffn_w_up, ffn_conv_w, ffn_w_down, final_norm_g, loss_target, m_ada_w, m_ada_b, m_norm1_g, m_norm2_g, m_ab_w_in, m_a_conv_w, m_b_mix_w, m_b_scale, m_ab_w_out, m_cd_w_in, m_c_q_norm_g, m_c_w_uq, m_c_kv_norm_g, m_c_w_ukv, m_d_ln_g, m_d_ln_b, m_d_w_s, m_d_b_s, m_cd_w_out, m_ffn_w_up, m_ffn_conv_w, m_ffn_w_down, m_final_norm_g, v_ada_w, v_ada_b, v_norm1_g, v_norm2_g, v_ab_w_in, v_a_conv_w, v_b_mix_w, v_b_scale, v_ab_w_out, v_cd_w_in, v_c_q_norm_g, v_c_w_uq, v_c_kv_norm_g, v_c_w_ukv, v_d_ln_g, v_d_ln_b, v_d_w_s, v_d_b_s, v_cd_w_out, v_ffn_w_up, v_ffn_conv_w, v_ffn_w_down, v_final_norm_g)
    a = dict(zip(_INPUTS, args, strict=True))
    xi, yi, ci = _place()
    chip = 2 * xi + yi
    dev = 4 * xi + 2 * yi + ci
    x = a["x"][0]
    tgt = a["loss_target"][0]

    bf = lambda t: t.astype(BF16)
    mix0_handle, tok = start_gather([bf(a["ab_w_in"][0]), bf(a["ab_w_out"][0])], "mix0")
    up0_16, down0_16, up1_16, down1_16 = [bf(a[n][l]) for l in (0, 1) for n in ("ffn_w_up", "ffn_w_down")]
    mix1_16 = [bf(a[n][0]) for n in ("cd_w_in", "c_w_uq", "c_w_ukv", "cd_w_out")]

    small_parts = [a["c"] + tok] + [a[n] for n, _, _ in _SMALL_SHARDED]
    rows1 = -(-sum(p.size for p in small_parts) // LANES // 8) * 8
    g1 = all_gather8(_pack_rows(small_parts, rows1, F32), "gather_small",
                     [up0_16, down0_16, up1_16, down1_16, mix1_16[0], mix1_16[3]]).reshape(N_DEV, rows1 * LANES)
    c_all = g1[:, :D_MODEL]
    per_chip = g1[0::2]
    small_full = {}
    off = D_MODEL
    for n, shp, axis in _SMALL_SHARDED:
        piece = per_chip[:, off:off + _size(shp)].reshape((N_CHIPS,) + shp)
        small_full[n] = jnp.concatenate([piece[k] for k in range(N_CHIPS)], axis=axis)
        off += _size(shp)

    merge = lambda t: t.reshape(t.shape[0] * t.shape[1], t.shape[2])
    w = dict(norm1_g=a["norm1_g"], norm2_g=a["norm2_g"], b_mix_w=a["b_mix_w"][0], b_scale=a["b_scale"],
             c_kv_norm_g=a["c_kv_norm_g"], d_w_s=a["d_w_s"][0], d_b_s=a["d_b_s"][0],
             final_norm_g=a["final_norm_g"].reshape(1, D_MODEL), **small_full)

    ncol = N_MOD * D_MODEL // N_CHIPS
    ada_b_mine = lax.dynamic_slice_in_dim(a["ada_b"], chip * ncol, ncol, axis=1)
    mod_cols = ada_mod(c_all, a["ada_w"], ada_b_mine)
    g2_rows = all_gather8(mod_cols.reshape(-1, LANES), "gather_mod")
    g2 = g2_rows.reshape(N_DEV, 2, N_DEV, ncol)
    mod = lax.dynamic_index_in_dim(g2[0::2], dev, axis=2, keepdims=False)
    mod = mod.transpose(1, 0, 2).reshape(2, N_MOD * D_MODEL)

    late = [g2_rows]
    up0_handle, tok_a = start_gather([up0_16], "up0", late)
    down0_handle, tok_b = start_gather([down0_16], "down0", late)
    mix1_handle, tok_c = start_gather(mix1_16, "mix1", late)
    ffn1_handle, tok_d = start_gather([up1_16, down1_16], "ffn1", late)
    mod = mod + (tok_a + tok_b + tok_c + tok_d)

    ropes = rope_tables(a["positions"][0])
    cm16 = lambda t: chip_major(t).astype(BF16)
    w.update(ffn_w_up=[None, None], ffn_w_down=[None, None])
    handles = dict(mix0=mix0_handle, up0=up0_handle, down0=down0_handle, mix1=mix1_handle, ffn1=ffn1_handle)
    reducing, reduced = {}, {}

    class Hooks(StepHooks):
        def weights(self, stage, after):
            got = finish_gather(handles[stage], chip, stage, after)
            if stage == "mix0":
                w.update(ab_w_in=got[0], ab_w_out=merge(got[1]))
            elif stage == "up0":
                w["ffn_w_up"][0] = got[0]
            elif stage == "down0":
                w["ffn_w_down"][0] = merge(got[0])
            elif stage == "mix1":
                cd_in, uq, ukv, cd_out = got
                w.update(prepare_weights(dict(cd_w_in=from_chip_major(cd_in), c_w_uq=from_chip_major(uq),
                                              c_w_ukv=from_chip_major(ukv), cd_w_out=merge(cd_out))))
            else:
                w["ffn_w_up"][1], w["ffn_w_down"][1] = got[0], merge(got[1])

        def gradients(self, stage, grads, after):
            if stage in ("ffn0", "ffn1"):
                parts = [grads["ffn_w_up"], _rows_major(grads["ffn_w_down"])]
            elif stage == "mix1":
                grads.update(unprepare_grads(grads))
                parts = [cm16(grads["cd_w_in"]), cm16(grads["c_w_uq"]), cm16(grads["c_w_ukv"]),
                         _rows_major(grads["cd_w_out"]).astype(BF16)]
            else:
                parts = [grads["ab_w_in"], _rows_major(grads["ab_w_out"])]
            reducing[stage], tok = start_reduce(parts, ci, stage)
            before = {"mix1": "ffn1", "ffn0": "mix1", "mix0": "ffn0"}.get(stage)
            if before is not None:
                reduced[before] = finish_reduce(reducing[before], chip, ci, before, after)
            return tok

    loss, grad_x, dmod, by_stage = run_step(x, tgt, mod, ropes, w, Hooks())
    grads = merge_grads(by_stage)

    parts3 = [dmod] + [grads[n] for n, _ in _SMALL_GRADS] + [loss[0, 0]]
    rows3 = -(-sum(p.size for p in parts3) // LANES // 8) * 8
    small_handle, _ = split_start("small_grads_start", EVERYONE, [_pack_rows(parts3, rows3, F32)],
                                  [_sds((N_DEV, rows3, LANES))])
    red_up1, red_down1 = reduced["ffn1"]
    red_cd_in, red_uq, red_ukv, red_cd_out = reduced["mix1"]
    red_up0, red_down0 = reduced["ffn0"]
    out_grads = dict(cd_w_in=red_cd_in, c_w_uq=red_uq, c_w_ukv=red_ukv, cd_w_out=red_cd_out)
    per_layer = dict(ffn_w_up=(red_up0, red_up1), ffn_w_down=(red_down0, red_down1))
    updates = {}

    def update(n):
        if n in per_layer:
            updates[n] = adamw_layers(a[n], *per_layer[n], a["m_" + n], a["v_" + n], "adamw_" + n)
        else:
            updates[n] = adamw(a[n], out_grads[n].reshape(a[n].shape), a["m_" + n], a["v_" + n], "adamw_" + n,
                               copy_grad=n != "ada_w")

    early =("ffn_w_up", "ffn_w_down", "cd_w_in", "c_w_uq", "c_w_ukv", "cd_w_out")
    for n in early:
        update(n)
    (mine,), (landed,) = split_wait("small_grads_wait", EVERYONE, small_handle, [updates[n][1] for n in early])
    g3 = lax.dynamic_update_index_in_dim(landed, mine, dev, 0)
    summed = sum8(g3).reshape(-1)
    nmod = 2 * N_MOD * D_MODEL
    out_grads["ada_b"] = summed[:nmod].reshape(2, N_MOD * D_MODEL)
    off = nmod
    for n, shp in _SMALL_GRADS:
        out_grads[n] = summed[off:off + _size(shp)].reshape(shp)
        off += _size(shp)
    loss = summed[off]
    for n, shp, axis in _SMALL_SHARDED:
        width = out_grads[n].shape[-1] // N_CHIPS
        out_grads[n] = lax.dynamic_slice_in_dim(out_grads[n], chip * width, width, axis=out_grads[n].ndim - 1)
    dmod_all = g3.reshape(N_DEV, rows3 * LANES)[:, :nmod].reshape(N_DEV, 2, N_MOD * D_MODEL)
    dmod_mine = lax.dynamic_slice_in_dim(dmod_all, chip * ncol, ncol, axis=2).transpose(1, 0, 2)
    out_grads["ada_w"] = ada_grad(c_all.T, dmod_mine)

    red_in0, red_out0 = finish_reduce(reducing["mix0"], chip, ci, "mix0", out_grads["ada_w"])
    out_grads.update(ab_w_in=red_in0, ab_w_out=red_out0)

    for n in ("ada_w", "ab_w_in", "ab_w_out"):
        update(n)
    small = [n for n in _WEIGHTS if n not in updates]
    for n, res in zip(small, adamw_small([a[n] for n in small], [out_grads[n].reshape(a[n].shape) for n in small],
                                         [a["m_" + n] for n in small], [a["v_" + n] for n in small])):
        updates[n] = res
    return (loss, grad_x[None], *[updates[n][i] for i in range(4) for n in _WEIGHTS])
```

```python
import functools

import jax
import jax.numpy as jnp
from jax import lax
from jax.experimental import pallas as pl
from jax.experimental.pallas import tpu as pltpu

F32 = jnp.float32
BF16 = jnp.bfloat16
EPS = 1e-6
D_MODEL = 1024
N_MOD = 6
A_WIDTH = 512
B_GROUPS = 4
POOL_WINDOWS = (2, 4, 8, 16)
C_HEADS = 8
C_NOPE = 64
C_ROPE = 32
C_V = 64
C_Q_RANK = 256
C_KV_RANK = 128
HEAD_PAD = 128
ROPE_THETA = 10000.0
D_GROUPS = 4
D_CHUNK = 128
D_FF = 2816
FF_UNIT = 128
ADAM_LR = 0.001
ADAM_B1 = 0.9
ADAM_B2 = 0.999
ADAM_EPS = 1e-08
ADAM_WD = 0.01
ADAM_STEP = 10
N_CHIPS = 4
N_DEV = 8
LANES = 128
VMEM_BIG = 56 * 1024 * 1024
MESH = pl.DeviceIdType.MESH


def _sds(shape, dtype=F32):
    return jax.ShapeDtypeStruct(tuple(shape), dtype)


def _tile(n, cap, mult=128):
    if n <= cap:
        return n
    best = None
    for t in range(mult, cap + 1, mult):
        if n % t == 0:
            best = t
    assert best is not None, (n, cap, mult)
    return best


def _params(dims=None, vmem=None):
    return pltpu.CompilerParams(dimension_semantics=dims, vmem_limit_bytes=vmem)


def _shift_down(v, k):
    r = pltpu.roll(v, k, axis=0)
    t = lax.broadcasted_iota(jnp.int32, v.shape, 0)
    return jnp.where(t >= k, r, 0.0)


def _shift_up(v, k):
    n = v.shape[0]
    r = pltpu.roll(v, n - k, axis=0)
    t = lax.broadcasted_iota(jnp.int32, v.shape, 0)
    return jnp.where(t < n - k, r, 0.0)


def _sigmoid(v):
    return 1.0 / (1.0 + jnp.exp(-v))


_GELU_C = 0.7978845608028654
_GELU_A = 0.044715


def _gelu(v):
    return 0.5 * v * (1.0 + jnp.tanh(_GELU_C * (v + _GELU_A * v * v * v)))


def _gelu_grad(v):
    th = jnp.tanh(_GELU_C * (v + _GELU_A * v * v * v))
    return 0.5 * (1.0 + th) + 0.5 * v * (1.0 - th * th) * _GELU_C * (1.0 + 3.0 * _GELU_A * v * v)


_NN = (((1,), (0,)), ((), ()))
_NT = (((1,), (1,)), ((), ()))
_TN = (((0,), (0,)), ((), ()))


def _dot(a, b, dims=_NN):
    return lax.dot_general(a, b, dims, preferred_element_type=F32)


def _logical(t, groups):
    return (t.shape[-2], t.shape[-1] * groups)


def _block(tr, tc, groups, cols, where):
    if groups == 1:
        return pl.BlockSpec((tr, tc), where)
    per = cols // groups // tc

    def index(i, j, s):
        r, c = where(i, j, s)
        return (c // per, r, c % per)

    return pl.BlockSpec((None, tr, tc), index)


def matmul(a, b, mode, out_dtype, name, ga=1, gb=1, go=1, tm=None, tn=None, tk=None):
    (ar, ac), (br, bc) = _logical(a, ga), _logical(b, gb)
    if mode == "nn":
        m, k, n = ar, ac, bc
        a_col, b_col = "k", "n"
    elif mode == "nt":
        m, k, n = ar, ac, br
        a_col, b_col = "k", "k"
    else:
        k, m, n = ar, ac, bc
        a_col, b_col = "m", "n"
    limit = {"m": m, "n": n // go, "k": k}
    limit[a_col] = min(limit[a_col], ac // ga)
    limit[b_col] = min(limit[b_col], bc // gb)
    tm = tm or _tile(limit["m"], 1024, 128 if mode == "tn" else 16)
    tn = tn or _tile(limit["n"], 512)
    tk = tk or _tile(limit["k"], 2048, 16 if mode == "tn" else 128)
    nk = k // tk
    if mode == "nn":
        a_spec = _block(tm, tk, ga, ac, lambda i, j, s: (i, s))
        b_spec = _block(tk, tn, gb, bc, lambda i, j, s: (s, j))
        dims = _NN
    elif mode == "nt":
        a_spec = _block(tm, tk, ga, ac, lambda i, j, s: (i, s))
        b_spec = _block(tn, tk, gb, bc, lambda i, j, s: (j, s))
        dims = _NT
    else:
        a_spec = _block(tk, tm, ga, ac, lambda i, j, s: (s, i))
        b_spec = _block(tk, tn, gb, bc, lambda i, j, s: (s, j))
        dims = _TN
    o_spec = _block(tm, tn, go, n, lambda i, j, s: (i, j))
    out_shape = _sds((m, n), out_dtype) if go == 1 else _sds((go, m, n // go), out_dtype)

    def body(a_ref, b_ref, o_ref, acc_ref):
        s = pl.program_id(2)

        @pl.when(s == 0)
        def _():
            acc_ref[...] = jnp.zeros_like(acc_ref)

        acc_ref[...] += _dot(a_ref[...], b_ref[...], dims)

        @pl.when(s == nk - 1)
        def _():
            o_ref[...] = acc_ref[...].astype(o_ref.dtype)

    return pl.pallas_call(
        body, name=name, out_shape=out_shape, grid=(m // tm, n // tn, nk),
        in_specs=[a_spec, b_spec], out_specs=o_spec,
        scratch_shapes=[pltpu.VMEM((tm, tn), F32)],
        compiler_params=_params(("parallel", "parallel", "arbitrary"), VMEM_BIG),
    )(a, b)


def _rows(tm, n):
    return pl.BlockSpec((tm, n), lambda i: (i, 0))


def _vec(n):
    return pl.BlockSpec((1, n), lambda i: (0, 0))


def modnorm_fwd(x, g, sc, sh, name):
    s, d = x.shape
    tm = _tile(s, 256, 8)

    def body(x_ref, g_ref, sc_ref, sh_ref, o_ref):
        xv = x_ref[...]
        r = lax.rsqrt(jnp.mean(xv * xv, axis=-1, keepdims=True) + EPS)
        o_ref[...] = ((xv * r) * g_ref[...] * (1.0 + sc_ref[...]) + sh_ref[...]).astype(BF16)

    return pl.pallas_call(
        body, name=name, out_shape=_sds((s, d), BF16), grid=(s // tm,),
        in_specs=[_rows(tm, d), _vec(d), _vec(d), _vec(d)], out_specs=_rows(tm, d),
        compiler_params=_params(("parallel",)),
    )(x, g, sc, sh)


def norm_bwd(x, dh, g, sc, dres, name):
    s, d = x.shape
    tm = _tile(s, 256, 8)
    nsteps = s // tm

    def body(x_ref, dh_ref, g_ref, sc_ref, dr_ref, dx_ref, dsh_ref, dsc_ref, dg_ref, a2_ref):
        i = pl.program_id(0)

        @pl.when(i == 0)
        def _():
            dsh_ref[...] = jnp.zeros_like(dsh_ref)
            a2_ref[...] = jnp.zeros_like(a2_ref)

        xv = x_ref[...]
        dh = dh_ref[...]
        r = lax.rsqrt(jnp.mean(xv * xv, axis=-1, keepdims=True) + EPS)
        xh = xv * r
        dsh_ref[...] += jnp.sum(dh, axis=0, keepdims=True)
        a2_ref[...] += jnp.sum(dh * xh, axis=0, keepdims=True)
        dxh = dh * (g_ref[...] * (1.0 + sc_ref[...]))
        dx = r * (dxh - xh * jnp.mean(dxh * xh, axis=-1, keepdims=True))
        dx_ref[...] = dr_ref[...] + dx

        @pl.when(i == nsteps - 1)
        def _():
            dsc_ref[...] = a2_ref[...] * g_ref[...]
            dg_ref[...] = a2_ref[...] * (1.0 + sc_ref[...])

    return pl.pallas_call(
        body, name=name, out_shape=(_sds((s, d)), _sds((1, d)), _sds((1, d)), _sds((1, d))), grid=(nsteps,),
        in_specs=[_rows(tm, d), _rows(tm, d), _vec(d), _vec(d), _rows(tm, d)],
        out_specs=(_rows(tm, d), _vec(d), _vec(d), _vec(d)),
        scratch_shapes=[pltpu.VMEM((1, d), F32)],
        compiler_params=_params(("arbitrary",)),
    )(x, dh, g, sc, dres)


def resid_modnorm_fwd(x, y, gate, g, sc, sh, name):
    s, d = x.shape
    tm = _tile(s, 256, 8)

    def body(x_ref, y_ref, gate_ref, g_ref, sc_ref, sh_ref, xo_ref, h_ref):
        xv = x_ref[...] + gate_ref[...] * y_ref[...].astype(F32)
        xo_ref[...] = xv
        r = lax.rsqrt(jnp.mean(xv * xv, axis=-1, keepdims=True) + EPS)
        h_ref[...] = ((xv * r) * g_ref[...] * (1.0 + sc_ref[...]) + sh_ref[...]).astype(BF16)

    return pl.pallas_call(
        body, name=name, out_shape=(_sds((s, d)), _sds((s, d), BF16)), grid=(s // tm,),
        in_specs=[_rows(tm, d), _rows(tm, d), _vec(d), _vec(d), _vec(d), _vec(d)], out_specs=(_rows(tm, d), _rows(tm, d)),
        compiler_params=_params(("parallel",)),
    )(x, y, gate, g, sc, sh)


def norm_gate_bwd(x, dh, g, sc, dres, y, gate, name):
    s, d = x.shape
    tm = _tile(s, 256, 8)
    nsteps = s // tm

    def body(x_ref, dh_ref, g_ref, sc_ref, dr_ref, y_ref, gate_ref, dx_ref, dsh_ref, dsc_ref, dg_ref, dy_ref,
             dgate_ref, a2_ref):
        i = pl.program_id(0)

        @pl.when(i == 0)
        def _():
            dsh_ref[...] = jnp.zeros_like(dsh_ref)
            a2_ref[...] = jnp.zeros_like(a2_ref)
            dgate_ref[...] = jnp.zeros_like(dgate_ref)

        xv = x_ref[...]
        dh = dh_ref[...]
        r = lax.rsqrt(jnp.mean(xv * xv, axis=-1, keepdims=True) + EPS)
        xh = xv * r
        dsh_ref[...] += jnp.sum(dh, axis=0, keepdims=True)
        a2_ref[...] += jnp.sum(dh * xh, axis=0, keepdims=True)
        dxh = dh * (g_ref[...] * (1.0 + sc_ref[...]))
        dr = dr_ref[...] + r * (dxh - xh * jnp.mean(dxh * xh, axis=-1, keepdims=True))
        dx_ref[...] = dr
        dy_ref[...] = (dr * gate_ref[...]).astype(BF16)
        dgate_ref[...] += jnp.sum(dr * y_ref[...].astype(F32), axis=0, keepdims=True)

        @pl.when(i == nsteps - 1)
        def _():
            dsc_ref[...] = a2_ref[...] * g_ref[...]
            dg_ref[...] = a2_ref[...] * (1.0 + sc_ref[...])

    vec = _sds((1, d))
    return pl.pallas_call(
        body, name=name, out_shape=(_sds((s, d)), vec, vec, vec, _sds((s, d), BF16), vec), grid=(nsteps,),
        in_specs=[_rows(tm, d), _rows(tm, d), _vec(d), _vec(d), _rows(tm, d), _rows(tm, d), _vec(d)],
        out_specs=(_rows(tm, d), _vec(d), _vec(d), _vec(d), _rows(tm, d), _vec(d)),
        scratch_shapes=[pltpu.VMEM((1, d), F32)],
        compiler_params=_params(("arbitrary",)),
    )(x, dh, g, sc, dres, y, gate)


def final_fused(x, f, gate, g, tgt):
    s, d = x.shape
    tm = _tile(s, 256, 8)

    def body(x_ref, f_ref, gate_ref, g_ref, t_ref, dx_ref, dg_ref, loss_ref, df_ref, dgate_ref):
        @pl.when(pl.program_id(0) == 0)
        def _():
            dg_ref[...] = jnp.zeros_like(dg_ref)
            loss_ref[...] = jnp.zeros_like(loss_ref)
            dgate_ref[...] = jnp.zeros_like(dgate_ref)

        fv, gatev, gv = f_ref[...].astype(F32), gate_ref[...], g_ref[...]
        xv = x_ref[...] + gatev * fv
        r = lax.rsqrt(jnp.mean(xv * xv, axis=-1, keepdims=True) + EPS)
        xh = xv * r
        e = xh * gv - t_ref[...]
        row = jnp.sum(e * e, axis=-1, keepdims=True) * (0.5 / d)
        loss_ref[...] += jnp.sum(row, axis=0, keepdims=True)
        dy = e * (1.0 / d)
        dg_ref[...] += jnp.sum(dy * xh, axis=0, keepdims=True)
        dxh = dy * gv
        dx = r * (dxh - xh * jnp.mean(dxh * xh, axis=-1, keepdims=True))
        dx_ref[...] = dx
        df_ref[...] = (dx * gatev).astype(BF16)
        dgate_ref[...] += jnp.sum(dx * fv, axis=0, keepdims=True)

    vec = _sds((1, d))
    return pl.pallas_call(
        body, name="final_fused", out_shape=(_sds((s, d)), vec, _sds((1, LANES)), _sds((s, d), BF16), vec),
        grid=(s // tm,),
        in_specs=[_rows(tm, d), _rows(tm, d), _vec(d), _vec(d), _rows(tm, d)],
        out_specs=(_rows(tm, d), _vec(d), _vec(LANES), _rows(tm, d), _vec(d)),
        compiler_params=_params(("arbitrary",)),
    )(x, f, gate, g, tgt)


def _taps(v):
    return _shift_down(v, 2), _shift_down(v, 1), v


def _conv3_taps(taps, w):
    return w[0:1, :] * taps[0] + w[1:2, :] * taps[1] + w[2:3, :] * taps[2]


def _conv3(v, w):
    return _conv3_taps(_taps(v), w)


def _conv3_t(dv, w):
    return w[0:1, :] * _shift_up(dv, 2) + w[1:2, :] * _shift_up(dv, 1) + w[2:3, :] * dv


def _conv3_dw_taps(dv, taps):
    return jnp.concatenate([jnp.sum(dv * t, axis=0, keepdims=True) for t in taps], axis=0)


def _conv3_dw(dv, v):
    return _conv3_dw_taps(dv, _taps(v))


def gconv_fwd(z, conv_w):
    s = z.shape[0]
    nb = A_WIDTH // LANES

    def body(b_ref, c_ref, a_ref, w_ref, o_ref):
        b, c, a = b_ref[...].astype(F32), c_ref[...].astype(F32), a_ref[...].astype(F32)
        o_ref[...] = (b * _conv3(c * a, w_ref[...])).astype(BF16)

    col = lambda off: pl.BlockSpec((s, LANES), lambda j: (0, off + j))
    return pl.pallas_call(
        body, name="gconv_fwd", out_shape=_sds((s, A_WIDTH), BF16), grid=(nb,),
        in_specs=[col(0), col(nb), col(2 * nb), pl.BlockSpec((3, LANES), lambda j: (0, j))],
        out_specs=pl.BlockSpec((s, LANES), lambda j: (0, j)),
        compiler_params=_params(("parallel",), VMEM_BIG),
    )(z, z, z, conv_w)


def gconv_bwd(z, conv_w, dycat):
    s = z.shape[0]
    nb = A_WIDTH // LANES

    def body(b_ref, c_ref, a_ref, w_ref, dy_ref, db_ref, dc_ref, da_ref, dw_ref):
        c, a, w, dy = c_ref[...].astype(F32), a_ref[...].astype(F32), w_ref[...], dy_ref[...].astype(F32)
        ca = c * a
        db_ref[...] = (dy * _conv3(ca, w)).astype(BF16)
        dconv = dy * b_ref[...].astype(F32)
        dw_ref[...] = _conv3_dw(dconv, ca)
        dca = _conv3_t(dconv, w)
        dc_ref[...] = (dca * a).astype(BF16)
        da_ref[...] = (dca * c).astype(BF16)

    col = lambda off: pl.BlockSpec((s, LANES), lambda j: (0, off + j))
    wspec = pl.BlockSpec((3, LANES), lambda j: (0, j))
    part = _sds((s, A_WIDTH), BF16)
    return pl.pallas_call(
        body, name="gconv_bwd", out_shape=(part, part, part, _sds((3, A_WIDTH))), grid=(nb,),
        in_specs=[col(0), col(nb), col(2 * nb), wspec, col(0)],
        out_specs=(col(0), col(0), col(0), wspec),
        compiler_params=_params(("parallel",), VMEM_BIG),
    )(z, z, z, conv_w, dycat)


def _pool_counts(s, w):
    t = lax.broadcasted_iota(jnp.int32, (s, 1), 0)
    return jnp.minimum(t + 1, w).astype(F32)


def _pooled(p, levels):
    acc = p
    for lv in range(levels):
        acc = acc + _shift_down(acc, 2 ** lv)
    return acc / _pool_counts(p.shape[0], 2 ** levels) - p


def pool_fwd(z, mix_w, scale):
    s = z.shape[0]

    def make(g):
        def body_g(p_ref, m_ref, sc_ref, o_ref):
            pooled = _pooled(p_ref[...].astype(F32), g + 1)
            y = _dot(pooled.astype(BF16), m_ref[...].astype(BF16))
            o_ref[...] = (y * sc_ref[...]).astype(BF16)
        return body_g

    outs = []
    for g in range(B_GROUPS):
        outs.append(pl.pallas_call(
            make(g), name=f"pool_fwd{g}", out_shape=_sds((s, LANES), BF16), grid=(1,),
            in_specs=[pl.BlockSpec((s, LANES), lambda i, g=g: (0, 3 * (A_WIDTH // LANES) + g)),
                      pl.BlockSpec((None, LANES, LANES), lambda i, g=g: (g, 0, 0)),
                      pl.BlockSpec((1, LANES), lambda i, g=g: (0, g))],
            out_specs=pl.BlockSpec((s, LANES), lambda i: (0, 0)),
            compiler_params=_params(("arbitrary",), VMEM_BIG),
        )(z, mix_w, scale))
    return outs


def pool_bwd(z, mix_w, scale, dycat):
    s = z.shape[0]

    def make(g):
        w = 2 ** (g + 1)

        def body_g(p_ref, m_ref, sc_ref, dy_ref, dp_ref, dm_ref, dsc_ref):
            pooled = _pooled(p_ref[...].astype(F32), g + 1)
            mw = m_ref[...].astype(BF16)
            pb = pooled.astype(BF16)
            dy = dy_ref[...].astype(F32)
            dsc_ref[...] = jnp.sum(dy * _dot(pb, mw), axis=0, keepdims=True)
            dmix = (dy * sc_ref[...]).astype(BF16)
            dm_ref[...] = _dot(pb, dmix, _TN)
            dpool = _dot(dmix, mw, _NT)
            acc = dpool / _pool_counts(s, w)
            for lv in range(g + 1):
                acc = acc + _shift_up(acc, 2 ** lv)
            dp_ref[...] = (acc - dpool).astype(BF16)
        return body_g

    outs = []
    for g in range(B_GROUPS):
        outs.append(pl.pallas_call(
            make(g), name=f"pool_bwd{g}",
            out_shape=(_sds((s, LANES), BF16), _sds((LANES, LANES)), _sds((1, LANES))), grid=(1,),
            in_specs=[pl.BlockSpec((s, LANES), lambda i, g=g: (0, 3 * (A_WIDTH // LANES) + g)),
                      pl.BlockSpec((None, LANES, LANES), lambda i, g=g: (g, 0, 0)),
                      pl.BlockSpec((1, LANES), lambda i, g=g: (0, g)),
                      pl.BlockSpec((s, LANES), lambda i, g=g: (0, A_WIDTH // LANES + g))],
            out_specs=(pl.BlockSpec((s, LANES), lambda i: (0, 0)), pl.BlockSpec((LANES, LANES), lambda i: (0, 0)),
                       pl.BlockSpec((1, LANES), lambda i: (0, 0))),
            compiler_params=_params(("arbitrary",), VMEM_BIG),
        )(z, mix_w, scale, dycat))
    return outs


_FF_BLOCKS = D_FF // FF_UNIT


def _ff_spec(s):
    return pl.BlockSpec((2, s, FF_UNIT), lambda j: (0, 0, j))


def _ff_wspecs():
    return [pl.BlockSpec((3, FF_UNIT), lambda j: (0, j)), pl.BlockSpec((3, FF_UNIT), lambda j: (0, _FF_BLOCKS + j))]


_FF_ROWS = 64
_FF_HALO = 16


def _chunk_taps(z_ref, half, c):
    start = pl.multiple_of(c * _FF_ROWS, _FF_ROWS)
    before = pl.multiple_of(jnp.maximum(c * _FF_ROWS - _FF_HALO, 0), _FF_HALO)
    halo = z_ref[half, pl.ds(before, _FF_HALO), :].astype(F32)
    halo = jnp.where(c > 0, halo, 0.0)
    win = jnp.concatenate([halo, z_ref[half, pl.ds(start, _FF_ROWS), :].astype(F32)], axis=0)
    return tuple(pltpu.roll(win, k, axis=0)[_FF_HALO:] for k in (2, 1)) + (win[_FF_HALO:],)


def _fold8(v):
    acc = v[0:8]
    for r in range(8, v.shape[0], 8):
        acc = acc + v[r:r + 8]
    return acc


def ffn_act_fwd(zf, conv_w, name):
    s = zf.shape[1]
    assert s % _FF_ROWS == 0

    def body(z_ref, wg_ref, wu_ref, o_ref):
        g = _conv3(z_ref[0].astype(F32), wg_ref[...])
        u = _conv3(z_ref[1].astype(F32), wu_ref[...])
        o_ref[...] = (g * _sigmoid(g) * u).astype(BF16)

    return pl.pallas_call(
        body, name=name, out_shape=_sds((s, D_FF), BF16), grid=(_FF_BLOCKS,),
        in_specs=[_ff_spec(s)] + _ff_wspecs(), out_specs=pl.BlockSpec((s, FF_UNIT), lambda j: (0, j)),
        compiler_params=_params(("parallel",), VMEM_BIG),
    )(zf, conv_w, conv_w)


def ffn_act_bwd(zf, conv_w, da, name):
    s = zf.shape[1]
    assert s % _FF_ROWS == 0
    nchunks = s // _FF_ROWS

    def body(z_ref, wg_ref, wu_ref, da_ref, dz_ref, dw_ref, dg_ref, du_ref):
        wg, wu = wg_ref[...], wu_ref[...]

        def first(c, acc):
            rows = pl.ds(pl.multiple_of(c * _FF_ROWS, _FF_ROWS), _FF_ROWS)
            tg, tu = _chunk_taps(z_ref, 0, c), _chunk_taps(z_ref, 1, c)
            g = _conv3_taps(tg, wg)
            u = _conv3_taps(tu, wu)
            dav = da_ref[rows, :].astype(F32)
            sg = _sigmoid(g)
            dg = dav * u * (sg * (1.0 + g * (1.0 - sg)))
            du = dav * (g * sg)
            dg_ref[rows, :] = dg
            du_ref[rows, :] = du
            return tuple(a + _fold8(d * t) for a, (d, t) in zip(acc, [(dg, t) for t in tg] + [(du, t) for t in tu]))

        zero = jnp.zeros((8, FF_UNIT), F32)
        acc = lax.fori_loop(0, nchunks, first, (zero,) * 6)
        sums = [jnp.sum(a, axis=0, keepdims=True) for a in acc]
        dw_ref[0] = jnp.concatenate(sums[:3], axis=0)
        dw_ref[1] = jnp.concatenate(sums[3:], axis=0)

        tail = pl.ds(s, _FF_HALO)
        dg_ref[tail, :] = jnp.zeros((_FF_HALO, FF_UNIT), F32)
        du_ref[tail, :] = jnp.zeros((_FF_HALO, FF_UNIT), F32)
        span = _FF_ROWS + _FF_HALO

        def second(c, carry):
            start = pl.multiple_of(c * _FF_ROWS, _FF_ROWS)
            for half, (d_ref, w) in enumerate(((dg_ref, wg), (du_ref, wu))):
                win = d_ref[pl.ds(start, span), :]
                dz = (w[0:1, :] * pltpu.roll(win, span - 2, axis=0)[:_FF_ROWS]
                      + w[1:2, :] * pltpu.roll(win, span - 1, axis=0)[:_FF_ROWS] + w[2:3, :] * win[:_FF_ROWS])
                dz_ref[half, pl.ds(start, _FF_ROWS), :] = dz.astype(BF16)
            return carry

        lax.fori_loop(0, nchunks, second, 0)

    return pl.pallas_call(
        body, name=name, out_shape=(_sds((2, s, D_FF), BF16), _sds((2, 3, D_FF))), grid=(_FF_BLOCKS,),
        in_specs=[_ff_spec(s)] + _ff_wspecs() + [pl.BlockSpec((s, FF_UNIT), lambda j: (0, j))],
        out_specs=(_ff_spec(s), pl.BlockSpec((2, 3, FF_UNIT), lambda j: (0, 0, j))),
        scratch_shapes=[pltpu.VMEM((s + _FF_HALO, FF_UNIT), F32), pltpu.VMEM((s + _FF_HALO, FF_UNIT), F32)],
        compiler_params=_params(("parallel",), VMEM_BIG),
    )(zf, conv_w, conv_w, da)


def _rope(v, cs, s1, s2):
    return v * cs + pltpu.roll(v, LANES - C_ROPE // 2, axis=1) * s1 + pltpu.roll(v, C_ROPE // 2, axis=1) * s2


def _rope_t(dv, cs, s1, s2):
    return dv * cs + pltpu.roll(dv * s1, C_ROPE // 2, axis=1) + pltpu.roll(dv * s2, LANES - C_ROPE // 2, axis=1)


def _kpe_mask(shape):
    lane = lax.broadcasted_iota(jnp.int32, shape, 1)
    return (lane >= C_NOPE) & (lane < C_NOPE + C_ROPE)


def _rms(v, g):
    r = lax.rsqrt(jnp.mean(v * v, axis=-1, keepdims=True) + EPS)
    return v * r, r


def _rms_bwd(dn, xh, r, g):
    dxh = dn * g
    return r * (dxh - xh * jnp.mean(dxh * xh, axis=-1, keepdims=True)), jnp.sum(dn * xh, axis=0, keepdims=True)


_ZQ = C_Q_RANK + C_KV_RANK + HEAD_PAD
_HW = C_HEADS * HEAD_PAD


def mla_pre_fwd(z, gq, gkv, wq, wk, wv, cs, s1, s2):
    s = z.shape[0]
    tm = _tile(s, 256, 8)

    def body(z_ref, gq_ref, gkv_ref, wq_ref, wk_ref, wv_ref, cs_ref, s1_ref, s2_ref, q_ref, k_ref, v_ref):
        zv = z_ref[...].astype(F32)
        cst, s1t, s2t = cs_ref[...], s1_ref[...], s2_ref[...]
        qh, _ = _rms(zv[:, :C_Q_RANK], None)
        qn = (qh * gq_ref[...]).astype(BF16)
        q = _dot(qn, wq_ref[...])
        kh, _ = _rms(zv[:, C_Q_RANK:C_Q_RANK + C_KV_RANK], None)
        kvn = (kh * gkv_ref[...]).astype(BF16)
        k = _dot(kvn, wk_ref[...])
        v_ref[...] = _dot(kvn, wv_ref[...]).astype(BF16)
        kpe = _rope(zv[:, C_Q_RANK + C_KV_RANK:], cst, s1t, s2t)
        for h in range(C_HEADS):
            sl = slice(h * HEAD_PAD, (h + 1) * HEAD_PAD)
            q_ref[:, sl] = _rope(q[:, sl], cst, s1t, s2t).astype(BF16)
            k_ref[:, sl] = (k[:, sl] + kpe).astype(BF16)

    full = lambda r, c: pl.BlockSpec((r, c), lambda i: (0, 0))
    hw = _sds((s, _HW), BF16)
    return pl.pallas_call(
        body, name="mla_pre_fwd", out_shape=(hw, hw, hw), grid=(s // tm,),
        in_specs=[_rows(tm, _ZQ), _vec(C_Q_RANK), _vec(C_KV_RANK), full(C_Q_RANK, _HW), full(C_KV_RANK, _HW),
                  full(C_KV_RANK, _HW), _rows(tm, LANES), _rows(tm, LANES), _rows(tm, LANES)],
        out_specs=(_rows(tm, _HW), _rows(tm, _HW), _rows(tm, _HW)),
        compiler_params=_params(("parallel",), VMEM_BIG),
    )(z, gq, gkv, wq, wk, wv, cs, s1, s2)


def mla_pre_bwd(z, gq, gkv, wq, wk, wv, cs, s1, s2, dq, dk, dv):
    s = z.shape[0]
    tm = _tile(s, 256, 8)

    def body(z_ref, gq_ref, gkv_ref, wq_ref, wk_ref, wv_ref, cs_ref, s1_ref, s2_ref, dq_ref, dk_ref, dv_ref,
             dz_ref, dwq_ref, dwk_ref, dwv_ref, dgq_ref, dgkv_ref):
        @pl.when(pl.program_id(0) == 0)
        def _():
            dwq_ref[...] = jnp.zeros_like(dwq_ref)
            dwk_ref[...] = jnp.zeros_like(dwk_ref)
            dwv_ref[...] = jnp.zeros_like(dwv_ref)
            dgq_ref[...] = jnp.zeros_like(dgq_ref)
            dgkv_ref[...] = jnp.zeros_like(dgkv_ref)

        zv = z_ref[...].astype(F32)
        cst, s1t, s2t = cs_ref[...], s1_ref[...], s2_ref[...]
        gqv, gkvv = gq_ref[...], gkv_ref[...]
        qh, rq = _rms(zv[:, :C_Q_RANK], None)
        qn = (qh * gqv).astype(BF16)
        kh, rk = _rms(zv[:, C_Q_RANK:C_Q_RANK + C_KV_RANK], None)
        kvn = (kh * gkvv).astype(BF16)

        dqv = dq_ref[...].astype(F32)
        dqp = jnp.concatenate(
            [_rope_t(dqv[:, h * HEAD_PAD:(h + 1) * HEAD_PAD], cst, s1t, s2t) for h in range(C_HEADS)], axis=1
        ).astype(BF16)
        dwq_ref[...] += _dot(qn, dqp, _TN)
        dqn = _dot(dqp, wq_ref[...], _NT)
        dql, dgq = _rms_bwd(dqn, qh, rq, gqv)
        dgq_ref[...] += dgq

        dkv = dk_ref[...]
        dkb = dkv.astype(BF16)
        dvb = dv_ref[...].astype(BF16)
        dwk_ref[...] += _dot(kvn, dkb, _TN)
        dwv_ref[...] += _dot(kvn, dvb, _TN)
        dkvn = _dot(dkb, wk_ref[...], _NT) + _dot(dvb, wv_ref[...], _NT)
        dkl, dgkv = _rms_bwd(dkvn, kh, rk, gkvv)
        dgkv_ref[...] += dgkv

        dkpe = dkv[:, :HEAD_PAD]
        for h in range(1, C_HEADS):
            dkpe = dkpe + dkv[:, h * HEAD_PAD:(h + 1) * HEAD_PAD]
        dkpe = _rope_t(jnp.where(_kpe_mask(dkpe.shape), dkpe, 0.0), cst, s1t, s2t)
        dz_ref[...] = jnp.concatenate([dql, dkl, dkpe], axis=1).astype(BF16)

    full = lambda r, c: pl.BlockSpec((r, c), lambda i: (0, 0))
    return pl.pallas_call(
        body, name="mla_pre_bwd",
        out_shape=(_sds((s, _ZQ), BF16), _sds((C_Q_RANK, _HW)), _sds((C_KV_RANK, _HW)), _sds((C_KV_RANK, _HW)),
                   _sds((1, C_Q_RANK)), _sds((1, C_KV_RANK))),
        grid=(s // tm,),
        in_specs=[_rows(tm, _ZQ), _vec(C_Q_RANK), _vec(C_KV_RANK), full(C_Q_RANK, _HW), full(C_KV_RANK, _HW),
                  full(C_KV_RANK, _HW), _rows(tm, LANES), _rows(tm, LANES), _rows(tm, LANES),
                  _rows(tm, _HW), _rows(tm, _HW), _rows(tm, _HW)],
        out_specs=(_rows(tm, _ZQ), full(C_Q_RANK, _HW), full(C_KV_RANK, _HW), full(C_KV_RANK, _HW),
                   _vec(C_Q_RANK), _vec(C_KV_RANK)),
        compiler_params=_params(("arbitrary",), VMEM_BIG),
    )(z, gq, gkv, wq, wk, wv, cs, s1, s2, dq, dk, dv)


_ATT_SCALE = (C_NOPE + C_ROPE) ** -0.5
_NEG = -1e30


def _att_exp(q, k, row0, ends_here):
    sc = _dot(q, k, _NT) * _ATT_SCALE
    tq, nk = sc.shape
    if ends_here:
        last = sc[:, nk - tq:]
        row = lax.broadcasted_iota(jnp.int32, last.shape, 0)
        col = lax.broadcasted_iota(jnp.int32, last.shape, 1)
        last = jnp.where(col <= row, last, _NEG)
        sc = last if nk == tq else jnp.concatenate([sc[:, :nk - tq], last], axis=1)
    else:
        qpos = row0 + lax.broadcasted_iota(jnp.int32, sc.shape, 0)
        kpos = lax.broadcasted_iota(jnp.int32, sc.shape, 1)
        sc = jnp.where(kpos <= qpos, sc, _NEG)
    e = jnp.exp(sc - jnp.max(sc, axis=-1, keepdims=True))
    return e, 1.0 / jnp.sum(e, axis=-1, keepdims=True)


def _causal_cases(i, nq, tq, fn):
    if nq > 8:
        fn(nq * tq, False)
        return
    for blk in range(nq):
        pl.when(i == blk)(functools.partial(fn, (blk + 1) * tq, True))


def attn_fwd(q, k, v):
    s = q.shape[0]
    tq = _tile(s, 256, 8)
    nq = s // tq

    def body(q_ref, k_ref, v_ref, o_ref):
        i = pl.program_id(1)

        def case(nk, ends_here):
            e, inv = _att_exp(q_ref[...], k_ref[:nk, :], i * tq, ends_here)
            o_ref[...] = (_dot(e.astype(BF16), v_ref[:nk, :]) * inv).astype(BF16)

        _causal_cases(i, nq, tq, case)

    qspec = pl.BlockSpec((tq, HEAD_PAD), lambda h, i: (i, h))
    kspec = pl.BlockSpec((s, HEAD_PAD), lambda h, i: (0, h))
    return pl.pallas_call(
        body, name="attn_fwd", out_shape=_sds((s, _HW), BF16), grid=(C_HEADS, s // tq),
        in_specs=[qspec, kspec, kspec], out_specs=qspec,
        compiler_params=_params(("parallel", "parallel"), VMEM_BIG),
    )(q, k, v)


def attn_bwd(q, k, v, o, do_all, do_col0):
    s = q.shape[0]
    tq = _tile(s, 256, 8)

    def body(q_ref, k_ref, v_ref, o_ref, do_ref, dq_ref, dk_ref, dv_ref):
        i = pl.program_id(1)

        @pl.when(i == 0)
        def _():
            dk_ref[...] = jnp.zeros_like(dk_ref)
            dv_ref[...] = jnp.zeros_like(dv_ref)

        def case(nk, ends_here):
            qv, kv, vv, dov = q_ref[...], k_ref[:nk, :], v_ref[:nk, :], do_ref[...]
            e, inv = _att_exp(qv, kv, i * tq, ends_here)
            p = e * inv
            dp = _dot(dov, vv, _NT)
            delta = jnp.sum(dov.astype(F32) * o_ref[...].astype(F32), axis=-1, keepdims=True)
            ds = (p * (dp - delta) * _ATT_SCALE).astype(BF16)
            dq_ref[...] = _dot(ds, kv).astype(BF16)
            dk_ref[:nk, :] += _dot(ds, qv, _TN)
            dv_ref[:nk, :] += _dot(p.astype(BF16), dov, _TN)

        _causal_cases(i, s // tq, tq, case)

    qspec = pl.BlockSpec((tq, HEAD_PAD), lambda h, i: (i, h))
    dospec = pl.BlockSpec((tq, HEAD_PAD), lambda h, i: (i, do_col0 + h))
    kspec = pl.BlockSpec((s, HEAD_PAD), lambda h, i: (0, h))
    return pl.pallas_call(
        body, name="attn_bwd", out_shape=(_sds((s, _HW), BF16), _sds((s, _HW)), _sds((s, _HW))),
        grid=(C_HEADS, s // tq),
        in_specs=[qspec, kspec, kspec, qspec, dospec], out_specs=(qspec, kspec, kspec),
        compiler_params=_params(("parallel", "arbitrary"), VMEM_BIG),
    )(q, k, v, o, do_all)


_DW = D_GROUPS * LANES


def _tril_bf16(w):
    r = lax.broadcasted_iota(jnp.int32, w.shape, 0)
    c = lax.broadcasted_iota(jnp.int32, w.shape, 1)
    return jnp.where(c <= r, w, 0.0).astype(BF16)


def _sgu_forward(zu, zv, lg, lb, ws_ref, bs):
    u = _gelu(zu)
    v = _gelu(zv)
    mu = jnp.mean(v, axis=-1, keepdims=True)
    vc = v - mu
    rstd = lax.rsqrt(jnp.mean(vc * vc, axis=-1, keepdims=True) + EPS)
    xh = vc * rstd
    vln = (xh * lg + lb).astype(BF16)
    mixed = []
    for g in range(D_GROUPS):
        wg = _tril_bf16(ws_ref[g])
        mixed.append(_dot(wg, vln[:, g * LANES:(g + 1) * LANES]) + bs[:, g:g + 1])
    return u, xh, rstd, vln, jnp.concatenate(mixed, axis=1)


def sgu_fwd(z, lg, lb, ws, bs_t):
    s = z.shape[0]
    nchunk = s // D_CHUNK

    def body(zu_ref, zv_ref, lg_ref, lb_ref, ws_ref, bs_ref, o_ref):
        u, _, _, _, mixed = _sgu_forward(zu_ref[...].astype(F32), zv_ref[...].astype(F32), lg_ref[...], lb_ref[...],
                                         ws_ref, bs_ref[...])
        o_ref[...] = (u * mixed).astype(BF16)

    return pl.pallas_call(
        body, name="sgu_fwd", out_shape=_sds((s, _DW), BF16), grid=(nchunk,),
        in_specs=[pl.BlockSpec((D_CHUNK, _DW), lambda n: (n, 1)), pl.BlockSpec((D_CHUNK, _DW), lambda n: (n, 2)),
                  _vec(_DW), _vec(_DW), pl.BlockSpec((D_GROUPS, D_CHUNK, D_CHUNK), lambda n: (0, 0, 0)),
                  pl.BlockSpec((D_CHUNK, LANES), lambda n: (0, 0))],
        out_specs=pl.BlockSpec((D_CHUNK, _DW), lambda n: (n, 0)),
        compiler_params=_params(("parallel",)),
    )(z, z, lg, lb, ws, bs_t)


def sgu_bwd(z, lg, lb, ws, bs_t, dycat, dy_col):
    s = z.shape[0]
    nchunk = s // D_CHUNK

    def body(zu_ref, zv_ref, lg_ref, lb_ref, ws_ref, bs_ref, dy_ref, dzu_ref, dzv_ref, dws_ref, dbs_ref, dlg_ref,
             dlb_ref):
        @pl.when(pl.program_id(0) == 0)
        def _():
            dws_ref[...] = jnp.zeros_like(dws_ref)
            dbs_ref[...] = jnp.zeros_like(dbs_ref)
            dlg_ref[...] = jnp.zeros_like(dlg_ref)
            dlb_ref[...] = jnp.zeros_like(dlb_ref)

        zu, zv, lg = zu_ref[...].astype(F32), zv_ref[...].astype(F32), lg_ref[...]
        u, xh, rstd, vln, mixed = _sgu_forward(zu, zv, lg, lb_ref[...], ws_ref, bs_ref[...])
        dy = dy_ref[...].astype(F32)
        dzu_ref[...] = (dy * mixed * _gelu_grad(zu)).astype(BF16)
        dmix = dy * u
        lane = lax.broadcasted_iota(jnp.int32, (D_CHUNK, LANES), 1)
        row = lax.broadcasted_iota(jnp.int32, (D_CHUNK, D_CHUNK), 0)
        colm = lax.broadcasted_iota(jnp.int32, (D_CHUNK, D_CHUNK), 1)
        dvln = []
        dbs = jnp.zeros((D_CHUNK, LANES), F32)
        for g in range(D_GROUPS):
            sl = slice(g * LANES, (g + 1) * LANES)
            dmg = dmix[:, sl]
            dbs = dbs + jnp.where(lane == g, jnp.sum(dmg, axis=-1, keepdims=True), 0.0)
            dmb = dmg.astype(BF16)
            dws_ref[g] += jnp.where(colm <= row, _dot(dmb, vln[:, sl], _NT), 0.0)
            dvln.append(_dot(_tril_bf16(ws_ref[g]), dmb, _TN))
        dbs_ref[...] += dbs
        dvln = jnp.concatenate(dvln, axis=1)
        dlg_ref[...] += jnp.sum(dvln * xh, axis=0, keepdims=True)
        dlb_ref[...] += jnp.sum(dvln, axis=0, keepdims=True)
        dxh = dvln * lg
        dvv = rstd * (dxh - jnp.mean(dxh, axis=-1, keepdims=True) - xh * jnp.mean(dxh * xh, axis=-1, keepdims=True))
        dzv_ref[...] = (dvv * _gelu_grad(zv)).astype(BF16)

    wsspec = pl.BlockSpec((D_GROUPS, D_CHUNK, D_CHUNK), lambda n: (0, 0, 0))
    chunk = lambda cidx: pl.BlockSpec((D_CHUNK, _DW), lambda n: (n, cidx))
    return pl.pallas_call(
        body, name="sgu_bwd",
        out_shape=(_sds((s, _DW), BF16), _sds((s, _DW), BF16), _sds((D_GROUPS, D_CHUNK, D_CHUNK)),
                   _sds((D_CHUNK, LANES)), _sds((1, _DW)), _sds((1, _DW))),
        grid=(nchunk,),
        in_specs=[chunk(1), chunk(2), _vec(_DW), _vec(_DW), wsspec, pl.BlockSpec((D_CHUNK, LANES), lambda n: (0, 0)),
                  chunk(dy_col)],
        out_specs=(chunk(0), chunk(0), wsspec, pl.BlockSpec((D_CHUNK, LANES), lambda n: (0, 0)), _vec(_DW), _vec(_DW)),
        compiler_params=_params(("arbitrary",)),
    )(z, z, lg, lb, ws, bs_t, dycat)


def ada_mod(c_all, ada_w, ada_b):
    nl, d, n = ada_w.shape
    nb = c_all.shape[0]
    tn = _tile(n, 512)

    def body(c_ref, w_ref, b_ref, o_ref):
        cv = c_ref[...]
        ca = (cv * _sigmoid(cv)).astype(BF16)
        o_ref[...] = _dot(ca, w_ref[...].astype(BF16)) + b_ref[...]

    return pl.pallas_call(
        body, name="ada_mod", out_shape=_sds((nl, nb, n)), grid=(nl, n // tn),
        in_specs=[pl.BlockSpec((nb, d), lambda l, j: (0, 0)), pl.BlockSpec((None, d, tn), lambda l, j: (l, 0, j)),
                  pl.BlockSpec((None, 1, tn), lambda l, j: (l, 0, j))],
        out_specs=pl.BlockSpec((None, nb, tn), lambda l, j: (l, 0, j)),
        compiler_params=_params(("parallel", "parallel")),
    )(c_all, ada_w, ada_b.reshape(nl, 1, n))


_ADAM_BLOCK = 256 * 1024


def _adam_rows(rows, cols):
    if rows * cols <= _ADAM_BLOCK or rows % 8:
        return rows
    return _tile(rows, max(8, _ADAM_BLOCK // cols), 8)


def _adam_update(w, gv, m, v):
    inv_bc1 = 1.0 / (1.0 - ADAM_B1 ** ADAM_STEP)
    inv_bc2 = 1.0 / (1.0 - ADAM_B2 ** ADAM_STEP)
    nm = ADAM_B1 * m + (1.0 - ADAM_B1) * gv
    nv = ADAM_B2 * v + (1.0 - ADAM_B2) * (gv * gv)
    return -ADAM_LR * ((nm * inv_bc1) / (jnp.sqrt(nv * inv_bc2) + ADAM_EPS) + ADAM_WD * w), nm, nv


def adamw(w, g, m, v, name):
    shape = w.shape
    cols = shape[-1]
    rows = w.size // cols
    tr = _adam_rows(rows, cols)

    def body(w_ref, g_ref, m_ref, v_ref, go_ref, d_ref, nm_ref, nv_ref):
        gv = g_ref[...]
        go_ref[...] = gv
        d_ref[...], nm_ref[...], nv_ref[...] = _adam_update(w_ref[...], gv, m_ref[...], v_ref[...])

    spec = pl.BlockSpec((tr, cols), lambda i: (i, 0))
    out = _sds((rows, cols))
    r2 = lambda t: t.reshape(rows, cols)
    res = pl.pallas_call(
        body, name=name, out_shape=(out,) * 4, grid=(rows // tr,),
        in_specs=[spec] * 4, out_specs=(spec,) * 4, compiler_params=_params(("parallel",)),
    )(r2(w), r2(g), r2(m), r2(v))
    return tuple(t.reshape(shape) for t in res)


def adamw_ada(w, c_all, dmod, m, v):
    nl, d, n = w.shape
    tr = _adam_rows(d, n)
    pad = 16 - c_all.shape[0]
    c16 = jnp.pad(c_all, ((0, pad), (0, 0)))
    dm16 = jnp.pad(dmod, ((0, 0), (0, pad), (0, 0)))

    def body(w_ref, c_ref, dm_ref, m_ref, v_ref, g_ref, d_ref, nm_ref, nv_ref):
        cv = c_ref[...]
        gv = _dot((cv * _sigmoid(cv)).astype(BF16), dm_ref[...].astype(BF16), _TN)
        g_ref[...] = gv
        d_ref[...], nm_ref[...], nv_ref[...] = _adam_update(w_ref[...], gv, m_ref[...], v_ref[...])

    spec = pl.BlockSpec((None, tr, n), lambda l, i: (l, i, 0))
    out = _sds((nl, d, n))
    return pl.pallas_call(
        body, name="adamw_ada_w", out_shape=(out, out, out, out), grid=(nl, d // tr),
        in_specs=[spec, pl.BlockSpec((16, tr), lambda l, i: (0, i)), pl.BlockSpec((None, 16, n), lambda l, i: (l, 0, 0)),
                  spec, spec],
        out_specs=(spec,) * 4, compiler_params=_params(("parallel", "parallel")),
    )(w, c16, dm16, m, v)


def adamw_small(ws, gs, ms, vs):
    n = len(ws)
    flat = lambda t: t.reshape(-1, t.shape[-1])

    def body(*refs):
        ins, outs = refs[:4 * n], refs[4 * n:]
        for i in range(n):
            w_ref, g_ref, m_ref, v_ref = ins[4 * i:4 * i + 4]
            outs[3 * i][...], outs[3 * i + 1][...], outs[3 * i + 2][...] = _adam_update(
                w_ref[...], g_ref[...], m_ref[...], v_ref[...])

    operands = [flat(t) for quad in zip(ws, gs, ms, vs) for t in quad]
    res = pl.pallas_call(
        body, name="adamw_small", out_shape=tuple(_sds(flat(w).shape) for w in ws for _ in range(3)),
    )(*operands)
    return [(g, res[3 * i].reshape(w.shape), res[3 * i + 1].reshape(w.shape), res[3 * i + 2].reshape(w.shape))
            for i, (w, g) in enumerate(zip(ws, gs))]


def adamw_layers(w, g0, g1, m, v, name):
    _, rows, cols = w.shape
    tr = _adam_rows(rows, cols)

    def body(w_ref, g0_ref, g1_ref, m_ref, v_ref, g_ref, d_ref, nm_ref, nv_ref):
        gv = jnp.where(pl.program_id(0) == 0, g0_ref[...], g1_ref[...])
        g_ref[...] = gv
        d_ref[...], nm_ref[...], nv_ref[...] = _adam_update(w_ref[...], gv, m_ref[...], v_ref[...])

    spec = pl.BlockSpec((None, tr, cols), lambda l, i: (l, i, 0))
    gspec = pl.BlockSpec((tr, cols), lambda l, i: (i, 0))
    out = _sds((2, rows, cols))
    return pl.pallas_call(
        body, name=name, out_shape=(out, out, out, out), grid=(2, rows // tr),
        in_specs=[spec, gspec, gspec, spec, spec], out_specs=(spec,) * 4, compiler_params=_params(("parallel", "parallel")),
    )(w, g0, g1, m, v)


def sum8(gathered):
    _, r, _ = gathered.shape
    tr = _tile(r, 512, 8)

    def body(g_ref, o_ref):
        acc = g_ref[0]
        for dev in range(1, N_DEV):
            acc = acc + g_ref[dev]
        o_ref[...] = acc

    return pl.pallas_call(
        body, name="sum8", out_shape=_sds((r, LANES)), grid=(r // tr,),
        in_specs=[pl.BlockSpec((N_DEV, tr, LANES), lambda i: (0, i, 0))], out_specs=pl.BlockSpec((tr, LANES), lambda i: (i, 0)),
        compiler_params=_params(("parallel",)),
    )(gathered)


_SUM_STEPS = 2


def pair_sums(gs, recvs, core, name):
    n = len(gs)
    trs = [g.shape[1] // 2 // _SUM_STEPS for g in gs]

    def body(c_ref, *refs):
        del c_ref
        for i in range(n):
            a_ref, b_ref, o_ref = refs[2 * i], refs[2 * i + 1], refs[2 * n + i]
            o_ref[...] = (a_ref[...].astype(F32) + b_ref[...].astype(F32)).astype(BF16)

    in_specs, out_specs = [], []
    for g, tr in zip(gs, trs):
        cols = g.shape[2]
        in_specs.append(pl.BlockSpec((None, tr, cols), lambda k, s, c: (k, c[0] * _SUM_STEPS + s, 0)))
        in_specs.append(pl.BlockSpec((None, tr, cols), lambda k, s, c: (k, s, 0)))
        out_specs.append(pl.BlockSpec((None, tr, cols), lambda k, s, c: (k, s, 0)))
    grid_spec = pltpu.PrefetchScalarGridSpec(num_scalar_prefetch=1, grid=(N_CHIPS, _SUM_STEPS), in_specs=in_specs,
                                             out_specs=tuple(out_specs))
    return list(pl.pallas_call(
        body, name=name, out_shape=tuple(_sds((N_CHIPS, g.shape[1] // 2, g.shape[2]), BF16) for g in gs),
        grid_spec=grid_spec, compiler_params=_params(("parallel", "parallel")),
    )(core.reshape(1).astype(jnp.int32), *[t for pair in zip(gs, recvs) for t in pair]))


def chip_sums(pairs, recvs, chip, core, name):
    n = len(pairs)
    trs = [p.shape[1] // _SUM_STEPS for p in pairs]

    def body(p_ref, *refs):
        del p_ref
        for i in range(n):
            own_ref, r_ref, o_ref = refs[2 * i], refs[2 * i + 1], refs[2 * n + i]
            acc = own_ref[...].astype(F32)
            for j in range(N_CHIPS - 1):
                acc = acc + r_ref[j].astype(F32)
            o_ref[...] = acc

    in_specs, out_specs = [], []
    for p, tr in zip(pairs, trs):
        cols = p.shape[2]
        in_specs.append(pl.BlockSpec((None, tr, cols), lambda s, q: (q[0], s, 0)))
        in_specs.append(pl.BlockSpec((N_CHIPS - 1, tr, cols), lambda s, q: (0, s, 0)))
        out_specs.append(pl.BlockSpec((None, tr, cols), lambda s, q: (q[1], s, 0)))
    grid_spec = pltpu.PrefetchScalarGridSpec(num_scalar_prefetch=1, grid=(_SUM_STEPS,), in_specs=in_specs,
                                             out_specs=tuple(out_specs))
    return list(pl.pallas_call(
        body, name=name, out_shape=tuple(_sds((2,) + p.shape[1:]) for p in pairs), grid_spec=grid_spec,
        compiler_params=_params(("parallel",)),
    )(jnp.stack([chip, core]).astype(jnp.int32), *[t for pair in zip(pairs, recvs) for t in pair]))


def _place():
    return lax.axis_index("x"), lax.axis_index("y"), lax.axis_index("c")


def _other_chips(x, y):
    return [(x, 1 - y), (1 - x, y), (1 - x, 1 - y)]


_HBM = pl.BlockSpec(memory_space=pltpu.HBM)


def all_gather8(v, name, after=()):
    m, n = v.shape

    def body(x_ref, *refs):
        out_ref, send_sems, recv_sems, local_sem = refs[len(after):]
        x, y, c = _place()
        me, sibling = (x, y, c), (x, y, 1 - c)
        chips = _other_chips(x, y)

        def rows(px, py, pc):
            return out_ref.at[pl.ds((4 * px + 2 * py + pc) * m, m), :]

        def copy(k, block, to, src=None):
            return pltpu.make_async_remote_copy(
                src_ref=rows(*block) if src is None else src, dst_ref=rows(*block),
                send_sem=send_sems.at[k], recv_sem=recv_sems.at[k], device_id=to, device_id_type=MESH)

        mine = pltpu.make_async_copy(x_ref, rows(*me), local_sem)
        mine.start()
        first = [copy(0, me, sibling, src=x_ref)]
        first += [copy(1 + j, me, (*chip, c), src=x_ref) for j, chip in enumerate(chips)]
        for cp in first:
            cp.start()
        passed = [copy(4 + j, (*chip, c), sibling) for j, chip in enumerate(chips)]
        for j, chip in enumerate(chips):
            copy(1 + j, (*chip, c), me).wait_recv()
            passed[j].start()
        copy(0, sibling, me).wait_recv()
        for j, chip in enumerate(chips):
            copy(4 + j, (*chip, 1 - c), me).wait_recv()
        for cp in first + passed:
            cp.wait_send()
        mine.wait()

    return pl.pallas_call(
        body, name=name, out_shape=_sds((N_DEV * m, n), v.dtype),
        in_specs=[pl.BlockSpec(memory_space=pltpu.VMEM)] + [pl.BlockSpec(memory_space=pl.ANY)] * len(after),
        out_specs=pl.BlockSpec(memory_space=pltpu.VMEM),
        scratch_shapes=[pltpu.SemaphoreType.DMA((7,)), pltpu.SemaphoreType.DMA((7,)), pltpu.SemaphoreType.DMA],
        compiler_params=_params(None, VMEM_BIG),
    )(v, *after)


def _comm_call(body, name, ins, out_shapes, nsem, aliases=None):
    return pl.pallas_call(
        body, name=name, out_shape=tuple(out_shapes), in_specs=[_HBM] * len(ins), out_specs=tuple([_HBM] * len(out_shapes)),
        scratch_shapes=[pltpu.SemaphoreType.DMA((nsem,)), pltpu.SemaphoreType.DMA((nsem,))],
        input_output_aliases=aliases or {},
    )(*ins)


def _remote(src, dst, send_sems, recv_sems, k, to):
    return pltpu.make_async_remote_copy(src_ref=src, dst_ref=dst, send_sem=send_sems.at[k], recv_sem=recv_sems.at[k],
                                        device_id=to, device_id_type=MESH)


def _half(core, rh):
    return pl.ds(pl.multiple_of(core * rh, 16), rh)


def swap_halves(gs, name):
    n = len(gs)

    def body(*refs):
        ins, outs, (send_sems, recv_sems) = refs[:n], refs[n:2 * n], refs[2 * n:]
        x, y, c = _place()
        copies = []
        for i in range(n):
            theirs = _half(1 - c, ins[i].shape[1] // 2)
            cp = _remote(ins[i].at[:, theirs], outs[i], send_sems, recv_sems, i, (x, y, 1 - c))
            cp.start()
            copies.append(cp)
        for cp in copies:
            cp.wait()

    return _comm_call(body, name, gs, [_sds((g.shape[0], g.shape[1] // 2, g.shape[2]), g.dtype) for g in gs], n)


def join_halves(bufs, name):
    n = len(bufs)

    def body(*refs):
        ins, outs, (send_sems, recv_sems) = refs[:n], refs[n:2 * n], refs[2 * n:]
        x, y, c = _place()
        copies = []
        for i in range(n):
            cp = _remote(ins[i].at[c], outs[i].at[c], send_sems, recv_sems, i, (x, y, 1 - c))
            cp.start()
            copies.append(cp)
        for i in range(n):
            theirs = outs[i].at[1 - c]
            _remote(theirs, theirs, send_sems, recv_sems, i, (x, y, 1 - c)).wait_recv()
        for cp in copies:
            cp.wait_send()

    return _comm_call(body, name, bufs, [_sds(b.shape, b.dtype) for b in bufs], n, {i: i for i in range(n)})


def forward_halves(lands, name):
    n = len(lands)

    def body(*refs):
        ins, outs, (send_sems, recv_sems) = refs[:n], refs[n:2 * n], refs[2 * n:]
        x, y, c = _place()
        sibling = (x, y, 1 - c)
        chips = _other_chips(x, y)
        copies = []
        for i in range(n):
            mine = _half(c, ins[i].shape[1] // 2)
            for j, (px, py) in enumerate(chips):
                cp = _remote(ins[i].at[2 * px + py, mine], outs[i].at[2 * px + py, mine], send_sems, recv_sems, 3 * i + j, sibling)
                cp.start()
                copies.append(cp)
        for i in range(n):
            theirs = _half(1 - c, ins[i].shape[1] // 2)
            for j, (px, py) in enumerate(chips):
                landed = outs[i].at[2 * px + py, theirs]
                _remote(landed, landed, send_sems, recv_sems, 3 * i + j, sibling).wait_recv()
        for cp in copies:
            cp.wait_send()

    return _comm_call(body, name, lands, [_sds(b.shape, b.dtype) for b in lands], 3 * n, {i: i for i in range(n)})


_SEM = pl.BlockSpec(memory_space=pltpu.SEMAPHORE)
_EFFECT = pltpu.SideEffectType.DATAFLOW_SIDE_EFFECTING


def _gather_copies(srcs, lands, send_sems, recv_sems):
    x, y, c = _place()
    copies = []
    for i in range(len(srcs)):
        mine = _half(c, srcs[i].shape[0] // 2)
        for j, chip in enumerate(_other_chips(x, y)):
            copies.append(_remote(srcs[i].at[mine], lands[i].at[2 * x + y, mine], send_sems, recv_sems, 3 * i + j, (*chip, c)))
    return copies


def _exchange_copies(srcs, lands, send_sems, recv_sems):
    x, y, c = _place()
    copies = []
    for i in range(len(srcs)):
        for j, (px, py) in enumerate(_other_chips(x, y)):
            copies.append(_remote(srcs[i].at[2 * px + py], lands[i].at[j], send_sems, recv_sems, 3 * i + j, (px, py, c)))
    return copies


def _everyone_copies(srcs, lands, send_sems, recv_sems):
    x, y, c = _place()
    flip = lambda v, b: 1 - v if b else v
    dst = lands[0].at[4 * x + 2 * y + c]
    return [_remote(srcs[0], dst, send_sems, recv_sems, j - 1, (flip(x, j & 4), flip(y, j & 2), flip(c, j & 1)))
            for j in range(1, N_DEV)]


GATHER = (_gather_copies, 3)
EXCHANGE = (_exchange_copies, 3)
EVERYONE = (_everyone_copies, N_DEV - 1)


def split_start(name, plan, srcs, land_shapes, after=()):
    copies_fn, per_source = plan
    n, m, k = len(srcs), len(land_shapes), len(after)
    ncopies = per_source * n

    def body(*refs):
        src_refs, land_refs = refs[:n], refs[n:n + m]
        send_sems, recv_sems = refs[n + m + k], refs[n + m + k + 1]
        token = refs[-1]
        for cp in copies_fn(src_refs, land_refs, send_sems, recv_sems):
            cp.start()
        token[...] = jnp.zeros_like(token)

    hbm = lambda s: pltpu.HBM(tuple(s.shape), s.dtype)
    outs = pl.pallas_call(
        body, name=name,
        out_shape=(pltpu.SemaphoreType.DMA((ncopies,)), pltpu.SemaphoreType.DMA((ncopies,)), *[hbm(s) for s in srcs],
                   *[hbm(s) for s in land_shapes], _sds((8, LANES))),
        in_specs=[_HBM] * (n + m) + [pl.BlockSpec(memory_space=pl.ANY)] * k,
        out_specs=(_SEM, _SEM, *([_HBM] * (n + m)), pl.BlockSpec(memory_space=pltpu.VMEM)),
        input_output_aliases={i: 2 + i for i in range(n + m)},
        compiler_params=pltpu.CompilerParams(has_side_effects=_EFFECT),
    )(*[pltpu.with_memory_space_constraint(s, pltpu.HBM) for s in srcs],
      *[pltpu.with_memory_space_constraint(lax.empty(tuple(s.shape), s.dtype), pltpu.HBM) for s in land_shapes], *after)
    handle = (outs[0], outs[1], list(outs[2:2 + n]), list(outs[2 + n:2 + n + m]))
    return handle, outs[-1][0, 0]


def split_wait(name, plan, handle, after):
    copies_fn, _ = plan
    send_sems, recv_sems, srcs, lands = handle
    n, m = len(srcs), len(lands)
    after = list(after) if isinstance(after, (list, tuple)) else [after]

    def body(*refs):
        src_refs, land_refs = refs[:n], refs[n:n + m]
        for cp in copies_fn(src_refs, land_refs, refs[n + m], refs[n + m + 1]):
            cp.wait_send()
            cp.wait_recv()

    hbm = lambda s: pltpu.HBM(tuple(s.shape), s.dtype)
    outs = pl.pallas_call(
        body, name=name, out_shape=tuple(hbm(s) for s in srcs + lands),
        in_specs=[_HBM] * (n + m) + [_SEM, _SEM] + [pl.BlockSpec(memory_space=pl.ANY)] * len(after),
        out_specs=tuple([_HBM] * (n + m)), input_output_aliases={i: i for i in range(n + m)},
        compiler_params=pltpu.CompilerParams(has_side_effects=_EFFECT),
    )(*srcs, *lands, send_sems, recv_sems, *after)
    return list(outs[:n]), list(outs[n:])


_CD_PAD = C_Q_RANK + C_KV_RANK + HEAD_PAD + 2 * _DW


def chip_major(w, groups=N_CHIPS):
    r, c = w.shape
    return w.reshape(r, groups, c // groups).transpose(1, 0, 2)


def from_chip_major(w):
    g, r, c = w.shape
    return w.transpose(1, 0, 2).reshape(r, g * c)


def _cd_in_pad(w):
    a = C_Q_RANK + C_KV_RANK
    z = lambda n: jnp.zeros((w.shape[0], n), w.dtype)
    return jnp.concatenate([w[:, :a], z(C_NOPE), w[:, a:a + C_ROPE], z(HEAD_PAD - C_NOPE - C_ROPE), w[:, a + C_ROPE:]], axis=1)


def _cd_in_unpad(w):
    a = C_Q_RANK + C_KV_RANK
    return jnp.concatenate([w[:, :a], w[:, a + C_NOPE:a + C_NOPE + C_ROPE], w[:, a + HEAD_PAD:]], axis=1)


def _pad_heads(w, width):
    r = w.shape[0]
    w = w.reshape(r, C_HEADS, width)
    return jnp.pad(w, ((0, 0), (0, 0), (0, HEAD_PAD - width))).reshape(r, _HW)


def _unpad_heads(w, width):
    r = w.shape[0]
    return w.reshape(r, C_HEADS, HEAD_PAD)[:, :, :width].reshape(r, C_HEADS * width)


_MATMUL_WEIGHTS = ("ab_w_in", "ab_w_out", "cd_w_in", "c_w_uq", "c_w_ukv", "cd_w_out", "ffn_w_up", "ffn_w_down")
_LAYER_STACKED = ("norm1_g", "norm2_g", "ffn_w_up", "ffn_conv_w", "ffn_w_down")
_ROW_VECTORS = ("b_scale", "c_q_norm_g", "c_kv_norm_g", "d_ln_g", "d_ln_b")


def full_to_local(p):
    q = {}
    for k, v in p.items():
        if k == "final_norm_g":
            v = v.reshape(1, -1)
        elif k not in _LAYER_STACKED and k not in _ROW_VECTORS:
            v = v[0]
        q[k] = v.astype(BF16) if k in _MATMUL_WEIGHTS else v
    return q


def local_to_full(g):
    q = {}
    for k, v in g.items():
        if k == "final_norm_g":
            q[k] = v.reshape(-1)
        elif k not in _LAYER_STACKED and k not in _ROW_VECTORS:
            q[k] = v[None]
        else:
            q[k] = v
    return q


def prepare_weights(p):
    q = dict(p)
    q["cd_w_in"] = _cd_in_pad(p["cd_w_in"])
    q["c_w_uq"] = _pad_heads(p["c_w_uq"], C_NOPE + C_ROPE)
    ukv = p["c_w_ukv"].reshape(C_KV_RANK, C_HEADS, C_NOPE + C_V)
    q["c_w_uk"] = _pad_heads(ukv[:, :, :C_NOPE].reshape(C_KV_RANK, -1), C_NOPE)
    q["c_w_uv"] = _pad_heads(ukv[:, :, C_NOPE:].reshape(C_KV_RANK, -1), C_V)
    wo = p["cd_w_out"]
    att_rows = jnp.pad(wo[:C_HEADS * C_V].reshape(C_HEADS, C_V, D_MODEL), ((0, 0), (0, HEAD_PAD - C_V), (0, 0)))
    q["cd_w_out"] = jnp.concatenate([att_rows.reshape(_HW, D_MODEL), wo[C_HEADS * C_V:]], axis=0)
    return q


def unprepare_grads(g):
    q = dict(g)
    q["cd_w_in"] = _cd_in_unpad(g["cd_w_in"])
    q["c_w_uq"] = _unpad_heads(g["c_w_uq"], C_NOPE + C_ROPE)
    uk = g.pop("c_w_uk").reshape(C_KV_RANK, C_HEADS, HEAD_PAD)[:, :, :C_NOPE]
    uv = g.pop("c_w_uv").reshape(C_KV_RANK, C_HEADS, HEAD_PAD)[:, :, :C_V]
    q.pop("c_w_uk", None)
    q.pop("c_w_uv", None)
    q["c_w_ukv"] = jnp.concatenate([uk, uv], axis=-1).reshape(C_KV_RANK, C_HEADS * (C_NOPE + C_V))
    wo = g["cd_w_out"]
    att = wo[:_HW].reshape(C_HEADS, HEAD_PAD, D_MODEL)[:, :C_V].reshape(C_HEADS * C_V, D_MODEL)
    q["cd_w_out"] = jnp.concatenate([att, wo[_HW:]], axis=0)
    return q


def rope_tables(positions):
    half = C_ROPE // 2
    inv_freq = ROPE_THETA ** (-jnp.arange(half, dtype=F32) / half)
    ang = positions.astype(F32)[:, None] * inv_freq
    cos, sin = jnp.cos(ang), jnp.sin(ang)
    s = positions.shape[0]
    z = lambda n: jnp.zeros((s, n), F32)
    cs = jnp.concatenate([jnp.ones((s, C_NOPE), F32), cos, cos, z(HEAD_PAD - C_NOPE - C_ROPE)], axis=1)
    s1 = jnp.concatenate([z(C_NOPE), -sin, z(HEAD_PAD - C_NOPE - half)], axis=1)
    s2 = jnp.concatenate([z(C_NOPE + half), sin, z(HEAD_PAD - C_NOPE - C_ROPE)], axis=1)
    return cs, s1, s2


def _mods(mod_l):
    return [mod_l[:, i * D_MODEL:(i + 1) * D_MODEL] for i in range(N_MOD)]


_UP_COLS = 2 * D_FF // N_CHIPS


def ffn_fwd(h2, w, l, late_down=None):
    zf = matmul(h2, w["ffn_w_up"][l], "nn", BF16, f"ffn_up{l}", gb=N_CHIPS, go=2, tn=_UP_COLS)
    a = ffn_act_fwd(zf, w["ffn_conv_w"][l], f"ffn_act_fwd{l}")
    if late_down is not None:
        late_down(a)
    f = matmul(a, w["ffn_w_down"][l], "nn", BF16, f"ffn_down{l}", tk=D_FF)
    return f, (zf, a)


def ffn_bwd(df, h2, saved, w, l):
    zf, a = saved
    da = matmul(df, w["ffn_w_down"][l], "nt", BF16, f"ffn_down_dx{l}", tn=D_FF // 2)
    d_down = matmul(a, df, "tn", BF16, f"ffn_down_dw{l}", tm=D_FF // 2)
    dzf, d_conv = ffn_act_bwd(zf, w["ffn_conv_w"][l], da, f"ffn_act_bwd{l}")
    dh2 = matmul(dzf, w["ffn_w_up"][l], "nt", F32, f"ffn_up_dx{l}", ga=2, gb=N_CHIPS, tk=_UP_COLS, tn=D_MODEL)
    d_up = matmul(h2, dzf, "tn", BF16, f"ffn_up_dw{l}", gb=2, go=N_CHIPS, tn=_UP_COLS)
    d_conv = d_conv.transpose(1, 0, 2).reshape(3, 2 * D_FF)
    return dh2, dict(ffn_w_down=d_down, ffn_conv_w=d_conv, ffn_w_up=d_up)


def mixer0_fwd(h, w):
    z = matmul(h, w["ab_w_in"], "nn", BF16, "ab_in", gb=N_CHIPS)
    ya = gconv_fwd(z, w["a_conv_w"])
    yb = pool_fwd(z, w["b_mix_w"], w["b_scale"])
    ycat = jnp.concatenate([ya] + yb, axis=1)
    y = matmul(ycat, w["ab_w_out"], "nn", BF16, "ab_out", tn=D_MODEL)
    return y, (z, ycat)


def mixer0_bwd(dy, h, saved, w):
    z, ycat = saved
    grads = {}
    dycat = matmul(dy, w["ab_w_out"], "nt", BF16, "ab_out_dx")
    grads["ab_w_out"] = matmul(ycat, dy, "tn", BF16, "ab_out_dw")
    db, dc, da, d_conv = gconv_bwd(z, w["a_conv_w"], dycat)
    pb = pool_bwd(z, w["b_mix_w"], w["b_scale"], dycat)
    dz = jnp.concatenate([db, dc, da] + [t[0] for t in pb], axis=1)
    dh = matmul(dz, w["ab_w_in"], "nt", F32, "ab_in_dx", gb=N_CHIPS, tn=D_MODEL)
    grads["ab_w_in"] = matmul(h, dz, "tn", BF16, "ab_in_dw", go=N_CHIPS)
    grads.update(a_conv_w=d_conv, b_mix_w=jnp.stack([t[1] for t in pb]),
                 b_scale=jnp.concatenate([t[2] for t in pb], axis=1))
    return dh, grads


def mixer1_fwd(h, ropes, w):
    cs, s1, s2 = ropes
    z = matmul(h, w["cd_w_in"], "nn", BF16, "cd_in")
    bs_t = jnp.pad(w["d_b_s"].T, ((0, 0), (0, LANES - D_GROUPS)))
    qh, kh, vh = mla_pre_fwd(z, w["c_q_norm_g"], w["c_kv_norm_g"], w["c_w_uq"], w["c_w_uk"], w["c_w_uv"], cs, s1, s2)
    oh = attn_fwd(qh, kh, vh)
    yd = sgu_fwd(z, w["d_ln_g"], w["d_ln_b"], w["d_w_s"], bs_t)
    ycat = jnp.concatenate([oh, yd], axis=1)
    y = matmul(ycat, w["cd_w_out"], "nn", BF16, "cd_out", tn=D_MODEL)
    return y, (z, bs_t, qh, kh, vh, oh, ycat)


def mixer1_bwd(dy, h, saved, ropes, w):
    cs, s1, s2 = ropes
    z, bs_t, qh, kh, vh, oh, ycat = saved
    grads = {}
    dycat = matmul(dy, w["cd_w_out"], "nt", BF16, "cd_out_dx")
    grads["cd_w_out"] = matmul(ycat, dy, "tn", F32, "cd_out_dw")
    dqh, dkh, dvh = attn_bwd(qh, kh, vh, oh, dycat, 0)
    dzq, d_uq, d_uk, d_uv, d_gq, d_gkv = mla_pre_bwd(
        z, w["c_q_norm_g"], w["c_kv_norm_g"], w["c_w_uq"], w["c_w_uk"], w["c_w_uv"], cs, s1, s2, dqh, dkh, dvh)
    dzu, dzv, d_ws, d_bs, d_lg, d_lb = sgu_bwd(z, w["d_ln_g"], w["d_ln_b"], w["d_w_s"], bs_t, dycat, _HW // _DW)
    dz = jnp.concatenate([dzq, dzu, dzv], axis=1)
    dh = matmul(dz, w["cd_w_in"], "nt", F32, "cd_in_dx", tn=D_MODEL)
    grads["cd_w_in"] = matmul(h, dz, "tn", F32, "cd_in_dw")
    grads.update(c_w_uq=d_uq, c_w_uk=d_uk, c_w_uv=d_uv, c_q_norm_g=d_gq, c_kv_norm_g=d_gkv, d_w_s=d_ws,
                 d_b_s=d_bs[:, :D_GROUPS].T, d_ln_g=d_lg, d_ln_b=d_lb)
    return dh, grads


class StepHooks:
    def weights(self, stage, after):
        pass

    def gradients(self, stage, grads, after):
        return 0.0


def run_step(x, tgt, mod, ropes, w, hooks):
    sh1a, sc1a, g1a, sh2a, sc2a, g2a = _mods(mod[0:1])
    sh1b, sc1b, g1b, sh2b, sc2b, g2b = _mods(mod[1:2])
    n1, n2 = w["norm1_g"], w["norm2_g"]

    hooks.weights("mix0", mod)
    h0 = modnorm_fwd(x, n1[0:1], sc1a, sh1a, "modnorm_0")
    y0, mix0 = mixer0_fwd(h0, w)
    x1, h1 = resid_modnorm_fwd(x, y0, g1a, n2[0:1], sc2a, sh2a, "resid_modnorm_1")
    hooks.weights("up0", x1)
    f0, ffn0 = ffn_fwd(h1, w, 0, lambda act: hooks.weights("down0", act))
    x2, h2 = resid_modnorm_fwd(x1, f0, g2a, n1[1:2], sc1b, sh1b, "resid_modnorm_2")
    hooks.weights("mix1", x2)
    y1, mix1 = mixer1_fwd(h2, ropes, w)
    x3, h3 = resid_modnorm_fwd(x2, y1, g1b, n2[1:2], sc2b, sh2b, "resid_modnorm_3")
    hooks.weights("ffn1", x3)
    f1, ffn1 = ffn_fwd(h3, w, 1)
    dres, d_final, loss, df1, dg2b = final_fused(x3, f1, g2b, w["final_norm_g"], tgt)

    dh3, gf1 = ffn_bwd(df1, h3, ffn1, w, 1)
    tok = hooks.gradients("ffn1", gf1, dh3)
    dres, dsh2b, dsc2b, dn2b, dy1, dg1b = norm_gate_bwd(x3, dh3, n2[1:2], sc2b, dres, y1, g1b + tok, "norm_gate_bwd_3")
    dh2, gm1 = mixer1_bwd(dy1, h2, mix1, ropes, w)
    tok = hooks.gradients("mix1", gm1, dh2)
    dres, dsh1b, dsc1b, dn1b, df0, dg2a = norm_gate_bwd(x2, dh2, n1[1:2], sc1b, dres, f0, g2a + tok, "norm_gate_bwd_2")
    dh1, gf0 = ffn_bwd(df0, h1, ffn0, w, 0)
    tok = hooks.gradients("ffn0", gf0, dh1)
    dres, dsh2a, dsc2a, dn2a, dy0, dg1a = norm_gate_bwd(x1, dh1, n2[0:1], sc2a, dres, y0, g1a + tok, "norm_gate_bwd_1")
    dh0, gm0 = mixer0_bwd(dy0, h0, mix0, w)
    tok = hooks.gradients("mix0", gm0, dh0)
    grad_x, dsh1a, dsc1a, dn1a = norm_bwd(x, dh0, n1[0:1], sc1a + tok, dres, "norm_bwd_0")

    dmod = jnp.concatenate([jnp.concatenate([dsh1a, dsc1a, dg1a, dsh2a, dsc2a, dg2a], axis=1),
                            jnp.concatenate([dsh1b, dsc1b, dg1b, dsh2b, dsc2b, dg2b], axis=1)], axis=0)
    norms = dict(norm1_g=jnp.concatenate([dn1a, dn1b], axis=0), norm2_g=jnp.concatenate([dn2a, dn2b], axis=0),
                 final_norm_g=d_final)
    return loss, grad_x, dmod, dict(mix0=gm0, ffn0=gf0, mix1=gm1, ffn1=gf1, norms=norms)


def merge_grads(by_stage):
    grads = {**by_stage["mix0"], **by_stage["mix1"], **by_stage["norms"]}
    for k in ("ffn_w_down", "ffn_w_up"):
        grads[k] = [by_stage["ffn0"][k], by_stage["ffn1"][k]]
    grads["ffn_conv_w"] = jnp.stack([by_stage["ffn0"]["ffn_conv_w"], by_stage["ffn1"]["ffn_conv_w"]])
    return grads


def local_step(x, tgt, mod, ropes, w):
    loss, grad_x, dmod, by_stage = run_step(x, tgt, mod, ropes, w, StepHooks())
    return loss, grad_x, dmod, merge_grads(by_stage)


_WEIGHTS = ("ada_w", "ada_b", "norm1_g", "norm2_g", "ab_w_in", "a_conv_w", "b_mix_w", "b_scale", "ab_w_out", "cd_w_in",
            "c_q_norm_g", "c_w_uq", "c_kv_norm_g", "c_w_ukv", "d_ln_g", "d_ln_b", "d_w_s", "d_b_s", "cd_w_out",
            "ffn_w_up", "ffn_conv_w", "ffn_w_down", "final_norm_g")
_INPUTS = ("x", "c", "positions") + _WEIGHTS + ("loss_target",) + tuple("m_" + n for n in _WEIGHTS) + tuple(
    "v_" + n for n in _WEIGHTS)

def _pack_rows(parts, rows, dtype):
    flat = jnp.concatenate([p.reshape(-1).astype(dtype) for p in parts])
    return jnp.pad(flat, (0, rows * LANES - flat.shape[0])).reshape(rows, LANES)


def _rows_major(w):
    r, c = w.shape
    return w.reshape(N_CHIPS, r // N_CHIPS, c)


def start_gather(shards, tag, after=()):
    lands = [_sds((N_CHIPS,) + s.shape, s.dtype) for s in shards]
    return split_start("gather_start_" + tag, GATHER, shards, lands, after)


def finish_gather(handle, chip, tag, after):
    shards, lands = split_wait("gather_wait_" + tag, GATHER, handle, after)
    lands = forward_halves(lands, "gather_forward_" + tag)
    return [lax.dynamic_update_index_in_dim(o, s, chip, 0) for o, s in zip(lands, shards)]


def start_reduce(gs, core, tag):
    recv = swap_halves(gs, "swap_halves_" + tag)
    pairs = pair_sums(gs, recv, core, "pair_sums_" + tag)
    lands = [_sds((N_CHIPS - 1,) + p.shape[1:], p.dtype) for p in pairs]
    return split_start("exchange_start_" + tag, EXCHANGE, pairs, lands)


def finish_reduce(handle, chip, core, tag, after):
    pairs, others = split_wait("exchange_wait_" + tag, EXCHANGE, handle, after)
    halves = chip_sums(pairs, others, chip, core, "chip_sums_" + tag)
    full = join_halves(halves, "join_halves_" + tag)
    return [f.reshape(f.shape[1] * 2, f.shape[2]) for f in full]


_SMALL_SHARDED = (("a_conv_w", (3, 128), 1), ("c_q_norm_g", (1, 64), 1), ("d_ln_g", (1, 128), 1), ("d_ln_b", (1, 128), 1),
                  ("ffn_conv_w", (2, 3, 2 * D_FF // N_CHIPS), 2))
_SMALL_GRADS = (("norm1_g", (2, D_MODEL)), ("norm2_g", (2, D_MODEL)), ("b_mix_w", (4, 128, 128)), ("b_scale", (1, 512)),
                ("c_kv_norm_g", (1, 128)), ("d_w_s", (4, 128, 128)), ("d_b_s", (4, 128)), ("final_norm_g", (1, D_MODEL)),
                ("a_conv_w", (3, 512)), ("c_q_norm_g", (1, 256)), ("d_ln_g", (1, 512)), ("d_ln_b", (1, 512)),
                ("ffn_conv_w", (2, 3, 2 * D_FF)))


def _size(shape):
    n = 1
    for d in shape:
        n *= d
    return n


def kernel(x, c, positions, ada_w, ada_b, norm1_g, norm2_g, ab_w_in, a_conv_w, b_mix_w, b_scale, ab_w_out, cd_w_in, c_q_norm_g, c_w_uq, c_kv_norm_g, c_w_ukv, d_ln_g, d_ln_b, d_w_s, d_b_s, cd_w_out, ffn_w_up, ffn_conv_w, ffn_w_down, final_norm_g, loss_target, m_ada_w, m_ada_b, m_norm1_g, m_norm2_g, m_ab_w_in, m_a_conv_w, m_b_mix_w, m_b_scale, m_ab_w_out, m_cd_w_in, m_c_q_norm_g, m_c_w_uq, m_c_kv_norm_g, m_c_w_ukv, m_d_ln_g, m_d_ln_b, m_d_w_s, m_d_b_s, m_cd_w_out, m_ffn_w_up, m_ffn_conv_w, m_ffn_w_down, m_final_norm_g, v_ada_w, v_ada_b, v_norm1_g, v_norm2_g, v_ab_w_in, v_a_conv_w, v_b_mix_w, v_b_scale, v_ab_w_out, v_cd_w_in, v_c_q_norm_g, v_c_w_uq, v_c_kv_norm_g, v_c_w_ukv, v_d_ln_g, v_d_ln_b, v_d_w_s, v_d_b_s, v_cd_w_out, v_ffn_w_up, v_ffn_conv_w, v_ffn_w_down, v_final_norm_g):
    args = (x, c, positions, ada_w, ada_b, norm1_g, norm2_g, ab_w_in, a_conv_w, b_mix_w, b_scale, ab_w_out, cd_w_in, c_q_norm_g, c_w_uq, c_kv_norm_g, c_w_ukv, d_ln_g, d_ln_b, d_w_s, d_b_s, cd_w_out, ffn_w_up, ffn_conv_w, ffn_w_down, final_norm_g, loss_target, m_ada_w, m_ada_b, m_norm1_g, m_norm2_g, m_ab_w_in, m_a_conv_w, m_b_mix_w, m_b_scale, m_ab_w_out, m_cd_w_in, m_c_q_norm_g, m_c_w_uq, m_c_kv_norm_g, m_c_w_ukv, m_d_ln_g, m_d_ln_b, m_d_w_s, m_d_b_s, m_cd_w_out, m_ffn_w_up, m_ffn_conv_w, m_ffn_w_down, m_final_norm_g, v_ada_w, v_ada_b, v_norm1_g, v_norm2_g, v_ab_w_in, v_a_conv_w, v_b_mix_w, v_b_scale, v_ab_w_out, v_cd_w_in, v_c_q_norm_g, v_c_w_uq, v_c_kv_norm_g, v_c_w_ukv, v_d_ln_g, v_d_ln_b, v_d_w_s, v_d_b_s, v_cd_w_out, v_ffn_w_up, v_ffn_conv_w, v_ffn_w_down, v_final_norm_g)
    a = dict(zip(_INPUTS, args, strict=True))
    xi, yi, ci = _place()
    chip = 2 * xi + yi
    dev = 4 * xi + 2 * yi + ci
    x = a["x"][0]
    tgt = a["loss_target"][0]

    bf = lambda t: t.astype(BF16)
    mix0_handle, tok = start_gather([bf(a["ab_w_in"][0]), bf(a["ab_w_out"][0])], "mix0")
    up0_16, down0_16, up1_16, down1_16 = [bf(a[n][l]) for l in (0, 1) for n in ("ffn_w_up", "ffn_w_down")]
    mix1_16 = [bf(a[n][0]) for n in ("cd_w_in", "c_w_uq", "c_w_ukv", "cd_w_out")]

    small_parts = [a["c"] + tok] + [a[n] for n, _, _ in _SMALL_SHARDED]
    rows1 = -(-sum(p.size for p in small_parts) // LANES // 8) * 8
    g1 = all_gather8(_pack_rows(small_parts, rows1, F32), "gather_small",
                     [up0_16, down0_16, up1_16, down1_16, mix1_16[0], mix1_16[3]]).reshape(N_DEV, rows1 * LANES)
    c_all = g1[:, :D_MODEL]
    per_chip = g1[0::2]
    small_full = {}
    off = D_MODEL
    for n, shp, axis in _SMALL_SHARDED:
        piece = per_chip[:, off:off + _size(shp)].reshape((N_CHIPS,) + shp)
        small_full[n] = jnp.concatenate([piece[k] for k in range(N_CHIPS)], axis=axis)
        off += _size(shp)

    merge = lambda t: t.reshape(t.shape[0] * t.shape[1], t.shape[2])
    w = dict(norm1_g=a["norm1_g"], norm2_g=a["norm2_g"], b_mix_w=a["b_mix_w"][0], b_scale=a["b_scale"],
             c_kv_norm_g=a["c_kv_norm_g"], d_w_s=a["d_w_s"][0], d_b_s=a["d_b_s"][0],
             final_norm_g=a["final_norm_g"].reshape(1, D_MODEL), **small_full)

    ncol = N_MOD * D_MODEL // N_CHIPS
    ada_b_mine = lax.dynamic_slice_in_dim(a["ada_b"], chip * ncol, ncol, axis=1)
    mod_cols = ada_mod(c_all, a["ada_w"], ada_b_mine)
    g2_rows = all_gather8(mod_cols.reshape(-1, LANES), "gather_mod")
    g2 = g2_rows.reshape(N_DEV, 2, N_DEV, ncol)
    mod = lax.dynamic_index_in_dim(g2[0::2], dev, axis=2, keepdims=False)
    mod = mod.transpose(1, 0, 2).reshape(2, N_MOD * D_MODEL)

    late = [g2_rows]
    up0_handle, tok_a = start_gather([up0_16], "up0", late)
    down0_handle, tok_b = start_gather([down0_16], "down0", late)
    mix1_handle, tok_c = start_gather(mix1_16, "mix1", late)
    ffn1_handle, tok_d = start_gather([up1_16, down1_16], "ffn1", late)
    mod = mod + (tok_a + tok_b + tok_c + tok_d)

    ropes = rope_tables(a["positions"][0])
    cm16 = lambda t: chip_major(t).astype(BF16)
    w.update(ffn_w_up=[None, None], ffn_w_down=[None, None])
    handles = dict(mix0=mix0_handle, up0=up0_handle, down0=down0_handle, mix1=mix1_handle, ffn1=ffn1_handle)
    reducing, reduced = {}, {}

    class Hooks(StepHooks):
        def weights(self, stage, after):
            got = finish_gather(handles[stage], chip, stage, after)
            if stage == "mix0":
                w.update(ab_w_in=got[0], ab_w_out=merge(got[1]))
            elif stage == "up0":
                w["ffn_w_up"][0] = got[0]
            elif stage == "down0":
                w["ffn_w_down"][0] = merge(got[0])
            elif stage == "mix1":
                cd_in, uq, ukv, cd_out = got
                w.update(prepare_weights(dict(cd_w_in=from_chip_major(cd_in), c_w_uq=from_chip_major(uq),
                                              c_w_ukv=from_chip_major(ukv), cd_w_out=merge(cd_out))))
            else:
                w["ffn_w_up"][1], w["ffn_w_down"][1] = got[0], merge(got[1])

        def gradients(self, stage, grads, after):
            if stage in ("ffn0", "ffn1"):
                parts = [grads["ffn_w_up"], _rows_major(grads["ffn_w_down"])]
            elif stage == "mix1":
                grads.update(unprepare_grads(grads))
                parts = [cm16(grads["cd_w_in"]), cm16(grads["c_w_uq"]), cm16(grads["c_w_ukv"]),
                         _rows_major(grads["cd_w_out"]).astype(BF16)]
            else:
                parts = [grads["ab_w_in"], _rows_major(grads["ab_w_out"])]
            reducing[stage], tok = start_reduce(parts, ci, stage)
            before = {"mix1": "ffn1", "ffn0": "mix1", "mix0": "ffn0"}.get(stage)
            if before is not None:
                reduced[before] = finish_reduce(reducing[before], chip, ci, before, after)
            return tok

    loss, grad_x, dmod, by_stage = run_step(x, tgt, mod, ropes, w, Hooks())
    grads = merge_grads(by_stage)

    parts3 = [dmod] + [grads[n] for n, _ in _SMALL_GRADS] + [loss[0, 0]]
    rows3 = -(-sum(p.size for p in parts3) // LANES // 8) * 8
    small_handle, _ = split_start("small_grads_start", EVERYONE, [_pack_rows(parts3, rows3, F32)],
                                  [_sds((N_DEV, rows3, LANES))])
    red_up1, red_down1 = reduced["ffn1"]
    red_cd_in, red_uq, red_ukv, red_cd_out = reduced["mix1"]
    red_up0, red_down0 = reduced["ffn0"]
    out_grads = dict(cd_w_in=red_cd_in, c_w_uq=red_uq, c_w_ukv=red_ukv, cd_w_out=red_cd_out)
    per_layer = dict(ffn_w_up=(red_up0, red_up1), ffn_w_down=(red_down0, red_down1))
    updates = {}

    def update(n):
        if n in per_layer:
            updates[n] = adamw_layers(a[n], *per_layer[n], a["m_" + n], a["v_" + n], "adamw_" + n)
        else:
            updates[n] = adamw(a[n], out_grads[n].reshape(a[n].shape), a["m_" + n], a["v_" + n], "adamw_" + n)

    early =("ffn_w_up", "ffn_w_down", "cd_w_in", "c_w_uq", "c_w_ukv", "cd_w_out")
    for n in early:
        update(n)
    (mine,), (landed,) = split_wait("small_grads_wait", EVERYONE, small_handle, [updates[n][1] for n in early])
    g3 = lax.dynamic_update_index_in_dim(landed, mine, dev, 0)
    summed = sum8(g3).reshape(-1)
    nmod = 2 * N_MOD * D_MODEL
    out_grads["ada_b"] = summed[:nmod].reshape(2, N_MOD * D_MODEL)
    off = nmod
    for n, shp in _SMALL_GRADS:
        out_grads[n] = summed[off:off + _size(shp)].reshape(shp)
        off += _size(shp)
    loss = summed[off]
    for n, shp, axis in _SMALL_SHARDED:
        width = out_grads[n].shape[-1] // N_CHIPS
        out_grads[n] = lax.dynamic_slice_in_dim(out_grads[n], chip * width, width, axis=out_grads[n].ndim - 1)
    dmod_all = g3.reshape(N_DEV, rows3 * LANES)[:, :nmod].reshape(N_DEV, 2, N_MOD * D_MODEL)
    dmod_mine = lax.dynamic_slice_in_dim(dmod_all, chip * ncol, ncol, axis=2).transpose(1, 0, 2)
    updates["ada_w"] = adamw_ada(a["ada_w"], c_all, dmod_mine, a["m_ada_w"], a["v_ada_w"])

    red_in0, red_out0 = finish_reduce(reducing["mix0"], chip, ci, "mix0", updates["ada_w"][1])
    out_grads.update(ab_w_in=red_in0, ab_w_out=red_out0)

    for n in ("ab_w_in", "ab_w_out"):
        update(n)
    small = [n for n in _WEIGHTS if n not in updates]
    for n, res in zip(small, adamw_small([a[n] for n in small], [out_grads[n].reshape(a[n].shape) for n in small],
                                         [a["m_" + n] for n in small], [a["v_" + n] for n in small])):
        updates[n] = res
    return (loss, grad_x[None], *[updates[n][i] for i in range(4) for n in _WEIGHTS])
```

```python
import functools

import jax
import jax.numpy as jnp
from jax import lax
from jax.experimental import pallas as pl
from jax.experimental.pallas import tpu as pltpu

F32 = jnp.float32
BF16 = jnp.bfloat16
EPS = 1e-6
D_MODEL = 1024
N_MOD = 6
A_WIDTH = 512
B_GROUPS = 4
POOL_WINDOWS = (2, 4, 8, 16)
C_HEADS = 8
C_NOPE = 64
C_ROPE = 32
C_V = 64
C_Q_RANK = 256
C_KV_RANK = 128
HEAD_PAD = 128
ROPE_THETA = 10000.0
D_GROUPS = 4
D_CHUNK = 128
D_FF = 2816
FF_UNIT = 128
ADAM_LR = 0.001
ADAM_B1 = 0.9
ADAM_B2 = 0.999
ADAM_EPS = 1e-08
ADAM_WD = 0.01
ADAM_STEP = 10
N_CHIPS = 4
N_DEV = 8
LANES = 128
VMEM_BIG = 56 * 1024 * 1024
MESH = pl.DeviceIdType.MESH


def _sds(shape, dtype=F32):
    return jax.ShapeDtypeStruct(tuple(shape), dtype)


def _tile(n, cap, mult=128):
    if n <= cap:
        return n
    best = None
    for t in range(mult, cap + 1, mult):
        if n % t == 0:
            best = t
    assert best is not None, (n, cap, mult)
    return best


def _params(dims=None, vmem=None):
    return pltpu.CompilerParams(dimension_semantics=dims, vmem_limit_bytes=vmem)


def _shift_down(v, k):
    r = pltpu.roll(v, k, axis=0)
    t = lax.broadcasted_iota(jnp.int32, v.shape, 0)
    return jnp.where(t >= k, r, 0.0)


def _shift_up(v, k):
    n = v.shape[0]
    r = pltpu.roll(v, n - k, axis=0)
    t = lax.broadcasted_iota(jnp.int32, v.shape, 0)
    return jnp.where(t < n - k, r, 0.0)


def _sigmoid(v):
    return 1.0 / (1.0 + jnp.exp(-v))


_GELU_C = 0.7978845608028654
_GELU_A = 0.044715


def _gelu(v):
    return 0.5 * v * (1.0 + jnp.tanh(_GELU_C * (v + _GELU_A * v * v * v)))


def _gelu_grad(v):
    th = jnp.tanh(_GELU_C * (v + _GELU_A * v * v * v))
    return 0.5 * (1.0 + th) + 0.5 * v * (1.0 - th * th) * _GELU_C * (1.0 + 3.0 * _GELU_A * v * v)


_NN = (((1,), (0,)), ((), ()))
_NT = (((1,), (1,)), ((), ()))
_TN = (((0,), (0,)), ((), ()))


def _dot(a, b, dims=_NN):
    return lax.dot_general(a, b, dims, preferred_element_type=F32)


def _logical(t, groups):
    return (t.shape[-2], t.shape[-1] * groups)


def _block(tr, tc, groups, cols, where):
    if groups == 1:
        return pl.BlockSpec((tr, tc), where)
    per = cols // groups // tc

    def index(i, j, s):
        r, c = where(i, j, s)
        return (c // per, r, c % per)

    return pl.BlockSpec((None, tr, tc), index)


def matmul(a, b, mode, out_dtype, name, ga=1, gb=1, go=1, tm=None, tn=None, tk=None):
    (ar, ac), (br, bc) = _logical(a, ga), _logical(b, gb)
    if mode == "nn":
        m, k, n = ar, ac, bc
        a_col, b_col = "k", "n"
    elif mode == "nt":
        m, k, n = ar, ac, br
        a_col, b_col = "k", "k"
    else:
        k, m, n = ar, ac, bc
        a_col, b_col = "m", "n"
    limit = {"m": m, "n": n // go, "k": k}
    limit[a_col] = min(limit[a_col], ac // ga)
    limit[b_col] = min(limit[b_col], bc // gb)
    tm = tm or _tile(limit["m"], 1024, 128 if mode == "tn" else 16)
    tn = tn or _tile(limit["n"], 512)
    tk = tk or _tile(limit["k"], 2048, 16 if mode == "tn" else 128)
    nk = k // tk
    if mode == "nn":
        a_spec = _block(tm, tk, ga, ac, lambda i, j, s: (i, s))
        b_spec = _block(tk, tn, gb, bc, lambda i, j, s: (s, j))
        dims = _NN
    elif mode == "nt":
        a_spec = _block(tm, tk, ga, ac, lambda i, j, s: (i, s))
        b_spec = _block(tn, tk, gb, bc, lambda i, j, s: (j, s))
        dims = _NT
    else:
        a_spec = _block(tk, tm, ga, ac, lambda i, j, s: (s, i))
        b_spec = _block(tk, tn, gb, bc, lambda i, j, s: (s, j))
        dims = _TN
    o_spec = _block(tm, tn, go, n, lambda i, j, s: (i, j))
    out_shape = _sds((m, n), out_dtype) if go == 1 else _sds((go, m, n // go), out_dtype)

    def body(a_ref, b_ref, o_ref, acc_ref):
        s = pl.program_id(2)

        @pl.when(s == 0)
        def _():
            acc_ref[...] = jnp.zeros_like(acc_ref)

        acc_ref[...] += _dot(a_ref[...], b_ref[...], dims)

        @pl.when(s == nk - 1)
        def _():
            o_ref[...] = acc_ref[...].astype(o_ref.dtype)

    return pl.pallas_call(
        body, name=name, out_shape=out_shape, grid=(m // tm, n // tn, nk),
        in_specs=[a_spec, b_spec], out_specs=o_spec,
        scratch_shapes=[pltpu.VMEM((tm, tn), F32)],
        compiler_params=_params(("parallel", "parallel", "arbitrary"), VMEM_BIG),
    )(a, b)


def _rows(tm, n):
    return pl.BlockSpec((tm, n), lambda i: (i, 0))


def _vec(n):
    return pl.BlockSpec((1, n), lambda i: (0, 0))


def modnorm_fwd(x, g, sc, sh, name):
    s, d = x.shape
    tm = _tile(s, 256, 8)

    def body(x_ref, g_ref, sc_ref, sh_ref, o_ref):
        xv = x_ref[...]
        r = lax.rsqrt(jnp.mean(xv * xv, axis=-1, keepdims=True) + EPS)
        o_ref[...] = ((xv * r) * g_ref[...] * (1.0 + sc_ref[...]) + sh_ref[...]).astype(BF16)

    return pl.pallas_call(
        body, name=name, out_shape=_sds((s, d), BF16), grid=(s // tm,),
        in_specs=[_rows(tm, d), _vec(d), _vec(d), _vec(d)], out_specs=_rows(tm, d),
        compiler_params=_params(("parallel",)),
    )(x, g, sc, sh)


def norm_bwd(x, dh, g, sc, dres, name):
    s, d = x.shape
    tm = _tile(s, 256, 8)
    nsteps = s // tm

    def body(x_ref, dh_ref, g_ref, sc_ref, dr_ref, dx_ref, dsh_ref, dsc_ref, dg_ref, a2_ref):
        i = pl.program_id(0)

        @pl.when(i == 0)
        def _():
            dsh_ref[...] = jnp.zeros_like(dsh_ref)
            a2_ref[...] = jnp.zeros_like(a2_ref)

        xv = x_ref[...]
        dh = dh_ref[...]
        r = lax.rsqrt(jnp.mean(xv * xv, axis=-1, keepdims=True) + EPS)
        xh = xv * r
        dsh_ref[...] += jnp.sum(dh, axis=0, keepdims=True)
        a2_ref[...] += jnp.sum(dh * xh, axis=0, keepdims=True)
        dxh = dh * (g_ref[...] * (1.0 + sc_ref[...]))
        dx = r * (dxh - xh * jnp.mean(dxh * xh, axis=-1, keepdims=True))
        dx_ref[...] = dr_ref[...] + dx

        @pl.when(i == nsteps - 1)
        def _():
            dsc_ref[...] = a2_ref[...] * g_ref[...]
            dg_ref[...] = a2_ref[...] * (1.0 + sc_ref[...])

    return pl.pallas_call(
        body, name=name, out_shape=(_sds((s, d)), _sds((1, d)), _sds((1, d)), _sds((1, d))), grid=(nsteps,),
        in_specs=[_rows(tm, d), _rows(tm, d), _vec(d), _vec(d), _rows(tm, d)],
        out_specs=(_rows(tm, d), _vec(d), _vec(d), _vec(d)),
        scratch_shapes=[pltpu.VMEM((1, d), F32)],
        compiler_params=_params(("arbitrary",)),
    )(x, dh, g, sc, dres)


def resid_modnorm_fwd(x, y, gate, g, sc, sh, name):
    s, d = x.shape
    tm = _tile(s, 256, 8)

    def body(x_ref, y_ref, gate_ref, g_ref, sc_ref, sh_ref, xo_ref, h_ref):
        xv = x_ref[...] + gate_ref[...] * y_ref[...].astype(F32)
        xo_ref[...] = xv
        r = lax.rsqrt(jnp.mean(xv * xv, axis=-1, keepdims=True) + EPS)
        h_ref[...] = ((xv * r) * g_ref[...] * (1.0 + sc_ref[...]) + sh_ref[...]).astype(BF16)

    return pl.pallas_call(
        body, name=name, out_shape=(_sds((s, d)), _sds((s, d), BF16)), grid=(s // tm,),
        in_specs=[_rows(tm, d), _rows(tm, d), _vec(d), _vec(d), _vec(d), _vec(d)], out_specs=(_rows(tm, d), _rows(tm, d)),
        compiler_params=_params(("parallel",)),
    )(x, y, gate, g, sc, sh)


def norm_gate_bwd(x, dh, g, sc, dres, y, gate, name):
    s, d = x.shape
    tm = _tile(s, 256, 8)
    nsteps = s // tm

    def body(x_ref, dh_ref, g_ref, sc_ref, dr_ref, y_ref, gate_ref, dx_ref, dsh_ref, dsc_ref, dg_ref, dy_ref,
             dgate_ref, a2_ref):
        i = pl.program_id(0)

        @pl.when(i == 0)
        def _():
            dsh_ref[...] = jnp.zeros_like(dsh_ref)
            a2_ref[...] = jnp.zeros_like(a2_ref)
            dgate_ref[...] = jnp.zeros_like(dgate_ref)

        xv = x_ref[...]
        dh = dh_ref[...]
        r = lax.rsqrt(jnp.mean(xv * xv, axis=-1, keepdims=True) + EPS)
        xh = xv * r
        dsh_ref[...] += jnp.sum(dh, axis=0, keepdims=True)
        a2_ref[...] += jnp.sum(dh * xh, axis=0, keepdims=True)
        dxh = dh * (g_ref[...] * (1.0 + sc_ref[...]))
        dr = dr_ref[...] + r * (dxh - xh * jnp.mean(dxh * xh, axis=-1, keepdims=True))
        dx_ref[...] = dr
        dy_ref[...] = (dr * gate_ref[...]).astype(BF16)
        dgate_ref[...] += jnp.sum(dr * y_ref[...].astype(F32), axis=0, keepdims=True)

        @pl.when(i == nsteps - 1)
        def _():
            dsc_ref[...] = a2_ref[...] * g_ref[...]
            dg_ref[...] = a2_ref[...] * (1.0 + sc_ref[...])

    vec = _sds((1, d))
    return pl.pallas_call(
        body, name=name, out_shape=(_sds((s, d)), vec, vec, vec, _sds((s, d), BF16), vec), grid=(nsteps,),
        in_specs=[_rows(tm, d), _rows(tm, d), _vec(d), _vec(d), _rows(tm, d), _rows(tm, d), _vec(d)],
        out_specs=(_rows(tm, d), _vec(d), _vec(d), _vec(d), _rows(tm, d), _vec(d)),
        scratch_shapes=[pltpu.VMEM((1, d), F32)],
        compiler_params=_params(("arbitrary",)),
    )(x, dh, g, sc, dres, y, gate)


def final_fused(x, f, gate, g, tgt):
    s, d = x.shape
    tm = _tile(s, 256, 8)

    def body(x_ref, f_ref, gate_ref, g_ref, t_ref, dx_ref, dg_ref, loss_ref, df_ref, dgate_ref):
        @pl.when(pl.program_id(0) == 0)
        def _():
            dg_ref[...] = jnp.zeros_like(dg_ref)
            loss_ref[...] = jnp.zeros_like(loss_ref)
            dgate_ref[...] = jnp.zeros_like(dgate_ref)

        fv, gatev, gv = f_ref[...].astype(F32), gate_ref[...], g_ref[...]
        xv = x_ref[...] + gatev * fv
        r = lax.rsqrt(jnp.mean(xv * xv, axis=-1, keepdims=True) + EPS)
        xh = xv * r
        e = xh * gv - t_ref[...]
        row = jnp.sum(e * e, axis=-1, keepdims=True) * (0.5 / d)
        loss_ref[...] += jnp.sum(row, axis=0, keepdims=True)
        dy = e * (1.0 / d)
        dg_ref[...] += jnp.sum(dy * xh, axis=0, keepdims=True)
        dxh = dy * gv
        dx = r * (dxh - xh * jnp.mean(dxh * xh, axis=-1, keepdims=True))
        dx_ref[...] = dx
        df_ref[...] = (dx * gatev).astype(BF16)
        dgate_ref[...] += jnp.sum(dx * fv, axis=0, keepdims=True)

    vec = _sds((1, d))
    return pl.pallas_call(
        body, name="final_fused", out_shape=(_sds((s, d)), vec, _sds((1, LANES)), _sds((s, d), BF16), vec),
        grid=(s // tm,),
        in_specs=[_rows(tm, d), _rows(tm, d), _vec(d), _vec(d), _rows(tm, d)],
        out_specs=(_rows(tm, d), _vec(d), _vec(LANES), _rows(tm, d), _vec(d)),
        compiler_params=_params(("arbitrary",)),
    )(x, f, gate, g, tgt)


def _taps(v):
    return _shift_down(v, 2), _shift_down(v, 1), v


def _conv3_taps(taps, w):
    return w[0:1, :] * taps[0] + w[1:2, :] * taps[1] + w[2:3, :] * taps[2]


def _conv3(v, w):
    return _conv3_taps(_taps(v), w)


def _conv3_t(dv, w):
    return w[0:1, :] * _shift_up(dv, 2) + w[1:2, :] * _shift_up(dv, 1) + w[2:3, :] * dv


def _conv3_dw_taps(dv, taps):
    return jnp.concatenate([jnp.sum(dv * t, axis=0, keepdims=True) for t in taps], axis=0)


def _conv3_dw(dv, v):
    return _conv3_dw_taps(dv, _taps(v))


def gconv_fwd(z, conv_w):
    s = z.shape[0]
    nb = A_WIDTH // LANES

    def body(b_ref, c_ref, a_ref, w_ref, o_ref):
        b, c, a = b_ref[...].astype(F32), c_ref[...].astype(F32), a_ref[...].astype(F32)
        o_ref[...] = (b * _conv3(c * a, w_ref[...])).astype(BF16)

    col = lambda off: pl.BlockSpec((s, LANES), lambda j: (0, off + j))
    return pl.pallas_call(
        body, name="gconv_fwd", out_shape=_sds((s, A_WIDTH), BF16), grid=(nb,),
        in_specs=[col(0), col(nb), col(2 * nb), pl.BlockSpec((3, LANES), lambda j: (0, j))],
        out_specs=pl.BlockSpec((s, LANES), lambda j: (0, j)),
        compiler_params=_params(("parallel",), VMEM_BIG),
    )(z, z, z, conv_w)


def gconv_bwd(z, conv_w, dycat):
    s = z.shape[0]
    nb = A_WIDTH // LANES

    def body(b_ref, c_ref, a_ref, w_ref, dy_ref, db_ref, dc_ref, da_ref, dw_ref):
        c, a, w, dy = c_ref[...].astype(F32), a_ref[...].astype(F32), w_ref[...], dy_ref[...].astype(F32)
        ca = c * a
        db_ref[...] = (dy * _conv3(ca, w)).astype(BF16)
        dconv = dy * b_ref[...].astype(F32)
        dw_ref[...] = _conv3_dw(dconv, ca)
        dca = _conv3_t(dconv, w)
        dc_ref[...] = (dca * a).astype(BF16)
        da_ref[...] = (dca * c).astype(BF16)

    col = lambda off: pl.BlockSpec((s, LANES), lambda j: (0, off + j))
    wspec = pl.BlockSpec((3, LANES), lambda j: (0, j))
    part = _sds((s, A_WIDTH), BF16)
    return pl.pallas_call(
        body, name="gconv_bwd", out_shape=(part, part, part, _sds((3, A_WIDTH))), grid=(nb,),
        in_specs=[col(0), col(nb), col(2 * nb), wspec, col(0)],
        out_specs=(col(0), col(0), col(0), wspec),
        compiler_params=_params(("parallel",), VMEM_BIG),
    )(z, z, z, conv_w, dycat)


def _pool_counts(s, w):
    t = lax.broadcasted_iota(jnp.int32, (s, 1), 0)
    return jnp.minimum(t + 1, w).astype(F32)


def _pooled(p, levels):
    acc = p
    for lv in range(levels):
        acc = acc + _shift_down(acc, 2 ** lv)
    return acc / _pool_counts(p.shape[0], 2 ** levels) - p


_B_WIDTH = B_GROUPS * LANES


def pool_fwd(z, mix_w, scale):
    s = z.shape[0]

    def body(p_ref, m_ref, sc_ref, o_ref):
        for g in range(B_GROUPS):
            cols = slice(g * LANES, (g + 1) * LANES)
            pooled = _pooled(p_ref[:, cols].astype(F32), g + 1)
            y = _dot(pooled.astype(BF16), m_ref[g].astype(BF16))
            o_ref[:, cols] = (y * sc_ref[:, cols]).astype(BF16)

    return pl.pallas_call(
        body, name="pool_fwd", out_shape=_sds((s, _B_WIDTH), BF16), grid=(1,),
        in_specs=[pl.BlockSpec((s, _B_WIDTH), lambda i: (0, 3 * A_WIDTH // _B_WIDTH)),
                  pl.BlockSpec((B_GROUPS, LANES, LANES), lambda i: (0, 0, 0)), pl.BlockSpec((1, _B_WIDTH), lambda i: (0, 0))],
        out_specs=pl.BlockSpec((s, _B_WIDTH), lambda i: (0, 0)),
        compiler_params=_params(("arbitrary",), VMEM_BIG),
    )(z, mix_w, scale)


def pool_bwd(z, mix_w, scale, dycat):
    s = z.shape[0]

    def body(p_ref, m_ref, sc_ref, dy_ref, dp_ref, dm_ref, dsc_ref):
        for g in range(B_GROUPS):
            cols = slice(g * LANES, (g + 1) * LANES)
            pooled = _pooled(p_ref[:, cols].astype(F32), g + 1)
            mw = m_ref[g].astype(BF16)
            pb = pooled.astype(BF16)
            dy = dy_ref[:, cols].astype(F32)
            dsc_ref[:, cols] = jnp.sum(dy * _dot(pb, mw), axis=0, keepdims=True)
            dmix = (dy * sc_ref[:, cols]).astype(BF16)
            dm_ref[g] = _dot(pb, dmix, _TN)
            dpool = _dot(dmix, mw, _NT)
            acc = dpool / _pool_counts(s, 2 ** (g + 1))
            for lv in range(g + 1):
                acc = acc + _shift_up(acc, 2 ** lv)
            dp_ref[:, cols] = (acc - dpool).astype(BF16)

    wide = lambda c: pl.BlockSpec((s, _B_WIDTH), lambda i: (0, c))
    mspec = pl.BlockSpec((B_GROUPS, LANES, LANES), lambda i: (0, 0, 0))
    vspec = pl.BlockSpec((1, _B_WIDTH), lambda i: (0, 0))
    return pl.pallas_call(
        body, name="pool_bwd", out_shape=(_sds((s, _B_WIDTH), BF16), _sds((B_GROUPS, LANES, LANES)), _sds((1, _B_WIDTH))),
        grid=(1,), in_specs=[wide(3 * A_WIDTH // _B_WIDTH), mspec, vspec, wide(A_WIDTH // _B_WIDTH)],
        out_specs=(wide(0), mspec, vspec),
        compiler_params=_params(("arbitrary",), VMEM_BIG),
    )(z, mix_w, scale, dycat)


_FF_BLOCKS = D_FF // FF_UNIT


def _ff_spec(s):
    return pl.BlockSpec((2, s, FF_UNIT), lambda j: (0, 0, j))


def _ff_wspecs():
    return [pl.BlockSpec((3, FF_UNIT), lambda j: (0, j)), pl.BlockSpec((3, FF_UNIT), lambda j: (0, _FF_BLOCKS + j))]


_FF_ROWS = 64
_FF_HALO = 16


def _chunk_taps(z_ref, half, c):
    start = pl.multiple_of(c * _FF_ROWS, _FF_ROWS)
    before = pl.multiple_of(jnp.maximum(c * _FF_ROWS - _FF_HALO, 0), _FF_HALO)
    halo = z_ref[half, pl.ds(before, _FF_HALO), :].astype(F32)
    halo = jnp.where(c > 0, halo, 0.0)
    win = jnp.concatenate([halo, z_ref[half, pl.ds(start, _FF_ROWS), :].astype(F32)], axis=0)
    return tuple(pltpu.roll(win, k, axis=0)[_FF_HALO:] for k in (2, 1)) + (win[_FF_HALO:],)


def _fold8(v):
    acc = v[0:8]
    for r in range(8, v.shape[0], 8):
        acc = acc + v[r:r + 8]
    return acc


def ffn_act_fwd(zf, conv_w, name):
    s = zf.shape[1]
    assert s % _FF_ROWS == 0

    def body(z_ref, wg_ref, wu_ref, o_ref):
        g = _conv3(z_ref[0].astype(F32), wg_ref[...])
        u = _conv3(z_ref[1].astype(F32), wu_ref[...])
        o_ref[...] = (g * _sigmoid(g) * u).astype(BF16)

    return pl.pallas_call(
        body, name=name, out_shape=_sds((s, D_FF), BF16), grid=(_FF_BLOCKS,),
        in_specs=[_ff_spec(s)] + _ff_wspecs(), out_specs=pl.BlockSpec((s, FF_UNIT), lambda j: (0, j)),
        compiler_params=_params(("parallel",), VMEM_BIG),
    )(zf, conv_w, conv_w)


def ffn_act_bwd(zf, conv_w, da, name):
    s = zf.shape[1]
    assert s % _FF_ROWS == 0
    nchunks = s // _FF_ROWS

    def body(z_ref, wg_ref, wu_ref, da_ref, dz_ref, dw_ref, dg_ref, du_ref):
        wg, wu = wg_ref[...], wu_ref[...]

        def first(c, acc):
            rows = pl.ds(pl.multiple_of(c * _FF_ROWS, _FF_ROWS), _FF_ROWS)
            tg, tu = _chunk_taps(z_ref, 0, c), _chunk_taps(z_ref, 1, c)
            g = _conv3_taps(tg, wg)
            u = _conv3_taps(tu, wu)
            dav = da_ref[rows, :].astype(F32)
            sg = _sigmoid(g)
            dg = dav * u * (sg * (1.0 + g * (1.0 - sg)))
            du = dav * (g * sg)
            dg_ref[rows, :] = dg
            du_ref[rows, :] = du
            return tuple(a + _fold8(d * t) for a, (d, t) in zip(acc, [(dg, t) for t in tg] + [(du, t) for t in tu]))

        zero = jnp.zeros((8, FF_UNIT), F32)
        acc = lax.fori_loop(0, nchunks, first, (zero,) * 6)
        sums = [jnp.sum(a, axis=0, keepdims=True) for a in acc]
        dw_ref[0] = jnp.concatenate(sums[:3], axis=0)
        dw_ref[1] = jnp.concatenate(sums[3:], axis=0)

        tail = pl.ds(s, _FF_HALO)
        dg_ref[tail, :] = jnp.zeros((_FF_HALO, FF_UNIT), F32)
        du_ref[tail, :] = jnp.zeros((_FF_HALO, FF_UNIT), F32)
        span = _FF_ROWS + _FF_HALO

        def second(c, carry):
            start = pl.multiple_of(c * _FF_ROWS, _FF_ROWS)
            for half, (d_ref, w) in enumerate(((dg_ref, wg), (du_ref, wu))):
                win = d_ref[pl.ds(start, span), :]
                dz = (w[0:1, :] * pltpu.roll(win, span - 2, axis=0)[:_FF_ROWS]
                      + w[1:2, :] * pltpu.roll(win, span - 1, axis=0)[:_FF_ROWS] + w[2:3, :] * win[:_FF_ROWS])
                dz_ref[half, pl.ds(start, _FF_ROWS), :] = dz.astype(BF16)
            return carry

        lax.fori_loop(0, nchunks, second, 0)

    return pl.pallas_call(
        body, name=name, out_shape=(_sds((2, s, D_FF), BF16), _sds((2, 3, D_FF))), grid=(_FF_BLOCKS,),
        in_specs=[_ff_spec(s)] + _ff_wspecs() + [pl.BlockSpec((s, FF_UNIT), lambda j: (0, j))],
        out_specs=(_ff_spec(s), pl.BlockSpec((2, 3, FF_UNIT), lambda j: (0, 0, j))),
        scratch_shapes=[pltpu.VMEM((s + _FF_HALO, FF_UNIT), F32), pltpu.VMEM((s + _FF_HALO, FF_UNIT), F32)],
        compiler_params=_params(("parallel",), VMEM_BIG),
    )(zf, conv_w, conv_w, da)


def _rope(v, cs, s1, s2):
    return v * cs + pltpu.roll(v, LANES - C_ROPE // 2, axis=1) * s1 + pltpu.roll(v, C_ROPE // 2, axis=1) * s2


def _rope_t(dv, cs, s1, s2):
    return dv * cs + pltpu.roll(dv * s1, C_ROPE // 2, axis=1) + pltpu.roll(dv * s2, LANES - C_ROPE // 2, axis=1)


def _kpe_mask(shape):
    lane = lax.broadcasted_iota(jnp.int32, shape, 1)
    return (lane >= C_NOPE) & (lane < C_NOPE + C_ROPE)


def _rms(v, g):
    r = lax.rsqrt(jnp.mean(v * v, axis=-1, keepdims=True) + EPS)
    return v * r, r


def _rms_bwd(dn, xh, r, g):
    dxh = dn * g
    return r * (dxh - xh * jnp.mean(dxh * xh, axis=-1, keepdims=True)), jnp.sum(dn * xh, axis=0, keepdims=True)


_ZQ = C_Q_RANK + C_KV_RANK + HEAD_PAD
_HW = C_HEADS * HEAD_PAD


def mla_pre_fwd(z, gq, gkv, wq, wk, wv, cs, s1, s2):
    s = z.shape[0]
    tm = _tile(s, 256, 8)

    def body(z_ref, gq_ref, gkv_ref, wq_ref, wk_ref, wv_ref, cs_ref, s1_ref, s2_ref, q_ref, k_ref, v_ref):
        zv = z_ref[...].astype(F32)
        cst, s1t, s2t = cs_ref[...], s1_ref[...], s2_ref[...]
        qh, _ = _rms(zv[:, :C_Q_RANK], None)
        qn = (qh * gq_ref[...]).astype(BF16)
        q = _dot(qn, wq_ref[...])
        kh, _ = _rms(zv[:, C_Q_RANK:C_Q_RANK + C_KV_RANK], None)
        kvn = (kh * gkv_ref[...]).astype(BF16)
        k = _dot(kvn, wk_ref[...])
        v_ref[...] = _dot(kvn, wv_ref[...]).astype(BF16)
        kpe = _rope(zv[:, C_Q_RANK + C_KV_RANK:], cst, s1t, s2t)
        for h in range(C_HEADS):
            sl = slice(h * HEAD_PAD, (h + 1) * HEAD_PAD)
            q_ref[:, sl] = _rope(q[:, sl], cst, s1t, s2t).astype(BF16)
            k_ref[:, sl] = (k[:, sl] + kpe).astype(BF16)

    full = lambda r, c: pl.BlockSpec((r, c), lambda i: (0, 0))
    hw = _sds((s, _HW), BF16)
    return pl.pallas_call(
        body, name="mla_pre_fwd", out_shape=(hw, hw, hw), grid=(s // tm,),
        in_specs=[_rows(tm, _ZQ), _vec(C_Q_RANK), _vec(C_KV_RANK), full(C_Q_RANK, _HW), full(C_KV_RANK, _HW),
                  full(C_KV_RANK, _HW), _rows(tm, LANES), _rows(tm, LANES), _rows(tm, LANES)],
        out_specs=(_rows(tm, _HW), _rows(tm, _HW), _rows(tm, _HW)),
        compiler_params=_params(("parallel",), VMEM_BIG),
    )(z, gq, gkv, wq, wk, wv, cs, s1, s2)


def mla_pre_bwd(z, gq, gkv, wq, wk, wv, cs, s1, s2, dq, dk, dv):
    s = z.shape[0]
    tm = _tile(s, 256, 8)

    def body(z_ref, gq_ref, gkv_ref, wq_ref, wk_ref, wv_ref, cs_ref, s1_ref, s2_ref, dq_ref, dk_ref, dv_ref,
             dz_ref, dwq_ref, dwk_ref, dwv_ref, dgq_ref, dgkv_ref):
        @pl.when(pl.program_id(0) == 0)
        def _():
            dwq_ref[...] = jnp.zeros_like(dwq_ref)
            dwk_ref[...] = jnp.zeros_like(dwk_ref)
            dwv_ref[...] = jnp.zeros_like(dwv_ref)
            dgq_ref[...] = jnp.zeros_like(dgq_ref)
            dgkv_ref[...] = jnp.zeros_like(dgkv_ref)

        zv = z_ref[...].astype(F32)
        cst, s1t, s2t = cs_ref[...], s1_ref[...], s2_ref[...]
        gqv, gkvv = gq_ref[...], gkv_ref[...]
        qh, rq = _rms(zv[:, :C_Q_RANK], None)
        qn = (qh * gqv).astype(BF16)
        kh, rk = _rms(zv[:, C_Q_RANK:C_Q_RANK + C_KV_RANK], None)
        kvn = (kh * gkvv).astype(BF16)

        dqv = dq_ref[...].astype(F32)
        dqp = jnp.concatenate(
            [_rope_t(dqv[:, h * HEAD_PAD:(h + 1) * HEAD_PAD], cst, s1t, s2t) for h in range(C_HEADS)], axis=1
        ).astype(BF16)
        dwq_ref[...] += _dot(qn, dqp, _TN)
        dqn = _dot(dqp, wq_ref[...], _NT)
        dql, dgq = _rms_bwd(dqn, qh, rq, gqv)
        dgq_ref[...] += dgq

        dkv = dk_ref[...]
        dkb = dkv.astype(BF16)
        dvb = dv_ref[...].astype(BF16)
        dwk_ref[...] += _dot(kvn, dkb, _TN)
        dwv_ref[...] += _dot(kvn, dvb, _TN)
        dkvn = _dot(dkb, wk_ref[...], _NT) + _dot(dvb, wv_ref[...], _NT)
        dkl, dgkv = _rms_bwd(dkvn, kh, rk, gkvv)
        dgkv_ref[...] += dgkv

        dkpe = dkv[:, :HEAD_PAD]
        for h in range(1, C_HEADS):
            dkpe = dkpe + dkv[:, h * HEAD_PAD:(h + 1) * HEAD_PAD]
        dkpe = _rope_t(jnp.where(_kpe_mask(dkpe.shape), dkpe, 0.0), cst, s1t, s2t)
        dz_ref[...] = jnp.concatenate([dql, dkl, dkpe], axis=1).astype(BF16)

    full = lambda r, c: pl.BlockSpec((r, c), lambda i: (0, 0))
    return pl.pallas_call(
        body, name="mla_pre_bwd",
        out_shape=(_sds((s, _ZQ), BF16), _sds((C_Q_RANK, _HW)), _sds((C_KV_RANK, _HW)), _sds((C_KV_RANK, _HW)),
                   _sds((1, C_Q_RANK)), _sds((1, C_KV_RANK))),
        grid=(s // tm,),
        in_specs=[_rows(tm, _ZQ), _vec(C_Q_RANK), _vec(C_KV_RANK), full(C_Q_RANK, _HW), full(C_KV_RANK, _HW),
                  full(C_KV_RANK, _HW), _rows(tm, LANES), _rows(tm, LANES), _rows(tm, LANES),
                  _rows(tm, _HW), _rows(tm, _HW), _rows(tm, _HW)],
        out_specs=(_rows(tm, _ZQ), full(C_Q_RANK, _HW), full(C_KV_RANK, _HW), full(C_KV_RANK, _HW),
                   _vec(C_Q_RANK), _vec(C_KV_RANK)),
        compiler_params=_params(("arbitrary",), VMEM_BIG),
    )(z, gq, gkv, wq, wk, wv, cs, s1, s2, dq, dk, dv)


_ATT_SCALE = (C_NOPE + C_ROPE) ** -0.5
_NEG = -1e30


def _att_exp(q, k, row0, ends_here):
    sc = _dot(q, k, _NT) * _ATT_SCALE
    tq, nk = sc.shape
    if ends_here:
        last = sc[:, nk - tq:]
        row = lax.broadcasted_iota(jnp.int32, last.shape, 0)
        col = lax.broadcasted_iota(jnp.int32, last.shape, 1)
        last = jnp.where(col <= row, last, _NEG)
        sc = last if nk == tq else jnp.concatenate([sc[:, :nk - tq], last], axis=1)
    else:
        qpos = row0 + lax.broadcasted_iota(jnp.int32, sc.shape, 0)
        kpos = lax.broadcasted_iota(jnp.int32, sc.shape, 1)
        sc = jnp.where(kpos <= qpos, sc, _NEG)
    e = jnp.exp(sc - jnp.max(sc, axis=-1, keepdims=True))
    return e, 1.0 / jnp.sum(e, axis=-1, keepdims=True)


def _causal_cases(i, nq, tq, fn):
    if nq > 8:
        fn(nq * tq, False)
        return
    for blk in range(nq):
        pl.when(i == blk)(functools.partial(fn, (blk + 1) * tq, True))


def attn_fwd(q, k, v):
    s = q.shape[0]
    tq = _tile(s, 256, 8)
    nq = s // tq

    def body(q_ref, k_ref, v_ref, o_ref):
        i = pl.program_id(1)

        def case(nk, ends_here):
            e, inv = _att_exp(q_ref[...], k_ref[:nk, :], i * tq, ends_here)
            o_ref[...] = (_dot(e.astype(BF16), v_ref[:nk, :]) * inv).astype(BF16)

        _causal_cases(i, nq, tq, case)

    qspec = pl.BlockSpec((tq, HEAD_PAD), lambda h, i: (i, h))
    kspec = pl.BlockSpec((s, HEAD_PAD), lambda h, i: (0, h))
    return pl.pallas_call(
        body, name="attn_fwd", out_shape=_sds((s, _HW), BF16), grid=(C_HEADS, s // tq),
        in_specs=[qspec, kspec, kspec], out_specs=qspec,
        compiler_params=_params(("parallel", "parallel"), VMEM_BIG),
    )(q, k, v)


def attn_bwd(q, k, v, o, do_all, do_col0):
    s = q.shape[0]
    tq = _tile(s, 256, 8)

    def body(q_ref, k_ref, v_ref, o_ref, do_ref, dq_ref, dk_ref, dv_ref):
        i = pl.program_id(1)

        @pl.when(i == 0)
        def _():
            dk_ref[...] = jnp.zeros_like(dk_ref)
            dv_ref[...] = jnp.zeros_like(dv_ref)

        def case(nk, ends_here):
            qv, kv, vv, dov = q_ref[...], k_ref[:nk, :], v_ref[:nk, :], do_ref[...]
            e, inv = _att_exp(qv, kv, i * tq, ends_here)
            p = e * inv
            dp = _dot(dov, vv, _NT)
            delta = jnp.sum(dov.astype(F32) * o_ref[...].astype(F32), axis=-1, keepdims=True)
            ds = (p * (dp - delta) * _ATT_SCALE).astype(BF16)
            dq_ref[...] = _dot(ds, kv).astype(BF16)
            dk_ref[:nk, :] += _dot(ds, qv, _TN)
            dv_ref[:nk, :] += _dot(p.astype(BF16), dov, _TN)

        _causal_cases(i, s // tq, tq, case)

    qspec = pl.BlockSpec((tq, HEAD_PAD), lambda h, i: (i, h))
    dospec = pl.BlockSpec((tq, HEAD_PAD), lambda h, i: (i, do_col0 + h))
    kspec = pl.BlockSpec((s, HEAD_PAD), lambda h, i: (0, h))
    return pl.pallas_call(
        body, name="attn_bwd", out_shape=(_sds((s, _HW), BF16), _sds((s, _HW)), _sds((s, _HW))),
        grid=(C_HEADS, s // tq),
        in_specs=[qspec, kspec, kspec, qspec, dospec], out_specs=(qspec, kspec, kspec),
        compiler_params=_params(("parallel", "arbitrary"), VMEM_BIG),
    )(q, k, v, o, do_all)


_DW = D_GROUPS * LANES


def _tril_bf16(w):
    r = lax.broadcasted_iota(jnp.int32, w.shape, 0)
    c = lax.broadcasted_iota(jnp.int32, w.shape, 1)
    return jnp.where(c <= r, w, 0.0).astype(BF16)


def _sgu_forward(zu, zv, lg, lb, ws_ref, bs):
    u = _gelu(zu)
    v = _gelu(zv)
    mu = jnp.mean(v, axis=-1, keepdims=True)
    vc = v - mu
    rstd = lax.rsqrt(jnp.mean(vc * vc, axis=-1, keepdims=True) + EPS)
    xh = vc * rstd
    vln = (xh * lg + lb).astype(BF16)
    mixed = []
    for g in range(D_GROUPS):
        wg = _tril_bf16(ws_ref[g])
        mixed.append(_dot(wg, vln[:, g * LANES:(g + 1) * LANES]) + bs[:, g:g + 1])
    return u, xh, rstd, vln, jnp.concatenate(mixed, axis=1)


def sgu_fwd(z, lg, lb, ws, bs_t):
    s = z.shape[0]
    nchunk = s // D_CHUNK

    def body(zu_ref, zv_ref, lg_ref, lb_ref, ws_ref, bs_ref, o_ref):
        u, _, _, _, mixed = _sgu_forward(zu_ref[...].astype(F32), zv_ref[...].astype(F32), lg_ref[...], lb_ref[...],
                                         ws_ref, bs_ref[...])
        o_ref[...] = (u * mixed).astype(BF16)

    return pl.pallas_call(
        body, name="sgu_fwd", out_shape=_sds((s, _DW), BF16), grid=(nchunk,),
        in_specs=[pl.BlockSpec((D_CHUNK, _DW), lambda n: (n, 1)), pl.BlockSpec((D_CHUNK, _DW), lambda n: (n, 2)),
                  _vec(_DW), _vec(_DW), pl.BlockSpec((D_GROUPS, D_CHUNK, D_CHUNK), lambda n: (0, 0, 0)),
                  pl.BlockSpec((D_CHUNK, LANES), lambda n: (0, 0))],
        out_specs=pl.BlockSpec((D_CHUNK, _DW), lambda n: (n, 0)),
        compiler_params=_params(("parallel",)),
    )(z, z, lg, lb, ws, bs_t)


def sgu_bwd(z, lg, lb, ws, bs_t, dycat, dy_col):
    s = z.shape[0]
    nchunk = s // D_CHUNK

    def body(zu_ref, zv_ref, lg_ref, lb_ref, ws_ref, bs_ref, dy_ref, dzu_ref, dzv_ref, dws_ref, dbs_ref, dlg_ref,
             dlb_ref):
        @pl.when(pl.program_id(0) == 0)
        def _():
            dws_ref[...] = jnp.zeros_like(dws_ref)
            dbs_ref[...] = jnp.zeros_like(dbs_ref)
            dlg_ref[...] = jnp.zeros_like(dlg_ref)
            dlb_ref[...] = jnp.zeros_like(dlb_ref)

        zu, zv, lg = zu_ref[...].astype(F32), zv_ref[...].astype(F32), lg_ref[...]
        u, xh, rstd, vln, mixed = _sgu_forward(zu, zv, lg, lb_ref[...], ws_ref, bs_ref[...])
        dy = dy_ref[...].astype(F32)
        dzu_ref[...] = (dy * mixed * _gelu_grad(zu)).astype(BF16)
        dmix = dy * u
        lane = lax.broadcasted_iota(jnp.int32, (D_CHUNK, LANES), 1)
        row = lax.broadcasted_iota(jnp.int32, (D_CHUNK, D_CHUNK), 0)
        colm = lax.broadcasted_iota(jnp.int32, (D_CHUNK, D_CHUNK), 1)
        dvln = []
        dbs = jnp.zeros((D_CHUNK, LANES), F32)
        for g in range(D_GROUPS):
            sl = slice(g * LANES, (g + 1) * LANES)
            dmg = dmix[:, sl]
            dbs = dbs + jnp.where(lane == g, jnp.sum(dmg, axis=-1, keepdims=True), 0.0)
            dmb = dmg.astype(BF16)
            dws_ref[g] += jnp.where(colm <= row, _dot(dmb, vln[:, sl], _NT), 0.0)
            dvln.append(_dot(_tril_bf16(ws_ref[g]), dmb, _TN))
        dbs_ref[...] += dbs
        dvln = jnp.concatenate(dvln, axis=1)
        dlg_ref[...] += jnp.sum(dvln * xh, axis=0, keepdims=True)
        dlb_ref[...] += jnp.sum(dvln, axis=0, keepdims=True)
        dxh = dvln * lg
        dvv = rstd * (dxh - jnp.mean(dxh, axis=-1, keepdims=True) - xh * jnp.mean(dxh * xh, axis=-1, keepdims=True))
        dzv_ref[...] = (dvv * _gelu_grad(zv)).astype(BF16)

    wsspec = pl.BlockSpec((D_GROUPS, D_CHUNK, D_CHUNK), lambda n: (0, 0, 0))
    chunk = lambda cidx: pl.BlockSpec((D_CHUNK, _DW), lambda n: (n, cidx))
    return pl.pallas_call(
        body, name="sgu_bwd",
        out_shape=(_sds((s, _DW), BF16), _sds((s, _DW), BF16), _sds((D_GROUPS, D_CHUNK, D_CHUNK)),
                   _sds((D_CHUNK, LANES)), _sds((1, _DW)), _sds((1, _DW))),
        grid=(nchunk,),
        in_specs=[chunk(1), chunk(2), _vec(_DW), _vec(_DW), wsspec, pl.BlockSpec((D_CHUNK, LANES), lambda n: (0, 0)),
                  chunk(dy_col)],
        out_specs=(chunk(0), chunk(0), wsspec, pl.BlockSpec((D_CHUNK, LANES), lambda n: (0, 0)), _vec(_DW), _vec(_DW)),
        compiler_params=_params(("arbitrary",)),
    )(z, z, lg, lb, ws, bs_t, dycat)


def ada_mod(c_all, ada_w, ada_b):
    nl, d, n = ada_w.shape
    nb = c_all.shape[0]
    tn = _tile(n, 512)

    def body(c_ref, w_ref, b_ref, o_ref):
        cv = c_ref[...]
        ca = (cv * _sigmoid(cv)).astype(BF16)
        o_ref[...] = _dot(ca, w_ref[...].astype(BF16)) + b_ref[...]

    return pl.pallas_call(
        body, name="ada_mod", out_shape=_sds((nl, nb, n)), grid=(nl, n // tn),
        in_specs=[pl.BlockSpec((nb, d), lambda l, j: (0, 0)), pl.BlockSpec((None, d, tn), lambda l, j: (l, 0, j)),
                  pl.BlockSpec((None, 1, tn), lambda l, j: (l, 0, j))],
        out_specs=pl.BlockSpec((None, nb, tn), lambda l, j: (l, 0, j)),
        compiler_params=_params(("parallel", "parallel")),
    )(c_all, ada_w, ada_b.reshape(nl, 1, n))


_ADAM_BLOCK = 256 * 1024


def _adam_rows(rows, cols):
    if rows * cols <= _ADAM_BLOCK or rows % 8:
        return rows
    return _tile(rows, max(8, _ADAM_BLOCK // cols), 8)


def _adam_update(w, gv, m, v):
    inv_bc1 = 1.0 / (1.0 - ADAM_B1 ** ADAM_STEP)
    inv_bc2 = 1.0 / (1.0 - ADAM_B2 ** ADAM_STEP)
    nm = ADAM_B1 * m + (1.0 - ADAM_B1) * gv
    nv = ADAM_B2 * v + (1.0 - ADAM_B2) * (gv * gv)
    return -ADAM_LR * ((nm * inv_bc1) / (jnp.sqrt(nv * inv_bc2) + ADAM_EPS) + ADAM_WD * w), nm, nv


def adamw(w, g, m, v, name):
    shape = w.shape
    cols = shape[-1]
    rows = w.size // cols
    tr = _adam_rows(rows, cols)

    def body(w_ref, g_ref, m_ref, v_ref, go_ref, d_ref, nm_ref, nv_ref):
        gv = g_ref[...]
        go_ref[...] = gv
        d_ref[...], nm_ref[...], nv_ref[...] = _adam_update(w_ref[...], gv, m_ref[...], v_ref[...])

    spec = pl.BlockSpec((tr, cols), lambda i: (i, 0))
    out = _sds((rows, cols))
    r2 = lambda t: t.reshape(rows, cols)
    res = pl.pallas_call(
        body, name=name, out_shape=(out,) * 4, grid=(rows // tr,),
        in_specs=[spec] * 4, out_specs=(spec,) * 4, compiler_params=_params(("parallel",)),
    )(r2(w), r2(g), r2(m), r2(v))
    return tuple(t.reshape(shape) for t in res)


def adamw_ada(w, c_all, dmod, m, v):
    nl, d, n = w.shape
    tr = _adam_rows(d, n)
    pad = 16 - c_all.shape[0]
    c16 = jnp.pad(c_all, ((0, pad), (0, 0)))
    dm16 = jnp.pad(dmod, ((0, 0), (0, pad), (0, 0)))

    def body(w_ref, c_ref, dm_ref, m_ref, v_ref, g_ref, d_ref, nm_ref, nv_ref):
        cv = c_ref[...]
        gv = _dot((cv * _sigmoid(cv)).astype(BF16), dm_ref[...].astype(BF16), _TN)
        g_ref[...] = gv
        d_ref[...], nm_ref[...], nv_ref[...] = _adam_update(w_ref[...], gv, m_ref[...], v_ref[...])

    spec = pl.BlockSpec((None, tr, n), lambda l, i: (l, i, 0))
    out = _sds((nl, d, n))
    return pl.pallas_call(
        body, name="adamw_ada_w", out_shape=(out, out, out, out), grid=(nl, d // tr),
        in_specs=[spec, pl.BlockSpec((16, tr), lambda l, i: (0, i)), pl.BlockSpec((None, 16, n), lambda l, i: (l, 0, 0)),
                  spec, spec],
        out_specs=(spec,) * 4, compiler_params=_params(("parallel", "parallel")),
    )(w, c16, dm16, m, v)


def adamw_small(ws, gs, ms, vs):
    n = len(ws)
    flat = lambda t: t.reshape(-1, t.shape[-1])

    def body(*refs):
        ins, outs = refs[:4 * n], refs[4 * n:]
        for i in range(n):
            w_ref, g_ref, m_ref, v_ref = ins[4 * i:4 * i + 4]
            outs[3 * i][...], outs[3 * i + 1][...], outs[3 * i + 2][...] = _adam_update(
                w_ref[...], g_ref[...], m_ref[...], v_ref[...])

    operands = [flat(t) for quad in zip(ws, gs, ms, vs) for t in quad]
    res = pl.pallas_call(
        body, name="adamw_small", out_shape=tuple(_sds(flat(w).shape) for w in ws for _ in range(3)),
    )(*operands)
    return [(g, res[3 * i].reshape(w.shape), res[3 * i + 1].reshape(w.shape), res[3 * i + 2].reshape(w.shape))
            for i, (w, g) in enumerate(zip(ws, gs))]


def adamw_layers(w, g0, g1, m, v, name):
    _, rows, cols = w.shape
    tr = _adam_rows(rows, cols)

    def body(w_ref, g0_ref, g1_ref, m_ref, v_ref, g_ref, d_ref, nm_ref, nv_ref):
        gv = jnp.where(pl.program_id(0) == 0, g0_ref[...], g1_ref[...])
        g_ref[...] = gv
        d_ref[...], nm_ref[...], nv_ref[...] = _adam_update(w_ref[...], gv, m_ref[...], v_ref[...])

    spec = pl.BlockSpec((None, tr, cols), lambda l, i: (l, i, 0))
    gspec = pl.BlockSpec((tr, cols), lambda l, i: (i, 0))
    out = _sds((2, rows, cols))
    return pl.pallas_call(
        body, name=name, out_shape=(out, out, out, out), grid=(2, rows // tr),
        in_specs=[spec, gspec, gspec, spec, spec], out_specs=(spec,) * 4, compiler_params=_params(("parallel", "parallel")),
    )(w, g0, g1, m, v)


def sum8(gathered):
    _, r, _ = gathered.shape
    tr = _tile(r, 512, 8)

    def body(g_ref, o_ref):
        acc = g_ref[0]
        for dev in range(1, N_DEV):
            acc = acc + g_ref[dev]
        o_ref[...] = acc

    return pl.pallas_call(
        body, name="sum8", out_shape=_sds((r, LANES)), grid=(r // tr,),
        in_specs=[pl.BlockSpec((N_DEV, tr, LANES), lambda i: (0, i, 0))], out_specs=pl.BlockSpec((tr, LANES), lambda i: (i, 0)),
        compiler_params=_params(("parallel",)),
    )(gathered)


_SUM_STEPS = 2


def pair_sums(gs, recvs, core, name):
    n = len(gs)
    trs = [g.shape[1] // 2 // _SUM_STEPS for g in gs]

    def body(c_ref, *refs):
        del c_ref
        for i in range(n):
            a_ref, b_ref, o_ref = refs[2 * i], refs[2 * i + 1], refs[2 * n + i]
            o_ref[...] = (a_ref[...].astype(F32) + b_ref[...].astype(F32)).astype(BF16)

    in_specs, out_specs = [], []
    for g, tr in zip(gs, trs):
        cols = g.shape[2]
        in_specs.append(pl.BlockSpec((None, tr, cols), lambda k, s, c: (k, c[0] * _SUM_STEPS + s, 0)))
        in_specs.append(pl.BlockSpec((None, tr, cols), lambda k, s, c: (k, s, 0)))
        out_specs.append(pl.BlockSpec((None, tr, cols), lambda k, s, c: (k, s, 0)))
    grid_spec = pltpu.PrefetchScalarGridSpec(num_scalar_prefetch=1, grid=(N_CHIPS, _SUM_STEPS), in_specs=in_specs,
                                             out_specs=tuple(out_specs))
    return list(pl.pallas_call(
        body, name=name, out_shape=tuple(_sds((N_CHIPS, g.shape[1] // 2, g.shape[2]), BF16) for g in gs),
        grid_spec=grid_spec, compiler_params=_params(("parallel", "parallel")),
    )(core.reshape(1).astype(jnp.int32), *[t for pair in zip(gs, recvs) for t in pair]))


def chip_sums(pairs, recvs, chip, core, name):
    n = len(pairs)
    trs = [p.shape[1] // _SUM_STEPS for p in pairs]

    def body(p_ref, *refs):
        del p_ref
        for i in range(n):
            own_ref, r_ref, o_ref = refs[2 * i], refs[2 * i + 1], refs[2 * n + i]
            acc = own_ref[...].astype(F32)
            for j in range(N_CHIPS - 1):
                acc = acc + r_ref[j].astype(F32)
            o_ref[...] = acc

    in_specs, out_specs = [], []
    for p, tr in zip(pairs, trs):
        cols = p.shape[2]
        in_specs.append(pl.BlockSpec((None, tr, cols), lambda s, q: (q[0], s, 0)))
        in_specs.append(pl.BlockSpec((N_CHIPS - 1, tr, cols), lambda s, q: (0, s, 0)))
        out_specs.append(pl.BlockSpec((None, tr, cols), lambda s, q: (q[1], s, 0)))
    grid_spec = pltpu.PrefetchScalarGridSpec(num_scalar_prefetch=1, grid=(_SUM_STEPS,), in_specs=in_specs,
                                             out_specs=tuple(out_specs))
    return list(pl.pallas_call(
        body, name=name, out_shape=tuple(_sds((2,) + p.shape[1:]) for p in pairs), grid_spec=grid_spec,
        compiler_params=_params(("parallel",)),
    )(jnp.stack([chip, core]).astype(jnp.int32), *[t for pair in zip(pairs, recvs) for t in pair]))


def _place():
    return lax.axis_index("x"), lax.axis_index("y"), lax.axis_index("c")


def _other_chips(x, y):
    return [(x, 1 - y), (1 - x, y), (1 - x, 1 - y)]


_HBM = pl.BlockSpec(memory_space=pltpu.HBM)


def all_gather8(v, name, after=()):
    m, n = v.shape

    def body(x_ref, *refs):
        out_ref, send_sems, recv_sems, local_sem = refs[len(after):]
        x, y, c = _place()
        me, sibling = (x, y, c), (x, y, 1 - c)
        chips = _other_chips(x, y)

        def rows(px, py, pc):
            return out_ref.at[pl.ds((4 * px + 2 * py + pc) * m, m), :]

        def copy(k, block, to, src=None):
            return pltpu.make_async_remote_copy(
                src_ref=rows(*block) if src is None else src, dst_ref=rows(*block),
                send_sem=send_sems.at[k], recv_sem=recv_sems.at[k], device_id=to, device_id_type=MESH)

        mine = pltpu.make_async_copy(x_ref, rows(*me), local_sem)
        mine.start()
        first = [copy(0, me, sibling, src=x_ref)]
        first += [copy(1 + j, me, (*chip, c), src=x_ref) for j, chip in enumerate(chips)]
        for cp in first:
            cp.start()
        passed = [copy(4 + j, (*chip, c), sibling) for j, chip in enumerate(chips)]
        for j, chip in enumerate(chips):
            copy(1 + j, (*chip, c), me).wait_recv()
            passed[j].start()
        copy(0, sibling, me).wait_recv()
        for j, chip in enumerate(chips):
            copy(4 + j, (*chip, 1 - c), me).wait_recv()
        for cp in first + passed:
            cp.wait_send()
        mine.wait()

    return pl.pallas_call(
        body, name=name, out_shape=_sds((N_DEV * m, n), v.dtype),
        in_specs=[pl.BlockSpec(memory_space=pltpu.VMEM)] + [pl.BlockSpec(memory_space=pl.ANY)] * len(after),
        out_specs=pl.BlockSpec(memory_space=pltpu.VMEM),
        scratch_shapes=[pltpu.SemaphoreType.DMA((7,)), pltpu.SemaphoreType.DMA((7,)), pltpu.SemaphoreType.DMA],
        compiler_params=_params(None, VMEM_BIG),
    )(v, *after)


def _comm_call(body, name, ins, out_shapes, nsem, aliases=None):
    return pl.pallas_call(
        body, name=name, out_shape=tuple(out_shapes), in_specs=[_HBM] * len(ins), out_specs=tuple([_HBM] * len(out_shapes)),
        scratch_shapes=[pltpu.SemaphoreType.DMA((nsem,)), pltpu.SemaphoreType.DMA((nsem,))],
        input_output_aliases=aliases or {},
    )(*ins)


def _remote(src, dst, send_sems, recv_sems, k, to):
    return pltpu.make_async_remote_copy(src_ref=src, dst_ref=dst, send_sem=send_sems.at[k], recv_sem=recv_sems.at[k],
                                        device_id=to, device_id_type=MESH)


def _half(core, rh):
    return pl.ds(pl.multiple_of(core * rh, 16), rh)


def swap_halves(gs, name):
    n = len(gs)

    def body(*refs):
        ins, outs, (send_sems, recv_sems) = refs[:n], refs[n:2 * n], refs[2 * n:]
        x, y, c = _place()
        copies = []
        for i in range(n):
            theirs = _half(1 - c, ins[i].shape[1] // 2)
            cp = _remote(ins[i].at[:, theirs], outs[i], send_sems, recv_sems, i, (x, y, 1 - c))
            cp.start()
            copies.append(cp)
        for cp in copies:
            cp.wait()

    return _comm_call(body, name, gs, [_sds((g.shape[0], g.shape[1] // 2, g.shape[2]), g.dtype) for g in gs], n)


def join_halves(bufs, name):
    n = len(bufs)

    def body(*refs):
        ins, outs, (send_sems, recv_sems) = refs[:n], refs[n:2 * n], refs[2 * n:]
        x, y, c = _place()
        copies = []
        for i in range(n):
            cp = _remote(ins[i].at[c], outs[i].at[c], send_sems, recv_sems, i, (x, y, 1 - c))
            cp.start()
            copies.append(cp)
        for i in range(n):
            theirs = outs[i].at[1 - c]
            _remote(theirs, theirs, send_sems, recv_sems, i, (x, y, 1 - c)).wait_recv()
        for cp in copies:
            cp.wait_send()

    return _comm_call(body, name, bufs, [_sds(b.shape, b.dtype) for b in bufs], n, {i: i for i in range(n)})


def forward_halves(lands, name):
    n = len(lands)

    def body(*refs):
        ins, outs, (send_sems, recv_sems) = refs[:n], refs[n:2 * n], refs[2 * n:]
        x, y, c = _place()
        sibling = (x, y, 1 - c)
        chips = _other_chips(x, y)
        copies = []
        for i in range(n):
            mine = _half(c, ins[i].shape[1] // 2)
            for j, (px, py) in enumerate(chips):
                cp = _remote(ins[i].at[2 * px + py, mine], outs[i].at[2 * px + py, mine], send_sems, recv_sems, 3 * i + j, sibling)
                cp.start()
                copies.append(cp)
        for i in range(n):
            theirs = _half(1 - c, ins[i].shape[1] // 2)
            for j, (px, py) in enumerate(chips):
                landed = outs[i].at[2 * px + py, theirs]
                _remote(landed, landed, send_sems, recv_sems, 3 * i + j, sibling).wait_recv()
        for cp in copies:
            cp.wait_send()

    return _comm_call(body, name, lands, [_sds(b.shape, b.dtype) for b in lands], 3 * n, {i: i for i in range(n)})


_SEM = pl.BlockSpec(memory_space=pltpu.SEMAPHORE)
_EFFECT = pltpu.SideEffectType.DATAFLOW_SIDE_EFFECTING


def _gather_copies(srcs, lands, send_sems, recv_sems):
    x, y, c = _place()
    copies = []
    for i in range(len(srcs)):
        mine = _half(c, srcs[i].shape[0] // 2)
        for j, chip in enumerate(_other_chips(x, y)):
            copies.append(_remote(srcs[i].at[mine], lands[i].at[2 * x + y, mine], send_sems, recv_sems, 3 * i + j, (*chip, c)))
    return copies


def _exchange_copies(srcs, lands, send_sems, recv_sems):
    x, y, c = _place()
    copies = []
    for i in range(len(srcs)):
        for j, (px, py) in enumerate(_other_chips(x, y)):
            copies.append(_remote(srcs[i].at[2 * px + py], lands[i].at[j], send_sems, recv_sems, 3 * i + j, (px, py, c)))
    return copies


def _everyone_copies(srcs, lands, send_sems, recv_sems):
    x, y, c = _place()
    flip = lambda v, b: 1 - v if b else v
    dst = lands[0].at[4 * x + 2 * y + c]
    return [_remote(srcs[0], dst, send_sems, recv_sems, j - 1, (flip(x, j & 4), flip(y, j & 2), flip(c, j & 1)))
            for j in range(1, N_DEV)]


GATHER = (_gather_copies, 3)
EXCHANGE = (_exchange_copies, 3)
EVERYONE = (_everyone_copies, N_DEV - 1)


def split_start(name, plan, srcs, land_shapes, after=()):
    copies_fn, per_source = plan
    n, m, k = len(srcs), len(land_shapes), len(after)
    ncopies = per_source * n

    def body(*refs):
        src_refs, land_refs = refs[:n], refs[n:n + m]
        send_sems, recv_sems = refs[n + m + k], refs[n + m + k + 1]
        token = refs[-1]
        for cp in copies_fn(src_refs, land_refs, send_sems, recv_sems):
            cp.start()
        token[...] = jnp.zeros_like(token)

    hbm = lambda s: pltpu.HBM(tuple(s.shape), s.dtype)
    outs = pl.pallas_call(
        body, name=name,
        out_shape=(pltpu.SemaphoreType.DMA((ncopies,)), pltpu.SemaphoreType.DMA((ncopies,)), *[hbm(s) for s in srcs],
                   *[hbm(s) for s in land_shapes], _sds((8, LANES))),
        in_specs=[_HBM] * (n + m) + [pl.BlockSpec(memory_space=pl.ANY)] * k,
        out_specs=(_SEM, _SEM, *([_HBM] * (n + m)), pl.BlockSpec(memory_space=pltpu.VMEM)),
        input_output_aliases={i: 2 + i for i in range(n + m)},
        compiler_params=pltpu.CompilerParams(has_side_effects=_EFFECT),
    )(*[pltpu.with_memory_space_constraint(s, pltpu.HBM) for s in srcs],
      *[pltpu.with_memory_space_constraint(lax.empty(tuple(s.shape), s.dtype), pltpu.HBM) for s in land_shapes], *after)
    handle = (outs[0], outs[1], list(outs[2:2 + n]), list(outs[2 + n:2 + n + m]))
    return handle, outs[-1][0, 0]


def split_wait(name, plan, handle, after):
    copies_fn, _ = plan
    send_sems, recv_sems, srcs, lands = handle
    n, m = len(srcs), len(lands)
    after = list(after) if isinstance(after, (list, tuple)) else [after]

    def body(*refs):
        src_refs, land_refs = refs[:n], refs[n:n + m]
        for cp in copies_fn(src_refs, land_refs, refs[n + m], refs[n + m + 1]):
            cp.wait_send()
            cp.wait_recv()

    hbm = lambda s: pltpu.HBM(tuple(s.shape), s.dtype)
    outs = pl.pallas_call(
        body, name=name, out_shape=tuple(hbm(s) for s in srcs + lands),
        in_specs=[_HBM] * (n + m) + [_SEM, _SEM] + [pl.BlockSpec(memory_space=pl.ANY)] * len(after),
        out_specs=tuple([_HBM] * (n + m)), input_output_aliases={i: i for i in range(n + m)},
        compiler_params=pltpu.CompilerParams(has_side_effects=_EFFECT),
    )(*srcs, *lands, send_sems, recv_sems, *after)
    return list(outs[:n]), list(outs[n:])


_CD_PAD = C_Q_RANK + C_KV_RANK + HEAD_PAD + 2 * _DW


def chip_major(w, groups=N_CHIPS):
    r, c = w.shape
    return w.reshape(r, groups, c // groups).transpose(1, 0, 2)


def from_chip_major(w):
    g, r, c = w.shape
    return w.transpose(1, 0, 2).reshape(r, g * c)


def _cd_in_pad(w):
    a = C_Q_RANK + C_KV_RANK
    z = lambda n: jnp.zeros((w.shape[0], n), w.dtype)
    return jnp.concatenate([w[:, :a], z(C_NOPE), w[:, a:a + C_ROPE], z(HEAD_PAD - C_NOPE - C_ROPE), w[:, a + C_ROPE:]], axis=1)


def _cd_in_unpad(w):
    a = C_Q_RANK + C_KV_RANK
    return jnp.concatenate([w[:, :a], w[:, a + C_NOPE:a + C_NOPE + C_ROPE], w[:, a + HEAD_PAD:]], axis=1)


def _pad_heads(w, width):
    r = w.shape[0]
    w = w.reshape(r, C_HEADS, width)
    return jnp.pad(w, ((0, 0), (0, 0), (0, HEAD_PAD - width))).reshape(r, _HW)


def _unpad_heads(w, width):
    r = w.shape[0]
    return w.reshape(r, C_HEADS, HEAD_PAD)[:, :, :width].reshape(r, C_HEADS * width)


_MATMUL_WEIGHTS = ("ab_w_in", "ab_w_out", "cd_w_in", "c_w_uq", "c_w_ukv", "cd_w_out", "ffn_w_up", "ffn_w_down")
_LAYER_STACKED = ("norm1_g", "norm2_g", "ffn_w_up", "ffn_conv_w", "ffn_w_down")
_ROW_VECTORS = ("b_scale", "c_q_norm_g", "c_kv_norm_g", "d_ln_g", "d_ln_b")


def full_to_local(p):
    q = {}
    for k, v in p.items():
        if k == "final_norm_g":
            v = v.reshape(1, -1)
        elif k not in _LAYER_STACKED and k not in _ROW_VECTORS:
            v = v[0]
        q[k] = v.astype(BF16) if k in _MATMUL_WEIGHTS else v
    return q


def local_to_full(g):
    q = {}
    for k, v in g.items():
        if k == "final_norm_g":
            q[k] = v.reshape(-1)
        elif k not in _LAYER_STACKED and k not in _ROW_VECTORS:
            q[k] = v[None]
        else:
            q[k] = v
    return q


def prepare_weights(p):
    q = dict(p)
    q["cd_w_in"] = _cd_in_pad(p["cd_w_in"])
    q["c_w_uq"] = _pad_heads(p["c_w_uq"], C_NOPE + C_ROPE)
    ukv = p["c_w_ukv"].reshape(C_KV_RANK, C_HEADS, C_NOPE + C_V)
    q["c_w_uk"] = _pad_heads(ukv[:, :, :C_NOPE].reshape(C_KV_RANK, -1), C_NOPE)
    q["c_w_uv"] = _pad_heads(ukv[:, :, C_NOPE:].reshape(C_KV_RANK, -1), C_V)
    wo = p["cd_w_out"]
    att_rows = jnp.pad(wo[:C_HEADS * C_V].reshape(C_HEADS, C_V, D_MODEL), ((0, 0), (0, HEAD_PAD - C_V), (0, 0)))
    q["cd_w_out"] = jnp.concatenate([att_rows.reshape(_HW, D_MODEL), wo[C_HEADS * C_V:]], axis=0)
    return q


def unprepare_grads(g):
    q = dict(g)
    q["cd_w_in"] = _cd_in_unpad(g["cd_w_in"])
    q["c_w_uq"] = _unpad_heads(g["c_w_uq"], C_NOPE + C_ROPE)
    uk = g.pop("c_w_uk").reshape(C_KV_RANK, C_HEADS, HEAD_PAD)[:, :, :C_NOPE]
    uv = g.pop("c_w_uv").reshape(C_KV_RANK, C_HEADS, HEAD_PAD)[:, :, :C_V]
    q.pop("c_w_uk", None)
    q.pop("c_w_uv", None)
    q["c_w_ukv"] = jnp.concatenate([uk, uv], axis=-1).reshape(C_KV_RANK, C_HEADS * (C_NOPE + C_V))
    wo = g["cd_w_out"]
    att = wo[:_HW].reshape(C_HEADS, HEAD_PAD, D_MODEL)[:, :C_V].reshape(C_HEADS * C_V, D_MODEL)
    q["cd_w_out"] = jnp.concatenate([att, wo[_HW:]], axis=0)
    return q


def rope_tables(positions):
    half = C_ROPE // 2
    inv_freq = ROPE_THETA ** (-jnp.arange(half, dtype=F32) / half)
    ang = positions.astype(F32)[:, None] * inv_freq
    cos, sin = jnp.cos(ang), jnp.sin(ang)
    s = positions.shape[0]
    z = lambda n: jnp.zeros((s, n), F32)
    cs = jnp.concatenate([jnp.ones((s, C_NOPE), F32), cos, cos, z(HEAD_PAD - C_NOPE - C_ROPE)], axis=1)
    s1 = jnp.concatenate([z(C_NOPE), -sin, z(HEAD_PAD - C_NOPE - half)], axis=1)
    s2 = jnp.concatenate([z(C_NOPE + half), sin, z(HEAD_PAD - C_NOPE - C_ROPE)], axis=1)
    return cs, s1, s2


def _mods(mod_l):
    return [mod_l[:, i * D_MODEL:(i + 1) * D_MODEL] for i in range(N_MOD)]


_UP_COLS = 2 * D_FF // N_CHIPS


def ffn_fwd(h2, w, l, late_down=None):
    zf = matmul(h2, w["ffn_w_up"][l], "nn", BF16, f"ffn_up{l}", gb=N_CHIPS, go=2, tn=_UP_COLS)
    a = ffn_act_fwd(zf, w["ffn_conv_w"][l], f"ffn_act_fwd{l}")
    if late_down is not None:
        late_down(a)
    f = matmul(a, w["ffn_w_down"][l], "nn", BF16, f"ffn_down{l}", tk=D_FF)
    return f, (zf, a)


def ffn_bwd(df, h2, saved, w, l):
    zf, a = saved
    da = matmul(df, w["ffn_w_down"][l], "nt", BF16, f"ffn_down_dx{l}", tn=D_FF // 2)
    d_down = matmul(a, df, "tn", BF16, f"ffn_down_dw{l}", tm=D_FF // 2)
    dzf, d_conv = ffn_act_bwd(zf, w["ffn_conv_w"][l], da, f"ffn_act_bwd{l}")
    dh2 = matmul(dzf, w["ffn_w_up"][l], "nt", F32, f"ffn_up_dx{l}", ga=2, gb=N_CHIPS, tk=_UP_COLS, tn=D_MODEL)
    d_up = matmul(h2, dzf, "tn", BF16, f"ffn_up_dw{l}", gb=2, go=N_CHIPS, tn=_UP_COLS)
    d_conv = d_conv.transpose(1, 0, 2).reshape(3, 2 * D_FF)
    return dh2, dict(ffn_w_down=d_down, ffn_conv_w=d_conv, ffn_w_up=d_up)


def mixer0_fwd(h, w):
    z = matmul(h, w["ab_w_in"], "nn", BF16, "ab_in", gb=N_CHIPS)
    ya = gconv_fwd(z, w["a_conv_w"])
    yb = pool_fwd(z, w["b_mix_w"], w["b_scale"])
    ycat = jnp.concatenate([ya, yb], axis=1)
    y = matmul(ycat, w["ab_w_out"], "nn", BF16, "ab_out", tn=D_MODEL)
    return y, (z, ycat)


def mixer0_bwd(dy, h, saved, w):
    z, ycat = saved
    grads = {}
    dycat = matmul(dy, w["ab_w_out"], "nt", BF16, "ab_out_dx")
    grads["ab_w_out"] = matmul(ycat, dy, "tn", BF16, "ab_out_dw")
    db, dc, da, d_conv = gconv_bwd(z, w["a_conv_w"], dycat)
    dp, d_mix, d_scale = pool_bwd(z, w["b_mix_w"], w["b_scale"], dycat)
    dz = jnp.concatenate([db, dc, da, dp], axis=1)
    dh = matmul(dz, w["ab_w_in"], "nt", F32, "ab_in_dx", gb=N_CHIPS, tn=D_MODEL)
    grads["ab_w_in"] = matmul(h, dz, "tn", BF16, "ab_in_dw", go=N_CHIPS)
    grads.update(a_conv_w=d_conv, b_mix_w=d_mix, b_scale=d_scale)
    return dh, grads


def mixer1_fwd(h, ropes, w):
    cs, s1, s2 = ropes
    z = matmul(h, w["cd_w_in"], "nn", BF16, "cd_in")
    bs_t = jnp.pad(w["d_b_s"].T, ((0, 0), (0, LANES - D_GROUPS)))
    qh, kh, vh = mla_pre_fwd(z, w["c_q_norm_g"], w["c_kv_norm_g"], w["c_w_uq"], w["c_w_uk"], w["c_w_uv"], cs, s1, s2)
    oh = attn_fwd(qh, kh, vh)
    yd = sgu_fwd(z, w["d_ln_g"], w["d_ln_b"], w["d_w_s"], bs_t)
    ycat = jnp.concatenate([oh, yd], axis=1)
    y = matmul(ycat, w["cd_w_out"], "nn", BF16, "cd_out", tn=D_MODEL)
    return y, (z, bs_t, qh, kh, vh, oh, ycat)


def mixer1_bwd(dy, h, saved, ropes, w):
    cs, s1, s2 = ropes
    z, bs_t, qh, kh, vh, oh, ycat = saved
    grads = {}
    dycat = matmul(dy, w["cd_w_out"], "nt", BF16, "cd_out_dx")
    grads["cd_w_out"] = matmul(ycat, dy, "tn", F32, "cd_out_dw")
    dqh, dkh, dvh = attn_bwd(qh, kh, vh, oh, dycat, 0)
    dzq, d_uq, d_uk, d_uv, d_gq, d_gkv = mla_pre_bwd(
        z, w["c_q_norm_g"], w["c_kv_norm_g"], w["c_w_uq"], w["c_w_uk"], w["c_w_uv"], cs, s1, s2, dqh, dkh, dvh)
    dzu, dzv, d_ws, d_bs, d_lg, d_lb = sgu_bwd(z, w["d_ln_g"], w["d_ln_b"], w["d_w_s"], bs_t, dycat, _HW // _DW)
    dz = jnp.concatenate([dzq, dzu, dzv], axis=1)
    dh = matmul(dz, w["cd_w_in"], "nt", F32, "cd_in_dx", tn=D_MODEL)
    grads["cd_w_in"] = matmul(h, dz, "tn", F32, "cd_in_dw")
    grads.update(c_w_uq=d_uq, c_w_uk=d_uk, c_w_uv=d_uv, c_q_norm_g=d_gq, c_kv_norm_g=d_gkv, d_w_s=d_ws,
                 d_b_s=d_bs[:, :D_GROUPS].T, d_ln_g=d_lg, d_ln_b=d_lb)
    return dh, grads


class StepHooks:
    def weights(self, stage, after):
        pass

    def gradients(self, stage, grads, after):
        return 0.0


def run_step(x, tgt, mod, ropes, w, hooks):
    sh1a, sc1a, g1a, sh2a, sc2a, g2a = _mods(mod[0:1])
    sh1b, sc1b, g1b, sh2b, sc2b, g2b = _mods(mod[1:2])
    n1, n2 = w["norm1_g"], w["norm2_g"]

    hooks.weights("mix0", mod)
    h0 = modnorm_fwd(x, n1[0:1], sc1a, sh1a, "modnorm_0")
    y0, mix0 = mixer0_fwd(h0, w)
    x1, h1 = resid_modnorm_fwd(x, y0, g1a, n2[0:1], sc2a, sh2a, "resid_modnorm_1")
    hooks.weights("up0", x1)
    f0, ffn0 = ffn_fwd(h1, w, 0, lambda act: hooks.weights("down0", act))
    x2, h2 = resid_modnorm_fwd(x1, f0, g2a, n1[1:2], sc1b, sh1b, "resid_modnorm_2")
    hooks.weights("mix1", x2)
    y1, mix1 = mixer1_fwd(h2, ropes, w)
    x3, h3 = resid_modnorm_fwd(x2, y1, g1b, n2[1:2], sc2b, sh2b, "resid_modnorm_3")
    hooks.weights("ffn1", x3)
    f1, ffn1 = ffn_fwd(h3, w, 1)
    dres, d_final, loss, df1, dg2b = final_fused(x3, f1, g2b, w["final_norm_g"], tgt)

    dh3, gf1 = ffn_bwd(df1, h3, ffn1, w, 1)
    tok = hooks.gradients("ffn1", gf1, dh3)
    dres, dsh2b, dsc2b, dn2b, dy1, dg1b = norm_gate_bwd(x3, dh3, n2[1:2], sc2b, dres, y1, g1b + tok, "norm_gate_bwd_3")
    dh2, gm1 = mixer1_bwd(dy1, h2, mix1, ropes, w)
    tok = hooks.gradients("mix1", gm1, dh2)
    dres, dsh1b, dsc1b, dn1b, df0, dg2a = norm_gate_bwd(x2, dh2, n1[1:2], sc1b, dres, f0, g2a + tok, "norm_gate_bwd_2")
    dh1, gf0 = ffn_bwd(df0, h1, ffn0, w, 0)
    tok = hooks.gradients("ffn0", gf0, dh1)
    dres, dsh2a, dsc2a, dn2a, dy0, dg1a = norm_gate_bwd(x1, dh1, n2[0:1], sc2a, dres, y0, g1a + tok, "norm_gate_bwd_1")
    dh0, gm0 = mixer0_bwd(dy0, h0, mix0, w)
    tok = hooks.gradients("mix0", gm0, dh0)
    grad_x, dsh1a, dsc1a, dn1a = norm_bwd(x, dh0, n1[0:1], sc1a + tok, dres, "norm_bwd_0")

    dmod = jnp.concatenate([jnp.concatenate([dsh1a, dsc1a, dg1a, dsh2a, dsc2a, dg2a], axis=1),
                            jnp.concatenate([dsh1b, dsc1b, dg1b, dsh2b, dsc2b, dg2b], axis=1)], axis=0)
    norms = dict(norm1_g=jnp.concatenate([dn1a, dn1b], axis=0), norm2_g=jnp.concatenate([dn2a, dn2b], axis=0),
                 final_norm_g=d_final)
    return loss, grad_x, dmod, dict(mix0=gm0, ffn0=gf0, mix1=gm1, ffn1=gf1, norms=norms)


def merge_grads(by_stage):
    grads = {**by_stage["mix0"], **by_stage["mix1"], **by_stage["norms"]}
    for k in ("ffn_w_down", "ffn_w_up"):
        grads[k] = [by_stage["ffn0"][k], by_stage["ffn1"][k]]
    grads["ffn_conv_w"] = jnp.stack([by_stage["ffn0"]["ffn_conv_w"], by_stage["ffn1"]["ffn_conv_w"]])
    return grads


def local_step(x, tgt, mod, ropes, w):
    loss, grad_x, dmod, by_stage = run_step(x, tgt, mod, ropes, w, StepHooks())
    return loss, grad_x, dmod, merge_grads(by_stage)


_WEIGHTS = ("ada_w", "ada_b", "norm1_g", "norm2_g", "ab_w_in", "a_conv_w", "b_mix_w", "b_scale", "ab_w_out", "cd_w_in",
            "c_q_norm_g", "c_w_uq", "c_kv_norm_g", "c_w_ukv", "d_ln_g", "d_ln_b", "d_w_s", "d_b_s", "cd_w_out",
            "ffn_w_up", "ffn_conv_w", "ffn_w_down", "final_norm_g")
_INPUTS = ("x", "c", "positions") + _WEIGHTS + ("loss_target",) + tuple("m_" + n for n in _WEIGHTS) + tuple(
    "v_" + n for n in _WEIGHTS)

def _pack_rows(parts, rows, dtype):
    flat = jnp.concatenate([p.reshape(-1).astype(dtype) for p in parts])
    return jnp.pad(flat, (0, rows * LANES - flat.shape[0])).reshape(rows, LANES)


def _rows_major(w):
    r, c = w.shape
    return w.reshape(N_CHIPS, r // N_CHIPS, c)


def start_gather(shards, tag, after=()):
    lands = [_sds((N_CHIPS,) + s.shape, s.dtype) for s in shards]
    return split_start("gather_start_" + tag, GATHER, shards, lands, after)


def finish_gather(handle, chip, tag, after):
    shards, lands = split_wait("gather_wait_" + tag, GATHER, handle, after)
    lands = forward_halves(lands, "gather_forward_" + tag)
    return [lax.dynamic_update_index_in_dim(o, s, chip, 0) for o, s in zip(lands, shards)]


def start_reduce(gs, core, tag):
    recv = swap_halves(gs, "swap_halves_" + tag)
    pairs = pair_sums(gs, recv, core, "pair_sums_" + tag)
    lands = [_sds((N_CHIPS - 1,) + p.shape[1:], p.dtype) for p in pairs]
    return split_start("exchange_start_" + tag, EXCHANGE, pairs, lands)


def finish_reduce(handle, chip, core, tag, after):
    pairs, others = split_wait("exchange_wait_" + tag, EXCHANGE, handle, after)
    halves = chip_sums(pairs, others, chip, core, "chip_sums_" + tag)
    full = join_halves(halves, "join_halves_" + tag)
    return [f.reshape(f.shape[1] * 2, f.shape[2]) for f in full]


_SMALL_SHARDED = (("a_conv_w", (3, 128), 1), ("c_q_norm_g", (1, 64), 1), ("d_ln_g", (1, 128), 1), ("d_ln_b", (1, 128), 1),
                  ("ffn_conv_w", (2, 3, 2 * D_FF // N_CHIPS), 2))
_SMALL_GRADS = (("norm1_g", (2, D_MODEL)), ("norm2_g", (2, D_MODEL)), ("b_mix_w", (4, 128, 128)), ("b_scale", (1, 512)),
                ("c_kv_norm_g", (1, 128)), ("d_w_s", (4, 128, 128)), ("d_b_s", (4, 128)), ("final_norm_g", (1, D_MODEL)),
                ("a_conv_w", (3, 512)), ("c_q_norm_g", (1, 256)), ("d_ln_g", (1, 512)), ("d_ln_b", (1, 512)),
                ("ffn_conv_w", (2, 3, 2 * D_FF)))


def _size(shape):
    n = 1
    for d in shape:
        n *= d
    return n


def kernel(x, c, positions, ada_w, ada_b, norm1_g, norm2_g, ab_w_in, a_conv_w, b_mix_w, b_scale, ab_w_out, cd_w_in, c_q_norm_g, c_w_uq, c_kv_norm_g, c_w_ukv, d_ln_g, d_ln_b, d_w_s, d_b_s, cd_w_out, ffn_w_up, ffn_conv_w, ffn_w_down, final_norm_g, loss_target, m_ada_w, m_ada_b, m_norm1_g, m_norm2_g, m_ab_w_in, m_a_conv_w, m_b_mix_w, m_b_scale, m_ab_w_out, m_cd_w_in, m_c_q_norm_g, m_c_w_uq, m_c_kv_norm_g, m_c_w_ukv, m_d_ln_g, m_d_ln_b, m_d_w_s, m_d_b_s, m_cd_w_out, m_ffn_w_up, m_ffn_conv_w, m_ffn_w_down, m_final_norm_g, v_ada_w, v_ada_b, v_norm1_g, v_norm2_g, v_ab_w_in, v_a_conv_w, v_b_mix_w, v_b_scale, v_ab_w_out, v_cd_w_in, v_c_q_norm_g, v_c_w_uq, v_c_kv_norm_g, v_c_w_ukv, v_d_ln_g, v_d_ln_b, v_d_w_s, v_d_b_s, v_cd_w_out, v_ffn_w_up, v_ffn_conv_w, v_ffn_w_down, v_final_norm_g):
    args = (x, c, positions, ada_w, ada_b, norm1_g, norm2_g, ab_w_in, a_conv_w, b_mix_w, b_scale, ab_w_out, cd_w_in, c_q_norm_g, c_w_uq, c_kv_norm_g, c_w_ukv, d_ln_g, d_ln_b, d_w_s, d_b_s, cd_w_out, ffn_w_up, ffn_conv_w, ffn_w_down, final_norm_g, loss_target, m_ada_w, m_ada_b, m_norm1_g, m_norm2_g, m_ab_w_in, m_a_conv_w, m_b_mix_w, m_b_scale, m_ab_w_out, m_cd_w_in, m_c_q_norm_g, m_c_w_uq, m_c_kv_norm_g, m_c_w_ukv, m_d_ln_g, m_d_ln_b, m_d_w_s, m_d_b_s, m_cd_w_out, m_ffn_w_up, m_ffn_conv_w, m_ffn_w_down, m_final_norm_g, v_ada_w, v_ada_b, v_norm1_g, v_norm2_g, v_ab_w_in, v_a_conv_w, v_b_mix_w, v_b_scale, v_ab_w_out, v_cd_w_in, v_c_q_norm_g, v_c_w_uq, v_c_kv_norm_g, v_c_w_ukv, v_d_ln_g, v_d_ln_b, v_d_w_s, v_d_b_s, v_cd_w_out, v_ffn_w_up, v_ffn_conv_w, v_ffn_w_down, v_final_norm_g)
    a = dict(zip(_INPUTS, args, strict=True))
    xi, yi, ci = _place()
    chip = 2 * xi + yi
    dev = 4 * xi + 2 * yi + ci
    x = a["x"][0]
    tgt = a["loss_target"][0]

    bf = lambda t: t.astype(BF16)
    mix0_handle, tok = start_gather([bf(a["ab_w_in"][0]), bf(a["ab_w_out"][0])], "mix0")
    up0_16, down0_16, up1_16, down1_16 = [bf(a[n][l]) for l in (0, 1) for n in ("ffn_w_up", "ffn_w_down")]
    mix1_16 = [bf(a[n][0]) for n in ("cd_w_in", "c_w_uq", "c_w_ukv", "cd_w_out")]

    small_parts = [a["c"] + tok] + [a[n] for n, _, _ in _SMALL_SHARDED]
    rows1 = -(-sum(p.size for p in small_parts) // LANES // 8) * 8
    g1 = all_gather8(_pack_rows(small_parts, rows1, F32), "gather_small",
                     [up0_16, down0_16, up1_16, down1_16, mix1_16[0], mix1_16[3]]).reshape(N_DEV, rows1 * LANES)
    c_all = g1[:, :D_MODEL]
    per_chip = g1[0::2]
    small_full = {}
    off = D_MODEL
    for n, shp, axis in _SMALL_SHARDED:
        piece = per_chip[:, off:off + _size(shp)].reshape((N_CHIPS,) + shp)
        small_full[n] = jnp.concatenate([piece[k] for k in range(N_CHIPS)], axis=axis)
        off += _size(shp)

    merge = lambda t: t.reshape(t.shape[0] * t.shape[1], t.shape[2])
    w = dict(norm1_g=a["norm1_g"], norm2_g=a["norm2_g"], b_mix_w=a["b_mix_w"][0], b_scale=a["b_scale"],
             c_kv_norm_g=a["c_kv_norm_g"], d_w_s=a["d_w_s"][0], d_b_s=a["d_b_s"][0],
             final_norm_g=a["final_norm_g"].reshape(1, D_MODEL), **small_full)

    ncol = N_MOD * D_MODEL // N_CHIPS
    ada_b_mine = lax.dynamic_slice_in_dim(a["ada_b"], chip * ncol, ncol, axis=1)
    mod_cols = ada_mod(c_all, a["ada_w"], ada_b_mine)
    g2_rows = all_gather8(mod_cols.reshape(-1, LANES), "gather_mod")
    g2 = g2_rows.reshape(N_DEV, 2, N_DEV, ncol)
    mod = lax.dynamic_index_in_dim(g2[0::2], dev, axis=2, keepdims=False)
    mod = mod.transpose(1, 0, 2).reshape(2, N_MOD * D_MODEL)

    late = [g2_rows]
    up0_handle, tok_a = start_gather([up0_16], "up0", late)
    down0_handle, tok_b = start_gather([down0_16], "down0", late)
    mix1_handle, tok_c = start_gather(mix1_16, "mix1", late)
    ffn1_handle, tok_d = start_gather([up1_16, down1_16], "ffn1", late)
    mod = mod + (tok_a + tok_b + tok_c + tok_d)

    ropes = rope_tables(a["positions"][0])
    cm16 = lambda t: chip_major(t).astype(BF16)
    w.update(ffn_w_up=[None, None], ffn_w_down=[None, None])
    handles = dict(mix0=mix0_handle, up0=up0_handle, down0=down0_handle, mix1=mix1_handle, ffn1=ffn1_handle)
    reducing, reduced = {}, {}

    class Hooks(StepHooks):
        def weights(self, stage, after):
            got = finish_gather(handles[stage], chip, stage, after)
            if stage == "mix0":
                w.update(ab_w_in=got[0], ab_w_out=merge(got[1]))
            elif stage == "up0":
                w["ffn_w_up"][0] = got[0]
            elif stage == "down0":
                w["ffn_w_down"][0] = merge(got[0])
            elif stage == "mix1":
                cd_in, uq, ukv, cd_out = got
                w.update(prepare_weights(dict(cd_w_in=from_chip_major(cd_in), c_w_uq=from_chip_major(uq),
                                              c_w_ukv=from_chip_major(ukv), cd_w_out=merge(cd_out))))
            else:
                w["ffn_w_up"][1], w["ffn_w_down"][1] = got[0], merge(got[1])

        def gradients(self, stage, grads, after):
            if stage in ("ffn0", "ffn1"):
                parts = [grads["ffn_w_up"], _rows_major(grads["ffn_w_down"])]
            elif stage == "mix1":
                grads.update(unprepare_grads(grads))
                parts = [cm16(grads["cd_w_in"]), cm16(grads["c_w_uq"]), cm16(grads["c_w_ukv"]),
                         _rows_major(grads["cd_w_out"]).astype(BF16)]
            else:
                parts = [grads["ab_w_in"], _rows_major(grads["ab_w_out"])]
            reducing[stage], tok = start_reduce(parts, ci, stage)
            before = {"mix1": "ffn1", "ffn0": "mix1", "mix0": "ffn0"}.get(stage)
            if before is not None:
                reduced[before] = finish_reduce(reducing[before], chip, ci, before, after)
            return tok

    loss, grad_x, dmod, by_stage = run_step(x, tgt, mod, ropes, w, Hooks())
    grads = merge_grads(by_stage)

    parts3 = [dmod] + [grads[n] for n, _ in _SMALL_GRADS] + [loss[0, 0]]
    rows3 = -(-sum(p.size for p in parts3) // LANES // 8) * 8
    small_handle, _ = split_start("small_grads_start", EVERYONE, [_pack_rows(parts3, rows3, F32)],
                                  [_sds((N_DEV, rows3, LANES))])
    red_up1, red_down1 = reduced["ffn1"]
    red_cd_in, red_uq, red_ukv, red_cd_out = reduced["mix1"]
    red_up0, red_down0 = reduced["ffn0"]
    out_grads = dict(cd_w_in=red_cd_in, c_w_uq=red_uq, c_w_ukv=red_ukv, cd_w_out=red_cd_out)
    per_layer = dict(ffn_w_up=(red_up0, red_up1), ffn_w_down=(red_down0, red_down1))
    updates = {}

    def update(n):
        if n in per_layer:
            updates[n] = adamw_layers(a[n], *per_layer[n], a["m_" + n], a["v_" + n], "adamw_" + n)
        else:
            updates[n] = adamw(a[n], out_grads[n].reshape(a[n].shape), a["m_" + n], a["v_" + n], "adamw_" + n)

    early =("ffn_w_up", "ffn_w_down", "cd_w_in", "c_w_uq", "c_w_ukv", "cd_w_out")
    for n in early:
        update(n)
    (mine,), (landed,) = split_wait("small_grads_wait", EVERYONE, small_handle, [updates[n][1] for n in early])
    g3 = lax.dynamic_update_index_in_dim(landed, mine, dev, 0)
    summed = sum8(g3).reshape(-1)
    nmod = 2 * N_MOD * D_MODEL
    out_grads["ada_b"] = summed[:nmod].reshape(2, N_MOD * D_MODEL)
    off = nmod
    for n, shp in _SMALL_GRADS:
        out_grads[n] = summed[off:off + _size(shp)].reshape(shp)
        off += _size(shp)
    loss = summed[off]
    for n, shp, axis in _SMALL_SHARDED:
        width = out_grads[n].shape[-1] // N_CHIPS
        out_grads[n] = lax.dynamic_slice_in_dim(out_grads[n], chip * width, width, axis=out_grads[n].ndim - 1)
    dmod_all = g3.reshape(N_DEV, rows3 * LANES)[:, :nmod].reshape(N_DEV, 2, N_MOD * D_MODEL)
    dmod_mine = lax.dynamic_slice_in_dim(dmod_all, chip * ncol, ncol, axis=2).transpose(1, 0, 2)
    updates["ada_w"] = adamw_ada(a["ada_w"], c_all, dmod_mine, a["m_ada_w"], a["v_ada_w"])

    red_in0, red_out0 = finish_reduce(reducing["mix0"], chip, ci, "mix0", updates["ada_w"][1])
    out_grads.update(ab_w_in=red_in0, ab_w_out=red_out0)

    for n in ("ab_w_in", "ab_w_out"):
        update(n)
    small = [n for n in _WEIGHTS if n not in updates]
    for n, res in zip(small, adamw_small([a[n] for n in small], [out_grads[n].reshape(a[n].shape) for n in small],
                                         [a["m_" + n] for n in small], [a["v_" + n] for n in small])):
        updates[n] = res
    return (loss, grad_x[None], *[updates[n][i] for i in range(4) for n in _WEIGHTS])
```

```python
import functools

import jax
import jax.numpy as jnp
from jax import lax
from jax.experimental import pallas as pl
from jax.experimental.pallas import tpu as pltpu

F32 = jnp.float32
BF16 = jnp.bfloat16
EPS = 1e-6
D_MODEL = 1024
N_MOD = 6
A_WIDTH = 512
B_GROUPS = 4
POOL_WINDOWS = (2, 4, 8, 16)
C_HEADS = 8
C_NOPE = 64
C_ROPE = 32
C_V = 64
C_Q_RANK = 256
C_KV_RANK = 128
HEAD_PAD = 128
ROPE_THETA = 10000.0
D_GROUPS = 4
D_CHUNK = 128
D_FF = 2816
FF_UNIT = 128
ADAM_LR = 0.001
ADAM_B1 = 0.9
ADAM_B2 = 0.999
ADAM_EPS = 1e-08
ADAM_WD = 0.01
ADAM_STEP = 10
N_CHIPS = 4
N_DEV = 8
LANES = 128
VMEM_BIG = 56 * 1024 * 1024
MESH = pl.DeviceIdType.MESH


def _sds(shape, dtype=F32):
    return jax.ShapeDtypeStruct(tuple(shape), dtype)


def _tile(n, cap, mult=128):
    if n <= cap:
        return n
    best = None
    for t in range(mult, cap + 1, mult):
        if n % t == 0:
            best = t
    assert best is not None, (n, cap, mult)
    return best


def _params(dims=None, vmem=None):
    return pltpu.CompilerParams(dimension_semantics=dims, vmem_limit_bytes=vmem)


def _shift_down(v, k):
    r = pltpu.roll(v, k, axis=0)
    t = lax.broadcasted_iota(jnp.int32, v.shape, 0)
    return jnp.where(t >= k, r, 0.0)


def _shift_up(v, k):
    n = v.shape[0]
    r = pltpu.roll(v, n - k, axis=0)
    t = lax.broadcasted_iota(jnp.int32, v.shape, 0)
    return jnp.where(t < n - k, r, 0.0)


def _sigmoid(v):
    return 1.0 / (1.0 + jnp.exp(-v))


_GELU_C = 0.7978845608028654
_GELU_A = 0.044715


def _gelu(v):
    return 0.5 * v * (1.0 + jnp.tanh(_GELU_C * (v + _GELU_A * v * v * v)))


def _gelu_grad(v):
    th = jnp.tanh(_GELU_C * (v + _GELU_A * v * v * v))
    return 0.5 * (1.0 + th) + 0.5 * v * (1.0 - th * th) * _GELU_C * (1.0 + 3.0 * _GELU_A * v * v)


_NN = (((1,), (0,)), ((), ()))
_NT = (((1,), (1,)), ((), ()))
_TN = (((0,), (0,)), ((), ()))


def _dot(a, b, dims=_NN):
    return lax.dot_general(a, b, dims, preferred_element_type=F32)


def _logical(t, groups):
    return (t.shape[-2], t.shape[-1] * groups)


def _block(tr, tc, groups, cols, where):
    if groups == 1:
        return pl.BlockSpec((tr, tc), where)
    per = cols // groups // tc

    def index(i, j, s):
        r, c = where(i, j, s)
        return (c // per, r, c % per)

    return pl.BlockSpec((None, tr, tc), index)


def matmul(a, b, mode, out_dtype, name, ga=1, gb=1, go=1, tm=None, tn=None, tk=None):
    (ar, ac), (br, bc) = _logical(a, ga), _logical(b, gb)
    if mode == "nn":
        m, k, n = ar, ac, bc
        a_col, b_col = "k", "n"
    elif mode == "nt":
        m, k, n = ar, ac, br
        a_col, b_col = "k", "k"
    else:
        k, m, n = ar, ac, bc
        a_col, b_col = "m", "n"
    limit = {"m": m, "n": n // go, "k": k}
    limit[a_col] = min(limit[a_col], ac // ga)
    limit[b_col] = min(limit[b_col], bc // gb)
    tm = tm or _tile(limit["m"], 1024, 128 if mode == "tn" else 16)
    tn = tn or _tile(limit["n"], 512)
    tk = tk or _tile(limit["k"], 2048, 16 if mode == "tn" else 128)
    nk = k // tk
    if mode == "nn":
        a_spec = _block(tm, tk, ga, ac, lambda i, j, s: (i, s))
        b_spec = _block(tk, tn, gb, bc, lambda i, j, s: (s, j))
        dims = _NN
    elif mode == "nt":
        a_spec = _block(tm, tk, ga, ac, lambda i, j, s: (i, s))
        b_spec = _block(tn, tk, gb, bc, lambda i, j, s: (j, s))
        dims = _NT
    else:
        a_spec = _block(tk, tm, ga, ac, lambda i, j, s: (s, i))
        b_spec = _block(tk, tn, gb, bc, lambda i, j, s: (s, j))
        dims = _TN
    o_spec = _block(tm, tn, go, n, lambda i, j, s: (i, j))
    out_shape = _sds((m, n), out_dtype) if go == 1 else _sds((go, m, n // go), out_dtype)

    def body(a_ref, b_ref, o_ref, acc_ref):
        s = pl.program_id(2)

        @pl.when(s == 0)
        def _():
            acc_ref[...] = jnp.zeros_like(acc_ref)

        acc_ref[...] += _dot(a_ref[...], b_ref[...], dims)

        @pl.when(s == nk - 1)
        def _():
            o_ref[...] = acc_ref[...].astype(o_ref.dtype)

    return pl.pallas_call(
        body, name=name, out_shape=out_shape, grid=(m // tm, n // tn, nk),
        in_specs=[a_spec, b_spec], out_specs=o_spec,
        scratch_shapes=[pltpu.VMEM((tm, tn), F32)],
        compiler_params=_params(("parallel", "parallel", "arbitrary"), VMEM_BIG),
    )(a, b)


def _rows(tm, n):
    return pl.BlockSpec((tm, n), lambda i: (i, 0))


def _vec(n):
    return pl.BlockSpec((1, n), lambda i: (0, 0))


def modnorm_fwd(x, g, sc, sh, name):
    s, d = x.shape
    tm = _tile(s, 256, 8)

    def body(x_ref, g_ref, sc_ref, sh_ref, o_ref):
        xv = x_ref[...]
        r = lax.rsqrt(jnp.mean(xv * xv, axis=-1, keepdims=True) + EPS)
        o_ref[...] = ((xv * r) * g_ref[...] * (1.0 + sc_ref[...]) + sh_ref[...]).astype(BF16)

    return pl.pallas_call(
        body, name=name, out_shape=_sds((s, d), BF16), grid=(s // tm,),
        in_specs=[_rows(tm, d), _vec(d), _vec(d), _vec(d)], out_specs=_rows(tm, d),
        compiler_params=_params(("parallel",)),
    )(x, g, sc, sh)


def norm_bwd(x, dh, g, sc, dres, name):
    s, d = x.shape
    tm = _tile(s, 256, 8)
    nsteps = s // tm

    def body(x_ref, dh_ref, g_ref, sc_ref, dr_ref, dx_ref, dsh_ref, dsc_ref, dg_ref, a2_ref):
        i = pl.program_id(0)

        @pl.when(i == 0)
        def _():
            dsh_ref[...] = jnp.zeros_like(dsh_ref)
            a2_ref[...] = jnp.zeros_like(a2_ref)

        xv = x_ref[...]
        dh = dh_ref[...]
        r = lax.rsqrt(jnp.mean(xv * xv, axis=-1, keepdims=True) + EPS)
        xh = xv * r
        dsh_ref[...] += jnp.sum(dh, axis=0, keepdims=True)
        a2_ref[...] += jnp.sum(dh * xh, axis=0, keepdims=True)
        dxh = dh * (g_ref[...] * (1.0 + sc_ref[...]))
        dx = r * (dxh - xh * jnp.mean(dxh * xh, axis=-1, keepdims=True))
        dx_ref[...] = dr_ref[...] + dx

        @pl.when(i == nsteps - 1)
        def _():
            dsc_ref[...] = a2_ref[...] * g_ref[...]
            dg_ref[...] = a2_ref[...] * (1.0 + sc_ref[...])

    return pl.pallas_call(
        body, name=name, out_shape=(_sds((s, d)), _sds((1, d)), _sds((1, d)), _sds((1, d))), grid=(nsteps,),
        in_specs=[_rows(tm, d), _rows(tm, d), _vec(d), _vec(d), _rows(tm, d)],
        out_specs=(_rows(tm, d), _vec(d), _vec(d), _vec(d)),
        scratch_shapes=[pltpu.VMEM((1, d), F32)],
        compiler_params=_params(("arbitrary",)),
    )(x, dh, g, sc, dres)


def resid_modnorm_fwd(x, y, gate, g, sc, sh, name):
    s, d = x.shape
    tm = _tile(s, 256, 8)

    def body(x_ref, y_ref, gate_ref, g_ref, sc_ref, sh_ref, xo_ref, h_ref):
        xv = x_ref[...] + gate_ref[...] * y_ref[...].astype(F32)
        xo_ref[...] = xv
        r = lax.rsqrt(jnp.mean(xv * xv, axis=-1, keepdims=True) + EPS)
        h_ref[...] = ((xv * r) * g_ref[...] * (1.0 + sc_ref[...]) + sh_ref[...]).astype(BF16)

    return pl.pallas_call(
        body, name=name, out_shape=(_sds((s, d)), _sds((s, d), BF16)), grid=(s // tm,),
        in_specs=[_rows(tm, d), _rows(tm, d), _vec(d), _vec(d), _vec(d), _vec(d)], out_specs=(_rows(tm, d), _rows(tm, d)),
        compiler_params=_params(("parallel",)),
    )(x, y, gate, g, sc, sh)


def norm_gate_bwd(x, dh, g, sc, dres, y, gate, name):
    s, d = x.shape
    tm = _tile(s, 256, 8)
    nsteps = s // tm

    def body(x_ref, dh_ref, g_ref, sc_ref, dr_ref, y_ref, gate_ref, dx_ref, dsh_ref, dsc_ref, dg_ref, dy_ref,
             dgate_ref, a2_ref):
        i = pl.program_id(0)

        @pl.when(i == 0)
        def _():
            dsh_ref[...] = jnp.zeros_like(dsh_ref)
            a2_ref[...] = jnp.zeros_like(a2_ref)
            dgate_ref[...] = jnp.zeros_like(dgate_ref)

        xv = x_ref[...]
        dh = dh_ref[...]
        r = lax.rsqrt(jnp.mean(xv * xv, axis=-1, keepdims=True) + EPS)
        xh = xv * r
        dsh_ref[...] += jnp.sum(dh, axis=0, keepdims=True)
        a2_ref[...] += jnp.sum(dh * xh, axis=0, keepdims=True)
        dxh = dh * (g_ref[...] * (1.0 + sc_ref[...]))
        dr = dr_ref[...] + r * (dxh - xh * jnp.mean(dxh * xh, axis=-1, keepdims=True))
        dx_ref[...] = dr
        dy_ref[...] = (dr * gate_ref[...]).astype(BF16)
        dgate_ref[...] += jnp.sum(dr * y_ref[...].astype(F32), axis=0, keepdims=True)

        @pl.when(i == nsteps - 1)
        def _():
            dsc_ref[...] = a2_ref[...] * g_ref[...]
            dg_ref[...] = a2_ref[...] * (1.0 + sc_ref[...])

    vec = _sds((1, d))
    return pl.pallas_call(
        body, name=name, out_shape=(_sds((s, d)), vec, vec, vec, _sds((s, d), BF16), vec), grid=(nsteps,),
        in_specs=[_rows(tm, d), _rows(tm, d), _vec(d), _vec(d), _rows(tm, d), _rows(tm, d), _vec(d)],
        out_specs=(_rows(tm, d), _vec(d), _vec(d), _vec(d), _rows(tm, d), _vec(d)),
        scratch_shapes=[pltpu.VMEM((1, d), F32)],
        compiler_params=_params(("arbitrary",)),
    )(x, dh, g, sc, dres, y, gate)


def final_fused(x, f, gate, g, tgt):
    s, d = x.shape
    tm = _tile(s, 256, 8)

    def body(x_ref, f_ref, gate_ref, g_ref, t_ref, dx_ref, dg_ref, loss_ref, df_ref, dgate_ref):
        @pl.when(pl.program_id(0) == 0)
        def _():
            dg_ref[...] = jnp.zeros_like(dg_ref)
            loss_ref[...] = jnp.zeros_like(loss_ref)
            dgate_ref[...] = jnp.zeros_like(dgate_ref)

        fv, gatev, gv = f_ref[...].astype(F32), gate_ref[...], g_ref[...]
        xv = x_ref[...] + gatev * fv
        r = lax.rsqrt(jnp.mean(xv * xv, axis=-1, keepdims=True) + EPS)
        xh = xv * r
        e = xh * gv - t_ref[...]
        row = jnp.sum(e * e, axis=-1, keepdims=True) * (0.5 / d)
        loss_ref[...] += jnp.sum(row, axis=0, keepdims=True)
        dy = e * (1.0 / d)
        dg_ref[...] += jnp.sum(dy * xh, axis=0, keepdims=True)
        dxh = dy * gv
        dx = r * (dxh - xh * jnp.mean(dxh * xh, axis=-1, keepdims=True))
        dx_ref[...] = dx
        df_ref[...] = (dx * gatev).astype(BF16)
        dgate_ref[...] += jnp.sum(dx * fv, axis=0, keepdims=True)

    vec = _sds((1, d))
    return pl.pallas_call(
        body, name="final_fused", out_shape=(_sds((s, d)), vec, _sds((1, LANES)), _sds((s, d), BF16), vec),
        grid=(s // tm,),
        in_specs=[_rows(tm, d), _rows(tm, d), _vec(d), _vec(d), _rows(tm, d)],
        out_specs=(_rows(tm, d), _vec(d), _vec(LANES), _rows(tm, d), _vec(d)),
        compiler_params=_params(("arbitrary",)),
    )(x, f, gate, g, tgt)


def _taps(v):
    return _shift_down(v, 2), _shift_down(v, 1), v


def _conv3_taps(taps, w):
    return w[0:1, :] * taps[0] + w[1:2, :] * taps[1] + w[2:3, :] * taps[2]


def _conv3(v, w):
    return _conv3_taps(_taps(v), w)


def _conv3_t(dv, w):
    return w[0:1, :] * _shift_up(dv, 2) + w[1:2, :] * _shift_up(dv, 1) + w[2:3, :] * dv


def _conv3_dw_taps(dv, taps):
    return jnp.concatenate([jnp.sum(dv * t, axis=0, keepdims=True) for t in taps], axis=0)


def _conv3_dw(dv, v):
    return _conv3_dw_taps(dv, _taps(v))


def gconv_fwd(z, conv_w):
    s = z.shape[0]
    nb = A_WIDTH // LANES

    def body(b_ref, c_ref, a_ref, w_ref, o_ref):
        b, c, a = b_ref[...].astype(F32), c_ref[...].astype(F32), a_ref[...].astype(F32)
        o_ref[...] = (b * _conv3(c * a, w_ref[...])).astype(BF16)

    col = lambda off: pl.BlockSpec((s, LANES), lambda j: (0, off + j))
    return pl.pallas_call(
        body, name="gconv_fwd", out_shape=_sds((s, A_WIDTH), BF16), grid=(nb,),
        in_specs=[col(0), col(nb), col(2 * nb), pl.BlockSpec((3, LANES), lambda j: (0, j))],
        out_specs=pl.BlockSpec((s, LANES), lambda j: (0, j)),
        compiler_params=_params(("parallel",), VMEM_BIG),
    )(z, z, z, conv_w)


def gconv_bwd(z, conv_w, dycat):
    s = z.shape[0]
    nb = A_WIDTH // LANES

    def body(b_ref, c_ref, a_ref, w_ref, dy_ref, db_ref, dc_ref, da_ref, dw_ref):
        c, a, w, dy = c_ref[...].astype(F32), a_ref[...].astype(F32), w_ref[...], dy_ref[...].astype(F32)
        ca = c * a
        db_ref[...] = (dy * _conv3(ca, w)).astype(BF16)
        dconv = dy * b_ref[...].astype(F32)
        dw_ref[...] = _conv3_dw(dconv, ca)
        dca = _conv3_t(dconv, w)
        dc_ref[...] = (dca * a).astype(BF16)
        da_ref[...] = (dca * c).astype(BF16)

    col = lambda off: pl.BlockSpec((s, LANES), lambda j: (0, off + j))
    wspec = pl.BlockSpec((3, LANES), lambda j: (0, j))
    part = _sds((s, A_WIDTH), BF16)
    return pl.pallas_call(
        body, name="gconv_bwd", out_shape=(part, part, part, _sds((3, A_WIDTH))), grid=(nb,),
        in_specs=[col(0), col(nb), col(2 * nb), wspec, col(0)],
        out_specs=(col(0), col(0), col(0), wspec),
        compiler_params=_params(("parallel",), VMEM_BIG),
    )(z, z, z, conv_w, dycat)


def _pool_counts(s, w):
    t = lax.broadcasted_iota(jnp.int32, (s, 1), 0)
    return jnp.minimum(t + 1, w).astype(F32)


def _pooled(p, levels):
    acc = p
    for lv in range(levels):
        acc = acc + _shift_down(acc, 2 ** lv)
    return acc / _pool_counts(p.shape[0], 2 ** levels) - p


_B_WIDTH = B_GROUPS * LANES


def pool_fwd(z, mix_w, scale):
    s = z.shape[0]

    def body(p_ref, m_ref, sc_ref, o_ref):
        for g in range(B_GROUPS):
            cols = slice(g * LANES, (g + 1) * LANES)
            pooled = _pooled(p_ref[:, cols].astype(F32), g + 1)
            y = _dot(pooled.astype(BF16), m_ref[g].astype(BF16))
            o_ref[:, cols] = (y * sc_ref[:, cols]).astype(BF16)

    return pl.pallas_call(
        body, name="pool_fwd", out_shape=_sds((s, _B_WIDTH), BF16), grid=(1,),
        in_specs=[pl.BlockSpec((s, _B_WIDTH), lambda i: (0, 3 * A_WIDTH // _B_WIDTH)),
                  pl.BlockSpec((B_GROUPS, LANES, LANES), lambda i: (0, 0, 0)), pl.BlockSpec((1, _B_WIDTH), lambda i: (0, 0))],
        out_specs=pl.BlockSpec((s, _B_WIDTH), lambda i: (0, 0)),
        compiler_params=_params(("arbitrary",), VMEM_BIG),
    )(z, mix_w, scale)


def pool_bwd(z, mix_w, scale, dycat):
    s = z.shape[0]

    def body(p_ref, m_ref, sc_ref, dy_ref, dp_ref, dm_ref, dsc_ref):
        for g in range(B_GROUPS):
            cols = slice(g * LANES, (g + 1) * LANES)
            pooled = _pooled(p_ref[:, cols].astype(F32), g + 1)
            mw = m_ref[g].astype(BF16)
            pb = pooled.astype(BF16)
            dy = dy_ref[:, cols].astype(F32)
            dsc_ref[:, cols] = jnp.sum(dy * _dot(pb, mw), axis=0, keepdims=True)
            dmix = (dy * sc_ref[:, cols]).astype(BF16)
            dm_ref[g] = _dot(pb, dmix, _TN)
            dpool = _dot(dmix, mw, _NT)
            acc = dpool / _pool_counts(s, 2 ** (g + 1))
            for lv in range(g + 1):
                acc = acc + _shift_up(acc, 2 ** lv)
            dp_ref[:, cols] = (acc - dpool).astype(BF16)

    wide = lambda c: pl.BlockSpec((s, _B_WIDTH), lambda i: (0, c))
    mspec = pl.BlockSpec((B_GROUPS, LANES, LANES), lambda i: (0, 0, 0))
    vspec = pl.BlockSpec((1, _B_WIDTH), lambda i: (0, 0))
    return pl.pallas_call(
        body, name="pool_bwd", out_shape=(_sds((s, _B_WIDTH), BF16), _sds((B_GROUPS, LANES, LANES)), _sds((1, _B_WIDTH))),
        grid=(1,), in_specs=[wide(3 * A_WIDTH // _B_WIDTH), mspec, vspec, wide(A_WIDTH // _B_WIDTH)],
        out_specs=(wide(0), mspec, vspec),
        compiler_params=_params(("arbitrary",), VMEM_BIG),
    )(z, mix_w, scale, dycat)


_FF_BLOCKS = D_FF // FF_UNIT


def _ff_spec(s):
    return pl.BlockSpec((2, s, FF_UNIT), lambda j: (0, 0, j))


def _ff_wspecs():
    return [pl.BlockSpec((3, FF_UNIT), lambda j: (0, j)), pl.BlockSpec((3, FF_UNIT), lambda j: (0, _FF_BLOCKS + j))]


_FF_ROWS = 64
_FF_HALO = 16


def _chunk_taps(z_ref, half, c):
    start = pl.multiple_of(c * _FF_ROWS, _FF_ROWS)
    before = pl.multiple_of(jnp.maximum(c * _FF_ROWS - _FF_HALO, 0), _FF_HALO)
    halo = z_ref[half, pl.ds(before, _FF_HALO), :].astype(F32)
    halo = jnp.where(c > 0, halo, 0.0)
    win = jnp.concatenate([halo, z_ref[half, pl.ds(start, _FF_ROWS), :].astype(F32)], axis=0)
    return tuple(pltpu.roll(win, k, axis=0)[_FF_HALO:] for k in (2, 1)) + (win[_FF_HALO:],)


def _fold8(v):
    acc = v[0:8]
    for r in range(8, v.shape[0], 8):
        acc = acc + v[r:r + 8]
    return acc


_FF_CHUNK = 256


def ffn_act_down(zf, conv_w, w_down, name):
    s, d = zf.shape[1], w_down.shape[1]
    nk = D_FF // _FF_CHUNK
    chunk = lambda k: jnp.minimum(k, nk - 1)

    def body(z_ref, wg_ref, wu_ref, wd_ref, a_ref, f_ref, held_ref, acc_ref):
        k = pl.program_id(0)

        @pl.when(k == 0)
        def _():
            held_ref[...] = jnp.zeros_like(held_ref)
            acc_ref[...] = jnp.zeros_like(acc_ref)

        acc_ref[...] += _dot(held_ref[(k + 1) % 2], wd_ref[...])
        g = _conv3(z_ref[0].astype(F32), wg_ref[...])
        u = _conv3(z_ref[1].astype(F32), wu_ref[...])
        act = (g * _sigmoid(g) * u).astype(BF16)
        a_ref[...] = act
        held_ref[k % 2] = act

        @pl.when(k == nk)
        def _():
            f_ref[...] = acc_ref[...].astype(BF16)

    return pl.pallas_call(
        body, name=name, out_shape=(_sds((s, D_FF), BF16), _sds((s, d), BF16)), grid=(nk + 1,),
        in_specs=[pl.BlockSpec((2, s, _FF_CHUNK), lambda k: (0, 0, chunk(k))),
                  pl.BlockSpec((3, _FF_CHUNK), lambda k: (0, chunk(k))),
                  pl.BlockSpec((3, _FF_CHUNK), lambda k: (0, nk + chunk(k))),
                  pl.BlockSpec((_FF_CHUNK, d), lambda k: (jnp.maximum(k - 1, 0), 0))],
        out_specs=(pl.BlockSpec((s, _FF_CHUNK), lambda k: (0, chunk(k))), pl.BlockSpec((s, d), lambda k: (0, 0))),
        scratch_shapes=[pltpu.VMEM((2, s, _FF_CHUNK), BF16), pltpu.VMEM((s, d), F32)],
        compiler_params=_params(("arbitrary",), VMEM_BIG),
    )(zf, conv_w, conv_w, w_down)


def ffn_act_bwd(zf, conv_w, da, name):
    s = zf.shape[1]
    assert s % _FF_ROWS == 0
    nchunks = s // _FF_ROWS

    def body(z_ref, wg_ref, wu_ref, da_ref, dz_ref, dw_ref, dg_ref, du_ref):
        wg, wu = wg_ref[...], wu_ref[...]

        def first(c, acc):
            rows = pl.ds(pl.multiple_of(c * _FF_ROWS, _FF_ROWS), _FF_ROWS)
            tg, tu = _chunk_taps(z_ref, 0, c), _chunk_taps(z_ref, 1, c)
            g = _conv3_taps(tg, wg)
            u = _conv3_taps(tu, wu)
            dav = da_ref[rows, :].astype(F32)
            sg = _sigmoid(g)
            dg = dav * u * (sg * (1.0 + g * (1.0 - sg)))
            du = dav * (g * sg)
            dg_ref[rows, :] = dg
            du_ref[rows, :] = du
            return tuple(a + _fold8(d * t) for a, (d, t) in zip(acc, [(dg, t) for t in tg] + [(du, t) for t in tu]))

        zero = jnp.zeros((8, FF_UNIT), F32)
        acc = lax.fori_loop(0, nchunks, first, (zero,) * 6)
        sums = [jnp.sum(a, axis=0, keepdims=True) for a in acc]
        dw_ref[0] = jnp.concatenate(sums[:3], axis=0)
        dw_ref[1] = jnp.concatenate(sums[3:], axis=0)

        tail = pl.ds(s, _FF_HALO)
        dg_ref[tail, :] = jnp.zeros((_FF_HALO, FF_UNIT), F32)
        du_ref[tail, :] = jnp.zeros((_FF_HALO, FF_UNIT), F32)
        span = _FF_ROWS + _FF_HALO

        def second(c, carry):
            start = pl.multiple_of(c * _FF_ROWS, _FF_ROWS)
            for half, (d_ref, w) in enumerate(((dg_ref, wg), (du_ref, wu))):
                win = d_ref[pl.ds(start, span), :]
                dz = (w[0:1, :] * pltpu.roll(win, span - 2, axis=0)[:_FF_ROWS]
                      + w[1:2, :] * pltpu.roll(win, span - 1, axis=0)[:_FF_ROWS] + w[2:3, :] * win[:_FF_ROWS])
                dz_ref[half, pl.ds(start, _FF_ROWS), :] = dz.astype(BF16)
            return carry

        lax.fori_loop(0, nchunks, second, 0)

    return pl.pallas_call(
        body, name=name, out_shape=(_sds((2, s, D_FF), BF16), _sds((2, 3, D_FF))), grid=(_FF_BLOCKS,),
        in_specs=[_ff_spec(s)] + _ff_wspecs() + [pl.BlockSpec((s, FF_UNIT), lambda j: (0, j))],
        out_specs=(_ff_spec(s), pl.BlockSpec((2, 3, FF_UNIT), lambda j: (0, 0, j))),
        scratch_shapes=[pltpu.VMEM((s + _FF_HALO, FF_UNIT), F32), pltpu.VMEM((s + _FF_HALO, FF_UNIT), F32)],
        compiler_params=_params(("parallel",), VMEM_BIG),
    )(zf, conv_w, conv_w, da)


def _rope(v, cs, s1, s2):
    return v * cs + pltpu.roll(v, LANES - C_ROPE // 2, axis=1) * s1 + pltpu.roll(v, C_ROPE // 2, axis=1) * s2


def _rope_t(dv, cs, s1, s2):
    return dv * cs + pltpu.roll(dv * s1, C_ROPE // 2, axis=1) + pltpu.roll(dv * s2, LANES - C_ROPE // 2, axis=1)


def _kpe_mask(shape):
    lane = lax.broadcasted_iota(jnp.int32, shape, 1)
    return (lane >= C_NOPE) & (lane < C_NOPE + C_ROPE)


def _rms(v, g):
    r = lax.rsqrt(jnp.mean(v * v, axis=-1, keepdims=True) + EPS)
    return v * r, r


def _rms_bwd(dn, xh, r, g):
    dxh = dn * g
    return r * (dxh - xh * jnp.mean(dxh * xh, axis=-1, keepdims=True)), jnp.sum(dn * xh, axis=0, keepdims=True)


_ZQ = C_Q_RANK + C_KV_RANK + HEAD_PAD
_HW = C_HEADS * HEAD_PAD


def mla_pre_fwd(z, gq, gkv, wq, wk, wv, cs, s1, s2):
    s = z.shape[0]
    tm = _tile(s, 256, 8)

    def body(z_ref, gq_ref, gkv_ref, wq_ref, wk_ref, wv_ref, cs_ref, s1_ref, s2_ref, q_ref, k_ref, v_ref):
        zv = z_ref[...].astype(F32)
        cst, s1t, s2t = cs_ref[...], s1_ref[...], s2_ref[...]
        qh, _ = _rms(zv[:, :C_Q_RANK], None)
        qn = (qh * gq_ref[...]).astype(BF16)
        q = _dot(qn, wq_ref[...])
        kh, _ = _rms(zv[:, C_Q_RANK:C_Q_RANK + C_KV_RANK], None)
        kvn = (kh * gkv_ref[...]).astype(BF16)
        k = _dot(kvn, wk_ref[...])
        v_ref[...] = _dot(kvn, wv_ref[...]).astype(BF16)
        kpe = _rope(zv[:, C_Q_RANK + C_KV_RANK:], cst, s1t, s2t)
        for h in range(C_HEADS):
            sl = slice(h * HEAD_PAD, (h + 1) * HEAD_PAD)
            q_ref[:, sl] = _rope(q[:, sl], cst, s1t, s2t).astype(BF16)
            k_ref[:, sl] = (k[:, sl] + kpe).astype(BF16)

    full = lambda r, c: pl.BlockSpec((r, c), lambda i: (0, 0))
    hw = _sds((s, _HW), BF16)
    return pl.pallas_call(
        body, name="mla_pre_fwd", out_shape=(hw, hw, hw), grid=(s // tm,),
        in_specs=[_rows(tm, _ZQ), _vec(C_Q_RANK), _vec(C_KV_RANK), full(C_Q_RANK, _HW), full(C_KV_RANK, _HW),
                  full(C_KV_RANK, _HW), _rows(tm, LANES), _rows(tm, LANES), _rows(tm, LANES)],
        out_specs=(_rows(tm, _HW), _rows(tm, _HW), _rows(tm, _HW)),
        compiler_params=_params(("parallel",), VMEM_BIG),
    )(z, gq, gkv, wq, wk, wv, cs, s1, s2)


def mla_pre_bwd(z, gq, gkv, wq, wk, wv, cs, s1, s2, dq, dk, dv):
    s = z.shape[0]
    tm = _tile(s, 256, 8)

    def body(z_ref, gq_ref, gkv_ref, wq_ref, wk_ref, wv_ref, cs_ref, s1_ref, s2_ref, dq_ref, dk_ref, dv_ref,
             dz_ref, dwq_ref, dwk_ref, dwv_ref, dgq_ref, dgkv_ref):
        @pl.when(pl.program_id(0) == 0)
        def _():
            dwq_ref[...] = jnp.zeros_like(dwq_ref)
            dwk_ref[...] = jnp.zeros_like(dwk_ref)
            dwv_ref[...] = jnp.zeros_like(dwv_ref)
            dgq_ref[...] = jnp.zeros_like(dgq_ref)
            dgkv_ref[...] = jnp.zeros_like(dgkv_ref)

        zv = z_ref[...].astype(F32)
        cst, s1t, s2t = cs_ref[...], s1_ref[...], s2_ref[...]
        gqv, gkvv = gq_ref[...], gkv_ref[...]
        qh, rq = _rms(zv[:, :C_Q_RANK], None)
        qn = (qh * gqv).astype(BF16)
        kh, rk = _rms(zv[:, C_Q_RANK:C_Q_RANK + C_KV_RANK], None)
        kvn = (kh * gkvv).astype(BF16)

        dqv = dq_ref[...].astype(F32)
        dqp = jnp.concatenate(
            [_rope_t(dqv[:, h * HEAD_PAD:(h + 1) * HEAD_PAD], cst, s1t, s2t) for h in range(C_HEADS)], axis=1
        ).astype(BF16)
        dwq_ref[...] += _dot(qn, dqp, _TN)
        dqn = _dot(dqp, wq_ref[...], _NT)
        dql, dgq = _rms_bwd(dqn, qh, rq, gqv)
        dgq_ref[...] += dgq

        dkv = dk_ref[...]
        dkb = dkv.astype(BF16)
        dvb = dv_ref[...].astype(BF16)
        dwk_ref[...] += _dot(kvn, dkb, _TN)
        dwv_ref[...] += _dot(kvn, dvb, _TN)
        dkvn = _dot(dkb, wk_ref[...], _NT) + _dot(dvb, wv_ref[...], _NT)
        dkl, dgkv = _rms_bwd(dkvn, kh, rk, gkvv)
        dgkv_ref[...] += dgkv

        dkpe = dkv[:, :HEAD_PAD]
        for h in range(1, C_HEADS):
            dkpe = dkpe + dkv[:, h * HEAD_PAD:(h + 1) * HEAD_PAD]
        dkpe = _rope_t(jnp.where(_kpe_mask(dkpe.shape), dkpe, 0.0), cst, s1t, s2t)
        dz_ref[...] = jnp.concatenate([dql, dkl, dkpe], axis=1).astype(BF16)

    full = lambda r, c: pl.BlockSpec((r, c), lambda i: (0, 0))
    return pl.pallas_call(
        body, name="mla_pre_bwd",
        out_shape=(_sds((s, _ZQ), BF16), _sds((C_Q_RANK, _HW)), _sds((C_KV_RANK, _HW)), _sds((C_KV_RANK, _HW)),
                   _sds((1, C_Q_RANK)), _sds((1, C_KV_RANK))),
        grid=(s // tm,),
        in_specs=[_rows(tm, _ZQ), _vec(C_Q_RANK), _vec(C_KV_RANK), full(C_Q_RANK, _HW), full(C_KV_RANK, _HW),
                  full(C_KV_RANK, _HW), _rows(tm, LANES), _rows(tm, LANES), _rows(tm, LANES),
                  _rows(tm, _HW), _rows(tm, _HW), _rows(tm, _HW)],
        out_specs=(_rows(tm, _ZQ), full(C_Q_RANK, _HW), full(C_KV_RANK, _HW), full(C_KV_RANK, _HW),
                   _vec(C_Q_RANK), _vec(C_KV_RANK)),
        compiler_params=_params(("arbitrary",), VMEM_BIG),
    )(z, gq, gkv, wq, wk, wv, cs, s1, s2, dq, dk, dv)


_ATT_SCALE = (C_NOPE + C_ROPE) ** -0.5
_NEG = -1e30


def _att_exp(q, k, row0, ends_here):
    sc = _dot(q, k, _NT) * _ATT_SCALE
    tq, nk = sc.shape
    if ends_here:
        last = sc[:, nk - tq:]
        row = lax.broadcasted_iota(jnp.int32, last.shape, 0)
        col = lax.broadcasted_iota(jnp.int32, last.shape, 1)
        last = jnp.where(col <= row, last, _NEG)
        sc = last if nk == tq else jnp.concatenate([sc[:, :nk - tq], last], axis=1)
    else:
        qpos = row0 + lax.broadcasted_iota(jnp.int32, sc.shape, 0)
        kpos = lax.broadcasted_iota(jnp.int32, sc.shape, 1)
        sc = jnp.where(kpos <= qpos, sc, _NEG)
    e = jnp.exp(sc - jnp.max(sc, axis=-1, keepdims=True))
    return e, 1.0 / jnp.sum(e, axis=-1, keepdims=True)


def _causal_cases(i, nq, tq, fn):
    if nq > 8:
        fn(nq * tq, False)
        return
    for blk in range(nq):
        pl.when(i == blk)(functools.partial(fn, (blk + 1) * tq, True))


def attn_fwd(q, k, v):
    s = q.shape[0]
    tq = _tile(s, 256, 8)
    nq = s // tq

    def body(q_ref, k_ref, v_ref, o_ref):
        i = pl.program_id(1)

        def case(nk, ends_here):
            e, inv = _att_exp(q_ref[...], k_ref[:nk, :], i * tq, ends_here)
            o_ref[...] = (_dot(e.astype(BF16), v_ref[:nk, :]) * inv).astype(BF16)

        _causal_cases(i, nq, tq, case)

    qspec = pl.BlockSpec((tq, HEAD_PAD), lambda h, i: (i, h))
    kspec = pl.BlockSpec((s, HEAD_PAD), lambda h, i: (0, h))
    return pl.pallas_call(
        body, name="attn_fwd", out_shape=_sds((s, _HW), BF16), grid=(C_HEADS, s // tq),
        in_specs=[qspec, kspec, kspec], out_specs=qspec,
        compiler_params=_params(("parallel", "parallel"), VMEM_BIG),
    )(q, k, v)


def attn_bwd(q, k, v, o, do_all, do_col0):
    s = q.shape[0]
    tq = _tile(s, 256, 8)

    def body(q_ref, k_ref, v_ref, o_ref, do_ref, dq_ref, dk_ref, dv_ref):
        i = pl.program_id(1)

        @pl.when(i == 0)
        def _():
            dk_ref[...] = jnp.zeros_like(dk_ref)
            dv_ref[...] = jnp.zeros_like(dv_ref)

        def case(nk, ends_here):
            qv, kv, vv, dov = q_ref[...], k_ref[:nk, :], v_ref[:nk, :], do_ref[...]
            e, inv = _att_exp(qv, kv, i * tq, ends_here)
            p = e * inv
            dp = _dot(dov, vv, _NT)
            delta = jnp.sum(dov.astype(F32) * o_ref[...].astype(F32), axis=-1, keepdims=True)
            ds = (p * (dp - delta) * _ATT_SCALE).astype(BF16)
            dq_ref[...] = _dot(ds, kv).astype(BF16)
            dk_ref[:nk, :] += _dot(ds, qv, _TN)
            dv_ref[:nk, :] += _dot(p.astype(BF16), dov, _TN)

        _causal_cases(i, s // tq, tq, case)

    qspec = pl.BlockSpec((tq, HEAD_PAD), lambda h, i: (i, h))
    dospec = pl.BlockSpec((tq, HEAD_PAD), lambda h, i: (i, do_col0 + h))
    kspec = pl.BlockSpec((s, HEAD_PAD), lambda h, i: (0, h))
    return pl.pallas_call(
        body, name="attn_bwd", out_shape=(_sds((s, _HW), BF16), _sds((s, _HW)), _sds((s, _HW))),
        grid=(C_HEADS, s // tq),
        in_specs=[qspec, kspec, kspec, qspec, dospec], out_specs=(qspec, kspec, kspec),
        compiler_params=_params(("parallel", "arbitrary"), VMEM_BIG),
    )(q, k, v, o, do_all)


_DW = D_GROUPS * LANES


def _tril_bf16(w):
    r = lax.broadcasted_iota(jnp.int32, w.shape, 0)
    c = lax.broadcasted_iota(jnp.int32, w.shape, 1)
    return jnp.where(c <= r, w, 0.0).astype(BF16)


def _sgu_forward(zu, zv, lg, lb, ws_ref, bs):
    u = _gelu(zu)
    v = _gelu(zv)
    mu = jnp.mean(v, axis=-1, keepdims=True)
    vc = v - mu
    rstd = lax.rsqrt(jnp.mean(vc * vc, axis=-1, keepdims=True) + EPS)
    xh = vc * rstd
    vln = (xh * lg + lb).astype(BF16)
    mixed = []
    for g in range(D_GROUPS):
        wg = _tril_bf16(ws_ref[g])
        mixed.append(_dot(wg, vln[:, g * LANES:(g + 1) * LANES]) + bs[:, g:g + 1])
    return u, xh, rstd, vln, jnp.concatenate(mixed, axis=1)


def sgu_fwd(z, lg, lb, ws, bs_t):
    s = z.shape[0]
    nchunk = s // D_CHUNK

    def body(zu_ref, zv_ref, lg_ref, lb_ref, ws_ref, bs_ref, o_ref):
        u, _, _, _, mixed = _sgu_forward(zu_ref[...].astype(F32), zv_ref[...].astype(F32), lg_ref[...], lb_ref[...],
                                         ws_ref, bs_ref[...])
        o_ref[...] = (u * mixed).astype(BF16)

    return pl.pallas_call(
        body, name="sgu_fwd", out_shape=_sds((s, _DW), BF16), grid=(nchunk,),
        in_specs=[pl.BlockSpec((D_CHUNK, _DW), lambda n: (n, 1)), pl.BlockSpec((D_CHUNK, _DW), lambda n: (n, 2)),
                  _vec(_DW), _vec(_DW), pl.BlockSpec((D_GROUPS, D_CHUNK, D_CHUNK), lambda n: (0, 0, 0)),
                  pl.BlockSpec((D_CHUNK, LANES), lambda n: (0, 0))],
        out_specs=pl.BlockSpec((D_CHUNK, _DW), lambda n: (n, 0)),
        compiler_params=_params(("parallel",)),
    )(z, z, lg, lb, ws, bs_t)


def sgu_bwd(z, lg, lb, ws, bs_t, dycat, dy_col):
    s = z.shape[0]
    nchunk = s // D_CHUNK

    def body(zu_ref, zv_ref, lg_ref, lb_ref, ws_ref, bs_ref, dy_ref, dzu_ref, dzv_ref, dws_ref, dbs_ref, dlg_ref,
             dlb_ref):
        @pl.when(pl.program_id(0) == 0)
        def _():
            dws_ref[...] = jnp.zeros_like(dws_ref)
            dbs_ref[...] = jnp.zeros_like(dbs_ref)
            dlg_ref[...] = jnp.zeros_like(dlg_ref)
            dlb_ref[...] = jnp.zeros_like(dlb_ref)

        zu, zv, lg = zu_ref[...].astype(F32), zv_ref[...].astype(F32), lg_ref[...]
        u, xh, rstd, vln, mixed = _sgu_forward(zu, zv, lg, lb_ref[...], ws_ref, bs_ref[...])
        dy = dy_ref[...].astype(F32)
        dzu_ref[...] = (dy * mixed * _gelu_grad(zu)).astype(BF16)
        dmix = dy * u
        lane = lax.broadcasted_iota(jnp.int32, (D_CHUNK, LANES), 1)
        row = lax.broadcasted_iota(jnp.int32, (D_CHUNK, D_CHUNK), 0)
        colm = lax.broadcasted_iota(jnp.int32, (D_CHUNK, D_CHUNK), 1)
        dvln = []
        dbs = jnp.zeros((D_CHUNK, LANES), F32)
        for g in range(D_GROUPS):
            sl = slice(g * LANES, (g + 1) * LANES)
            dmg = dmix[:, sl]
            dbs = dbs + jnp.where(lane == g, jnp.sum(dmg, axis=-1, keepdims=True), 0.0)
            dmb = dmg.astype(BF16)
            dws_ref[g] += jnp.where(colm <= row, _dot(dmb, vln[:, sl], _NT), 0.0)
            dvln.append(_dot(_tril_bf16(ws_ref[g]), dmb, _TN))
        dbs_ref[...] += dbs
        dvln = jnp.concatenate(dvln, axis=1)
        dlg_ref[...] += jnp.sum(dvln * xh, axis=0, keepdims=True)
        dlb_ref[...] += jnp.sum(dvln, axis=0, keepdims=True)
        dxh = dvln * lg
        dvv = rstd * (dxh - jnp.mean(dxh, axis=-1, keepdims=True) - xh * jnp.mean(dxh * xh, axis=-1, keepdims=True))
        dzv_ref[...] = (dvv * _gelu_grad(zv)).astype(BF16)

    wsspec = pl.BlockSpec((D_GROUPS, D_CHUNK, D_CHUNK), lambda n: (0, 0, 0))
    chunk = lambda cidx: pl.BlockSpec((D_CHUNK, _DW), lambda n: (n, cidx))
    return pl.pallas_call(
        body, name="sgu_bwd",
        out_shape=(_sds((s, _DW), BF16), _sds((s, _DW), BF16), _sds((D_GROUPS, D_CHUNK, D_CHUNK)),
                   _sds((D_CHUNK, LANES)), _sds((1, _DW)), _sds((1, _DW))),
        grid=(nchunk,),
        in_specs=[chunk(1), chunk(2), _vec(_DW), _vec(_DW), wsspec, pl.BlockSpec((D_CHUNK, LANES), lambda n: (0, 0)),
                  chunk(dy_col)],
        out_specs=(chunk(0), chunk(0), wsspec, pl.BlockSpec((D_CHUNK, LANES), lambda n: (0, 0)), _vec(_DW), _vec(_DW)),
        compiler_params=_params(("arbitrary",)),
    )(z, z, lg, lb, ws, bs_t, dycat)


def ada_mod(c_all, ada_w, ada_b):
    nl, d, n = ada_w.shape
    nb = c_all.shape[0]
    tn = _tile(n, 512)

    def body(c_ref, w_ref, b_ref, o_ref):
        cv = c_ref[...]
        ca = (cv * _sigmoid(cv)).astype(BF16)
        o_ref[...] = _dot(ca, w_ref[...].astype(BF16)) + b_ref[...]

    return pl.pallas_call(
        body, name="ada_mod", out_shape=_sds((nl, nb, n)), grid=(nl, n // tn),
        in_specs=[pl.BlockSpec((nb, d), lambda l, j: (0, 0)), pl.BlockSpec((None, d, tn), lambda l, j: (l, 0, j)),
                  pl.BlockSpec((None, 1, tn), lambda l, j: (l, 0, j))],
        out_specs=pl.BlockSpec((None, nb, tn), lambda l, j: (l, 0, j)),
        compiler_params=_params(("parallel", "parallel")),
    )(c_all, ada_w, ada_b.reshape(nl, 1, n))


_ADAM_BLOCK = 256 * 1024


def _adam_rows(rows, cols):
    if rows * cols <= _ADAM_BLOCK or rows % 8:
        return rows
    return _tile(rows, max(8, _ADAM_BLOCK // cols), 8)


def _adam_update(w, gv, m, v):
    inv_bc1 = 1.0 / (1.0 - ADAM_B1 ** ADAM_STEP)
    inv_bc2 = 1.0 / (1.0 - ADAM_B2 ** ADAM_STEP)
    nm = ADAM_B1 * m + (1.0 - ADAM_B1) * gv
    nv = ADAM_B2 * v + (1.0 - ADAM_B2) * (gv * gv)
    return -ADAM_LR * ((nm * inv_bc1) / (jnp.sqrt(nv * inv_bc2) + ADAM_EPS) + ADAM_WD * w), nm, nv


def adamw(w, g, m, v, name):
    shape = w.shape
    cols = shape[-1]
    rows = w.size // cols
    tr = _adam_rows(rows, cols)

    def body(w_ref, g_ref, m_ref, v_ref, go_ref, d_ref, nm_ref, nv_ref):
        gv = g_ref[...]
        go_ref[...] = gv
        d_ref[...], nm_ref[...], nv_ref[...] = _adam_update(w_ref[...], gv, m_ref[...], v_ref[...])

    spec = pl.BlockSpec((tr, cols), lambda i: (i, 0))
    out = _sds((rows, cols))
    r2 = lambda t: t.reshape(rows, cols)
    res = pl.pallas_call(
        body, name=name, out_shape=(out,) * 4, grid=(rows // tr,),
        in_specs=[spec] * 4, out_specs=(spec,) * 4, compiler_params=_params(("parallel",)),
    )(r2(w), r2(g), r2(m), r2(v))
    return tuple(t.reshape(shape) for t in res)


def adamw_ada(w, c_all, dmod, m, v):
    nl, d, n = w.shape
    tr = _adam_rows(d, n)
    pad = 16 - c_all.shape[0]
    c16 = jnp.pad(c_all, ((0, pad), (0, 0)))
    dm16 = jnp.pad(dmod, ((0, 0), (0, pad), (0, 0)))

    def body(w_ref, c_ref, dm_ref, m_ref, v_ref, g_ref, d_ref, nm_ref, nv_ref):
        cv = c_ref[...]
        gv = _dot((cv * _sigmoid(cv)).astype(BF16), dm_ref[...].astype(BF16), _TN)
        g_ref[...] = gv
        d_ref[...], nm_ref[...], nv_ref[...] = _adam_update(w_ref[...], gv, m_ref[...], v_ref[...])

    spec = pl.BlockSpec((None, tr, n), lambda l, i: (l, i, 0))
    out = _sds((nl, d, n))
    return pl.pallas_call(
        body, name="adamw_ada_w", out_shape=(out, out, out, out), grid=(nl, d // tr),
        in_specs=[spec, pl.BlockSpec((16, tr), lambda l, i: (0, i)), pl.BlockSpec((None, 16, n), lambda l, i: (l, 0, 0)),
                  spec, spec],
        out_specs=(spec,) * 4, compiler_params=_params(("parallel", "parallel")),
    )(w, c16, dm16, m, v)


def adamw_small(ws, gs, ms, vs):
    n = len(ws)
    flat = lambda t: t.reshape(-1, t.shape[-1])

    def body(*refs):
        ins, outs = refs[:4 * n], refs[4 * n:]
        for i in range(n):
            w_ref, g_ref, m_ref, v_ref = ins[4 * i:4 * i + 4]
            outs[3 * i][...], outs[3 * i + 1][...], outs[3 * i + 2][...] = _adam_update(
                w_ref[...], g_ref[...], m_ref[...], v_ref[...])

    operands = [flat(t) for quad in zip(ws, gs, ms, vs) for t in quad]
    res = pl.pallas_call(
        body, name="adamw_small", out_shape=tuple(_sds(flat(w).shape) for w in ws for _ in range(3)),
    )(*operands)
    return [(g, res[3 * i].reshape(w.shape), res[3 * i + 1].reshape(w.shape), res[3 * i + 2].reshape(w.shape))
            for i, (w, g) in enumerate(zip(ws, gs))]


def adamw_layers(w, g0, g1, m, v, name):
    _, rows, cols = w.shape
    tr = _adam_rows(rows, cols)

    def body(w_ref, g0_ref, g1_ref, m_ref, v_ref, g_ref, d_ref, nm_ref, nv_ref):
        gv = jnp.where(pl.program_id(0) == 0, g0_ref[...], g1_ref[...])
        g_ref[...] = gv
        d_ref[...], nm_ref[...], nv_ref[...] = _adam_update(w_ref[...], gv, m_ref[...], v_ref[...])

    spec = pl.BlockSpec((None, tr, cols), lambda l, i: (l, i, 0))
    gspec = pl.BlockSpec((tr, cols), lambda l, i: (i, 0))
    out = _sds((2, rows, cols))
    return pl.pallas_call(
        body, name=name, out_shape=(out, out, out, out), grid=(2, rows // tr),
        in_specs=[spec, gspec, gspec, spec, spec], out_specs=(spec,) * 4, compiler_params=_params(("parallel", "parallel")),
    )(w, g0, g1, m, v)


def sum8(gathered):
    _, r, _ = gathered.shape
    tr = _tile(r, 512, 8)

    def body(g_ref, o_ref):
        acc = g_ref[0]
        for dev in range(1, N_DEV):
            acc = acc + g_ref[dev]
        o_ref[...] = acc

    return pl.pallas_call(
        body, name="sum8", out_shape=_sds((r, LANES)), grid=(r // tr,),
        in_specs=[pl.BlockSpec((N_DEV, tr, LANES), lambda i: (0, i, 0))], out_specs=pl.BlockSpec((tr, LANES), lambda i: (i, 0)),
        compiler_params=_params(("parallel",)),
    )(gathered)


_SUM_STEPS = 2


def pair_sums(gs, recvs, core, name):
    n = len(gs)
    trs = [g.shape[1] // 2 // _SUM_STEPS for g in gs]

    def body(c_ref, *refs):
        del c_ref
        for i in range(n):
            a_ref, b_ref, o_ref = refs[2 * i], refs[2 * i + 1], refs[2 * n + i]
            o_ref[...] = (a_ref[...].astype(F32) + b_ref[...].astype(F32)).astype(BF16)

    in_specs, out_specs = [], []
    for g, tr in zip(gs, trs):
        cols = g.shape[2]
        in_specs.append(pl.BlockSpec((None, tr, cols), lambda k, s, c: (k, c[0] * _SUM_STEPS + s, 0)))
        in_specs.append(pl.BlockSpec((None, tr, cols), lambda k, s, c: (k, s, 0)))
        out_specs.append(pl.BlockSpec((None, tr, cols), lambda k, s, c: (k, s, 0)))
    grid_spec = pltpu.PrefetchScalarGridSpec(num_scalar_prefetch=1, grid=(N_CHIPS, _SUM_STEPS), in_specs=in_specs,
                                             out_specs=tuple(out_specs))
    return list(pl.pallas_call(
        body, name=name, out_shape=tuple(_sds((N_CHIPS, g.shape[1] // 2, g.shape[2]), BF16) for g in gs),
        grid_spec=grid_spec, compiler_params=_params(("parallel", "parallel")),
    )(core.reshape(1).astype(jnp.int32), *[t for pair in zip(gs, recvs) for t in pair]))


def chip_sums(pairs, recvs, chip, core, name):
    n = len(pairs)
    trs = [p.shape[1] // _SUM_STEPS for p in pairs]

    def body(p_ref, *refs):
        del p_ref
        for i in range(n):
            own_ref, r_ref, o_ref = refs[2 * i], refs[2 * i + 1], refs[2 * n + i]
            acc = own_ref[...].astype(F32)
            for j in range(N_CHIPS - 1):
                acc = acc + r_ref[j].astype(F32)
            o_ref[...] = acc

    in_specs, out_specs = [], []
    for p, tr in zip(pairs, trs):
        cols = p.shape[2]
        in_specs.append(pl.BlockSpec((None, tr, cols), lambda s, q: (q[0], s, 0)))
        in_specs.append(pl.BlockSpec((N_CHIPS - 1, tr, cols), lambda s, q: (0, s, 0)))
        out_specs.append(pl.BlockSpec((None, tr, cols), lambda s, q: (q[1], s, 0)))
    grid_spec = pltpu.PrefetchScalarGridSpec(num_scalar_prefetch=1, grid=(_SUM_STEPS,), in_specs=in_specs,
                                             out_specs=tuple(out_specs))
    return list(pl.pallas_call(
        body, name=name, out_shape=tuple(_sds((2,) + p.shape[1:]) for p in pairs), grid_spec=grid_spec,
        compiler_params=_params(("parallel",)),
    )(jnp.stack([chip, core]).astype(jnp.int32), *[t for pair in zip(pairs, recvs) for t in pair]))


def _place():
    return lax.axis_index("x"), lax.axis_index("y"), lax.axis_index("c")


def _other_chips(x, y):
    return [(x, 1 - y), (1 - x, y), (1 - x, 1 - y)]


_HBM = pl.BlockSpec(memory_space=pltpu.HBM)


def all_gather8(v, name, after=()):
    m, n = v.shape

    def body(x_ref, *refs):
        out_ref, send_sems, recv_sems, local_sem = refs[len(after):]
        x, y, c = _place()
        me, sibling = (x, y, c), (x, y, 1 - c)
        chips = _other_chips(x, y)

        def rows(px, py, pc):
            return out_ref.at[pl.ds((4 * px + 2 * py + pc) * m, m), :]

        def copy(k, block, to, src=None):
            return pltpu.make_async_remote_copy(
                src_ref=rows(*block) if src is None else src, dst_ref=rows(*block),
                send_sem=send_sems.at[k], recv_sem=recv_sems.at[k], device_id=to, device_id_type=MESH)

        mine = pltpu.make_async_copy(x_ref, rows(*me), local_sem)
        mine.start()
        first = [copy(0, me, sibling, src=x_ref)]
        first += [copy(1 + j, me, (*chip, c), src=x_ref) for j, chip in enumerate(chips)]
        for cp in first:
            cp.start()
        passed = [copy(4 + j, (*chip, c), sibling) for j, chip in enumerate(chips)]
        for j, chip in enumerate(chips):
            copy(1 + j, (*chip, c), me).wait_recv()
            passed[j].start()
        copy(0, sibling, me).wait_recv()
        for j, chip in enumerate(chips):
            copy(4 + j, (*chip, 1 - c), me).wait_recv()
        for cp in first + passed:
            cp.wait_send()
        mine.wait()

    return pl.pallas_call(
        body, name=name, out_shape=_sds((N_DEV * m, n), v.dtype),
        in_specs=[pl.BlockSpec(memory_space=pltpu.VMEM)] + [pl.BlockSpec(memory_space=pl.ANY)] * len(after),
        out_specs=pl.BlockSpec(memory_space=pltpu.VMEM),
        scratch_shapes=[pltpu.SemaphoreType.DMA((7,)), pltpu.SemaphoreType.DMA((7,)), pltpu.SemaphoreType.DMA],
        compiler_params=_params(None, VMEM_BIG),
    )(v, *after)


def _comm_call(body, name, ins, out_shapes, nsem, aliases=None):
    return pl.pallas_call(
        body, name=name, out_shape=tuple(out_shapes), in_specs=[_HBM] * len(ins), out_specs=tuple([_HBM] * len(out_shapes)),
        scratch_shapes=[pltpu.SemaphoreType.DMA((nsem,)), pltpu.SemaphoreType.DMA((nsem,))],
        input_output_aliases=aliases or {},
    )(*ins)


def _remote(src, dst, send_sems, recv_sems, k, to):
    return pltpu.make_async_remote_copy(src_ref=src, dst_ref=dst, send_sem=send_sems.at[k], recv_sem=recv_sems.at[k],
                                        device_id=to, device_id_type=MESH)


def _half(core, rh):
    return pl.ds(pl.multiple_of(core * rh, 16), rh)


def swap_halves(gs, name):
    n = len(gs)

    def body(*refs):
        ins, outs, (send_sems, recv_sems) = refs[:n], refs[n:2 * n], refs[2 * n:]
        x, y, c = _place()
        copies = []
        for i in range(n):
            theirs = _half(1 - c, ins[i].shape[1] // 2)
            cp = _remote(ins[i].at[:, theirs], outs[i], send_sems, recv_sems, i, (x, y, 1 - c))
            cp.start()
            copies.append(cp)
        for cp in copies:
            cp.wait()

    return _comm_call(body, name, gs, [_sds((g.shape[0], g.shape[1] // 2, g.shape[2]), g.dtype) for g in gs], n)


def join_halves(bufs, name):
    n = len(bufs)

    def body(*refs):
        ins, outs, (send_sems, recv_sems) = refs[:n], refs[n:2 * n], refs[2 * n:]
        x, y, c = _place()
        copies = []
        for i in range(n):
            cp = _remote(ins[i].at[c], outs[i].at[c], send_sems, recv_sems, i, (x, y, 1 - c))
            cp.start()
            copies.append(cp)
        for i in range(n):
            theirs = outs[i].at[1 - c]
            _remote(theirs, theirs, send_sems, recv_sems, i, (x, y, 1 - c)).wait_recv()
        for cp in copies:
            cp.wait_send()

    return _comm_call(body, name, bufs, [_sds(b.shape, b.dtype) for b in bufs], n, {i: i for i in range(n)})


def forward_halves(lands, name):
    n = len(lands)

    def body(*refs):
        ins, outs, (send_sems, recv_sems) = refs[:n], refs[n:2 * n], refs[2 * n:]
        x, y, c = _place()
        sibling = (x, y, 1 - c)
        chips = _other_chips(x, y)
        copies = []
        for i in range(n):
            mine = _half(c, ins[i].shape[1] // 2)
            for j, (px, py) in enumerate(chips):
                cp = _remote(ins[i].at[2 * px + py, mine], outs[i].at[2 * px + py, mine], send_sems, recv_sems, 3 * i + j, sibling)
                cp.start()
                copies.append(cp)
        for i in range(n):
            theirs = _half(1 - c, ins[i].shape[1] // 2)
            for j, (px, py) in enumerate(chips):
                landed = outs[i].at[2 * px + py, theirs]
                _remote(landed, landed, send_sems, recv_sems, 3 * i + j, sibling).wait_recv()
        for cp in copies:
            cp.wait_send()

    return _comm_call(body, name, lands, [_sds(b.shape, b.dtype) for b in lands], 3 * n, {i: i for i in range(n)})


_SEM = pl.BlockSpec(memory_space=pltpu.SEMAPHORE)
_EFFECT = pltpu.SideEffectType.DATAFLOW_SIDE_EFFECTING


def _gather_copies(srcs, lands, send_sems, recv_sems):
    x, y, c = _place()
    copies = []
    for i in range(len(srcs)):
        mine = _half(c, srcs[i].shape[0] // 2)
        for j, chip in enumerate(_other_chips(x, y)):
            copies.append(_remote(srcs[i].at[mine], lands[i].at[2 * x + y, mine], send_sems, recv_sems, 3 * i + j, (*chip, c)))
    return copies


def _exchange_copies(srcs, lands, send_sems, recv_sems):
    x, y, c = _place()
    copies = []
    for i in range(len(srcs)):
        for j, (px, py) in enumerate(_other_chips(x, y)):
            copies.append(_remote(srcs[i].at[2 * px + py], lands[i].at[j], send_sems, recv_sems, 3 * i + j, (px, py, c)))
    return copies


def _everyone_copies(srcs, lands, send_sems, recv_sems):
    x, y, c = _place()
    flip = lambda v, b: 1 - v if b else v
    dst = lands[0].at[4 * x + 2 * y + c]
    return [_remote(srcs[0], dst, send_sems, recv_sems, j - 1, (flip(x, j & 4), flip(y, j & 2), flip(c, j & 1)))
            for j in range(1, N_DEV)]


GATHER = (_gather_copies, 3)
EXCHANGE = (_exchange_copies, 3)
EVERYONE = (_everyone_copies, N_DEV - 1)


def split_start(name, plan, srcs, land_shapes, after=()):
    copies_fn, per_source = plan
    n, m, k = len(srcs), len(land_shapes), len(after)
    ncopies = per_source * n

    def body(*refs):
        src_refs, land_refs = refs[:n], refs[n:n + m]
        send_sems, recv_sems = refs[n + m + k], refs[n + m + k + 1]
        token = refs[-1]
        for cp in copies_fn(src_refs, land_refs, send_sems, recv_sems):
            cp.start()
        token[...] = jnp.zeros_like(token)

    hbm = lambda s: pltpu.HBM(tuple(s.shape), s.dtype)
    outs = pl.pallas_call(
        body, name=name,
        out_shape=(pltpu.SemaphoreType.DMA((ncopies,)), pltpu.SemaphoreType.DMA((ncopies,)), *[hbm(s) for s in srcs],
                   *[hbm(s) for s in land_shapes], _sds((8, LANES))),
        in_specs=[_HBM] * (n + m) + [pl.BlockSpec(memory_space=pl.ANY)] * k,
        out_specs=(_SEM, _SEM, *([_HBM] * (n + m)), pl.BlockSpec(memory_space=pltpu.VMEM)),
        input_output_aliases={i: 2 + i for i in range(n + m)},
        compiler_params=pltpu.CompilerParams(has_side_effects=_EFFECT),
    )(*[pltpu.with_memory_space_constraint(s, pltpu.HBM) for s in srcs],
      *[pltpu.with_memory_space_constraint(lax.empty(tuple(s.shape), s.dtype), pltpu.HBM) for s in land_shapes], *after)
    handle = (outs[0], outs[1], list(outs[2:2 + n]), list(outs[2 + n:2 + n + m]))
    return handle, outs[-1][0, 0]


def split_wait(name, plan, handle, after):
    copies_fn, _ = plan
    send_sems, recv_sems, srcs, lands = handle
    n, m = len(srcs), len(lands)
    after = list(after) if isinstance(after, (list, tuple)) else [after]

    def body(*refs):
        src_refs, land_refs = refs[:n], refs[n:n + m]
        for cp in copies_fn(src_refs, land_refs, refs[n + m], refs[n + m + 1]):
            cp.wait_send()
            cp.wait_recv()

    hbm = lambda s: pltpu.HBM(tuple(s.shape), s.dtype)
    outs = pl.pallas_call(
        body, name=name, out_shape=tuple(hbm(s) for s in srcs + lands),
        in_specs=[_HBM] * (n + m) + [_SEM, _SEM] + [pl.BlockSpec(memory_space=pl.ANY)] * len(after),
        out_specs=tuple([_HBM] * (n + m)), input_output_aliases={i: i for i in range(n + m)},
        compiler_params=pltpu.CompilerParams(has_side_effects=_EFFECT),
    )(*srcs, *lands, send_sems, recv_sems, *after)
    return list(outs[:n]), list(outs[n:])


_CD_PAD = C_Q_RANK + C_KV_RANK + HEAD_PAD + 2 * _DW


def chip_major(w, groups=N_CHIPS):
    r, c = w.shape
    return w.reshape(r, groups, c // groups).transpose(1, 0, 2)


def from_chip_major(w):
    g, r, c = w.shape
    return w.transpose(1, 0, 2).reshape(r, g * c)


def _cd_in_pad(w):
    a = C_Q_RANK + C_KV_RANK
    z = lambda n: jnp.zeros((w.shape[0], n), w.dtype)
    return jnp.concatenate([w[:, :a], z(C_NOPE), w[:, a:a + C_ROPE], z(HEAD_PAD - C_NOPE - C_ROPE), w[:, a + C_ROPE:]], axis=1)


def _cd_in_unpad(w):
    a = C_Q_RANK + C_KV_RANK
    return jnp.concatenate([w[:, :a], w[:, a + C_NOPE:a + C_NOPE + C_ROPE], w[:, a + HEAD_PAD:]], axis=1)


def _pad_heads(w, width):
    r = w.shape[0]
    w = w.reshape(r, C_HEADS, width)
    return jnp.pad(w, ((0, 0), (0, 0), (0, HEAD_PAD - width))).reshape(r, _HW)


def _unpad_heads(w, width):
    r = w.shape[0]
    return w.reshape(r, C_HEADS, HEAD_PAD)[:, :, :width].reshape(r, C_HEADS * width)


_MATMUL_WEIGHTS = ("ab_w_in", "ab_w_out", "cd_w_in", "c_w_uq", "c_w_ukv", "cd_w_out", "ffn_w_up", "ffn_w_down")
_LAYER_STACKED = ("norm1_g", "norm2_g", "ffn_w_up", "ffn_conv_w", "ffn_w_down")
_ROW_VECTORS = ("b_scale", "c_q_norm_g", "c_kv_norm_g", "d_ln_g", "d_ln_b")


def full_to_local(p):
    q = {}
    for k, v in p.items():
        if k == "final_norm_g":
            v = v.reshape(1, -1)
        elif k not in _LAYER_STACKED and k not in _ROW_VECTORS:
            v = v[0]
        q[k] = v.astype(BF16) if k in _MATMUL_WEIGHTS else v
    return q


def local_to_full(g):
    q = {}
    for k, v in g.items():
        if k == "final_norm_g":
            q[k] = v.reshape(-1)
        elif k not in _LAYER_STACKED and k not in _ROW_VECTORS:
            q[k] = v[None]
        else:
            q[k] = v
    return q


def prepare_weights(p):
    q = dict(p)
    q["cd_w_in"] = _cd_in_pad(p["cd_w_in"])
    q["c_w_uq"] = _pad_heads(p["c_w_uq"], C_NOPE + C_ROPE)
    ukv = p["c_w_ukv"].reshape(C_KV_RANK, C_HEADS, C_NOPE + C_V)
    q["c_w_uk"] = _pad_heads(ukv[:, :, :C_NOPE].reshape(C_KV_RANK, -1), C_NOPE)
    q["c_w_uv"] = _pad_heads(ukv[:, :, C_NOPE:].reshape(C_KV_RANK, -1), C_V)
    wo = p["cd_w_out"]
    att_rows = jnp.pad(wo[:C_HEADS * C_V].reshape(C_HEADS, C_V, D_MODEL), ((0, 0), (0, HEAD_PAD - C_V), (0, 0)))
    q["cd_w_out"] = jnp.concatenate([att_rows.reshape(_HW, D_MODEL), wo[C_HEADS * C_V:]], axis=0)
    return q


def unprepare_grads(g):
    q = dict(g)
    q["cd_w_in"] = _cd_in_unpad(g["cd_w_in"])
    q["c_w_uq"] = _unpad_heads(g["c_w_uq"], C_NOPE + C_ROPE)
    uk = g.pop("c_w_uk").reshape(C_KV_RANK, C_HEADS, HEAD_PAD)[:, :, :C_NOPE]
    uv = g.pop("c_w_uv").reshape(C_KV_RANK, C_HEADS, HEAD_PAD)[:, :, :C_V]
    q.pop("c_w_uk", None)
    q.pop("c_w_uv", None)
    q["c_w_ukv"] = jnp.concatenate([uk, uv], axis=-1).reshape(C_KV_RANK, C_HEADS * (C_NOPE + C_V))
    wo = g["cd_w_out"]
    att = wo[:_HW].reshape(C_HEADS, HEAD_PAD, D_MODEL)[:, :C_V].reshape(C_HEADS * C_V, D_MODEL)
    q["cd_w_out"] = jnp.concatenate([att, wo[_HW:]], axis=0)
    return q


def rope_tables(positions):
    half = C_ROPE // 2
    inv_freq = ROPE_THETA ** (-jnp.arange(half, dtype=F32) / half)
    ang = positions.astype(F32)[:, None] * inv_freq
    cos, sin = jnp.cos(ang), jnp.sin(ang)
    s = positions.shape[0]
    z = lambda n: jnp.zeros((s, n), F32)
    cs = jnp.concatenate([jnp.ones((s, C_NOPE), F32), cos, cos, z(HEAD_PAD - C_NOPE - C_ROPE)], axis=1)
    s1 = jnp.concatenate([z(C_NOPE), -sin, z(HEAD_PAD - C_NOPE - half)], axis=1)
    s2 = jnp.concatenate([z(C_NOPE + half), sin, z(HEAD_PAD - C_NOPE - C_ROPE)], axis=1)
    return cs, s1, s2


def _mods(mod_l):
    return [mod_l[:, i * D_MODEL:(i + 1) * D_MODEL] for i in range(N_MOD)]


_UP_COLS = 2 * D_FF // N_CHIPS


def ffn_fwd(h2, w, l, late_down=None):
    zf = matmul(h2, w["ffn_w_up"][l], "nn", BF16, f"ffn_up{l}", gb=N_CHIPS, go=2, tn=_UP_COLS)
    if late_down is not None:
        late_down(zf)
    a, f = ffn_act_down(zf, w["ffn_conv_w"][l], w["ffn_w_down"][l], f"ffn_act_down{l}")
    return f, (zf, a)


def ffn_bwd(df, h2, saved, w, l):
    zf, a = saved
    da = matmul(df, w["ffn_w_down"][l], "nt", BF16, f"ffn_down_dx{l}", tn=D_FF // 2)
    d_down = matmul(a, df, "tn", BF16, f"ffn_down_dw{l}", tm=D_FF // 2)
    dzf, d_conv = ffn_act_bwd(zf, w["ffn_conv_w"][l], da, f"ffn_act_bwd{l}")
    dh2 = matmul(dzf, w["ffn_w_up"][l], "nt", F32, f"ffn_up_dx{l}", ga=2, gb=N_CHIPS, tk=_UP_COLS, tn=D_MODEL)
    d_up = matmul(h2, dzf, "tn", BF16, f"ffn_up_dw{l}", gb=2, go=N_CHIPS, tn=_UP_COLS)
    d_conv = d_conv.transpose(1, 0, 2).reshape(3, 2 * D_FF)
    return dh2, dict(ffn_w_down=d_down, ffn_conv_w=d_conv, ffn_w_up=d_up)


def mixer0_fwd(h, w):
    z = matmul(h, w["ab_w_in"], "nn", BF16, "ab_in", gb=N_CHIPS)
    ya = gconv_fwd(z, w["a_conv_w"])
    yb = pool_fwd(z, w["b_mix_w"], w["b_scale"])
    ycat = jnp.concatenate([ya, yb], axis=1)
    y = matmul(ycat, w["ab_w_out"], "nn", BF16, "ab_out", tn=D_MODEL)
    return y, (z, ycat)


def mixer0_bwd(dy, h, saved, w):
    z, ycat = saved
    grads = {}
    dycat = matmul(dy, w["ab_w_out"], "nt", BF16, "ab_out_dx")
    grads["ab_w_out"] = matmul(ycat, dy, "tn", BF16, "ab_out_dw")
    db, dc, da, d_conv = gconv_bwd(z, w["a_conv_w"], dycat)
    dp, d_mix, d_scale = pool_bwd(z, w["b_mix_w"], w["b_scale"], dycat)
    dz = jnp.concatenate([db, dc, da, dp], axis=1)
    dh = matmul(dz, w["ab_w_in"], "nt", F32, "ab_in_dx", gb=N_CHIPS, tn=D_MODEL)
    grads["ab_w_in"] = matmul(h, dz, "tn", BF16, "ab_in_dw", go=N_CHIPS)
    grads.update(a_conv_w=d_conv, b_mix_w=d_mix, b_scale=d_scale)
    return dh, grads


def mixer1_fwd(h, ropes, w):
    cs, s1, s2 = ropes
    z = matmul(h, w["cd_w_in"], "nn", BF16, "cd_in")
    bs_t = jnp.pad(w["d_b_s"].T, ((0, 0), (0, LANES - D_GROUPS)))
    qh, kh, vh = mla_pre_fwd(z, w["c_q_norm_g"], w["c_kv_norm_g"], w["c_w_uq"], w["c_w_uk"], w["c_w_uv"], cs, s1, s2)
    oh = attn_fwd(qh, kh, vh)
    yd = sgu_fwd(z, w["d_ln_g"], w["d_ln_b"], w["d_w_s"], bs_t)
    ycat = jnp.concatenate([oh, yd], axis=1)
    y = matmul(ycat, w["cd_w_out"], "nn", BF16, "cd_out", tn=D_MODEL)
    return y, (z, bs_t, qh, kh, vh, oh, ycat)


def mixer1_bwd(dy, h, saved, ropes, w):
    cs, s1, s2 = ropes
    z, bs_t, qh, kh, vh, oh, ycat = saved
    grads = {}
    dycat = matmul(dy, w["cd_w_out"], "nt", BF16, "cd_out_dx")
    grads["cd_w_out"] = matmul(ycat, dy, "tn", F32, "cd_out_dw")
    dqh, dkh, dvh = attn_bwd(qh, kh, vh, oh, dycat, 0)
    dzq, d_uq, d_uk, d_uv, d_gq, d_gkv = mla_pre_bwd(
        z, w["c_q_norm_g"], w["c_kv_norm_g"], w["c_w_uq"], w["c_w_uk"], w["c_w_uv"], cs, s1, s2, dqh, dkh, dvh)
    dzu, dzv, d_ws, d_bs, d_lg, d_lb = sgu_bwd(z, w["d_ln_g"], w["d_ln_b"], w["d_w_s"], bs_t, dycat, _HW // _DW)
    dz = jnp.concatenate([dzq, dzu, dzv], axis=1)
    dh = matmul(dz, w["cd_w_in"], "nt", F32, "cd_in_dx", tn=D_MODEL)
    grads["cd_w_in"] = matmul(h, dz, "tn", F32, "cd_in_dw")
    grads.update(c_w_uq=d_uq, c_w_uk=d_uk, c_w_uv=d_uv, c_q_norm_g=d_gq, c_kv_norm_g=d_gkv, d_w_s=d_ws,
                 d_b_s=d_bs[:, :D_GROUPS].T, d_ln_g=d_lg, d_ln_b=d_lb)
    return dh, grads


class StepHooks:
    def weights(self, stage, after):
        pass

    def gradients(self, stage, grads, after):
        return 0.0


def run_step(x, tgt, mod, ropes, w, hooks):
    sh1a, sc1a, g1a, sh2a, sc2a, g2a = _mods(mod[0:1])
    sh1b, sc1b, g1b, sh2b, sc2b, g2b = _mods(mod[1:2])
    n1, n2 = w["norm1_g"], w["norm2_g"]

    hooks.weights("mix0", mod)
    h0 = modnorm_fwd(x, n1[0:1], sc1a, sh1a, "modnorm_0")
    y0, mix0 = mixer0_fwd(h0, w)
    x1, h1 = resid_modnorm_fwd(x, y0, g1a, n2[0:1], sc2a, sh2a, "resid_modnorm_1")
    hooks.weights("up0", x1)
    f0, ffn0 = ffn_fwd(h1, w, 0, lambda act: hooks.weights("down0", act))
    x2, h2 = resid_modnorm_fwd(x1, f0, g2a, n1[1:2], sc1b, sh1b, "resid_modnorm_2")
    hooks.weights("mix1", x2)
    y1, mix1 = mixer1_fwd(h2, ropes, w)
    x3, h3 = resid_modnorm_fwd(x2, y1, g1b, n2[1:2], sc2b, sh2b, "resid_modnorm_3")
    hooks.weights("ffn1", x3)
    f1, ffn1 = ffn_fwd(h3, w, 1)
    dres, d_final, loss, df1, dg2b = final_fused(x3, f1, g2b, w["final_norm_g"], tgt)

    dh3, gf1 = ffn_bwd(df1, h3, ffn1, w, 1)
    tok = hooks.gradients("ffn1", gf1, dh3)
    dres, dsh2b, dsc2b, dn2b, dy1, dg1b = norm_gate_bwd(x3, dh3, n2[1:2], sc2b, dres, y1, g1b + tok, "norm_gate_bwd_3")
    dh2, gm1 = mixer1_bwd(dy1, h2, mix1, ropes, w)
    tok = hooks.gradients("mix1", gm1, dh2)
    dres, dsh1b, dsc1b, dn1b, df0, dg2a = norm_gate_bwd(x2, dh2, n1[1:2], sc1b, dres, f0, g2a + tok, "norm_gate_bwd_2")
    dh1, gf0 = ffn_bwd(df0, h1, ffn0, w, 0)
    tok = hooks.gradients("ffn0", gf0, dh1)
    dres, dsh2a, dsc2a, dn2a, dy0, dg1a = norm_gate_bwd(x1, dh1, n2[0:1], sc2a, dres, y0, g1a + tok, "norm_gate_bwd_1")
    dh0, gm0 = mixer0_bwd(dy0, h0, mix0, w)
    tok = hooks.gradients("mix0", gm0, dh0)
    grad_x, dsh1a, dsc1a, dn1a = norm_bwd(x, dh0, n1[0:1], sc1a + tok, dres, "norm_bwd_0")

    dmod = jnp.concatenate([jnp.concatenate([dsh1a, dsc1a, dg1a, dsh2a, dsc2a, dg2a], axis=1),
                            jnp.concatenate([dsh1b, dsc1b, dg1b, dsh2b, dsc2b, dg2b], axis=1)], axis=0)
    norms = dict(norm1_g=jnp.concatenate([dn1a, dn1b], axis=0), norm2_g=jnp.concatenate([dn2a, dn2b], axis=0),
                 final_norm_g=d_final)
    return loss, grad_x, dmod, dict(mix0=gm0, ffn0=gf0, mix1=gm1, ffn1=gf1, norms=norms)


def merge_grads(by_stage):
    grads = {**by_stage["mix0"], **by_stage["mix1"], **by_stage["norms"]}
    for k in ("ffn_w_down", "ffn_w_up"):
        grads[k] = [by_stage["ffn0"][k], by_stage["ffn1"][k]]
    grads["ffn_conv_w"] = jnp.stack([by_stage["ffn0"]["ffn_conv_w"], by_stage["ffn1"]["ffn_conv_w"]])
    return grads


def local_step(x, tgt, mod, ropes, w):
    loss, grad_x, dmod, by_stage = run_step(x, tgt, mod, ropes, w, StepHooks())
    return loss, grad_x, dmod, merge_grads(by_stage)


_WEIGHTS = ("ada_w", "ada_b", "norm1_g", "norm2_g", "ab_w_in", "a_conv_w", "b_mix_w", "b_scale", "ab_w_out", "cd_w_in",
            "c_q_norm_g", "c_w_uq", "c_kv_norm_g", "c_w_ukv", "d_ln_g", "d_ln_b", "d_w_s", "d_b_s", "cd_w_out",
            "ffn_w_up", "ffn_conv_w", "ffn_w_down", "final_norm_g")
_INPUTS = ("x", "c", "positions") + _WEIGHTS + ("loss_target",) + tuple("m_" + n for n in _WEIGHTS) + tuple(
    "v_" + n for n in _WEIGHTS)

def _pack_rows(parts, rows, dtype):
    flat = jnp.concatenate([p.reshape(-1).astype(dtype) for p in parts])
    return jnp.pad(flat, (0, rows * LANES - flat.shape[0])).reshape(rows, LANES)


def _rows_major(w):
    r, c = w.shape
    return w.reshape(N_CHIPS, r // N_CHIPS, c)


def start_gather(shards, tag, after=()):
    lands = [_sds((N_CHIPS,) + s.shape, s.dtype) for s in shards]
    return split_start("gather_start_" + tag, GATHER, shards, lands, after)


def finish_gather(handle, chip, tag, after):
    shards, lands = split_wait("gather_wait_" + tag, GATHER, handle, after)
    lands = forward_halves(lands, "gather_forward_" + tag)
    return [lax.dynamic_update_index_in_dim(o, s, chip, 0) for o, s in zip(lands, shards)]


def start_reduce(gs, core, tag):
    recv = swap_halves(gs, "swap_halves_" + tag)
    pairs = pair_sums(gs, recv, core, "pair_sums_" + tag)
    lands = [_sds((N_CHIPS - 1,) + p.shape[1:], p.dtype) for p in pairs]
    return split_start("exchange_start_" + tag, EXCHANGE, pairs, lands)


def finish_reduce(handle, chip, core, tag, after):
    pairs, others = split_wait("exchange_wait_" + tag, EXCHANGE, handle, after)
    halves = chip_sums(pairs, others, chip, core, "chip_sums_" + tag)
    full = join_halves(halves, "join_halves_" + tag)
    return [f.reshape(f.shape[1] * 2, f.shape[2]) for f in full]


_SMALL_SHARDED = (("a_conv_w", (3, 128), 1), ("c_q_norm_g", (1, 64), 1), ("d_ln_g", (1, 128), 1), ("d_ln_b", (1, 128), 1),
                  ("ffn_conv_w", (2, 3, 2 * D_FF // N_CHIPS), 2))
_SMALL_GRADS = (("norm1_g", (2, D_MODEL)), ("norm2_g", (2, D_MODEL)), ("b_mix_w", (4, 128, 128)), ("b_scale", (1, 512)),
                ("c_kv_norm_g", (1, 128)), ("d_w_s", (4, 128, 128)), ("d_b_s", (4, 128)), ("final_norm_g", (1, D_MODEL)),
                ("a_conv_w", (3, 512)), ("c_q_norm_g", (1, 256)), ("d_ln_g", (1, 512)), ("d_ln_b", (1, 512)),
                ("ffn_conv_w", (2, 3, 2 * D_FF)))


def _size(shape):
    n = 1
    for d in shape:
        n *= d
    return n


def kernel(x, c, positions, ada_w, ada_b, norm1_g, norm2_g, ab_w_in, a_conv_w, b_mix_w, b_scale, ab_w_out, cd_w_in, c_q_norm_g, c_w_uq, c_kv_norm_g, c_w_ukv, d_ln_g, d_ln_b, d_w_s, d_b_s, cd_w_out, ffn_w_up, ffn_conv_w, ffn_w_down, final_norm_g, loss_target, m_ada_w, m_ada_b, m_norm1_g, m_norm2_g, m_ab_w_in, m_a_conv_w, m_b_mix_w, m_b_scale, m_ab_w_out, m_cd_w_in, m_c_q_norm_g, m_c_w_uq, m_c_kv_norm_g, m_c_w_ukv, m_d_ln_g, m_d_ln_b, m_d_w_s, m_d_b_s, m_cd_w_out, m_ffn_w_up, m_ffn_conv_w, m_ffn_w_down, m_final_norm_g, v_ada_w, v_ada_b, v_norm1_g, v_norm2_g, v_ab_w_in, v_a_conv_w, v_b_mix_w, v_b_scale, v_ab_w_out, v_cd_w_in, v_c_q_norm_g, v_c_w_uq, v_c_kv_norm_g, v_c_w_ukv, v_d_ln_g, v_d_ln_b, v_d_w_s, v_d_b_s, v_cd_w_out, v_ffn_w_up, v_ffn_conv_w, v_ffn_w_down, v_final_norm_g):
    args = (x, c, positions, ada_w, ada_b, norm1_g, norm2_g, ab_w_in, a_conv_w, b_mix_w, b_scale, ab_w_out, cd_w_in, c_q_norm_g, c_w_uq, c_kv_norm_g, c_w_ukv, d_ln_g, d_ln_b, d_w_s, d_b_s, cd_w_out, ffn_w_up, ffn_conv_w, ffn_w_down, final_norm_g, loss_target, m_ada_w, m_ada_b, m_norm1_g, m_norm2_g, m_ab_w_in, m_a_conv_w, m_b_mix_w, m_b_scale, m_ab_w_out, m_cd_w_in, m_c_q_norm_g, m_c_w_uq, m_c_kv_norm_g, m_c_w_ukv, m_d_ln_g, m_d_ln_b, m_d_w_s, m_d_b_s, m_cd_w_out, m_ffn_w_up, m_ffn_conv_w, m_ffn_w_down, m_final_norm_g, v_ada_w, v_ada_b, v_norm1_g, v_norm2_g, v_ab_w_in, v_a_conv_w, v_b_mix_w, v_b_scale, v_ab_w_out, v_cd_w_in, v_c_q_norm_g, v_c_w_uq, v_c_kv_norm_g, v_c_w_ukv, v_d_ln_g, v_d_ln_b, v_d_w_s, v_d_b_s, v_cd_w_out, v_ffn_w_up, v_ffn_conv_w, v_ffn_w_down, v_final_norm_g)
    a = dict(zip(_INPUTS, args, strict=True))
    xi, yi, ci = _place()
    chip = 2 * xi + yi
    dev = 4 * xi + 2 * yi + ci
    x = a["x"][0]
    tgt = a["loss_target"][0]

    bf = lambda t: t.astype(BF16)
    mix0_handle, tok = start_gather([bf(a["ab_w_in"][0]), bf(a["ab_w_out"][0])], "mix0")
    up0_16, down0_16, up1_16, down1_16 = [bf(a[n][l]) for l in (0, 1) for n in ("ffn_w_up", "ffn_w_down")]
    mix1_16 = [bf(a[n][0]) for n in ("cd_w_in", "c_w_uq", "c_w_ukv", "cd_w_out")]

    small_parts = [a["c"] + tok] + [a[n] for n, _, _ in _SMALL_SHARDED]
    rows1 = -(-sum(p.size for p in small_parts) // LANES // 8) * 8
    g1 = all_gather8(_pack_rows(small_parts, rows1, F32), "gather_small",
                     [up0_16, down0_16, up1_16, down1_16, mix1_16[0], mix1_16[3]]).reshape(N_DEV, rows1 * LANES)
    c_all = g1[:, :D_MODEL]
    per_chip = g1[0::2]
    small_full = {}
    off = D_MODEL
    for n, shp, axis in _SMALL_SHARDED:
        piece = per_chip[:, off:off + _size(shp)].reshape((N_CHIPS,) + shp)
        small_full[n] = jnp.concatenate([piece[k] for k in range(N_CHIPS)], axis=axis)
        off += _size(shp)

    merge = lambda t: t.reshape(t.shape[0] * t.shape[1], t.shape[2])
    w = dict(norm1_g=a["norm1_g"], norm2_g=a["norm2_g"], b_mix_w=a["b_mix_w"][0], b_scale=a["b_scale"],
             c_kv_norm_g=a["c_kv_norm_g"], d_w_s=a["d_w_s"][0], d_b_s=a["d_b_s"][0],
             final_norm_g=a["final_norm_g"].reshape(1, D_MODEL), **small_full)

    ncol = N_MOD * D_MODEL // N_CHIPS
    ada_b_mine = lax.dynamic_slice_in_dim(a["ada_b"], chip * ncol, ncol, axis=1)
    mod_cols = ada_mod(c_all, a["ada_w"], ada_b_mine)
    g2_rows = all_gather8(mod_cols.reshape(-1, LANES), "gather_mod")
    g2 = g2_rows.reshape(N_DEV, 2, N_DEV, ncol)
    mod = lax.dynamic_index_in_dim(g2[0::2], dev, axis=2, keepdims=False)
    mod = mod.transpose(1, 0, 2).reshape(2, N_MOD * D_MODEL)

    late = [g2_rows]
    up0_handle, tok_a = start_gather([up0_16], "up0", late)
    down0_handle, tok_b = start_gather([down0_16], "down0", late)
    mix1_handle, tok_c = start_gather(mix1_16, "mix1", late)
    ffn1_handle, tok_d = start_gather([up1_16, down1_16], "ffn1", late)
    mod = mod + (tok_a + tok_b + tok_c + tok_d)

    ropes = rope_tables(a["positions"][0])
    cm16 = lambda t: chip_major(t).astype(BF16)
    w.update(ffn_w_up=[None, None], ffn_w_down=[None, None])
    handles = dict(mix0=mix0_handle, up0=up0_handle, down0=down0_handle, mix1=mix1_handle, ffn1=ffn1_handle)
    reducing, reduced = {}, {}

    class Hooks(StepHooks):
        def weights(self, stage, after):
            got = finish_gather(handles[stage], chip, stage, after)
            if stage == "mix0":
                w.update(ab_w_in=got[0], ab_w_out=merge(got[1]))
            elif stage == "up0":
                w["ffn_w_up"][0] = got[0]
            elif stage == "down0":
                w["ffn_w_down"][0] = merge(got[0])
            elif stage == "mix1":
                cd_in, uq, ukv, cd_out = got
                w.update(prepare_weights(dict(cd_w_in=from_chip_major(cd_in), c_w_uq=from_chip_major(uq),
                                              c_w_ukv=from_chip_major(ukv), cd_w_out=merge(cd_out))))
            else:
                w["ffn_w_up"][1], w["ffn_w_down"][1] = got[0], merge(got[1])

        def gradients(self, stage, grads, after):
            if stage in ("ffn0", "ffn1"):
                parts = [grads["ffn_w_up"], _rows_major(grads["ffn_w_down"])]
            elif stage == "mix1":
                grads.update(unprepare_grads(grads))
                parts = [cm16(grads["cd_w_in"]), cm16(grads["c_w_uq"]), cm16(grads["c_w_ukv"]),
                         _rows_major(grads["cd_w_out"]).astype(BF16)]
            else:
                parts = [grads["ab_w_in"], _rows_major(grads["ab_w_out"])]
            reducing[stage], tok = start_reduce(parts, ci, stage)
            before = {"mix1": "ffn1", "ffn0": "mix1", "mix0": "ffn0"}.get(stage)
            if before is not None:
                reduced[before] = finish_reduce(reducing[before], chip, ci, before, after)
            return tok

    loss, grad_x, dmod, by_stage = run_step(x, tgt, mod, ropes, w, Hooks())
    grads = merge_grads(by_stage)

    parts3 = [dmod] + [grads[n] for n, _ in _SMALL_GRADS] + [loss[0, 0]]
    rows3 = -(-sum(p.size for p in parts3) // LANES // 8) * 8
    small_handle, _ = split_start("small_grads_start", EVERYONE, [_pack_rows(parts3, rows3, F32)],
                                  [_sds((N_DEV, rows3, LANES))])
    red_up1, red_down1 = reduced["ffn1"]
    red_cd_in, red_uq, red_ukv, red_cd_out = reduced["mix1"]
    red_up0, red_down0 = reduced["ffn0"]
    out_grads = dict(cd_w_in=red_cd_in, c_w_uq=red_uq, c_w_ukv=red_ukv, cd_w_out=red_cd_out)
    per_layer = dict(ffn_w_up=(red_up0, red_up1), ffn_w_down=(red_down0, red_down1))
    updates = {}

    def update(n):
        if n in per_layer:
            updates[n] = adamw_layers(a[n], *per_layer[n], a["m_" + n], a["v_" + n], "adamw_" + n)
        else:
            updates[n] = adamw(a[n], out_grads[n].reshape(a[n].shape), a["m_" + n], a["v_" + n], "adamw_" + n)

    early =("ffn_w_up", "ffn_w_down", "cd_w_in", "c_w_uq", "c_w_ukv", "cd_w_out")
    for n in early:
        update(n)
    (mine,), (landed,) = split_wait("small_grads_wait", EVERYONE, small_handle, [updates[n][1] for n in early])
    g3 = lax.dynamic_update_index_in_dim(landed, mine, dev, 0)
    summed = sum8(g3).reshape(-1)
    nmod = 2 * N_MOD * D_MODEL
    out_grads["ada_b"] = summed[:nmod].reshape(2, N_MOD * D_MODEL)
    off = nmod
    for n, shp in _SMALL_GRADS:
        out_grads[n] = summed[off:off + _size(shp)].reshape(shp)
        off += _size(shp)
    loss = summed[off]
    for n, shp, axis in _SMALL_SHARDED:
        width = out_grads[n].shape[-1] // N_CHIPS
        out_grads[n] = lax.dynamic_slice_in_dim(out_grads[n], chip * width, width, axis=out_grads[n].ndim - 1)
    dmod_all = g3.reshape(N_DEV, rows3 * LANES)[:, :nmod].reshape(N_DEV, 2, N_MOD * D_MODEL)
    dmod_mine = lax.dynamic_slice_in_dim(dmod_all, chip * ncol, ncol, axis=2).transpose(1, 0, 2)
    updates["ada_w"] = adamw_ada(a["ada_w"], c_all, dmod_mine, a["m_ada_w"], a["v_ada_w"])

    red_in0, red_out0 = finish_reduce(reducing["mix0"], chip, ci, "mix0", updates["ada_w"][1])
    out_grads.update(ab_w_in=red_in0, ab_w_out=red_out0)

    for n in ("ab_w_in", "ab_w_out"):
        update(n)
    small = [n for n in _WEIGHTS if n not in updates]
    for n, res in zip(small, adamw_small([a[n] for n in small], [out_grads[n].reshape(a[n].shape) for n in small],
                                         [a["m_" + n] for n in small], [a["v_" + n] for n in small])):
        updates[n] = res
    return (loss, grad_x[None], *[updates[n][i] for i in range(4) for n in _WEIGHTS])
```

```python
import functools

import jax
import jax.numpy as jnp
from jax import lax
from jax.experimental import pallas as pl
from jax.experimental.pallas import tpu as pltpu

F32 = jnp.float32
BF16 = jnp.bfloat16
EPS = 1e-6
D_MODEL = 1024
N_MOD = 6
A_WIDTH = 512
B_GROUPS = 4
POOL_WINDOWS = (2, 4, 8, 16)
C_HEADS = 8
C_NOPE = 64
C_ROPE = 32
C_V = 64
C_Q_RANK = 256
C_KV_RANK = 128
HEAD_PAD = 128
ROPE_THETA = 10000.0
D_GROUPS = 4
D_CHUNK = 128
D_FF = 2816
FF_UNIT = 128
ADAM_LR = 0.001
ADAM_B1 = 0.9
ADAM_B2 = 0.999
ADAM_EPS = 1e-08
ADAM_WD = 0.01
ADAM_STEP = 10
N_CHIPS = 4
N_DEV = 8
LANES = 128
VMEM_BIG = 56 * 1024 * 1024
MESH = pl.DeviceIdType.MESH


def _sds(shape, dtype=F32):
    return jax.ShapeDtypeStruct(tuple(shape), dtype)


def _tile(n, cap, mult=128):
    if n <= cap:
        return n
    best = None
    for t in range(mult, cap + 1, mult):
        if n % t == 0:
            best = t
    assert best is not None, (n, cap, mult)
    return best


def _params(dims=None, vmem=None):
    return pltpu.CompilerParams(dimension_semantics=dims, vmem_limit_bytes=vmem)


def _shift_down(v, k):
    r = pltpu.roll(v, k, axis=0)
    t = lax.broadcasted_iota(jnp.int32, v.shape, 0)
    return jnp.where(t >= k, r, 0.0)


def _shift_up(v, k):
    n = v.shape[0]
    r = pltpu.roll(v, n - k, axis=0)
    t = lax.broadcasted_iota(jnp.int32, v.shape, 0)
    return jnp.where(t < n - k, r, 0.0)


def _sigmoid(v):
    return 1.0 / (1.0 + jnp.exp(-v))


_GELU_C = 0.7978845608028654
_GELU_A = 0.044715


def _gelu(v):
    return 0.5 * v * (1.0 + jnp.tanh(_GELU_C * (v + _GELU_A * v * v * v)))


def _gelu_grad(v):
    th = jnp.tanh(_GELU_C * (v + _GELU_A * v * v * v))
    return 0.5 * (1.0 + th) + 0.5 * v * (1.0 - th * th) * _GELU_C * (1.0 + 3.0 * _GELU_A * v * v)


_NN = (((1,), (0,)), ((), ()))
_NT = (((1,), (1,)), ((), ()))
_TN = (((0,), (0,)), ((), ()))


def _dot(a, b, dims=_NN):
    return lax.dot_general(a, b, dims, preferred_element_type=F32)


def _logical(t, groups):
    return (t.shape[-2], t.shape[-1] * groups)


def _block(tr, tc, groups, cols, where):
    if groups == 1:
        return pl.BlockSpec((tr, tc), where)
    per = cols // groups // tc

    def index(i, j, s):
        r, c = where(i, j, s)
        return (c // per, r, c % per)

    return pl.BlockSpec((None, tr, tc), index)


def matmul(a, b, mode, out_dtype, name, ga=1, gb=1, go=1, tm=None, tn=None, tk=None):
    (ar, ac), (br, bc) = _logical(a, ga), _logical(b, gb)
    if mode == "nn":
        m, k, n = ar, ac, bc
        a_col, b_col = "k", "n"
    elif mode == "nt":
        m, k, n = ar, ac, br
        a_col, b_col = "k", "k"
    else:
        k, m, n = ar, ac, bc
        a_col, b_col = "m", "n"
    limit = {"m": m, "n": n // go, "k": k}
    limit[a_col] = min(limit[a_col], ac // ga)
    limit[b_col] = min(limit[b_col], bc // gb)
    tm = tm or _tile(limit["m"], 2048, 128 if mode == "tn" else 16)
    tn = tn or _tile(limit["n"], 512)
    tk = tk or _tile(limit["k"], 2048, 16 if mode == "tn" else 128)
    nk = k // tk
    if mode == "nn":
        a_spec = _block(tm, tk, ga, ac, lambda i, j, s: (i, s))
        b_spec = _block(tk, tn, gb, bc, lambda i, j, s: (s, j))
        dims = _NN
    elif mode == "nt":
        a_spec = _block(tm, tk, ga, ac, lambda i, j, s: (i, s))
        b_spec = _block(tn, tk, gb, bc, lambda i, j, s: (j, s))
        dims = _NT
    else:
        a_spec = _block(tk, tm, ga, ac, lambda i, j, s: (s, i))
        b_spec = _block(tk, tn, gb, bc, lambda i, j, s: (s, j))
        dims = _TN
    o_spec = _block(tm, tn, go, n, lambda i, j, s: (i, j))
    out_shape = _sds((m, n), out_dtype) if go == 1 else _sds((go, m, n // go), out_dtype)

    def body(a_ref, b_ref, o_ref, acc_ref):
        s = pl.program_id(2)

        @pl.when(s == 0)
        def _():
            acc_ref[...] = jnp.zeros_like(acc_ref)

        acc_ref[...] += _dot(a_ref[...], b_ref[...], dims)

        @pl.when(s == nk - 1)
        def _():
            o_ref[...] = acc_ref[...].astype(o_ref.dtype)

    return pl.pallas_call(
        body, name=name, out_shape=out_shape, grid=(m // tm, n // tn, nk),
        in_specs=[a_spec, b_spec], out_specs=o_spec,
        scratch_shapes=[pltpu.VMEM((tm, tn), F32)],
        compiler_params=_params(("parallel", "parallel", "arbitrary"), VMEM_BIG),
    )(a, b)


def _rows(tm, n):
    return pl.BlockSpec((tm, n), lambda i: (i, 0))


def _vec(n):
    return pl.BlockSpec((1, n), lambda i: (0, 0))


def modnorm_fwd(x, g, sc, sh, name):
    s, d = x.shape
    tm = _tile(s, 256, 8)

    def body(x_ref, g_ref, sc_ref, sh_ref, o_ref):
        xv = x_ref[...]
        r = lax.rsqrt(jnp.mean(xv * xv, axis=-1, keepdims=True) + EPS)
        o_ref[...] = ((xv * r) * g_ref[...] * (1.0 + sc_ref[...]) + sh_ref[...]).astype(BF16)

    return pl.pallas_call(
        body, name=name, out_shape=_sds((s, d), BF16), grid=(s // tm,),
        in_specs=[_rows(tm, d), _vec(d), _vec(d), _vec(d)], out_specs=_rows(tm, d),
        compiler_params=_params(("parallel",)),
    )(x, g, sc, sh)


def norm_bwd(x, dh, g, sc, dres, name):
    s, d = x.shape
    tm = _tile(s, 256, 8)
    nsteps = s // tm

    def body(x_ref, dh_ref, g_ref, sc_ref, dr_ref, dx_ref, dsh_ref, dsc_ref, dg_ref, a2_ref):
        i = pl.program_id(0)

        @pl.when(i == 0)
        def _():
            dsh_ref[...] = jnp.zeros_like(dsh_ref)
            a2_ref[...] = jnp.zeros_like(a2_ref)

        xv = x_ref[...]
        dh = dh_ref[...]
        r = lax.rsqrt(jnp.mean(xv * xv, axis=-1, keepdims=True) + EPS)
        xh = xv * r
        dsh_ref[...] += jnp.sum(dh, axis=0, keepdims=True)
        a2_ref[...] += jnp.sum(dh * xh, axis=0, keepdims=True)
        dxh = dh * (g_ref[...] * (1.0 + sc_ref[...]))
        dx = r * (dxh - xh * jnp.mean(dxh * xh, axis=-1, keepdims=True))
        dx_ref[...] = dr_ref[...] + dx

        @pl.when(i == nsteps - 1)
        def _():
            dsc_ref[...] = a2_ref[...] * g_ref[...]
            dg_ref[...] = a2_ref[...] * (1.0 + sc_ref[...])

    return pl.pallas_call(
        body, name=name, out_shape=(_sds((s, d)), _sds((1, d)), _sds((1, d)), _sds((1, d))), grid=(nsteps,),
        in_specs=[_rows(tm, d), _rows(tm, d), _vec(d), _vec(d), _rows(tm, d)],
        out_specs=(_rows(tm, d), _vec(d), _vec(d), _vec(d)),
        scratch_shapes=[pltpu.VMEM((1, d), F32)],
        compiler_params=_params(("arbitrary",)),
    )(x, dh, g, sc, dres)


def resid_modnorm_fwd(x, y, gate, g, sc, sh, name):
    s, d = x.shape
    tm = _tile(s, 256, 8)

    def body(x_ref, y_ref, gate_ref, g_ref, sc_ref, sh_ref, xo_ref, h_ref):
        xv = x_ref[...] + gate_ref[...] * y_ref[...].astype(F32)
        xo_ref[...] = xv
        r = lax.rsqrt(jnp.mean(xv * xv, axis=-1, keepdims=True) + EPS)
        h_ref[...] = ((xv * r) * g_ref[...] * (1.0 + sc_ref[...]) + sh_ref[...]).astype(BF16)

    return pl.pallas_call(
        body, name=name, out_shape=(_sds((s, d)), _sds((s, d), BF16)), grid=(s // tm,),
        in_specs=[_rows(tm, d), _rows(tm, d), _vec(d), _vec(d), _vec(d), _vec(d)], out_specs=(_rows(tm, d), _rows(tm, d)),
        compiler_params=_params(("parallel",)),
    )(x, y, gate, g, sc, sh)


def norm_gate_bwd(x, dh, g, sc, dres, y, gate, name):
    s, d = x.shape
    tm = _tile(s, 256, 8)
    nsteps = s // tm

    def body(x_ref, dh_ref, g_ref, sc_ref, dr_ref, y_ref, gate_ref, dx_ref, dsh_ref, dsc_ref, dg_ref, dy_ref,
             dgate_ref, a2_ref):
        i = pl.program_id(0)

        @pl.when(i == 0)
        def _():
            dsh_ref[...] = jnp.zeros_like(dsh_ref)
            a2_ref[...] = jnp.zeros_like(a2_ref)
            dgate_ref[...] = jnp.zeros_like(dgate_ref)

        xv = x_ref[...]
        dh = dh_ref[...]
        r = lax.rsqrt(jnp.mean(xv * xv, axis=-1, keepdims=True) + EPS)
        xh = xv * r
        dsh_ref[...] += jnp.sum(dh, axis=0, keepdims=True)
        a2_ref[...] += jnp.sum(dh * xh, axis=0, keepdims=True)
        dxh = dh * (g_ref[...] * (1.0 + sc_ref[...]))
        dr = dr_ref[...] + r * (dxh - xh * jnp.mean(dxh * xh, axis=-1, keepdims=True))
        dx_ref[...] = dr
        dy_ref[...] = (dr * gate_ref[...]).astype(BF16)
        dgate_ref[...] += jnp.sum(dr * y_ref[...].astype(F32), axis=0, keepdims=True)

        @pl.when(i == nsteps - 1)
        def _():
            dsc_ref[...] = a2_ref[...] * g_ref[...]
            dg_ref[...] = a2_ref[...] * (1.0 + sc_ref[...])

    vec = _sds((1, d))
    return pl.pallas_call(
        body, name=name, out_shape=(_sds((s, d)), vec, vec, vec, _sds((s, d), BF16), vec), grid=(nsteps,),
        in_specs=[_rows(tm, d), _rows(tm, d), _vec(d), _vec(d), _rows(tm, d), _rows(tm, d), _vec(d)],
        out_specs=(_rows(tm, d), _vec(d), _vec(d), _vec(d), _rows(tm, d), _vec(d)),
        scratch_shapes=[pltpu.VMEM((1, d), F32)],
        compiler_params=_params(("arbitrary",)),
    )(x, dh, g, sc, dres, y, gate)


def final_fused(x, f, gate, g, tgt):
    s, d = x.shape
    tm = _tile(s, 256, 8)

    def body(x_ref, f_ref, gate_ref, g_ref, t_ref, dx_ref, dg_ref, loss_ref, df_ref, dgate_ref):
        @pl.when(pl.program_id(0) == 0)
        def _():
            dg_ref[...] = jnp.zeros_like(dg_ref)
            loss_ref[...] = jnp.zeros_like(loss_ref)
            dgate_ref[...] = jnp.zeros_like(dgate_ref)

        fv, gatev, gv = f_ref[...].astype(F32), gate_ref[...], g_ref[...]
        xv = x_ref[...] + gatev * fv
        r = lax.rsqrt(jnp.mean(xv * xv, axis=-1, keepdims=True) + EPS)
        xh = xv * r
        e = xh * gv - t_ref[...]
        row = jnp.sum(e * e, axis=-1, keepdims=True) * (0.5 / d)
        loss_ref[...] += jnp.sum(row, axis=0, keepdims=True)
        dy = e * (1.0 / d)
        dg_ref[...] += jnp.sum(dy * xh, axis=0, keepdims=True)
        dxh = dy * gv
        dx = r * (dxh - xh * jnp.mean(dxh * xh, axis=-1, keepdims=True))
        dx_ref[...] = dx
        df_ref[...] = (dx * gatev).astype(BF16)
        dgate_ref[...] += jnp.sum(dx * fv, axis=0, keepdims=True)

    vec = _sds((1, d))
    return pl.pallas_call(
        body, name="final_fused", out_shape=(_sds((s, d)), vec, _sds((1, LANES)), _sds((s, d), BF16), vec),
        grid=(s // tm,),
        in_specs=[_rows(tm, d), _rows(tm, d), _vec(d), _vec(d), _rows(tm, d)],
        out_specs=(_rows(tm, d), _vec(d), _vec(LANES), _rows(tm, d), _vec(d)),
        compiler_params=_params(("arbitrary",)),
    )(x, f, gate, g, tgt)


def _taps(v):
    return _shift_down(v, 2), _shift_down(v, 1), v


def _conv3_taps(taps, w):
    return w[0:1, :] * taps[0] + w[1:2, :] * taps[1] + w[2:3, :] * taps[2]


def _conv3(v, w):
    return _conv3_taps(_taps(v), w)


def _conv3_t(dv, w):
    return w[0:1, :] * _shift_up(dv, 2) + w[1:2, :] * _shift_up(dv, 1) + w[2:3, :] * dv


def _conv3_dw_taps(dv, taps):
    return jnp.concatenate([jnp.sum(dv * t, axis=0, keepdims=True) for t in taps], axis=0)


def _conv3_dw(dv, v):
    return _conv3_dw_taps(dv, _taps(v))


def gconv_fwd(z, conv_w):
    s = z.shape[0]
    nb = A_WIDTH // LANES

    def body(b_ref, c_ref, a_ref, w_ref, o_ref):
        b, c, a = b_ref[...].astype(F32), c_ref[...].astype(F32), a_ref[...].astype(F32)
        o_ref[...] = (b * _conv3(c * a, w_ref[...])).astype(BF16)

    col = lambda off: pl.BlockSpec((s, LANES), lambda j: (0, off + j))
    return pl.pallas_call(
        body, name="gconv_fwd", out_shape=_sds((s, A_WIDTH), BF16), grid=(nb,),
        in_specs=[col(0), col(nb), col(2 * nb), pl.BlockSpec((3, LANES), lambda j: (0, j))],
        out_specs=pl.BlockSpec((s, LANES), lambda j: (0, j)),
        compiler_params=_params(("parallel",), VMEM_BIG),
    )(z, z, z, conv_w)


def gconv_bwd(z, conv_w, dycat):
    s = z.shape[0]
    nb = A_WIDTH // LANES

    def body(b_ref, c_ref, a_ref, w_ref, dy_ref, db_ref, dc_ref, da_ref, dw_ref):
        c, a, w, dy = c_ref[...].astype(F32), a_ref[...].astype(F32), w_ref[...], dy_ref[...].astype(F32)
        ca = c * a
        db_ref[...] = (dy * _conv3(ca, w)).astype(BF16)
        dconv = dy * b_ref[...].astype(F32)
        dw_ref[...] = _conv3_dw(dconv, ca)
        dca = _conv3_t(dconv, w)
        dc_ref[...] = (dca * a).astype(BF16)
        da_ref[...] = (dca * c).astype(BF16)

    col = lambda off: pl.BlockSpec((s, LANES), lambda j: (0, off + j))
    wspec = pl.BlockSpec((3, LANES), lambda j: (0, j))
    part = _sds((s, A_WIDTH), BF16)
    return pl.pallas_call(
        body, name="gconv_bwd", out_shape=(part, part, part, _sds((3, A_WIDTH))), grid=(nb,),
        in_specs=[col(0), col(nb), col(2 * nb), wspec, col(0)],
        out_specs=(col(0), col(0), col(0), wspec),
        compiler_params=_params(("parallel",), VMEM_BIG),
    )(z, z, z, conv_w, dycat)


def _pool_counts(s, w):
    t = lax.broadcasted_iota(jnp.int32, (s, 1), 0)
    return jnp.minimum(t + 1, w).astype(F32)


def _pooled(p, levels):
    acc = p
    for lv in range(levels):
        acc = acc + _shift_down(acc, 2 ** lv)
    return acc / _pool_counts(p.shape[0], 2 ** levels) - p


_B_WIDTH = B_GROUPS * LANES


def pool_fwd(z, mix_w, scale):
    s = z.shape[0]

    def body(p_ref, m_ref, sc_ref, o_ref):
        for g in range(B_GROUPS):
            cols = slice(g * LANES, (g + 1) * LANES)
            pooled = _pooled(p_ref[:, cols].astype(F32), g + 1)
            y = _dot(pooled.astype(BF16), m_ref[g].astype(BF16))
            o_ref[:, cols] = (y * sc_ref[:, cols]).astype(BF16)

    return pl.pallas_call(
        body, name="pool_fwd", out_shape=_sds((s, _B_WIDTH), BF16), grid=(1,),
        in_specs=[pl.BlockSpec((s, _B_WIDTH), lambda i: (0, 3 * A_WIDTH // _B_WIDTH)),
                  pl.BlockSpec((B_GROUPS, LANES, LANES), lambda i: (0, 0, 0)), pl.BlockSpec((1, _B_WIDTH), lambda i: (0, 0))],
        out_specs=pl.BlockSpec((s, _B_WIDTH), lambda i: (0, 0)),
        compiler_params=_params(("arbitrary",), VMEM_BIG),
    )(z, mix_w, scale)


def pool_bwd(z, mix_w, scale, dycat):
    s = z.shape[0]

    def body(p_ref, m_ref, sc_ref, dy_ref, dp_ref, dm_ref, dsc_ref):
        for g in range(B_GROUPS):
            cols = slice(g * LANES, (g + 1) * LANES)
            pooled = _pooled(p_ref[:, cols].astype(F32), g + 1)
            mw = m_ref[g].astype(BF16)
            pb = pooled.astype(BF16)
            dy = dy_ref[:, cols].astype(F32)
            dsc_ref[:, cols] = jnp.sum(dy * _dot(pb, mw), axis=0, keepdims=True)
            dmix = (dy * sc_ref[:, cols]).astype(BF16)
            dm_ref[g] = _dot(pb, dmix, _TN)
            dpool = _dot(dmix, mw, _NT)
            acc = dpool / _pool_counts(s, 2 ** (g + 1))
            for lv in range(g + 1):
                acc = acc + _shift_up(acc, 2 ** lv)
            dp_ref[:, cols] = (acc - dpool).astype(BF16)

    wide = lambda c: pl.BlockSpec((s, _B_WIDTH), lambda i: (0, c))
    mspec = pl.BlockSpec((B_GROUPS, LANES, LANES), lambda i: (0, 0, 0))
    vspec = pl.BlockSpec((1, _B_WIDTH), lambda i: (0, 0))
    return pl.pallas_call(
        body, name="pool_bwd", out_shape=(_sds((s, _B_WIDTH), BF16), _sds((B_GROUPS, LANES, LANES)), _sds((1, _B_WIDTH))),
        grid=(1,), in_specs=[wide(3 * A_WIDTH // _B_WIDTH), mspec, vspec, wide(A_WIDTH // _B_WIDTH)],
        out_specs=(wide(0), mspec, vspec),
        compiler_params=_params(("arbitrary",), VMEM_BIG),
    )(z, mix_w, scale, dycat)


_FF_BLOCKS = D_FF // FF_UNIT


def _ff_spec(s):
    return pl.BlockSpec((2, s, FF_UNIT), lambda j: (0, 0, j))


def _ff_wspecs():
    return [pl.BlockSpec((3, FF_UNIT), lambda j: (0, j)), pl.BlockSpec((3, FF_UNIT), lambda j: (0, _FF_BLOCKS + j))]


_FF_ROWS = 64
_FF_HALO = 16


def _chunk_taps(z_ref, half, c):
    start = pl.multiple_of(c * _FF_ROWS, _FF_ROWS)
    before = pl.multiple_of(jnp.maximum(c * _FF_ROWS - _FF_HALO, 0), _FF_HALO)
    halo = z_ref[half, pl.ds(before, _FF_HALO), :].astype(F32)
    halo = jnp.where(c > 0, halo, 0.0)
    win = jnp.concatenate([halo, z_ref[half, pl.ds(start, _FF_ROWS), :].astype(F32)], axis=0)
    return tuple(pltpu.roll(win, k, axis=0)[_FF_HALO:] for k in (2, 1)) + (win[_FF_HALO:],)


def _fold8(v):
    acc = v[0:8]
    for r in range(8, v.shape[0], 8):
        acc = acc + v[r:r + 8]
    return acc


_FF_CHUNK = 256


def ffn_act_down(zf, conv_w, w_down, name):
    s, d = zf.shape[1], w_down.shape[1]
    nk = D_FF // _FF_CHUNK
    chunk = lambda k: jnp.minimum(k, nk - 1)

    def body(z_ref, wg_ref, wu_ref, wd_ref, a_ref, f_ref, held_ref, acc_ref):
        k = pl.program_id(0)

        @pl.when(k == 0)
        def _():
            held_ref[...] = jnp.zeros_like(held_ref)
            acc_ref[...] = jnp.zeros_like(acc_ref)

        acc_ref[...] += _dot(held_ref[(k + 1) % 2], wd_ref[...])
        g = _conv3(z_ref[0].astype(F32), wg_ref[...])
        u = _conv3(z_ref[1].astype(F32), wu_ref[...])
        act = (g * _sigmoid(g) * u).astype(BF16)
        a_ref[...] = act
        held_ref[k % 2] = act

        @pl.when(k == nk)
        def _():
            f_ref[...] = acc_ref[...].astype(BF16)

    return pl.pallas_call(
        body, name=name, out_shape=(_sds((s, D_FF), BF16), _sds((s, d), BF16)), grid=(nk + 1,),
        in_specs=[pl.BlockSpec((2, s, _FF_CHUNK), lambda k: (0, 0, chunk(k))),
                  pl.BlockSpec((3, _FF_CHUNK), lambda k: (0, chunk(k))),
                  pl.BlockSpec((3, _FF_CHUNK), lambda k: (0, nk + chunk(k))),
                  pl.BlockSpec((_FF_CHUNK, d), lambda k: (jnp.maximum(k - 1, 0), 0))],
        out_specs=(pl.BlockSpec((s, _FF_CHUNK), lambda k: (0, chunk(k))), pl.BlockSpec((s, d), lambda k: (0, 0))),
        scratch_shapes=[pltpu.VMEM((2, s, _FF_CHUNK), BF16), pltpu.VMEM((s, d), F32)],
        compiler_params=_params(("arbitrary",), VMEM_BIG),
    )(zf, conv_w, conv_w, w_down)


def ffn_act_bwd(zf, conv_w, da, name):
    s = zf.shape[1]
    assert s % _FF_ROWS == 0
    nchunks = s // _FF_ROWS

    def body(z_ref, wg_ref, wu_ref, da_ref, dz_ref, dw_ref, dg_ref, du_ref):
        wg, wu = wg_ref[...], wu_ref[...]

        def first(c, acc):
            rows = pl.ds(pl.multiple_of(c * _FF_ROWS, _FF_ROWS), _FF_ROWS)
            tg, tu = _chunk_taps(z_ref, 0, c), _chunk_taps(z_ref, 1, c)
            g = _conv3_taps(tg, wg)
            u = _conv3_taps(tu, wu)
            dav = da_ref[rows, :].astype(F32)
            sg = _sigmoid(g)
            dg = dav * u * (sg * (1.0 + g * (1.0 - sg)))
            du = dav * (g * sg)
            dg_ref[rows, :] = dg
            du_ref[rows, :] = du
            return tuple(a + _fold8(d * t) for a, (d, t) in zip(acc, [(dg, t) for t in tg] + [(du, t) for t in tu]))

        zero = jnp.zeros((8, FF_UNIT), F32)
        acc = lax.fori_loop(0, nchunks, first, (zero,) * 6)
        sums = [jnp.sum(a, axis=0, keepdims=True) for a in acc]
        dw_ref[0] = jnp.concatenate(sums[:3], axis=0)
        dw_ref[1] = jnp.concatenate(sums[3:], axis=0)

        tail = pl.ds(s, _FF_HALO)
        dg_ref[tail, :] = jnp.zeros((_FF_HALO, FF_UNIT), F32)
        du_ref[tail, :] = jnp.zeros((_FF_HALO, FF_UNIT), F32)
        span = _FF_ROWS + _FF_HALO

        def second(c, carry):
            start = pl.multiple_of(c * _FF_ROWS, _FF_ROWS)
            for half, (d_ref, w) in enumerate(((dg_ref, wg), (du_ref, wu))):
                win = d_ref[pl.ds(start, span), :]
                dz = (w[0:1, :] * pltpu.roll(win, span - 2, axis=0)[:_FF_ROWS]
                      + w[1:2, :] * pltpu.roll(win, span - 1, axis=0)[:_FF_ROWS] + w[2:3, :] * win[:_FF_ROWS])
                dz_ref[half, pl.ds(start, _FF_ROWS), :] = dz.astype(BF16)
            return carry

        lax.fori_loop(0, nchunks, second, 0)

    return pl.pallas_call(
        body, name=name, out_shape=(_sds((2, s, D_FF), BF16), _sds((2, 3, D_FF))), grid=(_FF_BLOCKS,),
        in_specs=[_ff_spec(s)] + _ff_wspecs() + [pl.BlockSpec((s, FF_UNIT), lambda j: (0, j))],
        out_specs=(_ff_spec(s), pl.BlockSpec((2, 3, FF_UNIT), lambda j: (0, 0, j))),
        scratch_shapes=[pltpu.VMEM((s + _FF_HALO, FF_UNIT), F32), pltpu.VMEM((s + _FF_HALO, FF_UNIT), F32)],
        compiler_params=_params(("parallel",), VMEM_BIG),
    )(zf, conv_w, conv_w, da)


def _rope(v, cs, s1, s2):
    return v * cs + pltpu.roll(v, LANES - C_ROPE // 2, axis=1) * s1 + pltpu.roll(v, C_ROPE // 2, axis=1) * s2


def _rope_t(dv, cs, s1, s2):
    return dv * cs + pltpu.roll(dv * s1, C_ROPE // 2, axis=1) + pltpu.roll(dv * s2, LANES - C_ROPE // 2, axis=1)


def _kpe_mask(shape):
    lane = lax.broadcasted_iota(jnp.int32, shape, 1)
    return (lane >= C_NOPE) & (lane < C_NOPE + C_ROPE)


def _rms(v, g):
    r = lax.rsqrt(jnp.mean(v * v, axis=-1, keepdims=True) + EPS)
    return v * r, r


def _rms_bwd(dn, xh, r, g):
    dxh = dn * g
    return r * (dxh - xh * jnp.mean(dxh * xh, axis=-1, keepdims=True)), jnp.sum(dn * xh, axis=0, keepdims=True)


_ZQ = C_Q_RANK + C_KV_RANK + HEAD_PAD
_HW = C_HEADS * HEAD_PAD


def mla_pre_fwd(z, gq, gkv, wq, wk, wv, cs, s1, s2):
    s = z.shape[0]
    tm = _tile(s, 256, 8)

    def body(z_ref, gq_ref, gkv_ref, wq_ref, wk_ref, wv_ref, cs_ref, s1_ref, s2_ref, q_ref, k_ref, v_ref):
        zv = z_ref[...].astype(F32)
        cst, s1t, s2t = cs_ref[...], s1_ref[...], s2_ref[...]
        qh, _ = _rms(zv[:, :C_Q_RANK], None)
        qn = (qh * gq_ref[...]).astype(BF16)
        q = _dot(qn, wq_ref[...])
        kh, _ = _rms(zv[:, C_Q_RANK:C_Q_RANK + C_KV_RANK], None)
        kvn = (kh * gkv_ref[...]).astype(BF16)
        k = _dot(kvn, wk_ref[...])
        v_ref[...] = _dot(kvn, wv_ref[...]).astype(BF16)
        kpe = _rope(zv[:, C_Q_RANK + C_KV_RANK:], cst, s1t, s2t)
        for h in range(C_HEADS):
            sl = slice(h * HEAD_PAD, (h + 1) * HEAD_PAD)
            q_ref[:, sl] = _rope(q[:, sl], cst, s1t, s2t).astype(BF16)
            k_ref[:, sl] = (k[:, sl] + kpe).astype(BF16)

    full = lambda r, c: pl.BlockSpec((r, c), lambda i: (0, 0))
    hw = _sds((s, _HW), BF16)
    return pl.pallas_call(
        body, name="mla_pre_fwd", out_shape=(hw, hw, hw), grid=(s // tm,),
        in_specs=[_rows(tm, _ZQ), _vec(C_Q_RANK), _vec(C_KV_RANK), full(C_Q_RANK, _HW), full(C_KV_RANK, _HW),
                  full(C_KV_RANK, _HW), _rows(tm, LANES), _rows(tm, LANES), _rows(tm, LANES)],
        out_specs=(_rows(tm, _HW), _rows(tm, _HW), _rows(tm, _HW)),
        compiler_params=_params(("parallel",), VMEM_BIG),
    )(z, gq, gkv, wq, wk, wv, cs, s1, s2)


def mla_pre_bwd(z, gq, gkv, wq, wk, wv, cs, s1, s2, dq, dk, dv):
    s = z.shape[0]
    tm = _tile(s, 256, 8)

    def body(z_ref, gq_ref, gkv_ref, wq_ref, wk_ref, wv_ref, cs_ref, s1_ref, s2_ref, dq_ref, dk_ref, dv_ref,
             dz_ref, dwq_ref, dwk_ref, dwv_ref, dgq_ref, dgkv_ref):
        @pl.when(pl.program_id(0) == 0)
        def _():
            dwq_ref[...] = jnp.zeros_like(dwq_ref)
            dwk_ref[...] = jnp.zeros_like(dwk_ref)
            dwv_ref[...] = jnp.zeros_like(dwv_ref)
            dgq_ref[...] = jnp.zeros_like(dgq_ref)
            dgkv_ref[...] = jnp.zeros_like(dgkv_ref)

        zv = z_ref[...].astype(F32)
        cst, s1t, s2t = cs_ref[...], s1_ref[...], s2_ref[...]
        gqv, gkvv = gq_ref[...], gkv_ref[...]
        qh, rq = _rms(zv[:, :C_Q_RANK], None)
        qn = (qh * gqv).astype(BF16)
        kh, rk = _rms(zv[:, C_Q_RANK:C_Q_RANK + C_KV_RANK], None)
        kvn = (kh * gkvv).astype(BF16)

        dqv = dq_ref[...].astype(F32)
        dqp = jnp.concatenate(
            [_rope_t(dqv[:, h * HEAD_PAD:(h + 1) * HEAD_PAD], cst, s1t, s2t) for h in range(C_HEADS)], axis=1
        ).astype(BF16)
        dwq_ref[...] += _dot(qn, dqp, _TN)
        dqn = _dot(dqp, wq_ref[...], _NT)
        dql, dgq = _rms_bwd(dqn, qh, rq, gqv)
        dgq_ref[...] += dgq

        dkv = dk_ref[...]
        dkb = dkv.astype(BF16)
        dvb = dv_ref[...].astype(BF16)
        dwk_ref[...] += _dot(kvn, dkb, _TN)
        dwv_ref[...] += _dot(kvn, dvb, _TN)
        dkvn = _dot(dkb, wk_ref[...], _NT) + _dot(dvb, wv_ref[...], _NT)
        dkl, dgkv = _rms_bwd(dkvn, kh, rk, gkvv)
        dgkv_ref[...] += dgkv

        dkpe = dkv[:, :HEAD_PAD]
        for h in range(1, C_HEADS):
            dkpe = dkpe + dkv[:, h * HEAD_PAD:(h + 1) * HEAD_PAD]
        dkpe = _rope_t(jnp.where(_kpe_mask(dkpe.shape), dkpe, 0.0), cst, s1t, s2t)
        dz_ref[...] = jnp.concatenate([dql, dkl, dkpe], axis=1).astype(BF16)

    full = lambda r, c: pl.BlockSpec((r, c), lambda i: (0, 0))
    return pl.pallas_call(
        body, name="mla_pre_bwd",
        out_shape=(_sds((s, _ZQ), BF16), _sds((C_Q_RANK, _HW)), _sds((C_KV_RANK, _HW)), _sds((C_KV_RANK, _HW)),
                   _sds((1, C_Q_RANK)), _sds((1, C_KV_RANK))),
        grid=(s // tm,),
        in_specs=[_rows(tm, _ZQ), _vec(C_Q_RANK), _vec(C_KV_RANK), full(C_Q_RANK, _HW), full(C_KV_RANK, _HW),
                  full(C_KV_RANK, _HW), _rows(tm, LANES), _rows(tm, LANES), _rows(tm, LANES),
                  _rows(tm, _HW), _rows(tm, _HW), _rows(tm, _HW)],
        out_specs=(_rows(tm, _ZQ), full(C_Q_RANK, _HW), full(C_KV_RANK, _HW), full(C_KV_RANK, _HW),
                   _vec(C_Q_RANK), _vec(C_KV_RANK)),
        compiler_params=_params(("arbitrary",), VMEM_BIG),
    )(z, gq, gkv, wq, wk, wv, cs, s1, s2, dq, dk, dv)


_ATT_SCALE = (C_NOPE + C_ROPE) ** -0.5
_NEG = -1e30


def _att_exp(q, k, row0, ends_here):
    sc = _dot(q, k, _NT) * _ATT_SCALE
    tq, nk = sc.shape
    if ends_here:
        last = sc[:, nk - tq:]
        row = lax.broadcasted_iota(jnp.int32, last.shape, 0)
        col = lax.broadcasted_iota(jnp.int32, last.shape, 1)
        last = jnp.where(col <= row, last, _NEG)
        sc = last if nk == tq else jnp.concatenate([sc[:, :nk - tq], last], axis=1)
    else:
        qpos = row0 + lax.broadcasted_iota(jnp.int32, sc.shape, 0)
        kpos = lax.broadcasted_iota(jnp.int32, sc.shape, 1)
        sc = jnp.where(kpos <= qpos, sc, _NEG)
    e = jnp.exp(sc - jnp.max(sc, axis=-1, keepdims=True))
    return e, 1.0 / jnp.sum(e, axis=-1, keepdims=True)


def _causal_cases(i, nq, tq, fn):
    if nq > 8:
        fn(nq * tq, False)
        return
    for blk in range(nq):
        pl.when(i == blk)(functools.partial(fn, (blk + 1) * tq, True))


def attn_fwd(q, k, v):
    s = q.shape[0]
    tq = _tile(s, 256, 8)
    nq = s // tq

    def body(q_ref, k_ref, v_ref, o_ref):
        i = pl.program_id(1)

        def case(nk, ends_here):
            e, inv = _att_exp(q_ref[...], k_ref[:nk, :], i * tq, ends_here)
            o_ref[...] = (_dot(e.astype(BF16), v_ref[:nk, :]) * inv).astype(BF16)

        _causal_cases(i, nq, tq, case)

    qspec = pl.BlockSpec((tq, HEAD_PAD), lambda h, i: (i, h))
    kspec = pl.BlockSpec((s, HEAD_PAD), lambda h, i: (0, h))
    return pl.pallas_call(
        body, name="attn_fwd", out_shape=_sds((s, _HW), BF16), grid=(C_HEADS, s // tq),
        in_specs=[qspec, kspec, kspec], out_specs=qspec,
        compiler_params=_params(("parallel", "parallel"), VMEM_BIG),
    )(q, k, v)


def attn_bwd(q, k, v, o, do_all, do_col0):
    s = q.shape[0]
    tq = _tile(s, 256, 8)

    def body(q_ref, k_ref, v_ref, o_ref, do_ref, dq_ref, dk_ref, dv_ref):
        i = pl.program_id(1)

        @pl.when(i == 0)
        def _():
            dk_ref[...] = jnp.zeros_like(dk_ref)
            dv_ref[...] = jnp.zeros_like(dv_ref)

        def case(nk, ends_here):
            qv, kv, vv, dov = q_ref[...], k_ref[:nk, :], v_ref[:nk, :], do_ref[...]
            e, inv = _att_exp(qv, kv, i * tq, ends_here)
            p = e * inv
            dp = _dot(dov, vv, _NT)
            delta = jnp.sum(dov.astype(F32) * o_ref[...].astype(F32), axis=-1, keepdims=True)
            ds = (p * (dp - delta) * _ATT_SCALE).astype(BF16)
            dq_ref[...] = _dot(ds, kv).astype(BF16)
            dk_ref[:nk, :] += _dot(ds, qv, _TN)
            dv_ref[:nk, :] += _dot(p.astype(BF16), dov, _TN)

        _causal_cases(i, s // tq, tq, case)

    qspec = pl.BlockSpec((tq, HEAD_PAD), lambda h, i: (i, h))
    dospec = pl.BlockSpec((tq, HEAD_PAD), lambda h, i: (i, do_col0 + h))
    kspec = pl.BlockSpec((s, HEAD_PAD), lambda h, i: (0, h))
    return pl.pallas_call(
        body, name="attn_bwd", out_shape=(_sds((s, _HW), BF16), _sds((s, _HW)), _sds((s, _HW))),
        grid=(C_HEADS, s // tq),
        in_specs=[qspec, kspec, kspec, qspec, dospec], out_specs=(qspec, kspec, kspec),
        compiler_params=_params(("parallel", "arbitrary"), VMEM_BIG),
    )(q, k, v, o, do_all)


_DW = D_GROUPS * LANES


def _tril_bf16(w):
    r = lax.broadcasted_iota(jnp.int32, w.shape, 0)
    c = lax.broadcasted_iota(jnp.int32, w.shape, 1)
    return jnp.where(c <= r, w, 0.0).astype(BF16)


def _sgu_forward(zu, zv, lg, lb, ws_ref, bs):
    u = _gelu(zu)
    v = _gelu(zv)
    mu = jnp.mean(v, axis=-1, keepdims=True)
    vc = v - mu
    rstd = lax.rsqrt(jnp.mean(vc * vc, axis=-1, keepdims=True) + EPS)
    xh = vc * rstd
    vln = (xh * lg + lb).astype(BF16)
    mixed = []
    for g in range(D_GROUPS):
        wg = _tril_bf16(ws_ref[g])
        mixed.append(_dot(wg, vln[:, g * LANES:(g + 1) * LANES]) + bs[:, g:g + 1])
    return u, xh, rstd, vln, jnp.concatenate(mixed, axis=1)


def sgu_fwd(z, lg, lb, ws, bs_t):
    s = z.shape[0]
    nchunk = s // D_CHUNK

    def body(zu_ref, zv_ref, lg_ref, lb_ref, ws_ref, bs_ref, o_ref):
        u, _, _, _, mixed = _sgu_forward(zu_ref[...].astype(F32), zv_ref[...].astype(F32), lg_ref[...], lb_ref[...],
                                         ws_ref, bs_ref[...])
        o_ref[...] = (u * mixed).astype(BF16)

    return pl.pallas_call(
        body, name="sgu_fwd", out_shape=_sds((s, _DW), BF16), grid=(nchunk,),
        in_specs=[pl.BlockSpec((D_CHUNK, _DW), lambda n: (n, 1)), pl.BlockSpec((D_CHUNK, _DW), lambda n: (n, 2)),
                  _vec(_DW), _vec(_DW), pl.BlockSpec((D_GROUPS, D_CHUNK, D_CHUNK), lambda n: (0, 0, 0)),
                  pl.BlockSpec((D_CHUNK, LANES), lambda n: (0, 0))],
        out_specs=pl.BlockSpec((D_CHUNK, _DW), lambda n: (n, 0)),
        compiler_params=_params(("parallel",)),
    )(z, z, lg, lb, ws, bs_t)


def sgu_bwd(z, lg, lb, ws, bs_t, dycat, dy_col):
    s = z.shape[0]
    nchunk = s // D_CHUNK

    def body(zu_ref, zv_ref, lg_ref, lb_ref, ws_ref, bs_ref, dy_ref, dzu_ref, dzv_ref, dws_ref, dbs_ref, dlg_ref,
             dlb_ref):
        @pl.when(pl.program_id(0) == 0)
        def _():
            dws_ref[...] = jnp.zeros_like(dws_ref)
            dbs_ref[...] = jnp.zeros_like(dbs_ref)
            dlg_ref[...] = jnp.zeros_like(dlg_ref)
            dlb_ref[...] = jnp.zeros_like(dlb_ref)

        zu, zv, lg = zu_ref[...].astype(F32), zv_ref[...].astype(F32), lg_ref[...]
        u, xh, rstd, vln, mixed = _sgu_forward(zu, zv, lg, lb_ref[...], ws_ref, bs_ref[...])
        dy = dy_ref[...].astype(F32)
        dzu_ref[...] = (dy * mixed * _gelu_grad(zu)).astype(BF16)
        dmix = dy * u
        lane = lax.broadcasted_iota(jnp.int32, (D_CHUNK, LANES), 1)
        row = lax.broadcasted_iota(jnp.int32, (D_CHUNK, D_CHUNK), 0)
        colm = lax.broadcasted_iota(jnp.int32, (D_CHUNK, D_CHUNK), 1)
        dvln = []
        dbs = jnp.zeros((D_CHUNK, LANES), F32)
        for g in range(D_GROUPS):
            sl = slice(g * LANES, (g + 1) * LANES)
            dmg = dmix[:, sl]
            dbs = dbs + jnp.where(lane == g, jnp.sum(dmg, axis=-1, keepdims=True), 0.0)
            dmb = dmg.astype(BF16)
            dws_ref[g] += jnp.where(colm <= row, _dot(dmb, vln[:, sl], _NT), 0.0)
            dvln.append(_dot(_tril_bf16(ws_ref[g]), dmb, _TN))
        dbs_ref[...] += dbs
        dvln = jnp.concatenate(dvln, axis=1)
        dlg_ref[...] += jnp.sum(dvln * xh, axis=0, keepdims=True)
        dlb_ref[...] += jnp.sum(dvln, axis=0, keepdims=True)
        dxh = dvln * lg
        dvv = rstd * (dxh - jnp.mean(dxh, axis=-1, keepdims=True) - xh * jnp.mean(dxh * xh, axis=-1, keepdims=True))
        dzv_ref[...] = (dvv * _gelu_grad(zv)).astype(BF16)

    wsspec = pl.BlockSpec((D_GROUPS, D_CHUNK, D_CHUNK), lambda n: (0, 0, 0))
    chunk = lambda cidx: pl.BlockSpec((D_CHUNK, _DW), lambda n: (n, cidx))
    return pl.pallas_call(
        body, name="sgu_bwd",
        out_shape=(_sds((s, _DW), BF16), _sds((s, _DW), BF16), _sds((D_GROUPS, D_CHUNK, D_CHUNK)),
                   _sds((D_CHUNK, LANES)), _sds((1, _DW)), _sds((1, _DW))),
        grid=(nchunk,),
        in_specs=[chunk(1), chunk(2), _vec(_DW), _vec(_DW), wsspec, pl.BlockSpec((D_CHUNK, LANES), lambda n: (0, 0)),
                  chunk(dy_col)],
        out_specs=(chunk(0), chunk(0), wsspec, pl.BlockSpec((D_CHUNK, LANES), lambda n: (0, 0)), _vec(_DW), _vec(_DW)),
        compiler_params=_params(("arbitrary",)),
    )(z, z, lg, lb, ws, bs_t, dycat)


def ada_mod(c_all, ada_w, ada_b):
    nl, d, n = ada_w.shape
    nb = c_all.shape[0]
    tn = _tile(n, 512)

    def body(c_ref, w_ref, b_ref, o_ref):
        cv = c_ref[...]
        ca = (cv * _sigmoid(cv)).astype(BF16)
        o_ref[...] = _dot(ca, w_ref[...].astype(BF16)) + b_ref[...]

    return pl.pallas_call(
        body, name="ada_mod", out_shape=_sds((nl, nb, n)), grid=(nl, n // tn),
        in_specs=[pl.BlockSpec((nb, d), lambda l, j: (0, 0)), pl.BlockSpec((None, d, tn), lambda l, j: (l, 0, j)),
                  pl.BlockSpec((None, 1, tn), lambda l, j: (l, 0, j))],
        out_specs=pl.BlockSpec((None, nb, tn), lambda l, j: (l, 0, j)),
        compiler_params=_params(("parallel", "parallel")),
    )(c_all, ada_w, ada_b.reshape(nl, 1, n))


_ADAM_BLOCK = 256 * 1024


def _adam_rows(rows, cols):
    if rows * cols <= _ADAM_BLOCK or rows % 8:
        return rows
    return _tile(rows, max(8, _ADAM_BLOCK // cols), 8)


def _adam_update(w, gv, m, v):
    inv_bc1 = 1.0 / (1.0 - ADAM_B1 ** ADAM_STEP)
    inv_bc2 = 1.0 / (1.0 - ADAM_B2 ** ADAM_STEP)
    nm = ADAM_B1 * m + (1.0 - ADAM_B1) * gv
    nv = ADAM_B2 * v + (1.0 - ADAM_B2) * (gv * gv)
    return -ADAM_LR * ((nm * inv_bc1) / (jnp.sqrt(nv * inv_bc2) + ADAM_EPS) + ADAM_WD * w), nm, nv


def adamw(w, g, m, v, name):
    shape = w.shape
    cols = shape[-1]
    rows = w.size // cols
    tr = _adam_rows(rows, cols)

    def body(w_ref, g_ref, m_ref, v_ref, go_ref, d_ref, nm_ref, nv_ref):
        gv = g_ref[...]
        go_ref[...] = gv
        d_ref[...], nm_ref[...], nv_ref[...] = _adam_update(w_ref[...], gv, m_ref[...], v_ref[...])

    spec = pl.BlockSpec((tr, cols), lambda i: (i, 0))
    out = _sds((rows, cols))
    r2 = lambda t: t.reshape(rows, cols)
    res = pl.pallas_call(
        body, name=name, out_shape=(out,) * 4, grid=(rows // tr,),
        in_specs=[spec] * 4, out_specs=(spec,) * 4, compiler_params=_params(("parallel",)),
    )(r2(w), r2(g), r2(m), r2(v))
    return tuple(t.reshape(shape) for t in res)


def adamw_ada(w, c_all, dmod, m, v):
    nl, d, n = w.shape
    tr = _adam_rows(d, n)
    pad = 16 - c_all.shape[0]
    c16 = jnp.pad(c_all, ((0, pad), (0, 0)))
    dm16 = jnp.pad(dmod, ((0, 0), (0, pad), (0, 0)))

    def body(w_ref, c_ref, dm_ref, m_ref, v_ref, g_ref, d_ref, nm_ref, nv_ref):
        cv = c_ref[...]
        gv = _dot((cv * _sigmoid(cv)).astype(BF16), dm_ref[...].astype(BF16), _TN)
        g_ref[...] = gv
        d_ref[...], nm_ref[...], nv_ref[...] = _adam_update(w_ref[...], gv, m_ref[...], v_ref[...])

    spec = pl.BlockSpec((None, tr, n), lambda l, i: (l, i, 0))
    out = _sds((nl, d, n))
    return pl.pallas_call(
        body, name="adamw_ada_w", out_shape=(out, out, out, out), grid=(nl, d // tr),
        in_specs=[spec, pl.BlockSpec((16, tr), lambda l, i: (0, i)), pl.BlockSpec((None, 16, n), lambda l, i: (l, 0, 0)),
                  spec, spec],
        out_specs=(spec,) * 4, compiler_params=_params(("parallel", "parallel")),
    )(w, c16, dm16, m, v)


def adamw_small(ws, gs, ms, vs):
    n = len(ws)
    flat = lambda t: t.reshape(-1, t.shape[-1])

    def body(*refs):
        ins, outs = refs[:4 * n], refs[4 * n:]
        for i in range(n):
            w_ref, g_ref, m_ref, v_ref = ins[4 * i:4 * i + 4]
            outs[3 * i][...], outs[3 * i + 1][...], outs[3 * i + 2][...] = _adam_update(
                w_ref[...], g_ref[...], m_ref[...], v_ref[...])

    operands = [flat(t) for quad in zip(ws, gs, ms, vs) for t in quad]
    res = pl.pallas_call(
        body, name="adamw_small", out_shape=tuple(_sds(flat(w).shape) for w in ws for _ in range(3)),
    )(*operands)
    return [(g, res[3 * i].reshape(w.shape), res[3 * i + 1].reshape(w.shape), res[3 * i + 2].reshape(w.shape))
            for i, (w, g) in enumerate(zip(ws, gs))]


def adamw_layers(w, g0, g1, m, v, name):
    _, rows, cols = w.shape
    tr = _adam_rows(rows, cols)

    def body(w_ref, g0_ref, g1_ref, m_ref, v_ref, g_ref, d_ref, nm_ref, nv_ref):
        gv = jnp.where(pl.program_id(0) == 0, g0_ref[...], g1_ref[...])
        g_ref[...] = gv
        d_ref[...], nm_ref[...], nv_ref[...] = _adam_update(w_ref[...], gv, m_ref[...], v_ref[...])

    spec = pl.BlockSpec((None, tr, cols), lambda l, i: (l, i, 0))
    gspec = pl.BlockSpec((tr, cols), lambda l, i: (i, 0))
    out = _sds((2, rows, cols))
    return pl.pallas_call(
        body, name=name, out_shape=(out, out, out, out), grid=(2, rows // tr),
        in_specs=[spec, gspec, gspec, spec, spec], out_specs=(spec,) * 4, compiler_params=_params(("parallel", "parallel")),
    )(w, g0, g1, m, v)


def sum8(gathered):
    _, r, _ = gathered.shape
    tr = _tile(r, 512, 8)

    def body(g_ref, o_ref):
        acc = g_ref[0]
        for dev in range(1, N_DEV):
            acc = acc + g_ref[dev]
        o_ref[...] = acc

    return pl.pallas_call(
        body, name="sum8", out_shape=_sds((r, LANES)), grid=(r // tr,),
        in_specs=[pl.BlockSpec((N_DEV, tr, LANES), lambda i: (0, i, 0))], out_specs=pl.BlockSpec((tr, LANES), lambda i: (i, 0)),
        compiler_params=_params(("parallel",)),
    )(gathered)


_SUM_STEPS = 2


def pair_sums(gs, recvs, core, name):
    n = len(gs)
    trs = [g.shape[1] // 2 // _SUM_STEPS for g in gs]

    def body(c_ref, *refs):
        del c_ref
        for i in range(n):
            a_ref, b_ref, o_ref = refs[2 * i], refs[2 * i + 1], refs[2 * n + i]
            o_ref[...] = (a_ref[...].astype(F32) + b_ref[...].astype(F32)).astype(BF16)

    in_specs, out_specs = [], []
    for g, tr in zip(gs, trs):
        cols = g.shape[2]
        in_specs.append(pl.BlockSpec((None, tr, cols), lambda k, s, c: (k, c[0] * _SUM_STEPS + s, 0)))
        in_specs.append(pl.BlockSpec((None, tr, cols), lambda k, s, c: (k, s, 0)))
        out_specs.append(pl.BlockSpec((None, tr, cols), lambda k, s, c: (k, s, 0)))
    grid_spec = pltpu.PrefetchScalarGridSpec(num_scalar_prefetch=1, grid=(N_CHIPS, _SUM_STEPS), in_specs=in_specs,
                                             out_specs=tuple(out_specs))
    return list(pl.pallas_call(
        body, name=name, out_shape=tuple(_sds((N_CHIPS, g.shape[1] // 2, g.shape[2]), BF16) for g in gs),
        grid_spec=grid_spec, compiler_params=_params(("parallel", "parallel")),
    )(core.reshape(1).astype(jnp.int32), *[t for pair in zip(gs, recvs) for t in pair]))


def chip_sums(pairs, recvs, chip, core, name):
    n = len(pairs)
    trs = [p.shape[1] // _SUM_STEPS for p in pairs]

    def body(p_ref, *refs):
        del p_ref
        for i in range(n):
            own_ref, r_ref, o_ref = refs[2 * i], refs[2 * i + 1], refs[2 * n + i]
            acc = own_ref[...].astype(F32)
            for j in range(N_CHIPS - 1):
                acc = acc + r_ref[j].astype(F32)
            o_ref[...] = acc

    in_specs, out_specs = [], []
    for p, tr in zip(pairs, trs):
        cols = p.shape[2]
        in_specs.append(pl.BlockSpec((None, tr, cols), lambda s, q: (q[0], s, 0)))
        in_specs.append(pl.BlockSpec((N_CHIPS - 1, tr, cols), lambda s, q: (0, s, 0)))
        out_specs.append(pl.BlockSpec((None, tr, cols), lambda s, q: (q[1], s, 0)))
    grid_spec = pltpu.PrefetchScalarGridSpec(num_scalar_prefetch=1, grid=(_SUM_STEPS,), in_specs=in_specs,
                                             out_specs=tuple(out_specs))
    return list(pl.pallas_call(
        body, name=name, out_shape=tuple(_sds((2,) + p.shape[1:]) for p in pairs), grid_spec=grid_spec,
        compiler_params=_params(("parallel",)),
    )(jnp.stack([chip, core]).astype(jnp.int32), *[t for pair in zip(pairs, recvs) for t in pair]))


def _place():
    return lax.axis_index("x"), lax.axis_index("y"), lax.axis_index("c")


def _other_chips(x, y):
    return [(x, 1 - y), (1 - x, y), (1 - x, 1 - y)]


_HBM = pl.BlockSpec(memory_space=pltpu.HBM)


def all_gather8(v, name, after=()):
    m, n = v.shape

    def body(x_ref, *refs):
        out_ref, send_sems, recv_sems, local_sem = refs[len(after):]
        x, y, c = _place()
        me, sibling = (x, y, c), (x, y, 1 - c)
        chips = _other_chips(x, y)

        def rows(px, py, pc):
            return out_ref.at[pl.ds((4 * px + 2 * py + pc) * m, m), :]

        def copy(k, block, to, src=None):
            return pltpu.make_async_remote_copy(
                src_ref=rows(*block) if src is None else src, dst_ref=rows(*block),
                send_sem=send_sems.at[k], recv_sem=recv_sems.at[k], device_id=to, device_id_type=MESH)

        mine = pltpu.make_async_copy(x_ref, rows(*me), local_sem)
        mine.start()
        first = [copy(0, me, sibling, src=x_ref)]
        first += [copy(1 + j, me, (*chip, c), src=x_ref) for j, chip in enumerate(chips)]
        for cp in first:
            cp.start()
        passed = [copy(4 + j, (*chip, c), sibling) for j, chip in enumerate(chips)]
        for j, chip in enumerate(chips):
            copy(1 + j, (*chip, c), me).wait_recv()
            passed[j].start()
        copy(0, sibling, me).wait_recv()
        for j, chip in enumerate(chips):
            copy(4 + j, (*chip, 1 - c), me).wait_recv()
        for cp in first + passed:
            cp.wait_send()
        mine.wait()

    return pl.pallas_call(
        body, name=name, out_shape=_sds((N_DEV * m, n), v.dtype),
        in_specs=[pl.BlockSpec(memory_space=pltpu.VMEM)] + [pl.BlockSpec(memory_space=pl.ANY)] * len(after),
        out_specs=pl.BlockSpec(memory_space=pltpu.VMEM),
        scratch_shapes=[pltpu.SemaphoreType.DMA((7,)), pltpu.SemaphoreType.DMA((7,)), pltpu.SemaphoreType.DMA],
        compiler_params=_params(None, VMEM_BIG),
    )(v, *after)


def _comm_call(body, name, ins, out_shapes, nsem, aliases=None):
    return pl.pallas_call(
        body, name=name, out_shape=tuple(out_shapes), in_specs=[_HBM] * len(ins), out_specs=tuple([_HBM] * len(out_shapes)),
        scratch_shapes=[pltpu.SemaphoreType.DMA((nsem,)), pltpu.SemaphoreType.DMA((nsem,))],
        input_output_aliases=aliases or {},
    )(*ins)


def _remote(src, dst, send_sems, recv_sems, k, to):
    return pltpu.make_async_remote_copy(src_ref=src, dst_ref=dst, send_sem=send_sems.at[k], recv_sem=recv_sems.at[k],
                                        device_id=to, device_id_type=MESH)


def _half(core, rh):
    return pl.ds(pl.multiple_of(core * rh, 16), rh)


def swap_halves(gs, name):
    n = len(gs)

    def body(*refs):
        ins, outs, (send_sems, recv_sems) = refs[:n], refs[n:2 * n], refs[2 * n:]
        x, y, c = _place()
        copies = []
        for i in range(n):
            theirs = _half(1 - c, ins[i].shape[1] // 2)
            cp = _remote(ins[i].at[:, theirs], outs[i], send_sems, recv_sems, i, (x, y, 1 - c))
            cp.start()
            copies.append(cp)
        for cp in copies:
            cp.wait()

    return _comm_call(body, name, gs, [_sds((g.shape[0], g.shape[1] // 2, g.shape[2]), g.dtype) for g in gs], n)


def join_halves(bufs, name):
    n = len(bufs)

    def body(*refs):
        ins, outs, (send_sems, recv_sems) = refs[:n], refs[n:2 * n], refs[2 * n:]
        x, y, c = _place()
        copies = []
        for i in range(n):
            cp = _remote(ins[i].at[c], outs[i].at[c], send_sems, recv_sems, i, (x, y, 1 - c))
            cp.start()
            copies.append(cp)
        for i in range(n):
            theirs = outs[i].at[1 - c]
            _remote(theirs, theirs, send_sems, recv_sems, i, (x, y, 1 - c)).wait_recv()
        for cp in copies:
            cp.wait_send()

    return _comm_call(body, name, bufs, [_sds(b.shape, b.dtype) for b in bufs], n, {i: i for i in range(n)})


def forward_halves(lands, name):
    n = len(lands)

    def body(*refs):
        ins, outs, (send_sems, recv_sems) = refs[:n], refs[n:2 * n], refs[2 * n:]
        x, y, c = _place()
        sibling = (x, y, 1 - c)
        chips = _other_chips(x, y)
        copies = []
        for i in range(n):
            mine = _half(c, ins[i].shape[1] // 2)
            for j, (px, py) in enumerate(chips):
                cp = _remote(ins[i].at[2 * px + py, mine], outs[i].at[2 * px + py, mine], send_sems, recv_sems, 3 * i + j, sibling)
                cp.start()
                copies.append(cp)
        for i in range(n):
            theirs = _half(1 - c, ins[i].shape[1] // 2)
            for j, (px, py) in enumerate(chips):
                landed = outs[i].at[2 * px + py, theirs]
                _remote(landed, landed, send_sems, recv_sems, 3 * i + j, sibling).wait_recv()
        for cp in copies:
            cp.wait_send()

    return _comm_call(body, name, lands, [_sds(b.shape, b.dtype) for b in lands], 3 * n, {i: i for i in range(n)})


_SEM = pl.BlockSpec(memory_space=pltpu.SEMAPHORE)
_EFFECT = pltpu.SideEffectType.DATAFLOW_SIDE_EFFECTING


def _gather_copies(srcs, lands, send_sems, recv_sems):
    x, y, c = _place()
    copies = []
    for i in range(len(srcs)):
        mine = _half(c, srcs[i].shape[0] // 2)
        for j, chip in enumerate(_other_chips(x, y)):
            copies.append(_remote(srcs[i].at[mine], lands[i].at[2 * x + y, mine], send_sems, recv_sems, 3 * i + j, (*chip, c)))
    return copies


def _exchange_copies(srcs, lands, send_sems, recv_sems):
    x, y, c = _place()
    copies = []
    for i in range(len(srcs)):
        for j, (px, py) in enumerate(_other_chips(x, y)):
            copies.append(_remote(srcs[i].at[2 * px + py], lands[i].at[j], send_sems, recv_sems, 3 * i + j, (px, py, c)))
    return copies


def _everyone_copies(srcs, lands, send_sems, recv_sems):
    x, y, c = _place()
    flip = lambda v, b: 1 - v if b else v
    dst = lands[0].at[4 * x + 2 * y + c]
    return [_remote(srcs[0], dst, send_sems, recv_sems, j - 1, (flip(x, j & 4), flip(y, j & 2), flip(c, j & 1)))
            for j in range(1, N_DEV)]


GATHER = (_gather_copies, 3)
EXCHANGE = (_exchange_copies, 3)
EVERYONE = (_everyone_copies, N_DEV - 1)


def split_start(name, plan, srcs, land_shapes, after=()):
    copies_fn, per_source = plan
    n, m, k = len(srcs), len(land_shapes), len(after)
    ncopies = per_source * n

    def body(*refs):
        src_refs, land_refs = refs[:n], refs[n:n + m]
        send_sems, recv_sems = refs[n + m + k], refs[n + m + k + 1]
        token = refs[-1]
        for cp in copies_fn(src_refs, land_refs, send_sems, recv_sems):
            cp.start()
        token[...] = jnp.zeros_like(token)

    hbm = lambda s: pltpu.HBM(tuple(s.shape), s.dtype)
    outs = pl.pallas_call(
        body, name=name,
        out_shape=(pltpu.SemaphoreType.DMA((ncopies,)), pltpu.SemaphoreType.DMA((ncopies,)), *[hbm(s) for s in srcs],
                   *[hbm(s) for s in land_shapes], _sds((8, LANES))),
        in_specs=[_HBM] * (n + m) + [pl.BlockSpec(memory_space=pl.ANY)] * k,
        out_specs=(_SEM, _SEM, *([_HBM] * (n + m)), pl.BlockSpec(memory_space=pltpu.VMEM)),
        input_output_aliases={i: 2 + i for i in range(n + m)},
        compiler_params=pltpu.CompilerParams(has_side_effects=_EFFECT),
    )(*[pltpu.with_memory_space_constraint(s, pltpu.HBM) for s in srcs],
      *[pltpu.with_memory_space_constraint(lax.empty(tuple(s.shape), s.dtype), pltpu.HBM) for s in land_shapes], *after)
    handle = (outs[0], outs[1], list(outs[2:2 + n]), list(outs[2 + n:2 + n + m]))
    return handle, outs[-1][0, 0]


def split_wait(name, plan, handle, after):
    copies_fn, _ = plan
    send_sems, recv_sems, srcs, lands = handle
    n, m = len(srcs), len(lands)
    after = list(after) if isinstance(after, (list, tuple)) else [after]

    def body(*refs):
        src_refs, land_refs = refs[:n], refs[n:n + m]
        for cp in copies_fn(src_refs, land_refs, refs[n + m], refs[n + m + 1]):
            cp.wait_send()
            cp.wait_recv()

    hbm = lambda s: pltpu.HBM(tuple(s.shape), s.dtype)
    outs = pl.pallas_call(
        body, name=name, out_shape=tuple(hbm(s) for s in srcs + lands),
        in_specs=[_HBM] * (n + m) + [_SEM, _SEM] + [pl.BlockSpec(memory_space=pl.ANY)] * len(after),
        out_specs=tuple([_HBM] * (n + m)), input_output_aliases={i: i for i in range(n + m)},
        compiler_params=pltpu.CompilerParams(has_side_effects=_EFFECT),
    )(*srcs, *lands, send_sems, recv_sems, *after)
    return list(outs[:n]), list(outs[n:])


_CD_PAD = C_Q_RANK + C_KV_RANK + HEAD_PAD + 2 * _DW


def chip_major(w, groups=N_CHIPS):
    r, c = w.shape
    return w.reshape(r, groups, c // groups).transpose(1, 0, 2)


def from_chip_major(w):
    g, r, c = w.shape
    return w.transpose(1, 0, 2).reshape(r, g * c)


def _cd_in_pad(w):
    a = C_Q_RANK + C_KV_RANK
    z = lambda n: jnp.zeros((w.shape[0], n), w.dtype)
    return jnp.concatenate([w[:, :a], z(C_NOPE), w[:, a:a + C_ROPE], z(HEAD_PAD - C_NOPE - C_ROPE), w[:, a + C_ROPE:]], axis=1)


def _cd_in_unpad(w):
    a = C_Q_RANK + C_KV_RANK
    return jnp.concatenate([w[:, :a], w[:, a + C_NOPE:a + C_NOPE + C_ROPE], w[:, a + HEAD_PAD:]], axis=1)


def _pad_heads(w, width):
    r = w.shape[0]
    w = w.reshape(r, C_HEADS, width)
    return jnp.pad(w, ((0, 0), (0, 0), (0, HEAD_PAD - width))).reshape(r, _HW)


def _unpad_heads(w, width):
    r = w.shape[0]
    return w.reshape(r, C_HEADS, HEAD_PAD)[:, :, :width].reshape(r, C_HEADS * width)


_MATMUL_WEIGHTS = ("ab_w_in", "ab_w_out", "cd_w_in", "c_w_uq", "c_w_ukv", "cd_w_out", "ffn_w_up", "ffn_w_down")
_LAYER_STACKED = ("norm1_g", "norm2_g", "ffn_w_up", "ffn_conv_w", "ffn_w_down")
_ROW_VECTORS = ("b_scale", "c_q_norm_g", "c_kv_norm_g", "d_ln_g", "d_ln_b")


def full_to_local(p):
    q = {}
    for k, v in p.items():
        if k == "final_norm_g":
            v = v.reshape(1, -1)
        elif k not in _LAYER_STACKED and k not in _ROW_VECTORS:
            v = v[0]
        q[k] = v.astype(BF16) if k in _MATMUL_WEIGHTS else v
    return q


def local_to_full(g):
    q = {}
    for k, v in g.items():
        if k == "final_norm_g":
            q[k] = v.reshape(-1)
        elif k not in _LAYER_STACKED and k not in _ROW_VECTORS:
            q[k] = v[None]
        else:
            q[k] = v
    return q


def prepare_weights(p):
    q = dict(p)
    q["cd_w_in"] = _cd_in_pad(p["cd_w_in"])
    q["c_w_uq"] = _pad_heads(p["c_w_uq"], C_NOPE + C_ROPE)
    ukv = p["c_w_ukv"].reshape(C_KV_RANK, C_HEADS, C_NOPE + C_V)
    q["c_w_uk"] = _pad_heads(ukv[:, :, :C_NOPE].reshape(C_KV_RANK, -1), C_NOPE)
    q["c_w_uv"] = _pad_heads(ukv[:, :, C_NOPE:].reshape(C_KV_RANK, -1), C_V)
    wo = p["cd_w_out"]
    att_rows = jnp.pad(wo[:C_HEADS * C_V].reshape(C_HEADS, C_V, D_MODEL), ((0, 0), (0, HEAD_PAD - C_V), (0, 0)))
    q["cd_w_out"] = jnp.concatenate([att_rows.reshape(_HW, D_MODEL), wo[C_HEADS * C_V:]], axis=0)
    return q


def unprepare_grads(g):
    q = dict(g)
    q["cd_w_in"] = _cd_in_unpad(g["cd_w_in"])
    q["c_w_uq"] = _unpad_heads(g["c_w_uq"], C_NOPE + C_ROPE)
    uk = g.pop("c_w_uk").reshape(C_KV_RANK, C_HEADS, HEAD_PAD)[:, :, :C_NOPE]
    uv = g.pop("c_w_uv").reshape(C_KV_RANK, C_HEADS, HEAD_PAD)[:, :, :C_V]
    q.pop("c_w_uk", None)
    q.pop("c_w_uv", None)
    q["c_w_ukv"] = jnp.concatenate([uk, uv], axis=-1).reshape(C_KV_RANK, C_HEADS * (C_NOPE + C_V))
    wo = g["cd_w_out"]
    att = wo[:_HW].reshape(C_HEADS, HEAD_PAD, D_MODEL)[:, :C_V].reshape(C_HEADS * C_V, D_MODEL)
    q["cd_w_out"] = jnp.concatenate([att, wo[_HW:]], axis=0)
    return q


def rope_tables(positions):
    half = C_ROPE // 2
    inv_freq = ROPE_THETA ** (-jnp.arange(half, dtype=F32) / half)
    ang = positions.astype(F32)[:, None] * inv_freq
    cos, sin = jnp.cos(ang), jnp.sin(ang)
    s = positions.shape[0]
    z = lambda n: jnp.zeros((s, n), F32)
    cs = jnp.concatenate([jnp.ones((s, C_NOPE), F32), cos, cos, z(HEAD_PAD - C_NOPE - C_ROPE)], axis=1)
    s1 = jnp.concatenate([z(C_NOPE), -sin, z(HEAD_PAD - C_NOPE - half)], axis=1)
    s2 = jnp.concatenate([z(C_NOPE + half), sin, z(HEAD_PAD - C_NOPE - C_ROPE)], axis=1)
    return cs, s1, s2


def _mods(mod_l):
    return [mod_l[:, i * D_MODEL:(i + 1) * D_MODEL] for i in range(N_MOD)]


_UP_COLS = 2 * D_FF // N_CHIPS


def ffn_fwd(h2, w, l, late_down=None):
    zf = matmul(h2, w["ffn_w_up"][l], "nn", BF16, f"ffn_up{l}", gb=N_CHIPS, go=2, tn=_UP_COLS)
    if late_down is not None:
        late_down(zf)
    a, f = ffn_act_down(zf, w["ffn_conv_w"][l], w["ffn_w_down"][l], f"ffn_act_down{l}")
    return f, (zf, a)


def ffn_bwd(df, h2, saved, w, l):
    zf, a = saved
    da = matmul(df, w["ffn_w_down"][l], "nt", BF16, f"ffn_down_dx{l}", tn=D_FF // 2)
    d_down = matmul(a, df, "tn", BF16, f"ffn_down_dw{l}", tm=D_FF // 2)
    dzf, d_conv = ffn_act_bwd(zf, w["ffn_conv_w"][l], da, f"ffn_act_bwd{l}")
    dh2 = matmul(dzf, w["ffn_w_up"][l], "nt", F32, f"ffn_up_dx{l}", ga=2, gb=N_CHIPS, tk=_UP_COLS, tn=D_MODEL)
    d_up = matmul(h2, dzf, "tn", BF16, f"ffn_up_dw{l}", gb=2, go=N_CHIPS, tn=_UP_COLS)
    d_conv = d_conv.transpose(1, 0, 2).reshape(3, 2 * D_FF)
    return dh2, dict(ffn_w_down=d_down, ffn_conv_w=d_conv, ffn_w_up=d_up)


def mixer0_fwd(h, w):
    z = matmul(h, w["ab_w_in"], "nn", BF16, "ab_in", gb=N_CHIPS)
    ya = gconv_fwd(z, w["a_conv_w"])
    yb = pool_fwd(z, w["b_mix_w"], w["b_scale"])
    ycat = jnp.concatenate([ya, yb], axis=1)
    y = matmul(ycat, w["ab_w_out"], "nn", BF16, "ab_out", tn=D_MODEL)
    return y, (z, ycat)


def mixer0_bwd(dy, h, saved, w):
    z, ycat = saved
    grads = {}
    dycat = matmul(dy, w["ab_w_out"], "nt", BF16, "ab_out_dx")
    grads["ab_w_out"] = matmul(ycat, dy, "tn", BF16, "ab_out_dw")
    db, dc, da, d_conv = gconv_bwd(z, w["a_conv_w"], dycat)
    dp, d_mix, d_scale = pool_bwd(z, w["b_mix_w"], w["b_scale"], dycat)
    dz = jnp.concatenate([db, dc, da, dp], axis=1)
    dh = matmul(dz, w["ab_w_in"], "nt", F32, "ab_in_dx", gb=N_CHIPS, tn=D_MODEL)
    grads["ab_w_in"] = matmul(h, dz, "tn", BF16, "ab_in_dw", go=N_CHIPS)
    grads.update(a_conv_w=d_conv, b_mix_w=d_mix, b_scale=d_scale)
    return dh, grads


def mixer1_fwd(h, ropes, w):
    cs, s1, s2 = ropes
    z = matmul(h, w["cd_w_in"], "nn", BF16, "cd_in")
    bs_t = jnp.pad(w["d_b_s"].T, ((0, 0), (0, LANES - D_GROUPS)))
    qh, kh, vh = mla_pre_fwd(z, w["c_q_norm_g"], w["c_kv_norm_g"], w["c_w_uq"], w["c_w_uk"], w["c_w_uv"], cs, s1, s2)
    oh = attn_fwd(qh, kh, vh)
    yd = sgu_fwd(z, w["d_ln_g"], w["d_ln_b"], w["d_w_s"], bs_t)
    ycat = jnp.concatenate([oh, yd], axis=1)
    y = matmul(ycat, w["cd_w_out"], "nn", BF16, "cd_out", tn=D_MODEL)
    return y, (z, bs_t, qh, kh, vh, oh, ycat)


def mixer1_bwd(dy, h, saved, ropes, w):
    cs, s1, s2 = ropes
    z, bs_t, qh, kh, vh, oh, ycat = saved
    grads = {}
    dycat = matmul(dy, w["cd_w_out"], "nt", BF16, "cd_out_dx")
    grads["cd_w_out"] = matmul(ycat, dy, "tn", F32, "cd_out_dw")
    dqh, dkh, dvh = attn_bwd(qh, kh, vh, oh, dycat, 0)
    dzq, d_uq, d_uk, d_uv, d_gq, d_gkv = mla_pre_bwd(
        z, w["c_q_norm_g"], w["c_kv_norm_g"], w["c_w_uq"], w["c_w_uk"], w["c_w_uv"], cs, s1, s2, dqh, dkh, dvh)
    dzu, dzv, d_ws, d_bs, d_lg, d_lb = sgu_bwd(z, w["d_ln_g"], w["d_ln_b"], w["d_w_s"], bs_t, dycat, _HW // _DW)
    dz = jnp.concatenate([dzq, dzu, dzv], axis=1)
    dh = matmul(dz, w["cd_w_in"], "nt", F32, "cd_in_dx", tn=D_MODEL)
    grads["cd_w_in"] = matmul(h, dz, "tn", F32, "cd_in_dw")
    grads.update(c_w_uq=d_uq, c_w_uk=d_uk, c_w_uv=d_uv, c_q_norm_g=d_gq, c_kv_norm_g=d_gkv, d_w_s=d_ws,
                 d_b_s=d_bs[:, :D_GROUPS].T, d_ln_g=d_lg, d_ln_b=d_lb)
    return dh, grads


class StepHooks:
    def weights(self, stage, after):
        pass

    def gradients(self, stage, grads, after):
        return 0.0


def run_step(x, tgt, mod, ropes, w, hooks):
    sh1a, sc1a, g1a, sh2a, sc2a, g2a = _mods(mod[0:1])
    sh1b, sc1b, g1b, sh2b, sc2b, g2b = _mods(mod[1:2])
    n1, n2 = w["norm1_g"], w["norm2_g"]

    hooks.weights("mix0", mod)
    h0 = modnorm_fwd(x, n1[0:1], sc1a, sh1a, "modnorm_0")
    y0, mix0 = mixer0_fwd(h0, w)
    x1, h1 = resid_modnorm_fwd(x, y0, g1a, n2[0:1], sc2a, sh2a, "resid_modnorm_1")
    hooks.weights("up0", x1)
    f0, ffn0 = ffn_fwd(h1, w, 0, lambda act: hooks.weights("down0", act))
    x2, h2 = resid_modnorm_fwd(x1, f0, g2a, n1[1:2], sc1b, sh1b, "resid_modnorm_2")
    hooks.weights("mix1", x2)
    y1, mix1 = mixer1_fwd(h2, ropes, w)
    x3, h3 = resid_modnorm_fwd(x2, y1, g1b, n2[1:2], sc2b, sh2b, "resid_modnorm_3")
    hooks.weights("ffn1", x3)
    f1, ffn1 = ffn_fwd(h3, w, 1)
    dres, d_final, loss, df1, dg2b = final_fused(x3, f1, g2b, w["final_norm_g"], tgt)

    dh3, gf1 = ffn_bwd(df1, h3, ffn1, w, 1)
    tok = hooks.gradients("ffn1", gf1, dh3)
    dres, dsh2b, dsc2b, dn2b, dy1, dg1b = norm_gate_bwd(x3, dh3, n2[1:2], sc2b, dres, y1, g1b + tok, "norm_gate_bwd_3")
    dh2, gm1 = mixer1_bwd(dy1, h2, mix1, ropes, w)
    tok = hooks.gradients("mix1", gm1, dh2)
    dres, dsh1b, dsc1b, dn1b, df0, dg2a = norm_gate_bwd(x2, dh2, n1[1:2], sc1b, dres, f0, g2a + tok, "norm_gate_bwd_2")
    dh1, gf0 = ffn_bwd(df0, h1, ffn0, w, 0)
    tok = hooks.gradients("ffn0", gf0, dh1)
    dres, dsh2a, dsc2a, dn2a, dy0, dg1a = norm_gate_bwd(x1, dh1, n2[0:1], sc2a, dres, y0, g1a + tok, "norm_gate_bwd_1")
    dh0, gm0 = mixer0_bwd(dy0, h0, mix0, w)
    tok = hooks.gradients("mix0", gm0, dh0)
    grad_x, dsh1a, dsc1a, dn1a = norm_bwd(x, dh0, n1[0:1], sc1a + tok, dres, "norm_bwd_0")

    dmod = jnp.concatenate([jnp.concatenate([dsh1a, dsc1a, dg1a, dsh2a, dsc2a, dg2a], axis=1),
                            jnp.concatenate([dsh1b, dsc1b, dg1b, dsh2b, dsc2b, dg2b], axis=1)], axis=0)
    norms = dict(norm1_g=jnp.concatenate([dn1a, dn1b], axis=0), norm2_g=jnp.concatenate([dn2a, dn2b], axis=0),
                 final_norm_g=d_final)
    return loss, grad_x, dmod, dict(mix0=gm0, ffn0=gf0, mix1=gm1, ffn1=gf1, norms=norms)


def merge_grads(by_stage):
    grads = {**by_stage["mix0"], **by_stage["mix1"], **by_stage["norms"]}
    for k in ("ffn_w_down", "ffn_w_up"):
        grads[k] = [by_stage["ffn0"][k], by_stage["ffn1"][k]]
    grads["ffn_conv_w"] = jnp.stack([by_stage["ffn0"]["ffn_conv_w"], by_stage["ffn1"]["ffn_conv_w"]])
    return grads


def local_step(x, tgt, mod, ropes, w):
    loss, grad_x, dmod, by_stage = run_step(x, tgt, mod, ropes, w, StepHooks())
    return loss, grad_x, dmod, merge_grads(by_stage)


_WEIGHTS = ("ada_w", "ada_b", "norm1_g", "norm2_g", "ab_w_in", "a_conv_w", "b_mix_w", "b_scale", "ab_w_out", "cd_w_in",
            "c_q_norm_g", "c_w_uq", "c_kv_norm_g", "c_w_ukv", "d_ln_g", "d_ln_b", "d_w_s", "d_b_s", "cd_w_out",
            "ffn_w_up", "ffn_conv_w", "ffn_w_down", "final_norm_g")
_INPUTS = ("x", "c", "positions") + _WEIGHTS + ("loss_target",) + tuple("m_" + n for n in _WEIGHTS) + tuple(
    "v_" + n for n in _WEIGHTS)

def _pack_rows(parts, rows, dtype):
    flat = jnp.concatenate([p.reshape(-1).astype(dtype) for p in parts])
    return jnp.pad(flat, (0, rows * LANES - flat.shape[0])).reshape(rows, LANES)


def _rows_major(w):
    r, c = w.shape
    return w.reshape(N_CHIPS, r // N_CHIPS, c)


def start_gather(shards, tag, after=()):
    lands = [_sds((N_CHIPS,) + s.shape, s.dtype) for s in shards]
    return split_start("gather_start_" + tag, GATHER, shards, lands, after)


def finish_gather(handle, chip, tag, after):
    shards, lands = split_wait("gather_wait_" + tag, GATHER, handle, after)
    lands = forward_halves(lands, "gather_forward_" + tag)
    return [lax.dynamic_update_index_in_dim(o, s, chip, 0) for o, s in zip(lands, shards)]


def start_reduce(gs, core, tag):
    recv = swap_halves(gs, "swap_halves_" + tag)
    pairs = pair_sums(gs, recv, core, "pair_sums_" + tag)
    lands = [_sds((N_CHIPS - 1,) + p.shape[1:], p.dtype) for p in pairs]
    return split_start("exchange_start_" + tag, EXCHANGE, pairs, lands)


def finish_reduce(handle, chip, core, tag, after):
    pairs, others = split_wait("exchange_wait_" + tag, EXCHANGE, handle, after)
    halves = chip_sums(pairs, others, chip, core, "chip_sums_" + tag)
    full = join_halves(halves, "join_halves_" + tag)
    return [f.reshape(f.shape[1] * 2, f.shape[2]) for f in full]


_SMALL_SHARDED = (("a_conv_w", (3, 128), 1), ("c_q_norm_g", (1, 64), 1), ("d_ln_g", (1, 128), 1), ("d_ln_b", (1, 128), 1),
                  ("ffn_conv_w", (2, 3, 2 * D_FF // N_CHIPS), 2))
_SMALL_GRADS = (("norm1_g", (2, D_MODEL)), ("norm2_g", (2, D_MODEL)), ("b_mix_w", (4, 128, 128)), ("b_scale", (1, 512)),
                ("c_kv_norm_g", (1, 128)), ("d_w_s", (4, 128, 128)), ("d_b_s", (4, 128)), ("final_norm_g", (1, D_MODEL)),
                ("a_conv_w", (3, 512)), ("c_q_norm_g", (1, 256)), ("d_ln_g", (1, 512)), ("d_ln_b", (1, 512)),
                ("ffn_conv_w", (2, 3, 2 * D_FF)))


def _size(shape):
    n = 1
    for d in shape:
        n *= d
    return n


def kernel(x, c, positions, ada_w, ada_b, norm1_g, norm2_g, ab_w_in, a_conv_w, b_mix_w, b_scale, ab_w_out, cd_w_in, c_q_norm_g, c_w_uq, c_kv_norm_g, c_w_ukv, d_ln_g, d_ln_b, d_w_s, d_b_s, cd_w_out, ffn_w_up, ffn_conv_w, ffn_w_down, final_norm_g, loss_target, m_ada_w, m_ada_b, m_norm1_g, m_norm2_g, m_ab_w_in, m_a_conv_w, m_b_mix_w, m_b_scale, m_ab_w_out, m_cd_w_in, m_c_q_norm_g, m_c_w_uq, m_c_kv_norm_g, m_c_w_ukv, m_d_ln_g, m_d_ln_b, m_d_w_s, m_d_b_s, m_cd_w_out, m_ffn_w_up, m_ffn_conv_w, m_ffn_w_down, m_final_norm_g, v_ada_w, v_ada_b, v_norm1_g, v_norm2_g, v_ab_w_in, v_a_conv_w, v_b_mix_w, v_b_scale, v_ab_w_out, v_cd_w_in, v_c_q_norm_g, v_c_w_uq, v_c_kv_norm_g, v_c_w_ukv, v_d_ln_g, v_d_ln_b, v_d_w_s, v_d_b_s, v_cd_w_out, v_ffn_w_up, v_ffn_conv_w, v_ffn_w_down, v_final_norm_g):
    args = (x, c, positions, ada_w, ada_b, norm1_g, norm2_g, ab_w_in, a_conv_w, b_mix_w, b_scale, ab_w_out, cd_w_in, c_q_norm_g, c_w_uq, c_kv_norm_g, c_w_ukv, d_ln_g, d_ln_b, d_w_s, d_b_s, cd_w_out, ffn_w_up, ffn_conv_w, ffn_w_down, final_norm_g, loss_target, m_ada_w, m_ada_b, m_norm1_g, m_norm2_g, m_ab_w_in, m_a_conv_w, m_b_mix_w, m_b_scale, m_ab_w_out, m_cd_w_in, m_c_q_norm_g, m_c_w_uq, m_c_kv_norm_g, m_c_w_ukv, m_d_ln_g, m_d_ln_b, m_d_w_s, m_d_b_s, m_cd_w_out, m_ffn_w_up, m_ffn_conv_w, m_ffn_w_down, m_final_norm_g, v_ada_w, v_ada_b, v_norm1_g, v_norm2_g, v_ab_w_in, v_a_conv_w, v_b_mix_w, v_b_scale, v_ab_w_out, v_cd_w_in, v_c_q_norm_g, v_c_w_uq, v_c_kv_norm_g, v_c_w_ukv, v_d_ln_g, v_d_ln_b, v_d_w_s, v_d_b_s, v_cd_w_out, v_ffn_w_up, v_ffn_conv_w, v_ffn_w_down, v_final_norm_g)
    a = dict(zip(_INPUTS, args, strict=True))
    xi, yi, ci = _place()
    chip = 2 * xi + yi
    dev = 4 * xi + 2 * yi + ci
    x = a["x"][0]
    tgt = a["loss_target"][0]

    bf = lambda t: t.astype(BF16)
    mix0_handle, tok = start_gather([bf(a["ab_w_in"][0]), bf(a["ab_w_out"][0])], "mix0")
    up0_16, down0_16, up1_16, down1_16 = [bf(a[n][l]) for l in (0, 1) for n in ("ffn_w_up", "ffn_w_down")]
    mix1_16 = [bf(a[n][0]) for n in ("cd_w_in", "c_w_uq", "c_w_ukv", "cd_w_out")]

    small_parts = [a["c"] + tok] + [a[n] for n, _, _ in _SMALL_SHARDED]
    rows1 = -(-sum(p.size for p in small_parts) // LANES // 8) * 8
    g1 = all_gather8(_pack_rows(small_parts, rows1, F32), "gather_small",
                     [up0_16, down0_16, up1_16, down1_16, mix1_16[0], mix1_16[3]]).reshape(N_DEV, rows1 * LANES)
    c_all = g1[:, :D_MODEL]
    per_chip = g1[0::2]
    small_full = {}
    off = D_MODEL
    for n, shp, axis in _SMALL_SHARDED:
        piece = per_chip[:, off:off + _size(shp)].reshape((N_CHIPS,) + shp)
        small_full[n] = jnp.concatenate([piece[k] for k in range(N_CHIPS)], axis=axis)
        off += _size(shp)

    merge = lambda t: t.reshape(t.shape[0] * t.shape[1], t.shape[2])
    w = dict(norm1_g=a["norm1_g"], norm2_g=a["norm2_g"], b_mix_w=a["b_mix_w"][0], b_scale=a["b_scale"],
             c_kv_norm_g=a["c_kv_norm_g"], d_w_s=a["d_w_s"][0], d_b_s=a["d_b_s"][0],
             final_norm_g=a["final_norm_g"].reshape(1, D_MODEL), **small_full)

    ncol = N_MOD * D_MODEL // N_CHIPS
    ada_b_mine = lax.dynamic_slice_in_dim(a["ada_b"], chip * ncol, ncol, axis=1)
    mod_cols = ada_mod(c_all, a["ada_w"], ada_b_mine)
    g2_rows = all_gather8(mod_cols.reshape(-1, LANES), "gather_mod")
    g2 = g2_rows.reshape(N_DEV, 2, N_DEV, ncol)
    mod = lax.dynamic_index_in_dim(g2[0::2], dev, axis=2, keepdims=False)
    mod = mod.transpose(1, 0, 2).reshape(2, N_MOD * D_MODEL)

    late = [g2_rows]
    up0_handle, tok_a = start_gather([up0_16], "up0", late)
    down0_handle, tok_b = start_gather([down0_16], "down0", late)
    mix1_handle, tok_c = start_gather(mix1_16, "mix1", late)
    ffn1_handle, tok_d = start_gather([up1_16, down1_16], "ffn1", late)
    mod = mod + (tok_a + tok_b + tok_c + tok_d)

    ropes = rope_tables(a["positions"][0])
    cm16 = lambda t: chip_major(t).astype(BF16)
    w.update(ffn_w_up=[None, None], ffn_w_down=[None, None])
    handles = dict(mix0=mix0_handle, up0=up0_handle, down0=down0_handle, mix1=mix1_handle, ffn1=ffn1_handle)
    reducing, reduced = {}, {}

    class Hooks(StepHooks):
        def weights(self, stage, after):
            got = finish_gather(handles[stage], chip, stage, after)
            if stage == "mix0":
                w.update(ab_w_in=got[0], ab_w_out=merge(got[1]))
            elif stage == "up0":
                w["ffn_w_up"][0] = got[0]
            elif stage == "down0":
                w["ffn_w_down"][0] = merge(got[0])
            elif stage == "mix1":
                cd_in, uq, ukv, cd_out = got
                w.update(prepare_weights(dict(cd_w_in=from_chip_major(cd_in), c_w_uq=from_chip_major(uq),
                                              c_w_ukv=from_chip_major(ukv), cd_w_out=merge(cd_out))))
            else:
                w["ffn_w_up"][1], w["ffn_w_down"][1] = got[0], merge(got[1])

        def gradients(self, stage, grads, after):
            if stage in ("ffn0", "ffn1"):
                parts = [grads["ffn_w_up"], _rows_major(grads["ffn_w_down"])]
            elif stage == "mix1":
                grads.update(unprepare_grads(grads))
                parts = [cm16(grads["cd_w_in"]), cm16(grads["c_w_uq"]), cm16(grads["c_w_ukv"]),
                         _rows_major(grads["cd_w_out"]).astype(BF16)]
            else:
                parts = [grads["ab_w_in"], _rows_major(grads["ab_w_out"])]
            reducing[stage], tok = start_reduce(parts, ci, stage)
            before = {"mix1": "ffn1", "ffn0": "mix1", "mix0": "ffn0"}.get(stage)
            if before is not None:
                reduced[before] = finish_reduce(reducing[before], chip, ci, before, after)
            return tok

    loss, grad_x, dmod, by_stage = run_step(x, tgt, mod, ropes, w, Hooks())
    grads = merge_grads(by_stage)

    parts3 = [dmod] + [grads[n] for n, _ in _SMALL_GRADS] + [loss[0, 0]]
    rows3 = -(-sum(p.size for p in parts3) // LANES // 8) * 8
    small_handle, _ = split_start("small_grads_start", EVERYONE, [_pack_rows(parts3, rows3, F32)],
                                  [_sds((N_DEV, rows3, LANES))])
    red_up1, red_down1 = reduced["ffn1"]
    red_cd_in, red_uq, red_ukv, red_cd_out = reduced["mix1"]
    red_up0, red_down0 = reduced["ffn0"]
    out_grads = dict(cd_w_in=red_cd_in, c_w_uq=red_uq, c_w_ukv=red_ukv, cd_w_out=red_cd_out)
    per_layer = dict(ffn_w_up=(red_up0, red_up1), ffn_w_down=(red_down0, red_down1))
    updates = {}

    def update(n):
        if n in per_layer:
            updates[n] = adamw_layers(a[n], *per_layer[n], a["m_" + n], a["v_" + n], "adamw_" + n)
        else:
            updates[n] = adamw(a[n], out_grads[n].reshape(a[n].shape), a["m_" + n], a["v_" + n], "adamw_" + n)

    early =("ffn_w_up", "ffn_w_down", "cd_w_in", "c_w_uq", "c_w_ukv", "cd_w_out")
    for n in early:
        update(n)
    (mine,), (landed,) = split_wait("small_grads_wait", EVERYONE, small_handle, [updates[n][1] for n in early])
    g3 = lax.dynamic_update_index_in_dim(landed, mine, dev, 0)
    summed = sum8(g3).reshape(-1)
    nmod = 2 * N_MOD * D_MODEL
    out_grads["ada_b"] = summed[:nmod].reshape(2, N_MOD * D_MODEL)
    off = nmod
    for n, shp in _SMALL_GRADS:
        out_grads[n] = summed[off:off + _size(shp)].reshape(shp)
        off += _size(shp)
    loss = summed[off]
    for n, shp, axis in _SMALL_SHARDED:
        width = out_grads[n].shape[-1] // N_CHIPS
        out_grads[n] = lax.dynamic_slice_in_dim(out_grads[n], chip * width, width, axis=out_grads[n].ndim - 1)
    dmod_all = g3.reshape(N_DEV, rows3 * LANES)[:, :nmod].reshape(N_DEV, 2, N_MOD * D_MODEL)
    dmod_mine = lax.dynamic_slice_in_dim(dmod_all, chip * ncol, ncol, axis=2).transpose(1, 0, 2)
    updates["ada_w"] = adamw_ada(a["ada_w"], c_all, dmod_mine, a["m_ada_w"], a["v_ada_w"])

    red_in0, red_out0 = finish_reduce(reducing["mix0"], chip, ci, "mix0", updates["ada_w"][1])
    out_grads.update(ab_w_in=red_in0, ab_w_out=red_out0)

    for n in ("ab_w_in", "ab_w_out"):
        update(n)
    small = [n for n in _WEIGHTS if n not in updates]
    for n, res in zip(small, adamw_small([a[n] for n in small], [out_grads[n].reshape(a[n].shape) for n in small],
                                         [a["m_" + n] for n in small], [a["v_" + n] for n in small])):
        updates[n] = res
    return (loss, grad_x[None], *[updates[n][i] for i in range(4) for n in _WEIGHTS])
```

```python
import functools
from typing import NamedTuple

import jax
import jax.numpy as jnp
from jax import lax
from jax.experimental import pallas as pl
from jax.experimental.pallas import tpu as pltpu

F32 = jnp.float32
BF16 = jnp.bfloat16
EPS = 1e-6
D_MODEL = 1024
N_MOD = 6
A_WIDTH = 512
B_GROUPS = 4
C_HEADS = 8
C_NOPE = 64
C_ROPE = 32
C_V = 64
C_Q_RANK = 256
C_KV_RANK = 128
HEAD_PAD = 128
ROPE_THETA = 10000.0
D_GROUPS = 4
D_CHUNK = 128
D_FF = 2816
FF_UNIT = 128
ADAM_LR = 0.001
ADAM_B1 = 0.9
ADAM_B2 = 0.999
ADAM_EPS = 1e-08
ADAM_WD = 0.01
ADAM_STEP = 10
N_CHIPS = 4
N_DEV = 8
LANES = 128
VMEM_BIG = 56 * 1024 * 1024
MESH = pl.DeviceIdType.MESH


def _sds(shape, dtype=F32):
    return jax.ShapeDtypeStruct(tuple(shape), dtype)


def _tile(n, cap, mult=128):
    if n <= cap:
        return n
    best = None
    for t in range(mult, cap + 1, mult):
        if n % t == 0:
            best = t
    assert best is not None, (n, cap, mult)
    return best


def _params(dims=None, vmem=None):
    return pltpu.CompilerParams(dimension_semantics=dims, vmem_limit_bytes=vmem)


def _shift_down(v, k):
    r = pltpu.roll(v, k, axis=0)
    t = lax.broadcasted_iota(jnp.int32, v.shape, 0)
    return jnp.where(t >= k, r, 0.0)


def _shift_up(v, k):
    n = v.shape[0]
    r = pltpu.roll(v, n - k, axis=0)
    t = lax.broadcasted_iota(jnp.int32, v.shape, 0)
    return jnp.where(t < n - k, r, 0.0)


def _sigmoid(v):
    return 1.0 / (1.0 + jnp.exp(-v))


_GELU_C = 0.7978845608028654
_GELU_A = 0.044715


def _gelu(v):
    return 0.5 * v * (1.0 + jnp.tanh(_GELU_C * (v + _GELU_A * v * v * v)))


def _gelu_grad(v):
    th = jnp.tanh(_GELU_C * (v + _GELU_A * v * v * v))
    return 0.5 * (1.0 + th) + 0.5 * v * (1.0 - th * th) * _GELU_C * (1.0 + 3.0 * _GELU_A * v * v)


_NN = (((1,), (0,)), ((), ()))
_NT = (((1,), (1,)), ((), ()))
_TN = (((0,), (0,)), ((), ()))


def _dot(a, b, dims=_NN):
    return lax.dot_general(a, b, dims, preferred_element_type=F32)


def _logical(t, groups):
    return (t.shape[-2], t.shape[-1] * groups)


def _block(tr, tc, groups, cols, where):
    if groups == 1:
        return pl.BlockSpec((tr, tc), where)
    per = cols // groups // tc

    def index(i, j, s):
        r, c = where(i, j, s)
        return (c // per, r, c % per)

    return pl.BlockSpec((None, tr, tc), index)


def matmul(a, b, mode, out_dtype, name, ga=1, gb=1, go=1, tm=None, tn=None, tk=None):
    (ar, ac), (br, bc) = _logical(a, ga), _logical(b, gb)
    if mode == "nn":
        m, k, n = ar, ac, bc
        a_col, b_col = "k", "n"
    elif mode == "nt":
        m, k, n = ar, ac, br
        a_col, b_col = "k", "k"
    else:
        k, m, n = ar, ac, bc
        a_col, b_col = "m", "n"
    limit = {"m": m, "n": n // go, "k": k}
    limit[a_col] = min(limit[a_col], ac // ga)
    limit[b_col] = min(limit[b_col], bc // gb)
    tm = tm or _tile(limit["m"], 2048, 128 if mode == "tn" else 16)
    tn = tn or _tile(limit["n"], 512)
    tk = tk or _tile(limit["k"], 2048, 16 if mode == "tn" else 128)
    nk = k // tk
    if mode == "nn":
        a_spec = _block(tm, tk, ga, ac, lambda i, j, s: (i, s))
        b_spec = _block(tk, tn, gb, bc, lambda i, j, s: (s, j))
        dims = _NN
    elif mode == "nt":
        a_spec = _block(tm, tk, ga, ac, lambda i, j, s: (i, s))
        b_spec = _block(tn, tk, gb, bc, lambda i, j, s: (j, s))
        dims = _NT
    else:
        a_spec = _block(tk, tm, ga, ac, lambda i, j, s: (s, i))
        b_spec = _block(tk, tn, gb, bc, lambda i, j, s: (s, j))
        dims = _TN
    o_spec = _block(tm, tn, go, n, lambda i, j, s: (i, j))
    out_shape = _sds((m, n), out_dtype) if go == 1 else _sds((go, m, n // go), out_dtype)

    def body(a_ref, b_ref, o_ref, acc_ref):
        s = pl.program_id(2)

        @pl.when(s == 0)
        def _():
            acc_ref[...] = jnp.zeros_like(acc_ref)

        acc_ref[...] += _dot(a_ref[...], b_ref[...], dims)

        @pl.when(s == nk - 1)
        def _():
            o_ref[...] = acc_ref[...].astype(o_ref.dtype)

    return pl.pallas_call(
        body, name=name, out_shape=out_shape, grid=(m // tm, n // tn, nk),
        in_specs=[a_spec, b_spec], out_specs=o_spec,
        scratch_shapes=[pltpu.VMEM((tm, tn), F32)],
        compiler_params=_params(("parallel", "parallel", "arbitrary"), VMEM_BIG),
    )(a, b)


def _rows(tm, n):
    return pl.BlockSpec((tm, n), lambda i: (i, 0))


def _vec(n):
    return pl.BlockSpec((1, n), lambda i: (0, 0))


class Vec(NamedTuple):
    array: jax.Array
    row: int
    col: int


def _vec_in(v, d):
    return pl.BlockSpec((None, 1, d), lambda i: (v.row, 0, v.col))


def modnorm_fwd(x, g, sc, sh, name):
    s, d = x.shape
    tm = _tile(s, 256, 8)

    def body(x_ref, g_ref, sc_ref, sh_ref, o_ref):
        xv = x_ref[...]
        r = lax.rsqrt(jnp.mean(xv * xv, axis=-1, keepdims=True) + EPS)
        o_ref[...] = ((xv * r) * g_ref[...] * (1.0 + sc_ref[...]) + sh_ref[...]).astype(BF16)

    return pl.pallas_call(
        body, name=name, out_shape=_sds((s, d), BF16), grid=(s // tm,),
        in_specs=[_rows(tm, d), _vec_in(g, d), _vec_in(sc, d), _vec_in(sh, d)], out_specs=_rows(tm, d),
        compiler_params=_params(("parallel",)),
    )(x, g.array, sc.array, sh.array)


def norm_bwd(x, dh, g, sc, dres, name):
    s, d = x.shape
    tm = _tile(s, 256, 8)
    nsteps = s // tm

    def body(x_ref, dh_ref, g_ref, sc_ref, dr_ref, dx_ref, dsh_ref, dsc_ref, dg_ref, a2_ref):
        i = pl.program_id(0)

        @pl.when(i == 0)
        def _():
            dsh_ref[...] = jnp.zeros_like(dsh_ref)
            a2_ref[...] = jnp.zeros_like(a2_ref)

        xv = x_ref[...]
        dh = dh_ref[...]
        r = lax.rsqrt(jnp.mean(xv * xv, axis=-1, keepdims=True) + EPS)
        xh = xv * r
        dsh_ref[...] += jnp.sum(dh, axis=0, keepdims=True)
        a2_ref[...] += jnp.sum(dh * xh, axis=0, keepdims=True)
        dxh = dh * (g_ref[...] * (1.0 + sc_ref[...]))
        dx = r * (dxh - xh * jnp.mean(dxh * xh, axis=-1, keepdims=True))
        dx_ref[...] = dr_ref[...] + dx

        @pl.when(i == nsteps - 1)
        def _():
            dsc_ref[...] = a2_ref[...] * g_ref[...]
            dg_ref[...] = a2_ref[...] * (1.0 + sc_ref[...])

    return pl.pallas_call(
        body, name=name, out_shape=(_sds((s, d)), _sds((1, d)), _sds((1, d)), _sds((1, d))), grid=(nsteps,),
        in_specs=[_rows(tm, d), _rows(tm, d), _vec_in(g, d), _vec_in(sc, d), _rows(tm, d)],
        out_specs=(_rows(tm, d), _vec(d), _vec(d), _vec(d)),
        scratch_shapes=[pltpu.VMEM((1, d), F32)],
        compiler_params=_params(("arbitrary",)),
    )(x, dh, g.array, sc.array, dres)


def resid_modnorm_fwd(x, y, gate, g, sc, sh, name):
    s, d = x.shape
    tm = _tile(s, 256, 8)

    def body(x_ref, y_ref, gate_ref, g_ref, sc_ref, sh_ref, xo_ref, h_ref):
        xv = x_ref[...] + gate_ref[...] * y_ref[...].astype(F32)
        xo_ref[...] = xv
        r = lax.rsqrt(jnp.mean(xv * xv, axis=-1, keepdims=True) + EPS)
        h_ref[...] = ((xv * r) * g_ref[...] * (1.0 + sc_ref[...]) + sh_ref[...]).astype(BF16)

    return pl.pallas_call(
        body, name=name, out_shape=(_sds((s, d)), _sds((s, d), BF16)), grid=(s // tm,),
        in_specs=[_rows(tm, d), _rows(tm, d), _vec_in(gate, d), _vec_in(g, d), _vec_in(sc, d), _vec_in(sh, d)],
        out_specs=(_rows(tm, d), _rows(tm, d)),
        compiler_params=_params(("parallel",)),
    )(x, y, gate.array, g.array, sc.array, sh.array)


def norm_gate_bwd(x, dh, g, sc, dres, y, gate, name):
    s, d = x.shape
    tm = _tile(s, 256, 8)
    nsteps = s // tm

    def body(x_ref, dh_ref, g_ref, sc_ref, dr_ref, y_ref, gate_ref, dx_ref, dsh_ref, dsc_ref, dg_ref, dy_ref,
             dgate_ref, a2_ref):
        i = pl.program_id(0)

        @pl.when(i == 0)
        def _():
            dsh_ref[...] = jnp.zeros_like(dsh_ref)
            a2_ref[...] = jnp.zeros_like(a2_ref)
            dgate_ref[...] = jnp.zeros_like(dgate_ref)

        xv = x_ref[...]
        dh = dh_ref[...]
        r = lax.rsqrt(jnp.mean(xv * xv, axis=-1, keepdims=True) + EPS)
        xh = xv * r
        dsh_ref[...] += jnp.sum(dh, axis=0, keepdims=True)
        a2_ref[...] += jnp.sum(dh * xh, axis=0, keepdims=True)
        dxh = dh * (g_ref[...] * (1.0 + sc_ref[...]))
        dr = dr_ref[...] + r * (dxh - xh * jnp.mean(dxh * xh, axis=-1, keepdims=True))
        dx_ref[...] = dr
        dy_ref[...] = (dr * gate_ref[...]).astype(BF16)
        dgate_ref[...] += jnp.sum(dr * y_ref[...].astype(F32), axis=0, keepdims=True)

        @pl.when(i == nsteps - 1)
        def _():
            dsc_ref[...] = a2_ref[...] * g_ref[...]
            dg_ref[...] = a2_ref[...] * (1.0 + sc_ref[...])

    vec = _sds((1, d))
    return pl.pallas_call(
        body, name=name, out_shape=(_sds((s, d)), vec, vec, vec, _sds((s, d), BF16), vec), grid=(nsteps,),
        in_specs=[_rows(tm, d), _rows(tm, d), _vec_in(g, d), _vec_in(sc, d), _rows(tm, d), _rows(tm, d), _vec_in(gate, d)],
        out_specs=(_rows(tm, d), _vec(d), _vec(d), _vec(d), _rows(tm, d), _vec(d)),
        scratch_shapes=[pltpu.VMEM((1, d), F32)],
        compiler_params=_params(("arbitrary",)),
    )(x, dh, g.array, sc.array, dres, y, gate.array)


def final_fused(x, f, gate, g, tgt):
    s, d = x.shape
    tm = _tile(s, 256, 8)

    def body(x_ref, f_ref, gate_ref, g_ref, t_ref, dx_ref, dg_ref, loss_ref, df_ref, dgate_ref):
        @pl.when(pl.program_id(0) == 0)
        def _():
            dg_ref[...] = jnp.zeros_like(dg_ref)
            loss_ref[...] = jnp.zeros_like(loss_ref)
            dgate_ref[...] = jnp.zeros_like(dgate_ref)

        fv, gatev, gv = f_ref[...].astype(F32), gate_ref[...], g_ref[...]
        xv = x_ref[...] + gatev * fv
        r = lax.rsqrt(jnp.mean(xv * xv, axis=-1, keepdims=True) + EPS)
        xh = xv * r
        e = xh * gv - t_ref[...]
        row = jnp.sum(e * e, axis=-1, keepdims=True) * (0.5 / d)
        loss_ref[...] += jnp.sum(row, axis=0, keepdims=True)
        dy = e * (1.0 / d)
        dg_ref[...] += jnp.sum(dy * xh, axis=0, keepdims=True)
        dxh = dy * gv
        dx = r * (dxh - xh * jnp.mean(dxh * xh, axis=-1, keepdims=True))
        dx_ref[...] = dx
        df_ref[...] = (dx * gatev).astype(BF16)
        dgate_ref[...] += jnp.sum(dx * fv, axis=0, keepdims=True)

    vec = _sds((1, d))
    return pl.pallas_call(
        body, name="final_fused", out_shape=(_sds((s, d)), vec, _sds((1, LANES)), _sds((s, d), BF16), vec),
        grid=(s // tm,),
        in_specs=[_rows(tm, d), _rows(tm, d), _vec_in(gate, d), _vec_in(g, d), _rows(tm, d)],
        out_specs=(_rows(tm, d), _vec(d), _vec(LANES), _rows(tm, d), _vec(d)),
        compiler_params=_params(("arbitrary",)),
    )(x, f, gate.array, g.array, tgt)


def _taps(v):
    return _shift_down(v, 2), _shift_down(v, 1), v


def _conv3_taps(taps, w):
    return w[0:1, :] * taps[0] + w[1:2, :] * taps[1] + w[2:3, :] * taps[2]


def _conv3(v, w):
    return _conv3_taps(_taps(v), w)


def _conv3_t(dv, w):
    return w[0:1, :] * _shift_up(dv, 2) + w[1:2, :] * _shift_up(dv, 1) + w[2:3, :] * dv


def _conv3_dw_taps(dv, taps):
    return jnp.concatenate([jnp.sum(dv * t, axis=0, keepdims=True) for t in taps], axis=0)


def _conv3_dw(dv, v):
    return _conv3_dw_taps(dv, _taps(v))


def gconv_fwd(z, conv_w):
    s = z.shape[0]
    nb = A_WIDTH // LANES

    def body(b_ref, c_ref, a_ref, w_ref, o_ref):
        b, c, a = b_ref[...].astype(F32), c_ref[...].astype(F32), a_ref[...].astype(F32)
        o_ref[...] = (b * _conv3(c * a, w_ref[...])).astype(BF16)

    col = lambda off: pl.BlockSpec((s, LANES), lambda j: (0, off + j))
    return pl.pallas_call(
        body, name="gconv_fwd", out_shape=_sds((s, A_WIDTH), BF16), grid=(nb,),
        in_specs=[col(0), col(nb), col(2 * nb), pl.BlockSpec((3, LANES), lambda j: (0, j))],
        out_specs=pl.BlockSpec((s, LANES), lambda j: (0, j)),
        compiler_params=_params(("parallel",), VMEM_BIG),
    )(z, z, z, conv_w)


def gconv_bwd(z, conv_w, dycat):
    s = z.shape[0]
    nb = A_WIDTH // LANES

    def body(b_ref, c_ref, a_ref, w_ref, dy_ref, db_ref, dc_ref, da_ref, dw_ref):
        c, a, w, dy = c_ref[...].astype(F32), a_ref[...].astype(F32), w_ref[...], dy_ref[...].astype(F32)
        ca = c * a
        db_ref[...] = (dy * _conv3(ca, w)).astype(BF16)
        dconv = dy * b_ref[...].astype(F32)
        dw_ref[...] = _conv3_dw(dconv, ca)
        dca = _conv3_t(dconv, w)
        dc_ref[...] = (dca * a).astype(BF16)
        da_ref[...] = (dca * c).astype(BF16)

    col = lambda off: pl.BlockSpec((s, LANES), lambda j: (0, off + j))
    wspec = pl.BlockSpec((3, LANES), lambda j: (0, j))
    part = _sds((s, A_WIDTH), BF16)
    return pl.pallas_call(
        body, name="gconv_bwd", out_shape=(part, part, part, _sds((3, A_WIDTH))), grid=(nb,),
        in_specs=[col(0), col(nb), col(2 * nb), wspec, col(0)],
        out_specs=(col(0), col(0), col(0), wspec),
        compiler_params=_params(("parallel",), VMEM_BIG),
    )(z, z, z, conv_w, dycat)


def _pool_counts(s, w):
    t = lax.broadcasted_iota(jnp.int32, (s, 1), 0)
    return jnp.minimum(t + 1, w).astype(F32)


def _pooled(p, levels):
    acc = p
    for lv in range(levels):
        acc = acc + _shift_down(acc, 2 ** lv)
    return acc / _pool_counts(p.shape[0], 2 ** levels) - p


_B_WIDTH = B_GROUPS * LANES


def pool_fwd(z, mix_w, scale):
    s = z.shape[0]

    def body(p_ref, m_ref, sc_ref, o_ref):
        for g in range(B_GROUPS):
            cols = slice(g * LANES, (g + 1) * LANES)
            pooled = _pooled(p_ref[:, cols].astype(F32), g + 1)
            y = _dot(pooled.astype(BF16), m_ref[g].astype(BF16))
            o_ref[:, cols] = (y * sc_ref[:, cols]).astype(BF16)

    return pl.pallas_call(
        body, name="pool_fwd", out_shape=_sds((s, _B_WIDTH), BF16), grid=(1,),
        in_specs=[pl.BlockSpec((s, _B_WIDTH), lambda i: (0, 3 * A_WIDTH // _B_WIDTH)),
                  pl.BlockSpec((B_GROUPS, LANES, LANES), lambda i: (0, 0, 0)), pl.BlockSpec((1, _B_WIDTH), lambda i: (0, 0))],
        out_specs=pl.BlockSpec((s, _B_WIDTH), lambda i: (0, 0)),
        compiler_params=_params(("arbitrary",), VMEM_BIG),
    )(z, mix_w, scale)


def pool_bwd(z, mix_w, scale, dycat):
    s = z.shape[0]

    def body(p_ref, m_ref, sc_ref, dy_ref, dp_ref, dm_ref, dsc_ref):
        for g in range(B_GROUPS):
            cols = slice(g * LANES, (g + 1) * LANES)
            pooled = _pooled(p_ref[:, cols].astype(F32), g + 1)
            mw = m_ref[g].astype(BF16)
            pb = pooled.astype(BF16)
            dy = dy_ref[:, cols].astype(F32)
            dsc_ref[:, cols] = jnp.sum(dy * _dot(pb, mw), axis=0, keepdims=True)
            dmix = (dy * sc_ref[:, cols]).astype(BF16)
            dm_ref[g] = _dot(pb, dmix, _TN)
            dpool = _dot(dmix, mw, _NT)
            acc = dpool / _pool_counts(s, 2 ** (g + 1))
            for lv in range(g + 1):
                acc = acc + _shift_up(acc, 2 ** lv)
            dp_ref[:, cols] = (acc - dpool).astype(BF16)

    wide = lambda c: pl.BlockSpec((s, _B_WIDTH), lambda i: (0, c))
    mspec = pl.BlockSpec((B_GROUPS, LANES, LANES), lambda i: (0, 0, 0))
    vspec = pl.BlockSpec((1, _B_WIDTH), lambda i: (0, 0))
    return pl.pallas_call(
        body, name="pool_bwd", out_shape=(_sds((s, _B_WIDTH), BF16), _sds((B_GROUPS, LANES, LANES)), _sds((1, _B_WIDTH))),
        grid=(1,), in_specs=[wide(3 * A_WIDTH // _B_WIDTH), mspec, vspec, wide(A_WIDTH // _B_WIDTH)],
        out_specs=(wide(0), mspec, vspec),
        compiler_params=_params(("arbitrary",), VMEM_BIG),
    )(z, mix_w, scale, dycat)


_FF_BLOCKS = D_FF // FF_UNIT


def _ff_spec(s):
    return pl.BlockSpec((2, s, FF_UNIT), lambda j: (0, 0, j))


def _ff_wspecs():
    return [pl.BlockSpec((3, FF_UNIT), lambda j: (0, j)), pl.BlockSpec((3, FF_UNIT), lambda j: (0, _FF_BLOCKS + j))]


_FF_ROWS = 64
_FF_HALO = 16


def _chunk_taps(z_ref, half, c):
    start = pl.multiple_of(c * _FF_ROWS, _FF_ROWS)
    before = pl.multiple_of(jnp.maximum(c * _FF_ROWS - _FF_HALO, 0), _FF_HALO)
    halo = z_ref[half, pl.ds(before, _FF_HALO), :].astype(F32)
    halo = jnp.where(c > 0, halo, 0.0)
    win = jnp.concatenate([halo, z_ref[half, pl.ds(start, _FF_ROWS), :].astype(F32)], axis=0)
    return tuple(pltpu.roll(win, k, axis=0)[_FF_HALO:] for k in (2, 1)) + (win[_FF_HALO:],)


def _fold8(v):
    acc = v[0:8]
    for r in range(8, v.shape[0], 8):
        acc = acc + v[r:r + 8]
    return acc


_FF_CHUNK = 256


def ffn_act_down(zf, conv_w, w_down, name):
    s, d = zf.shape[1], w_down.shape[1]
    nk = D_FF // _FF_CHUNK
    chunk = lambda k: jnp.minimum(k, nk - 1)

    def body(z_ref, wg_ref, wu_ref, wd_ref, a_ref, f_ref, held_ref, acc_ref):
        k = pl.program_id(0)

        @pl.when(k == 0)
        def _():
            held_ref[...] = jnp.zeros_like(held_ref)
            acc_ref[...] = jnp.zeros_like(acc_ref)

        acc_ref[...] += _dot(held_ref[(k + 1) % 2], wd_ref[...])
        g = _conv3(z_ref[0].astype(F32), wg_ref[...])
        u = _conv3(z_ref[1].astype(F32), wu_ref[...])
        act = (g * _sigmoid(g) * u).astype(BF16)
        a_ref[...] = act
        held_ref[k % 2] = act

        @pl.when(k == nk)
        def _():
            f_ref[...] = acc_ref[...].astype(BF16)

    return pl.pallas_call(
        body, name=name, out_shape=(_sds((s, D_FF), BF16), _sds((s, d), BF16)), grid=(nk + 1,),
        in_specs=[pl.BlockSpec((2, s, _FF_CHUNK), lambda k: (0, 0, chunk(k))),
                  pl.BlockSpec((3, _FF_CHUNK), lambda k: (0, chunk(k))),
                  pl.BlockSpec((3, _FF_CHUNK), lambda k: (0, nk + chunk(k))),
                  pl.BlockSpec((_FF_CHUNK, d), lambda k: (jnp.maximum(k - 1, 0), 0))],
        out_specs=(pl.BlockSpec((s, _FF_CHUNK), lambda k: (0, chunk(k))), pl.BlockSpec((s, d), lambda k: (0, 0))),
        scratch_shapes=[pltpu.VMEM((2, s, _FF_CHUNK), BF16), pltpu.VMEM((s, d), F32)],
        compiler_params=_params(("arbitrary",), VMEM_BIG),
    )(zf, conv_w, conv_w, w_down)


def ffn_act_bwd(zf, conv_w, da, name):
    s = zf.shape[1]
    assert s % _FF_ROWS == 0
    nchunks = s // _FF_ROWS

    def body(z_ref, wg_ref, wu_ref, da_ref, dz_ref, dw_ref, dg_ref, du_ref):
        wg, wu = wg_ref[...], wu_ref[...]

        def first(c, acc):
            rows = pl.ds(pl.multiple_of(c * _FF_ROWS, _FF_ROWS), _FF_ROWS)
            tg, tu = _chunk_taps(z_ref, 0, c), _chunk_taps(z_ref, 1, c)
            g = _conv3_taps(tg, wg)
            u = _conv3_taps(tu, wu)
            dav = da_ref[rows, :].astype(F32)
            sg = _sigmoid(g)
            dg = dav * u * (sg * (1.0 + g * (1.0 - sg)))
            du = dav * (g * sg)
            dg_ref[rows, :] = dg
            du_ref[rows, :] = du
            return tuple(a + _fold8(d * t) for a, (d, t) in zip(acc, [(dg, t) for t in tg] + [(du, t) for t in tu]))

        zero = jnp.zeros((8, FF_UNIT), F32)
        acc = lax.fori_loop(0, nchunks, first, (zero,) * 6)
        sums = [jnp.sum(a, axis=0, keepdims=True) for a in acc]
        dw_ref[0] = jnp.concatenate(sums[:3], axis=0)
        dw_ref[1] = jnp.concatenate(sums[3:], axis=0)

        tail = pl.ds(s, _FF_HALO)
        dg_ref[tail, :] = jnp.zeros((_FF_HALO, FF_UNIT), F32)
        du_ref[tail, :] = jnp.zeros((_FF_HALO, FF_UNIT), F32)
        span = _FF_ROWS + _FF_HALO

        def second(c, carry):
            start = pl.multiple_of(c * _FF_ROWS, _FF_ROWS)
            for half, (d_ref, w) in enumerate(((dg_ref, wg), (du_ref, wu))):
                win = d_ref[pl.ds(start, span), :]
                dz = (w[0:1, :] * pltpu.roll(win, span - 2, axis=0)[:_FF_ROWS]
                      + w[1:2, :] * pltpu.roll(win, span - 1, axis=0)[:_FF_ROWS] + w[2:3, :] * win[:_FF_ROWS])
                dz_ref[half, pl.ds(start, _FF_ROWS), :] = dz.astype(BF16)
            return carry

        lax.fori_loop(0, nchunks, second, 0)

    return pl.pallas_call(
        body, name=name, out_shape=(_sds((2, s, D_FF), BF16), _sds((2, 3, D_FF))), grid=(_FF_BLOCKS,),
        in_specs=[_ff_spec(s)] + _ff_wspecs() + [pl.BlockSpec((s, FF_UNIT), lambda j: (0, j))],
        out_specs=(_ff_spec(s), pl.BlockSpec((2, 3, FF_UNIT), lambda j: (0, 0, j))),
        scratch_shapes=[pltpu.VMEM((s + _FF_HALO, FF_UNIT), F32), pltpu.VMEM((s + _FF_HALO, FF_UNIT), F32)],
        compiler_params=_params(("parallel",), VMEM_BIG),
    )(zf, conv_w, conv_w, da)


def _rope(v, cs, s1, s2):
    return v * cs + pltpu.roll(v, LANES - C_ROPE // 2, axis=1) * s1 + pltpu.roll(v, C_ROPE // 2, axis=1) * s2


def _rope_t(dv, cs, s1, s2):
    return dv * cs + pltpu.roll(dv * s1, C_ROPE // 2, axis=1) + pltpu.roll(dv * s2, LANES - C_ROPE // 2, axis=1)


def _kpe_mask(shape):
    lane = lax.broadcasted_iota(jnp.int32, shape, 1)
    return (lane >= C_NOPE) & (lane < C_NOPE + C_ROPE)


def _rms(v, g):
    r = lax.rsqrt(jnp.mean(v * v, axis=-1, keepdims=True) + EPS)
    return v * r, r


def _rms_bwd(dn, xh, r, g):
    dxh = dn * g
    return r * (dxh - xh * jnp.mean(dxh * xh, axis=-1, keepdims=True)), jnp.sum(dn * xh, axis=0, keepdims=True)


_ZQ = C_Q_RANK + C_KV_RANK + HEAD_PAD
_HW = C_HEADS * HEAD_PAD


def mla_pre_fwd(z, gq, gkv, wq, wk, wv, cs, s1, s2):
    s = z.shape[0]
    tm = _tile(s, 256, 8)

    def body(z_ref, gq_ref, gkv_ref, wq_ref, wk_ref, wv_ref, cs_ref, s1_ref, s2_ref, q_ref, k_ref, v_ref):
        zv = z_ref[...].astype(F32)
        cst, s1t, s2t = cs_ref[...], s1_ref[...], s2_ref[...]
        qh, _ = _rms(zv[:, :C_Q_RANK], None)
        qn = (qh * gq_ref[...]).astype(BF16)
        q = _dot(qn, wq_ref[...])
        kh, _ = _rms(zv[:, C_Q_RANK:C_Q_RANK + C_KV_RANK], None)
        kvn = (kh * gkv_ref[...]).astype(BF16)
        k = _dot(kvn, wk_ref[...])
        v_ref[...] = _dot(kvn, wv_ref[...]).astype(BF16)
        kpe = _rope(zv[:, C_Q_RANK + C_KV_RANK:], cst, s1t, s2t)
        for h in range(C_HEADS):
            sl = slice(h * HEAD_PAD, (h + 1) * HEAD_PAD)
            q_ref[:, sl] = _rope(q[:, sl], cst, s1t, s2t).astype(BF16)
            k_ref[:, sl] = (k[:, sl] + kpe).astype(BF16)

    full = lambda r, c: pl.BlockSpec((r, c), lambda i: (0, 0))
    hw = _sds((s, _HW), BF16)
    return pl.pallas_call(
        body, name="mla_pre_fwd", out_shape=(hw, hw, hw), grid=(s // tm,),
        in_specs=[_rows(tm, _ZQ), _vec(C_Q_RANK), _vec(C_KV_RANK), full(C_Q_RANK, _HW), full(C_KV_RANK, _HW),
                  full(C_KV_RANK, _HW), _rows(tm, LANES), _rows(tm, LANES), _rows(tm, LANES)],
        out_specs=(_rows(tm, _HW), _rows(tm, _HW), _rows(tm, _HW)),
        compiler_params=_params(("parallel",), VMEM_BIG),
    )(z, gq, gkv, wq, wk, wv, cs, s1, s2)


def mla_pre_bwd(z, gq, gkv, wq, wk, wv, cs, s1, s2, dq, dk, dv):
    s = z.shape[0]
    tm = _tile(s, 256, 8)

    def body(z_ref, gq_ref, gkv_ref, wq_ref, wk_ref, wv_ref, cs_ref, s1_ref, s2_ref, dq_ref, dk_ref, dv_ref,
             dz_ref, dwq_ref, dwk_ref, dwv_ref, dgq_ref, dgkv_ref):
        @pl.when(pl.program_id(0) == 0)
        def _():
            dwq_ref[...] = jnp.zeros_like(dwq_ref)
            dwk_ref[...] = jnp.zeros_like(dwk_ref)
            dwv_ref[...] = jnp.zeros_like(dwv_ref)
            dgq_ref[...] = jnp.zeros_like(dgq_ref)
            dgkv_ref[...] = jnp.zeros_like(dgkv_ref)

        zv = z_ref[...].astype(F32)
        cst, s1t, s2t = cs_ref[...], s1_ref[...], s2_ref[...]
        gqv, gkvv = gq_ref[...], gkv_ref[...]
        qh, rq = _rms(zv[:, :C_Q_RANK], None)
        qn = (qh * gqv).astype(BF16)
        kh, rk = _rms(zv[:, C_Q_RANK:C_Q_RANK + C_KV_RANK], None)
        kvn = (kh * gkvv).astype(BF16)

        dqv = dq_ref[...].astype(F32)
        dqp = jnp.concatenate(
            [_rope_t(dqv[:, h * HEAD_PAD:(h + 1) * HEAD_PAD], cst, s1t, s2t) for h in range(C_HEADS)], axis=1
        ).astype(BF16)
        dwq_ref[...] += _dot(qn, dqp, _TN)
        dqn = _dot(dqp, wq_ref[...], _NT)
        dql, dgq = _rms_bwd(dqn, qh, rq, gqv)
        dgq_ref[...] += dgq

        dkv = dk_ref[...]
        dkb = dkv.astype(BF16)
        dvb = dv_ref[...].astype(BF16)
        dwk_ref[...] += _dot(kvn, dkb, _TN)
        dwv_ref[...] += _dot(kvn, dvb, _TN)
        dkvn = _dot(dkb, wk_ref[...], _NT) + _dot(dvb, wv_ref[...], _NT)
        dkl, dgkv = _rms_bwd(dkvn, kh, rk, gkvv)
        dgkv_ref[...] += dgkv

        dkpe = dkv[:, :HEAD_PAD]
        for h in range(1, C_HEADS):
            dkpe = dkpe + dkv[:, h * HEAD_PAD:(h + 1) * HEAD_PAD]
        dkpe = _rope_t(jnp.where(_kpe_mask(dkpe.shape), dkpe, 0.0), cst, s1t, s2t)
        dz_ref[...] = jnp.concatenate([dql, dkl, dkpe], axis=1).astype(BF16)

    full = lambda r, c: pl.BlockSpec((r, c), lambda i: (0, 0))
    return pl.pallas_call(
        body, name="mla_pre_bwd",
        out_shape=(_sds((s, _ZQ), BF16), _sds((C_Q_RANK, _HW)), _sds((C_KV_RANK, _HW)), _sds((C_KV_RANK, _HW)),
                   _sds((1, C_Q_RANK)), _sds((1, C_KV_RANK))),
        grid=(s // tm,),
        in_specs=[_rows(tm, _ZQ), _vec(C_Q_RANK), _vec(C_KV_RANK), full(C_Q_RANK, _HW), full(C_KV_RANK, _HW),
                  full(C_KV_RANK, _HW), _rows(tm, LANES), _rows(tm, LANES), _rows(tm, LANES),
                  _rows(tm, _HW), _rows(tm, _HW), _rows(tm, _HW)],
        out_specs=(_rows(tm, _ZQ), full(C_Q_RANK, _HW), full(C_KV_RANK, _HW), full(C_KV_RANK, _HW),
                   _vec(C_Q_RANK), _vec(C_KV_RANK)),
        compiler_params=_params(("arbitrary",), VMEM_BIG),
    )(z, gq, gkv, wq, wk, wv, cs, s1, s2, dq, dk, dv)


_ATT_SCALE = (C_NOPE + C_ROPE) ** -0.5
_NEG = -1e30


def _att_exp(q, k, row0, ends_here):
    sc = _dot(q, k, _NT) * _ATT_SCALE
    tq, nk = sc.shape
    if ends_here:
        last = sc[:, nk - tq:]
        row = lax.broadcasted_iota(jnp.int32, last.shape, 0)
        col = lax.broadcasted_iota(jnp.int32, last.shape, 1)
        last = jnp.where(col <= row, last, _NEG)
        sc = last if nk == tq else jnp.concatenate([sc[:, :nk - tq], last], axis=1)
    else:
        qpos = row0 + lax.broadcasted_iota(jnp.int32, sc.shape, 0)
        kpos = lax.broadcasted_iota(jnp.int32, sc.shape, 1)
        sc = jnp.where(kpos <= qpos, sc, _NEG)
    e = jnp.exp(sc - jnp.max(sc, axis=-1, keepdims=True))
    return e, 1.0 / jnp.sum(e, axis=-1, keepdims=True)


def _causal_cases(i, nq, tq, fn):
    if nq > 8:
        fn(nq * tq, False)
        return
    for blk in range(nq):
        pl.when(i == blk)(functools.partial(fn, (blk + 1) * tq, True))


def attn_fwd(q, k, v):
    s = q.shape[0]
    tq = _tile(s, 256, 8)
    nq = s // tq

    def body(q_ref, k_ref, v_ref, o_ref):
        i = pl.program_id(1)

        def case(nk, ends_here):
            e, inv = _att_exp(q_ref[...], k_ref[:nk, :], i * tq, ends_here)
            o_ref[...] = (_dot(e.astype(BF16), v_ref[:nk, :]) * inv).astype(BF16)

        _causal_cases(i, nq, tq, case)

    qspec = pl.BlockSpec((tq, HEAD_PAD), lambda h, i: (i, h))
    kspec = pl.BlockSpec((s, HEAD_PAD), lambda h, i: (0, h))
    return pl.pallas_call(
        body, name="attn_fwd", out_shape=_sds((s, _HW), BF16), grid=(C_HEADS, s // tq),
        in_specs=[qspec, kspec, kspec], out_specs=qspec,
        compiler_params=_params(("parallel", "parallel"), VMEM_BIG),
    )(q, k, v)


def attn_bwd(q, k, v, o, do_all, do_col0):
    s = q.shape[0]
    tq = _tile(s, 256, 8)

    def body(q_ref, k_ref, v_ref, o_ref, do_ref, dq_ref, dk_ref, dv_ref):
        i = pl.program_id(1)

        @pl.when(i == 0)
        def _():
            dk_ref[...] = jnp.zeros_like(dk_ref)
            dv_ref[...] = jnp.zeros_like(dv_ref)

        def case(nk, ends_here):
            qv, kv, vv, dov = q_ref[...], k_ref[:nk, :], v_ref[:nk, :], do_ref[...]
            e, inv = _att_exp(qv, kv, i * tq, ends_here)
            p = e * inv
            dp = _dot(dov, vv, _NT)
            delta = jnp.sum(dov.astype(F32) * o_ref[...].astype(F32), axis=-1, keepdims=True)
            ds = (p * (dp - delta) * _ATT_SCALE).astype(BF16)
            dq_ref[...] = _dot(ds, kv).astype(BF16)
            dk_ref[:nk, :] += _dot(ds, qv, _TN)
            dv_ref[:nk, :] += _dot(p.astype(BF16), dov, _TN)

        _causal_cases(i, s // tq, tq, case)

    qspec = pl.BlockSpec((tq, HEAD_PAD), lambda h, i: (i, h))
    dospec = pl.BlockSpec((tq, HEAD_PAD), lambda h, i: (i, do_col0 + h))
    kspec = pl.BlockSpec((s, HEAD_PAD), lambda h, i: (0, h))
    return pl.pallas_call(
        body, name="attn_bwd", out_shape=(_sds((s, _HW), BF16), _sds((s, _HW)), _sds((s, _HW))),
        grid=(C_HEADS, s // tq),
        in_specs=[qspec, kspec, kspec, qspec, dospec], out_specs=(qspec, kspec, kspec),
        compiler_params=_params(("parallel", "arbitrary"), VMEM_BIG),
    )(q, k, v, o, do_all)


_DW = D_GROUPS * LANES


def _tril_bf16(w):
    r = lax.broadcasted_iota(jnp.int32, w.shape, 0)
    c = lax.broadcasted_iota(jnp.int32, w.shape, 1)
    return jnp.where(c <= r, w, 0.0).astype(BF16)


def _sgu_forward(zu, zv, lg, lb, ws_ref, bs):
    u = _gelu(zu)
    v = _gelu(zv)
    mu = jnp.mean(v, axis=-1, keepdims=True)
    vc = v - mu
    rstd = lax.rsqrt(jnp.mean(vc * vc, axis=-1, keepdims=True) + EPS)
    xh = vc * rstd
    vln = (xh * lg + lb).astype(BF16)
    mixed = []
    for g in range(D_GROUPS):
        wg = _tril_bf16(ws_ref[g])
        mixed.append(_dot(wg, vln[:, g * LANES:(g + 1) * LANES]) + bs[:, g:g + 1])
    return u, xh, rstd, vln, jnp.concatenate(mixed, axis=1)


def sgu_fwd(z, lg, lb, ws, bs_t):
    s = z.shape[0]
    nchunk = s // D_CHUNK

    def body(zu_ref, zv_ref, lg_ref, lb_ref, ws_ref, bs_ref, o_ref):
        u, _, _, _, mixed = _sgu_forward(zu_ref[...].astype(F32), zv_ref[...].astype(F32), lg_ref[...], lb_ref[...],
                                         ws_ref, bs_ref[...])
        o_ref[...] = (u * mixed).astype(BF16)

    return pl.pallas_call(
        body, name="sgu_fwd", out_shape=_sds((s, _DW), BF16), grid=(nchunk,),
        in_specs=[pl.BlockSpec((D_CHUNK, _DW), lambda n: (n, 1)), pl.BlockSpec((D_CHUNK, _DW), lambda n: (n, 2)),
                  _vec(_DW), _vec(_DW), pl.BlockSpec((D_GROUPS, D_CHUNK, D_CHUNK), lambda n: (0, 0, 0)),
                  pl.BlockSpec((D_CHUNK, LANES), lambda n: (0, 0))],
        out_specs=pl.BlockSpec((D_CHUNK, _DW), lambda n: (n, 0)),
        compiler_params=_params(("parallel",)),
    )(z, z, lg, lb, ws, bs_t)


def sgu_bwd(z, lg, lb, ws, bs_t, dycat, dy_col):
    s = z.shape[0]
    nchunk = s // D_CHUNK

    def body(zu_ref, zv_ref, lg_ref, lb_ref, ws_ref, bs_ref, dy_ref, dzu_ref, dzv_ref, dws_ref, dbs_ref, dlg_ref,
             dlb_ref):
        @pl.when(pl.program_id(0) == 0)
        def _():
            dws_ref[...] = jnp.zeros_like(dws_ref)
            dbs_ref[...] = jnp.zeros_like(dbs_ref)
            dlg_ref[...] = jnp.zeros_like(dlg_ref)
            dlb_ref[...] = jnp.zeros_like(dlb_ref)

        zu, zv, lg = zu_ref[...].astype(F32), zv_ref[...].astype(F32), lg_ref[...]
        u, xh, rstd, vln, mixed = _sgu_forward(zu, zv, lg, lb_ref[...], ws_ref, bs_ref[...])
        dy = dy_ref[...].astype(F32)
        dzu_ref[...] = (dy * mixed * _gelu_grad(zu)).astype(BF16)
        dmix = dy * u
        lane = lax.broadcasted_iota(jnp.int32, (D_CHUNK, LANES), 1)
        row = lax.broadcasted_iota(jnp.int32, (D_CHUNK, D_CHUNK), 0)
        colm = lax.broadcasted_iota(jnp.int32, (D_CHUNK, D_CHUNK), 1)
        dvln = []
        dbs = jnp.zeros((D_CHUNK, LANES), F32)
        for g in range(D_GROUPS):
            sl = slice(g * LANES, (g + 1) * LANES)
            dmg = dmix[:, sl]
            dbs = dbs + jnp.where(lane == g, jnp.sum(dmg, axis=-1, keepdims=True), 0.0)
            dmb = dmg.astype(BF16)
            dws_ref[g] += jnp.where(colm <= row, _dot(dmb, vln[:, sl], _NT), 0.0)
            dvln.append(_dot(_tril_bf16(ws_ref[g]), dmb, _TN))
        dbs_ref[...] += dbs
        dvln = jnp.concatenate(dvln, axis=1)
        dlg_ref[...] += jnp.sum(dvln * xh, axis=0, keepdims=True)
        dlb_ref[...] += jnp.sum(dvln, axis=0, keepdims=True)
        dxh = dvln * lg
        dvv = rstd * (dxh - jnp.mean(dxh, axis=-1, keepdims=True) - xh * jnp.mean(dxh * xh, axis=-1, keepdims=True))
        dzv_ref[...] = (dvv * _gelu_grad(zv)).astype(BF16)

    wsspec = pl.BlockSpec((D_GROUPS, D_CHUNK, D_CHUNK), lambda n: (0, 0, 0))
    chunk = lambda cidx: pl.BlockSpec((D_CHUNK, _DW), lambda n: (n, cidx))
    return pl.pallas_call(
        body, name="sgu_bwd",
        out_shape=(_sds((s, _DW), BF16), _sds((s, _DW), BF16), _sds((D_GROUPS, D_CHUNK, D_CHUNK)),
                   _sds((D_CHUNK, LANES)), _sds((1, _DW)), _sds((1, _DW))),
        grid=(nchunk,),
        in_specs=[chunk(1), chunk(2), _vec(_DW), _vec(_DW), wsspec, pl.BlockSpec((D_CHUNK, LANES), lambda n: (0, 0)),
                  chunk(dy_col)],
        out_specs=(chunk(0), chunk(0), wsspec, pl.BlockSpec((D_CHUNK, LANES), lambda n: (0, 0)), _vec(_DW), _vec(_DW)),
        compiler_params=_params(("arbitrary",)),
    )(z, z, lg, lb, ws, bs_t, dycat)


def ada_mod(c_all, ada_w, ada_b):
    nl, d, n = ada_w.shape
    nb = c_all.shape[0]
    tn = _tile(n, 512)

    def body(c_ref, w_ref, b_ref, o_ref):
        cv = c_ref[...]
        ca = (cv * _sigmoid(cv)).astype(BF16)
        o_ref[...] = _dot(ca, w_ref[...].astype(BF16)) + b_ref[...]

    return pl.pallas_call(
        body, name="ada_mod", out_shape=_sds((nl, nb, n)), grid=(nl, n // tn),
        in_specs=[pl.BlockSpec((nb, d), lambda l, j: (0, 0)), pl.BlockSpec((None, d, tn), lambda l, j: (l, 0, j)),
                  pl.BlockSpec((None, 1, tn), lambda l, j: (l, 0, j))],
        out_specs=pl.BlockSpec((None, nb, tn), lambda l, j: (l, 0, j)),
        compiler_params=_params(("parallel", "parallel")),
    )(c_all, ada_w, ada_b.reshape(nl, 1, n))


_ADAM_BLOCK = 256 * 1024


def _adam_rows(rows, cols):
    if rows * cols <= _ADAM_BLOCK or rows % 8:
        return rows
    return _tile(rows, max(8, _ADAM_BLOCK // cols), 8)


def _adam_update(w, gv, m, v):
    inv_bc1 = 1.0 / (1.0 - ADAM_B1 ** ADAM_STEP)
    inv_bc2 = 1.0 / (1.0 - ADAM_B2 ** ADAM_STEP)
    nm = ADAM_B1 * m + (1.0 - ADAM_B1) * gv
    nv = ADAM_B2 * v + (1.0 - ADAM_B2) * (gv * gv)
    return -ADAM_LR * ((nm * inv_bc1) / (jnp.sqrt(nv * inv_bc2) + ADAM_EPS) + ADAM_WD * w), nm, nv


def adamw(w, g, m, v, name):
    shape = w.shape
    cols = shape[-1]
    rows = w.size // cols
    tr = _adam_rows(rows, cols)

    def body(w_ref, g_ref, m_ref, v_ref, go_ref, d_ref, nm_ref, nv_ref):
        gv = g_ref[...]
        go_ref[...] = gv
        d_ref[...], nm_ref[...], nv_ref[...] = _adam_update(w_ref[...], gv, m_ref[...], v_ref[...])

    spec = pl.BlockSpec((tr, cols), lambda i: (i, 0))
    out = _sds((rows, cols))
    r2 = lambda t: t.reshape(rows, cols)
    res = pl.pallas_call(
        body, name=name, out_shape=(out,) * 4, grid=(rows // tr,),
        in_specs=[spec] * 4, out_specs=(spec,) * 4, compiler_params=_params(("parallel",)),
    )(r2(w), r2(g), r2(m), r2(v))
    return tuple(t.reshape(shape) for t in res)


def adamw_ada(w, c_all, dmod, m, v):
    nl, d, n = w.shape
    tr = _adam_rows(d, n)
    pad = 16 - c_all.shape[0]
    c16 = jnp.pad(c_all, ((0, pad), (0, 0)))
    dm16 = jnp.pad(dmod, ((0, 0), (0, pad), (0, 0)))

    def body(w_ref, c_ref, dm_ref, m_ref, v_ref, g_ref, d_ref, nm_ref, nv_ref):
        cv = c_ref[...]
        gv = _dot((cv * _sigmoid(cv)).astype(BF16), dm_ref[...].astype(BF16), _TN)
        g_ref[...] = gv
        d_ref[...], nm_ref[...], nv_ref[...] = _adam_update(w_ref[...], gv, m_ref[...], v_ref[...])

    spec = pl.BlockSpec((None, tr, n), lambda l, i: (l, i, 0))
    out = _sds((nl, d, n))
    return pl.pallas_call(
        body, name="adamw_ada_w", out_shape=(out, out, out, out), grid=(nl, d // tr),
        in_specs=[spec, pl.BlockSpec((16, tr), lambda l, i: (0, i)), pl.BlockSpec((None, 16, n), lambda l, i: (l, 0, 0)),
                  spec, spec],
        out_specs=(spec,) * 4, compiler_params=_params(("parallel", "parallel")),
    )(w, c16, dm16, m, v)


def adamw_small(ws, gs, ms, vs):
    n = len(ws)
    flat = lambda t: t.reshape(-1, t.shape[-1])

    def body(*refs):
        ins, outs = refs[:4 * n], refs[4 * n:]
        for i in range(n):
            w_ref, g_ref, m_ref, v_ref = ins[4 * i:4 * i + 4]
            outs[3 * i][...], outs[3 * i + 1][...], outs[3 * i + 2][...] = _adam_update(
                w_ref[...], g_ref[...], m_ref[...], v_ref[...])

    operands = [flat(t) for quad in zip(ws, gs, ms, vs) for t in quad]
    res = pl.pallas_call(
        body, name="adamw_small", out_shape=tuple(_sds(flat(w).shape) for w in ws for _ in range(3)),
    )(*operands)
    return [(g, res[3 * i].reshape(w.shape), res[3 * i + 1].reshape(w.shape), res[3 * i + 2].reshape(w.shape))
            for i, (w, g) in enumerate(zip(ws, gs))]


def adamw_layers(w, g0, g1, m, v, name):
    _, rows, cols = w.shape
    tr = _adam_rows(rows, cols)

    def body(w_ref, g0_ref, g1_ref, m_ref, v_ref, g_ref, d_ref, nm_ref, nv_ref):
        gv = jnp.where(pl.program_id(0) == 0, g0_ref[...], g1_ref[...])
        g_ref[...] = gv
        d_ref[...], nm_ref[...], nv_ref[...] = _adam_update(w_ref[...], gv, m_ref[...], v_ref[...])

    spec = pl.BlockSpec((None, tr, cols), lambda l, i: (l, i, 0))
    gspec = pl.BlockSpec((tr, cols), lambda l, i: (i, 0))
    out = _sds((2, rows, cols))
    return pl.pallas_call(
        body, name=name, out_shape=(out, out, out, out), grid=(2, rows // tr),
        in_specs=[spec, gspec, gspec, spec, spec], out_specs=(spec,) * 4, compiler_params=_params(("parallel", "parallel")),
    )(w, g0, g1, m, v)


def sum8(gathered):
    _, r, _ = gathered.shape
    tr = _tile(r, 512, 8)

    def body(g_ref, o_ref):
        acc = g_ref[0]
        for dev in range(1, N_DEV):
            acc = acc + g_ref[dev]
        o_ref[...] = acc

    return pl.pallas_call(
        body, name="sum8", out_shape=_sds((r, LANES)), grid=(r // tr,),
        in_specs=[pl.BlockSpec((N_DEV, tr, LANES), lambda i: (0, i, 0))], out_specs=pl.BlockSpec((tr, LANES), lambda i: (i, 0)),
        compiler_params=_params(("parallel",)),
    )(gathered)


_SUM_STEPS = 2


def pair_sums(gs, recvs, core, name):
    n = len(gs)
    trs = [g.shape[1] // 2 // _SUM_STEPS for g in gs]

    def body(c_ref, *refs):
        del c_ref
        for i in range(n):
            a_ref, b_ref, o_ref = refs[2 * i], refs[2 * i + 1], refs[2 * n + i]
            o_ref[...] = (a_ref[...].astype(F32) + b_ref[...].astype(F32)).astype(BF16)

    in_specs, out_specs = [], []
    for g, tr in zip(gs, trs):
        cols = g.shape[2]
        in_specs.append(pl.BlockSpec((None, tr, cols), lambda k, s, c: (k, c[0] * _SUM_STEPS + s, 0)))
        in_specs.append(pl.BlockSpec((None, tr, cols), lambda k, s, c: (k, s, 0)))
        out_specs.append(pl.BlockSpec((None, tr, cols), lambda k, s, c: (k, s, 0)))
    grid_spec = pltpu.PrefetchScalarGridSpec(num_scalar_prefetch=1, grid=(N_CHIPS, _SUM_STEPS), in_specs=in_specs,
                                             out_specs=tuple(out_specs))
    return list(pl.pallas_call(
        body, name=name, out_shape=tuple(_sds((N_CHIPS, g.shape[1] // 2, g.shape[2]), BF16) for g in gs),
        grid_spec=grid_spec, compiler_params=_params(("parallel", "parallel")),
    )(core.reshape(1).astype(jnp.int32), *[t for pair in zip(gs, recvs) for t in pair]))


def chip_sums(pairs, recvs, chip, core, name):
    n = len(pairs)
    trs = [p.shape[1] // _SUM_STEPS for p in pairs]

    def body(p_ref, *refs):
        del p_ref
        for i in range(n):
            own_ref, r_ref, o_ref = refs[2 * i], refs[2 * i + 1], refs[2 * n + i]
            acc = own_ref[...].astype(F32)
            for j in range(N_CHIPS - 1):
                acc = acc + r_ref[j].astype(F32)
            o_ref[...] = acc

    in_specs, out_specs = [], []
    for p, tr in zip(pairs, trs):
        cols = p.shape[2]
        in_specs.append(pl.BlockSpec((None, tr, cols), lambda s, q: (q[0], s, 0)))
        in_specs.append(pl.BlockSpec((N_CHIPS - 1, tr, cols), lambda s, q: (0, s, 0)))
        out_specs.append(pl.BlockSpec((None, tr, cols), lambda s, q: (q[1], s, 0)))
    grid_spec = pltpu.PrefetchScalarGridSpec(num_scalar_prefetch=1, grid=(_SUM_STEPS,), in_specs=in_specs,
                                             out_specs=tuple(out_specs))
    return list(pl.pallas_call(
        body, name=name, out_shape=tuple(_sds((2,) + p.shape[1:]) for p in pairs), grid_spec=grid_spec,
        compiler_params=_params(("parallel",)),
    )(jnp.stack([chip, core]).astype(jnp.int32), *[t for pair in zip(pairs, recvs) for t in pair]))


def _place():
    return lax.axis_index("x"), lax.axis_index("y"), lax.axis_index("c")


def _other_chips(x, y):
    return [(x, 1 - y), (1 - x, y), (1 - x, 1 - y)]


_HBM = pl.BlockSpec(memory_space=pltpu.HBM)


def all_gather8(v, name, after=()):
    m, n = v.shape

    def body(x_ref, *refs):
        out_ref, send_sems, recv_sems, local_sem = refs[len(after):]
        x, y, c = _place()
        me, sibling = (x, y, c), (x, y, 1 - c)
        chips = _other_chips(x, y)

        def rows(px, py, pc):
            return out_ref.at[pl.ds((4 * px + 2 * py + pc) * m, m), :]

        def copy(k, block, to, src=None):
            return pltpu.make_async_remote_copy(
                src_ref=rows(*block) if src is None else src, dst_ref=rows(*block),
                send_sem=send_sems.at[k], recv_sem=recv_sems.at[k], device_id=to, device_id_type=MESH)

        mine = pltpu.make_async_copy(x_ref, rows(*me), local_sem)
        mine.start()
        first = [copy(0, me, sibling, src=x_ref)]
        first += [copy(1 + j, me, (*chip, c), src=x_ref) for j, chip in enumerate(chips)]
        for cp in first:
            cp.start()
        passed = [copy(4 + j, (*chip, c), sibling) for j, chip in enumerate(chips)]
        for j, chip in enumerate(chips):
            copy(1 + j, (*chip, c), me).wait_recv()
            passed[j].start()
        copy(0, sibling, me).wait_recv()
        for j, chip in enumerate(chips):
            copy(4 + j, (*chip, 1 - c), me).wait_recv()
        for cp in first + passed:
            cp.wait_send()
        mine.wait()

    return pl.pallas_call(
        body, name=name, out_shape=_sds((N_DEV * m, n), v.dtype),
        in_specs=[pl.BlockSpec(memory_space=pltpu.VMEM)] + [pl.BlockSpec(memory_space=pl.ANY)] * len(after),
        out_specs=pl.BlockSpec(memory_space=pltpu.VMEM),
        scratch_shapes=[pltpu.SemaphoreType.DMA((7,)), pltpu.SemaphoreType.DMA((7,)), pltpu.SemaphoreType.DMA],
        compiler_params=_params(None, VMEM_BIG),
    )(v, *after)


def _comm_call(body, name, ins, out_shapes, nsem, aliases=None):
    return pl.pallas_call(
        body, name=name, out_shape=tuple(out_shapes), in_specs=[_HBM] * len(ins), out_specs=tuple([_HBM] * len(out_shapes)),
        scratch_shapes=[pltpu.SemaphoreType.DMA((nsem,)), pltpu.SemaphoreType.DMA((nsem,))],
        input_output_aliases=aliases or {},
    )(*ins)


def _remote(src, dst, send_sems, recv_sems, k, to):
    return pltpu.make_async_remote_copy(src_ref=src, dst_ref=dst, send_sem=send_sems.at[k], recv_sem=recv_sems.at[k],
                                        device_id=to, device_id_type=MESH)


def _half(core, rh):
    return pl.ds(pl.multiple_of(core * rh, 16), rh)


def swap_halves(gs, name):
    n = len(gs)

    def body(*refs):
        ins, outs, (send_sems, recv_sems) = refs[:n], refs[n:2 * n], refs[2 * n:]
        x, y, c = _place()
        copies = []
        for i in range(n):
            theirs = _half(1 - c, ins[i].shape[1] // 2)
            cp = _remote(ins[i].at[:, theirs], outs[i], send_sems, recv_sems, i, (x, y, 1 - c))
            cp.start()
            copies.append(cp)
        for cp in copies:
            cp.wait()

    return _comm_call(body, name, gs, [_sds((g.shape[0], g.shape[1] // 2, g.shape[2]), g.dtype) for g in gs], n)


def join_halves(bufs, name):
    n = len(bufs)

    def body(*refs):
        ins, outs, (send_sems, recv_sems) = refs[:n], refs[n:2 * n], refs[2 * n:]
        x, y, c = _place()
        copies = []
        for i in range(n):
            cp = _remote(ins[i].at[c], outs[i].at[c], send_sems, recv_sems, i, (x, y, 1 - c))
            cp.start()
            copies.append(cp)
        for i in range(n):
            theirs = outs[i].at[1 - c]
            _remote(theirs, theirs, send_sems, recv_sems, i, (x, y, 1 - c)).wait_recv()
        for cp in copies:
            cp.wait_send()

    return _comm_call(body, name, bufs, [_sds(b.shape, b.dtype) for b in bufs], n, {i: i for i in range(n)})


def forward_halves(lands, name):
    n = len(lands)

    def body(*refs):
        ins, outs, (send_sems, recv_sems) = refs[:n], refs[n:2 * n], refs[2 * n:]
        x, y, c = _place()
        sibling = (x, y, 1 - c)
        chips = _other_chips(x, y)
        copies = []
        for i in range(n):
            mine = _half(c, ins[i].shape[1] // 2)
            for j, (px, py) in enumerate(chips):
                cp = _remote(ins[i].at[2 * px + py, mine], outs[i].at[2 * px + py, mine], send_sems, recv_sems, 3 * i + j, sibling)
                cp.start()
                copies.append(cp)
        for i in range(n):
            theirs = _half(1 - c, ins[i].shape[1] // 2)
            for j, (px, py) in enumerate(chips):
                landed = outs[i].at[2 * px + py, theirs]
                _remote(landed, landed, send_sems, recv_sems, 3 * i + j, sibling).wait_recv()
        for cp in copies:
            cp.wait_send()

    return _comm_call(body, name, lands, [_sds(b.shape, b.dtype) for b in lands], 3 * n, {i: i for i in range(n)})


_SEM = pl.BlockSpec(memory_space=pltpu.SEMAPHORE)
_EFFECT = pltpu.SideEffectType.DATAFLOW_SIDE_EFFECTING


def _gather_copies(srcs, lands, send_sems, recv_sems):
    x, y, c = _place()
    copies = []
    for i in range(len(srcs)):
        mine = _half(c, srcs[i].shape[0] // 2)
        for j, chip in enumerate(_other_chips(x, y)):
            copies.append(_remote(srcs[i].at[mine], lands[i].at[2 * x + y, mine], send_sems, recv_sems, 3 * i + j, (*chip, c)))
    return copies


def _exchange_copies(srcs, lands, send_sems, recv_sems):
    x, y, c = _place()
    copies = []
    for i in range(len(srcs)):
        for j, (px, py) in enumerate(_other_chips(x, y)):
            copies.append(_remote(srcs[i].at[2 * px + py], lands[i].at[j], send_sems, recv_sems, 3 * i + j, (px, py, c)))
    return copies


def _everyone_copies(srcs, lands, send_sems, recv_sems):
    x, y, c = _place()
    flip = lambda v, b: 1 - v if b else v
    dst = lands[0].at[4 * x + 2 * y + c]
    return [_remote(srcs[0], dst, send_sems, recv_sems, j - 1, (flip(x, j & 4), flip(y, j & 2), flip(c, j & 1)))
            for j in range(1, N_DEV)]


GATHER = (_gather_copies, 3)
EXCHANGE = (_exchange_copies, 3)
EVERYONE = (_everyone_copies, N_DEV - 1)


def split_start(name, plan, srcs, land_shapes, after=()):
    copies_fn, per_source = plan
    n, m, k = len(srcs), len(land_shapes), len(after)
    ncopies = per_source * n

    def body(*refs):
        src_refs, land_refs = refs[:n], refs[n:n + m]
        send_sems, recv_sems = refs[n + m + k], refs[n + m + k + 1]
        token = refs[-1]
        for cp in copies_fn(src_refs, land_refs, send_sems, recv_sems):
            cp.start()
        token[...] = jnp.zeros_like(token)

    hbm = lambda s: pltpu.HBM(tuple(s.shape), s.dtype)
    outs = pl.pallas_call(
        body, name=name,
        out_shape=(pltpu.SemaphoreType.DMA((ncopies,)), pltpu.SemaphoreType.DMA((ncopies,)), *[hbm(s) for s in srcs],
                   *[hbm(s) for s in land_shapes], _sds((8, LANES))),
        in_specs=[_HBM] * (n + m) + [pl.BlockSpec(memory_space=pl.ANY)] * k,
        out_specs=(_SEM, _SEM, *([_HBM] * (n + m)), pl.BlockSpec(memory_space=pltpu.VMEM)),
        input_output_aliases={i: 2 + i for i in range(n + m)},
        compiler_params=pltpu.CompilerParams(has_side_effects=_EFFECT),
    )(*[pltpu.with_memory_space_constraint(s, pltpu.HBM) for s in srcs],
      *[pltpu.with_memory_space_constraint(lax.empty(tuple(s.shape), s.dtype), pltpu.HBM) for s in land_shapes], *after)
    handle = (outs[0], outs[1], list(outs[2:2 + n]), list(outs[2 + n:2 + n + m]))
    return handle, outs[-1][0, 0]


def split_wait(name, plan, handle, after):
    copies_fn, _ = plan
    send_sems, recv_sems, srcs, lands = handle
    n, m = len(srcs), len(lands)
    after = list(after) if isinstance(after, (list, tuple)) else [after]

    def body(*refs):
        src_refs, land_refs = refs[:n], refs[n:n + m]
        for cp in copies_fn(src_refs, land_refs, refs[n + m], refs[n + m + 1]):
            cp.wait_send()
            cp.wait_recv()

    hbm = lambda s: pltpu.HBM(tuple(s.shape), s.dtype)
    outs = pl.pallas_call(
        body, name=name, out_shape=tuple(hbm(s) for s in srcs + lands),
        in_specs=[_HBM] * (n + m) + [_SEM, _SEM] + [pl.BlockSpec(memory_space=pl.ANY)] * len(after),
        out_specs=tuple([_HBM] * (n + m)), input_output_aliases={i: i for i in range(n + m)},
        compiler_params=pltpu.CompilerParams(has_side_effects=_EFFECT),
    )(*srcs, *lands, send_sems, recv_sems, *after)
    return list(outs[:n]), list(outs[n:])


def chip_major(w, groups=N_CHIPS):
    r, c = w.shape
    return w.reshape(r, groups, c // groups).transpose(1, 0, 2)


def from_chip_major(w):
    g, r, c = w.shape
    return w.transpose(1, 0, 2).reshape(r, g * c)


def _cd_in_pad(w):
    a = C_Q_RANK + C_KV_RANK
    z = lambda n: jnp.zeros((w.shape[0], n), w.dtype)
    return jnp.concatenate([w[:, :a], z(C_NOPE), w[:, a:a + C_ROPE], z(HEAD_PAD - C_NOPE - C_ROPE), w[:, a + C_ROPE:]], axis=1)


def _cd_in_unpad(w):
    a = C_Q_RANK + C_KV_RANK
    return jnp.concatenate([w[:, :a], w[:, a + C_NOPE:a + C_NOPE + C_ROPE], w[:, a + HEAD_PAD:]], axis=1)


def _pad_heads(w, width):
    r = w.shape[0]
    w = w.reshape(r, C_HEADS, width)
    return jnp.pad(w, ((0, 0), (0, 0), (0, HEAD_PAD - width))).reshape(r, _HW)


def _unpad_heads(w, width):
    r = w.shape[0]
    return w.reshape(r, C_HEADS, HEAD_PAD)[:, :, :width].reshape(r, C_HEADS * width)


def prepare_weights(p):
    q = dict(p)
    q["cd_w_in"] = _cd_in_pad(p["cd_w_in"])
    q["c_w_uq"] = _pad_heads(p["c_w_uq"], C_NOPE + C_ROPE)
    ukv = p["c_w_ukv"].reshape(C_KV_RANK, C_HEADS, C_NOPE + C_V)
    q["c_w_uk"] = _pad_heads(ukv[:, :, :C_NOPE].reshape(C_KV_RANK, -1), C_NOPE)
    q["c_w_uv"] = _pad_heads(ukv[:, :, C_NOPE:].reshape(C_KV_RANK, -1), C_V)
    wo = p["cd_w_out"]
    att_rows = jnp.pad(wo[:C_HEADS * C_V].reshape(C_HEADS, C_V, D_MODEL), ((0, 0), (0, HEAD_PAD - C_V), (0, 0)))
    q["cd_w_out"] = jnp.concatenate([att_rows.reshape(_HW, D_MODEL), wo[C_HEADS * C_V:]], axis=0)
    return q


def unprepare_grads(g):
    q = dict(g)
    q["cd_w_in"] = _cd_in_unpad(g["cd_w_in"])
    q["c_w_uq"] = _unpad_heads(g["c_w_uq"], C_NOPE + C_ROPE)
    uk = g.pop("c_w_uk").reshape(C_KV_RANK, C_HEADS, HEAD_PAD)[:, :, :C_NOPE]
    uv = g.pop("c_w_uv").reshape(C_KV_RANK, C_HEADS, HEAD_PAD)[:, :, :C_V]
    q.pop("c_w_uk", None)
    q.pop("c_w_uv", None)
    q["c_w_ukv"] = jnp.concatenate([uk, uv], axis=-1).reshape(C_KV_RANK, C_HEADS * (C_NOPE + C_V))
    wo = g["cd_w_out"]
    att = wo[:_HW].reshape(C_HEADS, HEAD_PAD, D_MODEL)[:, :C_V].reshape(C_HEADS * C_V, D_MODEL)
    q["cd_w_out"] = jnp.concatenate([att, wo[_HW:]], axis=0)
    return q


def rope_tables(positions):
    half = C_ROPE // 2
    inv_freq = ROPE_THETA ** (-jnp.arange(half, dtype=F32) / half)
    ang = positions.astype(F32)[:, None] * inv_freq
    cos, sin = jnp.cos(ang), jnp.sin(ang)
    s = positions.shape[0]
    z = lambda n: jnp.zeros((s, n), F32)
    cs = jnp.concatenate([jnp.ones((s, C_NOPE), F32), cos, cos, z(HEAD_PAD - C_NOPE - C_ROPE)], axis=1)
    s1 = jnp.concatenate([z(C_NOPE), -sin, z(HEAD_PAD - C_NOPE - half)], axis=1)
    s2 = jnp.concatenate([z(C_NOPE + half), sin, z(HEAD_PAD - C_NOPE - C_ROPE)], axis=1)
    return cs, s1, s2


_UP_COLS = 2 * D_FF // N_CHIPS


def ffn_fwd(h2, w, l, late_down=None):
    zf = matmul(h2, w["ffn_w_up"][l], "nn", BF16, f"ffn_up{l}", gb=N_CHIPS, go=2, tn=_UP_COLS)
    if late_down is not None:
        late_down(zf)
    a, f = ffn_act_down(zf, w["ffn_conv_w"][l], w["ffn_w_down"][l], f"ffn_act_down{l}")
    return f, (zf, a)


def ffn_bwd(df, h2, saved, w, l):
    zf, a = saved
    da = matmul(df, w["ffn_w_down"][l], "nt", BF16, f"ffn_down_dx{l}", tn=D_FF // 2)
    d_down = matmul(a, df, "tn", BF16, f"ffn_down_dw{l}", tm=D_FF // 2)
    dzf, d_conv = ffn_act_bwd(zf, w["ffn_conv_w"][l], da, f"ffn_act_bwd{l}")
    dh2 = matmul(dzf, w["ffn_w_up"][l], "nt", F32, f"ffn_up_dx{l}", ga=2, gb=N_CHIPS, tk=_UP_COLS, tn=D_MODEL)
    d_up = matmul(h2, dzf, "tn", BF16, f"ffn_up_dw{l}", gb=2, go=N_CHIPS, tn=_UP_COLS)
    d_conv = d_conv.transpose(1, 0, 2).reshape(3, 2 * D_FF)
    return dh2, dict(ffn_w_down=d_down, ffn_conv_w=d_conv, ffn_w_up=d_up)


def mixer0_fwd(h, w):
    z = matmul(h, w["ab_w_in"], "nn", BF16, "ab_in", gb=N_CHIPS)
    ya = gconv_fwd(z, w["a_conv_w"])
    yb = pool_fwd(z, w["b_mix_w"], w["b_scale"])
    ycat = jnp.concatenate([ya, yb], axis=1)
    y = matmul(ycat, w["ab_w_out"], "nn", BF16, "ab_out", tn=D_MODEL)
    return y, (z, ycat)


def mixer0_bwd(dy, h, saved, w):
    z, ycat = saved
    grads = {}
    dycat = matmul(dy, w["ab_w_out"], "nt", BF16, "ab_out_dx")
    grads["ab_w_out"] = matmul(ycat, dy, "tn", BF16, "ab_out_dw")
    db, dc, da, d_conv = gconv_bwd(z, w["a_conv_w"], dycat)
    dp, d_mix, d_scale = pool_bwd(z, w["b_mix_w"], w["b_scale"], dycat)
    dz = jnp.concatenate([db, dc, da, dp], axis=1)
    dh = matmul(dz, w["ab_w_in"], "nt", F32, "ab_in_dx", gb=N_CHIPS, tn=D_MODEL)
    grads["ab_w_in"] = matmul(h, dz, "tn", BF16, "ab_in_dw", go=N_CHIPS)
    grads.update(a_conv_w=d_conv, b_mix_w=d_mix, b_scale=d_scale)
    return dh, grads


def mixer1_fwd(h, ropes, w):
    cs, s1, s2 = ropes
    z = matmul(h, w["cd_w_in"], "nn", BF16, "cd_in")
    bs_t = jnp.pad(w["d_b_s"].T, ((0, 0), (0, LANES - D_GROUPS)))
    qh, kh, vh = mla_pre_fwd(z, w["c_q_norm_g"], w["c_kv_norm_g"], w["c_w_uq"], w["c_w_uk"], w["c_w_uv"], cs, s1, s2)
    oh = attn_fwd(qh, kh, vh)
    yd = sgu_fwd(z, w["d_ln_g"], w["d_ln_b"], w["d_w_s"], bs_t)
    ycat = jnp.concatenate([oh, yd], axis=1)
    y = matmul(ycat, w["cd_w_out"], "nn", BF16, "cd_out", tn=D_MODEL)
    return y, (z, bs_t, qh, kh, vh, oh, ycat)


def mixer1_bwd(dy, h, saved, ropes, w):
    cs, s1, s2 = ropes
    z, bs_t, qh, kh, vh, oh, ycat = saved
    grads = {}
    dycat = matmul(dy, w["cd_w_out"], "nt", BF16, "cd_out_dx")
    grads["cd_w_out"] = matmul(ycat, dy, "tn", F32, "cd_out_dw")
    dqh, dkh, dvh = attn_bwd(qh, kh, vh, oh, dycat, 0)
    dzq, d_uq, d_uk, d_uv, d_gq, d_gkv = mla_pre_bwd(
        z, w["c_q_norm_g"], w["c_kv_norm_g"], w["c_w_uq"], w["c_w_uk"], w["c_w_uv"], cs, s1, s2, dqh, dkh, dvh)
    dzu, dzv, d_ws, d_bs, d_lg, d_lb = sgu_bwd(z, w["d_ln_g"], w["d_ln_b"], w["d_w_s"], bs_t, dycat, _HW // _DW)
    dz = jnp.concatenate([dzq, dzu, dzv], axis=1)
    dh = matmul(dz, w["cd_w_in"], "nt", F32, "cd_in_dx", tn=D_MODEL)
    grads["cd_w_in"] = matmul(h, dz, "tn", F32, "cd_in_dw")
    grads.update(c_w_uq=d_uq, c_w_uk=d_uk, c_w_uv=d_uv, c_q_norm_g=d_gq, c_kv_norm_g=d_gkv, d_w_s=d_ws,
                 d_b_s=d_bs[:, :D_GROUPS].T, d_ln_g=d_lg, d_ln_b=d_lb)
    return dh, grads


class StepHooks:
    def weights(self, stage, after):
        pass

    def gradients(self, stage, grads, after):
        return 0.0


def run_step(x, tgt, mod, ropes, w, hooks):
    sh1, sc1, g1, sh2, sc2, g2 = range(N_MOD)
    mods = mod.reshape(2, 1, N_MOD * D_MODEL)
    n1 = w["norm1_g"].reshape(2, 1, D_MODEL)
    n2 = w["norm2_g"].reshape(2, 1, D_MODEL)
    final_g = Vec(w["final_norm_g"].reshape(1, 1, D_MODEL), 0, 0)

    hooks.weights("mix0", mod)
    h0 = modnorm_fwd(x, Vec(n1, 0, 0), Vec(mods, 0, sc1), Vec(mods, 0, sh1), "modnorm_0")
    y0, mix0 = mixer0_fwd(h0, w)
    x1, h1 = resid_modnorm_fwd(x, y0, Vec(mods, 0, g1), Vec(n2, 0, 0), Vec(mods, 0, sc2), Vec(mods, 0, sh2), "resid_modnorm_1")
    hooks.weights("up0", x1)
    f0, ffn0 = ffn_fwd(h1, w, 0, lambda act: hooks.weights("down0", act))
    x2, h2 = resid_modnorm_fwd(x1, f0, Vec(mods, 0, g2), Vec(n1, 1, 0), Vec(mods, 1, sc1), Vec(mods, 1, sh1), "resid_modnorm_2")
    hooks.weights("mix1", x2)
    y1, mix1 = mixer1_fwd(h2, ropes, w)
    x3, h3 = resid_modnorm_fwd(x2, y1, Vec(mods, 1, g1), Vec(n2, 1, 0), Vec(mods, 1, sc2), Vec(mods, 1, sh2), "resid_modnorm_3")
    hooks.weights("ffn1", x3)
    f1, ffn1 = ffn_fwd(h3, w, 1)
    dres, d_final, loss, df1, dg2b = final_fused(x3, f1, Vec(mods, 1, g2), final_g, tgt)

    dh3, gf1 = ffn_bwd(df1, h3, ffn1, w, 1)
    late = mods + hooks.gradients("ffn1", gf1, dh3)
    dres, dsh2b, dsc2b, dn2b, dy1, dg1b = norm_gate_bwd(
        x3, dh3, Vec(n2, 1, 0), Vec(late, 1, sc2), dres, y1, Vec(late, 1, g1), "norm_gate_bwd_3")
    dh2, gm1 = mixer1_bwd(dy1, h2, mix1, ropes, w)
    late = mods + hooks.gradients("mix1", gm1, dh2)
    dres, dsh1b, dsc1b, dn1b, df0, dg2a = norm_gate_bwd(
        x2, dh2, Vec(n1, 1, 0), Vec(late, 1, sc1), dres, f0, Vec(late, 0, g2), "norm_gate_bwd_2")
    dh1, gf0 = ffn_bwd(df0, h1, ffn0, w, 0)
    late = mods + hooks.gradients("ffn0", gf0, dh1)
    dres, dsh2a, dsc2a, dn2a, dy0, dg1a = norm_gate_bwd(
        x1, dh1, Vec(n2, 0, 0), Vec(late, 0, sc2), dres, y0, Vec(late, 0, g1), "norm_gate_bwd_1")
    dh0, gm0 = mixer0_bwd(dy0, h0, mix0, w)
    late = mods + hooks.gradients("mix0", gm0, dh0)
    grad_x, dsh1a, dsc1a, dn1a = norm_bwd(x, dh0, Vec(n1, 0, 0), Vec(late, 0, sc1), dres, "norm_bwd_0")

    dmod = jnp.concatenate([jnp.concatenate([dsh1a, dsc1a, dg1a, dsh2a, dsc2a, dg2a], axis=1),
                            jnp.concatenate([dsh1b, dsc1b, dg1b, dsh2b, dsc2b, dg2b], axis=1)], axis=0)
    norms = dict(norm1_g=jnp.concatenate([dn1a, dn1b], axis=0), norm2_g=jnp.concatenate([dn2a, dn2b], axis=0),
                 final_norm_g=d_final)
    return loss, grad_x, dmod, dict(mix0=gm0, ffn0=gf0, mix1=gm1, ffn1=gf1, norms=norms)


def merge_grads(by_stage):
    grads = {**by_stage["mix0"], **by_stage["mix1"], **by_stage["norms"]}
    for k in ("ffn_w_down", "ffn_w_up"):
        grads[k] = [by_stage["ffn0"][k], by_stage["ffn1"][k]]
    grads["ffn_conv_w"] = jnp.stack([by_stage["ffn0"]["ffn_conv_w"], by_stage["ffn1"]["ffn_conv_w"]])
    return grads


_WEIGHTS = ("ada_w", "ada_b", "norm1_g", "norm2_g", "ab_w_in", "a_conv_w", "b_mix_w", "b_scale", "ab_w_out", "cd_w_in",
            "c_q_norm_g", "c_w_uq", "c_kv_norm_g", "c_w_ukv", "d_ln_g", "d_ln_b", "d_w_s", "d_b_s", "cd_w_out",
            "ffn_w_up", "ffn_conv_w", "ffn_w_down", "final_norm_g")
_INPUTS = ("x", "c", "positions") + _WEIGHTS + ("loss_target",) + tuple("m_" + n for n in _WEIGHTS) + tuple(
    "v_" + n for n in _WEIGHTS)

def _pack_rows(parts, rows, dtype):
    flat = jnp.concatenate([p.reshape(-1).astype(dtype) for p in parts])
    return jnp.pad(flat, (0, rows * LANES - flat.shape[0])).reshape(rows, LANES)


def _rows_major(w):
    r, c = w.shape
    return w.reshape(N_CHIPS, r // N_CHIPS, c)


def start_gather(shards, tag, after=()):
    lands = [_sds((N_CHIPS,) + s.shape, s.dtype) for s in shards]
    return split_start("gather_start_" + tag, GATHER, shards, lands, after)


def finish_gather(handle, chip, tag, after):
    shards, lands = split_wait("gather_wait_" + tag, GATHER, handle, after)
    lands = forward_halves(lands, "gather_forward_" + tag)
    return [lax.dynamic_update_index_in_dim(o, s, chip, 0) for o, s in zip(lands, shards)]


def start_reduce(gs, core, tag):
    recv = swap_halves(gs, "swap_halves_" + tag)
    pairs = pair_sums(gs, recv, core, "pair_sums_" + tag)
    lands = [_sds((N_CHIPS - 1,) + p.shape[1:], p.dtype) for p in pairs]
    return split_start("exchange_start_" + tag, EXCHANGE, pairs, lands)


def finish_reduce(handle, chip, core, tag, after):
    pairs, others = split_wait("exchange_wait_" + tag, EXCHANGE, handle, after)
    halves = chip_sums(pairs, others, chip, core, "chip_sums_" + tag)
    full = join_halves(halves, "join_halves_" + tag)
    return [f.reshape(f.shape[1] * 2, f.shape[2]) for f in full]


_SMALL_SHARDED = (("a_conv_w", (3, 128), 1), ("c_q_norm_g", (1, 64), 1), ("d_ln_g", (1, 128), 1), ("d_ln_b", (1, 128), 1),
                  ("ffn_conv_w", (2, 3, 2 * D_FF // N_CHIPS), 2))
_SMALL_GRADS = (("norm1_g", (2, D_MODEL)), ("norm2_g", (2, D_MODEL)), ("b_mix_w", (4, 128, 128)), ("b_scale", (1, 512)),
                ("c_kv_norm_g", (1, 128)), ("d_w_s", (4, 128, 128)), ("d_b_s", (4, 128)), ("final_norm_g", (1, D_MODEL)),
                ("a_conv_w", (3, 512)), ("c_q_norm_g", (1, 256)), ("d_ln_g", (1, 512)), ("d_ln_b", (1, 512)),
                ("ffn_conv_w", (2, 3, 2 * D_FF)))


def _size(shape):
    n = 1
    for d in shape:
        n *= d
    return n


def kernel(x, c, positions, ada_w, ada_b, norm1_g, norm2_g, ab_w_in, a_conv_w, b_mix_w, b_scale, ab_w_out, cd_w_in, c_q_norm_g, c_w_uq, c_kv_norm_g, c_w_ukv, d_ln_g, d_ln_b, d_w_s, d_b_s, cd_w_out, ffn_w_up, ffn_conv_w, ffn_w_down, final_norm_g, loss_target, m_ada_w, m_ada_b, m_norm1_g, m_norm2_g, m_ab_w_in, m_a_conv_w, m_b_mix_w, m_b_scale, m_ab_w_out, m_cd_w_in, m_c_q_norm_g, m_c_w_uq, m_c_kv_norm_g, m_c_w_ukv, m_d_ln_g, m_d_ln_b, m_d_w_s, m_d_b_s, m_cd_w_out, m_ffn_w_up, m_ffn_conv_w, m_ffn_w_down, m_final_norm_g, v_ada_w, v_ada_b, v_norm1_g, v_norm2_g, v_ab_w_in, v_a_conv_w, v_b_mix_w, v_b_scale, v_ab_w_out, v_cd_w_in, v_c_q_norm_g, v_c_w_uq, v_c_kv_norm_g, v_c_w_ukv, v_d_ln_g, v_d_ln_b, v_d_w_s, v_d_b_s, v_cd_w_out, v_ffn_w_up, v_ffn_conv_w, v_ffn_w_down, v_final_norm_g):
    args = (x, c, positions, ada_w, ada_b, norm1_g, norm2_g, ab_w_in, a_conv_w, b_mix_w, b_scale, ab_w_out, cd_w_in, c_q_norm_g, c_w_uq, c_kv_norm_g, c_w_ukv, d_ln_g, d_ln_b, d_w_s, d_b_s, cd_w_out, ffn_w_up, ffn_conv_w, ffn_w_down, final_norm_g, loss_target, m_ada_w, m_ada_b, m_norm1_g, m_norm2_g, m_ab_w_in, m_a_conv_w, m_b_mix_w, m_b_scale, m_ab_w_out, m_cd_w_in, m_c_q_norm_g, m_c_w_uq, m_c_kv_norm_g, m_c_w_ukv, m_d_ln_g, m_d_ln_b, m_d_w_s, m_d_b_s, m_cd_w_out, m_ffn_w_up, m_ffn_conv_w, m_ffn_w_down, m_final_norm_g, v_ada_w, v_ada_b, v_norm1_g, v_norm2_g, v_ab_w_in, v_a_conv_w, v_b_mix_w, v_b_scale, v_ab_w_out, v_cd_w_in, v_c_q_norm_g, v_c_w_uq, v_c_kv_norm_g, v_c_w_ukv, v_d_ln_g, v_d_ln_b, v_d_w_s, v_d_b_s, v_cd_w_out, v_ffn_w_up, v_ffn_conv_w, v_ffn_w_down, v_final_norm_g)
    a = dict(zip(_INPUTS, args, strict=True))
    xi, yi, ci = _place()
    chip = 2 * xi + yi
    dev = 4 * xi + 2 * yi + ci
    x = a["x"][0]
    tgt = a["loss_target"][0]

    bf = lambda t: t.astype(BF16)
    mix0_handle, tok = start_gather([bf(a["ab_w_in"][0]), bf(a["ab_w_out"][0])], "mix0")
    up0_16, down0_16, up1_16, down1_16 = [bf(a[n][l]) for l in (0, 1) for n in ("ffn_w_up", "ffn_w_down")]
    mix1_16 = [bf(a[n][0]) for n in ("cd_w_in", "c_w_uq", "c_w_ukv", "cd_w_out")]

    small_parts = [a["c"] + tok] + [a[n] for n, _, _ in _SMALL_SHARDED]
    rows1 = -(-sum(p.size for p in small_parts) // LANES // 8) * 8
    g1 = all_gather8(_pack_rows(small_parts, rows1, F32), "gather_small",
                     [up0_16, down0_16, up1_16, down1_16, mix1_16[0], mix1_16[3]]).reshape(N_DEV, rows1 * LANES)
    c_all = g1[:, :D_MODEL]
    per_chip = g1[0::2]
    small_full = {}
    off = D_MODEL
    for n, shp, axis in _SMALL_SHARDED:
        piece = per_chip[:, off:off + _size(shp)].reshape((N_CHIPS,) + shp)
        small_full[n] = jnp.concatenate([piece[k] for k in range(N_CHIPS)], axis=axis)
        off += _size(shp)

    merge = lambda t: t.reshape(t.shape[0] * t.shape[1], t.shape[2])
    w = dict(norm1_g=a["norm1_g"], norm2_g=a["norm2_g"], b_mix_w=a["b_mix_w"][0], b_scale=a["b_scale"],
             c_kv_norm_g=a["c_kv_norm_g"], d_w_s=a["d_w_s"][0], d_b_s=a["d_b_s"][0],
             final_norm_g=a["final_norm_g"].reshape(1, D_MODEL), **small_full)

    ncol = N_MOD * D_MODEL // N_CHIPS
    ada_b_mine = lax.dynamic_slice_in_dim(a["ada_b"], chip * ncol, ncol, axis=1)
    mod_cols = ada_mod(c_all, a["ada_w"], ada_b_mine)
    g2_rows = all_gather8(mod_cols.reshape(-1, LANES), "gather_mod")
    g2 = g2_rows.reshape(N_DEV, 2, N_DEV, ncol)
    mod = lax.dynamic_index_in_dim(g2[0::2], dev, axis=2, keepdims=False)
    mod = mod.transpose(1, 0, 2).reshape(2, N_MOD * D_MODEL)

    late = [g2_rows]
    up0_handle, tok_a = start_gather([up0_16], "up0", late)
    down0_handle, tok_b = start_gather([down0_16], "down0", late)
    mix1_handle, tok_c = start_gather(mix1_16, "mix1", late)
    ffn1_handle, tok_d = start_gather([up1_16, down1_16], "ffn1", late)
    mod = mod + (tok_a + tok_b + tok_c + tok_d)

    ropes = rope_tables(a["positions"][0])
    cm16 = lambda t: chip_major(t).astype(BF16)
    w.update(ffn_w_up=[None, None], ffn_w_down=[None, None])
    handles = dict(mix0=mix0_handle, up0=up0_handle, down0=down0_handle, mix1=mix1_handle, ffn1=ffn1_handle)
    reducing, reduced = {}, {}

    class Hooks(StepHooks):
        def weights(self, stage, after):
            got = finish_gather(handles[stage], chip, stage, after)
            if stage == "mix0":
                w.update(ab_w_in=got[0], ab_w_out=merge(got[1]))
            elif stage == "up0":
                w["ffn_w_up"][0] = got[0]
            elif stage == "down0":
                w["ffn_w_down"][0] = merge(got[0])
            elif stage == "mix1":
                cd_in, uq, ukv, cd_out = got
                w.update(prepare_weights(dict(cd_w_in=from_chip_major(cd_in), c_w_uq=from_chip_major(uq),
                                              c_w_ukv=from_chip_major(ukv), cd_w_out=merge(cd_out))))
            else:
                w["ffn_w_up"][1], w["ffn_w_down"][1] = got[0], merge(got[1])

        def gradients(self, stage, grads, after):
            if stage in ("ffn0", "ffn1"):
                parts = [grads["ffn_w_up"], _rows_major(grads["ffn_w_down"])]
            elif stage == "mix1":
                grads.update(unprepare_grads(grads))
                parts = [cm16(grads["cd_w_in"]), cm16(grads["c_w_uq"]), cm16(grads["c_w_ukv"]),
                         _rows_major(grads["cd_w_out"]).astype(BF16)]
            else:
                parts = [grads["ab_w_in"], _rows_major(grads["ab_w_out"])]
            reducing[stage], tok = start_reduce(parts, ci, stage)
            before = {"mix1": "ffn1", "ffn0": "mix1", "mix0": "ffn0"}.get(stage)
            if before is not None:
                reduced[before] = finish_reduce(reducing[before], chip, ci, before, after)
            return tok

    loss, grad_x, dmod, by_stage = run_step(x, tgt, mod, ropes, w, Hooks())
    grads = merge_grads(by_stage)

    parts3 = [dmod] + [grads[n] for n, _ in _SMALL_GRADS] + [loss[0, 0]]
    rows3 = -(-sum(p.size for p in parts3) // LANES // 8) * 8
    small_handle, _ = split_start("small_grads_start", EVERYONE, [_pack_rows(parts3, rows3, F32)],
                                  [_sds((N_DEV, rows3, LANES))])
    red_up1, red_down1 = reduced["ffn1"]
    red_cd_in, red_uq, red_ukv, red_cd_out = reduced["mix1"]
    red_up0, red_down0 = reduced["ffn0"]
    out_grads = dict(cd_w_in=red_cd_in, c_w_uq=red_uq, c_w_ukv=red_ukv, cd_w_out=red_cd_out)
    per_layer = dict(ffn_w_up=(red_up0, red_up1), ffn_w_down=(red_down0, red_down1))
    updates = {}

    def update(n):
        if n in per_layer:
            updates[n] = adamw_layers(a[n], *per_layer[n], a["m_" + n], a["v_" + n], "adamw_" + n)
        else:
            updates[n] = adamw(a[n], out_grads[n].reshape(a[n].shape), a["m_" + n], a["v_" + n], "adamw_" + n)

    early =("ffn_w_up", "ffn_w_down", "cd_w_in", "c_w_uq", "c_w_ukv", "cd_w_out")
    for n in early:
        update(n)
    (mine,), (landed,) = split_wait("small_grads_wait", EVERYONE, small_handle, [updates[n][1] for n in early])
    g3 = lax.dynamic_update_index_in_dim(landed, mine, dev, 0)
    summed = sum8(g3).reshape(-1)
    nmod = 2 * N_MOD * D_MODEL
    out_grads["ada_b"] = summed[:nmod].reshape(2, N_MOD * D_MODEL)
    off = nmod
    for n, shp in _SMALL_GRADS:
        out_grads[n] = summed[off:off + _size(shp)].reshape(shp)
        off += _size(shp)
    loss = summed[off]
    for n, shp, axis in _SMALL_SHARDED:
        width = out_grads[n].shape[-1] // N_CHIPS
        out_grads[n] = lax.dynamic_slice_in_dim(out_grads[n], chip * width, width, axis=out_grads[n].ndim - 1)
    dmod_all = g3.reshape(N_DEV, rows3 * LANES)[:, :nmod].reshape(N_DEV, 2, N_MOD * D_MODEL)
    dmod_mine = lax.dynamic_slice_in_dim(dmod_all, chip * ncol, ncol, axis=2).transpose(1, 0, 2)
    updates["ada_w"] = adamw_ada(a["ada_w"], c_all, dmod_mine, a["m_ada_w"], a["v_ada_w"])

    red_in0, red_out0 = finish_reduce(reducing["mix0"], chip, ci, "mix0", updates["ada_w"][1])
    out_grads.update(ab_w_in=red_in0, ab_w_out=red_out0)

    for n in ("ab_w_in", "ab_w_out"):
        update(n)
    small = [n for n in _WEIGHTS if n not in updates]
    for n, res in zip(small, adamw_small([a[n] for n in small], [out_grads[n].reshape(a[n].shape) for n in small],
                                         [a["m_" + n] for n in small], [a["v_" + n] for n in small])):
        updates[n] = res
    return (loss, grad_x[None], *[updates[n][i] for i in range(4) for n in _WEIGHTS])
```

```python
import functools
from typing import NamedTuple

import jax
import jax.numpy as jnp
from jax import lax
from jax.experimental import pallas as pl
from jax.experimental.pallas import tpu as pltpu

F32 = jnp.float32
BF16 = jnp.bfloat16
EPS = 1e-6
D_MODEL = 1024
N_MOD = 6
A_WIDTH = 512
B_GROUPS = 4
C_HEADS = 8
C_NOPE = 64
C_ROPE = 32
C_V = 64
C_Q_RANK = 256
C_KV_RANK = 128
HEAD_PAD = 128
ROPE_THETA = 10000.0
D_GROUPS = 4
D_CHUNK = 128
D_FF = 2816
FF_UNIT = 128
ADAM_LR = 0.001
ADAM_B1 = 0.9
ADAM_B2 = 0.999
ADAM_EPS = 1e-08
ADAM_WD = 0.01
ADAM_STEP = 10
N_CHIPS = 4
N_DEV = 8
LANES = 128
VMEM_BIG = 56 * 1024 * 1024
MESH = pl.DeviceIdType.MESH


def _sds(shape, dtype=F32):
    return jax.ShapeDtypeStruct(tuple(shape), dtype)


def _tile(n, cap, mult=128):
    if n <= cap:
        return n
    best = None
    for t in range(mult, cap + 1, mult):
        if n % t == 0:
            best = t
    assert best is not None, (n, cap, mult)
    return best


def _params(dims=None, vmem=None):
    return pltpu.CompilerParams(dimension_semantics=dims, vmem_limit_bytes=vmem)


def _shift_down(v, k):
    r = pltpu.roll(v, k, axis=0)
    t = lax.broadcasted_iota(jnp.int32, v.shape, 0)
    return jnp.where(t >= k, r, 0.0)


def _shift_up(v, k):
    n = v.shape[0]
    r = pltpu.roll(v, n - k, axis=0)
    t = lax.broadcasted_iota(jnp.int32, v.shape, 0)
    return jnp.where(t < n - k, r, 0.0)


def _sigmoid(v):
    return 1.0 / (1.0 + jnp.exp(-v))


_GELU_C = 0.7978845608028654
_GELU_A = 0.044715


def _gelu(v):
    return 0.5 * v * (1.0 + jnp.tanh(_GELU_C * (v + _GELU_A * v * v * v)))


def _gelu_grad(v):
    th = jnp.tanh(_GELU_C * (v + _GELU_A * v * v * v))
    return 0.5 * (1.0 + th) + 0.5 * v * (1.0 - th * th) * _GELU_C * (1.0 + 3.0 * _GELU_A * v * v)


_NN = (((1,), (0,)), ((), ()))
_NT = (((1,), (1,)), ((), ()))
_TN = (((0,), (0,)), ((), ()))


def _dot(a, b, dims=_NN):
    return lax.dot_general(a, b, dims, preferred_element_type=F32)


def _logical(t, groups):
    return (t.shape[-2], t.shape[-1] * groups)


def _block(tr, tc, groups, cols, where):
    if groups == 1:
        return pl.BlockSpec((tr, tc), where)
    per = cols // groups // tc

    def index(i, j, s):
        r, c = where(i, j, s)
        return (c // per, r, c % per)

    return pl.BlockSpec((None, tr, tc), index)


def matmul(a, b, mode, out_dtype, name, ga=1, gb=1, go=1, tm=None, tn=None, tk=None):
    (ar, ac), (br, bc) = _logical(a, ga), _logical(b, gb)
    if mode == "nn":
        m, k, n = ar, ac, bc
        a_col, b_col = "k", "n"
    elif mode == "nt":
        m, k, n = ar, ac, br
        a_col, b_col = "k", "k"
    else:
        k, m, n = ar, ac, bc
        a_col, b_col = "m", "n"
    limit = {"m": m, "n": n // go, "k": k}
    limit[a_col] = min(limit[a_col], ac // ga)
    limit[b_col] = min(limit[b_col], bc // gb)
    tm = tm or _tile(limit["m"], 2048, 128 if mode == "tn" else 16)
    tn = tn or _tile(limit["n"], 512)
    tk = tk or _tile(limit["k"], 2048, 16 if mode == "tn" else 128)
    nk = k // tk
    if mode == "nn":
        a_spec = _block(tm, tk, ga, ac, lambda i, j, s: (i, s))
        b_spec = _block(tk, tn, gb, bc, lambda i, j, s: (s, j))
        dims = _NN
    elif mode == "nt":
        a_spec = _block(tm, tk, ga, ac, lambda i, j, s: (i, s))
        b_spec = _block(tn, tk, gb, bc, lambda i, j, s: (j, s))
        dims = _NT
    else:
        a_spec = _block(tk, tm, ga, ac, lambda i, j, s: (s, i))
        b_spec = _block(tk, tn, gb, bc, lambda i, j, s: (s, j))
        dims = _TN
    o_spec = _block(tm, tn, go, n, lambda i, j, s: (i, j))
    out_shape = _sds((m, n), out_dtype) if go == 1 else _sds((go, m, n // go), out_dtype)

    def body(a_ref, b_ref, o_ref, acc_ref):
        s = pl.program_id(2)

        @pl.when(s == 0)
        def _():
            acc_ref[...] = jnp.zeros_like(acc_ref)

        acc_ref[...] += _dot(a_ref[...], b_ref[...], dims)

        @pl.when(s == nk - 1)
        def _():
            o_ref[...] = acc_ref[...].astype(o_ref.dtype)

    return pl.pallas_call(
        body, name=name, out_shape=out_shape, grid=(m // tm, n // tn, nk),
        in_specs=[a_spec, b_spec], out_specs=o_spec,
        scratch_shapes=[pltpu.VMEM((tm, tn), F32)],
        compiler_params=_params(("parallel", "parallel", "arbitrary"), VMEM_BIG),
    )(a, b)


def _rows(tm, n):
    return pl.BlockSpec((tm, n), lambda i: (i, 0))


def _vec(n):
    return pl.BlockSpec((1, n), lambda i: (0, 0))


class Vec(NamedTuple):
    array: jax.Array
    row: int
    col: int


def _vec_in(v, d):
    return pl.BlockSpec((None, 1, d), lambda i: (v.row, 0, v.col))


def modnorm_fwd(x, g, sc, sh, name):
    s, d = x.shape
    tm = _tile(s, 256, 8)

    def body(x_ref, g_ref, sc_ref, sh_ref, o_ref):
        xv = x_ref[...]
        r = lax.rsqrt(jnp.mean(xv * xv, axis=-1, keepdims=True) + EPS)
        o_ref[...] = ((xv * r) * g_ref[...] * (1.0 + sc_ref[...]) + sh_ref[...]).astype(BF16)

    return pl.pallas_call(
        body, name=name, out_shape=_sds((s, d), BF16), grid=(s // tm,),
        in_specs=[_rows(tm, d), _vec_in(g, d), _vec_in(sc, d), _vec_in(sh, d)], out_specs=_rows(tm, d),
        compiler_params=_params(("parallel",)),
    )(x, g.array, sc.array, sh.array)


def norm_bwd(x, dh, g, sc, dres, name):
    s, d = x.shape
    tm = _tile(s, 256, 8)
    nsteps = s // tm

    def body(x_ref, dh_ref, g_ref, sc_ref, dr_ref, dx_ref, dsh_ref, dsc_ref, dg_ref, a2_ref):
        i = pl.program_id(0)

        @pl.when(i == 0)
        def _():
            dsh_ref[...] = jnp.zeros_like(dsh_ref)
            a2_ref[...] = jnp.zeros_like(a2_ref)

        xv = x_ref[...]
        dh = dh_ref[...].astype(F32)
        r = lax.rsqrt(jnp.mean(xv * xv, axis=-1, keepdims=True) + EPS)
        xh = xv * r
        dsh_ref[...] += jnp.sum(dh, axis=0, keepdims=True)
        a2_ref[...] += jnp.sum(dh * xh, axis=0, keepdims=True)
        dxh = dh * (g_ref[...] * (1.0 + sc_ref[...]))
        dx = r * (dxh - xh * jnp.mean(dxh * xh, axis=-1, keepdims=True))
        dx_ref[...] = dr_ref[...] + dx

        @pl.when(i == nsteps - 1)
        def _():
            dsc_ref[...] = a2_ref[...] * g_ref[...]
            dg_ref[...] = a2_ref[...] * (1.0 + sc_ref[...])

    return pl.pallas_call(
        body, name=name, out_shape=(_sds((s, d)), _sds((1, d)), _sds((1, d)), _sds((1, d))), grid=(nsteps,),
        in_specs=[_rows(tm, d), _rows(tm, d), _vec_in(g, d), _vec_in(sc, d), _rows(tm, d)],
        out_specs=(_rows(tm, d), _vec(d), _vec(d), _vec(d)),
        scratch_shapes=[pltpu.VMEM((1, d), F32)],
        compiler_params=_params(("arbitrary",)),
    )(x, dh, g.array, sc.array, dres)


def resid_modnorm_fwd(x, y, gate, g, sc, sh, name):
    s, d = x.shape
    tm = _tile(s, 256, 8)

    def body(x_ref, y_ref, gate_ref, g_ref, sc_ref, sh_ref, xo_ref, h_ref):
        xv = x_ref[...] + gate_ref[...] * y_ref[...].astype(F32)
        xo_ref[...] = xv
        r = lax.rsqrt(jnp.mean(xv * xv, axis=-1, keepdims=True) + EPS)
        h_ref[...] = ((xv * r) * g_ref[...] * (1.0 + sc_ref[...]) + sh_ref[...]).astype(BF16)

    return pl.pallas_call(
        body, name=name, out_shape=(_sds((s, d)), _sds((s, d), BF16)), grid=(s // tm,),
        in_specs=[_rows(tm, d), _rows(tm, d), _vec_in(gate, d), _vec_in(g, d), _vec_in(sc, d), _vec_in(sh, d)],
        out_specs=(_rows(tm, d), _rows(tm, d)),
        compiler_params=_params(("parallel",)),
    )(x, y, gate.array, g.array, sc.array, sh.array)


def norm_gate_bwd(x, dh, g, sc, dres, y, gate, name):
    s, d = x.shape
    tm = _tile(s, 256, 8)
    nsteps = s // tm

    def body(x_ref, dh_ref, g_ref, sc_ref, dr_ref, y_ref, gate_ref, dx_ref, dsh_ref, dsc_ref, dg_ref, dy_ref,
             dgate_ref, a2_ref):
        i = pl.program_id(0)

        @pl.when(i == 0)
        def _():
            dsh_ref[...] = jnp.zeros_like(dsh_ref)
            a2_ref[...] = jnp.zeros_like(a2_ref)
            dgate_ref[...] = jnp.zeros_like(dgate_ref)

        xv = x_ref[...]
        dh = dh_ref[...].astype(F32)
        r = lax.rsqrt(jnp.mean(xv * xv, axis=-1, keepdims=True) + EPS)
        xh = xv * r
        dsh_ref[...] += jnp.sum(dh, axis=0, keepdims=True)
        a2_ref[...] += jnp.sum(dh * xh, axis=0, keepdims=True)
        dxh = dh * (g_ref[...] * (1.0 + sc_ref[...]))
        dr = dr_ref[...] + r * (dxh - xh * jnp.mean(dxh * xh, axis=-1, keepdims=True))
        dx_ref[...] = dr
        dy_ref[...] = (dr * gate_ref[...]).astype(BF16)
        dgate_ref[...] += jnp.sum(dr * y_ref[...].astype(F32), axis=0, keepdims=True)

        @pl.when(i == nsteps - 1)
        def _():
            dsc_ref[...] = a2_ref[...] * g_ref[...]
            dg_ref[...] = a2_ref[...] * (1.0 + sc_ref[...])

    vec = _sds((1, d))
    return pl.pallas_call(
        body, name=name, out_shape=(_sds((s, d)), vec, vec, vec, _sds((s, d), BF16), vec), grid=(nsteps,),
        in_specs=[_rows(tm, d), _rows(tm, d), _vec_in(g, d), _vec_in(sc, d), _rows(tm, d), _rows(tm, d), _vec_in(gate, d)],
        out_specs=(_rows(tm, d), _vec(d), _vec(d), _vec(d), _rows(tm, d), _vec(d)),
        scratch_shapes=[pltpu.VMEM((1, d), F32)],
        compiler_params=_params(("arbitrary",)),
    )(x, dh, g.array, sc.array, dres, y, gate.array)


def final_fused(x, f, gate, g, tgt):
    s, d = x.shape
    tm = _tile(s, 256, 8)

    def body(x_ref, f_ref, gate_ref, g_ref, t_ref, dx_ref, dg_ref, loss_ref, df_ref, dgate_ref):
        @pl.when(pl.program_id(0) == 0)
        def _():
            dg_ref[...] = jnp.zeros_like(dg_ref)
            loss_ref[...] = jnp.zeros_like(loss_ref)
            dgate_ref[...] = jnp.zeros_like(dgate_ref)

        fv, gatev, gv = f_ref[...].astype(F32), gate_ref[...], g_ref[...]
        xv = x_ref[...] + gatev * fv
        r = lax.rsqrt(jnp.mean(xv * xv, axis=-1, keepdims=True) + EPS)
        xh = xv * r
        e = xh * gv - t_ref[...]
        row = jnp.sum(e * e, axis=-1, keepdims=True) * (0.5 / d)
        loss_ref[...] += jnp.sum(row, axis=0, keepdims=True)
        dy = e * (1.0 / d)
        dg_ref[...] += jnp.sum(dy * xh, axis=0, keepdims=True)
        dxh = dy * gv
        dx = r * (dxh - xh * jnp.mean(dxh * xh, axis=-1, keepdims=True))
        dx_ref[...] = dx
        df_ref[...] = (dx * gatev).astype(BF16)
        dgate_ref[...] += jnp.sum(dx * fv, axis=0, keepdims=True)

    vec = _sds((1, d))
    return pl.pallas_call(
        body, name="final_fused", out_shape=(_sds((s, d)), vec, _sds((1, LANES)), _sds((s, d), BF16), vec),
        grid=(s // tm,),
        in_specs=[_rows(tm, d), _rows(tm, d), _vec_in(gate, d), _vec_in(g, d), _rows(tm, d)],
        out_specs=(_rows(tm, d), _vec(d), _vec(LANES), _rows(tm, d), _vec(d)),
        compiler_params=_params(("arbitrary",)),
    )(x, f, gate.array, g.array, tgt)


def _taps(v):
    return _shift_down(v, 2), _shift_down(v, 1), v


def _conv3_taps(taps, w):
    return w[0:1, :] * taps[0] + w[1:2, :] * taps[1] + w[2:3, :] * taps[2]


def _conv3(v, w):
    return _conv3_taps(_taps(v), w)


def _conv3_t(dv, w):
    return w[0:1, :] * _shift_up(dv, 2) + w[1:2, :] * _shift_up(dv, 1) + w[2:3, :] * dv


def _conv3_dw_taps(dv, taps):
    return jnp.concatenate([jnp.sum(dv * t, axis=0, keepdims=True) for t in taps], axis=0)


def _conv3_dw(dv, v):
    return _conv3_dw_taps(dv, _taps(v))


def gconv_fwd(z, conv_w):
    s = z.shape[0]
    nb = A_WIDTH // LANES

    def body(b_ref, c_ref, a_ref, w_ref, o_ref):
        b, c, a = b_ref[...].astype(F32), c_ref[...].astype(F32), a_ref[...].astype(F32)
        o_ref[...] = (b * _conv3(c * a, w_ref[...])).astype(BF16)

    col = lambda off: pl.BlockSpec((s, LANES), lambda j: (0, off + j))
    return pl.pallas_call(
        body, name="gconv_fwd", out_shape=_sds((s, A_WIDTH + _B_WIDTH), BF16), grid=(nb,),
        in_specs=[col(0), col(nb), col(2 * nb), pl.BlockSpec((3, LANES), lambda j: (0, j))],
        out_specs=pl.BlockSpec((s, LANES), lambda j: (0, j)),
        compiler_params=_params(("parallel",), VMEM_BIG),
    )(z, z, z, conv_w)


def gconv_bwd(z, conv_w, dycat):
    s = z.shape[0]
    nb = A_WIDTH // LANES

    def body(b_ref, c_ref, a_ref, w_ref, dy_ref, db_ref, dc_ref, da_ref, dw_ref):
        c, a, w, dy = c_ref[...].astype(F32), a_ref[...].astype(F32), w_ref[...], dy_ref[...].astype(F32)
        ca = c * a
        db_ref[...] = (dy * _conv3(ca, w)).astype(BF16)
        dconv = dy * b_ref[...].astype(F32)
        dw_ref[...] = _conv3_dw(dconv, ca)
        dca = _conv3_t(dconv, w)
        dc_ref[...] = (dca * a).astype(BF16)
        da_ref[...] = (dca * c).astype(BF16)

    col = lambda off: pl.BlockSpec((s, LANES), lambda j: (0, off + j))
    wspec = pl.BlockSpec((3, LANES), lambda j: (0, j))
    part = _sds((s, A_WIDTH), BF16)
    return pl.pallas_call(
        body, name="gconv_bwd", out_shape=(part, part, part, _sds((3, A_WIDTH))), grid=(nb,),
        in_specs=[col(0), col(nb), col(2 * nb), wspec, col(0)],
        out_specs=(col(0), col(0), col(0), wspec),
        compiler_params=_params(("parallel",), VMEM_BIG),
    )(z, z, z, conv_w, dycat)


def _pool_counts(s, w):
    t = lax.broadcasted_iota(jnp.int32, (s, 1), 0)
    return jnp.minimum(t + 1, w).astype(F32)


def _pooled(p, levels):
    acc = p
    for lv in range(levels):
        acc = acc + _shift_down(acc, 2 ** lv)
    return acc / _pool_counts(p.shape[0], 2 ** levels) - p


_B_WIDTH = B_GROUPS * LANES


def pool_fwd(z, mix_w, scale, ycat):
    s = z.shape[0]

    def body(p_ref, m_ref, sc_ref, ycat_ref, o_ref):
        del ycat_ref
        for g in range(B_GROUPS):
            cols = slice(g * LANES, (g + 1) * LANES)
            pooled = _pooled(p_ref[:, cols].astype(F32), g + 1)
            y = _dot(pooled.astype(BF16), m_ref[g].astype(BF16))
            o_ref[:, cols] = (y * sc_ref[:, cols]).astype(BF16)

    return pl.pallas_call(
        body, name="pool_fwd", out_shape=_sds(ycat.shape, BF16), grid=(1,),
        in_specs=[pl.BlockSpec((s, _B_WIDTH), lambda i: (0, 3 * A_WIDTH // _B_WIDTH)),
                  pl.BlockSpec((B_GROUPS, LANES, LANES), lambda i: (0, 0, 0)), pl.BlockSpec((1, _B_WIDTH), lambda i: (0, 0)),
                  pl.BlockSpec(memory_space=pl.ANY)],
        out_specs=pl.BlockSpec((s, _B_WIDTH), lambda i: (0, A_WIDTH // _B_WIDTH)),
        input_output_aliases={3: 0},
        compiler_params=_params(("arbitrary",), VMEM_BIG),
    )(z, mix_w, scale, ycat)


def pool_bwd(z, mix_w, scale, dycat):
    s = z.shape[0]

    def body(p_ref, m_ref, sc_ref, dy_ref, dp_ref, dm_ref, dsc_ref):
        for g in range(B_GROUPS):
            cols = slice(g * LANES, (g + 1) * LANES)
            pooled = _pooled(p_ref[:, cols].astype(F32), g + 1)
            mw = m_ref[g].astype(BF16)
            pb = pooled.astype(BF16)
            dy = dy_ref[:, cols].astype(F32)
            dsc_ref[:, cols] = jnp.sum(dy * _dot(pb, mw), axis=0, keepdims=True)
            dmix = (dy * sc_ref[:, cols]).astype(BF16)
            dm_ref[g] = _dot(pb, dmix, _TN)
            dpool = _dot(dmix, mw, _NT)
            acc = dpool / _pool_counts(s, 2 ** (g + 1))
            for lv in range(g + 1):
                acc = acc + _shift_up(acc, 2 ** lv)
            dp_ref[:, cols] = (acc - dpool).astype(BF16)

    wide = lambda c: pl.BlockSpec((s, _B_WIDTH), lambda i: (0, c))
    mspec = pl.BlockSpec((B_GROUPS, LANES, LANES), lambda i: (0, 0, 0))
    vspec = pl.BlockSpec((1, _B_WIDTH), lambda i: (0, 0))
    return pl.pallas_call(
        body, name="pool_bwd", out_shape=(_sds((s, _B_WIDTH), BF16), _sds((B_GROUPS, LANES, LANES)), _sds((1, _B_WIDTH))),
        grid=(1,), in_specs=[wide(3 * A_WIDTH // _B_WIDTH), mspec, vspec, wide(A_WIDTH // _B_WIDTH)],
        out_specs=(wide(0), mspec, vspec),
        compiler_params=_params(("arbitrary",), VMEM_BIG),
    )(z, mix_w, scale, dycat)


_FF_BLOCKS = D_FF // FF_UNIT


def _ff_spec(s):
    return pl.BlockSpec((2, s, FF_UNIT), lambda j: (0, 0, j))


def _ff_wspecs():
    return [pl.BlockSpec((3, FF_UNIT), lambda j: (0, j)), pl.BlockSpec((3, FF_UNIT), lambda j: (0, _FF_BLOCKS + j))]


_FF_ROWS = 64
_FF_HALO = 16


def _chunk_taps(z_ref, half, c):
    start = pl.multiple_of(c * _FF_ROWS, _FF_ROWS)
    before = pl.multiple_of(jnp.maximum(c * _FF_ROWS - _FF_HALO, 0), _FF_HALO)
    halo = z_ref[half, pl.ds(before, _FF_HALO), :].astype(F32)
    halo = jnp.where(c > 0, halo, 0.0)
    win = jnp.concatenate([halo, z_ref[half, pl.ds(start, _FF_ROWS), :].astype(F32)], axis=0)
    return tuple(pltpu.roll(win, k, axis=0)[_FF_HALO:] for k in (2, 1)) + (win[_FF_HALO:],)


def _fold8(v):
    acc = v[0:8]
    for r in range(8, v.shape[0], 8):
        acc = acc + v[r:r + 8]
    return acc


_FF_CHUNK = 256


def ffn_act_down(zf, conv_w, w_down, name):
    s, d = zf.shape[1], w_down.shape[1]
    nk = D_FF // _FF_CHUNK
    chunk = lambda k: jnp.minimum(k, nk - 1)

    def body(z_ref, wg_ref, wu_ref, wd_ref, a_ref, f_ref, held_ref, acc_ref):
        k = pl.program_id(0)

        @pl.when(k == 0)
        def _():
            held_ref[...] = jnp.zeros_like(held_ref)
            acc_ref[...] = jnp.zeros_like(acc_ref)

        acc_ref[...] += _dot(held_ref[(k + 1) % 2], wd_ref[...])
        g = _conv3(z_ref[0].astype(F32), wg_ref[...])
        u = _conv3(z_ref[1].astype(F32), wu_ref[...])
        act = (g * _sigmoid(g) * u).astype(BF16)
        a_ref[...] = act
        held_ref[k % 2] = act

        @pl.when(k == nk)
        def _():
            f_ref[...] = acc_ref[...].astype(BF16)

    return pl.pallas_call(
        body, name=name, out_shape=(_sds((s, D_FF), BF16), _sds((s, d), BF16)), grid=(nk + 1,),
        in_specs=[pl.BlockSpec((2, s, _FF_CHUNK), lambda k: (0, 0, chunk(k))),
                  pl.BlockSpec((3, _FF_CHUNK), lambda k: (0, chunk(k))),
                  pl.BlockSpec((3, _FF_CHUNK), lambda k: (0, nk + chunk(k))),
                  pl.BlockSpec((_FF_CHUNK, d), lambda k: (jnp.maximum(k - 1, 0), 0))],
        out_specs=(pl.BlockSpec((s, _FF_CHUNK), lambda k: (0, chunk(k))), pl.BlockSpec((s, d), lambda k: (0, 0))),
        scratch_shapes=[pltpu.VMEM((2, s, _FF_CHUNK), BF16), pltpu.VMEM((s, d), F32)],
        compiler_params=_params(("arbitrary",), VMEM_BIG),
    )(zf, conv_w, conv_w, w_down)


def ffn_act_bwd(zf, conv_w, da, name):
    s = zf.shape[1]
    assert s % _FF_ROWS == 0
    nchunks = s // _FF_ROWS

    def body(z_ref, wg_ref, wu_ref, da_ref, dz_ref, dw_ref, dg_ref, du_ref):
        wg, wu = wg_ref[...], wu_ref[...]

        def first(c, acc):
            rows = pl.ds(pl.multiple_of(c * _FF_ROWS, _FF_ROWS), _FF_ROWS)
            tg, tu = _chunk_taps(z_ref, 0, c), _chunk_taps(z_ref, 1, c)
            g = _conv3_taps(tg, wg)
            u = _conv3_taps(tu, wu)
            dav = da_ref[rows, :].astype(F32)
            sg = _sigmoid(g)
            dg = dav * u * (sg * (1.0 + g * (1.0 - sg)))
            du = dav * (g * sg)
            dg_ref[rows, :] = dg
            du_ref[rows, :] = du
            return tuple(a + _fold8(d * t) for a, (d, t) in zip(acc, [(dg, t) for t in tg] + [(du, t) for t in tu]))

        zero = jnp.zeros((8, FF_UNIT), F32)
        acc = lax.fori_loop(0, nchunks, first, (zero,) * 6)
        sums = [jnp.sum(a, axis=0, keepdims=True) for a in acc]
        dw_ref[0] = jnp.concatenate(sums[:3], axis=0)
        dw_ref[1] = jnp.concatenate(sums[3:], axis=0)

        tail = pl.ds(s, _FF_HALO)
        dg_ref[tail, :] = jnp.zeros((_FF_HALO, FF_UNIT), F32)
        du_ref[tail, :] = jnp.zeros((_FF_HALO, FF_UNIT), F32)
        span = _FF_ROWS + _FF_HALO

        def second(c, carry):
            start = pl.multiple_of(c * _FF_ROWS, _FF_ROWS)
            for half, (d_ref, w) in enumerate(((dg_ref, wg), (du_ref, wu))):
                win = d_ref[pl.ds(start, span), :]
                dz = (w[0:1, :] * pltpu.roll(win, span - 2, axis=0)[:_FF_ROWS]
                      + w[1:2, :] * pltpu.roll(win, span - 1, axis=0)[:_FF_ROWS] + w[2:3, :] * win[:_FF_ROWS])
                dz_ref[half, pl.ds(start, _FF_ROWS), :] = dz.astype(BF16)
            return carry

        lax.fori_loop(0, nchunks, second, 0)

    return pl.pallas_call(
        body, name=name, out_shape=(_sds((2, s, D_FF), BF16), _sds((2, 3, D_FF))), grid=(_FF_BLOCKS,),
        in_specs=[_ff_spec(s)] + _ff_wspecs() + [pl.BlockSpec((s, FF_UNIT), lambda j: (0, j))],
        out_specs=(_ff_spec(s), pl.BlockSpec((2, 3, FF_UNIT), lambda j: (0, 0, j))),
        scratch_shapes=[pltpu.VMEM((s + _FF_HALO, FF_UNIT), F32), pltpu.VMEM((s + _FF_HALO, FF_UNIT), F32)],
        compiler_params=_params(("parallel",), VMEM_BIG),
    )(zf, conv_w, conv_w, da)


def _rope(v, cs, s1, s2):
    return v * cs + pltpu.roll(v, LANES - C_ROPE // 2, axis=1) * s1 + pltpu.roll(v, C_ROPE // 2, axis=1) * s2


def _rope_t(dv, cs, s1, s2):
    return dv * cs + pltpu.roll(dv * s1, C_ROPE // 2, axis=1) + pltpu.roll(dv * s2, LANES - C_ROPE // 2, axis=1)


def _kpe_mask(shape):
    lane = lax.broadcasted_iota(jnp.int32, shape, 1)
    return (lane >= C_NOPE) & (lane < C_NOPE + C_ROPE)


def _rms(v, g):
    r = lax.rsqrt(jnp.mean(v * v, axis=-1, keepdims=True) + EPS)
    return v * r, r


def _rms_bwd(dn, xh, r, g):
    dxh = dn * g
    return r * (dxh - xh * jnp.mean(dxh * xh, axis=-1, keepdims=True)), jnp.sum(dn * xh, axis=0, keepdims=True)


_ZQ = C_Q_RANK + C_KV_RANK + HEAD_PAD
_HW = C_HEADS * HEAD_PAD


def mla_pre_fwd(z, gq, gkv, wq, wk, wv, cs, s1, s2):
    s = z.shape[0]
    tm = _tile(s, 256, 8)

    def body(z_ref, gq_ref, gkv_ref, wq_ref, wk_ref, wv_ref, cs_ref, s1_ref, s2_ref, q_ref, k_ref, v_ref):
        zv = z_ref[...].astype(F32)
        cst, s1t, s2t = cs_ref[...], s1_ref[...], s2_ref[...]
        qh, _ = _rms(zv[:, :C_Q_RANK], None)
        qn = (qh * gq_ref[...]).astype(BF16)
        q = _dot(qn, wq_ref[...])
        kh, _ = _rms(zv[:, C_Q_RANK:C_Q_RANK + C_KV_RANK], None)
        kvn = (kh * gkv_ref[...]).astype(BF16)
        k = _dot(kvn, wk_ref[...])
        v_ref[...] = _dot(kvn, wv_ref[...]).astype(BF16)
        kpe = _rope(zv[:, C_Q_RANK + C_KV_RANK:], cst, s1t, s2t)
        for h in range(C_HEADS):
            sl = slice(h * HEAD_PAD, (h + 1) * HEAD_PAD)
            q_ref[:, sl] = _rope(q[:, sl], cst, s1t, s2t).astype(BF16)
            k_ref[:, sl] = (k[:, sl] + kpe).astype(BF16)

    full = lambda r, c: pl.BlockSpec((r, c), lambda i: (0, 0))
    hw = _sds((s, _HW), BF16)
    return pl.pallas_call(
        body, name="mla_pre_fwd", out_shape=(hw, hw, hw), grid=(s // tm,),
        in_specs=[_rows(tm, _ZQ), _vec(C_Q_RANK), _vec(C_KV_RANK), full(C_Q_RANK, _HW), full(C_KV_RANK, _HW),
                  full(C_KV_RANK, _HW), _rows(tm, LANES), _rows(tm, LANES), _rows(tm, LANES)],
        out_specs=(_rows(tm, _HW), _rows(tm, _HW), _rows(tm, _HW)),
        compiler_params=_params(("parallel",), VMEM_BIG),
    )(z, gq, gkv, wq, wk, wv, cs, s1, s2)


def mla_pre_bwd(z, gq, gkv, wq, wk, wv, cs, s1, s2, dq, dk, dv):
    s = z.shape[0]
    tm = _tile(s, 256, 8)

    def body(z_ref, gq_ref, gkv_ref, wq_ref, wk_ref, wv_ref, cs_ref, s1_ref, s2_ref, dq_ref, dk_ref, dv_ref,
             dz_ref, dwq_ref, dwk_ref, dwv_ref, dgq_ref, dgkv_ref):
        @pl.when(pl.program_id(0) == 0)
        def _():
            dwq_ref[...] = jnp.zeros_like(dwq_ref)
            dwk_ref[...] = jnp.zeros_like(dwk_ref)
            dwv_ref[...] = jnp.zeros_like(dwv_ref)
            dgq_ref[...] = jnp.zeros_like(dgq_ref)
            dgkv_ref[...] = jnp.zeros_like(dgkv_ref)

        zv = z_ref[...].astype(F32)
        cst, s1t, s2t = cs_ref[...], s1_ref[...], s2_ref[...]
        gqv, gkvv = gq_ref[...], gkv_ref[...]
        qh, rq = _rms(zv[:, :C_Q_RANK], None)
        qn = (qh * gqv).astype(BF16)
        kh, rk = _rms(zv[:, C_Q_RANK:C_Q_RANK + C_KV_RANK], None)
        kvn = (kh * gkvv).astype(BF16)

        dqv = dq_ref[...].astype(F32)
        dqp = jnp.concatenate(
            [_rope_t(dqv[:, h * HEAD_PAD:(h + 1) * HEAD_PAD], cst, s1t, s2t) for h in range(C_HEADS)], axis=1
        ).astype(BF16)
        dwq_ref[...] += _dot(qn, dqp, _TN)
        dqn = _dot(dqp, wq_ref[...], _NT)
        dql, dgq = _rms_bwd(dqn, qh, rq, gqv)
        dgq_ref[...] += dgq

        dkv = dk_ref[...]
        dkb = dkv.astype(BF16)
        dvb = dv_ref[...].astype(BF16)
        dwk_ref[...] += _dot(kvn, dkb, _TN)
        dwv_ref[...] += _dot(kvn, dvb, _TN)
        dkvn = _dot(dkb, wk_ref[...], _NT) + _dot(dvb, wv_ref[...], _NT)
        dkl, dgkv = _rms_bwd(dkvn, kh, rk, gkvv)
        dgkv_ref[...] += dgkv

        dkpe = dkv[:, :HEAD_PAD]
        for h in range(1, C_HEADS):
            dkpe = dkpe + dkv[:, h * HEAD_PAD:(h + 1) * HEAD_PAD]
        dkpe = _rope_t(jnp.where(_kpe_mask(dkpe.shape), dkpe, 0.0), cst, s1t, s2t)
        dz_ref[...] = jnp.concatenate([dql, dkl, dkpe], axis=1).astype(BF16)

    full = lambda r, c: pl.BlockSpec((r, c), lambda i: (0, 0))
    return pl.pallas_call(
        body, name="mla_pre_bwd",
        out_shape=(_sds((s, _ZQ), BF16), _sds((C_Q_RANK, _HW)), _sds((C_KV_RANK, _HW)), _sds((C_KV_RANK, _HW)),
                   _sds((1, C_Q_RANK)), _sds((1, C_KV_RANK))),
        grid=(s // tm,),
        in_specs=[_rows(tm, _ZQ), _vec(C_Q_RANK), _vec(C_KV_RANK), full(C_Q_RANK, _HW), full(C_KV_RANK, _HW),
                  full(C_KV_RANK, _HW), _rows(tm, LANES), _rows(tm, LANES), _rows(tm, LANES),
                  _rows(tm, _HW), _rows(tm, _HW), _rows(tm, _HW)],
        out_specs=(_rows(tm, _ZQ), full(C_Q_RANK, _HW), full(C_KV_RANK, _HW), full(C_KV_RANK, _HW),
                   _vec(C_Q_RANK), _vec(C_KV_RANK)),
        compiler_params=_params(("arbitrary",), VMEM_BIG),
    )(z, gq, gkv, wq, wk, wv, cs, s1, s2, dq, dk, dv)


_ATT_SCALE = (C_NOPE + C_ROPE) ** -0.5
_NEG = -1e30


def _att_exp(q, k, row0, ends_here):
    sc = _dot(q, k, _NT) * _ATT_SCALE
    tq, nk = sc.shape
    if ends_here:
        last = sc[:, nk - tq:]
        row = lax.broadcasted_iota(jnp.int32, last.shape, 0)
        col = lax.broadcasted_iota(jnp.int32, last.shape, 1)
        last = jnp.where(col <= row, last, _NEG)
        sc = last if nk == tq else jnp.concatenate([sc[:, :nk - tq], last], axis=1)
    else:
        qpos = row0 + lax.broadcasted_iota(jnp.int32, sc.shape, 0)
        kpos = lax.broadcasted_iota(jnp.int32, sc.shape, 1)
        sc = jnp.where(kpos <= qpos, sc, _NEG)
    e = jnp.exp(sc - jnp.max(sc, axis=-1, keepdims=True))
    return e, 1.0 / jnp.sum(e, axis=-1, keepdims=True)


def _causal_cases(i, nq, tq, fn):
    if nq > 8:
        fn(nq * tq, False)
        return
    for blk in range(nq):
        pl.when(i == blk)(functools.partial(fn, (blk + 1) * tq, True))


def attn_fwd(q, k, v):
    s = q.shape[0]
    tq = _tile(s, 256, 8)
    nq = s // tq

    def body(q_ref, k_ref, v_ref, o_ref):
        i = pl.program_id(1)

        def case(nk, ends_here):
            e, inv = _att_exp(q_ref[...], k_ref[:nk, :], i * tq, ends_here)
            o_ref[...] = (_dot(e.astype(BF16), v_ref[:nk, :]) * inv).astype(BF16)

        _causal_cases(i, nq, tq, case)

    qspec = pl.BlockSpec((tq, HEAD_PAD), lambda h, i: (i, h))
    kspec = pl.BlockSpec((s, HEAD_PAD), lambda h, i: (0, h))
    return pl.pallas_call(
        body, name="attn_fwd", out_shape=_sds((s, _HW + _DW), BF16), grid=(C_HEADS, s // tq),
        in_specs=[qspec, kspec, kspec], out_specs=qspec,
        compiler_params=_params(("parallel", "parallel"), VMEM_BIG),
    )(q, k, v)


def attn_bwd(q, k, v, o, do_all, do_col0):
    s = q.shape[0]
    tq = _tile(s, 256, 8)

    def body(q_ref, k_ref, v_ref, o_ref, do_ref, dq_ref, dk_ref, dv_ref):
        i = pl.program_id(1)

        @pl.when(i == 0)
        def _():
            dk_ref[...] = jnp.zeros_like(dk_ref)
            dv_ref[...] = jnp.zeros_like(dv_ref)

        def case(nk, ends_here):
            qv, kv, vv, dov = q_ref[...], k_ref[:nk, :], v_ref[:nk, :], do_ref[...]
            e, inv = _att_exp(qv, kv, i * tq, ends_here)
            p = e * inv
            dp = _dot(dov, vv, _NT)
            delta = jnp.sum(dov.astype(F32) * o_ref[...].astype(F32), axis=-1, keepdims=True)
            ds = (p * (dp - delta) * _ATT_SCALE).astype(BF16)
            dq_ref[...] = _dot(ds, kv).astype(BF16)
            dk_ref[:nk, :] += _dot(ds, qv, _TN)
            dv_ref[:nk, :] += _dot(p.astype(BF16), dov, _TN)

        _causal_cases(i, s // tq, tq, case)

    qspec = pl.BlockSpec((tq, HEAD_PAD), lambda h, i: (i, h))
    dospec = pl.BlockSpec((tq, HEAD_PAD), lambda h, i: (i, do_col0 + h))
    kspec = pl.BlockSpec((s, HEAD_PAD), lambda h, i: (0, h))
    return pl.pallas_call(
        body, name="attn_bwd", out_shape=(_sds((s, _HW), BF16), _sds((s, _HW)), _sds((s, _HW))),
        grid=(C_HEADS, s // tq),
        in_specs=[qspec, kspec, kspec, qspec, dospec], out_specs=(qspec, kspec, kspec),
        compiler_params=_params(("parallel", "arbitrary"), VMEM_BIG),
    )(q, k, v, o, do_all)


_DW = D_GROUPS * LANES


def _tril_bf16(w):
    r = lax.broadcasted_iota(jnp.int32, w.shape, 0)
    c = lax.broadcasted_iota(jnp.int32, w.shape, 1)
    return jnp.where(c <= r, w, 0.0).astype(BF16)


def _sgu_forward(zu, zv, lg, lb, ws_ref, bs):
    u = _gelu(zu)
    v = _gelu(zv)
    mu = jnp.mean(v, axis=-1, keepdims=True)
    vc = v - mu
    rstd = lax.rsqrt(jnp.mean(vc * vc, axis=-1, keepdims=True) + EPS)
    xh = vc * rstd
    vln = (xh * lg + lb).astype(BF16)
    mixed = []
    for g in range(D_GROUPS):
        wg = _tril_bf16(ws_ref[g])
        mixed.append(_dot(wg, vln[:, g * LANES:(g + 1) * LANES]) + bs[:, g:g + 1])
    return u, xh, rstd, vln, jnp.concatenate(mixed, axis=1)


def sgu_fwd(z, lg, lb, ws, bs_t, ycat):
    s = z.shape[0]
    nchunk = s // D_CHUNK

    def body(zu_ref, zv_ref, lg_ref, lb_ref, ws_ref, bs_ref, ycat_ref, o_ref):
        del ycat_ref
        u, _, _, _, mixed = _sgu_forward(zu_ref[...].astype(F32), zv_ref[...].astype(F32), lg_ref[...], lb_ref[...],
                                         ws_ref, bs_ref[...])
        o_ref[...] = (u * mixed).astype(BF16)

    return pl.pallas_call(
        body, name="sgu_fwd", out_shape=_sds(ycat.shape, BF16), grid=(nchunk,),
        in_specs=[pl.BlockSpec((D_CHUNK, _DW), lambda n: (n, 1)), pl.BlockSpec((D_CHUNK, _DW), lambda n: (n, 2)),
                  _vec(_DW), _vec(_DW), pl.BlockSpec((D_GROUPS, D_CHUNK, D_CHUNK), lambda n: (0, 0, 0)),
                  pl.BlockSpec((D_CHUNK, LANES), lambda n: (0, 0)), pl.BlockSpec(memory_space=pl.ANY)],
        out_specs=pl.BlockSpec((D_CHUNK, _DW), lambda n: (n, _HW // _DW)),
        input_output_aliases={6: 0},
        compiler_params=_params(("parallel",)),
    )(z, z, lg, lb, ws, bs_t, ycat)


def sgu_bwd(z, lg, lb, ws, bs_t, dycat, dy_col):
    s = z.shape[0]
    nchunk = s // D_CHUNK

    def body(zu_ref, zv_ref, lg_ref, lb_ref, ws_ref, bs_ref, dy_ref, dzu_ref, dzv_ref, dws_ref, dbs_ref, dlg_ref,
             dlb_ref):
        @pl.when(pl.program_id(0) == 0)
        def _():
            dws_ref[...] = jnp.zeros_like(dws_ref)
            dbs_ref[...] = jnp.zeros_like(dbs_ref)
            dlg_ref[...] = jnp.zeros_like(dlg_ref)
            dlb_ref[...] = jnp.zeros_like(dlb_ref)

        zu, zv, lg = zu_ref[...].astype(F32), zv_ref[...].astype(F32), lg_ref[...]
        u, xh, rstd, vln, mixed = _sgu_forward(zu, zv, lg, lb_ref[...], ws_ref, bs_ref[...])
        dy = dy_ref[...].astype(F32)
        dzu_ref[...] = (dy * mixed * _gelu_grad(zu)).astype(BF16)
        dmix = dy * u
        lane = lax.broadcasted_iota(jnp.int32, (D_CHUNK, LANES), 1)
        row = lax.broadcasted_iota(jnp.int32, (D_CHUNK, D_CHUNK), 0)
        colm = lax.broadcasted_iota(jnp.int32, (D_CHUNK, D_CHUNK), 1)
        dvln = []
        dbs = jnp.zeros((D_CHUNK, LANES), F32)
        for g in range(D_GROUPS):
            sl = slice(g * LANES, (g + 1) * LANES)
            dmg = dmix[:, sl]
            dbs = dbs + jnp.where(lane == g, jnp.sum(dmg, axis=-1, keepdims=True), 0.0)
            dmb = dmg.astype(BF16)
            dws_ref[g] += jnp.where(colm <= row, _dot(dmb, vln[:, sl], _NT), 0.0)
            dvln.append(_dot(_tril_bf16(ws_ref[g]), dmb, _TN))
        dbs_ref[...] += dbs
        dvln = jnp.concatenate(dvln, axis=1)
        dlg_ref[...] += jnp.sum(dvln * xh, axis=0, keepdims=True)
        dlb_ref[...] += jnp.sum(dvln, axis=0, keepdims=True)
        dxh = dvln * lg
        dvv = rstd * (dxh - jnp.mean(dxh, axis=-1, keepdims=True) - xh * jnp.mean(dxh * xh, axis=-1, keepdims=True))
        dzv_ref[...] = (dvv * _gelu_grad(zv)).astype(BF16)

    wsspec = pl.BlockSpec((D_GROUPS, D_CHUNK, D_CHUNK), lambda n: (0, 0, 0))
    chunk = lambda cidx: pl.BlockSpec((D_CHUNK, _DW), lambda n: (n, cidx))
    return pl.pallas_call(
        body, name="sgu_bwd",
        out_shape=(_sds((s, _DW), BF16), _sds((s, _DW), BF16), _sds((D_GROUPS, D_CHUNK, D_CHUNK)),
                   _sds((D_CHUNK, LANES)), _sds((1, _DW)), _sds((1, _DW))),
        grid=(nchunk,),
        in_specs=[chunk(1), chunk(2), _vec(_DW), _vec(_DW), wsspec, pl.BlockSpec((D_CHUNK, LANES), lambda n: (0, 0)),
                  chunk(dy_col)],
        out_specs=(chunk(0), chunk(0), wsspec, pl.BlockSpec((D_CHUNK, LANES), lambda n: (0, 0)), _vec(_DW), _vec(_DW)),
        compiler_params=_params(("arbitrary",)),
    )(z, z, lg, lb, ws, bs_t, dycat)


def ada_mod(c_all, ada_w, ada_b):
    nl, d, n = ada_w.shape
    nb = c_all.shape[0]
    tn = _tile(n, 512)

    def body(c_ref, w_ref, b_ref, o_ref):
        cv = c_ref[...]
        ca = (cv * _sigmoid(cv)).astype(BF16)
        o_ref[...] = _dot(ca, w_ref[...].astype(BF16)) + b_ref[...]

    return pl.pallas_call(
        body, name="ada_mod", out_shape=_sds((nl, nb, n)), grid=(nl, n // tn),
        in_specs=[pl.BlockSpec((nb, d), lambda l, j: (0, 0)), pl.BlockSpec((None, d, tn), lambda l, j: (l, 0, j)),
                  pl.BlockSpec((None, 1, tn), lambda l, j: (l, 0, j))],
        out_specs=pl.BlockSpec((None, nb, tn), lambda l, j: (l, 0, j)),
        compiler_params=_params(("parallel", "parallel")),
    )(c_all, ada_w, ada_b.reshape(nl, 1, n))


_ADAM_BLOCK = 256 * 1024


def _adam_rows(rows, cols):
    if rows * cols <= _ADAM_BLOCK or rows % 8:
        return rows
    return _tile(rows, max(8, _ADAM_BLOCK // cols), 8)


def _adam_update(w, gv, m, v):
    inv_bc1 = 1.0 / (1.0 - ADAM_B1 ** ADAM_STEP)
    inv_bc2 = 1.0 / (1.0 - ADAM_B2 ** ADAM_STEP)
    nm = ADAM_B1 * m + (1.0 - ADAM_B1) * gv
    nv = ADAM_B2 * v + (1.0 - ADAM_B2) * (gv * gv)
    return -ADAM_LR * ((nm * inv_bc1) / (jnp.sqrt(nv * inv_bc2) + ADAM_EPS) + ADAM_WD * w), nm, nv


def adamw(w, g, m, v, name):
    shape = w.shape
    cols = shape[-1]
    rows = w.size // cols
    tr = _adam_rows(rows, cols)

    def body(w_ref, g_ref, m_ref, v_ref, go_ref, d_ref, nm_ref, nv_ref):
        gv = g_ref[...]
        go_ref[...] = gv
        d_ref[...], nm_ref[...], nv_ref[...] = _adam_update(w_ref[...], gv, m_ref[...], v_ref[...])

    spec = pl.BlockSpec((tr, cols), lambda i: (i, 0))
    out = _sds((rows, cols))
    r2 = lambda t: t.reshape(rows, cols)
    res = pl.pallas_call(
        body, name=name, out_shape=(out,) * 4, grid=(rows // tr,),
        in_specs=[spec] * 4, out_specs=(spec,) * 4, compiler_params=_params(("parallel",)),
    )(r2(w), r2(g), r2(m), r2(v))
    return tuple(t.reshape(shape) for t in res)


def adamw_ada(w, c_all, dmod, m, v):
    nl, d, n = w.shape
    tr = _adam_rows(d, n)
    pad = 16 - c_all.shape[0]
    c16 = jnp.pad(c_all, ((0, pad), (0, 0)))
    dm16 = jnp.pad(dmod, ((0, 0), (0, pad), (0, 0)))

    def body(w_ref, c_ref, dm_ref, m_ref, v_ref, g_ref, d_ref, nm_ref, nv_ref):
        cv = c_ref[...]
        gv = _dot((cv * _sigmoid(cv)).astype(BF16), dm_ref[...].astype(BF16), _TN)
        g_ref[...] = gv
        d_ref[...], nm_ref[...], nv_ref[...] = _adam_update(w_ref[...], gv, m_ref[...], v_ref[...])

    spec = pl.BlockSpec((None, tr, n), lambda l, i: (l, i, 0))
    out = _sds((nl, d, n))
    return pl.pallas_call(
        body, name="adamw_ada_w", out_shape=(out, out, out, out), grid=(nl, d // tr),
        in_specs=[spec, pl.BlockSpec((16, tr), lambda l, i: (0, i)), pl.BlockSpec((None, 16, n), lambda l, i: (l, 0, 0)),
                  spec, spec],
        out_specs=(spec,) * 4, compiler_params=_params(("parallel", "parallel")),
    )(w, c16, dm16, m, v)


def adamw_small(ws, gs, ms, vs):
    n = len(ws)
    flat = lambda t: t.reshape(-1, t.shape[-1])

    def body(*refs):
        ins, outs = refs[:4 * n], refs[4 * n:]
        for i in range(n):
            w_ref, g_ref, m_ref, v_ref = ins[4 * i:4 * i + 4]
            outs[3 * i][...], outs[3 * i + 1][...], outs[3 * i + 2][...] = _adam_update(
                w_ref[...], g_ref[...], m_ref[...], v_ref[...])

    operands = [flat(t) for quad in zip(ws, gs, ms, vs) for t in quad]
    res = pl.pallas_call(
        body, name="adamw_small", out_shape=tuple(_sds(flat(w).shape) for w in ws for _ in range(3)),
    )(*operands)
    return [(g, res[3 * i].reshape(w.shape), res[3 * i + 1].reshape(w.shape), res[3 * i + 2].reshape(w.shape))
            for i, (w, g) in enumerate(zip(ws, gs))]


def adamw_layers(w, g0, g1, m, v, name):
    _, rows, cols = w.shape
    tr = _adam_rows(rows, cols)

    def body(w_ref, g0_ref, g1_ref, m_ref, v_ref, g_ref, d_ref, nm_ref, nv_ref):
        gv = jnp.where(pl.program_id(0) == 0, g0_ref[...], g1_ref[...])
        g_ref[...] = gv
        d_ref[...], nm_ref[...], nv_ref[...] = _adam_update(w_ref[...], gv, m_ref[...], v_ref[...])

    spec = pl.BlockSpec((None, tr, cols), lambda l, i: (l, i, 0))
    gspec = pl.BlockSpec((tr, cols), lambda l, i: (i, 0))
    out = _sds((2, rows, cols))
    return pl.pallas_call(
        body, name=name, out_shape=(out, out, out, out), grid=(2, rows // tr),
        in_specs=[spec, gspec, gspec, spec, spec], out_specs=(spec,) * 4, compiler_params=_params(("parallel", "parallel")),
    )(w, g0, g1, m, v)


def sum8(gathered):
    _, r, _ = gathered.shape
    tr = _tile(r, 512, 8)

    def body(g_ref, o_ref):
        acc = g_ref[0]
        for dev in range(1, N_DEV):
            acc = acc + g_ref[dev]
        o_ref[...] = acc

    return pl.pallas_call(
        body, name="sum8", out_shape=_sds((r, LANES)), grid=(r // tr,),
        in_specs=[pl.BlockSpec((N_DEV, tr, LANES), lambda i: (0, i, 0))], out_specs=pl.BlockSpec((tr, LANES), lambda i: (i, 0)),
        compiler_params=_params(("parallel",)),
    )(gathered)


_SUM_STEPS = 2


def pair_sums(gs, recvs, core, name):
    n = len(gs)
    trs = [g.shape[1] // 2 // _SUM_STEPS for g in gs]

    def body(c_ref, *refs):
        del c_ref
        for i in range(n):
            a_ref, b_ref, o_ref = refs[2 * i], refs[2 * i + 1], refs[2 * n + i]
            o_ref[...] = (a_ref[...].astype(F32) + b_ref[...].astype(F32)).astype(BF16)

    in_specs, out_specs = [], []
    for g, tr in zip(gs, trs):
        cols = g.shape[2]
        in_specs.append(pl.BlockSpec((None, tr, cols), lambda k, s, c: (k, c[0] * _SUM_STEPS + s, 0)))
        in_specs.append(pl.BlockSpec((None, tr, cols), lambda k, s, c: (k, s, 0)))
        out_specs.append(pl.BlockSpec((None, tr, cols), lambda k, s, c: (k, s, 0)))
    grid_spec = pltpu.PrefetchScalarGridSpec(num_scalar_prefetch=1, grid=(N_CHIPS, _SUM_STEPS), in_specs=in_specs,
                                             out_specs=tuple(out_specs))
    return list(pl.pallas_call(
        body, name=name, out_shape=tuple(_sds((N_CHIPS, g.shape[1] // 2, g.shape[2]), BF16) for g in gs),
        grid_spec=grid_spec, compiler_params=_params(("parallel", "parallel")),
    )(core.reshape(1).astype(jnp.int32), *[t for pair in zip(gs, recvs) for t in pair]))


def chip_sums(pairs, recvs, chip, core, name):
    n = len(pairs)
    trs = [p.shape[1] // _SUM_STEPS for p in pairs]

    def body(p_ref, *refs):
        del p_ref
        for i in range(n):
            own_ref, r_ref, o_ref = refs[2 * i], refs[2 * i + 1], refs[2 * n + i]
            acc = own_ref[...].astype(F32)
            for j in range(N_CHIPS - 1):
                acc = acc + r_ref[j].astype(F32)
            o_ref[...] = acc

    in_specs, out_specs = [], []
    for p, tr in zip(pairs, trs):
        cols = p.shape[2]
        in_specs.append(pl.BlockSpec((None, tr, cols), lambda s, q: (q[0], s, 0)))
        in_specs.append(pl.BlockSpec((N_CHIPS - 1, tr, cols), lambda s, q: (0, s, 0)))
        out_specs.append(pl.BlockSpec((None, tr, cols), lambda s, q: (q[1], s, 0)))
    grid_spec = pltpu.PrefetchScalarGridSpec(num_scalar_prefetch=1, grid=(_SUM_STEPS,), in_specs=in_specs,
                                             out_specs=tuple(out_specs))
    return list(pl.pallas_call(
        body, name=name, out_shape=tuple(_sds((2,) + p.shape[1:]) for p in pairs), grid_spec=grid_spec,
        compiler_params=_params(("parallel",)),
    )(jnp.stack([chip, core]).astype(jnp.int32), *[t for pair in zip(pairs, recvs) for t in pair]))


def _place():
    return lax.axis_index("x"), lax.axis_index("y"), lax.axis_index("c")


def _other_chips(x, y):
    return [(x, 1 - y), (1 - x, y), (1 - x, 1 - y)]


_HBM = pl.BlockSpec(memory_space=pltpu.HBM)


def all_gather8(v, name, after=()):
    m, n = v.shape

    def body(x_ref, *refs):
        out_ref, send_sems, recv_sems, local_sem = refs[len(after):]
        x, y, c = _place()
        me, sibling = (x, y, c), (x, y, 1 - c)
        chips = _other_chips(x, y)

        def rows(px, py, pc):
            return out_ref.at[pl.ds((4 * px + 2 * py + pc) * m, m), :]

        def copy(k, block, to, src=None):
            return pltpu.make_async_remote_copy(
                src_ref=rows(*block) if src is None else src, dst_ref=rows(*block),
                send_sem=send_sems.at[k], recv_sem=recv_sems.at[k], device_id=to, device_id_type=MESH)

        mine = pltpu.make_async_copy(x_ref, rows(*me), local_sem)
        mine.start()
        first = [copy(0, me, sibling, src=x_ref)]
        first += [copy(1 + j, me, (*chip, c), src=x_ref) for j, chip in enumerate(chips)]
        for cp in first:
            cp.start()
        passed = [copy(4 + j, (*chip, c), sibling) for j, chip in enumerate(chips)]
        for j, chip in enumerate(chips):
            copy(1 + j, (*chip, c), me).wait_recv()
            passed[j].start()
        copy(0, sibling, me).wait_recv()
        for j, chip in enumerate(chips):
            copy(4 + j, (*chip, 1 - c), me).wait_recv()
        for cp in first + passed:
            cp.wait_send()
        mine.wait()

    return pl.pallas_call(
        body, name=name, out_shape=_sds((N_DEV * m, n), v.dtype),
        in_specs=[pl.BlockSpec(memory_space=pltpu.VMEM)] + [pl.BlockSpec(memory_space=pl.ANY)] * len(after),
        out_specs=pl.BlockSpec(memory_space=pltpu.VMEM),
        scratch_shapes=[pltpu.SemaphoreType.DMA((7,)), pltpu.SemaphoreType.DMA((7,)), pltpu.SemaphoreType.DMA],
        compiler_params=_params(None, VMEM_BIG),
    )(v, *after)


def _comm_call(body, name, ins, out_shapes, nsem, aliases=None):
    return pl.pallas_call(
        body, name=name, out_shape=tuple(out_shapes), in_specs=[_HBM] * len(ins), out_specs=tuple([_HBM] * len(out_shapes)),
        scratch_shapes=[pltpu.SemaphoreType.DMA((nsem,)), pltpu.SemaphoreType.DMA((nsem,))],
        input_output_aliases=aliases or {},
    )(*ins)


def _remote(src, dst, send_sems, recv_sems, k, to):
    return pltpu.make_async_remote_copy(src_ref=src, dst_ref=dst, send_sem=send_sems.at[k], recv_sem=recv_sems.at[k],
                                        device_id=to, device_id_type=MESH)


def _half(core, rh):
    return pl.ds(pl.multiple_of(core * rh, 16), rh)


def swap_halves(gs, name):
    n = len(gs)

    def body(*refs):
        ins, outs, (send_sems, recv_sems) = refs[:n], refs[n:2 * n], refs[2 * n:]
        x, y, c = _place()
        copies = []
        for i in range(n):
            theirs = _half(1 - c, ins[i].shape[1] // 2)
            cp = _remote(ins[i].at[:, theirs], outs[i], send_sems, recv_sems, i, (x, y, 1 - c))
            cp.start()
            copies.append(cp)
        for cp in copies:
            cp.wait()

    return _comm_call(body, name, gs, [_sds((g.shape[0], g.shape[1] // 2, g.shape[2]), g.dtype) for g in gs], n)


def join_halves(bufs, name):
    n = len(bufs)

    def body(*refs):
        ins, outs, (send_sems, recv_sems) = refs[:n], refs[n:2 * n], refs[2 * n:]
        x, y, c = _place()
        copies = []
        for i in range(n):
            cp = _remote(ins[i].at[c], outs[i].at[c], send_sems, recv_sems, i, (x, y, 1 - c))
            cp.start()
            copies.append(cp)
        for i in range(n):
            theirs = outs[i].at[1 - c]
            _remote(theirs, theirs, send_sems, recv_sems, i, (x, y, 1 - c)).wait_recv()
        for cp in copies:
            cp.wait_send()

    return _comm_call(body, name, bufs, [_sds(b.shape, b.dtype) for b in bufs], n, {i: i for i in range(n)})


def forward_halves(lands, name):
    n = len(lands)

    def body(*refs):
        ins, outs, (send_sems, recv_sems) = refs[:n], refs[n:2 * n], refs[2 * n:]
        x, y, c = _place()
        sibling = (x, y, 1 - c)
        chips = _other_chips(x, y)
        copies = []
        for i in range(n):
            mine = _half(c, ins[i].shape[1] // 2)
            for j, (px, py) in enumerate(chips):
                cp = _remote(ins[i].at[2 * px + py, mine], outs[i].at[2 * px + py, mine], send_sems, recv_sems, 3 * i + j, sibling)
                cp.start()
                copies.append(cp)
        for i in range(n):
            theirs = _half(1 - c, ins[i].shape[1] // 2)
            for j, (px, py) in enumerate(chips):
                landed = outs[i].at[2 * px + py, theirs]
                _remote(landed, landed, send_sems, recv_sems, 3 * i + j, sibling).wait_recv()
        for cp in copies:
            cp.wait_send()

    return _comm_call(body, name, lands, [_sds(b.shape, b.dtype) for b in lands], 3 * n, {i: i for i in range(n)})


_SEM = pl.BlockSpec(memory_space=pltpu.SEMAPHORE)
_EFFECT = pltpu.SideEffectType.DATAFLOW_SIDE_EFFECTING


def _gather_copies(srcs, lands, send_sems, recv_sems):
    x, y, c = _place()
    copies = []
    for i in range(len(srcs)):
        mine = _half(c, srcs[i].shape[0] // 2)
        for j, chip in enumerate(_other_chips(x, y)):
            copies.append(_remote(srcs[i].at[mine], lands[i].at[2 * x + y, mine], send_sems, recv_sems, 3 * i + j, (*chip, c)))
    return copies


def _exchange_copies(srcs, lands, send_sems, recv_sems):
    x, y, c = _place()
    copies = []
    for i in range(len(srcs)):
        for j, (px, py) in enumerate(_other_chips(x, y)):
            copies.append(_remote(srcs[i].at[2 * px + py], lands[i].at[j], send_sems, recv_sems, 3 * i + j, (px, py, c)))
    return copies


def _everyone_copies(srcs, lands, send_sems, recv_sems):
    x, y, c = _place()
    flip = lambda v, b: 1 - v if b else v
    dst = lands[0].at[4 * x + 2 * y + c]
    return [_remote(srcs[0], dst, send_sems, recv_sems, j - 1, (flip(x, j & 4), flip(y, j & 2), flip(c, j & 1)))
            for j in range(1, N_DEV)]


GATHER = (_gather_copies, 3)
EXCHANGE = (_exchange_copies, 3)
EVERYONE = (_everyone_copies, N_DEV - 1)


def split_start(name, plan, srcs, land_shapes, after=()):
    copies_fn, per_source = plan
    n, m, k = len(srcs), len(land_shapes), len(after)
    ncopies = per_source * n

    def body(*refs):
        src_refs, land_refs = refs[:n], refs[n:n + m]
        send_sems, recv_sems = refs[n + m + k], refs[n + m + k + 1]
        token = refs[-1]
        for cp in copies_fn(src_refs, land_refs, send_sems, recv_sems):
            cp.start()
        token[...] = jnp.zeros_like(token)

    hbm = lambda s: pltpu.HBM(tuple(s.shape), s.dtype)
    outs = pl.pallas_call(
        body, name=name,
        out_shape=(pltpu.SemaphoreType.DMA((ncopies,)), pltpu.SemaphoreType.DMA((ncopies,)), *[hbm(s) for s in srcs],
                   *[hbm(s) for s in land_shapes], _sds((8, LANES))),
        in_specs=[_HBM] * (n + m) + [pl.BlockSpec(memory_space=pl.ANY)] * k,
        out_specs=(_SEM, _SEM, *([_HBM] * (n + m)), pl.BlockSpec(memory_space=pltpu.VMEM)),
        input_output_aliases={i: 2 + i for i in range(n + m)},
        compiler_params=pltpu.CompilerParams(has_side_effects=_EFFECT),
    )(*[pltpu.with_memory_space_constraint(s, pltpu.HBM) for s in srcs],
      *[pltpu.with_memory_space_constraint(lax.empty(tuple(s.shape), s.dtype), pltpu.HBM) for s in land_shapes], *after)
    handle = (outs[0], outs[1], list(outs[2:2 + n]), list(outs[2 + n:2 + n + m]))
    return handle, outs[-1][0, 0]


def split_wait(name, plan, handle, after):
    copies_fn, _ = plan
    send_sems, recv_sems, srcs, lands = handle
    n, m = len(srcs), len(lands)
    after = list(after) if isinstance(after, (list, tuple)) else [after]

    def body(*refs):
        src_refs, land_refs = refs[:n], refs[n:n + m]
        for cp in copies_fn(src_refs, land_refs, refs[n + m], refs[n + m + 1]):
            cp.wait_send()
            cp.wait_recv()

    hbm = lambda s: pltpu.HBM(tuple(s.shape), s.dtype)
    outs = pl.pallas_call(
        body, name=name, out_shape=tuple(hbm(s) for s in srcs + lands),
        in_specs=[_HBM] * (n + m) + [_SEM, _SEM] + [pl.BlockSpec(memory_space=pl.ANY)] * len(after),
        out_specs=tuple([_HBM] * (n + m)), input_output_aliases={i: i for i in range(n + m)},
        compiler_params=pltpu.CompilerParams(has_side_effects=_EFFECT),
    )(*srcs, *lands, send_sems, recv_sems, *after)
    return list(outs[:n]), list(outs[n:])


def chip_major(w, groups=N_CHIPS):
    r, c = w.shape
    return w.reshape(r, groups, c // groups).transpose(1, 0, 2)


def from_chip_major(w):
    g, r, c = w.shape
    return w.transpose(1, 0, 2).reshape(r, g * c)


def _cd_in_pad(w):
    a = C_Q_RANK + C_KV_RANK
    z = lambda n: jnp.zeros((w.shape[0], n), w.dtype)
    return jnp.concatenate([w[:, :a], z(C_NOPE), w[:, a:a + C_ROPE], z(HEAD_PAD - C_NOPE - C_ROPE), w[:, a + C_ROPE:]], axis=1)


def _cd_in_unpad(w):
    a = C_Q_RANK + C_KV_RANK
    return jnp.concatenate([w[:, :a], w[:, a + C_NOPE:a + C_NOPE + C_ROPE], w[:, a + HEAD_PAD:]], axis=1)


def _pad_heads(w, width):
    r = w.shape[0]
    w = w.reshape(r, C_HEADS, width)
    return jnp.pad(w, ((0, 0), (0, 0), (0, HEAD_PAD - width))).reshape(r, _HW)


def _unpad_heads(w, width):
    r = w.shape[0]
    return w.reshape(r, C_HEADS, HEAD_PAD)[:, :, :width].reshape(r, C_HEADS * width)


def prepare_weights(p):
    q = dict(p)
    q["cd_w_in"] = _cd_in_pad(p["cd_w_in"])
    q["c_w_uq"] = _pad_heads(p["c_w_uq"], C_NOPE + C_ROPE)
    ukv = p["c_w_ukv"].reshape(C_KV_RANK, C_HEADS, C_NOPE + C_V)
    q["c_w_uk"] = _pad_heads(ukv[:, :, :C_NOPE].reshape(C_KV_RANK, -1), C_NOPE)
    q["c_w_uv"] = _pad_heads(ukv[:, :, C_NOPE:].reshape(C_KV_RANK, -1), C_V)
    wo = p["cd_w_out"]
    att_rows = jnp.pad(wo[:C_HEADS * C_V].reshape(C_HEADS, C_V, D_MODEL), ((0, 0), (0, HEAD_PAD - C_V), (0, 0)))
    q["cd_w_out"] = jnp.concatenate([att_rows.reshape(_HW, D_MODEL), wo[C_HEADS * C_V:]], axis=0)
    return q


def unprepare_grads(g):
    q = dict(g)
    q["cd_w_in"] = _cd_in_unpad(g["cd_w_in"])
    q["c_w_uq"] = _unpad_heads(g["c_w_uq"], C_NOPE + C_ROPE)
    uk = g.pop("c_w_uk").reshape(C_KV_RANK, C_HEADS, HEAD_PAD)[:, :, :C_NOPE]
    uv = g.pop("c_w_uv").reshape(C_KV_RANK, C_HEADS, HEAD_PAD)[:, :, :C_V]
    q.pop("c_w_uk", None)
    q.pop("c_w_uv", None)
    q["c_w_ukv"] = jnp.concatenate([uk, uv], axis=-1).reshape(C_KV_RANK, C_HEADS * (C_NOPE + C_V))
    wo = g["cd_w_out"]
    att = wo[:_HW].reshape(C_HEADS, HEAD_PAD, D_MODEL)[:, :C_V].reshape(C_HEADS * C_V, D_MODEL)
    q["cd_w_out"] = jnp.concatenate([att, wo[_HW:]], axis=0)
    return q


def rope_tables(positions):
    half = C_ROPE // 2
    inv_freq = ROPE_THETA ** (-jnp.arange(half, dtype=F32) / half)
    ang = positions.astype(F32)[:, None] * inv_freq
    cos, sin = jnp.cos(ang), jnp.sin(ang)
    s = positions.shape[0]
    z = lambda n: jnp.zeros((s, n), F32)
    cs = jnp.concatenate([jnp.ones((s, C_NOPE), F32), cos, cos, z(HEAD_PAD - C_NOPE - C_ROPE)], axis=1)
    s1 = jnp.concatenate([z(C_NOPE), -sin, z(HEAD_PAD - C_NOPE - half)], axis=1)
    s2 = jnp.concatenate([z(C_NOPE + half), sin, z(HEAD_PAD - C_NOPE - C_ROPE)], axis=1)
    return cs, s1, s2


_UP_COLS = 2 * D_FF // N_CHIPS


def ffn_fwd(h2, w, l, late_down=None):
    zf = matmul(h2, w["ffn_w_up"][l], "nn", BF16, f"ffn_up{l}", gb=N_CHIPS, go=2, tn=_UP_COLS)
    if late_down is not None:
        late_down(zf)
    a, f = ffn_act_down(zf, w["ffn_conv_w"][l], w["ffn_w_down"][l], f"ffn_act_down{l}")
    return f, (zf, a)


def ffn_bwd(df, h2, saved, w, l):
    zf, a = saved
    da = matmul(df, w["ffn_w_down"][l], "nt", BF16, f"ffn_down_dx{l}", tn=D_FF // 2)
    d_down = matmul(a, df, "tn", BF16, f"ffn_down_dw{l}", tm=D_FF // 2)
    dzf, d_conv = ffn_act_bwd(zf, w["ffn_conv_w"][l], da, f"ffn_act_bwd{l}")
    dh2 = matmul(dzf, w["ffn_w_up"][l], "nt", BF16, f"ffn_up_dx{l}", ga=2, gb=N_CHIPS, tk=_UP_COLS, tn=D_MODEL)
    d_up = matmul(h2, dzf, "tn", BF16, f"ffn_up_dw{l}", gb=2, go=N_CHIPS, tn=_UP_COLS)
    d_conv = d_conv.transpose(1, 0, 2).reshape(3, 2 * D_FF)
    return dh2, dict(ffn_w_down=d_down, ffn_conv_w=d_conv, ffn_w_up=d_up)


def mixer0_fwd(h, w):
    z = matmul(h, w["ab_w_in"], "nn", BF16, "ab_in", gb=N_CHIPS)
    ycat = pool_fwd(z, w["b_mix_w"], w["b_scale"], gconv_fwd(z, w["a_conv_w"]))
    y = matmul(ycat, w["ab_w_out"], "nn", BF16, "ab_out", tn=D_MODEL)
    return y, (z, ycat)


def mixer0_bwd(dy, h, saved, w):
    z, ycat = saved
    grads = {}
    dycat = matmul(dy, w["ab_w_out"], "nt", BF16, "ab_out_dx")
    grads["ab_w_out"] = matmul(ycat, dy, "tn", BF16, "ab_out_dw")
    db, dc, da, d_conv = gconv_bwd(z, w["a_conv_w"], dycat)
    dp, d_mix, d_scale = pool_bwd(z, w["b_mix_w"], w["b_scale"], dycat)
    dz = jnp.concatenate([db, dc, da, dp], axis=1)
    dh = matmul(dz, w["ab_w_in"], "nt", BF16, "ab_in_dx", gb=N_CHIPS, tn=D_MODEL)
    grads["ab_w_in"] = matmul(h, dz, "tn", BF16, "ab_in_dw", go=N_CHIPS)
    grads.update(a_conv_w=d_conv, b_mix_w=d_mix, b_scale=d_scale)
    return dh, grads


def mixer1_fwd(h, ropes, w):
    cs, s1, s2 = ropes
    z = matmul(h, w["cd_w_in"], "nn", BF16, "cd_in")
    bs_t = jnp.pad(w["d_b_s"].T, ((0, 0), (0, LANES - D_GROUPS)))
    qh, kh, vh = mla_pre_fwd(z, w["c_q_norm_g"], w["c_kv_norm_g"], w["c_w_uq"], w["c_w_uk"], w["c_w_uv"], cs, s1, s2)
    ycat = sgu_fwd(z, w["d_ln_g"], w["d_ln_b"], w["d_w_s"], bs_t, attn_fwd(qh, kh, vh))
    y = matmul(ycat, w["cd_w_out"], "nn", BF16, "cd_out", tn=D_MODEL)
    return y, (z, bs_t, qh, kh, vh, ycat)


def mixer1_bwd(dy, h, saved, ropes, w):
    cs, s1, s2 = ropes
    z, bs_t, qh, kh, vh, ycat = saved
    grads = {}
    dycat = matmul(dy, w["cd_w_out"], "nt", BF16, "cd_out_dx")
    grads["cd_w_out"] = matmul(ycat, dy, "tn", F32, "cd_out_dw")
    dqh, dkh, dvh = attn_bwd(qh, kh, vh, ycat, dycat, 0)
    dzq, d_uq, d_uk, d_uv, d_gq, d_gkv = mla_pre_bwd(
        z, w["c_q_norm_g"], w["c_kv_norm_g"], w["c_w_uq"], w["c_w_uk"], w["c_w_uv"], cs, s1, s2, dqh, dkh, dvh)
    dzu, dzv, d_ws, d_bs, d_lg, d_lb = sgu_bwd(z, w["d_ln_g"], w["d_ln_b"], w["d_w_s"], bs_t, dycat, _HW // _DW)
    dz = jnp.concatenate([dzq, dzu, dzv], axis=1)
    dh = matmul(dz, w["cd_w_in"], "nt", BF16, "cd_in_dx", tn=D_MODEL)
    grads["cd_w_in"] = matmul(h, dz, "tn", F32, "cd_in_dw")
    grads.update(c_w_uq=d_uq, c_w_uk=d_uk, c_w_uv=d_uv, c_q_norm_g=d_gq, c_kv_norm_g=d_gkv, d_w_s=d_ws,
                 d_b_s=d_bs[:, :D_GROUPS].T, d_ln_g=d_lg, d_ln_b=d_lb)
    return dh, grads


class StepHooks:
    def weights(self, stage, after):
        pass

    def gradients(self, stage, grads, after):
        return 0.0


def run_step(x, tgt, mod, ropes, w, hooks):
    sh1, sc1, g1, sh2, sc2, g2 = range(N_MOD)
    mods = mod.reshape(2, 1, N_MOD * D_MODEL)
    n1 = w["norm1_g"].reshape(2, 1, D_MODEL)
    n2 = w["norm2_g"].reshape(2, 1, D_MODEL)
    final_g = Vec(w["final_norm_g"].reshape(1, 1, D_MODEL), 0, 0)

    hooks.weights("mix0", mod)
    h0 = modnorm_fwd(x, Vec(n1, 0, 0), Vec(mods, 0, sc1), Vec(mods, 0, sh1), "modnorm_0")
    y0, mix0 = mixer0_fwd(h0, w)
    x1, h1 = resid_modnorm_fwd(x, y0, Vec(mods, 0, g1), Vec(n2, 0, 0), Vec(mods, 0, sc2), Vec(mods, 0, sh2), "resid_modnorm_1")
    hooks.weights("up0", x1)
    f0, ffn0 = ffn_fwd(h1, w, 0, lambda act: hooks.weights("down0", act))
    x2, h2 = resid_modnorm_fwd(x1, f0, Vec(mods, 0, g2), Vec(n1, 1, 0), Vec(mods, 1, sc1), Vec(mods, 1, sh1), "resid_modnorm_2")
    hooks.weights("mix1", x2)
    y1, mix1 = mixer1_fwd(h2, ropes, w)
    x3, h3 = resid_modnorm_fwd(x2, y1, Vec(mods, 1, g1), Vec(n2, 1, 0), Vec(mods, 1, sc2), Vec(mods, 1, sh2), "resid_modnorm_3")
    hooks.weights("ffn1", x3)
    f1, ffn1 = ffn_fwd(h3, w, 1)
    dres, d_final, loss, df1, dg2b = final_fused(x3, f1, Vec(mods, 1, g2), final_g, tgt)

    dh3, gf1 = ffn_bwd(df1, h3, ffn1, w, 1)
    late = mods + hooks.gradients("ffn1", gf1, dh3)
    dres, dsh2b, dsc2b, dn2b, dy1, dg1b = norm_gate_bwd(
        x3, dh3, Vec(n2, 1, 0), Vec(late, 1, sc2), dres, y1, Vec(late, 1, g1), "norm_gate_bwd_3")
    dh2, gm1 = mixer1_bwd(dy1, h2, mix1, ropes, w)
    late = mods + hooks.gradients("mix1", gm1, dh2)
    dres, dsh1b, dsc1b, dn1b, df0, dg2a = norm_gate_bwd(
        x2, dh2, Vec(n1, 1, 0), Vec(late, 1, sc1), dres, f0, Vec(late, 0, g2), "norm_gate_bwd_2")
    dh1, gf0 = ffn_bwd(df0, h1, ffn0, w, 0)
    late = mods + hooks.gradients("ffn0", gf0, dh1)
    dres, dsh2a, dsc2a, dn2a, dy0, dg1a = norm_gate_bwd(
        x1, dh1, Vec(n2, 0, 0), Vec(late, 0, sc2), dres, y0, Vec(late, 0, g1), "norm_gate_bwd_1")
    dh0, gm0 = mixer0_bwd(dy0, h0, mix0, w)
    late = mods + hooks.gradients("mix0", gm0, dh0)
    grad_x, dsh1a, dsc1a, dn1a = norm_bwd(x, dh0, Vec(n1, 0, 0), Vec(late, 0, sc1), dres, "norm_bwd_0")

    dmod = jnp.concatenate([jnp.concatenate([dsh1a, dsc1a, dg1a, dsh2a, dsc2a, dg2a], axis=1),
                            jnp.concatenate([dsh1b, dsc1b, dg1b, dsh2b, dsc2b, dg2b], axis=1)], axis=0)
    norms = dict(norm1_g=jnp.concatenate([dn1a, dn1b], axis=0), norm2_g=jnp.concatenate([dn2a, dn2b], axis=0),
                 final_norm_g=d_final)
    return loss, grad_x, dmod, dict(mix0=gm0, ffn0=gf0, mix1=gm1, ffn1=gf1, norms=norms)


def merge_grads(by_stage):
    grads = {**by_stage["mix0"], **by_stage["mix1"], **by_stage["norms"]}
    for k in ("ffn_w_down", "ffn_w_up"):
        grads[k] = [by_stage["ffn0"][k], by_stage["ffn1"][k]]
    grads["ffn_conv_w"] = jnp.stack([by_stage["ffn0"]["ffn_conv_w"], by_stage["ffn1"]["ffn_conv_w"]])
    return grads


_WEIGHTS = ("ada_w", "ada_b", "norm1_g", "norm2_g", "ab_w_in", "a_conv_w", "b_mix_w", "b_scale", "ab_w_out", "cd_w_in",
            "c_q_norm_g", "c_w_uq", "c_kv_norm_g", "c_w_ukv", "d_ln_g", "d_ln_b", "d_w_s", "d_b_s", "cd_w_out",
            "ffn_w_up", "ffn_conv_w", "ffn_w_down", "final_norm_g")
_INPUTS = ("x", "c", "positions") + _WEIGHTS + ("loss_target",) + tuple("m_" + n for n in _WEIGHTS) + tuple(
    "v_" + n for n in _WEIGHTS)

def _pack_rows(parts, rows, dtype):
    flat = jnp.concatenate([p.reshape(-1).astype(dtype) for p in parts])
    return jnp.pad(flat, (0, rows * LANES - flat.shape[0])).reshape(rows, LANES)


def _rows_major(w):
    r, c = w.shape
    return w.reshape(N_CHIPS, r // N_CHIPS, c)


def start_gather(shards, tag, after=()):
    lands = [_sds((N_CHIPS,) + s.shape, s.dtype) for s in shards]
    return split_start("gather_start_" + tag, GATHER, shards, lands, after)


def finish_gather(handle, chip, tag, after):
    shards, lands = split_wait("gather_wait_" + tag, GATHER, handle, after)
    lands = forward_halves(lands, "gather_forward_" + tag)
    return [lax.dynamic_update_index_in_dim(o, s, chip, 0) for o, s in zip(lands, shards)]


def start_reduce(gs, core, tag):
    recv = swap_halves(gs, "swap_halves_" + tag)
    pairs = pair_sums(gs, recv, core, "pair_sums_" + tag)
    lands = [_sds((N_CHIPS - 1,) + p.shape[1:], p.dtype) for p in pairs]
    return split_start("exchange_start_" + tag, EXCHANGE, pairs, lands)


def finish_reduce(handle, chip, core, tag, after):
    pairs, others = split_wait("exchange_wait_" + tag, EXCHANGE, handle, after)
    halves = chip_sums(pairs, others, chip, core, "chip_sums_" + tag)
    full = join_halves(halves, "join_halves_" + tag)
    return [f.reshape(f.shape[1] * 2, f.shape[2]) for f in full]


_SMALL_SHARDED = (("a_conv_w", (3, 128), 1), ("c_q_norm_g", (1, 64), 1), ("d_ln_g", (1, 128), 1), ("d_ln_b", (1, 128), 1),
                  ("ffn_conv_w", (2, 3, 2 * D_FF // N_CHIPS), 2))
_SMALL_GRADS = (("norm1_g", (2, D_MODEL)), ("norm2_g", (2, D_MODEL)), ("b_mix_w", (4, 128, 128)), ("b_scale", (1, 512)),
                ("c_kv_norm_g", (1, 128)), ("d_w_s", (4, 128, 128)), ("d_b_s", (4, 128)), ("final_norm_g", (1, D_MODEL)),
                ("a_conv_w", (3, 512)), ("c_q_norm_g", (1, 256)), ("d_ln_g", (1, 512)), ("d_ln_b", (1, 512)),
                ("ffn_conv_w", (2, 3, 2 * D_FF)))


def _size(shape):
    n = 1
    for d in shape:
        n *= d
    return n


def kernel(x, c, positions, ada_w, ada_b, norm1_g, norm2_g, ab_w_in, a_conv_w, b_mix_w, b_scale, ab_w_out, cd_w_in, c_q_norm_g, c_w_uq, c_kv_norm_g, c_w_ukv, d_ln_g, d_ln_b, d_w_s, d_b_s, cd_w_out, ffn_w_up, ffn_conv_w, ffn_w_down, final_norm_g, loss_target, m_ada_w, m_ada_b, m_norm1_g, m_norm2_g, m_ab_w_in, m_a_conv_w, m_b_mix_w, m_b_scale, m_ab_w_out, m_cd_w_in, m_c_q_norm_g, m_c_w_uq, m_c_kv_norm_g, m_c_w_ukv, m_d_ln_g, m_d_ln_b, m_d_w_s, m_d_b_s, m_cd_w_out, m_ffn_w_up, m_ffn_conv_w, m_ffn_w_down, m_final_norm_g, v_ada_w, v_ada_b, v_norm1_g, v_norm2_g, v_ab_w_in, v_a_conv_w, v_b_mix_w, v_b_scale, v_ab_w_out, v_cd_w_in, v_c_q_norm_g, v_c_w_uq, v_c_kv_norm_g, v_c_w_ukv, v_d_ln_g, v_d_ln_b, v_d_w_s, v_d_b_s, v_cd_w_out, v_ffn_w_up, v_ffn_conv_w, v_ffn_w_down, v_final_norm_g):
    args = (x, c, positions, ada_w, ada_b, norm1_g, norm2_g, ab_w_in, a_conv_w, b_mix_w, b_scale, ab_w_out, cd_w_in, c_q_norm_g, c_w_uq, c_kv_norm_g, c_w_ukv, d_ln_g, d_ln_b, d_w_s, d_b_s, cd_w_out, ffn_w_up, ffn_conv_w, ffn_w_down, final_norm_g, loss_target, m_ada_w, m_ada_b, m_norm1_g, m_norm2_g, m_ab_w_in, m_a_conv_w, m_b_mix_w, m_b_scale, m_ab_w_out, m_cd_w_in, m_c_q_norm_g, m_c_w_uq, m_c_kv_norm_g, m_c_w_ukv, m_d_ln_g, m_d_ln_b, m_d_w_s, m_d_b_s, m_cd_w_out, m_ffn_w_up, m_ffn_conv_w, m_ffn_w_down, m_final_norm_g, v_ada_w, v_ada_b, v_norm1_g, v_norm2_g, v_ab_w_in, v_a_conv_w, v_b_mix_w, v_b_scale, v_ab_w_out, v_cd_w_in, v_c_q_norm_g, v_c_w_uq, v_c_kv_norm_g, v_c_w_ukv, v_d_ln_g, v_d_ln_b, v_d_w_s, v_d_b_s, v_cd_w_out, v_ffn_w_up, v_ffn_conv_w, v_ffn_w_down, v_final_norm_g)
    a = dict(zip(_INPUTS, args, strict=True))
    xi, yi, ci = _place()
    chip = 2 * xi + yi
    dev = 4 * xi + 2 * yi + ci
    x = a["x"][0]
    tgt = a["loss_target"][0]

    bf = lambda t: t.astype(BF16)
    mix0_handle, tok = start_gather([bf(a["ab_w_in"][0]), bf(a["ab_w_out"][0])], "mix0")
    up0_16, down0_16, up1_16, down1_16 = [bf(a[n][l]) for l in (0, 1) for n in ("ffn_w_up", "ffn_w_down")]
    mix1_16 = [bf(a[n][0]) for n in ("cd_w_in", "c_w_uq", "c_w_ukv", "cd_w_out")]

    small_parts = [a["c"] + tok] + [a[n] for n, _, _ in _SMALL_SHARDED]
    rows1 = -(-sum(p.size for p in small_parts) // LANES // 8) * 8
    g1 = all_gather8(_pack_rows(small_parts, rows1, F32), "gather_small",
                     [up0_16, down0_16, up1_16, down1_16, mix1_16[0], mix1_16[3]]).reshape(N_DEV, rows1 * LANES)
    c_all = g1[:, :D_MODEL]
    per_chip = g1[0::2]
    small_full = {}
    off = D_MODEL
    for n, shp, axis in _SMALL_SHARDED:
        piece = per_chip[:, off:off + _size(shp)].reshape((N_CHIPS,) + shp)
        small_full[n] = jnp.concatenate([piece[k] for k in range(N_CHIPS)], axis=axis)
        off += _size(shp)

    merge = lambda t: t.reshape(t.shape[0] * t.shape[1], t.shape[2])
    w = dict(norm1_g=a["norm1_g"], norm2_g=a["norm2_g"], b_mix_w=a["b_mix_w"][0], b_scale=a["b_scale"],
             c_kv_norm_g=a["c_kv_norm_g"], d_w_s=a["d_w_s"][0], d_b_s=a["d_b_s"][0],
             final_norm_g=a["final_norm_g"].reshape(1, D_MODEL), **small_full)

    ncol = N_MOD * D_MODEL // N_CHIPS
    ada_b_mine = lax.dynamic_slice_in_dim(a["ada_b"], chip * ncol, ncol, axis=1)
    mod_cols = ada_mod(c_all, a["ada_w"], ada_b_mine)
    g2_rows = all_gather8(mod_cols.reshape(-1, LANES), "gather_mod")
    g2 = g2_rows.reshape(N_DEV, 2, N_DEV, ncol)
    mod = lax.dynamic_index_in_dim(g2[0::2], dev, axis=2, keepdims=False)
    mod = mod.transpose(1, 0, 2).reshape(2, N_MOD * D_MODEL)

    late = [g2_rows]
    up0_handle, tok_a = start_gather([up0_16], "up0", late)
    down0_handle, tok_b = start_gather([down0_16], "down0", late)
    mix1_handle, tok_c = start_gather(mix1_16, "mix1", late)
    ffn1_handle, tok_d = start_gather([up1_16, down1_16], "ffn1", late)
    mod = mod + (tok_a + tok_b + tok_c + tok_d)

    ropes = rope_tables(a["positions"][0])
    cm16 = lambda t: chip_major(t).astype(BF16)
    w.update(ffn_w_up=[None, None], ffn_w_down=[None, None])
    handles = dict(mix0=mix0_handle, up0=up0_handle, down0=down0_handle, mix1=mix1_handle, ffn1=ffn1_handle)
    reducing, reduced = {}, {}

    class Hooks(StepHooks):
        def weights(self, stage, after):
            got = finish_gather(handles[stage], chip, stage, after)
            if stage == "mix0":
                w.update(ab_w_in=got[0], ab_w_out=merge(got[1]))
            elif stage == "up0":
                w["ffn_w_up"][0] = got[0]
            elif stage == "down0":
                w["ffn_w_down"][0] = merge(got[0])
            elif stage == "mix1":
                cd_in, uq, ukv, cd_out = got
                w.update(prepare_weights(dict(cd_w_in=from_chip_major(cd_in), c_w_uq=from_chip_major(uq),
                                              c_w_ukv=from_chip_major(ukv), cd_w_out=merge(cd_out))))
            else:
                w["ffn_w_up"][1], w["ffn_w_down"][1] = got[0], merge(got[1])

        def gradients(self, stage, grads, after):
            if stage in ("ffn0", "ffn1"):
                parts = [grads["ffn_w_up"], _rows_major(grads["ffn_w_down"])]
            elif stage == "mix1":
                grads.update(unprepare_grads(grads))
                parts = [cm16(grads["cd_w_in"]), cm16(grads["c_w_uq"]), cm16(grads["c_w_ukv"]),
                         _rows_major(grads["cd_w_out"]).astype(BF16)]
            else:
                parts = [grads["ab_w_in"], _rows_major(grads["ab_w_out"])]
            reducing[stage], tok = start_reduce(parts, ci, stage)
            before = {"mix1": "ffn1", "ffn0": "mix1", "mix0": "ffn0"}.get(stage)
            if before is not None:
                reduced[before] = finish_reduce(reducing[before], chip, ci, before, after)
            return tok

    loss, grad_x, dmod, by_stage = run_step(x, tgt, mod, ropes, w, Hooks())
    grads = merge_grads(by_stage)

    parts3 = [dmod] + [grads[n] for n, _ in _SMALL_GRADS] + [loss[0, 0]]
    rows3 = -(-sum(p.size for p in parts3) // LANES // 8) * 8
    small_handle, _ = split_start("small_grads_start", EVERYONE, [_pack_rows(parts3, rows3, F32)],
                                  [_sds((N_DEV, rows3, LANES))])
    red_up1, red_down1 = reduced["ffn1"]
    red_cd_in, red_uq, red_ukv, red_cd_out = reduced["mix1"]
    red_up0, red_down0 = reduced["ffn0"]
    out_grads = dict(cd_w_in=red_cd_in, c_w_uq=red_uq, c_w_ukv=red_ukv, cd_w_out=red_cd_out)
    per_layer = dict(ffn_w_up=(red_up0, red_up1), ffn_w_down=(red_down0, red_down1))
    updates = {}

    def update(n):
        if n in per_layer:
            updates[n] = adamw_layers(a[n], *per_layer[n], a["m_" + n], a["v_" + n], "adamw_" + n)
        else:
            updates[n] = adamw(a[n], out_grads[n].reshape(a[n].shape), a["m_" + n], a["v_" + n], "adamw_" + n)

    early =("ffn_w_up", "ffn_w_down", "cd_w_in", "c_w_uq", "c_w_ukv", "cd_w_out")
    for n in early:
        update(n)
    (mine,), (landed,) = split_wait("small_grads_wait", EVERYONE, small_handle, [updates[n][1] for n in early])
    g3 = lax.dynamic_update_index_in_dim(landed, mine, dev, 0)
    summed = sum8(g3).reshape(-1)
    nmod = 2 * N_MOD * D_MODEL
    out_grads["ada_b"] = summed[:nmod].reshape(2, N_MOD * D_MODEL)
    off = nmod
    for n, shp in _SMALL_GRADS:
        out_grads[n] = summed[off:off + _size(shp)].reshape(shp)
        off += _size(shp)
    loss = summed[off]
    for n, shp, axis in _SMALL_SHARDED:
        width = out_grads[n].shape[-1] // N_CHIPS
        out_grads[n] = lax.dynamic_slice_in_dim(out_grads[n], chip * width, width, axis=out_grads[n].ndim - 1)
    dmod_all = g3.reshape(N_DEV, rows3 * LANES)[:, :nmod].reshape(N_DEV, 2, N_MOD * D_MODEL)
    dmod_mine = lax.dynamic_slice_in_dim(dmod_all, chip * ncol, ncol, axis=2).transpose(1, 0, 2)
    updates["ada_w"] = adamw_ada(a["ada_w"], c_all, dmod_mine, a["m_ada_w"], a["v_ada_w"])

    red_in0, red_out0 = finish_reduce(reducing["mix0"], chip, ci, "mix0", updates["ada_w"][1])
    out_grads.update(ab_w_in=red_in0, ab_w_out=red_out0)

    for n in ("ab_w_in", "ab_w_out"):
        update(n)
    small = [n for n in _WEIGHTS if n not in updates]
    for n, res in zip(small, adamw_small([a[n] for n in small], [out_grads[n].reshape(a[n].shape) for n in small],
                                         [a["m_" + n] for n in small], [a["v_" + n] for n in small])):
        updates[n] = res
    return (loss, grad_x[None], *[updates[n][i] for i in range(4) for n in _WEIGHTS])
```

```python
import functools
from typing import NamedTuple

import jax
import jax.numpy as jnp
from jax import lax
from jax.experimental import pallas as pl
from jax.experimental.pallas import tpu as pltpu

F32 = jnp.float32
BF16 = jnp.bfloat16
EPS = 1e-6
D_MODEL = 1024
N_MOD = 6
A_WIDTH = 512
B_GROUPS = 4
C_HEADS = 8
C_NOPE = 64
C_ROPE = 32
C_V = 64
C_Q_RANK = 256
C_KV_RANK = 128
HEAD_PAD = 128
ROPE_THETA = 10000.0
D_GROUPS = 4
D_CHUNK = 128
D_FF = 2816
FF_UNIT = 128
ADAM_LR = 0.001
ADAM_B1 = 0.9
ADAM_B2 = 0.999
ADAM_EPS = 1e-08
ADAM_WD = 0.01
ADAM_STEP = 10
N_CHIPS = 4
N_DEV = 8
LANES = 128
VMEM_BIG = 56 * 1024 * 1024
MESH = pl.DeviceIdType.MESH


def _sds(shape, dtype=F32):
    return jax.ShapeDtypeStruct(tuple(shape), dtype)


def _tile(n, cap, mult=128):
    if n <= cap:
        return n
    best = None
    for t in range(mult, cap + 1, mult):
        if n % t == 0:
            best = t
    assert best is not None, (n, cap, mult)
    return best


def _params(dims=None, vmem=None):
    return pltpu.CompilerParams(dimension_semantics=dims, vmem_limit_bytes=vmem)


def _shift_down(v, k):
    r = pltpu.roll(v, k, axis=0)
    t = lax.broadcasted_iota(jnp.int32, v.shape, 0)
    return jnp.where(t >= k, r, 0.0)


def _shift_up(v, k):
    n = v.shape[0]
    r = pltpu.roll(v, n - k, axis=0)
    t = lax.broadcasted_iota(jnp.int32, v.shape, 0)
    return jnp.where(t < n - k, r, 0.0)


def _sigmoid(v):
    return 1.0 / (1.0 + jnp.exp(-v))


_GELU_C = 0.7978845608028654
_GELU_A = 0.044715


def _gelu(v):
    return 0.5 * v * (1.0 + jnp.tanh(_GELU_C * (v + _GELU_A * v * v * v)))


def _gelu_grad(v):
    th = jnp.tanh(_GELU_C * (v + _GELU_A * v * v * v))
    return 0.5 * (1.0 + th) + 0.5 * v * (1.0 - th * th) * _GELU_C * (1.0 + 3.0 * _GELU_A * v * v)


_NN = (((1,), (0,)), ((), ()))
_NT = (((1,), (1,)), ((), ()))
_TN = (((0,), (0,)), ((), ()))


def _dot(a, b, dims=_NN):
    return lax.dot_general(a, b, dims, preferred_element_type=F32)


def _logical(t, groups):
    return (t.shape[-2], t.shape[-1] * groups)


def _block(tr, tc, groups, cols, where):
    if groups == 1:
        return pl.BlockSpec((tr, tc), where)
    per = cols // groups // tc

    def index(i, j, s):
        r, c = where(i, j, s)
        return (c // per, r, c % per)

    return pl.BlockSpec((None, tr, tc), index)


def matmul(a, b, mode, out_dtype, name, ga=1, gb=1, go=1, tm=None, tn=None, tk=None):
    (ar, ac), (br, bc) = _logical(a, ga), _logical(b, gb)
    if mode == "nn":
        m, k, n = ar, ac, bc
        a_col, b_col = "k", "n"
    elif mode == "nt":
        m, k, n = ar, ac, br
        a_col, b_col = "k", "k"
    else:
        k, m, n = ar, ac, bc
        a_col, b_col = "m", "n"
    limit = {"m": m, "n": n // go, "k": k}
    limit[a_col] = min(limit[a_col], ac // ga)
    limit[b_col] = min(limit[b_col], bc // gb)
    tm = tm or _tile(limit["m"], 2048, 128 if mode == "tn" else 16)
    tn = tn or _tile(limit["n"], 512)
    tk = tk or _tile(limit["k"], 2048, 16 if mode == "tn" else 128)
    nk = k // tk
    if mode == "nn":
        a_spec = _block(tm, tk, ga, ac, lambda i, j, s: (i, s))
        b_spec = _block(tk, tn, gb, bc, lambda i, j, s: (s, j))
        dims = _NN
    elif mode == "nt":
        a_spec = _block(tm, tk, ga, ac, lambda i, j, s: (i, s))
        b_spec = _block(tn, tk, gb, bc, lambda i, j, s: (j, s))
        dims = _NT
    else:
        a_spec = _block(tk, tm, ga, ac, lambda i, j, s: (s, i))
        b_spec = _block(tk, tn, gb, bc, lambda i, j, s: (s, j))
        dims = _TN
    o_spec = _block(tm, tn, go, n, lambda i, j, s: (i, j))
    out_shape = _sds((m, n), out_dtype) if go == 1 else _sds((go, m, n // go), out_dtype)

    def body(a_ref, b_ref, o_ref, acc_ref):
        s = pl.program_id(2)

        @pl.when(s == 0)
        def _():
            acc_ref[...] = jnp.zeros_like(acc_ref)

        acc_ref[...] += _dot(a_ref[...], b_ref[...], dims)

        @pl.when(s == nk - 1)
        def _():
            o_ref[...] = acc_ref[...].astype(o_ref.dtype)

    return pl.pallas_call(
        body, name=name, out_shape=out_shape, grid=(m // tm, n // tn, nk),
        in_specs=[a_spec, b_spec], out_specs=o_spec,
        scratch_shapes=[pltpu.VMEM((tm, tn), F32)],
        compiler_params=_params(("parallel", "parallel", "arbitrary"), VMEM_BIG),
    )(a, b)


def _rows(tm, n):
    return pl.BlockSpec((tm, n), lambda i: (i, 0))


def _vec(n):
    return pl.BlockSpec((1, n), lambda i: (0, 0))


class Vec(NamedTuple):
    array: jax.Array
    row: int
    col: int


def _vec_in(v, d):
    return pl.BlockSpec((None, 1, d), lambda i: (v.row, 0, v.col))


def modnorm_fwd(x, g, sc, sh, name):
    s, d = x.shape
    tm = _tile(s, 256, 8)

    def body(x_ref, g_ref, sc_ref, sh_ref, o_ref):
        xv = x_ref[...]
        r = lax.rsqrt(jnp.mean(xv * xv, axis=-1, keepdims=True) + EPS)
        o_ref[...] = ((xv * r) * g_ref[...] * (1.0 + sc_ref[...]) + sh_ref[...]).astype(BF16)

    return pl.pallas_call(
        body, name=name, out_shape=_sds((s, d), BF16), grid=(s // tm,),
        in_specs=[_rows(tm, d), _vec_in(g, d), _vec_in(sc, d), _vec_in(sh, d)], out_specs=_rows(tm, d),
        compiler_params=_params(("parallel",)),
    )(x, g.array, sc.array, sh.array)


def norm_bwd(x, dh, g, sc, dres, name):
    s, d = x.shape
    tm = _tile(s, 256, 8)
    nsteps = s // tm

    def body(x_ref, dh_ref, g_ref, sc_ref, dr_ref, dx_ref, dsh_ref, dsc_ref, dg_ref, a2_ref):
        i = pl.program_id(0)

        @pl.when(i == 0)
        def _():
            dsh_ref[...] = jnp.zeros_like(dsh_ref)
            a2_ref[...] = jnp.zeros_like(a2_ref)

        xv = x_ref[...]
        dh = dh_ref[...].astype(F32)
        r = lax.rsqrt(jnp.mean(xv * xv, axis=-1, keepdims=True) + EPS)
        xh = xv * r
        dsh_ref[...] += jnp.sum(dh, axis=0, keepdims=True)
        a2_ref[...] += jnp.sum(dh * xh, axis=0, keepdims=True)
        dxh = dh * (g_ref[...] * (1.0 + sc_ref[...]))
        dx = r * (dxh - xh * jnp.mean(dxh * xh, axis=-1, keepdims=True))
        dx_ref[...] = dr_ref[...] + dx

        @pl.when(i == nsteps - 1)
        def _():
            dsc_ref[...] = a2_ref[...] * g_ref[...]
            dg_ref[...] = a2_ref[...] * (1.0 + sc_ref[...])

    return pl.pallas_call(
        body, name=name, out_shape=(_sds((s, d)), _sds((1, d)), _sds((1, d)), _sds((1, d))), grid=(nsteps,),
        in_specs=[_rows(tm, d), _rows(tm, d), _vec_in(g, d), _vec_in(sc, d), _rows(tm, d)],
        out_specs=(_rows(tm, d), _vec(d), _vec(d), _vec(d)),
        scratch_shapes=[pltpu.VMEM((1, d), F32)],
        compiler_params=_params(("arbitrary",)),
    )(x, dh, g.array, sc.array, dres)


def resid_modnorm_fwd(x, y, gate, g, sc, sh, name):
    s, d = x.shape
    tm = _tile(s, 256, 8)

    def body(x_ref, y_ref, gate_ref, g_ref, sc_ref, sh_ref, xo_ref, h_ref):
        xv = x_ref[...] + gate_ref[...] * y_ref[...].astype(F32)
        xo_ref[...] = xv
        r = lax.rsqrt(jnp.mean(xv * xv, axis=-1, keepdims=True) + EPS)
        h_ref[...] = ((xv * r) * g_ref[...] * (1.0 + sc_ref[...]) + sh_ref[...]).astype(BF16)

    return pl.pallas_call(
        body, name=name, out_shape=(_sds((s, d)), _sds((s, d), BF16)), grid=(s // tm,),
        in_specs=[_rows(tm, d), _rows(tm, d), _vec_in(gate, d), _vec_in(g, d), _vec_in(sc, d), _vec_in(sh, d)],
        out_specs=(_rows(tm, d), _rows(tm, d)),
        compiler_params=_params(("parallel",)),
    )(x, y, gate.array, g.array, sc.array, sh.array)


def norm_gate_bwd(x, dh, g, sc, dres, y, gate, name):
    s, d = x.shape
    tm = _tile(s, 256, 8)
    nsteps = s // tm

    def body(x_ref, dh_ref, g_ref, sc_ref, dr_ref, y_ref, gate_ref, dx_ref, dsh_ref, dsc_ref, dg_ref, dy_ref,
             dgate_ref, a2_ref):
        i = pl.program_id(0)

        @pl.when(i == 0)
        def _():
            dsh_ref[...] = jnp.zeros_like(dsh_ref)
            a2_ref[...] = jnp.zeros_like(a2_ref)
            dgate_ref[...] = jnp.zeros_like(dgate_ref)

        xv = x_ref[...]
        dh = dh_ref[...].astype(F32)
        r = lax.rsqrt(jnp.mean(xv * xv, axis=-1, keepdims=True) + EPS)
        xh = xv * r
        dsh_ref[...] += jnp.sum(dh, axis=0, keepdims=True)
        a2_ref[...] += jnp.sum(dh * xh, axis=0, keepdims=True)
        dxh = dh * (g_ref[...] * (1.0 + sc_ref[...]))
        dr = dr_ref[...] + r * (dxh - xh * jnp.mean(dxh * xh, axis=-1, keepdims=True))
        dx_ref[...] = dr
        dy_ref[...] = (dr * gate_ref[...]).astype(BF16)
        dgate_ref[...] += jnp.sum(dr * y_ref[...].astype(F32), axis=0, keepdims=True)

        @pl.when(i == nsteps - 1)
        def _():
            dsc_ref[...] = a2_ref[...] * g_ref[...]
            dg_ref[...] = a2_ref[...] * (1.0 + sc_ref[...])

    vec = _sds((1, d))
    return pl.pallas_call(
        body, name=name, out_shape=(_sds((s, d)), vec, vec, vec, _sds((s, d), BF16), vec), grid=(nsteps,),
        in_specs=[_rows(tm, d), _rows(tm, d), _vec_in(g, d), _vec_in(sc, d), _rows(tm, d), _rows(tm, d), _vec_in(gate, d)],
        out_specs=(_rows(tm, d), _vec(d), _vec(d), _vec(d), _rows(tm, d), _vec(d)),
        scratch_shapes=[pltpu.VMEM((1, d), F32)],
        compiler_params=_params(("arbitrary",)),
    )(x, dh, g.array, sc.array, dres, y, gate.array)


def final_fused(x, f, gate, g, tgt):
    s, d = x.shape
    tm = _tile(s, 256, 8)

    def body(x_ref, f_ref, gate_ref, g_ref, t_ref, dx_ref, dg_ref, loss_ref, df_ref, dgate_ref):
        @pl.when(pl.program_id(0) == 0)
        def _():
            dg_ref[...] = jnp.zeros_like(dg_ref)
            loss_ref[...] = jnp.zeros_like(loss_ref)
            dgate_ref[...] = jnp.zeros_like(dgate_ref)

        fv, gatev, gv = f_ref[...].astype(F32), gate_ref[...], g_ref[...]
        xv = x_ref[...] + gatev * fv
        r = lax.rsqrt(jnp.mean(xv * xv, axis=-1, keepdims=True) + EPS)
        xh = xv * r
        e = xh * gv - t_ref[...]
        row = jnp.sum(e * e, axis=-1, keepdims=True) * (0.5 / d)
        loss_ref[...] += jnp.sum(row, axis=0, keepdims=True)
        dy = e * (1.0 / d)
        dg_ref[...] += jnp.sum(dy * xh, axis=0, keepdims=True)
        dxh = dy * gv
        dx = r * (dxh - xh * jnp.mean(dxh * xh, axis=-1, keepdims=True))
        dx_ref[...] = dx
        df_ref[...] = (dx * gatev).astype(BF16)
        dgate_ref[...] += jnp.sum(dx * fv, axis=0, keepdims=True)

    vec = _sds((1, d))
    return pl.pallas_call(
        body, name="final_fused", out_shape=(_sds((s, d)), vec, _sds((1, LANES)), _sds((s, d), BF16), vec),
        grid=(s // tm,),
        in_specs=[_rows(tm, d), _rows(tm, d), _vec_in(gate, d), _vec_in(g, d), _rows(tm, d)],
        out_specs=(_rows(tm, d), _vec(d), _vec(LANES), _rows(tm, d), _vec(d)),
        compiler_params=_params(("arbitrary",)),
    )(x, f, gate.array, g.array, tgt)


def _taps(v):
    return _shift_down(v, 2), _shift_down(v, 1), v


def _conv3_taps(taps, w):
    return w[0:1, :] * taps[0] + w[1:2, :] * taps[1] + w[2:3, :] * taps[2]


def _conv3(v, w):
    return _conv3_taps(_taps(v), w)


def _conv3_t(dv, w):
    return w[0:1, :] * _shift_up(dv, 2) + w[1:2, :] * _shift_up(dv, 1) + w[2:3, :] * dv


def _conv3_dw_taps(dv, taps):
    return jnp.concatenate([jnp.sum(dv * t, axis=0, keepdims=True) for t in taps], axis=0)


def _conv3_dw(dv, v):
    return _conv3_dw_taps(dv, _taps(v))


def gconv_fwd(z, conv_w):
    s = z.shape[0]
    nb = A_WIDTH // LANES

    def body(b_ref, c_ref, a_ref, w_ref, o_ref):
        b, c, a = b_ref[...].astype(F32), c_ref[...].astype(F32), a_ref[...].astype(F32)
        o_ref[...] = (b * _conv3(c * a, w_ref[...])).astype(BF16)

    col = lambda off: pl.BlockSpec((s, LANES), lambda j: (0, off + j))
    return pl.pallas_call(
        body, name="gconv_fwd", out_shape=_sds((s, A_WIDTH + _B_WIDTH), BF16), grid=(nb,),
        in_specs=[col(0), col(nb), col(2 * nb), pl.BlockSpec((3, LANES), lambda j: (0, j))],
        out_specs=pl.BlockSpec((s, LANES), lambda j: (0, j)),
        compiler_params=_params(("parallel",), VMEM_BIG),
    )(z, z, z, conv_w)


def gconv_bwd(z, conv_w, dycat):
    s = z.shape[0]
    nb = A_WIDTH // LANES

    def body(b_ref, c_ref, a_ref, w_ref, dy_ref, db_ref, dc_ref, da_ref, dw_ref):
        c, a, w, dy = c_ref[...].astype(F32), a_ref[...].astype(F32), w_ref[...], dy_ref[...].astype(F32)
        ca = c * a
        db_ref[...] = (dy * _conv3(ca, w)).astype(BF16)
        dconv = dy * b_ref[...].astype(F32)
        dw_ref[...] = _conv3_dw(dconv, ca)
        dca = _conv3_t(dconv, w)
        dc_ref[...] = (dca * a).astype(BF16)
        da_ref[...] = (dca * c).astype(BF16)

    col = lambda off: pl.BlockSpec((s, LANES), lambda j: (0, off + j))
    wspec = pl.BlockSpec((3, LANES), lambda j: (0, j))
    part = _sds((s, A_WIDTH), BF16)
    return pl.pallas_call(
        body, name="gconv_bwd", out_shape=(part, part, part, _sds((3, A_WIDTH))), grid=(nb,),
        in_specs=[col(0), col(nb), col(2 * nb), wspec, col(0)],
        out_specs=(col(0), col(0), col(0), wspec),
        compiler_params=_params(("parallel",), VMEM_BIG),
    )(z, z, z, conv_w, dycat)


def _pool_counts(s, w):
    t = lax.broadcasted_iota(jnp.int32, (s, 1), 0)
    return jnp.minimum(t + 1, w).astype(F32)


def _pooled(p, levels):
    acc = p
    for lv in range(levels):
        acc = acc + _shift_down(acc, 2 ** lv)
    return acc / _pool_counts(p.shape[0], 2 ** levels) - p


_B_WIDTH = B_GROUPS * LANES


def pool_fwd(z, mix_w, scale, ycat):
    s = z.shape[0]

    def body(p_ref, m_ref, sc_ref, ycat_ref, o_ref):
        del ycat_ref
        for g in range(B_GROUPS):
            cols = slice(g * LANES, (g + 1) * LANES)
            pooled = _pooled(p_ref[:, cols].astype(F32), g + 1)
            y = _dot(pooled.astype(BF16), m_ref[g].astype(BF16))
            o_ref[:, cols] = (y * sc_ref[:, cols]).astype(BF16)

    return pl.pallas_call(
        body, name="pool_fwd", out_shape=_sds(ycat.shape, BF16), grid=(1,),
        in_specs=[pl.BlockSpec((s, _B_WIDTH), lambda i: (0, 3 * A_WIDTH // _B_WIDTH)),
                  pl.BlockSpec((B_GROUPS, LANES, LANES), lambda i: (0, 0, 0)), pl.BlockSpec((1, _B_WIDTH), lambda i: (0, 0)),
                  pl.BlockSpec(memory_space=pl.ANY)],
        out_specs=pl.BlockSpec((s, _B_WIDTH), lambda i: (0, A_WIDTH // _B_WIDTH)),
        input_output_aliases={3: 0},
        compiler_params=_params(("arbitrary",), VMEM_BIG),
    )(z, mix_w, scale, ycat)


def pool_bwd(z, mix_w, scale, dycat):
    s = z.shape[0]

    def body(p_ref, m_ref, sc_ref, dy_ref, dp_ref, dm_ref, dsc_ref):
        for g in range(B_GROUPS):
            cols = slice(g * LANES, (g + 1) * LANES)
            pooled = _pooled(p_ref[:, cols].astype(F32), g + 1)
            mw = m_ref[g].astype(BF16)
            pb = pooled.astype(BF16)
            dy = dy_ref[:, cols].astype(F32)
            dsc_ref[:, cols] = jnp.sum(dy * _dot(pb, mw), axis=0, keepdims=True)
            dmix = (dy * sc_ref[:, cols]).astype(BF16)
            dm_ref[g] = _dot(pb, dmix, _TN)
            dpool = _dot(dmix, mw, _NT)
            acc = dpool / _pool_counts(s, 2 ** (g + 1))
            for lv in range(g + 1):
                acc = acc + _shift_up(acc, 2 ** lv)
            dp_ref[:, cols] = (acc - dpool).astype(BF16)

    wide = lambda c: pl.BlockSpec((s, _B_WIDTH), lambda i: (0, c))
    mspec = pl.BlockSpec((B_GROUPS, LANES, LANES), lambda i: (0, 0, 0))
    vspec = pl.BlockSpec((1, _B_WIDTH), lambda i: (0, 0))
    return pl.pallas_call(
        body, name="pool_bwd", out_shape=(_sds((s, _B_WIDTH), BF16), _sds((B_GROUPS, LANES, LANES)), _sds((1, _B_WIDTH))),
        grid=(1,), in_specs=[wide(3 * A_WIDTH // _B_WIDTH), mspec, vspec, wide(A_WIDTH // _B_WIDTH)],
        out_specs=(wide(0), mspec, vspec),
        compiler_params=_params(("arbitrary",), VMEM_BIG),
    )(z, mix_w, scale, dycat)


_FF_BLOCKS = D_FF // FF_UNIT


def _ff_spec(s):
    return pl.BlockSpec((2, s, FF_UNIT), lambda j: (0, 0, j))


def _ff_wspecs():
    return [pl.BlockSpec((3, FF_UNIT), lambda j: (0, j)), pl.BlockSpec((3, FF_UNIT), lambda j: (0, _FF_BLOCKS + j))]


_FF_ROWS = 64
_FF_HALO = 16


def _chunk_taps(z_ref, half, c):
    start = pl.multiple_of(c * _FF_ROWS, _FF_ROWS)
    before = pl.multiple_of(jnp.maximum(c * _FF_ROWS - _FF_HALO, 0), _FF_HALO)
    halo = z_ref[half, pl.ds(before, _FF_HALO), :].astype(F32)
    halo = jnp.where(c > 0, halo, 0.0)
    win = jnp.concatenate([halo, z_ref[half, pl.ds(start, _FF_ROWS), :].astype(F32)], axis=0)
    return tuple(pltpu.roll(win, k, axis=0)[_FF_HALO:] for k in (2, 1)) + (win[_FF_HALO:],)


def _fold8(v):
    acc = v[0:8]
    for r in range(8, v.shape[0], 8):
        acc = acc + v[r:r + 8]
    return acc


_FF_CHUNK = 256


def ffn_act_down(zf, conv_w, w_down, name):
    s, d = zf.shape[1], w_down.shape[1]
    nk = D_FF // _FF_CHUNK
    chunk = lambda k: jnp.minimum(k, nk - 1)

    def body(z_ref, wg_ref, wu_ref, wd_ref, a_ref, f_ref, held_ref, acc_ref):
        k = pl.program_id(0)

        @pl.when(k == 0)
        def _():
            held_ref[...] = jnp.zeros_like(held_ref)
            acc_ref[...] = jnp.zeros_like(acc_ref)

        acc_ref[...] += _dot(held_ref[(k + 1) % 2], wd_ref[...])
        g = _conv3(z_ref[0].astype(F32), wg_ref[...])
        u = _conv3(z_ref[1].astype(F32), wu_ref[...])
        act = (g * _sigmoid(g) * u).astype(BF16)
        a_ref[...] = act
        held_ref[k % 2] = act

        @pl.when(k == nk)
        def _():
            f_ref[...] = acc_ref[...].astype(BF16)

    return pl.pallas_call(
        body, name=name, out_shape=(_sds((s, D_FF), BF16), _sds((s, d), BF16)), grid=(nk + 1,),
        in_specs=[pl.BlockSpec((2, s, _FF_CHUNK), lambda k: (0, 0, chunk(k))),
                  pl.BlockSpec((3, _FF_CHUNK), lambda k: (0, chunk(k))),
                  pl.BlockSpec((3, _FF_CHUNK), lambda k: (0, nk + chunk(k))),
                  pl.BlockSpec((_FF_CHUNK, d), lambda k: (jnp.maximum(k - 1, 0), 0))],
        out_specs=(pl.BlockSpec((s, _FF_CHUNK), lambda k: (0, chunk(k))), pl.BlockSpec((s, d), lambda k: (0, 0))),
        scratch_shapes=[pltpu.VMEM((2, s, _FF_CHUNK), BF16), pltpu.VMEM((s, d), F32)],
        compiler_params=_params(("arbitrary",), VMEM_BIG),
    )(zf, conv_w, conv_w, w_down)


def ffn_act_bwd(zf, conv_w, da, name):
    s = zf.shape[1]
    assert s % _FF_ROWS == 0
    nchunks = s // _FF_ROWS

    def body(z_ref, wg_ref, wu_ref, da_ref, dz_ref, dw_ref, dg_ref, du_ref):
        wg, wu = wg_ref[...], wu_ref[...]

        def first(c, acc):
            rows = pl.ds(pl.multiple_of(c * _FF_ROWS, _FF_ROWS), _FF_ROWS)
            tg, tu = _chunk_taps(z_ref, 0, c), _chunk_taps(z_ref, 1, c)
            g = _conv3_taps(tg, wg)
            u = _conv3_taps(tu, wu)
            dav = da_ref[rows, :].astype(F32)
            sg = _sigmoid(g)
            dg = dav * u * (sg * (1.0 + g * (1.0 - sg)))
            du = dav * (g * sg)
            dg_ref[rows, :] = dg
            du_ref[rows, :] = du
            return tuple(a + _fold8(d * t) for a, (d, t) in zip(acc, [(dg, t) for t in tg] + [(du, t) for t in tu]))

        zero = jnp.zeros((8, FF_UNIT), F32)
        acc = lax.fori_loop(0, nchunks, first, (zero,) * 6)
        sums = [jnp.sum(a, axis=0, keepdims=True) for a in acc]
        dw_ref[0] = jnp.concatenate(sums[:3], axis=0)
        dw_ref[1] = jnp.concatenate(sums[3:], axis=0)

        tail = pl.ds(s, _FF_HALO)
        dg_ref[tail, :] = jnp.zeros((_FF_HALO, FF_UNIT), F32)
        du_ref[tail, :] = jnp.zeros((_FF_HALO, FF_UNIT), F32)
        span = _FF_ROWS + _FF_HALO

        def second(c, carry):
            start = pl.multiple_of(c * _FF_ROWS, _FF_ROWS)
            for half, (d_ref, w) in enumerate(((dg_ref, wg), (du_ref, wu))):
                win = d_ref[pl.ds(start, span), :]
                dz = (w[0:1, :] * pltpu.roll(win, span - 2, axis=0)[:_FF_ROWS]
                      + w[1:2, :] * pltpu.roll(win, span - 1, axis=0)[:_FF_ROWS] + w[2:3, :] * win[:_FF_ROWS])
                dz_ref[half, pl.ds(start, _FF_ROWS), :] = dz.astype(BF16)
            return carry

        lax.fori_loop(0, nchunks, second, 0)

    return pl.pallas_call(
        body, name=name, out_shape=(_sds((2, s, D_FF), BF16), _sds((2, 3, D_FF))), grid=(_FF_BLOCKS,),
        in_specs=[_ff_spec(s)] + _ff_wspecs() + [pl.BlockSpec((s, FF_UNIT), lambda j: (0, j))],
        out_specs=(_ff_spec(s), pl.BlockSpec((2, 3, FF_UNIT), lambda j: (0, 0, j))),
        scratch_shapes=[pltpu.VMEM((s + _FF_HALO, FF_UNIT), F32), pltpu.VMEM((s + _FF_HALO, FF_UNIT), F32)],
        compiler_params=_params(("parallel",), VMEM_BIG),
    )(zf, conv_w, conv_w, da)


def _rope(v, cs, s1, s2):
    return v * cs + pltpu.roll(v, LANES - C_ROPE // 2, axis=1) * s1 + pltpu.roll(v, C_ROPE // 2, axis=1) * s2


def _rope_t(dv, cs, s1, s2):
    return dv * cs + pltpu.roll(dv * s1, C_ROPE // 2, axis=1) + pltpu.roll(dv * s2, LANES - C_ROPE // 2, axis=1)


def _kpe_mask(shape):
    lane = lax.broadcasted_iota(jnp.int32, shape, 1)
    return (lane >= C_NOPE) & (lane < C_NOPE + C_ROPE)


def _rms(v, g):
    r = lax.rsqrt(jnp.mean(v * v, axis=-1, keepdims=True) + EPS)
    return v * r, r


def _rms_bwd(dn, xh, r, g):
    dxh = dn * g
    return r * (dxh - xh * jnp.mean(dxh * xh, axis=-1, keepdims=True)), jnp.sum(dn * xh, axis=0, keepdims=True)


_ZQ = C_Q_RANK + C_KV_RANK + HEAD_PAD
_HW = C_HEADS * HEAD_PAD


def mla_pre_fwd(z, gq, gkv, wq, wk, wv, cs, s1, s2):
    s = z.shape[0]
    tm = _tile(s, 256, 8)

    def body(z_ref, gq_ref, gkv_ref, wq_ref, wk_ref, wv_ref, cs_ref, s1_ref, s2_ref, q_ref, k_ref, v_ref):
        zv = z_ref[...].astype(F32)
        cst, s1t, s2t = cs_ref[...], s1_ref[...], s2_ref[...]
        qh, _ = _rms(zv[:, :C_Q_RANK], None)
        qn = (qh * gq_ref[...]).astype(BF16)
        q = _dot(qn, wq_ref[...])
        kh, _ = _rms(zv[:, C_Q_RANK:C_Q_RANK + C_KV_RANK], None)
        kvn = (kh * gkv_ref[...]).astype(BF16)
        k = _dot(kvn, wk_ref[...])
        v_ref[...] = _dot(kvn, wv_ref[...]).astype(BF16)
        kpe = _rope(zv[:, C_Q_RANK + C_KV_RANK:], cst, s1t, s2t)
        for h in range(C_HEADS):
            sl = slice(h * HEAD_PAD, (h + 1) * HEAD_PAD)
            q_ref[:, sl] = _rope(q[:, sl], cst, s1t, s2t).astype(BF16)
            k_ref[:, sl] = (k[:, sl] + kpe).astype(BF16)

    full = lambda r, c: pl.BlockSpec((r, c), lambda i: (0, 0))
    hw = _sds((s, _HW), BF16)
    return pl.pallas_call(
        body, name="mla_pre_fwd", out_shape=(hw, hw, hw), grid=(s // tm,),
        in_specs=[_rows(tm, _ZQ), _vec(C_Q_RANK), _vec(C_KV_RANK), full(C_Q_RANK, _HW), full(C_KV_RANK, _HW),
                  full(C_KV_RANK, _HW), _rows(tm, LANES), _rows(tm, LANES), _rows(tm, LANES)],
        out_specs=(_rows(tm, _HW), _rows(tm, _HW), _rows(tm, _HW)),
        compiler_params=_params(("parallel",), VMEM_BIG),
    )(z, gq, gkv, wq, wk, wv, cs, s1, s2)


def mla_pre_bwd(z, gq, gkv, wq, wk, wv, cs, s1, s2, dq, dk, dv):
    s = z.shape[0]
    tm = _tile(s, 256, 8)

    def body(z_ref, gq_ref, gkv_ref, wq_ref, wk_ref, wv_ref, cs_ref, s1_ref, s2_ref, dq_ref, dk_ref, dv_ref,
             dz_ref, dwq_ref, dwk_ref, dwv_ref, dgq_ref, dgkv_ref):
        @pl.when(pl.program_id(0) == 0)
        def _():
            dwq_ref[...] = jnp.zeros_like(dwq_ref)
            dwk_ref[...] = jnp.zeros_like(dwk_ref)
            dwv_ref[...] = jnp.zeros_like(dwv_ref)
            dgq_ref[...] = jnp.zeros_like(dgq_ref)
            dgkv_ref[...] = jnp.zeros_like(dgkv_ref)

        zv = z_ref[...].astype(F32)
        cst, s1t, s2t = cs_ref[...], s1_ref[...], s2_ref[...]
        gqv, gkvv = gq_ref[...], gkv_ref[...]
        qh, rq = _rms(zv[:, :C_Q_RANK], None)
        qn = (qh * gqv).astype(BF16)
        kh, rk = _rms(zv[:, C_Q_RANK:C_Q_RANK + C_KV_RANK], None)
        kvn = (kh * gkvv).astype(BF16)

        dqv = dq_ref[...].astype(F32)
        dqp = jnp.concatenate(
            [_rope_t(dqv[:, h * HEAD_PAD:(h + 1) * HEAD_PAD], cst, s1t, s2t) for h in range(C_HEADS)], axis=1
        ).astype(BF16)
        dwq_ref[...] += _dot(qn, dqp, _TN)
        dqn = _dot(dqp, wq_ref[...], _NT)
        dql, dgq = _rms_bwd(dqn, qh, rq, gqv)
        dgq_ref[...] += dgq

        dkv = dk_ref[...]
        dkb = dkv.astype(BF16)
        dvb = dv_ref[...].astype(BF16)
        dwk_ref[...] += _dot(kvn, dkb, _TN)
        dwv_ref[...] += _dot(kvn, dvb, _TN)
        dkvn = _dot(dkb, wk_ref[...], _NT) + _dot(dvb, wv_ref[...], _NT)
        dkl, dgkv = _rms_bwd(dkvn, kh, rk, gkvv)
        dgkv_ref[...] += dgkv

        dkpe = dkv[:, :HEAD_PAD]
        for h in range(1, C_HEADS):
            dkpe = dkpe + dkv[:, h * HEAD_PAD:(h + 1) * HEAD_PAD]
        dkpe = _rope_t(jnp.where(_kpe_mask(dkpe.shape), dkpe, 0.0), cst, s1t, s2t)
        dz_ref[...] = jnp.concatenate([dql, dkl, dkpe], axis=1).astype(BF16)

    full = lambda r, c: pl.BlockSpec((r, c), lambda i: (0, 0))
    return pl.pallas_call(
        body, name="mla_pre_bwd",
        out_shape=(_sds((s, _ZQ), BF16), _sds((C_Q_RANK, _HW)), _sds((C_KV_RANK, _HW)), _sds((C_KV_RANK, _HW)),
                   _sds((1, C_Q_RANK)), _sds((1, C_KV_RANK))),
        grid=(s // tm,),
        in_specs=[_rows(tm, _ZQ), _vec(C_Q_RANK), _vec(C_KV_RANK), full(C_Q_RANK, _HW), full(C_KV_RANK, _HW),
                  full(C_KV_RANK, _HW), _rows(tm, LANES), _rows(tm, LANES), _rows(tm, LANES),
                  _rows(tm, _HW), _rows(tm, _HW), _rows(tm, _HW)],
        out_specs=(_rows(tm, _ZQ), full(C_Q_RANK, _HW), full(C_KV_RANK, _HW), full(C_KV_RANK, _HW),
                   _vec(C_Q_RANK), _vec(C_KV_RANK)),
        compiler_params=_params(("arbitrary",), VMEM_BIG),
    )(z, gq, gkv, wq, wk, wv, cs, s1, s2, dq, dk, dv)


_ATT_SCALE = (C_NOPE + C_ROPE) ** -0.5
_NEG = -1e30


def _att_exp(q, k, row0, ends_here):
    sc = _dot(q, k, _NT) * _ATT_SCALE
    tq, nk = sc.shape
    if ends_here:
        last = sc[:, nk - tq:]
        row = lax.broadcasted_iota(jnp.int32, last.shape, 0)
        col = lax.broadcasted_iota(jnp.int32, last.shape, 1)
        last = jnp.where(col <= row, last, _NEG)
        sc = last if nk == tq else jnp.concatenate([sc[:, :nk - tq], last], axis=1)
    else:
        qpos = row0 + lax.broadcasted_iota(jnp.int32, sc.shape, 0)
        kpos = lax.broadcasted_iota(jnp.int32, sc.shape, 1)
        sc = jnp.where(kpos <= qpos, sc, _NEG)
    e = jnp.exp(sc - jnp.max(sc, axis=-1, keepdims=True))
    return e, 1.0 / jnp.sum(e, axis=-1, keepdims=True)


def _causal_cases(i, nq, tq, fn):
    if nq > 8:
        fn(nq * tq, False)
        return
    for blk in range(nq):
        pl.when(i == blk)(functools.partial(fn, (blk + 1) * tq, True))


_HEADS_PER_STEP = 2
_HEAD_LANES = [slice(h * HEAD_PAD, (h + 1) * HEAD_PAD) for h in range(_HEADS_PER_STEP)]


def attn_fwd(q, k, v):
    s = q.shape[0]
    tq = _tile(s, 256, 8)
    nq = s // tq
    wide = _HEADS_PER_STEP * HEAD_PAD

    def body(q_ref, k_ref, v_ref, o_ref):
        i = pl.program_id(1)

        def case(nk, ends_here):
            for hd in _HEAD_LANES:
                e, inv = _att_exp(q_ref[:, hd], k_ref[:nk, hd], i * tq, ends_here)
                o_ref[:, hd] = (_dot(e.astype(BF16), v_ref[:nk, hd]) * inv).astype(BF16)

        _causal_cases(i, nq, tq, case)

    qspec = pl.BlockSpec((tq, wide), lambda h, i: (i, h))
    kspec = pl.BlockSpec((s, wide), lambda h, i: (0, h))
    return pl.pallas_call(
        body, name="attn_fwd", out_shape=_sds((s, _HW + _DW), BF16), grid=(C_HEADS // _HEADS_PER_STEP, s // tq),
        in_specs=[qspec, kspec, kspec], out_specs=qspec,
        compiler_params=_params(("parallel", "parallel"), VMEM_BIG),
    )(q, k, v)


def attn_bwd(q, k, v, o, do_all):
    s = q.shape[0]
    tq = _tile(s, 256, 8)
    wide = _HEADS_PER_STEP * HEAD_PAD

    def body(q_ref, k_ref, v_ref, o_ref, do_ref, dq_ref, dk_ref, dv_ref):
        i = pl.program_id(1)

        @pl.when(i == 0)
        def _():
            dk_ref[...] = jnp.zeros_like(dk_ref)
            dv_ref[...] = jnp.zeros_like(dv_ref)

        def case(nk, ends_here):
            for hd in _HEAD_LANES:
                qv, kv, vv, dov = q_ref[:, hd], k_ref[:nk, hd], v_ref[:nk, hd], do_ref[:, hd]
                e, inv = _att_exp(qv, kv, i * tq, ends_here)
                p = e * inv
                dp = _dot(dov, vv, _NT)
                delta = jnp.sum(dov.astype(F32) * o_ref[:, hd].astype(F32), axis=-1, keepdims=True)
                ds = (p * (dp - delta) * _ATT_SCALE).astype(BF16)
                dq_ref[:, hd] = _dot(ds, kv).astype(BF16)
                dk_ref[:nk, hd] += _dot(ds, qv, _TN)
                dv_ref[:nk, hd] += _dot(p.astype(BF16), dov, _TN)

        _causal_cases(i, s // tq, tq, case)

    qspec = pl.BlockSpec((tq, wide), lambda h, i: (i, h))
    kspec = pl.BlockSpec((s, wide), lambda h, i: (0, h))
    return pl.pallas_call(
        body, name="attn_bwd", out_shape=(_sds((s, _HW), BF16), _sds((s, _HW)), _sds((s, _HW))),
        grid=(C_HEADS // _HEADS_PER_STEP, s // tq),
        in_specs=[qspec, kspec, kspec, qspec, qspec], out_specs=(qspec, kspec, kspec),
        compiler_params=_params(("parallel", "arbitrary"), VMEM_BIG),
    )(q, k, v, o, do_all)


_DW = D_GROUPS * LANES


def _tril_bf16(w):
    r = lax.broadcasted_iota(jnp.int32, w.shape, 0)
    c = lax.broadcasted_iota(jnp.int32, w.shape, 1)
    return jnp.where(c <= r, w, 0.0).astype(BF16)


def _sgu_forward(zu, zv, lg, lb, ws_ref, bs):
    u = _gelu(zu)
    v = _gelu(zv)
    mu = jnp.mean(v, axis=-1, keepdims=True)
    vc = v - mu
    rstd = lax.rsqrt(jnp.mean(vc * vc, axis=-1, keepdims=True) + EPS)
    xh = vc * rstd
    vln = (xh * lg + lb).astype(BF16)
    mixed = []
    for g in range(D_GROUPS):
        wg = _tril_bf16(ws_ref[g])
        mixed.append(_dot(wg, vln[:, g * LANES:(g + 1) * LANES]) + bs[:, g:g + 1])
    return u, xh, rstd, vln, jnp.concatenate(mixed, axis=1)


def sgu_fwd(z, lg, lb, ws, bs_t, ycat):
    s = z.shape[0]
    nchunk = s // D_CHUNK

    def body(zu_ref, zv_ref, lg_ref, lb_ref, ws_ref, bs_ref, ycat_ref, o_ref):
        del ycat_ref
        u, _, _, _, mixed = _sgu_forward(zu_ref[...].astype(F32), zv_ref[...].astype(F32), lg_ref[...], lb_ref[...],
                                         ws_ref, bs_ref[...])
        o_ref[...] = (u * mixed).astype(BF16)

    return pl.pallas_call(
        body, name="sgu_fwd", out_shape=_sds(ycat.shape, BF16), grid=(nchunk,),
        in_specs=[pl.BlockSpec((D_CHUNK, _DW), lambda n: (n, 1)), pl.BlockSpec((D_CHUNK, _DW), lambda n: (n, 2)),
                  _vec(_DW), _vec(_DW), pl.BlockSpec((D_GROUPS, D_CHUNK, D_CHUNK), lambda n: (0, 0, 0)),
                  pl.BlockSpec((D_CHUNK, LANES), lambda n: (0, 0)), pl.BlockSpec(memory_space=pl.ANY)],
        out_specs=pl.BlockSpec((D_CHUNK, _DW), lambda n: (n, _HW // _DW)),
        input_output_aliases={6: 0},
        compiler_params=_params(("parallel",)),
    )(z, z, lg, lb, ws, bs_t, ycat)


def sgu_bwd(z, lg, lb, ws, bs_t, dycat, dy_col):
    s = z.shape[0]
    nchunk = s // D_CHUNK

    def body(zu_ref, zv_ref, lg_ref, lb_ref, ws_ref, bs_ref, dy_ref, dzu_ref, dzv_ref, dws_ref, dbs_ref, dlg_ref,
             dlb_ref):
        @pl.when(pl.program_id(0) == 0)
        def _():
            dws_ref[...] = jnp.zeros_like(dws_ref)
            dbs_ref[...] = jnp.zeros_like(dbs_ref)
            dlg_ref[...] = jnp.zeros_like(dlg_ref)
            dlb_ref[...] = jnp.zeros_like(dlb_ref)

        zu, zv, lg = zu_ref[...].astype(F32), zv_ref[...].astype(F32), lg_ref[...]
        u, xh, rstd, vln, mixed = _sgu_forward(zu, zv, lg, lb_ref[...], ws_ref, bs_ref[...])
        dy = dy_ref[...].astype(F32)
        dzu_ref[...] = (dy * mixed * _gelu_grad(zu)).astype(BF16)
        dmix = dy * u
        lane = lax.broadcasted_iota(jnp.int32, (D_CHUNK, LANES), 1)
        row = lax.broadcasted_iota(jnp.int32, (D_CHUNK, D_CHUNK), 0)
        colm = lax.broadcasted_iota(jnp.int32, (D_CHUNK, D_CHUNK), 1)
        dvln = []
        dbs = jnp.zeros((D_CHUNK, LANES), F32)
        for g in range(D_GROUPS):
            sl = slice(g * LANES, (g + 1) * LANES)
            dmg = dmix[:, sl]
            dbs = dbs + jnp.where(lane == g, jnp.sum(dmg, axis=-1, keepdims=True), 0.0)
            dmb = dmg.astype(BF16)
            dws_ref[g] += jnp.where(colm <= row, _dot(dmb, vln[:, sl], _NT), 0.0)
            dvln.append(_dot(_tril_bf16(ws_ref[g]), dmb, _TN))
        dbs_ref[...] += dbs
        dvln = jnp.concatenate(dvln, axis=1)
        dlg_ref[...] += jnp.sum(dvln * xh, axis=0, keepdims=True)
        dlb_ref[...] += jnp.sum(dvln, axis=0, keepdims=True)
        dxh = dvln * lg
        dvv = rstd * (dxh - jnp.mean(dxh, axis=-1, keepdims=True) - xh * jnp.mean(dxh * xh, axis=-1, keepdims=True))
        dzv_ref[...] = (dvv * _gelu_grad(zv)).astype(BF16)

    wsspec = pl.BlockSpec((D_GROUPS, D_CHUNK, D_CHUNK), lambda n: (0, 0, 0))
    chunk = lambda cidx: pl.BlockSpec((D_CHUNK, _DW), lambda n: (n, cidx))
    return pl.pallas_call(
        body, name="sgu_bwd",
        out_shape=(_sds((s, _DW), BF16), _sds((s, _DW), BF16), _sds((D_GROUPS, D_CHUNK, D_CHUNK)),
                   _sds((D_CHUNK, LANES)), _sds((1, _DW)), _sds((1, _DW))),
        grid=(nchunk,),
        in_specs=[chunk(1), chunk(2), _vec(_DW), _vec(_DW), wsspec, pl.BlockSpec((D_CHUNK, LANES), lambda n: (0, 0)),
                  chunk(dy_col)],
        out_specs=(chunk(0), chunk(0), wsspec, pl.BlockSpec((D_CHUNK, LANES), lambda n: (0, 0)), _vec(_DW), _vec(_DW)),
        compiler_params=_params(("arbitrary",)),
    )(z, z, lg, lb, ws, bs_t, dycat)


def ada_mod(c_all, ada_w, ada_b):
    nl, d, n = ada_w.shape
    nb = c_all.shape[0]
    tn = _tile(n, 512)

    def body(c_ref, w_ref, b_ref, o_ref):
        cv = c_ref[...]
        ca = (cv * _sigmoid(cv)).astype(BF16)
        o_ref[...] = _dot(ca, w_ref[...].astype(BF16)) + b_ref[...]

    return pl.pallas_call(
        body, name="ada_mod", out_shape=_sds((nl, nb, n)), grid=(nl, n // tn),
        in_specs=[pl.BlockSpec((nb, d), lambda l, j: (0, 0)), pl.BlockSpec((None, d, tn), lambda l, j: (l, 0, j)),
                  pl.BlockSpec((None, 1, tn), lambda l, j: (l, 0, j))],
        out_specs=pl.BlockSpec((None, nb, tn), lambda l, j: (l, 0, j)),
        compiler_params=_params(("parallel", "parallel")),
    )(c_all, ada_w, ada_b.reshape(nl, 1, n))


_ADAM_BLOCK = 256 * 1024


def _adam_rows(rows, cols):
    if rows * cols <= _ADAM_BLOCK or rows % 8:
        return rows
    return _tile(rows, max(8, _ADAM_BLOCK // cols), 8)


def _adam_update(w, gv, m, v):
    inv_bc1 = 1.0 / (1.0 - ADAM_B1 ** ADAM_STEP)
    inv_bc2 = 1.0 / (1.0 - ADAM_B2 ** ADAM_STEP)
    nm = ADAM_B1 * m + (1.0 - ADAM_B1) * gv
    nv = ADAM_B2 * v + (1.0 - ADAM_B2) * (gv * gv)
    return -ADAM_LR * ((nm * inv_bc1) / (jnp.sqrt(nv * inv_bc2) + ADAM_EPS) + ADAM_WD * w), nm, nv


def adamw(w, g, m, v, name):
    shape = w.shape
    cols = shape[-1]
    rows = w.size // cols
    tr = _adam_rows(rows, cols)

    def body(w_ref, g_ref, m_ref, v_ref, go_ref, d_ref, nm_ref, nv_ref):
        gv = g_ref[...]
        go_ref[...] = gv
        d_ref[...], nm_ref[...], nv_ref[...] = _adam_update(w_ref[...], gv, m_ref[...], v_ref[...])

    spec = pl.BlockSpec((tr, cols), lambda i: (i, 0))
    out = _sds((rows, cols))
    r2 = lambda t: t.reshape(rows, cols)
    res = pl.pallas_call(
        body, name=name, out_shape=(out,) * 4, grid=(rows // tr,),
        in_specs=[spec] * 4, out_specs=(spec,) * 4, compiler_params=_params(("parallel",)),
    )(r2(w), r2(g), r2(m), r2(v))
    return tuple(t.reshape(shape) for t in res)


def adamw_ada(w, c_all, dmod, m, v):
    nl, d, n = w.shape
    tr = _adam_rows(d, n)
    pad = 16 - c_all.shape[0]
    c16 = jnp.pad(c_all, ((0, pad), (0, 0)))
    dm16 = jnp.pad(dmod, ((0, 0), (0, pad), (0, 0)))

    def body(w_ref, c_ref, dm_ref, m_ref, v_ref, g_ref, d_ref, nm_ref, nv_ref):
        cv = c_ref[...]
        gv = _dot((cv * _sigmoid(cv)).astype(BF16), dm_ref[...].astype(BF16), _TN)
        g_ref[...] = gv
        d_ref[...], nm_ref[...], nv_ref[...] = _adam_update(w_ref[...], gv, m_ref[...], v_ref[...])

    spec = pl.BlockSpec((None, tr, n), lambda l, i: (l, i, 0))
    out = _sds((nl, d, n))
    return pl.pallas_call(
        body, name="adamw_ada_w", out_shape=(out, out, out, out), grid=(nl, d // tr),
        in_specs=[spec, pl.BlockSpec((16, tr), lambda l, i: (0, i)), pl.BlockSpec((None, 16, n), lambda l, i: (l, 0, 0)),
                  spec, spec],
        out_specs=(spec,) * 4, compiler_params=_params(("parallel", "parallel")),
    )(w, c16, dm16, m, v)


def adamw_small(ws, gs, ms, vs):
    n = len(ws)
    flat = lambda t: t.reshape(-1, t.shape[-1])

    def body(*refs):
        ins, outs = refs[:4 * n], refs[4 * n:]
        for i in range(n):
            w_ref, g_ref, m_ref, v_ref = ins[4 * i:4 * i + 4]
            outs[3 * i][...], outs[3 * i + 1][...], outs[3 * i + 2][...] = _adam_update(
                w_ref[...], g_ref[...], m_ref[...], v_ref[...])

    operands = [flat(t) for quad in zip(ws, gs, ms, vs) for t in quad]
    res = pl.pallas_call(
        body, name="adamw_small", out_shape=tuple(_sds(flat(w).shape) for w in ws for _ in range(3)),
    )(*operands)
    return [(g, res[3 * i].reshape(w.shape), res[3 * i + 1].reshape(w.shape), res[3 * i + 2].reshape(w.shape))
            for i, (w, g) in enumerate(zip(ws, gs))]


def adamw_layers(w, g0, g1, m, v, name):
    _, rows, cols = w.shape
    tr = _adam_rows(rows, cols)

    def body(w_ref, g0_ref, g1_ref, m_ref, v_ref, g_ref, d_ref, nm_ref, nv_ref):
        gv = jnp.where(pl.program_id(0) == 0, g0_ref[...], g1_ref[...])
        g_ref[...] = gv
        d_ref[...], nm_ref[...], nv_ref[...] = _adam_update(w_ref[...], gv, m_ref[...], v_ref[...])

    spec = pl.BlockSpec((None, tr, cols), lambda l, i: (l, i, 0))
    gspec = pl.BlockSpec((tr, cols), lambda l, i: (i, 0))
    out = _sds((2, rows, cols))
    return pl.pallas_call(
        body, name=name, out_shape=(out, out, out, out), grid=(2, rows // tr),
        in_specs=[spec, gspec, gspec, spec, spec], out_specs=(spec,) * 4, compiler_params=_params(("parallel", "parallel")),
    )(w, g0, g1, m, v)


def sum8(gathered):
    _, r, _ = gathered.shape
    tr = _tile(r, 512, 8)

    def body(g_ref, o_ref):
        acc = g_ref[0]
        for dev in range(1, N_DEV):
            acc = acc + g_ref[dev]
        o_ref[...] = acc

    return pl.pallas_call(
        body, name="sum8", out_shape=_sds((r, LANES)), grid=(r // tr,),
        in_specs=[pl.BlockSpec((N_DEV, tr, LANES), lambda i: (0, i, 0))], out_specs=pl.BlockSpec((tr, LANES), lambda i: (i, 0)),
        compiler_params=_params(("parallel",)),
    )(gathered)


_SUM_STEPS = 2


def pair_sums(gs, recvs, core, name):
    n = len(gs)
    trs = [g.shape[1] // 2 // _SUM_STEPS for g in gs]

    def body(c_ref, *refs):
        del c_ref
        for i in range(n):
            a_ref, b_ref, o_ref = refs[2 * i], refs[2 * i + 1], refs[2 * n + i]
            o_ref[...] = (a_ref[...].astype(F32) + b_ref[...].astype(F32)).astype(BF16)

    in_specs, out_specs = [], []
    for g, tr in zip(gs, trs):
        cols = g.shape[2]
        in_specs.append(pl.BlockSpec((None, tr, cols), lambda k, s, c: (k, c[0] * _SUM_STEPS + s, 0)))
        in_specs.append(pl.BlockSpec((None, tr, cols), lambda k, s, c: (k, s, 0)))
        out_specs.append(pl.BlockSpec((None, tr, cols), lambda k, s, c: (k, s, 0)))
    grid_spec = pltpu.PrefetchScalarGridSpec(num_scalar_prefetch=1, grid=(N_CHIPS, _SUM_STEPS), in_specs=in_specs,
                                             out_specs=tuple(out_specs))
    return list(pl.pallas_call(
        body, name=name, out_shape=tuple(_sds((N_CHIPS, g.shape[1] // 2, g.shape[2]), BF16) for g in gs),
        grid_spec=grid_spec, compiler_params=_params(("parallel", "parallel")),
    )(core.reshape(1).astype(jnp.int32), *[t for pair in zip(gs, recvs) for t in pair]))


def chip_sums(pairs, recvs, chip, core, name):
    n = len(pairs)
    trs = [p.shape[1] // _SUM_STEPS for p in pairs]

    def body(p_ref, *refs):
        del p_ref
        for i in range(n):
            own_ref, r_ref, o_ref = refs[2 * i], refs[2 * i + 1], refs[2 * n + i]
            acc = own_ref[...].astype(F32)
            for j in range(N_CHIPS - 1):
                acc = acc + r_ref[j].astype(F32)
            o_ref[...] = acc

    in_specs, out_specs = [], []
    for p, tr in zip(pairs, trs):
        cols = p.shape[2]
        in_specs.append(pl.BlockSpec((None, tr, cols), lambda s, q: (q[0], s, 0)))
        in_specs.append(pl.BlockSpec((N_CHIPS - 1, tr, cols), lambda s, q: (0, s, 0)))
        out_specs.append(pl.BlockSpec((None, tr, cols), lambda s, q: (q[1], s, 0)))
    grid_spec = pltpu.PrefetchScalarGridSpec(num_scalar_prefetch=1, grid=(_SUM_STEPS,), in_specs=in_specs,
                                             out_specs=tuple(out_specs))
    return list(pl.pallas_call(
        body, name=name, out_shape=tuple(_sds((2,) + p.shape[1:]) for p in pairs), grid_spec=grid_spec,
        compiler_params=_params(("parallel",)),
    )(jnp.stack([chip, core]).astype(jnp.int32), *[t for pair in zip(pairs, recvs) for t in pair]))


def _place():
    return lax.axis_index("x"), lax.axis_index("y"), lax.axis_index("c")


def _other_chips(x, y):
    return [(x, 1 - y), (1 - x, y), (1 - x, 1 - y)]


_HBM = pl.BlockSpec(memory_space=pltpu.HBM)


def all_gather8(v, name, after=()):
    m, n = v.shape

    def body(x_ref, *refs):
        out_ref, send_sems, recv_sems, local_sem = refs[len(after):]
        x, y, c = _place()
        me, sibling = (x, y, c), (x, y, 1 - c)
        chips = _other_chips(x, y)

        def rows(px, py, pc):
            return out_ref.at[pl.ds((4 * px + 2 * py + pc) * m, m), :]

        def copy(k, block, to, src=None):
            return pltpu.make_async_remote_copy(
                src_ref=rows(*block) if src is None else src, dst_ref=rows(*block),
                send_sem=send_sems.at[k], recv_sem=recv_sems.at[k], device_id=to, device_id_type=MESH)

        mine = pltpu.make_async_copy(x_ref, rows(*me), local_sem)
        mine.start()
        first = [copy(0, me, sibling, src=x_ref)]
        first += [copy(1 + j, me, (*chip, c), src=x_ref) for j, chip in enumerate(chips)]
        for cp in first:
            cp.start()
        passed = [copy(4 + j, (*chip, c), sibling) for j, chip in enumerate(chips)]
        for j, chip in enumerate(chips):
            copy(1 + j, (*chip, c), me).wait_recv()
            passed[j].start()
        copy(0, sibling, me).wait_recv()
        for j, chip in enumerate(chips):
            copy(4 + j, (*chip, 1 - c), me).wait_recv()
        for cp in first + passed:
            cp.wait_send()
        mine.wait()

    return pl.pallas_call(
        body, name=name, out_shape=_sds((N_DEV * m, n), v.dtype),
        in_specs=[pl.BlockSpec(memory_space=pltpu.VMEM)] + [pl.BlockSpec(memory_space=pl.ANY)] * len(after),
        out_specs=pl.BlockSpec(memory_space=pltpu.VMEM),
        scratch_shapes=[pltpu.SemaphoreType.DMA((7,)), pltpu.SemaphoreType.DMA((7,)), pltpu.SemaphoreType.DMA],
        compiler_params=_params(None, VMEM_BIG),
    )(v, *after)


def _comm_call(body, name, ins, out_shapes, nsem, aliases=None):
    return pl.pallas_call(
        body, name=name, out_shape=tuple(out_shapes), in_specs=[_HBM] * len(ins), out_specs=tuple([_HBM] * len(out_shapes)),
        scratch_shapes=[pltpu.SemaphoreType.DMA((nsem,)), pltpu.SemaphoreType.DMA((nsem,))],
        input_output_aliases=aliases or {},
    )(*ins)


def _remote(src, dst, send_sems, recv_sems, k, to):
    return pltpu.make_async_remote_copy(src_ref=src, dst_ref=dst, send_sem=send_sems.at[k], recv_sem=recv_sems.at[k],
                                        device_id=to, device_id_type=MESH)


def _half(core, rh):
    return pl.ds(pl.multiple_of(core * rh, 16), rh)


def swap_halves(gs, name):
    n = len(gs)

    def body(*refs):
        ins, outs, (send_sems, recv_sems) = refs[:n], refs[n:2 * n], refs[2 * n:]
        x, y, c = _place()
        copies = []
        for i in range(n):
            theirs = _half(1 - c, ins[i].shape[1] // 2)
            cp = _remote(ins[i].at[:, theirs], outs[i], send_sems, recv_sems, i, (x, y, 1 - c))
            cp.start()
            copies.append(cp)
        for cp in copies:
            cp.wait()

    return _comm_call(body, name, gs, [_sds((g.shape[0], g.shape[1] // 2, g.shape[2]), g.dtype) for g in gs], n)


def join_halves(bufs, name):
    n = len(bufs)

    def body(*refs):
        ins, outs, (send_sems, recv_sems) = refs[:n], refs[n:2 * n], refs[2 * n:]
        x, y, c = _place()
        copies = []
        for i in range(n):
            cp = _remote(ins[i].at[c], outs[i].at[c], send_sems, recv_sems, i, (x, y, 1 - c))
            cp.start()
            copies.append(cp)
        for i in range(n):
            theirs = outs[i].at[1 - c]
            _remote(theirs, theirs, send_sems, recv_sems, i, (x, y, 1 - c)).wait_recv()
        for cp in copies:
            cp.wait_send()

    return _comm_call(body, name, bufs, [_sds(b.shape, b.dtype) for b in bufs], n, {i: i for i in range(n)})


def forward_halves(lands, name):
    n = len(lands)

    def body(*refs):
        ins, outs, (send_sems, recv_sems) = refs[:n], refs[n:2 * n], refs[2 * n:]
        x, y, c = _place()
        sibling = (x, y, 1 - c)
        chips = _other_chips(x, y)
        copies = []
        for i in range(n):
            mine = _half(c, ins[i].shape[1] // 2)
            for j, (px, py) in enumerate(chips):
                cp = _remote(ins[i].at[2 * px + py, mine], outs[i].at[2 * px + py, mine], send_sems, recv_sems, 3 * i + j, sibling)
                cp.start()
                copies.append(cp)
        for i in range(n):
            theirs = _half(1 - c, ins[i].shape[1] // 2)
            for j, (px, py) in enumerate(chips):
                landed = outs[i].at[2 * px + py, theirs]
                _remote(landed, landed, send_sems, recv_sems, 3 * i + j, sibling).wait_recv()
        for cp in copies:
            cp.wait_send()

    return _comm_call(body, name, lands, [_sds(b.shape, b.dtype) for b in lands], 3 * n, {i: i for i in range(n)})


_SEM = pl.BlockSpec(memory_space=pltpu.SEMAPHORE)
_EFFECT = pltpu.SideEffectType.DATAFLOW_SIDE_EFFECTING


def _gather_copies(srcs, lands, send_sems, recv_sems):
    x, y, c = _place()
    copies = []
    for i in range(len(srcs)):
        mine = _half(c, srcs[i].shape[0] // 2)
        for j, chip in enumerate(_other_chips(x, y)):
            copies.append(_remote(srcs[i].at[mine], lands[i].at[2 * x + y, mine], send_sems, recv_sems, 3 * i + j, (*chip, c)))
    return copies


def _exchange_copies(srcs, lands, send_sems, recv_sems):
    x, y, c = _place()
    copies = []
    for i in range(len(srcs)):
        for j, (px, py) in enumerate(_other_chips(x, y)):
            copies.append(_remote(srcs[i].at[2 * px + py], lands[i].at[j], send_sems, recv_sems, 3 * i + j, (px, py, c)))
    return copies


def _everyone_copies(srcs, lands, send_sems, recv_sems):
    x, y, c = _place()
    flip = lambda v, b: 1 - v if b else v
    dst = lands[0].at[4 * x + 2 * y + c]
    return [_remote(srcs[0], dst, send_sems, recv_sems, j - 1, (flip(x, j & 4), flip(y, j & 2), flip(c, j & 1)))
            for j in range(1, N_DEV)]


GATHER = (_gather_copies, 3)
EXCHANGE = (_exchange_copies, 3)
EVERYONE = (_everyone_copies, N_DEV - 1)


def split_start(name, plan, srcs, land_shapes, after=()):
    copies_fn, per_source = plan
    n, m, k = len(srcs), len(land_shapes), len(after)
    ncopies = per_source * n

    def body(*refs):
        src_refs, land_refs = refs[:n], refs[n:n + m]
        send_sems, recv_sems = refs[n + m + k], refs[n + m + k + 1]
        token = refs[-1]
        for cp in copies_fn(src_refs, land_refs, send_sems, recv_sems):
            cp.start()
        token[...] = jnp.zeros_like(token)

    hbm = lambda s: pltpu.HBM(tuple(s.shape), s.dtype)
    outs = pl.pallas_call(
        body, name=name,
        out_shape=(pltpu.SemaphoreType.DMA((ncopies,)), pltpu.SemaphoreType.DMA((ncopies,)), *[hbm(s) for s in srcs],
                   *[hbm(s) for s in land_shapes], _sds((8, LANES))),
        in_specs=[_HBM] * (n + m) + [pl.BlockSpec(memory_space=pl.ANY)] * k,
        out_specs=(_SEM, _SEM, *([_HBM] * (n + m)), pl.BlockSpec(memory_space=pltpu.VMEM)),
        input_output_aliases={i: 2 + i for i in range(n + m)},
        compiler_params=pltpu.CompilerParams(has_side_effects=_EFFECT),
    )(*[pltpu.with_memory_space_constraint(s, pltpu.HBM) for s in srcs],
      *[pltpu.with_memory_space_constraint(lax.empty(tuple(s.shape), s.dtype), pltpu.HBM) for s in land_shapes], *after)
    handle = (outs[0], outs[1], list(outs[2:2 + n]), list(outs[2 + n:2 + n + m]))
    return handle, outs[-1][0, 0]


def split_wait(name, plan, handle, after):
    copies_fn, _ = plan
    send_sems, recv_sems, srcs, lands = handle
    n, m = len(srcs), len(lands)
    after = list(after) if isinstance(after, (list, tuple)) else [after]

    def body(*refs):
        src_refs, land_refs = refs[:n], refs[n:n + m]
        for cp in copies_fn(src_refs, land_refs, refs[n + m], refs[n + m + 1]):
            cp.wait_send()
            cp.wait_recv()

    hbm = lambda s: pltpu.HBM(tuple(s.shape), s.dtype)
    outs = pl.pallas_call(
        body, name=name, out_shape=tuple(hbm(s) for s in srcs + lands),
        in_specs=[_HBM] * (n + m) + [_SEM, _SEM] + [pl.BlockSpec(memory_space=pl.ANY)] * len(after),
        out_specs=tuple([_HBM] * (n + m)), input_output_aliases={i: i for i in range(n + m)},
        compiler_params=pltpu.CompilerParams(has_side_effects=_EFFECT),
    )(*srcs, *lands, send_sems, recv_sems, *after)
    return list(outs[:n]), list(outs[n:])


def chip_major(w, groups=N_CHIPS):
    r, c = w.shape
    return w.reshape(r, groups, c // groups).transpose(1, 0, 2)


def from_chip_major(w):
    g, r, c = w.shape
    return w.transpose(1, 0, 2).reshape(r, g * c)


def _cd_in_pad(w):
    a = C_Q_RANK + C_KV_RANK
    z = lambda n: jnp.zeros((w.shape[0], n), w.dtype)
    return jnp.concatenate([w[:, :a], z(C_NOPE), w[:, a:a + C_ROPE], z(HEAD_PAD - C_NOPE - C_ROPE), w[:, a + C_ROPE:]], axis=1)


def _cd_in_unpad(w):
    a = C_Q_RANK + C_KV_RANK
    return jnp.concatenate([w[:, :a], w[:, a + C_NOPE:a + C_NOPE + C_ROPE], w[:, a + HEAD_PAD:]], axis=1)


def _pad_heads(w, width):
    r = w.shape[0]
    w = w.reshape(r, C_HEADS, width)
    return jnp.pad(w, ((0, 0), (0, 0), (0, HEAD_PAD - width))).reshape(r, _HW)


def _unpad_heads(w, width):
    r = w.shape[0]
    return w.reshape(r, C_HEADS, HEAD_PAD)[:, :, :width].reshape(r, C_HEADS * width)


def prepare_weights(p):
    q = dict(p)
    q["cd_w_in"] = _cd_in_pad(p["cd_w_in"])
    q["c_w_uq"] = _pad_heads(p["c_w_uq"], C_NOPE + C_ROPE)
    ukv = p["c_w_ukv"].reshape(C_KV_RANK, C_HEADS, C_NOPE + C_V)
    q["c_w_uk"] = _pad_heads(ukv[:, :, :C_NOPE].reshape(C_KV_RANK, -1), C_NOPE)
    q["c_w_uv"] = _pad_heads(ukv[:, :, C_NOPE:].reshape(C_KV_RANK, -1), C_V)
    wo = p["cd_w_out"]
    att_rows = jnp.pad(wo[:C_HEADS * C_V].reshape(C_HEADS, C_V, D_MODEL), ((0, 0), (0, HEAD_PAD - C_V), (0, 0)))
    q["cd_w_out"] = jnp.concatenate([att_rows.reshape(_HW, D_MODEL), wo[C_HEADS * C_V:]], axis=0)
    return q


def unprepare_grads(g):
    q = dict(g)
    q["cd_w_in"] = _cd_in_unpad(g["cd_w_in"])
    q["c_w_uq"] = _unpad_heads(g["c_w_uq"], C_NOPE + C_ROPE)
    uk = g.pop("c_w_uk").reshape(C_KV_RANK, C_HEADS, HEAD_PAD)[:, :, :C_NOPE]
    uv = g.pop("c_w_uv").reshape(C_KV_RANK, C_HEADS, HEAD_PAD)[:, :, :C_V]
    q.pop("c_w_uk", None)
    q.pop("c_w_uv", None)
    q["c_w_ukv"] = jnp.concatenate([uk, uv], axis=-1).reshape(C_KV_RANK, C_HEADS * (C_NOPE + C_V))
    wo = g["cd_w_out"]
    att = wo[:_HW].reshape(C_HEADS, HEAD_PAD, D_MODEL)[:, :C_V].reshape(C_HEADS * C_V, D_MODEL)
    q["cd_w_out"] = jnp.concatenate([att, wo[_HW:]], axis=0)
    return q


def rope_tables(positions):
    half = C_ROPE // 2
    inv_freq = ROPE_THETA ** (-jnp.arange(half, dtype=F32) / half)
    ang = positions.astype(F32)[:, None] * inv_freq
    cos, sin = jnp.cos(ang), jnp.sin(ang)
    s = positions.shape[0]
    z = lambda n: jnp.zeros((s, n), F32)
    cs = jnp.concatenate([jnp.ones((s, C_NOPE), F32), cos, cos, z(HEAD_PAD - C_NOPE - C_ROPE)], axis=1)
    s1 = jnp.concatenate([z(C_NOPE), -sin, z(HEAD_PAD - C_NOPE - half)], axis=1)
    s2 = jnp.concatenate([z(C_NOPE + half), sin, z(HEAD_PAD - C_NOPE - C_ROPE)], axis=1)
    return cs, s1, s2


_UP_COLS = 2 * D_FF // N_CHIPS


def ffn_fwd(h2, w, l, late_down=None):
    zf = matmul(h2, w["ffn_w_up"][l], "nn", BF16, f"ffn_up{l}", gb=N_CHIPS, go=2, tn=_UP_COLS)
    if late_down is not None:
        late_down(zf)
    a, f = ffn_act_down(zf, w["ffn_conv_w"][l], w["ffn_w_down"][l], f"ffn_act_down{l}")
    return f, (zf, a)


def ffn_bwd(df, h2, saved, w, l):
    zf, a = saved
    da = matmul(df, w["ffn_w_down"][l], "nt", BF16, f"ffn_down_dx{l}", tn=D_FF // 2)
    d_down = matmul(a, df, "tn", BF16, f"ffn_down_dw{l}", tm=D_FF // 2)
    dzf, d_conv = ffn_act_bwd(zf, w["ffn_conv_w"][l], da, f"ffn_act_bwd{l}")
    dh2 = matmul(dzf, w["ffn_w_up"][l], "nt", BF16, f"ffn_up_dx{l}", ga=2, gb=N_CHIPS, tk=_UP_COLS, tn=D_MODEL)
    d_up = matmul(h2, dzf, "tn", BF16, f"ffn_up_dw{l}", gb=2, go=N_CHIPS, tn=_UP_COLS)
    d_conv = d_conv.transpose(1, 0, 2).reshape(3, 2 * D_FF)
    return dh2, dict(ffn_w_down=d_down, ffn_conv_w=d_conv, ffn_w_up=d_up)


def mixer0_fwd(h, w):
    z = matmul(h, w["ab_w_in"], "nn", BF16, "ab_in", gb=N_CHIPS)
    ycat = pool_fwd(z, w["b_mix_w"], w["b_scale"], gconv_fwd(z, w["a_conv_w"]))
    y = matmul(ycat, w["ab_w_out"], "nn", BF16, "ab_out", tn=D_MODEL)
    return y, (z, ycat)


def mixer0_bwd(dy, h, saved, w):
    z, ycat = saved
    grads = {}
    dycat = matmul(dy, w["ab_w_out"], "nt", BF16, "ab_out_dx")
    grads["ab_w_out"] = matmul(ycat, dy, "tn", BF16, "ab_out_dw")
    db, dc, da, d_conv = gconv_bwd(z, w["a_conv_w"], dycat)
    dp, d_mix, d_scale = pool_bwd(z, w["b_mix_w"], w["b_scale"], dycat)
    dz = jnp.concatenate([db, dc, da, dp], axis=1)
    dh = matmul(dz, w["ab_w_in"], "nt", BF16, "ab_in_dx", gb=N_CHIPS, tn=D_MODEL)
    grads["ab_w_in"] = matmul(h, dz, "tn", BF16, "ab_in_dw", go=N_CHIPS)
    grads.update(a_conv_w=d_conv, b_mix_w=d_mix, b_scale=d_scale)
    return dh, grads


def mixer1_fwd(h, ropes, w):
    cs, s1, s2 = ropes
    z = matmul(h, w["cd_w_in"], "nn", BF16, "cd_in")
    bs_t = jnp.pad(w["d_b_s"].T, ((0, 0), (0, LANES - D_GROUPS)))
    qh, kh, vh = mla_pre_fwd(z, w["c_q_norm_g"], w["c_kv_norm_g"], w["c_w_uq"], w["c_w_uk"], w["c_w_uv"], cs, s1, s2)
    ycat = sgu_fwd(z, w["d_ln_g"], w["d_ln_b"], w["d_w_s"], bs_t, attn_fwd(qh, kh, vh))
    y = matmul(ycat, w["cd_w_out"], "nn", BF16, "cd_out", tn=D_MODEL)
    return y, (z, bs_t, qh, kh, vh, ycat)


def mixer1_bwd(dy, h, saved, ropes, w):
    cs, s1, s2 = ropes
    z, bs_t, qh, kh, vh, ycat = saved
    grads = {}
    dycat = matmul(dy, w["cd_w_out"], "nt", BF16, "cd_out_dx")
    grads["cd_w_out"] = matmul(ycat, dy, "tn", F32, "cd_out_dw")
    dqh, dkh, dvh = attn_bwd(qh, kh, vh, ycat, dycat)
    dzq, d_uq, d_uk, d_uv, d_gq, d_gkv = mla_pre_bwd(
        z, w["c_q_norm_g"], w["c_kv_norm_g"], w["c_w_uq"], w["c_w_uk"], w["c_w_uv"], cs, s1, s2, dqh, dkh, dvh)
    dzu, dzv, d_ws, d_bs, d_lg, d_lb = sgu_bwd(z, w["d_ln_g"], w["d_ln_b"], w["d_w_s"], bs_t, dycat, _HW // _DW)
    dz = jnp.concatenate([dzq, dzu, dzv], axis=1)
    dh = matmul(dz, w["cd_w_in"], "nt", BF16, "cd_in_dx", tn=D_MODEL)
    grads["cd_w_in"] = matmul(h, dz, "tn", F32, "cd_in_dw")
    grads.update(c_w_uq=d_uq, c_w_uk=d_uk, c_w_uv=d_uv, c_q_norm_g=d_gq, c_kv_norm_g=d_gkv, d_w_s=d_ws,
                 d_b_s=d_bs[:, :D_GROUPS].T, d_ln_g=d_lg, d_ln_b=d_lb)
    return dh, grads


class StepHooks:
    def weights(self, stage, after):
        pass

    def gradients(self, stage, grads, after):
        return 0.0


def run_step(x, tgt, mod, ropes, w, hooks):
    sh1, sc1, g1, sh2, sc2, g2 = range(N_MOD)
    mods = mod.reshape(2, 1, N_MOD * D_MODEL)
    n1 = w["norm1_g"].reshape(2, 1, D_MODEL)
    n2 = w["norm2_g"].reshape(2, 1, D_MODEL)
    final_g = Vec(w["final_norm_g"].reshape(1, 1, D_MODEL), 0, 0)

    hooks.weights("mix0", mod)
    h0 = modnorm_fwd(x, Vec(n1, 0, 0), Vec(mods, 0, sc1), Vec(mods, 0, sh1), "modnorm_0")
    y0, mix0 = mixer0_fwd(h0, w)
    x1, h1 = resid_modnorm_fwd(x, y0, Vec(mods, 0, g1), Vec(n2, 0, 0), Vec(mods, 0, sc2), Vec(mods, 0, sh2), "resid_modnorm_1")
    hooks.weights("up0", x1)
    f0, ffn0 = ffn_fwd(h1, w, 0, lambda act: hooks.weights("down0", act))
    x2, h2 = resid_modnorm_fwd(x1, f0, Vec(mods, 0, g2), Vec(n1, 1, 0), Vec(mods, 1, sc1), Vec(mods, 1, sh1), "resid_modnorm_2")
    hooks.weights("mix1", x2)
    y1, mix1 = mixer1_fwd(h2, ropes, w)
    x3, h3 = resid_modnorm_fwd(x2, y1, Vec(mods, 1, g1), Vec(n2, 1, 0), Vec(mods, 1, sc2), Vec(mods, 1, sh2), "resid_modnorm_3")
    hooks.weights("ffn1", x3)
    f1, ffn1 = ffn_fwd(h3, w, 1)
    dres, d_final, loss, df1, dg2b = final_fused(x3, f1, Vec(mods, 1, g2), final_g, tgt)

    dh3, gf1 = ffn_bwd(df1, h3, ffn1, w, 1)
    late = mods + hooks.gradients("ffn1", gf1, dh3)
    dres, dsh2b, dsc2b, dn2b, dy1, dg1b = norm_gate_bwd(
        x3, dh3, Vec(n2, 1, 0), Vec(late, 1, sc2), dres, y1, Vec(late, 1, g1), "norm_gate_bwd_3")
    dh2, gm1 = mixer1_bwd(dy1, h2, mix1, ropes, w)
    late = mods + hooks.gradients("mix1", gm1, dh2)
    dres, dsh1b, dsc1b, dn1b, df0, dg2a = norm_gate_bwd(
        x2, dh2, Vec(n1, 1, 0), Vec(late, 1, sc1), dres, f0, Vec(late, 0, g2), "norm_gate_bwd_2")
    dh1, gf0 = ffn_bwd(df0, h1, ffn0, w, 0)
    late = mods + hooks.gradients("ffn0", gf0, dh1)
    dres, dsh2a, dsc2a, dn2a, dy0, dg1a = norm_gate_bwd(
        x1, dh1, Vec(n2, 0, 0), Vec(late, 0, sc2), dres, y0, Vec(late, 0, g1), "norm_gate_bwd_1")
    dh0, gm0 = mixer0_bwd(dy0, h0, mix0, w)
    late = mods + hooks.gradients("mix0", gm0, dh0)
    grad_x, dsh1a, dsc1a, dn1a = norm_bwd(x, dh0, Vec(n1, 0, 0), Vec(late, 0, sc1), dres, "norm_bwd_0")

    dmod = jnp.concatenate([jnp.concatenate([dsh1a, dsc1a, dg1a, dsh2a, dsc2a, dg2a], axis=1),
                            jnp.concatenate([dsh1b, dsc1b, dg1b, dsh2b, dsc2b, dg2b], axis=1)], axis=0)
    norms = dict(norm1_g=jnp.concatenate([dn1a, dn1b], axis=0), norm2_g=jnp.concatenate([dn2a, dn2b], axis=0),
                 final_norm_g=d_final)
    return loss, grad_x, dmod, dict(mix0=gm0, ffn0=gf0, mix1=gm1, ffn1=gf1, norms=norms)


def merge_grads(by_stage):
    grads = {**by_stage["mix0"], **by_stage["mix1"], **by_stage["norms"]}
    for k in ("ffn_w_down", "ffn_w_up"):
        grads[k] = [by_stage["ffn0"][k], by_stage["ffn1"][k]]
    grads["ffn_conv_w"] = jnp.stack([by_stage["ffn0"]["ffn_conv_w"], by_stage["ffn1"]["ffn_conv_w"]])
    return grads


_WEIGHTS = ("ada_w", "ada_b", "norm1_g", "norm2_g", "ab_w_in", "a_conv_w", "b_mix_w", "b_scale", "ab_w_out", "cd_w_in",
            "c_q_norm_g", "c_w_uq", "c_kv_norm_g", "c_w_ukv", "d_ln_g", "d_ln_b", "d_w_s", "d_b_s", "cd_w_out",
            "ffn_w_up", "ffn_conv_w", "ffn_w_down", "final_norm_g")
_INPUTS = ("x", "c", "positions") + _WEIGHTS + ("loss_target",) + tuple("m_" + n for n in _WEIGHTS) + tuple(
    "v_" + n for n in _WEIGHTS)

def _pack_rows(parts, rows, dtype):
    flat = jnp.concatenate([p.reshape(-1).astype(dtype) for p in parts])
    return jnp.pad(flat, (0, rows * LANES - flat.shape[0])).reshape(rows, LANES)


def _rows_major(w):
    r, c = w.shape
    return w.reshape(N_CHIPS, r // N_CHIPS, c)


def start_gather(shards, tag, after=()):
    lands = [_sds((N_CHIPS,) + s.shape, s.dtype) for s in shards]
    return split_start("gather_start_" + tag, GATHER, shards, lands, after)


def finish_gather(handle, chip, tag, after):
    shards, lands = split_wait("gather_wait_" + tag, GATHER, handle, after)
    lands = forward_halves(lands, "gather_forward_" + tag)
    return [lax.dynamic_update_index_in_dim(o, s, chip, 0) for o, s in zip(lands, shards)]


def start_reduce(gs, core, tag):
    recv = swap_halves(gs, "swap_halves_" + tag)
    pairs = pair_sums(gs, recv, core, "pair_sums_" + tag)
    lands = [_sds((N_CHIPS - 1,) + p.shape[1:], p.dtype) for p in pairs]
    return split_start("exchange_start_" + tag, EXCHANGE, pairs, lands)


def finish_reduce(handle, chip, core, tag, after):
    pairs, others = split_wait("exchange_wait_" + tag, EXCHANGE, handle, after)
    halves = chip_sums(pairs, others, chip, core, "chip_sums_" + tag)
    full = join_halves(halves, "join_halves_" + tag)
    return [f.reshape(f.shape[1] * 2, f.shape[2]) for f in full]


_SMALL_SHARDED = (("a_conv_w", (3, 128), 1), ("c_q_norm_g", (1, 64), 1), ("d_ln_g", (1, 128), 1), ("d_ln_b", (1, 128), 1),
                  ("ffn_conv_w", (2, 3, 2 * D_FF // N_CHIPS), 2))
_SMALL_GRADS = (("norm1_g", (2, D_MODEL)), ("norm2_g", (2, D_MODEL)), ("b_mix_w", (4, 128, 128)), ("b_scale", (1, 512)),
                ("c_kv_norm_g", (1, 128)), ("d_w_s", (4, 128, 128)), ("d_b_s", (4, 128)), ("final_norm_g", (1, D_MODEL)),
                ("a_conv_w", (3, 512)), ("c_q_norm_g", (1, 256)), ("d_ln_g", (1, 512)), ("d_ln_b", (1, 512)),
                ("ffn_conv_w", (2, 3, 2 * D_FF)))


def _size(shape):
    n = 1
    for d in shape:
        n *= d
    return n


def kernel(x, c, positions, ada_w, ada_b, norm1_g, norm2_g, ab_w_in, a_conv_w, b_mix_w, b_scale, ab_w_out, cd_w_in, c_q_norm_g, c_w_uq, c_kv_norm_g, c_w_ukv, d_ln_g, d_ln_b, d_w_s, d_b_s, cd_w_out, ffn_w_up, ffn_conv_w, ffn_w_down, final_norm_g, loss_target, m_ada_w, m_ada_b, m_norm1_g, m_norm2_g, m_ab_w_in, m_a_conv_w, m_b_mix_w, m_b_scale, m_ab_w_out, m_cd_w_in, m_c_q_norm_g, m_c_w_uq, m_c_kv_norm_g, m_c_w_ukv, m_d_ln_g, m_d_ln_b, m_d_w_s, m_d_b_s, m_cd_w_out, m_ffn_w_up, m_ffn_conv_w, m_ffn_w_down, m_final_norm_g, v_ada_w, v_ada_b, v_norm1_g, v_norm2_g, v_ab_w_in, v_a_conv_w, v_b_mix_w, v_b_scale, v_ab_w_out, v_cd_w_in, v_c_q_norm_g, v_c_w_uq, v_c_kv_norm_g, v_c_w_ukv, v_d_ln_g, v_d_ln_b, v_d_w_s, v_d_b_s, v_cd_w_out, v_ffn_w_up, v_ffn_conv_w, v_ffn_w_down, v_final_norm_g):
    args = (x, c, positions, ada_w, ada_b, norm1_g, norm2_g, ab_w_in, a_conv_w, b_mix_w, b_scale, ab_w_out, cd_w_in, c_q_norm_g, c_w_uq, c_kv_norm_g, c_w_ukv, d_ln_g, d_ln_b, d_w_s, d_b_s, cd_w_out, ffn_w_up, ffn_conv_w, ffn_w_down, final_norm_g, loss_target, m_ada_w, m_ada_b, m_norm1_g, m_norm2_g, m_ab_w_in, m_a_conv_w, m_b_mix_w, m_b_scale, m_ab_w_out, m_cd_w_in, m_c_q_norm_g, m_c_w_uq, m_c_kv_norm_g, m_c_w_ukv, m_d_ln_g, m_d_ln_b, m_d_w_s, m_d_b_s, m_cd_w_out, m_ffn_w_up, m_ffn_conv_w, m_ffn_w_down, m_final_norm_g, v_ada_w, v_ada_b, v_norm1_g, v_norm2_g, v_ab_w_in, v_a_conv_w, v_b_mix_w, v_b_scale, v_ab_w_out, v_cd_w_in, v_c_q_norm_g, v_c_w_uq, v_c_kv_norm_g, v_c_w_ukv, v_d_ln_g, v_d_ln_b, v_d_w_s, v_d_b_s, v_cd_w_out, v_ffn_w_up, v_ffn_conv_w, v_ffn_w_down, v_final_norm_g)
    a = dict(zip(_INPUTS, args, strict=True))
    xi, yi, ci = _place()
    chip = 2 * xi + yi
    dev = 4 * xi + 2 * yi + ci
    x = a["x"][0]
    tgt = a["loss_target"][0]

    bf = lambda t: t.astype(BF16)
    mix0_handle, tok = start_gather([bf(a["ab_w_in"][0]), bf(a["ab_w_out"][0])], "mix0")
    up0_16, down0_16, up1_16, down1_16 = [bf(a[n][l]) for l in (0, 1) for n in ("ffn_w_up", "ffn_w_down")]
    mix1_16 = [bf(a[n][0]) for n in ("cd_w_in", "c_w_uq", "c_w_ukv", "cd_w_out")]

    small_parts = [a["c"] + tok] + [a[n] for n, _, _ in _SMALL_SHARDED]
    rows1 = -(-sum(p.size for p in small_parts) // LANES // 8) * 8
    g1 = all_gather8(_pack_rows(small_parts, rows1, F32), "gather_small",
                     [up0_16, down0_16, up1_16, down1_16, mix1_16[0], mix1_16[3]]).reshape(N_DEV, rows1 * LANES)
    c_all = g1[:, :D_MODEL]
    per_chip = g1[0::2]
    small_full = {}
    off = D_MODEL
    for n, shp, axis in _SMALL_SHARDED:
        piece = per_chip[:, off:off + _size(shp)].reshape((N_CHIPS,) + shp)
        small_full[n] = jnp.concatenate([piece[k] for k in range(N_CHIPS)], axis=axis)
        off += _size(shp)

    merge = lambda t: t.reshape(t.shape[0] * t.shape[1], t.shape[2])
    w = dict(norm1_g=a["norm1_g"], norm2_g=a["norm2_g"], b_mix_w=a["b_mix_w"][0], b_scale=a["b_scale"],
             c_kv_norm_g=a["c_kv_norm_g"], d_w_s=a["d_w_s"][0], d_b_s=a["d_b_s"][0],
             final_norm_g=a["final_norm_g"].reshape(1, D_MODEL), **small_full)

    ncol = N_MOD * D_MODEL // N_CHIPS
    ada_b_mine = lax.dynamic_slice_in_dim(a["ada_b"], chip * ncol, ncol, axis=1)
    mod_cols = ada_mod(c_all, a["ada_w"], ada_b_mine)
    g2_rows = all_gather8(mod_cols.reshape(-1, LANES), "gather_mod")
    g2 = g2_rows.reshape(N_DEV, 2, N_DEV, ncol)
    mod = lax.dynamic_index_in_dim(g2[0::2], dev, axis=2, keepdims=False)
    mod = mod.transpose(1, 0, 2).reshape(2, N_MOD * D_MODEL)

    late = [g2_rows]
    up0_handle, tok_a = start_gather([up0_16], "up0", late)
    down0_handle, tok_b = start_gather([down0_16], "down0", late)
    mix1_handle, tok_c = start_gather(mix1_16, "mix1", late)
    ffn1_handle, tok_d = start_gather([up1_16, down1_16], "ffn1", late)
    mod = mod + (tok_a + tok_b + tok_c + tok_d)

    ropes = rope_tables(a["positions"][0])
    cm16 = lambda t: chip_major(t).astype(BF16)
    w.update(ffn_w_up=[None, None], ffn_w_down=[None, None])
    handles = dict(mix0=mix0_handle, up0=up0_handle, down0=down0_handle, mix1=mix1_handle, ffn1=ffn1_handle)
    reducing, reduced = {}, {}

    class Hooks(StepHooks):
        def weights(self, stage, after):
            got = finish_gather(handles[stage], chip, stage, after)
            if stage == "mix0":
                w.update(ab_w_in=got[0], ab_w_out=merge(got[1]))
            elif stage == "up0":
                w["ffn_w_up"][0] = got[0]
            elif stage == "down0":
                w["ffn_w_down"][0] = merge(got[0])
            elif stage == "mix1":
                cd_in, uq, ukv, cd_out = got
                w.update(prepare_weights(dict(cd_w_in=from_chip_major(cd_in), c_w_uq=from_chip_major(uq),
                                              c_w_ukv=from_chip_major(ukv), cd_w_out=merge(cd_out))))
            else:
                w["ffn_w_up"][1], w["ffn_w_down"][1] = got[0], merge(got[1])

        def gradients(self, stage, grads, after):
            if stage in ("ffn0", "ffn1"):
                parts = [grads["ffn_w_up"], _rows_major(grads["ffn_w_down"])]
            elif stage == "mix1":
                grads.update(unprepare_grads(grads))
                parts = [cm16(grads["cd_w_in"]), cm16(grads["c_w_uq"]), cm16(grads["c_w_ukv"]),
                         _rows_major(grads["cd_w_out"]).astype(BF16)]
            else:
                parts = [grads["ab_w_in"], _rows_major(grads["ab_w_out"])]
            reducing[stage], tok = start_reduce(parts, ci, stage)
            before = {"mix1": "ffn1", "ffn0": "mix1", "mix0": "ffn0"}.get(stage)
            if before is not None:
                reduced[before] = finish_reduce(reducing[before], chip, ci, before, after)
            return tok

    loss, grad_x, dmod, by_stage = run_step(x, tgt, mod, ropes, w, Hooks())
    grads = merge_grads(by_stage)

    parts3 = [dmod] + [grads[n] for n, _ in _SMALL_GRADS] + [loss[0, 0]]
    rows3 = -(-sum(p.size for p in parts3) // LANES // 8) * 8
    small_handle, _ = split_start("small_grads_start", EVERYONE, [_pack_rows(parts3, rows3, F32)],
                                  [_sds((N_DEV, rows3, LANES))])
    red_up1, red_down1 = reduced["ffn1"]
    red_cd_in, red_uq, red_ukv, red_cd_out = reduced["mix1"]
    red_up0, red_down0 = reduced["ffn0"]
    out_grads = dict(cd_w_in=red_cd_in, c_w_uq=red_uq, c_w_ukv=red_ukv, cd_w_out=red_cd_out)
    per_layer = dict(ffn_w_up=(red_up0, red_up1), ffn_w_down=(red_down0, red_down1))
    updates = {}

    def update(n):
        if n in per_layer:
            updates[n] = adamw_layers(a[n], *per_layer[n], a["m_" + n], a["v_" + n], "adamw_" + n)
        else:
            updates[n] = adamw(a[n], out_grads[n].reshape(a[n].shape), a["m_" + n], a["v_" + n], "adamw_" + n)

    early =("ffn_w_up", "ffn_w_down", "cd_w_in", "c_w_uq", "c_w_ukv", "cd_w_out")
    for n in early:
        update(n)
    (mine,), (landed,) = split_wait("small_grads_wait", EVERYONE, small_handle, [updates[n][1] for n in early])
    g3 = lax.dynamic_update_index_in_dim(landed, mine, dev, 0)
    summed = sum8(g3).reshape(-1)
    nmod = 2 * N_MOD * D_MODEL
    out_grads["ada_b"] = summed[:nmod].reshape(2, N_MOD * D_MODEL)
    off = nmod
    for n, shp in _SMALL_GRADS:
        out_grads[n] = summed[off:off + _size(shp)].reshape(shp)
        off += _size(shp)
    loss = summed[off]
    for n, shp, axis in _SMALL_SHARDED:
        width = out_grads[n].shape[-1] // N_CHIPS
        out_grads[n] = lax.dynamic_slice_in_dim(out_grads[n], chip * width, width, axis=out_grads[n].ndim - 1)
    dmod_all = g3.reshape(N_DEV, rows3 * LANES)[:, :nmod].reshape(N_DEV, 2, N_MOD * D_MODEL)
    dmod_mine = lax.dynamic_slice_in_dim(dmod_all, chip * ncol, ncol, axis=2).transpose(1, 0, 2)
    updates["ada_w"] = adamw_ada(a["ada_w"], c_all, dmod_mine, a["m_ada_w"], a["v_ada_w"])

    red_in0, red_out0 = finish_reduce(reducing["mix0"], chip, ci, "mix0", updates["ada_w"][1])
    out_grads.update(ab_w_in=red_in0, ab_w_out=red_out0)

    for n in ("ab_w_in", "ab_w_out"):
        update(n)
    small = [n for n in _WEIGHTS if n not in updates]
    for n, res in zip(small, adamw_small([a[n] for n in small], [out_grads[n].reshape(a[n].shape) for n in small],
                                         [a["m_" + n] for n in small], [a["v_" + n] for n in small])):
        updates[n] = res
    return (loss, grad_x[None], *[updates[n][i] for i in range(4) for n in _WEIGHTS])
```

```python
import functools
from typing import NamedTuple

import jax
import jax.numpy as jnp
from jax import lax
from jax.experimental import pallas as pl
from jax.experimental.pallas import tpu as pltpu

F32 = jnp.float32
BF16 = jnp.bfloat16
EPS = 1e-6
D_MODEL = 1024
N_MOD = 6
A_WIDTH = 512
B_GROUPS = 4
C_HEADS = 8
C_NOPE = 64
C_ROPE = 32
C_V = 64
C_Q_RANK = 256
C_KV_RANK = 128
HEAD_PAD = 128
ROPE_THETA = 10000.0
D_GROUPS = 4
D_CHUNK = 128
D_FF = 2816
FF_UNIT = 128
ADAM_LR = 0.001
ADAM_B1 = 0.9
ADAM_B2 = 0.999
ADAM_EPS = 1e-08
ADAM_WD = 0.01
ADAM_STEP = 10
N_CHIPS = 4
N_DEV = 8
LANES = 128
VMEM_BIG = 56 * 1024 * 1024
MESH = pl.DeviceIdType.MESH


def _sds(shape, dtype=F32):
    return jax.ShapeDtypeStruct(tuple(shape), dtype)


def _tile(n, cap, mult=128):
    if n <= cap:
        return n
    best = None
    for t in range(mult, cap + 1, mult):
        if n % t == 0:
            best = t
    assert best is not None, (n, cap, mult)
    return best


def _params(dims=None, vmem=None):
    return pltpu.CompilerParams(dimension_semantics=dims, vmem_limit_bytes=vmem)


def _shift_down(v, k):
    r = pltpu.roll(v, k, axis=0)
    t = lax.broadcasted_iota(jnp.int32, v.shape, 0)
    return jnp.where(t >= k, r, 0.0)


def _shift_up(v, k):
    n = v.shape[0]
    r = pltpu.roll(v, n - k, axis=0)
    t = lax.broadcasted_iota(jnp.int32, v.shape, 0)
    return jnp.where(t < n - k, r, 0.0)


def _sigmoid(v):
    return 1.0 / (1.0 + jnp.exp(-v))


_GELU_C = 0.7978845608028654
_GELU_A = 0.044715


def _gelu(v):
    return 0.5 * v * (1.0 + jnp.tanh(_GELU_C * (v + _GELU_A * v * v * v)))


def _gelu_grad(v):
    th = jnp.tanh(_GELU_C * (v + _GELU_A * v * v * v))
    return 0.5 * (1.0 + th) + 0.5 * v * (1.0 - th * th) * _GELU_C * (1.0 + 3.0 * _GELU_A * v * v)


_NN = (((1,), (0,)), ((), ()))
_NT = (((1,), (1,)), ((), ()))
_TN = (((0,), (0,)), ((), ()))


def _dot(a, b, dims=_NN):
    return lax.dot_general(a, b, dims, preferred_element_type=F32)


def _logical(t, groups):
    return (t.shape[-2], t.shape[-1] * groups)


def _block(tr, tc, groups, cols, where):
    if groups == 1:
        return pl.BlockSpec((tr, tc), where)
    per = cols // groups // tc

    def index(i, j, s):
        r, c = where(i, j, s)
        return (c // per, r, c % per)

    return pl.BlockSpec((None, tr, tc), index)


def matmul(a, b, mode, out_dtype, name, ga=1, gb=1, go=1, tm=None, tn=None, tk=None):
    (ar, ac), (br, bc) = _logical(a, ga), _logical(b, gb)
    if mode == "nn":
        m, k, n = ar, ac, bc
        a_col, b_col = "k", "n"
    elif mode == "nt":
        m, k, n = ar, ac, br
        a_col, b_col = "k", "k"
    else:
        k, m, n = ar, ac, bc
        a_col, b_col = "m", "n"
    limit = {"m": m, "n": n // go, "k": k}
    limit[a_col] = min(limit[a_col], ac // ga)
    limit[b_col] = min(limit[b_col], bc // gb)
    tm = tm or _tile(limit["m"], 2048, 128 if mode == "tn" else 16)
    tn = tn or _tile(limit["n"], 512)
    tk = tk or _tile(limit["k"], 2048, 16 if mode == "tn" else 128)
    nk = k // tk
    if mode == "nn":
        a_spec = _block(tm, tk, ga, ac, lambda i, j, s: (i, s))
        b_spec = _block(tk, tn, gb, bc, lambda i, j, s: (s, j))
        dims = _NN
    elif mode == "nt":
        a_spec = _block(tm, tk, ga, ac, lambda i, j, s: (i, s))
        b_spec = _block(tn, tk, gb, bc, lambda i, j, s: (j, s))
        dims = _NT
    else:
        a_spec = _block(tk, tm, ga, ac, lambda i, j, s: (s, i))
        b_spec = _block(tk, tn, gb, bc, lambda i, j, s: (s, j))
        dims = _TN
    o_spec = _block(tm, tn, go, n, lambda i, j, s: (i, j))
    out_shape = _sds((m, n), out_dtype) if go == 1 else _sds((go, m, n // go), out_dtype)

    def body(a_ref, b_ref, o_ref, acc_ref):
        s = pl.program_id(2)

        @pl.when(s == 0)
        def _():
            acc_ref[...] = jnp.zeros_like(acc_ref)

        acc_ref[...] += _dot(a_ref[...], b_ref[...], dims)

        @pl.when(s == nk - 1)
        def _():
            o_ref[...] = acc_ref[...].astype(o_ref.dtype)

    return pl.pallas_call(
        body, name=name, out_shape=out_shape, grid=(m // tm, n // tn, nk),
        in_specs=[a_spec, b_spec], out_specs=o_spec,
        scratch_shapes=[pltpu.VMEM((tm, tn), F32)],
        compiler_params=_params(("parallel", "parallel", "arbitrary"), VMEM_BIG),
    )(a, b)


def _rows(tm, n):
    return pl.BlockSpec((tm, n), lambda i: (i, 0))


def _vec(n):
    return pl.BlockSpec((1, n), lambda i: (0, 0))


class Vec(NamedTuple):
    array: jax.Array
    row: int
    col: int


def _vec_in(v, d):
    return pl.BlockSpec((None, 1, d), lambda i: (v.row, 0, v.col))


def modnorm_fwd(x, g, sc, sh, name):
    s, d = x.shape
    tm = _tile(s, 256, 8)

    def body(x_ref, g_ref, sc_ref, sh_ref, o_ref):
        xv = x_ref[...]
        r = lax.rsqrt(jnp.mean(xv * xv, axis=-1, keepdims=True) + EPS)
        o_ref[...] = ((xv * r) * g_ref[...] * (1.0 + sc_ref[...]) + sh_ref[...]).astype(BF16)

    return pl.pallas_call(
        body, name=name, out_shape=_sds((s, d), BF16), grid=(s // tm,),
        in_specs=[_rows(tm, d), _vec_in(g, d), _vec_in(sc, d), _vec_in(sh, d)], out_specs=_rows(tm, d),
        compiler_params=_params(("parallel",)),
    )(x, g.array, sc.array, sh.array)


def norm_bwd(x, dh, g, sc, dres, name):
    s, d = x.shape
    tm = _tile(s, 256, 8)
    nsteps = s // tm

    def body(x_ref, dh_ref, g_ref, sc_ref, dr_ref, dx_ref, dsh_ref, dsc_ref, dg_ref, a2_ref):
        i = pl.program_id(0)

        @pl.when(i == 0)
        def _():
            dsh_ref[...] = jnp.zeros_like(dsh_ref)
            a2_ref[...] = jnp.zeros_like(a2_ref)

        xv = x_ref[...]
        dh = dh_ref[...].astype(F32)
        r = lax.rsqrt(jnp.mean(xv * xv, axis=-1, keepdims=True) + EPS)
        xh = xv * r
        dsh_ref[...] += jnp.sum(dh, axis=0, keepdims=True)
        a2_ref[...] += jnp.sum(dh * xh, axis=0, keepdims=True)
        dxh = dh * (g_ref[...] * (1.0 + sc_ref[...]))
        dx = r * (dxh - xh * jnp.mean(dxh * xh, axis=-1, keepdims=True))
        dx_ref[...] = dr_ref[...] + dx

        @pl.when(i == nsteps - 1)
        def _():
            dsc_ref[...] = a2_ref[...] * g_ref[...]
            dg_ref[...] = a2_ref[...] * (1.0 + sc_ref[...])

    return pl.pallas_call(
        body, name=name, out_shape=(_sds((s, d)), _sds((1, d)), _sds((1, d)), _sds((1, d))), grid=(nsteps,),
        in_specs=[_rows(tm, d), _rows(tm, d), _vec_in(g, d), _vec_in(sc, d), _rows(tm, d)],
        out_specs=(_rows(tm, d), _vec(d), _vec(d), _vec(d)),
        scratch_shapes=[pltpu.VMEM((1, d), F32)],
        compiler_params=_params(("arbitrary",)),
    )(x, dh, g.array, sc.array, dres)


def resid_modnorm_fwd(x, y, gate, g, sc, sh, name):
    s, d = x.shape
    tm = _tile(s, 256, 8)

    def body(x_ref, y_ref, gate_ref, g_ref, sc_ref, sh_ref, xo_ref, h_ref):
        xv = x_ref[...] + gate_ref[...] * y_ref[...].astype(F32)
        xo_ref[...] = xv
        r = lax.rsqrt(jnp.mean(xv * xv, axis=-1, keepdims=True) + EPS)
        h_ref[...] = ((xv * r) * g_ref[...] * (1.0 + sc_ref[...]) + sh_ref[...]).astype(BF16)

    return pl.pallas_call(
        body, name=name, out_shape=(_sds((s, d)), _sds((s, d), BF16)), grid=(s // tm,),
        in_specs=[_rows(tm, d), _rows(tm, d), _vec_in(gate, d), _vec_in(g, d), _vec_in(sc, d), _vec_in(sh, d)],
        out_specs=(_rows(tm, d), _rows(tm, d)),
        compiler_params=_params(("parallel",)),
    )(x, y, gate.array, g.array, sc.array, sh.array)


def norm_gate_bwd(x, dh, g, sc, dres, y, gate, name):
    s, d = x.shape
    tm = _tile(s, 256, 8)
    nsteps = s // tm

    def body(x_ref, dh_ref, g_ref, sc_ref, dr_ref, y_ref, gate_ref, dx_ref, dsh_ref, dsc_ref, dg_ref, dy_ref,
             dgate_ref, a2_ref):
        i = pl.program_id(0)

        @pl.when(i == 0)
        def _():
            dsh_ref[...] = jnp.zeros_like(dsh_ref)
            a2_ref[...] = jnp.zeros_like(a2_ref)
            dgate_ref[...] = jnp.zeros_like(dgate_ref)

        xv = x_ref[...]
        dh = dh_ref[...].astype(F32)
        r = lax.rsqrt(jnp.mean(xv * xv, axis=-1, keepdims=True) + EPS)
        xh = xv * r
        dsh_ref[...] += jnp.sum(dh, axis=0, keepdims=True)
        a2_ref[...] += jnp.sum(dh * xh, axis=0, keepdims=True)
        dxh = dh * (g_ref[...] * (1.0 + sc_ref[...]))
        dr = dr_ref[...] + r * (dxh - xh * jnp.mean(dxh * xh, axis=-1, keepdims=True))
        dx_ref[...] = dr
        dy_ref[...] = (dr * gate_ref[...]).astype(BF16)
        dgate_ref[...] += jnp.sum(dr * y_ref[...].astype(F32), axis=0, keepdims=True)

        @pl.when(i == nsteps - 1)
        def _():
            dsc_ref[...] = a2_ref[...] * g_ref[...]
            dg_ref[...] = a2_ref[...] * (1.0 + sc_ref[...])

    vec = _sds((1, d))
    return pl.pallas_call(
        body, name=name, out_shape=(_sds((s, d)), vec, vec, vec, _sds((s, d), BF16), vec), grid=(nsteps,),
        in_specs=[_rows(tm, d), _rows(tm, d), _vec_in(g, d), _vec_in(sc, d), _rows(tm, d), _rows(tm, d), _vec_in(gate, d)],
        out_specs=(_rows(tm, d), _vec(d), _vec(d), _vec(d), _rows(tm, d), _vec(d)),
        scratch_shapes=[pltpu.VMEM((1, d), F32)],
        compiler_params=_params(("arbitrary",)),
    )(x, dh, g.array, sc.array, dres, y, gate.array)


def final_fused(x, f, gate, g, tgt):
    s, d = x.shape
    tm = _tile(s, 256, 8)

    def body(x_ref, f_ref, gate_ref, g_ref, t_ref, dx_ref, dg_ref, loss_ref, df_ref, dgate_ref):
        @pl.when(pl.program_id(0) == 0)
        def _():
            dg_ref[...] = jnp.zeros_like(dg_ref)
            loss_ref[...] = jnp.zeros_like(loss_ref)
            dgate_ref[...] = jnp.zeros_like(dgate_ref)

        fv, gatev, gv = f_ref[...].astype(F32), gate_ref[...], g_ref[...]
        xv = x_ref[...] + gatev * fv
        r = lax.rsqrt(jnp.mean(xv * xv, axis=-1, keepdims=True) + EPS)
        xh = xv * r
        e = xh * gv - t_ref[...]
        row = jnp.sum(e * e, axis=-1, keepdims=True) * (0.5 / d)
        loss_ref[...] += jnp.sum(row, axis=0, keepdims=True)
        dy = e * (1.0 / d)
        dg_ref[...] += jnp.sum(dy * xh, axis=0, keepdims=True)
        dxh = dy * gv
        dx = r * (dxh - xh * jnp.mean(dxh * xh, axis=-1, keepdims=True))
        dx_ref[...] = dx
        df_ref[...] = (dx * gatev).astype(BF16)
        dgate_ref[...] += jnp.sum(dx * fv, axis=0, keepdims=True)

    vec = _sds((1, d))
    return pl.pallas_call(
        body, name="final_fused", out_shape=(_sds((s, d)), vec, _sds((1, LANES)), _sds((s, d), BF16), vec),
        grid=(s // tm,),
        in_specs=[_rows(tm, d), _rows(tm, d), _vec_in(gate, d), _vec_in(g, d), _rows(tm, d)],
        out_specs=(_rows(tm, d), _vec(d), _vec(LANES), _rows(tm, d), _vec(d)),
        compiler_params=_params(("arbitrary",)),
    )(x, f, gate.array, g.array, tgt)


def _taps(v):
    return _shift_down(v, 2), _shift_down(v, 1), v


def _conv3_taps(taps, w):
    return w[0:1, :] * taps[0] + w[1:2, :] * taps[1] + w[2:3, :] * taps[2]


def _conv3(v, w):
    return _conv3_taps(_taps(v), w)


def _conv3_t(dv, w):
    return w[0:1, :] * _shift_up(dv, 2) + w[1:2, :] * _shift_up(dv, 1) + w[2:3, :] * dv


def _conv3_dw_taps(dv, taps):
    return jnp.concatenate([jnp.sum(dv * t, axis=0, keepdims=True) for t in taps], axis=0)


def _conv3_dw(dv, v):
    return _conv3_dw_taps(dv, _taps(v))


def gconv_fwd(z, conv_w):
    s = z.shape[0]
    nb = A_WIDTH // LANES

    def body(b_ref, c_ref, a_ref, w_ref, o_ref):
        b, c, a = b_ref[...].astype(F32), c_ref[...].astype(F32), a_ref[...].astype(F32)
        o_ref[...] = (b * _conv3(c * a, w_ref[...])).astype(BF16)

    col = lambda off: pl.BlockSpec((s, LANES), lambda j: (0, off + j))
    return pl.pallas_call(
        body, name="gconv_fwd", out_shape=_sds((s, A_WIDTH + _B_WIDTH), BF16), grid=(nb,),
        in_specs=[col(0), col(nb), col(2 * nb), pl.BlockSpec((3, LANES), lambda j: (0, j))],
        out_specs=pl.BlockSpec((s, LANES), lambda j: (0, j)),
        compiler_params=_params(("parallel",), VMEM_BIG),
    )(z, z, z, conv_w)


def gconv_bwd(z, conv_w, dycat):
    s = z.shape[0]
    nb = A_WIDTH // LANES

    def body(b_ref, c_ref, a_ref, w_ref, dy_ref, db_ref, dc_ref, da_ref, dw_ref):
        c, a, w, dy = c_ref[...].astype(F32), a_ref[...].astype(F32), w_ref[...], dy_ref[...].astype(F32)
        ca = c * a
        db_ref[...] = (dy * _conv3(ca, w)).astype(BF16)
        dconv = dy * b_ref[...].astype(F32)
        dw_ref[...] = _conv3_dw(dconv, ca)
        dca = _conv3_t(dconv, w)
        dc_ref[...] = (dca * a).astype(BF16)
        da_ref[...] = (dca * c).astype(BF16)

    col = lambda off: pl.BlockSpec((s, LANES), lambda j: (0, off + j))
    wspec = pl.BlockSpec((3, LANES), lambda j: (0, j))
    part = _sds((s, A_WIDTH), BF16)
    return pl.pallas_call(
        body, name="gconv_bwd", out_shape=(part, part, part, _sds((3, A_WIDTH))), grid=(nb,),
        in_specs=[col(0), col(nb), col(2 * nb), wspec, col(0)],
        out_specs=(col(0), col(0), col(0), wspec),
        compiler_params=_params(("parallel",), VMEM_BIG),
    )(z, z, z, conv_w, dycat)


def _pool_counts(s, w):
    t = lax.broadcasted_iota(jnp.int32, (s, 1), 0)
    return jnp.minimum(t + 1, w).astype(F32)


def _pooled(p, levels):
    acc = p
    for lv in range(levels):
        acc = acc + _shift_down(acc, 2 ** lv)
    return acc / _pool_counts(p.shape[0], 2 ** levels) - p


_B_WIDTH = B_GROUPS * LANES


def pool_fwd(z, mix_w, scale, ycat):
    s = z.shape[0]

    def body(p_ref, m_ref, sc_ref, ycat_ref, o_ref):
        del ycat_ref
        for g in range(B_GROUPS):
            cols = slice(g * LANES, (g + 1) * LANES)
            pooled = _pooled(p_ref[:, cols].astype(F32), g + 1)
            y = _dot(pooled.astype(BF16), m_ref[g].astype(BF16))
            o_ref[:, cols] = (y * sc_ref[:, cols]).astype(BF16)

    return pl.pallas_call(
        body, name="pool_fwd", out_shape=_sds(ycat.shape, BF16), grid=(1,),
        in_specs=[pl.BlockSpec((s, _B_WIDTH), lambda i: (0, 3 * A_WIDTH // _B_WIDTH)),
                  pl.BlockSpec((B_GROUPS, LANES, LANES), lambda i: (0, 0, 0)), pl.BlockSpec((1, _B_WIDTH), lambda i: (0, 0)),
                  pl.BlockSpec(memory_space=pl.ANY)],
        out_specs=pl.BlockSpec((s, _B_WIDTH), lambda i: (0, A_WIDTH // _B_WIDTH)),
        input_output_aliases={3: 0},
        compiler_params=_params(("arbitrary",), VMEM_BIG),
    )(z, mix_w, scale, ycat)


def pool_bwd(z, mix_w, scale, dycat):
    s = z.shape[0]

    def body(p_ref, m_ref, sc_ref, dy_ref, dp_ref, dm_ref, dsc_ref):
        for g in range(B_GROUPS):
            cols = slice(g * LANES, (g + 1) * LANES)
            pooled = _pooled(p_ref[:, cols].astype(F32), g + 1)
            mw = m_ref[g].astype(BF16)
            pb = pooled.astype(BF16)
            dy = dy_ref[:, cols].astype(F32)
            dsc_ref[:, cols] = jnp.sum(dy * _dot(pb, mw), axis=0, keepdims=True)
            dmix = (dy * sc_ref[:, cols]).astype(BF16)
            dm_ref[g] = _dot(pb, dmix, _TN)
            dpool = _dot(dmix, mw, _NT)
            acc = dpool / _pool_counts(s, 2 ** (g + 1))
            for lv in range(g + 1):
                acc = acc + _shift_up(acc, 2 ** lv)
            dp_ref[:, cols] = (acc - dpool).astype(BF16)

    wide = lambda c: pl.BlockSpec((s, _B_WIDTH), lambda i: (0, c))
    mspec = pl.BlockSpec((B_GROUPS, LANES, LANES), lambda i: (0, 0, 0))
    vspec = pl.BlockSpec((1, _B_WIDTH), lambda i: (0, 0))
    return pl.pallas_call(
        body, name="pool_bwd", out_shape=(_sds((s, _B_WIDTH), BF16), _sds((B_GROUPS, LANES, LANES)), _sds((1, _B_WIDTH))),
        grid=(1,), in_specs=[wide(3 * A_WIDTH // _B_WIDTH), mspec, vspec, wide(A_WIDTH // _B_WIDTH)],
        out_specs=(wide(0), mspec, vspec),
        compiler_params=_params(("arbitrary",), VMEM_BIG),
    )(z, mix_w, scale, dycat)


_FF_BLOCKS = D_FF // FF_UNIT


def _ff_spec(s):
    return pl.BlockSpec((2, s, FF_UNIT), lambda j: (0, 0, j))


def _ff_wspecs():
    return [pl.BlockSpec((3, FF_UNIT), lambda j: (0, j)), pl.BlockSpec((3, FF_UNIT), lambda j: (0, _FF_BLOCKS + j))]


_FF_ROWS = 64
_FF_HALO = 16


def _chunk_taps(z_ref, half, c):
    start = pl.multiple_of(c * _FF_ROWS, _FF_ROWS)
    before = pl.multiple_of(jnp.maximum(c * _FF_ROWS - _FF_HALO, 0), _FF_HALO)
    halo = z_ref[half, pl.ds(before, _FF_HALO), :].astype(F32)
    halo = jnp.where(c > 0, halo, 0.0)
    win = jnp.concatenate([halo, z_ref[half, pl.ds(start, _FF_ROWS), :].astype(F32)], axis=0)
    return tuple(pltpu.roll(win, k, axis=0)[_FF_HALO:] for k in (2, 1)) + (win[_FF_HALO:],)


def _fold8(v):
    acc = v[0:8]
    for r in range(8, v.shape[0], 8):
        acc = acc + v[r:r + 8]
    return acc


_FF_CHUNK = 256


def ffn_act_down(zf, conv_w, w_down, name):
    s, d = zf.shape[1], w_down.shape[1]
    nk = D_FF // _FF_CHUNK
    chunk = lambda k: jnp.minimum(k, nk - 1)

    def body(z_ref, wg_ref, wu_ref, wd_ref, a_ref, f_ref, held_ref, acc_ref):
        k = pl.program_id(0)

        @pl.when(k == 0)
        def _():
            held_ref[...] = jnp.zeros_like(held_ref)
            acc_ref[...] = jnp.zeros_like(acc_ref)

        acc_ref[...] += _dot(held_ref[(k + 1) % 2], wd_ref[...])
        g = _conv3(z_ref[0].astype(F32), wg_ref[...])
        u = _conv3(z_ref[1].astype(F32), wu_ref[...])
        act = (g * _sigmoid(g) * u).astype(BF16)
        a_ref[...] = act
        held_ref[k % 2] = act

        @pl.when(k == nk)
        def _():
            f_ref[...] = acc_ref[...].astype(BF16)

    return pl.pallas_call(
        body, name=name, out_shape=(_sds((s, D_FF), BF16), _sds((s, d), BF16)), grid=(nk + 1,),
        in_specs=[pl.BlockSpec((2, s, _FF_CHUNK), lambda k: (0, 0, chunk(k))),
                  pl.BlockSpec((3, _FF_CHUNK), lambda k: (0, chunk(k))),
                  pl.BlockSpec((3, _FF_CHUNK), lambda k: (0, nk + chunk(k))),
                  pl.BlockSpec((_FF_CHUNK, d), lambda k: (jnp.maximum(k - 1, 0), 0))],
        out_specs=(pl.BlockSpec((s, _FF_CHUNK), lambda k: (0, chunk(k))), pl.BlockSpec((s, d), lambda k: (0, 0))),
        scratch_shapes=[pltpu.VMEM((2, s, _FF_CHUNK), BF16), pltpu.VMEM((s, d), F32)],
        compiler_params=_params(("arbitrary",), VMEM_BIG),
    )(zf, conv_w, conv_w, w_down)


def ffn_act_bwd(zf, conv_w, da, name):
    s = zf.shape[1]
    assert s % _FF_ROWS == 0
    nchunks = s // _FF_ROWS

    def body(z_ref, wg_ref, wu_ref, da_ref, dz_ref, dw_ref, dg_ref, du_ref):
        wg, wu = wg_ref[...], wu_ref[...]

        def first(c, acc):
            rows = pl.ds(pl.multiple_of(c * _FF_ROWS, _FF_ROWS), _FF_ROWS)
            tg, tu = _chunk_taps(z_ref, 0, c), _chunk_taps(z_ref, 1, c)
            g = _conv3_taps(tg, wg)
            u = _conv3_taps(tu, wu)
            dav = da_ref[rows, :].astype(F32)
            sg = _sigmoid(g)
            dg = dav * u * (sg * (1.0 + g * (1.0 - sg)))
            du = dav * (g * sg)
            dg_ref[rows, :] = dg
            du_ref[rows, :] = du
            return tuple(a + _fold8(d * t) for a, (d, t) in zip(acc, [(dg, t) for t in tg] + [(du, t) for t in tu]))

        zero = jnp.zeros((8, FF_UNIT), F32)
        acc = lax.fori_loop(0, nchunks, first, (zero,) * 6)
        sums = [jnp.sum(a, axis=0, keepdims=True) for a in acc]
        dw_ref[0] = jnp.concatenate(sums[:3], axis=0)
        dw_ref[1] = jnp.concatenate(sums[3:], axis=0)

        tail = pl.ds(s, _FF_HALO)
        dg_ref[tail, :] = jnp.zeros((_FF_HALO, FF_UNIT), F32)
        du_ref[tail, :] = jnp.zeros((_FF_HALO, FF_UNIT), F32)
        span = _FF_ROWS + _FF_HALO

        def second(c, carry):
            start = pl.multiple_of(c * _FF_ROWS, _FF_ROWS)
            for half, (d_ref, w) in enumerate(((dg_ref, wg), (du_ref, wu))):
                win = d_ref[pl.ds(start, span), :]
                dz = (w[0:1, :] * pltpu.roll(win, span - 2, axis=0)[:_FF_ROWS]
                      + w[1:2, :] * pltpu.roll(win, span - 1, axis=0)[:_FF_ROWS] + w[2:3, :] * win[:_FF_ROWS])
                dz_ref[half, pl.ds(start, _FF_ROWS), :] = dz.astype(BF16)
            return carry

        lax.fori_loop(0, nchunks, second, 0)

    return pl.pallas_call(
        body, name=name, out_shape=(_sds((2, s, D_FF), BF16), _sds((2, 3, D_FF))), grid=(_FF_BLOCKS,),
        in_specs=[_ff_spec(s)] + _ff_wspecs() + [pl.BlockSpec((s, FF_UNIT), lambda j: (0, j))],
        out_specs=(_ff_spec(s), pl.BlockSpec((2, 3, FF_UNIT), lambda j: (0, 0, j))),
        scratch_shapes=[pltpu.VMEM((s + _FF_HALO, FF_UNIT), F32), pltpu.VMEM((s + _FF_HALO, FF_UNIT), F32)],
        compiler_params=_params(("parallel",), VMEM_BIG),
    )(zf, conv_w, conv_w, da)


def _rope(v, cs, s1, s2):
    return v * cs + pltpu.roll(v, LANES - C_ROPE // 2, axis=1) * s1 + pltpu.roll(v, C_ROPE // 2, axis=1) * s2


def _rope_t(dv, cs, s1, s2):
    return dv * cs + pltpu.roll(dv * s1, C_ROPE // 2, axis=1) + pltpu.roll(dv * s2, LANES - C_ROPE // 2, axis=1)


def _kpe_mask(shape):
    lane = lax.broadcasted_iota(jnp.int32, shape, 1)
    return (lane >= C_NOPE) & (lane < C_NOPE + C_ROPE)


def _rms(v, g):
    r = lax.rsqrt(jnp.mean(v * v, axis=-1, keepdims=True) + EPS)
    return v * r, r


def _rms_bwd(dn, xh, r, g):
    dxh = dn * g
    return r * (dxh - xh * jnp.mean(dxh * xh, axis=-1, keepdims=True)), jnp.sum(dn * xh, axis=0, keepdims=True)


_ZQ = C_Q_RANK + C_KV_RANK + HEAD_PAD
_HW = C_HEADS * HEAD_PAD


def mla_pre_fwd(z, gq, gkv, wq, wk, wv, cs, s1, s2):
    s = z.shape[0]
    tm = _tile(s, 256, 8)

    def body(z_ref, gq_ref, gkv_ref, wq_ref, wk_ref, wv_ref, cs_ref, s1_ref, s2_ref, q_ref, k_ref, v_ref):
        zv = z_ref[...].astype(F32)
        cst, s1t, s2t = cs_ref[...], s1_ref[...], s2_ref[...]
        qh, _ = _rms(zv[:, :C_Q_RANK], None)
        qn = (qh * gq_ref[...]).astype(BF16)
        q = _dot(qn, wq_ref[...])
        kh, _ = _rms(zv[:, C_Q_RANK:C_Q_RANK + C_KV_RANK], None)
        kvn = (kh * gkv_ref[...]).astype(BF16)
        k = _dot(kvn, wk_ref[...])
        v_ref[...] = _dot(kvn, wv_ref[...]).astype(BF16)
        kpe = _rope(zv[:, C_Q_RANK + C_KV_RANK:], cst, s1t, s2t)
        for h in range(C_HEADS):
            sl = slice(h * HEAD_PAD, (h + 1) * HEAD_PAD)
            q_ref[:, sl] = _rope(q[:, sl], cst, s1t, s2t).astype(BF16)
            k_ref[:, sl] = (k[:, sl] + kpe).astype(BF16)

    full = lambda r, c: pl.BlockSpec((r, c), lambda i: (0, 0))
    hw = _sds((s, _HW), BF16)
    return pl.pallas_call(
        body, name="mla_pre_fwd", out_shape=(hw, hw, hw), grid=(s // tm,),
        in_specs=[_rows(tm, _ZQ), _vec(C_Q_RANK), _vec(C_KV_RANK), full(C_Q_RANK, _HW), full(C_KV_RANK, _HW),
                  full(C_KV_RANK, _HW), _rows(tm, LANES), _rows(tm, LANES), _rows(tm, LANES)],
        out_specs=(_rows(tm, _HW), _rows(tm, _HW), _rows(tm, _HW)),
        compiler_params=_params(("parallel",), VMEM_BIG),
    )(z, gq, gkv, wq, wk, wv, cs, s1, s2)


def mla_pre_bwd(z, gq, gkv, wq, wk, wv, cs, s1, s2, dq, dk, dv):
    s = z.shape[0]
    tm = _tile(s, 256, 8)

    def body(z_ref, gq_ref, gkv_ref, wq_ref, wk_ref, wv_ref, cs_ref, s1_ref, s2_ref, dq_ref, dk_ref, dv_ref,
             dz_ref, dwq_ref, dwk_ref, dwv_ref, dgq_ref, dgkv_ref):
        @pl.when(pl.program_id(0) == 0)
        def _():
            dwq_ref[...] = jnp.zeros_like(dwq_ref)
            dwk_ref[...] = jnp.zeros_like(dwk_ref)
            dwv_ref[...] = jnp.zeros_like(dwv_ref)
            dgq_ref[...] = jnp.zeros_like(dgq_ref)
            dgkv_ref[...] = jnp.zeros_like(dgkv_ref)

        zv = z_ref[...].astype(F32)
        cst, s1t, s2t = cs_ref[...], s1_ref[...], s2_ref[...]
        gqv, gkvv = gq_ref[...], gkv_ref[...]
        qh, rq = _rms(zv[:, :C_Q_RANK], None)
        qn = (qh * gqv).astype(BF16)
        kh, rk = _rms(zv[:, C_Q_RANK:C_Q_RANK + C_KV_RANK], None)
        kvn = (kh * gkvv).astype(BF16)

        dqv = dq_ref[...].astype(F32)
        dqp = jnp.concatenate(
            [_rope_t(dqv[:, h * HEAD_PAD:(h + 1) * HEAD_PAD], cst, s1t, s2t) for h in range(C_HEADS)], axis=1
        ).astype(BF16)
        dwq_ref[...] += _dot(qn, dqp, _TN)
        dqn = _dot(dqp, wq_ref[...], _NT)
        dql, dgq = _rms_bwd(dqn, qh, rq, gqv)
        dgq_ref[...] += dgq

        dkv = dk_ref[...]
        dkb = dkv.astype(BF16)
        dvb = dv_ref[...].astype(BF16)
        dwk_ref[...] += _dot(kvn, dkb, _TN)
        dwv_ref[...] += _dot(kvn, dvb, _TN)
        dkvn = _dot(dkb, wk_ref[...], _NT) + _dot(dvb, wv_ref[...], _NT)
        dkl, dgkv = _rms_bwd(dkvn, kh, rk, gkvv)
        dgkv_ref[...] += dgkv

        dkpe = dkv[:, :HEAD_PAD]
        for h in range(1, C_HEADS):
            dkpe = dkpe + dkv[:, h * HEAD_PAD:(h + 1) * HEAD_PAD]
        dkpe = _rope_t(jnp.where(_kpe_mask(dkpe.shape), dkpe, 0.0), cst, s1t, s2t)
        dz_ref[...] = jnp.concatenate([dql, dkl, dkpe], axis=1).astype(BF16)

    full = lambda r, c: pl.BlockSpec((r, c), lambda i: (0, 0))
    return pl.pallas_call(
        body, name="mla_pre_bwd",
        out_shape=(_sds((s, _ZQ), BF16), _sds((C_Q_RANK, _HW)), _sds((C_KV_RANK, _HW)), _sds((C_KV_RANK, _HW)),
                   _sds((1, C_Q_RANK)), _sds((1, C_KV_RANK))),
        grid=(s // tm,),
        in_specs=[_rows(tm, _ZQ), _vec(C_Q_RANK), _vec(C_KV_RANK), full(C_Q_RANK, _HW), full(C_KV_RANK, _HW),
                  full(C_KV_RANK, _HW), _rows(tm, LANES), _rows(tm, LANES), _rows(tm, LANES),
                  _rows(tm, _HW), _rows(tm, _HW), _rows(tm, _HW)],
        out_specs=(_rows(tm, _ZQ), full(C_Q_RANK, _HW), full(C_KV_RANK, _HW), full(C_KV_RANK, _HW),
                   _vec(C_Q_RANK), _vec(C_KV_RANK)),
        compiler_params=_params(("arbitrary",), VMEM_BIG),
    )(z, gq, gkv, wq, wk, wv, cs, s1, s2, dq, dk, dv)


_ATT_SCALE = (C_NOPE + C_ROPE) ** -0.5
_NEG = -1e30


def _att_exp(q, k, row0, ends_here):
    sc = _dot(q, k, _NT) * _ATT_SCALE
    tq, nk = sc.shape
    if ends_here:
        last = sc[:, nk - tq:]
        row = lax.broadcasted_iota(jnp.int32, last.shape, 0)
        col = lax.broadcasted_iota(jnp.int32, last.shape, 1)
        last = jnp.where(col <= row, last, _NEG)
        sc = last if nk == tq else jnp.concatenate([sc[:, :nk - tq], last], axis=1)
    else:
        qpos = row0 + lax.broadcasted_iota(jnp.int32, sc.shape, 0)
        kpos = lax.broadcasted_iota(jnp.int32, sc.shape, 1)
        sc = jnp.where(kpos <= qpos, sc, _NEG)
    e = jnp.exp(sc - jnp.max(sc, axis=-1, keepdims=True))
    return e, 1.0 / jnp.sum(e, axis=-1, keepdims=True)


def _causal_cases(i, nq, tq, fn):
    if nq > 8:
        fn(nq * tq, False)
        return
    for blk in range(nq):
        pl.when(i == blk)(functools.partial(fn, (blk + 1) * tq, True))


_HEADS_PER_STEP = 4
_HEAD_LANES = [slice(h * HEAD_PAD, (h + 1) * HEAD_PAD) for h in range(_HEADS_PER_STEP)]


def attn_fwd(q, k, v):
    s = q.shape[0]
    tq = _tile(s, 256, 8)
    nq = s // tq
    wide = _HEADS_PER_STEP * HEAD_PAD

    def body(q_ref, k_ref, v_ref, o_ref):
        i = pl.program_id(1)

        def case(nk, ends_here):
            for hd in _HEAD_LANES:
                e, inv = _att_exp(q_ref[:, hd], k_ref[:nk, hd], i * tq, ends_here)
                o_ref[:, hd] = (_dot(e.astype(BF16), v_ref[:nk, hd]) * inv).astype(BF16)

        _causal_cases(i, nq, tq, case)

    qspec = pl.BlockSpec((tq, wide), lambda h, i: (i, h))
    kspec = pl.BlockSpec((s, wide), lambda h, i: (0, h))
    return pl.pallas_call(
        body, name="attn_fwd", out_shape=_sds((s, _HW + _DW), BF16), grid=(C_HEADS // _HEADS_PER_STEP, s // tq),
        in_specs=[qspec, kspec, kspec], out_specs=qspec,
        compiler_params=_params(("parallel", "parallel"), VMEM_BIG),
    )(q, k, v)


def attn_bwd(q, k, v, o, do_all):
    s = q.shape[0]
    tq = _tile(s, 256, 8)
    wide = _HEADS_PER_STEP * HEAD_PAD

    def body(q_ref, k_ref, v_ref, o_ref, do_ref, dq_ref, dk_ref, dv_ref):
        i = pl.program_id(1)

        @pl.when(i == 0)
        def _():
            dk_ref[...] = jnp.zeros_like(dk_ref)
            dv_ref[...] = jnp.zeros_like(dv_ref)

        def case(nk, ends_here):
            for hd in _HEAD_LANES:
                qv, kv, vv, dov = q_ref[:, hd], k_ref[:nk, hd], v_ref[:nk, hd], do_ref[:, hd]
                e, inv = _att_exp(qv, kv, i * tq, ends_here)
                p = e * inv
                dp = _dot(dov, vv, _NT)
                delta = jnp.sum(dov.astype(F32) * o_ref[:, hd].astype(F32), axis=-1, keepdims=True)
                ds = (p * (dp - delta) * _ATT_SCALE).astype(BF16)
                dq_ref[:, hd] = _dot(ds, kv).astype(BF16)
                dk_ref[:nk, hd] += _dot(ds, qv, _TN)
                dv_ref[:nk, hd] += _dot(p.astype(BF16), dov, _TN)

        _causal_cases(i, s // tq, tq, case)

    qspec = pl.BlockSpec((tq, wide), lambda h, i: (i, h))
    kspec = pl.BlockSpec((s, wide), lambda h, i: (0, h))
    return pl.pallas_call(
        body, name="attn_bwd", out_shape=(_sds((s, _HW), BF16), _sds((s, _HW)), _sds((s, _HW))),
        grid=(C_HEADS // _HEADS_PER_STEP, s // tq),
        in_specs=[qspec, kspec, kspec, qspec, qspec], out_specs=(qspec, kspec, kspec),
        compiler_params=_params(("parallel", "arbitrary"), VMEM_BIG),
    )(q, k, v, o, do_all)


_DW = D_GROUPS * LANES


def _tril_bf16(w):
    r = lax.broadcasted_iota(jnp.int32, w.shape, 0)
    c = lax.broadcasted_iota(jnp.int32, w.shape, 1)
    return jnp.where(c <= r, w, 0.0).astype(BF16)


def _sgu_forward(zu, zv, lg, lb, ws_ref, bs):
    u = _gelu(zu)
    v = _gelu(zv)
    mu = jnp.mean(v, axis=-1, keepdims=True)
    vc = v - mu
    rstd = lax.rsqrt(jnp.mean(vc * vc, axis=-1, keepdims=True) + EPS)
    xh = vc * rstd
    vln = (xh * lg + lb).astype(BF16)
    mixed = []
    for g in range(D_GROUPS):
        wg = _tril_bf16(ws_ref[g])
        mixed.append(_dot(wg, vln[:, g * LANES:(g + 1) * LANES]) + bs[:, g:g + 1])
    return u, xh, rstd, vln, jnp.concatenate(mixed, axis=1)


def sgu_fwd(z, lg, lb, ws, bs_t, ycat):
    s = z.shape[0]
    nchunk = s // D_CHUNK

    def body(zu_ref, zv_ref, lg_ref, lb_ref, ws_ref, bs_ref, ycat_ref, o_ref):
        del ycat_ref
        u, _, _, _, mixed = _sgu_forward(zu_ref[...].astype(F32), zv_ref[...].astype(F32), lg_ref[...], lb_ref[...],
                                         ws_ref, bs_ref[...])
        o_ref[...] = (u * mixed).astype(BF16)

    return pl.pallas_call(
        body, name="sgu_fwd", out_shape=_sds(ycat.shape, BF16), grid=(nchunk,),
        in_specs=[pl.BlockSpec((D_CHUNK, _DW), lambda n: (n, 1)), pl.BlockSpec((D_CHUNK, _DW), lambda n: (n, 2)),
                  _vec(_DW), _vec(_DW), pl.BlockSpec((D_GROUPS, D_CHUNK, D_CHUNK), lambda n: (0, 0, 0)),
                  pl.BlockSpec((D_CHUNK, LANES), lambda n: (0, 0)), pl.BlockSpec(memory_space=pl.ANY)],
        out_specs=pl.BlockSpec((D_CHUNK, _DW), lambda n: (n, _HW // _DW)),
        input_output_aliases={6: 0},
        compiler_params=_params(("parallel",)),
    )(z, z, lg, lb, ws, bs_t, ycat)


def sgu_bwd(z, lg, lb, ws, bs_t, dycat, dy_col):
    s = z.shape[0]
    nchunk = s // D_CHUNK

    def body(zu_ref, zv_ref, lg_ref, lb_ref, ws_ref, bs_ref, dy_ref, dzu_ref, dzv_ref, dws_ref, dbs_ref, dlg_ref,
             dlb_ref):
        @pl.when(pl.program_id(0) == 0)
        def _():
            dws_ref[...] = jnp.zeros_like(dws_ref)
            dbs_ref[...] = jnp.zeros_like(dbs_ref)
            dlg_ref[...] = jnp.zeros_like(dlg_ref)
            dlb_ref[...] = jnp.zeros_like(dlb_ref)

        zu, zv, lg = zu_ref[...].astype(F32), zv_ref[...].astype(F32), lg_ref[...]
        u, xh, rstd, vln, mixed = _sgu_forward(zu, zv, lg, lb_ref[...], ws_ref, bs_ref[...])
        dy = dy_ref[...].astype(F32)
        dzu_ref[...] = (dy * mixed * _gelu_grad(zu)).astype(BF16)
        dmix = dy * u
        lane = lax.broadcasted_iota(jnp.int32, (D_CHUNK, LANES), 1)
        row = lax.broadcasted_iota(jnp.int32, (D_CHUNK, D_CHUNK), 0)
        colm = lax.broadcasted_iota(jnp.int32, (D_CHUNK, D_CHUNK), 1)
        dvln = []
        dbs = jnp.zeros((D_CHUNK, LANES), F32)
        for g in range(D_GROUPS):
            sl = slice(g * LANES, (g + 1) * LANES)
            dmg = dmix[:, sl]
            dbs = dbs + jnp.where(lane == g, jnp.sum(dmg, axis=-1, keepdims=True), 0.0)
            dmb = dmg.astype(BF16)
            dws_ref[g] += jnp.where(colm <= row, _dot(dmb, vln[:, sl], _NT), 0.0)
            dvln.append(_dot(_tril_bf16(ws_ref[g]), dmb, _TN))
        dbs_ref[...] += dbs
        dvln = jnp.concatenate(dvln, axis=1)
        dlg_ref[...] += jnp.sum(dvln * xh, axis=0, keepdims=True)
        dlb_ref[...] += jnp.sum(dvln, axis=0, keepdims=True)
        dxh = dvln * lg
        dvv = rstd * (dxh - jnp.mean(dxh, axis=-1, keepdims=True) - xh * jnp.mean(dxh * xh, axis=-1, keepdims=True))
        dzv_ref[...] = (dvv * _gelu_grad(zv)).astype(BF16)

    wsspec = pl.BlockSpec((D_GROUPS, D_CHUNK, D_CHUNK), lambda n: (0, 0, 0))
    chunk = lambda cidx: pl.BlockSpec((D_CHUNK, _DW), lambda n: (n, cidx))
    return pl.pallas_call(
        body, name="sgu_bwd",
        out_shape=(_sds((s, _DW), BF16), _sds((s, _DW), BF16), _sds((D_GROUPS, D_CHUNK, D_CHUNK)),
                   _sds((D_CHUNK, LANES)), _sds((1, _DW)), _sds((1, _DW))),
        grid=(nchunk,),
        in_specs=[chunk(1), chunk(2), _vec(_DW), _vec(_DW), wsspec, pl.BlockSpec((D_CHUNK, LANES), lambda n: (0, 0)),
                  chunk(dy_col)],
        out_specs=(chunk(0), chunk(0), wsspec, pl.BlockSpec((D_CHUNK, LANES), lambda n: (0, 0)), _vec(_DW), _vec(_DW)),
        compiler_params=_params(("arbitrary",)),
    )(z, z, lg, lb, ws, bs_t, dycat)


def ada_mod(c_all, ada_w, ada_b):
    nl, d, n = ada_w.shape
    nb = c_all.shape[0]
    tn = _tile(n, 512)

    def body(c_ref, w_ref, b_ref, o_ref):
        cv = c_ref[...]
        ca = (cv * _sigmoid(cv)).astype(BF16)
        o_ref[...] = _dot(ca, w_ref[...].astype(BF16)) + b_ref[...]

    return pl.pallas_call(
        body, name="ada_mod", out_shape=_sds((nl, nb, n)), grid=(nl, n // tn),
        in_specs=[pl.BlockSpec((nb, d), lambda l, j: (0, 0)), pl.BlockSpec((None, d, tn), lambda l, j: (l, 0, j)),
                  pl.BlockSpec((None, 1, tn), lambda l, j: (l, 0, j))],
        out_specs=pl.BlockSpec((None, nb, tn), lambda l, j: (l, 0, j)),
        compiler_params=_params(("parallel", "parallel")),
    )(c_all, ada_w, ada_b.reshape(nl, 1, n))


_ADAM_BLOCK = 256 * 1024


def _adam_rows(rows, cols):
    if rows * cols <= _ADAM_BLOCK or rows % 8:
        return rows
    return _tile(rows, max(8, _ADAM_BLOCK // cols), 8)


def _adam_update(w, gv, m, v):
    inv_bc1 = 1.0 / (1.0 - ADAM_B1 ** ADAM_STEP)
    inv_bc2 = 1.0 / (1.0 - ADAM_B2 ** ADAM_STEP)
    nm = ADAM_B1 * m + (1.0 - ADAM_B1) * gv
    nv = ADAM_B2 * v + (1.0 - ADAM_B2) * (gv * gv)
    return -ADAM_LR * ((nm * inv_bc1) / (jnp.sqrt(nv * inv_bc2) + ADAM_EPS) + ADAM_WD * w), nm, nv


def adamw(w, g, m, v, name):
    shape = w.shape
    cols = shape[-1]
    rows = w.size // cols
    tr = _adam_rows(rows, cols)

    def body(w_ref, g_ref, m_ref, v_ref, go_ref, d_ref, nm_ref, nv_ref):
        gv = g_ref[...]
        go_ref[...] = gv
        d_ref[...], nm_ref[...], nv_ref[...] = _adam_update(w_ref[...], gv, m_ref[...], v_ref[...])

    spec = pl.BlockSpec((tr, cols), lambda i: (i, 0))
    out = _sds((rows, cols))
    r2 = lambda t: t.reshape(rows, cols)
    res = pl.pallas_call(
        body, name=name, out_shape=(out,) * 4, grid=(rows // tr,),
        in_specs=[spec] * 4, out_specs=(spec,) * 4, compiler_params=_params(("parallel",)),
    )(r2(w), r2(g), r2(m), r2(v))
    return tuple(t.reshape(shape) for t in res)


def adamw_ada(w, c_all, dmod, m, v):
    nl, d, n = w.shape
    tr = _adam_rows(d, n)
    pad = 16 - c_all.shape[0]
    c16 = jnp.pad(c_all, ((0, pad), (0, 0)))
    dm16 = jnp.pad(dmod, ((0, 0), (0, pad), (0, 0)))

    def body(w_ref, c_ref, dm_ref, m_ref, v_ref, g_ref, d_ref, nm_ref, nv_ref):
        cv = c_ref[...]
        gv = _dot((cv * _sigmoid(cv)).astype(BF16), dm_ref[...].astype(BF16), _TN)
        g_ref[...] = gv
        d_ref[...], nm_ref[...], nv_ref[...] = _adam_update(w_ref[...], gv, m_ref[...], v_ref[...])

    spec = pl.BlockSpec((None, tr, n), lambda l, i: (l, i, 0))
    out = _sds((nl, d, n))
    return pl.pallas_call(
        body, name="adamw_ada_w", out_shape=(out, out, out, out), grid=(nl, d // tr),
        in_specs=[spec, pl.BlockSpec((16, tr), lambda l, i: (0, i)), pl.BlockSpec((None, 16, n), lambda l, i: (l, 0, 0)),
                  spec, spec],
        out_specs=(spec,) * 4, compiler_params=_params(("parallel", "parallel")),
    )(w, c16, dm16, m, v)


def adamw_small(ws, gs, ms, vs):
    n = len(ws)
    flat = lambda t: t.reshape(-1, t.shape[-1])

    def body(*refs):
        ins, outs = refs[:4 * n], refs[4 * n:]
        for i in range(n):
            w_ref, g_ref, m_ref, v_ref = ins[4 * i:4 * i + 4]
            outs[3 * i][...], outs[3 * i + 1][...], outs[3 * i + 2][...] = _adam_update(
                w_ref[...], g_ref[...], m_ref[...], v_ref[...])

    operands = [flat(t) for quad in zip(ws, gs, ms, vs) for t in quad]
    res = pl.pallas_call(
        body, name="adamw_small", out_shape=tuple(_sds(flat(w).shape) for w in ws for _ in range(3)),
    )(*operands)
    return [(g, res[3 * i].reshape(w.shape), res[3 * i + 1].reshape(w.shape), res[3 * i + 2].reshape(w.shape))
            for i, (w, g) in enumerate(zip(ws, gs))]


def adamw_layers(w, g0, g1, m, v, name):
    _, rows, cols = w.shape
    tr = _adam_rows(rows, cols)

    def body(w_ref, g0_ref, g1_ref, m_ref, v_ref, g_ref, d_ref, nm_ref, nv_ref):
        gv = jnp.where(pl.program_id(0) == 0, g0_ref[...], g1_ref[...])
        g_ref[...] = gv
        d_ref[...], nm_ref[...], nv_ref[...] = _adam_update(w_ref[...], gv, m_ref[...], v_ref[...])

    spec = pl.BlockSpec((None, tr, cols), lambda l, i: (l, i, 0))
    gspec = pl.BlockSpec((tr, cols), lambda l, i: (i, 0))
    out = _sds((2, rows, cols))
    return pl.pallas_call(
        body, name=name, out_shape=(out, out, out, out), grid=(2, rows // tr),
        in_specs=[spec, gspec, gspec, spec, spec], out_specs=(spec,) * 4, compiler_params=_params(("parallel", "parallel")),
    )(w, g0, g1, m, v)


def sum8(gathered):
    _, r, _ = gathered.shape
    tr = _tile(r, 512, 8)

    def body(g_ref, o_ref):
        acc = g_ref[0]
        for dev in range(1, N_DEV):
            acc = acc + g_ref[dev]
        o_ref[...] = acc

    return pl.pallas_call(
        body, name="sum8", out_shape=_sds((r, LANES)), grid=(r // tr,),
        in_specs=[pl.BlockSpec((N_DEV, tr, LANES), lambda i: (0, i, 0))], out_specs=pl.BlockSpec((tr, LANES), lambda i: (i, 0)),
        compiler_params=_params(("parallel",)),
    )(gathered)


_SUM_STEPS = 2


def pair_sums(gs, recvs, core, name):
    n = len(gs)
    trs = [g.shape[1] // 2 // _SUM_STEPS for g in gs]

    def body(c_ref, *refs):
        del c_ref
        for i in range(n):
            a_ref, b_ref, o_ref = refs[2 * i], refs[2 * i + 1], refs[2 * n + i]
            o_ref[...] = (a_ref[...].astype(F32) + b_ref[...].astype(F32)).astype(BF16)

    in_specs, out_specs = [], []
    for g, tr in zip(gs, trs):
        cols = g.shape[2]
        in_specs.append(pl.BlockSpec((None, tr, cols), lambda k, s, c: (k, c[0] * _SUM_STEPS + s, 0)))
        in_specs.append(pl.BlockSpec((None, tr, cols), lambda k, s, c: (k, s, 0)))
        out_specs.append(pl.BlockSpec((None, tr, cols), lambda k, s, c: (k, s, 0)))
    grid_spec = pltpu.PrefetchScalarGridSpec(num_scalar_prefetch=1, grid=(N_CHIPS, _SUM_STEPS), in_specs=in_specs,
                                             out_specs=tuple(out_specs))
    return list(pl.pallas_call(
        body, name=name, out_shape=tuple(_sds((N_CHIPS, g.shape[1] // 2, g.shape[2]), BF16) for g in gs),
        grid_spec=grid_spec, compiler_params=_params(("parallel", "parallel")),
    )(core.reshape(1).astype(jnp.int32), *[t for pair in zip(gs, recvs) for t in pair]))


def chip_sums(pairs, recvs, chip, core, name):
    n = len(pairs)
    trs = [p.shape[1] // _SUM_STEPS for p in pairs]

    def body(p_ref, *refs):
        del p_ref
        for i in range(n):
            own_ref, r_ref, o_ref = refs[2 * i], refs[2 * i + 1], refs[2 * n + i]
            acc = own_ref[...].astype(F32)
            for j in range(N_CHIPS - 1):
                acc = acc + r_ref[j].astype(F32)
            o_ref[...] = acc

    in_specs, out_specs = [], []
    for p, tr in zip(pairs, trs):
        cols = p.shape[2]
        in_specs.append(pl.BlockSpec((None, tr, cols), lambda s, q: (q[0], s, 0)))
        in_specs.append(pl.BlockSpec((N_CHIPS - 1, tr, cols), lambda s, q: (0, s, 0)))
        out_specs.append(pl.BlockSpec((None, tr, cols), lambda s, q: (q[1], s, 0)))
    grid_spec = pltpu.PrefetchScalarGridSpec(num_scalar_prefetch=1, grid=(_SUM_STEPS,), in_specs=in_specs,
                                             out_specs=tuple(out_specs))
    return list(pl.pallas_call(
        body, name=name, out_shape=tuple(_sds((2,) + p.shape[1:]) for p in pairs), grid_spec=grid_spec,
        compiler_params=_params(("parallel",)),
    )(jnp.stack([chip, core]).astype(jnp.int32), *[t for pair in zip(pairs, recvs) for t in pair]))


def _place():
    return lax.axis_index("x"), lax.axis_index("y"), lax.axis_index("c")


def _other_chips(x, y):
    return [(x, 1 - y), (1 - x, y), (1 - x, 1 - y)]


_HBM = pl.BlockSpec(memory_space=pltpu.HBM)


def all_gather8(v, name, after=()):
    m, n = v.shape

    def body(x_ref, *refs):
        out_ref, send_sems, recv_sems, local_sem = refs[len(after):]
        x, y, c = _place()
        me, sibling = (x, y, c), (x, y, 1 - c)
        chips = _other_chips(x, y)

        def rows(px, py, pc):
            return out_ref.at[pl.ds((4 * px + 2 * py + pc) * m, m), :]

        def copy(k, block, to, src=None):
            return pltpu.make_async_remote_copy(
                src_ref=rows(*block) if src is None else src, dst_ref=rows(*block),
                send_sem=send_sems.at[k], recv_sem=recv_sems.at[k], device_id=to, device_id_type=MESH)

        mine = pltpu.make_async_copy(x_ref, rows(*me), local_sem)
        mine.start()
        first = [copy(0, me, sibling, src=x_ref)]
        first += [copy(1 + j, me, (*chip, c), src=x_ref) for j, chip in enumerate(chips)]
        for cp in first:
            cp.start()
        passed = [copy(4 + j, (*chip, c), sibling) for j, chip in enumerate(chips)]
        for j, chip in enumerate(chips):
            copy(1 + j, (*chip, c), me).wait_recv()
            passed[j].start()
        copy(0, sibling, me).wait_recv()
        for j, chip in enumerate(chips):
            copy(4 + j, (*chip, 1 - c), me).wait_recv()
        for cp in first + passed:
            cp.wait_send()
        mine.wait()

    return pl.pallas_call(
        body, name=name, out_shape=_sds((N_DEV * m, n), v.dtype),
        in_specs=[pl.BlockSpec(memory_space=pltpu.VMEM)] + [pl.BlockSpec(memory_space=pl.ANY)] * len(after),
        out_specs=pl.BlockSpec(memory_space=pltpu.VMEM),
        scratch_shapes=[pltpu.SemaphoreType.DMA((7,)), pltpu.SemaphoreType.DMA((7,)), pltpu.SemaphoreType.DMA],
        compiler_params=_params(None, VMEM_BIG),
    )(v, *after)


def _comm_call(body, name, ins, out_shapes, nsem, aliases=None):
    return pl.pallas_call(
        body, name=name, out_shape=tuple(out_shapes), in_specs=[_HBM] * len(ins), out_specs=tuple([_HBM] * len(out_shapes)),
        scratch_shapes=[pltpu.SemaphoreType.DMA((nsem,)), pltpu.SemaphoreType.DMA((nsem,))],
        input_output_aliases=aliases or {},
    )(*ins)


def _remote(src, dst, send_sems, recv_sems, k, to):
    return pltpu.make_async_remote_copy(src_ref=src, dst_ref=dst, send_sem=send_sems.at[k], recv_sem=recv_sems.at[k],
                                        device_id=to, device_id_type=MESH)


def _half(core, rh):
    return pl.ds(pl.multiple_of(core * rh, 16), rh)


def swap_halves(gs, name):
    n = len(gs)

    def body(*refs):
        ins, outs, (send_sems, recv_sems) = refs[:n], refs[n:2 * n], refs[2 * n:]
        x, y, c = _place()
        copies = []
        for i in range(n):
            theirs = _half(1 - c, ins[i].shape[1] // 2)
            cp = _remote(ins[i].at[:, theirs], outs[i], send_sems, recv_sems, i, (x, y, 1 - c))
            cp.start()
            copies.append(cp)
        for cp in copies:
            cp.wait()

    return _comm_call(body, name, gs, [_sds((g.shape[0], g.shape[1] // 2, g.shape[2]), g.dtype) for g in gs], n)


def join_halves(bufs, name):
    n = len(bufs)

    def body(*refs):
        ins, outs, (send_sems, recv_sems) = refs[:n], refs[n:2 * n], refs[2 * n:]
        x, y, c = _place()
        copies = []
        for i in range(n):
            cp = _remote(ins[i].at[c], outs[i].at[c], send_sems, recv_sems, i, (x, y, 1 - c))
            cp.start()
            copies.append(cp)
        for i in range(n):
            theirs = outs[i].at[1 - c]
            _remote(theirs, theirs, send_sems, recv_sems, i, (x, y, 1 - c)).wait_recv()
        for cp in copies:
            cp.wait_send()

    return _comm_call(body, name, bufs, [_sds(b.shape, b.dtype) for b in bufs], n, {i: i for i in range(n)})


def forward_halves(lands, name):
    n = len(lands)

    def body(*refs):
        ins, outs, (send_sems, recv_sems) = refs[:n], refs[n:2 * n], refs[2 * n:]
        x, y, c = _place()
        sibling = (x, y, 1 - c)
        chips = _other_chips(x, y)
        copies = []
        for i in range(n):
            mine = _half(c, ins[i].shape[1] // 2)
            for j, (px, py) in enumerate(chips):
                cp = _remote(ins[i].at[2 * px + py, mine], outs[i].at[2 * px + py, mine], send_sems, recv_sems, 3 * i + j, sibling)
                cp.start()
                copies.append(cp)
        for i in range(n):
            theirs = _half(1 - c, ins[i].shape[1] // 2)
            for j, (px, py) in enumerate(chips):
                landed = outs[i].at[2 * px + py, theirs]
                _remote(landed, landed, send_sems, recv_sems, 3 * i + j, sibling).wait_recv()
        for cp in copies:
            cp.wait_send()

    return _comm_call(body, name, lands, [_sds(b.shape, b.dtype) for b in lands], 3 * n, {i: i for i in range(n)})


_SEM = pl.BlockSpec(memory_space=pltpu.SEMAPHORE)
_EFFECT = pltpu.SideEffectType.DATAFLOW_SIDE_EFFECTING


def _gather_copies(srcs, lands, send_sems, recv_sems):
    x, y, c = _place()
    copies = []
    for i in range(len(srcs)):
        mine = _half(c, srcs[i].shape[0] // 2)
        for j, chip in enumerate(_other_chips(x, y)):
            copies.append(_remote(srcs[i].at[mine], lands[i].at[2 * x + y, mine], send_sems, recv_sems, 3 * i + j, (*chip, c)))
    return copies


def _exchange_copies(srcs, lands, send_sems, recv_sems):
    x, y, c = _place()
    copies = []
    for i in range(len(srcs)):
        for j, (px, py) in enumerate(_other_chips(x, y)):
            copies.append(_remote(srcs[i].at[2 * px + py], lands[i].at[j], send_sems, recv_sems, 3 * i + j, (px, py, c)))
    return copies


def _everyone_copies(srcs, lands, send_sems, recv_sems):
    x, y, c = _place()
    flip = lambda v, b: 1 - v if b else v
    dst = lands[0].at[4 * x + 2 * y + c]
    return [_remote(srcs[0], dst, send_sems, recv_sems, j - 1, (flip(x, j & 4), flip(y, j & 2), flip(c, j & 1)))
            for j in range(1, N_DEV)]


GATHER = (_gather_copies, 3)
EXCHANGE = (_exchange_copies, 3)
EVERYONE = (_everyone_copies, N_DEV - 1)


def split_start(name, plan, srcs, land_shapes, after=()):
    copies_fn, per_source = plan
    n, m, k = len(srcs), len(land_shapes), len(after)
    ncopies = per_source * n

    def body(*refs):
        src_refs, land_refs = refs[:n], refs[n:n + m]
        send_sems, recv_sems = refs[n + m + k], refs[n + m + k + 1]
        token = refs[-1]
        for cp in copies_fn(src_refs, land_refs, send_sems, recv_sems):
            cp.start()
        token[...] = jnp.zeros_like(token)

    hbm = lambda s: pltpu.HBM(tuple(s.shape), s.dtype)
    outs = pl.pallas_call(
        body, name=name,
        out_shape=(pltpu.SemaphoreType.DMA((ncopies,)), pltpu.SemaphoreType.DMA((ncopies,)), *[hbm(s) for s in srcs],
                   *[hbm(s) for s in land_shapes], _sds((8, LANES))),
        in_specs=[_HBM] * (n + m) + [pl.BlockSpec(memory_space=pl.ANY)] * k,
        out_specs=(_SEM, _SEM, *([_HBM] * (n + m)), pl.BlockSpec(memory_space=pltpu.VMEM)),
        input_output_aliases={i: 2 + i for i in range(n + m)},
        compiler_params=pltpu.CompilerParams(has_side_effects=_EFFECT),
    )(*[pltpu.with_memory_space_constraint(s, pltpu.HBM) for s in srcs],
      *[pltpu.with_memory_space_constraint(lax.empty(tuple(s.shape), s.dtype), pltpu.HBM) for s in land_shapes], *after)
    handle = (outs[0], outs[1], list(outs[2:2 + n]), list(outs[2 + n:2 + n + m]))
    return handle, outs[-1][0, 0]


def split_wait(name, plan, handle, after):
    copies_fn, _ = plan
    send_sems, recv_sems, srcs, lands = handle
    n, m = len(srcs), len(lands)
    after = list(after) if isinstance(after, (list, tuple)) else [after]

    def body(*refs):
        src_refs, land_refs = refs[:n], refs[n:n + m]
        for cp in copies_fn(src_refs, land_refs, refs[n + m], refs[n + m + 1]):
            cp.wait_send()
            cp.wait_recv()

    hbm = lambda s: pltpu.HBM(tuple(s.shape), s.dtype)
    outs = pl.pallas_call(
        body, name=name, out_shape=tuple(hbm(s) for s in srcs + lands),
        in_specs=[_HBM] * (n + m) + [_SEM, _SEM] + [pl.BlockSpec(memory_space=pl.ANY)] * len(after),
        out_specs=tuple([_HBM] * (n + m)), input_output_aliases={i: i for i in range(n + m)},
        compiler_params=pltpu.CompilerParams(has_side_effects=_EFFECT),
    )(*srcs, *lands, send_sems, recv_sems, *after)
    return list(outs[:n]), list(outs[n:])


def chip_major(w, groups=N_CHIPS):
    r, c = w.shape
    return w.reshape(r, groups, c // groups).transpose(1, 0, 2)


def from_chip_major(w):
    g, r, c = w.shape
    return w.transpose(1, 0, 2).reshape(r, g * c)


def _cd_in_pad(w):
    a = C_Q_RANK + C_KV_RANK
    z = lambda n: jnp.zeros((w.shape[0], n), w.dtype)
    return jnp.concatenate([w[:, :a], z(C_NOPE), w[:, a:a + C_ROPE], z(HEAD_PAD - C_NOPE - C_ROPE), w[:, a + C_ROPE:]], axis=1)


def _cd_in_unpad(w):
    a = C_Q_RANK + C_KV_RANK
    return jnp.concatenate([w[:, :a], w[:, a + C_NOPE:a + C_NOPE + C_ROPE], w[:, a + HEAD_PAD:]], axis=1)


def _pad_heads(w, width):
    r = w.shape[0]
    w = w.reshape(r, C_HEADS, width)
    return jnp.pad(w, ((0, 0), (0, 0), (0, HEAD_PAD - width))).reshape(r, _HW)


def _unpad_heads(w, width):
    r = w.shape[0]
    return w.reshape(r, C_HEADS, HEAD_PAD)[:, :, :width].reshape(r, C_HEADS * width)


def prepare_weights(p):
    q = dict(p)
    q["cd_w_in"] = _cd_in_pad(p["cd_w_in"])
    q["c_w_uq"] = _pad_heads(p["c_w_uq"], C_NOPE + C_ROPE)
    ukv = p["c_w_ukv"].reshape(C_KV_RANK, C_HEADS, C_NOPE + C_V)
    q["c_w_uk"] = _pad_heads(ukv[:, :, :C_NOPE].reshape(C_KV_RANK, -1), C_NOPE)
    q["c_w_uv"] = _pad_heads(ukv[:, :, C_NOPE:].reshape(C_KV_RANK, -1), C_V)
    wo = p["cd_w_out"]
    att_rows = jnp.pad(wo[:C_HEADS * C_V].reshape(C_HEADS, C_V, D_MODEL), ((0, 0), (0, HEAD_PAD - C_V), (0, 0)))
    q["cd_w_out"] = jnp.concatenate([att_rows.reshape(_HW, D_MODEL), wo[C_HEADS * C_V:]], axis=0)
    return q


def unprepare_grads(g):
    q = dict(g)
    q["cd_w_in"] = _cd_in_unpad(g["cd_w_in"])
    q["c_w_uq"] = _unpad_heads(g["c_w_uq"], C_NOPE + C_ROPE)
    uk = g.pop("c_w_uk").reshape(C_KV_RANK, C_HEADS, HEAD_PAD)[:, :, :C_NOPE]
    uv = g.pop("c_w_uv").reshape(C_KV_RANK, C_HEADS, HEAD_PAD)[:, :, :C_V]
    q.pop("c_w_uk", None)
    q.pop("c_w_uv", None)
    q["c_w_ukv"] = jnp.concatenate([uk, uv], axis=-1).reshape(C_KV_RANK, C_HEADS * (C_NOPE + C_V))
    wo = g["cd_w_out"]
    att = wo[:_HW].reshape(C_HEADS, HEAD_PAD, D_MODEL)[:, :C_V].reshape(C_HEADS * C_V, D_MODEL)
    q["cd_w_out"] = jnp.concatenate([att, wo[_HW:]], axis=0)
    return q


def rope_tables(positions):
    half = C_ROPE // 2
    inv_freq = ROPE_THETA ** (-jnp.arange(half, dtype=F32) / half)
    ang = positions.astype(F32)[:, None] * inv_freq
    cos, sin = jnp.cos(ang), jnp.sin(ang)
    s = positions.shape[0]
    z = lambda n: jnp.zeros((s, n), F32)
    cs = jnp.concatenate([jnp.ones((s, C_NOPE), F32), cos, cos, z(HEAD_PAD - C_NOPE - C_ROPE)], axis=1)
    s1 = jnp.concatenate([z(C_NOPE), -sin, z(HEAD_PAD - C_NOPE - half)], axis=1)
    s2 = jnp.concatenate([z(C_NOPE + half), sin, z(HEAD_PAD - C_NOPE - C_ROPE)], axis=1)
    return cs, s1, s2


_UP_COLS = 2 * D_FF // N_CHIPS


def ffn_fwd(h2, w, l, late_down=None):
    zf = matmul(h2, w["ffn_w_up"][l], "nn", BF16, f"ffn_up{l}", gb=N_CHIPS, go=2, tn=_UP_COLS)
    if late_down is not None:
        late_down(zf)
    a, f = ffn_act_down(zf, w["ffn_conv_w"][l], w["ffn_w_down"][l], f"ffn_act_down{l}")
    return f, (zf, a)


def ffn_bwd(df, h2, saved, w, l):
    zf, a = saved
    da = matmul(df, w["ffn_w_down"][l], "nt", BF16, f"ffn_down_dx{l}", tn=D_FF // 2)
    d_down = matmul(a, df, "tn", BF16, f"ffn_down_dw{l}", tm=D_FF // 2)
    dzf, d_conv = ffn_act_bwd(zf, w["ffn_conv_w"][l], da, f"ffn_act_bwd{l}")
    dh2 = matmul(dzf, w["ffn_w_up"][l], "nt", BF16, f"ffn_up_dx{l}", ga=2, gb=N_CHIPS, tk=_UP_COLS, tn=D_MODEL)
    d_up = matmul(h2, dzf, "tn", BF16, f"ffn_up_dw{l}", gb=2, go=N_CHIPS, tn=_UP_COLS)
    d_conv = d_conv.transpose(1, 0, 2).reshape(3, 2 * D_FF)
    return dh2, dict(ffn_w_down=d_down, ffn_conv_w=d_conv, ffn_w_up=d_up)


def mixer0_fwd(h, w):
    z = matmul(h, w["ab_w_in"], "nn", BF16, "ab_in", gb=N_CHIPS)
    ycat = pool_fwd(z, w["b_mix_w"], w["b_scale"], gconv_fwd(z, w["a_conv_w"]))
    y = matmul(ycat, w["ab_w_out"], "nn", BF16, "ab_out", tn=D_MODEL)
    return y, (z, ycat)


def mixer0_bwd(dy, h, saved, w):
    z, ycat = saved
    grads = {}
    dycat = matmul(dy, w["ab_w_out"], "nt", BF16, "ab_out_dx")
    grads["ab_w_out"] = matmul(ycat, dy, "tn", BF16, "ab_out_dw")
    db, dc, da, d_conv = gconv_bwd(z, w["a_conv_w"], dycat)
    dp, d_mix, d_scale = pool_bwd(z, w["b_mix_w"], w["b_scale"], dycat)
    dz = jnp.concatenate([db, dc, da, dp], axis=1)
    dh = matmul(dz, w["ab_w_in"], "nt", BF16, "ab_in_dx", gb=N_CHIPS, tn=D_MODEL)
    grads["ab_w_in"] = matmul(h, dz, "tn", BF16, "ab_in_dw", go=N_CHIPS)
    grads.update(a_conv_w=d_conv, b_mix_w=d_mix, b_scale=d_scale)
    return dh, grads


def mixer1_fwd(h, ropes, w):
    cs, s1, s2 = ropes
    z = matmul(h, w["cd_w_in"], "nn", BF16, "cd_in")
    bs_t = jnp.pad(w["d_b_s"].T, ((0, 0), (0, LANES - D_GROUPS)))
    qh, kh, vh = mla_pre_fwd(z, w["c_q_norm_g"], w["c_kv_norm_g"], w["c_w_uq"], w["c_w_uk"], w["c_w_uv"], cs, s1, s2)
    ycat = sgu_fwd(z, w["d_ln_g"], w["d_ln_b"], w["d_w_s"], bs_t, attn_fwd(qh, kh, vh))
    y = matmul(ycat, w["cd_w_out"], "nn", BF16, "cd_out", tn=D_MODEL)
    return y, (z, bs_t, qh, kh, vh, ycat)


def mixer1_bwd(dy, h, saved, ropes, w):
    cs, s1, s2 = ropes
    z, bs_t, qh, kh, vh, ycat = saved
    grads = {}
    dycat = matmul(dy, w["cd_w_out"], "nt", BF16, "cd_out_dx")
    grads["cd_w_out"] = matmul(ycat, dy, "tn", F32, "cd_out_dw")
    dqh, dkh, dvh = attn_bwd(qh, kh, vh, ycat, dycat)
    dzq, d_uq, d_uk, d_uv, d_gq, d_gkv = mla_pre_bwd(
        z, w["c_q_norm_g"], w["c_kv_norm_g"], w["c_w_uq"], w["c_w_uk"], w["c_w_uv"], cs, s1, s2, dqh, dkh, dvh)
    dzu, dzv, d_ws, d_bs, d_lg, d_lb = sgu_bwd(z, w["d_ln_g"], w["d_ln_b"], w["d_w_s"], bs_t, dycat, _HW // _DW)
    dz = jnp.concatenate([dzq, dzu, dzv], axis=1)
    dh = matmul(dz, w["cd_w_in"], "nt", BF16, "cd_in_dx", tn=D_MODEL)
    grads["cd_w_in"] = matmul(h, dz, "tn", F32, "cd_in_dw")
    grads.update(c_w_uq=d_uq, c_w_uk=d_uk, c_w_uv=d_uv, c_q_norm_g=d_gq, c_kv_norm_g=d_gkv, d_w_s=d_ws,
                 d_b_s=d_bs[:, :D_GROUPS].T, d_ln_g=d_lg, d_ln_b=d_lb)
    return dh, grads


class StepHooks:
    def weights(self, stage, after):
        pass

    def gradients(self, stage, grads, after):
        return 0.0


def run_step(x, tgt, mod, ropes, w, hooks):
    sh1, sc1, g1, sh2, sc2, g2 = range(N_MOD)
    mods = mod.reshape(2, 1, N_MOD * D_MODEL)
    n1 = w["norm1_g"].reshape(2, 1, D_MODEL)
    n2 = w["norm2_g"].reshape(2, 1, D_MODEL)
    final_g = Vec(w["final_norm_g"].reshape(1, 1, D_MODEL), 0, 0)

    hooks.weights("mix0", mod)
    h0 = modnorm_fwd(x, Vec(n1, 0, 0), Vec(mods, 0, sc1), Vec(mods, 0, sh1), "modnorm_0")
    y0, mix0 = mixer0_fwd(h0, w)
    x1, h1 = resid_modnorm_fwd(x, y0, Vec(mods, 0, g1), Vec(n2, 0, 0), Vec(mods, 0, sc2), Vec(mods, 0, sh2), "resid_modnorm_1")
    hooks.weights("up0", x1)
    f0, ffn0 = ffn_fwd(h1, w, 0, lambda act: hooks.weights("down0", act))
    x2, h2 = resid_modnorm_fwd(x1, f0, Vec(mods, 0, g2), Vec(n1, 1, 0), Vec(mods, 1, sc1), Vec(mods, 1, sh1), "resid_modnorm_2")
    hooks.weights("mix1", x2)
    y1, mix1 = mixer1_fwd(h2, ropes, w)
    x3, h3 = resid_modnorm_fwd(x2, y1, Vec(mods, 1, g1), Vec(n2, 1, 0), Vec(mods, 1, sc2), Vec(mods, 1, sh2), "resid_modnorm_3")
    hooks.weights("ffn1", x3)
    f1, ffn1 = ffn_fwd(h3, w, 1)
    dres, d_final, loss, df1, dg2b = final_fused(x3, f1, Vec(mods, 1, g2), final_g, tgt)

    dh3, gf1 = ffn_bwd(df1, h3, ffn1, w, 1)
    late = mods + hooks.gradients("ffn1", gf1, dh3)
    dres, dsh2b, dsc2b, dn2b, dy1, dg1b = norm_gate_bwd(
        x3, dh3, Vec(n2, 1, 0), Vec(late, 1, sc2), dres, y1, Vec(late, 1, g1), "norm_gate_bwd_3")
    dh2, gm1 = mixer1_bwd(dy1, h2, mix1, ropes, w)
    late = mods + hooks.gradients("mix1", gm1, dh2)
    dres, dsh1b, dsc1b, dn1b, df0, dg2a = norm_gate_bwd(
        x2, dh2, Vec(n1, 1, 0), Vec(late, 1, sc1), dres, f0, Vec(late, 0, g2), "norm_gate_bwd_2")
    dh1, gf0 = ffn_bwd(df0, h1, ffn0, w, 0)
    late = mods + hooks.gradients("ffn0", gf0, dh1)
    dres, dsh2a, dsc2a, dn2a, dy0, dg1a = norm_gate_bwd(
        x1, dh1, Vec(n2, 0, 0), Vec(late, 0, sc2), dres, y0, Vec(late, 0, g1), "norm_gate_bwd_1")
    dh0, gm0 = mixer0_bwd(dy0, h0, mix0, w)
    late = mods + hooks.gradients("mix0", gm0, dh0)
    grad_x, dsh1a, dsc1a, dn1a = norm_bwd(x, dh0, Vec(n1, 0, 0), Vec(late, 0, sc1), dres, "norm_bwd_0")

    dmod = jnp.concatenate([jnp.concatenate([dsh1a, dsc1a, dg1a, dsh2a, dsc2a, dg2a], axis=1),
                            jnp.concatenate([dsh1b, dsc1b, dg1b, dsh2b, dsc2b, dg2b], axis=1)], axis=0)
    norms = dict(norm1_g=jnp.concatenate([dn1a, dn1b], axis=0), norm2_g=jnp.concatenate([dn2a, dn2b], axis=0),
                 final_norm_g=d_final)
    return loss, grad_x, dmod, dict(mix0=gm0, ffn0=gf0, mix1=gm1, ffn1=gf1, norms=norms)


def merge_grads(by_stage):
    grads = {**by_stage["mix0"], **by_stage["mix1"], **by_stage["norms"]}
    for k in ("ffn_w_down", "ffn_w_up"):
        grads[k] = [by_stage["ffn0"][k], by_stage["ffn1"][k]]
    grads["ffn_conv_w"] = jnp.stack([by_stage["ffn0"]["ffn_conv_w"], by_stage["ffn1"]["ffn_conv_w"]])
    return grads


_WEIGHTS = ("ada_w", "ada_b", "norm1_g", "norm2_g", "ab_w_in", "a_conv_w", "b_mix_w", "b_scale", "ab_w_out", "cd_w_in",
            "c_q_norm_g", "c_w_uq", "c_kv_norm_g", "c_w_ukv", "d_ln_g", "d_ln_b", "d_w_s", "d_b_s", "cd_w_out",
            "ffn_w_up", "ffn_conv_w", "ffn_w_down", "final_norm_g")
_INPUTS = ("x", "c", "positions") + _WEIGHTS + ("loss_target",) + tuple("m_" + n for n in _WEIGHTS) + tuple(
    "v_" + n for n in _WEIGHTS)

def _pack_rows(parts, rows, dtype):
    flat = jnp.concatenate([p.reshape(-1).astype(dtype) for p in parts])
    return jnp.pad(flat, (0, rows * LANES - flat.shape[0])).reshape(rows, LANES)


def _rows_major(w):
    r, c = w.shape
    return w.reshape(N_CHIPS, r // N_CHIPS, c)


def start_gather(shards, tag, after=()):
    lands = [_sds((N_CHIPS,) + s.shape, s.dtype) for s in shards]
    return split_start("gather_start_" + tag, GATHER, shards, lands, after)


def finish_gather(handle, chip, tag, after):
    shards, lands = split_wait("gather_wait_" + tag, GATHER, handle, after)
    lands = forward_halves(lands, "gather_forward_" + tag)
    return [lax.dynamic_update_index_in_dim(o, s, chip, 0) for o, s in zip(lands, shards)]


def start_reduce(gs, core, tag):
    recv = swap_halves(gs, "swap_halves_" + tag)
    pairs = pair_sums(gs, recv, core, "pair_sums_" + tag)
    lands = [_sds((N_CHIPS - 1,) + p.shape[1:], p.dtype) for p in pairs]
    return split_start("exchange_start_" + tag, EXCHANGE, pairs, lands)


def finish_reduce(handle, chip, core, tag, after):
    pairs, others = split_wait("exchange_wait_" + tag, EXCHANGE, handle, after)
    halves = chip_sums(pairs, others, chip, core, "chip_sums_" + tag)
    full = join_halves(halves, "join_halves_" + tag)
    return [f.reshape(f.shape[1] * 2, f.shape[2]) for f in full]


_SMALL_SHARDED = (("a_conv_w", (3, 128), 1), ("c_q_norm_g", (1, 64), 1), ("d_ln_g", (1, 128), 1), ("d_ln_b", (1, 128), 1),
                  ("ffn_conv_w", (2, 3, 2 * D_FF // N_CHIPS), 2))
_SMALL_GRADS = (("norm1_g", (2, D_MODEL)), ("norm2_g", (2, D_MODEL)), ("b_mix_w", (4, 128, 128)), ("b_scale", (1, 512)),
                ("c_kv_norm_g", (1, 128)), ("d_w_s", (4, 128, 128)), ("d_b_s", (4, 128)), ("final_norm_g", (1, D_MODEL)),
                ("a_conv_w", (3, 512)), ("c_q_norm_g", (1, 256)), ("d_ln_g", (1, 512)), ("d_ln_b", (1, 512)),
                ("ffn_conv_w", (2, 3, 2 * D_FF)))


def _size(shape):
    n = 1
    for d in shape:
        n *= d
    return n


def kernel(x, c, positions, ada_w, ada_b, norm1_g, norm2_g, ab_w_in, a_conv_w, b_mix_w, b_scale, ab_w_out, cd_w_in, c_q_norm_g, c_w_uq, c_kv_norm_g, c_w_ukv, d_ln_g, d_ln_b, d_w_s, d_b_s, cd_w_out, ffn_w_up, ffn_conv_w, ffn_w_down, final_norm_g, loss_target, m_ada_w, m_ada_b, m_norm1_g, m_norm2_g, m_ab_w_in, m_a_conv_w, m_b_mix_w, m_b_scale, m_ab_w_out, m_cd_w_in, m_c_q_norm_g, m_c_w_uq, m_c_kv_norm_g, m_c_w_ukv, m_d_ln_g, m_d_ln_b, m_d_w_s, m_d_b_s, m_cd_w_out, m_ffn_w_up, m_ffn_conv_w, m_ffn_w_down, m_final_norm_g, v_ada_w, v_ada_b, v_norm1_g, v_norm2_g, v_ab_w_in, v_a_conv_w, v_b_mix_w, v_b_scale, v_ab_w_out, v_cd_w_in, v_c_q_norm_g, v_c_w_uq, v_c_kv_norm_g, v_c_w_ukv, v_d_ln_g, v_d_ln_b, v_d_w_s, v_d_b_s, v_cd_w_out, v_ffn_w_up, v_ffn_conv_w, v_ffn_w_down, v_final_norm_g):
    args = (x, c, positions, ada_w, ada_b, norm1_g, norm2_g, ab_w_in, a_conv_w, b_mix_w, b_scale, ab_w_out, cd_w_in, c_q_norm_g, c_w_uq, c_kv_norm_g, c_w_ukv, d_ln_g, d_ln_b, d_w_s, d_b_s, cd_w_out, ffn_w_up, ffn_conv_w, ffn_w_down, final_norm_g, loss_target, m_ada_w, m_ada_b, m_norm1_g, m_norm2_g, m_ab_w_in, m_a_conv_w, m_b_mix_w, m_b_scale, m_ab_w_out, m_cd_w_in, m_c_q_norm_g, m_c_w_uq, m_c_kv_norm_g, m_c_w_ukv, m_d_ln_g, m_d_ln_b, m_d_w_s, m_d_b_s, m_cd_w_out, m_ffn_w_up, m_ffn_conv_w, m_ffn_w_down, m_final_norm_g, v_ada_w, v_ada_b, v_norm1_g, v_norm2_g, v_ab_w_in, v_a_conv_w, v_b_mix_w, v_b_scale, v_ab_w_out, v_cd_w_in, v_c_q_norm_g, v_c_w_uq, v_c_kv_norm_g, v_c_w_ukv, v_d_ln_g, v_d_ln_b, v_d_w_s, v_d_b_s, v_cd_w_out, v_ffn_w_up, v_ffn_conv_w, v_ffn_w_down, v_final_norm_g)
    a = dict(zip(_INPUTS, args, strict=True))
    xi, yi, ci = _place()
    chip = 2 * xi + yi
    dev = 4 * xi + 2 * yi + ci
    x = a["x"][0]
    tgt = a["loss_target"][0]

    bf = lambda t: t.astype(BF16)
    mix0_handle, tok = start_gather([bf(a["ab_w_in"][0]), bf(a["ab_w_out"][0])], "mix0")
    up0_16, down0_16, up1_16, down1_16 = [bf(a[n][l]) for l in (0, 1) for n in ("ffn_w_up", "ffn_w_down")]
    mix1_16 = [bf(a[n][0]) for n in ("cd_w_in", "c_w_uq", "c_w_ukv", "cd_w_out")]

    small_parts = [a["c"] + tok] + [a[n] for n, _, _ in _SMALL_SHARDED]
    rows1 = -(-sum(p.size for p in small_parts) // LANES // 8) * 8
    g1 = all_gather8(_pack_rows(small_parts, rows1, F32), "gather_small",
                     [up0_16, down0_16, up1_16, down1_16, mix1_16[0], mix1_16[3]]).reshape(N_DEV, rows1 * LANES)
    c_all = g1[:, :D_MODEL]
    per_chip = g1[0::2]
    small_full = {}
    off = D_MODEL
    for n, shp, axis in _SMALL_SHARDED:
        piece = per_chip[:, off:off + _size(shp)].reshape((N_CHIPS,) + shp)
        small_full[n] = jnp.concatenate([piece[k] for k in range(N_CHIPS)], axis=axis)
        off += _size(shp)

    merge = lambda t: t.reshape(t.shape[0] * t.shape[1], t.shape[2])
    w = dict(norm1_g=a["norm1_g"], norm2_g=a["norm2_g"], b_mix_w=a["b_mix_w"][0], b_scale=a["b_scale"],
             c_kv_norm_g=a["c_kv_norm_g"], d_w_s=a["d_w_s"][0], d_b_s=a["d_b_s"][0],
             final_norm_g=a["final_norm_g"].reshape(1, D_MODEL), **small_full)

    ncol = N_MOD * D_MODEL // N_CHIPS
    ada_b_mine = lax.dynamic_slice_in_dim(a["ada_b"], chip * ncol, ncol, axis=1)
    mod_cols = ada_mod(c_all, a["ada_w"], ada_b_mine)
    g2_rows = all_gather8(mod_cols.reshape(-1, LANES), "gather_mod")
    g2 = g2_rows.reshape(N_DEV, 2, N_DEV, ncol)
    mod = lax.dynamic_index_in_dim(g2[0::2], dev, axis=2, keepdims=False)
    mod = mod.transpose(1, 0, 2).reshape(2, N_MOD * D_MODEL)

    late = [g2_rows]
    up0_handle, tok_a = start_gather([up0_16], "up0", late)
    down0_handle, tok_b = start_gather([down0_16], "down0", late)
    mix1_handle, tok_c = start_gather(mix1_16, "mix1", late)
    ffn1_handle, tok_d = start_gather([up1_16, down1_16], "ffn1", late)
    mod = mod + (tok_a + tok_b + tok_c + tok_d)

    ropes = rope_tables(a["positions"][0])
    cm16 = lambda t: chip_major(t).astype(BF16)
    w.update(ffn_w_up=[None, None], ffn_w_down=[None, None])
    handles = dict(mix0=mix0_handle, up0=up0_handle, down0=down0_handle, mix1=mix1_handle, ffn1=ffn1_handle)
    reducing, reduced = {}, {}

    class Hooks(StepHooks):
        def weights(self, stage, after):
            got = finish_gather(handles[stage], chip, stage, after)
            if stage == "mix0":
                w.update(ab_w_in=got[0], ab_w_out=merge(got[1]))
            elif stage == "up0":
                w["ffn_w_up"][0] = got[0]
            elif stage == "down0":
                w["ffn_w_down"][0] = merge(got[0])
            elif stage == "mix1":
                cd_in, uq, ukv, cd_out = got
                w.update(prepare_weights(dict(cd_w_in=from_chip_major(cd_in), c_w_uq=from_chip_major(uq),
                                              c_w_ukv=from_chip_major(ukv), cd_w_out=merge(cd_out))))
            else:
                w["ffn_w_up"][1], w["ffn_w_down"][1] = got[0], merge(got[1])

        def gradients(self, stage, grads, after):
            if stage in ("ffn0", "ffn1"):
                parts = [grads["ffn_w_up"], _rows_major(grads["ffn_w_down"])]
            elif stage == "mix1":
                grads.update(unprepare_grads(grads))
                parts = [cm16(grads["cd_w_in"]), cm16(grads["c_w_uq"]), cm16(grads["c_w_ukv"]),
                         _rows_major(grads["cd_w_out"]).astype(BF16)]
            else:
                parts = [grads["ab_w_in"], _rows_major(grads["ab_w_out"])]
            reducing[stage], tok = start_reduce(parts, ci, stage)
            before = {"mix1": "ffn1", "ffn0": "mix1", "mix0": "ffn0"}.get(stage)
            if before is not None:
                reduced[before] = finish_reduce(reducing[before], chip, ci, before, after)
            return tok

    loss, grad_x, dmod, by_stage = run_step(x, tgt, mod, ropes, w, Hooks())
    grads = merge_grads(by_stage)

    parts3 = [dmod] + [grads[n] for n, _ in _SMALL_GRADS] + [loss[0, 0]]
    rows3 = -(-sum(p.size for p in parts3) // LANES // 8) * 8
    small_handle, _ = split_start("small_grads_start", EVERYONE, [_pack_rows(parts3, rows3, F32)],
                                  [_sds((N_DEV, rows3, LANES))])
    red_up1, red_down1 = reduced["ffn1"]
    red_cd_in, red_uq, red_ukv, red_cd_out = reduced["mix1"]
    red_up0, red_down0 = reduced["ffn0"]
    out_grads = dict(cd_w_in=red_cd_in, c_w_uq=red_uq, c_w_ukv=red_ukv, cd_w_out=red_cd_out)
    per_layer = dict(ffn_w_up=(red_up0, red_up1), ffn_w_down=(red_down0, red_down1))
    updates = {}

    def update(n):
        if n in per_layer:
            updates[n] = adamw_layers(a[n], *per_layer[n], a["m_" + n], a["v_" + n], "adamw_" + n)
        else:
            updates[n] = adamw(a[n], out_grads[n].reshape(a[n].shape), a["m_" + n], a["v_" + n], "adamw_" + n)

    early =("ffn_w_up", "ffn_w_down", "cd_w_in", "c_w_uq", "c_w_ukv", "cd_w_out")
    for n in early:
        update(n)
    (mine,), (landed,) = split_wait("small_grads_wait", EVERYONE, small_handle, [updates[n][1] for n in early])
    g3 = lax.dynamic_update_index_in_dim(landed, mine, dev, 0)
    summed = sum8(g3).reshape(-1)
    nmod = 2 * N_MOD * D_MODEL
    out_grads["ada_b"] = summed[:nmod].reshape(2, N_MOD * D_MODEL)
    off = nmod
    for n, shp in _SMALL_GRADS:
        out_grads[n] = summed[off:off + _size(shp)].reshape(shp)
        off += _size(shp)
    loss = summed[off]
    for n, shp, axis in _SMALL_SHARDED:
        width = out_grads[n].shape[-1] // N_CHIPS
        out_grads[n] = lax.dynamic_slice_in_dim(out_grads[n], chip * width, width, axis=out_grads[n].ndim - 1)
    dmod_all = g3.reshape(N_DEV, rows3 * LANES)[:, :nmod].reshape(N_DEV, 2, N_MOD * D_MODEL)
    dmod_mine = lax.dynamic_slice_in_dim(dmod_all, chip * ncol, ncol, axis=2).transpose(1, 0, 2)
    updates["ada_w"] = adamw_ada(a["ada_w"], c_all, dmod_mine, a["m_ada_w"], a["v_ada_w"])

    red_in0, red_out0 = finish_reduce(reducing["mix0"], chip, ci, "mix0", updates["ada_w"][1])
    out_grads.update(ab_w_in=red_in0, ab_w_out=red_out0)

    for n in ("ab_w_in", "ab_w_out"):
        update(n)
    small = [n for n in _WEIGHTS if n not in updates]
    for n, res in zip(small, adamw_small([a[n] for n in small], [out_grads[n].reshape(a[n].shape) for n in small],
                                         [a["m_" + n] for n in small], [a["v_" + n] for n in small])):
        updates[n] = res
    return (loss, grad_x[None], *[updates[n][i] for i in range(4) for n in _WEIGHTS])
```

```python
import functools
from typing import NamedTuple

import jax
import jax.numpy as jnp
from jax import lax
from jax.experimental import pallas as pl
from jax.experimental.pallas import tpu as pltpu

F32 = jnp.float32
BF16 = jnp.bfloat16
EPS = 1e-6
D_MODEL = 1024
N_MOD = 6
A_WIDTH = 512
B_GROUPS = 4
C_HEADS = 8
C_NOPE = 64
C_ROPE = 32
C_V = 64
C_Q_RANK = 256
C_KV_RANK = 128
HEAD_PAD = 128
ROPE_THETA = 10000.0
D_GROUPS = 4
D_CHUNK = 128
D_FF = 2816
FF_UNIT = 128
ADAM_LR = 0.001
ADAM_B1 = 0.9
ADAM_B2 = 0.999
ADAM_EPS = 1e-08
ADAM_WD = 0.01
ADAM_STEP = 10
N_CHIPS = 4
N_DEV = 8
LANES = 128
VMEM_BIG = 56 * 1024 * 1024
MESH = pl.DeviceIdType.MESH


def _sds(shape, dtype=F32):
    return jax.ShapeDtypeStruct(tuple(shape), dtype)


def _tile(n, cap, mult=128):
    if n <= cap:
        return n
    best = None
    for t in range(mult, cap + 1, mult):
        if n % t == 0:
            best = t
    assert best is not None, (n, cap, mult)
    return best


def _params(dims=None, vmem=None):
    return pltpu.CompilerParams(dimension_semantics=dims, vmem_limit_bytes=vmem)


def _shift_down(v, k):
    r = pltpu.roll(v, k, axis=0)
    t = lax.broadcasted_iota(jnp.int32, v.shape, 0)
    return jnp.where(t >= k, r, 0.0)


def _shift_up(v, k):
    n = v.shape[0]
    r = pltpu.roll(v, n - k, axis=0)
    t = lax.broadcasted_iota(jnp.int32, v.shape, 0)
    return jnp.where(t < n - k, r, 0.0)


def _sigmoid(v):
    return 1.0 / (1.0 + jnp.exp(-v))


_GELU_C = 0.7978845608028654
_GELU_A = 0.044715


def _gelu(v):
    return 0.5 * v * (1.0 + jnp.tanh(_GELU_C * (v + _GELU_A * v * v * v)))


def _gelu_grad(v):
    th = jnp.tanh(_GELU_C * (v + _GELU_A * v * v * v))
    return 0.5 * (1.0 + th) + 0.5 * v * (1.0 - th * th) * _GELU_C * (1.0 + 3.0 * _GELU_A * v * v)


_NN = (((1,), (0,)), ((), ()))
_NT = (((1,), (1,)), ((), ()))
_TN = (((0,), (0,)), ((), ()))


def _dot(a, b, dims=_NN):
    return lax.dot_general(a, b, dims, preferred_element_type=F32)


def _logical(t, groups):
    return (t.shape[-2], t.shape[-1] * groups)


def _block(tr, tc, groups, cols, where):
    if groups == 1:
        return pl.BlockSpec((tr, tc), where)
    per = cols // groups // tc

    def index(i, j, s):
        r, c = where(i, j, s)
        return (c // per, r, c % per)

    return pl.BlockSpec((None, tr, tc), index)


def matmul(a, b, mode, out_dtype, name, ga=1, gb=1, go=1, tm=None, tn=None, tk=None):
    (ar, ac), (br, bc) = _logical(a, ga), _logical(b, gb)
    if mode == "nn":
        m, k, n = ar, ac, bc
        a_col, b_col = "k", "n"
    elif mode == "nt":
        m, k, n = ar, ac, br
        a_col, b_col = "k", "k"
    else:
        k, m, n = ar, ac, bc
        a_col, b_col = "m", "n"
    limit = {"m": m, "n": n // go, "k": k}
    limit[a_col] = min(limit[a_col], ac // ga)
    limit[b_col] = min(limit[b_col], bc // gb)
    tm = tm or _tile(limit["m"], 2048, 128 if mode == "tn" else 16)
    tn = tn or _tile(limit["n"], 512)
    tk = tk or _tile(limit["k"], 2048, 16 if mode == "tn" else 128)
    nk = k // tk
    if mode == "nn":
        a_spec = _block(tm, tk, ga, ac, lambda i, j, s: (i, s))
        b_spec = _block(tk, tn, gb, bc, lambda i, j, s: (s, j))
        dims = _NN
    elif mode == "nt":
        a_spec = _block(tm, tk, ga, ac, lambda i, j, s: (i, s))
        b_spec = _block(tn, tk, gb, bc, lambda i, j, s: (j, s))
        dims = _NT
    else:
        a_spec = _block(tk, tm, ga, ac, lambda i, j, s: (s, i))
        b_spec = _block(tk, tn, gb, bc, lambda i, j, s: (s, j))
        dims = _TN
    o_spec = _block(tm, tn, go, n, lambda i, j, s: (i, j))
    out_shape = _sds((m, n), out_dtype) if go == 1 else _sds((go, m, n // go), out_dtype)

    def body(a_ref, b_ref, o_ref, acc_ref):
        s = pl.program_id(2)

        @pl.when(s == 0)
        def _():
            acc_ref[...] = jnp.zeros_like(acc_ref)

        acc_ref[...] += _dot(a_ref[...], b_ref[...], dims)

        @pl.when(s == nk - 1)
        def _():
            o_ref[...] = acc_ref[...].astype(o_ref.dtype)

    return pl.pallas_call(
        body, name=name, out_shape=out_shape, grid=(m // tm, n // tn, nk),
        in_specs=[a_spec, b_spec], out_specs=o_spec,
        scratch_shapes=[pltpu.VMEM((tm, tn), F32)],
        compiler_params=_params(("parallel", "parallel", "arbitrary"), VMEM_BIG),
    )(a, b)


def _rows(tm, n):
    return pl.BlockSpec((tm, n), lambda i: (i, 0))


def _vec(n):
    return pl.BlockSpec((1, n), lambda i: (0, 0))


class Vec(NamedTuple):
    array: jax.Array
    row: int
    col: int


def _vec_in(v, d):
    return pl.BlockSpec((None, 1, d), lambda i: (v.row, 0, v.col))


def modnorm_fwd(x, g, sc, sh, name):
    s, d = x.shape
    tm = _tile(s, 256, 8)

    def body(x_ref, g_ref, sc_ref, sh_ref, o_ref):
        xv = x_ref[...]
        r = lax.rsqrt(jnp.mean(xv * xv, axis=-1, keepdims=True) + EPS)
        o_ref[...] = ((xv * r) * g_ref[...] * (1.0 + sc_ref[...]) + sh_ref[...]).astype(BF16)

    return pl.pallas_call(
        body, name=name, out_shape=_sds((s, d), BF16), grid=(s // tm,),
        in_specs=[_rows(tm, d), _vec_in(g, d), _vec_in(sc, d), _vec_in(sh, d)], out_specs=_rows(tm, d),
        compiler_params=_params(("parallel",)),
    )(x, g.array, sc.array, sh.array)


def norm_bwd(x, dh, g, sc, dres, name):
    s, d = x.shape
    tm = _tile(s, 256, 8)
    nsteps = s // tm

    def body(x_ref, dh_ref, g_ref, sc_ref, dr_ref, dx_ref, dsh_ref, dsc_ref, dg_ref, a2_ref):
        i = pl.program_id(0)

        @pl.when(i == 0)
        def _():
            dsh_ref[...] = jnp.zeros_like(dsh_ref)
            a2_ref[...] = jnp.zeros_like(a2_ref)

        xv = x_ref[...]
        dh = dh_ref[...].astype(F32)
        r = lax.rsqrt(jnp.mean(xv * xv, axis=-1, keepdims=True) + EPS)
        xh = xv * r
        dsh_ref[...] += jnp.sum(dh, axis=0, keepdims=True)
        a2_ref[...] += jnp.sum(dh * xh, axis=0, keepdims=True)
        dxh = dh * (g_ref[...] * (1.0 + sc_ref[...]))
        dx = r * (dxh - xh * jnp.mean(dxh * xh, axis=-1, keepdims=True))
        dx_ref[...] = dr_ref[...] + dx

        @pl.when(i == nsteps - 1)
        def _():
            dsc_ref[...] = a2_ref[...] * g_ref[...]
            dg_ref[...] = a2_ref[...] * (1.0 + sc_ref[...])

    return pl.pallas_call(
        body, name=name, out_shape=(_sds((s, d)), _sds((1, d)), _sds((1, d)), _sds((1, d))), grid=(nsteps,),
        in_specs=[_rows(tm, d), _rows(tm, d), _vec_in(g, d), _vec_in(sc, d), _rows(tm, d)],
        out_specs=(_rows(tm, d), _vec(d), _vec(d), _vec(d)),
        scratch_shapes=[pltpu.VMEM((1, d), F32)],
        compiler_params=_params(("arbitrary",)),
    )(x, dh, g.array, sc.array, dres)


def resid_modnorm_fwd(x, y, gate, g, sc, sh, name):
    s, d = x.shape
    tm = _tile(s, 256, 8)

    def body(x_ref, y_ref, gate_ref, g_ref, sc_ref, sh_ref, xo_ref, h_ref):
        xv = x_ref[...] + gate_ref[...] * y_ref[...].astype(F32)
        xo_ref[...] = xv
        r = lax.rsqrt(jnp.mean(xv * xv, axis=-1, keepdims=True) + EPS)
        h_ref[...] = ((xv * r) * g_ref[...] * (1.0 + sc_ref[...]) + sh_ref[...]).astype(BF16)

    return pl.pallas_call(
        body, name=name, out_shape=(_sds((s, d)), _sds((s, d), BF16)), grid=(s // tm,),
        in_specs=[_rows(tm, d), _rows(tm, d), _vec_in(gate, d), _vec_in(g, d), _vec_in(sc, d), _vec_in(sh, d)],
        out_specs=(_rows(tm, d), _rows(tm, d)),
        compiler_params=_params(("parallel",)),
    )(x, y, gate.array, g.array, sc.array, sh.array)


def norm_gate_bwd(x, dh, g, sc, dres, y, gate, name):
    s, d = x.shape
    tm = _tile(s, 256, 8)
    nsteps = s // tm

    def body(x_ref, dh_ref, g_ref, sc_ref, dr_ref, y_ref, gate_ref, dx_ref, dsh_ref, dsc_ref, dg_ref, dy_ref,
             dgate_ref, a2_ref):
        i = pl.program_id(0)

        @pl.when(i == 0)
        def _():
            dsh_ref[...] = jnp.zeros_like(dsh_ref)
            a2_ref[...] = jnp.zeros_like(a2_ref)
            dgate_ref[...] = jnp.zeros_like(dgate_ref)

        xv = x_ref[...]
        dh = dh_ref[...].astype(F32)
        r = lax.rsqrt(jnp.mean(xv * xv, axis=-1, keepdims=True) + EPS)
        xh = xv * r
        dsh_ref[...] += jnp.sum(dh, axis=0, keepdims=True)
        a2_ref[...] += jnp.sum(dh * xh, axis=0, keepdims=True)
        dxh = dh * (g_ref[...] * (1.0 + sc_ref[...]))
        dr = dr_ref[...] + r * (dxh - xh * jnp.mean(dxh * xh, axis=-1, keepdims=True))
        dx_ref[...] = dr
        dy_ref[...] = (dr * gate_ref[...]).astype(BF16)
        dgate_ref[...] += jnp.sum(dr * y_ref[...].astype(F32), axis=0, keepdims=True)

        @pl.when(i == nsteps - 1)
        def _():
            dsc_ref[...] = a2_ref[...] * g_ref[...]
            dg_ref[...] = a2_ref[...] * (1.0 + sc_ref[...])

    vec = _sds((1, d))
    return pl.pallas_call(
        body, name=name, out_shape=(_sds((s, d)), vec, vec, vec, _sds((s, d), BF16), vec), grid=(nsteps,),
        in_specs=[_rows(tm, d), _rows(tm, d), _vec_in(g, d), _vec_in(sc, d), _rows(tm, d), _rows(tm, d), _vec_in(gate, d)],
        out_specs=(_rows(tm, d), _vec(d), _vec(d), _vec(d), _rows(tm, d), _vec(d)),
        scratch_shapes=[pltpu.VMEM((1, d), F32)],
        compiler_params=_params(("arbitrary",)),
    )(x, dh, g.array, sc.array, dres, y, gate.array)


def final_fused(x, f, gate, g, tgt):
    s, d = x.shape
    tm = _tile(s, 256, 8)

    def body(x_ref, f_ref, gate_ref, g_ref, t_ref, dx_ref, dg_ref, loss_ref, df_ref, dgate_ref):
        @pl.when(pl.program_id(0) == 0)
        def _():
            dg_ref[...] = jnp.zeros_like(dg_ref)
            loss_ref[...] = jnp.zeros_like(loss_ref)
            dgate_ref[...] = jnp.zeros_like(dgate_ref)

        fv, gatev, gv = f_ref[...].astype(F32), gate_ref[...], g_ref[...]
        xv = x_ref[...] + gatev * fv
        r = lax.rsqrt(jnp.mean(xv * xv, axis=-1, keepdims=True) + EPS)
        xh = xv * r
        e = xh * gv - t_ref[...]
        row = jnp.sum(e * e, axis=-1, keepdims=True) * (0.5 / d)
        loss_ref[...] += jnp.sum(row, axis=0, keepdims=True)
        dy = e * (1.0 / d)
        dg_ref[...] += jnp.sum(dy * xh, axis=0, keepdims=True)
        dxh = dy * gv
        dx = r * (dxh - xh * jnp.mean(dxh * xh, axis=-1, keepdims=True))
        dx_ref[...] = dx
        df_ref[...] = (dx * gatev).astype(BF16)
        dgate_ref[...] += jnp.sum(dx * fv, axis=0, keepdims=True)

    vec = _sds((1, d))
    return pl.pallas_call(
        body, name="final_fused", out_shape=(_sds((s, d)), vec, _sds((1, LANES)), _sds((s, d), BF16), vec),
        grid=(s // tm,),
        in_specs=[_rows(tm, d), _rows(tm, d), _vec_in(gate, d), _vec_in(g, d), _rows(tm, d)],
        out_specs=(_rows(tm, d), _vec(d), _vec(LANES), _rows(tm, d), _vec(d)),
        compiler_params=_params(("arbitrary",)),
    )(x, f, gate.array, g.array, tgt)


def _taps(v):
    return _shift_down(v, 2), _shift_down(v, 1), v


def _conv3_taps(taps, w):
    return w[0:1, :] * taps[0] + w[1:2, :] * taps[1] + w[2:3, :] * taps[2]


def _conv3(v, w):
    return _conv3_taps(_taps(v), w)


def _conv3_t(dv, w):
    return w[0:1, :] * _shift_up(dv, 2) + w[1:2, :] * _shift_up(dv, 1) + w[2:3, :] * dv


def _conv3_dw_taps(dv, taps):
    return jnp.concatenate([jnp.sum(dv * t, axis=0, keepdims=True) for t in taps], axis=0)


def _conv3_dw(dv, v):
    return _conv3_dw_taps(dv, _taps(v))


def gconv_fwd(z, conv_w):
    s = z.shape[0]
    nb = A_WIDTH // LANES

    def body(b_ref, c_ref, a_ref, w_ref, o_ref):
        b, c, a = b_ref[...].astype(F32), c_ref[...].astype(F32), a_ref[...].astype(F32)
        o_ref[...] = (b * _conv3(c * a, w_ref[...])).astype(BF16)

    col = lambda off: pl.BlockSpec((s, LANES), lambda j: (0, off + j))
    return pl.pallas_call(
        body, name="gconv_fwd", out_shape=_sds((s, A_WIDTH + _B_WIDTH), BF16), grid=(nb,),
        in_specs=[col(0), col(nb), col(2 * nb), pl.BlockSpec((3, LANES), lambda j: (0, j))],
        out_specs=pl.BlockSpec((s, LANES), lambda j: (0, j)),
        compiler_params=_params(("parallel",), VMEM_BIG),
    )(z, z, z, conv_w)


def gconv_bwd(z, conv_w, dycat):
    s = z.shape[0]
    nb = A_WIDTH // LANES

    def body(b_ref, c_ref, a_ref, w_ref, dy_ref, db_ref, dc_ref, da_ref, dw_ref):
        c, a, w, dy = c_ref[...].astype(F32), a_ref[...].astype(F32), w_ref[...], dy_ref[...].astype(F32)
        ca = c * a
        db_ref[...] = (dy * _conv3(ca, w)).astype(BF16)
        dconv = dy * b_ref[...].astype(F32)
        dw_ref[...] = _conv3_dw(dconv, ca)
        dca = _conv3_t(dconv, w)
        dc_ref[...] = (dca * a).astype(BF16)
        da_ref[...] = (dca * c).astype(BF16)

    col = lambda off: pl.BlockSpec((s, LANES), lambda j: (0, off + j))
    wspec = pl.BlockSpec((3, LANES), lambda j: (0, j))
    part = _sds((s, A_WIDTH), BF16)
    return pl.pallas_call(
        body, name="gconv_bwd", out_shape=(part, part, part, _sds((3, A_WIDTH))), grid=(nb,),
        in_specs=[col(0), col(nb), col(2 * nb), wspec, col(0)],
        out_specs=(col(0), col(0), col(0), wspec),
        compiler_params=_params(("parallel",), VMEM_BIG),
    )(z, z, z, conv_w, dycat)


def _pool_counts(s, w):
    t = lax.broadcasted_iota(jnp.int32, (s, 1), 0)
    return jnp.minimum(t + 1, w).astype(F32)


def _pooled(p, levels):
    acc = p
    for lv in range(levels):
        acc = acc + _shift_down(acc, 2 ** lv)
    return acc / _pool_counts(p.shape[0], 2 ** levels) - p


_B_WIDTH = B_GROUPS * LANES


def pool_fwd(z, mix_w, scale, ycat):
    s = z.shape[0]

    def body(p_ref, m_ref, sc_ref, ycat_ref, o_ref):
        del ycat_ref
        for g in range(B_GROUPS):
            cols = slice(g * LANES, (g + 1) * LANES)
            pooled = _pooled(p_ref[:, cols].astype(F32), g + 1)
            y = _dot(pooled.astype(BF16), m_ref[g].astype(BF16))
            o_ref[:, cols] = (y * sc_ref[:, cols]).astype(BF16)

    return pl.pallas_call(
        body, name="pool_fwd", out_shape=_sds(ycat.shape, BF16), grid=(1,),
        in_specs=[pl.BlockSpec((s, _B_WIDTH), lambda i: (0, 3 * A_WIDTH // _B_WIDTH)),
                  pl.BlockSpec((B_GROUPS, LANES, LANES), lambda i: (0, 0, 0)), pl.BlockSpec((1, _B_WIDTH), lambda i: (0, 0)),
                  pl.BlockSpec(memory_space=pl.ANY)],
        out_specs=pl.BlockSpec((s, _B_WIDTH), lambda i: (0, A_WIDTH // _B_WIDTH)),
        input_output_aliases={3: 0},
        compiler_params=_params(("arbitrary",), VMEM_BIG),
    )(z, mix_w, scale, ycat)


def pool_bwd(z, mix_w, scale, dycat):
    s = z.shape[0]

    def body(p_ref, m_ref, sc_ref, dy_ref, dp_ref, dm_ref, dsc_ref):
        for g in range(B_GROUPS):
            cols = slice(g * LANES, (g + 1) * LANES)
            pooled = _pooled(p_ref[:, cols].astype(F32), g + 1)
            mw = m_ref[g].astype(BF16)
            pb = pooled.astype(BF16)
            dy = dy_ref[:, cols].astype(F32)
            dsc_ref[:, cols] = jnp.sum(dy * _dot(pb, mw), axis=0, keepdims=True)
            dmix = (dy * sc_ref[:, cols]).astype(BF16)
            dm_ref[g] = _dot(pb, dmix, _TN)
            dpool = _dot(dmix, mw, _NT)
            acc = dpool / _pool_counts(s, 2 ** (g + 1))
            for lv in range(g + 1):
                acc = acc + _shift_up(acc, 2 ** lv)
            dp_ref[:, cols] = (acc - dpool).astype(BF16)

    wide = lambda c: pl.BlockSpec((s, _B_WIDTH), lambda i: (0, c))
    mspec = pl.BlockSpec((B_GROUPS, LANES, LANES), lambda i: (0, 0, 0))
    vspec = pl.BlockSpec((1, _B_WIDTH), lambda i: (0, 0))
    return pl.pallas_call(
        body, name="pool_bwd", out_shape=(_sds((s, _B_WIDTH), BF16), _sds((B_GROUPS, LANES, LANES)), _sds((1, _B_WIDTH))),
        grid=(1,), in_specs=[wide(3 * A_WIDTH // _B_WIDTH), mspec, vspec, wide(A_WIDTH // _B_WIDTH)],
        out_specs=(wide(0), mspec, vspec),
        compiler_params=_params(("arbitrary",), VMEM_BIG),
    )(z, mix_w, scale, dycat)


_FF_BLOCKS = D_FF // FF_UNIT


def _ff_spec(s):
    return pl.BlockSpec((2, s, FF_UNIT), lambda j: (0, 0, j))


def _ff_wspecs():
    return [pl.BlockSpec((3, FF_UNIT), lambda j: (0, j)), pl.BlockSpec((3, FF_UNIT), lambda j: (0, _FF_BLOCKS + j))]


_FF_ROWS = 64
_FF_HALO = 16


def _chunk_taps(z_ref, half, c):
    start = pl.multiple_of(c * _FF_ROWS, _FF_ROWS)
    before = pl.multiple_of(jnp.maximum(c * _FF_ROWS - _FF_HALO, 0), _FF_HALO)
    halo = z_ref[half, pl.ds(before, _FF_HALO), :].astype(F32)
    halo = jnp.where(c > 0, halo, 0.0)
    win = jnp.concatenate([halo, z_ref[half, pl.ds(start, _FF_ROWS), :].astype(F32)], axis=0)
    return tuple(pltpu.roll(win, k, axis=0)[_FF_HALO:] for k in (2, 1)) + (win[_FF_HALO:],)


def _fold8(v):
    acc = v[0:8]
    for r in range(8, v.shape[0], 8):
        acc = acc + v[r:r + 8]
    return acc


_FF_CHUNK = 256


def ffn_act_down(zf, conv_w, w_down, name):
    s, d = zf.shape[1], w_down.shape[1]
    nk = D_FF // _FF_CHUNK
    chunk = lambda k: jnp.minimum(k, nk - 1)

    def body(z_ref, wg_ref, wu_ref, wd_ref, a_ref, f_ref, held_ref, acc_ref):
        k = pl.program_id(0)

        @pl.when(k == 0)
        def _():
            held_ref[...] = jnp.zeros_like(held_ref)
            acc_ref[...] = jnp.zeros_like(acc_ref)

        acc_ref[...] += _dot(held_ref[(k + 1) % 2], wd_ref[...])
        g = _conv3(z_ref[0].astype(F32), wg_ref[...])
        u = _conv3(z_ref[1].astype(F32), wu_ref[...])
        act = (g * _sigmoid(g) * u).astype(BF16)
        a_ref[...] = act
        held_ref[k % 2] = act

        @pl.when(k == nk)
        def _():
            f_ref[...] = acc_ref[...].astype(BF16)

    return pl.pallas_call(
        body, name=name, out_shape=(_sds((s, D_FF), BF16), _sds((s, d), BF16)), grid=(nk + 1,),
        in_specs=[pl.BlockSpec((2, s, _FF_CHUNK), lambda k: (0, 0, chunk(k))),
                  pl.BlockSpec((3, _FF_CHUNK), lambda k: (0, chunk(k))),
                  pl.BlockSpec((3, _FF_CHUNK), lambda k: (0, nk + chunk(k))),
                  pl.BlockSpec((_FF_CHUNK, d), lambda k: (jnp.maximum(k - 1, 0), 0))],
        out_specs=(pl.BlockSpec((s, _FF_CHUNK), lambda k: (0, chunk(k))), pl.BlockSpec((s, d), lambda k: (0, 0))),
        scratch_shapes=[pltpu.VMEM((2, s, _FF_CHUNK), BF16), pltpu.VMEM((s, d), F32)],
        compiler_params=_params(("arbitrary",), VMEM_BIG),
    )(zf, conv_w, conv_w, w_down)


def ffn_act_bwd(zf, conv_w, da, name):
    s = zf.shape[1]
    assert s % _FF_ROWS == 0
    nchunks = s // _FF_ROWS

    def body(z_ref, wg_ref, wu_ref, da_ref, dz_ref, dw_ref, dg_ref, du_ref):
        wg, wu = wg_ref[...], wu_ref[...]

        def first(c, acc):
            rows = pl.ds(pl.multiple_of(c * _FF_ROWS, _FF_ROWS), _FF_ROWS)
            tg, tu = _chunk_taps(z_ref, 0, c), _chunk_taps(z_ref, 1, c)
            g = _conv3_taps(tg, wg)
            u = _conv3_taps(tu, wu)
            dav = da_ref[rows, :].astype(F32)
            sg = _sigmoid(g)
            dg = dav * u * (sg * (1.0 + g * (1.0 - sg)))
            du = dav * (g * sg)
            dg_ref[rows, :] = dg
            du_ref[rows, :] = du
            return tuple(a + _fold8(d * t) for a, (d, t) in zip(acc, [(dg, t) for t in tg] + [(du, t) for t in tu]))

        zero = jnp.zeros((8, FF_UNIT), F32)
        acc = lax.fori_loop(0, nchunks, first, (zero,) * 6)
        sums = [jnp.sum(a, axis=0, keepdims=True) for a in acc]
        dw_ref[0] = jnp.concatenate(sums[:3], axis=0)
        dw_ref[1] = jnp.concatenate(sums[3:], axis=0)

        tail = pl.ds(s, _FF_HALO)
        dg_ref[tail, :] = jnp.zeros((_FF_HALO, FF_UNIT), F32)
        du_ref[tail, :] = jnp.zeros((_FF_HALO, FF_UNIT), F32)
        span = _FF_ROWS + _FF_HALO

        def second(c, carry):
            start = pl.multiple_of(c * _FF_ROWS, _FF_ROWS)
            for half, (d_ref, w) in enumerate(((dg_ref, wg), (du_ref, wu))):
                win = d_ref[pl.ds(start, span), :]
                dz = (w[0:1, :] * pltpu.roll(win, span - 2, axis=0)[:_FF_ROWS]
                      + w[1:2, :] * pltpu.roll(win, span - 1, axis=0)[:_FF_ROWS] + w[2:3, :] * win[:_FF_ROWS])
                dz_ref[half, pl.ds(start, _FF_ROWS), :] = dz.astype(BF16)
            return carry

        lax.fori_loop(0, nchunks, second, 0)

    return pl.pallas_call(
        body, name=name, out_shape=(_sds((2, s, D_FF), BF16), _sds((2, 3, D_FF))), grid=(_FF_BLOCKS,),
        in_specs=[_ff_spec(s)] + _ff_wspecs() + [pl.BlockSpec((s, FF_UNIT), lambda j: (0, j))],
        out_specs=(_ff_spec(s), pl.BlockSpec((2, 3, FF_UNIT), lambda j: (0, 0, j))),
        scratch_shapes=[pltpu.VMEM((s + _FF_HALO, FF_UNIT), F32), pltpu.VMEM((s + _FF_HALO, FF_UNIT), F32)],
        compiler_params=_params(("parallel",), VMEM_BIG),
    )(zf, conv_w, conv_w, da)


def _rope(v, cs, s1, s2):
    return v * cs + pltpu.roll(v, LANES - C_ROPE // 2, axis=1) * s1 + pltpu.roll(v, C_ROPE // 2, axis=1) * s2


def _rope_t(dv, cs, s1, s2):
    return dv * cs + pltpu.roll(dv * s1, C_ROPE // 2, axis=1) + pltpu.roll(dv * s2, LANES - C_ROPE // 2, axis=1)


def _kpe_mask(shape):
    lane = lax.broadcasted_iota(jnp.int32, shape, 1)
    return (lane >= C_NOPE) & (lane < C_NOPE + C_ROPE)


def _rms(v, g):
    r = lax.rsqrt(jnp.mean(v * v, axis=-1, keepdims=True) + EPS)
    return v * r, r


def _rms_bwd(dn, xh, r, g):
    dxh = dn * g
    return r * (dxh - xh * jnp.mean(dxh * xh, axis=-1, keepdims=True)), jnp.sum(dn * xh, axis=0, keepdims=True)


_ZQ = C_Q_RANK + C_KV_RANK + HEAD_PAD
_HW = C_HEADS * HEAD_PAD


def mla_pre_fwd(z, gq, gkv, wq, wk, wv, cs, s1, s2):
    s = z.shape[0]
    tm = _tile(s, 256, 8)

    def body(z_ref, gq_ref, gkv_ref, wq_ref, wk_ref, wv_ref, cs_ref, s1_ref, s2_ref, q_ref, k_ref, v_ref):
        zv = z_ref[...].astype(F32)
        cst, s1t, s2t = cs_ref[...], s1_ref[...], s2_ref[...]
        qh, _ = _rms(zv[:, :C_Q_RANK], None)
        qn = (qh * gq_ref[...]).astype(BF16)
        q = _dot(qn, wq_ref[...])
        kh, _ = _rms(zv[:, C_Q_RANK:C_Q_RANK + C_KV_RANK], None)
        kvn = (kh * gkv_ref[...]).astype(BF16)
        k = _dot(kvn, wk_ref[...])
        v_ref[...] = _dot(kvn, wv_ref[...]).astype(BF16)
        kpe = _rope(zv[:, C_Q_RANK + C_KV_RANK:], cst, s1t, s2t)
        for h in range(C_HEADS):
            sl = slice(h * HEAD_PAD, (h + 1) * HEAD_PAD)
            q_ref[:, sl] = _rope(q[:, sl], cst, s1t, s2t).astype(BF16)
            k_ref[:, sl] = (k[:, sl] + kpe).astype(BF16)

    full = lambda r, c: pl.BlockSpec((r, c), lambda i: (0, 0))
    hw = _sds((s, _HW), BF16)
    return pl.pallas_call(
        body, name="mla_pre_fwd", out_shape=(hw, hw, hw), grid=(s // tm,),
        in_specs=[_rows(tm, _ZQ), _vec(C_Q_RANK), _vec(C_KV_RANK), full(C_Q_RANK, _HW), full(C_KV_RANK, _HW),
                  full(C_KV_RANK, _HW), _rows(tm, LANES), _rows(tm, LANES), _rows(tm, LANES)],
        out_specs=(_rows(tm, _HW), _rows(tm, _HW), _rows(tm, _HW)),
        compiler_params=_params(("parallel",), VMEM_BIG),
    )(z, gq, gkv, wq, wk, wv, cs, s1, s2)


def mla_pre_bwd(z, gq, gkv, wq, wk, wv, cs, s1, s2, dq, dk, dv):
    s = z.shape[0]
    tm = _tile(s, 256, 8)

    def body(z_ref, gq_ref, gkv_ref, wq_ref, wk_ref, wv_ref, cs_ref, s1_ref, s2_ref, dq_ref, dk_ref, dv_ref,
             dz_ref, dwq_ref, dwk_ref, dwv_ref, dgq_ref, dgkv_ref):
        @pl.when(pl.program_id(0) == 0)
        def _():
            dwq_ref[...] = jnp.zeros_like(dwq_ref)
            dwk_ref[...] = jnp.zeros_like(dwk_ref)
            dwv_ref[...] = jnp.zeros_like(dwv_ref)
            dgq_ref[...] = jnp.zeros_like(dgq_ref)
            dgkv_ref[...] = jnp.zeros_like(dgkv_ref)

        zv = z_ref[...].astype(F32)
        cst, s1t, s2t = cs_ref[...], s1_ref[...], s2_ref[...]
        gqv, gkvv = gq_ref[...], gkv_ref[...]
        qh, rq = _rms(zv[:, :C_Q_RANK], None)
        qn = (qh * gqv).astype(BF16)
        kh, rk = _rms(zv[:, C_Q_RANK:C_Q_RANK + C_KV_RANK], None)
        kvn = (kh * gkvv).astype(BF16)

        dqv = dq_ref[...].astype(F32)
        dqp = jnp.concatenate(
            [_rope_t(dqv[:, h * HEAD_PAD:(h + 1) * HEAD_PAD], cst, s1t, s2t) for h in range(C_HEADS)], axis=1
        ).astype(BF16)
        dwq_ref[...] += _dot(qn, dqp, _TN)
        dqn = _dot(dqp, wq_ref[...], _NT)
        dql, dgq = _rms_bwd(dqn, qh, rq, gqv)
        dgq_ref[...] += dgq

        dkv = dk_ref[...]
        dkb = dkv.astype(BF16)
        dvb = dv_ref[...].astype(BF16)
        dwk_ref[...] += _dot(kvn, dkb, _TN)
        dwv_ref[...] += _dot(kvn, dvb, _TN)
        dkvn = _dot(dkb, wk_ref[...], _NT) + _dot(dvb, wv_ref[...], _NT)
        dkl, dgkv = _rms_bwd(dkvn, kh, rk, gkvv)
        dgkv_ref[...] += dgkv

        dkpe = dkv[:, :HEAD_PAD]
        for h in range(1, C_HEADS):
            dkpe = dkpe + dkv[:, h * HEAD_PAD:(h + 1) * HEAD_PAD]
        dkpe = _rope_t(jnp.where(_kpe_mask(dkpe.shape), dkpe, 0.0), cst, s1t, s2t)
        dz_ref[...] = jnp.concatenate([dql, dkl, dkpe], axis=1).astype(BF16)

    full = lambda r, c: pl.BlockSpec((r, c), lambda i: (0, 0))
    return pl.pallas_call(
        body, name="mla_pre_bwd",
        out_shape=(_sds((s, _ZQ), BF16), _sds((C_Q_RANK, _HW)), _sds((C_KV_RANK, _HW)), _sds((C_KV_RANK, _HW)),
                   _sds((1, C_Q_RANK)), _sds((1, C_KV_RANK))),
        grid=(s // tm,),
        in_specs=[_rows(tm, _ZQ), _vec(C_Q_RANK), _vec(C_KV_RANK), full(C_Q_RANK, _HW), full(C_KV_RANK, _HW),
                  full(C_KV_RANK, _HW), _rows(tm, LANES), _rows(tm, LANES), _rows(tm, LANES),
                  _rows(tm, _HW), _rows(tm, _HW), _rows(tm, _HW)],
        out_specs=(_rows(tm, _ZQ), full(C_Q_RANK, _HW), full(C_KV_RANK, _HW), full(C_KV_RANK, _HW),
                   _vec(C_Q_RANK), _vec(C_KV_RANK)),
        compiler_params=_params(("arbitrary",), VMEM_BIG),
    )(z, gq, gkv, wq, wk, wv, cs, s1, s2, dq, dk, dv)


_ATT_SCALE = (C_NOPE + C_ROPE) ** -0.5
_NEG = -1e30


def _att_exp(q, k, row0, ends_here):
    sc = _dot(q, k, _NT) * _ATT_SCALE
    tq, nk = sc.shape
    if ends_here:
        last = sc[:, nk - tq:]
        row = lax.broadcasted_iota(jnp.int32, last.shape, 0)
        col = lax.broadcasted_iota(jnp.int32, last.shape, 1)
        last = jnp.where(col <= row, last, _NEG)
        sc = last if nk == tq else jnp.concatenate([sc[:, :nk - tq], last], axis=1)
    else:
        qpos = row0 + lax.broadcasted_iota(jnp.int32, sc.shape, 0)
        kpos = lax.broadcasted_iota(jnp.int32, sc.shape, 1)
        sc = jnp.where(kpos <= qpos, sc, _NEG)
    e = jnp.exp(sc - jnp.max(sc, axis=-1, keepdims=True))
    return e, 1.0 / jnp.sum(e, axis=-1, keepdims=True)


def _causal_cases(i, nq, tq, fn):
    if nq > 8:
        fn(nq * tq, False)
        return
    for blk in range(nq):
        pl.when(i == blk)(functools.partial(fn, (blk + 1) * tq, True))


_FWD_HEADS_PER_STEP = 4
_BWD_HEADS_PER_STEP = 2


def _head_lanes(heads):
    return [slice(h * HEAD_PAD, (h + 1) * HEAD_PAD) for h in range(heads)]


def attn_fwd(q, k, v):
    s = q.shape[0]
    tq = _tile(s, 256, 8)
    nq = s // tq
    heads = _FWD_HEADS_PER_STEP
    wide = heads * HEAD_PAD

    def body(q_ref, k_ref, v_ref, o_ref):
        i = pl.program_id(1)

        def case(nk, ends_here):
            for hd in _head_lanes(heads):
                e, inv = _att_exp(q_ref[:, hd], k_ref[:nk, hd], i * tq, ends_here)
                o_ref[:, hd] = (_dot(e.astype(BF16), v_ref[:nk, hd]) * inv).astype(BF16)

        _causal_cases(i, nq, tq, case)

    qspec = pl.BlockSpec((tq, wide), lambda h, i: (i, h))
    kspec = pl.BlockSpec((s, wide), lambda h, i: (0, h))
    return pl.pallas_call(
        body, name="attn_fwd", out_shape=_sds((s, _HW + _DW), BF16), grid=(C_HEADS // heads, s // tq),
        in_specs=[qspec, kspec, kspec], out_specs=qspec,
        compiler_params=_params(("parallel", "parallel"), VMEM_BIG),
    )(q, k, v)


def attn_bwd(q, k, v, o, do_all):
    s = q.shape[0]
    tq = _tile(s, 256, 8)
    heads = _BWD_HEADS_PER_STEP
    wide = heads * HEAD_PAD

    def body(q_ref, k_ref, v_ref, o_ref, do_ref, dq_ref, dk_ref, dv_ref):
        i = pl.program_id(1)

        @pl.when(i == 0)
        def _():
            dk_ref[...] = jnp.zeros_like(dk_ref)
            dv_ref[...] = jnp.zeros_like(dv_ref)

        def case(nk, ends_here):
            for hd in _head_lanes(heads):
                qv, kv, vv, dov = q_ref[:, hd], k_ref[:nk, hd], v_ref[:nk, hd], do_ref[:, hd]
                e, inv = _att_exp(qv, kv, i * tq, ends_here)
                p = e * inv
                dp = _dot(dov, vv, _NT)
                delta = jnp.sum(dov.astype(F32) * o_ref[:, hd].astype(F32), axis=-1, keepdims=True)
                ds = (p * (dp - delta) * _ATT_SCALE).astype(BF16)
                dq_ref[:, hd] = _dot(ds, kv).astype(BF16)
                dk_ref[:nk, hd] += _dot(ds, qv, _TN)
                dv_ref[:nk, hd] += _dot(p.astype(BF16), dov, _TN)

        _causal_cases(i, s // tq, tq, case)

    qspec = pl.BlockSpec((tq, wide), lambda h, i: (i, h))
    kspec = pl.BlockSpec((s, wide), lambda h, i: (0, h))
    return pl.pallas_call(
        body, name="attn_bwd", out_shape=(_sds((s, _HW), BF16), _sds((s, _HW)), _sds((s, _HW))),
        grid=(C_HEADS // heads, s // tq),
        in_specs=[qspec, kspec, kspec, qspec, qspec], out_specs=(qspec, kspec, kspec),
        compiler_params=_params(("parallel", "arbitrary"), VMEM_BIG),
    )(q, k, v, o, do_all)


_DW = D_GROUPS * LANES


def _tril_bf16(w):
    r = lax.broadcasted_iota(jnp.int32, w.shape, 0)
    c = lax.broadcasted_iota(jnp.int32, w.shape, 1)
    return jnp.where(c <= r, w, 0.0).astype(BF16)


def _sgu_forward(zu, zv, lg, lb, ws_ref, bs):
    u = _gelu(zu)
    v = _gelu(zv)
    mu = jnp.mean(v, axis=-1, keepdims=True)
    vc = v - mu
    rstd = lax.rsqrt(jnp.mean(vc * vc, axis=-1, keepdims=True) + EPS)
    xh = vc * rstd
    vln = (xh * lg + lb).astype(BF16)
    mixed = []
    for g in range(D_GROUPS):
        wg = _tril_bf16(ws_ref[g])
        mixed.append(_dot(wg, vln[:, g * LANES:(g + 1) * LANES]) + bs[:, g:g + 1])
    return u, xh, rstd, vln, jnp.concatenate(mixed, axis=1)


def sgu_fwd(z, lg, lb, ws, bs_t, ycat):
    s = z.shape[0]
    nchunk = s // D_CHUNK

    def body(zu_ref, zv_ref, lg_ref, lb_ref, ws_ref, bs_ref, ycat_ref, o_ref):
        del ycat_ref
        u, _, _, _, mixed = _sgu_forward(zu_ref[...].astype(F32), zv_ref[...].astype(F32), lg_ref[...], lb_ref[...],
                                         ws_ref, bs_ref[...])
        o_ref[...] = (u * mixed).astype(BF16)

    return pl.pallas_call(
        body, name="sgu_fwd", out_shape=_sds(ycat.shape, BF16), grid=(nchunk,),
        in_specs=[pl.BlockSpec((D_CHUNK, _DW), lambda n: (n, 1)), pl.BlockSpec((D_CHUNK, _DW), lambda n: (n, 2)),
                  _vec(_DW), _vec(_DW), pl.BlockSpec((D_GROUPS, D_CHUNK, D_CHUNK), lambda n: (0, 0, 0)),
                  pl.BlockSpec((D_CHUNK, LANES), lambda n: (0, 0)), pl.BlockSpec(memory_space=pl.ANY)],
        out_specs=pl.BlockSpec((D_CHUNK, _DW), lambda n: (n, _HW // _DW)),
        input_output_aliases={6: 0},
        compiler_params=_params(("parallel",)),
    )(z, z, lg, lb, ws, bs_t, ycat)


def sgu_bwd(z, lg, lb, ws, bs_t, dycat, dy_col):
    s = z.shape[0]
    nchunk = s // D_CHUNK

    def body(zu_ref, zv_ref, lg_ref, lb_ref, ws_ref, bs_ref, dy_ref, dzu_ref, dzv_ref, dws_ref, dbs_ref, dlg_ref,
             dlb_ref):
        @pl.when(pl.program_id(0) == 0)
        def _():
            dws_ref[...] = jnp.zeros_like(dws_ref)
            dbs_ref[...] = jnp.zeros_like(dbs_ref)
            dlg_ref[...] = jnp.zeros_like(dlg_ref)
            dlb_ref[...] = jnp.zeros_like(dlb_ref)

        zu, zv, lg = zu_ref[...].astype(F32), zv_ref[...].astype(F32), lg_ref[...]
        u, xh, rstd, vln, mixed = _sgu_forward(zu, zv, lg, lb_ref[...], ws_ref, bs_ref[...])
        dy = dy_ref[...].astype(F32)
        dzu_ref[...] = (dy * mixed * _gelu_grad(zu)).astype(BF16)
        dmix = dy * u
        lane = lax.broadcasted_iota(jnp.int32, (D_CHUNK, LANES), 1)
        row = lax.broadcasted_iota(jnp.int32, (D_CHUNK, D_CHUNK), 0)
        colm = lax.broadcasted_iota(jnp.int32, (D_CHUNK, D_CHUNK), 1)
        dvln = []
        dbs = jnp.zeros((D_CHUNK, LANES), F32)
        for g in range(D_GROUPS):
            sl = slice(g * LANES, (g + 1) * LANES)
            dmg = dmix[:, sl]
            dbs = dbs + jnp.where(lane == g, jnp.sum(dmg, axis=-1, keepdims=True), 0.0)
            dmb = dmg.astype(BF16)
            dws_ref[g] += jnp.where(colm <= row, _dot(dmb, vln[:, sl], _NT), 0.0)
            dvln.append(_dot(_tril_bf16(ws_ref[g]), dmb, _TN))
        dbs_ref[...] += dbs
        dvln = jnp.concatenate(dvln, axis=1)
        dlg_ref[...] += jnp.sum(dvln * xh, axis=0, keepdims=True)
        dlb_ref[...] += jnp.sum(dvln, axis=0, keepdims=True)
        dxh = dvln * lg
        dvv = rstd * (dxh - jnp.mean(dxh, axis=-1, keepdims=True) - xh * jnp.mean(dxh * xh, axis=-1, keepdims=True))
        dzv_ref[...] = (dvv * _gelu_grad(zv)).astype(BF16)

    wsspec = pl.BlockSpec((D_GROUPS, D_CHUNK, D_CHUNK), lambda n: (0, 0, 0))
    chunk = lambda cidx: pl.BlockSpec((D_CHUNK, _DW), lambda n: (n, cidx))
    return pl.pallas_call(
        body, name="sgu_bwd",
        out_shape=(_sds((s, _DW), BF16), _sds((s, _DW), BF16), _sds((D_GROUPS, D_CHUNK, D_CHUNK)),
                   _sds((D_CHUNK, LANES)), _sds((1, _DW)), _sds((1, _DW))),
        grid=(nchunk,),
        in_specs=[chunk(1), chunk(2), _vec(_DW), _vec(_DW), wsspec, pl.BlockSpec((D_CHUNK, LANES), lambda n: (0, 0)),
                  chunk(dy_col)],
        out_specs=(chunk(0), chunk(0), wsspec, pl.BlockSpec((D_CHUNK, LANES), lambda n: (0, 0)), _vec(_DW), _vec(_DW)),
        compiler_params=_params(("arbitrary",)),
    )(z, z, lg, lb, ws, bs_t, dycat)


def ada_mod(c_all, ada_w, ada_b):
    nl, d, n = ada_w.shape
    nb = c_all.shape[0]
    tn = _tile(n, 512)

    def body(c_ref, w_ref, b_ref, o_ref):
        cv = c_ref[...]
        ca = (cv * _sigmoid(cv)).astype(BF16)
        o_ref[...] = _dot(ca, w_ref[...].astype(BF16)) + b_ref[...]

    return pl.pallas_call(
        body, name="ada_mod", out_shape=_sds((nl, nb, n)), grid=(nl, n // tn),
        in_specs=[pl.BlockSpec((nb, d), lambda l, j: (0, 0)), pl.BlockSpec((None, d, tn), lambda l, j: (l, 0, j)),
                  pl.BlockSpec((None, 1, tn), lambda l, j: (l, 0, j))],
        out_specs=pl.BlockSpec((None, nb, tn), lambda l, j: (l, 0, j)),
        compiler_params=_params(("parallel", "parallel")),
    )(c_all, ada_w, ada_b.reshape(nl, 1, n))


_ADAM_BLOCK = 256 * 1024


def _adam_rows(rows, cols):
    if rows * cols <= _ADAM_BLOCK or rows % 8:
        return rows
    return _tile(rows, max(8, _ADAM_BLOCK // cols), 8)


def _adam_update(w, gv, m, v):
    inv_bc1 = 1.0 / (1.0 - ADAM_B1 ** ADAM_STEP)
    inv_bc2 = 1.0 / (1.0 - ADAM_B2 ** ADAM_STEP)
    nm = ADAM_B1 * m + (1.0 - ADAM_B1) * gv
    nv = ADAM_B2 * v + (1.0 - ADAM_B2) * (gv * gv)
    return -ADAM_LR * ((nm * inv_bc1) / (jnp.sqrt(nv * inv_bc2) + ADAM_EPS) + ADAM_WD * w), nm, nv


def adamw(w, g, m, v, name):
    shape = w.shape
    cols = shape[-1]
    rows = w.size // cols
    tr = _adam_rows(rows, cols)

    def body(w_ref, g_ref, m_ref, v_ref, go_ref, d_ref, nm_ref, nv_ref):
        gv = g_ref[...]
        go_ref[...] = gv
        d_ref[...], nm_ref[...], nv_ref[...] = _adam_update(w_ref[...], gv, m_ref[...], v_ref[...])

    spec = pl.BlockSpec((tr, cols), lambda i: (i, 0))
    out = _sds((rows, cols))
    r2 = lambda t: t.reshape(rows, cols)
    res = pl.pallas_call(
        body, name=name, out_shape=(out,) * 4, grid=(rows // tr,),
        in_specs=[spec] * 4, out_specs=(spec,) * 4, compiler_params=_params(("parallel",)),
    )(r2(w), r2(g), r2(m), r2(v))
    return tuple(t.reshape(shape) for t in res)


def adamw_ada(w, c_all, dmod, m, v):
    nl, d, n = w.shape
    tr = _adam_rows(d, n)
    pad = 16 - c_all.shape[0]
    c16 = jnp.pad(c_all, ((0, pad), (0, 0)))
    dm16 = jnp.pad(dmod, ((0, 0), (0, pad), (0, 0)))

    def body(w_ref, c_ref, dm_ref, m_ref, v_ref, g_ref, d_ref, nm_ref, nv_ref):
        cv = c_ref[...]
        gv = _dot((cv * _sigmoid(cv)).astype(BF16), dm_ref[...].astype(BF16), _TN)
        g_ref[...] = gv
        d_ref[...], nm_ref[...], nv_ref[...] = _adam_update(w_ref[...], gv, m_ref[...], v_ref[...])

    spec = pl.BlockSpec((None, tr, n), lambda l, i: (l, i, 0))
    out = _sds((nl, d, n))
    return pl.pallas_call(
        body, name="adamw_ada_w", out_shape=(out, out, out, out), grid=(nl, d // tr),
        in_specs=[spec, pl.BlockSpec((16, tr), lambda l, i: (0, i)), pl.BlockSpec((None, 16, n), lambda l, i: (l, 0, 0)),
                  spec, spec],
        out_specs=(spec,) * 4, compiler_params=_params(("parallel", "parallel")),
    )(w, c16, dm16, m, v)


def adamw_small(ws, gs, ms, vs):
    n = len(ws)
    flat = lambda t: t.reshape(-1, t.shape[-1])

    def body(*refs):
        ins, outs = refs[:4 * n], refs[4 * n:]
        for i in range(n):
            w_ref, g_ref, m_ref, v_ref = ins[4 * i:4 * i + 4]
            outs[3 * i][...], outs[3 * i + 1][...], outs[3 * i + 2][...] = _adam_update(
                w_ref[...], g_ref[...], m_ref[...], v_ref[...])

    operands = [flat(t) for quad in zip(ws, gs, ms, vs) for t in quad]
    res = pl.pallas_call(
        body, name="adamw_small", out_shape=tuple(_sds(flat(w).shape) for w in ws for _ in range(3)),
    )(*operands)
    return [(g, res[3 * i].reshape(w.shape), res[3 * i + 1].reshape(w.shape), res[3 * i + 2].reshape(w.shape))
            for i, (w, g) in enumerate(zip(ws, gs))]


def adamw_layers(w, g0, g1, m, v, name):
    _, rows, cols = w.shape
    tr = _adam_rows(rows, cols)

    def body(w_ref, g0_ref, g1_ref, m_ref, v_ref, g_ref, d_ref, nm_ref, nv_ref):
        gv = jnp.where(pl.program_id(0) == 0, g0_ref[...], g1_ref[...])
        g_ref[...] = gv
        d_ref[...], nm_ref[...], nv_ref[...] = _adam_update(w_ref[...], gv, m_ref[...], v_ref[...])

    spec = pl.BlockSpec((None, tr, cols), lambda l, i: (l, i, 0))
    gspec = pl.BlockSpec((tr, cols), lambda l, i: (i, 0))
    out = _sds((2, rows, cols))
    return pl.pallas_call(
        body, name=name, out_shape=(out, out, out, out), grid=(2, rows // tr),
        in_specs=[spec, gspec, gspec, spec, spec], out_specs=(spec,) * 4, compiler_params=_params(("parallel", "parallel")),
    )(w, g0, g1, m, v)


def sum8(gathered):
    _, r, _ = gathered.shape
    tr = _tile(r, 512, 8)

    def body(g_ref, o_ref):
        acc = g_ref[0]
        for dev in range(1, N_DEV):
            acc = acc + g_ref[dev]
        o_ref[...] = acc

    return pl.pallas_call(
        body, name="sum8", out_shape=_sds((r, LANES)), grid=(r // tr,),
        in_specs=[pl.BlockSpec((N_DEV, tr, LANES), lambda i: (0, i, 0))], out_specs=pl.BlockSpec((tr, LANES), lambda i: (i, 0)),
        compiler_params=_params(("parallel",)),
    )(gathered)


_SUM_STEPS = 2


def pair_sums(gs, recvs, core, name):
    n = len(gs)
    trs = [g.shape[1] // 2 // _SUM_STEPS for g in gs]

    def body(c_ref, *refs):
        del c_ref
        for i in range(n):
            a_ref, b_ref, o_ref = refs[2 * i], refs[2 * i + 1], refs[2 * n + i]
            o_ref[...] = (a_ref[...].astype(F32) + b_ref[...].astype(F32)).astype(BF16)

    in_specs, out_specs = [], []
    for g, tr in zip(gs, trs):
        cols = g.shape[2]
        in_specs.append(pl.BlockSpec((None, tr, cols), lambda k, s, c: (k, c[0] * _SUM_STEPS + s, 0)))
        in_specs.append(pl.BlockSpec((None, tr, cols), lambda k, s, c: (k, s, 0)))
        out_specs.append(pl.BlockSpec((None, tr, cols), lambda k, s, c: (k, s, 0)))
    grid_spec = pltpu.PrefetchScalarGridSpec(num_scalar_prefetch=1, grid=(N_CHIPS, _SUM_STEPS), in_specs=in_specs,
                                             out_specs=tuple(out_specs))
    return list(pl.pallas_call(
        body, name=name, out_shape=tuple(_sds((N_CHIPS, g.shape[1] // 2, g.shape[2]), BF16) for g in gs),
        grid_spec=grid_spec, compiler_params=_params(("parallel", "parallel")),
    )(core.reshape(1).astype(jnp.int32), *[t for pair in zip(gs, recvs) for t in pair]))


def chip_sums(pairs, recvs, chip, core, name):
    n = len(pairs)
    trs = [p.shape[1] // _SUM_STEPS for p in pairs]

    def body(p_ref, *refs):
        del p_ref
        for i in range(n):
            own_ref, r_ref, o_ref = refs[2 * i], refs[2 * i + 1], refs[2 * n + i]
            acc = own_ref[...].astype(F32)
            for j in range(N_CHIPS - 1):
                acc = acc + r_ref[j].astype(F32)
            o_ref[...] = acc

    in_specs, out_specs = [], []
    for p, tr in zip(pairs, trs):
        cols = p.shape[2]
        in_specs.append(pl.BlockSpec((None, tr, cols), lambda s, q: (q[0], s, 0)))
        in_specs.append(pl.BlockSpec((N_CHIPS - 1, tr, cols), lambda s, q: (0, s, 0)))
        out_specs.append(pl.BlockSpec((None, tr, cols), lambda s, q: (q[1], s, 0)))
    grid_spec = pltpu.PrefetchScalarGridSpec(num_scalar_prefetch=1, grid=(_SUM_STEPS,), in_specs=in_specs,
                                             out_specs=tuple(out_specs))
    return list(pl.pallas_call(
        body, name=name, out_shape=tuple(_sds((2,) + p.shape[1:]) for p in pairs), grid_spec=grid_spec,
        compiler_params=_params(("parallel",)),
    )(jnp.stack([chip, core]).astype(jnp.int32), *[t for pair in zip(pairs, recvs) for t in pair]))


def _place():
    return lax.axis_index("x"), lax.axis_index("y"), lax.axis_index("c")


def _other_chips(x, y):
    return [(x, 1 - y), (1 - x, y), (1 - x, 1 - y)]


_HBM = pl.BlockSpec(memory_space=pltpu.HBM)


def all_gather8(v, name, after=()):
    m, n = v.shape

    def body(x_ref, *refs):
        out_ref, send_sems, recv_sems, local_sem = refs[len(after):]
        x, y, c = _place()
        me, sibling = (x, y, c), (x, y, 1 - c)
        chips = _other_chips(x, y)

        def rows(px, py, pc):
            return out_ref.at[pl.ds((4 * px + 2 * py + pc) * m, m), :]

        def copy(k, block, to, src=None):
            return pltpu.make_async_remote_copy(
                src_ref=rows(*block) if src is None else src, dst_ref=rows(*block),
                send_sem=send_sems.at[k], recv_sem=recv_sems.at[k], device_id=to, device_id_type=MESH)

        mine = pltpu.make_async_copy(x_ref, rows(*me), local_sem)
        mine.start()
        first = [copy(0, me, sibling, src=x_ref)]
        first += [copy(1 + j, me, (*chip, c), src=x_ref) for j, chip in enumerate(chips)]
        for cp in first:
            cp.start()
        passed = [copy(4 + j, (*chip, c), sibling) for j, chip in enumerate(chips)]
        for j, chip in enumerate(chips):
            copy(1 + j, (*chip, c), me).wait_recv()
            passed[j].start()
        copy(0, sibling, me).wait_recv()
        for j, chip in enumerate(chips):
            copy(4 + j, (*chip, 1 - c), me).wait_recv()
        for cp in first + passed:
            cp.wait_send()
        mine.wait()

    return pl.pallas_call(
        body, name=name, out_shape=_sds((N_DEV * m, n), v.dtype),
        in_specs=[pl.BlockSpec(memory_space=pltpu.VMEM)] + [pl.BlockSpec(memory_space=pl.ANY)] * len(after),
        out_specs=pl.BlockSpec(memory_space=pltpu.VMEM),
        scratch_shapes=[pltpu.SemaphoreType.DMA((7,)), pltpu.SemaphoreType.DMA((7,)), pltpu.SemaphoreType.DMA],
        compiler_params=_params(None, VMEM_BIG),
    )(v, *after)


def _comm_call(body, name, ins, out_shapes, nsem, aliases=None):
    return pl.pallas_call(
        body, name=name, out_shape=tuple(out_shapes), in_specs=[_HBM] * len(ins), out_specs=tuple([_HBM] * len(out_shapes)),
        scratch_shapes=[pltpu.SemaphoreType.DMA((nsem,)), pltpu.SemaphoreType.DMA((nsem,))],
        input_output_aliases=aliases or {},
    )(*ins)


def _remote(src, dst, send_sems, recv_sems, k, to):
    return pltpu.make_async_remote_copy(src_ref=src, dst_ref=dst, send_sem=send_sems.at[k], recv_sem=recv_sems.at[k],
                                        device_id=to, device_id_type=MESH)


def _half(core, rh):
    return pl.ds(pl.multiple_of(core * rh, 16), rh)


def swap_halves(gs, name):
    n = len(gs)

    def body(*refs):
        ins, outs, (send_sems, recv_sems) = refs[:n], refs[n:2 * n], refs[2 * n:]
        x, y, c = _place()
        copies = []
        for i in range(n):
            theirs = _half(1 - c, ins[i].shape[1] // 2)
            cp = _remote(ins[i].at[:, theirs], outs[i], send_sems, recv_sems, i, (x, y, 1 - c))
            cp.start()
            copies.append(cp)
        for cp in copies:
            cp.wait()

    return _comm_call(body, name, gs, [_sds((g.shape[0], g.shape[1] // 2, g.shape[2]), g.dtype) for g in gs], n)


def join_halves(bufs, name):
    n = len(bufs)

    def body(*refs):
        ins, outs, (send_sems, recv_sems) = refs[:n], refs[n:2 * n], refs[2 * n:]
        x, y, c = _place()
        copies = []
        for i in range(n):
            cp = _remote(ins[i].at[c], outs[i].at[c], send_sems, recv_sems, i, (x, y, 1 - c))
            cp.start()
            copies.append(cp)
        for i in range(n):
            theirs = outs[i].at[1 - c]
            _remote(theirs, theirs, send_sems, recv_sems, i, (x, y, 1 - c)).wait_recv()
        for cp in copies:
            cp.wait_send()

    return _comm_call(body, name, bufs, [_sds(b.shape, b.dtype) for b in bufs], n, {i: i for i in range(n)})


def forward_halves(lands, name):
    n = len(lands)

    def body(*refs):
        ins, outs, (send_sems, recv_sems) = refs[:n], refs[n:2 * n], refs[2 * n:]
        x, y, c = _place()
        sibling = (x, y, 1 - c)
        chips = _other_chips(x, y)
        copies = []
        for i in range(n):
            mine = _half(c, ins[i].shape[1] // 2)
            for j, (px, py) in enumerate(chips):
                cp = _remote(ins[i].at[2 * px + py, mine], outs[i].at[2 * px + py, mine], send_sems, recv_sems, 3 * i + j, sibling)
                cp.start()
                copies.append(cp)
        for i in range(n):
            theirs = _half(1 - c, ins[i].shape[1] // 2)
            for j, (px, py) in enumerate(chips):
                landed = outs[i].at[2 * px + py, theirs]
                _remote(landed, landed, send_sems, recv_sems, 3 * i + j, sibling).wait_recv()
        for cp in copies:
            cp.wait_send()

    return _comm_call(body, name, lands, [_sds(b.shape, b.dtype) for b in lands], 3 * n, {i: i for i in range(n)})


_SEM = pl.BlockSpec(memory_space=pltpu.SEMAPHORE)
_EFFECT = pltpu.SideEffectType.DATAFLOW_SIDE_EFFECTING


def _gather_copies(srcs, lands, send_sems, recv_sems):
    x, y, c = _place()
    copies = []
    for i in range(len(srcs)):
        mine = _half(c, srcs[i].shape[0] // 2)
        for j, chip in enumerate(_other_chips(x, y)):
            copies.append(_remote(srcs[i].at[mine], lands[i].at[2 * x + y, mine], send_sems, recv_sems, 3 * i + j, (*chip, c)))
    return copies


def _exchange_copies(srcs, lands, send_sems, recv_sems):
    x, y, c = _place()
    copies = []
    for i in range(len(srcs)):
        for j, (px, py) in enumerate(_other_chips(x, y)):
            copies.append(_remote(srcs[i].at[2 * px + py], lands[i].at[j], send_sems, recv_sems, 3 * i + j, (px, py, c)))
    return copies


def _everyone_copies(srcs, lands, send_sems, recv_sems):
    x, y, c = _place()
    flip = lambda v, b: 1 - v if b else v
    dst = lands[0].at[4 * x + 2 * y + c]
    return [_remote(srcs[0], dst, send_sems, recv_sems, j - 1, (flip(x, j & 4), flip(y, j & 2), flip(c, j & 1)))
            for j in range(1, N_DEV)]


GATHER = (_gather_copies, 3)
EXCHANGE = (_exchange_copies, 3)
EVERYONE = (_everyone_copies, N_DEV - 1)


def split_start(name, plan, srcs, land_shapes, after=()):
    copies_fn, per_source = plan
    n, m, k = len(srcs), len(land_shapes), len(after)
    ncopies = per_source * n

    def body(*refs):
        src_refs, land_refs = refs[:n], refs[n:n + m]
        send_sems, recv_sems = refs[n + m + k], refs[n + m + k + 1]
        token = refs[-1]
        for cp in copies_fn(src_refs, land_refs, send_sems, recv_sems):
            cp.start()
        token[...] = jnp.zeros_like(token)

    hbm = lambda s: pltpu.HBM(tuple(s.shape), s.dtype)
    outs = pl.pallas_call(
        body, name=name,
        out_shape=(pltpu.SemaphoreType.DMA((ncopies,)), pltpu.SemaphoreType.DMA((ncopies,)), *[hbm(s) for s in srcs],
                   *[hbm(s) for s in land_shapes], _sds((8, LANES))),
        in_specs=[_HBM] * (n + m) + [pl.BlockSpec(memory_space=pl.ANY)] * k,
        out_specs=(_SEM, _SEM, *([_HBM] * (n + m)), pl.BlockSpec(memory_space=pltpu.VMEM)),
        input_output_aliases={i: 2 + i for i in range(n + m)},
        compiler_params=pltpu.CompilerParams(has_side_effects=_EFFECT),
    )(*[pltpu.with_memory_space_constraint(s, pltpu.HBM) for s in srcs],
      *[pltpu.with_memory_space_constraint(lax.empty(tuple(s.shape), s.dtype), pltpu.HBM) for s in land_shapes], *after)
    handle = (outs[0], outs[1], list(outs[2:2 + n]), list(outs[2 + n:2 + n + m]))
    return handle, outs[-1][0, 0]


def split_wait(name, plan, handle, after):
    copies_fn, _ = plan
    send_sems, recv_sems, srcs, lands = handle
    n, m = len(srcs), len(lands)
    after = list(after) if isinstance(after, (list, tuple)) else [after]

    def body(*refs):
        src_refs, land_refs = refs[:n], refs[n:n + m]
        for cp in copies_fn(src_refs, land_refs, refs[n + m], refs[n + m + 1]):
            cp.wait_send()
            cp.wait_recv()

    hbm = lambda s: pltpu.HBM(tuple(s.shape), s.dtype)
    outs = pl.pallas_call(
        body, name=name, out_shape=tuple(hbm(s) for s in srcs + lands),
        in_specs=[_HBM] * (n + m) + [_SEM, _SEM] + [pl.BlockSpec(memory_space=pl.ANY)] * len(after),
        out_specs=tuple([_HBM] * (n + m)), input_output_aliases={i: i for i in range(n + m)},
        compiler_params=pltpu.CompilerParams(has_side_effects=_EFFECT),
    )(*srcs, *lands, send_sems, recv_sems, *after)
    return list(outs[:n]), list(outs[n:])


def chip_major(w, groups=N_CHIPS):
    r, c = w.shape
    return w.reshape(r, groups, c // groups).transpose(1, 0, 2)


def from_chip_major(w):
    g, r, c = w.shape
    return w.transpose(1, 0, 2).reshape(r, g * c)


def _cd_in_pad(w):
    a = C_Q_RANK + C_KV_RANK
    z = lambda n: jnp.zeros((w.shape[0], n), w.dtype)
    return jnp.concatenate([w[:, :a], z(C_NOPE), w[:, a:a + C_ROPE], z(HEAD_PAD - C_NOPE - C_ROPE), w[:, a + C_ROPE:]], axis=1)


def _cd_in_unpad(w):
    a = C_Q_RANK + C_KV_RANK
    return jnp.concatenate([w[:, :a], w[:, a + C_NOPE:a + C_NOPE + C_ROPE], w[:, a + HEAD_PAD:]], axis=1)


def _pad_heads(w, width):
    r = w.shape[0]
    w = w.reshape(r, C_HEADS, width)
    return jnp.pad(w, ((0, 0), (0, 0), (0, HEAD_PAD - width))).reshape(r, _HW)


def _unpad_heads(w, width):
    r = w.shape[0]
    return w.reshape(r, C_HEADS, HEAD_PAD)[:, :, :width].reshape(r, C_HEADS * width)


def prepare_weights(p):
    q = dict(p)
    q["cd_w_in"] = _cd_in_pad(p["cd_w_in"])
    q["c_w_uq"] = _pad_heads(p["c_w_uq"], C_NOPE + C_ROPE)
    ukv = p["c_w_ukv"].reshape(C_KV_RANK, C_HEADS, C_NOPE + C_V)
    q["c_w_uk"] = _pad_heads(ukv[:, :, :C_NOPE].reshape(C_KV_RANK, -1), C_NOPE)
    q["c_w_uv"] = _pad_heads(ukv[:, :, C_NOPE:].reshape(C_KV_RANK, -1), C_V)
    wo = p["cd_w_out"]
    att_rows = jnp.pad(wo[:C_HEADS * C_V].reshape(C_HEADS, C_V, D_MODEL), ((0, 0), (0, HEAD_PAD - C_V), (0, 0)))
    q["cd_w_out"] = jnp.concatenate([att_rows.reshape(_HW, D_MODEL), wo[C_HEADS * C_V:]], axis=0)
    return q


def unprepare_grads(g):
    q = dict(g)
    q["cd_w_in"] = _cd_in_unpad(g["cd_w_in"])
    q["c_w_uq"] = _unpad_heads(g["c_w_uq"], C_NOPE + C_ROPE)
    uk = g.pop("c_w_uk").reshape(C_KV_RANK, C_HEADS, HEAD_PAD)[:, :, :C_NOPE]
    uv = g.pop("c_w_uv").reshape(C_KV_RANK, C_HEADS, HEAD_PAD)[:, :, :C_V]
    q.pop("c_w_uk", None)
    q.pop("c_w_uv", None)
    q["c_w_ukv"] = jnp.concatenate([uk, uv], axis=-1).reshape(C_KV_RANK, C_HEADS * (C_NOPE + C_V))
    wo = g["cd_w_out"]
    att = wo[:_HW].reshape(C_HEADS, HEAD_PAD, D_MODEL)[:, :C_V].reshape(C_HEADS * C_V, D_MODEL)
    q["cd_w_out"] = jnp.concatenate([att, wo[_HW:]], axis=0)
    return q


def rope_tables(positions):
    half = C_ROPE // 2
    inv_freq = ROPE_THETA ** (-jnp.arange(half, dtype=F32) / half)
    ang = positions.astype(F32)[:, None] * inv_freq
    cos, sin = jnp.cos(ang), jnp.sin(ang)
    s = positions.shape[0]
    z = lambda n: jnp.zeros((s, n), F32)
    cs = jnp.concatenate([jnp.ones((s, C_NOPE), F32), cos, cos, z(HEAD_PAD - C_NOPE - C_ROPE)], axis=1)
    s1 = jnp.concatenate([z(C_NOPE), -sin, z(HEAD_PAD - C_NOPE - half)], axis=1)
    s2 = jnp.concatenate([z(C_NOPE + half), sin, z(HEAD_PAD - C_NOPE - C_ROPE)], axis=1)
    return cs, s1, s2


_UP_COLS = 2 * D_FF // N_CHIPS


def ffn_fwd(h2, w, l, late_down=None):
    zf = matmul(h2, w["ffn_w_up"][l], "nn", BF16, f"ffn_up{l}", gb=N_CHIPS, go=2, tn=_UP_COLS)
    if late_down is not None:
        late_down(zf)
    a, f = ffn_act_down(zf, w["ffn_conv_w"][l], w["ffn_w_down"][l], f"ffn_act_down{l}")
    return f, (zf, a)


def ffn_bwd(df, h2, saved, w, l):
    zf, a = saved
    da = matmul(df, w["ffn_w_down"][l], "nt", BF16, f"ffn_down_dx{l}", tn=D_FF // 2)
    d_down = matmul(a, df, "tn", BF16, f"ffn_down_dw{l}", tm=D_FF // 2)
    dzf, d_conv = ffn_act_bwd(zf, w["ffn_conv_w"][l], da, f"ffn_act_bwd{l}")
    dh2 = matmul(dzf, w["ffn_w_up"][l], "nt", BF16, f"ffn_up_dx{l}", ga=2, gb=N_CHIPS, tk=_UP_COLS, tn=D_MODEL)
    d_up = matmul(h2, dzf, "tn", BF16, f"ffn_up_dw{l}", gb=2, go=N_CHIPS, tn=_UP_COLS)
    d_conv = d_conv.transpose(1, 0, 2).reshape(3, 2 * D_FF)
    return dh2, dict(ffn_w_down=d_down, ffn_conv_w=d_conv, ffn_w_up=d_up)


def mixer0_fwd(h, w):
    z = matmul(h, w["ab_w_in"], "nn", BF16, "ab_in", gb=N_CHIPS)
    ycat = pool_fwd(z, w["b_mix_w"], w["b_scale"], gconv_fwd(z, w["a_conv_w"]))
    y = matmul(ycat, w["ab_w_out"], "nn", BF16, "ab_out", tn=D_MODEL)
    return y, (z, ycat)


def mixer0_bwd(dy, h, saved, w):
    z, ycat = saved
    grads = {}
    dycat = matmul(dy, w["ab_w_out"], "nt", BF16, "ab_out_dx")
    grads["ab_w_out"] = matmul(ycat, dy, "tn", BF16, "ab_out_dw")
    db, dc, da, d_conv = gconv_bwd(z, w["a_conv_w"], dycat)
    dp, d_mix, d_scale = pool_bwd(z, w["b_mix_w"], w["b_scale"], dycat)
    dz = jnp.concatenate([db, dc, da, dp], axis=1)
    dh = matmul(dz, w["ab_w_in"], "nt", BF16, "ab_in_dx", gb=N_CHIPS, tn=D_MODEL)
    grads["ab_w_in"] = matmul(h, dz, "tn", BF16, "ab_in_dw", go=N_CHIPS)
    grads.update(a_conv_w=d_conv, b_mix_w=d_mix, b_scale=d_scale)
    return dh, grads


def mixer1_fwd(h, ropes, w):
    cs, s1, s2 = ropes
    z = matmul(h, w["cd_w_in"], "nn", BF16, "cd_in")
    bs_t = jnp.pad(w["d_b_s"].T, ((0, 0), (0, LANES - D_GROUPS)))
    qh, kh, vh = mla_pre_fwd(z, w["c_q_norm_g"], w["c_kv_norm_g"], w["c_w_uq"], w["c_w_uk"], w["c_w_uv"], cs, s1, s2)
    ycat = sgu_fwd(z, w["d_ln_g"], w["d_ln_b"], w["d_w_s"], bs_t, attn_fwd(qh, kh, vh))
    y = matmul(ycat, w["cd_w_out"], "nn", BF16, "cd_out", tn=D_MODEL)
    return y, (z, bs_t, qh, kh, vh, ycat)


def mixer1_bwd(dy, h, saved, ropes, w):
    cs, s1, s2 = ropes
    z, bs_t, qh, kh, vh, ycat = saved
    grads = {}
    dycat = matmul(dy, w["cd_w_out"], "nt", BF16, "cd_out_dx")
    grads["cd_w_out"] = matmul(ycat, dy, "tn", F32, "cd_out_dw")
    dqh, dkh, dvh = attn_bwd(qh, kh, vh, ycat, dycat)
    dzq, d_uq, d_uk, d_uv, d_gq, d_gkv = mla_pre_bwd(
        z, w["c_q_norm_g"], w["c_kv_norm_g"], w["c_w_uq"], w["c_w_uk"], w["c_w_uv"], cs, s1, s2, dqh, dkh, dvh)
    dzu, dzv, d_ws, d_bs, d_lg, d_lb = sgu_bwd(z, w["d_ln_g"], w["d_ln_b"], w["d_w_s"], bs_t, dycat, _HW // _DW)
    dz = jnp.concatenate([dzq, dzu, dzv], axis=1)
    dh = matmul(dz, w["cd_w_in"], "nt", BF16, "cd_in_dx", tn=D_MODEL)
    grads["cd_w_in"] = matmul(h, dz, "tn", F32, "cd_in_dw")
    grads.update(c_w_uq=d_uq, c_w_uk=d_uk, c_w_uv=d_uv, c_q_norm_g=d_gq, c_kv_norm_g=d_gkv, d_w_s=d_ws,
                 d_b_s=d_bs[:, :D_GROUPS].T, d_ln_g=d_lg, d_ln_b=d_lb)
    return dh, grads


class StepHooks:
    def weights(self, stage, after):
        pass

    def gradients(self, stage, grads, after):
        return 0.0


def run_step(x, tgt, mod, ropes, w, hooks):
    sh1, sc1, g1, sh2, sc2, g2 = range(N_MOD)
    mods = mod.reshape(2, 1, N_MOD * D_MODEL)
    n1 = w["norm1_g"].reshape(2, 1, D_MODEL)
    n2 = w["norm2_g"].reshape(2, 1, D_MODEL)
    final_g = Vec(w["final_norm_g"].reshape(1, 1, D_MODEL), 0, 0)

    hooks.weights("mix0", mod)
    h0 = modnorm_fwd(x, Vec(n1, 0, 0), Vec(mods, 0, sc1), Vec(mods, 0, sh1), "modnorm_0")
    y0, mix0 = mixer0_fwd(h0, w)
    x1, h1 = resid_modnorm_fwd(x, y0, Vec(mods, 0, g1), Vec(n2, 0, 0), Vec(mods, 0, sc2), Vec(mods, 0, sh2), "resid_modnorm_1")
    hooks.weights("up0", x1)
    f0, ffn0 = ffn_fwd(h1, w, 0, lambda act: hooks.weights("down0", act))
    x2, h2 = resid_modnorm_fwd(x1, f0, Vec(mods, 0, g2), Vec(n1, 1, 0), Vec(mods, 1, sc1), Vec(mods, 1, sh1), "resid_modnorm_2")
    hooks.weights("mix1", x2)
    y1, mix1 = mixer1_fwd(h2, ropes, w)
    x3, h3 = resid_modnorm_fwd(x2, y1, Vec(mods, 1, g1), Vec(n2, 1, 0), Vec(mods, 1, sc2), Vec(mods, 1, sh2), "resid_modnorm_3")
    hooks.weights("ffn1", x3)
    f1, ffn1 = ffn_fwd(h3, w, 1)
    dres, d_final, loss, df1, dg2b = final_fused(x3, f1, Vec(mods, 1, g2), final_g, tgt)

    dh3, gf1 = ffn_bwd(df1, h3, ffn1, w, 1)
    late = mods + hooks.gradients("ffn1", gf1, dh3)
    dres, dsh2b, dsc2b, dn2b, dy1, dg1b = norm_gate_bwd(
        x3, dh3, Vec(n2, 1, 0), Vec(late, 1, sc2), dres, y1, Vec(late, 1, g1), "norm_gate_bwd_3")
    dh2, gm1 = mixer1_bwd(dy1, h2, mix1, ropes, w)
    late = mods + hooks.gradients("mix1", gm1, dh2)
    dres, dsh1b, dsc1b, dn1b, df0, dg2a = norm_gate_bwd(
        x2, dh2, Vec(n1, 1, 0), Vec(late, 1, sc1), dres, f0, Vec(late, 0, g2), "norm_gate_bwd_2")
    dh1, gf0 = ffn_bwd(df0, h1, ffn0, w, 0)
    late = mods + hooks.gradients("ffn0", gf0, dh1)
    dres, dsh2a, dsc2a, dn2a, dy0, dg1a = norm_gate_bwd(
        x1, dh1, Vec(n2, 0, 0), Vec(late, 0, sc2), dres, y0, Vec(late, 0, g1), "norm_gate_bwd_1")
    dh0, gm0 = mixer0_bwd(dy0, h0, mix0, w)
    late = mods + hooks.gradients("mix0", gm0, dh0)
    grad_x, dsh1a, dsc1a, dn1a = norm_bwd(x, dh0, Vec(n1, 0, 0), Vec(late, 0, sc1), dres, "norm_bwd_0")

    dmod = jnp.concatenate([jnp.concatenate([dsh1a, dsc1a, dg1a, dsh2a, dsc2a, dg2a], axis=1),
                            jnp.concatenate([dsh1b, dsc1b, dg1b, dsh2b, dsc2b, dg2b], axis=1)], axis=0)
    norms = dict(norm1_g=jnp.concatenate([dn1a, dn1b], axis=0), norm2_g=jnp.concatenate([dn2a, dn2b], axis=0),
                 final_norm_g=d_final)
    return loss, grad_x, dmod, dict(mix0=gm0, ffn0=gf0, mix1=gm1, ffn1=gf1, norms=norms)


def merge_grads(by_stage):
    grads = {**by_stage["mix0"], **by_stage["mix1"], **by_stage["norms"]}
    for k in ("ffn_w_down", "ffn_w_up"):
        grads[k] = [by_stage["ffn0"][k], by_stage["ffn1"][k]]
    grads["ffn_conv_w"] = jnp.stack([by_stage["ffn0"]["ffn_conv_w"], by_stage["ffn1"]["ffn_conv_w"]])
    return grads


_WEIGHTS = ("ada_w", "ada_b", "norm1_g", "norm2_g", "ab_w_in", "a_conv_w", "b_mix_w", "b_scale", "ab_w_out", "cd_w_in",
            "c_q_norm_g", "c_w_uq", "c_kv_norm_g", "c_w_ukv", "d_ln_g", "d_ln_b", "d_w_s", "d_b_s", "cd_w_out",
            "ffn_w_up", "ffn_conv_w", "ffn_w_down", "final_norm_g")
_INPUTS = ("x", "c", "positions") + _WEIGHTS + ("loss_target",) + tuple("m_" + n for n in _WEIGHTS) + tuple(
    "v_" + n for n in _WEIGHTS)

def _pack_rows(parts, rows, dtype):
    flat = jnp.concatenate([p.reshape(-1).astype(dtype) for p in parts])
    return jnp.pad(flat, (0, rows * LANES - flat.shape[0])).reshape(rows, LANES)


def _rows_major(w):
    r, c = w.shape
    return w.reshape(N_CHIPS, r // N_CHIPS, c)


def start_gather(shards, tag, after=()):
    lands = [_sds((N_CHIPS,) + s.shape, s.dtype) for s in shards]
    return split_start("gather_start_" + tag, GATHER, shards, lands, after)


def finish_gather(handle, chip, tag, after):
    shards, lands = split_wait("gather_wait_" + tag, GATHER, handle, after)
    lands = forward_halves(lands, "gather_forward_" + tag)
    return [lax.dynamic_update_index_in_dim(o, s, chip, 0) for o, s in zip(lands, shards)]


def start_reduce(gs, core, tag):
    recv = swap_halves(gs, "swap_halves_" + tag)
    pairs = pair_sums(gs, recv, core, "pair_sums_" + tag)
    lands = [_sds((N_CHIPS - 1,) + p.shape[1:], p.dtype) for p in pairs]
    return split_start("exchange_start_" + tag, EXCHANGE, pairs, lands)


def finish_reduce(handle, chip, core, tag, after):
    pairs, others = split_wait("exchange_wait_" + tag, EXCHANGE, handle, after)
    halves = chip_sums(pairs, others, chip, core, "chip_sums_" + tag)
    full = join_halves(halves, "join_halves_" + tag)
    return [f.reshape(f.shape[1] * 2, f.shape[2]) for f in full]


_SMALL_SHARDED = (("a_conv_w", (3, 128), 1), ("c_q_norm_g", (1, 64), 1), ("d_ln_g", (1, 128), 1), ("d_ln_b", (1, 128), 1),
                  ("ffn_conv_w", (2, 3, 2 * D_FF // N_CHIPS), 2))
_SMALL_GRADS = (("norm1_g", (2, D_MODEL)), ("norm2_g", (2, D_MODEL)), ("b_mix_w", (4, 128, 128)), ("b_scale", (1, 512)),
                ("c_kv_norm_g", (1, 128)), ("d_w_s", (4, 128, 128)), ("d_b_s", (4, 128)), ("final_norm_g", (1, D_MODEL)),
                ("a_conv_w", (3, 512)), ("c_q_norm_g", (1, 256)), ("d_ln_g", (1, 512)), ("d_ln_b", (1, 512)),
                ("ffn_conv_w", (2, 3, 2 * D_FF)))


def _size(shape):
    n = 1
    for d in shape:
        n *= d
    return n


def kernel(x, c, positions, ada_w, ada_b, norm1_g, norm2_g, ab_w_in, a_conv_w, b_mix_w, b_scale, ab_w_out, cd_w_in, c_q_norm_g, c_w_uq, c_kv_norm_g, c_w_ukv, d_ln_g, d_ln_b, d_w_s, d_b_s, cd_w_out, ffn_w_up, ffn_conv_w, ffn_w_down, final_norm_g, loss_target, m_ada_w, m_ada_b, m_norm1_g, m_norm2_g, m_ab_w_in, m_a_conv_w, m_b_mix_w, m_b_scale, m_ab_w_out, m_cd_w_in, m_c_q_norm_g, m_c_w_uq, m_c_kv_norm_g, m_c_w_ukv, m_d_ln_g, m_d_ln_b, m_d_w_s, m_d_b_s, m_cd_w_out, m_ffn_w_up, m_ffn_conv_w, m_ffn_w_down, m_final_norm_g, v_ada_w, v_ada_b, v_norm1_g, v_norm2_g, v_ab_w_in, v_a_conv_w, v_b_mix_w, v_b_scale, v_ab_w_out, v_cd_w_in, v_c_q_norm_g, v_c_w_uq, v_c_kv_norm_g, v_c_w_ukv, v_d_ln_g, v_d_ln_b, v_d_w_s, v_d_b_s, v_cd_w_out, v_ffn_w_up, v_ffn_conv_w, v_ffn_w_down, v_final_norm_g):
    args = (x, c, positions, ada_w, ada_b, norm1_g, norm2_g, ab_w_in, a_conv_w, b_mix_w, b_scale, ab_w_out, cd_w_in, c_q_norm_g, c_w_uq, c_kv_norm_g, c_w_ukv, d_ln_g, d_ln_b, d_w_s, d_b_s, cd_w_out, ffn_w_up, ffn_conv_w, ffn_w_down, final_norm_g, loss_target, m_ada_w, m_ada_b, m_norm1_g, m_norm2_g, m_ab_w_in, m_a_conv_w, m_b_mix_w, m_b_scale, m_ab_w_out, m_cd_w_in, m_c_q_norm_g, m_c_w_uq, m_c_kv_norm_g, m_c_w_ukv, m_d_ln_g, m_d_ln_b, m_d_w_s, m_d_b_s, m_cd_w_out, m_ffn_w_up, m_ffn_conv_w, m_ffn_w_down, m_final_norm_g, v_ada_w, v_ada_b, v_norm1_g, v_norm2_g, v_ab_w_in, v_a_conv_w, v_b_mix_w, v_b_scale, v_ab_w_out, v_cd_w_in, v_c_q_norm_g, v_c_w_uq, v_c_kv_norm_g, v_c_w_ukv, v_d_ln_g, v_d_ln_b, v_d_w_s, v_d_b_s, v_cd_w_out, v_ffn_w_up, v_ffn_conv_w, v_ffn_w_down, v_final_norm_g)
    a = dict(zip(_INPUTS, args, strict=True))
    xi, yi, ci = _place()
    chip = 2 * xi + yi
    dev = 4 * xi + 2 * yi + ci
    x = a["x"][0]
    tgt = a["loss_target"][0]

    bf = lambda t: t.astype(BF16)
    mix0_handle, tok = start_gather([bf(a["ab_w_in"][0]), bf(a["ab_w_out"][0])], "mix0")
    up0_16, down0_16, up1_16, down1_16 = [bf(a[n][l]) for l in (0, 1) for n in ("ffn_w_up", "ffn_w_down")]
    mix1_16 = [bf(a[n][0]) for n in ("cd_w_in", "c_w_uq", "c_w_ukv", "cd_w_out")]

    small_parts = [a["c"] + tok] + [a[n] for n, _, _ in _SMALL_SHARDED]
    rows1 = -(-sum(p.size for p in small_parts) // LANES // 8) * 8
    g1 = all_gather8(_pack_rows(small_parts, rows1, F32), "gather_small",
                     [up0_16, down0_16, up1_16, down1_16, mix1_16[0], mix1_16[3]]).reshape(N_DEV, rows1 * LANES)
    c_all = g1[:, :D_MODEL]
    per_chip = g1[0::2]
    small_full = {}
    off = D_MODEL
    for n, shp, axis in _SMALL_SHARDED:
        piece = per_chip[:, off:off + _size(shp)].reshape((N_CHIPS,) + shp)
        small_full[n] = jnp.concatenate([piece[k] for k in range(N_CHIPS)], axis=axis)
        off += _size(shp)

    merge = lambda t: t.reshape(t.shape[0] * t.shape[1], t.shape[2])
    w = dict(norm1_g=a["norm1_g"], norm2_g=a["norm2_g"], b_mix_w=a["b_mix_w"][0], b_scale=a["b_scale"],
             c_kv_norm_g=a["c_kv_norm_g"], d_w_s=a["d_w_s"][0], d_b_s=a["d_b_s"][0],
             final_norm_g=a["final_norm_g"].reshape(1, D_MODEL), **small_full)

    ncol = N_MOD * D_MODEL // N_CHIPS
    ada_b_mine = lax.dynamic_slice_in_dim(a["ada_b"], chip * ncol, ncol, axis=1)
    mod_cols = ada_mod(c_all, a["ada_w"], ada_b_mine)
    g2_rows = all_gather8(mod_cols.reshape(-1, LANES), "gather_mod")
    g2 = g2_rows.reshape(N_DEV, 2, N_DEV, ncol)
    mod = lax.dynamic_index_in_dim(g2[0::2], dev, axis=2, keepdims=False)
    mod = mod.transpose(1, 0, 2).reshape(2, N_MOD * D_MODEL)

    late = [g2_rows]
    up0_handle, tok_a = start_gather([up0_16], "up0", late)
    down0_handle, tok_b = start_gather([down0_16], "down0", late)
    mix1_handle, tok_c = start_gather(mix1_16, "mix1", late)
    ffn1_handle, tok_d = start_gather([up1_16, down1_16], "ffn1", late)
    mod = mod + (tok_a + tok_b + tok_c + tok_d)

    ropes = rope_tables(a["positions"][0])
    cm16 = lambda t: chip_major(t).astype(BF16)
    w.update(ffn_w_up=[None, None], ffn_w_down=[None, None])
    handles = dict(mix0=mix0_handle, up0=up0_handle, down0=down0_handle, mix1=mix1_handle, ffn1=ffn1_handle)
    reducing, reduced = {}, {}

    class Hooks(StepHooks):
        def weights(self, stage, after):
            got = finish_gather(handles[stage], chip, stage, after)
            if stage == "mix0":
                w.update(ab_w_in=got[0], ab_w_out=merge(got[1]))
            elif stage == "up0":
                w["ffn_w_up"][0] = got[0]
            elif stage == "down0":
                w["ffn_w_down"][0] = merge(got[0])
            elif stage == "mix1":
                cd_in, uq, ukv, cd_out = got
                w.update(prepare_weights(dict(cd_w_in=from_chip_major(cd_in), c_w_uq=from_chip_major(uq),
                                              c_w_ukv=from_chip_major(ukv), cd_w_out=merge(cd_out))))
            else:
                w["ffn_w_up"][1], w["ffn_w_down"][1] = got[0], merge(got[1])

        def gradients(self, stage, grads, after):
            if stage in ("ffn0", "ffn1"):
                parts = [grads["ffn_w_up"], _rows_major(grads["ffn_w_down"])]
            elif stage == "mix1":
                grads.update(unprepare_grads(grads))
                parts = [cm16(grads["cd_w_in"]), cm16(grads["c_w_uq"]), cm16(grads["c_w_ukv"]),
                         _rows_major(grads["cd_w_out"]).astype(BF16)]
            else:
                parts = [grads["ab_w_in"], _rows_major(grads["ab_w_out"])]
            reducing[stage], tok = start_reduce(parts, ci, stage)
            before = {"mix1": "ffn1", "ffn0": "mix1", "mix0": "ffn0"}.get(stage)
            if before is not None:
                reduced[before] = finish_reduce(reducing[before], chip, ci, before, after)
            return tok

    loss, grad_x, dmod, by_stage = run_step(x, tgt, mod, ropes, w, Hooks())
    grads = merge_grads(by_stage)

    parts3 = [dmod] + [grads[n] for n, _ in _SMALL_GRADS] + [loss[0, 0]]
    rows3 = -(-sum(p.size for p in parts3) // LANES // 8) * 8
    small_handle, _ = split_start("small_grads_start", EVERYONE, [_pack_rows(parts3, rows3, F32)],
                                  [_sds((N_DEV, rows3, LANES))])
    red_up1, red_down1 = reduced["ffn1"]
    red_cd_in, red_uq, red_ukv, red_cd_out = reduced["mix1"]
    red_up0, red_down0 = reduced["ffn0"]
    out_grads = dict(cd_w_in=red_cd_in, c_w_uq=red_uq, c_w_ukv=red_ukv, cd_w_out=red_cd_out)
    per_layer = dict(ffn_w_up=(red_up0, red_up1), ffn_w_down=(red_down0, red_down1))
    updates = {}

    def update(n):
        if n in per_layer:
            updates[n] = adamw_layers(a[n], *per_layer[n], a["m_" + n], a["v_" + n], "adamw_" + n)
        else:
            updates[n] = adamw(a[n], out_grads[n].reshape(a[n].shape), a["m_" + n], a["v_" + n], "adamw_" + n)

    early =("ffn_w_up", "ffn_w_down", "cd_w_in", "c_w_uq", "c_w_ukv", "cd_w_out")
    for n in early:
        update(n)
    (mine,), (landed,) = split_wait("small_grads_wait", EVERYONE, small_handle, [updates[n][1] for n in early])
    g3 = lax.dynamic_update_index_in_dim(landed, mine, dev, 0)
    summed = sum8(g3).reshape(-1)
    nmod = 2 * N_MOD * D_MODEL
    out_grads["ada_b"] = summed[:nmod].reshape(2, N_MOD * D_MODEL)
    off = nmod
    for n, shp in _SMALL_GRADS:
        out_grads[n] = summed[off:off + _size(shp)].reshape(shp)
        off += _size(shp)
    loss = summed[off]
    for n, shp, axis in _SMALL_SHARDED:
        width = out_grads[n].shape[-1] // N_CHIPS
        out_grads[n] = lax.dynamic_slice_in_dim(out_grads[n], chip * width, width, axis=out_grads[n].ndim - 1)
    dmod_all = g3.reshape(N_DEV, rows3 * LANES)[:, :nmod].reshape(N_DEV, 2, N_MOD * D_MODEL)
    dmod_mine = lax.dynamic_slice_in_dim(dmod_all, chip * ncol, ncol, axis=2).transpose(1, 0, 2)
    updates["ada_w"] = adamw_ada(a["ada_w"], c_all, dmod_mine, a["m_ada_w"], a["v_ada_w"])

    red_in0, red_out0 = finish_reduce(reducing["mix0"], chip, ci, "mix0", updates["ada_w"][1])
    out_grads.update(ab_w_in=red_in0, ab_w_out=red_out0)

    for n in ("ab_w_in", "ab_w_out"):
        update(n)
    small = [n for n in _WEIGHTS if n not in updates]
    for n, res in zip(small, adamw_small([a[n] for n in small], [out_grads[n].reshape(a[n].shape) for n in small],
                                         [a["m_" + n] for n in small], [a["v_" + n] for n in small])):
        updates[n] = res
    return (loss, grad_x[None], *[updates[n][i] for i in range(4) for n in _WEIGHTS])
```

```python
import functools
from typing import NamedTuple

import jax
import jax.numpy as jnp
from jax import lax
from jax.experimental import pallas as pl
from jax.experimental.pallas import tpu as pltpu

F32 = jnp.float32
BF16 = jnp.bfloat16
EPS = 1e-6
D_MODEL = 1024
N_MOD = 6
A_WIDTH = 512
B_GROUPS = 4
C_HEADS = 8
C_NOPE = 64
C_ROPE = 32
C_V = 64
C_Q_RANK = 256
C_KV_RANK = 128
HEAD_PAD = 128
ROPE_THETA = 10000.0
D_GROUPS = 4
D_CHUNK = 128
D_FF = 2816
FF_UNIT = 128
ADAM_LR = 0.001
ADAM_B1 = 0.9
ADAM_B2 = 0.999
ADAM_EPS = 1e-08
ADAM_WD = 0.01
ADAM_STEP = 10
N_CHIPS = 4
N_DEV = 8
LANES = 128
VMEM_BIG = 56 * 1024 * 1024
MESH = pl.DeviceIdType.MESH


def _sds(shape, dtype=F32):
    return jax.ShapeDtypeStruct(tuple(shape), dtype)


def _tile(n, cap, mult=128):
    if n <= cap:
        return n
    best = None
    for t in range(mult, cap + 1, mult):
        if n % t == 0:
            best = t
    assert best is not None, (n, cap, mult)
    return best


def _params(dims=None, vmem=None):
    return pltpu.CompilerParams(dimension_semantics=dims, vmem_limit_bytes=vmem)


def _shift_down(v, k):
    r = pltpu.roll(v, k, axis=0)
    t = lax.broadcasted_iota(jnp.int32, v.shape, 0)
    return jnp.where(t >= k, r, 0.0)


def _shift_up(v, k):
    n = v.shape[0]
    r = pltpu.roll(v, n - k, axis=0)
    t = lax.broadcasted_iota(jnp.int32, v.shape, 0)
    return jnp.where(t < n - k, r, 0.0)


def _sigmoid(v):
    return 1.0 / (1.0 + jnp.exp(-v))


_GELU_C = 0.7978845608028654
_GELU_A = 0.044715


def _gelu(v):
    return 0.5 * v * (1.0 + jnp.tanh(_GELU_C * (v + _GELU_A * v * v * v)))


def _gelu_grad(v):
    th = jnp.tanh(_GELU_C * (v + _GELU_A * v * v * v))
    return 0.5 * (1.0 + th) + 0.5 * v * (1.0 - th * th) * _GELU_C * (1.0 + 3.0 * _GELU_A * v * v)


_NN = (((1,), (0,)), ((), ()))
_NT = (((1,), (1,)), ((), ()))
_TN = (((0,), (0,)), ((), ()))


def _dot(a, b, dims=_NN):
    return lax.dot_general(a, b, dims, preferred_element_type=F32)


def _logical(t, groups):
    return (t.shape[-2], t.shape[-1] * groups)


def _block(tr, tc, groups, cols, where):
    if groups == 1:
        return pl.BlockSpec((tr, tc), where)
    per = cols // groups // tc

    def index(i, j, s):
        r, c = where(i, j, s)
        return (c // per, r, c % per)

    return pl.BlockSpec((None, tr, tc), index)


def matmul(a, b, mode, out_dtype, name, ga=1, gb=1, go=1, tm=None, tn=None, tk=None):
    (ar, ac), (br, bc) = _logical(a, ga), _logical(b, gb)
    if mode == "nn":
        m, k, n = ar, ac, bc
        a_col, b_col = "k", "n"
    elif mode == "nt":
        m, k, n = ar, ac, br
        a_col, b_col = "k", "k"
    else:
        k, m, n = ar, ac, bc
        a_col, b_col = "m", "n"
    limit = {"m": m, "n": n // go, "k": k}
    limit[a_col] = min(limit[a_col], ac // ga)
    limit[b_col] = min(limit[b_col], bc // gb)
    tm = tm or _tile(limit["m"], 2048, 128 if mode == "tn" else 16)
    tn = tn or _tile(limit["n"], 512)
    tk = tk or _tile(limit["k"], 2048, 16 if mode == "tn" else 128)
    nk = k // tk
    if mode == "nn":
        a_spec = _block(tm, tk, ga, ac, lambda i, j, s: (i, s))
        b_spec = _block(tk, tn, gb, bc, lambda i, j, s: (s, j))
        dims = _NN
    elif mode == "nt":
        a_spec = _block(tm, tk, ga, ac, lambda i, j, s: (i, s))
        b_spec = _block(tn, tk, gb, bc, lambda i, j, s: (j, s))
        dims = _NT
    else:
        a_spec = _block(tk, tm, ga, ac, lambda i, j, s: (s, i))
        b_spec = _block(tk, tn, gb, bc, lambda i, j, s: (s, j))
        dims = _TN
    o_spec = _block(tm, tn, go, n, lambda i, j, s: (i, j))
    out_shape = _sds((m, n), out_dtype) if go == 1 else _sds((go, m, n // go), out_dtype)

    def body(a_ref, b_ref, o_ref, acc_ref):
        s = pl.program_id(2)

        @pl.when(s == 0)
        def _():
            acc_ref[...] = jnp.zeros_like(acc_ref)

        acc_ref[...] += _dot(a_ref[...], b_ref[...], dims)

        @pl.when(s == nk - 1)
        def _():
            o_ref[...] = acc_ref[...].astype(o_ref.dtype)

    return pl.pallas_call(
        body, name=name, out_shape=out_shape, grid=(m // tm, n // tn, nk),
        in_specs=[a_spec, b_spec], out_specs=o_spec,
        scratch_shapes=[pltpu.VMEM((tm, tn), F32)],
        compiler_params=_params(("parallel", "parallel", "arbitrary"), VMEM_BIG),
    )(a, b)


def _rows(tm, n):
    return pl.BlockSpec((tm, n), lambda i: (i, 0))


def _vec(n):
    return pl.BlockSpec((1, n), lambda i: (0, 0))


class Vec(NamedTuple):
    array: jax.Array
    row: int
    col: int


def _vec_in(v, d):
    return pl.BlockSpec((None, 1, d), lambda i: (v.row, 0, v.col))


def modnorm_fwd(x, g, sc, sh, name):
    s, d = x.shape
    tm = _tile(s, 256, 8)

    def body(x_ref, g_ref, sc_ref, sh_ref, o_ref):
        xv = x_ref[...]
        r = lax.rsqrt(jnp.mean(xv * xv, axis=-1, keepdims=True) + EPS)
        o_ref[...] = ((xv * r) * g_ref[...] * (1.0 + sc_ref[...]) + sh_ref[...]).astype(BF16)

    return pl.pallas_call(
        body, name=name, out_shape=_sds((s, d), BF16), grid=(s // tm,),
        in_specs=[_rows(tm, d), _vec_in(g, d), _vec_in(sc, d), _vec_in(sh, d)], out_specs=_rows(tm, d),
        compiler_params=_params(("parallel",)),
    )(x, g.array, sc.array, sh.array)


def norm_bwd(x, dh, g, sc, dres, name):
    s, d = x.shape
    tm = _tile(s, 256, 8)
    nsteps = s // tm

    def body(x_ref, dh_ref, g_ref, sc_ref, dr_ref, dx_ref, dsh_ref, dsc_ref, dg_ref, a2_ref):
        i = pl.program_id(0)

        @pl.when(i == 0)
        def _():
            dsh_ref[...] = jnp.zeros_like(dsh_ref)
            a2_ref[...] = jnp.zeros_like(a2_ref)

        xv = x_ref[...]
        dh = dh_ref[...].astype(F32)
        r = lax.rsqrt(jnp.mean(xv * xv, axis=-1, keepdims=True) + EPS)
        xh = xv * r
        dsh_ref[...] += jnp.sum(dh, axis=0, keepdims=True)
        a2_ref[...] += jnp.sum(dh * xh, axis=0, keepdims=True)
        dxh = dh * (g_ref[...] * (1.0 + sc_ref[...]))
        dx = r * (dxh - xh * jnp.mean(dxh * xh, axis=-1, keepdims=True))
        dx_ref[...] = dr_ref[...] + dx

        @pl.when(i == nsteps - 1)
        def _():
            dsc_ref[...] = a2_ref[...] * g_ref[...]
            dg_ref[...] = a2_ref[...] * (1.0 + sc_ref[...])

    return pl.pallas_call(
        body, name=name, out_shape=(_sds((s, d)), _sds((1, d)), _sds((1, d)), _sds((1, d))), grid=(nsteps,),
        in_specs=[_rows(tm, d), _rows(tm, d), _vec_in(g, d), _vec_in(sc, d), _rows(tm, d)],
        out_specs=(_rows(tm, d), _vec(d), _vec(d), _vec(d)),
        scratch_shapes=[pltpu.VMEM((1, d), F32)],
        compiler_params=_params(("arbitrary",)),
    )(x, dh, g.array, sc.array, dres)


def resid_modnorm_fwd(x, y, gate, g, sc, sh, name):
    s, d = x.shape
    tm = _tile(s, 256, 8)

    def body(x_ref, y_ref, gate_ref, g_ref, sc_ref, sh_ref, xo_ref, h_ref):
        xv = x_ref[...] + gate_ref[...] * y_ref[...].astype(F32)
        xo_ref[...] = xv
        r = lax.rsqrt(jnp.mean(xv * xv, axis=-1, keepdims=True) + EPS)
        h_ref[...] = ((xv * r) * g_ref[...] * (1.0 + sc_ref[...]) + sh_ref[...]).astype(BF16)

    return pl.pallas_call(
        body, name=name, out_shape=(_sds((s, d)), _sds((s, d), BF16)), grid=(s // tm,),
        in_specs=[_rows(tm, d), _rows(tm, d), _vec_in(gate, d), _vec_in(g, d), _vec_in(sc, d), _vec_in(sh, d)],
        out_specs=(_rows(tm, d), _rows(tm, d)),
        compiler_params=_params(("parallel",)),
    )(x, y, gate.array, g.array, sc.array, sh.array)


def norm_gate_bwd(x, dh, g, sc, dres, y, gate, name):
    s, d = x.shape
    tm = _tile(s, 256, 8)
    nsteps = s // tm

    def body(x_ref, dh_ref, g_ref, sc_ref, dr_ref, y_ref, gate_ref, dx_ref, dsh_ref, dsc_ref, dg_ref, dy_ref,
             dgate_ref, a2_ref):
        i = pl.program_id(0)

        @pl.when(i == 0)
        def _():
            dsh_ref[...] = jnp.zeros_like(dsh_ref)
            a2_ref[...] = jnp.zeros_like(a2_ref)
            dgate_ref[...] = jnp.zeros_like(dgate_ref)

        xv = x_ref[...]
        dh = dh_ref[...].astype(F32)
        r = lax.rsqrt(jnp.mean(xv * xv, axis=-1, keepdims=True) + EPS)
        xh = xv * r
        dsh_ref[...] += jnp.sum(dh, axis=0, keepdims=True)
        a2_ref[...] += jnp.sum(dh * xh, axis=0, keepdims=True)
        dxh = dh * (g_ref[...] * (1.0 + sc_ref[...]))
        dr = dr_ref[...] + r * (dxh - xh * jnp.mean(dxh * xh, axis=-1, keepdims=True))
        dx_ref[...] = dr
        dy_ref[...] = (dr * gate_ref[...]).astype(BF16)
        dgate_ref[...] += jnp.sum(dr * y_ref[...].astype(F32), axis=0, keepdims=True)

        @pl.when(i == nsteps - 1)
        def _():
            dsc_ref[...] = a2_ref[...] * g_ref[...]
            dg_ref[...] = a2_ref[...] * (1.0 + sc_ref[...])

    vec = _sds((1, d))
    return pl.pallas_call(
        body, name=name, out_shape=(_sds((s, d)), vec, vec, vec, _sds((s, d), BF16), vec), grid=(nsteps,),
        in_specs=[_rows(tm, d), _rows(tm, d), _vec_in(g, d), _vec_in(sc, d), _rows(tm, d), _rows(tm, d), _vec_in(gate, d)],
        out_specs=(_rows(tm, d), _vec(d), _vec(d), _vec(d), _rows(tm, d), _vec(d)),
        scratch_shapes=[pltpu.VMEM((1, d), F32)],
        compiler_params=_params(("arbitrary",)),
    )(x, dh, g.array, sc.array, dres, y, gate.array)


def final_fused(x, f, gate, g, tgt):
    s, d = x.shape
    tm = _tile(s, 256, 8)

    def body(x_ref, f_ref, gate_ref, g_ref, t_ref, dx_ref, dg_ref, loss_ref, df_ref, dgate_ref):
        @pl.when(pl.program_id(0) == 0)
        def _():
            dg_ref[...] = jnp.zeros_like(dg_ref)
            loss_ref[...] = jnp.zeros_like(loss_ref)
            dgate_ref[...] = jnp.zeros_like(dgate_ref)

        fv, gatev, gv = f_ref[...].astype(F32), gate_ref[...], g_ref[...]
        xv = x_ref[...] + gatev * fv
        r = lax.rsqrt(jnp.mean(xv * xv, axis=-1, keepdims=True) + EPS)
        xh = xv * r
        e = xh * gv - t_ref[...]
        row = jnp.sum(e * e, axis=-1, keepdims=True) * (0.5 / d)
        loss_ref[...] += jnp.sum(row, axis=0, keepdims=True)
        dy = e * (1.0 / d)
        dg_ref[...] += jnp.sum(dy * xh, axis=0, keepdims=True)
        dxh = dy * gv
        dx = r * (dxh - xh * jnp.mean(dxh * xh, axis=-1, keepdims=True))
        dx_ref[...] = dx
        df_ref[...] = (dx * gatev).astype(BF16)
        dgate_ref[...] += jnp.sum(dx * fv, axis=0, keepdims=True)

    vec = _sds((1, d))
    return pl.pallas_call(
        body, name="final_fused", out_shape=(_sds((s, d)), vec, _sds((1, LANES)), _sds((s, d), BF16), vec),
        grid=(s // tm,),
        in_specs=[_rows(tm, d), _rows(tm, d), _vec_in(gate, d), _vec_in(g, d), _rows(tm, d)],
        out_specs=(_rows(tm, d), _vec(d), _vec(LANES), _rows(tm, d), _vec(d)),
        compiler_params=_params(("arbitrary",)),
    )(x, f, gate.array, g.array, tgt)


def _taps(v):
    return _shift_down(v, 2), _shift_down(v, 1), v


def _conv3_taps(taps, w):
    return w[0:1, :] * taps[0] + w[1:2, :] * taps[1] + w[2:3, :] * taps[2]


def _conv3(v, w):
    return _conv3_taps(_taps(v), w)


def _conv3_t(dv, w):
    return w[0:1, :] * _shift_up(dv, 2) + w[1:2, :] * _shift_up(dv, 1) + w[2:3, :] * dv


def _conv3_dw_taps(dv, taps):
    return jnp.concatenate([jnp.sum(dv * t, axis=0, keepdims=True) for t in taps], axis=0)


def _conv3_dw(dv, v):
    return _conv3_dw_taps(dv, _taps(v))


def gconv_fwd(z, conv_w):
    s = z.shape[0]
    nb = A_WIDTH // LANES

    def body(b_ref, c_ref, a_ref, w_ref, o_ref):
        b, c, a = b_ref[...].astype(F32), c_ref[...].astype(F32), a_ref[...].astype(F32)
        o_ref[...] = (b * _conv3(c * a, w_ref[...])).astype(BF16)

    col = lambda off: pl.BlockSpec((s, LANES), lambda j: (0, off + j))
    return pl.pallas_call(
        body, name="gconv_fwd", out_shape=_sds((s, A_WIDTH + _B_WIDTH), BF16), grid=(nb,),
        in_specs=[col(0), col(nb), col(2 * nb), pl.BlockSpec((3, LANES), lambda j: (0, j))],
        out_specs=pl.BlockSpec((s, LANES), lambda j: (0, j)),
        compiler_params=_params(("parallel",), VMEM_BIG),
    )(z, z, z, conv_w)


def gconv_bwd(z, conv_w, dycat):
    s = z.shape[0]
    nb = A_WIDTH // LANES

    def body(b_ref, c_ref, a_ref, w_ref, dy_ref, db_ref, dc_ref, da_ref, dw_ref):
        c, a, w, dy = c_ref[...].astype(F32), a_ref[...].astype(F32), w_ref[...], dy_ref[...].astype(F32)
        ca = c * a
        db_ref[...] = (dy * _conv3(ca, w)).astype(BF16)
        dconv = dy * b_ref[...].astype(F32)
        dw_ref[...] = _conv3_dw(dconv, ca)
        dca = _conv3_t(dconv, w)
        dc_ref[...] = (dca * a).astype(BF16)
        da_ref[...] = (dca * c).astype(BF16)

    col = lambda off: pl.BlockSpec((s, LANES), lambda j: (0, off + j))
    wspec = pl.BlockSpec((3, LANES), lambda j: (0, j))
    part = _sds((s, A_WIDTH), BF16)
    return pl.pallas_call(
        body, name="gconv_bwd", out_shape=(part, part, part, _sds((3, A_WIDTH))), grid=(nb,),
        in_specs=[col(0), col(nb), col(2 * nb), wspec, col(0)],
        out_specs=(col(0), col(0), col(0), wspec),
        compiler_params=_params(("parallel",), VMEM_BIG),
    )(z, z, z, conv_w, dycat)


def _pool_counts(s, w):
    t = lax.broadcasted_iota(jnp.int32, (s, 1), 0)
    return jnp.minimum(t + 1, w).astype(F32)


def _pooled(p, levels):
    acc = p
    for lv in range(levels):
        acc = acc + _shift_down(acc, 2 ** lv)
    return acc / _pool_counts(p.shape[0], 2 ** levels) - p


_B_WIDTH = B_GROUPS * LANES


def pool_fwd(z, mix_w, scale, ycat):
    s = z.shape[0]

    def body(p_ref, m_ref, sc_ref, ycat_ref, o_ref):
        del ycat_ref
        for g in range(B_GROUPS):
            cols = slice(g * LANES, (g + 1) * LANES)
            pooled = _pooled(p_ref[:, cols].astype(F32), g + 1)
            y = _dot(pooled.astype(BF16), m_ref[g].astype(BF16))
            o_ref[:, cols] = (y * sc_ref[:, cols]).astype(BF16)

    return pl.pallas_call(
        body, name="pool_fwd", out_shape=_sds(ycat.shape, BF16), grid=(1,),
        in_specs=[pl.BlockSpec((s, _B_WIDTH), lambda i: (0, 3 * A_WIDTH // _B_WIDTH)),
                  pl.BlockSpec((B_GROUPS, LANES, LANES), lambda i: (0, 0, 0)), pl.BlockSpec((1, _B_WIDTH), lambda i: (0, 0)),
                  pl.BlockSpec(memory_space=pl.ANY)],
        out_specs=pl.BlockSpec((s, _B_WIDTH), lambda i: (0, A_WIDTH // _B_WIDTH)),
        input_output_aliases={3: 0},
        compiler_params=_params(("arbitrary",), VMEM_BIG),
    )(z, mix_w, scale, ycat)


def pool_bwd(z, mix_w, scale, dycat):
    s = z.shape[0]

    def body(p_ref, m_ref, sc_ref, dy_ref, dp_ref, dm_ref, dsc_ref):
        for g in range(B_GROUPS):
            cols = slice(g * LANES, (g + 1) * LANES)
            pooled = _pooled(p_ref[:, cols].astype(F32), g + 1)
            mw = m_ref[g].astype(BF16)
            pb = pooled.astype(BF16)
            dy = dy_ref[:, cols].astype(F32)
            dsc_ref[:, cols] = jnp.sum(dy * _dot(pb, mw), axis=0, keepdims=True)
            dmix = (dy * sc_ref[:, cols]).astype(BF16)
            dm_ref[g] = _dot(pb, dmix, _TN)
            dpool = _dot(dmix, mw, _NT)
            acc = dpool / _pool_counts(s, 2 ** (g + 1))
            for lv in range(g + 1):
                acc = acc + _shift_up(acc, 2 ** lv)
            dp_ref[:, cols] = (acc - dpool).astype(BF16)

    wide = lambda c: pl.BlockSpec((s, _B_WIDTH), lambda i: (0, c))
    mspec = pl.BlockSpec((B_GROUPS, LANES, LANES), lambda i: (0, 0, 0))
    vspec = pl.BlockSpec((1, _B_WIDTH), lambda i: (0, 0))
    return pl.pallas_call(
        body, name="pool_bwd", out_shape=(_sds((s, _B_WIDTH), BF16), _sds((B_GROUPS, LANES, LANES)), _sds((1, _B_WIDTH))),
        grid=(1,), in_specs=[wide(3 * A_WIDTH // _B_WIDTH), mspec, vspec, wide(A_WIDTH // _B_WIDTH)],
        out_specs=(wide(0), mspec, vspec),
        compiler_params=_params(("arbitrary",), VMEM_BIG),
    )(z, mix_w, scale, dycat)


_FF_BLOCKS = D_FF // FF_UNIT


def _ff_spec(s):
    return pl.BlockSpec((2, s, FF_UNIT), lambda j: (0, 0, j))


def _ff_wspecs():
    return [pl.BlockSpec((3, FF_UNIT), lambda j: (0, j)), pl.BlockSpec((3, FF_UNIT), lambda j: (0, _FF_BLOCKS + j))]


_FF_ROWS = 64
_FF_HALO = 16


def _chunk_taps(z_ref, half, c):
    start = pl.multiple_of(c * _FF_ROWS, _FF_ROWS)
    before = pl.multiple_of(jnp.maximum(c * _FF_ROWS - _FF_HALO, 0), _FF_HALO)
    halo = z_ref[half, pl.ds(before, _FF_HALO), :].astype(F32)
    halo = jnp.where(c > 0, halo, 0.0)
    win = jnp.concatenate([halo, z_ref[half, pl.ds(start, _FF_ROWS), :].astype(F32)], axis=0)
    return tuple(pltpu.roll(win, k, axis=0)[_FF_HALO:] for k in (2, 1)) + (win[_FF_HALO:],)


def _fold8(v):
    acc = v[0:8]
    for r in range(8, v.shape[0], 8):
        acc = acc + v[r:r + 8]
    return acc


_FF_CHUNK = 256


def ffn_act_down(zf, conv_w, w_down, name):
    s, d = zf.shape[1], w_down.shape[1]
    nk = D_FF // _FF_CHUNK
    chunk = lambda k: jnp.minimum(k, nk - 1)

    def body(z_ref, wg_ref, wu_ref, wd_ref, a_ref, f_ref, held_ref, acc_ref):
        k = pl.program_id(0)

        @pl.when(k == 0)
        def _():
            held_ref[...] = jnp.zeros_like(held_ref)
            acc_ref[...] = jnp.zeros_like(acc_ref)

        acc_ref[...] += _dot(held_ref[(k + 1) % 2], wd_ref[...])
        g = _conv3(z_ref[0].astype(F32), wg_ref[...])
        u = _conv3(z_ref[1].astype(F32), wu_ref[...])
        act = (g * _sigmoid(g) * u).astype(BF16)
        a_ref[...] = act
        held_ref[k % 2] = act

        @pl.when(k == nk)
        def _():
            f_ref[...] = acc_ref[...].astype(BF16)

    return pl.pallas_call(
        body, name=name, out_shape=(_sds((s, D_FF), BF16), _sds((s, d), BF16)), grid=(nk + 1,),
        in_specs=[pl.BlockSpec((2, s, _FF_CHUNK), lambda k: (0, 0, chunk(k))),
                  pl.BlockSpec((3, _FF_CHUNK), lambda k: (0, chunk(k))),
                  pl.BlockSpec((3, _FF_CHUNK), lambda k: (0, nk + chunk(k))),
                  pl.BlockSpec((_FF_CHUNK, d), lambda k: (jnp.maximum(k - 1, 0), 0))],
        out_specs=(pl.BlockSpec((s, _FF_CHUNK), lambda k: (0, chunk(k))), pl.BlockSpec((s, d), lambda k: (0, 0))),
        scratch_shapes=[pltpu.VMEM((2, s, _FF_CHUNK), BF16), pltpu.VMEM((s, d), F32)],
        compiler_params=_params(("arbitrary",), VMEM_BIG),
    )(zf, conv_w, conv_w, w_down)


def ffn_act_bwd(zf, conv_w, da, name):
    s = zf.shape[1]
    assert s % _FF_ROWS == 0
    nchunks = s // _FF_ROWS

    def body(z_ref, wg_ref, wu_ref, da_ref, dz_ref, dw_ref, dg_ref, du_ref):
        wg, wu = wg_ref[...], wu_ref[...]

        def first(c, acc):
            rows = pl.ds(pl.multiple_of(c * _FF_ROWS, _FF_ROWS), _FF_ROWS)
            tg, tu = _chunk_taps(z_ref, 0, c), _chunk_taps(z_ref, 1, c)
            g = _conv3_taps(tg, wg)
            u = _conv3_taps(tu, wu)
            dav = da_ref[rows, :].astype(F32)
            sg = _sigmoid(g)
            dg = dav * u * (sg * (1.0 + g * (1.0 - sg)))
            du = dav * (g * sg)
            dg_ref[rows, :] = dg
            du_ref[rows, :] = du
            return tuple(a + _fold8(d * t) for a, (d, t) in zip(acc, [(dg, t) for t in tg] + [(du, t) for t in tu]))

        zero = jnp.zeros((8, FF_UNIT), F32)
        acc = lax.fori_loop(0, nchunks, first, (zero,) * 6)
        sums = [jnp.sum(a, axis=0, keepdims=True) for a in acc]
        dw_ref[0] = jnp.concatenate(sums[:3], axis=0)
        dw_ref[1] = jnp.concatenate(sums[3:], axis=0)

        tail = pl.ds(s, _FF_HALO)
        dg_ref[tail, :] = jnp.zeros((_FF_HALO, FF_UNIT), F32)
        du_ref[tail, :] = jnp.zeros((_FF_HALO, FF_UNIT), F32)
        span = _FF_ROWS + _FF_HALO

        def second(c, carry):
            start = pl.multiple_of(c * _FF_ROWS, _FF_ROWS)
            for half, (d_ref, w) in enumerate(((dg_ref, wg), (du_ref, wu))):
                win = d_ref[pl.ds(start, span), :]
                dz = (w[0:1, :] * pltpu.roll(win, span - 2, axis=0)[:_FF_ROWS]
                      + w[1:2, :] * pltpu.roll(win, span - 1, axis=0)[:_FF_ROWS] + w[2:3, :] * win[:_FF_ROWS])
                dz_ref[half, pl.ds(start, _FF_ROWS), :] = dz.astype(BF16)
            return carry

        lax.fori_loop(0, nchunks, second, 0)

    return pl.pallas_call(
        body, name=name, out_shape=(_sds((2, s, D_FF), BF16), _sds((2, 3, D_FF))), grid=(_FF_BLOCKS,),
        in_specs=[_ff_spec(s)] + _ff_wspecs() + [pl.BlockSpec((s, FF_UNIT), lambda j: (0, j))],
        out_specs=(_ff_spec(s), pl.BlockSpec((2, 3, FF_UNIT), lambda j: (0, 0, j))),
        scratch_shapes=[pltpu.VMEM((s + _FF_HALO, FF_UNIT), F32), pltpu.VMEM((s + _FF_HALO, FF_UNIT), F32)],
        compiler_params=_params(("parallel",), VMEM_BIG),
    )(zf, conv_w, conv_w, da)


def _rope(v, cs, s1, s2):
    return v * cs + pltpu.roll(v, LANES - C_ROPE // 2, axis=1) * s1 + pltpu.roll(v, C_ROPE // 2, axis=1) * s2


def _rope_t(dv, cs, s1, s2):
    return dv * cs + pltpu.roll(dv * s1, C_ROPE // 2, axis=1) + pltpu.roll(dv * s2, LANES - C_ROPE // 2, axis=1)


def _kpe_mask(shape):
    lane = lax.broadcasted_iota(jnp.int32, shape, 1)
    return (lane >= C_NOPE) & (lane < C_NOPE + C_ROPE)


def _rms(v, g):
    r = lax.rsqrt(jnp.mean(v * v, axis=-1, keepdims=True) + EPS)
    return v * r, r


def _rms_bwd(dn, xh, r, g):
    dxh = dn * g
    return r * (dxh - xh * jnp.mean(dxh * xh, axis=-1, keepdims=True)), jnp.sum(dn * xh, axis=0, keepdims=True)


_ZQ = C_Q_RANK + C_KV_RANK + HEAD_PAD
_HW = C_HEADS * HEAD_PAD


def mla_pre_fwd(z, gq, gkv, wq, wk, wv, cs, s1, s2):
    s = z.shape[0]
    tm = _tile(s, 256, 8)

    def body(z_ref, gq_ref, gkv_ref, wq_ref, wk_ref, wv_ref, cs_ref, s1_ref, s2_ref, q_ref, k_ref, v_ref):
        zv = z_ref[...].astype(F32)
        cst, s1t, s2t = cs_ref[...], s1_ref[...], s2_ref[...]
        qh, _ = _rms(zv[:, :C_Q_RANK], None)
        qn = (qh * gq_ref[...]).astype(BF16)
        q = _dot(qn, wq_ref[...])
        kh, _ = _rms(zv[:, C_Q_RANK:C_Q_RANK + C_KV_RANK], None)
        kvn = (kh * gkv_ref[...]).astype(BF16)
        k = _dot(kvn, wk_ref[...])
        v_ref[...] = _dot(kvn, wv_ref[...]).astype(BF16)
        kpe = _rope(zv[:, C_Q_RANK + C_KV_RANK:], cst, s1t, s2t)
        for h in range(C_HEADS):
            sl = slice(h * HEAD_PAD, (h + 1) * HEAD_PAD)
            q_ref[:, sl] = _rope(q[:, sl], cst, s1t, s2t).astype(BF16)
            k_ref[:, sl] = (k[:, sl] + kpe).astype(BF16)

    full = lambda r, c: pl.BlockSpec((r, c), lambda i: (0, 0))
    hw = _sds((s, _HW), BF16)
    return pl.pallas_call(
        body, name="mla_pre_fwd", out_shape=(hw, hw, hw), grid=(s // tm,),
        in_specs=[_rows(tm, _ZQ), _vec(C_Q_RANK), _vec(C_KV_RANK), full(C_Q_RANK, _HW), full(C_KV_RANK, _HW),
                  full(C_KV_RANK, _HW), _rows(tm, LANES), _rows(tm, LANES), _rows(tm, LANES)],
        out_specs=(_rows(tm, _HW), _rows(tm, _HW), _rows(tm, _HW)),
        compiler_params=_params(("parallel",), VMEM_BIG),
    )(z, gq, gkv, wq, wk, wv, cs, s1, s2)


def mla_pre_bwd(z, gq, gkv, wq, wk, wv, cs, s1, s2, dq, dk, dv):
    s = z.shape[0]
    tm = _tile(s, 256, 8)

    def body(z_ref, gq_ref, gkv_ref, wq_ref, wk_ref, wv_ref, cs_ref, s1_ref, s2_ref, dq_ref, dk_ref, dv_ref,
             dz_ref, dwq_ref, dwk_ref, dwv_ref, dgq_ref, dgkv_ref):
        @pl.when(pl.program_id(0) == 0)
        def _():
            dwq_ref[...] = jnp.zeros_like(dwq_ref)
            dwk_ref[...] = jnp.zeros_like(dwk_ref)
            dwv_ref[...] = jnp.zeros_like(dwv_ref)
            dgq_ref[...] = jnp.zeros_like(dgq_ref)
            dgkv_ref[...] = jnp.zeros_like(dgkv_ref)

        zv = z_ref[...].astype(F32)
        cst, s1t, s2t = cs_ref[...], s1_ref[...], s2_ref[...]
        gqv, gkvv = gq_ref[...], gkv_ref[...]
        qh, rq = _rms(zv[:, :C_Q_RANK], None)
        qn = (qh * gqv).astype(BF16)
        kh, rk = _rms(zv[:, C_Q_RANK:C_Q_RANK + C_KV_RANK], None)
        kvn = (kh * gkvv).astype(BF16)

        dqv = dq_ref[...].astype(F32)
        dqp = jnp.concatenate(
            [_rope_t(dqv[:, h * HEAD_PAD:(h + 1) * HEAD_PAD], cst, s1t, s2t) for h in range(C_HEADS)], axis=1
        ).astype(BF16)
        dwq_ref[...] += _dot(qn, dqp, _TN)
        dqn = _dot(dqp, wq_ref[...], _NT)
        dql, dgq = _rms_bwd(dqn, qh, rq, gqv)
        dgq_ref[...] += dgq

        dkv = dk_ref[...]
        dkb = dkv.astype(BF16)
        dvb = dv_ref[...].astype(BF16)
        dwk_ref[...] += _dot(kvn, dkb, _TN)
        dwv_ref[...] += _dot(kvn, dvb, _TN)
        dkvn = _dot(dkb, wk_ref[...], _NT) + _dot(dvb, wv_ref[...], _NT)
        dkl, dgkv = _rms_bwd(dkvn, kh, rk, gkvv)
        dgkv_ref[...] += dgkv

        dkpe = dkv[:, :HEAD_PAD]
        for h in range(1, C_HEADS):
            dkpe = dkpe + dkv[:, h * HEAD_PAD:(h + 1) * HEAD_PAD]
        dkpe = _rope_t(jnp.where(_kpe_mask(dkpe.shape), dkpe, 0.0), cst, s1t, s2t)
        dz_ref[...] = jnp.concatenate([dql, dkl, dkpe], axis=1).astype(BF16)

    full = lambda r, c: pl.BlockSpec((r, c), lambda i: (0, 0))
    return pl.pallas_call(
        body, name="mla_pre_bwd",
        out_shape=(_sds((s, _ZQ), BF16), _sds((C_Q_RANK, _HW)), _sds((C_KV_RANK, _HW)), _sds((C_KV_RANK, _HW)),
                   _sds((1, C_Q_RANK)), _sds((1, C_KV_RANK))),
        grid=(s // tm,),
        in_specs=[_rows(tm, _ZQ), _vec(C_Q_RANK), _vec(C_KV_RANK), full(C_Q_RANK, _HW), full(C_KV_RANK, _HW),
                  full(C_KV_RANK, _HW), _rows(tm, LANES), _rows(tm, LANES), _rows(tm, LANES),
                  _rows(tm, _HW), _rows(tm, _HW), _rows(tm, _HW)],
        out_specs=(_rows(tm, _ZQ), full(C_Q_RANK, _HW), full(C_KV_RANK, _HW), full(C_KV_RANK, _HW),
                   _vec(C_Q_RANK), _vec(C_KV_RANK)),
        compiler_params=_params(("arbitrary",), VMEM_BIG),
    )(z, gq, gkv, wq, wk, wv, cs, s1, s2, dq, dk, dv)


_ATT_SCALE = (C_NOPE + C_ROPE) ** -0.5
_NEG = -1e30


def _att_exp(q, k, row0, ends_here):
    sc = _dot(q, k, _NT) * _ATT_SCALE
    tq, nk = sc.shape
    if ends_here:
        last = sc[:, nk - tq:]
        row = lax.broadcasted_iota(jnp.int32, last.shape, 0)
        col = lax.broadcasted_iota(jnp.int32, last.shape, 1)
        last = jnp.where(col <= row, last, _NEG)
        sc = last if nk == tq else jnp.concatenate([sc[:, :nk - tq], last], axis=1)
    else:
        qpos = row0 + lax.broadcasted_iota(jnp.int32, sc.shape, 0)
        kpos = lax.broadcasted_iota(jnp.int32, sc.shape, 1)
        sc = jnp.where(kpos <= qpos, sc, _NEG)
    e = jnp.exp(sc - jnp.max(sc, axis=-1, keepdims=True))
    return e, 1.0 / jnp.sum(e, axis=-1, keepdims=True)


def _causal_cases(i, nq, tq, fn):
    if nq > 8:
        fn(nq * tq, False)
        return
    for blk in range(nq):
        pl.when(i == blk)(functools.partial(fn, (blk + 1) * tq, True))


_FWD_HEADS_PER_STEP = 4
_BWD_HEADS_PER_STEP = 2


def _head_lanes(heads):
    return [slice(h * HEAD_PAD, (h + 1) * HEAD_PAD) for h in range(heads)]


def attn_fwd(q, k, v):
    s = q.shape[0]
    tq = _tile(s, 256, 8)
    nq = s // tq
    heads = _FWD_HEADS_PER_STEP
    wide = heads * HEAD_PAD

    def body(q_ref, k_ref, v_ref, o_ref):
        i = pl.program_id(1)

        def case(nk, ends_here):
            for hd in _head_lanes(heads):
                e, inv = _att_exp(q_ref[:, hd], k_ref[:nk, hd], i * tq, ends_here)
                o_ref[:, hd] = (_dot(e.astype(BF16), v_ref[:nk, hd]) * inv).astype(BF16)

        _causal_cases(i, nq, tq, case)

    qspec = pl.BlockSpec((tq, wide), lambda h, i: (i, h))
    kspec = pl.BlockSpec((s, wide), lambda h, i: (0, h))
    return pl.pallas_call(
        body, name="attn_fwd", out_shape=_sds((s, _HW + _DW), BF16), grid=(C_HEADS // heads, s // tq),
        in_specs=[qspec, kspec, kspec], out_specs=qspec,
        compiler_params=_params(("parallel", "parallel"), VMEM_BIG),
    )(q, k, v)


def attn_bwd(q, k, v, o, do_all):
    s = q.shape[0]
    tq = _tile(s, 256, 8)
    heads = _BWD_HEADS_PER_STEP
    wide = heads * HEAD_PAD

    def body(q_ref, k_ref, v_ref, o_ref, do_ref, dq_ref, dk_ref, dv_ref):
        i = pl.program_id(1)

        @pl.when(i == 0)
        def _():
            dk_ref[...] = jnp.zeros_like(dk_ref)
            dv_ref[...] = jnp.zeros_like(dv_ref)

        def case(nk, ends_here):
            for hd in _head_lanes(heads):
                qv, kv, vv, dov = q_ref[:, hd], k_ref[:nk, hd], v_ref[:nk, hd], do_ref[:, hd]
                e, inv = _att_exp(qv, kv, i * tq, ends_here)
                p = e * inv
                dp = _dot(dov, vv, _NT)
                delta = jnp.sum(dov.astype(F32) * o_ref[:, hd].astype(F32), axis=-1, keepdims=True)
                ds = (p * (dp - delta) * _ATT_SCALE).astype(BF16)
                dq_ref[:, hd] = _dot(ds, kv).astype(BF16)
                dk_ref[:nk, hd] += _dot(ds, qv, _TN)
                dv_ref[:nk, hd] += _dot(p.astype(BF16), dov, _TN)

        _causal_cases(i, s // tq, tq, case)

    qspec = pl.BlockSpec((tq, wide), lambda h, i: (i, h))
    kspec = pl.BlockSpec((s, wide), lambda h, i: (0, h))
    return pl.pallas_call(
        body, name="attn_bwd", out_shape=(_sds((s, _HW), BF16), _sds((s, _HW)), _sds((s, _HW))),
        grid=(C_HEADS // heads, s // tq),
        in_specs=[qspec, kspec, kspec, qspec, qspec], out_specs=(qspec, kspec, kspec),
        compiler_params=_params(("parallel", "arbitrary"), VMEM_BIG),
    )(q, k, v, o, do_all)


_DW = D_GROUPS * LANES


def _tril_bf16(w):
    r = lax.broadcasted_iota(jnp.int32, w.shape, 0)
    c = lax.broadcasted_iota(jnp.int32, w.shape, 1)
    return jnp.where(c <= r, w, 0.0).astype(BF16)


def _sgu_forward(zu, zv, lg, lb, ws_ref, bs):
    u = _gelu(zu)
    v = _gelu(zv)
    mu = jnp.mean(v, axis=-1, keepdims=True)
    vc = v - mu
    rstd = lax.rsqrt(jnp.mean(vc * vc, axis=-1, keepdims=True) + EPS)
    xh = vc * rstd
    vln = (xh * lg + lb).astype(BF16)
    mixed = []
    for g in range(D_GROUPS):
        wg = _tril_bf16(ws_ref[g])
        mixed.append(_dot(wg, vln[:, g * LANES:(g + 1) * LANES]) + bs[:, g:g + 1])
    return u, xh, rstd, vln, jnp.concatenate(mixed, axis=1)


_SGU_CHUNKS = 2


def sgu_fwd(z, lg, lb, ws, bs_t, ycat):
    s = z.shape[0]
    rows = _SGU_CHUNKS * D_CHUNK

    def body(zu_ref, zv_ref, lg_ref, lb_ref, ws_ref, bs_ref, ycat_ref, o_ref):
        del ycat_ref
        for c in range(_SGU_CHUNKS):
            rs = slice(c * D_CHUNK, (c + 1) * D_CHUNK)
            u, _, _, _, mixed = _sgu_forward(zu_ref[rs, :].astype(F32), zv_ref[rs, :].astype(F32), lg_ref[...],
                                             lb_ref[...], ws_ref, bs_ref[...])
            o_ref[rs, :] = (u * mixed).astype(BF16)

    return pl.pallas_call(
        body, name="sgu_fwd", out_shape=_sds(ycat.shape, BF16), grid=(s // rows,),
        in_specs=[pl.BlockSpec((rows, _DW), lambda n: (n, 1)), pl.BlockSpec((rows, _DW), lambda n: (n, 2)),
                  _vec(_DW), _vec(_DW), pl.BlockSpec((D_GROUPS, D_CHUNK, D_CHUNK), lambda n: (0, 0, 0)),
                  pl.BlockSpec((D_CHUNK, LANES), lambda n: (0, 0)), pl.BlockSpec(memory_space=pl.ANY)],
        out_specs=pl.BlockSpec((rows, _DW), lambda n: (n, _HW // _DW)),
        input_output_aliases={6: 0},
        compiler_params=_params(("parallel",)),
    )(z, z, lg, lb, ws, bs_t, ycat)


def sgu_bwd(z, lg, lb, ws, bs_t, dycat, dy_col):
    s = z.shape[0]
    rows = _SGU_CHUNKS * D_CHUNK

    def body(zu_ref, zv_ref, lg_ref, lb_ref, ws_ref, bs_ref, dy_ref, dzu_ref, dzv_ref, dws_ref, dbs_ref, dlg_ref,
             dlb_ref):
        @pl.when(pl.program_id(0) == 0)
        def _():
            dws_ref[...] = jnp.zeros_like(dws_ref)
            dbs_ref[...] = jnp.zeros_like(dbs_ref)
            dlg_ref[...] = jnp.zeros_like(dlg_ref)
            dlb_ref[...] = jnp.zeros_like(dlb_ref)

        lg = lg_ref[...]
        lane = lax.broadcasted_iota(jnp.int32, (D_CHUNK, LANES), 1)
        row = lax.broadcasted_iota(jnp.int32, (D_CHUNK, D_CHUNK), 0)
        colm = lax.broadcasted_iota(jnp.int32, (D_CHUNK, D_CHUNK), 1)
        for c in range(_SGU_CHUNKS):
            rs = slice(c * D_CHUNK, (c + 1) * D_CHUNK)
            zu, zv = zu_ref[rs, :].astype(F32), zv_ref[rs, :].astype(F32)
            u, xh, rstd, vln, mixed = _sgu_forward(zu, zv, lg, lb_ref[...], ws_ref, bs_ref[...])
            dy = dy_ref[rs, :].astype(F32)
            dzu_ref[rs, :] = (dy * mixed * _gelu_grad(zu)).astype(BF16)
            dmix = dy * u
            dvln = []
            dbs = jnp.zeros((D_CHUNK, LANES), F32)
            for g in range(D_GROUPS):
                sl = slice(g * LANES, (g + 1) * LANES)
                dmg = dmix[:, sl]
                dbs = dbs + jnp.where(lane == g, jnp.sum(dmg, axis=-1, keepdims=True), 0.0)
                dmb = dmg.astype(BF16)
                dws_ref[g] += jnp.where(colm <= row, _dot(dmb, vln[:, sl], _NT), 0.0)
                dvln.append(_dot(_tril_bf16(ws_ref[g]), dmb, _TN))
            dbs_ref[...] += dbs
            dvln = jnp.concatenate(dvln, axis=1)
            dlg_ref[...] += jnp.sum(dvln * xh, axis=0, keepdims=True)
            dlb_ref[...] += jnp.sum(dvln, axis=0, keepdims=True)
            dxh = dvln * lg
            dvv = rstd * (dxh - jnp.mean(dxh, axis=-1, keepdims=True)
                          - xh * jnp.mean(dxh * xh, axis=-1, keepdims=True))
            dzv_ref[rs, :] = (dvv * _gelu_grad(zv)).astype(BF16)

    wsspec = pl.BlockSpec((D_GROUPS, D_CHUNK, D_CHUNK), lambda n: (0, 0, 0))
    chunk = lambda cidx: pl.BlockSpec((rows, _DW), lambda n: (n, cidx))
    return pl.pallas_call(
        body, name="sgu_bwd",
        out_shape=(_sds((s, _DW), BF16), _sds((s, _DW), BF16), _sds((D_GROUPS, D_CHUNK, D_CHUNK)),
                   _sds((D_CHUNK, LANES)), _sds((1, _DW)), _sds((1, _DW))),
        grid=(s // rows,),
        in_specs=[chunk(1), chunk(2), _vec(_DW), _vec(_DW), wsspec, pl.BlockSpec((D_CHUNK, LANES), lambda n: (0, 0)),
                  chunk(dy_col)],
        out_specs=(chunk(0), chunk(0), wsspec, pl.BlockSpec((D_CHUNK, LANES), lambda n: (0, 0)), _vec(_DW), _vec(_DW)),
        compiler_params=_params(("arbitrary",)),
    )(z, z, lg, lb, ws, bs_t, dycat)


def ada_mod(c_all, ada_w, ada_b):
    nl, d, n = ada_w.shape
    nb = c_all.shape[0]
    tn = _tile(n, 512)

    def body(c_ref, w_ref, b_ref, o_ref):
        cv = c_ref[...]
        ca = (cv * _sigmoid(cv)).astype(BF16)
        o_ref[...] = _dot(ca, w_ref[...].astype(BF16)) + b_ref[...]

    return pl.pallas_call(
        body, name="ada_mod", out_shape=_sds((nl, nb, n)), grid=(nl, n // tn),
        in_specs=[pl.BlockSpec((nb, d), lambda l, j: (0, 0)), pl.BlockSpec((None, d, tn), lambda l, j: (l, 0, j)),
                  pl.BlockSpec((None, 1, tn), lambda l, j: (l, 0, j))],
        out_specs=pl.BlockSpec((None, nb, tn), lambda l, j: (l, 0, j)),
        compiler_params=_params(("parallel", "parallel")),
    )(c_all, ada_w, ada_b.reshape(nl, 1, n))


_ADAM_BLOCK = 256 * 1024


def _adam_rows(rows, cols):
    if rows * cols <= _ADAM_BLOCK or rows % 8:
        return rows
    return _tile(rows, max(8, _ADAM_BLOCK // cols), 8)


def _adam_update(w, gv, m, v):
    inv_bc1 = 1.0 / (1.0 - ADAM_B1 ** ADAM_STEP)
    inv_bc2 = 1.0 / (1.0 - ADAM_B2 ** ADAM_STEP)
    nm = ADAM_B1 * m + (1.0 - ADAM_B1) * gv
    nv = ADAM_B2 * v + (1.0 - ADAM_B2) * (gv * gv)
    return -ADAM_LR * ((nm * inv_bc1) / (jnp.sqrt(nv * inv_bc2) + ADAM_EPS) + ADAM_WD * w), nm, nv


def adamw(w, g, m, v, name):
    shape = w.shape
    cols = shape[-1]
    rows = w.size // cols
    tr = _adam_rows(rows, cols)

    def body(w_ref, g_ref, m_ref, v_ref, go_ref, d_ref, nm_ref, nv_ref):
        gv = g_ref[...]
        go_ref[...] = gv
        d_ref[...], nm_ref[...], nv_ref[...] = _adam_update(w_ref[...], gv, m_ref[...], v_ref[...])

    spec = pl.BlockSpec((tr, cols), lambda i: (i, 0))
    out = _sds((rows, cols))
    r2 = lambda t: t.reshape(rows, cols)
    res = pl.pallas_call(
        body, name=name, out_shape=(out,) * 4, grid=(rows // tr,),
        in_specs=[spec] * 4, out_specs=(spec,) * 4, compiler_params=_params(("parallel",)),
    )(r2(w), r2(g), r2(m), r2(v))
    return tuple(t.reshape(shape) for t in res)


def adamw_ada(w, c_all, dmod, m, v):
    nl, d, n = w.shape
    tr = _adam_rows(d, n)
    pad = 16 - c_all.shape[0]
    c16 = jnp.pad(c_all, ((0, pad), (0, 0)))
    dm16 = jnp.pad(dmod, ((0, 0), (0, pad), (0, 0)))

    def body(w_ref, c_ref, dm_ref, m_ref, v_ref, g_ref, d_ref, nm_ref, nv_ref):
        cv = c_ref[...]
        gv = _dot((cv * _sigmoid(cv)).astype(BF16), dm_ref[...].astype(BF16), _TN)
        g_ref[...] = gv
        d_ref[...], nm_ref[...], nv_ref[...] = _adam_update(w_ref[...], gv, m_ref[...], v_ref[...])

    spec = pl.BlockSpec((None, tr, n), lambda l, i: (l, i, 0))
    out = _sds((nl, d, n))
    return pl.pallas_call(
        body, name="adamw_ada_w", out_shape=(out, out, out, out), grid=(nl, d // tr),
        in_specs=[spec, pl.BlockSpec((16, tr), lambda l, i: (0, i)), pl.BlockSpec((None, 16, n), lambda l, i: (l, 0, 0)),
                  spec, spec],
        out_specs=(spec,) * 4, compiler_params=_params(("parallel", "parallel")),
    )(w, c16, dm16, m, v)


def adamw_small(ws, gs, ms, vs):
    n = len(ws)
    flat = lambda t: t.reshape(-1, t.shape[-1])

    def body(*refs):
        ins, outs = refs[:4 * n], refs[4 * n:]
        for i in range(n):
            w_ref, g_ref, m_ref, v_ref = ins[4 * i:4 * i + 4]
            outs[3 * i][...], outs[3 * i + 1][...], outs[3 * i + 2][...] = _adam_update(
                w_ref[...], g_ref[...], m_ref[...], v_ref[...])

    operands = [flat(t) for quad in zip(ws, gs, ms, vs) for t in quad]
    res = pl.pallas_call(
        body, name="adamw_small", out_shape=tuple(_sds(flat(w).shape) for w in ws for _ in range(3)),
    )(*operands)
    return [(g, res[3 * i].reshape(w.shape), res[3 * i + 1].reshape(w.shape), res[3 * i + 2].reshape(w.shape))
            for i, (w, g) in enumerate(zip(ws, gs))]


def adamw_layers(w, g0, g1, m, v, name):
    _, rows, cols = w.shape
    tr = _adam_rows(rows, cols)

    def body(w_ref, g0_ref, g1_ref, m_ref, v_ref, g_ref, d_ref, nm_ref, nv_ref):
        gv = jnp.where(pl.program_id(0) == 0, g0_ref[...], g1_ref[...])
        g_ref[...] = gv
        d_ref[...], nm_ref[...], nv_ref[...] = _adam_update(w_ref[...], gv, m_ref[...], v_ref[...])

    spec = pl.BlockSpec((None, tr, cols), lambda l, i: (l, i, 0))
    gspec = pl.BlockSpec((tr, cols), lambda l, i: (i, 0))
    out = _sds((2, rows, cols))
    return pl.pallas_call(
        body, name=name, out_shape=(out, out, out, out), grid=(2, rows // tr),
        in_specs=[spec, gspec, gspec, spec, spec], out_specs=(spec,) * 4, compiler_params=_params(("parallel", "parallel")),
    )(w, g0, g1, m, v)


def sum8(gathered):
    _, r, _ = gathered.shape
    tr = _tile(r, 512, 8)

    def body(g_ref, o_ref):
        acc = g_ref[0]
        for dev in range(1, N_DEV):
            acc = acc + g_ref[dev]
        o_ref[...] = acc

    return pl.pallas_call(
        body, name="sum8", out_shape=_sds((r, LANES)), grid=(r // tr,),
        in_specs=[pl.BlockSpec((N_DEV, tr, LANES), lambda i: (0, i, 0))], out_specs=pl.BlockSpec((tr, LANES), lambda i: (i, 0)),
        compiler_params=_params(("parallel",)),
    )(gathered)


_SUM_STEPS = 2


def pair_sums(gs, recvs, core, name):
    n = len(gs)
    trs = [g.shape[1] // 2 // _SUM_STEPS for g in gs]

    def body(c_ref, *refs):
        del c_ref
        for i in range(n):
            a_ref, b_ref, o_ref = refs[2 * i], refs[2 * i + 1], refs[2 * n + i]
            o_ref[...] = (a_ref[...].astype(F32) + b_ref[...].astype(F32)).astype(BF16)

    in_specs, out_specs = [], []
    for g, tr in zip(gs, trs):
        cols = g.shape[2]
        in_specs.append(pl.BlockSpec((None, tr, cols), lambda k, s, c: (k, c[0] * _SUM_STEPS + s, 0)))
        in_specs.append(pl.BlockSpec((None, tr, cols), lambda k, s, c: (k, s, 0)))
        out_specs.append(pl.BlockSpec((None, tr, cols), lambda k, s, c: (k, s, 0)))
    grid_spec = pltpu.PrefetchScalarGridSpec(num_scalar_prefetch=1, grid=(N_CHIPS, _SUM_STEPS), in_specs=in_specs,
                                             out_specs=tuple(out_specs))
    return list(pl.pallas_call(
        body, name=name, out_shape=tuple(_sds((N_CHIPS, g.shape[1] // 2, g.shape[2]), BF16) for g in gs),
        grid_spec=grid_spec, compiler_params=_params(("parallel", "parallel")),
    )(core.reshape(1).astype(jnp.int32), *[t for pair in zip(gs, recvs) for t in pair]))


def chip_sums(pairs, recvs, chip, core, name):
    n = len(pairs)
    trs = [p.shape[1] // _SUM_STEPS for p in pairs]

    def body(p_ref, *refs):
        del p_ref
        for i in range(n):
            own_ref, r_ref, o_ref = refs[2 * i], refs[2 * i + 1], refs[2 * n + i]
            acc = own_ref[...].astype(F32)
            for j in range(N_CHIPS - 1):
                acc = acc + r_ref[j].astype(F32)
            o_ref[...] = acc

    in_specs, out_specs = [], []
    for p, tr in zip(pairs, trs):
        cols = p.shape[2]
        in_specs.append(pl.BlockSpec((None, tr, cols), lambda s, q: (q[0], s, 0)))
        in_specs.append(pl.BlockSpec((N_CHIPS - 1, tr, cols), lambda s, q: (0, s, 0)))
        out_specs.append(pl.BlockSpec((None, tr, cols), lambda s, q: (q[1], s, 0)))
    grid_spec = pltpu.PrefetchScalarGridSpec(num_scalar_prefetch=1, grid=(_SUM_STEPS,), in_specs=in_specs,
                                             out_specs=tuple(out_specs))
    return list(pl.pallas_call(
        body, name=name, out_shape=tuple(_sds((2,) + p.shape[1:]) for p in pairs), grid_spec=grid_spec,
        compiler_params=_params(("parallel",)),
    )(jnp.stack([chip, core]).astype(jnp.int32), *[t for pair in zip(pairs, recvs) for t in pair]))


def _place():
    return lax.axis_index("x"), lax.axis_index("y"), lax.axis_index("c")


def _other_chips(x, y):
    return [(x, 1 - y), (1 - x, y), (1 - x, 1 - y)]


_HBM = pl.BlockSpec(memory_space=pltpu.HBM)


def all_gather8(v, name, after=()):
    m, n = v.shape

    def body(x_ref, *refs):
        out_ref, send_sems, recv_sems, local_sem = refs[len(after):]
        x, y, c = _place()
        me, sibling = (x, y, c), (x, y, 1 - c)
        chips = _other_chips(x, y)

        def rows(px, py, pc):
            return out_ref.at[pl.ds((4 * px + 2 * py + pc) * m, m), :]

        def copy(k, block, to, src=None):
            return pltpu.make_async_remote_copy(
                src_ref=rows(*block) if src is None else src, dst_ref=rows(*block),
                send_sem=send_sems.at[k], recv_sem=recv_sems.at[k], device_id=to, device_id_type=MESH)

        mine = pltpu.make_async_copy(x_ref, rows(*me), local_sem)
        mine.start()
        first = [copy(0, me, sibling, src=x_ref)]
        first += [copy(1 + j, me, (*chip, c), src=x_ref) for j, chip in enumerate(chips)]
        for cp in first:
            cp.start()
        passed = [copy(4 + j, (*chip, c), sibling) for j, chip in enumerate(chips)]
        for j, chip in enumerate(chips):
            copy(1 + j, (*chip, c), me).wait_recv()
            passed[j].start()
        copy(0, sibling, me).wait_recv()
        for j, chip in enumerate(chips):
            copy(4 + j, (*chip, 1 - c), me).wait_recv()
        for cp in first + passed:
            cp.wait_send()
        mine.wait()

    return pl.pallas_call(
        body, name=name, out_shape=_sds((N_DEV * m, n), v.dtype),
        in_specs=[pl.BlockSpec(memory_space=pltpu.VMEM)] + [pl.BlockSpec(memory_space=pl.ANY)] * len(after),
        out_specs=pl.BlockSpec(memory_space=pltpu.VMEM),
        scratch_shapes=[pltpu.SemaphoreType.DMA((7,)), pltpu.SemaphoreType.DMA((7,)), pltpu.SemaphoreType.DMA],
        compiler_params=_params(None, VMEM_BIG),
    )(v, *after)


def _comm_call(body, name, ins, out_shapes, nsem, aliases=None):
    return pl.pallas_call(
        body, name=name, out_shape=tuple(out_shapes), in_specs=[_HBM] * len(ins), out_specs=tuple([_HBM] * len(out_shapes)),
        scratch_shapes=[pltpu.SemaphoreType.DMA((nsem,)), pltpu.SemaphoreType.DMA((nsem,))],
        input_output_aliases=aliases or {},
    )(*ins)


def _remote(src, dst, send_sems, recv_sems, k, to):
    return pltpu.make_async_remote_copy(src_ref=src, dst_ref=dst, send_sem=send_sems.at[k], recv_sem=recv_sems.at[k],
                                        device_id=to, device_id_type=MESH)


def _half(core, rh):
    return pl.ds(pl.multiple_of(core * rh, 16), rh)


def swap_halves(gs, name):
    n = len(gs)

    def body(*refs):
        ins, outs, (send_sems, recv_sems) = refs[:n], refs[n:2 * n], refs[2 * n:]
        x, y, c = _place()
        copies = []
        for i in range(n):
            theirs = _half(1 - c, ins[i].shape[1] // 2)
            cp = _remote(ins[i].at[:, theirs], outs[i], send_sems, recv_sems, i, (x, y, 1 - c))
            cp.start()
            copies.append(cp)
        for cp in copies:
            cp.wait()

    return _comm_call(body, name, gs, [_sds((g.shape[0], g.shape[1] // 2, g.shape[2]), g.dtype) for g in gs], n)


def join_halves(bufs, name):
    n = len(bufs)

    def body(*refs):
        ins, outs, (send_sems, recv_sems) = refs[:n], refs[n:2 * n], refs[2 * n:]
        x, y, c = _place()
        copies = []
        for i in range(n):
            cp = _remote(ins[i].at[c], outs[i].at[c], send_sems, recv_sems, i, (x, y, 1 - c))
            cp.start()
            copies.append(cp)
        for i in range(n):
            theirs = outs[i].at[1 - c]
            _remote(theirs, theirs, send_sems, recv_sems, i, (x, y, 1 - c)).wait_recv()
        for cp in copies:
            cp.wait_send()

    return _comm_call(body, name, bufs, [_sds(b.shape, b.dtype) for b in bufs], n, {i: i for i in range(n)})


def forward_halves(lands, name):
    n = len(lands)

    def body(*refs):
        ins, outs, (send_sems, recv_sems) = refs[:n], refs[n:2 * n], refs[2 * n:]
        x, y, c = _place()
        sibling = (x, y, 1 - c)
        chips = _other_chips(x, y)
        copies = []
        for i in range(n):
            mine = _half(c, ins[i].shape[1] // 2)
            for j, (px, py) in enumerate(chips):
                cp = _remote(ins[i].at[2 * px + py, mine], outs[i].at[2 * px + py, mine], send_sems, recv_sems, 3 * i + j, sibling)
                cp.start()
                copies.append(cp)
        for i in range(n):
            theirs = _half(1 - c, ins[i].shape[1] // 2)
            for j, (px, py) in enumerate(chips):
                landed = outs[i].at[2 * px + py, theirs]
                _remote(landed, landed, send_sems, recv_sems, 3 * i + j, sibling).wait_recv()
        for cp in copies:
            cp.wait_send()

    return _comm_call(body, name, lands, [_sds(b.shape, b.dtype) for b in lands], 3 * n, {i: i for i in range(n)})


_SEM = pl.BlockSpec(memory_space=pltpu.SEMAPHORE)
_EFFECT = pltpu.SideEffectType.DATAFLOW_SIDE_EFFECTING


def _gather_copies(srcs, lands, send_sems, recv_sems):
    x, y, c = _place()
    copies = []
    for i in range(len(srcs)):
        mine = _half(c, srcs[i].shape[0] // 2)
        for j, chip in enumerate(_other_chips(x, y)):
            copies.append(_remote(srcs[i].at[mine], lands[i].at[2 * x + y, mine], send_sems, recv_sems, 3 * i + j, (*chip, c)))
    return copies


def _exchange_copies(srcs, lands, send_sems, recv_sems):
    x, y, c = _place()
    copies = []
    for i in range(len(srcs)):
        for j, (px, py) in enumerate(_other_chips(x, y)):
            copies.append(_remote(srcs[i].at[2 * px + py], lands[i].at[j], send_sems, recv_sems, 3 * i + j, (px, py, c)))
    return copies


def _everyone_copies(srcs, lands, send_sems, recv_sems):
    x, y, c = _place()
    flip = lambda v, b: 1 - v if b else v
    dst = lands[0].at[4 * x + 2 * y + c]
    return [_remote(srcs[0], dst, send_sems, recv_sems, j - 1, (flip(x, j & 4), flip(y, j & 2), flip(c, j & 1)))
            for j in range(1, N_DEV)]


GATHER = (_gather_copies, 3)
EXCHANGE = (_exchange_copies, 3)
EVERYONE = (_everyone_copies, N_DEV - 1)


def split_start(name, plan, srcs, land_shapes, after=()):
    copies_fn, per_source = plan
    n, m, k = len(srcs), len(land_shapes), len(after)
    ncopies = per_source * n

    def body(*refs):
        src_refs, land_refs = refs[:n], refs[n:n + m]
        send_sems, recv_sems = refs[n + m + k], refs[n + m + k + 1]
        token = refs[-1]
        for cp in copies_fn(src_refs, land_refs, send_sems, recv_sems):
            cp.start()
        token[...] = jnp.zeros_like(token)

    hbm = lambda s: pltpu.HBM(tuple(s.shape), s.dtype)
    outs = pl.pallas_call(
        body, name=name,
        out_shape=(pltpu.SemaphoreType.DMA((ncopies,)), pltpu.SemaphoreType.DMA((ncopies,)), *[hbm(s) for s in srcs],
                   *[hbm(s) for s in land_shapes], _sds((8, LANES))),
        in_specs=[_HBM] * (n + m) + [pl.BlockSpec(memory_space=pl.ANY)] * k,
        out_specs=(_SEM, _SEM, *([_HBM] * (n + m)), pl.BlockSpec(memory_space=pltpu.VMEM)),
        input_output_aliases={i: 2 + i for i in range(n + m)},
        compiler_params=pltpu.CompilerParams(has_side_effects=_EFFECT),
    )(*[pltpu.with_memory_space_constraint(s, pltpu.HBM) for s in srcs],
      *[pltpu.with_memory_space_constraint(lax.empty(tuple(s.shape), s.dtype), pltpu.HBM) for s in land_shapes], *after)
    handle = (outs[0], outs[1], list(outs[2:2 + n]), list(outs[2 + n:2 + n + m]))
    return handle, outs[-1][0, 0]


def split_wait(name, plan, handle, after):
    copies_fn, _ = plan
    send_sems, recv_sems, srcs, lands = handle
    n, m = len(srcs), len(lands)
    after = list(after) if isinstance(after, (list, tuple)) else [after]

    def body(*refs):
        src_refs, land_refs = refs[:n], refs[n:n + m]
        for cp in copies_fn(src_refs, land_refs, refs[n + m], refs[n + m + 1]):
            cp.wait_send()
            cp.wait_recv()

    hbm = lambda s: pltpu.HBM(tuple(s.shape), s.dtype)
    outs = pl.pallas_call(
        body, name=name, out_shape=tuple(hbm(s) for s in srcs + lands),
        in_specs=[_HBM] * (n + m) + [_SEM, _SEM] + [pl.BlockSpec(memory_space=pl.ANY)] * len(after),
        out_specs=tuple([_HBM] * (n + m)), input_output_aliases={i: i for i in range(n + m)},
        compiler_params=pltpu.CompilerParams(has_side_effects=_EFFECT),
    )(*srcs, *lands, send_sems, recv_sems, *after)
    return list(outs[:n]), list(outs[n:])


def chip_major(w, groups=N_CHIPS):
    r, c = w.shape
    return w.reshape(r, groups, c // groups).transpose(1, 0, 2)


def from_chip_major(w):
    g, r, c = w.shape
    return w.transpose(1, 0, 2).reshape(r, g * c)


def _cd_in_pad(w):
    a = C_Q_RANK + C_KV_RANK
    z = lambda n: jnp.zeros((w.shape[0], n), w.dtype)
    return jnp.concatenate([w[:, :a], z(C_NOPE), w[:, a:a + C_ROPE], z(HEAD_PAD - C_NOPE - C_ROPE), w[:, a + C_ROPE:]], axis=1)


def _cd_in_unpad(w):
    a = C_Q_RANK + C_KV_RANK
    return jnp.concatenate([w[:, :a], w[:, a + C_NOPE:a + C_NOPE + C_ROPE], w[:, a + HEAD_PAD:]], axis=1)


def _pad_heads(w, width):
    r = w.shape[0]
    w = w.reshape(r, C_HEADS, width)
    return jnp.pad(w, ((0, 0), (0, 0), (0, HEAD_PAD - width))).reshape(r, _HW)


def _unpad_heads(w, width):
    r = w.shape[0]
    return w.reshape(r, C_HEADS, HEAD_PAD)[:, :, :width].reshape(r, C_HEADS * width)


def prepare_weights(p):
    q = dict(p)
    q["cd_w_in"] = _cd_in_pad(p["cd_w_in"])
    q["c_w_uq"] = _pad_heads(p["c_w_uq"], C_NOPE + C_ROPE)
    ukv = p["c_w_ukv"].reshape(C_KV_RANK, C_HEADS, C_NOPE + C_V)
    q["c_w_uk"] = _pad_heads(ukv[:, :, :C_NOPE].reshape(C_KV_RANK, -1), C_NOPE)
    q["c_w_uv"] = _pad_heads(ukv[:, :, C_NOPE:].reshape(C_KV_RANK, -1), C_V)
    wo = p["cd_w_out"]
    att_rows = jnp.pad(wo[:C_HEADS * C_V].reshape(C_HEADS, C_V, D_MODEL), ((0, 0), (0, HEAD_PAD - C_V), (0, 0)))
    q["cd_w_out"] = jnp.concatenate([att_rows.reshape(_HW, D_MODEL), wo[C_HEADS * C_V:]], axis=0)
    return q


def unprepare_grads(g):
    q = dict(g)
    q["cd_w_in"] = _cd_in_unpad(g["cd_w_in"])
    q["c_w_uq"] = _unpad_heads(g["c_w_uq"], C_NOPE + C_ROPE)
    uk = g.pop("c_w_uk").reshape(C_KV_RANK, C_HEADS, HEAD_PAD)[:, :, :C_NOPE]
    uv = g.pop("c_w_uv").reshape(C_KV_RANK, C_HEADS, HEAD_PAD)[:, :, :C_V]
    q.pop("c_w_uk", None)
    q.pop("c_w_uv", None)
    q["c_w_ukv"] = jnp.concatenate([uk, uv], axis=-1).reshape(C_KV_RANK, C_HEADS * (C_NOPE + C_V))
    wo = g["cd_w_out"]
    att = wo[:_HW].reshape(C_HEADS, HEAD_PAD, D_MODEL)[:, :C_V].reshape(C_HEADS * C_V, D_MODEL)
    q["cd_w_out"] = jnp.concatenate([att, wo[_HW:]], axis=0)
    return q


def rope_tables(positions):
    half = C_ROPE // 2
    inv_freq = ROPE_THETA ** (-jnp.arange(half, dtype=F32) / half)
    ang = positions.astype(F32)[:, None] * inv_freq
    cos, sin = jnp.cos(ang), jnp.sin(ang)
    s = positions.shape[0]
    z = lambda n: jnp.zeros((s, n), F32)
    cs = jnp.concatenate([jnp.ones((s, C_NOPE), F32), cos, cos, z(HEAD_PAD - C_NOPE - C_ROPE)], axis=1)
    s1 = jnp.concatenate([z(C_NOPE), -sin, z(HEAD_PAD - C_NOPE - half)], axis=1)
    s2 = jnp.concatenate([z(C_NOPE + half), sin, z(HEAD_PAD - C_NOPE - C_ROPE)], axis=1)
    return cs, s1, s2


_UP_COLS = 2 * D_FF // N_CHIPS


def ffn_fwd(h2, w, l, late_down=None):
    zf = matmul(h2, w["ffn_w_up"][l], "nn", BF16, f"ffn_up{l}", gb=N_CHIPS, go=2, tn=_UP_COLS)
    if late_down is not None:
        late_down(zf)
    a, f = ffn_act_down(zf, w["ffn_conv_w"][l], w["ffn_w_down"][l], f"ffn_act_down{l}")
    return f, (zf, a)


def ffn_bwd(df, h2, saved, w, l):
    zf, a = saved
    da = matmul(df, w["ffn_w_down"][l], "nt", BF16, f"ffn_down_dx{l}", tn=D_FF // 2)
    d_down = matmul(a, df, "tn", BF16, f"ffn_down_dw{l}", tm=D_FF // 2)
    dzf, d_conv = ffn_act_bwd(zf, w["ffn_conv_w"][l], da, f"ffn_act_bwd{l}")
    dh2 = matmul(dzf, w["ffn_w_up"][l], "nt", BF16, f"ffn_up_dx{l}", ga=2, gb=N_CHIPS, tk=_UP_COLS, tn=D_MODEL)
    d_up = matmul(h2, dzf, "tn", BF16, f"ffn_up_dw{l}", gb=2, go=N_CHIPS, tn=_UP_COLS)
    d_conv = d_conv.transpose(1, 0, 2).reshape(3, 2 * D_FF)
    return dh2, dict(ffn_w_down=d_down, ffn_conv_w=d_conv, ffn_w_up=d_up)


def mixer0_fwd(h, w):
    z = matmul(h, w["ab_w_in"], "nn", BF16, "ab_in", gb=N_CHIPS)
    ycat = pool_fwd(z, w["b_mix_w"], w["b_scale"], gconv_fwd(z, w["a_conv_w"]))
    y = matmul(ycat, w["ab_w_out"], "nn", BF16, "ab_out", tn=D_MODEL)
    return y, (z, ycat)


def mixer0_bwd(dy, h, saved, w):
    z, ycat = saved
    grads = {}
    dycat = matmul(dy, w["ab_w_out"], "nt", BF16, "ab_out_dx")
    grads["ab_w_out"] = matmul(ycat, dy, "tn", BF16, "ab_out_dw")
    db, dc, da, d_conv = gconv_bwd(z, w["a_conv_w"], dycat)
    dp, d_mix, d_scale = pool_bwd(z, w["b_mix_w"], w["b_scale"], dycat)
    dz = jnp.concatenate([db, dc, da, dp], axis=1)
    dh = matmul(dz, w["ab_w_in"], "nt", BF16, "ab_in_dx", gb=N_CHIPS, tn=D_MODEL)
    grads["ab_w_in"] = matmul(h, dz, "tn", BF16, "ab_in_dw", go=N_CHIPS)
    grads.update(a_conv_w=d_conv, b_mix_w=d_mix, b_scale=d_scale)
    return dh, grads


def mixer1_fwd(h, ropes, w):
    cs, s1, s2 = ropes
    z = matmul(h, w["cd_w_in"], "nn", BF16, "cd_in")
    bs_t = jnp.pad(w["d_b_s"].T, ((0, 0), (0, LANES - D_GROUPS)))
    qh, kh, vh = mla_pre_fwd(z, w["c_q_norm_g"], w["c_kv_norm_g"], w["c_w_uq"], w["c_w_uk"], w["c_w_uv"], cs, s1, s2)
    ycat = sgu_fwd(z, w["d_ln_g"], w["d_ln_b"], w["d_w_s"], bs_t, attn_fwd(qh, kh, vh))
    y = matmul(ycat, w["cd_w_out"], "nn", BF16, "cd_out", tn=D_MODEL)
    return y, (z, bs_t, qh, kh, vh, ycat)


def mixer1_bwd(dy, h, saved, ropes, w):
    cs, s1, s2 = ropes
    z, bs_t, qh, kh, vh, ycat = saved
    grads = {}
    dycat = matmul(dy, w["cd_w_out"], "nt", BF16, "cd_out_dx")
    grads["cd_w_out"] = matmul(ycat, dy, "tn", BF16, "cd_out_dw")
    dqh, dkh, dvh = attn_bwd(qh, kh, vh, ycat, dycat)
    dzq, d_uq, d_uk, d_uv, d_gq, d_gkv = mla_pre_bwd(
        z, w["c_q_norm_g"], w["c_kv_norm_g"], w["c_w_uq"], w["c_w_uk"], w["c_w_uv"], cs, s1, s2, dqh, dkh, dvh)
    dzu, dzv, d_ws, d_bs, d_lg, d_lb = sgu_bwd(z, w["d_ln_g"], w["d_ln_b"], w["d_w_s"], bs_t, dycat, _HW // _DW)
    dz = jnp.concatenate([dzq, dzu, dzv], axis=1)
    dh = matmul(dz, w["cd_w_in"], "nt", BF16, "cd_in_dx", tn=D_MODEL)
    grads["cd_w_in"] = matmul(h, dz, "tn", BF16, "cd_in_dw")
    grads.update(c_w_uq=d_uq, c_w_uk=d_uk, c_w_uv=d_uv, c_q_norm_g=d_gq, c_kv_norm_g=d_gkv, d_w_s=d_ws,
                 d_b_s=d_bs[:, :D_GROUPS].T, d_ln_g=d_lg, d_ln_b=d_lb)
    return dh, grads


class StepHooks:
    def weights(self, stage, after):
        pass

    def gradients(self, stage, grads, after):
        return 0.0


def run_step(x, tgt, mod, ropes, w, hooks):
    sh1, sc1, g1, sh2, sc2, g2 = range(N_MOD)
    mods = mod.reshape(2, 1, N_MOD * D_MODEL)
    n1 = w["norm1_g"].reshape(2, 1, D_MODEL)
    n2 = w["norm2_g"].reshape(2, 1, D_MODEL)
    final_g = Vec(w["final_norm_g"].reshape(1, 1, D_MODEL), 0, 0)

    hooks.weights("mix0", mod)
    h0 = modnorm_fwd(x, Vec(n1, 0, 0), Vec(mods, 0, sc1), Vec(mods, 0, sh1), "modnorm_0")
    y0, mix0 = mixer0_fwd(h0, w)
    x1, h1 = resid_modnorm_fwd(x, y0, Vec(mods, 0, g1), Vec(n2, 0, 0), Vec(mods, 0, sc2), Vec(mods, 0, sh2), "resid_modnorm_1")
    hooks.weights("up0", x1)
    f0, ffn0 = ffn_fwd(h1, w, 0, lambda act: hooks.weights("down0", act))
    x2, h2 = resid_modnorm_fwd(x1, f0, Vec(mods, 0, g2), Vec(n1, 1, 0), Vec(mods, 1, sc1), Vec(mods, 1, sh1), "resid_modnorm_2")
    hooks.weights("mix1", x2)
    y1, mix1 = mixer1_fwd(h2, ropes, w)
    x3, h3 = resid_modnorm_fwd(x2, y1, Vec(mods, 1, g1), Vec(n2, 1, 0), Vec(mods, 1, sc2), Vec(mods, 1, sh2), "resid_modnorm_3")
    hooks.weights("ffn1", x3)
    f1, ffn1 = ffn_fwd(h3, w, 1)
    dres, d_final, loss, df1, dg2b = final_fused(x3, f1, Vec(mods, 1, g2), final_g, tgt)

    dh3, gf1 = ffn_bwd(df1, h3, ffn1, w, 1)
    late = mods + hooks.gradients("ffn1", gf1, dh3)
    dres, dsh2b, dsc2b, dn2b, dy1, dg1b = norm_gate_bwd(
        x3, dh3, Vec(n2, 1, 0), Vec(late, 1, sc2), dres, y1, Vec(late, 1, g1), "norm_gate_bwd_3")
    dh2, gm1 = mixer1_bwd(dy1, h2, mix1, ropes, w)
    late = mods + hooks.gradients("mix1", gm1, dh2)
    dres, dsh1b, dsc1b, dn1b, df0, dg2a = norm_gate_bwd(
        x2, dh2, Vec(n1, 1, 0), Vec(late, 1, sc1), dres, f0, Vec(late, 0, g2), "norm_gate_bwd_2")
    dh1, gf0 = ffn_bwd(df0, h1, ffn0, w, 0)
    late = mods + hooks.gradients("ffn0", gf0, dh1)
    dres, dsh2a, dsc2a, dn2a, dy0, dg1a = norm_gate_bwd(
        x1, dh1, Vec(n2, 0, 0), Vec(late, 0, sc2), dres, y0, Vec(late, 0, g1), "norm_gate_bwd_1")
    dh0, gm0 = mixer0_bwd(dy0, h0, mix0, w)
    late = mods + hooks.gradients("mix0", gm0, dh0)
    grad_x, dsh1a, dsc1a, dn1a = norm_bwd(x, dh0, Vec(n1, 0, 0), Vec(late, 0, sc1), dres, "norm_bwd_0")

    dmod = jnp.concatenate([jnp.concatenate([dsh1a, dsc1a, dg1a, dsh2a, dsc2a, dg2a], axis=1),
                            jnp.concatenate([dsh1b, dsc1b, dg1b, dsh2b, dsc2b, dg2b], axis=1)], axis=0)
    norms = dict(norm1_g=jnp.concatenate([dn1a, dn1b], axis=0), norm2_g=jnp.concatenate([dn2a, dn2b], axis=0),
                 final_norm_g=d_final)
    return loss, grad_x, dmod, dict(mix0=gm0, ffn0=gf0, mix1=gm1, ffn1=gf1, norms=norms)


def merge_grads(by_stage):
    grads = {**by_stage["mix0"], **by_stage["mix1"], **by_stage["norms"]}
    for k in ("ffn_w_down", "ffn_w_up"):
        grads[k] = [by_stage["ffn0"][k], by_stage["ffn1"][k]]
    grads["ffn_conv_w"] = jnp.stack([by_stage["ffn0"]["ffn_conv_w"], by_stage["ffn1"]["ffn_conv_w"]])
    return grads


_WEIGHTS = ("ada_w", "ada_b", "norm1_g", "norm2_g", "ab_w_in", "a_conv_w", "b_mix_w", "b_scale", "ab_w_out", "cd_w_in",
            "c_q_norm_g", "c_w_uq", "c_kv_norm_g", "c_w_ukv", "d_ln_g", "d_ln_b", "d_w_s", "d_b_s", "cd_w_out",
            "ffn_w_up", "ffn_conv_w", "ffn_w_down", "final_norm_g")
_INPUTS = ("x", "c", "positions") + _WEIGHTS + ("loss_target",) + tuple("m_" + n for n in _WEIGHTS) + tuple(
    "v_" + n for n in _WEIGHTS)

def _pack_rows(parts, rows, dtype):
    flat = jnp.concatenate([p.reshape(-1).astype(dtype) for p in parts])
    return jnp.pad(flat, (0, rows * LANES - flat.shape[0])).reshape(rows, LANES)


def _rows_major(w):
    r, c = w.shape
    return w.reshape(N_CHIPS, r // N_CHIPS, c)


def start_gather(shards, tag, after=()):
    lands = [_sds((N_CHIPS,) + s.shape, s.dtype) for s in shards]
    return split_start("gather_start_" + tag, GATHER, shards, lands, after)


def finish_gather(handle, chip, tag, after):
    shards, lands = split_wait("gather_wait_" + tag, GATHER, handle, after)
    lands = forward_halves(lands, "gather_forward_" + tag)
    return [lax.dynamic_update_index_in_dim(o, s, chip, 0) for o, s in zip(lands, shards)]


def start_reduce(gs, core, tag):
    recv = swap_halves(gs, "swap_halves_" + tag)
    pairs = pair_sums(gs, recv, core, "pair_sums_" + tag)
    lands = [_sds((N_CHIPS - 1,) + p.shape[1:], p.dtype) for p in pairs]
    return split_start("exchange_start_" + tag, EXCHANGE, pairs, lands)


def finish_reduce(handle, chip, core, tag, after):
    pairs, others = split_wait("exchange_wait_" + tag, EXCHANGE, handle, after)
    halves = chip_sums(pairs, others, chip, core, "chip_sums_" + tag)
    full = join_halves(halves, "join_halves_" + tag)
    return [f.reshape(f.shape[1] * 2, f.shape[2]) for f in full]


_SMALL_SHARDED = (("a_conv_w", (3, 128), 1), ("c_q_norm_g", (1, 64), 1), ("d_ln_g", (1, 128), 1), ("d_ln_b", (1, 128), 1),
                  ("ffn_conv_w", (2, 3, 2 * D_FF // N_CHIPS), 2))
_SMALL_GRADS = (("norm1_g", (2, D_MODEL)), ("norm2_g", (2, D_MODEL)), ("b_mix_w", (4, 128, 128)), ("b_scale", (1, 512)),
                ("c_kv_norm_g", (1, 128)), ("d_w_s", (4, 128, 128)), ("d_b_s", (4, 128)), ("final_norm_g", (1, D_MODEL)),
                ("a_conv_w", (3, 512)), ("c_q_norm_g", (1, 256)), ("d_ln_g", (1, 512)), ("d_ln_b", (1, 512)),
                ("ffn_conv_w", (2, 3, 2 * D_FF)))


def _size(shape):
    n = 1
    for d in shape:
        n *= d
    return n


def kernel(x, c, positions, ada_w, ada_b, norm1_g, norm2_g, ab_w_in, a_conv_w, b_mix_w, b_scale, ab_w_out, cd_w_in, c_q_norm_g, c_w_uq, c_kv_norm_g, c_w_ukv, d_ln_g, d_ln_b, d_w_s, d_b_s, cd_w_out, ffn_w_up, ffn_conv_w, ffn_w_down, final_norm_g, loss_target, m_ada_w, m_ada_b, m_norm1_g, m_norm2_g, m_ab_w_in, m_a_conv_w, m_b_mix_w, m_b_scale, m_ab_w_out, m_cd_w_in, m_c_q_norm_g, m_c_w_uq, m_c_kv_norm_g, m_c_w_ukv, m_d_ln_g, m_d_ln_b, m_d_w_s, m_d_b_s, m_cd_w_out, m_ffn_w_up, m_ffn_conv_w, m_ffn_w_down, m_final_norm_g, v_ada_w, v_ada_b, v_norm1_g, v_norm2_g, v_ab_w_in, v_a_conv_w, v_b_mix_w, v_b_scale, v_ab_w_out, v_cd_w_in, v_c_q_norm_g, v_c_w_uq, v_c_kv_norm_g, v_c_w_ukv, v_d_ln_g, v_d_ln_b, v_d_w_s, v_d_b_s, v_cd_w_out, v_ffn_w_up, v_ffn_conv_w, v_ffn_w_down, v_final_norm_g):
    args = (x, c, positions, ada_w, ada_b, norm1_g, norm2_g, ab_w_in, a_conv_w, b_mix_w, b_scale, ab_w_out, cd_w_in, c_q_norm_g, c_w_uq, c_kv_norm_g, c_w_ukv, d_ln_g, d_ln_b, d_w_s, d_b_s, cd_w_out, ffn_w_up, ffn_conv_w, ffn_w_down, final_norm_g, loss_target, m_ada_w, m_ada_b, m_norm1_g, m_norm2_g, m_ab_w_in, m_a_conv_w, m_b_mix_w, m_b_scale, m_ab_w_out, m_cd_w_in, m_c_q_norm_g, m_c_w_uq, m_c_kv_norm_g, m_c_w_ukv, m_d_ln_g, m_d_ln_b, m_d_w_s, m_d_b_s, m_cd_w_out, m_ffn_w_up, m_ffn_conv_w, m_ffn_w_down, m_final_norm_g, v_ada_w, v_ada_b, v_norm1_g, v_norm2_g, v_ab_w_in, v_a_conv_w, v_b_mix_w, v_b_scale, v_ab_w_out, v_cd_w_in, v_c_q_norm_g, v_c_w_uq, v_c_kv_norm_g, v_c_w_ukv, v_d_ln_g, v_d_ln_b, v_d_w_s, v_d_b_s, v_cd_w_out, v_ffn_w_up, v_ffn_conv_w, v_ffn_w_down, v_final_norm_g)
    a = dict(zip(_INPUTS, args, strict=True))
    xi, yi, ci = _place()
    chip = 2 * xi + yi
    dev = 4 * xi + 2 * yi + ci
    x = a["x"][0]
    tgt = a["loss_target"][0]

    bf = lambda t: t.astype(BF16)
    mix0_handle, tok = start_gather([bf(a["ab_w_in"][0]), bf(a["ab_w_out"][0])], "mix0")
    up0_16, down0_16, up1_16, down1_16 = [bf(a[n][l]) for l in (0, 1) for n in ("ffn_w_up", "ffn_w_down")]
    mix1_16 = [bf(a[n][0]) for n in ("cd_w_in", "c_w_uq", "c_w_ukv", "cd_w_out")]

    small_parts = [a["c"] + tok] + [a[n] for n, _, _ in _SMALL_SHARDED]
    rows1 = -(-sum(p.size for p in small_parts) // LANES // 8) * 8
    g1 = all_gather8(_pack_rows(small_parts, rows1, F32), "gather_small",
                     [up0_16, down0_16, up1_16, down1_16, mix1_16[0], mix1_16[3]]).reshape(N_DEV, rows1 * LANES)
    c_all = g1[:, :D_MODEL]
    per_chip = g1[0::2]
    small_full = {}
    off = D_MODEL
    for n, shp, axis in _SMALL_SHARDED:
        piece = per_chip[:, off:off + _size(shp)].reshape((N_CHIPS,) + shp)
        small_full[n] = jnp.concatenate([piece[k] for k in range(N_CHIPS)], axis=axis)
        off += _size(shp)

    merge = lambda t: t.reshape(t.shape[0] * t.shape[1], t.shape[2])
    w = dict(norm1_g=a["norm1_g"], norm2_g=a["norm2_g"], b_mix_w=a["b_mix_w"][0], b_scale=a["b_scale"],
             c_kv_norm_g=a["c_kv_norm_g"], d_w_s=a["d_w_s"][0], d_b_s=a["d_b_s"][0],
             final_norm_g=a["final_norm_g"].reshape(1, D_MODEL), **small_full)

    ncol = N_MOD * D_MODEL // N_CHIPS
    ada_b_mine = lax.dynamic_slice_in_dim(a["ada_b"], chip * ncol, ncol, axis=1)
    mod_cols = ada_mod(c_all, a["ada_w"], ada_b_mine)
    g2_rows = all_gather8(mod_cols.reshape(-1, LANES), "gather_mod")
    g2 = g2_rows.reshape(N_DEV, 2, N_DEV, ncol)
    mod = lax.dynamic_index_in_dim(g2[0::2], dev, axis=2, keepdims=False)
    mod = mod.transpose(1, 0, 2).reshape(2, N_MOD * D_MODEL)

    late = [g2_rows]
    up0_handle, tok_a = start_gather([up0_16], "up0", late)
    down0_handle, tok_b = start_gather([down0_16], "down0", late)
    mix1_handle, tok_c = start_gather(mix1_16, "mix1", late)
    ffn1_handle, tok_d = start_gather([up1_16, down1_16], "ffn1", late)
    mod = mod + (tok_a + tok_b + tok_c + tok_d)

    ropes = rope_tables(a["positions"][0])
    cm16 = lambda t: chip_major(t).astype(BF16)
    w.update(ffn_w_up=[None, None], ffn_w_down=[None, None])
    handles = dict(mix0=mix0_handle, up0=up0_handle, down0=down0_handle, mix1=mix1_handle, ffn1=ffn1_handle)
    reducing, reduced = {}, {}

    class Hooks(StepHooks):
        def weights(self, stage, after):
            got = finish_gather(handles[stage], chip, stage, after)
            if stage == "mix0":
                w.update(ab_w_in=got[0], ab_w_out=merge(got[1]))
            elif stage == "up0":
                w["ffn_w_up"][0] = got[0]
            elif stage == "down0":
                w["ffn_w_down"][0] = merge(got[0])
            elif stage == "mix1":
                cd_in, uq, ukv, cd_out = got
                w.update(prepare_weights(dict(cd_w_in=from_chip_major(cd_in), c_w_uq=from_chip_major(uq),
                                              c_w_ukv=from_chip_major(ukv), cd_w_out=merge(cd_out))))
            else:
                w["ffn_w_up"][1], w["ffn_w_down"][1] = got[0], merge(got[1])

        def gradients(self, stage, grads, after):
            if stage in ("ffn0", "ffn1"):
                parts = [grads["ffn_w_up"], _rows_major(grads["ffn_w_down"])]
            elif stage == "mix1":
                grads.update(unprepare_grads(grads))
                parts = [cm16(grads["cd_w_in"]), cm16(grads["c_w_uq"]), cm16(grads["c_w_ukv"]),
                         _rows_major(grads["cd_w_out"]).astype(BF16)]
            else:
                parts = [grads["ab_w_in"], _rows_major(grads["ab_w_out"])]
            reducing[stage], tok = start_reduce(parts, ci, stage)
            before = {"mix1": "ffn1", "ffn0": "mix1", "mix0": "ffn0"}.get(stage)
            if before is not None:
                reduced[before] = finish_reduce(reducing[before], chip, ci, before, after)
            return tok

    loss, grad_x, dmod, by_stage = run_step(x, tgt, mod, ropes, w, Hooks())
    grads = merge_grads(by_stage)

    parts3 = [dmod] + [grads[n] for n, _ in _SMALL_GRADS] + [loss[0, 0]]
    rows3 = -(-sum(p.size for p in parts3) // LANES // 8) * 8
    small_handle, _ = split_start("small_grads_start", EVERYONE, [_pack_rows(parts3, rows3, F32)],
                                  [_sds((N_DEV, rows3, LANES))])
    red_up1, red_down1 = reduced["ffn1"]
    red_cd_in, red_uq, red_ukv, red_cd_out = reduced["mix1"]
    red_up0, red_down0 = reduced["ffn0"]
    out_grads = dict(cd_w_in=red_cd_in, c_w_uq=red_uq, c_w_ukv=red_ukv, cd_w_out=red_cd_out)
    per_layer = dict(ffn_w_up=(red_up0, red_up1), ffn_w_down=(red_down0, red_down1))
    updates = {}

    def update(n):
        if n in per_layer:
            updates[n] = adamw_layers(a[n], *per_layer[n], a["m_" + n], a["v_" + n], "adamw_" + n)
        else:
            updates[n] = adamw(a[n], out_grads[n].reshape(a[n].shape), a["m_" + n], a["v_" + n], "adamw_" + n)

    early =("ffn_w_up", "ffn_w_down", "cd_w_in", "c_w_uq", "c_w_ukv", "cd_w_out")
    for n in early:
        update(n)
    (mine,), (landed,) = split_wait("small_grads_wait", EVERYONE, small_handle, [updates[n][1] for n in early])
    g3 = lax.dynamic_update_index_in_dim(landed, mine, dev, 0)
    summed = sum8(g3).reshape(-1)
    nmod = 2 * N_MOD * D_MODEL
    out_grads["ada_b"] = summed[:nmod].reshape(2, N_MOD * D_MODEL)
    off = nmod
    for n, shp in _SMALL_GRADS:
        out_grads[n] = summed[off:off + _size(shp)].reshape(shp)
        off += _size(shp)
    loss = summed[off]
    for n, shp, axis in _SMALL_SHARDED:
        width = out_grads[n].shape[-1] // N_CHIPS
        out_grads[n] = lax.dynamic_slice_in_dim(out_grads[n], chip * width, width, axis=out_grads[n].ndim - 1)
    dmod_all = g3.reshape(N_DEV, rows3 * LANES)[:, :nmod].reshape(N_DEV, 2, N_MOD * D_MODEL)
    dmod_mine = lax.dynamic_slice_in_dim(dmod_all, chip * ncol, ncol, axis=2).transpose(1, 0, 2)
    updates["ada_w"] = adamw_ada(a["ada_w"], c_all, dmod_mine, a["m_ada_w"], a["v_ada_w"])

    red_in0, red_out0 = finish_reduce(reducing["mix0"], chip, ci, "mix0", updates["ada_w"][1])
    out_grads.update(ab_w_in=red_in0, ab_w_out=red_out0)

    for n in ("ab_w_in", "ab_w_out"):
        update(n)
    small = [n for n in _WEIGHTS if n not in updates]
    for n, res in zip(small, adamw_small([a[n] for n in small], [out_grads[n].reshape(a[n].shape) for n in small],
                                         [a["m_" + n] for n in small], [a["v_" + n] for n in small])):
        updates[n] = res
    return (loss, grad_x[None], *[updates[n][i] for i in range(4) for n in _WEIGHTS])
```

```python
import functools
from typing import NamedTuple

import jax
import jax.numpy as jnp
from jax import lax
from jax.experimental import pallas as pl
from jax.experimental.pallas import tpu as pltpu

F32 = jnp.float32
BF16 = jnp.bfloat16
EPS = 1e-6
D_MODEL = 1024
N_MOD = 6
A_WIDTH = 512
B_GROUPS = 4
C_HEADS = 8
C_NOPE = 64
C_ROPE = 32
C_V = 64
C_Q_RANK = 256
C_KV_RANK = 128
HEAD_PAD = 128
ROPE_THETA = 10000.0
D_GROUPS = 4
D_CHUNK = 128
D_FF = 2816
FF_UNIT = 128
ADAM_LR = 0.001
ADAM_B1 = 0.9
ADAM_B2 = 0.999
ADAM_EPS = 1e-08
ADAM_WD = 0.01
ADAM_STEP = 10
N_CHIPS = 4
N_DEV = 8
LANES = 128
VMEM_BIG = 56 * 1024 * 1024
MESH = pl.DeviceIdType.MESH


def _sds(shape, dtype=F32):
    return jax.ShapeDtypeStruct(tuple(shape), dtype)


def _tile(n, cap, mult=128):
    if n <= cap:
        return n
    best = None
    for t in range(mult, cap + 1, mult):
        if n % t == 0:
            best = t
    assert best is not None, (n, cap, mult)
    return best


def _params(dims=None, vmem=None):
    return pltpu.CompilerParams(dimension_semantics=dims, vmem_limit_bytes=vmem)


def _shift_down(v, k):
    r = pltpu.roll(v, k, axis=0)
    t = lax.broadcasted_iota(jnp.int32, v.shape, 0)
    return jnp.where(t >= k, r, 0.0)


def _shift_up(v, k):
    n = v.shape[0]
    r = pltpu.roll(v, n - k, axis=0)
    t = lax.broadcasted_iota(jnp.int32, v.shape, 0)
    return jnp.where(t < n - k, r, 0.0)


def _sigmoid(v):
    return 1.0 / (1.0 + jnp.exp(-v))


_GELU_C = 0.7978845608028654
_GELU_A = 0.044715


def _gelu(v):
    return 0.5 * v * (1.0 + jnp.tanh(_GELU_C * (v + _GELU_A * v * v * v)))


def _gelu_grad(v):
    th = jnp.tanh(_GELU_C * (v + _GELU_A * v * v * v))
    return 0.5 * (1.0 + th) + 0.5 * v * (1.0 - th * th) * _GELU_C * (1.0 + 3.0 * _GELU_A * v * v)


_NN = (((1,), (0,)), ((), ()))
_NT = (((1,), (1,)), ((), ()))
_TN = (((0,), (0,)), ((), ()))


def _dot(a, b, dims=_NN):
    return lax.dot_general(a, b, dims, preferred_element_type=F32)


def _logical(t, groups):
    return (t.shape[-2], t.shape[-1] * groups)


def _block(tr, tc, groups, cols, where):
    if groups == 1:
        return pl.BlockSpec((tr, tc), where)
    per = cols // groups // tc

    def index(i, j, s):
        r, c = where(i, j, s)
        return (c // per, r, c % per)

    return pl.BlockSpec((None, tr, tc), index)


def matmul(a, b, mode, out_dtype, name, ga=1, gb=1, go=1, tm=None, tn=None, tk=None):
    (ar, ac), (br, bc) = _logical(a, ga), _logical(b, gb)
    if mode == "nn":
        m, k, n = ar, ac, bc
        a_col, b_col = "k", "n"
    elif mode == "nt":
        m, k, n = ar, ac, br
        a_col, b_col = "k", "k"
    else:
        k, m, n = ar, ac, bc
        a_col, b_col = "m", "n"
    limit = {"m": m, "n": n // go, "k": k}
    limit[a_col] = min(limit[a_col], ac // ga)
    limit[b_col] = min(limit[b_col], bc // gb)
    tm = tm or _tile(limit["m"], 2048, 128 if mode == "tn" else 16)
    tn = tn or _tile(limit["n"], 512)
    tk = tk or _tile(limit["k"], 2048, 16 if mode == "tn" else 128)
    nk = k // tk
    if mode == "nn":
        a_spec = _block(tm, tk, ga, ac, lambda i, j, s: (i, s))
        b_spec = _block(tk, tn, gb, bc, lambda i, j, s: (s, j))
        dims = _NN
    elif mode == "nt":
        a_spec = _block(tm, tk, ga, ac, lambda i, j, s: (i, s))
        b_spec = _block(tn, tk, gb, bc, lambda i, j, s: (j, s))
        dims = _NT
    else:
        a_spec = _block(tk, tm, ga, ac, lambda i, j, s: (s, i))
        b_spec = _block(tk, tn, gb, bc, lambda i, j, s: (s, j))
        dims = _TN
    o_spec = _block(tm, tn, go, n, lambda i, j, s: (i, j))
    out_shape = _sds((m, n), out_dtype) if go == 1 else _sds((go, m, n // go), out_dtype)

    def body(a_ref, b_ref, o_ref, acc_ref):
        s = pl.program_id(2)

        @pl.when(s == 0)
        def _():
            acc_ref[...] = jnp.zeros_like(acc_ref)

        acc_ref[...] += _dot(a_ref[...], b_ref[...], dims)

        @pl.when(s == nk - 1)
        def _():
            o_ref[...] = acc_ref[...].astype(o_ref.dtype)

    return pl.pallas_call(
        body, name=name, out_shape=out_shape, grid=(m // tm, n // tn, nk),
        in_specs=[a_spec, b_spec], out_specs=o_spec,
        scratch_shapes=[pltpu.VMEM((tm, tn), F32)],
        compiler_params=_params(("parallel", "parallel", "arbitrary"), VMEM_BIG),
    )(a, b)


def _rows(tm, n):
    return pl.BlockSpec((tm, n), lambda i: (i, 0))


def _vec(n):
    return pl.BlockSpec((1, n), lambda i: (0, 0))


class Vec(NamedTuple):
    array: jax.Array
    row: int
    col: int


def _vec_in(v, d):
    return pl.BlockSpec((None, 1, d), lambda i: (v.row, 0, v.col))


def modnorm_fwd(x, g, sc, sh, name):
    s, d = x.shape
    tm = _tile(s, 256, 8)

    def body(x_ref, g_ref, sc_ref, sh_ref, o_ref):
        xv = x_ref[...]
        r = lax.rsqrt(jnp.mean(xv * xv, axis=-1, keepdims=True) + EPS)
        o_ref[...] = ((xv * r) * g_ref[...] * (1.0 + sc_ref[...]) + sh_ref[...]).astype(BF16)

    return pl.pallas_call(
        body, name=name, out_shape=_sds((s, d), BF16), grid=(s // tm,),
        in_specs=[_rows(tm, d), _vec_in(g, d), _vec_in(sc, d), _vec_in(sh, d)], out_specs=_rows(tm, d),
        compiler_params=_params(("parallel",)),
    )(x, g.array, sc.array, sh.array)


def norm_bwd(x, dh, g, sc, dres, name):
    s, d = x.shape
    tm = _tile(s, 256, 8)
    nsteps = s // tm

    def body(x_ref, dh_ref, g_ref, sc_ref, dr_ref, dx_ref, dsh_ref, dsc_ref, dg_ref, a2_ref):
        i = pl.program_id(0)

        @pl.when(i == 0)
        def _():
            dsh_ref[...] = jnp.zeros_like(dsh_ref)
            a2_ref[...] = jnp.zeros_like(a2_ref)

        xv = x_ref[...]
        dh = dh_ref[...].astype(F32)
        r = lax.rsqrt(jnp.mean(xv * xv, axis=-1, keepdims=True) + EPS)
        xh = xv * r
        dsh_ref[...] += jnp.sum(dh, axis=0, keepdims=True)
        a2_ref[...] += jnp.sum(dh * xh, axis=0, keepdims=True)
        dxh = dh * (g_ref[...] * (1.0 + sc_ref[...]))
        dx = r * (dxh - xh * jnp.mean(dxh * xh, axis=-1, keepdims=True))
        dx_ref[...] = dr_ref[...] + dx

        @pl.when(i == nsteps - 1)
        def _():
            dsc_ref[...] = a2_ref[...] * g_ref[...]
            dg_ref[...] = a2_ref[...] * (1.0 + sc_ref[...])

    return pl.pallas_call(
        body, name=name, out_shape=(_sds((s, d)), _sds((1, d)), _sds((1, d)), _sds((1, d))), grid=(nsteps,),
        in_specs=[_rows(tm, d), _rows(tm, d), _vec_in(g, d), _vec_in(sc, d), _rows(tm, d)],
        out_specs=(_rows(tm, d), _vec(d), _vec(d), _vec(d)),
        scratch_shapes=[pltpu.VMEM((1, d), F32)],
        compiler_params=_params(("arbitrary",)),
    )(x, dh, g.array, sc.array, dres)


def resid_modnorm_fwd(x, y, gate, g, sc, sh, name):
    s, d = x.shape
    tm = _tile(s, 256, 8)

    def body(x_ref, y_ref, gate_ref, g_ref, sc_ref, sh_ref, xo_ref, h_ref):
        xv = x_ref[...] + gate_ref[...] * y_ref[...].astype(F32)
        xo_ref[...] = xv
        r = lax.rsqrt(jnp.mean(xv * xv, axis=-1, keepdims=True) + EPS)
        h_ref[...] = ((xv * r) * g_ref[...] * (1.0 + sc_ref[...]) + sh_ref[...]).astype(BF16)

    return pl.pallas_call(
        body, name=name, out_shape=(_sds((s, d)), _sds((s, d), BF16)), grid=(s // tm,),
        in_specs=[_rows(tm, d), _rows(tm, d), _vec_in(gate, d), _vec_in(g, d), _vec_in(sc, d), _vec_in(sh, d)],
        out_specs=(_rows(tm, d), _rows(tm, d)),
        compiler_params=_params(("parallel",)),
    )(x, y, gate.array, g.array, sc.array, sh.array)


def norm_gate_bwd(x, dh, g, sc, dres, y, gate, name):
    s, d = x.shape
    tm = _tile(s, 256, 8)
    nsteps = s // tm

    def body(x_ref, dh_ref, g_ref, sc_ref, dr_ref, y_ref, gate_ref, dx_ref, dsh_ref, dsc_ref, dg_ref, dy_ref,
             dgate_ref, a2_ref):
        i = pl.program_id(0)

        @pl.when(i == 0)
        def _():
            dsh_ref[...] = jnp.zeros_like(dsh_ref)
            a2_ref[...] = jnp.zeros_like(a2_ref)
            dgate_ref[...] = jnp.zeros_like(dgate_ref)

        xv = x_ref[...]
        dh = dh_ref[...].astype(F32)
        r = lax.rsqrt(jnp.mean(xv * xv, axis=-1, keepdims=True) + EPS)
        xh = xv * r
        dsh_ref[...] += jnp.sum(dh, axis=0, keepdims=True)
        a2_ref[...] += jnp.sum(dh * xh, axis=0, keepdims=True)
        dxh = dh * (g_ref[...] * (1.0 + sc_ref[...]))
        dr = dr_ref[...] + r * (dxh - xh * jnp.mean(dxh * xh, axis=-1, keepdims=True))
        dx_ref[...] = dr
        dy_ref[...] = (dr * gate_ref[...]).astype(BF16)
        dgate_ref[...] += jnp.sum(dr * y_ref[...].astype(F32), axis=0, keepdims=True)

        @pl.when(i == nsteps - 1)
        def _():
            dsc_ref[...] = a2_ref[...] * g_ref[...]
            dg_ref[...] = a2_ref[...] * (1.0 + sc_ref[...])

    vec = _sds((1, d))
    return pl.pallas_call(
        body, name=name, out_shape=(_sds((s, d)), vec, vec, vec, _sds((s, d), BF16), vec), grid=(nsteps,),
        in_specs=[_rows(tm, d), _rows(tm, d), _vec_in(g, d), _vec_in(sc, d), _rows(tm, d), _rows(tm, d), _vec_in(gate, d)],
        out_specs=(_rows(tm, d), _vec(d), _vec(d), _vec(d), _rows(tm, d), _vec(d)),
        scratch_shapes=[pltpu.VMEM((1, d), F32)],
        compiler_params=_params(("arbitrary",)),
    )(x, dh, g.array, sc.array, dres, y, gate.array)


def final_fused(x, f, gate, g, tgt):
    s, d = x.shape
    tm = _tile(s, 256, 8)

    def body(x_ref, f_ref, gate_ref, g_ref, t_ref, dx_ref, dg_ref, loss_ref, df_ref, dgate_ref):
        @pl.when(pl.program_id(0) == 0)
        def _():
            dg_ref[...] = jnp.zeros_like(dg_ref)
            loss_ref[...] = jnp.zeros_like(loss_ref)
            dgate_ref[...] = jnp.zeros_like(dgate_ref)

        fv, gatev, gv = f_ref[...].astype(F32), gate_ref[...], g_ref[...]
        xv = x_ref[...] + gatev * fv
        r = lax.rsqrt(jnp.mean(xv * xv, axis=-1, keepdims=True) + EPS)
        xh = xv * r
        e = xh * gv - t_ref[...]
        row = jnp.sum(e * e, axis=-1, keepdims=True) * (0.5 / d)
        loss_ref[...] += jnp.sum(row, axis=0, keepdims=True)
        dy = e * (1.0 / d)
        dg_ref[...] += jnp.sum(dy * xh, axis=0, keepdims=True)
        dxh = dy * gv
        dx = r * (dxh - xh * jnp.mean(dxh * xh, axis=-1, keepdims=True))
        dx_ref[...] = dx
        df_ref[...] = (dx * gatev).astype(BF16)
        dgate_ref[...] += jnp.sum(dx * fv, axis=0, keepdims=True)

    vec = _sds((1, d))
    return pl.pallas_call(
        body, name="final_fused", out_shape=(_sds((s, d)), vec, _sds((1, LANES)), _sds((s, d), BF16), vec),
        grid=(s // tm,),
        in_specs=[_rows(tm, d), _rows(tm, d), _vec_in(gate, d), _vec_in(g, d), _rows(tm, d)],
        out_specs=(_rows(tm, d), _vec(d), _vec(LANES), _rows(tm, d), _vec(d)),
        compiler_params=_params(("arbitrary",)),
    )(x, f, gate.array, g.array, tgt)


def _taps(v):
    return _shift_down(v, 2), _shift_down(v, 1), v


def _conv3_taps(taps, w):
    return w[0:1, :] * taps[0] + w[1:2, :] * taps[1] + w[2:3, :] * taps[2]


def _conv3(v, w):
    return _conv3_taps(_taps(v), w)


def _conv3_t(dv, w):
    return w[0:1, :] * _shift_up(dv, 2) + w[1:2, :] * _shift_up(dv, 1) + w[2:3, :] * dv


def _conv3_dw_taps(dv, taps):
    return jnp.concatenate([jnp.sum(dv * t, axis=0, keepdims=True) for t in taps], axis=0)


def _conv3_dw(dv, v):
    return _conv3_dw_taps(dv, _taps(v))


def gconv_fwd(z, conv_w):
    s = z.shape[0]
    nb = A_WIDTH // LANES

    def body(b_ref, c_ref, a_ref, w_ref, o_ref):
        b, c, a = b_ref[...].astype(F32), c_ref[...].astype(F32), a_ref[...].astype(F32)
        o_ref[...] = (b * _conv3(c * a, w_ref[...])).astype(BF16)

    col = lambda off: pl.BlockSpec((s, LANES), lambda j: (0, off + j))
    return pl.pallas_call(
        body, name="gconv_fwd", out_shape=_sds((s, A_WIDTH + _B_WIDTH), BF16), grid=(nb,),
        in_specs=[col(0), col(nb), col(2 * nb), pl.BlockSpec((3, LANES), lambda j: (0, j))],
        out_specs=pl.BlockSpec((s, LANES), lambda j: (0, j)),
        compiler_params=_params(("parallel",), VMEM_BIG),
    )(z, z, z, conv_w)


def gconv_bwd(z, conv_w, dycat):
    s = z.shape[0]
    nb = A_WIDTH // LANES

    def body(b_ref, c_ref, a_ref, w_ref, dy_ref, db_ref, dc_ref, da_ref, dw_ref):
        c, a, w, dy = c_ref[...].astype(F32), a_ref[...].astype(F32), w_ref[...], dy_ref[...].astype(F32)
        ca = c * a
        db_ref[...] = (dy * _conv3(ca, w)).astype(BF16)
        dconv = dy * b_ref[...].astype(F32)
        dw_ref[...] = _conv3_dw(dconv, ca)
        dca = _conv3_t(dconv, w)
        dc_ref[...] = (dca * a).astype(BF16)
        da_ref[...] = (dca * c).astype(BF16)

    col = lambda off: pl.BlockSpec((s, LANES), lambda j: (0, off + j))
    wspec = pl.BlockSpec((3, LANES), lambda j: (0, j))
    part = _sds((s, A_WIDTH), BF16)
    return pl.pallas_call(
        body, name="gconv_bwd", out_shape=(part, part, part, _sds((3, A_WIDTH))), grid=(nb,),
        in_specs=[col(0), col(nb), col(2 * nb), wspec, col(0)],
        out_specs=(col(0), col(0), col(0), wspec),
        compiler_params=_params(("parallel",), VMEM_BIG),
    )(z, z, z, conv_w, dycat)


def _pool_counts(s, w):
    t = lax.broadcasted_iota(jnp.int32, (s, 1), 0)
    return jnp.minimum(t + 1, w).astype(F32)


def _pooled(p, levels):
    acc = p
    for lv in range(levels):
        acc = acc + _shift_down(acc, 2 ** lv)
    return acc / _pool_counts(p.shape[0], 2 ** levels) - p


_B_WIDTH = B_GROUPS * LANES


def pool_fwd(z, mix_w, scale, ycat):
    s = z.shape[0]

    def body(p_ref, m_ref, sc_ref, ycat_ref, o_ref):
        del ycat_ref
        for g in range(B_GROUPS):
            cols = slice(g * LANES, (g + 1) * LANES)
            pooled = _pooled(p_ref[:, cols].astype(F32), g + 1)
            y = _dot(pooled.astype(BF16), m_ref[g].astype(BF16))
            o_ref[:, cols] = (y * sc_ref[:, cols]).astype(BF16)

    return pl.pallas_call(
        body, name="pool_fwd", out_shape=_sds(ycat.shape, BF16), grid=(1,),
        in_specs=[pl.BlockSpec((s, _B_WIDTH), lambda i: (0, 3 * A_WIDTH // _B_WIDTH)),
                  pl.BlockSpec((B_GROUPS, LANES, LANES), lambda i: (0, 0, 0)), pl.BlockSpec((1, _B_WIDTH), lambda i: (0, 0)),
                  pl.BlockSpec(memory_space=pl.ANY)],
        out_specs=pl.BlockSpec((s, _B_WIDTH), lambda i: (0, A_WIDTH // _B_WIDTH)),
        input_output_aliases={3: 0},
        compiler_params=_params(("arbitrary",), VMEM_BIG),
    )(z, mix_w, scale, ycat)


def pool_bwd(z, mix_w, scale, dycat):
    s = z.shape[0]

    def body(p_ref, m_ref, sc_ref, dy_ref, dp_ref, dm_ref, dsc_ref):
        for g in range(B_GROUPS):
            cols = slice(g * LANES, (g + 1) * LANES)
            pooled = _pooled(p_ref[:, cols].astype(F32), g + 1)
            mw = m_ref[g].astype(BF16)
            pb = pooled.astype(BF16)
            dy = dy_ref[:, cols].astype(F32)
            dsc_ref[:, cols] = jnp.sum(dy * _dot(pb, mw), axis=0, keepdims=True)
            dmix = (dy * sc_ref[:, cols]).astype(BF16)
            dm_ref[g] = _dot(pb, dmix, _TN)
            dpool = _dot(dmix, mw, _NT)
            acc = dpool / _pool_counts(s, 2 ** (g + 1))
            for lv in range(g + 1):
                acc = acc + _shift_up(acc, 2 ** lv)
            dp_ref[:, cols] = (acc - dpool).astype(BF16)

    wide = lambda c: pl.BlockSpec((s, _B_WIDTH), lambda i: (0, c))
    mspec = pl.BlockSpec((B_GROUPS, LANES, LANES), lambda i: (0, 0, 0))
    vspec = pl.BlockSpec((1, _B_WIDTH), lambda i: (0, 0))
    return pl.pallas_call(
        body, name="pool_bwd", out_shape=(_sds((s, _B_WIDTH), BF16), _sds((B_GROUPS, LANES, LANES)), _sds((1, _B_WIDTH))),
        grid=(1,), in_specs=[wide(3 * A_WIDTH // _B_WIDTH), mspec, vspec, wide(A_WIDTH // _B_WIDTH)],
        out_specs=(wide(0), mspec, vspec),
        compiler_params=_params(("arbitrary",), VMEM_BIG),
    )(z, mix_w, scale, dycat)


_FF_BLOCKS = D_FF // FF_UNIT


def _ff_spec(s):
    return pl.BlockSpec((2, s, FF_UNIT), lambda j: (0, 0, j))


def _ff_wspecs():
    return [pl.BlockSpec((3, FF_UNIT), lambda j: (0, j)), pl.BlockSpec((3, FF_UNIT), lambda j: (0, _FF_BLOCKS + j))]


_FF_ROWS = 64
_FF_HALO = 16


def _chunk_taps(z_ref, half, c):
    start = pl.multiple_of(c * _FF_ROWS, _FF_ROWS)
    before = pl.multiple_of(jnp.maximum(c * _FF_ROWS - _FF_HALO, 0), _FF_HALO)
    halo = z_ref[half, pl.ds(before, _FF_HALO), :].astype(F32)
    halo = jnp.where(c > 0, halo, 0.0)
    win = jnp.concatenate([halo, z_ref[half, pl.ds(start, _FF_ROWS), :].astype(F32)], axis=0)
    return tuple(pltpu.roll(win, k, axis=0)[_FF_HALO:] for k in (2, 1)) + (win[_FF_HALO:],)


def _fold8(v):
    acc = v[0:8]
    for r in range(8, v.shape[0], 8):
        acc = acc + v[r:r + 8]
    return acc


_FF_CHUNK = 256


def ffn_act_down(zf, conv_w, w_down, name):
    s, d = zf.shape[1], w_down.shape[1]
    nk = D_FF // _FF_CHUNK
    chunk = lambda k: jnp.minimum(k, nk - 1)

    def body(z_ref, wg_ref, wu_ref, wd_ref, a_ref, f_ref, held_ref, acc_ref):
        k = pl.program_id(0)

        @pl.when(k == 0)
        def _():
            held_ref[...] = jnp.zeros_like(held_ref)
            acc_ref[...] = jnp.zeros_like(acc_ref)

        acc_ref[...] += _dot(held_ref[(k + 1) % 2], wd_ref[...])
        g = _conv3(z_ref[0].astype(F32), wg_ref[...])
        u = _conv3(z_ref[1].astype(F32), wu_ref[...])
        act = (g * _sigmoid(g) * u).astype(BF16)
        a_ref[...] = act
        held_ref[k % 2] = act

        @pl.when(k == nk)
        def _():
            f_ref[...] = acc_ref[...].astype(BF16)

    return pl.pallas_call(
        body, name=name, out_shape=(_sds((s, D_FF), BF16), _sds((s, d), BF16)), grid=(nk + 1,),
        in_specs=[pl.BlockSpec((2, s, _FF_CHUNK), lambda k: (0, 0, chunk(k))),
                  pl.BlockSpec((3, _FF_CHUNK), lambda k: (0, chunk(k))),
                  pl.BlockSpec((3, _FF_CHUNK), lambda k: (0, nk + chunk(k))),
                  pl.BlockSpec((_FF_CHUNK, d), lambda k: (jnp.maximum(k - 1, 0), 0))],
        out_specs=(pl.BlockSpec((s, _FF_CHUNK), lambda k: (0, chunk(k))), pl.BlockSpec((s, d), lambda k: (0, 0))),
        scratch_shapes=[pltpu.VMEM((2, s, _FF_CHUNK), BF16), pltpu.VMEM((s, d), F32)],
        compiler_params=_params(("arbitrary",), VMEM_BIG),
    )(zf, conv_w, conv_w, w_down)


def ffn_act_bwd(zf, conv_w, da, name):
    s = zf.shape[1]
    assert s % _FF_ROWS == 0
    nchunks = s // _FF_ROWS

    def body(z_ref, wg_ref, wu_ref, da_ref, dz_ref, dw_ref, dg_ref, du_ref):
        wg, wu = wg_ref[...], wu_ref[...]

        def first(c, acc):
            rows = pl.ds(pl.multiple_of(c * _FF_ROWS, _FF_ROWS), _FF_ROWS)
            tg, tu = _chunk_taps(z_ref, 0, c), _chunk_taps(z_ref, 1, c)
            g = _conv3_taps(tg, wg)
            u = _conv3_taps(tu, wu)
            dav = da_ref[rows, :].astype(F32)
            sg = _sigmoid(g)
            dg = dav * u * (sg * (1.0 + g * (1.0 - sg)))
            du = dav * (g * sg)
            dg_ref[rows, :] = dg
            du_ref[rows, :] = du
            return tuple(a + _fold8(d * t) for a, (d, t) in zip(acc, [(dg, t) for t in tg] + [(du, t) for t in tu]))

        zero = jnp.zeros((8, FF_UNIT), F32)
        acc = lax.fori_loop(0, nchunks, first, (zero,) * 6)
        sums = [jnp.sum(a, axis=0, keepdims=True) for a in acc]
        dw_ref[0] = jnp.concatenate(sums[:3], axis=0)
        dw_ref[1] = jnp.concatenate(sums[3:], axis=0)

        tail = pl.ds(s, _FF_HALO)
        dg_ref[tail, :] = jnp.zeros((_FF_HALO, FF_UNIT), F32)
        du_ref[tail, :] = jnp.zeros((_FF_HALO, FF_UNIT), F32)
        span = _FF_ROWS + _FF_HALO

        def second(c, carry):
            start = pl.multiple_of(c * _FF_ROWS, _FF_ROWS)
            for half, (d_ref, w) in enumerate(((dg_ref, wg), (du_ref, wu))):
                win = d_ref[pl.ds(start, span), :]
                dz = (w[0:1, :] * pltpu.roll(win, span - 2, axis=0)[:_FF_ROWS]
                      + w[1:2, :] * pltpu.roll(win, span - 1, axis=0)[:_FF_ROWS] + w[2:3, :] * win[:_FF_ROWS])
                dz_ref[half, pl.ds(start, _FF_ROWS), :] = dz.astype(BF16)
            return carry

        lax.fori_loop(0, nchunks, second, 0)

    return pl.pallas_call(
        body, name=name, out_shape=(_sds((2, s, D_FF), BF16), _sds((2, 3, D_FF))), grid=(_FF_BLOCKS,),
        in_specs=[_ff_spec(s)] + _ff_wspecs() + [pl.BlockSpec((s, FF_UNIT), lambda j: (0, j))],
        out_specs=(_ff_spec(s), pl.BlockSpec((2, 3, FF_UNIT), lambda j: (0, 0, j))),
        scratch_shapes=[pltpu.VMEM((s + _FF_HALO, FF_UNIT), F32), pltpu.VMEM((s + _FF_HALO, FF_UNIT), F32)],
        compiler_params=_params(("parallel",), VMEM_BIG),
    )(zf, conv_w, conv_w, da)


def _rope(v, cs, s1, s2):
    return v * cs + pltpu.roll(v, LANES - C_ROPE // 2, axis=1) * s1 + pltpu.roll(v, C_ROPE // 2, axis=1) * s2


def _rope_t(dv, cs, s1, s2):
    return dv * cs + pltpu.roll(dv * s1, C_ROPE // 2, axis=1) + pltpu.roll(dv * s2, LANES - C_ROPE // 2, axis=1)


def _kpe_mask(shape):
    lane = lax.broadcasted_iota(jnp.int32, shape, 1)
    return (lane >= C_NOPE) & (lane < C_NOPE + C_ROPE)


def _rms(v, g):
    r = lax.rsqrt(jnp.mean(v * v, axis=-1, keepdims=True) + EPS)
    return v * r, r


def _rms_bwd(dn, xh, r, g):
    dxh = dn * g
    return r * (dxh - xh * jnp.mean(dxh * xh, axis=-1, keepdims=True)), jnp.sum(dn * xh, axis=0, keepdims=True)


_ZQ = C_Q_RANK + C_KV_RANK + HEAD_PAD
_HW = C_HEADS * HEAD_PAD


_MLA_ROWS = 256


def _mla_tiles(s):
    tm = _tile(s, 2 * _MLA_ROWS, 8)
    sub = min(tm, _MLA_ROWS)
    return tm, [slice(r * sub, (r + 1) * sub) for r in range(tm // sub)]


def mla_pre_fwd(z, gq, gkv, wq, wk, wv, cs, s1, s2):
    s = z.shape[0]
    tm, streams = _mla_tiles(s)

    def body(z_ref, gq_ref, gkv_ref, wq_ref, wk_ref, wv_ref, cs_ref, s1_ref, s2_ref, q_ref, k_ref, v_ref):
        for rs in streams:
            zv = z_ref[rs, :].astype(F32)
            cst, s1t, s2t = cs_ref[rs, :], s1_ref[rs, :], s2_ref[rs, :]
            qh, _ = _rms(zv[:, :C_Q_RANK], None)
            qn = (qh * gq_ref[...]).astype(BF16)
            q = _dot(qn, wq_ref[...])
            kh, _ = _rms(zv[:, C_Q_RANK:C_Q_RANK + C_KV_RANK], None)
            kvn = (kh * gkv_ref[...]).astype(BF16)
            k = _dot(kvn, wk_ref[...])
            v_ref[rs, :] = _dot(kvn, wv_ref[...]).astype(BF16)
            kpe = _rope(zv[:, C_Q_RANK + C_KV_RANK:], cst, s1t, s2t)
            for h in range(C_HEADS):
                sl = slice(h * HEAD_PAD, (h + 1) * HEAD_PAD)
                q_ref[rs, sl] = _rope(q[:, sl], cst, s1t, s2t).astype(BF16)
                k_ref[rs, sl] = (k[:, sl] + kpe).astype(BF16)

    full = lambda r, c: pl.BlockSpec((r, c), lambda i: (0, 0))
    hw = _sds((s, _HW), BF16)
    return pl.pallas_call(
        body, name="mla_pre_fwd", out_shape=(hw, hw, hw), grid=(s // tm,),
        in_specs=[_rows(tm, _ZQ), _vec(C_Q_RANK), _vec(C_KV_RANK), full(C_Q_RANK, _HW), full(C_KV_RANK, _HW),
                  full(C_KV_RANK, _HW), _rows(tm, LANES), _rows(tm, LANES), _rows(tm, LANES)],
        out_specs=(_rows(tm, _HW), _rows(tm, _HW), _rows(tm, _HW)),
        compiler_params=_params(("parallel",), VMEM_BIG),
    )(z, gq, gkv, wq, wk, wv, cs, s1, s2)


def mla_pre_bwd(z, gq, gkv, wq, wk, wv, cs, s1, s2, dq, dk, dv):
    s = z.shape[0]
    tm, streams = _mla_tiles(s)

    def body(z_ref, gq_ref, gkv_ref, wq_ref, wk_ref, wv_ref, cs_ref, s1_ref, s2_ref, dq_ref, dk_ref, dv_ref,
             dz_ref, dwq_ref, dwk_ref, dwv_ref, dgq_ref, dgkv_ref):
        @pl.when(pl.program_id(0) == 0)
        def _():
            dwq_ref[...] = jnp.zeros_like(dwq_ref)
            dwk_ref[...] = jnp.zeros_like(dwk_ref)
            dwv_ref[...] = jnp.zeros_like(dwv_ref)
            dgq_ref[...] = jnp.zeros_like(dgq_ref)
            dgkv_ref[...] = jnp.zeros_like(dgkv_ref)

        gqv, gkvv = gq_ref[...], gkv_ref[...]
        for rs in streams:
            zv = z_ref[rs, :].astype(F32)
            cst, s1t, s2t = cs_ref[rs, :], s1_ref[rs, :], s2_ref[rs, :]
            qh, rq = _rms(zv[:, :C_Q_RANK], None)
            qn = (qh * gqv).astype(BF16)
            kh, rk = _rms(zv[:, C_Q_RANK:C_Q_RANK + C_KV_RANK], None)
            kvn = (kh * gkvv).astype(BF16)

            dqv = dq_ref[rs, :].astype(F32)
            dqp = jnp.concatenate(
                [_rope_t(dqv[:, h * HEAD_PAD:(h + 1) * HEAD_PAD], cst, s1t, s2t) for h in range(C_HEADS)], axis=1
            ).astype(BF16)
            dwq_ref[...] += _dot(qn, dqp, _TN)
            dqn = _dot(dqp, wq_ref[...], _NT)
            dql, dgq = _rms_bwd(dqn, qh, rq, gqv)
            dgq_ref[...] += dgq

            dkv = dk_ref[rs, :]
            dkb = dkv.astype(BF16)
            dvb = dv_ref[rs, :].astype(BF16)
            dwk_ref[...] += _dot(kvn, dkb, _TN)
            dwv_ref[...] += _dot(kvn, dvb, _TN)
            dkvn = _dot(dkb, wk_ref[...], _NT) + _dot(dvb, wv_ref[...], _NT)
            dkl, dgkv = _rms_bwd(dkvn, kh, rk, gkvv)
            dgkv_ref[...] += dgkv

            dkpe = dkv[:, :HEAD_PAD]
            for h in range(1, C_HEADS):
                dkpe = dkpe + dkv[:, h * HEAD_PAD:(h + 1) * HEAD_PAD]
            dkpe = _rope_t(jnp.where(_kpe_mask(dkpe.shape), dkpe, 0.0), cst, s1t, s2t)
            dz_ref[rs, :] = jnp.concatenate([dql, dkl, dkpe], axis=1).astype(BF16)

    full = lambda r, c: pl.BlockSpec((r, c), lambda i: (0, 0))
    return pl.pallas_call(
        body, name="mla_pre_bwd",
        out_shape=(_sds((s, _ZQ), BF16), _sds((C_Q_RANK, _HW)), _sds((C_KV_RANK, _HW)), _sds((C_KV_RANK, _HW)),
                   _sds((1, C_Q_RANK)), _sds((1, C_KV_RANK))),
        grid=(s // tm,),
        in_specs=[_rows(tm, _ZQ), _vec(C_Q_RANK), _vec(C_KV_RANK), full(C_Q_RANK, _HW), full(C_KV_RANK, _HW),
                  full(C_KV_RANK, _HW), _rows(tm, LANES), _rows(tm, LANES), _rows(tm, LANES),
                  _rows(tm, _HW), _rows(tm, _HW), _rows(tm, _HW)],
        out_specs=(_rows(tm, _ZQ), full(C_Q_RANK, _HW), full(C_KV_RANK, _HW), full(C_KV_RANK, _HW),
                   _vec(C_Q_RANK), _vec(C_KV_RANK)),
        compiler_params=_params(("arbitrary",), VMEM_BIG),
    )(z, gq, gkv, wq, wk, wv, cs, s1, s2, dq, dk, dv)


_ATT_SCALE = (C_NOPE + C_ROPE) ** -0.5
_NEG = -1e30


def _att_exp(q, k, row0, ends_here):
    sc = _dot(q, k, _NT) * _ATT_SCALE
    tq, nk = sc.shape
    if ends_here:
        last = sc[:, nk - tq:]
        row = lax.broadcasted_iota(jnp.int32, last.shape, 0)
        col = lax.broadcasted_iota(jnp.int32, last.shape, 1)
        last = jnp.where(col <= row, last, _NEG)
        sc = last if nk == tq else jnp.concatenate([sc[:, :nk - tq], last], axis=1)
    else:
        qpos = row0 + lax.broadcasted_iota(jnp.int32, sc.shape, 0)
        kpos = lax.broadcasted_iota(jnp.int32, sc.shape, 1)
        sc = jnp.where(kpos <= qpos, sc, _NEG)
    e = jnp.exp(sc - jnp.max(sc, axis=-1, keepdims=True))
    return e, 1.0 / jnp.sum(e, axis=-1, keepdims=True)


def _causal_cases(i, nq, tq, fn):
    if nq > 8:
        fn(nq * tq, False)
        return
    for blk in range(nq):
        pl.when(i == blk)(functools.partial(fn, (blk + 1) * tq, True))


_FWD_HEADS_PER_STEP = 4
_BWD_HEADS_PER_STEP = 2


def _head_lanes(heads):
    return [slice(h * HEAD_PAD, (h + 1) * HEAD_PAD) for h in range(heads)]


def attn_fwd(q, k, v):
    s = q.shape[0]
    tq = _tile(s, 256, 8)
    nq = s // tq
    heads = _FWD_HEADS_PER_STEP
    wide = heads * HEAD_PAD

    def body(q_ref, k_ref, v_ref, o_ref):
        i = pl.program_id(1)

        def case(nk, ends_here):
            for hd in _head_lanes(heads):
                e, inv = _att_exp(q_ref[:, hd], k_ref[:nk, hd], i * tq, ends_here)
                o_ref[:, hd] = (_dot(e.astype(BF16), v_ref[:nk, hd]) * inv).astype(BF16)

        _causal_cases(i, nq, tq, case)

    qspec = pl.BlockSpec((tq, wide), lambda h, i: (i, h))
    kspec = pl.BlockSpec((s, wide), lambda h, i: (0, h))
    return pl.pallas_call(
        body, name="attn_fwd", out_shape=_sds((s, _HW + _DW), BF16), grid=(C_HEADS // heads, s // tq),
        in_specs=[qspec, kspec, kspec], out_specs=qspec,
        compiler_params=_params(("parallel", "parallel"), VMEM_BIG),
    )(q, k, v)


def attn_bwd(q, k, v, o, do_all):
    s = q.shape[0]
    tq = _tile(s, 256, 8)
    heads = _BWD_HEADS_PER_STEP
    wide = heads * HEAD_PAD

    def body(q_ref, k_ref, v_ref, o_ref, do_ref, dq_ref, dk_ref, dv_ref):
        i = pl.program_id(1)

        @pl.when(i == 0)
        def _():
            dk_ref[...] = jnp.zeros_like(dk_ref)
            dv_ref[...] = jnp.zeros_like(dv_ref)

        def case(nk, ends_here):
            for hd in _head_lanes(heads):
                qv, kv, vv, dov = q_ref[:, hd], k_ref[:nk, hd], v_ref[:nk, hd], do_ref[:, hd]
                e, inv = _att_exp(qv, kv, i * tq, ends_here)
                p = e * inv
                dp = _dot(dov, vv, _NT)
                delta = jnp.sum(dov.astype(F32) * o_ref[:, hd].astype(F32), axis=-1, keepdims=True)
                ds = (p * (dp - delta) * _ATT_SCALE).astype(BF16)
                dq_ref[:, hd] = _dot(ds, kv).astype(BF16)
                dk_ref[:nk, hd] += _dot(ds, qv, _TN)
                dv_ref[:nk, hd] += _dot(p.astype(BF16), dov, _TN)

        _causal_cases(i, s // tq, tq, case)

    qspec = pl.BlockSpec((tq, wide), lambda h, i: (i, h))
    kspec = pl.BlockSpec((s, wide), lambda h, i: (0, h))
    return pl.pallas_call(
        body, name="attn_bwd", out_shape=(_sds((s, _HW), BF16), _sds((s, _HW)), _sds((s, _HW))),
        grid=(C_HEADS // heads, s // tq),
        in_specs=[qspec, kspec, kspec, qspec, qspec], out_specs=(qspec, kspec, kspec),
        compiler_params=_params(("parallel", "arbitrary"), VMEM_BIG),
    )(q, k, v, o, do_all)


_DW = D_GROUPS * LANES


def _tril_bf16(w):
    r = lax.broadcasted_iota(jnp.int32, w.shape, 0)
    c = lax.broadcasted_iota(jnp.int32, w.shape, 1)
    return jnp.where(c <= r, w, 0.0).astype(BF16)


def _sgu_forward(zu, zv, lg, lb, ws_ref, bs):
    u = _gelu(zu)
    v = _gelu(zv)
    mu = jnp.mean(v, axis=-1, keepdims=True)
    vc = v - mu
    rstd = lax.rsqrt(jnp.mean(vc * vc, axis=-1, keepdims=True) + EPS)
    xh = vc * rstd
    vln = (xh * lg + lb).astype(BF16)
    mixed = []
    for g in range(D_GROUPS):
        wg = _tril_bf16(ws_ref[g])
        mixed.append(_dot(wg, vln[:, g * LANES:(g + 1) * LANES]) + bs[:, g:g + 1])
    return u, xh, rstd, vln, jnp.concatenate(mixed, axis=1)


_SGU_CHUNKS = 4


def sgu_fwd(z, lg, lb, ws, bs_t, ycat):
    s = z.shape[0]
    rows = _SGU_CHUNKS * D_CHUNK

    def body(zu_ref, zv_ref, lg_ref, lb_ref, ws_ref, bs_ref, ycat_ref, o_ref):
        del ycat_ref
        for c in range(_SGU_CHUNKS):
            rs = slice(c * D_CHUNK, (c + 1) * D_CHUNK)
            u, _, _, _, mixed = _sgu_forward(zu_ref[rs, :].astype(F32), zv_ref[rs, :].astype(F32), lg_ref[...],
                                             lb_ref[...], ws_ref, bs_ref[...])
            o_ref[rs, :] = (u * mixed).astype(BF16)

    return pl.pallas_call(
        body, name="sgu_fwd", out_shape=_sds(ycat.shape, BF16), grid=(s // rows,),
        in_specs=[pl.BlockSpec((rows, _DW), lambda n: (n, 1)), pl.BlockSpec((rows, _DW), lambda n: (n, 2)),
                  _vec(_DW), _vec(_DW), pl.BlockSpec((D_GROUPS, D_CHUNK, D_CHUNK), lambda n: (0, 0, 0)),
                  pl.BlockSpec((D_CHUNK, LANES), lambda n: (0, 0)), pl.BlockSpec(memory_space=pl.ANY)],
        out_specs=pl.BlockSpec((rows, _DW), lambda n: (n, _HW // _DW)),
        input_output_aliases={6: 0},
        compiler_params=_params(("parallel",)),
    )(z, z, lg, lb, ws, bs_t, ycat)


def sgu_bwd(z, lg, lb, ws, bs_t, dycat, dy_col):
    s = z.shape[0]
    rows = _SGU_CHUNKS * D_CHUNK

    def body(zu_ref, zv_ref, lg_ref, lb_ref, ws_ref, bs_ref, dy_ref, dzu_ref, dzv_ref, dws_ref, dbs_ref, dlg_ref,
             dlb_ref):
        @pl.when(pl.program_id(0) == 0)
        def _():
            dws_ref[...] = jnp.zeros_like(dws_ref)
            dbs_ref[...] = jnp.zeros_like(dbs_ref)
            dlg_ref[...] = jnp.zeros_like(dlg_ref)
            dlb_ref[...] = jnp.zeros_like(dlb_ref)

        lg = lg_ref[...]
        lane = lax.broadcasted_iota(jnp.int32, (D_CHUNK, LANES), 1)
        row = lax.broadcasted_iota(jnp.int32, (D_CHUNK, D_CHUNK), 0)
        colm = lax.broadcasted_iota(jnp.int32, (D_CHUNK, D_CHUNK), 1)
        for c in range(_SGU_CHUNKS):
            rs = slice(c * D_CHUNK, (c + 1) * D_CHUNK)
            zu, zv = zu_ref[rs, :].astype(F32), zv_ref[rs, :].astype(F32)
            u, xh, rstd, vln, mixed = _sgu_forward(zu, zv, lg, lb_ref[...], ws_ref, bs_ref[...])
            dy = dy_ref[rs, :].astype(F32)
            dzu_ref[rs, :] = (dy * mixed * _gelu_grad(zu)).astype(BF16)
            dmix = dy * u
            dvln = []
            dbs = jnp.zeros((D_CHUNK, LANES), F32)
            for g in range(D_GROUPS):
                sl = slice(g * LANES, (g + 1) * LANES)
                dmg = dmix[:, sl]
                dbs = dbs + jnp.where(lane == g, jnp.sum(dmg, axis=-1, keepdims=True), 0.0)
                dmb = dmg.astype(BF16)
                dws_ref[g] += jnp.where(colm <= row, _dot(dmb, vln[:, sl], _NT), 0.0)
                dvln.append(_dot(_tril_bf16(ws_ref[g]), dmb, _TN))
            dbs_ref[...] += dbs
            dvln = jnp.concatenate(dvln, axis=1)
            dlg_ref[...] += jnp.sum(dvln * xh, axis=0, keepdims=True)
            dlb_ref[...] += jnp.sum(dvln, axis=0, keepdims=True)
            dxh = dvln * lg
            dvv = rstd * (dxh - jnp.mean(dxh, axis=-1, keepdims=True)
                          - xh * jnp.mean(dxh * xh, axis=-1, keepdims=True))
            dzv_ref[rs, :] = (dvv * _gelu_grad(zv)).astype(BF16)

    wsspec = pl.BlockSpec((D_GROUPS, D_CHUNK, D_CHUNK), lambda n: (0, 0, 0))
    chunk = lambda cidx: pl.BlockSpec((rows, _DW), lambda n: (n, cidx))
    return pl.pallas_call(
        body, name="sgu_bwd",
        out_shape=(_sds((s, _DW), BF16), _sds((s, _DW), BF16), _sds((D_GROUPS, D_CHUNK, D_CHUNK)),
                   _sds((D_CHUNK, LANES)), _sds((1, _DW)), _sds((1, _DW))),
        grid=(s // rows,),
        in_specs=[chunk(1), chunk(2), _vec(_DW), _vec(_DW), wsspec, pl.BlockSpec((D_CHUNK, LANES), lambda n: (0, 0)),
                  chunk(dy_col)],
        out_specs=(chunk(0), chunk(0), wsspec, pl.BlockSpec((D_CHUNK, LANES), lambda n: (0, 0)), _vec(_DW), _vec(_DW)),
        compiler_params=_params(("arbitrary",)),
    )(z, z, lg, lb, ws, bs_t, dycat)


def ada_mod(c_all, ada_w, ada_b):
    nl, d, n = ada_w.shape
    nb = c_all.shape[0]
    tn = _tile(n, 512)

    def body(c_ref, w_ref, b_ref, o_ref):
        cv = c_ref[...]
        ca = (cv * _sigmoid(cv)).astype(BF16)
        o_ref[...] = _dot(ca, w_ref[...].astype(BF16)) + b_ref[...]

    return pl.pallas_call(
        body, name="ada_mod", out_shape=_sds((nl, nb, n)), grid=(nl, n // tn),
        in_specs=[pl.BlockSpec((nb, d), lambda l, j: (0, 0)), pl.BlockSpec((None, d, tn), lambda l, j: (l, 0, j)),
                  pl.BlockSpec((None, 1, tn), lambda l, j: (l, 0, j))],
        out_specs=pl.BlockSpec((None, nb, tn), lambda l, j: (l, 0, j)),
        compiler_params=_params(("parallel", "parallel")),
    )(c_all, ada_w, ada_b.reshape(nl, 1, n))


_ADAM_BLOCK = 256 * 1024


def _adam_rows(rows, cols):
    if rows * cols <= _ADAM_BLOCK or rows % 8:
        return rows
    return _tile(rows, max(8, _ADAM_BLOCK // cols), 8)


def _adam_update(w, gv, m, v):
    inv_bc1 = 1.0 / (1.0 - ADAM_B1 ** ADAM_STEP)
    inv_bc2 = 1.0 / (1.0 - ADAM_B2 ** ADAM_STEP)
    nm = ADAM_B1 * m + (1.0 - ADAM_B1) * gv
    nv = ADAM_B2 * v + (1.0 - ADAM_B2) * (gv * gv)
    return -ADAM_LR * ((nm * inv_bc1) / (jnp.sqrt(nv * inv_bc2) + ADAM_EPS) + ADAM_WD * w), nm, nv


def adamw(w, g, m, v, name):
    shape = w.shape
    cols = shape[-1]
    rows = w.size // cols
    tr = _adam_rows(rows, cols)

    def body(w_ref, g_ref, m_ref, v_ref, go_ref, d_ref, nm_ref, nv_ref):
        gv = g_ref[...]
        go_ref[...] = gv
        d_ref[...], nm_ref[...], nv_ref[...] = _adam_update(w_ref[...], gv, m_ref[...], v_ref[...])

    spec = pl.BlockSpec((tr, cols), lambda i: (i, 0))
    out = _sds((rows, cols))
    r2 = lambda t: t.reshape(rows, cols)
    res = pl.pallas_call(
        body, name=name, out_shape=(out,) * 4, grid=(rows // tr,),
        in_specs=[spec] * 4, out_specs=(spec,) * 4, compiler_params=_params(("parallel",)),
    )(r2(w), r2(g), r2(m), r2(v))
    return tuple(t.reshape(shape) for t in res)


def adamw_ada(w, c_all, dmod, m, v):
    nl, d, n = w.shape
    tr = _adam_rows(d, n)
    pad = 16 - c_all.shape[0]
    c16 = jnp.pad(c_all, ((0, pad), (0, 0)))
    dm16 = jnp.pad(dmod, ((0, 0), (0, pad), (0, 0)))

    def body(w_ref, c_ref, dm_ref, m_ref, v_ref, g_ref, d_ref, nm_ref, nv_ref):
        cv = c_ref[...]
        gv = _dot((cv * _sigmoid(cv)).astype(BF16), dm_ref[...].astype(BF16), _TN)
        g_ref[...] = gv
        d_ref[...], nm_ref[...], nv_ref[...] = _adam_update(w_ref[...], gv, m_ref[...], v_ref[...])

    spec = pl.BlockSpec((None, tr, n), lambda l, i: (l, i, 0))
    out = _sds((nl, d, n))
    return pl.pallas_call(
        body, name="adamw_ada_w", out_shape=(out, out, out, out), grid=(nl, d // tr),
        in_specs=[spec, pl.BlockSpec((16, tr), lambda l, i: (0, i)), pl.BlockSpec((None, 16, n), lambda l, i: (l, 0, 0)),
                  spec, spec],
        out_specs=(spec,) * 4, compiler_params=_params(("parallel", "parallel")),
    )(w, c16, dm16, m, v)


def adamw_small(ws, gs, ms, vs):
    n = len(ws)
    flat = lambda t: t.reshape(-1, t.shape[-1])

    def body(*refs):
        ins, outs = refs[:4 * n], refs[4 * n:]
        for i in range(n):
            w_ref, g_ref, m_ref, v_ref = ins[4 * i:4 * i + 4]
            outs[3 * i][...], outs[3 * i + 1][...], outs[3 * i + 2][...] = _adam_update(
                w_ref[...], g_ref[...], m_ref[...], v_ref[...])

    operands = [flat(t) for quad in zip(ws, gs, ms, vs) for t in quad]
    res = pl.pallas_call(
        body, name="adamw_small", out_shape=tuple(_sds(flat(w).shape) for w in ws for _ in range(3)),
    )(*operands)
    return [(g, res[3 * i].reshape(w.shape), res[3 * i + 1].reshape(w.shape), res[3 * i + 2].reshape(w.shape))
            for i, (w, g) in enumerate(zip(ws, gs))]


def adamw_layers(w, g0, g1, m, v, name):
    _, rows, cols = w.shape
    tr = _adam_rows(rows, cols)

    def body(w_ref, g0_ref, g1_ref, m_ref, v_ref, g_ref, d_ref, nm_ref, nv_ref):
        gv = jnp.where(pl.program_id(0) == 0, g0_ref[...], g1_ref[...])
        g_ref[...] = gv
        d_ref[...], nm_ref[...], nv_ref[...] = _adam_update(w_ref[...], gv, m_ref[...], v_ref[...])

    spec = pl.BlockSpec((None, tr, cols), lambda l, i: (l, i, 0))
    gspec = pl.BlockSpec((tr, cols), lambda l, i: (i, 0))
    out = _sds((2, rows, cols))
    return pl.pallas_call(
        body, name=name, out_shape=(out, out, out, out), grid=(2, rows // tr),
        in_specs=[spec, gspec, gspec, spec, spec], out_specs=(spec,) * 4, compiler_params=_params(("parallel", "parallel")),
    )(w, g0, g1, m, v)


def sum8(gathered):
    _, r, _ = gathered.shape
    tr = _tile(r, 512, 8)

    def body(g_ref, o_ref):
        acc = g_ref[0]
        for dev in range(1, N_DEV):
            acc = acc + g_ref[dev]
        o_ref[...] = acc

    return pl.pallas_call(
        body, name="sum8", out_shape=_sds((r, LANES)), grid=(r // tr,),
        in_specs=[pl.BlockSpec((N_DEV, tr, LANES), lambda i: (0, i, 0))], out_specs=pl.BlockSpec((tr, LANES), lambda i: (i, 0)),
        compiler_params=_params(("parallel",)),
    )(gathered)


_SUM_STEPS = 2


def pair_sums(gs, recvs, core, name):
    n = len(gs)
    trs = [g.shape[1] // 2 // _SUM_STEPS for g in gs]

    def body(c_ref, *refs):
        del c_ref
        for i in range(n):
            a_ref, b_ref, o_ref = refs[2 * i], refs[2 * i + 1], refs[2 * n + i]
            o_ref[...] = (a_ref[...].astype(F32) + b_ref[...].astype(F32)).astype(BF16)

    in_specs, out_specs = [], []
    for g, tr in zip(gs, trs):
        cols = g.shape[2]
        in_specs.append(pl.BlockSpec((None, tr, cols), lambda k, s, c: (k, c[0] * _SUM_STEPS + s, 0)))
        in_specs.append(pl.BlockSpec((None, tr, cols), lambda k, s, c: (k, s, 0)))
        out_specs.append(pl.BlockSpec((None, tr, cols), lambda k, s, c: (k, s, 0)))
    grid_spec = pltpu.PrefetchScalarGridSpec(num_scalar_prefetch=1, grid=(N_CHIPS, _SUM_STEPS), in_specs=in_specs,
                                             out_specs=tuple(out_specs))
    return list(pl.pallas_call(
        body, name=name, out_shape=tuple(_sds((N_CHIPS, g.shape[1] // 2, g.shape[2]), BF16) for g in gs),
        grid_spec=grid_spec, compiler_params=_params(("parallel", "parallel")),
    )(core.reshape(1).astype(jnp.int32), *[t for pair in zip(gs, recvs) for t in pair]))


def chip_sums(pairs, recvs, chip, core, name):
    n = len(pairs)
    trs = [p.shape[1] // _SUM_STEPS for p in pairs]

    def body(p_ref, *refs):
        del p_ref
        for i in range(n):
            own_ref, r_ref, o_ref = refs[2 * i], refs[2 * i + 1], refs[2 * n + i]
            acc = own_ref[...].astype(F32)
            for j in range(N_CHIPS - 1):
                acc = acc + r_ref[j].astype(F32)
            o_ref[...] = acc

    in_specs, out_specs = [], []
    for p, tr in zip(pairs, trs):
        cols = p.shape[2]
        in_specs.append(pl.BlockSpec((None, tr, cols), lambda s, q: (q[0], s, 0)))
        in_specs.append(pl.BlockSpec((N_CHIPS - 1, tr, cols), lambda s, q: (0, s, 0)))
        out_specs.append(pl.BlockSpec((None, tr, cols), lambda s, q: (q[1], s, 0)))
    grid_spec = pltpu.PrefetchScalarGridSpec(num_scalar_prefetch=1, grid=(_SUM_STEPS,), in_specs=in_specs,
                                             out_specs=tuple(out_specs))
    return list(pl.pallas_call(
        body, name=name, out_shape=tuple(_sds((2,) + p.shape[1:]) for p in pairs), grid_spec=grid_spec,
        compiler_params=_params(("parallel",)),
    )(jnp.stack([chip, core]).astype(jnp.int32), *[t for pair in zip(pairs, recvs) for t in pair]))


def _place():
    return lax.axis_index("x"), lax.axis_index("y"), lax.axis_index("c")


def _other_chips(x, y):
    return [(x, 1 - y), (1 - x, y), (1 - x, 1 - y)]


_HBM = pl.BlockSpec(memory_space=pltpu.HBM)


def all_gather8(v, name, after=()):
    m, n = v.shape

    def body(x_ref, *refs):
        out_ref, send_sems, recv_sems, local_sem = refs[len(after):]
        x, y, c = _place()
        me, sibling = (x, y, c), (x, y, 1 - c)
        chips = _other_chips(x, y)

        def rows(px, py, pc):
            return out_ref.at[pl.ds((4 * px + 2 * py + pc) * m, m), :]

        def copy(k, block, to, src=None):
            return pltpu.make_async_remote_copy(
                src_ref=rows(*block) if src is None else src, dst_ref=rows(*block),
                send_sem=send_sems.at[k], recv_sem=recv_sems.at[k], device_id=to, device_id_type=MESH)

        mine = pltpu.make_async_copy(x_ref, rows(*me), local_sem)
        mine.start()
        first = [copy(0, me, sibling, src=x_ref)]
        first += [copy(1 + j, me, (*chip, c), src=x_ref) for j, chip in enumerate(chips)]
        for cp in first:
            cp.start()
        passed = [copy(4 + j, (*chip, c), sibling) for j, chip in enumerate(chips)]
        for j, chip in enumerate(chips):
            copy(1 + j, (*chip, c), me).wait_recv()
            passed[j].start()
        copy(0, sibling, me).wait_recv()
        for j, chip in enumerate(chips):
            copy(4 + j, (*chip, 1 - c), me).wait_recv()
        for cp in first + passed:
            cp.wait_send()
        mine.wait()

    return pl.pallas_call(
        body, name=name, out_shape=_sds((N_DEV * m, n), v.dtype),
        in_specs=[pl.BlockSpec(memory_space=pltpu.VMEM)] + [pl.BlockSpec(memory_space=pl.ANY)] * len(after),
        out_specs=pl.BlockSpec(memory_space=pltpu.VMEM),
        scratch_shapes=[pltpu.SemaphoreType.DMA((7,)), pltpu.SemaphoreType.DMA((7,)), pltpu.SemaphoreType.DMA],
        compiler_params=_params(None, VMEM_BIG),
    )(v, *after)


def _comm_call(body, name, ins, out_shapes, nsem, aliases=None):
    return pl.pallas_call(
        body, name=name, out_shape=tuple(out_shapes), in_specs=[_HBM] * len(ins), out_specs=tuple([_HBM] * len(out_shapes)),
        scratch_shapes=[pltpu.SemaphoreType.DMA((nsem,)), pltpu.SemaphoreType.DMA((nsem,))],
        input_output_aliases=aliases or {},
    )(*ins)


def _remote(src, dst, send_sems, recv_sems, k, to):
    return pltpu.make_async_remote_copy(src_ref=src, dst_ref=dst, send_sem=send_sems.at[k], recv_sem=recv_sems.at[k],
                                        device_id=to, device_id_type=MESH)


def _half(core, rh):
    return pl.ds(pl.multiple_of(core * rh, 16), rh)


def swap_halves(gs, name):
    n = len(gs)

    def body(*refs):
        ins, outs, (send_sems, recv_sems) = refs[:n], refs[n:2 * n], refs[2 * n:]
        x, y, c = _place()
        copies = []
        for i in range(n):
            theirs = _half(1 - c, ins[i].shape[1] // 2)
            cp = _remote(ins[i].at[:, theirs], outs[i], send_sems, recv_sems, i, (x, y, 1 - c))
            cp.start()
            copies.append(cp)
        for cp in copies:
            cp.wait()

    return _comm_call(body, name, gs, [_sds((g.shape[0], g.shape[1] // 2, g.shape[2]), g.dtype) for g in gs], n)


def join_halves(bufs, name):
    n = len(bufs)

    def body(*refs):
        ins, outs, (send_sems, recv_sems) = refs[:n], refs[n:2 * n], refs[2 * n:]
        x, y, c = _place()
        copies = []
        for i in range(n):
            cp = _remote(ins[i].at[c], outs[i].at[c], send_sems, recv_sems, i, (x, y, 1 - c))
            cp.start()
            copies.append(cp)
        for i in range(n):
            theirs = outs[i].at[1 - c]
            _remote(theirs, theirs, send_sems, recv_sems, i, (x, y, 1 - c)).wait_recv()
        for cp in copies:
            cp.wait_send()

    return _comm_call(body, name, bufs, [_sds(b.shape, b.dtype) for b in bufs], n, {i: i for i in range(n)})


def forward_halves(lands, name):
    n = len(lands)

    def body(*refs):
        ins, outs, (send_sems, recv_sems) = refs[:n], refs[n:2 * n], refs[2 * n:]
        x, y, c = _place()
        sibling = (x, y, 1 - c)
        chips = _other_chips(x, y)
        copies = []
        for i in range(n):
            mine = _half(c, ins[i].shape[1] // 2)
            for j, (px, py) in enumerate(chips):
                cp = _remote(ins[i].at[2 * px + py, mine], outs[i].at[2 * px + py, mine], send_sems, recv_sems, 3 * i + j, sibling)
                cp.start()
                copies.append(cp)
        for i in range(n):
            theirs = _half(1 - c, ins[i].shape[1] // 2)
            for j, (px, py) in enumerate(chips):
                landed = outs[i].at[2 * px + py, theirs]
                _remote(landed, landed, send_sems, recv_sems, 3 * i + j, sibling).wait_recv()
        for cp in copies:
            cp.wait_send()

    return _comm_call(body, name, lands, [_sds(b.shape, b.dtype) for b in lands], 3 * n, {i: i for i in range(n)})


_SEM = pl.BlockSpec(memory_space=pltpu.SEMAPHORE)
_EFFECT = pltpu.SideEffectType.DATAFLOW_SIDE_EFFECTING


def _gather_copies(srcs, lands, send_sems, recv_sems):
    x, y, c = _place()
    copies = []
    for i in range(len(srcs)):
        mine = _half(c, srcs[i].shape[0] // 2)
        for j, chip in enumerate(_other_chips(x, y)):
            copies.append(_remote(srcs[i].at[mine], lands[i].at[2 * x + y, mine], send_sems, recv_sems, 3 * i + j, (*chip, c)))
    return copies


def _exchange_copies(srcs, lands, send_sems, recv_sems):
    x, y, c = _place()
    copies = []
    for i in range(len(srcs)):
        for j, (px, py) in enumerate(_other_chips(x, y)):
            copies.append(_remote(srcs[i].at[2 * px + py], lands[i].at[j], send_sems, recv_sems, 3 * i + j, (px, py, c)))
    return copies


def _everyone_copies(srcs, lands, send_sems, recv_sems):
    x, y, c = _place()
    flip = lambda v, b: 1 - v if b else v
    dst = lands[0].at[4 * x + 2 * y + c]
    return [_remote(srcs[0], dst, send_sems, recv_sems, j - 1, (flip(x, j & 4), flip(y, j & 2), flip(c, j & 1)))
            for j in range(1, N_DEV)]


GATHER = (_gather_copies, 3)
EXCHANGE = (_exchange_copies, 3)
EVERYONE = (_everyone_copies, N_DEV - 1)


def split_start(name, plan, srcs, land_shapes, after=()):
    copies_fn, per_source = plan
    n, m, k = len(srcs), len(land_shapes), len(after)
    ncopies = per_source * n

    def body(*refs):
        src_refs, land_refs = refs[:n], refs[n:n + m]
        send_sems, recv_sems = refs[n + m + k], refs[n + m + k + 1]
        token = refs[-1]
        for cp in copies_fn(src_refs, land_refs, send_sems, recv_sems):
            cp.start()
        token[...] = jnp.zeros_like(token)

    hbm = lambda s: pltpu.HBM(tuple(s.shape), s.dtype)
    outs = pl.pallas_call(
        body, name=name,
        out_shape=(pltpu.SemaphoreType.DMA((ncopies,)), pltpu.SemaphoreType.DMA((ncopies,)), *[hbm(s) for s in srcs],
                   *[hbm(s) for s in land_shapes], _sds((8, LANES))),
        in_specs=[_HBM] * (n + m) + [pl.BlockSpec(memory_space=pl.ANY)] * k,
        out_specs=(_SEM, _SEM, *([_HBM] * (n + m)), pl.BlockSpec(memory_space=pltpu.VMEM)),
        input_output_aliases={i: 2 + i for i in range(n + m)},
        compiler_params=pltpu.CompilerParams(has_side_effects=_EFFECT),
    )(*[pltpu.with_memory_space_constraint(s, pltpu.HBM) for s in srcs],
      *[pltpu.with_memory_space_constraint(lax.empty(tuple(s.shape), s.dtype), pltpu.HBM) for s in land_shapes], *after)
    handle = (outs[0], outs[1], list(outs[2:2 + n]), list(outs[2 + n:2 + n + m]))
    return handle, outs[-1][0, 0]


def split_wait(name, plan, handle, after):
    copies_fn, _ = plan
    send_sems, recv_sems, srcs, lands = handle
    n, m = len(srcs), len(lands)
    after = list(after) if isinstance(after, (list, tuple)) else [after]

    def body(*refs):
        src_refs, land_refs = refs[:n], refs[n:n + m]
        for cp in copies_fn(src_refs, land_refs, refs[n + m], refs[n + m + 1]):
            cp.wait_send()
            cp.wait_recv()

    hbm = lambda s: pltpu.HBM(tuple(s.shape), s.dtype)
    outs = pl.pallas_call(
        body, name=name, out_shape=tuple(hbm(s) for s in srcs + lands),
        in_specs=[_HBM] * (n + m) + [_SEM, _SEM] + [pl.BlockSpec(memory_space=pl.ANY)] * len(after),
        out_specs=tuple([_HBM] * (n + m)), input_output_aliases={i: i for i in range(n + m)},
        compiler_params=pltpu.CompilerParams(has_side_effects=_EFFECT),
    )(*srcs, *lands, send_sems, recv_sems, *after)
    return list(outs[:n]), list(outs[n:])


def chip_major(w, groups=N_CHIPS):
    r, c = w.shape
    return w.reshape(r, groups, c // groups).transpose(1, 0, 2)


def from_chip_major(w):
    g, r, c = w.shape
    return w.transpose(1, 0, 2).reshape(r, g * c)


def _cd_in_pad(w):
    a = C_Q_RANK + C_KV_RANK
    z = lambda n: jnp.zeros((w.shape[0], n), w.dtype)
    return jnp.concatenate([w[:, :a], z(C_NOPE), w[:, a:a + C_ROPE], z(HEAD_PAD - C_NOPE - C_ROPE), w[:, a + C_ROPE:]], axis=1)


def _cd_in_unpad(w):
    a = C_Q_RANK + C_KV_RANK
    return jnp.concatenate([w[:, :a], w[:, a + C_NOPE:a + C_NOPE + C_ROPE], w[:, a + HEAD_PAD:]], axis=1)


def _pad_heads(w, width):
    r = w.shape[0]
    w = w.reshape(r, C_HEADS, width)
    return jnp.pad(w, ((0, 0), (0, 0), (0, HEAD_PAD - width))).reshape(r, _HW)


def _unpad_heads(w, width):
    r = w.shape[0]
    return w.reshape(r, C_HEADS, HEAD_PAD)[:, :, :width].reshape(r, C_HEADS * width)


def prepare_weights(p):
    q = dict(p)
    q["cd_w_in"] = _cd_in_pad(p["cd_w_in"])
    q["c_w_uq"] = _pad_heads(p["c_w_uq"], C_NOPE + C_ROPE)
    ukv = p["c_w_ukv"].reshape(C_KV_RANK, C_HEADS, C_NOPE + C_V)
    q["c_w_uk"] = _pad_heads(ukv[:, :, :C_NOPE].reshape(C_KV_RANK, -1), C_NOPE)
    q["c_w_uv"] = _pad_heads(ukv[:, :, C_NOPE:].reshape(C_KV_RANK, -1), C_V)
    wo = p["cd_w_out"]
    att_rows = jnp.pad(wo[:C_HEADS * C_V].reshape(C_HEADS, C_V, D_MODEL), ((0, 0), (0, HEAD_PAD - C_V), (0, 0)))
    q["cd_w_out"] = jnp.concatenate([att_rows.reshape(_HW, D_MODEL), wo[C_HEADS * C_V:]], axis=0)
    return q


def unprepare_grads(g):
    q = dict(g)
    q["cd_w_in"] = _cd_in_unpad(g["cd_w_in"])
    q["c_w_uq"] = _unpad_heads(g["c_w_uq"], C_NOPE + C_ROPE)
    uk = g.pop("c_w_uk").reshape(C_KV_RANK, C_HEADS, HEAD_PAD)[:, :, :C_NOPE]
    uv = g.pop("c_w_uv").reshape(C_KV_RANK, C_HEADS, HEAD_PAD)[:, :, :C_V]
    q.pop("c_w_uk", None)
    q.pop("c_w_uv", None)
    q["c_w_ukv"] = jnp.concatenate([uk, uv], axis=-1).reshape(C_KV_RANK, C_HEADS * (C_NOPE + C_V))
    wo = g["cd_w_out"]
    att = wo[:_HW].reshape(C_HEADS, HEAD_PAD, D_MODEL)[:, :C_V].reshape(C_HEADS * C_V, D_MODEL)
    q["cd_w_out"] = jnp.concatenate([att, wo[_HW:]], axis=0)
    return q


def rope_tables(positions):
    half = C_ROPE // 2
    inv_freq = ROPE_THETA ** (-jnp.arange(half, dtype=F32) / half)
    ang = positions.astype(F32)[:, None] * inv_freq
    cos, sin = jnp.cos(ang), jnp.sin(ang)
    s = positions.shape[0]
    z = lambda n: jnp.zeros((s, n), F32)
    cs = jnp.concatenate([jnp.ones((s, C_NOPE), F32), cos, cos, z(HEAD_PAD - C_NOPE - C_ROPE)], axis=1)
    s1 = jnp.concatenate([z(C_NOPE), -sin, z(HEAD_PAD - C_NOPE - half)], axis=1)
    s2 = jnp.concatenate([z(C_NOPE + half), sin, z(HEAD_PAD - C_NOPE - C_ROPE)], axis=1)
    return cs, s1, s2


_UP_COLS = 2 * D_FF // N_CHIPS


def ffn_fwd(h2, w, l, late_down=None):
    zf = matmul(h2, w["ffn_w_up"][l], "nn", BF16, f"ffn_up{l}", gb=N_CHIPS, go=2, tn=_UP_COLS)
    if late_down is not None:
        late_down(zf)
    a, f = ffn_act_down(zf, w["ffn_conv_w"][l], w["ffn_w_down"][l], f"ffn_act_down{l}")
    return f, (zf, a)


def ffn_bwd(df, h2, saved, w, l):
    zf, a = saved
    da = matmul(df, w["ffn_w_down"][l], "nt", BF16, f"ffn_down_dx{l}", tn=D_FF // 2)
    d_down = matmul(a, df, "tn", BF16, f"ffn_down_dw{l}", tm=D_FF // 2)
    dzf, d_conv = ffn_act_bwd(zf, w["ffn_conv_w"][l], da, f"ffn_act_bwd{l}")
    dh2 = matmul(dzf, w["ffn_w_up"][l], "nt", BF16, f"ffn_up_dx{l}", ga=2, gb=N_CHIPS, tk=_UP_COLS, tn=D_MODEL)
    d_up = matmul(h2, dzf, "tn", BF16, f"ffn_up_dw{l}", gb=2, go=N_CHIPS, tn=_UP_COLS)
    d_conv = d_conv.transpose(1, 0, 2).reshape(3, 2 * D_FF)
    return dh2, dict(ffn_w_down=d_down, ffn_conv_w=d_conv, ffn_w_up=d_up)


def mixer0_fwd(h, w):
    z = matmul(h, w["ab_w_in"], "nn", BF16, "ab_in", gb=N_CHIPS)
    ycat = pool_fwd(z, w["b_mix_w"], w["b_scale"], gconv_fwd(z, w["a_conv_w"]))
    y = matmul(ycat, w["ab_w_out"], "nn", BF16, "ab_out", tn=D_MODEL)
    return y, (z, ycat)


def mixer0_bwd(dy, h, saved, w):
    z, ycat = saved
    grads = {}
    dycat = matmul(dy, w["ab_w_out"], "nt", BF16, "ab_out_dx")
    grads["ab_w_out"] = matmul(ycat, dy, "tn", BF16, "ab_out_dw")
    db, dc, da, d_conv = gconv_bwd(z, w["a_conv_w"], dycat)
    dp, d_mix, d_scale = pool_bwd(z, w["b_mix_w"], w["b_scale"], dycat)
    dz = jnp.concatenate([db, dc, da, dp], axis=1)
    dh = matmul(dz, w["ab_w_in"], "nt", BF16, "ab_in_dx", gb=N_CHIPS, tn=D_MODEL)
    grads["ab_w_in"] = matmul(h, dz, "tn", BF16, "ab_in_dw", go=N_CHIPS)
    grads.update(a_conv_w=d_conv, b_mix_w=d_mix, b_scale=d_scale)
    return dh, grads


def mixer1_fwd(h, ropes, w):
    cs, s1, s2 = ropes
    z = matmul(h, w["cd_w_in"], "nn", BF16, "cd_in")
    bs_t = jnp.pad(w["d_b_s"].T, ((0, 0), (0, LANES - D_GROUPS)))
    qh, kh, vh = mla_pre_fwd(z, w["c_q_norm_g"], w["c_kv_norm_g"], w["c_w_uq"], w["c_w_uk"], w["c_w_uv"], cs, s1, s2)
    ycat = sgu_fwd(z, w["d_ln_g"], w["d_ln_b"], w["d_w_s"], bs_t, attn_fwd(qh, kh, vh))
    y = matmul(ycat, w["cd_w_out"], "nn", BF16, "cd_out", tn=D_MODEL)
    return y, (z, bs_t, qh, kh, vh, ycat)


def mixer1_bwd(dy, h, saved, ropes, w):
    cs, s1, s2 = ropes
    z, bs_t, qh, kh, vh, ycat = saved
    grads = {}
    dycat = matmul(dy, w["cd_w_out"], "nt", BF16, "cd_out_dx")
    grads["cd_w_out"] = matmul(ycat, dy, "tn", BF16, "cd_out_dw")
    dqh, dkh, dvh = attn_bwd(qh, kh, vh, ycat, dycat)
    dzq, d_uq, d_uk, d_uv, d_gq, d_gkv = mla_pre_bwd(
        z, w["c_q_norm_g"], w["c_kv_norm_g"], w["c_w_uq"], w["c_w_uk"], w["c_w_uv"], cs, s1, s2, dqh, dkh, dvh)
    dzu, dzv, d_ws, d_bs, d_lg, d_lb = sgu_bwd(z, w["d_ln_g"], w["d_ln_b"], w["d_w_s"], bs_t, dycat, _HW // _DW)
    dz = jnp.concatenate([dzq, dzu, dzv], axis=1)
    dh = matmul(dz, w["cd_w_in"], "nt", BF16, "cd_in_dx", tn=D_MODEL)
    grads["cd_w_in"] = matmul(h, dz, "tn", BF16, "cd_in_dw")
    grads.update(c_w_uq=d_uq, c_w_uk=d_uk, c_w_uv=d_uv, c_q_norm_g=d_gq, c_kv_norm_g=d_gkv, d_w_s=d_ws,
                 d_b_s=d_bs[:, :D_GROUPS].T, d_ln_g=d_lg, d_ln_b=d_lb)
    return dh, grads


class StepHooks:
    def weights(self, stage, after):
        pass

    def gradients(self, stage, grads, after):
        return 0.0


def run_step(x, tgt, mod, ropes, w, hooks):
    sh1, sc1, g1, sh2, sc2, g2 = range(N_MOD)
    mods = mod.reshape(2, 1, N_MOD * D_MODEL)
    n1 = w["norm1_g"].reshape(2, 1, D_MODEL)
    n2 = w["norm2_g"].reshape(2, 1, D_MODEL)
    final_g = Vec(w["final_norm_g"].reshape(1, 1, D_MODEL), 0, 0)

    hooks.weights("mix0", mod)
    h0 = modnorm_fwd(x, Vec(n1, 0, 0), Vec(mods, 0, sc1), Vec(mods, 0, sh1), "modnorm_0")
    y0, mix0 = mixer0_fwd(h0, w)
    x1, h1 = resid_modnorm_fwd(x, y0, Vec(mods, 0, g1), Vec(n2, 0, 0), Vec(mods, 0, sc2), Vec(mods, 0, sh2), "resid_modnorm_1")
    hooks.weights("up0", x1)
    f0, ffn0 = ffn_fwd(h1, w, 0, lambda act: hooks.weights("down0", act))
    x2, h2 = resid_modnorm_fwd(x1, f0, Vec(mods, 0, g2), Vec(n1, 1, 0), Vec(mods, 1, sc1), Vec(mods, 1, sh1), "resid_modnorm_2")
    hooks.weights("mix1", x2)
    y1, mix1 = mixer1_fwd(h2, ropes, w)
    x3, h3 = resid_modnorm_fwd(x2, y1, Vec(mods, 1, g1), Vec(n2, 1, 0), Vec(mods, 1, sc2), Vec(mods, 1, sh2), "resid_modnorm_3")
    hooks.weights("ffn1", x3)
    f1, ffn1 = ffn_fwd(h3, w, 1)
    dres, d_final, loss, df1, dg2b = final_fused(x3, f1, Vec(mods, 1, g2), final_g, tgt)

    dh3, gf1 = ffn_bwd(df1, h3, ffn1, w, 1)
    late = mods + hooks.gradients("ffn1", gf1, dh3)
    dres, dsh2b, dsc2b, dn2b, dy1, dg1b = norm_gate_bwd(
        x3, dh3, Vec(n2, 1, 0), Vec(late, 1, sc2), dres, y1, Vec(late, 1, g1), "norm_gate_bwd_3")
    dh2, gm1 = mixer1_bwd(dy1, h2, mix1, ropes, w)
    late = mods + hooks.gradients("mix1", gm1, dh2)
    dres, dsh1b, dsc1b, dn1b, df0, dg2a = norm_gate_bwd(
        x2, dh2, Vec(n1, 1, 0), Vec(late, 1, sc1), dres, f0, Vec(late, 0, g2), "norm_gate_bwd_2")
    dh1, gf0 = ffn_bwd(df0, h1, ffn0, w, 0)
    late = mods + hooks.gradients("ffn0", gf0, dh1)
    dres, dsh2a, dsc2a, dn2a, dy0, dg1a = norm_gate_bwd(
        x1, dh1, Vec(n2, 0, 0), Vec(late, 0, sc2), dres, y0, Vec(late, 0, g1), "norm_gate_bwd_1")
    dh0, gm0 = mixer0_bwd(dy0, h0, mix0, w)
    late = mods + hooks.gradients("mix0", gm0, dh0)
    grad_x, dsh1a, dsc1a, dn1a = norm_bwd(x, dh0, Vec(n1, 0, 0), Vec(late, 0, sc1), dres, "norm_bwd_0")

    dmod = jnp.concatenate([jnp.concatenate([dsh1a, dsc1a, dg1a, dsh2a, dsc2a, dg2a], axis=1),
                            jnp.concatenate([dsh1b, dsc1b, dg1b, dsh2b, dsc2b, dg2b], axis=1)], axis=0)
    norms = dict(norm1_g=jnp.concatenate([dn1a, dn1b], axis=0), norm2_g=jnp.concatenate([dn2a, dn2b], axis=0),
                 final_norm_g=d_final)
    return loss, grad_x, dmod, dict(mix0=gm0, ffn0=gf0, mix1=gm1, ffn1=gf1, norms=norms)


def merge_grads(by_stage):
    grads = {**by_stage["mix0"], **by_stage["mix1"], **by_stage["norms"]}
    for k in ("ffn_w_down", "ffn_w_up"):
        grads[k] = [by_stage["ffn0"][k], by_stage["ffn1"][k]]
    grads["ffn_conv_w"] = jnp.stack([by_stage["ffn0"]["ffn_conv_w"], by_stage["ffn1"]["ffn_conv_w"]])
    return grads


_WEIGHTS = ("ada_w", "ada_b", "norm1_g", "norm2_g", "ab_w_in", "a_conv_w", "b_mix_w", "b_scale", "ab_w_out", "cd_w_in",
            "c_q_norm_g", "c_w_uq", "c_kv_norm_g", "c_w_ukv", "d_ln_g", "d_ln_b", "d_w_s", "d_b_s", "cd_w_out",
            "ffn_w_up", "ffn_conv_w", "ffn_w_down", "final_norm_g")
_INPUTS = ("x", "c", "positions") + _WEIGHTS + ("loss_target",) + tuple("m_" + n for n in _WEIGHTS) + tuple(
    "v_" + n for n in _WEIGHTS)

def _pack_rows(parts, rows, dtype):
    flat = jnp.concatenate([p.reshape(-1).astype(dtype) for p in parts])
    return jnp.pad(flat, (0, rows * LANES - flat.shape[0])).reshape(rows, LANES)


def _rows_major(w):
    r, c = w.shape
    return w.reshape(N_CHIPS, r // N_CHIPS, c)


def start_gather(shards, tag, after=()):
    lands = [_sds((N_CHIPS,) + s.shape, s.dtype) for s in shards]
    return split_start("gather_start_" + tag, GATHER, shards, lands, after)


def finish_gather(handle, chip, tag, after):
    shards, lands = split_wait("gather_wait_" + tag, GATHER, handle, after)
    lands = forward_halves(lands, "gather_forward_" + tag)
    return [lax.dynamic_update_index_in_dim(o, s, chip, 0) for o, s in zip(lands, shards)]


def start_reduce(gs, core, tag):
    recv = swap_halves(gs, "swap_halves_" + tag)
    pairs = pair_sums(gs, recv, core, "pair_sums_" + tag)
    lands = [_sds((N_CHIPS - 1,) + p.shape[1:], p.dtype) for p in pairs]
    return split_start("exchange_start_" + tag, EXCHANGE, pairs, lands)


def finish_reduce(handle, chip, core, tag, after):
    pairs, others = split_wait("exchange_wait_" + tag, EXCHANGE, handle, after)
    halves = chip_sums(pairs, others, chip, core, "chip_sums_" + tag)
    full = join_halves(halves, "join_halves_" + tag)
    return [f.reshape(f.shape[1] * 2, f.shape[2]) for f in full]


_SMALL_SHARDED = (("a_conv_w", (3, 128), 1), ("c_q_norm_g", (1, 64), 1), ("d_ln_g", (1, 128), 1), ("d_ln_b", (1, 128), 1),
                  ("ffn_conv_w", (2, 3, 2 * D_FF // N_CHIPS), 2))
_SMALL_GRADS = (("norm1_g", (2, D_MODEL)), ("norm2_g", (2, D_MODEL)), ("b_mix_w", (4, 128, 128)), ("b_scale", (1, 512)),
                ("c_kv_norm_g", (1, 128)), ("d_w_s", (4, 128, 128)), ("d_b_s", (4, 128)), ("final_norm_g", (1, D_MODEL)),
                ("a_conv_w", (3, 512)), ("c_q_norm_g", (1, 256)), ("d_ln_g", (1, 512)), ("d_ln_b", (1, 512)),
                ("ffn_conv_w", (2, 3, 2 * D_FF)))


def _size(shape):
    n = 1
    for d in shape:
        n *= d
    return n


def kernel(x, c, positions, ada_w, ada_b, norm1_g, norm2_g, ab_w_in, a_conv_w, b_mix_w, b_scale, ab_w_out, cd_w_in, c_q_norm_g, c_w_uq, c_kv_norm_g, c_w_ukv, d_ln_g, d_ln_b, d_w_s, d_b_s, cd_w_out, ffn_w_up, ffn_conv_w, ffn_w_down, final_norm_g, loss_target, m_ada_w, m_ada_b, m_norm1_g, m_norm2_g, m_ab_w_in, m_a_conv_w, m_b_mix_w, m_b_scale, m_ab_w_out, m_cd_w_in, m_c_q_norm_g, m_c_w_uq, m_c_kv_norm_g, m_c_w_ukv, m_d_ln_g, m_d_ln_b, m_d_w_s, m_d_b_s, m_cd_w_out, m_ffn_w_up, m_ffn_conv_w, m_ffn_w_down, m_final_norm_g, v_ada_w, v_ada_b, v_norm1_g, v_norm2_g, v_ab_w_in, v_a_conv_w, v_b_mix_w, v_b_scale, v_ab_w_out, v_cd_w_in, v_c_q_norm_g, v_c_w_uq, v_c_kv_norm_g, v_c_w_ukv, v_d_ln_g, v_d_ln_b, v_d_w_s, v_d_b_s, v_cd_w_out, v_ffn_w_up, v_ffn_conv_w, v_ffn_w_down, v_final_norm_g):
    args = (x, c, positions, ada_w, ada_b, norm1_g, norm2_g, ab_w_in, a_conv_w, b_mix_w, b_scale, ab_w_out, cd_w_in, c_q_norm_g, c_w_uq, c_kv_norm_g, c_w_ukv, d_ln_g, d_ln_b, d_w_s, d_b_s, cd_w_out, ffn_w_up, ffn_conv_w, ffn_w_down, final_norm_g, loss_target, m_ada_w, m_ada_b, m_norm1_g, m_norm2_g, m_ab_w_in, m_a_conv_w, m_b_mix_w, m_b_scale, m_ab_w_out, m_cd_w_in, m_c_q_norm_g, m_c_w_uq, m_c_kv_norm_g, m_c_w_ukv, m_d_ln_g, m_d_ln_b, m_d_w_s, m_d_b_s, m_cd_w_out, m_ffn_w_up, m_ffn_conv_w, m_ffn_w_down, m_final_norm_g, v_ada_w, v_ada_b, v_norm1_g, v_norm2_g, v_ab_w_in, v_a_conv_w, v_b_mix_w, v_b_scale, v_ab_w_out, v_cd_w_in, v_c_q_norm_g, v_c_w_uq, v_c_kv_norm_g, v_c_w_ukv, v_d_ln_g, v_d_ln_b, v_d_w_s, v_d_b_s, v_cd_w_out, v_ffn_w_up, v_ffn_conv_w, v_ffn_w_down, v_final_norm_g)
    a = dict(zip(_INPUTS, args, strict=True))
    xi, yi, ci = _place()
    chip = 2 * xi + yi
    dev = 4 * xi + 2 * yi + ci
    x = a["x"][0]
    tgt = a["loss_target"][0]

    bf = lambda t: t.astype(BF16)
    mix0_handle, tok = start_gather([bf(a["ab_w_in"][0]), bf(a["ab_w_out"][0])], "mix0")
    up0_16, down0_16, up1_16, down1_16 = [bf(a[n][l]) for l in (0, 1) for n in ("ffn_w_up", "ffn_w_down")]
    mix1_16 = [bf(a[n][0]) for n in ("cd_w_in", "c_w_uq", "c_w_ukv", "cd_w_out")]

    small_parts = [a["c"] + tok] + [a[n] for n, _, _ in _SMALL_SHARDED]
    rows1 = -(-sum(p.size for p in small_parts) // LANES // 8) * 8
    g1 = all_gather8(_pack_rows(small_parts, rows1, F32), "gather_small",
                     [up0_16, down0_16, up1_16, down1_16, mix1_16[0], mix1_16[3]]).reshape(N_DEV, rows1 * LANES)
    c_all = g1[:, :D_MODEL]
    per_chip = g1[0::2]
    small_full = {}
    off = D_MODEL
    for n, shp, axis in _SMALL_SHARDED:
        piece = per_chip[:, off:off + _size(shp)].reshape((N_CHIPS,) + shp)
        small_full[n] = jnp.concatenate([piece[k] for k in range(N_CHIPS)], axis=axis)
        off += _size(shp)

    merge = lambda t: t.reshape(t.shape[0] * t.shape[1], t.shape[2])
    w = dict(norm1_g=a["norm1_g"], norm2_g=a["norm2_g"], b_mix_w=a["b_mix_w"][0], b_scale=a["b_scale"],
             c_kv_norm_g=a["c_kv_norm_g"], d_w_s=a["d_w_s"][0], d_b_s=a["d_b_s"][0],
             final_norm_g=a["final_norm_g"].reshape(1, D_MODEL), **small_full)

    ncol = N_MOD * D_MODEL // N_CHIPS
    ada_b_mine = lax.dynamic_slice_in_dim(a["ada_b"], chip * ncol, ncol, axis=1)
    mod_cols = ada_mod(c_all, a["ada_w"], ada_b_mine)
    g2_rows = all_gather8(mod_cols.reshape(-1, LANES), "gather_mod")
    g2 = g2_rows.reshape(N_DEV, 2, N_DEV, ncol)
    mod = lax.dynamic_index_in_dim(g2[0::2], dev, axis=2, keepdims=False)
    mod = mod.transpose(1, 0, 2).reshape(2, N_MOD * D_MODEL)

    late = [g2_rows]
    up0_handle, tok_a = start_gather([up0_16], "up0", late)
    down0_handle, tok_b = start_gather([down0_16], "down0", late)
    mix1_handle, tok_c = start_gather(mix1_16, "mix1", late)
    ffn1_handle, tok_d = start_gather([up1_16, down1_16], "ffn1", late)
    mod = mod + (tok_a + tok_b + tok_c + tok_d)

    ropes = rope_tables(a["positions"][0])
    cm16 = lambda t: chip_major(t).astype(BF16)
    w.update(ffn_w_up=[None, None], ffn_w_down=[None, None])
    handles = dict(mix0=mix0_handle, up0=up0_handle, down0=down0_handle, mix1=mix1_handle, ffn1=ffn1_handle)
    reducing, reduced = {}, {}

    class Hooks(StepHooks):
        def weights(self, stage, after):
            got = finish_gather(handles[stage], chip, stage, after)
            if stage == "mix0":
                w.update(ab_w_in=got[0], ab_w_out=merge(got[1]))
            elif stage == "up0":
                w["ffn_w_up"][0] = got[0]
            elif stage == "down0":
                w["ffn_w_down"][0] = merge(got[0])
            elif stage == "mix1":
                cd_in, uq, ukv, cd_out = got
                w.update(prepare_weights(dict(cd_w_in=from_chip_major(cd_in), c_w_uq=from_chip_major(uq),
                                              c_w_ukv=from_chip_major(ukv), cd_w_out=merge(cd_out))))
            else:
                w["ffn_w_up"][1], w["ffn_w_down"][1] = got[0], merge(got[1])

        def gradients(self, stage, grads, after):
            if stage in ("ffn0", "ffn1"):
                parts = [grads["ffn_w_up"], _rows_major(grads["ffn_w_down"])]
            elif stage == "mix1":
                grads.update(unprepare_grads(grads))
                parts = [cm16(grads["cd_w_in"]), cm16(grads["c_w_uq"]), cm16(grads["c_w_ukv"]),
                         _rows_major(grads["cd_w_out"]).astype(BF16)]
            else:
                parts = [grads["ab_w_in"], _rows_major(grads["ab_w_out"])]
            reducing[stage], tok = start_reduce(parts, ci, stage)
            before = {"mix1": "ffn1", "ffn0": "mix1", "mix0": "ffn0"}.get(stage)
            if before is not None:
                reduced[before] = finish_reduce(reducing[before], chip, ci, before, after)
            return tok

    loss, grad_x, dmod, by_stage = run_step(x, tgt, mod, ropes, w, Hooks())
    grads = merge_grads(by_stage)

    parts3 = [dmod] + [grads[n] for n, _ in _SMALL_GRADS] + [loss[0, 0]]
    rows3 = -(-sum(p.size for p in parts3) // LANES // 8) * 8
    small_handle, _ = split_start("small_grads_start", EVERYONE, [_pack_rows(parts3, rows3, F32)],
                                  [_sds((N_DEV, rows3, LANES))])
    red_up1, red_down1 = reduced["ffn1"]
    red_cd_in, red_uq, red_ukv, red_cd_out = reduced["mix1"]
    red_up0, red_down0 = reduced["ffn0"]
    out_grads = dict(cd_w_in=red_cd_in, c_w_uq=red_uq, c_w_ukv=red_ukv, cd_w_out=red_cd_out)
    per_layer = dict(ffn_w_up=(red_up0, red_up1), ffn_w_down=(red_down0, red_down1))
    updates = {}

    def update(n):
        if n in per_layer:
            updates[n] = adamw_layers(a[n], *per_layer[n], a["m_" + n], a["v_" + n], "adamw_" + n)
        else:
            updates[n] = adamw(a[n], out_grads[n].reshape(a[n].shape), a["m_" + n], a["v_" + n], "adamw_" + n)

    early =("ffn_w_up", "ffn_w_down", "cd_w_in", "c_w_uq", "c_w_ukv", "cd_w_out")
    for n in early:
        update(n)
    (mine,), (landed,) = split_wait("small_grads_wait", EVERYONE, small_handle, [updates[n][1] for n in early])
    g3 = lax.dynamic_update_index_in_dim(landed, mine, dev, 0)
    summed = sum8(g3).reshape(-1)
    nmod = 2 * N_MOD * D_MODEL
    out_grads["ada_b"] = summed[:nmod].reshape(2, N_MOD * D_MODEL)
    off = nmod
    for n, shp in _SMALL_GRADS:
        out_grads[n] = summed[off:off + _size(shp)].reshape(shp)
        off += _size(shp)
    loss = summed[off]
    for n, shp, axis in _SMALL_SHARDED:
        width = out_grads[n].shape[-1] // N_CHIPS
        out_grads[n] = lax.dynamic_slice_in_dim(out_grads[n], chip * width, width, axis=out_grads[n].ndim - 1)
    dmod_all = g3.reshape(N_DEV, rows3 * LANES)[:, :nmod].reshape(N_DEV, 2, N_MOD * D_MODEL)
    dmod_mine = lax.dynamic_slice_in_dim(dmod_all, chip * ncol, ncol, axis=2).transpose(1, 0, 2)
    updates["ada_w"] = adamw_ada(a["ada_w"], c_all, dmod_mine, a["m_ada_w"], a["v_ada_w"])

    red_in0, red_out0 = finish_reduce(reducing["mix0"], chip, ci, "mix0", updates["ada_w"][1])
    out_grads.update(ab_w_in=red_in0, ab_w_out=red_out0)

    for n in ("ab_w_in", "ab_w_out"):
        update(n)
    small = [n for n in _WEIGHTS if n not in updates]
    for n, res in zip(small, adamw_small([a[n] for n in small], [out_grads[n].reshape(a[n].shape) for n in small],
                                         [a["m_" + n] for n in small], [a["v_" + n] for n in small])):
        updates[n] = res
    return (loss, grad_x[None], *[updates[n][i] for i in range(4) for n in _WEIGHTS])
```

```python
import functools
from typing import NamedTuple

import jax
import jax.numpy as jnp
from jax import lax
from jax.experimental import pallas as pl
from jax.experimental.pallas import tpu as pltpu

F32 = jnp.float32
BF16 = jnp.bfloat16
EPS = 1e-6
D_MODEL = 1024
N_MOD = 6
A_WIDTH = 512
B_GROUPS = 4
C_HEADS = 8
C_NOPE = 64
C_ROPE = 32
C_V = 64
C_Q_RANK = 256
C_KV_RANK = 128
HEAD_PAD = 128
ROPE_THETA = 10000.0
D_GROUPS = 4
D_CHUNK = 128
D_FF = 2816
FF_UNIT = 128
ADAM_LR = 0.001
ADAM_B1 = 0.9
ADAM_B2 = 0.999
ADAM_EPS = 1e-08
ADAM_WD = 0.01
ADAM_STEP = 10
N_CHIPS = 4
N_DEV = 8
LANES = 128
VMEM_BIG = 56 * 1024 * 1024
MESH = pl.DeviceIdType.MESH


def _sds(shape, dtype=F32):
    return jax.ShapeDtypeStruct(tuple(shape), dtype)


def _tile(n, cap, mult=128):
    if n <= cap:
        return n
    best = None
    for t in range(mult, cap + 1, mult):
        if n % t == 0:
            best = t
    assert best is not None, (n, cap, mult)
    return best


def _params(dims=None, vmem=None):
    return pltpu.CompilerParams(dimension_semantics=dims, vmem_limit_bytes=vmem)


def _shift_down(v, k):
    r = pltpu.roll(v, k, axis=0)
    t = lax.broadcasted_iota(jnp.int32, v.shape, 0)
    return jnp.where(t >= k, r, 0.0)


def _shift_up(v, k):
    n = v.shape[0]
    r = pltpu.roll(v, n - k, axis=0)
    t = lax.broadcasted_iota(jnp.int32, v.shape, 0)
    return jnp.where(t < n - k, r, 0.0)


def _sigmoid(v):
    return 1.0 / (1.0 + jnp.exp(-v))


_GELU_C = 0.7978845608028654
_GELU_A = 0.044715


def _gelu(v):
    return 0.5 * v * (1.0 + jnp.tanh(_GELU_C * (v + _GELU_A * v * v * v)))


def _gelu_grad(v):
    th = jnp.tanh(_GELU_C * (v + _GELU_A * v * v * v))
    return 0.5 * (1.0 + th) + 0.5 * v * (1.0 - th * th) * _GELU_C * (1.0 + 3.0 * _GELU_A * v * v)


_NN = (((1,), (0,)), ((), ()))
_NT = (((1,), (1,)), ((), ()))
_TN = (((0,), (0,)), ((), ()))


def _dot(a, b, dims=_NN):
    return lax.dot_general(a, b, dims, preferred_element_type=F32)


def _logical(t, groups):
    return (t.shape[-2], t.shape[-1] * groups)


def _block(tr, tc, groups, cols, where):
    if groups == 1:
        return pl.BlockSpec((tr, tc), where)
    per = cols // groups // tc

    def index(i, j, s):
        r, c = where(i, j, s)
        return (c // per, r, c % per)

    return pl.BlockSpec((None, tr, tc), index)


def matmul(a, b, mode, out_dtype, name, ga=1, gb=1, go=1, tm=None, tn=None, tk=None):
    (ar, ac), (br, bc) = _logical(a, ga), _logical(b, gb)
    if mode == "nn":
        m, k, n = ar, ac, bc
        a_col, b_col = "k", "n"
    elif mode == "nt":
        m, k, n = ar, ac, br
        a_col, b_col = "k", "k"
    else:
        k, m, n = ar, ac, bc
        a_col, b_col = "m", "n"
    limit = {"m": m, "n": n // go, "k": k}
    limit[a_col] = min(limit[a_col], ac // ga)
    limit[b_col] = min(limit[b_col], bc // gb)
    tm = tm or _tile(limit["m"], 2048, 128 if mode == "tn" else 16)
    tn = tn or _tile(limit["n"], 512)
    tk = tk or _tile(limit["k"], 2048, 16 if mode == "tn" else 128)
    nk = k // tk
    if mode == "nn":
        a_spec = _block(tm, tk, ga, ac, lambda i, j, s: (i, s))
        b_spec = _block(tk, tn, gb, bc, lambda i, j, s: (s, j))
        dims = _NN
    elif mode == "nt":
        a_spec = _block(tm, tk, ga, ac, lambda i, j, s: (i, s))
        b_spec = _block(tn, tk, gb, bc, lambda i, j, s: (j, s))
        dims = _NT
    else:
        a_spec = _block(tk, tm, ga, ac, lambda i, j, s: (s, i))
        b_spec = _block(tk, tn, gb, bc, lambda i, j, s: (s, j))
        dims = _TN
    o_spec = _block(tm, tn, go, n, lambda i, j, s: (i, j))
    out_shape = _sds((m, n), out_dtype) if go == 1 else _sds((go, m, n // go), out_dtype)

    def whole(a_ref, b_ref, o_ref):
        o_ref[...] = _dot(a_ref[...], b_ref[...], dims).astype(o_ref.dtype)

    def accumulating(a_ref, b_ref, o_ref, acc_ref):
        s = pl.program_id(2)

        @pl.when(s == 0)
        def _():
            acc_ref[...] = jnp.zeros_like(acc_ref)

        acc_ref[...] += _dot(a_ref[...], b_ref[...], dims)

        @pl.when(s == nk - 1)
        def _():
            o_ref[...] = acc_ref[...].astype(o_ref.dtype)

    return pl.pallas_call(
        whole if nk == 1 else accumulating, name=name, out_shape=out_shape, grid=(m // tm, n // tn, nk),
        in_specs=[a_spec, b_spec], out_specs=o_spec,
        scratch_shapes=[] if nk == 1 else [pltpu.VMEM((tm, tn), F32)],
        compiler_params=_params(("parallel", "parallel", "arbitrary"), VMEM_BIG),
    )(a, b)


def _rows(tm, n):
    return pl.BlockSpec((tm, n), lambda i: (i, 0))


def _vec(n):
    return pl.BlockSpec((1, n), lambda i: (0, 0))


class Vec(NamedTuple):
    array: jax.Array
    row: int
    col: int


def _vec_in(v, d):
    return pl.BlockSpec((None, 1, d), lambda i: (v.row, 0, v.col))


def modnorm_fwd(x, g, sc, sh, name):
    s, d = x.shape
    tm = _tile(s, 256, 8)

    def body(x_ref, g_ref, sc_ref, sh_ref, o_ref):
        xv = x_ref[...]
        r = lax.rsqrt(jnp.mean(xv * xv, axis=-1, keepdims=True) + EPS)
        o_ref[...] = ((xv * r) * g_ref[...] * (1.0 + sc_ref[...]) + sh_ref[...]).astype(BF16)

    return pl.pallas_call(
        body, name=name, out_shape=_sds((s, d), BF16), grid=(s // tm,),
        in_specs=[_rows(tm, d), _vec_in(g, d), _vec_in(sc, d), _vec_in(sh, d)], out_specs=_rows(tm, d),
        compiler_params=_params(("parallel",)),
    )(x, g.array, sc.array, sh.array)


def norm_bwd(x, dh, g, sc, dres, name):
    s, d = x.shape
    tm = _tile(s, 256, 8)
    nsteps = s // tm

    def body(x_ref, dh_ref, g_ref, sc_ref, dr_ref, dx_ref, dsh_ref, dsc_ref, dg_ref, a2_ref):
        i = pl.program_id(0)

        @pl.when(i == 0)
        def _():
            dsh_ref[...] = jnp.zeros_like(dsh_ref)
            a2_ref[...] = jnp.zeros_like(a2_ref)

        xv = x_ref[...]
        dh = dh_ref[...].astype(F32)
        r = lax.rsqrt(jnp.mean(xv * xv, axis=-1, keepdims=True) + EPS)
        xh = xv * r
        dsh_ref[...] += jnp.sum(dh, axis=0, keepdims=True)
        a2_ref[...] += jnp.sum(dh * xh, axis=0, keepdims=True)
        dxh = dh * (g_ref[...] * (1.0 + sc_ref[...]))
        dx = r * (dxh - xh * jnp.mean(dxh * xh, axis=-1, keepdims=True))
        dx_ref[...] = dr_ref[...] + dx

        @pl.when(i == nsteps - 1)
        def _():
            dsc_ref[...] = a2_ref[...] * g_ref[...]
            dg_ref[...] = a2_ref[...] * (1.0 + sc_ref[...])

    return pl.pallas_call(
        body, name=name, out_shape=(_sds((s, d)), _sds((1, d)), _sds((1, d)), _sds((1, d))), grid=(nsteps,),
        in_specs=[_rows(tm, d), _rows(tm, d), _vec_in(g, d), _vec_in(sc, d), _rows(tm, d)],
        out_specs=(_rows(tm, d), _vec(d), _vec(d), _vec(d)),
        scratch_shapes=[pltpu.VMEM((1, d), F32)],
        compiler_params=_params(("arbitrary",)),
    )(x, dh, g.array, sc.array, dres)


def resid_modnorm_fwd(x, y, gate, g, sc, sh, name):
    s, d = x.shape
    tm = _tile(s, 256, 8)

    def body(x_ref, y_ref, gate_ref, g_ref, sc_ref, sh_ref, xo_ref, h_ref):
        xv = x_ref[...] + gate_ref[...] * y_ref[...].astype(F32)
        xo_ref[...] = xv
        r = lax.rsqrt(jnp.mean(xv * xv, axis=-1, keepdims=True) + EPS)
        h_ref[...] = ((xv * r) * g_ref[...] * (1.0 + sc_ref[...]) + sh_ref[...]).astype(BF16)

    return pl.pallas_call(
        body, name=name, out_shape=(_sds((s, d)), _sds((s, d), BF16)), grid=(s // tm,),
        in_specs=[_rows(tm, d), _rows(tm, d), _vec_in(gate, d), _vec_in(g, d), _vec_in(sc, d), _vec_in(sh, d)],
        out_specs=(_rows(tm, d), _rows(tm, d)),
        compiler_params=_params(("parallel",)),
    )(x, y, gate.array, g.array, sc.array, sh.array)


def norm_gate_bwd(x, dh, g, sc, dres, y, gate, name):
    s, d = x.shape
    tm = _tile(s, 256, 8)
    nsteps = s // tm

    def body(x_ref, dh_ref, g_ref, sc_ref, dr_ref, y_ref, gate_ref, dx_ref, dsh_ref, dsc_ref, dg_ref, dy_ref,
             dgate_ref, a2_ref):
        i = pl.program_id(0)

        @pl.when(i == 0)
        def _():
            dsh_ref[...] = jnp.zeros_like(dsh_ref)
            a2_ref[...] = jnp.zeros_like(a2_ref)
            dgate_ref[...] = jnp.zeros_like(dgate_ref)

        xv = x_ref[...]
        dh = dh_ref[...].astype(F32)
        r = lax.rsqrt(jnp.mean(xv * xv, axis=-1, keepdims=True) + EPS)
        xh = xv * r
        dsh_ref[...] += jnp.sum(dh, axis=0, keepdims=True)
        a2_ref[...] += jnp.sum(dh * xh, axis=0, keepdims=True)
        dxh = dh * (g_ref[...] * (1.0 + sc_ref[...]))
        dr = dr_ref[...] + r * (dxh - xh * jnp.mean(dxh * xh, axis=-1, keepdims=True))
        dx_ref[...] = dr
        dy_ref[...] = (dr * gate_ref[...]).astype(BF16)
        dgate_ref[...] += jnp.sum(dr * y_ref[...].astype(F32), axis=0, keepdims=True)

        @pl.when(i == nsteps - 1)
        def _():
            dsc_ref[...] = a2_ref[...] * g_ref[...]
            dg_ref[...] = a2_ref[...] * (1.0 + sc_ref[...])

    vec = _sds((1, d))
    return pl.pallas_call(
        body, name=name, out_shape=(_sds((s, d)), vec, vec, vec, _sds((s, d), BF16), vec), grid=(nsteps,),
        in_specs=[_rows(tm, d), _rows(tm, d), _vec_in(g, d), _vec_in(sc, d), _rows(tm, d), _rows(tm, d), _vec_in(gate, d)],
        out_specs=(_rows(tm, d), _vec(d), _vec(d), _vec(d), _rows(tm, d), _vec(d)),
        scratch_shapes=[pltpu.VMEM((1, d), F32)],
        compiler_params=_params(("arbitrary",)),
    )(x, dh, g.array, sc.array, dres, y, gate.array)


def final_fused(x, f, gate, g, tgt):
    s, d = x.shape
    tm = _tile(s, 256, 8)

    def body(x_ref, f_ref, gate_ref, g_ref, t_ref, dx_ref, dg_ref, loss_ref, df_ref, dgate_ref):
        @pl.when(pl.program_id(0) == 0)
        def _():
            dg_ref[...] = jnp.zeros_like(dg_ref)
            loss_ref[...] = jnp.zeros_like(loss_ref)
            dgate_ref[...] = jnp.zeros_like(dgate_ref)

        fv, gatev, gv = f_ref[...].astype(F32), gate_ref[...], g_ref[...]
        xv = x_ref[...] + gatev * fv
        r = lax.rsqrt(jnp.mean(xv * xv, axis=-1, keepdims=True) + EPS)
        xh = xv * r
        e = xh * gv - t_ref[...]
        row = jnp.sum(e * e, axis=-1, keepdims=True) * (0.5 / d)
        loss_ref[...] += jnp.sum(row, axis=0, keepdims=True)
        dy = e * (1.0 / d)
        dg_ref[...] += jnp.sum(dy * xh, axis=0, keepdims=True)
        dxh = dy * gv
        dx = r * (dxh - xh * jnp.mean(dxh * xh, axis=-1, keepdims=True))
        dx_ref[...] = dx
        df_ref[...] = (dx * gatev).astype(BF16)
        dgate_ref[...] += jnp.sum(dx * fv, axis=0, keepdims=True)

    vec = _sds((1, d))
    return pl.pallas_call(
        body, name="final_fused", out_shape=(_sds((s, d)), vec, _sds((1, LANES)), _sds((s, d), BF16), vec),
        grid=(s // tm,),
        in_specs=[_rows(tm, d), _rows(tm, d), _vec_in(gate, d), _vec_in(g, d), _rows(tm, d)],
        out_specs=(_rows(tm, d), _vec(d), _vec(LANES), _rows(tm, d), _vec(d)),
        compiler_params=_params(("arbitrary",)),
    )(x, f, gate.array, g.array, tgt)


def _taps(v):
    return _shift_down(v, 2), _shift_down(v, 1), v


def _conv3_taps(taps, w):
    return w[0:1, :] * taps[0] + w[1:2, :] * taps[1] + w[2:3, :] * taps[2]


def _conv3(v, w):
    return _conv3_taps(_taps(v), w)


def _conv3_t(dv, w):
    return w[0:1, :] * _shift_up(dv, 2) + w[1:2, :] * _shift_up(dv, 1) + w[2:3, :] * dv


def _conv3_dw_taps(dv, taps):
    return jnp.concatenate([jnp.sum(dv * t, axis=0, keepdims=True) for t in taps], axis=0)


def _conv3_dw(dv, v):
    return _conv3_dw_taps(dv, _taps(v))


def gconv_fwd(z, conv_w):
    s = z.shape[0]
    nb = A_WIDTH // LANES

    def body(b_ref, c_ref, a_ref, w_ref, o_ref):
        b, c, a = b_ref[...].astype(F32), c_ref[...].astype(F32), a_ref[...].astype(F32)
        o_ref[...] = (b * _conv3(c * a, w_ref[...])).astype(BF16)

    col = lambda off: pl.BlockSpec((s, LANES), lambda j: (0, off + j))
    return pl.pallas_call(
        body, name="gconv_fwd", out_shape=_sds((s, A_WIDTH + _B_WIDTH), BF16), grid=(nb,),
        in_specs=[col(0), col(nb), col(2 * nb), pl.BlockSpec((3, LANES), lambda j: (0, j))],
        out_specs=pl.BlockSpec((s, LANES), lambda j: (0, j)),
        compiler_params=_params(("parallel",), VMEM_BIG),
    )(z, z, z, conv_w)


def gconv_bwd(z, conv_w, dycat):
    s = z.shape[0]
    nb = A_WIDTH // LANES

    def body(b_ref, c_ref, a_ref, w_ref, dy_ref, db_ref, dc_ref, da_ref, dw_ref):
        c, a, w, dy = c_ref[...].astype(F32), a_ref[...].astype(F32), w_ref[...], dy_ref[...].astype(F32)
        ca = c * a
        db_ref[...] = (dy * _conv3(ca, w)).astype(BF16)
        dconv = dy * b_ref[...].astype(F32)
        dw_ref[...] = _conv3_dw(dconv, ca)
        dca = _conv3_t(dconv, w)
        dc_ref[...] = (dca * a).astype(BF16)
        da_ref[...] = (dca * c).astype(BF16)

    col = lambda off: pl.BlockSpec((s, LANES), lambda j: (0, off + j))
    wspec = pl.BlockSpec((3, LANES), lambda j: (0, j))
    part = _sds((s, A_WIDTH), BF16)
    return pl.pallas_call(
        body, name="gconv_bwd", out_shape=(part, part, part, _sds((3, A_WIDTH))), grid=(nb,),
        in_specs=[col(0), col(nb), col(2 * nb), wspec, col(0)],
        out_specs=(col(0), col(0), col(0), wspec),
        compiler_params=_params(("parallel",), VMEM_BIG),
    )(z, z, z, conv_w, dycat)


def _pool_counts(s, w):
    t = lax.broadcasted_iota(jnp.int32, (s, 1), 0)
    return jnp.minimum(t + 1, w).astype(F32)


def _pooled(p, levels):
    acc = p
    for lv in range(levels):
        acc = acc + _shift_down(acc, 2 ** lv)
    return acc / _pool_counts(p.shape[0], 2 ** levels) - p


_B_WIDTH = B_GROUPS * LANES


def pool_fwd(z, mix_w, scale, ycat):
    s = z.shape[0]

    def body(p_ref, m_ref, sc_ref, ycat_ref, o_ref):
        del ycat_ref
        for g in range(B_GROUPS):
            cols = slice(g * LANES, (g + 1) * LANES)
            pooled = _pooled(p_ref[:, cols].astype(F32), g + 1)
            y = _dot(pooled.astype(BF16), m_ref[g].astype(BF16))
            o_ref[:, cols] = (y * sc_ref[:, cols]).astype(BF16)

    return pl.pallas_call(
        body, name="pool_fwd", out_shape=_sds(ycat.shape, BF16), grid=(1,),
        in_specs=[pl.BlockSpec((s, _B_WIDTH), lambda i: (0, 3 * A_WIDTH // _B_WIDTH)),
                  pl.BlockSpec((B_GROUPS, LANES, LANES), lambda i: (0, 0, 0)), pl.BlockSpec((1, _B_WIDTH), lambda i: (0, 0)),
                  pl.BlockSpec(memory_space=pl.ANY)],
        out_specs=pl.BlockSpec((s, _B_WIDTH), lambda i: (0, A_WIDTH // _B_WIDTH)),
        input_output_aliases={3: 0},
        compiler_params=_params(("arbitrary",), VMEM_BIG),
    )(z, mix_w, scale, ycat)


def pool_bwd(z, mix_w, scale, dycat):
    s = z.shape[0]

    def body(p_ref, m_ref, sc_ref, dy_ref, dp_ref, dm_ref, dsc_ref):
        for g in range(B_GROUPS):
            cols = slice(g * LANES, (g + 1) * LANES)
            pooled = _pooled(p_ref[:, cols].astype(F32), g + 1)
            mw = m_ref[g].astype(BF16)
            pb = pooled.astype(BF16)
            dy = dy_ref[:, cols].astype(F32)
            dsc_ref[:, cols] = jnp.sum(dy * _dot(pb, mw), axis=0, keepdims=True)
            dmix = (dy * sc_ref[:, cols]).astype(BF16)
            dm_ref[g] = _dot(pb, dmix, _TN)
            dpool = _dot(dmix, mw, _NT)
            acc = dpool / _pool_counts(s, 2 ** (g + 1))
            for lv in range(g + 1):
                acc = acc + _shift_up(acc, 2 ** lv)
            dp_ref[:, cols] = (acc - dpool).astype(BF16)

    wide = lambda c: pl.BlockSpec((s, _B_WIDTH), lambda i: (0, c))
    mspec = pl.BlockSpec((B_GROUPS, LANES, LANES), lambda i: (0, 0, 0))
    vspec = pl.BlockSpec((1, _B_WIDTH), lambda i: (0, 0))
    return pl.pallas_call(
        body, name="pool_bwd", out_shape=(_sds((s, _B_WIDTH), BF16), _sds((B_GROUPS, LANES, LANES)), _sds((1, _B_WIDTH))),
        grid=(1,), in_specs=[wide(3 * A_WIDTH // _B_WIDTH), mspec, vspec, wide(A_WIDTH // _B_WIDTH)],
        out_specs=(wide(0), mspec, vspec),
        compiler_params=_params(("arbitrary",), VMEM_BIG),
    )(z, mix_w, scale, dycat)


_FF_BLOCKS = D_FF // FF_UNIT


def _ff_spec(s):
    return pl.BlockSpec((2, s, FF_UNIT), lambda j: (0, 0, j))


def _ff_wspecs():
    return [pl.BlockSpec((3, FF_UNIT), lambda j: (0, j)), pl.BlockSpec((3, FF_UNIT), lambda j: (0, _FF_BLOCKS + j))]


_FF_ROWS = 64
_FF_HALO = 16


def _chunk_taps(z_ref, half, c):
    start = pl.multiple_of(c * _FF_ROWS, _FF_ROWS)
    before = pl.multiple_of(jnp.maximum(c * _FF_ROWS - _FF_HALO, 0), _FF_HALO)
    halo = z_ref[half, pl.ds(before, _FF_HALO), :].astype(F32)
    halo = jnp.where(c > 0, halo, 0.0)
    win = jnp.concatenate([halo, z_ref[half, pl.ds(start, _FF_ROWS), :].astype(F32)], axis=0)
    return tuple(pltpu.roll(win, k, axis=0)[_FF_HALO:] for k in (2, 1)) + (win[_FF_HALO:],)


def _fold8(v):
    acc = v[0:8]
    for r in range(8, v.shape[0], 8):
        acc = acc + v[r:r + 8]
    return acc


_FF_CHUNK = 256


def ffn_act_down(zf, conv_w, w_down, name):
    s, d = zf.shape[1], w_down.shape[1]
    nk = D_FF // _FF_CHUNK
    chunk = lambda k: jnp.minimum(k, nk - 1)

    def body(z_ref, wg_ref, wu_ref, wd_ref, a_ref, f_ref, held_ref, acc_ref):
        k = pl.program_id(0)

        @pl.when(k == 0)
        def _():
            held_ref[...] = jnp.zeros_like(held_ref)
            acc_ref[...] = jnp.zeros_like(acc_ref)

        acc_ref[...] += _dot(held_ref[(k + 1) % 2], wd_ref[...])
        g = _conv3(z_ref[0].astype(F32), wg_ref[...])
        u = _conv3(z_ref[1].astype(F32), wu_ref[...])
        act = (g * _sigmoid(g) * u).astype(BF16)
        a_ref[...] = act
        held_ref[k % 2] = act

        @pl.when(k == nk)
        def _():
            f_ref[...] = acc_ref[...].astype(BF16)

    return pl.pallas_call(
        body, name=name, out_shape=(_sds((s, D_FF), BF16), _sds((s, d), BF16)), grid=(nk + 1,),
        in_specs=[pl.BlockSpec((2, s, _FF_CHUNK), lambda k: (0, 0, chunk(k))),
                  pl.BlockSpec((3, _FF_CHUNK), lambda k: (0, chunk(k))),
                  pl.BlockSpec((3, _FF_CHUNK), lambda k: (0, nk + chunk(k))),
                  pl.BlockSpec((_FF_CHUNK, d), lambda k: (jnp.maximum(k - 1, 0), 0))],
        out_specs=(pl.BlockSpec((s, _FF_CHUNK), lambda k: (0, chunk(k))), pl.BlockSpec((s, d), lambda k: (0, 0))),
        scratch_shapes=[pltpu.VMEM((2, s, _FF_CHUNK), BF16), pltpu.VMEM((s, d), F32)],
        compiler_params=_params(("arbitrary",), VMEM_BIG),
    )(zf, conv_w, conv_w, w_down)


def ffn_act_bwd(zf, conv_w, da, name):
    s = zf.shape[1]
    assert s % _FF_ROWS == 0
    nchunks = s // _FF_ROWS

    def body(z_ref, wg_ref, wu_ref, da_ref, dz_ref, dw_ref, dg_ref, du_ref):
        wg, wu = wg_ref[...], wu_ref[...]

        def first(c, acc):
            rows = pl.ds(pl.multiple_of(c * _FF_ROWS, _FF_ROWS), _FF_ROWS)
            tg, tu = _chunk_taps(z_ref, 0, c), _chunk_taps(z_ref, 1, c)
            g = _conv3_taps(tg, wg)
            u = _conv3_taps(tu, wu)
            dav = da_ref[rows, :].astype(F32)
            sg = _sigmoid(g)
            dg = dav * u * (sg * (1.0 + g * (1.0 - sg)))
            du = dav * (g * sg)
            dg_ref[rows, :] = dg
            du_ref[rows, :] = du
            return tuple(a + _fold8(d * t) for a, (d, t) in zip(acc, [(dg, t) for t in tg] + [(du, t) for t in tu]))

        zero = jnp.zeros((8, FF_UNIT), F32)
        acc = lax.fori_loop(0, nchunks, first, (zero,) * 6)
        sums = [jnp.sum(a, axis=0, keepdims=True) for a in acc]
        dw_ref[0] = jnp.concatenate(sums[:3], axis=0)
        dw_ref[1] = jnp.concatenate(sums[3:], axis=0)

        tail = pl.ds(s, _FF_HALO)
        dg_ref[tail, :] = jnp.zeros((_FF_HALO, FF_UNIT), F32)
        du_ref[tail, :] = jnp.zeros((_FF_HALO, FF_UNIT), F32)
        span = _FF_ROWS + _FF_HALO

        def second(c, carry):
            start = pl.multiple_of(c * _FF_ROWS, _FF_ROWS)
            for half, (d_ref, w) in enumerate(((dg_ref, wg), (du_ref, wu))):
                win = d_ref[pl.ds(start, span), :]
                dz = (w[0:1, :] * pltpu.roll(win, span - 2, axis=0)[:_FF_ROWS]
                      + w[1:2, :] * pltpu.roll(win, span - 1, axis=0)[:_FF_ROWS] + w[2:3, :] * win[:_FF_ROWS])
                dz_ref[half, pl.ds(start, _FF_ROWS), :] = dz.astype(BF16)
            return carry

        lax.fori_loop(0, nchunks, second, 0)

    return pl.pallas_call(
        body, name=name, out_shape=(_sds((2, s, D_FF), BF16), _sds((2, 3, D_FF))), grid=(_FF_BLOCKS,),
        in_specs=[_ff_spec(s)] + _ff_wspecs() + [pl.BlockSpec((s, FF_UNIT), lambda j: (0, j))],
        out_specs=(_ff_spec(s), pl.BlockSpec((2, 3, FF_UNIT), lambda j: (0, 0, j))),
        scratch_shapes=[pltpu.VMEM((s + _FF_HALO, FF_UNIT), F32), pltpu.VMEM((s + _FF_HALO, FF_UNIT), F32)],
        compiler_params=_params(("parallel",), VMEM_BIG),
    )(zf, conv_w, conv_w, da)


def _rope(v, cs, s1, s2):
    return v * cs + pltpu.roll(v, LANES - C_ROPE // 2, axis=1) * s1 + pltpu.roll(v, C_ROPE // 2, axis=1) * s2


def _rope_t(dv, cs, s1, s2):
    return dv * cs + pltpu.roll(dv * s1, C_ROPE // 2, axis=1) + pltpu.roll(dv * s2, LANES - C_ROPE // 2, axis=1)


def _kpe_mask(shape):
    lane = lax.broadcasted_iota(jnp.int32, shape, 1)
    return (lane >= C_NOPE) & (lane < C_NOPE + C_ROPE)


def _rms(v, g):
    r = lax.rsqrt(jnp.mean(v * v, axis=-1, keepdims=True) + EPS)
    return v * r, r


def _rms_bwd(dn, xh, r, g):
    dxh = dn * g
    return r * (dxh - xh * jnp.mean(dxh * xh, axis=-1, keepdims=True)), jnp.sum(dn * xh, axis=0, keepdims=True)


_ZQ = C_Q_RANK + C_KV_RANK + HEAD_PAD
_HW = C_HEADS * HEAD_PAD


_MLA_ROWS = 256


def _mla_tiles(s):
    tm = _tile(s, 2 * _MLA_ROWS, 8)
    sub = min(tm, _MLA_ROWS)
    return tm, [slice(r * sub, (r + 1) * sub) for r in range(tm // sub)]


def mla_pre_fwd(z, gq, gkv, wq, wk, wv, cs, s1, s2):
    s = z.shape[0]
    tm, streams = _mla_tiles(s)

    def body(z_ref, gq_ref, gkv_ref, wq_ref, wk_ref, wv_ref, cs_ref, s1_ref, s2_ref, q_ref, k_ref, v_ref):
        for rs in streams:
            zv = z_ref[rs, :].astype(F32)
            cst, s1t, s2t = cs_ref[rs, :], s1_ref[rs, :], s2_ref[rs, :]
            qh, _ = _rms(zv[:, :C_Q_RANK], None)
            qn = (qh * gq_ref[...]).astype(BF16)
            q = _dot(qn, wq_ref[...])
            kh, _ = _rms(zv[:, C_Q_RANK:C_Q_RANK + C_KV_RANK], None)
            kvn = (kh * gkv_ref[...]).astype(BF16)
            k = _dot(kvn, wk_ref[...])
            v_ref[rs, :] = _dot(kvn, wv_ref[...]).astype(BF16)
            kpe = _rope(zv[:, C_Q_RANK + C_KV_RANK:], cst, s1t, s2t)
            for h in range(C_HEADS):
                sl = slice(h * HEAD_PAD, (h + 1) * HEAD_PAD)
                q_ref[rs, sl] = _rope(q[:, sl], cst, s1t, s2t).astype(BF16)
                k_ref[rs, sl] = (k[:, sl] + kpe).astype(BF16)

    full = lambda r, c: pl.BlockSpec((r, c), lambda i: (0, 0))
    hw = _sds((s, _HW), BF16)
    return pl.pallas_call(
        body, name="mla_pre_fwd", out_shape=(hw, hw, hw), grid=(s // tm,),
        in_specs=[_rows(tm, _ZQ), _vec(C_Q_RANK), _vec(C_KV_RANK), full(C_Q_RANK, _HW), full(C_KV_RANK, _HW),
                  full(C_KV_RANK, _HW), _rows(tm, LANES), _rows(tm, LANES), _rows(tm, LANES)],
        out_specs=(_rows(tm, _HW), _rows(tm, _HW), _rows(tm, _HW)),
        compiler_params=_params(("parallel",), VMEM_BIG),
    )(z, gq, gkv, wq, wk, wv, cs, s1, s2)


def mla_pre_bwd(z, gq, gkv, wq, wk, wv, cs, s1, s2, dq, dk, dv):
    s = z.shape[0]
    tm, streams = _mla_tiles(s)

    def body(z_ref, gq_ref, gkv_ref, wq_ref, wk_ref, wv_ref, cs_ref, s1_ref, s2_ref, dq_ref, dk_ref, dv_ref,
             dz_ref, dwq_ref, dwk_ref, dwv_ref, dgq_ref, dgkv_ref):
        @pl.when(pl.program_id(0) == 0)
        def _():
            dwq_ref[...] = jnp.zeros_like(dwq_ref)
            dwk_ref[...] = jnp.zeros_like(dwk_ref)
            dwv_ref[...] = jnp.zeros_like(dwv_ref)
            dgq_ref[...] = jnp.zeros_like(dgq_ref)
            dgkv_ref[...] = jnp.zeros_like(dgkv_ref)

        gqv, gkvv = gq_ref[...], gkv_ref[...]
        for rs in streams:
            zv = z_ref[rs, :].astype(F32)
            cst, s1t, s2t = cs_ref[rs, :], s1_ref[rs, :], s2_ref[rs, :]
            qh, rq = _rms(zv[:, :C_Q_RANK], None)
            qn = (qh * gqv).astype(BF16)
            kh, rk = _rms(zv[:, C_Q_RANK:C_Q_RANK + C_KV_RANK], None)
            kvn = (kh * gkvv).astype(BF16)

            dqv = dq_ref[rs, :].astype(F32)
            dqp = jnp.concatenate(
                [_rope_t(dqv[:, h * HEAD_PAD:(h + 1) * HEAD_PAD], cst, s1t, s2t) for h in range(C_HEADS)], axis=1
            ).astype(BF16)
            dwq_ref[...] += _dot(qn, dqp, _TN)
            dqn = _dot(dqp, wq_ref[...], _NT)
            dql, dgq = _rms_bwd(dqn, qh, rq, gqv)
            dgq_ref[...] += dgq

            dkv = dk_ref[rs, :]
            dkb = dkv.astype(BF16)
            dvb = dv_ref[rs, :].astype(BF16)
            dwk_ref[...] += _dot(kvn, dkb, _TN)
            dwv_ref[...] += _dot(kvn, dvb, _TN)
            dkvn = _dot(dkb, wk_ref[...], _NT) + _dot(dvb, wv_ref[...], _NT)
            dkl, dgkv = _rms_bwd(dkvn, kh, rk, gkvv)
            dgkv_ref[...] += dgkv

            dkpe = dkv[:, :HEAD_PAD]
            for h in range(1, C_HEADS):
                dkpe = dkpe + dkv[:, h * HEAD_PAD:(h + 1) * HEAD_PAD]
            dkpe = _rope_t(jnp.where(_kpe_mask(dkpe.shape), dkpe, 0.0), cst, s1t, s2t)
            dz_ref[rs, :] = jnp.concatenate([dql, dkl, dkpe], axis=1).astype(BF16)

    full = lambda r, c: pl.BlockSpec((r, c), lambda i: (0, 0))
    return pl.pallas_call(
        body, name="mla_pre_bwd",
        out_shape=(_sds((s, _ZQ), BF16), _sds((C_Q_RANK, _HW)), _sds((C_KV_RANK, _HW)), _sds((C_KV_RANK, _HW)),
                   _sds((1, C_Q_RANK)), _sds((1, C_KV_RANK))),
        grid=(s // tm,),
        in_specs=[_rows(tm, _ZQ), _vec(C_Q_RANK), _vec(C_KV_RANK), full(C_Q_RANK, _HW), full(C_KV_RANK, _HW),
                  full(C_KV_RANK, _HW), _rows(tm, LANES), _rows(tm, LANES), _rows(tm, LANES),
                  _rows(tm, _HW), _rows(tm, _HW), _rows(tm, _HW)],
        out_specs=(_rows(tm, _ZQ), full(C_Q_RANK, _HW), full(C_KV_RANK, _HW), full(C_KV_RANK, _HW),
                   _vec(C_Q_RANK), _vec(C_KV_RANK)),
        compiler_params=_params(("arbitrary",), VMEM_BIG),
    )(z, gq, gkv, wq, wk, wv, cs, s1, s2, dq, dk, dv)


_ATT_SCALE = (C_NOPE + C_ROPE) ** -0.5
_NEG = -1e30


def _att_exp(q, k, row0, ends_here):
    sc = _dot(q, k, _NT) * _ATT_SCALE
    tq, nk = sc.shape
    if ends_here:
        last = sc[:, nk - tq:]
        row = lax.broadcasted_iota(jnp.int32, last.shape, 0)
        col = lax.broadcasted_iota(jnp.int32, last.shape, 1)
        last = jnp.where(col <= row, last, _NEG)
        sc = last if nk == tq else jnp.concatenate([sc[:, :nk - tq], last], axis=1)
    else:
        qpos = row0 + lax.broadcasted_iota(jnp.int32, sc.shape, 0)
        kpos = lax.broadcasted_iota(jnp.int32, sc.shape, 1)
        sc = jnp.where(kpos <= qpos, sc, _NEG)
    e = jnp.exp(sc - jnp.max(sc, axis=-1, keepdims=True))
    return e, 1.0 / jnp.sum(e, axis=-1, keepdims=True)


def _causal_cases(i, nq, tq, fn):
    if nq > 8:
        fn(nq * tq, False)
        return
    for blk in range(nq):
        pl.when(i == blk)(functools.partial(fn, (blk + 1) * tq, True))


_FWD_HEADS_PER_STEP = 4
_BWD_HEADS_PER_STEP = 2


def _head_lanes(heads):
    return [slice(h * HEAD_PAD, (h + 1) * HEAD_PAD) for h in range(heads)]


def attn_fwd(q, k, v):
    s = q.shape[0]
    tq = _tile(s, 256, 8)
    nq = s // tq
    heads = _FWD_HEADS_PER_STEP
    wide = heads * HEAD_PAD

    def body(q_ref, k_ref, v_ref, o_ref):
        i = pl.program_id(1)

        def case(nk, ends_here):
            for hd in _head_lanes(heads):
                e, inv = _att_exp(q_ref[:, hd], k_ref[:nk, hd], i * tq, ends_here)
                o_ref[:, hd] = (_dot(e.astype(BF16), v_ref[:nk, hd]) * inv).astype(BF16)

        _causal_cases(i, nq, tq, case)

    qspec = pl.BlockSpec((tq, wide), lambda h, i: (i, h))
    kspec = pl.BlockSpec((s, wide), lambda h, i: (0, h))
    return pl.pallas_call(
        body, name="attn_fwd", out_shape=_sds((s, _HW + _DW), BF16), grid=(C_HEADS // heads, s // tq),
        in_specs=[qspec, kspec, kspec], out_specs=qspec,
        compiler_params=_params(("parallel", "parallel"), VMEM_BIG),
    )(q, k, v)


def attn_bwd(q, k, v, o, do_all):
    s = q.shape[0]
    tq = _tile(s, 256, 8)
    heads = _BWD_HEADS_PER_STEP
    wide = heads * HEAD_PAD

    def body(q_ref, k_ref, v_ref, o_ref, do_ref, dq_ref, dk_ref, dv_ref):
        i = pl.program_id(1)

        @pl.when(i == 0)
        def _():
            dk_ref[...] = jnp.zeros_like(dk_ref)
            dv_ref[...] = jnp.zeros_like(dv_ref)

        def case(nk, ends_here):
            for hd in _head_lanes(heads):
                qv, kv, vv, dov = q_ref[:, hd], k_ref[:nk, hd], v_ref[:nk, hd], do_ref[:, hd]
                e, inv = _att_exp(qv, kv, i * tq, ends_here)
                p = e * inv
                dp = _dot(dov, vv, _NT)
                delta = jnp.sum(dov.astype(F32) * o_ref[:, hd].astype(F32), axis=-1, keepdims=True)
                ds = (p * (dp - delta) * _ATT_SCALE).astype(BF16)
                dq_ref[:, hd] = _dot(ds, kv).astype(BF16)
                dk_ref[:nk, hd] += _dot(ds, qv, _TN)
                dv_ref[:nk, hd] += _dot(p.astype(BF16), dov, _TN)

        _causal_cases(i, s // tq, tq, case)

    qspec = pl.BlockSpec((tq, wide), lambda h, i: (i, h))
    kspec = pl.BlockSpec((s, wide), lambda h, i: (0, h))
    return pl.pallas_call(
        body, name="attn_bwd", out_shape=(_sds((s, _HW), BF16), _sds((s, _HW)), _sds((s, _HW))),
        grid=(C_HEADS // heads, s // tq),
        in_specs=[qspec, kspec, kspec, qspec, qspec], out_specs=(qspec, kspec, kspec),
        compiler_params=_params(("parallel", "arbitrary"), VMEM_BIG),
    )(q, k, v, o, do_all)


_DW = D_GROUPS * LANES


def _tril_bf16(w):
    r = lax.broadcasted_iota(jnp.int32, w.shape, 0)
    c = lax.broadcasted_iota(jnp.int32, w.shape, 1)
    return jnp.where(c <= r, w, 0.0).astype(BF16)


def _sgu_forward(zu, zv, lg, lb, ws_ref, bs):
    u = _gelu(zu)
    v = _gelu(zv)
    mu = jnp.mean(v, axis=-1, keepdims=True)
    vc = v - mu
    rstd = lax.rsqrt(jnp.mean(vc * vc, axis=-1, keepdims=True) + EPS)
    xh = vc * rstd
    vln = (xh * lg + lb).astype(BF16)
    mixed = []
    for g in range(D_GROUPS):
        wg = _tril_bf16(ws_ref[g])
        mixed.append(_dot(wg, vln[:, g * LANES:(g + 1) * LANES]) + bs[:, g:g + 1])
    return u, xh, rstd, vln, jnp.concatenate(mixed, axis=1)


_SGU_CHUNKS = 4


def sgu_fwd(z, lg, lb, ws, bs_t, ycat):
    s = z.shape[0]
    rows = _SGU_CHUNKS * D_CHUNK

    def body(zu_ref, zv_ref, lg_ref, lb_ref, ws_ref, bs_ref, ycat_ref, o_ref):
        del ycat_ref
        for c in range(_SGU_CHUNKS):
            rs = slice(c * D_CHUNK, (c + 1) * D_CHUNK)
            u, _, _, _, mixed = _sgu_forward(zu_ref[rs, :].astype(F32), zv_ref[rs, :].astype(F32), lg_ref[...],
                                             lb_ref[...], ws_ref, bs_ref[...])
            o_ref[rs, :] = (u * mixed).astype(BF16)

    return pl.pallas_call(
        body, name="sgu_fwd", out_shape=_sds(ycat.shape, BF16), grid=(s // rows,),
        in_specs=[pl.BlockSpec((rows, _DW), lambda n: (n, 1)), pl.BlockSpec((rows, _DW), lambda n: (n, 2)),
                  _vec(_DW), _vec(_DW), pl.BlockSpec((D_GROUPS, D_CHUNK, D_CHUNK), lambda n: (0, 0, 0)),
                  pl.BlockSpec((D_CHUNK, LANES), lambda n: (0, 0)), pl.BlockSpec(memory_space=pl.ANY)],
        out_specs=pl.BlockSpec((rows, _DW), lambda n: (n, _HW // _DW)),
        input_output_aliases={6: 0},
        compiler_params=_params(("parallel",)),
    )(z, z, lg, lb, ws, bs_t, ycat)


def sgu_bwd(z, lg, lb, ws, bs_t, dycat, dy_col):
    s = z.shape[0]
    rows = _SGU_CHUNKS * D_CHUNK

    def body(zu_ref, zv_ref, lg_ref, lb_ref, ws_ref, bs_ref, dy_ref, dzu_ref, dzv_ref, dws_ref, dbs_ref, dlg_ref,
             dlb_ref):
        @pl.when(pl.program_id(0) == 0)
        def _():
            dws_ref[...] = jnp.zeros_like(dws_ref)
            dbs_ref[...] = jnp.zeros_like(dbs_ref)
            dlg_ref[...] = jnp.zeros_like(dlg_ref)
            dlb_ref[...] = jnp.zeros_like(dlb_ref)

        lg = lg_ref[...]
        lane = lax.broadcasted_iota(jnp.int32, (D_CHUNK, LANES), 1)
        row = lax.broadcasted_iota(jnp.int32, (D_CHUNK, D_CHUNK), 0)
        colm = lax.broadcasted_iota(jnp.int32, (D_CHUNK, D_CHUNK), 1)
        for c in range(_SGU_CHUNKS):
            rs = slice(c * D_CHUNK, (c + 1) * D_CHUNK)
            zu, zv = zu_ref[rs, :].astype(F32), zv_ref[rs, :].astype(F32)
            u, xh, rstd, vln, mixed = _sgu_forward(zu, zv, lg, lb_ref[...], ws_ref, bs_ref[...])
            dy = dy_ref[rs, :].astype(F32)
            dzu_ref[rs, :] = (dy * mixed * _gelu_grad(zu)).astype(BF16)
            dmix = dy * u
            dvln = []
            dbs = jnp.zeros((D_CHUNK, LANES), F32)
            for g in range(D_GROUPS):
                sl = slice(g * LANES, (g + 1) * LANES)
                dmg = dmix[:, sl]
                dbs = dbs + jnp.where(lane == g, jnp.sum(dmg, axis=-1, keepdims=True), 0.0)
                dmb = dmg.astype(BF16)
                dws_ref[g] += jnp.where(colm <= row, _dot(dmb, vln[:, sl], _NT), 0.0)
                dvln.append(_dot(_tril_bf16(ws_ref[g]), dmb, _TN))
            dbs_ref[...] += dbs
            dvln = jnp.concatenate(dvln, axis=1)
            dlg_ref[...] += jnp.sum(dvln * xh, axis=0, keepdims=True)
            dlb_ref[...] += jnp.sum(dvln, axis=0, keepdims=True)
            dxh = dvln * lg
            dvv = rstd * (dxh - jnp.mean(dxh, axis=-1, keepdims=True)
                          - xh * jnp.mean(dxh * xh, axis=-1, keepdims=True))
            dzv_ref[rs, :] = (dvv * _gelu_grad(zv)).astype(BF16)

    wsspec = pl.BlockSpec((D_GROUPS, D_CHUNK, D_CHUNK), lambda n: (0, 0, 0))
    chunk = lambda cidx: pl.BlockSpec((rows, _DW), lambda n: (n, cidx))
    return pl.pallas_call(
        body, name="sgu_bwd",
        out_shape=(_sds((s, _DW), BF16), _sds((s, _DW), BF16), _sds((D_GROUPS, D_CHUNK, D_CHUNK)),
                   _sds((D_CHUNK, LANES)), _sds((1, _DW)), _sds((1, _DW))),
        grid=(s // rows,),
        in_specs=[chunk(1), chunk(2), _vec(_DW), _vec(_DW), wsspec, pl.BlockSpec((D_CHUNK, LANES), lambda n: (0, 0)),
                  chunk(dy_col)],
        out_specs=(chunk(0), chunk(0), wsspec, pl.BlockSpec((D_CHUNK, LANES), lambda n: (0, 0)), _vec(_DW), _vec(_DW)),
        compiler_params=_params(("arbitrary",)),
    )(z, z, lg, lb, ws, bs_t, dycat)


def ada_mod(c_all, ada_w, ada_b):
    nl, d, n = ada_w.shape
    nb = c_all.shape[0]
    tn = _tile(n, 512)

    def body(c_ref, w_ref, b_ref, o_ref):
        cv = c_ref[...]
        ca = (cv * _sigmoid(cv)).astype(BF16)
        o_ref[...] = _dot(ca, w_ref[...].astype(BF16)) + b_ref[...]

    return pl.pallas_call(
        body, name="ada_mod", out_shape=_sds((nl, nb, n)), grid=(nl, n // tn),
        in_specs=[pl.BlockSpec((nb, d), lambda l, j: (0, 0)), pl.BlockSpec((None, d, tn), lambda l, j: (l, 0, j)),
                  pl.BlockSpec((None, 1, tn), lambda l, j: (l, 0, j))],
        out_specs=pl.BlockSpec((None, nb, tn), lambda l, j: (l, 0, j)),
        compiler_params=_params(("parallel", "parallel")),
    )(c_all, ada_w, ada_b.reshape(nl, 1, n))


_ADAM_BLOCK = 256 * 1024


def _adam_rows(rows, cols):
    if rows * cols <= _ADAM_BLOCK or rows % 8:
        return rows
    return _tile(rows, max(8, _ADAM_BLOCK // cols), 8)


def _adam_update(w, gv, m, v):
    inv_bc1 = 1.0 / (1.0 - ADAM_B1 ** ADAM_STEP)
    inv_bc2 = 1.0 / (1.0 - ADAM_B2 ** ADAM_STEP)
    nm = ADAM_B1 * m + (1.0 - ADAM_B1) * gv
    nv = ADAM_B2 * v + (1.0 - ADAM_B2) * (gv * gv)
    return -ADAM_LR * ((nm * inv_bc1) / (jnp.sqrt(nv * inv_bc2) + ADAM_EPS) + ADAM_WD * w), nm, nv


def adamw(w, g, m, v, name):
    shape = w.shape
    cols = shape[-1]
    rows = w.size // cols
    tr = _adam_rows(rows, cols)

    def body(w_ref, g_ref, m_ref, v_ref, go_ref, d_ref, nm_ref, nv_ref):
        gv = g_ref[...]
        go_ref[...] = gv
        d_ref[...], nm_ref[...], nv_ref[...] = _adam_update(w_ref[...], gv, m_ref[...], v_ref[...])

    spec = pl.BlockSpec((tr, cols), lambda i: (i, 0))
    out = _sds((rows, cols))
    r2 = lambda t: t.reshape(rows, cols)
    res = pl.pallas_call(
        body, name=name, out_shape=(out,) * 4, grid=(rows // tr,),
        in_specs=[spec] * 4, out_specs=(spec,) * 4, compiler_params=_params(("parallel",)),
    )(r2(w), r2(g), r2(m), r2(v))
    return tuple(t.reshape(shape) for t in res)


def adamw_ada(w, c_all, dmod, m, v):
    nl, d, n = w.shape
    tr = _adam_rows(d, n)
    pad = 16 - c_all.shape[0]
    c16 = jnp.pad(c_all, ((0, pad), (0, 0)))
    dm16 = jnp.pad(dmod, ((0, 0), (0, pad), (0, 0)))

    def body(w_ref, c_ref, dm_ref, m_ref, v_ref, g_ref, d_ref, nm_ref, nv_ref):
        cv = c_ref[...]
        gv = _dot((cv * _sigmoid(cv)).astype(BF16), dm_ref[...].astype(BF16), _TN)
        g_ref[...] = gv
        d_ref[...], nm_ref[...], nv_ref[...] = _adam_update(w_ref[...], gv, m_ref[...], v_ref[...])

    spec = pl.BlockSpec((None, tr, n), lambda l, i: (l, i, 0))
    out = _sds((nl, d, n))
    return pl.pallas_call(
        body, name="adamw_ada_w", out_shape=(out, out, out, out), grid=(nl, d // tr),
        in_specs=[spec, pl.BlockSpec((16, tr), lambda l, i: (0, i)), pl.BlockSpec((None, 16, n), lambda l, i: (l, 0, 0)),
                  spec, spec],
        out_specs=(spec,) * 4, compiler_params=_params(("parallel", "parallel")),
    )(w, c16, dm16, m, v)


def adamw_small(ws, gs, ms, vs):
    n = len(ws)
    flat = lambda t: t.reshape(-1, t.shape[-1])

    def body(*refs):
        ins, outs = refs[:4 * n], refs[4 * n:]
        for i in range(n):
            w_ref, g_ref, m_ref, v_ref = ins[4 * i:4 * i + 4]
            outs[3 * i][...], outs[3 * i + 1][...], outs[3 * i + 2][...] = _adam_update(
                w_ref[...], g_ref[...], m_ref[...], v_ref[...])

    operands = [flat(t) for quad in zip(ws, gs, ms, vs) for t in quad]
    res = pl.pallas_call(
        body, name="adamw_small", out_shape=tuple(_sds(flat(w).shape) for w in ws for _ in range(3)),
    )(*operands)
    return [(g, res[3 * i].reshape(w.shape), res[3 * i + 1].reshape(w.shape), res[3 * i + 2].reshape(w.shape))
            for i, (w, g) in enumerate(zip(ws, gs))]


def adamw_layers(w, g0, g1, m, v, name):
    _, rows, cols = w.shape
    tr = _adam_rows(rows, cols)

    def body(w_ref, g0_ref, g1_ref, m_ref, v_ref, g_ref, d_ref, nm_ref, nv_ref):
        gv = jnp.where(pl.program_id(0) == 0, g0_ref[...], g1_ref[...])
        g_ref[...] = gv
        d_ref[...], nm_ref[...], nv_ref[...] = _adam_update(w_ref[...], gv, m_ref[...], v_ref[...])

    spec = pl.BlockSpec((None, tr, cols), lambda l, i: (l, i, 0))
    gspec = pl.BlockSpec((tr, cols), lambda l, i: (i, 0))
    out = _sds((2, rows, cols))
    return pl.pallas_call(
        body, name=name, out_shape=(out, out, out, out), grid=(2, rows // tr),
        in_specs=[spec, gspec, gspec, spec, spec], out_specs=(spec,) * 4, compiler_params=_params(("parallel", "parallel")),
    )(w, g0, g1, m, v)


def sum8(gathered):
    _, r, _ = gathered.shape
    tr = _tile(r, 512, 8)

    def body(g_ref, o_ref):
        acc = g_ref[0]
        for dev in range(1, N_DEV):
            acc = acc + g_ref[dev]
        o_ref[...] = acc

    return pl.pallas_call(
        body, name="sum8", out_shape=_sds((r, LANES)), grid=(r // tr,),
        in_specs=[pl.BlockSpec((N_DEV, tr, LANES), lambda i: (0, i, 0))], out_specs=pl.BlockSpec((tr, LANES), lambda i: (i, 0)),
        compiler_params=_params(("parallel",)),
    )(gathered)


_SUM_STEPS = 2


def pair_sums(gs, recvs, core, name):
    n = len(gs)
    trs = [g.shape[1] // 2 // _SUM_STEPS for g in gs]

    def body(c_ref, *refs):
        del c_ref
        for i in range(n):
            a_ref, b_ref, o_ref = refs[2 * i], refs[2 * i + 1], refs[2 * n + i]
            o_ref[...] = (a_ref[...].astype(F32) + b_ref[...].astype(F32)).astype(BF16)

    in_specs, out_specs = [], []
    for g, tr in zip(gs, trs):
        cols = g.shape[2]
        in_specs.append(pl.BlockSpec((None, tr, cols), lambda k, s, c: (k, c[0] * _SUM_STEPS + s, 0)))
        in_specs.append(pl.BlockSpec((None, tr, cols), lambda k, s, c: (k, s, 0)))
        out_specs.append(pl.BlockSpec((None, tr, cols), lambda k, s, c: (k, s, 0)))
    grid_spec = pltpu.PrefetchScalarGridSpec(num_scalar_prefetch=1, grid=(N_CHIPS, _SUM_STEPS), in_specs=in_specs,
                                             out_specs=tuple(out_specs))
    return list(pl.pallas_call(
        body, name=name, out_shape=tuple(_sds((N_CHIPS, g.shape[1] // 2, g.shape[2]), BF16) for g in gs),
        grid_spec=grid_spec, compiler_params=_params(("parallel", "parallel")),
    )(core.reshape(1).astype(jnp.int32), *[t for pair in zip(gs, recvs) for t in pair]))


def chip_sums(pairs, recvs, chip, core, name):
    n = len(pairs)
    trs = [p.shape[1] // _SUM_STEPS for p in pairs]

    def body(p_ref, *refs):
        del p_ref
        for i in range(n):
            own_ref, r_ref, o_ref = refs[2 * i], refs[2 * i + 1], refs[2 * n + i]
            acc = own_ref[...].astype(F32)
            for j in range(N_CHIPS - 1):
                acc = acc + r_ref[j].astype(F32)
            o_ref[...] = acc

    in_specs, out_specs = [], []
    for p, tr in zip(pairs, trs):
        cols = p.shape[2]
        in_specs.append(pl.BlockSpec((None, tr, cols), lambda s, q: (q[0], s, 0)))
        in_specs.append(pl.BlockSpec((N_CHIPS - 1, tr, cols), lambda s, q: (0, s, 0)))
        out_specs.append(pl.BlockSpec((None, tr, cols), lambda s, q: (q[1], s, 0)))
    grid_spec = pltpu.PrefetchScalarGridSpec(num_scalar_prefetch=1, grid=(_SUM_STEPS,), in_specs=in_specs,
                                             out_specs=tuple(out_specs))
    return list(pl.pallas_call(
        body, name=name, out_shape=tuple(_sds((2,) + p.shape[1:]) for p in pairs), grid_spec=grid_spec,
        compiler_params=_params(("parallel",)),
    )(jnp.stack([chip, core]).astype(jnp.int32), *[t for pair in zip(pairs, recvs) for t in pair]))


def _place():
    return lax.axis_index("x"), lax.axis_index("y"), lax.axis_index("c")


def _other_chips(x, y):
    return [(x, 1 - y), (1 - x, y), (1 - x, 1 - y)]


_HBM = pl.BlockSpec(memory_space=pltpu.HBM)


def all_gather8(v, name, after=()):
    m, n = v.shape

    def body(x_ref, *refs):
        out_ref, send_sems, recv_sems, local_sem = refs[len(after):]
        x, y, c = _place()
        me, sibling = (x, y, c), (x, y, 1 - c)
        chips = _other_chips(x, y)

        def rows(px, py, pc):
            return out_ref.at[pl.ds((4 * px + 2 * py + pc) * m, m), :]

        def copy(k, block, to, src=None):
            return pltpu.make_async_remote_copy(
                src_ref=rows(*block) if src is None else src, dst_ref=rows(*block),
                send_sem=send_sems.at[k], recv_sem=recv_sems.at[k], device_id=to, device_id_type=MESH)

        mine = pltpu.make_async_copy(x_ref, rows(*me), local_sem)
        mine.start()
        first = [copy(0, me, sibling, src=x_ref)]
        first += [copy(1 + j, me, (*chip, c), src=x_ref) for j, chip in enumerate(chips)]
        for cp in first:
            cp.start()
        passed = [copy(4 + j, (*chip, c), sibling) for j, chip in enumerate(chips)]
        for j, chip in enumerate(chips):
            copy(1 + j, (*chip, c), me).wait_recv()
            passed[j].start()
        copy(0, sibling, me).wait_recv()
        for j, chip in enumerate(chips):
            copy(4 + j, (*chip, 1 - c), me).wait_recv()
        for cp in first + passed:
            cp.wait_send()
        mine.wait()

    return pl.pallas_call(
        body, name=name, out_shape=_sds((N_DEV * m, n), v.dtype),
        in_specs=[pl.BlockSpec(memory_space=pltpu.VMEM)] + [pl.BlockSpec(memory_space=pl.ANY)] * len(after),
        out_specs=pl.BlockSpec(memory_space=pltpu.VMEM),
        scratch_shapes=[pltpu.SemaphoreType.DMA((7,)), pltpu.SemaphoreType.DMA((7,)), pltpu.SemaphoreType.DMA],
        compiler_params=_params(None, VMEM_BIG),
    )(v, *after)


def _comm_call(body, name, ins, out_shapes, nsem, aliases=None):
    return pl.pallas_call(
        body, name=name, out_shape=tuple(out_shapes), in_specs=[_HBM] * len(ins), out_specs=tuple([_HBM] * len(out_shapes)),
        scratch_shapes=[pltpu.SemaphoreType.DMA((nsem,)), pltpu.SemaphoreType.DMA((nsem,))],
        input_output_aliases=aliases or {},
    )(*ins)


def _remote(src, dst, send_sems, recv_sems, k, to):
    return pltpu.make_async_remote_copy(src_ref=src, dst_ref=dst, send_sem=send_sems.at[k], recv_sem=recv_sems.at[k],
                                        device_id=to, device_id_type=MESH)


def _half(core, rh):
    return pl.ds(pl.multiple_of(core * rh, 16), rh)


def swap_halves(gs, name):
    n = len(gs)

    def body(*refs):
        ins, outs, (send_sems, recv_sems) = refs[:n], refs[n:2 * n], refs[2 * n:]
        x, y, c = _place()
        copies = []
        for i in range(n):
            theirs = _half(1 - c, ins[i].shape[1] // 2)
            cp = _remote(ins[i].at[:, theirs], outs[i], send_sems, recv_sems, i, (x, y, 1 - c))
            cp.start()
            copies.append(cp)
        for cp in copies:
            cp.wait()

    return _comm_call(body, name, gs, [_sds((g.shape[0], g.shape[1] // 2, g.shape[2]), g.dtype) for g in gs], n)


def join_halves(bufs, name):
    n = len(bufs)

    def body(*refs):
        ins, outs, (send_sems, recv_sems) = refs[:n], refs[n:2 * n], refs[2 * n:]
        x, y, c = _place()
        copies = []
        for i in range(n):
            cp = _remote(ins[i].at[c], outs[i].at[c], send_sems, recv_sems, i, (x, y, 1 - c))
            cp.start()
            copies.append(cp)
        for i in range(n):
            theirs = outs[i].at[1 - c]
            _remote(theirs, theirs, send_sems, recv_sems, i, (x, y, 1 - c)).wait_recv()
        for cp in copies:
            cp.wait_send()

    return _comm_call(body, name, bufs, [_sds(b.shape, b.dtype) for b in bufs], n, {i: i for i in range(n)})


def forward_halves(lands, name):
    n = len(lands)

    def body(*refs):
        ins, outs, (send_sems, recv_sems) = refs[:n], refs[n:2 * n], refs[2 * n:]
        x, y, c = _place()
        sibling = (x, y, 1 - c)
        chips = _other_chips(x, y)
        copies = []
        for i in range(n):
            mine = _half(c, ins[i].shape[1] // 2)
            for j, (px, py) in enumerate(chips):
                cp = _remote(ins[i].at[2 * px + py, mine], outs[i].at[2 * px + py, mine], send_sems, recv_sems, 3 * i + j, sibling)
                cp.start()
                copies.append(cp)
        for i in range(n):
            theirs = _half(1 - c, ins[i].shape[1] // 2)
            for j, (px, py) in enumerate(chips):
                landed = outs[i].at[2 * px + py, theirs]
                _remote(landed, landed, send_sems, recv_sems, 3 * i + j, sibling).wait_recv()
        for cp in copies:
            cp.wait_send()

    return _comm_call(body, name, lands, [_sds(b.shape, b.dtype) for b in lands], 3 * n, {i: i for i in range(n)})


_SEM = pl.BlockSpec(memory_space=pltpu.SEMAPHORE)
_EFFECT = pltpu.SideEffectType.DATAFLOW_SIDE_EFFECTING


def _gather_copies(srcs, lands, send_sems, recv_sems):
    x, y, c = _place()
    copies = []
    for i in range(len(srcs)):
        mine = _half(c, srcs[i].shape[0] // 2)
        for j, chip in enumerate(_other_chips(x, y)):
            copies.append(_remote(srcs[i].at[mine], lands[i].at[2 * x + y, mine], send_sems, recv_sems, 3 * i + j, (*chip, c)))
    return copies


def _exchange_copies(srcs, lands, send_sems, recv_sems):
    x, y, c = _place()
    copies = []
    for i in range(len(srcs)):
        for j, (px, py) in enumerate(_other_chips(x, y)):
            copies.append(_remote(srcs[i].at[2 * px + py], lands[i].at[j], send_sems, recv_sems, 3 * i + j, (px, py, c)))
    return copies


def _everyone_copies(srcs, lands, send_sems, recv_sems):
    x, y, c = _place()
    flip = lambda v, b: 1 - v if b else v
    dst = lands[0].at[4 * x + 2 * y + c]
    return [_remote(srcs[0], dst, send_sems, recv_sems, j - 1, (flip(x, j & 4), flip(y, j & 2), flip(c, j & 1)))
            for j in range(1, N_DEV)]


GATHER = (_gather_copies, 3)
EXCHANGE = (_exchange_copies, 3)
EVERYONE = (_everyone_copies, N_DEV - 1)


def split_start(name, plan, srcs, land_shapes, after=()):
    copies_fn, per_source = plan
    n, m, k = len(srcs), len(land_shapes), len(after)
    ncopies = per_source * n

    def body(*refs):
        src_refs, land_refs = refs[:n], refs[n:n + m]
        send_sems, recv_sems = refs[n + m + k], refs[n + m + k + 1]
        token = refs[-1]
        for cp in copies_fn(src_refs, land_refs, send_sems, recv_sems):
            cp.start()
        token[...] = jnp.zeros_like(token)

    hbm = lambda s: pltpu.HBM(tuple(s.shape), s.dtype)
    outs = pl.pallas_call(
        body, name=name,
        out_shape=(pltpu.SemaphoreType.DMA((ncopies,)), pltpu.SemaphoreType.DMA((ncopies,)), *[hbm(s) for s in srcs],
                   *[hbm(s) for s in land_shapes], _sds((8, LANES))),
        in_specs=[_HBM] * (n + m) + [pl.BlockSpec(memory_space=pl.ANY)] * k,
        out_specs=(_SEM, _SEM, *([_HBM] * (n + m)), pl.BlockSpec(memory_space=pltpu.VMEM)),
        input_output_aliases={i: 2 + i for i in range(n + m)},
        compiler_params=pltpu.CompilerParams(has_side_effects=_EFFECT),
    )(*[pltpu.with_memory_space_constraint(s, pltpu.HBM) for s in srcs],
      *[pltpu.with_memory_space_constraint(lax.empty(tuple(s.shape), s.dtype), pltpu.HBM) for s in land_shapes], *after)
    handle = (outs[0], outs[1], list(outs[2:2 + n]), list(outs[2 + n:2 + n + m]))
    return handle, outs[-1][0, 0]


def split_wait(name, plan, handle, after):
    copies_fn, _ = plan
    send_sems, recv_sems, srcs, lands = handle
    n, m = len(srcs), len(lands)
    after = list(after) if isinstance(after, (list, tuple)) else [after]

    def body(*refs):
        src_refs, land_refs = refs[:n], refs[n:n + m]
        for cp in copies_fn(src_refs, land_refs, refs[n + m], refs[n + m + 1]):
            cp.wait_send()
            cp.wait_recv()

    hbm = lambda s: pltpu.HBM(tuple(s.shape), s.dtype)
    outs = pl.pallas_call(
        body, name=name, out_shape=tuple(hbm(s) for s in srcs + lands),
        in_specs=[_HBM] * (n + m) + [_SEM, _SEM] + [pl.BlockSpec(memory_space=pl.ANY)] * len(after),
        out_specs=tuple([_HBM] * (n + m)), input_output_aliases={i: i for i in range(n + m)},
        compiler_params=pltpu.CompilerParams(has_side_effects=_EFFECT),
    )(*srcs, *lands, send_sems, recv_sems, *after)
    return list(outs[:n]), list(outs[n:])


def chip_major(w, groups=N_CHIPS):
    r, c = w.shape
    return w.reshape(r, groups, c // groups).transpose(1, 0, 2)


def from_chip_major(w):
    g, r, c = w.shape
    return w.transpose(1, 0, 2).reshape(r, g * c)


def _cd_in_pad(w):
    a = C_Q_RANK + C_KV_RANK
    z = lambda n: jnp.zeros((w.shape[0], n), w.dtype)
    return jnp.concatenate([w[:, :a], z(C_NOPE), w[:, a:a + C_ROPE], z(HEAD_PAD - C_NOPE - C_ROPE), w[:, a + C_ROPE:]], axis=1)


def _cd_in_unpad(w):
    a = C_Q_RANK + C_KV_RANK
    return jnp.concatenate([w[:, :a], w[:, a + C_NOPE:a + C_NOPE + C_ROPE], w[:, a + HEAD_PAD:]], axis=1)


def _pad_heads(w, width):
    r = w.shape[0]
    w = w.reshape(r, C_HEADS, width)
    return jnp.pad(w, ((0, 0), (0, 0), (0, HEAD_PAD - width))).reshape(r, _HW)


def _unpad_heads(w, width):
    r = w.shape[0]
    return w.reshape(r, C_HEADS, HEAD_PAD)[:, :, :width].reshape(r, C_HEADS * width)


def prepare_weights(p):
    q = dict(p)
    q["cd_w_in"] = _cd_in_pad(p["cd_w_in"])
    q["c_w_uq"] = _pad_heads(p["c_w_uq"], C_NOPE + C_ROPE)
    ukv = p["c_w_ukv"].reshape(C_KV_RANK, C_HEADS, C_NOPE + C_V)
    q["c_w_uk"] = _pad_heads(ukv[:, :, :C_NOPE].reshape(C_KV_RANK, -1), C_NOPE)
    q["c_w_uv"] = _pad_heads(ukv[:, :, C_NOPE:].reshape(C_KV_RANK, -1), C_V)
    wo = p["cd_w_out"]
    att_rows = jnp.pad(wo[:C_HEADS * C_V].reshape(C_HEADS, C_V, D_MODEL), ((0, 0), (0, HEAD_PAD - C_V), (0, 0)))
    q["cd_w_out"] = jnp.concatenate([att_rows.reshape(_HW, D_MODEL), wo[C_HEADS * C_V:]], axis=0)
    return q


def unprepare_grads(g):
    q = dict(g)
    q["cd_w_in"] = _cd_in_unpad(g["cd_w_in"])
    q["c_w_uq"] = _unpad_heads(g["c_w_uq"], C_NOPE + C_ROPE)
    uk = g.pop("c_w_uk").reshape(C_KV_RANK, C_HEADS, HEAD_PAD)[:, :, :C_NOPE]
    uv = g.pop("c_w_uv").reshape(C_KV_RANK, C_HEADS, HEAD_PAD)[:, :, :C_V]
    q.pop("c_w_uk", None)
    q.pop("c_w_uv", None)
    q["c_w_ukv"] = jnp.concatenate([uk, uv], axis=-1).reshape(C_KV_RANK, C_HEADS * (C_NOPE + C_V))
    wo = g["cd_w_out"]
    att = wo[:_HW].reshape(C_HEADS, HEAD_PAD, D_MODEL)[:, :C_V].reshape(C_HEADS * C_V, D_MODEL)
    q["cd_w_out"] = jnp.concatenate([att, wo[_HW:]], axis=0)
    return q


def rope_tables(positions):
    half = C_ROPE // 2
    inv_freq = ROPE_THETA ** (-jnp.arange(half, dtype=F32) / half)
    ang = positions.astype(F32)[:, None] * inv_freq
    cos, sin = jnp.cos(ang), jnp.sin(ang)
    s = positions.shape[0]
    z = lambda n: jnp.zeros((s, n), F32)
    cs = jnp.concatenate([jnp.ones((s, C_NOPE), F32), cos, cos, z(HEAD_PAD - C_NOPE - C_ROPE)], axis=1)
    s1 = jnp.concatenate([z(C_NOPE), -sin, z(HEAD_PAD - C_NOPE - half)], axis=1)
    s2 = jnp.concatenate([z(C_NOPE + half), sin, z(HEAD_PAD - C_NOPE - C_ROPE)], axis=1)
    return cs, s1, s2


_UP_COLS = 2 * D_FF // N_CHIPS


def ffn_fwd(h2, w, l, late_down=None):
    zf = matmul(h2, w["ffn_w_up"][l], "nn", BF16, f"ffn_up{l}", gb=N_CHIPS, go=2, tn=_UP_COLS)
    if late_down is not None:
        late_down(zf)
    a, f = ffn_act_down(zf, w["ffn_conv_w"][l], w["ffn_w_down"][l], f"ffn_act_down{l}")
    return f, (zf, a)


def ffn_bwd(df, h2, saved, w, l):
    zf, a = saved
    da = matmul(df, w["ffn_w_down"][l], "nt", BF16, f"ffn_down_dx{l}", tn=D_FF // 2)
    d_down = matmul(a, df, "tn", BF16, f"ffn_down_dw{l}", tm=D_FF // 2)
    dzf, d_conv = ffn_act_bwd(zf, w["ffn_conv_w"][l], da, f"ffn_act_bwd{l}")
    dh2 = matmul(dzf, w["ffn_w_up"][l], "nt", BF16, f"ffn_up_dx{l}", ga=2, gb=N_CHIPS, tk=_UP_COLS, tn=D_MODEL)
    d_up = matmul(h2, dzf, "tn", BF16, f"ffn_up_dw{l}", gb=2, go=N_CHIPS, tn=_UP_COLS)
    d_conv = d_conv.transpose(1, 0, 2).reshape(3, 2 * D_FF)
    return dh2, dict(ffn_w_down=d_down, ffn_conv_w=d_conv, ffn_w_up=d_up)


def mixer0_fwd(h, w):
    z = matmul(h, w["ab_w_in"], "nn", BF16, "ab_in", gb=N_CHIPS)
    ycat = pool_fwd(z, w["b_mix_w"], w["b_scale"], gconv_fwd(z, w["a_conv_w"]))
    y = matmul(ycat, w["ab_w_out"], "nn", BF16, "ab_out", tn=D_MODEL)
    return y, (z, ycat)


def mixer0_bwd(dy, h, saved, w):
    z, ycat = saved
    grads = {}
    dycat = matmul(dy, w["ab_w_out"], "nt", BF16, "ab_out_dx")
    grads["ab_w_out"] = matmul(ycat, dy, "tn", BF16, "ab_out_dw")
    db, dc, da, d_conv = gconv_bwd(z, w["a_conv_w"], dycat)
    dp, d_mix, d_scale = pool_bwd(z, w["b_mix_w"], w["b_scale"], dycat)
    dz = jnp.concatenate([db, dc, da, dp], axis=1)
    dh = matmul(dz, w["ab_w_in"], "nt", BF16, "ab_in_dx", gb=N_CHIPS, tn=D_MODEL)
    grads["ab_w_in"] = matmul(h, dz, "tn", BF16, "ab_in_dw", go=N_CHIPS)
    grads.update(a_conv_w=d_conv, b_mix_w=d_mix, b_scale=d_scale)
    return dh, grads


def mixer1_fwd(h, ropes, w):
    cs, s1, s2 = ropes
    z = matmul(h, w["cd_w_in"], "nn", BF16, "cd_in")
    bs_t = jnp.pad(w["d_b_s"].T, ((0, 0), (0, LANES - D_GROUPS)))
    qh, kh, vh = mla_pre_fwd(z, w["c_q_norm_g"], w["c_kv_norm_g"], w["c_w_uq"], w["c_w_uk"], w["c_w_uv"], cs, s1, s2)
    ycat = sgu_fwd(z, w["d_ln_g"], w["d_ln_b"], w["d_w_s"], bs_t, attn_fwd(qh, kh, vh))
    y = matmul(ycat, w["cd_w_out"], "nn", BF16, "cd_out", tn=D_MODEL)
    return y, (z, bs_t, qh, kh, vh, ycat)


def mixer1_bwd(dy, h, saved, ropes, w):
    cs, s1, s2 = ropes
    z, bs_t, qh, kh, vh, ycat = saved
    grads = {}
    dycat = matmul(dy, w["cd_w_out"], "nt", BF16, "cd_out_dx")
    grads["cd_w_out"] = matmul(ycat, dy, "tn", BF16, "cd_out_dw")
    dqh, dkh, dvh = attn_bwd(qh, kh, vh, ycat, dycat)
    dzq, d_uq, d_uk, d_uv, d_gq, d_gkv = mla_pre_bwd(
        z, w["c_q_norm_g"], w["c_kv_norm_g"], w["c_w_uq"], w["c_w_uk"], w["c_w_uv"], cs, s1, s2, dqh, dkh, dvh)
    dzu, dzv, d_ws, d_bs, d_lg, d_lb = sgu_bwd(z, w["d_ln_g"], w["d_ln_b"], w["d_w_s"], bs_t, dycat, _HW // _DW)
    dz = jnp.concatenate([dzq, dzu, dzv], axis=1)
    dh = matmul(dz, w["cd_w_in"], "nt", BF16, "cd_in_dx", tn=D_MODEL)
    grads["cd_w_in"] = matmul(h, dz, "tn", BF16, "cd_in_dw")
    grads.update(c_w_uq=d_uq, c_w_uk=d_uk, c_w_uv=d_uv, c_q_norm_g=d_gq, c_kv_norm_g=d_gkv, d_w_s=d_ws,
                 d_b_s=d_bs[:, :D_GROUPS].T, d_ln_g=d_lg, d_ln_b=d_lb)
    return dh, grads


class StepHooks:
    def weights(self, stage, after):
        pass

    def gradients(self, stage, grads, after):
        return 0.0


def run_step(x, tgt, mod, ropes, w, hooks):
    sh1, sc1, g1, sh2, sc2, g2 = range(N_MOD)
    mods = mod.reshape(2, 1, N_MOD * D_MODEL)
    n1 = w["norm1_g"].reshape(2, 1, D_MODEL)
    n2 = w["norm2_g"].reshape(2, 1, D_MODEL)
    final_g = Vec(w["final_norm_g"].reshape(1, 1, D_MODEL), 0, 0)

    hooks.weights("mix0", mod)
    h0 = modnorm_fwd(x, Vec(n1, 0, 0), Vec(mods, 0, sc1), Vec(mods, 0, sh1), "modnorm_0")
    y0, mix0 = mixer0_fwd(h0, w)
    x1, h1 = resid_modnorm_fwd(x, y0, Vec(mods, 0, g1), Vec(n2, 0, 0), Vec(mods, 0, sc2), Vec(mods, 0, sh2), "resid_modnorm_1")
    hooks.weights("up0", x1)
    f0, ffn0 = ffn_fwd(h1, w, 0, lambda act: hooks.weights("down0", act))
    x2, h2 = resid_modnorm_fwd(x1, f0, Vec(mods, 0, g2), Vec(n1, 1, 0), Vec(mods, 1, sc1), Vec(mods, 1, sh1), "resid_modnorm_2")
    hooks.weights("mix1", x2)
    y1, mix1 = mixer1_fwd(h2, ropes, w)
    x3, h3 = resid_modnorm_fwd(x2, y1, Vec(mods, 1, g1), Vec(n2, 1, 0), Vec(mods, 1, sc2), Vec(mods, 1, sh2), "resid_modnorm_3")
    hooks.weights("ffn1", x3)
    f1, ffn1 = ffn_fwd(h3, w, 1)
    dres, d_final, loss, df1, dg2b = final_fused(x3, f1, Vec(mods, 1, g2), final_g, tgt)

    dh3, gf1 = ffn_bwd(df1, h3, ffn1, w, 1)
    late = mods + hooks.gradients("ffn1", gf1, dh3)
    dres, dsh2b, dsc2b, dn2b, dy1, dg1b = norm_gate_bwd(
        x3, dh3, Vec(n2, 1, 0), Vec(late, 1, sc2), dres, y1, Vec(late, 1, g1), "norm_gate_bwd_3")
    dh2, gm1 = mixer1_bwd(dy1, h2, mix1, ropes, w)
    late = mods + hooks.gradients("mix1", gm1, dh2)
    dres, dsh1b, dsc1b, dn1b, df0, dg2a = norm_gate_bwd(
        x2, dh2, Vec(n1, 1, 0), Vec(late, 1, sc1), dres, f0, Vec(late, 0, g2), "norm_gate_bwd_2")
    dh1, gf0 = ffn_bwd(df0, h1, ffn0, w, 0)
    late = mods + hooks.gradients("ffn0", gf0, dh1)
    dres, dsh2a, dsc2a, dn2a, dy0, dg1a = norm_gate_bwd(
        x1, dh1, Vec(n2, 0, 0), Vec(late, 0, sc2), dres, y0, Vec(late, 0, g1), "norm_gate_bwd_1")
    dh0, gm0 = mixer0_bwd(dy0, h0, mix0, w)
    late = mods + hooks.gradients("mix0", gm0, dh0)
    grad_x, dsh1a, dsc1a, dn1a = norm_bwd(x, dh0, Vec(n1, 0, 0), Vec(late, 0, sc1), dres, "norm_bwd_0")

    dmod = jnp.concatenate([jnp.concatenate([dsh1a, dsc1a, dg1a, dsh2a, dsc2a, dg2a], axis=1),
                            jnp.concatenate([dsh1b, dsc1b, dg1b, dsh2b, dsc2b, dg2b], axis=1)], axis=0)
    norms = dict(norm1_g=jnp.concatenate([dn1a, dn1b], axis=0), norm2_g=jnp.concatenate([dn2a, dn2b], axis=0),
                 final_norm_g=d_final)
    return loss, grad_x, dmod, dict(mix0=gm0, ffn0=gf0, mix1=gm1, ffn1=gf1, norms=norms)


def merge_grads(by_stage):
    grads = {**by_stage["mix0"], **by_stage["mix1"], **by_stage["norms"]}
    for k in ("ffn_w_down", "ffn_w_up"):
        grads[k] = [by_stage["ffn0"][k], by_stage["ffn1"][k]]
    grads["ffn_conv_w"] = jnp.stack([by_stage["ffn0"]["ffn_conv_w"], by_stage["ffn1"]["ffn_conv_w"]])
    return grads


_WEIGHTS = ("ada_w", "ada_b", "norm1_g", "norm2_g", "ab_w_in", "a_conv_w", "b_mix_w", "b_scale", "ab_w_out", "cd_w_in",
            "c_q_norm_g", "c_w_uq", "c_kv_norm_g", "c_w_ukv", "d_ln_g", "d_ln_b", "d_w_s", "d_b_s", "cd_w_out",
            "ffn_w_up", "ffn_conv_w", "ffn_w_down", "final_norm_g")
_INPUTS = ("x", "c", "positions") + _WEIGHTS + ("loss_target",) + tuple("m_" + n for n in _WEIGHTS) + tuple(
    "v_" + n for n in _WEIGHTS)

def _pack_rows(parts, rows, dtype):
    flat = jnp.concatenate([p.reshape(-1).astype(dtype) for p in parts])
    return jnp.pad(flat, (0, rows * LANES - flat.shape[0])).reshape(rows, LANES)


def _rows_major(w):
    r, c = w.shape
    return w.reshape(N_CHIPS, r // N_CHIPS, c)


def start_gather(shards, tag, after=()):
    lands = [_sds((N_CHIPS,) + s.shape, s.dtype) for s in shards]
    return split_start("gather_start_" + tag, GATHER, shards, lands, after)


def finish_gather(handle, chip, tag, after):
    shards, lands = split_wait("gather_wait_" + tag, GATHER, handle, after)
    lands = forward_halves(lands, "gather_forward_" + tag)
    return [lax.dynamic_update_index_in_dim(o, s, chip, 0) for o, s in zip(lands, shards)]


def start_reduce(gs, core, tag):
    recv = swap_halves(gs, "swap_halves_" + tag)
    pairs = pair_sums(gs, recv, core, "pair_sums_" + tag)
    lands = [_sds((N_CHIPS - 1,) + p.shape[1:], p.dtype) for p in pairs]
    return split_start("exchange_start_" + tag, EXCHANGE, pairs, lands)


def finish_reduce(handle, chip, core, tag, after):
    pairs, others = split_wait("exchange_wait_" + tag, EXCHANGE, handle, after)
    halves = chip_sums(pairs, others, chip, core, "chip_sums_" + tag)
    full = join_halves(halves, "join_halves_" + tag)
    return [f.reshape(f.shape[1] * 2, f.shape[2]) for f in full]


_SMALL_SHARDED = (("a_conv_w", (3, 128), 1), ("c_q_norm_g", (1, 64), 1), ("d_ln_g", (1, 128), 1), ("d_ln_b", (1, 128), 1),
                  ("ffn_conv_w", (2, 3, 2 * D_FF // N_CHIPS), 2))
_SMALL_GRADS = (("norm1_g", (2, D_MODEL)), ("norm2_g", (2, D_MODEL)), ("b_mix_w", (4, 128, 128)), ("b_scale", (1, 512)),
                ("c_kv_norm_g", (1, 128)), ("d_w_s", (4, 128, 128)), ("d_b_s", (4, 128)), ("final_norm_g", (1, D_MODEL)),
                ("a_conv_w", (3, 512)), ("c_q_norm_g", (1, 256)), ("d_ln_g", (1, 512)), ("d_ln_b", (1, 512)),
                ("ffn_conv_w", (2, 3, 2 * D_FF)))


def _size(shape):
    n = 1
    for d in shape:
        n *= d
    return n


def kernel(x, c, positions, ada_w, ada_b, norm1_g, norm2_g, ab_w_in, a_conv_w, b_mix_w, b_scale, ab_w_out, cd_w_in, c_q_norm_g, c_w_uq, c_kv_norm_g, c_w_ukv, d_ln_g, d_ln_b, d_w_s, d_b_s, cd_w_out, ffn_w_up, ffn_conv_w, ffn_w_down, final_norm_g, loss_target, m_ada_w, m_ada_b, m_norm1_g, m_norm2_g, m_ab_w_in, m_a_conv_w, m_b_mix_w, m_b_scale, m_ab_w_out, m_cd_w_in, m_c_q_norm_g, m_c_w_uq, m_c_kv_norm_g, m_c_w_ukv, m_d_ln_g, m_d_ln_b, m_d_w_s, m_d_b_s, m_cd_w_out, m_ffn_w_up, m_ffn_conv_w, m_ffn_w_down, m_final_norm_g, v_ada_w, v_ada_b, v_norm1_g, v_norm2_g, v_ab_w_in, v_a_conv_w, v_b_mix_w, v_b_scale, v_ab_w_out, v_cd_w_in, v_c_q_norm_g, v_c_w_uq, v_c_kv_norm_g, v_c_w_ukv, v_d_ln_g, v_d_ln_b, v_d_w_s, v_d_b_s, v_cd_w_out, v_ffn_w_up, v_ffn_conv_w, v_ffn_w_down, v_final_norm_g):
    args = (x, c, positions, ada_w, ada_b, norm1_g, norm2_g, ab_w_in, a_conv_w, b_mix_w, b_scale, ab_w_out, cd_w_in, c_q_norm_g, c_w_uq, c_kv_norm_g, c_w_ukv, d_ln_g, d_ln_b, d_w_s, d_b_s, cd_w_out, ffn_w_up, ffn_conv_w, ffn_w_down, final_norm_g, loss_target, m_ada_w, m_ada_b, m_norm1_g, m_norm2_g, m_ab_w_in, m_a_conv_w, m_b_mix_w, m_b_scale, m_ab_w_out, m_cd_w_in, m_c_q_norm_g, m_c_w_uq, m_c_kv_norm_g, m_c_w_ukv, m_d_ln_g, m_d_ln_b, m_d_w_s, m_d_b_s, m_cd_w_out, m_ffn_w_up, m_ffn_conv_w, m_ffn_w_down, m_final_norm_g, v_ada_w, v_ada_b, v_norm1_g, v_norm2_g, v_ab_w_in, v_a_conv_w, v_b_mix_w, v_b_scale, v_ab_w_out, v_cd_w_in, v_c_q_norm_g, v_c_w_uq, v_c_kv_norm_g, v_c_w_ukv, v_d_ln_g, v_d_ln_b, v_d_w_s, v_d_b_s, v_cd_w_out, v_ffn_w_up, v_ffn_conv_w, v_ffn_w_down, v_final_norm_g)
    a = dict(zip(_INPUTS, args, strict=True))
    xi, yi, ci = _place()
    chip = 2 * xi + yi
    dev = 4 * xi + 2 * yi + ci
    x = a["x"][0]
    tgt = a["loss_target"][0]

    bf = lambda t: t.astype(BF16)
    mix0_handle, tok = start_gather([bf(a["ab_w_in"][0]), bf(a["ab_w_out"][0])], "mix0")
    up0_16, down0_16, up1_16, down1_16 = [bf(a[n][l]) for l in (0, 1) for n in ("ffn_w_up", "ffn_w_down")]
    mix1_16 = [bf(a[n][0]) for n in ("cd_w_in", "c_w_uq", "c_w_ukv", "cd_w_out")]

    small_parts = [a["c"] + tok] + [a[n] for n, _, _ in _SMALL_SHARDED]
    rows1 = -(-sum(p.size for p in small_parts) // LANES // 8) * 8
    g1 = all_gather8(_pack_rows(small_parts, rows1, F32), "gather_small",
                     [up0_16, down0_16, up1_16, down1_16, mix1_16[0], mix1_16[3]]).reshape(N_DEV, rows1 * LANES)
    c_all = g1[:, :D_MODEL]
    per_chip = g1[0::2]
    small_full = {}
    off = D_MODEL
    for n, shp, axis in _SMALL_SHARDED:
        piece = per_chip[:, off:off + _size(shp)].reshape((N_CHIPS,) + shp)
        small_full[n] = jnp.concatenate([piece[k] for k in range(N_CHIPS)], axis=axis)
        off += _size(shp)

    merge = lambda t: t.reshape(t.shape[0] * t.shape[1], t.shape[2])
    w = dict(norm1_g=a["norm1_g"], norm2_g=a["norm2_g"], b_mix_w=a["b_mix_w"][0], b_scale=a["b_scale"],
             c_kv_norm_g=a["c_kv_norm_g"], d_w_s=a["d_w_s"][0], d_b_s=a["d_b_s"][0],
             final_norm_g=a["final_norm_g"].reshape(1, D_MODEL), **small_full)

    ncol = N_MOD * D_MODEL // N_CHIPS
    ada_b_mine = lax.dynamic_slice_in_dim(a["ada_b"], chip * ncol, ncol, axis=1)
    mod_cols = ada_mod(c_all, a["ada_w"], ada_b_mine)
    g2_rows = all_gather8(mod_cols.reshape(-1, LANES), "gather_mod")
    g2 = g2_rows.reshape(N_DEV, 2, N_DEV, ncol)
    mod = lax.dynamic_index_in_dim(g2[0::2], dev, axis=2, keepdims=False)
    mod = mod.transpose(1, 0, 2).reshape(2, N_MOD * D_MODEL)

    late = [g2_rows]
    up0_handle, tok_a = start_gather([up0_16], "up0", late)
    down0_handle, tok_b = start_gather([down0_16], "down0", late)
    mix1_handle, tok_c = start_gather(mix1_16, "mix1", late)
    ffn1_handle, tok_d = start_gather([up1_16, down1_16], "ffn1", late)
    mod = mod + (tok_a + tok_b + tok_c + tok_d)

    ropes = rope_tables(a["positions"][0])
    cm16 = lambda t: chip_major(t).astype(BF16)
    w.update(ffn_w_up=[None, None], ffn_w_down=[None, None])
    handles = dict(mix0=mix0_handle, up0=up0_handle, down0=down0_handle, mix1=mix1_handle, ffn1=ffn1_handle)
    reducing, reduced = {}, {}

    class Hooks(StepHooks):
        def weights(self, stage, after):
            got = finish_gather(handles[stage], chip, stage, after)
            if stage == "mix0":
                w.update(ab_w_in=got[0], ab_w_out=merge(got[1]))
            elif stage == "up0":
                w["ffn_w_up"][0] = got[0]
            elif stage == "down0":
                w["ffn_w_down"][0] = merge(got[0])
            elif stage == "mix1":
                cd_in, uq, ukv, cd_out = got
                w.update(prepare_weights(dict(cd_w_in=from_chip_major(cd_in), c_w_uq=from_chip_major(uq),
                                              c_w_ukv=from_chip_major(ukv), cd_w_out=merge(cd_out))))
            else:
                w["ffn_w_up"][1], w["ffn_w_down"][1] = got[0], merge(got[1])

        def gradients(self, stage, grads, after):
            if stage in ("ffn0", "ffn1"):
                parts = [grads["ffn_w_up"], _rows_major(grads["ffn_w_down"])]
            elif stage == "mix1":
                grads.update(unprepare_grads(grads))
                parts = [cm16(grads["cd_w_in"]), cm16(grads["c_w_uq"]), cm16(grads["c_w_ukv"]),
                         _rows_major(grads["cd_w_out"]).astype(BF16)]
            else:
                parts = [grads["ab_w_in"], _rows_major(grads["ab_w_out"])]
            reducing[stage], tok = start_reduce(parts, ci, stage)
            before = {"mix1": "ffn1", "ffn0": "mix1", "mix0": "ffn0"}.get(stage)
            if before is not None:
                reduced[before] = finish_reduce(reducing[before], chip, ci, before, after)
            return tok

    loss, grad_x, dmod, by_stage = run_step(x, tgt, mod, ropes, w, Hooks())
    grads = merge_grads(by_stage)

    parts3 = [dmod] + [grads[n] for n, _ in _SMALL_GRADS] + [loss[0, 0]]
    rows3 = -(-sum(p.size for p in parts3) // LANES // 8) * 8
    small_handle, _ = split_start("small_grads_start", EVERYONE, [_pack_rows(parts3, rows3, F32)],
                                  [_sds((N_DEV, rows3, LANES))])
    red_up1, red_down1 = reduced["ffn1"]
    red_cd_in, red_uq, red_ukv, red_cd_out = reduced["mix1"]
    red_up0, red_down0 = reduced["ffn0"]
    out_grads = dict(cd_w_in=red_cd_in, c_w_uq=red_uq, c_w_ukv=red_ukv, cd_w_out=red_cd_out)
    per_layer = dict(ffn_w_up=(red_up0, red_up1), ffn_w_down=(red_down0, red_down1))
    updates = {}

    def update(n):
        if n in per_layer:
            updates[n] = adamw_layers(a[n], *per_layer[n], a["m_" + n], a["v_" + n], "adamw_" + n)
        else:
            updates[n] = adamw(a[n], out_grads[n].reshape(a[n].shape), a["m_" + n], a["v_" + n], "adamw_" + n)

    early =("ffn_w_up", "ffn_w_down", "cd_w_in", "c_w_uq", "c_w_ukv", "cd_w_out")
    for n in early:
        update(n)
    (mine,), (landed,) = split_wait("small_grads_wait", EVERYONE, small_handle, [updates[n][1] for n in early])
    g3 = lax.dynamic_update_index_in_dim(landed, mine, dev, 0)
    summed = sum8(g3).reshape(-1)
    nmod = 2 * N_MOD * D_MODEL
    out_grads["ada_b"] = summed[:nmod].reshape(2, N_MOD * D_MODEL)
    off = nmod
    for n, shp in _SMALL_GRADS:
        out_grads[n] = summed[off:off + _size(shp)].reshape(shp)
        off += _size(shp)
    loss = summed[off]
    for n, shp, axis in _SMALL_SHARDED:
        width = out_grads[n].shape[-1] // N_CHIPS
        out_grads[n] = lax.dynamic_slice_in_dim(out_grads[n], chip * width, width, axis=out_grads[n].ndim - 1)
    dmod_all = g3.reshape(N_DEV, rows3 * LANES)[:, :nmod].reshape(N_DEV, 2, N_MOD * D_MODEL)
    dmod_mine = lax.dynamic_slice_in_dim(dmod_all, chip * ncol, ncol, axis=2).transpose(1, 0, 2)
    updates["ada_w"] = adamw_ada(a["ada_w"], c_all, dmod_mine, a["m_ada_w"], a["v_ada_w"])

    red_in0, red_out0 = finish_reduce(reducing["mix0"], chip, ci, "mix0", updates["ada_w"][1])
    out_grads.update(ab_w_in=red_in0, ab_w_out=red_out0)

    for n in ("ab_w_in", "ab_w_out"):
        update(n)
    small = [n for n in _WEIGHTS if n not in updates]
    for n, res in zip(small, adamw_small([a[n] for n in small], [out_grads[n].reshape(a[n].shape) for n in small],
                                         [a["m_" + n] for n in small], [a["v_" + n] for n in small])):
        updates[n] = res
    return (loss, grad_x[None], *[updates[n][i] for i in range(4) for n in _WEIGHTS])
```

```python
import functools
from typing import NamedTuple

import jax
import jax.numpy as jnp
from jax import lax
from jax.experimental import pallas as pl
from jax.experimental.pallas import tpu as pltpu

F32 = jnp.float32
BF16 = jnp.bfloat16
EPS = 1e-6
D_MODEL = 1024
N_MOD = 6
A_WIDTH = 512
B_GROUPS = 4
C_HEADS = 8
C_NOPE = 64
C_ROPE = 32
C_V = 64
C_Q_RANK = 256
C_KV_RANK = 128
HEAD_PAD = 128
ROPE_THETA = 10000.0
D_GROUPS = 4
D_CHUNK = 128
D_FF = 2816
FF_UNIT = 128
ADAM_LR = 0.001
ADAM_B1 = 0.9
ADAM_B2 = 0.999
ADAM_EPS = 1e-08
ADAM_WD = 0.01
ADAM_STEP = 10
N_CHIPS = 4
N_DEV = 8
LANES = 128
VMEM_BIG = 56 * 1024 * 1024
MESH = pl.DeviceIdType.MESH


def _sds(shape, dtype=F32):
    return jax.ShapeDtypeStruct(tuple(shape), dtype)


def _tile(n, cap, mult=128):
    if n <= cap:
        return n
    best = None
    for t in range(mult, cap + 1, mult):
        if n % t == 0:
            best = t
    assert best is not None, (n, cap, mult)
    return best


def _params(dims=None, vmem=None):
    return pltpu.CompilerParams(dimension_semantics=dims, vmem_limit_bytes=vmem)


def _shift_down(v, k):
    r = pltpu.roll(v, k, axis=0)
    t = lax.broadcasted_iota(jnp.int32, v.shape, 0)
    return jnp.where(t >= k, r, 0.0)


def _shift_up(v, k):
    n = v.shape[0]
    r = pltpu.roll(v, n - k, axis=0)
    t = lax.broadcasted_iota(jnp.int32, v.shape, 0)
    return jnp.where(t < n - k, r, 0.0)


def _sigmoid(v):
    return 1.0 / (1.0 + jnp.exp(-v))


_GELU_C = 0.7978845608028654
_GELU_A = 0.044715


def _gelu(v):
    return 0.5 * v * (1.0 + jnp.tanh(_GELU_C * (v + _GELU_A * v * v * v)))


def _gelu_grad(v):
    th = jnp.tanh(_GELU_C * (v + _GELU_A * v * v * v))
    return 0.5 * (1.0 + th) + 0.5 * v * (1.0 - th * th) * _GELU_C * (1.0 + 3.0 * _GELU_A * v * v)


_NN = (((1,), (0,)), ((), ()))
_NT = (((1,), (1,)), ((), ()))
_TN = (((0,), (0,)), ((), ()))


def _dot(a, b, dims=_NN):
    return lax.dot_general(a, b, dims, preferred_element_type=F32)


def _logical(t, groups):
    return (t.shape[-2], t.shape[-1] * groups)


def _block(tr, tc, groups, cols, where):
    if groups == 1:
        return pl.BlockSpec((tr, tc), where)
    per = cols // groups // tc

    def index(i, j, s):
        r, c = where(i, j, s)
        return (c // per, r, c % per)

    return pl.BlockSpec((None, tr, tc), index)


def matmul(a, b, mode, out_dtype, name, ga=1, gb=1, go=1, tm=None, tn=None, tk=None):
    (ar, ac), (br, bc) = _logical(a, ga), _logical(b, gb)
    if mode == "nn":
        m, k, n = ar, ac, bc
        a_col, b_col = "k", "n"
    elif mode == "nt":
        m, k, n = ar, ac, br
        a_col, b_col = "k", "k"
    else:
        k, m, n = ar, ac, bc
        a_col, b_col = "m", "n"
    limit = {"m": m, "n": n // go, "k": k}
    limit[a_col] = min(limit[a_col], ac // ga)
    limit[b_col] = min(limit[b_col], bc // gb)
    tm = tm or _tile(limit["m"], 2048, 128 if mode == "tn" else 16)
    tn = tn or _tile(limit["n"], 512)
    tk = tk or _tile(limit["k"], 2048, 16 if mode == "tn" else 128)
    nk = k // tk
    if mode == "nn":
        a_spec = _block(tm, tk, ga, ac, lambda i, j, s: (i, s))
        b_spec = _block(tk, tn, gb, bc, lambda i, j, s: (s, j))
        dims = _NN
    elif mode == "nt":
        a_spec = _block(tm, tk, ga, ac, lambda i, j, s: (i, s))
        b_spec = _block(tn, tk, gb, bc, lambda i, j, s: (j, s))
        dims = _NT
    else:
        a_spec = _block(tk, tm, ga, ac, lambda i, j, s: (s, i))
        b_spec = _block(tk, tn, gb, bc, lambda i, j, s: (s, j))
        dims = _TN
    o_spec = _block(tm, tn, go, n, lambda i, j, s: (i, j))
    out_shape = _sds((m, n), out_dtype) if go == 1 else _sds((go, m, n // go), out_dtype)

    def body(a_ref, b_ref, o_ref, acc_ref):
        s = pl.program_id(2)

        @pl.when(s == 0)
        def _():
            acc_ref[...] = jnp.zeros_like(acc_ref)

        acc_ref[...] += _dot(a_ref[...], b_ref[...], dims)

        @pl.when(s == nk - 1)
        def _():
            o_ref[...] = acc_ref[...].astype(o_ref.dtype)

    return pl.pallas_call(
        body, name=name, out_shape=out_shape, grid=(m // tm, n // tn, nk),
        in_specs=[a_spec, b_spec], out_specs=o_spec,
        scratch_shapes=[pltpu.VMEM((tm, tn), F32)],
        compiler_params=_params(("parallel", "parallel", "arbitrary"), VMEM_BIG),
    )(a, b)


def _rows(tm, n):
    return pl.BlockSpec((tm, n), lambda i: (i, 0))


def _vec(n):
    return pl.BlockSpec((1, n), lambda i: (0, 0))


class Vec(NamedTuple):
    array: jax.Array
    row: int
    col: int


def _vec_in(v, d):
    return pl.BlockSpec((None, 1, d), lambda i: (v.row, 0, v.col))


def modnorm_fwd(x, g, sc, sh, name):
    s, d = x.shape
    tm = _tile(s, 256, 8)

    def body(x_ref, g_ref, sc_ref, sh_ref, o_ref):
        xv = x_ref[...]
        r = lax.rsqrt(jnp.mean(xv * xv, axis=-1, keepdims=True) + EPS)
        o_ref[...] = ((xv * r) * g_ref[...] * (1.0 + sc_ref[...]) + sh_ref[...]).astype(BF16)

    return pl.pallas_call(
        body, name=name, out_shape=_sds((s, d), BF16), grid=(s // tm,),
        in_specs=[_rows(tm, d), _vec_in(g, d), _vec_in(sc, d), _vec_in(sh, d)], out_specs=_rows(tm, d),
        compiler_params=_params(("parallel",)),
    )(x, g.array, sc.array, sh.array)


def norm_bwd(x, dh, g, sc, dres, name):
    s, d = x.shape
    tm = _tile(s, 256, 8)
    nsteps = s // tm

    def body(x_ref, dh_ref, g_ref, sc_ref, dr_ref, dx_ref, dsh_ref, dsc_ref, dg_ref, a2_ref):
        i = pl.program_id(0)

        @pl.when(i == 0)
        def _():
            dsh_ref[...] = jnp.zeros_like(dsh_ref)
            a2_ref[...] = jnp.zeros_like(a2_ref)

        xv = x_ref[...]
        dh = dh_ref[...].astype(F32)
        r = lax.rsqrt(jnp.mean(xv * xv, axis=-1, keepdims=True) + EPS)
        xh = xv * r
        dsh_ref[...] += jnp.sum(dh, axis=0, keepdims=True)
        a2_ref[...] += jnp.sum(dh * xh, axis=0, keepdims=True)
        dxh = dh * (g_ref[...] * (1.0 + sc_ref[...]))
        dx = r * (dxh - xh * jnp.mean(dxh * xh, axis=-1, keepdims=True))
        dx_ref[...] = dr_ref[...] + dx

        @pl.when(i == nsteps - 1)
        def _():
            dsc_ref[...] = a2_ref[...] * g_ref[...]
            dg_ref[...] = a2_ref[...] * (1.0 + sc_ref[...])

    return pl.pallas_call(
        body, name=name, out_shape=(_sds((s, d)), _sds((1, d)), _sds((1, d)), _sds((1, d))), grid=(nsteps,),
        in_specs=[_rows(tm, d), _rows(tm, d), _vec_in(g, d), _vec_in(sc, d), _rows(tm, d)],
        out_specs=(_rows(tm, d), _vec(d), _vec(d), _vec(d)),
        scratch_shapes=[pltpu.VMEM((1, d), F32)],
        compiler_params=_params(("arbitrary",)),
    )(x, dh, g.array, sc.array, dres)


_ROW_STREAM = 256


def _row_streams(s):
    tm = _tile(s, 2 * _ROW_STREAM, 8)
    sub = min(tm, _ROW_STREAM)
    return tm, [slice(r * sub, (r + 1) * sub) for r in range(tm // sub)]


def resid_modnorm_fwd(x, y, gate, g, sc, sh, name):
    s, d = x.shape
    tm, streams = _row_streams(s)

    def body(x_ref, y_ref, gate_ref, g_ref, sc_ref, sh_ref, xo_ref, h_ref):
        for rs in streams:
            xv = x_ref[rs, :] + gate_ref[...] * y_ref[rs, :].astype(F32)
            xo_ref[rs, :] = xv
            r = lax.rsqrt(jnp.mean(xv * xv, axis=-1, keepdims=True) + EPS)
            h_ref[rs, :] = ((xv * r) * g_ref[...] * (1.0 + sc_ref[...]) + sh_ref[...]).astype(BF16)

    return pl.pallas_call(
        body, name=name, out_shape=(_sds((s, d)), _sds((s, d), BF16)), grid=(s // tm,),
        in_specs=[_rows(tm, d), _rows(tm, d), _vec_in(gate, d), _vec_in(g, d), _vec_in(sc, d), _vec_in(sh, d)],
        out_specs=(_rows(tm, d), _rows(tm, d)),
        compiler_params=_params(("parallel",), VMEM_BIG),
    )(x, y, gate.array, g.array, sc.array, sh.array)


def norm_gate_bwd(x, dh, g, sc, dres, y, gate, name):
    s, d = x.shape
    tm, streams = _row_streams(s)
    nsteps = s // tm

    def body(x_ref, dh_ref, g_ref, sc_ref, dr_ref, y_ref, gate_ref, dx_ref, dsh_ref, dsc_ref, dg_ref, dy_ref,
             dgate_ref, a2_ref):
        i = pl.program_id(0)

        @pl.when(i == 0)
        def _():
            dsh_ref[...] = jnp.zeros_like(dsh_ref)
            a2_ref[...] = jnp.zeros_like(a2_ref)
            dgate_ref[...] = jnp.zeros_like(dgate_ref)

        for rs in streams:
            xv = x_ref[rs, :]
            dh = dh_ref[rs, :].astype(F32)
            r = lax.rsqrt(jnp.mean(xv * xv, axis=-1, keepdims=True) + EPS)
            xh = xv * r
            dsh_ref[...] += jnp.sum(dh, axis=0, keepdims=True)
            a2_ref[...] += jnp.sum(dh * xh, axis=0, keepdims=True)
            dxh = dh * (g_ref[...] * (1.0 + sc_ref[...]))
            dr = dr_ref[rs, :] + r * (dxh - xh * jnp.mean(dxh * xh, axis=-1, keepdims=True))
            dx_ref[rs, :] = dr
            dy_ref[rs, :] = (dr * gate_ref[...]).astype(BF16)
            dgate_ref[...] += jnp.sum(dr * y_ref[rs, :].astype(F32), axis=0, keepdims=True)

        @pl.when(i == nsteps - 1)
        def _():
            dsc_ref[...] = a2_ref[...] * g_ref[...]
            dg_ref[...] = a2_ref[...] * (1.0 + sc_ref[...])

    vec = _sds((1, d))
    return pl.pallas_call(
        body, name=name, out_shape=(_sds((s, d)), vec, vec, vec, _sds((s, d), BF16), vec), grid=(nsteps,),
        in_specs=[_rows(tm, d), _rows(tm, d), _vec_in(g, d), _vec_in(sc, d), _rows(tm, d), _rows(tm, d), _vec_in(gate, d)],
        out_specs=(_rows(tm, d), _vec(d), _vec(d), _vec(d), _rows(tm, d), _vec(d)),
        scratch_shapes=[pltpu.VMEM((1, d), F32)],
        compiler_params=_params(("arbitrary",), VMEM_BIG),
    )(x, dh, g.array, sc.array, dres, y, gate.array)


def final_fused(x, f, gate, g, tgt):
    s, d = x.shape
    tm = _tile(s, 256, 8)

    def body(x_ref, f_ref, gate_ref, g_ref, t_ref, dx_ref, dg_ref, loss_ref, df_ref, dgate_ref):
        @pl.when(pl.program_id(0) == 0)
        def _():
            dg_ref[...] = jnp.zeros_like(dg_ref)
            loss_ref[...] = jnp.zeros_like(loss_ref)
            dgate_ref[...] = jnp.zeros_like(dgate_ref)

        fv, gatev, gv = f_ref[...].astype(F32), gate_ref[...], g_ref[...]
        xv = x_ref[...] + gatev * fv
        r = lax.rsqrt(jnp.mean(xv * xv, axis=-1, keepdims=True) + EPS)
        xh = xv * r
        e = xh * gv - t_ref[...]
        row = jnp.sum(e * e, axis=-1, keepdims=True) * (0.5 / d)
        loss_ref[...] += jnp.sum(row, axis=0, keepdims=True)
        dy = e * (1.0 / d)
        dg_ref[...] += jnp.sum(dy * xh, axis=0, keepdims=True)
        dxh = dy * gv
        dx = r * (dxh - xh * jnp.mean(dxh * xh, axis=-1, keepdims=True))
        dx_ref[...] = dx
        df_ref[...] = (dx * gatev).astype(BF16)
        dgate_ref[...] += jnp.sum(dx * fv, axis=0, keepdims=True)

    vec = _sds((1, d))
    return pl.pallas_call(
        body, name="final_fused", out_shape=(_sds((s, d)), vec, _sds((1, LANES)), _sds((s, d), BF16), vec),
        grid=(s // tm,),
        in_specs=[_rows(tm, d), _rows(tm, d), _vec_in(gate, d), _vec_in(g, d), _rows(tm, d)],
        out_specs=(_rows(tm, d), _vec(d), _vec(LANES), _rows(tm, d), _vec(d)),
        compiler_params=_params(("arbitrary",)),
    )(x, f, gate.array, g.array, tgt)


def _taps(v):
    return _shift_down(v, 2), _shift_down(v, 1), v


def _conv3_taps(taps, w):
    return w[0:1, :] * taps[0] + w[1:2, :] * taps[1] + w[2:3, :] * taps[2]


def _conv3(v, w):
    return _conv3_taps(_taps(v), w)


def _conv3_t(dv, w):
    return w[0:1, :] * _shift_up(dv, 2) + w[1:2, :] * _shift_up(dv, 1) + w[2:3, :] * dv


def _conv3_dw_taps(dv, taps):
    return jnp.concatenate([jnp.sum(dv * t, axis=0, keepdims=True) for t in taps], axis=0)


def _conv3_dw(dv, v):
    return _conv3_dw_taps(dv, _taps(v))


def gconv_fwd(z, conv_w):
    s = z.shape[0]
    nb = A_WIDTH // LANES

    def body(b_ref, c_ref, a_ref, w_ref, o_ref):
        b, c, a = b_ref[...].astype(F32), c_ref[...].astype(F32), a_ref[...].astype(F32)
        o_ref[...] = (b * _conv3(c * a, w_ref[...])).astype(BF16)

    col = lambda off: pl.BlockSpec((s, LANES), lambda j: (0, off + j))
    return pl.pallas_call(
        body, name="gconv_fwd", out_shape=_sds((s, A_WIDTH + _B_WIDTH), BF16), grid=(nb,),
        in_specs=[col(0), col(nb), col(2 * nb), pl.BlockSpec((3, LANES), lambda j: (0, j))],
        out_specs=pl.BlockSpec((s, LANES), lambda j: (0, j)),
        compiler_params=_params(("parallel",), VMEM_BIG),
    )(z, z, z, conv_w)


def gconv_bwd(z, conv_w, dycat):
    s = z.shape[0]
    nb = A_WIDTH // LANES

    def body(b_ref, c_ref, a_ref, w_ref, dy_ref, db_ref, dc_ref, da_ref, dw_ref):
        c, a, w, dy = c_ref[...].astype(F32), a_ref[...].astype(F32), w_ref[...], dy_ref[...].astype(F32)
        ca = c * a
        db_ref[...] = (dy * _conv3(ca, w)).astype(BF16)
        dconv = dy * b_ref[...].astype(F32)
        dw_ref[...] = _conv3_dw(dconv, ca)
        dca = _conv3_t(dconv, w)
        dc_ref[...] = (dca * a).astype(BF16)
        da_ref[...] = (dca * c).astype(BF16)

    col = lambda off: pl.BlockSpec((s, LANES), lambda j: (0, off + j))
    wspec = pl.BlockSpec((3, LANES), lambda j: (0, j))
    part = _sds((s, A_WIDTH), BF16)
    return pl.pallas_call(
        body, name="gconv_bwd", out_shape=(part, part, part, _sds((3, A_WIDTH))), grid=(nb,),
        in_specs=[col(0), col(nb), col(2 * nb), wspec, col(0)],
        out_specs=(col(0), col(0), col(0), wspec),
        compiler_params=_params(("parallel",), VMEM_BIG),
    )(z, z, z, conv_w, dycat)


def _pool_counts(s, w):
    t = lax.broadcasted_iota(jnp.int32, (s, 1), 0)
    return jnp.minimum(t + 1, w).astype(F32)


def _pooled(p, levels):
    acc = p
    for lv in range(levels):
        acc = acc + _shift_down(acc, 2 ** lv)
    return acc / _pool_counts(p.shape[0], 2 ** levels) - p


_B_WIDTH = B_GROUPS * LANES


def pool_fwd(z, mix_w, scale, ycat):
    s = z.shape[0]

    def body(p_ref, m_ref, sc_ref, ycat_ref, o_ref):
        del ycat_ref
        for g in range(B_GROUPS):
            cols = slice(g * LANES, (g + 1) * LANES)
            pooled = _pooled(p_ref[:, cols].astype(F32), g + 1)
            y = _dot(pooled.astype(BF16), m_ref[g].astype(BF16))
            o_ref[:, cols] = (y * sc_ref[:, cols]).astype(BF16)

    return pl.pallas_call(
        body, name="pool_fwd", out_shape=_sds(ycat.shape, BF16), grid=(1,),
        in_specs=[pl.BlockSpec((s, _B_WIDTH), lambda i: (0, 3 * A_WIDTH // _B_WIDTH)),
                  pl.BlockSpec((B_GROUPS, LANES, LANES), lambda i: (0, 0, 0)), pl.BlockSpec((1, _B_WIDTH), lambda i: (0, 0)),
                  pl.BlockSpec(memory_space=pl.ANY)],
        out_specs=pl.BlockSpec((s, _B_WIDTH), lambda i: (0, A_WIDTH // _B_WIDTH)),
        input_output_aliases={3: 0},
        compiler_params=_params(("arbitrary",), VMEM_BIG),
    )(z, mix_w, scale, ycat)


def pool_bwd(z, mix_w, scale, dycat):
    s = z.shape[0]

    def body(p_ref, m_ref, sc_ref, dy_ref, dp_ref, dm_ref, dsc_ref):
        for g in range(B_GROUPS):
            cols = slice(g * LANES, (g + 1) * LANES)
            pooled = _pooled(p_ref[:, cols].astype(F32), g + 1)
            mw = m_ref[g].astype(BF16)
            pb = pooled.astype(BF16)
            dy = dy_ref[:, cols].astype(F32)
            dsc_ref[:, cols] = jnp.sum(dy * _dot(pb, mw), axis=0, keepdims=True)
            dmix = (dy * sc_ref[:, cols]).astype(BF16)
            dm_ref[g] = _dot(pb, dmix, _TN)
            dpool = _dot(dmix, mw, _NT)
            acc = dpool / _pool_counts(s, 2 ** (g + 1))
            for lv in range(g + 1):
                acc = acc + _shift_up(acc, 2 ** lv)
            dp_ref[:, cols] = (acc - dpool).astype(BF16)

    wide = lambda c: pl.BlockSpec((s, _B_WIDTH), lambda i: (0, c))
    mspec = pl.BlockSpec((B_GROUPS, LANES, LANES), lambda i: (0, 0, 0))
    vspec = pl.BlockSpec((1, _B_WIDTH), lambda i: (0, 0))
    return pl.pallas_call(
        body, name="pool_bwd", out_shape=(_sds((s, _B_WIDTH), BF16), _sds((B_GROUPS, LANES, LANES)), _sds((1, _B_WIDTH))),
        grid=(1,), in_specs=[wide(3 * A_WIDTH // _B_WIDTH), mspec, vspec, wide(A_WIDTH // _B_WIDTH)],
        out_specs=(wide(0), mspec, vspec),
        compiler_params=_params(("arbitrary",), VMEM_BIG),
    )(z, mix_w, scale, dycat)


_FF_BLOCKS = D_FF // FF_UNIT


def _ff_spec(s):
    return pl.BlockSpec((2, s, FF_UNIT), lambda j: (0, 0, j))


def _ff_wspecs():
    return [pl.BlockSpec((3, FF_UNIT), lambda j: (0, j)), pl.BlockSpec((3, FF_UNIT), lambda j: (0, _FF_BLOCKS + j))]


_FF_ROWS = 64
_FF_HALO = 16


def _chunk_taps(z_ref, half, c):
    start = pl.multiple_of(c * _FF_ROWS, _FF_ROWS)
    before = pl.multiple_of(jnp.maximum(c * _FF_ROWS - _FF_HALO, 0), _FF_HALO)
    halo = z_ref[half, pl.ds(before, _FF_HALO), :].astype(F32)
    halo = jnp.where(c > 0, halo, 0.0)
    win = jnp.concatenate([halo, z_ref[half, pl.ds(start, _FF_ROWS), :].astype(F32)], axis=0)
    return tuple(pltpu.roll(win, k, axis=0)[_FF_HALO:] for k in (2, 1)) + (win[_FF_HALO:],)


def _fold8(v):
    acc = v[0:8]
    for r in range(8, v.shape[0], 8):
        acc = acc + v[r:r + 8]
    return acc


_FF_CHUNK = 256


def ffn_act_down(zf, conv_w, w_down, name):
    s, d = zf.shape[1], w_down.shape[1]
    nk = D_FF // _FF_CHUNK
    chunk = lambda k: jnp.minimum(k, nk - 1)

    def body(z_ref, wg_ref, wu_ref, wd_ref, a_ref, f_ref, held_ref, acc_ref):
        k = pl.program_id(0)

        @pl.when(k == 0)
        def _():
            held_ref[...] = jnp.zeros_like(held_ref)
            acc_ref[...] = jnp.zeros_like(acc_ref)

        acc_ref[...] += _dot(held_ref[(k + 1) % 2], wd_ref[...])
        g = _conv3(z_ref[0].astype(F32), wg_ref[...])
        u = _conv3(z_ref[1].astype(F32), wu_ref[...])
        act = (g * _sigmoid(g) * u).astype(BF16)
        a_ref[...] = act
        held_ref[k % 2] = act

        @pl.when(k == nk)
        def _():
            f_ref[...] = acc_ref[...].astype(BF16)

    return pl.pallas_call(
        body, name=name, out_shape=(_sds((s, D_FF), BF16), _sds((s, d), BF16)), grid=(nk + 1,),
        in_specs=[pl.BlockSpec((2, s, _FF_CHUNK), lambda k: (0, 0, chunk(k))),
                  pl.BlockSpec((3, _FF_CHUNK), lambda k: (0, chunk(k))),
                  pl.BlockSpec((3, _FF_CHUNK), lambda k: (0, nk + chunk(k))),
                  pl.BlockSpec((_FF_CHUNK, d), lambda k: (jnp.maximum(k - 1, 0), 0))],
        out_specs=(pl.BlockSpec((s, _FF_CHUNK), lambda k: (0, chunk(k))), pl.BlockSpec((s, d), lambda k: (0, 0))),
        scratch_shapes=[pltpu.VMEM((2, s, _FF_CHUNK), BF16), pltpu.VMEM((s, d), F32)],
        compiler_params=_params(("arbitrary",), VMEM_BIG),
    )(zf, conv_w, conv_w, w_down)


def ffn_act_bwd(zf, conv_w, da, name):
    s = zf.shape[1]
    assert s % _FF_ROWS == 0
    nchunks = s // _FF_ROWS

    def body(z_ref, wg_ref, wu_ref, da_ref, dz_ref, dw_ref, dg_ref, du_ref):
        wg, wu = wg_ref[...], wu_ref[...]

        def first(c, acc):
            rows = pl.ds(pl.multiple_of(c * _FF_ROWS, _FF_ROWS), _FF_ROWS)
            tg, tu = _chunk_taps(z_ref, 0, c), _chunk_taps(z_ref, 1, c)
            g = _conv3_taps(tg, wg)
            u = _conv3_taps(tu, wu)
            dav = da_ref[rows, :].astype(F32)
            sg = _sigmoid(g)
            dg = dav * u * (sg * (1.0 + g * (1.0 - sg)))
            du = dav * (g * sg)
            dg_ref[rows, :] = dg
            du_ref[rows, :] = du
            return tuple(a + _fold8(d * t) for a, (d, t) in zip(acc, [(dg, t) for t in tg] + [(du, t) for t in tu]))

        zero = jnp.zeros((8, FF_UNIT), F32)
        acc = lax.fori_loop(0, nchunks, first, (zero,) * 6)
        sums = [jnp.sum(a, axis=0, keepdims=True) for a in acc]
        dw_ref[0] = jnp.concatenate(sums[:3], axis=0)
        dw_ref[1] = jnp.concatenate(sums[3:], axis=0)

        tail = pl.ds(s, _FF_HALO)
        dg_ref[tail, :] = jnp.zeros((_FF_HALO, FF_UNIT), F32)
        du_ref[tail, :] = jnp.zeros((_FF_HALO, FF_UNIT), F32)
        span = _FF_ROWS + _FF_HALO

        def second(c, carry):
            start = pl.multiple_of(c * _FF_ROWS, _FF_ROWS)
            for half, (d_ref, w) in enumerate(((dg_ref, wg), (du_ref, wu))):
                win = d_ref[pl.ds(start, span), :]
                dz = (w[0:1, :] * pltpu.roll(win, span - 2, axis=0)[:_FF_ROWS]
                      + w[1:2, :] * pltpu.roll(win, span - 1, axis=0)[:_FF_ROWS] + w[2:3, :] * win[:_FF_ROWS])
                dz_ref[half, pl.ds(start, _FF_ROWS), :] = dz.astype(BF16)
            return carry

        lax.fori_loop(0, nchunks, second, 0)

    return pl.pallas_call(
        body, name=name, out_shape=(_sds((2, s, D_FF), BF16), _sds((2, 3, D_FF))), grid=(_FF_BLOCKS,),
        in_specs=[_ff_spec(s)] + _ff_wspecs() + [pl.BlockSpec((s, FF_UNIT), lambda j: (0, j))],
        out_specs=(_ff_spec(s), pl.BlockSpec((2, 3, FF_UNIT), lambda j: (0, 0, j))),
        scratch_shapes=[pltpu.VMEM((s + _FF_HALO, FF_UNIT), F32), pltpu.VMEM((s + _FF_HALO, FF_UNIT), F32)],
        compiler_params=_params(("parallel",), VMEM_BIG),
    )(zf, conv_w, conv_w, da)


def _rope(v, cs, s1, s2):
    return v * cs + pltpu.roll(v, LANES - C_ROPE // 2, axis=1) * s1 + pltpu.roll(v, C_ROPE // 2, axis=1) * s2


def _rope_t(dv, cs, s1, s2):
    return dv * cs + pltpu.roll(dv * s1, C_ROPE // 2, axis=1) + pltpu.roll(dv * s2, LANES - C_ROPE // 2, axis=1)


def _kpe_mask(shape):
    lane = lax.broadcasted_iota(jnp.int32, shape, 1)
    return (lane >= C_NOPE) & (lane < C_NOPE + C_ROPE)


def _rms(v, g):
    r = lax.rsqrt(jnp.mean(v * v, axis=-1, keepdims=True) + EPS)
    return v * r, r


def _rms_bwd(dn, xh, r, g):
    dxh = dn * g
    return r * (dxh - xh * jnp.mean(dxh * xh, axis=-1, keepdims=True)), jnp.sum(dn * xh, axis=0, keepdims=True)


_ZQ = C_Q_RANK + C_KV_RANK + HEAD_PAD
_HW = C_HEADS * HEAD_PAD


_MLA_ROWS = 256


def _mla_tiles(s):
    tm = _tile(s, 2 * _MLA_ROWS, 8)
    sub = min(tm, _MLA_ROWS)
    return tm, [slice(r * sub, (r + 1) * sub) for r in range(tm // sub)]


def mla_pre_fwd(z, gq, gkv, wq, wk, wv, cs, s1, s2):
    s = z.shape[0]
    tm, streams = _mla_tiles(s)

    def body(z_ref, gq_ref, gkv_ref, wq_ref, wk_ref, wv_ref, cs_ref, s1_ref, s2_ref, q_ref, k_ref, v_ref):
        for rs in streams:
            zv = z_ref[rs, :].astype(F32)
            cst, s1t, s2t = cs_ref[rs, :], s1_ref[rs, :], s2_ref[rs, :]
            qh, _ = _rms(zv[:, :C_Q_RANK], None)
            qn = (qh * gq_ref[...]).astype(BF16)
            q = _dot(qn, wq_ref[...])
            kh, _ = _rms(zv[:, C_Q_RANK:C_Q_RANK + C_KV_RANK], None)
            kvn = (kh * gkv_ref[...]).astype(BF16)
            k = _dot(kvn, wk_ref[...])
            v_ref[rs, :] = _dot(kvn, wv_ref[...]).astype(BF16)
            kpe = _rope(zv[:, C_Q_RANK + C_KV_RANK:], cst, s1t, s2t)
            for h in range(C_HEADS):
                sl = slice(h * HEAD_PAD, (h + 1) * HEAD_PAD)
                q_ref[rs, sl] = _rope(q[:, sl], cst, s1t, s2t).astype(BF16)
                k_ref[rs, sl] = (k[:, sl] + kpe).astype(BF16)

    full = lambda r, c: pl.BlockSpec((r, c), lambda i: (0, 0))
    hw = _sds((s, _HW), BF16)
    return pl.pallas_call(
        body, name="mla_pre_fwd", out_shape=(hw, hw, hw), grid=(s // tm,),
        in_specs=[_rows(tm, _ZQ), _vec(C_Q_RANK), _vec(C_KV_RANK), full(C_Q_RANK, _HW), full(C_KV_RANK, _HW),
                  full(C_KV_RANK, _HW), _rows(tm, LANES), _rows(tm, LANES), _rows(tm, LANES)],
        out_specs=(_rows(tm, _HW), _rows(tm, _HW), _rows(tm, _HW)),
        compiler_params=_params(("parallel",), VMEM_BIG),
    )(z, gq, gkv, wq, wk, wv, cs, s1, s2)


def mla_pre_bwd(z, gq, gkv, wq, wk, wv, cs, s1, s2, dq, dk, dv):
    s = z.shape[0]
    tm, streams = _mla_tiles(s)

    def body(z_ref, gq_ref, gkv_ref, wq_ref, wk_ref, wv_ref, cs_ref, s1_ref, s2_ref, dq_ref, dk_ref, dv_ref,
             dz_ref, dwq_ref, dwk_ref, dwv_ref, dgq_ref, dgkv_ref):
        @pl.when(pl.program_id(0) == 0)
        def _():
            dwq_ref[...] = jnp.zeros_like(dwq_ref)
            dwk_ref[...] = jnp.zeros_like(dwk_ref)
            dwv_ref[...] = jnp.zeros_like(dwv_ref)
            dgq_ref[...] = jnp.zeros_like(dgq_ref)
            dgkv_ref[...] = jnp.zeros_like(dgkv_ref)

        gqv, gkvv = gq_ref[...], gkv_ref[...]
        for rs in streams:
            zv = z_ref[rs, :].astype(F32)
            cst, s1t, s2t = cs_ref[rs, :], s1_ref[rs, :], s2_ref[rs, :]
            qh, rq = _rms(zv[:, :C_Q_RANK], None)
            qn = (qh * gqv).astype(BF16)
            kh, rk = _rms(zv[:, C_Q_RANK:C_Q_RANK + C_KV_RANK], None)
            kvn = (kh * gkvv).astype(BF16)

            dqv = dq_ref[rs, :].astype(F32)
            dqp = jnp.concatenate(
                [_rope_t(dqv[:, h * HEAD_PAD:(h + 1) * HEAD_PAD], cst, s1t, s2t) for h in range(C_HEADS)], axis=1
            ).astype(BF16)
            dwq_ref[...] += _dot(qn, dqp, _TN)
            dqn = _dot(dqp, wq_ref[...], _NT)
            dql, dgq = _rms_bwd(dqn, qh, rq, gqv)
            dgq_ref[...] += dgq

            dkv = dk_ref[rs, :]
            dkb = dkv.astype(BF16)
            dvb = dv_ref[rs, :].astype(BF16)
            dwk_ref[...] += _dot(kvn, dkb, _TN)
            dwv_ref[...] += _dot(kvn, dvb, _TN)
            dkvn = _dot(dkb, wk_ref[...], _NT) + _dot(dvb, wv_ref[...], _NT)
            dkl, dgkv = _rms_bwd(dkvn, kh, rk, gkvv)
            dgkv_ref[...] += dgkv

            dkpe = dkv[:, :HEAD_PAD]
            for h in range(1, C_HEADS):
                dkpe = dkpe + dkv[:, h * HEAD_PAD:(h + 1) * HEAD_PAD]
            dkpe = _rope_t(jnp.where(_kpe_mask(dkpe.shape), dkpe, 0.0), cst, s1t, s2t)
            dz_ref[rs, :] = jnp.concatenate([dql, dkl, dkpe], axis=1).astype(BF16)

    full = lambda r, c: pl.BlockSpec((r, c), lambda i: (0, 0))
    return pl.pallas_call(
        body, name="mla_pre_bwd",
        out_shape=(_sds((s, _ZQ), BF16), _sds((C_Q_RANK, _HW)), _sds((C_KV_RANK, _HW)), _sds((C_KV_RANK, _HW)),
                   _sds((1, C_Q_RANK)), _sds((1, C_KV_RANK))),
        grid=(s // tm,),
        in_specs=[_rows(tm, _ZQ), _vec(C_Q_RANK), _vec(C_KV_RANK), full(C_Q_RANK, _HW), full(C_KV_RANK, _HW),
                  full(C_KV_RANK, _HW), _rows(tm, LANES), _rows(tm, LANES), _rows(tm, LANES),
                  _rows(tm, _HW), _rows(tm, _HW), _rows(tm, _HW)],
        out_specs=(_rows(tm, _ZQ), full(C_Q_RANK, _HW), full(C_KV_RANK, _HW), full(C_KV_RANK, _HW),
                   _vec(C_Q_RANK), _vec(C_KV_RANK)),
        compiler_params=_params(("arbitrary",), VMEM_BIG),
    )(z, gq, gkv, wq, wk, wv, cs, s1, s2, dq, dk, dv)


_ATT_SCALE = (C_NOPE + C_ROPE) ** -0.5
_NEG = -1e30


def _att_exp(q, k, row0, ends_here):
    sc = _dot(q, k, _NT) * _ATT_SCALE
    tq, nk = sc.shape
    if ends_here:
        last = sc[:, nk - tq:]
        row = lax.broadcasted_iota(jnp.int32, last.shape, 0)
        col = lax.broadcasted_iota(jnp.int32, last.shape, 1)
        last = jnp.where(col <= row, last, _NEG)
        sc = last if nk == tq else jnp.concatenate([sc[:, :nk - tq], last], axis=1)
    else:
        qpos = row0 + lax.broadcasted_iota(jnp.int32, sc.shape, 0)
        kpos = lax.broadcasted_iota(jnp.int32, sc.shape, 1)
        sc = jnp.where(kpos <= qpos, sc, _NEG)
    e = jnp.exp(sc - jnp.max(sc, axis=-1, keepdims=True))
    return e, 1.0 / jnp.sum(e, axis=-1, keepdims=True)


def _causal_cases(i, nq, tq, fn):
    if nq > 8:
        fn(nq * tq, False)
        return
    for blk in range(nq):
        pl.when(i == blk)(functools.partial(fn, (blk + 1) * tq, True))


_FWD_HEADS_PER_STEP = 4
_BWD_HEADS_PER_STEP = 2


def _head_lanes(heads):
    return [slice(h * HEAD_PAD, (h + 1) * HEAD_PAD) for h in range(heads)]


def attn_fwd(q, k, v):
    s = q.shape[0]
    tq = _tile(s, 256, 8)
    nq = s // tq
    heads = _FWD_HEADS_PER_STEP
    wide = heads * HEAD_PAD

    def body(q_ref, k_ref, v_ref, o_ref):
        i = pl.program_id(1)

        def case(nk, ends_here):
            for hd in _head_lanes(heads):
                e, inv = _att_exp(q_ref[:, hd], k_ref[:nk, hd], i * tq, ends_here)
                o_ref[:, hd] = (_dot(e.astype(BF16), v_ref[:nk, hd]) * inv).astype(BF16)

        _causal_cases(i, nq, tq, case)

    qspec = pl.BlockSpec((tq, wide), lambda h, i: (i, h))
    kspec = pl.BlockSpec((s, wide), lambda h, i: (0, h))
    return pl.pallas_call(
        body, name="attn_fwd", out_shape=_sds((s, _HW + _DW), BF16), grid=(C_HEADS // heads, s // tq),
        in_specs=[qspec, kspec, kspec], out_specs=qspec,
        compiler_params=_params(("parallel", "parallel"), VMEM_BIG),
    )(q, k, v)


def attn_bwd(q, k, v, o, do_all):
    s = q.shape[0]
    tq = _tile(s, 256, 8)
    heads = _BWD_HEADS_PER_STEP
    wide = heads * HEAD_PAD

    def body(q_ref, k_ref, v_ref, o_ref, do_ref, dq_ref, dk_ref, dv_ref):
        i = pl.program_id(1)

        @pl.when(i == 0)
        def _():
            dk_ref[...] = jnp.zeros_like(dk_ref)
            dv_ref[...] = jnp.zeros_like(dv_ref)

        def case(nk, ends_here):
            for hd in _head_lanes(heads):
                qv, kv, vv, dov = q_ref[:, hd], k_ref[:nk, hd], v_ref[:nk, hd], do_ref[:, hd]
                e, inv = _att_exp(qv, kv, i * tq, ends_here)
                p = e * inv
                dp = _dot(dov, vv, _NT)
                delta = jnp.sum(dov.astype(F32) * o_ref[:, hd].astype(F32), axis=-1, keepdims=True)
                ds = (p * (dp - delta) * _ATT_SCALE).astype(BF16)
                dq_ref[:, hd] = _dot(ds, kv).astype(BF16)
                dk_ref[:nk, hd] += _dot(ds, qv, _TN)
                dv_ref[:nk, hd] += _dot(p.astype(BF16), dov, _TN)

        _causal_cases(i, s // tq, tq, case)

    qspec = pl.BlockSpec((tq, wide), lambda h, i: (i, h))
    kspec = pl.BlockSpec((s, wide), lambda h, i: (0, h))
    return pl.pallas_call(
        body, name="attn_bwd", out_shape=(_sds((s, _HW), BF16), _sds((s, _HW)), _sds((s, _HW))),
        grid=(C_HEADS // heads, s // tq),
        in_specs=[qspec, kspec, kspec, qspec, qspec], out_specs=(qspec, kspec, kspec),
        compiler_params=_params(("parallel", "arbitrary"), VMEM_BIG),
    )(q, k, v, o, do_all)


_DW = D_GROUPS * LANES


def _tril_bf16(w):
    r = lax.broadcasted_iota(jnp.int32, w.shape, 0)
    c = lax.broadcasted_iota(jnp.int32, w.shape, 1)
    return jnp.where(c <= r, w, 0.0).astype(BF16)


def _sgu_forward(zu, zv, lg, lb, ws_ref, bs):
    u = _gelu(zu)
    v = _gelu(zv)
    mu = jnp.mean(v, axis=-1, keepdims=True)
    vc = v - mu
    rstd = lax.rsqrt(jnp.mean(vc * vc, axis=-1, keepdims=True) + EPS)
    xh = vc * rstd
    vln = (xh * lg + lb).astype(BF16)
    mixed = []
    for g in range(D_GROUPS):
        wg = _tril_bf16(ws_ref[g])
        mixed.append(_dot(wg, vln[:, g * LANES:(g + 1) * LANES]) + bs[:, g:g + 1])
    return u, xh, rstd, vln, jnp.concatenate(mixed, axis=1)


_SGU_CHUNKS = 4


def sgu_fwd(z, lg, lb, ws, bs_t, ycat):
    s = z.shape[0]
    rows = _SGU_CHUNKS * D_CHUNK

    def body(zu_ref, zv_ref, lg_ref, lb_ref, ws_ref, bs_ref, ycat_ref, o_ref):
        del ycat_ref
        for c in range(_SGU_CHUNKS):
            rs = slice(c * D_CHUNK, (c + 1) * D_CHUNK)
            u, _, _, _, mixed = _sgu_forward(zu_ref[rs, :].astype(F32), zv_ref[rs, :].astype(F32), lg_ref[...],
                                             lb_ref[...], ws_ref, bs_ref[...])
            o_ref[rs, :] = (u * mixed).astype(BF16)

    return pl.pallas_call(
        body, name="sgu_fwd", out_shape=_sds(ycat.shape, BF16), grid=(s // rows,),
        in_specs=[pl.BlockSpec((rows, _DW), lambda n: (n, 1)), pl.BlockSpec((rows, _DW), lambda n: (n, 2)),
                  _vec(_DW), _vec(_DW), pl.BlockSpec((D_GROUPS, D_CHUNK, D_CHUNK), lambda n: (0, 0, 0)),
                  pl.BlockSpec((D_CHUNK, LANES), lambda n: (0, 0)), pl.BlockSpec(memory_space=pl.ANY)],
        out_specs=pl.BlockSpec((rows, _DW), lambda n: (n, _HW // _DW)),
        input_output_aliases={6: 0},
        compiler_params=_params(("parallel",)),
    )(z, z, lg, lb, ws, bs_t, ycat)


def sgu_bwd(z, lg, lb, ws, bs_t, dycat, dy_col):
    s = z.shape[0]
    rows = _SGU_CHUNKS * D_CHUNK

    def body(zu_ref, zv_ref, lg_ref, lb_ref, ws_ref, bs_ref, dy_ref, dzu_ref, dzv_ref, dws_ref, dbs_ref, dlg_ref,
             dlb_ref):
        @pl.when(pl.program_id(0) == 0)
        def _():
            dws_ref[...] = jnp.zeros_like(dws_ref)
            dbs_ref[...] = jnp.zeros_like(dbs_ref)
            dlg_ref[...] = jnp.zeros_like(dlg_ref)
            dlb_ref[...] = jnp.zeros_like(dlb_ref)

        lg = lg_ref[...]
        lane = lax.broadcasted_iota(jnp.int32, (D_CHUNK, LANES), 1)
        row = lax.broadcasted_iota(jnp.int32, (D_CHUNK, D_CHUNK), 0)
        colm = lax.broadcasted_iota(jnp.int32, (D_CHUNK, D_CHUNK), 1)
        for c in range(_SGU_CHUNKS):
            rs = slice(c * D_CHUNK, (c + 1) * D_CHUNK)
            zu, zv = zu_ref[rs, :].astype(F32), zv_ref[rs, :].astype(F32)
            u, xh, rstd, vln, mixed = _sgu_forward(zu, zv, lg, lb_ref[...], ws_ref, bs_ref[...])
            dy = dy_ref[rs, :].astype(F32)
            dzu_ref[rs, :] = (dy * mixed * _gelu_grad(zu)).astype(BF16)
            dmix = dy * u
            dvln = []
            dbs = jnp.zeros((D_CHUNK, LANES), F32)
            for g in range(D_GROUPS):
                sl = slice(g * LANES, (g + 1) * LANES)
                dmg = dmix[:, sl]
                dbs = dbs + jnp.where(lane == g, jnp.sum(dmg, axis=-1, keepdims=True), 0.0)
                dmb = dmg.astype(BF16)
                dws_ref[g] += jnp.where(colm <= row, _dot(dmb, vln[:, sl], _NT), 0.0)
                dvln.append(_dot(_tril_bf16(ws_ref[g]), dmb, _TN))
            dbs_ref[...] += dbs
            dvln = jnp.concatenate(dvln, axis=1)
            dlg_ref[...] += jnp.sum(dvln * xh, axis=0, keepdims=True)
            dlb_ref[...] += jnp.sum(dvln, axis=0, keepdims=True)
            dxh = dvln * lg
            dvv = rstd * (dxh - jnp.mean(dxh, axis=-1, keepdims=True)
                          - xh * jnp.mean(dxh * xh, axis=-1, keepdims=True))
            dzv_ref[rs, :] = (dvv * _gelu_grad(zv)).astype(BF16)

    wsspec = pl.BlockSpec((D_GROUPS, D_CHUNK, D_CHUNK), lambda n: (0, 0, 0))
    chunk = lambda cidx: pl.BlockSpec((rows, _DW), lambda n: (n, cidx))
    return pl.pallas_call(
        body, name="sgu_bwd",
        out_shape=(_sds((s, _DW), BF16), _sds((s, _DW), BF16), _sds((D_GROUPS, D_CHUNK, D_CHUNK)),
                   _sds((D_CHUNK, LANES)), _sds((1, _DW)), _sds((1, _DW))),
        grid=(s // rows,),
        in_specs=[chunk(1), chunk(2), _vec(_DW), _vec(_DW), wsspec, pl.BlockSpec((D_CHUNK, LANES), lambda n: (0, 0)),
                  chunk(dy_col)],
        out_specs=(chunk(0), chunk(0), wsspec, pl.BlockSpec((D_CHUNK, LANES), lambda n: (0, 0)), _vec(_DW), _vec(_DW)),
        compiler_params=_params(("arbitrary",)),
    )(z, z, lg, lb, ws, bs_t, dycat)


def ada_mod(c_all, ada_w, ada_b):
    nl, d, n = ada_w.shape
    nb = c_all.shape[0]
    tn = _tile(n, 512)

    def body(c_ref, w_ref, b_ref, o_ref):
        cv = c_ref[...]
        ca = (cv * _sigmoid(cv)).astype(BF16)
        o_ref[...] = _dot(ca, w_ref[...].astype(BF16)) + b_ref[...]

    return pl.pallas_call(
        body, name="ada_mod", out_shape=_sds((nl, nb, n)), grid=(nl, n // tn),
        in_specs=[pl.BlockSpec((nb, d), lambda l, j: (0, 0)), pl.BlockSpec((None, d, tn), lambda l, j: (l, 0, j)),
                  pl.BlockSpec((None, 1, tn), lambda l, j: (l, 0, j))],
        out_specs=pl.BlockSpec((None, nb, tn), lambda l, j: (l, 0, j)),
        compiler_params=_params(("parallel", "parallel")),
    )(c_all, ada_w, ada_b.reshape(nl, 1, n))


_ADAM_BLOCK = 256 * 1024


def _adam_rows(rows, cols):
    if rows * cols <= _ADAM_BLOCK or rows % 8:
        return rows
    return _tile(rows, max(8, _ADAM_BLOCK // cols), 8)


def _adam_update(w, gv, m, v):
    inv_bc1 = 1.0 / (1.0 - ADAM_B1 ** ADAM_STEP)
    inv_bc2 = 1.0 / (1.0 - ADAM_B2 ** ADAM_STEP)
    nm = ADAM_B1 * m + (1.0 - ADAM_B1) * gv
    nv = ADAM_B2 * v + (1.0 - ADAM_B2) * (gv * gv)
    return -ADAM_LR * ((nm * inv_bc1) / (jnp.sqrt(nv * inv_bc2) + ADAM_EPS) + ADAM_WD * w), nm, nv


def adamw(w, g, m, v, name):
    shape = w.shape
    cols = shape[-1]
    rows = w.size // cols
    tr = _adam_rows(rows, cols)

    def body(w_ref, g_ref, m_ref, v_ref, go_ref, d_ref, nm_ref, nv_ref):
        gv = g_ref[...]
        go_ref[...] = gv
        d_ref[...], nm_ref[...], nv_ref[...] = _adam_update(w_ref[...], gv, m_ref[...], v_ref[...])

    spec = pl.BlockSpec((tr, cols), lambda i: (i, 0))
    out = _sds((rows, cols))
    r2 = lambda t: t.reshape(rows, cols)
    res = pl.pallas_call(
        body, name=name, out_shape=(out,) * 4, grid=(rows // tr,),
        in_specs=[spec] * 4, out_specs=(spec,) * 4, compiler_params=_params(("parallel",)),
    )(r2(w), r2(g), r2(m), r2(v))
    return tuple(t.reshape(shape) for t in res)


def adamw_ada(w, c_all, dmod, m, v):
    nl, d, n = w.shape
    tr = _adam_rows(d, n)
    pad = 16 - c_all.shape[0]
    c16 = jnp.pad(c_all, ((0, pad), (0, 0)))
    dm16 = jnp.pad(dmod, ((0, 0), (0, pad), (0, 0)))

    def body(w_ref, c_ref, dm_ref, m_ref, v_ref, g_ref, d_ref, nm_ref, nv_ref):
        cv = c_ref[...]
        gv = _dot((cv * _sigmoid(cv)).astype(BF16), dm_ref[...].astype(BF16), _TN)
        g_ref[...] = gv
        d_ref[...], nm_ref[...], nv_ref[...] = _adam_update(w_ref[...], gv, m_ref[...], v_ref[...])

    spec = pl.BlockSpec((None, tr, n), lambda l, i: (l, i, 0))
    out = _sds((nl, d, n))
    return pl.pallas_call(
        body, name="adamw_ada_w", out_shape=(out, out, out, out), grid=(nl, d // tr),
        in_specs=[spec, pl.BlockSpec((16, tr), lambda l, i: (0, i)), pl.BlockSpec((None, 16, n), lambda l, i: (l, 0, 0)),
                  spec, spec],
        out_specs=(spec,) * 4, compiler_params=_params(("parallel", "parallel")),
    )(w, c16, dm16, m, v)


def adamw_small(ws, gs, ms, vs):
    n = len(ws)
    flat = lambda t: t.reshape(-1, t.shape[-1])

    def body(*refs):
        ins, outs = refs[:4 * n], refs[4 * n:]
        for i in range(n):
            w_ref, g_ref, m_ref, v_ref = ins[4 * i:4 * i + 4]
            outs[3 * i][...], outs[3 * i + 1][...], outs[3 * i + 2][...] = _adam_update(
                w_ref[...], g_ref[...], m_ref[...], v_ref[...])

    operands = [flat(t) for quad in zip(ws, gs, ms, vs) for t in quad]
    res = pl.pallas_call(
        body, name="adamw_small", out_shape=tuple(_sds(flat(w).shape) for w in ws for _ in range(3)),
    )(*operands)
    return [(g, res[3 * i].reshape(w.shape), res[3 * i + 1].reshape(w.shape), res[3 * i + 2].reshape(w.shape))
            for i, (w, g) in enumerate(zip(ws, gs))]


def adamw_layers(w, g0, g1, m, v, name):
    _, rows, cols = w.shape
    tr = _adam_rows(rows, cols)

    def body(w_ref, g0_ref, g1_ref, m_ref, v_ref, g_ref, d_ref, nm_ref, nv_ref):
        gv = jnp.where(pl.program_id(0) == 0, g0_ref[...], g1_ref[...])
        g_ref[...] = gv
        d_ref[...], nm_ref[...], nv_ref[...] = _adam_update(w_ref[...], gv, m_ref[...], v_ref[...])

    spec = pl.BlockSpec((None, tr, cols), lambda l, i: (l, i, 0))
    gspec = pl.BlockSpec((tr, cols), lambda l, i: (i, 0))
    out = _sds((2, rows, cols))
    return pl.pallas_call(
        body, name=name, out_shape=(out, out, out, out), grid=(2, rows // tr),
        in_specs=[spec, gspec, gspec, spec, spec], out_specs=(spec,) * 4, compiler_params=_params(("parallel", "parallel")),
    )(w, g0, g1, m, v)


def sum8(gathered):
    _, r, _ = gathered.shape
    tr = _tile(r, 512, 8)

    def body(g_ref, o_ref):
        acc = g_ref[0]
        for dev in range(1, N_DEV):
            acc = acc + g_ref[dev]
        o_ref[...] = acc

    return pl.pallas_call(
        body, name="sum8", out_shape=_sds((r, LANES)), grid=(r // tr,),
        in_specs=[pl.BlockSpec((N_DEV, tr, LANES), lambda i: (0, i, 0))], out_specs=pl.BlockSpec((tr, LANES), lambda i: (i, 0)),
        compiler_params=_params(("parallel",)),
    )(gathered)


_SUM_STEPS = 2


def pair_sums(gs, recvs, core, name):
    n = len(gs)
    trs = [g.shape[1] // 2 // _SUM_STEPS for g in gs]

    def body(c_ref, *refs):
        del c_ref
        for i in range(n):
            a_ref, b_ref, o_ref = refs[2 * i], refs[2 * i + 1], refs[2 * n + i]
            o_ref[...] = (a_ref[...].astype(F32) + b_ref[...].astype(F32)).astype(BF16)

    in_specs, out_specs = [], []
    for g, tr in zip(gs, trs):
        cols = g.shape[2]
        in_specs.append(pl.BlockSpec((None, tr, cols), lambda k, s, c: (k, c[0] * _SUM_STEPS + s, 0)))
        in_specs.append(pl.BlockSpec((None, tr, cols), lambda k, s, c: (k, s, 0)))
        out_specs.append(pl.BlockSpec((None, tr, cols), lambda k, s, c: (k, s, 0)))
    grid_spec = pltpu.PrefetchScalarGridSpec(num_scalar_prefetch=1, grid=(N_CHIPS, _SUM_STEPS), in_specs=in_specs,
                                             out_specs=tuple(out_specs))
    return list(pl.pallas_call(
        body, name=name, out_shape=tuple(_sds((N_CHIPS, g.shape[1] // 2, g.shape[2]), BF16) for g in gs),
        grid_spec=grid_spec, compiler_params=_params(("parallel", "parallel")),
    )(core.reshape(1).astype(jnp.int32), *[t for pair in zip(gs, recvs) for t in pair]))


def chip_sums(pairs, recvs, chip, core, name):
    n = len(pairs)
    trs = [p.shape[1] // _SUM_STEPS for p in pairs]

    def body(p_ref, *refs):
        del p_ref
        for i in range(n):
            own_ref, r_ref, o_ref = refs[2 * i], refs[2 * i + 1], refs[2 * n + i]
            acc = own_ref[...].astype(F32)
            for j in range(N_CHIPS - 1):
                acc = acc + r_ref[j].astype(F32)
            o_ref[...] = acc

    in_specs, out_specs = [], []
    for p, tr in zip(pairs, trs):
        cols = p.shape[2]
        in_specs.append(pl.BlockSpec((None, tr, cols), lambda s, q: (q[0], s, 0)))
        in_specs.append(pl.BlockSpec((N_CHIPS - 1, tr, cols), lambda s, q: (0, s, 0)))
        out_specs.append(pl.BlockSpec((None, tr, cols), lambda s, q: (q[1], s, 0)))
    grid_spec = pltpu.PrefetchScalarGridSpec(num_scalar_prefetch=1, grid=(_SUM_STEPS,), in_specs=in_specs,
                                             out_specs=tuple(out_specs))
    return list(pl.pallas_call(
        body, name=name, out_shape=tuple(_sds((2,) + p.shape[1:]) for p in pairs), grid_spec=grid_spec,
        compiler_params=_params(("parallel",)),
    )(jnp.stack([chip, core]).astype(jnp.int32), *[t for pair in zip(pairs, recvs) for t in pair]))


def _place():
    return lax.axis_index("x"), lax.axis_index("y"), lax.axis_index("c")


def _other_chips(x, y):
    return [(x, 1 - y), (1 - x, y), (1 - x, 1 - y)]


_HBM = pl.BlockSpec(memory_space=pltpu.HBM)


def all_gather8(v, name, after=()):
    m, n = v.shape

    def body(x_ref, *refs):
        out_ref, send_sems, recv_sems, local_sem = refs[len(after):]
        x, y, c = _place()
        me, sibling = (x, y, c), (x, y, 1 - c)
        chips = _other_chips(x, y)

        def rows(px, py, pc):
            return out_ref.at[pl.ds((4 * px + 2 * py + pc) * m, m), :]

        def copy(k, block, to, src=None):
            return pltpu.make_async_remote_copy(
                src_ref=rows(*block) if src is None else src, dst_ref=rows(*block),
                send_sem=send_sems.at[k], recv_sem=recv_sems.at[k], device_id=to, device_id_type=MESH)

        mine = pltpu.make_async_copy(x_ref, rows(*me), local_sem)
        mine.start()
        first = [copy(0, me, sibling, src=x_ref)]
        first += [copy(1 + j, me, (*chip, c), src=x_ref) for j, chip in enumerate(chips)]
        for cp in first:
            cp.start()
        passed = [copy(4 + j, (*chip, c), sibling) for j, chip in enumerate(chips)]
        for j, chip in enumerate(chips):
            copy(1 + j, (*chip, c), me).wait_recv()
            passed[j].start()
        copy(0, sibling, me).wait_recv()
        for j, chip in enumerate(chips):
            copy(4 + j, (*chip, 1 - c), me).wait_recv()
        for cp in first + passed:
            cp.wait_send()
        mine.wait()

    return pl.pallas_call(
        body, name=name, out_shape=_sds((N_DEV * m, n), v.dtype),
        in_specs=[pl.BlockSpec(memory_space=pltpu.VMEM)] + [pl.BlockSpec(memory_space=pl.ANY)] * len(after),
        out_specs=pl.BlockSpec(memory_space=pltpu.VMEM),
        scratch_shapes=[pltpu.SemaphoreType.DMA((7,)), pltpu.SemaphoreType.DMA((7,)), pltpu.SemaphoreType.DMA],
        compiler_params=_params(None, VMEM_BIG),
    )(v, *after)


def _comm_call(body, name, ins, out_shapes, nsem, aliases=None):
    return pl.pallas_call(
        body, name=name, out_shape=tuple(out_shapes), in_specs=[_HBM] * len(ins), out_specs=tuple([_HBM] * len(out_shapes)),
        scratch_shapes=[pltpu.SemaphoreType.DMA((nsem,)), pltpu.SemaphoreType.DMA((nsem,))],
        input_output_aliases=aliases or {},
    )(*ins)


def _remote(src, dst, send_sems, recv_sems, k, to):
    return pltpu.make_async_remote_copy(src_ref=src, dst_ref=dst, send_sem=send_sems.at[k], recv_sem=recv_sems.at[k],
                                        device_id=to, device_id_type=MESH)


def _half(core, rh):
    return pl.ds(pl.multiple_of(core * rh, 16), rh)


def swap_halves(gs, name):
    n = len(gs)

    def body(*refs):
        ins, outs, (send_sems, recv_sems) = refs[:n], refs[n:2 * n], refs[2 * n:]
        x, y, c = _place()
        copies = []
        for i in range(n):
            theirs = _half(1 - c, ins[i].shape[1] // 2)
            cp = _remote(ins[i].at[:, theirs], outs[i], send_sems, recv_sems, i, (x, y, 1 - c))
            cp.start()
            copies.append(cp)
        for cp in copies:
            cp.wait()

    return _comm_call(body, name, gs, [_sds((g.shape[0], g.shape[1] // 2, g.shape[2]), g.dtype) for g in gs], n)


def join_halves(bufs, name):
    n = len(bufs)

    def body(*refs):
        ins, outs, (send_sems, recv_sems) = refs[:n], refs[n:2 * n], refs[2 * n:]
        x, y, c = _place()
        copies = []
        for i in range(n):
            cp = _remote(ins[i].at[c], outs[i].at[c], send_sems, recv_sems, i, (x, y, 1 - c))
            cp.start()
            copies.append(cp)
        for i in range(n):
            theirs = outs[i].at[1 - c]
            _remote(theirs, theirs, send_sems, recv_sems, i, (x, y, 1 - c)).wait_recv()
        for cp in copies:
            cp.wait_send()

    return _comm_call(body, name, bufs, [_sds(b.shape, b.dtype) for b in bufs], n, {i: i for i in range(n)})


def forward_halves(lands, name):
    n = len(lands)

    def body(*refs):
        ins, outs, (send_sems, recv_sems) = refs[:n], refs[n:2 * n], refs[2 * n:]
        x, y, c = _place()
        sibling = (x, y, 1 - c)
        chips = _other_chips(x, y)
        copies = []
        for i in range(n):
            mine = _half(c, ins[i].shape[1] // 2)
            for j, (px, py) in enumerate(chips):
                cp = _remote(ins[i].at[2 * px + py, mine], outs[i].at[2 * px + py, mine], send_sems, recv_sems, 3 * i + j, sibling)
                cp.start()
                copies.append(cp)
        for i in range(n):
            theirs = _half(1 - c, ins[i].shape[1] // 2)
            for j, (px, py) in enumerate(chips):
                landed = outs[i].at[2 * px + py, theirs]
                _remote(landed, landed, send_sems, recv_sems, 3 * i + j, sibling).wait_recv()
        for cp in copies:
            cp.wait_send()

    return _comm_call(body, name, lands, [_sds(b.shape, b.dtype) for b in lands], 3 * n, {i: i for i in range(n)})


_SEM = pl.BlockSpec(memory_space=pltpu.SEMAPHORE)
_EFFECT = pltpu.SideEffectType.DATAFLOW_SIDE_EFFECTING


def _gather_copies(srcs, lands, send_sems, recv_sems):
    x, y, c = _place()
    copies = []
    for i in range(len(srcs)):
        mine = _half(c, srcs[i].shape[0] // 2)
        for j, chip in enumerate(_other_chips(x, y)):
            copies.append(_remote(srcs[i].at[mine], lands[i].at[2 * x + y, mine], send_sems, recv_sems, 3 * i + j, (*chip, c)))
    return copies


def _exchange_copies(srcs, lands, send_sems, recv_sems):
    x, y, c = _place()
    copies = []
    for i in range(len(srcs)):
        for j, (px, py) in enumerate(_other_chips(x, y)):
            copies.append(_remote(srcs[i].at[2 * px + py], lands[i].at[j], send_sems, recv_sems, 3 * i + j, (px, py, c)))
    return copies


def _everyone_copies(srcs, lands, send_sems, recv_sems):
    x, y, c = _place()
    flip = lambda v, b: 1 - v if b else v
    dst = lands[0].at[4 * x + 2 * y + c]
    return [_remote(srcs[0], dst, send_sems, recv_sems, j - 1, (flip(x, j & 4), flip(y, j & 2), flip(c, j & 1)))
            for j in range(1, N_DEV)]


GATHER = (_gather_copies, 3)
EXCHANGE = (_exchange_copies, 3)
EVERYONE = (_everyone_copies, N_DEV - 1)


def split_start(name, plan, srcs, land_shapes, after=()):
    copies_fn, per_source = plan
    n, m, k = len(srcs), len(land_shapes), len(after)
    ncopies = per_source * n

    def body(*refs):
        src_refs, land_refs = refs[:n], refs[n:n + m]
        send_sems, recv_sems = refs[n + m + k], refs[n + m + k + 1]
        token = refs[-1]
        for cp in copies_fn(src_refs, land_refs, send_sems, recv_sems):
            cp.start()
        token[...] = jnp.zeros_like(token)

    hbm = lambda s: pltpu.HBM(tuple(s.shape), s.dtype)
    outs = pl.pallas_call(
        body, name=name,
        out_shape=(pltpu.SemaphoreType.DMA((ncopies,)), pltpu.SemaphoreType.DMA((ncopies,)), *[hbm(s) for s in srcs],
                   *[hbm(s) for s in land_shapes], _sds((8, LANES))),
        in_specs=[_HBM] * (n + m) + [pl.BlockSpec(memory_space=pl.ANY)] * k,
        out_specs=(_SEM, _SEM, *([_HBM] * (n + m)), pl.BlockSpec(memory_space=pltpu.VMEM)),
        input_output_aliases={i: 2 + i for i in range(n + m)},
        compiler_params=pltpu.CompilerParams(has_side_effects=_EFFECT),
    )(*[pltpu.with_memory_space_constraint(s, pltpu.HBM) for s in srcs],
      *[pltpu.with_memory_space_constraint(lax.empty(tuple(s.shape), s.dtype), pltpu.HBM) for s in land_shapes], *after)
    handle = (outs[0], outs[1], list(outs[2:2 + n]), list(outs[2 + n:2 + n + m]))
    return handle, outs[-1][0, 0]


def split_wait(name, plan, handle, after):
    copies_fn, _ = plan
    send_sems, recv_sems, srcs, lands = handle
    n, m = len(srcs), len(lands)
    after = list(after) if isinstance(after, (list, tuple)) else [after]

    def body(*refs):
        src_refs, land_refs = refs[:n], refs[n:n + m]
        for cp in copies_fn(src_refs, land_refs, refs[n + m], refs[n + m + 1]):
            cp.wait_send()
            cp.wait_recv()

    hbm = lambda s: pltpu.HBM(tuple(s.shape), s.dtype)
    outs = pl.pallas_call(
        body, name=name, out_shape=tuple(hbm(s) for s in srcs + lands),
        in_specs=[_HBM] * (n + m) + [_SEM, _SEM] + [pl.BlockSpec(memory_space=pl.ANY)] * len(after),
        out_specs=tuple([_HBM] * (n + m)), input_output_aliases={i: i for i in range(n + m)},
        compiler_params=pltpu.CompilerParams(has_side_effects=_EFFECT),
    )(*srcs, *lands, send_sems, recv_sems, *after)
    return list(outs[:n]), list(outs[n:])


def chip_major(w, groups=N_CHIPS):
    r, c = w.shape
    return w.reshape(r, groups, c // groups).transpose(1, 0, 2)


def from_chip_major(w):
    g, r, c = w.shape
    return w.transpose(1, 0, 2).reshape(r, g * c)


def _cd_in_pad(w):
    a = C_Q_RANK + C_KV_RANK
    z = lambda n: jnp.zeros((w.shape[0], n), w.dtype)
    return jnp.concatenate([w[:, :a], z(C_NOPE), w[:, a:a + C_ROPE], z(HEAD_PAD - C_NOPE - C_ROPE), w[:, a + C_ROPE:]], axis=1)


def _cd_in_unpad(w):
    a = C_Q_RANK + C_KV_RANK
    return jnp.concatenate([w[:, :a], w[:, a + C_NOPE:a + C_NOPE + C_ROPE], w[:, a + HEAD_PAD:]], axis=1)


def _pad_heads(w, width):
    r = w.shape[0]
    w = w.reshape(r, C_HEADS, width)
    return jnp.pad(w, ((0, 0), (0, 0), (0, HEAD_PAD - width))).reshape(r, _HW)


def _unpad_heads(w, width):
    r = w.shape[0]
    return w.reshape(r, C_HEADS, HEAD_PAD)[:, :, :width].reshape(r, C_HEADS * width)


def prepare_weights(p):
    q = dict(p)
    q["cd_w_in"] = _cd_in_pad(p["cd_w_in"])
    q["c_w_uq"] = _pad_heads(p["c_w_uq"], C_NOPE + C_ROPE)
    ukv = p["c_w_ukv"].reshape(C_KV_RANK, C_HEADS, C_NOPE + C_V)
    q["c_w_uk"] = _pad_heads(ukv[:, :, :C_NOPE].reshape(C_KV_RANK, -1), C_NOPE)
    q["c_w_uv"] = _pad_heads(ukv[:, :, C_NOPE:].reshape(C_KV_RANK, -1), C_V)
    wo = p["cd_w_out"]
    att_rows = jnp.pad(wo[:C_HEADS * C_V].reshape(C_HEADS, C_V, D_MODEL), ((0, 0), (0, HEAD_PAD - C_V), (0, 0)))
    q["cd_w_out"] = jnp.concatenate([att_rows.reshape(_HW, D_MODEL), wo[C_HEADS * C_V:]], axis=0)
    return q


def unprepare_grads(g):
    q = dict(g)
    q["cd_w_in"] = _cd_in_unpad(g["cd_w_in"])
    q["c_w_uq"] = _unpad_heads(g["c_w_uq"], C_NOPE + C_ROPE)
    uk = g.pop("c_w_uk").reshape(C_KV_RANK, C_HEADS, HEAD_PAD)[:, :, :C_NOPE]
    uv = g.pop("c_w_uv").reshape(C_KV_RANK, C_HEADS, HEAD_PAD)[:, :, :C_V]
    q.pop("c_w_uk", None)
    q.pop("c_w_uv", None)
    q["c_w_ukv"] = jnp.concatenate([uk, uv], axis=-1).reshape(C_KV_RANK, C_HEADS * (C_NOPE + C_V))
    wo = g["cd_w_out"]
    att = wo[:_HW].reshape(C_HEADS, HEAD_PAD, D_MODEL)[:, :C_V].reshape(C_HEADS * C_V, D_MODEL)
    q["cd_w_out"] = jnp.concatenate([att, wo[_HW:]], axis=0)
    return q


def rope_tables(positions):
    half = C_ROPE // 2
    inv_freq = ROPE_THETA ** (-jnp.arange(half, dtype=F32) / half)
    ang = positions.astype(F32)[:, None] * inv_freq
    cos, sin = jnp.cos(ang), jnp.sin(ang)
    s = positions.shape[0]
    z = lambda n: jnp.zeros((s, n), F32)
    cs = jnp.concatenate([jnp.ones((s, C_NOPE), F32), cos, cos, z(HEAD_PAD - C_NOPE - C_ROPE)], axis=1)
    s1 = jnp.concatenate([z(C_NOPE), -sin, z(HEAD_PAD - C_NOPE - half)], axis=1)
    s2 = jnp.concatenate([z(C_NOPE + half), sin, z(HEAD_PAD - C_NOPE - C_ROPE)], axis=1)
    return cs, s1, s2


_UP_COLS = 2 * D_FF // N_CHIPS


def ffn_fwd(h2, w, l, late_down=None):
    zf = matmul(h2, w["ffn_w_up"][l], "nn", BF16, f"ffn_up{l}", gb=N_CHIPS, go=2, tn=_UP_COLS)
    if late_down is not None:
        late_down(zf)
    a, f = ffn_act_down(zf, w["ffn_conv_w"][l], w["ffn_w_down"][l], f"ffn_act_down{l}")
    return f, (zf, a)


def ffn_bwd(df, h2, saved, w, l):
    zf, a = saved
    da = matmul(df, w["ffn_w_down"][l], "nt", BF16, f"ffn_down_dx{l}", tn=D_FF // 2)
    d_down = matmul(a, df, "tn", BF16, f"ffn_down_dw{l}", tm=D_FF // 2)
    dzf, d_conv = ffn_act_bwd(zf, w["ffn_conv_w"][l], da, f"ffn_act_bwd{l}")
    dh2 = matmul(dzf, w["ffn_w_up"][l], "nt", BF16, f"ffn_up_dx{l}", ga=2, gb=N_CHIPS, tk=_UP_COLS, tn=D_MODEL)
    d_up = matmul(h2, dzf, "tn", BF16, f"ffn_up_dw{l}", gb=2, go=N_CHIPS, tn=_UP_COLS)
    d_conv = d_conv.transpose(1, 0, 2).reshape(3, 2 * D_FF)
    return dh2, dict(ffn_w_down=d_down, ffn_conv_w=d_conv, ffn_w_up=d_up)


def mixer0_fwd(h, w):
    z = matmul(h, w["ab_w_in"], "nn", BF16, "ab_in", gb=N_CHIPS)
    ycat = pool_fwd(z, w["b_mix_w"], w["b_scale"], gconv_fwd(z, w["a_conv_w"]))
    y = matmul(ycat, w["ab_w_out"], "nn", BF16, "ab_out", tn=D_MODEL)
    return y, (z, ycat)


def mixer0_bwd(dy, h, saved, w):
    z, ycat = saved
    grads = {}
    dycat = matmul(dy, w["ab_w_out"], "nt", BF16, "ab_out_dx")
    grads["ab_w_out"] = matmul(ycat, dy, "tn", BF16, "ab_out_dw")
    db, dc, da, d_conv = gconv_bwd(z, w["a_conv_w"], dycat)
    dp, d_mix, d_scale = pool_bwd(z, w["b_mix_w"], w["b_scale"], dycat)
    dz = jnp.concatenate([db, dc, da, dp], axis=1)
    dh = matmul(dz, w["ab_w_in"], "nt", BF16, "ab_in_dx", gb=N_CHIPS, tn=D_MODEL)
    grads["ab_w_in"] = matmul(h, dz, "tn", BF16, "ab_in_dw", go=N_CHIPS)
    grads.update(a_conv_w=d_conv, b_mix_w=d_mix, b_scale=d_scale)
    return dh, grads


def mixer1_fwd(h, ropes, w):
    cs, s1, s2 = ropes
    z = matmul(h, w["cd_w_in"], "nn", BF16, "cd_in")
    bs_t = jnp.pad(w["d_b_s"].T, ((0, 0), (0, LANES - D_GROUPS)))
    qh, kh, vh = mla_pre_fwd(z, w["c_q_norm_g"], w["c_kv_norm_g"], w["c_w_uq"], w["c_w_uk"], w["c_w_uv"], cs, s1, s2)
    ycat = sgu_fwd(z, w["d_ln_g"], w["d_ln_b"], w["d_w_s"], bs_t, attn_fwd(qh, kh, vh))
    y = matmul(ycat, w["cd_w_out"], "nn", BF16, "cd_out", tn=D_MODEL)
    return y, (z, bs_t, qh, kh, vh, ycat)


def mixer1_bwd(dy, h, saved, ropes, w):
    cs, s1, s2 = ropes
    z, bs_t, qh, kh, vh, ycat = saved
    grads = {}
    dycat = matmul(dy, w["cd_w_out"], "nt", BF16, "cd_out_dx")
    grads["cd_w_out"] = matmul(ycat, dy, "tn", BF16, "cd_out_dw")
    dqh, dkh, dvh = attn_bwd(qh, kh, vh, ycat, dycat)
    dzq, d_uq, d_uk, d_uv, d_gq, d_gkv = mla_pre_bwd(
        z, w["c_q_norm_g"], w["c_kv_norm_g"], w["c_w_uq"], w["c_w_uk"], w["c_w_uv"], cs, s1, s2, dqh, dkh, dvh)
    dzu, dzv, d_ws, d_bs, d_lg, d_lb = sgu_bwd(z, w["d_ln_g"], w["d_ln_b"], w["d_w_s"], bs_t, dycat, _HW // _DW)
    dz = jnp.concatenate([dzq, dzu, dzv], axis=1)
    dh = matmul(dz, w["cd_w_in"], "nt", BF16, "cd_in_dx", tn=D_MODEL)
    grads["cd_w_in"] = matmul(h, dz, "tn", BF16, "cd_in_dw")
    grads.update(c_w_uq=d_uq, c_w_uk=d_uk, c_w_uv=d_uv, c_q_norm_g=d_gq, c_kv_norm_g=d_gkv, d_w_s=d_ws,
                 d_b_s=d_bs[:, :D_GROUPS].T, d_ln_g=d_lg, d_ln_b=d_lb)
    return dh, grads


class StepHooks:
    def weights(self, stage, after):
        pass

    def gradients(self, stage, grads, after):
        return 0.0


def run_step(x, tgt, mod, ropes, w, hooks):
    sh1, sc1, g1, sh2, sc2, g2 = range(N_MOD)
    mods = mod.reshape(2, 1, N_MOD * D_MODEL)
    n1 = w["norm1_g"].reshape(2, 1, D_MODEL)
    n2 = w["norm2_g"].reshape(2, 1, D_MODEL)
    final_g = Vec(w["final_norm_g"].reshape(1, 1, D_MODEL), 0, 0)

    hooks.weights("mix0", mod)
    h0 = modnorm_fwd(x, Vec(n1, 0, 0), Vec(mods, 0, sc1), Vec(mods, 0, sh1), "modnorm_0")
    y0, mix0 = mixer0_fwd(h0, w)
    x1, h1 = resid_modnorm_fwd(x, y0, Vec(mods, 0, g1), Vec(n2, 0, 0), Vec(mods, 0, sc2), Vec(mods, 0, sh2), "resid_modnorm_1")
    hooks.weights("up0", x1)
    f0, ffn0 = ffn_fwd(h1, w, 0, lambda act: hooks.weights("down0", act))
    x2, h2 = resid_modnorm_fwd(x1, f0, Vec(mods, 0, g2), Vec(n1, 1, 0), Vec(mods, 1, sc1), Vec(mods, 1, sh1), "resid_modnorm_2")
    hooks.weights("mix1", x2)
    y1, mix1 = mixer1_fwd(h2, ropes, w)
    x3, h3 = resid_modnorm_fwd(x2, y1, Vec(mods, 1, g1), Vec(n2, 1, 0), Vec(mods, 1, sc2), Vec(mods, 1, sh2), "resid_modnorm_3")
    hooks.weights("ffn1", x3)
    f1, ffn1 = ffn_fwd(h3, w, 1)
    dres, d_final, loss, df1, dg2b = final_fused(x3, f1, Vec(mods, 1, g2), final_g, tgt)

    dh3, gf1 = ffn_bwd(df1, h3, ffn1, w, 1)
    late = mods + hooks.gradients("ffn1", gf1, dh3)
    dres, dsh2b, dsc2b, dn2b, dy1, dg1b = norm_gate_bwd(
        x3, dh3, Vec(n2, 1, 0), Vec(late, 1, sc2), dres, y1, Vec(late, 1, g1), "norm_gate_bwd_3")
    dh2, gm1 = mixer1_bwd(dy1, h2, mix1, ropes, w)
    late = mods + hooks.gradients("mix1", gm1, dh2)
    dres, dsh1b, dsc1b, dn1b, df0, dg2a = norm_gate_bwd(
        x2, dh2, Vec(n1, 1, 0), Vec(late, 1, sc1), dres, f0, Vec(late, 0, g2), "norm_gate_bwd_2")
    dh1, gf0 = ffn_bwd(df0, h1, ffn0, w, 0)
    late = mods + hooks.gradients("ffn0", gf0, dh1)
    dres, dsh2a, dsc2a, dn2a, dy0, dg1a = norm_gate_bwd(
        x1, dh1, Vec(n2, 0, 0), Vec(late, 0, sc2), dres, y0, Vec(late, 0, g1), "norm_gate_bwd_1")
    dh0, gm0 = mixer0_bwd(dy0, h0, mix0, w)
    late = mods + hooks.gradients("mix0", gm0, dh0)
    grad_x, dsh1a, dsc1a, dn1a = norm_bwd(x, dh0, Vec(n1, 0, 0), Vec(late, 0, sc1), dres, "norm_bwd_0")

    dmod = jnp.concatenate([jnp.concatenate([dsh1a, dsc1a, dg1a, dsh2a, dsc2a, dg2a], axis=1),
                            jnp.concatenate([dsh1b, dsc1b, dg1b, dsh2b, dsc2b, dg2b], axis=1)], axis=0)
    norms = dict(norm1_g=jnp.concatenate([dn1a, dn1b], axis=0), norm2_g=jnp.concatenate([dn2a, dn2b], axis=0),
                 final_norm_g=d_final)
    return loss, grad_x, dmod, dict(mix0=gm0, ffn0=gf0, mix1=gm1, ffn1=gf1, norms=norms)


def merge_grads(by_stage):
    grads = {**by_stage["mix0"], **by_stage["mix1"], **by_stage["norms"]}
    for k in ("ffn_w_down", "ffn_w_up"):
        grads[k] = [by_stage["ffn0"][k], by_stage["ffn1"][k]]
    grads["ffn_conv_w"] = jnp.stack([by_stage["ffn0"]["ffn_conv_w"], by_stage["ffn1"]["ffn_conv_w"]])
    return grads


_WEIGHTS = ("ada_w", "ada_b", "norm1_g", "norm2_g", "ab_w_in", "a_conv_w", "b_mix_w", "b_scale", "ab_w_out", "cd_w_in",
            "c_q_norm_g", "c_w_uq", "c_kv_norm_g", "c_w_ukv", "d_ln_g", "d_ln_b", "d_w_s", "d_b_s", "cd_w_out",
            "ffn_w_up", "ffn_conv_w", "ffn_w_down", "final_norm_g")
_INPUTS = ("x", "c", "positions") + _WEIGHTS + ("loss_target",) + tuple("m_" + n for n in _WEIGHTS) + tuple(
    "v_" + n for n in _WEIGHTS)

def _pack_rows(parts, rows, dtype):
    flat = jnp.concatenate([p.reshape(-1).astype(dtype) for p in parts])
    return jnp.pad(flat, (0, rows * LANES - flat.shape[0])).reshape(rows, LANES)


def _rows_major(w):
    r, c = w.shape
    return w.reshape(N_CHIPS, r // N_CHIPS, c)


def start_gather(shards, tag, after=()):
    lands = [_sds((N_CHIPS,) + s.shape, s.dtype) for s in shards]
    return split_start("gather_start_" + tag, GATHER, shards, lands, after)


def finish_gather(handle, chip, tag, after):
    shards, lands = split_wait("gather_wait_" + tag, GATHER, handle, after)
    lands = forward_halves(lands, "gather_forward_" + tag)
    return [lax.dynamic_update_index_in_dim(o, s, chip, 0) for o, s in zip(lands, shards)]


def start_reduce(gs, core, tag):
    recv = swap_halves(gs, "swap_halves_" + tag)
    pairs = pair_sums(gs, recv, core, "pair_sums_" + tag)
    lands = [_sds((N_CHIPS - 1,) + p.shape[1:], p.dtype) for p in pairs]
    return split_start("exchange_start_" + tag, EXCHANGE, pairs, lands)


def finish_reduce(handle, chip, core, tag, after):
    pairs, others = split_wait("exchange_wait_" + tag, EXCHANGE, handle, after)
    halves = chip_sums(pairs, others, chip, core, "chip_sums_" + tag)
    full = join_halves(halves, "join_halves_" + tag)
    return [f.reshape(f.shape[1] * 2, f.shape[2]) for f in full]


_SMALL_SHARDED = (("a_conv_w", (3, 128), 1), ("c_q_norm_g", (1, 64), 1), ("d_ln_g", (1, 128), 1), ("d_ln_b", (1, 128), 1),
                  ("ffn_conv_w", (2, 3, 2 * D_FF // N_CHIPS), 2))
_SMALL_GRADS = (("norm1_g", (2, D_MODEL)), ("norm2_g", (2, D_MODEL)), ("b_mix_w", (4, 128, 128)), ("b_scale", (1, 512)),
                ("c_kv_norm_g", (1, 128)), ("d_w_s", (4, 128, 128)), ("d_b_s", (4, 128)), ("final_norm_g", (1, D_MODEL)),
                ("a_conv_w", (3, 512)), ("c_q_norm_g", (1, 256)), ("d_ln_g", (1, 512)), ("d_ln_b", (1, 512)),
                ("ffn_conv_w", (2, 3, 2 * D_FF)))


def _size(shape):
    n = 1
    for d in shape:
        n *= d
    return n


def kernel(x, c, positions, ada_w, ada_b, norm1_g, norm2_g, ab_w_in, a_conv_w, b_mix_w, b_scale, ab_w_out, cd_w_in, c_q_norm_g, c_w_uq, c_kv_norm_g, c_w_ukv, d_ln_g, d_ln_b, d_w_s, d_b_s, cd_w_out, ffn_w_up, ffn_conv_w, ffn_w_down, final_norm_g, loss_target, m_ada_w, m_ada_b, m_norm1_g, m_norm2_g, m_ab_w_in, m_a_conv_w, m_b_mix_w, m_b_scale, m_ab_w_out, m_cd_w_in, m_c_q_norm_g, m_c_w_uq, m_c_kv_norm_g, m_c_w_ukv, m_d_ln_g, m_d_ln_b, m_d_w_s, m_d_b_s, m_cd_w_out, m_ffn_w_up, m_ffn_conv_w, m_ffn_w_down, m_final_norm_g, v_ada_w, v_ada_b, v_norm1_g, v_norm2_g, v_ab_w_in, v_a_conv_w, v_b_mix_w, v_b_scale, v_ab_w_out, v_cd_w_in, v_c_q_norm_g, v_c_w_uq, v_c_kv_norm_g, v_c_w_ukv, v_d_ln_g, v_d_ln_b, v_d_w_s, v_d_b_s, v_cd_w_out, v_ffn_w_up, v_ffn_conv_w, v_ffn_w_down, v_final_norm_g):
    args = (x, c, positions, ada_w, ada_b, norm1_g, norm2_g, ab_w_in, a_conv_w, b_mix_w, b_scale, ab_w_out, cd_w_in, c_q_norm_g, c_w_uq, c_kv_norm_g, c_w_ukv, d_ln_g, d_ln_b, d_w_s, d_b_s, cd_w_out, ffn_w_up, ffn_conv_w, ffn_w_down, final_norm_g, loss_target, m_ada_w, m_ada_b, m_norm1_g, m_norm2_g, m_ab_w_in, m_a_conv_w, m_b_mix_w, m_b_scale, m_ab_w_out, m_cd_w_in, m_c_q_norm_g, m_c_w_uq, m_c_kv_norm_g, m_c_w_ukv, m_d_ln_g, m_d_ln_b, m_d_w_s, m_d_b_s, m_cd_w_out, m_ffn_w_up, m_ffn_conv_w, m_ffn_w_down, m_final_norm_g, v_ada_w, v_ada_b, v_norm1_g, v_norm2_g, v_ab_w_in, v_a_conv_w, v_b_mix_w, v_b_scale, v_ab_w_out, v_cd_w_in, v_c_q_norm_g, v_c_w_uq, v_c_kv_norm_g, v_c_w_ukv, v_d_ln_g, v_d_ln_b, v_d_w_s, v_d_b_s, v_cd_w_out, v_ffn_w_up, v_ffn_conv_w, v_ffn_w_down, v_final_norm_g)
    a = dict(zip(_INPUTS, args, strict=True))
    xi, yi, ci = _place()
    chip = 2 * xi + yi
    dev = 4 * xi + 2 * yi + ci
    x = a["x"][0]
    tgt = a["loss_target"][0]

    bf = lambda t: t.astype(BF16)
    mix0_handle, tok = start_gather([bf(a["ab_w_in"][0]), bf(a["ab_w_out"][0])], "mix0")
    up0_16, down0_16, up1_16, down1_16 = [bf(a[n][l]) for l in (0, 1) for n in ("ffn_w_up", "ffn_w_down")]
    mix1_16 = [bf(a[n][0]) for n in ("cd_w_in", "c_w_uq", "c_w_ukv", "cd_w_out")]

    small_parts = [a["c"] + tok] + [a[n] for n, _, _ in _SMALL_SHARDED]
    rows1 = -(-sum(p.size for p in small_parts) // LANES // 8) * 8
    g1 = all_gather8(_pack_rows(small_parts, rows1, F32), "gather_small",
                     [up0_16, down0_16, up1_16, down1_16, mix1_16[0], mix1_16[3]]).reshape(N_DEV, rows1 * LANES)
    c_all = g1[:, :D_MODEL]
    per_chip = g1[0::2]
    small_full = {}
    off = D_MODEL
    for n, shp, axis in _SMALL_SHARDED:
        piece = per_chip[:, off:off + _size(shp)].reshape((N_CHIPS,) + shp)
        small_full[n] = jnp.concatenate([piece[k] for k in range(N_CHIPS)], axis=axis)
        off += _size(shp)

    merge = lambda t: t.reshape(t.shape[0] * t.shape[1], t.shape[2])
    w = dict(norm1_g=a["norm1_g"], norm2_g=a["norm2_g"], b_mix_w=a["b_mix_w"][0], b_scale=a["b_scale"],
             c_kv_norm_g=a["c_kv_norm_g"], d_w_s=a["d_w_s"][0], d_b_s=a["d_b_s"][0],
             final_norm_g=a["final_norm_g"].reshape(1, D_MODEL), **small_full)

    ncol = N_MOD * D_MODEL // N_CHIPS
    ada_b_mine = lax.dynamic_slice_in_dim(a["ada_b"], chip * ncol, ncol, axis=1)
    mod_cols = ada_mod(c_all, a["ada_w"], ada_b_mine)
    g2_rows = all_gather8(mod_cols.reshape(-1, LANES), "gather_mod")
    g2 = g2_rows.reshape(N_DEV, 2, N_DEV, ncol)
    mod = lax.dynamic_index_in_dim(g2[0::2], dev, axis=2, keepdims=False)
    mod = mod.transpose(1, 0, 2).reshape(2, N_MOD * D_MODEL)

    late = [g2_rows]
    up0_handle, tok_a = start_gather([up0_16], "up0", late)
    down0_handle, tok_b = start_gather([down0_16], "down0", late)
    mix1_handle, tok_c = start_gather(mix1_16, "mix1", late)
    ffn1_handle, tok_d = start_gather([up1_16, down1_16], "ffn1", late)
    mod = mod + (tok_a + tok_b + tok_c + tok_d)

    ropes = rope_tables(a["positions"][0])
    cm16 = lambda t: chip_major(t).astype(BF16)
    w.update(ffn_w_up=[None, None], ffn_w_down=[None, None])
    handles = dict(mix0=mix0_handle, up0=up0_handle, down0=down0_handle, mix1=mix1_handle, ffn1=ffn1_handle)
    reducing, reduced = {}, {}

    class Hooks(StepHooks):
        def weights(self, stage, after):
            got = finish_gather(handles[stage], chip, stage, after)
            if stage == "mix0":
                w.update(ab_w_in=got[0], ab_w_out=merge(got[1]))
            elif stage == "up0":
                w["ffn_w_up"][0] = got[0]
            elif stage == "down0":
                w["ffn_w_down"][0] = merge(got[0])
            elif stage == "mix1":
                cd_in, uq, ukv, cd_out = got
                w.update(prepare_weights(dict(cd_w_in=from_chip_major(cd_in), c_w_uq=from_chip_major(uq),
                                              c_w_ukv=from_chip_major(ukv), cd_w_out=merge(cd_out))))
            else:
                w["ffn_w_up"][1], w["ffn_w_down"][1] = got[0], merge(got[1])

        def gradients(self, stage, grads, after):
            if stage in ("ffn0", "ffn1"):
                parts = [grads["ffn_w_up"], _rows_major(grads["ffn_w_down"])]
            elif stage == "mix1":
                grads.update(unprepare_grads(grads))
                parts = [cm16(grads["cd_w_in"]), cm16(grads["c_w_uq"]), cm16(grads["c_w_ukv"]),
                         _rows_major(grads["cd_w_out"]).astype(BF16)]
            else:
                parts = [grads["ab_w_in"], _rows_major(grads["ab_w_out"])]
            reducing[stage], tok = start_reduce(parts, ci, stage)
            before = {"mix1": "ffn1", "ffn0": "mix1", "mix0": "ffn0"}.get(stage)
            if before is not None:
                reduced[before] = finish_reduce(reducing[before], chip, ci, before, after)
            return tok

    loss, grad_x, dmod, by_stage = run_step(x, tgt, mod, ropes, w, Hooks())
    grads = merge_grads(by_stage)

    parts3 = [dmod] + [grads[n] for n, _ in _SMALL_GRADS] + [loss[0, 0]]
    rows3 = -(-sum(p.size for p in parts3) // LANES // 8) * 8
    small_handle, _ = split_start("small_grads_start", EVERYONE, [_pack_rows(parts3, rows3, F32)],
                                  [_sds((N_DEV, rows3, LANES))])
    red_up1, red_down1 = reduced["ffn1"]
    red_cd_in, red_uq, red_ukv, red_cd_out = reduced["mix1"]
    red_up0, red_down0 = reduced["ffn0"]
    out_grads = dict(cd_w_in=red_cd_in, c_w_uq=red_uq, c_w_ukv=red_ukv, cd_w_out=red_cd_out)
    per_layer = dict(ffn_w_up=(red_up0, red_up1), ffn_w_down=(red_down0, red_down1))
    updates = {}

    def update(n):
        if n in per_layer:
            updates[n] = adamw_layers(a[n], *per_layer[n], a["m_" + n], a["v_" + n], "adamw_" + n)
        else:
            updates[n] = adamw(a[n], out_grads[n].reshape(a[n].shape), a["m_" + n], a["v_" + n], "adamw_" + n)

    early =("ffn_w_up", "ffn_w_down", "cd_w_in", "c_w_uq", "c_w_ukv", "cd_w_out")
    for n in early:
        update(n)
    (mine,), (landed,) = split_wait("small_grads_wait", EVERYONE, small_handle, [updates[n][1] for n in early])
    g3 = lax.dynamic_update_index_in_dim(landed, mine, dev, 0)
    summed = sum8(g3).reshape(-1)
    nmod = 2 * N_MOD * D_MODEL
    out_grads["ada_b"] = summed[:nmod].reshape(2, N_MOD * D_MODEL)
    off = nmod
    for n, shp in _SMALL_GRADS:
        out_grads[n] = summed[off:off + _size(shp)].reshape(shp)
        off += _size(shp)
    loss = summed[off]
    for n, shp, axis in _SMALL_SHARDED:
        width = out_grads[n].shape[-1] // N_CHIPS
        out_grads[n] = lax.dynamic_slice_in_dim(out_grads[n], chip * width, width, axis=out_grads[n].ndim - 1)
    dmod_all = g3.reshape(N_DEV, rows3 * LANES)[:, :nmod].reshape(N_DEV, 2, N_MOD * D_MODEL)
    dmod_mine = lax.dynamic_slice_in_dim(dmod_all, chip * ncol, ncol, axis=2).transpose(1, 0, 2)
    updates["ada_w"] = adamw_ada(a["ada_w"], c_all, dmod_mine, a["m_ada_w"], a["v_ada_w"])

    red_in0, red_out0 = finish_reduce(reducing["mix0"], chip, ci, "mix0", updates["ada_w"][1])
    out_grads.update(ab_w_in=red_in0, ab_w_out=red_out0)

    for n in ("ab_w_in", "ab_w_out"):
        update(n)
    small = [n for n in _WEIGHTS if n not in updates]
    for n, res in zip(small, adamw_small([a[n] for n in small], [out_grads[n].reshape(a[n].shape) for n in small],
                                         [a["m_" + n] for n in small], [a["v_" + n] for n in small])):
        updates[n] = res
    return (loss, grad_x[None], *[updates[n][i] for i in range(4) for n in _WEIGHTS])
```

```python
import functools
from typing import NamedTuple

import jax
import jax.numpy as jnp
from jax import lax
from jax.experimental import pallas as pl
from jax.experimental.pallas import tpu as pltpu

F32 = jnp.float32
BF16 = jnp.bfloat16
EPS = 1e-6
D_MODEL = 1024
N_MOD = 6
A_WIDTH = 512
B_GROUPS = 4
C_HEADS = 8
C_NOPE = 64
C_ROPE = 32
C_V = 64
C_Q_RANK = 256
C_KV_RANK = 128
HEAD_PAD = 128
ROPE_THETA = 10000.0
D_GROUPS = 4
D_CHUNK = 128
D_FF = 2816
FF_UNIT = 128
ADAM_LR = 0.001
ADAM_B1 = 0.9
ADAM_B2 = 0.999
ADAM_EPS = 1e-08
ADAM_WD = 0.01
ADAM_STEP = 10
N_CHIPS = 4
N_DEV = 8
LANES = 128
VMEM_BIG = 56 * 1024 * 1024
MESH = pl.DeviceIdType.MESH


def _sds(shape, dtype=F32):
    return jax.ShapeDtypeStruct(tuple(shape), dtype)


def _tile(n, cap, mult=128):
    if n <= cap:
        return n
    best = None
    for t in range(mult, cap + 1, mult):
        if n % t == 0:
            best = t
    assert best is not None, (n, cap, mult)
    return best


def _params(dims=None, vmem=None):
    return pltpu.CompilerParams(dimension_semantics=dims, vmem_limit_bytes=vmem)


def _shift_down(v, k):
    r = pltpu.roll(v, k, axis=0)
    t = lax.broadcasted_iota(jnp.int32, v.shape, 0)
    return jnp.where(t >= k, r, 0.0)


def _shift_up(v, k):
    n = v.shape[0]
    r = pltpu.roll(v, n - k, axis=0)
    t = lax.broadcasted_iota(jnp.int32, v.shape, 0)
    return jnp.where(t < n - k, r, 0.0)


def _sigmoid(v):
    return 1.0 / (1.0 + jnp.exp(-v))


_GELU_C = 0.7978845608028654
_GELU_A = 0.044715


def _gelu(v):
    return 0.5 * v * (1.0 + jnp.tanh(_GELU_C * (v + _GELU_A * v * v * v)))


def _gelu_grad(v):
    th = jnp.tanh(_GELU_C * (v + _GELU_A * v * v * v))
    return 0.5 * (1.0 + th) + 0.5 * v * (1.0 - th * th) * _GELU_C * (1.0 + 3.0 * _GELU_A * v * v)


_NN = (((1,), (0,)), ((), ()))
_NT = (((1,), (1,)), ((), ()))
_TN = (((0,), (0,)), ((), ()))


def _dot(a, b, dims=_NN):
    return lax.dot_general(a, b, dims, preferred_element_type=F32)


def _logical(t, groups):
    return (t.shape[-2], t.shape[-1] * groups)


def _block(tr, tc, groups, cols, where):
    if groups == 1:
        return pl.BlockSpec((tr, tc), where)
    per = cols // groups // tc

    def index(i, j, s):
        r, c = where(i, j, s)
        return (c // per, r, c % per)

    return pl.BlockSpec((None, tr, tc), index)


def matmul(a, b, mode, out_dtype, name, ga=1, gb=1, go=1, tm=None, tn=None, tk=None):
    (ar, ac), (br, bc) = _logical(a, ga), _logical(b, gb)
    if mode == "nn":
        m, k, n = ar, ac, bc
        a_col, b_col = "k", "n"
    elif mode == "nt":
        m, k, n = ar, ac, br
        a_col, b_col = "k", "k"
    else:
        k, m, n = ar, ac, bc
        a_col, b_col = "m", "n"
    limit = {"m": m, "n": n // go, "k": k}
    limit[a_col] = min(limit[a_col], ac // ga)
    limit[b_col] = min(limit[b_col], bc // gb)
    tm = tm or _tile(limit["m"], 2048, 128 if mode == "tn" else 16)
    tn = tn or _tile(limit["n"], 512)
    tk = tk or _tile(limit["k"], 2048, 16 if mode == "tn" else 128)
    nk = k // tk
    if mode == "nn":
        a_spec = _block(tm, tk, ga, ac, lambda i, j, s: (i, s))
        b_spec = _block(tk, tn, gb, bc, lambda i, j, s: (s, j))
        dims = _NN
    elif mode == "nt":
        a_spec = _block(tm, tk, ga, ac, lambda i, j, s: (i, s))
        b_spec = _block(tn, tk, gb, bc, lambda i, j, s: (j, s))
        dims = _NT
    else:
        a_spec = _block(tk, tm, ga, ac, lambda i, j, s: (s, i))
        b_spec = _block(tk, tn, gb, bc, lambda i, j, s: (s, j))
        dims = _TN
    o_spec = _block(tm, tn, go, n, lambda i, j, s: (i, j))
    out_shape = _sds((m, n), out_dtype) if go == 1 else _sds((go, m, n // go), out_dtype)

    def body(a_ref, b_ref, o_ref, acc_ref):
        s = pl.program_id(2)

        @pl.when(s == 0)
        def _():
            acc_ref[...] = jnp.zeros_like(acc_ref)

        acc_ref[...] += _dot(a_ref[...], b_ref[...], dims)

        @pl.when(s == nk - 1)
        def _():
            o_ref[...] = acc_ref[...].astype(o_ref.dtype)

    return pl.pallas_call(
        body, name=name, out_shape=out_shape, grid=(m // tm, n // tn, nk),
        in_specs=[a_spec, b_spec], out_specs=o_spec,
        scratch_shapes=[pltpu.VMEM((tm, tn), F32)],
        compiler_params=_params(("parallel", "parallel", "arbitrary"), VMEM_BIG),
    )(a, b)


def _rows(tm, n):
    return pl.BlockSpec((tm, n), lambda i: (i, 0))


def _vec(n):
    return pl.BlockSpec((1, n), lambda i: (0, 0))


class Vec(NamedTuple):
    array: jax.Array
    row: int
    col: int


def _vec_in(v, d):
    return pl.BlockSpec((None, 1, d), lambda i: (v.row, 0, v.col))


def modnorm_fwd(x, g, sc, sh, name):
    s, d = x.shape
    tm = _tile(s, 256, 8)

    def body(x_ref, g_ref, sc_ref, sh_ref, o_ref):
        xv = x_ref[...]
        r = lax.rsqrt(jnp.mean(xv * xv, axis=-1, keepdims=True) + EPS)
        o_ref[...] = ((xv * r) * g_ref[...] * (1.0 + sc_ref[...]) + sh_ref[...]).astype(BF16)

    return pl.pallas_call(
        body, name=name, out_shape=_sds((s, d), BF16), grid=(s // tm,),
        in_specs=[_rows(tm, d), _vec_in(g, d), _vec_in(sc, d), _vec_in(sh, d)], out_specs=_rows(tm, d),
        compiler_params=_params(("parallel",)),
    )(x, g.array, sc.array, sh.array)


def norm_bwd(x, dh, g, sc, dres, name):
    s, d = x.shape
    tm = _tile(s, 256, 8)
    nsteps = s // tm

    def body(x_ref, dh_ref, g_ref, sc_ref, dr_ref, dx_ref, dsh_ref, dsc_ref, dg_ref, a2_ref):
        i = pl.program_id(0)

        @pl.when(i == 0)
        def _():
            dsh_ref[...] = jnp.zeros_like(dsh_ref)
            a2_ref[...] = jnp.zeros_like(a2_ref)

        xv = x_ref[...]
        dh = dh_ref[...].astype(F32)
        r = lax.rsqrt(jnp.mean(xv * xv, axis=-1, keepdims=True) + EPS)
        xh = xv * r
        dsh_ref[...] += jnp.sum(dh, axis=0, keepdims=True)
        a2_ref[...] += jnp.sum(dh * xh, axis=0, keepdims=True)
        dxh = dh * (g_ref[...] * (1.0 + sc_ref[...]))
        dx = r * (dxh - xh * jnp.mean(dxh * xh, axis=-1, keepdims=True))
        dx_ref[...] = dr_ref[...] + dx

        @pl.when(i == nsteps - 1)
        def _():
            dsc_ref[...] = a2_ref[...] * g_ref[...]
            dg_ref[...] = a2_ref[...] * (1.0 + sc_ref[...])

    return pl.pallas_call(
        body, name=name, out_shape=(_sds((s, d)), _sds((1, d)), _sds((1, d)), _sds((1, d))), grid=(nsteps,),
        in_specs=[_rows(tm, d), _rows(tm, d), _vec_in(g, d), _vec_in(sc, d), _rows(tm, d)],
        out_specs=(_rows(tm, d), _vec(d), _vec(d), _vec(d)),
        scratch_shapes=[pltpu.VMEM((1, d), F32)],
        compiler_params=_params(("arbitrary",)),
    )(x, dh, g.array, sc.array, dres)


_ROW_STREAM = 256


def _row_streams(s):
    tm = _tile(s, 2 * _ROW_STREAM, 8)
    sub = min(tm, _ROW_STREAM)
    return tm, [slice(r * sub, (r + 1) * sub) for r in range(tm // sub)]


def resid_modnorm_fwd(x, y, gate, g, sc, sh, name):
    s, d = x.shape
    tm, streams = _row_streams(s)

    def body(x_ref, y_ref, gate_ref, g_ref, sc_ref, sh_ref, xo_ref, h_ref):
        for rs in streams:
            xv = x_ref[rs, :] + gate_ref[...] * y_ref[rs, :].astype(F32)
            xo_ref[rs, :] = xv
            r = lax.rsqrt(jnp.mean(xv * xv, axis=-1, keepdims=True) + EPS)
            h_ref[rs, :] = ((xv * r) * g_ref[...] * (1.0 + sc_ref[...]) + sh_ref[...]).astype(BF16)

    return pl.pallas_call(
        body, name=name, out_shape=(_sds((s, d)), _sds((s, d), BF16)), grid=(s // tm,),
        in_specs=[_rows(tm, d), _rows(tm, d), _vec_in(gate, d), _vec_in(g, d), _vec_in(sc, d), _vec_in(sh, d)],
        out_specs=(_rows(tm, d), _rows(tm, d)),
        compiler_params=_params(("parallel",), VMEM_BIG),
    )(x, y, gate.array, g.array, sc.array, sh.array)


def norm_gate_bwd(x, dh, g, sc, dres, y, gate, name):
    s, d = x.shape
    tm, streams = _row_streams(s)
    nsteps = s // tm

    def body(x_ref, dh_ref, g_ref, sc_ref, dr_ref, y_ref, gate_ref, dx_ref, dsh_ref, dsc_ref, dg_ref, dy_ref,
             dgate_ref, a2_ref):
        i = pl.program_id(0)

        @pl.when(i == 0)
        def _():
            dsh_ref[...] = jnp.zeros_like(dsh_ref)
            a2_ref[...] = jnp.zeros_like(a2_ref)
            dgate_ref[...] = jnp.zeros_like(dgate_ref)

        for rs in streams:
            xv = x_ref[rs, :]
            dh = dh_ref[rs, :].astype(F32)
            r = lax.rsqrt(jnp.mean(xv * xv, axis=-1, keepdims=True) + EPS)
            xh = xv * r
            dsh_ref[...] += jnp.sum(dh, axis=0, keepdims=True)
            a2_ref[...] += jnp.sum(dh * xh, axis=0, keepdims=True)
            dxh = dh * (g_ref[...] * (1.0 + sc_ref[...]))
            dr = dr_ref[rs, :] + r * (dxh - xh * jnp.mean(dxh * xh, axis=-1, keepdims=True))
            dx_ref[rs, :] = dr
            dy_ref[rs, :] = (dr * gate_ref[...]).astype(BF16)
            dgate_ref[...] += jnp.sum(dr * y_ref[rs, :].astype(F32), axis=0, keepdims=True)

        @pl.when(i == nsteps - 1)
        def _():
            dsc_ref[...] = a2_ref[...] * g_ref[...]
            dg_ref[...] = a2_ref[...] * (1.0 + sc_ref[...])

    vec = _sds((1, d))
    return pl.pallas_call(
        body, name=name, out_shape=(_sds((s, d)), vec, vec, vec, _sds((s, d), BF16), vec), grid=(nsteps,),
        in_specs=[_rows(tm, d), _rows(tm, d), _vec_in(g, d), _vec_in(sc, d), _rows(tm, d), _rows(tm, d), _vec_in(gate, d)],
        out_specs=(_rows(tm, d), _vec(d), _vec(d), _vec(d), _rows(tm, d), _vec(d)),
        scratch_shapes=[pltpu.VMEM((1, d), F32)],
        compiler_params=_params(("arbitrary",), VMEM_BIG),
    )(x, dh, g.array, sc.array, dres, y, gate.array)


def final_fused(x, f, gate, g, tgt):
    s, d = x.shape
    tm = _tile(s, 256, 8)

    def body(x_ref, f_ref, gate_ref, g_ref, t_ref, dx_ref, dg_ref, loss_ref, df_ref, dgate_ref):
        @pl.when(pl.program_id(0) == 0)
        def _():
            dg_ref[...] = jnp.zeros_like(dg_ref)
            loss_ref[...] = jnp.zeros_like(loss_ref)
            dgate_ref[...] = jnp.zeros_like(dgate_ref)

        fv, gatev, gv = f_ref[...].astype(F32), gate_ref[...], g_ref[...]
        xv = x_ref[...] + gatev * fv
        r = lax.rsqrt(jnp.mean(xv * xv, axis=-1, keepdims=True) + EPS)
        xh = xv * r
        e = xh * gv - t_ref[...]
        row = jnp.sum(e * e, axis=-1, keepdims=True) * (0.5 / d)
        loss_ref[...] += jnp.sum(row, axis=0, keepdims=True)
        dy = e * (1.0 / d)
        dg_ref[...] += jnp.sum(dy * xh, axis=0, keepdims=True)
        dxh = dy * gv
        dx = r * (dxh - xh * jnp.mean(dxh * xh, axis=-1, keepdims=True))
        dx_ref[...] = dx
        df_ref[...] = (dx * gatev).astype(BF16)
        dgate_ref[...] += jnp.sum(dx * fv, axis=0, keepdims=True)

    vec = _sds((1, d))
    return pl.pallas_call(
        body, name="final_fused", out_shape=(_sds((s, d)), vec, _sds((1, LANES)), _sds((s, d), BF16), vec),
        grid=(s // tm,),
        in_specs=[_rows(tm, d), _rows(tm, d), _vec_in(gate, d), _vec_in(g, d), _rows(tm, d)],
        out_specs=(_rows(tm, d), _vec(d), _vec(LANES), _rows(tm, d), _vec(d)),
        compiler_params=_params(("arbitrary",)),
    )(x, f, gate.array, g.array, tgt)


def _taps(v):
    return _shift_down(v, 2), _shift_down(v, 1), v


def _conv3_taps(taps, w):
    return w[0:1, :] * taps[0] + w[1:2, :] * taps[1] + w[2:3, :] * taps[2]


def _conv3(v, w):
    return _conv3_taps(_taps(v), w)


def _conv3_t(dv, w):
    return w[0:1, :] * _shift_up(dv, 2) + w[1:2, :] * _shift_up(dv, 1) + w[2:3, :] * dv


def _conv3_dw_taps(dv, taps):
    return jnp.concatenate([jnp.sum(dv * t, axis=0, keepdims=True) for t in taps], axis=0)


def _conv3_dw(dv, v):
    return _conv3_dw_taps(dv, _taps(v))


def gconv_fwd(z, conv_w):
    s = z.shape[0]
    nb = A_WIDTH // LANES

    def body(b_ref, c_ref, a_ref, w_ref, o_ref):
        b, c, a = b_ref[...].astype(F32), c_ref[...].astype(F32), a_ref[...].astype(F32)
        o_ref[...] = (b * _conv3(c * a, w_ref[...])).astype(BF16)

    col = lambda off: pl.BlockSpec((s, LANES), lambda j: (0, off + j))
    return pl.pallas_call(
        body, name="gconv_fwd", out_shape=_sds((s, A_WIDTH + _B_WIDTH), BF16), grid=(nb,),
        in_specs=[col(0), col(nb), col(2 * nb), pl.BlockSpec((3, LANES), lambda j: (0, j))],
        out_specs=pl.BlockSpec((s, LANES), lambda j: (0, j)),
        compiler_params=_params(("parallel",), VMEM_BIG),
    )(z, z, z, conv_w)


def gconv_bwd(z, conv_w, dycat):
    s = z.shape[0]
    nb = A_WIDTH // LANES

    def body(b_ref, c_ref, a_ref, w_ref, dy_ref, db_ref, dc_ref, da_ref, dw_ref):
        c, a, w, dy = c_ref[...].astype(F32), a_ref[...].astype(F32), w_ref[...], dy_ref[...].astype(F32)
        ca = c * a
        db_ref[...] = (dy * _conv3(ca, w)).astype(BF16)
        dconv = dy * b_ref[...].astype(F32)
        dw_ref[...] = _conv3_dw(dconv, ca)
        dca = _conv3_t(dconv, w)
        dc_ref[...] = (dca * a).astype(BF16)
        da_ref[...] = (dca * c).astype(BF16)

    col = lambda off: pl.BlockSpec((s, LANES), lambda j: (0, off + j))
    wspec = pl.BlockSpec((3, LANES), lambda j: (0, j))
    part = _sds((s, A_WIDTH), BF16)
    return pl.pallas_call(
        body, name="gconv_bwd", out_shape=(part, part, part, _sds((3, A_WIDTH))), grid=(nb,),
        in_specs=[col(0), col(nb), col(2 * nb), wspec, col(0)],
        out_specs=(col(0), col(0), col(0), wspec),
        compiler_params=_params(("parallel",), VMEM_BIG),
    )(z, z, z, conv_w, dycat)


def _pool_counts(s, w):
    t = lax.broadcasted_iota(jnp.int32, (s, 1), 0)
    return jnp.minimum(t + 1, w).astype(F32)


def _pooled(p, levels):
    acc = p
    for lv in range(levels):
        acc = acc + _shift_down(acc, 2 ** lv)
    return acc / _pool_counts(p.shape[0], 2 ** levels) - p


_B_WIDTH = B_GROUPS * LANES


def pool_fwd(z, mix_w, scale, ycat):
    s = z.shape[0]

    def body(p_ref, m_ref, sc_ref, ycat_ref, o_ref):
        del ycat_ref
        for g in range(B_GROUPS):
            cols = slice(g * LANES, (g + 1) * LANES)
            pooled = _pooled(p_ref[:, cols].astype(F32), g + 1)
            y = _dot(pooled.astype(BF16), m_ref[g].astype(BF16))
            o_ref[:, cols] = (y * sc_ref[:, cols]).astype(BF16)

    return pl.pallas_call(
        body, name="pool_fwd", out_shape=_sds(ycat.shape, BF16), grid=(1,),
        in_specs=[pl.BlockSpec((s, _B_WIDTH), lambda i: (0, 3 * A_WIDTH // _B_WIDTH)),
                  pl.BlockSpec((B_GROUPS, LANES, LANES), lambda i: (0, 0, 0)), pl.BlockSpec((1, _B_WIDTH), lambda i: (0, 0)),
                  pl.BlockSpec(memory_space=pl.ANY)],
        out_specs=pl.BlockSpec((s, _B_WIDTH), lambda i: (0, A_WIDTH // _B_WIDTH)),
        input_output_aliases={3: 0},
        compiler_params=_params(("arbitrary",), VMEM_BIG),
    )(z, mix_w, scale, ycat)


def pool_bwd(z, mix_w, scale, dycat):
    s = z.shape[0]

    def body(p_ref, m_ref, sc_ref, dy_ref, dp_ref, dm_ref, dsc_ref):
        for g in range(B_GROUPS):
            cols = slice(g * LANES, (g + 1) * LANES)
            pooled = _pooled(p_ref[:, cols].astype(F32), g + 1)
            mw = m_ref[g].astype(BF16)
            pb = pooled.astype(BF16)
            dy = dy_ref[:, cols].astype(F32)
            dsc_ref[:, cols] = jnp.sum(dy * _dot(pb, mw), axis=0, keepdims=True)
            dmix = (dy * sc_ref[:, cols]).astype(BF16)
            dm_ref[g] = _dot(pb, dmix, _TN)
            dpool = _dot(dmix, mw, _NT)
            acc = dpool / _pool_counts(s, 2 ** (g + 1))
            for lv in range(g + 1):
                acc = acc + _shift_up(acc, 2 ** lv)
            dp_ref[:, cols] = (acc - dpool).astype(BF16)

    wide = lambda c: pl.BlockSpec((s, _B_WIDTH), lambda i: (0, c))
    mspec = pl.BlockSpec((B_GROUPS, LANES, LANES), lambda i: (0, 0, 0))
    vspec = pl.BlockSpec((1, _B_WIDTH), lambda i: (0, 0))
    return pl.pallas_call(
        body, name="pool_bwd", out_shape=(_sds((s, _B_WIDTH), BF16), _sds((B_GROUPS, LANES, LANES)), _sds((1, _B_WIDTH))),
        grid=(1,), in_specs=[wide(3 * A_WIDTH // _B_WIDTH), mspec, vspec, wide(A_WIDTH // _B_WIDTH)],
        out_specs=(wide(0), mspec, vspec),
        compiler_params=_params(("arbitrary",), VMEM_BIG),
    )(z, mix_w, scale, dycat)


_FF_BLOCKS = D_FF // FF_UNIT


def _ff_spec(s):
    return pl.BlockSpec((2, s, FF_UNIT), lambda j: (0, 0, j))


def _ff_wspecs():
    return [pl.BlockSpec((3, FF_UNIT), lambda j: (0, j)), pl.BlockSpec((3, FF_UNIT), lambda j: (0, _FF_BLOCKS + j))]


_FF_ROWS = 64
_FF_HALO = 16


def _chunk_taps(z_ref, half, c):
    start = pl.multiple_of(c * _FF_ROWS, _FF_ROWS)
    before = pl.multiple_of(jnp.maximum(c * _FF_ROWS - _FF_HALO, 0), _FF_HALO)
    halo = z_ref[half, pl.ds(before, _FF_HALO), :].astype(F32)
    halo = jnp.where(c > 0, halo, 0.0)
    win = jnp.concatenate([halo, z_ref[half, pl.ds(start, _FF_ROWS), :].astype(F32)], axis=0)
    return tuple(pltpu.roll(win, k, axis=0)[_FF_HALO:] for k in (2, 1)) + (win[_FF_HALO:],)


def _fold8(v):
    acc = v[0:8]
    for r in range(8, v.shape[0], 8):
        acc = acc + v[r:r + 8]
    return acc


_FF_CHUNK = 256


def ffn_act_down(zf, conv_w, w_down, name):
    s, d = zf.shape[1], w_down.shape[1]
    nk = D_FF // _FF_CHUNK
    chunk = lambda k: jnp.minimum(k, nk - 1)

    def body(z_ref, wg_ref, wu_ref, wd_ref, a_ref, f_ref, held_ref, acc_ref):
        k = pl.program_id(0)

        @pl.when(k == 0)
        def _():
            held_ref[...] = jnp.zeros_like(held_ref)
            acc_ref[...] = jnp.zeros_like(acc_ref)

        acc_ref[...] += _dot(held_ref[(k + 1) % 2], wd_ref[...])
        g = _conv3(z_ref[0].astype(F32), wg_ref[...])
        u = _conv3(z_ref[1].astype(F32), wu_ref[...])
        act = (g * _sigmoid(g) * u).astype(BF16)
        a_ref[...] = act
        held_ref[k % 2] = act

        @pl.when(k == nk)
        def _():
            f_ref[...] = acc_ref[...].astype(BF16)

    return pl.pallas_call(
        body, name=name, out_shape=(_sds((s, D_FF), BF16), _sds((s, d), BF16)), grid=(nk + 1,),
        in_specs=[pl.BlockSpec((2, s, _FF_CHUNK), lambda k: (0, 0, chunk(k))),
                  pl.BlockSpec((3, _FF_CHUNK), lambda k: (0, chunk(k))),
                  pl.BlockSpec((3, _FF_CHUNK), lambda k: (0, nk + chunk(k))),
                  pl.BlockSpec((_FF_CHUNK, d), lambda k: (jnp.maximum(k - 1, 0), 0))],
        out_specs=(pl.BlockSpec((s, _FF_CHUNK), lambda k: (0, chunk(k))), pl.BlockSpec((s, d), lambda k: (0, 0))),
        scratch_shapes=[pltpu.VMEM((2, s, _FF_CHUNK), BF16), pltpu.VMEM((s, d), F32)],
        compiler_params=_params(("arbitrary",), VMEM_BIG),
    )(zf, conv_w, conv_w, w_down)


def ffn_act_bwd(zf, conv_w, da, name):
    s = zf.shape[1]
    assert s % _FF_ROWS == 0
    nchunks = s // _FF_ROWS

    def body(z_ref, wg_ref, wu_ref, da_ref, dz_ref, dw_ref, dg_ref, du_ref):
        wg, wu = wg_ref[...], wu_ref[...]

        def first(c, acc):
            rows = pl.ds(pl.multiple_of(c * _FF_ROWS, _FF_ROWS), _FF_ROWS)
            tg, tu = _chunk_taps(z_ref, 0, c), _chunk_taps(z_ref, 1, c)
            g = _conv3_taps(tg, wg)
            u = _conv3_taps(tu, wu)
            dav = da_ref[rows, :].astype(F32)
            sg = _sigmoid(g)
            dg = dav * u * (sg * (1.0 + g * (1.0 - sg)))
            du = dav * (g * sg)
            dg_ref[rows, :] = dg
            du_ref[rows, :] = du
            return tuple(a + _fold8(d * t) for a, (d, t) in zip(acc, [(dg, t) for t in tg] + [(du, t) for t in tu]))

        zero = jnp.zeros((8, FF_UNIT), F32)
        acc = lax.fori_loop(0, nchunks, first, (zero,) * 6)
        sums = [jnp.sum(a, axis=0, keepdims=True) for a in acc]
        dw_ref[0] = jnp.concatenate(sums[:3], axis=0)
        dw_ref[1] = jnp.concatenate(sums[3:], axis=0)

        tail = pl.ds(s, _FF_HALO)
        dg_ref[tail, :] = jnp.zeros((_FF_HALO, FF_UNIT), F32)
        du_ref[tail, :] = jnp.zeros((_FF_HALO, FF_UNIT), F32)
        span = _FF_ROWS + _FF_HALO

        def second(c, carry):
            start = pl.multiple_of(c * _FF_ROWS, _FF_ROWS)
            for half, (d_ref, w) in enumerate(((dg_ref, wg), (du_ref, wu))):
                win = d_ref[pl.ds(start, span), :]
                dz = (w[0:1, :] * pltpu.roll(win, span - 2, axis=0)[:_FF_ROWS]
                      + w[1:2, :] * pltpu.roll(win, span - 1, axis=0)[:_FF_ROWS] + w[2:3, :] * win[:_FF_ROWS])
                dz_ref[half, pl.ds(start, _FF_ROWS), :] = dz.astype(BF16)
            return carry

        lax.fori_loop(0, nchunks, second, 0)

    return pl.pallas_call(
        body, name=name, out_shape=(_sds((2, s, D_FF), BF16), _sds((2, 3, D_FF))), grid=(_FF_BLOCKS,),
        in_specs=[_ff_spec(s)] + _ff_wspecs() + [pl.BlockSpec((s, FF_UNIT), lambda j: (0, j))],
        out_specs=(_ff_spec(s), pl.BlockSpec((2, 3, FF_UNIT), lambda j: (0, 0, j))),
        scratch_shapes=[pltpu.VMEM((s + _FF_HALO, FF_UNIT), F32), pltpu.VMEM((s + _FF_HALO, FF_UNIT), F32)],
        compiler_params=_params(("parallel",), VMEM_BIG),
    )(zf, conv_w, conv_w, da)


def _rope(v, cs, s1, s2):
    return v * cs + pltpu.roll(v, LANES - C_ROPE // 2, axis=1) * s1 + pltpu.roll(v, C_ROPE // 2, axis=1) * s2


def _rope_t(dv, cs, s1, s2):
    return dv * cs + pltpu.roll(dv * s1, C_ROPE // 2, axis=1) + pltpu.roll(dv * s2, LANES - C_ROPE // 2, axis=1)


def _kpe_mask(shape):
    lane = lax.broadcasted_iota(jnp.int32, shape, 1)
    return (lane >= C_NOPE) & (lane < C_NOPE + C_ROPE)


def _rms(v, g):
    r = lax.rsqrt(jnp.mean(v * v, axis=-1, keepdims=True) + EPS)
    return v * r, r


def _rms_bwd(dn, xh, r, g):
    dxh = dn * g
    return r * (dxh - xh * jnp.mean(dxh * xh, axis=-1, keepdims=True)), jnp.sum(dn * xh, axis=0, keepdims=True)


_ZQ = C_Q_RANK + C_KV_RANK + HEAD_PAD
_HW = C_HEADS * HEAD_PAD


_MLA_ROWS = 256


def _mla_tiles(s):
    tm = _tile(s, 2 * _MLA_ROWS, 8)
    sub = min(tm, _MLA_ROWS)
    return tm, [slice(r * sub, (r + 1) * sub) for r in range(tm // sub)]


def mla_pre_fwd(z, gq, gkv, wq, wk, wv, cs, s1, s2):
    s = z.shape[0]
    tm, streams = _mla_tiles(s)

    def body(z_ref, gq_ref, gkv_ref, wq_ref, wk_ref, wv_ref, cs_ref, s1_ref, s2_ref, q_ref, k_ref, v_ref):
        for rs in streams:
            zv = z_ref[rs, :].astype(F32)
            cst, s1t, s2t = cs_ref[rs, :], s1_ref[rs, :], s2_ref[rs, :]
            qh, _ = _rms(zv[:, :C_Q_RANK], None)
            qn = (qh * gq_ref[...]).astype(BF16)
            q = _dot(qn, wq_ref[...])
            kh, _ = _rms(zv[:, C_Q_RANK:C_Q_RANK + C_KV_RANK], None)
            kvn = (kh * gkv_ref[...]).astype(BF16)
            k = _dot(kvn, wk_ref[...])
            v_ref[rs, :] = _dot(kvn, wv_ref[...]).astype(BF16)
            kpe = _rope(zv[:, C_Q_RANK + C_KV_RANK:], cst, s1t, s2t)
            for h in range(C_HEADS):
                sl = slice(h * HEAD_PAD, (h + 1) * HEAD_PAD)
                q_ref[rs, sl] = _rope(q[:, sl], cst, s1t, s2t).astype(BF16)
                k_ref[rs, sl] = (k[:, sl] + kpe).astype(BF16)

    full = lambda r, c: pl.BlockSpec((r, c), lambda i: (0, 0))
    hw = _sds((s, _HW), BF16)
    return pl.pallas_call(
        body, name="mla_pre_fwd", out_shape=(hw, hw, hw), grid=(s // tm,),
        in_specs=[_rows(tm, _ZQ), _vec(C_Q_RANK), _vec(C_KV_RANK), full(C_Q_RANK, _HW), full(C_KV_RANK, _HW),
                  full(C_KV_RANK, _HW), _rows(tm, LANES), _rows(tm, LANES), _rows(tm, LANES)],
        out_specs=(_rows(tm, _HW), _rows(tm, _HW), _rows(tm, _HW)),
        compiler_params=_params(("parallel",), VMEM_BIG),
    )(z, gq, gkv, wq, wk, wv, cs, s1, s2)


def mla_pre_bwd(z, gq, gkv, wq, wk, wv, cs, s1, s2, dq, dk, dv):
    s = z.shape[0]
    tm, streams = _mla_tiles(s)

    def body(z_ref, gq_ref, gkv_ref, wq_ref, wk_ref, wv_ref, cs_ref, s1_ref, s2_ref, dq_ref, dk_ref, dv_ref,
             dz_ref, dwq_ref, dwk_ref, dwv_ref, dgq_ref, dgkv_ref):
        @pl.when(pl.program_id(0) == 0)
        def _():
            dwq_ref[...] = jnp.zeros_like(dwq_ref)
            dwk_ref[...] = jnp.zeros_like(dwk_ref)
            dwv_ref[...] = jnp.zeros_like(dwv_ref)
            dgq_ref[...] = jnp.zeros_like(dgq_ref)
            dgkv_ref[...] = jnp.zeros_like(dgkv_ref)

        gqv, gkvv = gq_ref[...], gkv_ref[...]
        for rs in streams:
            zv = z_ref[rs, :].astype(F32)
            cst, s1t, s2t = cs_ref[rs, :], s1_ref[rs, :], s2_ref[rs, :]
            qh, rq = _rms(zv[:, :C_Q_RANK], None)
            qn = (qh * gqv).astype(BF16)
            kh, rk = _rms(zv[:, C_Q_RANK:C_Q_RANK + C_KV_RANK], None)
            kvn = (kh * gkvv).astype(BF16)

            dqv = dq_ref[rs, :].astype(F32)
            dqp = jnp.concatenate(
                [_rope_t(dqv[:, h * HEAD_PAD:(h + 1) * HEAD_PAD], cst, s1t, s2t) for h in range(C_HEADS)], axis=1
            ).astype(BF16)
            dwq_ref[...] += _dot(qn, dqp, _TN)
            dqn = _dot(dqp, wq_ref[...], _NT)
            dql, dgq = _rms_bwd(dqn, qh, rq, gqv)
            dgq_ref[...] += dgq

            dkv = dk_ref[rs, :]
            dkb = dkv.astype(BF16)
            dvb = dv_ref[rs, :].astype(BF16)
            dwk_ref[...] += _dot(kvn, dkb, _TN)
            dwv_ref[...] += _dot(kvn, dvb, _TN)
            dkvn = _dot(dkb, wk_ref[...], _NT) + _dot(dvb, wv_ref[...], _NT)
            dkl, dgkv = _rms_bwd(dkvn, kh, rk, gkvv)
            dgkv_ref[...] += dgkv

            dkpe = dkv[:, :HEAD_PAD]
            for h in range(1, C_HEADS):
                dkpe = dkpe + dkv[:, h * HEAD_PAD:(h + 1) * HEAD_PAD]
            dkpe = _rope_t(jnp.where(_kpe_mask(dkpe.shape), dkpe, 0.0), cst, s1t, s2t)
            dz_ref[rs, :] = jnp.concatenate([dql, dkl, dkpe], axis=1).astype(BF16)

    full = lambda r, c: pl.BlockSpec((r, c), lambda i: (0, 0))
    return pl.pallas_call(
        body, name="mla_pre_bwd",
        out_shape=(_sds((s, _ZQ), BF16), _sds((C_Q_RANK, _HW)), _sds((C_KV_RANK, _HW)), _sds((C_KV_RANK, _HW)),
                   _sds((1, C_Q_RANK)), _sds((1, C_KV_RANK))),
        grid=(s // tm,),
        in_specs=[_rows(tm, _ZQ), _vec(C_Q_RANK), _vec(C_KV_RANK), full(C_Q_RANK, _HW), full(C_KV_RANK, _HW),
                  full(C_KV_RANK, _HW), _rows(tm, LANES), _rows(tm, LANES), _rows(tm, LANES),
                  _rows(tm, _HW), _rows(tm, _HW), _rows(tm, _HW)],
        out_specs=(_rows(tm, _ZQ), full(C_Q_RANK, _HW), full(C_KV_RANK, _HW), full(C_KV_RANK, _HW),
                   _vec(C_Q_RANK), _vec(C_KV_RANK)),
        compiler_params=_params(("arbitrary",), VMEM_BIG),
    )(z, gq, gkv, wq, wk, wv, cs, s1, s2, dq, dk, dv)


_ATT_SCALE = (C_NOPE + C_ROPE) ** -0.5
_NEG = -1e30


def _att_exp(q, k, row0, ends_here):
    sc = _dot(q, k, _NT) * _ATT_SCALE
    tq, nk = sc.shape
    if ends_here:
        last = sc[:, nk - tq:]
        row = lax.broadcasted_iota(jnp.int32, last.shape, 0)
        col = lax.broadcasted_iota(jnp.int32, last.shape, 1)
        last = jnp.where(col <= row, last, _NEG)
        sc = last if nk == tq else jnp.concatenate([sc[:, :nk - tq], last], axis=1)
    else:
        qpos = row0 + lax.broadcasted_iota(jnp.int32, sc.shape, 0)
        kpos = lax.broadcasted_iota(jnp.int32, sc.shape, 1)
        sc = jnp.where(kpos <= qpos, sc, _NEG)
    e = jnp.exp(sc - jnp.max(sc, axis=-1, keepdims=True))
    return e, 1.0 / jnp.sum(e, axis=-1, keepdims=True)


def _causal_cases(i, nq, tq, fn):
    if nq > 8:
        fn(nq * tq, False)
        return
    for blk in range(nq):
        pl.when(i == blk)(functools.partial(fn, (blk + 1) * tq, True))


_FWD_HEADS_PER_STEP = 4
_BWD_HEADS_PER_STEP = 2


def _head_lanes(heads):
    return [slice(h * HEAD_PAD, (h + 1) * HEAD_PAD) for h in range(heads)]


def attn_fwd(q, k, v):
    s = q.shape[0]
    tq = _tile(s, 256, 8)
    nq = s // tq
    heads = _FWD_HEADS_PER_STEP
    wide = heads * HEAD_PAD

    def body(q_ref, k_ref, v_ref, o_ref):
        i = pl.program_id(1)

        def case(nk, ends_here):
            for hd in _head_lanes(heads):
                e, inv = _att_exp(q_ref[:, hd], k_ref[:nk, hd], i * tq, ends_here)
                o_ref[:, hd] = (_dot(e.astype(BF16), v_ref[:nk, hd]) * inv).astype(BF16)

        _causal_cases(i, nq, tq, case)

    qspec = pl.BlockSpec((tq, wide), lambda h, i: (i, h))
    kspec = pl.BlockSpec((s, wide), lambda h, i: (0, h))
    return pl.pallas_call(
        body, name="attn_fwd", out_shape=_sds((s, _HW + _DW), BF16), grid=(C_HEADS // heads, s // tq),
        in_specs=[qspec, kspec, kspec], out_specs=qspec,
        compiler_params=_params(("parallel", "parallel"), VMEM_BIG),
    )(q, k, v)


def attn_bwd(q, k, v, o, do_all):
    s = q.shape[0]
    tq = _tile(s, 256, 8)
    heads = _BWD_HEADS_PER_STEP
    wide = heads * HEAD_PAD

    def body(q_ref, k_ref, v_ref, o_ref, do_ref, dq_ref, dk_ref, dv_ref):
        i = pl.program_id(1)

        @pl.when(i == 0)
        def _():
            dk_ref[...] = jnp.zeros_like(dk_ref)
            dv_ref[...] = jnp.zeros_like(dv_ref)

        def case(nk, ends_here):
            for hd in _head_lanes(heads):
                qv, kv, vv, dov = q_ref[:, hd], k_ref[:nk, hd], v_ref[:nk, hd], do_ref[:, hd]
                e, inv = _att_exp(qv, kv, i * tq, ends_here)
                p = e * inv
                dp = _dot(dov, vv, _NT)
                delta = jnp.sum(dov.astype(F32) * o_ref[:, hd].astype(F32), axis=-1, keepdims=True)
                ds = (p * (dp - delta) * _ATT_SCALE).astype(BF16)
                dq_ref[:, hd] = _dot(ds, kv).astype(BF16)
                dk_ref[:nk, hd] += _dot(ds, qv, _TN)
                dv_ref[:nk, hd] += _dot(p.astype(BF16), dov, _TN)

        _causal_cases(i, s // tq, tq, case)

    qspec = pl.BlockSpec((tq, wide), lambda h, i: (i, h))
    kspec = pl.BlockSpec((s, wide), lambda h, i: (0, h))
    return pl.pallas_call(
        body, name="attn_bwd", out_shape=(_sds((s, _HW), BF16), _sds((s, _HW)), _sds((s, _HW))),
        grid=(C_HEADS // heads, s // tq),
        in_specs=[qspec, kspec, kspec, qspec, qspec], out_specs=(qspec, kspec, kspec),
        compiler_params=_params(("parallel", "arbitrary"), VMEM_BIG),
    )(q, k, v, o, do_all)


_DW = D_GROUPS * LANES


def _tril_bf16(w):
    r = lax.broadcasted_iota(jnp.int32, w.shape, 0)
    c = lax.broadcasted_iota(jnp.int32, w.shape, 1)
    return jnp.where(c <= r, w, 0.0).astype(BF16)


def _sgu_forward(zu, zv, lg, lb, ws_ref, bs):
    u = _gelu(zu)
    v = _gelu(zv)
    mu = jnp.mean(v, axis=-1, keepdims=True)
    vc = v - mu
    rstd = lax.rsqrt(jnp.mean(vc * vc, axis=-1, keepdims=True) + EPS)
    xh = vc * rstd
    vln = (xh * lg + lb).astype(BF16)
    mixed = []
    for g in range(D_GROUPS):
        wg = _tril_bf16(ws_ref[g])
        mixed.append(_dot(wg, vln[:, g * LANES:(g + 1) * LANES]) + bs[:, g:g + 1])
    return u, xh, rstd, vln, jnp.concatenate(mixed, axis=1)


_SGU_CHUNKS = 4


def sgu_fwd(z, lg, lb, ws, bs_t, ycat):
    s = z.shape[0]
    rows = _SGU_CHUNKS * D_CHUNK

    def body(zu_ref, zv_ref, lg_ref, lb_ref, ws_ref, bs_ref, ycat_ref, o_ref):
        del ycat_ref
        for c in range(_SGU_CHUNKS):
            rs = slice(c * D_CHUNK, (c + 1) * D_CHUNK)
            u, _, _, _, mixed = _sgu_forward(zu_ref[rs, :].astype(F32), zv_ref[rs, :].astype(F32), lg_ref[...],
                                             lb_ref[...], ws_ref, bs_ref[...])
            o_ref[rs, :] = (u * mixed).astype(BF16)

    return pl.pallas_call(
        body, name="sgu_fwd", out_shape=_sds(ycat.shape, BF16), grid=(s // rows,),
        in_specs=[pl.BlockSpec((rows, _DW), lambda n: (n, 1)), pl.BlockSpec((rows, _DW), lambda n: (n, 2)),
                  _vec(_DW), _vec(_DW), pl.BlockSpec((D_GROUPS, D_CHUNK, D_CHUNK), lambda n: (0, 0, 0)),
                  pl.BlockSpec((D_CHUNK, LANES), lambda n: (0, 0)), pl.BlockSpec(memory_space=pl.ANY)],
        out_specs=pl.BlockSpec((rows, _DW), lambda n: (n, _HW // _DW)),
        input_output_aliases={6: 0},
        compiler_params=_params(("parallel",)),
    )(z, z, lg, lb, ws, bs_t, ycat)


def sgu_bwd(z, lg, lb, ws, bs_t, dycat, dy_col):
    s = z.shape[0]
    rows = _SGU_CHUNKS * D_CHUNK

    def body(zu_ref, zv_ref, lg_ref, lb_ref, ws_ref, bs_ref, dy_ref, dzu_ref, dzv_ref, dws_ref, dbs_ref, dlg_ref,
             dlb_ref):
        @pl.when(pl.program_id(0) == 0)
        def _():
            dws_ref[...] = jnp.zeros_like(dws_ref)
            dbs_ref[...] = jnp.zeros_like(dbs_ref)
            dlg_ref[...] = jnp.zeros_like(dlg_ref)
            dlb_ref[...] = jnp.zeros_like(dlb_ref)

        lg = lg_ref[...]
        lane = lax.broadcasted_iota(jnp.int32, (D_CHUNK, LANES), 1)
        row = lax.broadcasted_iota(jnp.int32, (D_CHUNK, D_CHUNK), 0)
        colm = lax.broadcasted_iota(jnp.int32, (D_CHUNK, D_CHUNK), 1)
        for c in range(_SGU_CHUNKS):
            rs = slice(c * D_CHUNK, (c + 1) * D_CHUNK)
            zu, zv = zu_ref[rs, :].astype(F32), zv_ref[rs, :].astype(F32)
            u, xh, rstd, vln, mixed = _sgu_forward(zu, zv, lg, lb_ref[...], ws_ref, bs_ref[...])
            dy = dy_ref[rs, :].astype(F32)
            dzu_ref[rs, :] = (dy * mixed * _gelu_grad(zu)).astype(BF16)
            dmix = dy * u
            dvln = []
            dbs = jnp.zeros((D_CHUNK, LANES), F32)
            for g in range(D_GROUPS):
                sl = slice(g * LANES, (g + 1) * LANES)
                dmg = dmix[:, sl]
                dbs = dbs + jnp.where(lane == g, jnp.sum(dmg, axis=-1, keepdims=True), 0.0)
                dmb = dmg.astype(BF16)
                dws_ref[g] += jnp.where(colm <= row, _dot(dmb, vln[:, sl], _NT), 0.0)
                dvln.append(_dot(_tril_bf16(ws_ref[g]), dmb, _TN))
            dbs_ref[...] += dbs
            dvln = jnp.concatenate(dvln, axis=1)
            dlg_ref[...] += jnp.sum(dvln * xh, axis=0, keepdims=True)
            dlb_ref[...] += jnp.sum(dvln, axis=0, keepdims=True)
            dxh = dvln * lg
            dvv = rstd * (dxh - jnp.mean(dxh, axis=-1, keepdims=True)
                          - xh * jnp.mean(dxh * xh, axis=-1, keepdims=True))
            dzv_ref[rs, :] = (dvv * _gelu_grad(zv)).astype(BF16)

    wsspec = pl.BlockSpec((D_GROUPS, D_CHUNK, D_CHUNK), lambda n: (0, 0, 0))
    chunk = lambda cidx: pl.BlockSpec((rows, _DW), lambda n: (n, cidx))
    return pl.pallas_call(
        body, name="sgu_bwd",
        out_shape=(_sds((s, _DW), BF16), _sds((s, _DW), BF16), _sds((D_GROUPS, D_CHUNK, D_CHUNK)),
                   _sds((D_CHUNK, LANES)), _sds((1, _DW)), _sds((1, _DW))),
        grid=(s // rows,),
        in_specs=[chunk(1), chunk(2), _vec(_DW), _vec(_DW), wsspec, pl.BlockSpec((D_CHUNK, LANES), lambda n: (0, 0)),
                  chunk(dy_col)],
        out_specs=(chunk(0), chunk(0), wsspec, pl.BlockSpec((D_CHUNK, LANES), lambda n: (0, 0)), _vec(_DW), _vec(_DW)),
        compiler_params=_params(("arbitrary",)),
    )(z, z, lg, lb, ws, bs_t, dycat)


def ada_mod(c_all, ada_w, ada_b):
    nl, d, n = ada_w.shape
    nb = c_all.shape[0]
    tn = _tile(n, 512)

    def body(c_ref, w_ref, b_ref, o_ref):
        cv = c_ref[...]
        ca = (cv * _sigmoid(cv)).astype(BF16)
        o_ref[...] = _dot(ca, w_ref[...].astype(BF16)) + b_ref[...]

    return pl.pallas_call(
        body, name="ada_mod", out_shape=_sds((nl, nb, n)), grid=(nl, n // tn),
        in_specs=[pl.BlockSpec((nb, d), lambda l, j: (0, 0)), pl.BlockSpec((None, d, tn), lambda l, j: (l, 0, j)),
                  pl.BlockSpec((None, 1, tn), lambda l, j: (l, 0, j))],
        out_specs=pl.BlockSpec((None, nb, tn), lambda l, j: (l, 0, j)),
        compiler_params=_params(("parallel", "parallel")),
    )(c_all, ada_w, ada_b.reshape(nl, 1, n))


_ADAM_BLOCK = 512 * 1024


def _adam_rows(rows, cols):
    if rows * cols <= _ADAM_BLOCK or rows % 8:
        return rows
    return _tile(rows, max(8, _ADAM_BLOCK // cols), 8)


def _adam_update(w, gv, m, v):
    inv_bc1 = 1.0 / (1.0 - ADAM_B1 ** ADAM_STEP)
    inv_bc2 = 1.0 / (1.0 - ADAM_B2 ** ADAM_STEP)
    nm = ADAM_B1 * m + (1.0 - ADAM_B1) * gv
    nv = ADAM_B2 * v + (1.0 - ADAM_B2) * (gv * gv)
    return -ADAM_LR * ((nm * inv_bc1) / (jnp.sqrt(nv * inv_bc2) + ADAM_EPS) + ADAM_WD * w), nm, nv


def adamw(w, g, m, v, name):
    shape = w.shape
    cols = shape[-1]
    rows = w.size // cols
    tr = _adam_rows(rows, cols)

    def body(w_ref, g_ref, m_ref, v_ref, go_ref, d_ref, nm_ref, nv_ref):
        gv = g_ref[...]
        go_ref[...] = gv
        d_ref[...], nm_ref[...], nv_ref[...] = _adam_update(w_ref[...], gv, m_ref[...], v_ref[...])

    spec = pl.BlockSpec((tr, cols), lambda i: (i, 0))
    out = _sds((rows, cols))
    r2 = lambda t: t.reshape(rows, cols)
    res = pl.pallas_call(
        body, name=name, out_shape=(out,) * 4, grid=(rows // tr,),
        in_specs=[spec] * 4, out_specs=(spec,) * 4, compiler_params=_params(("parallel",), VMEM_BIG),
    )(r2(w), r2(g), r2(m), r2(v))
    return tuple(t.reshape(shape) for t in res)


def adamw_ada(w, c_all, dmod, m, v):
    nl, d, n = w.shape
    tr = _adam_rows(d, n)
    pad = 16 - c_all.shape[0]
    c16 = jnp.pad(c_all, ((0, pad), (0, 0)))
    dm16 = jnp.pad(dmod, ((0, 0), (0, pad), (0, 0)))

    def body(w_ref, c_ref, dm_ref, m_ref, v_ref, g_ref, d_ref, nm_ref, nv_ref):
        cv = c_ref[...]
        gv = _dot((cv * _sigmoid(cv)).astype(BF16), dm_ref[...].astype(BF16), _TN)
        g_ref[...] = gv
        d_ref[...], nm_ref[...], nv_ref[...] = _adam_update(w_ref[...], gv, m_ref[...], v_ref[...])

    spec = pl.BlockSpec((None, tr, n), lambda l, i: (l, i, 0))
    out = _sds((nl, d, n))
    return pl.pallas_call(
        body, name="adamw_ada_w", out_shape=(out, out, out, out), grid=(nl, d // tr),
        in_specs=[spec, pl.BlockSpec((16, tr), lambda l, i: (0, i)), pl.BlockSpec((None, 16, n), lambda l, i: (l, 0, 0)),
                  spec, spec],
        out_specs=(spec,) * 4, compiler_params=_params(("parallel", "parallel"), VMEM_BIG),
    )(w, c16, dm16, m, v)


def adamw_small(ws, gs, ms, vs):
    n = len(ws)
    flat = lambda t: t.reshape(-1, t.shape[-1])

    def body(*refs):
        ins, outs = refs[:4 * n], refs[4 * n:]
        for i in range(n):
            w_ref, g_ref, m_ref, v_ref = ins[4 * i:4 * i + 4]
            outs[3 * i][...], outs[3 * i + 1][...], outs[3 * i + 2][...] = _adam_update(
                w_ref[...], g_ref[...], m_ref[...], v_ref[...])

    operands = [flat(t) for quad in zip(ws, gs, ms, vs) for t in quad]
    res = pl.pallas_call(
        body, name="adamw_small", out_shape=tuple(_sds(flat(w).shape) for w in ws for _ in range(3)),
    )(*operands)
    return [(g, res[3 * i].reshape(w.shape), res[3 * i + 1].reshape(w.shape), res[3 * i + 2].reshape(w.shape))
            for i, (w, g) in enumerate(zip(ws, gs))]


def adamw_layers(w, g0, g1, m, v, name):
    _, rows, cols = w.shape
    tr = _adam_rows(rows, cols)

    def body(w_ref, g0_ref, g1_ref, m_ref, v_ref, g_ref, d_ref, nm_ref, nv_ref):
        gv = jnp.where(pl.program_id(0) == 0, g0_ref[...], g1_ref[...])
        g_ref[...] = gv
        d_ref[...], nm_ref[...], nv_ref[...] = _adam_update(w_ref[...], gv, m_ref[...], v_ref[...])

    spec = pl.BlockSpec((None, tr, cols), lambda l, i: (l, i, 0))
    gspec = pl.BlockSpec((tr, cols), lambda l, i: (i, 0))
    out = _sds((2, rows, cols))
    return pl.pallas_call(
        body, name=name, out_shape=(out, out, out, out), grid=(2, rows // tr),
        in_specs=[spec, gspec, gspec, spec, spec], out_specs=(spec,) * 4,
        compiler_params=_params(("parallel", "parallel"), VMEM_BIG),
    )(w, g0, g1, m, v)


def sum8(gathered):
    _, r, _ = gathered.shape
    tr = _tile(r, 512, 8)

    def body(g_ref, o_ref):
        acc = g_ref[0]
        for dev in range(1, N_DEV):
            acc = acc + g_ref[dev]
        o_ref[...] = acc

    return pl.pallas_call(
        body, name="sum8", out_shape=_sds((r, LANES)), grid=(r // tr,),
        in_specs=[pl.BlockSpec((N_DEV, tr, LANES), lambda i: (0, i, 0))], out_specs=pl.BlockSpec((tr, LANES), lambda i: (i, 0)),
        compiler_params=_params(("parallel",)),
    )(gathered)


_SUM_STEPS = 2


def pair_sums(gs, recvs, core, name):
    n = len(gs)
    trs = [g.shape[1] // 2 // _SUM_STEPS for g in gs]

    def body(c_ref, *refs):
        del c_ref
        for i in range(n):
            a_ref, b_ref, o_ref = refs[2 * i], refs[2 * i + 1], refs[2 * n + i]
            o_ref[...] = (a_ref[...].astype(F32) + b_ref[...].astype(F32)).astype(BF16)

    in_specs, out_specs = [], []
    for g, tr in zip(gs, trs):
        cols = g.shape[2]
        in_specs.append(pl.BlockSpec((None, tr, cols), lambda k, s, c: (k, c[0] * _SUM_STEPS + s, 0)))
        in_specs.append(pl.BlockSpec((None, tr, cols), lambda k, s, c: (k, s, 0)))
        out_specs.append(pl.BlockSpec((None, tr, cols), lambda k, s, c: (k, s, 0)))
    grid_spec = pltpu.PrefetchScalarGridSpec(num_scalar_prefetch=1, grid=(N_CHIPS, _SUM_STEPS), in_specs=in_specs,
                                             out_specs=tuple(out_specs))
    return list(pl.pallas_call(
        body, name=name, out_shape=tuple(_sds((N_CHIPS, g.shape[1] // 2, g.shape[2]), BF16) for g in gs),
        grid_spec=grid_spec, compiler_params=_params(("parallel", "parallel")),
    )(core.reshape(1).astype(jnp.int32), *[t for pair in zip(gs, recvs) for t in pair]))


def chip_sums(pairs, recvs, chip, core, name):
    n = len(pairs)
    trs = [p.shape[1] // _SUM_STEPS for p in pairs]

    def body(p_ref, *refs):
        del p_ref
        for i in range(n):
            own_ref, r_ref, o_ref = refs[2 * i], refs[2 * i + 1], refs[2 * n + i]
            acc = own_ref[...].astype(F32)
            for j in range(N_CHIPS - 1):
                acc = acc + r_ref[j].astype(F32)
            o_ref[...] = acc

    in_specs, out_specs = [], []
    for p, tr in zip(pairs, trs):
        cols = p.shape[2]
        in_specs.append(pl.BlockSpec((None, tr, cols), lambda s, q: (q[0], s, 0)))
        in_specs.append(pl.BlockSpec((N_CHIPS - 1, tr, cols), lambda s, q: (0, s, 0)))
        out_specs.append(pl.BlockSpec((None, tr, cols), lambda s, q: (q[1], s, 0)))
    grid_spec = pltpu.PrefetchScalarGridSpec(num_scalar_prefetch=1, grid=(_SUM_STEPS,), in_specs=in_specs,
                                             out_specs=tuple(out_specs))
    return list(pl.pallas_call(
        body, name=name, out_shape=tuple(_sds((2,) + p.shape[1:]) for p in pairs), grid_spec=grid_spec,
        compiler_params=_params(("parallel",)),
    )(jnp.stack([chip, core]).astype(jnp.int32), *[t for pair in zip(pairs, recvs) for t in pair]))


def _place():
    return lax.axis_index("x"), lax.axis_index("y"), lax.axis_index("c")


def _other_chips(x, y):
    return [(x, 1 - y), (1 - x, y), (1 - x, 1 - y)]


_HBM = pl.BlockSpec(memory_space=pltpu.HBM)


def all_gather8(v, name, after=()):
    m, n = v.shape

    def body(x_ref, *refs):
        out_ref, send_sems, recv_sems, local_sem = refs[len(after):]
        x, y, c = _place()
        me, sibling = (x, y, c), (x, y, 1 - c)
        chips = _other_chips(x, y)

        def rows(px, py, pc):
            return out_ref.at[pl.ds((4 * px + 2 * py + pc) * m, m), :]

        def copy(k, block, to, src=None):
            return pltpu.make_async_remote_copy(
                src_ref=rows(*block) if src is None else src, dst_ref=rows(*block),
                send_sem=send_sems.at[k], recv_sem=recv_sems.at[k], device_id=to, device_id_type=MESH)

        mine = pltpu.make_async_copy(x_ref, rows(*me), local_sem)
        mine.start()
        first = [copy(0, me, sibling, src=x_ref)]
        first += [copy(1 + j, me, (*chip, c), src=x_ref) for j, chip in enumerate(chips)]
        for cp in first:
            cp.start()
        passed = [copy(4 + j, (*chip, c), sibling) for j, chip in enumerate(chips)]
        for j, chip in enumerate(chips):
            copy(1 + j, (*chip, c), me).wait_recv()
            passed[j].start()
        copy(0, sibling, me).wait_recv()
        for j, chip in enumerate(chips):
            copy(4 + j, (*chip, 1 - c), me).wait_recv()
        for cp in first + passed:
            cp.wait_send()
        mine.wait()

    return pl.pallas_call(
        body, name=name, out_shape=_sds((N_DEV * m, n), v.dtype),
        in_specs=[pl.BlockSpec(memory_space=pltpu.VMEM)] + [pl.BlockSpec(memory_space=pl.ANY)] * len(after),
        out_specs=pl.BlockSpec(memory_space=pltpu.VMEM),
        scratch_shapes=[pltpu.SemaphoreType.DMA((7,)), pltpu.SemaphoreType.DMA((7,)), pltpu.SemaphoreType.DMA],
        compiler_params=_params(None, VMEM_BIG),
    )(v, *after)


def _comm_call(body, name, ins, out_shapes, nsem, aliases=None):
    return pl.pallas_call(
        body, name=name, out_shape=tuple(out_shapes), in_specs=[_HBM] * len(ins), out_specs=tuple([_HBM] * len(out_shapes)),
        scratch_shapes=[pltpu.SemaphoreType.DMA((nsem,)), pltpu.SemaphoreType.DMA((nsem,))],
        input_output_aliases=aliases or {},
    )(*ins)


def _remote(src, dst, send_sems, recv_sems, k, to):
    return pltpu.make_async_remote_copy(src_ref=src, dst_ref=dst, send_sem=send_sems.at[k], recv_sem=recv_sems.at[k],
                                        device_id=to, device_id_type=MESH)


def _half(core, rh):
    return pl.ds(pl.multiple_of(core * rh, 16), rh)


def swap_halves(gs, name):
    n = len(gs)

    def body(*refs):
        ins, outs, (send_sems, recv_sems) = refs[:n], refs[n:2 * n], refs[2 * n:]
        x, y, c = _place()
        copies = []
        for i in range(n):
            theirs = _half(1 - c, ins[i].shape[1] // 2)
            cp = _remote(ins[i].at[:, theirs], outs[i], send_sems, recv_sems, i, (x, y, 1 - c))
            cp.start()
            copies.append(cp)
        for cp in copies:
            cp.wait()

    return _comm_call(body, name, gs, [_sds((g.shape[0], g.shape[1] // 2, g.shape[2]), g.dtype) for g in gs], n)


def join_halves(bufs, name):
    n = len(bufs)

    def body(*refs):
        ins, outs, (send_sems, recv_sems) = refs[:n], refs[n:2 * n], refs[2 * n:]
        x, y, c = _place()
        copies = []
        for i in range(n):
            cp = _remote(ins[i].at[c], outs[i].at[c], send_sems, recv_sems, i, (x, y, 1 - c))
            cp.start()
            copies.append(cp)
        for i in range(n):
            theirs = outs[i].at[1 - c]
            _remote(theirs, theirs, send_sems, recv_sems, i, (x, y, 1 - c)).wait_recv()
        for cp in copies:
            cp.wait_send()

    return _comm_call(body, name, bufs, [_sds(b.shape, b.dtype) for b in bufs], n, {i: i for i in range(n)})


def forward_halves(lands, name):
    n = len(lands)

    def body(*refs):
        ins, outs, (send_sems, recv_sems) = refs[:n], refs[n:2 * n], refs[2 * n:]
        x, y, c = _place()
        sibling = (x, y, 1 - c)
        chips = _other_chips(x, y)
        copies = []
        for i in range(n):
            mine = _half(c, ins[i].shape[1] // 2)
            for j, (px, py) in enumerate(chips):
                cp = _remote(ins[i].at[2 * px + py, mine], outs[i].at[2 * px + py, mine], send_sems, recv_sems, 3 * i + j, sibling)
                cp.start()
                copies.append(cp)
        for i in range(n):
            theirs = _half(1 - c, ins[i].shape[1] // 2)
            for j, (px, py) in enumerate(chips):
                landed = outs[i].at[2 * px + py, theirs]
                _remote(landed, landed, send_sems, recv_sems, 3 * i + j, sibling).wait_recv()
        for cp in copies:
            cp.wait_send()

    return _comm_call(body, name, lands, [_sds(b.shape, b.dtype) for b in lands], 3 * n, {i: i for i in range(n)})


_SEM = pl.BlockSpec(memory_space=pltpu.SEMAPHORE)
_EFFECT = pltpu.SideEffectType.DATAFLOW_SIDE_EFFECTING


def _gather_copies(srcs, lands, send_sems, recv_sems):
    x, y, c = _place()
    copies = []
    for i in range(len(srcs)):
        mine = _half(c, srcs[i].shape[0] // 2)
        for j, chip in enumerate(_other_chips(x, y)):
            copies.append(_remote(srcs[i].at[mine], lands[i].at[2 * x + y, mine], send_sems, recv_sems, 3 * i + j, (*chip, c)))
    return copies


def _exchange_copies(srcs, lands, send_sems, recv_sems):
    x, y, c = _place()
    copies = []
    for i in range(len(srcs)):
        for j, (px, py) in enumerate(_other_chips(x, y)):
            copies.append(_remote(srcs[i].at[2 * px + py], lands[i].at[j], send_sems, recv_sems, 3 * i + j, (px, py, c)))
    return copies


def _everyone_copies(srcs, lands, send_sems, recv_sems):
    x, y, c = _place()
    flip = lambda v, b: 1 - v if b else v
    dst = lands[0].at[4 * x + 2 * y + c]
    return [_remote(srcs[0], dst, send_sems, recv_sems, j - 1, (flip(x, j & 4), flip(y, j & 2), flip(c, j & 1)))
            for j in range(1, N_DEV)]


GATHER = (_gather_copies, 3)
EXCHANGE = (_exchange_copies, 3)
EVERYONE = (_everyone_copies, N_DEV - 1)


def split_start(name, plan, srcs, land_shapes, after=()):
    copies_fn, per_source = plan
    n, m, k = len(srcs), len(land_shapes), len(after)
    ncopies = per_source * n

    def body(*refs):
        src_refs, land_refs = refs[:n], refs[n:n + m]
        send_sems, recv_sems = refs[n + m + k], refs[n + m + k + 1]
        token = refs[-1]
        for cp in copies_fn(src_refs, land_refs, send_sems, recv_sems):
            cp.start()
        token[...] = jnp.zeros_like(token)

    hbm = lambda s: pltpu.HBM(tuple(s.shape), s.dtype)
    outs = pl.pallas_call(
        body, name=name,
        out_shape=(pltpu.SemaphoreType.DMA((ncopies,)), pltpu.SemaphoreType.DMA((ncopies,)), *[hbm(s) for s in srcs],
                   *[hbm(s) for s in land_shapes], _sds((8, LANES))),
        in_specs=[_HBM] * (n + m) + [pl.BlockSpec(memory_space=pl.ANY)] * k,
        out_specs=(_SEM, _SEM, *([_HBM] * (n + m)), pl.BlockSpec(memory_space=pltpu.VMEM)),
        input_output_aliases={i: 2 + i for i in range(n + m)},
        compiler_params=pltpu.CompilerParams(has_side_effects=_EFFECT),
    )(*[pltpu.with_memory_space_constraint(s, pltpu.HBM) for s in srcs],
      *[pltpu.with_memory_space_constraint(lax.empty(tuple(s.shape), s.dtype), pltpu.HBM) for s in land_shapes], *after)
    handle = (outs[0], outs[1], list(outs[2:2 + n]), list(outs[2 + n:2 + n + m]))
    return handle, outs[-1][0, 0]


def split_wait(name, plan, handle, after):
    copies_fn, _ = plan
    send_sems, recv_sems, srcs, lands = handle
    n, m = len(srcs), len(lands)
    after = list(after) if isinstance(after, (list, tuple)) else [after]

    def body(*refs):
        src_refs, land_refs = refs[:n], refs[n:n + m]
        for cp in copies_fn(src_refs, land_refs, refs[n + m], refs[n + m + 1]):
            cp.wait_send()
            cp.wait_recv()

    hbm = lambda s: pltpu.HBM(tuple(s.shape), s.dtype)
    outs = pl.pallas_call(
        body, name=name, out_shape=tuple(hbm(s) for s in srcs + lands),
        in_specs=[_HBM] * (n + m) + [_SEM, _SEM] + [pl.BlockSpec(memory_space=pl.ANY)] * len(after),
        out_specs=tuple([_HBM] * (n + m)), input_output_aliases={i: i for i in range(n + m)},
        compiler_params=pltpu.CompilerParams(has_side_effects=_EFFECT),
    )(*srcs, *lands, send_sems, recv_sems, *after)
    return list(outs[:n]), list(outs[n:])


def chip_major(w, groups=N_CHIPS):
    r, c = w.shape
    return w.reshape(r, groups, c // groups).transpose(1, 0, 2)


def from_chip_major(w):
    g, r, c = w.shape
    return w.transpose(1, 0, 2).reshape(r, g * c)


def _cd_in_pad(w):
    a = C_Q_RANK + C_KV_RANK
    z = lambda n: jnp.zeros((w.shape[0], n), w.dtype)
    return jnp.concatenate([w[:, :a], z(C_NOPE), w[:, a:a + C_ROPE], z(HEAD_PAD - C_NOPE - C_ROPE), w[:, a + C_ROPE:]], axis=1)


def _cd_in_unpad(w):
    a = C_Q_RANK + C_KV_RANK
    return jnp.concatenate([w[:, :a], w[:, a + C_NOPE:a + C_NOPE + C_ROPE], w[:, a + HEAD_PAD:]], axis=1)


def _pad_heads(w, width):
    r = w.shape[0]
    w = w.reshape(r, C_HEADS, width)
    return jnp.pad(w, ((0, 0), (0, 0), (0, HEAD_PAD - width))).reshape(r, _HW)


def _unpad_heads(w, width):
    r = w.shape[0]
    return w.reshape(r, C_HEADS, HEAD_PAD)[:, :, :width].reshape(r, C_HEADS * width)


def prepare_weights(p):
    q = dict(p)
    q["cd_w_in"] = _cd_in_pad(p["cd_w_in"])
    q["c_w_uq"] = _pad_heads(p["c_w_uq"], C_NOPE + C_ROPE)
    ukv = p["c_w_ukv"].reshape(C_KV_RANK, C_HEADS, C_NOPE + C_V)
    q["c_w_uk"] = _pad_heads(ukv[:, :, :C_NOPE].reshape(C_KV_RANK, -1), C_NOPE)
    q["c_w_uv"] = _pad_heads(ukv[:, :, C_NOPE:].reshape(C_KV_RANK, -1), C_V)
    wo = p["cd_w_out"]
    att_rows = jnp.pad(wo[:C_HEADS * C_V].reshape(C_HEADS, C_V, D_MODEL), ((0, 0), (0, HEAD_PAD - C_V), (0, 0)))
    q["cd_w_out"] = jnp.concatenate([att_rows.reshape(_HW, D_MODEL), wo[C_HEADS * C_V:]], axis=0)
    return q


def unprepare_grads(g):
    q = dict(g)
    q["cd_w_in"] = _cd_in_unpad(g["cd_w_in"])
    q["c_w_uq"] = _unpad_heads(g["c_w_uq"], C_NOPE + C_ROPE)
    uk = g.pop("c_w_uk").reshape(C_KV_RANK, C_HEADS, HEAD_PAD)[:, :, :C_NOPE]
    uv = g.pop("c_w_uv").reshape(C_KV_RANK, C_HEADS, HEAD_PAD)[:, :, :C_V]
    q.pop("c_w_uk", None)
    q.pop("c_w_uv", None)
    q["c_w_ukv"] = jnp.concatenate([uk, uv], axis=-1).reshape(C_KV_RANK, C_HEADS * (C_NOPE + C_V))
    wo = g["cd_w_out"]
    att = wo[:_HW].reshape(C_HEADS, HEAD_PAD, D_MODEL)[:, :C_V].reshape(C_HEADS * C_V, D_MODEL)
    q["cd_w_out"] = jnp.concatenate([att, wo[_HW:]], axis=0)
    return q


def rope_tables(positions):
    half = C_ROPE // 2
    inv_freq = ROPE_THETA ** (-jnp.arange(half, dtype=F32) / half)
    ang = positions.astype(F32)[:, None] * inv_freq
    cos, sin = jnp.cos(ang), jnp.sin(ang)
    s = positions.shape[0]
    z = lambda n: jnp.zeros((s, n), F32)
    cs = jnp.concatenate([jnp.ones((s, C_NOPE), F32), cos, cos, z(HEAD_PAD - C_NOPE - C_ROPE)], axis=1)
    s1 = jnp.concatenate([z(C_NOPE), -sin, z(HEAD_PAD - C_NOPE - half)], axis=1)
    s2 = jnp.concatenate([z(C_NOPE + half), sin, z(HEAD_PAD - C_NOPE - C_ROPE)], axis=1)
    return cs, s1, s2


_UP_COLS = 2 * D_FF // N_CHIPS


def ffn_fwd(h2, w, l, late_down=None):
    zf = matmul(h2, w["ffn_w_up"][l], "nn", BF16, f"ffn_up{l}", gb=N_CHIPS, go=2, tn=_UP_COLS)
    if late_down is not None:
        late_down(zf)
    a, f = ffn_act_down(zf, w["ffn_conv_w"][l], w["ffn_w_down"][l], f"ffn_act_down{l}")
    return f, (zf, a)


def ffn_bwd(df, h2, saved, w, l):
    zf, a = saved
    da = matmul(df, w["ffn_w_down"][l], "nt", BF16, f"ffn_down_dx{l}", tn=D_FF // 2)
    d_down = matmul(a, df, "tn", BF16, f"ffn_down_dw{l}", tm=D_FF // 2)
    dzf, d_conv = ffn_act_bwd(zf, w["ffn_conv_w"][l], da, f"ffn_act_bwd{l}")
    dh2 = matmul(dzf, w["ffn_w_up"][l], "nt", BF16, f"ffn_up_dx{l}", ga=2, gb=N_CHIPS, tk=_UP_COLS, tn=D_MODEL)
    d_up = matmul(h2, dzf, "tn", BF16, f"ffn_up_dw{l}", gb=2, go=N_CHIPS, tn=_UP_COLS)
    d_conv = d_conv.transpose(1, 0, 2).reshape(3, 2 * D_FF)
    return dh2, dict(ffn_w_down=d_down, ffn_conv_w=d_conv, ffn_w_up=d_up)


def mixer0_fwd(h, w):
    z = matmul(h, w["ab_w_in"], "nn", BF16, "ab_in", gb=N_CHIPS)
    ycat = pool_fwd(z, w["b_mix_w"], w["b_scale"], gconv_fwd(z, w["a_conv_w"]))
    y = matmul(ycat, w["ab_w_out"], "nn", BF16, "ab_out", tn=D_MODEL)
    return y, (z, ycat)


def mixer0_bwd(dy, h, saved, w):
    z, ycat = saved
    grads = {}
    dycat = matmul(dy, w["ab_w_out"], "nt", BF16, "ab_out_dx")
    grads["ab_w_out"] = matmul(ycat, dy, "tn", BF16, "ab_out_dw")
    db, dc, da, d_conv = gconv_bwd(z, w["a_conv_w"], dycat)
    dp, d_mix, d_scale = pool_bwd(z, w["b_mix_w"], w["b_scale"], dycat)
    dz = jnp.concatenate([db, dc, da, dp], axis=1)
    dh = matmul(dz, w["ab_w_in"], "nt", BF16, "ab_in_dx", gb=N_CHIPS, tn=D_MODEL)
    grads["ab_w_in"] = matmul(h, dz, "tn", BF16, "ab_in_dw", go=N_CHIPS)
    grads.update(a_conv_w=d_conv, b_mix_w=d_mix, b_scale=d_scale)
    return dh, grads


def mixer1_fwd(h, ropes, w):
    cs, s1, s2 = ropes
    z = matmul(h, w["cd_w_in"], "nn", BF16, "cd_in")
    bs_t = jnp.pad(w["d_b_s"].T, ((0, 0), (0, LANES - D_GROUPS)))
    qh, kh, vh = mla_pre_fwd(z, w["c_q_norm_g"], w["c_kv_norm_g"], w["c_w_uq"], w["c_w_uk"], w["c_w_uv"], cs, s1, s2)
    ycat = sgu_fwd(z, w["d_ln_g"], w["d_ln_b"], w["d_w_s"], bs_t, attn_fwd(qh, kh, vh))
    y = matmul(ycat, w["cd_w_out"], "nn", BF16, "cd_out", tn=D_MODEL)
    return y, (z, bs_t, qh, kh, vh, ycat)


def mixer1_bwd(dy, h, saved, ropes, w):
    cs, s1, s2 = ropes
    z, bs_t, qh, kh, vh, ycat = saved
    grads = {}
    dycat = matmul(dy, w["cd_w_out"], "nt", BF16, "cd_out_dx")
    grads["cd_w_out"] = matmul(ycat, dy, "tn", BF16, "cd_out_dw")
    dqh, dkh, dvh = attn_bwd(qh, kh, vh, ycat, dycat)
    dzq, d_uq, d_uk, d_uv, d_gq, d_gkv = mla_pre_bwd(
        z, w["c_q_norm_g"], w["c_kv_norm_g"], w["c_w_uq"], w["c_w_uk"], w["c_w_uv"], cs, s1, s2, dqh, dkh, dvh)
    dzu, dzv, d_ws, d_bs, d_lg, d_lb = sgu_bwd(z, w["d_ln_g"], w["d_ln_b"], w["d_w_s"], bs_t, dycat, _HW // _DW)
    dz = jnp.concatenate([dzq, dzu, dzv], axis=1)
    dh = matmul(dz, w["cd_w_in"], "nt", BF16, "cd_in_dx", tn=D_MODEL)
    grads["cd_w_in"] = matmul(h, dz, "tn", BF16, "cd_in_dw")
    grads.update(c_w_uq=d_uq, c_w_uk=d_uk, c_w_uv=d_uv, c_q_norm_g=d_gq, c_kv_norm_g=d_gkv, d_w_s=d_ws,
                 d_b_s=d_bs[:, :D_GROUPS].T, d_ln_g=d_lg, d_ln_b=d_lb)
    return dh, grads


class StepHooks:
    def weights(self, stage, after):
        pass

    def gradients(self, stage, grads, after):
        return 0.0


def run_step(x, tgt, mod, ropes, w, hooks):
    sh1, sc1, g1, sh2, sc2, g2 = range(N_MOD)
    mods = mod.reshape(2, 1, N_MOD * D_MODEL)
    n1 = w["norm1_g"].reshape(2, 1, D_MODEL)
    n2 = w["norm2_g"].reshape(2, 1, D_MODEL)
    final_g = Vec(w["final_norm_g"].reshape(1, 1, D_MODEL), 0, 0)

    hooks.weights("mix0", mod)
    h0 = modnorm_fwd(x, Vec(n1, 0, 0), Vec(mods, 0, sc1), Vec(mods, 0, sh1), "modnorm_0")
    y0, mix0 = mixer0_fwd(h0, w)
    x1, h1 = resid_modnorm_fwd(x, y0, Vec(mods, 0, g1), Vec(n2, 0, 0), Vec(mods, 0, sc2), Vec(mods, 0, sh2), "resid_modnorm_1")
    hooks.weights("up0", x1)
    f0, ffn0 = ffn_fwd(h1, w, 0, lambda act: hooks.weights("down0", act))
    x2, h2 = resid_modnorm_fwd(x1, f0, Vec(mods, 0, g2), Vec(n1, 1, 0), Vec(mods, 1, sc1), Vec(mods, 1, sh1), "resid_modnorm_2")
    hooks.weights("mix1", x2)
    y1, mix1 = mixer1_fwd(h2, ropes, w)
    x3, h3 = resid_modnorm_fwd(x2, y1, Vec(mods, 1, g1), Vec(n2, 1, 0), Vec(mods, 1, sc2), Vec(mods, 1, sh2), "resid_modnorm_3")
    hooks.weights("ffn1", x3)
    f1, ffn1 = ffn_fwd(h3, w, 1)
    dres, d_final, loss, df1, dg2b = final_fused(x3, f1, Vec(mods, 1, g2), final_g, tgt)

    dh3, gf1 = ffn_bwd(df1, h3, ffn1, w, 1)
    late = mods + hooks.gradients("ffn1", gf1, dh3)
    dres, dsh2b, dsc2b, dn2b, dy1, dg1b = norm_gate_bwd(
        x3, dh3, Vec(n2, 1, 0), Vec(late, 1, sc2), dres, y1, Vec(late, 1, g1), "norm_gate_bwd_3")
    dh2, gm1 = mixer1_bwd(dy1, h2, mix1, ropes, w)
    late = mods + hooks.gradients("mix1", gm1, dh2)
    dres, dsh1b, dsc1b, dn1b, df0, dg2a = norm_gate_bwd(
        x2, dh2, Vec(n1, 1, 0), Vec(late, 1, sc1), dres, f0, Vec(late, 0, g2), "norm_gate_bwd_2")
    dh1, gf0 = ffn_bwd(df0, h1, ffn0, w, 0)
    late = mods + hooks.gradients("ffn0", gf0, dh1)
    dres, dsh2a, dsc2a, dn2a, dy0, dg1a = norm_gate_bwd(
        x1, dh1, Vec(n2, 0, 0), Vec(late, 0, sc2), dres, y0, Vec(late, 0, g1), "norm_gate_bwd_1")
    dh0, gm0 = mixer0_bwd(dy0, h0, mix0, w)
    late = mods + hooks.gradients("mix0", gm0, dh0)
    grad_x, dsh1a, dsc1a, dn1a = norm_bwd(x, dh0, Vec(n1, 0, 0), Vec(late, 0, sc1), dres, "norm_bwd_0")

    dmod = jnp.concatenate([jnp.concatenate([dsh1a, dsc1a, dg1a, dsh2a, dsc2a, dg2a], axis=1),
                            jnp.concatenate([dsh1b, dsc1b, dg1b, dsh2b, dsc2b, dg2b], axis=1)], axis=0)
    norms = dict(norm1_g=jnp.concatenate([dn1a, dn1b], axis=0), norm2_g=jnp.concatenate([dn2a, dn2b], axis=0),
                 final_norm_g=d_final)
    return loss, grad_x, dmod, dict(mix0=gm0, ffn0=gf0, mix1=gm1, ffn1=gf1, norms=norms)


def merge_grads(by_stage):
    grads = {**by_stage["mix0"], **by_stage["mix1"], **by_stage["norms"]}
    for k in ("ffn_w_down", "ffn_w_up"):
        grads[k] = [by_stage["ffn0"][k], by_stage["ffn1"][k]]
    grads["ffn_conv_w"] = jnp.stack([by_stage["ffn0"]["ffn_conv_w"], by_stage["ffn1"]["ffn_conv_w"]])
    return grads


_WEIGHTS = ("ada_w", "ada_b", "norm1_g", "norm2_g", "ab_w_in", "a_conv_w", "b_mix_w", "b_scale", "ab_w_out", "cd_w_in",
            "c_q_norm_g", "c_w_uq", "c_kv_norm_g", "c_w_ukv", "d_ln_g", "d_ln_b", "d_w_s", "d_b_s", "cd_w_out",
            "ffn_w_up", "ffn_conv_w", "ffn_w_down", "final_norm_g")
_INPUTS = ("x", "c", "positions") + _WEIGHTS + ("loss_target",) + tuple("m_" + n for n in _WEIGHTS) + tuple(
    "v_" + n for n in _WEIGHTS)

def _pack_rows(parts, rows, dtype):
    flat = jnp.concatenate([p.reshape(-1).astype(dtype) for p in parts])
    return jnp.pad(flat, (0, rows * LANES - flat.shape[0])).reshape(rows, LANES)


def _rows_major(w):
    r, c = w.shape
    return w.reshape(N_CHIPS, r // N_CHIPS, c)


def start_gather(shards, tag, after=()):
    lands = [_sds((N_CHIPS,) + s.shape, s.dtype) for s in shards]
    return split_start("gather_start_" + tag, GATHER, shards, lands, after)


def finish_gather(handle, chip, tag, after):
    shards, lands = split_wait("gather_wait_" + tag, GATHER, handle, after)
    lands = forward_halves(lands, "gather_forward_" + tag)
    return [lax.dynamic_update_index_in_dim(o, s, chip, 0) for o, s in zip(lands, shards)]


def start_reduce(gs, core, tag):
    recv = swap_halves(gs, "swap_halves_" + tag)
    pairs = pair_sums(gs, recv, core, "pair_sums_" + tag)
    lands = [_sds((N_CHIPS - 1,) + p.shape[1:], p.dtype) for p in pairs]
    return split_start("exchange_start_" + tag, EXCHANGE, pairs, lands)


def finish_reduce(handle, chip, core, tag, after):
    pairs, others = split_wait("exchange_wait_" + tag, EXCHANGE, handle, after)
    halves = chip_sums(pairs, others, chip, core, "chip_sums_" + tag)
    full = join_halves(halves, "join_halves_" + tag)
    return [f.reshape(f.shape[1] * 2, f.shape[2]) for f in full]


_SMALL_SHARDED = (("a_conv_w", (3, 128), 1), ("c_q_norm_g", (1, 64), 1), ("d_ln_g", (1, 128), 1), ("d_ln_b", (1, 128), 1),
                  ("ffn_conv_w", (2, 3, 2 * D_FF // N_CHIPS), 2))
_SMALL_GRADS = (("norm1_g", (2, D_MODEL)), ("norm2_g", (2, D_MODEL)), ("b_mix_w", (4, 128, 128)), ("b_scale", (1, 512)),
                ("c_kv_norm_g", (1, 128)), ("d_w_s", (4, 128, 128)), ("d_b_s", (4, 128)), ("final_norm_g", (1, D_MODEL)),
                ("a_conv_w", (3, 512)), ("c_q_norm_g", (1, 256)), ("d_ln_g", (1, 512)), ("d_ln_b", (1, 512)),
                ("ffn_conv_w", (2, 3, 2 * D_FF)))


def _size(shape):
    n = 1
    for d in shape:
        n *= d
    return n


def kernel(x, c, positions, ada_w, ada_b, norm1_g, norm2_g, ab_w_in, a_conv_w, b_mix_w, b_scale, ab_w_out, cd_w_in, c_q_norm_g, c_w_uq, c_kv_norm_g, c_w_ukv, d_ln_g, d_ln_b, d_w_s, d_b_s, cd_w_out, ffn_w_up, ffn_conv_w, ffn_w_down, final_norm_g, loss_target, m_ada_w, m_ada_b, m_norm1_g, m_norm2_g, m_ab_w_in, m_a_conv_w, m_b_mix_w, m_b_scale, m_ab_w_out, m_cd_w_in, m_c_q_norm_g, m_c_w_uq, m_c_kv_norm_g, m_c_w_ukv, m_d_ln_g, m_d_ln_b, m_d_w_s, m_d_b_s, m_cd_w_out, m_ffn_w_up, m_ffn_conv_w, m_ffn_w_down, m_final_norm_g, v_ada_w, v_ada_b, v_norm1_g, v_norm2_g, v_ab_w_in, v_a_conv_w, v_b_mix_w, v_b_scale, v_ab_w_out, v_cd_w_in, v_c_q_norm_g, v_c_w_uq, v_c_kv_norm_g, v_c_w_ukv, v_d_ln_g, v_d_ln_b, v_d_w_s, v_d_b_s, v_cd_w_out, v_ffn_w_up, v_ffn_conv_w, v_ffn_w_down, v_final_norm_g):
    args = (x, c, positions, ada_w, ada_b, norm1_g, norm2_g, ab_w_in, a_conv_w, b_mix_w, b_scale, ab_w_out, cd_w_in, c_q_norm_g, c_w_uq, c_kv_norm_g, c_w_ukv, d_ln_g, d_ln_b, d_w_s, d_b_s, cd_w_out, ffn_w_up, ffn_conv_w, ffn_w_down, final_norm_g, loss_target, m_ada_w, m_ada_b, m_norm1_g, m_norm2_g, m_ab_w_in, m_a_conv_w, m_b_mix_w, m_b_scale, m_ab_w_out, m_cd_w_in, m_c_q_norm_g, m_c_w_uq, m_c_kv_norm_g, m_c_w_ukv, m_d_ln_g, m_d_ln_b, m_d_w_s, m_d_b_s, m_cd_w_out, m_ffn_w_up, m_ffn_conv_w, m_ffn_w_down, m_final_norm_g, v_ada_w, v_ada_b, v_norm1_g, v_norm2_g, v_ab_w_in, v_a_conv_w, v_b_mix_w, v_b_scale, v_ab_w_out, v_cd_w_in, v_c_q_norm_g, v_c_w_uq, v_c_kv_norm_g, v_c_w_ukv, v_d_ln_g, v_d_ln_b, v_d_w_s, v_d_b_s, v_cd_w_out, v_ffn_w_up, v_ffn_conv_w, v_ffn_w_down, v_final_norm_g)
    a = dict(zip(_INPUTS, args, strict=True))
    xi, yi, ci = _place()
    chip = 2 * xi + yi
    dev = 4 * xi + 2 * yi + ci
    x = a["x"][0]
    tgt = a["loss_target"][0]

    bf = lambda t: t.astype(BF16)
    mix0_handle, tok = start_gather([bf(a["ab_w_in"][0]), bf(a["ab_w_out"][0])], "mix0")
    up0_16, down0_16, up1_16, down1_16 = [bf(a[n][l]) for l in (0, 1) for n in ("ffn_w_up", "ffn_w_down")]
    mix1_16 = [bf(a[n][0]) for n in ("cd_w_in", "c_w_uq", "c_w_ukv", "cd_w_out")]

    small_parts = [a["c"] + tok] + [a[n] for n, _, _ in _SMALL_SHARDED]
    rows1 = -(-sum(p.size for p in small_parts) // LANES // 8) * 8
    g1 = all_gather8(_pack_rows(small_parts, rows1, F32), "gather_small",
                     [up0_16, down0_16, up1_16, down1_16, mix1_16[0], mix1_16[3]]).reshape(N_DEV, rows1 * LANES)
    c_all = g1[:, :D_MODEL]
    per_chip = g1[0::2]
    small_full = {}
    off = D_MODEL
    for n, shp, axis in _SMALL_SHARDED:
        piece = per_chip[:, off:off + _size(shp)].reshape((N_CHIPS,) + shp)
        small_full[n] = jnp.concatenate([piece[k] for k in range(N_CHIPS)], axis=axis)
        off += _size(shp)

    merge = lambda t: t.reshape(t.shape[0] * t.shape[1], t.shape[2])
    w = dict(norm1_g=a["norm1_g"], norm2_g=a["norm2_g"], b_mix_w=a["b_mix_w"][0], b_scale=a["b_scale"],
             c_kv_norm_g=a["c_kv_norm_g"], d_w_s=a["d_w_s"][0], d_b_s=a["d_b_s"][0],
             final_norm_g=a["final_norm_g"].reshape(1, D_MODEL), **small_full)

    ncol = N_MOD * D_MODEL // N_CHIPS
    ada_b_mine = lax.dynamic_slice_in_dim(a["ada_b"], chip * ncol, ncol, axis=1)
    mod_cols = ada_mod(c_all, a["ada_w"], ada_b_mine)
    g2_rows = all_gather8(mod_cols.reshape(-1, LANES), "gather_mod")
    g2 = g2_rows.reshape(N_DEV, 2, N_DEV, ncol)
    mod = lax.dynamic_index_in_dim(g2[0::2], dev, axis=2, keepdims=False)
    mod = mod.transpose(1, 0, 2).reshape(2, N_MOD * D_MODEL)

    late = [g2_rows]
    up0_handle, tok_a = start_gather([up0_16], "up0", late)
    down0_handle, tok_b = start_gather([down0_16], "down0", late)
    mix1_handle, tok_c = start_gather(mix1_16, "mix1", late)
    ffn1_handle, tok_d = start_gather([up1_16, down1_16], "ffn1", late)
    mod = mod + (tok_a + tok_b + tok_c + tok_d)

    ropes = rope_tables(a["positions"][0])
    cm16 = lambda t: chip_major(t).astype(BF16)
    w.update(ffn_w_up=[None, None], ffn_w_down=[None, None])
    handles = dict(mix0=mix0_handle, up0=up0_handle, down0=down0_handle, mix1=mix1_handle, ffn1=ffn1_handle)
    reducing, reduced = {}, {}

    class Hooks(StepHooks):
        def weights(self, stage, after):
            got = finish_gather(handles[stage], chip, stage, after)
            if stage == "mix0":
                w.update(ab_w_in=got[0], ab_w_out=merge(got[1]))
            elif stage == "up0":
                w["ffn_w_up"][0] = got[0]
            elif stage == "down0":
                w["ffn_w_down"][0] = merge(got[0])
            elif stage == "mix1":
                cd_in, uq, ukv, cd_out = got
                w.update(prepare_weights(dict(cd_w_in=from_chip_major(cd_in), c_w_uq=from_chip_major(uq),
                                              c_w_ukv=from_chip_major(ukv), cd_w_out=merge(cd_out))))
            else:
                w["ffn_w_up"][1], w["ffn_w_down"][1] = got[0], merge(got[1])

        def gradients(self, stage, grads, after):
            if stage in ("ffn0", "ffn1"):
                parts = [grads["ffn_w_up"], _rows_major(grads["ffn_w_down"])]
            elif stage == "mix1":
                grads.update(unprepare_grads(grads))
                parts = [cm16(grads["cd_w_in"]), cm16(grads["c_w_uq"]), cm16(grads["c_w_ukv"]),
                         _rows_major(grads["cd_w_out"]).astype(BF16)]
            else:
                parts = [grads["ab_w_in"], _rows_major(grads["ab_w_out"])]
            reducing[stage], tok = start_reduce(parts, ci, stage)
            before = {"mix1": "ffn1", "ffn0": "mix1", "mix0": "ffn0"}.get(stage)
            if before is not None:
                reduced[before] = finish_reduce(reducing[before], chip, ci, before, after)
            return tok

    loss, grad_x, dmod, by_stage = run_step(x, tgt, mod, ropes, w, Hooks())
    grads = merge_grads(by_stage)

    parts3 = [dmod] + [grads[n] for n, _ in _SMALL_GRADS] + [loss[0, 0]]
    rows3 = -(-sum(p.size for p in parts3) // LANES // 8) * 8
    small_handle, _ = split_start("small_grads_start", EVERYONE, [_pack_rows(parts3, rows3, F32)],
                                  [_sds((N_DEV, rows3, LANES))])
    red_up1, red_down1 = reduced["ffn1"]
    red_cd_in, red_uq, red_ukv, red_cd_out = reduced["mix1"]
    red_up0, red_down0 = reduced["ffn0"]
    out_grads = dict(cd_w_in=red_cd_in, c_w_uq=red_uq, c_w_ukv=red_ukv, cd_w_out=red_cd_out)
    per_layer = dict(ffn_w_up=(red_up0, red_up1), ffn_w_down=(red_down0, red_down1))
    updates = {}

    def update(n):
        if n in per_layer:
            updates[n] = adamw_layers(a[n], *per_layer[n], a["m_" + n], a["v_" + n], "adamw_" + n)
        else:
            updates[n] = adamw(a[n], out_grads[n].reshape(a[n].shape), a["m_" + n], a["v_" + n], "adamw_" + n)

    early =("ffn_w_up", "ffn_w_down", "cd_w_in", "c_w_uq", "c_w_ukv", "cd_w_out")
    for n in early:
        update(n)
    (mine,), (landed,) = split_wait("small_grads_wait", EVERYONE, small_handle, [updates[n][1] for n in early])
    g3 = lax.dynamic_update_index_in_dim(landed, mine, dev, 0)
    summed = sum8(g3).reshape(-1)
    nmod = 2 * N_MOD * D_MODEL
    out_grads["ada_b"] = summed[:nmod].reshape(2, N_MOD * D_MODEL)
    off = nmod
    for n, shp in _SMALL_GRADS:
        out_grads[n] = summed[off:off + _size(shp)].reshape(shp)
        off += _size(shp)
    loss = summed[off]
    for n, shp, axis in _SMALL_SHARDED:
        width = out_grads[n].shape[-1] // N_CHIPS
        out_grads[n] = lax.dynamic_slice_in_dim(out_grads[n], chip * width, width, axis=out_grads[n].ndim - 1)
    dmod_all = g3.reshape(N_DEV, rows3 * LANES)[:, :nmod].reshape(N_DEV, 2, N_MOD * D_MODEL)
    dmod_mine = lax.dynamic_slice_in_dim(dmod_all, chip * ncol, ncol, axis=2).transpose(1, 0, 2)
    updates["ada_w"] = adamw_ada(a["ada_w"], c_all, dmod_mine, a["m_ada_w"], a["v_ada_w"])

    red_in0, red_out0 = finish_reduce(reducing["mix0"], chip, ci, "mix0", updates["ada_w"][1])
    out_grads.update(ab_w_in=red_in0, ab_w_out=red_out0)

    for n in ("ab_w_in", "ab_w_out"):
        update(n)
    small = [n for n in _WEIGHTS if n not in updates]
    for n, res in zip(small, adamw_small([a[n] for n in small], [out_grads[n].reshape(a[n].shape) for n in small],
                                         [a["m_" + n] for n in small], [a["v_" + n] for n in small])):
        updates[n] = res
    return (loss, grad_x[None], *[updates[n][i] for i in range(4) for n in _WEIGHTS])
```

```python
import functools
from typing import NamedTuple

import jax
import jax.numpy as jnp
from jax import lax
from jax.experimental import pallas as pl
from jax.experimental.pallas import tpu as pltpu

F32 = jnp.float32
BF16 = jnp.bfloat16
EPS = 1e-6
D_MODEL = 1024
N_MOD = 6
A_WIDTH = 512
B_GROUPS = 4
C_HEADS = 8
C_NOPE = 64
C_ROPE = 32
C_V = 64
C_Q_RANK = 256
C_KV_RANK = 128
HEAD_PAD = 128
ROPE_THETA = 10000.0
D_GROUPS = 4
D_CHUNK = 128
D_FF = 2816
FF_UNIT = 128
ADAM_LR = 0.001
ADAM_B1 = 0.9
ADAM_B2 = 0.999
ADAM_EPS = 1e-08
ADAM_WD = 0.01
ADAM_STEP = 10
N_CHIPS = 4
N_DEV = 8
LANES = 128
VMEM_BIG = 56 * 1024 * 1024
MESH = pl.DeviceIdType.MESH


def _sds(shape, dtype=F32):
    return jax.ShapeDtypeStruct(tuple(shape), dtype)


def _tile(n, cap, mult=128):
    if n <= cap:
        return n
    best = None
    for t in range(mult, cap + 1, mult):
        if n % t == 0:
            best = t
    assert best is not None, (n, cap, mult)
    return best


def _params(dims=None, vmem=None):
    return pltpu.CompilerParams(dimension_semantics=dims, vmem_limit_bytes=vmem)


def _shift_down(v, k):
    r = pltpu.roll(v, k, axis=0)
    t = lax.broadcasted_iota(jnp.int32, v.shape, 0)
    return jnp.where(t >= k, r, 0.0)


def _shift_up(v, k):
    n = v.shape[0]
    r = pltpu.roll(v, n - k, axis=0)
    t = lax.broadcasted_iota(jnp.int32, v.shape, 0)
    return jnp.where(t < n - k, r, 0.0)


def _sigmoid(v):
    return 1.0 / (1.0 + jnp.exp(-v))


_GELU_C = 0.7978845608028654
_GELU_A = 0.044715


def _gelu(v):
    return 0.5 * v * (1.0 + jnp.tanh(_GELU_C * (v + _GELU_A * v * v * v)))


def _gelu_grad(v):
    th = jnp.tanh(_GELU_C * (v + _GELU_A * v * v * v))
    return 0.5 * (1.0 + th) + 0.5 * v * (1.0 - th * th) * _GELU_C * (1.0 + 3.0 * _GELU_A * v * v)


_NN = (((1,), (0,)), ((), ()))
_NT = (((1,), (1,)), ((), ()))
_TN = (((0,), (0,)), ((), ()))


def _dot(a, b, dims=_NN):
    return lax.dot_general(a, b, dims, preferred_element_type=F32)


def _logical(t, groups):
    return (t.shape[-2], t.shape[-1] * groups)


def _block(tr, tc, groups, cols, where):
    if groups == 1:
        return pl.BlockSpec((tr, tc), where)
    per = cols // groups // tc

    def index(i, j, s):
        r, c = where(i, j, s)
        return (c // per, r, c % per)

    return pl.BlockSpec((None, tr, tc), index)


def matmul(a, b, mode, out_dtype, name, ga=1, gb=1, go=1, tm=None, tn=None, tk=None):
    (ar, ac), (br, bc) = _logical(a, ga), _logical(b, gb)
    if mode == "nn":
        m, k, n = ar, ac, bc
        a_col, b_col = "k", "n"
    elif mode == "nt":
        m, k, n = ar, ac, br
        a_col, b_col = "k", "k"
    else:
        k, m, n = ar, ac, bc
        a_col, b_col = "m", "n"
    limit = {"m": m, "n": n // go, "k": k}
    limit[a_col] = min(limit[a_col], ac // ga)
    limit[b_col] = min(limit[b_col], bc // gb)
    tm = tm or _tile(limit["m"], 2048, 128 if mode == "tn" else 16)
    tn = tn or _tile(limit["n"], 512)
    tk = tk or _tile(limit["k"], 2048, 16 if mode == "tn" else 128)
    nk = k // tk
    if mode == "nn":
        a_spec = _block(tm, tk, ga, ac, lambda i, j, s: (i, s))
        b_spec = _block(tk, tn, gb, bc, lambda i, j, s: (s, j))
        dims = _NN
    elif mode == "nt":
        a_spec = _block(tm, tk, ga, ac, lambda i, j, s: (i, s))
        b_spec = _block(tn, tk, gb, bc, lambda i, j, s: (j, s))
        dims = _NT
    else:
        a_spec = _block(tk, tm, ga, ac, lambda i, j, s: (s, i))
        b_spec = _block(tk, tn, gb, bc, lambda i, j, s: (s, j))
        dims = _TN
    o_spec = _block(tm, tn, go, n, lambda i, j, s: (i, j))
    out_shape = _sds((m, n), out_dtype) if go == 1 else _sds((go, m, n // go), out_dtype)

    def body(a_ref, b_ref, o_ref, acc_ref):
        s = pl.program_id(2)

        @pl.when(s == 0)
        def _():
            acc_ref[...] = jnp.zeros_like(acc_ref)

        acc_ref[...] += _dot(a_ref[...], b_ref[...], dims)

        @pl.when(s == nk - 1)
        def _():
            o_ref[...] = acc_ref[...].astype(o_ref.dtype)

    return pl.pallas_call(
        body, name=name, out_shape=out_shape, grid=(m // tm, n // tn, nk),
        in_specs=[a_spec, b_spec], out_specs=o_spec,
        scratch_shapes=[pltpu.VMEM((tm, tn), F32)],
        compiler_params=_params(("parallel", "parallel", "arbitrary"), VMEM_BIG),
    )(a, b)


def _rows(tm, n):
    return pl.BlockSpec((tm, n), lambda i: (i, 0))


def _vec(n):
    return pl.BlockSpec((1, n), lambda i: (0, 0))


class Vec(NamedTuple):
    array: jax.Array
    row: int
    col: int


def _vec_in(v, d):
    return pl.BlockSpec((None, 1, d), lambda i: (v.row, 0, v.col))


def modnorm_fwd(x, g, sc, sh, name):
    s, d = x.shape
    tm = _tile(s, 256, 8)

    def body(x_ref, g_ref, sc_ref, sh_ref, o_ref):
        xv = x_ref[...]
        r = lax.rsqrt(jnp.mean(xv * xv, axis=-1, keepdims=True) + EPS)
        o_ref[...] = ((xv * r) * g_ref[...] * (1.0 + sc_ref[...]) + sh_ref[...]).astype(BF16)

    return pl.pallas_call(
        body, name=name, out_shape=_sds((s, d), BF16), grid=(s // tm,),
        in_specs=[_rows(tm, d), _vec_in(g, d), _vec_in(sc, d), _vec_in(sh, d)], out_specs=_rows(tm, d),
        compiler_params=_params(("parallel",)),
    )(x, g.array, sc.array, sh.array)


def norm_bwd(x, dh, g, sc, dres, name):
    s, d = x.shape
    tm = _tile(s, 256, 8)
    nsteps = s // tm

    def body(x_ref, dh_ref, g_ref, sc_ref, dr_ref, dx_ref, dsh_ref, dsc_ref, dg_ref, a2_ref):
        i = pl.program_id(0)

        @pl.when(i == 0)
        def _():
            dsh_ref[...] = jnp.zeros_like(dsh_ref)
            a2_ref[...] = jnp.zeros_like(a2_ref)

        xv = x_ref[...]
        dh = dh_ref[...].astype(F32)
        r = lax.rsqrt(jnp.mean(xv * xv, axis=-1, keepdims=True) + EPS)
        xh = xv * r
        dsh_ref[...] += jnp.sum(dh, axis=0, keepdims=True)
        a2_ref[...] += jnp.sum(dh * xh, axis=0, keepdims=True)
        dxh = dh * (g_ref[...] * (1.0 + sc_ref[...]))
        dx = r * (dxh - xh * jnp.mean(dxh * xh, axis=-1, keepdims=True))
        dx_ref[...] = dr_ref[...] + dx

        @pl.when(i == nsteps - 1)
        def _():
            dsc_ref[...] = a2_ref[...] * g_ref[...]
            dg_ref[...] = a2_ref[...] * (1.0 + sc_ref[...])

    return pl.pallas_call(
        body, name=name, out_shape=(_sds((s, d)), _sds((1, d)), _sds((1, d)), _sds((1, d))), grid=(nsteps,),
        in_specs=[_rows(tm, d), _rows(tm, d), _vec_in(g, d), _vec_in(sc, d), _rows(tm, d)],
        out_specs=(_rows(tm, d), _vec(d), _vec(d), _vec(d)),
        scratch_shapes=[pltpu.VMEM((1, d), F32)],
        compiler_params=_params(("arbitrary",)),
    )(x, dh, g.array, sc.array, dres)


_ROW_STREAM = 256


def _row_streams(s):
    tm = _tile(s, 2 * _ROW_STREAM, 8)
    sub = min(tm, _ROW_STREAM)
    return tm, [slice(r * sub, (r + 1) * sub) for r in range(tm // sub)]


def resid_modnorm_fwd(x, y, gate, g, sc, sh, name):
    s, d = x.shape
    tm, streams = _row_streams(s)

    def body(x_ref, y_ref, gate_ref, g_ref, sc_ref, sh_ref, xo_ref, h_ref):
        for rs in streams:
            xv = x_ref[rs, :] + gate_ref[...] * y_ref[rs, :].astype(F32)
            xo_ref[rs, :] = xv
            r = lax.rsqrt(jnp.mean(xv * xv, axis=-1, keepdims=True) + EPS)
            h_ref[rs, :] = ((xv * r) * g_ref[...] * (1.0 + sc_ref[...]) + sh_ref[...]).astype(BF16)

    return pl.pallas_call(
        body, name=name, out_shape=(_sds((s, d)), _sds((s, d), BF16)), grid=(s // tm,),
        in_specs=[_rows(tm, d), _rows(tm, d), _vec_in(gate, d), _vec_in(g, d), _vec_in(sc, d), _vec_in(sh, d)],
        out_specs=(_rows(tm, d), _rows(tm, d)),
        compiler_params=_params(("parallel",), VMEM_BIG),
    )(x, y, gate.array, g.array, sc.array, sh.array)


def norm_gate_bwd(x, dh, g, sc, dres, y, gate, name):
    s, d = x.shape
    tm, streams = _row_streams(s)
    nsteps = s // tm

    def body(x_ref, dh_ref, g_ref, sc_ref, dr_ref, y_ref, gate_ref, dx_ref, dsh_ref, dsc_ref, dg_ref, dy_ref,
             dgate_ref, a2_ref):
        i = pl.program_id(0)

        @pl.when(i == 0)
        def _():
            dsh_ref[...] = jnp.zeros_like(dsh_ref)
            a2_ref[...] = jnp.zeros_like(a2_ref)
            dgate_ref[...] = jnp.zeros_like(dgate_ref)

        for rs in streams:
            xv = x_ref[rs, :]
            dh = dh_ref[rs, :].astype(F32)
            r = lax.rsqrt(jnp.mean(xv * xv, axis=-1, keepdims=True) + EPS)
            xh = xv * r
            dsh_ref[...] += jnp.sum(dh, axis=0, keepdims=True)
            a2_ref[...] += jnp.sum(dh * xh, axis=0, keepdims=True)
            dxh = dh * (g_ref[...] * (1.0 + sc_ref[...]))
            dr = dr_ref[rs, :] + r * (dxh - xh * jnp.mean(dxh * xh, axis=-1, keepdims=True))
            dx_ref[rs, :] = dr
            dy_ref[rs, :] = (dr * gate_ref[...]).astype(BF16)
            dgate_ref[...] += jnp.sum(dr * y_ref[rs, :].astype(F32), axis=0, keepdims=True)

        @pl.when(i == nsteps - 1)
        def _():
            dsc_ref[...] = a2_ref[...] * g_ref[...]
            dg_ref[...] = a2_ref[...] * (1.0 + sc_ref[...])

    vec = _sds((1, d))
    return pl.pallas_call(
        body, name=name, out_shape=(_sds((s, d)), vec, vec, vec, _sds((s, d), BF16), vec), grid=(nsteps,),
        in_specs=[_rows(tm, d), _rows(tm, d), _vec_in(g, d), _vec_in(sc, d), _rows(tm, d), _rows(tm, d), _vec_in(gate, d)],
        out_specs=(_rows(tm, d), _vec(d), _vec(d), _vec(d), _rows(tm, d), _vec(d)),
        scratch_shapes=[pltpu.VMEM((1, d), F32)],
        compiler_params=_params(("arbitrary",), VMEM_BIG),
    )(x, dh, g.array, sc.array, dres, y, gate.array)


def final_fused(x, f, gate, g, tgt):
    s, d = x.shape
    tm = _tile(s, 256, 8)

    def body(x_ref, f_ref, gate_ref, g_ref, t_ref, dx_ref, dg_ref, loss_ref, df_ref, dgate_ref):
        @pl.when(pl.program_id(0) == 0)
        def _():
            dg_ref[...] = jnp.zeros_like(dg_ref)
            loss_ref[...] = jnp.zeros_like(loss_ref)
            dgate_ref[...] = jnp.zeros_like(dgate_ref)

        fv, gatev, gv = f_ref[...].astype(F32), gate_ref[...], g_ref[...]
        xv = x_ref[...] + gatev * fv
        r = lax.rsqrt(jnp.mean(xv * xv, axis=-1, keepdims=True) + EPS)
        xh = xv * r
        e = xh * gv - t_ref[...]
        row = jnp.sum(e * e, axis=-1, keepdims=True) * (0.5 / d)
        loss_ref[...] += jnp.sum(row, axis=0, keepdims=True)
        dy = e * (1.0 / d)
        dg_ref[...] += jnp.sum(dy * xh, axis=0, keepdims=True)
        dxh = dy * gv
        dx = r * (dxh - xh * jnp.mean(dxh * xh, axis=-1, keepdims=True))
        dx_ref[...] = dx
        df_ref[...] = (dx * gatev).astype(BF16)
        dgate_ref[...] += jnp.sum(dx * fv, axis=0, keepdims=True)

    vec = _sds((1, d))
    return pl.pallas_call(
        body, name="final_fused", out_shape=(_sds((s, d)), vec, _sds((1, LANES)), _sds((s, d), BF16), vec),
        grid=(s // tm,),
        in_specs=[_rows(tm, d), _rows(tm, d), _vec_in(gate, d), _vec_in(g, d), _rows(tm, d)],
        out_specs=(_rows(tm, d), _vec(d), _vec(LANES), _rows(tm, d), _vec(d)),
        compiler_params=_params(("arbitrary",)),
    )(x, f, gate.array, g.array, tgt)


def _taps(v):
    return _shift_down(v, 2), _shift_down(v, 1), v


def _conv3_taps(taps, w):
    return w[0:1, :] * taps[0] + w[1:2, :] * taps[1] + w[2:3, :] * taps[2]


def _conv3(v, w):
    return _conv3_taps(_taps(v), w)


def _conv3_t(dv, w):
    return w[0:1, :] * _shift_up(dv, 2) + w[1:2, :] * _shift_up(dv, 1) + w[2:3, :] * dv


def _conv3_dw_taps(dv, taps):
    return jnp.concatenate([jnp.sum(dv * t, axis=0, keepdims=True) for t in taps], axis=0)


def _conv3_dw(dv, v):
    return _conv3_dw_taps(dv, _taps(v))


def gconv_fwd(z, conv_w):
    s = z.shape[0]
    nb = A_WIDTH // LANES

    def body(b_ref, c_ref, a_ref, w_ref, o_ref):
        b, c, a = b_ref[...].astype(F32), c_ref[...].astype(F32), a_ref[...].astype(F32)
        o_ref[...] = (b * _conv3(c * a, w_ref[...])).astype(BF16)

    col = lambda off: pl.BlockSpec((s, LANES), lambda j: (0, off + j))
    return pl.pallas_call(
        body, name="gconv_fwd", out_shape=_sds((s, A_WIDTH + _B_WIDTH), BF16), grid=(nb,),
        in_specs=[col(0), col(nb), col(2 * nb), pl.BlockSpec((3, LANES), lambda j: (0, j))],
        out_specs=pl.BlockSpec((s, LANES), lambda j: (0, j)),
        compiler_params=_params(("parallel",), VMEM_BIG),
    )(z, z, z, conv_w)


def gconv_bwd(z, conv_w, dycat):
    s = z.shape[0]
    nb = A_WIDTH // LANES

    def body(b_ref, c_ref, a_ref, w_ref, dy_ref, db_ref, dc_ref, da_ref, dw_ref):
        c, a, w, dy = c_ref[...].astype(F32), a_ref[...].astype(F32), w_ref[...], dy_ref[...].astype(F32)
        ca = c * a
        db_ref[...] = (dy * _conv3(ca, w)).astype(BF16)
        dconv = dy * b_ref[...].astype(F32)
        dw_ref[...] = _conv3_dw(dconv, ca)
        dca = _conv3_t(dconv, w)
        dc_ref[...] = (dca * a).astype(BF16)
        da_ref[...] = (dca * c).astype(BF16)

    col = lambda off: pl.BlockSpec((s, LANES), lambda j: (0, off + j))
    wspec = pl.BlockSpec((3, LANES), lambda j: (0, j))
    part = _sds((s, A_WIDTH), BF16)
    return pl.pallas_call(
        body, name="gconv_bwd", out_shape=(part, part, part, _sds((3, A_WIDTH))), grid=(nb,),
        in_specs=[col(0), col(nb), col(2 * nb), wspec, col(0)],
        out_specs=(col(0), col(0), col(0), wspec),
        compiler_params=_params(("parallel",), VMEM_BIG),
    )(z, z, z, conv_w, dycat)


def _pool_counts(s, w):
    t = lax.broadcasted_iota(jnp.int32, (s, 1), 0)
    return jnp.minimum(t + 1, w).astype(F32)


def _pooled(p, levels):
    acc = p
    for lv in range(levels):
        acc = acc + _shift_down(acc, 2 ** lv)
    return acc / _pool_counts(p.shape[0], 2 ** levels) - p


_B_WIDTH = B_GROUPS * LANES


def pool_fwd(z, mix_w, scale, ycat):
    s = z.shape[0]

    def body(p_ref, m_ref, sc_ref, ycat_ref, o_ref):
        del ycat_ref
        for g in range(B_GROUPS):
            cols = slice(g * LANES, (g + 1) * LANES)
            pooled = _pooled(p_ref[:, cols].astype(F32), g + 1)
            y = _dot(pooled.astype(BF16), m_ref[g].astype(BF16))
            o_ref[:, cols] = (y * sc_ref[:, cols]).astype(BF16)

    return pl.pallas_call(
        body, name="pool_fwd", out_shape=_sds(ycat.shape, BF16), grid=(1,),
        in_specs=[pl.BlockSpec((s, _B_WIDTH), lambda i: (0, 3 * A_WIDTH // _B_WIDTH)),
                  pl.BlockSpec((B_GROUPS, LANES, LANES), lambda i: (0, 0, 0)), pl.BlockSpec((1, _B_WIDTH), lambda i: (0, 0)),
                  pl.BlockSpec(memory_space=pl.ANY)],
        out_specs=pl.BlockSpec((s, _B_WIDTH), lambda i: (0, A_WIDTH // _B_WIDTH)),
        input_output_aliases={3: 0},
        compiler_params=_params(("arbitrary",), VMEM_BIG),
    )(z, mix_w, scale, ycat)


def pool_bwd(z, mix_w, scale, dycat):
    s = z.shape[0]

    def body(p_ref, m_ref, sc_ref, dy_ref, dp_ref, dm_ref, dsc_ref):
        for g in range(B_GROUPS):
            cols = slice(g * LANES, (g + 1) * LANES)
            pooled = _pooled(p_ref[:, cols].astype(F32), g + 1)
            mw = m_ref[g].astype(BF16)
            pb = pooled.astype(BF16)
            dy = dy_ref[:, cols].astype(F32)
            dsc_ref[:, cols] = jnp.sum(dy * _dot(pb, mw), axis=0, keepdims=True)
            dmix = (dy * sc_ref[:, cols]).astype(BF16)
            dm_ref[g] = _dot(pb, dmix, _TN)
            dpool = _dot(dmix, mw, _NT)
            acc = dpool / _pool_counts(s, 2 ** (g + 1))
            for lv in range(g + 1):
                acc = acc + _shift_up(acc, 2 ** lv)
            dp_ref[:, cols] = (acc - dpool).astype(BF16)

    wide = lambda c: pl.BlockSpec((s, _B_WIDTH), lambda i: (0, c))
    mspec = pl.BlockSpec((B_GROUPS, LANES, LANES), lambda i: (0, 0, 0))
    vspec = pl.BlockSpec((1, _B_WIDTH), lambda i: (0, 0))
    return pl.pallas_call(
        body, name="pool_bwd", out_shape=(_sds((s, _B_WIDTH), BF16), _sds((B_GROUPS, LANES, LANES)), _sds((1, _B_WIDTH))),
        grid=(1,), in_specs=[wide(3 * A_WIDTH // _B_WIDTH), mspec, vspec, wide(A_WIDTH // _B_WIDTH)],
        out_specs=(wide(0), mspec, vspec),
        compiler_params=_params(("arbitrary",), VMEM_BIG),
    )(z, mix_w, scale, dycat)


_FF_BLOCKS = D_FF // FF_UNIT


def _ff_spec(s):
    return pl.BlockSpec((2, s, FF_UNIT), lambda j: (0, 0, j))


def _ff_wspecs():
    return [pl.BlockSpec((3, FF_UNIT), lambda j: (0, j)), pl.BlockSpec((3, FF_UNIT), lambda j: (0, _FF_BLOCKS + j))]


_FF_ROWS = 64
_FF_HALO = 16


def _chunk_taps(z_ref, half, c):
    start = pl.multiple_of(c * _FF_ROWS, _FF_ROWS)
    before = pl.multiple_of(jnp.maximum(c * _FF_ROWS - _FF_HALO, 0), _FF_HALO)
    halo = z_ref[half, pl.ds(before, _FF_HALO), :].astype(F32)
    halo = jnp.where(c > 0, halo, 0.0)
    win = jnp.concatenate([halo, z_ref[half, pl.ds(start, _FF_ROWS), :].astype(F32)], axis=0)
    return tuple(pltpu.roll(win, k, axis=0)[_FF_HALO:] for k in (2, 1)) + (win[_FF_HALO:],)


def _fold8(v):
    acc = v[0:8]
    for r in range(8, v.shape[0], 8):
        acc = acc + v[r:r + 8]
    return acc


_FF_CHUNK = 256


def ffn_act_down(zf, conv_w, w_down, name):
    s, d = zf.shape[1], w_down.shape[1]
    nk = D_FF // _FF_CHUNK
    chunk = lambda k: jnp.minimum(k, nk - 1)

    def body(z_ref, wg_ref, wu_ref, wd_ref, a_ref, f_ref, held_ref, acc_ref):
        k = pl.program_id(0)

        @pl.when(k == 0)
        def _():
            held_ref[...] = jnp.zeros_like(held_ref)
            acc_ref[...] = jnp.zeros_like(acc_ref)

        acc_ref[...] += _dot(held_ref[(k + 1) % 2], wd_ref[...])
        g = _conv3(z_ref[0].astype(F32), wg_ref[...])
        u = _conv3(z_ref[1].astype(F32), wu_ref[...])
        act = (g * _sigmoid(g) * u).astype(BF16)
        a_ref[...] = act
        held_ref[k % 2] = act

        @pl.when(k == nk)
        def _():
            f_ref[...] = acc_ref[...].astype(BF16)

    return pl.pallas_call(
        body, name=name, out_shape=(_sds((s, D_FF), BF16), _sds((s, d), BF16)), grid=(nk + 1,),
        in_specs=[pl.BlockSpec((2, s, _FF_CHUNK), lambda k: (0, 0, chunk(k))),
                  pl.BlockSpec((3, _FF_CHUNK), lambda k: (0, chunk(k))),
                  pl.BlockSpec((3, _FF_CHUNK), lambda k: (0, nk + chunk(k))),
                  pl.BlockSpec((_FF_CHUNK, d), lambda k: (jnp.maximum(k - 1, 0), 0))],
        out_specs=(pl.BlockSpec((s, _FF_CHUNK), lambda k: (0, chunk(k))), pl.BlockSpec((s, d), lambda k: (0, 0))),
        scratch_shapes=[pltpu.VMEM((2, s, _FF_CHUNK), BF16), pltpu.VMEM((s, d), F32)],
        compiler_params=_params(("arbitrary",), VMEM_BIG),
    )(zf, conv_w, conv_w, w_down)


def ffn_act_bwd(zf, conv_w, da, name):
    s = zf.shape[1]
    assert s % _FF_ROWS == 0
    nchunks = s // _FF_ROWS

    def body(z_ref, wg_ref, wu_ref, da_ref, dz_ref, dw_ref, dg_ref, du_ref):
        wg, wu = wg_ref[...], wu_ref[...]

        def first(c, acc):
            rows = pl.ds(pl.multiple_of(c * _FF_ROWS, _FF_ROWS), _FF_ROWS)
            tg, tu = _chunk_taps(z_ref, 0, c), _chunk_taps(z_ref, 1, c)
            g = _conv3_taps(tg, wg)
            u = _conv3_taps(tu, wu)
            dav = da_ref[rows, :].astype(F32)
            sg = _sigmoid(g)
            dg = dav * u * (sg * (1.0 + g * (1.0 - sg)))
            du = dav * (g * sg)
            dg_ref[rows, :] = dg
            du_ref[rows, :] = du
            return tuple(a + _fold8(d * t) for a, (d, t) in zip(acc, [(dg, t) for t in tg] + [(du, t) for t in tu]))

        zero = jnp.zeros((8, FF_UNIT), F32)
        acc = lax.fori_loop(0, nchunks, first, (zero,) * 6)
        sums = [jnp.sum(a, axis=0, keepdims=True) for a in acc]
        dw_ref[0] = jnp.concatenate(sums[:3], axis=0)
        dw_ref[1] = jnp.concatenate(sums[3:], axis=0)

        tail = pl.ds(s, _FF_HALO)
        dg_ref[tail, :] = jnp.zeros((_FF_HALO, FF_UNIT), F32)
        du_ref[tail, :] = jnp.zeros((_FF_HALO, FF_UNIT), F32)
        span = _FF_ROWS + _FF_HALO

        def second(c, carry):
            start = pl.multiple_of(c * _FF_ROWS, _FF_ROWS)
            for half, (d_ref, w) in enumerate(((dg_ref, wg), (du_ref, wu))):
                win = d_ref[pl.ds(start, span), :]
                dz = (w[0:1, :] * pltpu.roll(win, span - 2, axis=0)[:_FF_ROWS]
                      + w[1:2, :] * pltpu.roll(win, span - 1, axis=0)[:_FF_ROWS] + w[2:3, :] * win[:_FF_ROWS])
                dz_ref[half, pl.ds(start, _FF_ROWS), :] = dz.astype(BF16)
            return carry

        lax.fori_loop(0, nchunks, second, 0)

    return pl.pallas_call(
        body, name=name, out_shape=(_sds((2, s, D_FF), BF16), _sds((2, 3, D_FF))), grid=(_FF_BLOCKS,),
        in_specs=[_ff_spec(s)] + _ff_wspecs() + [pl.BlockSpec((s, FF_UNIT), lambda j: (0, j))],
        out_specs=(_ff_spec(s), pl.BlockSpec((2, 3, FF_UNIT), lambda j: (0, 0, j))),
        scratch_shapes=[pltpu.VMEM((s + _FF_HALO, FF_UNIT), F32), pltpu.VMEM((s + _FF_HALO, FF_UNIT), F32)],
        compiler_params=_params(("parallel",), VMEM_BIG),
    )(zf, conv_w, conv_w, da)


def _rope(v, cs, s1, s2):
    return v * cs + pltpu.roll(v, LANES - C_ROPE // 2, axis=1) * s1 + pltpu.roll(v, C_ROPE // 2, axis=1) * s2


def _rope_t(dv, cs, s1, s2):
    return dv * cs + pltpu.roll(dv * s1, C_ROPE // 2, axis=1) + pltpu.roll(dv * s2, LANES - C_ROPE // 2, axis=1)


def _kpe_mask(shape):
    lane = lax.broadcasted_iota(jnp.int32, shape, 1)
    return (lane >= C_NOPE) & (lane < C_NOPE + C_ROPE)


def _rms(v, g):
    r = lax.rsqrt(jnp.mean(v * v, axis=-1, keepdims=True) + EPS)
    return v * r, r


def _rms_bwd(dn, xh, r, g):
    dxh = dn * g
    return r * (dxh - xh * jnp.mean(dxh * xh, axis=-1, keepdims=True)), jnp.sum(dn * xh, axis=0, keepdims=True)


_ZQ = C_Q_RANK + C_KV_RANK + HEAD_PAD
_HW = C_HEADS * HEAD_PAD


_MLA_ROWS = 256


def _mla_tiles(s):
    tm = _tile(s, 2 * _MLA_ROWS, 8)
    sub = min(tm, _MLA_ROWS)
    return tm, [slice(r * sub, (r + 1) * sub) for r in range(tm // sub)]


def mla_pre_fwd(z, gq, gkv, wq, wk, wv, cs, s1, s2):
    s = z.shape[0]
    tm, streams = _mla_tiles(s)

    def body(z_ref, gq_ref, gkv_ref, wq_ref, wk_ref, wv_ref, cs_ref, s1_ref, s2_ref, q_ref, k_ref, v_ref):
        for rs in streams:
            zv = z_ref[rs, :].astype(F32)
            cst, s1t, s2t = cs_ref[rs, :], s1_ref[rs, :], s2_ref[rs, :]
            qh, _ = _rms(zv[:, :C_Q_RANK], None)
            qn = (qh * gq_ref[...]).astype(BF16)
            q = _dot(qn, wq_ref[...])
            kh, _ = _rms(zv[:, C_Q_RANK:C_Q_RANK + C_KV_RANK], None)
            kvn = (kh * gkv_ref[...]).astype(BF16)
            k = _dot(kvn, wk_ref[...])
            v_ref[rs, :] = _dot(kvn, wv_ref[...]).astype(BF16)
            kpe = _rope(zv[:, C_Q_RANK + C_KV_RANK:], cst, s1t, s2t)
            for h in range(C_HEADS):
                sl = slice(h * HEAD_PAD, (h + 1) * HEAD_PAD)
                q_ref[rs, sl] = _rope(q[:, sl], cst, s1t, s2t).astype(BF16)
                k_ref[rs, sl] = (k[:, sl] + kpe).astype(BF16)

    full = lambda r, c: pl.BlockSpec((r, c), lambda i: (0, 0))
    hw = _sds((s, _HW), BF16)
    return pl.pallas_call(
        body, name="mla_pre_fwd", out_shape=(hw, hw, hw), grid=(s // tm,),
        in_specs=[_rows(tm, _ZQ), _vec(C_Q_RANK), _vec(C_KV_RANK), full(C_Q_RANK, _HW), full(C_KV_RANK, _HW),
                  full(C_KV_RANK, _HW), _rows(tm, LANES), _rows(tm, LANES), _rows(tm, LANES)],
        out_specs=(_rows(tm, _HW), _rows(tm, _HW), _rows(tm, _HW)),
        compiler_params=_params(("parallel",), VMEM_BIG),
    )(z, gq, gkv, wq, wk, wv, cs, s1, s2)


def mla_pre_bwd(z, gq, gkv, wq, wk, wv, cs, s1, s2, dq, dk, dv):
    s = z.shape[0]
    tm, streams = _mla_tiles(s)

    def body(z_ref, gq_ref, gkv_ref, wq_ref, wk_ref, wv_ref, cs_ref, s1_ref, s2_ref, dq_ref, dk_ref, dv_ref,
             dz_ref, dwq_ref, dwk_ref, dwv_ref, dgq_ref, dgkv_ref):
        @pl.when(pl.program_id(0) == 0)
        def _():
            dwq_ref[...] = jnp.zeros_like(dwq_ref)
            dwk_ref[...] = jnp.zeros_like(dwk_ref)
            dwv_ref[...] = jnp.zeros_like(dwv_ref)
            dgq_ref[...] = jnp.zeros_like(dgq_ref)
            dgkv_ref[...] = jnp.zeros_like(dgkv_ref)

        gqv, gkvv = gq_ref[...], gkv_ref[...]
        for rs in streams:
            zv = z_ref[rs, :].astype(F32)
            cst, s1t, s2t = cs_ref[rs, :], s1_ref[rs, :], s2_ref[rs, :]
            qh, rq = _rms(zv[:, :C_Q_RANK], None)
            qn = (qh * gqv).astype(BF16)
            kh, rk = _rms(zv[:, C_Q_RANK:C_Q_RANK + C_KV_RANK], None)
            kvn = (kh * gkvv).astype(BF16)

            dqv = dq_ref[rs, :].astype(F32)
            dqp = jnp.concatenate(
                [_rope_t(dqv[:, h * HEAD_PAD:(h + 1) * HEAD_PAD], cst, s1t, s2t) for h in range(C_HEADS)], axis=1
            ).astype(BF16)
            dwq_ref[...] += _dot(qn, dqp, _TN)
            dqn = _dot(dqp, wq_ref[...], _NT)
            dql, dgq = _rms_bwd(dqn, qh, rq, gqv)
            dgq_ref[...] += dgq

            dkv = dk_ref[rs, :]
            dkb = dkv.astype(BF16)
            dvb = dv_ref[rs, :].astype(BF16)
            dwk_ref[...] += _dot(kvn, dkb, _TN)
            dwv_ref[...] += _dot(kvn, dvb, _TN)
            dkvn = _dot(dkb, wk_ref[...], _NT) + _dot(dvb, wv_ref[...], _NT)
            dkl, dgkv = _rms_bwd(dkvn, kh, rk, gkvv)
            dgkv_ref[...] += dgkv

            dkpe = dkv[:, :HEAD_PAD]
            for h in range(1, C_HEADS):
                dkpe = dkpe + dkv[:, h * HEAD_PAD:(h + 1) * HEAD_PAD]
            dkpe = _rope_t(jnp.where(_kpe_mask(dkpe.shape), dkpe, 0.0), cst, s1t, s2t)
            dz_ref[rs, :] = jnp.concatenate([dql, dkl, dkpe], axis=1).astype(BF16)

    full = lambda r, c: pl.BlockSpec((r, c), lambda i: (0, 0))
    return pl.pallas_call(
        body, name="mla_pre_bwd",
        out_shape=(_sds((s, _ZQ), BF16), _sds((C_Q_RANK, _HW)), _sds((C_KV_RANK, _HW)), _sds((C_KV_RANK, _HW)),
                   _sds((1, C_Q_RANK)), _sds((1, C_KV_RANK))),
        grid=(s // tm,),
        in_specs=[_rows(tm, _ZQ), _vec(C_Q_RANK), _vec(C_KV_RANK), full(C_Q_RANK, _HW), full(C_KV_RANK, _HW),
                  full(C_KV_RANK, _HW), _rows(tm, LANES), _rows(tm, LANES), _rows(tm, LANES),
                  _rows(tm, _HW), _rows(tm, _HW), _rows(tm, _HW)],
        out_specs=(_rows(tm, _ZQ), full(C_Q_RANK, _HW), full(C_KV_RANK, _HW), full(C_KV_RANK, _HW),
                   _vec(C_Q_RANK), _vec(C_KV_RANK)),
        compiler_params=_params(("arbitrary",), VMEM_BIG),
    )(z, gq, gkv, wq, wk, wv, cs, s1, s2, dq, dk, dv)


_ATT_SCALE = (C_NOPE + C_ROPE) ** -0.5
_NEG = -1e30


def _att_exp(q, k, row0, ends_here):
    sc = _dot(q, k, _NT) * _ATT_SCALE
    tq, nk = sc.shape
    if ends_here:
        last = sc[:, nk - tq:]
        row = lax.broadcasted_iota(jnp.int32, last.shape, 0)
        col = lax.broadcasted_iota(jnp.int32, last.shape, 1)
        last = jnp.where(col <= row, last, _NEG)
        sc = last if nk == tq else jnp.concatenate([sc[:, :nk - tq], last], axis=1)
    else:
        qpos = row0 + lax.broadcasted_iota(jnp.int32, sc.shape, 0)
        kpos = lax.broadcasted_iota(jnp.int32, sc.shape, 1)
        sc = jnp.where(kpos <= qpos, sc, _NEG)
    e = jnp.exp(sc - jnp.max(sc, axis=-1, keepdims=True))
    return e, 1.0 / jnp.sum(e, axis=-1, keepdims=True)


def _causal_cases(i, nq, tq, fn):
    if nq > 8:
        fn(nq * tq, False)
        return
    for blk in range(nq):
        pl.when(i == blk)(functools.partial(fn, (blk + 1) * tq, True))


_FWD_HEADS_PER_STEP = 4
_BWD_HEADS_PER_STEP = 2


def _head_lanes(heads):
    return [slice(h * HEAD_PAD, (h + 1) * HEAD_PAD) for h in range(heads)]


def attn_fwd(q, k, v):
    s = q.shape[0]
    tq = _tile(s, 256, 8)
    nq = s // tq
    heads = _FWD_HEADS_PER_STEP
    wide = heads * HEAD_PAD

    def body(q_ref, k_ref, v_ref, o_ref):
        i = pl.program_id(1)

        def case(nk, ends_here):
            for hd in _head_lanes(heads):
                e, inv = _att_exp(q_ref[:, hd], k_ref[:nk, hd], i * tq, ends_here)
                o_ref[:, hd] = (_dot(e.astype(BF16), v_ref[:nk, hd]) * inv).astype(BF16)

        _causal_cases(i, nq, tq, case)

    qspec = pl.BlockSpec((tq, wide), lambda h, i: (i, h))
    kspec = pl.BlockSpec((s, wide), lambda h, i: (0, h))
    return pl.pallas_call(
        body, name="attn_fwd", out_shape=_sds((s, _HW + _DW), BF16), grid=(C_HEADS // heads, s // tq),
        in_specs=[qspec, kspec, kspec], out_specs=qspec,
        compiler_params=_params(("parallel", "parallel"), VMEM_BIG),
    )(q, k, v)


def attn_bwd(q, k, v, o, do_all):
    s = q.shape[0]
    tq = _tile(s, 256, 8)
    heads = _BWD_HEADS_PER_STEP
    wide = heads * HEAD_PAD

    def body(q_ref, k_ref, v_ref, o_ref, do_ref, dq_ref, dk_ref, dv_ref):
        i = pl.program_id(1)

        @pl.when(i == 0)
        def _():
            dk_ref[...] = jnp.zeros_like(dk_ref)
            dv_ref[...] = jnp.zeros_like(dv_ref)

        def case(nk, ends_here):
            for hd in _head_lanes(heads):
                qv, kv, vv, dov = q_ref[:, hd], k_ref[:nk, hd], v_ref[:nk, hd], do_ref[:, hd]
                e, inv = _att_exp(qv, kv, i * tq, ends_here)
                p = e * inv
                dp = _dot(dov, vv, _NT)
                delta = jnp.sum(dov.astype(F32) * o_ref[:, hd].astype(F32), axis=-1, keepdims=True)
                ds = (p * (dp - delta) * _ATT_SCALE).astype(BF16)
                dq_ref[:, hd] = _dot(ds, kv).astype(BF16)
                dk_ref[:nk, hd] += _dot(ds, qv, _TN)
                dv_ref[:nk, hd] += _dot(p.astype(BF16), dov, _TN)

        _causal_cases(i, s // tq, tq, case)

    qspec = pl.BlockSpec((tq, wide), lambda h, i: (i, h))
    kspec = pl.BlockSpec((s, wide), lambda h, i: (0, h))
    return pl.pallas_call(
        body, name="attn_bwd", out_shape=(_sds((s, _HW), BF16), _sds((s, _HW)), _sds((s, _HW))),
        grid=(C_HEADS // heads, s // tq),
        in_specs=[qspec, kspec, kspec, qspec, qspec], out_specs=(qspec, kspec, kspec),
        compiler_params=_params(("parallel", "arbitrary"), VMEM_BIG),
    )(q, k, v, o, do_all)


_DW = D_GROUPS * LANES


def _tril_bf16(w):
    r = lax.broadcasted_iota(jnp.int32, w.shape, 0)
    c = lax.broadcasted_iota(jnp.int32, w.shape, 1)
    return jnp.where(c <= r, w, 0.0).astype(BF16)


def _sgu_forward(zu, zv, lg, lb, ws_ref, bs):
    u = _gelu(zu)
    v = _gelu(zv)
    mu = jnp.mean(v, axis=-1, keepdims=True)
    vc = v - mu
    rstd = lax.rsqrt(jnp.mean(vc * vc, axis=-1, keepdims=True) + EPS)
    xh = vc * rstd
    vln = (xh * lg + lb).astype(BF16)
    mixed = []
    for g in range(D_GROUPS):
        wg = _tril_bf16(ws_ref[g])
        mixed.append(_dot(wg, vln[:, g * LANES:(g + 1) * LANES]) + bs[:, g:g + 1])
    return u, xh, rstd, vln, jnp.concatenate(mixed, axis=1)


_SGU_CHUNKS = 4


def sgu_fwd(z, lg, lb, ws, bs_t, ycat):
    s = z.shape[0]
    rows = _SGU_CHUNKS * D_CHUNK

    def body(zu_ref, zv_ref, lg_ref, lb_ref, ws_ref, bs_ref, ycat_ref, o_ref):
        del ycat_ref
        for c in range(_SGU_CHUNKS):
            rs = slice(c * D_CHUNK, (c + 1) * D_CHUNK)
            u, _, _, _, mixed = _sgu_forward(zu_ref[rs, :].astype(F32), zv_ref[rs, :].astype(F32), lg_ref[...],
                                             lb_ref[...], ws_ref, bs_ref[...])
            o_ref[rs, :] = (u * mixed).astype(BF16)

    return pl.pallas_call(
        body, name="sgu_fwd", out_shape=_sds(ycat.shape, BF16), grid=(s // rows,),
        in_specs=[pl.BlockSpec((rows, _DW), lambda n: (n, 1)), pl.BlockSpec((rows, _DW), lambda n: (n, 2)),
                  _vec(_DW), _vec(_DW), pl.BlockSpec((D_GROUPS, D_CHUNK, D_CHUNK), lambda n: (0, 0, 0)),
                  pl.BlockSpec((D_CHUNK, LANES), lambda n: (0, 0)), pl.BlockSpec(memory_space=pl.ANY)],
        out_specs=pl.BlockSpec((rows, _DW), lambda n: (n, _HW // _DW)),
        input_output_aliases={6: 0},
        compiler_params=_params(("parallel",)),
    )(z, z, lg, lb, ws, bs_t, ycat)


def sgu_bwd(z, lg, lb, ws, bs_t, dycat, dy_col):
    s = z.shape[0]
    rows = _SGU_CHUNKS * D_CHUNK

    def body(zu_ref, zv_ref, lg_ref, lb_ref, ws_ref, bs_ref, dy_ref, dzu_ref, dzv_ref, dws_ref, dbs_ref, dlg_ref,
             dlb_ref):
        @pl.when(pl.program_id(0) == 0)
        def _():
            dws_ref[...] = jnp.zeros_like(dws_ref)
            dbs_ref[...] = jnp.zeros_like(dbs_ref)
            dlg_ref[...] = jnp.zeros_like(dlg_ref)
            dlb_ref[...] = jnp.zeros_like(dlb_ref)

        lg = lg_ref[...]
        lane = lax.broadcasted_iota(jnp.int32, (D_CHUNK, LANES), 1)
        row = lax.broadcasted_iota(jnp.int32, (D_CHUNK, D_CHUNK), 0)
        colm = lax.broadcasted_iota(jnp.int32, (D_CHUNK, D_CHUNK), 1)
        for c in range(_SGU_CHUNKS):
            rs = slice(c * D_CHUNK, (c + 1) * D_CHUNK)
            zu, zv = zu_ref[rs, :].astype(F32), zv_ref[rs, :].astype(F32)
            u, xh, rstd, vln, mixed = _sgu_forward(zu, zv, lg, lb_ref[...], ws_ref, bs_ref[...])
            dy = dy_ref[rs, :].astype(F32)
            dzu_ref[rs, :] = (dy * mixed * _gelu_grad(zu)).astype(BF16)
            dmix = dy * u
            dvln = []
            dbs = jnp.zeros((D_CHUNK, LANES), F32)
            for g in range(D_GROUPS):
                sl = slice(g * LANES, (g + 1) * LANES)
                dmg = dmix[:, sl]
                dbs = dbs + jnp.where(lane == g, jnp.sum(dmg, axis=-1, keepdims=True), 0.0)
                dmb = dmg.astype(BF16)
                dws_ref[g] += jnp.where(colm <= row, _dot(dmb, vln[:, sl], _NT), 0.0)
                dvln.append(_dot(_tril_bf16(ws_ref[g]), dmb, _TN))
            dbs_ref[...] += dbs
            dvln = jnp.concatenate(dvln, axis=1)
            dlg_ref[...] += jnp.sum(dvln * xh, axis=0, keepdims=True)
            dlb_ref[...] += jnp.sum(dvln, axis=0, keepdims=True)
            dxh = dvln * lg
            dvv = rstd * (dxh - jnp.mean(dxh, axis=-1, keepdims=True)
                          - xh * jnp.mean(dxh * xh, axis=-1, keepdims=True))
            dzv_ref[rs, :] = (dvv * _gelu_grad(zv)).astype(BF16)

    wsspec = pl.BlockSpec((D_GROUPS, D_CHUNK, D_CHUNK), lambda n: (0, 0, 0))
    chunk = lambda cidx: pl.BlockSpec((rows, _DW), lambda n: (n, cidx))
    return pl.pallas_call(
        body, name="sgu_bwd",
        out_shape=(_sds((s, _DW), BF16), _sds((s, _DW), BF16), _sds((D_GROUPS, D_CHUNK, D_CHUNK)),
                   _sds((D_CHUNK, LANES)), _sds((1, _DW)), _sds((1, _DW))),
        grid=(s // rows,),
        in_specs=[chunk(1), chunk(2), _vec(_DW), _vec(_DW), wsspec, pl.BlockSpec((D_CHUNK, LANES), lambda n: (0, 0)),
                  chunk(dy_col)],
        out_specs=(chunk(0), chunk(0), wsspec, pl.BlockSpec((D_CHUNK, LANES), lambda n: (0, 0)), _vec(_DW), _vec(_DW)),
        compiler_params=_params(("arbitrary",)),
    )(z, z, lg, lb, ws, bs_t, dycat)


def ada_mod(c_all, ada_w, ada_b):
    nl, d, n = ada_w.shape
    nb = c_all.shape[0]
    tn = _tile(n, 512)

    def body(c_ref, w_ref, b_ref, o_ref):
        cv = c_ref[...]
        ca = (cv * _sigmoid(cv)).astype(BF16)
        o_ref[...] = _dot(ca, w_ref[...].astype(BF16)) + b_ref[...]

    return pl.pallas_call(
        body, name="ada_mod", out_shape=_sds((nl, nb, n)), grid=(nl, n // tn),
        in_specs=[pl.BlockSpec((nb, d), lambda l, j: (0, 0)), pl.BlockSpec((None, d, tn), lambda l, j: (l, 0, j)),
                  pl.BlockSpec((None, 1, tn), lambda l, j: (l, 0, j))],
        out_specs=pl.BlockSpec((None, nb, tn), lambda l, j: (l, 0, j)),
        compiler_params=_params(("parallel", "parallel")),
    )(c_all, ada_w, ada_b.reshape(nl, 1, n))


_ADAM_BLOCK = 512 * 1024


def _adam_rows(rows, cols):
    if rows * cols <= _ADAM_BLOCK or rows % 8:
        return rows
    return _tile(rows, max(8, _ADAM_BLOCK // cols), 8)


def _adam_update(w, gv, m, v):
    inv_bc1 = 1.0 / (1.0 - ADAM_B1 ** ADAM_STEP)
    inv_bc2 = 1.0 / (1.0 - ADAM_B2 ** ADAM_STEP)
    nm = ADAM_B1 * m + (1.0 - ADAM_B1) * gv
    nv = ADAM_B2 * v + (1.0 - ADAM_B2) * (gv * gv)
    return -ADAM_LR * ((nm * inv_bc1) / (jnp.sqrt(nv * inv_bc2) + ADAM_EPS) + ADAM_WD * w), nm, nv


def adamw(w, g, m, v, name):
    shape = w.shape
    cols = shape[-1]
    rows = w.size // cols
    tr = _adam_rows(rows, cols)

    def body(w_ref, g_ref, m_ref, v_ref, go_ref, d_ref, nm_ref, nv_ref):
        gv = g_ref[...]
        go_ref[...] = gv
        d_ref[...], nm_ref[...], nv_ref[...] = _adam_update(w_ref[...], gv, m_ref[...], v_ref[...])

    spec = pl.BlockSpec((tr, cols), lambda i: (i, 0))
    out = _sds((rows, cols))
    r2 = lambda t: t.reshape(rows, cols)
    res = pl.pallas_call(
        body, name=name, out_shape=(out,) * 4, grid=(rows // tr,),
        in_specs=[spec] * 4, out_specs=(spec,) * 4, compiler_params=_params(("parallel",), VMEM_BIG),
    )(r2(w), r2(g), r2(m), r2(v))
    return tuple(t.reshape(shape) for t in res)


def adamw_ada(w, c_all, dmod, m, v):
    nl, d, n = w.shape
    tr = _adam_rows(d, n)
    pad = 16 - c_all.shape[0]
    c16 = jnp.pad(c_all, ((0, pad), (0, 0)))
    dm16 = jnp.pad(dmod, ((0, 0), (0, pad), (0, 0)))

    def body(w_ref, c_ref, dm_ref, m_ref, v_ref, g_ref, d_ref, nm_ref, nv_ref):
        cv = c_ref[...]
        gv = _dot((cv * _sigmoid(cv)).astype(BF16), dm_ref[...].astype(BF16), _TN)
        g_ref[...] = gv
        d_ref[...], nm_ref[...], nv_ref[...] = _adam_update(w_ref[...], gv, m_ref[...], v_ref[...])

    spec = pl.BlockSpec((None, tr, n), lambda l, i: (l, i, 0))
    out = _sds((nl, d, n))
    return pl.pallas_call(
        body, name="adamw_ada_w", out_shape=(out, out, out, out), grid=(nl, d // tr),
        in_specs=[spec, pl.BlockSpec((16, tr), lambda l, i: (0, i)), pl.BlockSpec((None, 16, n), lambda l, i: (l, 0, 0)),
                  spec, spec],
        out_specs=(spec,) * 4, compiler_params=_params(("parallel", "parallel"), VMEM_BIG),
    )(w, c16, dm16, m, v)


def adamw_small(ws, gs, ms, vs):
    n = len(ws)
    flat = lambda t: t.reshape(-1, t.shape[-1])

    def body(*refs):
        ins, outs = refs[:4 * n], refs[4 * n:]
        for i in range(n):
            w_ref, g_ref, m_ref, v_ref = ins[4 * i:4 * i + 4]
            outs[3 * i][...], outs[3 * i + 1][...], outs[3 * i + 2][...] = _adam_update(
                w_ref[...], g_ref[...], m_ref[...], v_ref[...])

    operands = [flat(t) for quad in zip(ws, gs, ms, vs) for t in quad]
    res = pl.pallas_call(
        body, name="adamw_small", out_shape=tuple(_sds(flat(w).shape) for w in ws for _ in range(3)),
    )(*operands)
    return [(g, res[3 * i].reshape(w.shape), res[3 * i + 1].reshape(w.shape), res[3 * i + 2].reshape(w.shape))
            for i, (w, g) in enumerate(zip(ws, gs))]


def adamw_layers(w, g0, g1, m, v, name):
    _, rows, cols = w.shape
    tr = _adam_rows(rows, cols)

    def body(w_ref, g0_ref, g1_ref, m_ref, v_ref, g_ref, d_ref, nm_ref, nv_ref):
        gv = jnp.where(pl.program_id(0) == 0, g0_ref[...], g1_ref[...])
        g_ref[...] = gv
        d_ref[...], nm_ref[...], nv_ref[...] = _adam_update(w_ref[...], gv, m_ref[...], v_ref[...])

    spec = pl.BlockSpec((None, tr, cols), lambda l, i: (l, i, 0))
    gspec = pl.BlockSpec((tr, cols), lambda l, i: (i, 0))
    out = _sds((2, rows, cols))
    return pl.pallas_call(
        body, name=name, out_shape=(out, out, out, out), grid=(2, rows // tr),
        in_specs=[spec, gspec, gspec, spec, spec], out_specs=(spec,) * 4,
        compiler_params=_params(("parallel", "parallel"), VMEM_BIG),
    )(w, g0, g1, m, v)


def sum8(gathered):
    _, r, _ = gathered.shape
    tr = _tile(r, 512, 8)

    def body(g_ref, o_ref):
        acc = g_ref[0]
        for dev in range(1, N_DEV):
            acc = acc + g_ref[dev]
        o_ref[...] = acc

    return pl.pallas_call(
        body, name="sum8", out_shape=_sds((r, LANES)), grid=(r // tr,),
        in_specs=[pl.BlockSpec((N_DEV, tr, LANES), lambda i: (0, i, 0))], out_specs=pl.BlockSpec((tr, LANES), lambda i: (i, 0)),
        compiler_params=_params(("parallel",)),
    )(gathered)


_SUM_STEPS = 2


def pair_sums(gs, recvs, core, name):
    n = len(gs)

    def body(c_ref, *refs):
        del c_ref
        for i in range(n):
            a_ref, b_ref, o_ref = refs[2 * i], refs[2 * i + 1], refs[2 * n + i]
            o_ref[...] = (a_ref[...].astype(F32) + b_ref[...].astype(F32)).astype(BF16)

    in_specs, out_specs = [], []
    for g in gs:
        half = (None, g.shape[1] // 2, g.shape[2])
        in_specs.append(pl.BlockSpec(half, lambda k, c: (k, c[0], 0)))
        in_specs.append(pl.BlockSpec(half, lambda k, c: (k, 0, 0)))
        out_specs.append(pl.BlockSpec(half, lambda k, c: (k, 0, 0)))
    grid_spec = pltpu.PrefetchScalarGridSpec(num_scalar_prefetch=1, grid=(N_CHIPS,), in_specs=in_specs,
                                             out_specs=tuple(out_specs))
    return list(pl.pallas_call(
        body, name=name, out_shape=tuple(_sds((N_CHIPS, g.shape[1] // 2, g.shape[2]), BF16) for g in gs),
        grid_spec=grid_spec, compiler_params=_params(("parallel",), VMEM_BIG),
    )(core.reshape(1).astype(jnp.int32), *[t for pair in zip(gs, recvs) for t in pair]))


def chip_sums(pairs, recvs, chip, core, name):
    n = len(pairs)
    trs = [p.shape[1] // _SUM_STEPS for p in pairs]

    def body(p_ref, *refs):
        del p_ref
        for i in range(n):
            own_ref, r_ref, o_ref = refs[2 * i], refs[2 * i + 1], refs[2 * n + i]
            acc = own_ref[...].astype(F32)
            for j in range(N_CHIPS - 1):
                acc = acc + r_ref[j].astype(F32)
            o_ref[...] = acc

    in_specs, out_specs = [], []
    for p, tr in zip(pairs, trs):
        cols = p.shape[2]
        in_specs.append(pl.BlockSpec((None, tr, cols), lambda s, q: (q[0], s, 0)))
        in_specs.append(pl.BlockSpec((N_CHIPS - 1, tr, cols), lambda s, q: (0, s, 0)))
        out_specs.append(pl.BlockSpec((None, tr, cols), lambda s, q: (q[1], s, 0)))
    grid_spec = pltpu.PrefetchScalarGridSpec(num_scalar_prefetch=1, grid=(_SUM_STEPS,), in_specs=in_specs,
                                             out_specs=tuple(out_specs))
    return list(pl.pallas_call(
        body, name=name, out_shape=tuple(_sds((2,) + p.shape[1:]) for p in pairs), grid_spec=grid_spec,
        compiler_params=_params(("parallel",)),
    )(jnp.stack([chip, core]).astype(jnp.int32), *[t for pair in zip(pairs, recvs) for t in pair]))


def _place():
    return lax.axis_index("x"), lax.axis_index("y"), lax.axis_index("c")


def _other_chips(x, y):
    return [(x, 1 - y), (1 - x, y), (1 - x, 1 - y)]


_HBM = pl.BlockSpec(memory_space=pltpu.HBM)


def all_gather8(v, name, after=()):
    m, n = v.shape

    def body(x_ref, *refs):
        out_ref, send_sems, recv_sems, local_sem = refs[len(after):]
        x, y, c = _place()
        me, sibling = (x, y, c), (x, y, 1 - c)
        chips = _other_chips(x, y)

        def rows(px, py, pc):
            return out_ref.at[pl.ds((4 * px + 2 * py + pc) * m, m), :]

        def copy(k, block, to, src=None):
            return pltpu.make_async_remote_copy(
                src_ref=rows(*block) if src is None else src, dst_ref=rows(*block),
                send_sem=send_sems.at[k], recv_sem=recv_sems.at[k], device_id=to, device_id_type=MESH)

        mine = pltpu.make_async_copy(x_ref, rows(*me), local_sem)
        mine.start()
        first = [copy(0, me, sibling, src=x_ref)]
        first += [copy(1 + j, me, (*chip, c), src=x_ref) for j, chip in enumerate(chips)]
        for cp in first:
            cp.start()
        passed = [copy(4 + j, (*chip, c), sibling) for j, chip in enumerate(chips)]
        for j, chip in enumerate(chips):
            copy(1 + j, (*chip, c), me).wait_recv()
            passed[j].start()
        copy(0, sibling, me).wait_recv()
        for j, chip in enumerate(chips):
            copy(4 + j, (*chip, 1 - c), me).wait_recv()
        for cp in first + passed:
            cp.wait_send()
        mine.wait()

    return pl.pallas_call(
        body, name=name, out_shape=_sds((N_DEV * m, n), v.dtype),
        in_specs=[pl.BlockSpec(memory_space=pltpu.VMEM)] + [pl.BlockSpec(memory_space=pl.ANY)] * len(after),
        out_specs=pl.BlockSpec(memory_space=pltpu.VMEM),
        scratch_shapes=[pltpu.SemaphoreType.DMA((7,)), pltpu.SemaphoreType.DMA((7,)), pltpu.SemaphoreType.DMA],
        compiler_params=_params(None, VMEM_BIG),
    )(v, *after)


def _comm_call(body, name, ins, out_shapes, nsem, aliases=None):
    return pl.pallas_call(
        body, name=name, out_shape=tuple(out_shapes), in_specs=[_HBM] * len(ins), out_specs=tuple([_HBM] * len(out_shapes)),
        scratch_shapes=[pltpu.SemaphoreType.DMA((nsem,)), pltpu.SemaphoreType.DMA((nsem,))],
        input_output_aliases=aliases or {},
    )(*ins)


def _remote(src, dst, send_sems, recv_sems, k, to):
    return pltpu.make_async_remote_copy(src_ref=src, dst_ref=dst, send_sem=send_sems.at[k], recv_sem=recv_sems.at[k],
                                        device_id=to, device_id_type=MESH)


def _half(core, rh):
    return pl.ds(pl.multiple_of(core * rh, 16), rh)


def swap_halves(gs, name):
    n = len(gs)

    def body(*refs):
        ins, outs, (send_sems, recv_sems) = refs[:n], refs[n:2 * n], refs[2 * n:]
        x, y, c = _place()
        copies = []
        for i in range(n):
            theirs = _half(1 - c, ins[i].shape[1] // 2)
            cp = _remote(ins[i].at[:, theirs], outs[i], send_sems, recv_sems, i, (x, y, 1 - c))
            cp.start()
            copies.append(cp)
        for cp in copies:
            cp.wait()

    return _comm_call(body, name, gs, [_sds((g.shape[0], g.shape[1] // 2, g.shape[2]), g.dtype) for g in gs], n)


def join_halves(bufs, name):
    n = len(bufs)

    def body(*refs):
        ins, outs, (send_sems, recv_sems) = refs[:n], refs[n:2 * n], refs[2 * n:]
        x, y, c = _place()
        copies = []
        for i in range(n):
            cp = _remote(ins[i].at[c], outs[i].at[c], send_sems, recv_sems, i, (x, y, 1 - c))
            cp.start()
            copies.append(cp)
        for i in range(n):
            theirs = outs[i].at[1 - c]
            _remote(theirs, theirs, send_sems, recv_sems, i, (x, y, 1 - c)).wait_recv()
        for cp in copies:
            cp.wait_send()

    return _comm_call(body, name, bufs, [_sds(b.shape, b.dtype) for b in bufs], n, {i: i for i in range(n)})


def forward_halves(lands, name):
    n = len(lands)

    def body(*refs):
        ins, outs, (send_sems, recv_sems) = refs[:n], refs[n:2 * n], refs[2 * n:]
        x, y, c = _place()
        sibling = (x, y, 1 - c)
        chips = _other_chips(x, y)
        copies = []
        for i in range(n):
            mine = _half(c, ins[i].shape[1] // 2)
            for j, (px, py) in enumerate(chips):
                cp = _remote(ins[i].at[2 * px + py, mine], outs[i].at[2 * px + py, mine], send_sems, recv_sems, 3 * i + j, sibling)
                cp.start()
                copies.append(cp)
        for i in range(n):
            theirs = _half(1 - c, ins[i].shape[1] // 2)
            for j, (px, py) in enumerate(chips):
                landed = outs[i].at[2 * px + py, theirs]
                _remote(landed, landed, send_sems, recv_sems, 3 * i + j, sibling).wait_recv()
        for cp in copies:
            cp.wait_send()

    return _comm_call(body, name, lands, [_sds(b.shape, b.dtype) for b in lands], 3 * n, {i: i for i in range(n)})


_SEM = pl.BlockSpec(memory_space=pltpu.SEMAPHORE)
_EFFECT = pltpu.SideEffectType.DATAFLOW_SIDE_EFFECTING


def _gather_copies(srcs, lands, send_sems, recv_sems):
    x, y, c = _place()
    copies = []
    for i in range(len(srcs)):
        mine = _half(c, srcs[i].shape[0] // 2)
        for j, chip in enumerate(_other_chips(x, y)):
            copies.append(_remote(srcs[i].at[mine], lands[i].at[2 * x + y, mine], send_sems, recv_sems, 3 * i + j, (*chip, c)))
    return copies


def _exchange_copies(srcs, lands, send_sems, recv_sems):
    x, y, c = _place()
    copies = []
    for i in range(len(srcs)):
        for j, (px, py) in enumerate(_other_chips(x, y)):
            copies.append(_remote(srcs[i].at[2 * px + py], lands[i].at[j], send_sems, recv_sems, 3 * i + j, (px, py, c)))
    return copies


def _everyone_copies(srcs, lands, send_sems, recv_sems):
    x, y, c = _place()
    flip = lambda v, b: 1 - v if b else v
    dst = lands[0].at[4 * x + 2 * y + c]
    return [_remote(srcs[0], dst, send_sems, recv_sems, j - 1, (flip(x, j & 4), flip(y, j & 2), flip(c, j & 1)))
            for j in range(1, N_DEV)]


GATHER = (_gather_copies, 3)
EXCHANGE = (_exchange_copies, 3)
EVERYONE = (_everyone_copies, N_DEV - 1)


def split_start(name, plan, srcs, land_shapes, after=()):
    copies_fn, per_source = plan
    n, m, k = len(srcs), len(land_shapes), len(after)
    ncopies = per_source * n

    def body(*refs):
        src_refs, land_refs = refs[:n], refs[n:n + m]
        send_sems, recv_sems = refs[n + m + k], refs[n + m + k + 1]
        token = refs[-1]
        for cp in copies_fn(src_refs, land_refs, send_sems, recv_sems):
            cp.start()
        token[...] = jnp.zeros_like(token)

    hbm = lambda s: pltpu.HBM(tuple(s.shape), s.dtype)
    outs = pl.pallas_call(
        body, name=name,
        out_shape=(pltpu.SemaphoreType.DMA((ncopies,)), pltpu.SemaphoreType.DMA((ncopies,)), *[hbm(s) for s in srcs],
                   *[hbm(s) for s in land_shapes], _sds((8, LANES))),
        in_specs=[_HBM] * (n + m) + [pl.BlockSpec(memory_space=pl.ANY)] * k,
        out_specs=(_SEM, _SEM, *([_HBM] * (n + m)), pl.BlockSpec(memory_space=pltpu.VMEM)),
        input_output_aliases={i: 2 + i for i in range(n + m)},
        compiler_params=pltpu.CompilerParams(has_side_effects=_EFFECT),
    )(*[pltpu.with_memory_space_constraint(s, pltpu.HBM) for s in srcs],
      *[pltpu.with_memory_space_constraint(lax.empty(tuple(s.shape), s.dtype), pltpu.HBM) for s in land_shapes], *after)
    handle = (outs[0], outs[1], list(outs[2:2 + n]), list(outs[2 + n:2 + n + m]))
    return handle, outs[-1][0, 0]


def split_wait(name, plan, handle, after):
    copies_fn, _ = plan
    send_sems, recv_sems, srcs, lands = handle
    n, m = len(srcs), len(lands)
    after = list(after) if isinstance(after, (list, tuple)) else [after]

    def body(*refs):
        src_refs, land_refs = refs[:n], refs[n:n + m]
        for cp in copies_fn(src_refs, land_refs, refs[n + m], refs[n + m + 1]):
            cp.wait_send()
            cp.wait_recv()

    hbm = lambda s: pltpu.HBM(tuple(s.shape), s.dtype)
    outs = pl.pallas_call(
        body, name=name, out_shape=tuple(hbm(s) for s in srcs + lands),
        in_specs=[_HBM] * (n + m) + [_SEM, _SEM] + [pl.BlockSpec(memory_space=pl.ANY)] * len(after),
        out_specs=tuple([_HBM] * (n + m)), input_output_aliases={i: i for i in range(n + m)},
        compiler_params=pltpu.CompilerParams(has_side_effects=_EFFECT),
    )(*srcs, *lands, send_sems, recv_sems, *after)
    return list(outs[:n]), list(outs[n:])


def chip_major(w, groups=N_CHIPS):
    r, c = w.shape
    return w.reshape(r, groups, c // groups).transpose(1, 0, 2)


def from_chip_major(w):
    g, r, c = w.shape
    return w.transpose(1, 0, 2).reshape(r, g * c)


def _cd_in_pad(w):
    a = C_Q_RANK + C_KV_RANK
    z = lambda n: jnp.zeros((w.shape[0], n), w.dtype)
    return jnp.concatenate([w[:, :a], z(C_NOPE), w[:, a:a + C_ROPE], z(HEAD_PAD - C_NOPE - C_ROPE), w[:, a + C_ROPE:]], axis=1)


def _cd_in_unpad(w):
    a = C_Q_RANK + C_KV_RANK
    return jnp.concatenate([w[:, :a], w[:, a + C_NOPE:a + C_NOPE + C_ROPE], w[:, a + HEAD_PAD:]], axis=1)


def _pad_heads(w, width):
    r = w.shape[0]
    w = w.reshape(r, C_HEADS, width)
    return jnp.pad(w, ((0, 0), (0, 0), (0, HEAD_PAD - width))).reshape(r, _HW)


def _unpad_heads(w, width):
    r = w.shape[0]
    return w.reshape(r, C_HEADS, HEAD_PAD)[:, :, :width].reshape(r, C_HEADS * width)


def prepare_weights(p):
    q = dict(p)
    q["cd_w_in"] = _cd_in_pad(p["cd_w_in"])
    q["c_w_uq"] = _pad_heads(p["c_w_uq"], C_NOPE + C_ROPE)
    ukv = p["c_w_ukv"].reshape(C_KV_RANK, C_HEADS, C_NOPE + C_V)
    q["c_w_uk"] = _pad_heads(ukv[:, :, :C_NOPE].reshape(C_KV_RANK, -1), C_NOPE)
    q["c_w_uv"] = _pad_heads(ukv[:, :, C_NOPE:].reshape(C_KV_RANK, -1), C_V)
    wo = p["cd_w_out"]
    att_rows = jnp.pad(wo[:C_HEADS * C_V].reshape(C_HEADS, C_V, D_MODEL), ((0, 0), (0, HEAD_PAD - C_V), (0, 0)))
    q["cd_w_out"] = jnp.concatenate([att_rows.reshape(_HW, D_MODEL), wo[C_HEADS * C_V:]], axis=0)
    return q


def unprepare_grads(g):
    q = dict(g)
    q["cd_w_in"] = _cd_in_unpad(g["cd_w_in"])
    q["c_w_uq"] = _unpad_heads(g["c_w_uq"], C_NOPE + C_ROPE)
    uk = g.pop("c_w_uk").reshape(C_KV_RANK, C_HEADS, HEAD_PAD)[:, :, :C_NOPE]
    uv = g.pop("c_w_uv").reshape(C_KV_RANK, C_HEADS, HEAD_PAD)[:, :, :C_V]
    q.pop("c_w_uk", None)
    q.pop("c_w_uv", None)
    q["c_w_ukv"] = jnp.concatenate([uk, uv], axis=-1).reshape(C_KV_RANK, C_HEADS * (C_NOPE + C_V))
    wo = g["cd_w_out"]
    att = wo[:_HW].reshape(C_HEADS, HEAD_PAD, D_MODEL)[:, :C_V].reshape(C_HEADS * C_V, D_MODEL)
    q["cd_w_out"] = jnp.concatenate([att, wo[_HW:]], axis=0)
    return q


def rope_tables(positions):
    half = C_ROPE // 2
    inv_freq = ROPE_THETA ** (-jnp.arange(half, dtype=F32) / half)
    ang = positions.astype(F32)[:, None] * inv_freq
    cos, sin = jnp.cos(ang), jnp.sin(ang)
    s = positions.shape[0]
    z = lambda n: jnp.zeros((s, n), F32)
    cs = jnp.concatenate([jnp.ones((s, C_NOPE), F32), cos, cos, z(HEAD_PAD - C_NOPE - C_ROPE)], axis=1)
    s1 = jnp.concatenate([z(C_NOPE), -sin, z(HEAD_PAD - C_NOPE - half)], axis=1)
    s2 = jnp.concatenate([z(C_NOPE + half), sin, z(HEAD_PAD - C_NOPE - C_ROPE)], axis=1)
    return cs, s1, s2


_UP_COLS = 2 * D_FF // N_CHIPS


def ffn_fwd(h2, w, l, late_down=None):
    zf = matmul(h2, w["ffn_w_up"][l], "nn", BF16, f"ffn_up{l}", gb=N_CHIPS, go=2, tn=_UP_COLS)
    if late_down is not None:
        late_down(zf)
    a, f = ffn_act_down(zf, w["ffn_conv_w"][l], w["ffn_w_down"][l], f"ffn_act_down{l}")
    return f, (zf, a)


def ffn_bwd(df, h2, saved, w, l):
    zf, a = saved
    da = matmul(df, w["ffn_w_down"][l], "nt", BF16, f"ffn_down_dx{l}", tn=D_FF // 2)
    d_down = matmul(a, df, "tn", BF16, f"ffn_down_dw{l}", tm=D_FF // 2)
    dzf, d_conv = ffn_act_bwd(zf, w["ffn_conv_w"][l], da, f"ffn_act_bwd{l}")
    dh2 = matmul(dzf, w["ffn_w_up"][l], "nt", BF16, f"ffn_up_dx{l}", ga=2, gb=N_CHIPS, tk=_UP_COLS, tn=D_MODEL)
    d_up = matmul(h2, dzf, "tn", BF16, f"ffn_up_dw{l}", gb=2, go=N_CHIPS, tn=_UP_COLS)
    d_conv = d_conv.transpose(1, 0, 2).reshape(3, 2 * D_FF)
    return dh2, dict(ffn_w_down=d_down, ffn_conv_w=d_conv, ffn_w_up=d_up)


def mixer0_fwd(h, w):
    z = matmul(h, w["ab_w_in"], "nn", BF16, "ab_in", gb=N_CHIPS)
    ycat = pool_fwd(z, w["b_mix_w"], w["b_scale"], gconv_fwd(z, w["a_conv_w"]))
    y = matmul(ycat, w["ab_w_out"], "nn", BF16, "ab_out", tn=D_MODEL)
    return y, (z, ycat)


def mixer0_bwd(dy, h, saved, w):
    z, ycat = saved
    grads = {}
    dycat = matmul(dy, w["ab_w_out"], "nt", BF16, "ab_out_dx")
    grads["ab_w_out"] = matmul(ycat, dy, "tn", BF16, "ab_out_dw")
    db, dc, da, d_conv = gconv_bwd(z, w["a_conv_w"], dycat)
    dp, d_mix, d_scale = pool_bwd(z, w["b_mix_w"], w["b_scale"], dycat)
    dz = jnp.concatenate([db, dc, da, dp], axis=1)
    dh = matmul(dz, w["ab_w_in"], "nt", BF16, "ab_in_dx", gb=N_CHIPS, tn=D_MODEL)
    grads["ab_w_in"] = matmul(h, dz, "tn", BF16, "ab_in_dw", go=N_CHIPS)
    grads.update(a_conv_w=d_conv, b_mix_w=d_mix, b_scale=d_scale)
    return dh, grads


def mixer1_fwd(h, ropes, w):
    cs, s1, s2 = ropes
    z = matmul(h, w["cd_w_in"], "nn", BF16, "cd_in")
    bs_t = jnp.pad(w["d_b_s"].T, ((0, 0), (0, LANES - D_GROUPS)))
    qh, kh, vh = mla_pre_fwd(z, w["c_q_norm_g"], w["c_kv_norm_g"], w["c_w_uq"], w["c_w_uk"], w["c_w_uv"], cs, s1, s2)
    ycat = sgu_fwd(z, w["d_ln_g"], w["d_ln_b"], w["d_w_s"], bs_t, attn_fwd(qh, kh, vh))
    y = matmul(ycat, w["cd_w_out"], "nn", BF16, "cd_out", tn=D_MODEL)
    return y, (z, bs_t, qh, kh, vh, ycat)


def mixer1_bwd(dy, h, saved, ropes, w):
    cs, s1, s2 = ropes
    z, bs_t, qh, kh, vh, ycat = saved
    grads = {}
    dycat = matmul(dy, w["cd_w_out"], "nt", BF16, "cd_out_dx")
    grads["cd_w_out"] = matmul(ycat, dy, "tn", BF16, "cd_out_dw")
    dqh, dkh, dvh = attn_bwd(qh, kh, vh, ycat, dycat)
    dzq, d_uq, d_uk, d_uv, d_gq, d_gkv = mla_pre_bwd(
        z, w["c_q_norm_g"], w["c_kv_norm_g"], w["c_w_uq"], w["c_w_uk"], w["c_w_uv"], cs, s1, s2, dqh, dkh, dvh)
    dzu, dzv, d_ws, d_bs, d_lg, d_lb = sgu_bwd(z, w["d_ln_g"], w["d_ln_b"], w["d_w_s"], bs_t, dycat, _HW // _DW)
    dz = jnp.concatenate([dzq, dzu, dzv], axis=1)
    dh = matmul(dz, w["cd_w_in"], "nt", BF16, "cd_in_dx", tn=D_MODEL)
    grads["cd_w_in"] = matmul(h, dz, "tn", BF16, "cd_in_dw")
    grads.update(c_w_uq=d_uq, c_w_uk=d_uk, c_w_uv=d_uv, c_q_norm_g=d_gq, c_kv_norm_g=d_gkv, d_w_s=d_ws,
                 d_b_s=d_bs[:, :D_GROUPS].T, d_ln_g=d_lg, d_ln_b=d_lb)
    return dh, grads


class StepHooks:
    def weights(self, stage, after):
        pass

    def gradients(self, stage, grads, after):
        return 0.0


def run_step(x, tgt, mod, ropes, w, hooks):
    sh1, sc1, g1, sh2, sc2, g2 = range(N_MOD)
    mods = mod.reshape(2, 1, N_MOD * D_MODEL)
    n1 = w["norm1_g"].reshape(2, 1, D_MODEL)
    n2 = w["norm2_g"].reshape(2, 1, D_MODEL)
    final_g = Vec(w["final_norm_g"].reshape(1, 1, D_MODEL), 0, 0)

    hooks.weights("mix0", mod)
    h0 = modnorm_fwd(x, Vec(n1, 0, 0), Vec(mods, 0, sc1), Vec(mods, 0, sh1), "modnorm_0")
    y0, mix0 = mixer0_fwd(h0, w)
    x1, h1 = resid_modnorm_fwd(x, y0, Vec(mods, 0, g1), Vec(n2, 0, 0), Vec(mods, 0, sc2), Vec(mods, 0, sh2), "resid_modnorm_1")
    hooks.weights("up0", x1)
    f0, ffn0 = ffn_fwd(h1, w, 0, lambda act: hooks.weights("down0", act))
    x2, h2 = resid_modnorm_fwd(x1, f0, Vec(mods, 0, g2), Vec(n1, 1, 0), Vec(mods, 1, sc1), Vec(mods, 1, sh1), "resid_modnorm_2")
    hooks.weights("mix1", x2)
    y1, mix1 = mixer1_fwd(h2, ropes, w)
    x3, h3 = resid_modnorm_fwd(x2, y1, Vec(mods, 1, g1), Vec(n2, 1, 0), Vec(mods, 1, sc2), Vec(mods, 1, sh2), "resid_modnorm_3")
    hooks.weights("ffn1", x3)
    f1, ffn1 = ffn_fwd(h3, w, 1)
    dres, d_final, loss, df1, dg2b = final_fused(x3, f1, Vec(mods, 1, g2), final_g, tgt)

    dh3, gf1 = ffn_bwd(df1, h3, ffn1, w, 1)
    late = mods + hooks.gradients("ffn1", gf1, dh3)
    dres, dsh2b, dsc2b, dn2b, dy1, dg1b = norm_gate_bwd(
        x3, dh3, Vec(n2, 1, 0), Vec(late, 1, sc2), dres, y1, Vec(late, 1, g1), "norm_gate_bwd_3")
    dh2, gm1 = mixer1_bwd(dy1, h2, mix1, ropes, w)
    late = mods + hooks.gradients("mix1", gm1, dh2)
    dres, dsh1b, dsc1b, dn1b, df0, dg2a = norm_gate_bwd(
        x2, dh2, Vec(n1, 1, 0), Vec(late, 1, sc1), dres, f0, Vec(late, 0, g2), "norm_gate_bwd_2")
    dh1, gf0 = ffn_bwd(df0, h1, ffn0, w, 0)
    late = mods + hooks.gradients("ffn0", gf0, dh1)
    dres, dsh2a, dsc2a, dn2a, dy0, dg1a = norm_gate_bwd(
        x1, dh1, Vec(n2, 0, 0), Vec(late, 0, sc2), dres, y0, Vec(late, 0, g1), "norm_gate_bwd_1")
    dh0, gm0 = mixer0_bwd(dy0, h0, mix0, w)
    late = mods + hooks.gradients("mix0", gm0, dh0)
    grad_x, dsh1a, dsc1a, dn1a = norm_bwd(x, dh0, Vec(n1, 0, 0), Vec(late, 0, sc1), dres, "norm_bwd_0")

    dmod = jnp.concatenate([jnp.concatenate([dsh1a, dsc1a, dg1a, dsh2a, dsc2a, dg2a], axis=1),
                            jnp.concatenate([dsh1b, dsc1b, dg1b, dsh2b, dsc2b, dg2b], axis=1)], axis=0)
    norms = dict(norm1_g=jnp.concatenate([dn1a, dn1b], axis=0), norm2_g=jnp.concatenate([dn2a, dn2b], axis=0),
                 final_norm_g=d_final)
    return loss, grad_x, dmod, dict(mix0=gm0, ffn0=gf0, mix1=gm1, ffn1=gf1, norms=norms)


def merge_grads(by_stage):
    grads = {**by_stage["mix0"], **by_stage["mix1"], **by_stage["norms"]}
    for k in ("ffn_w_down", "ffn_w_up"):
        grads[k] = [by_stage["ffn0"][k], by_stage["ffn1"][k]]
    grads["ffn_conv_w"] = jnp.stack([by_stage["ffn0"]["ffn_conv_w"], by_stage["ffn1"]["ffn_conv_w"]])
    return grads


_WEIGHTS = ("ada_w", "ada_b", "norm1_g", "norm2_g", "ab_w_in", "a_conv_w", "b_mix_w", "b_scale", "ab_w_out", "cd_w_in",
            "c_q_norm_g", "c_w_uq", "c_kv_norm_g", "c_w_ukv", "d_ln_g", "d_ln_b", "d_w_s", "d_b_s", "cd_w_out",
            "ffn_w_up", "ffn_conv_w", "ffn_w_down", "final_norm_g")
_INPUTS = ("x", "c", "positions") + _WEIGHTS + ("loss_target",) + tuple("m_" + n for n in _WEIGHTS) + tuple(
    "v_" + n for n in _WEIGHTS)

def _pack_rows(parts, rows, dtype):
    flat = jnp.concatenate([p.reshape(-1).astype(dtype) for p in parts])
    return jnp.pad(flat, (0, rows * LANES - flat.shape[0])).reshape(rows, LANES)


def _rows_major(w):
    r, c = w.shape
    return w.reshape(N_CHIPS, r // N_CHIPS, c)


def start_gather(shards, tag, after=()):
    lands = [_sds((N_CHIPS,) + s.shape, s.dtype) for s in shards]
    return split_start("gather_start_" + tag, GATHER, shards, lands, after)


def finish_gather(handle, chip, tag, after):
    shards, lands = split_wait("gather_wait_" + tag, GATHER, handle, after)
    lands = forward_halves(lands, "gather_forward_" + tag)
    return [lax.dynamic_update_index_in_dim(o, s, chip, 0) for o, s in zip(lands, shards)]


def start_reduce(gs, core, tag):
    recv = swap_halves(gs, "swap_halves_" + tag)
    pairs = pair_sums(gs, recv, core, "pair_sums_" + tag)
    lands = [_sds((N_CHIPS - 1,) + p.shape[1:], p.dtype) for p in pairs]
    return split_start("exchange_start_" + tag, EXCHANGE, pairs, lands)


def finish_reduce(handle, chip, core, tag, after):
    pairs, others = split_wait("exchange_wait_" + tag, EXCHANGE, handle, after)
    halves = chip_sums(pairs, others, chip, core, "chip_sums_" + tag)
    full = join_halves(halves, "join_halves_" + tag)
    return [f.reshape(f.shape[1] * 2, f.shape[2]) for f in full]


_SMALL_SHARDED = (("a_conv_w", (3, 128), 1), ("c_q_norm_g", (1, 64), 1), ("d_ln_g", (1, 128), 1), ("d_ln_b", (1, 128), 1),
                  ("ffn_conv_w", (2, 3, 2 * D_FF // N_CHIPS), 2))
_SMALL_GRADS = (("norm1_g", (2, D_MODEL)), ("norm2_g", (2, D_MODEL)), ("b_mix_w", (4, 128, 128)), ("b_scale", (1, 512)),
                ("c_kv_norm_g", (1, 128)), ("d_w_s", (4, 128, 128)), ("d_b_s", (4, 128)), ("final_norm_g", (1, D_MODEL)),
                ("a_conv_w", (3, 512)), ("c_q_norm_g", (1, 256)), ("d_ln_g", (1, 512)), ("d_ln_b", (1, 512)),
                ("ffn_conv_w", (2, 3, 2 * D_FF)))


def _size(shape):
    n = 1
    for d in shape:
        n *= d
    return n


def kernel(x, c, positions, ada_w, ada_b, norm1_g, norm2_g, ab_w_in, a_conv_w, b_mix_w, b_scale, ab_w_out, cd_w_in, c_q_norm_g, c_w_uq, c_kv_norm_g, c_w_ukv, d_ln_g, d_ln_b, d_w_s, d_b_s, cd_w_out, ffn_w_up, ffn_conv_w, ffn_w_down, final_norm_g, loss_target, m_ada_w, m_ada_b, m_norm1_g, m_norm2_g, m_ab_w_in, m_a_conv_w, m_b_mix_w, m_b_scale, m_ab_w_out, m_cd_w_in, m_c_q_norm_g, m_c_w_uq, m_c_kv_norm_g, m_c_w_ukv, m_d_ln_g, m_d_ln_b, m_d_w_s, m_d_b_s, m_cd_w_out, m_ffn_w_up, m_ffn_conv_w, m_ffn_w_down, m_final_norm_g, v_ada_w, v_ada_b, v_norm1_g, v_norm2_g, v_ab_w_in, v_a_conv_w, v_b_mix_w, v_b_scale, v_ab_w_out, v_cd_w_in, v_c_q_norm_g, v_c_w_uq, v_c_kv_norm_g, v_c_w_ukv, v_d_ln_g, v_d_ln_b, v_d_w_s, v_d_b_s, v_cd_w_out, v_ffn_w_up, v_ffn_conv_w, v_ffn_w_down, v_final_norm_g):
    args = (x, c, positions, ada_w, ada_b, norm1_g, norm2_g, ab_w_in, a_conv_w, b_mix_w, b_scale, ab_w_out, cd_w_in, c_q_norm_g, c_w_uq, c_kv_norm_g, c_w_ukv, d_ln_g, d_ln_b, d_w_s, d_b_s, cd_w_out, ffn_w_up, ffn_conv_w, ffn_w_down, final_norm_g, loss_target, m_ada_w, m_ada_b, m_norm1_g, m_norm2_g, m_ab_w_in, m_a_conv_w, m_b_mix_w, m_b_scale, m_ab_w_out, m_cd_w_in, m_c_q_norm_g, m_c_w_uq, m_c_kv_norm_g, m_c_w_ukv, m_d_ln_g, m_d_ln_b, m_d_w_s, m_d_b_s, m_cd_w_out, m_ffn_w_up, m_ffn_conv_w, m_ffn_w_down, m_final_norm_g, v_ada_w, v_ada_b, v_norm1_g, v_norm2_g, v_ab_w_in, v_a_conv_w, v_b_mix_w, v_b_scale, v_ab_w_out, v_cd_w_in, v_c_q_norm_g, v_c_w_uq, v_c_kv_norm_g, v_c_w_ukv, v_d_ln_g, v_d_ln_b, v_d_w_s, v_d_b_s, v_cd_w_out, v_ffn_w_up, v_ffn_conv_w, v_ffn_w_down, v_final_norm_g)
    a = dict(zip(_INPUTS, args, strict=True))
    xi, yi, ci = _place()
    chip = 2 * xi + yi
    dev = 4 * xi + 2 * yi + ci
    x = a["x"][0]
    tgt = a["loss_target"][0]

    bf = lambda t: t.astype(BF16)
    mix0_handle, tok = start_gather([bf(a["ab_w_in"][0]), bf(a["ab_w_out"][0])], "mix0")
    up0_16, down0_16, up1_16, down1_16 = [bf(a[n][l]) for l in (0, 1) for n in ("ffn_w_up", "ffn_w_down")]
    mix1_16 = [bf(a[n][0]) for n in ("cd_w_in", "c_w_uq", "c_w_ukv", "cd_w_out")]

    small_parts = [a["c"] + tok] + [a[n] for n, _, _ in _SMALL_SHARDED]
    rows1 = -(-sum(p.size for p in small_parts) // LANES // 8) * 8
    g1 = all_gather8(_pack_rows(small_parts, rows1, F32), "gather_small",
                     [up0_16, down0_16, up1_16, down1_16, mix1_16[0], mix1_16[3]]).reshape(N_DEV, rows1 * LANES)
    c_all = g1[:, :D_MODEL]
    per_chip = g1[0::2]
    small_full = {}
    off = D_MODEL
    for n, shp, axis in _SMALL_SHARDED:
        piece = per_chip[:, off:off + _size(shp)].reshape((N_CHIPS,) + shp)
        small_full[n] = jnp.concatenate([piece[k] for k in range(N_CHIPS)], axis=axis)
        off += _size(shp)

    merge = lambda t: t.reshape(t.shape[0] * t.shape[1], t.shape[2])
    w = dict(norm1_g=a["norm1_g"], norm2_g=a["norm2_g"], b_mix_w=a["b_mix_w"][0], b_scale=a["b_scale"],
             c_kv_norm_g=a["c_kv_norm_g"], d_w_s=a["d_w_s"][0], d_b_s=a["d_b_s"][0],
             final_norm_g=a["final_norm_g"].reshape(1, D_MODEL), **small_full)

    ncol = N_MOD * D_MODEL // N_CHIPS
    ada_b_mine = lax.dynamic_slice_in_dim(a["ada_b"], chip * ncol, ncol, axis=1)
    mod_cols = ada_mod(c_all, a["ada_w"], ada_b_mine)
    g2_rows = all_gather8(mod_cols.reshape(-1, LANES), "gather_mod")
    g2 = g2_rows.reshape(N_DEV, 2, N_DEV, ncol)
    mod = lax.dynamic_index_in_dim(g2[0::2], dev, axis=2, keepdims=False)
    mod = mod.transpose(1, 0, 2).reshape(2, N_MOD * D_MODEL)

    late = [g2_rows]
    up0_handle, tok_a = start_gather([up0_16], "up0", late)
    down0_handle, tok_b = start_gather([down0_16], "down0", late)
    mix1_handle, tok_c = start_gather(mix1_16, "mix1", late)
    ffn1_handle, tok_d = start_gather([up1_16, down1_16], "ffn1", late)
    mod = mod + (tok_a + tok_b + tok_c + tok_d)

    ropes = rope_tables(a["positions"][0])
    cm16 = lambda t: chip_major(t).astype(BF16)
    w.update(ffn_w_up=[None, None], ffn_w_down=[None, None])
    handles = dict(mix0=mix0_handle, up0=up0_handle, down0=down0_handle, mix1=mix1_handle, ffn1=ffn1_handle)
    reducing, reduced = {}, {}

    class Hooks(StepHooks):
        def weights(self, stage, after):
            got = finish_gather(handles[stage], chip, stage, after)
            if stage == "mix0":
                w.update(ab_w_in=got[0], ab_w_out=merge(got[1]))
            elif stage == "up0":
                w["ffn_w_up"][0] = got[0]
            elif stage == "down0":
                w["ffn_w_down"][0] = merge(got[0])
            elif stage == "mix1":
                cd_in, uq, ukv, cd_out = got
                w.update(prepare_weights(dict(cd_w_in=from_chip_major(cd_in), c_w_uq=from_chip_major(uq),
                                              c_w_ukv=from_chip_major(ukv), cd_w_out=merge(cd_out))))
            else:
                w["ffn_w_up"][1], w["ffn_w_down"][1] = got[0], merge(got[1])

        def gradients(self, stage, grads, after):
            if stage in ("ffn0", "ffn1"):
                parts = [grads["ffn_w_up"], _rows_major(grads["ffn_w_down"])]
            elif stage == "mix1":
                grads.update(unprepare_grads(grads))
                parts = [cm16(grads["cd_w_in"]), cm16(grads["c_w_uq"]), cm16(grads["c_w_ukv"]),
                         _rows_major(grads["cd_w_out"]).astype(BF16)]
            else:
                parts = [grads["ab_w_in"], _rows_major(grads["ab_w_out"])]
            reducing[stage], tok = start_reduce(parts, ci, stage)
            before = {"mix1": "ffn1", "ffn0": "mix1", "mix0": "ffn0"}.get(stage)
            if before is not None:
                reduced[before] = finish_reduce(reducing[before], chip, ci, before, after)
            return tok

    loss, grad_x, dmod, by_stage = run_step(x, tgt, mod, ropes, w, Hooks())
    grads = merge_grads(by_stage)

    parts3 = [dmod] + [grads[n] for n, _ in _SMALL_GRADS] + [loss[0, 0]]
    rows3 = -(-sum(p.size for p in parts3) // LANES // 8) * 8
    small_handle, _ = split_start("small_grads_start", EVERYONE, [_pack_rows(parts3, rows3, F32)],
                                  [_sds((N_DEV, rows3, LANES))])
    red_up1, red_down1 = reduced["ffn1"]
    red_cd_in, red_uq, red_ukv, red_cd_out = reduced["mix1"]
    red_up0, red_down0 = reduced["ffn0"]
    out_grads = dict(cd_w_in=red_cd_in, c_w_uq=red_uq, c_w_ukv=red_ukv, cd_w_out=red_cd_out)
    per_layer = dict(ffn_w_up=(red_up0, red_up1), ffn_w_down=(red_down0, red_down1))
    updates = {}

    def update(n):
        if n in per_layer:
            updates[n] = adamw_layers(a[n], *per_layer[n], a["m_" + n], a["v_" + n], "adamw_" + n)
        else:
            updates[n] = adamw(a[n], out_grads[n].reshape(a[n].shape), a["m_" + n], a["v_" + n], "adamw_" + n)

    early =("ffn_w_up", "ffn_w_down", "cd_w_in", "c_w_uq", "c_w_ukv", "cd_w_out")
    for n in early:
        update(n)
    (mine,), (landed,) = split_wait("small_grads_wait", EVERYONE, small_handle, [updates[n][1] for n in early])
    g3 = lax.dynamic_update_index_in_dim(landed, mine, dev, 0)
    summed = sum8(g3).reshape(-1)
    nmod = 2 * N_MOD * D_MODEL
    out_grads["ada_b"] = summed[:nmod].reshape(2, N_MOD * D_MODEL)
    off = nmod
    for n, shp in _SMALL_GRADS:
        out_grads[n] = summed[off:off + _size(shp)].reshape(shp)
        off += _size(shp)
    loss = summed[off]
    for n, shp, axis in _SMALL_SHARDED:
        width = out_grads[n].shape[-1] // N_CHIPS
        out_grads[n] = lax.dynamic_slice_in_dim(out_grads[n], chip * width, width, axis=out_grads[n].ndim - 1)
    dmod_all = g3.reshape(N_DEV, rows3 * LANES)[:, :nmod].reshape(N_DEV, 2, N_MOD * D_MODEL)
    dmod_mine = lax.dynamic_slice_in_dim(dmod_all, chip * ncol, ncol, axis=2).transpose(1, 0, 2)
    updates["ada_w"] = adamw_ada(a["ada_w"], c_all, dmod_mine, a["m_ada_w"], a["v_ada_w"])

    red_in0, red_out0 = finish_reduce(reducing["mix0"], chip, ci, "mix0", updates["ada_w"][1])
    out_grads.update(ab_w_in=red_in0, ab_w_out=red_out0)

    for n in ("ab_w_in", "ab_w_out"):
        update(n)
    small = [n for n in _WEIGHTS if n not in updates]
    for n, res in zip(small, adamw_small([a[n] for n in small], [out_grads[n].reshape(a[n].shape) for n in small],
                                         [a["m_" + n] for n in small], [a["v_" + n] for n in small])):
        updates[n] = res
    return (loss, grad_x[None], *[updates[n][i] for i in range(4) for n in _WEIGHTS])
```

```python
import functools
from typing import NamedTuple

import jax
import jax.numpy as jnp
from jax import lax
from jax.experimental import pallas as pl
from jax.experimental.pallas import tpu as pltpu

F32 = jnp.float32
BF16 = jnp.bfloat16
EPS = 1e-6
D_MODEL = 1024
N_MOD = 6
A_WIDTH = 512
B_GROUPS = 4
C_HEADS = 8
C_NOPE = 64
C_ROPE = 32
C_V = 64
C_Q_RANK = 256
C_KV_RANK = 128
HEAD_PAD = 128
ROPE_THETA = 10000.0
D_GROUPS = 4
D_CHUNK = 128
D_FF = 2816
FF_UNIT = 128
ADAM_LR = 0.001
ADAM_B1 = 0.9
ADAM_B2 = 0.999
ADAM_EPS = 1e-08
ADAM_WD = 0.01
ADAM_STEP = 10
N_CHIPS = 4
N_DEV = 8
LANES = 128
VMEM_BIG = 56 * 1024 * 1024
MESH = pl.DeviceIdType.MESH


def _sds(shape, dtype=F32):
    return jax.ShapeDtypeStruct(tuple(shape), dtype)


def _tile(n, cap, mult=128):
    if n <= cap:
        return n
    best = None
    for t in range(mult, cap + 1, mult):
        if n % t == 0:
            best = t
    assert best is not None, (n, cap, mult)
    return best


def _params(dims=None, vmem=None):
    return pltpu.CompilerParams(dimension_semantics=dims, vmem_limit_bytes=vmem)


def _shift_down(v, k):
    r = pltpu.roll(v, k, axis=0)
    t = lax.broadcasted_iota(jnp.int32, v.shape, 0)
    return jnp.where(t >= k, r, 0.0)


def _shift_up(v, k):
    n = v.shape[0]
    r = pltpu.roll(v, n - k, axis=0)
    t = lax.broadcasted_iota(jnp.int32, v.shape, 0)
    return jnp.where(t < n - k, r, 0.0)


def _sigmoid(v):
    return 1.0 / (1.0 + jnp.exp(-v))


_GELU_C = 0.7978845608028654
_GELU_A = 0.044715


def _gelu(v):
    return 0.5 * v * (1.0 + jnp.tanh(_GELU_C * (v + _GELU_A * v * v * v)))


def _gelu_grad(v):
    th = jnp.tanh(_GELU_C * (v + _GELU_A * v * v * v))
    return 0.5 * (1.0 + th) + 0.5 * v * (1.0 - th * th) * _GELU_C * (1.0 + 3.0 * _GELU_A * v * v)


_NN = (((1,), (0,)), ((), ()))
_NT = (((1,), (1,)), ((), ()))
_TN = (((0,), (0,)), ((), ()))


def _dot(a, b, dims=_NN):
    return lax.dot_general(a, b, dims, preferred_element_type=F32)


def _logical(t, groups):
    return (t.shape[-2], t.shape[-1] * groups)


def _block(tr, tc, groups, cols, where):
    if groups == 1:
        return pl.BlockSpec((tr, tc), where)
    per = cols // groups // tc

    def index(i, j, s):
        r, c = where(i, j, s)
        return (c // per, r, c % per)

    return pl.BlockSpec((None, tr, tc), index)


def matmul(a, b, mode, out_dtype, name, ga=1, gb=1, go=1, tm=None, tn=None, tk=None):
    (ar, ac), (br, bc) = _logical(a, ga), _logical(b, gb)
    if mode == "nn":
        m, k, n = ar, ac, bc
        a_col, b_col = "k", "n"
    elif mode == "nt":
        m, k, n = ar, ac, br
        a_col, b_col = "k", "k"
    else:
        k, m, n = ar, ac, bc
        a_col, b_col = "m", "n"
    limit = {"m": m, "n": n // go, "k": k}
    limit[a_col] = min(limit[a_col], ac // ga)
    limit[b_col] = min(limit[b_col], bc // gb)
    tm = tm or _tile(limit["m"], 2048, 128 if mode == "tn" else 16)
    tn = tn or _tile(limit["n"], 512)
    tk = tk or _tile(limit["k"], 2048, 16 if mode == "tn" else 128)
    nk = k // tk
    if mode == "nn":
        a_spec = _block(tm, tk, ga, ac, lambda i, j, s: (i, s))
        b_spec = _block(tk, tn, gb, bc, lambda i, j, s: (s, j))
        dims = _NN
    elif mode == "nt":
        a_spec = _block(tm, tk, ga, ac, lambda i, j, s: (i, s))
        b_spec = _block(tn, tk, gb, bc, lambda i, j, s: (j, s))
        dims = _NT
    else:
        a_spec = _block(tk, tm, ga, ac, lambda i, j, s: (s, i))
        b_spec = _block(tk, tn, gb, bc, lambda i, j, s: (s, j))
        dims = _TN
    o_spec = _block(tm, tn, go, n, lambda i, j, s: (i, j))
    out_shape = _sds((m, n), out_dtype) if go == 1 else _sds((go, m, n // go), out_dtype)

    def body(a_ref, b_ref, o_ref, acc_ref):
        s = pl.program_id(2)

        @pl.when(s == 0)
        def _():
            acc_ref[...] = jnp.zeros_like(acc_ref)

        acc_ref[...] += _dot(a_ref[...], b_ref[...], dims)

        @pl.when(s == nk - 1)
        def _():
            o_ref[...] = acc_ref[...].astype(o_ref.dtype)

    return pl.pallas_call(
        body, name=name, out_shape=out_shape, grid=(m // tm, n // tn, nk),
        in_specs=[a_spec, b_spec], out_specs=o_spec,
        scratch_shapes=[pltpu.VMEM((tm, tn), F32)],
        compiler_params=_params(("parallel", "parallel", "arbitrary"), VMEM_BIG),
    )(a, b)


def _rows(tm, n):
    return pl.BlockSpec((tm, n), lambda i: (i, 0))


def _vec(n):
    return pl.BlockSpec((1, n), lambda i: (0, 0))


class Vec(NamedTuple):
    array: jax.Array
    row: int
    col: int


def _vec_in(v, d):
    return pl.BlockSpec((None, 1, d), lambda i: (v.row, 0, v.col))


def modnorm_fwd(x, g, sc, sh, name):
    s, d = x.shape
    tm = _tile(s, 256, 8)

    def body(x_ref, g_ref, sc_ref, sh_ref, o_ref):
        xv = x_ref[...]
        r = lax.rsqrt(jnp.mean(xv * xv, axis=-1, keepdims=True) + EPS)
        o_ref[...] = ((xv * r) * g_ref[...] * (1.0 + sc_ref[...]) + sh_ref[...]).astype(BF16)

    return pl.pallas_call(
        body, name=name, out_shape=_sds((s, d), BF16), grid=(s // tm,),
        in_specs=[_rows(tm, d), _vec_in(g, d), _vec_in(sc, d), _vec_in(sh, d)], out_specs=_rows(tm, d),
        compiler_params=_params(("parallel",)),
    )(x, g.array, sc.array, sh.array)


def norm_bwd(x, dh, g, sc, dres, name):
    s, d = x.shape
    tm, streams = _row_streams(s)
    nsteps = s // tm

    def body(x_ref, dh_ref, g_ref, sc_ref, dr_ref, dx_ref, dsh_ref, dsc_ref, dg_ref, a2_ref):
        i = pl.program_id(0)

        @pl.when(i == 0)
        def _():
            dsh_ref[...] = jnp.zeros_like(dsh_ref)
            a2_ref[...] = jnp.zeros_like(a2_ref)

        for rs in streams:
            xv = x_ref[rs, :]
            dh = dh_ref[rs, :].astype(F32)
            r = lax.rsqrt(jnp.mean(xv * xv, axis=-1, keepdims=True) + EPS)
            xh = xv * r
            dsh_ref[...] += jnp.sum(dh, axis=0, keepdims=True)
            a2_ref[...] += jnp.sum(dh * xh, axis=0, keepdims=True)
            dxh = dh * (g_ref[...] * (1.0 + sc_ref[...]))
            dx = r * (dxh - xh * jnp.mean(dxh * xh, axis=-1, keepdims=True))
            dx_ref[rs, :] = dr_ref[rs, :] + dx

        @pl.when(i == nsteps - 1)
        def _():
            dsc_ref[...] = a2_ref[...] * g_ref[...]
            dg_ref[...] = a2_ref[...] * (1.0 + sc_ref[...])

    return pl.pallas_call(
        body, name=name, out_shape=(_sds((s, d)), _sds((1, d)), _sds((1, d)), _sds((1, d))), grid=(nsteps,),
        in_specs=[_rows(tm, d), _rows(tm, d), _vec_in(g, d), _vec_in(sc, d), _rows(tm, d)],
        out_specs=(_rows(tm, d), _vec(d), _vec(d), _vec(d)),
        scratch_shapes=[pltpu.VMEM((1, d), F32)],
        compiler_params=_params(("arbitrary",), VMEM_BIG),
    )(x, dh, g.array, sc.array, dres)


_ROW_STREAM = 256


def _row_streams(s):
    tm = _tile(s, 2 * _ROW_STREAM, 8)
    sub = min(tm, _ROW_STREAM)
    return tm, [slice(r * sub, (r + 1) * sub) for r in range(tm // sub)]


def resid_modnorm_fwd(x, y, gate, g, sc, sh, name):
    s, d = x.shape
    tm, streams = _row_streams(s)

    def body(x_ref, y_ref, gate_ref, g_ref, sc_ref, sh_ref, xo_ref, h_ref):
        for rs in streams:
            xv = x_ref[rs, :] + gate_ref[...] * y_ref[rs, :].astype(F32)
            xo_ref[rs, :] = xv
            r = lax.rsqrt(jnp.mean(xv * xv, axis=-1, keepdims=True) + EPS)
            h_ref[rs, :] = ((xv * r) * g_ref[...] * (1.0 + sc_ref[...]) + sh_ref[...]).astype(BF16)

    return pl.pallas_call(
        body, name=name, out_shape=(_sds((s, d)), _sds((s, d), BF16)), grid=(s // tm,),
        in_specs=[_rows(tm, d), _rows(tm, d), _vec_in(gate, d), _vec_in(g, d), _vec_in(sc, d), _vec_in(sh, d)],
        out_specs=(_rows(tm, d), _rows(tm, d)),
        compiler_params=_params(("parallel",), VMEM_BIG),
    )(x, y, gate.array, g.array, sc.array, sh.array)


def norm_gate_bwd(x, dh, g, sc, dres, y, gate, name):
    s, d = x.shape
    tm, streams = _row_streams(s)
    nsteps = s // tm

    def body(x_ref, dh_ref, g_ref, sc_ref, dr_ref, y_ref, gate_ref, dx_ref, dsh_ref, dsc_ref, dg_ref, dy_ref,
             dgate_ref, a2_ref):
        i = pl.program_id(0)

        @pl.when(i == 0)
        def _():
            dsh_ref[...] = jnp.zeros_like(dsh_ref)
            a2_ref[...] = jnp.zeros_like(a2_ref)
            dgate_ref[...] = jnp.zeros_like(dgate_ref)

        for rs in streams:
            xv = x_ref[rs, :]
            dh = dh_ref[rs, :].astype(F32)
            r = lax.rsqrt(jnp.mean(xv * xv, axis=-1, keepdims=True) + EPS)
            xh = xv * r
            dsh_ref[...] += jnp.sum(dh, axis=0, keepdims=True)
            a2_ref[...] += jnp.sum(dh * xh, axis=0, keepdims=True)
            dxh = dh * (g_ref[...] * (1.0 + sc_ref[...]))
            dr = dr_ref[rs, :] + r * (dxh - xh * jnp.mean(dxh * xh, axis=-1, keepdims=True))
            dx_ref[rs, :] = dr
            dy_ref[rs, :] = (dr * gate_ref[...]).astype(BF16)
            dgate_ref[...] += jnp.sum(dr * y_ref[rs, :].astype(F32), axis=0, keepdims=True)

        @pl.when(i == nsteps - 1)
        def _():
            dsc_ref[...] = a2_ref[...] * g_ref[...]
            dg_ref[...] = a2_ref[...] * (1.0 + sc_ref[...])

    vec = _sds((1, d))
    return pl.pallas_call(
        body, name=name, out_shape=(_sds((s, d)), vec, vec, vec, _sds((s, d), BF16), vec), grid=(nsteps,),
        in_specs=[_rows(tm, d), _rows(tm, d), _vec_in(g, d), _vec_in(sc, d), _rows(tm, d), _rows(tm, d), _vec_in(gate, d)],
        out_specs=(_rows(tm, d), _vec(d), _vec(d), _vec(d), _rows(tm, d), _vec(d)),
        scratch_shapes=[pltpu.VMEM((1, d), F32)],
        compiler_params=_params(("arbitrary",), VMEM_BIG),
    )(x, dh, g.array, sc.array, dres, y, gate.array)


def final_fused(x, f, gate, g, tgt):
    s, d = x.shape
    tm, streams = _row_streams(s)

    def body(x_ref, f_ref, gate_ref, g_ref, t_ref, dx_ref, dg_ref, loss_ref, df_ref, dgate_ref):
        @pl.when(pl.program_id(0) == 0)
        def _():
            dg_ref[...] = jnp.zeros_like(dg_ref)
            loss_ref[...] = jnp.zeros_like(loss_ref)
            dgate_ref[...] = jnp.zeros_like(dgate_ref)

        gatev, gv = gate_ref[...], g_ref[...]
        for rs in streams:
            fv = f_ref[rs, :].astype(F32)
            xv = x_ref[rs, :] + gatev * fv
            r = lax.rsqrt(jnp.mean(xv * xv, axis=-1, keepdims=True) + EPS)
            xh = xv * r
            e = xh * gv - t_ref[rs, :]
            row = jnp.sum(e * e, axis=-1, keepdims=True) * (0.5 / d)
            loss_ref[...] += jnp.sum(row, axis=0, keepdims=True)
            dy = e * (1.0 / d)
            dg_ref[...] += jnp.sum(dy * xh, axis=0, keepdims=True)
            dxh = dy * gv
            dx = r * (dxh - xh * jnp.mean(dxh * xh, axis=-1, keepdims=True))
            dx_ref[rs, :] = dx
            df_ref[rs, :] = (dx * gatev).astype(BF16)
            dgate_ref[...] += jnp.sum(dx * fv, axis=0, keepdims=True)

    vec = _sds((1, d))
    return pl.pallas_call(
        body, name="final_fused", out_shape=(_sds((s, d)), vec, _sds((1, LANES)), _sds((s, d), BF16), vec),
        grid=(s // tm,),
        in_specs=[_rows(tm, d), _rows(tm, d), _vec_in(gate, d), _vec_in(g, d), _rows(tm, d)],
        out_specs=(_rows(tm, d), _vec(d), _vec(LANES), _rows(tm, d), _vec(d)),
        compiler_params=_params(("arbitrary",), VMEM_BIG),
    )(x, f, gate.array, g.array, tgt)


def _taps(v):
    return _shift_down(v, 2), _shift_down(v, 1), v


def _conv3_taps(taps, w):
    return w[0:1, :] * taps[0] + w[1:2, :] * taps[1] + w[2:3, :] * taps[2]


def _conv3(v, w):
    return _conv3_taps(_taps(v), w)


def _conv3_t(dv, w):
    return w[0:1, :] * _shift_up(dv, 2) + w[1:2, :] * _shift_up(dv, 1) + w[2:3, :] * dv


def _conv3_dw_taps(dv, taps):
    return jnp.concatenate([jnp.sum(dv * t, axis=0, keepdims=True) for t in taps], axis=0)


def _conv3_dw(dv, v):
    return _conv3_dw_taps(dv, _taps(v))


def gconv_fwd(z, conv_w):
    s = z.shape[0]
    nb = A_WIDTH // LANES

    def body(b_ref, c_ref, a_ref, w_ref, o_ref):
        b, c, a = b_ref[...].astype(F32), c_ref[...].astype(F32), a_ref[...].astype(F32)
        o_ref[...] = (b * _conv3(c * a, w_ref[...])).astype(BF16)

    col = lambda off: pl.BlockSpec((s, LANES), lambda j: (0, off + j))
    return pl.pallas_call(
        body, name="gconv_fwd", out_shape=_sds((s, A_WIDTH + _B_WIDTH), BF16), grid=(nb,),
        in_specs=[col(0), col(nb), col(2 * nb), pl.BlockSpec((3, LANES), lambda j: (0, j))],
        out_specs=pl.BlockSpec((s, LANES), lambda j: (0, j)),
        compiler_params=_params(("parallel",), VMEM_BIG),
    )(z, z, z, conv_w)


def gconv_bwd(z, conv_w, dycat):
    s = z.shape[0]
    nb = A_WIDTH // LANES

    def body(b_ref, c_ref, a_ref, w_ref, dy_ref, db_ref, dc_ref, da_ref, dw_ref):
        c, a, w, dy = c_ref[...].astype(F32), a_ref[...].astype(F32), w_ref[...], dy_ref[...].astype(F32)
        ca = c * a
        db_ref[...] = (dy * _conv3(ca, w)).astype(BF16)
        dconv = dy * b_ref[...].astype(F32)
        dw_ref[...] = _conv3_dw(dconv, ca)
        dca = _conv3_t(dconv, w)
        dc_ref[...] = (dca * a).astype(BF16)
        da_ref[...] = (dca * c).astype(BF16)

    col = lambda off: pl.BlockSpec((s, LANES), lambda j: (0, off + j))
    wspec = pl.BlockSpec((3, LANES), lambda j: (0, j))
    part = _sds((s, A_WIDTH), BF16)
    return pl.pallas_call(
        body, name="gconv_bwd", out_shape=(part, part, part, _sds((3, A_WIDTH))), grid=(nb,),
        in_specs=[col(0), col(nb), col(2 * nb), wspec, col(0)],
        out_specs=(col(0), col(0), col(0), wspec),
        compiler_params=_params(("parallel",), VMEM_BIG),
    )(z, z, z, conv_w, dycat)


def _pool_counts(s, w):
    t = lax.broadcasted_iota(jnp.int32, (s, 1), 0)
    return jnp.minimum(t + 1, w).astype(F32)


def _pooled(p, levels):
    acc = p
    for lv in range(levels):
        acc = acc + _shift_down(acc, 2 ** lv)
    return acc / _pool_counts(p.shape[0], 2 ** levels) - p


_B_WIDTH = B_GROUPS * LANES


def pool_fwd(z, mix_w, scale, ycat):
    s = z.shape[0]

    def body(p_ref, m_ref, sc_ref, ycat_ref, o_ref):
        del ycat_ref
        for g in range(B_GROUPS):
            cols = slice(g * LANES, (g + 1) * LANES)
            pooled = _pooled(p_ref[:, cols].astype(F32), g + 1)
            y = _dot(pooled.astype(BF16), m_ref[g].astype(BF16))
            o_ref[:, cols] = (y * sc_ref[:, cols]).astype(BF16)

    return pl.pallas_call(
        body, name="pool_fwd", out_shape=_sds(ycat.shape, BF16), grid=(1,),
        in_specs=[pl.BlockSpec((s, _B_WIDTH), lambda i: (0, 3 * A_WIDTH // _B_WIDTH)),
                  pl.BlockSpec((B_GROUPS, LANES, LANES), lambda i: (0, 0, 0)), pl.BlockSpec((1, _B_WIDTH), lambda i: (0, 0)),
                  pl.BlockSpec(memory_space=pl.ANY)],
        out_specs=pl.BlockSpec((s, _B_WIDTH), lambda i: (0, A_WIDTH // _B_WIDTH)),
        input_output_aliases={3: 0},
        compiler_params=_params(("arbitrary",), VMEM_BIG),
    )(z, mix_w, scale, ycat)


def pool_bwd(z, mix_w, scale, dycat):
    s = z.shape[0]

    def body(p_ref, m_ref, sc_ref, dy_ref, dp_ref, dm_ref, dsc_ref):
        for g in range(B_GROUPS):
            cols = slice(g * LANES, (g + 1) * LANES)
            pooled = _pooled(p_ref[:, cols].astype(F32), g + 1)
            mw = m_ref[g].astype(BF16)
            pb = pooled.astype(BF16)
            dy = dy_ref[:, cols].astype(F32)
            dsc_ref[:, cols] = jnp.sum(dy * _dot(pb, mw), axis=0, keepdims=True)
            dmix = (dy * sc_ref[:, cols]).astype(BF16)
            dm_ref[g] = _dot(pb, dmix, _TN)
            dpool = _dot(dmix, mw, _NT)
            acc = dpool / _pool_counts(s, 2 ** (g + 1))
            for lv in range(g + 1):
                acc = acc + _shift_up(acc, 2 ** lv)
            dp_ref[:, cols] = (acc - dpool).astype(BF16)

    wide = lambda c: pl.BlockSpec((s, _B_WIDTH), lambda i: (0, c))
    mspec = pl.BlockSpec((B_GROUPS, LANES, LANES), lambda i: (0, 0, 0))
    vspec = pl.BlockSpec((1, _B_WIDTH), lambda i: (0, 0))
    return pl.pallas_call(
        body, name="pool_bwd", out_shape=(_sds((s, _B_WIDTH), BF16), _sds((B_GROUPS, LANES, LANES)), _sds((1, _B_WIDTH))),
        grid=(1,), in_specs=[wide(3 * A_WIDTH // _B_WIDTH), mspec, vspec, wide(A_WIDTH // _B_WIDTH)],
        out_specs=(wide(0), mspec, vspec),
        compiler_params=_params(("arbitrary",), VMEM_BIG),
    )(z, mix_w, scale, dycat)


_FF_BLOCKS = D_FF // FF_UNIT


def _ff_spec(s):
    return pl.BlockSpec((2, s, FF_UNIT), lambda j: (0, 0, j))


def _ff_wspecs():
    return [pl.BlockSpec((3, FF_UNIT), lambda j: (0, j)), pl.BlockSpec((3, FF_UNIT), lambda j: (0, _FF_BLOCKS + j))]


_FF_ROWS = 64
_FF_HALO = 16


def _chunk_taps(z_ref, half, c):
    start = pl.multiple_of(c * _FF_ROWS, _FF_ROWS)
    before = pl.multiple_of(jnp.maximum(c * _FF_ROWS - _FF_HALO, 0), _FF_HALO)
    halo = z_ref[half, pl.ds(before, _FF_HALO), :].astype(F32)
    halo = jnp.where(c > 0, halo, 0.0)
    win = jnp.concatenate([halo, z_ref[half, pl.ds(start, _FF_ROWS), :].astype(F32)], axis=0)
    return tuple(pltpu.roll(win, k, axis=0)[_FF_HALO:] for k in (2, 1)) + (win[_FF_HALO:],)


def _fold8(v):
    acc = v[0:8]
    for r in range(8, v.shape[0], 8):
        acc = acc + v[r:r + 8]
    return acc


_FF_CHUNK = 256


def ffn_act_down(zf, conv_w, w_down, name):
    s, d = zf.shape[1], w_down.shape[1]
    nk = D_FF // _FF_CHUNK
    chunk = lambda k: jnp.minimum(k, nk - 1)

    def body(z_ref, wg_ref, wu_ref, wd_ref, a_ref, f_ref, held_ref, acc_ref):
        k = pl.program_id(0)

        @pl.when(k == 0)
        def _():
            held_ref[...] = jnp.zeros_like(held_ref)
            acc_ref[...] = jnp.zeros_like(acc_ref)

        acc_ref[...] += _dot(held_ref[(k + 1) % 2], wd_ref[...])
        g = _conv3(z_ref[0].astype(F32), wg_ref[...])
        u = _conv3(z_ref[1].astype(F32), wu_ref[...])
        act = (g * _sigmoid(g) * u).astype(BF16)
        a_ref[...] = act
        held_ref[k % 2] = act

        @pl.when(k == nk)
        def _():
            f_ref[...] = acc_ref[...].astype(BF16)

    return pl.pallas_call(
        body, name=name, out_shape=(_sds((s, D_FF), BF16), _sds((s, d), BF16)), grid=(nk + 1,),
        in_specs=[pl.BlockSpec((2, s, _FF_CHUNK), lambda k: (0, 0, chunk(k))),
                  pl.BlockSpec((3, _FF_CHUNK), lambda k: (0, chunk(k))),
                  pl.BlockSpec((3, _FF_CHUNK), lambda k: (0, nk + chunk(k))),
                  pl.BlockSpec((_FF_CHUNK, d), lambda k: (jnp.maximum(k - 1, 0), 0))],
        out_specs=(pl.BlockSpec((s, _FF_CHUNK), lambda k: (0, chunk(k))), pl.BlockSpec((s, d), lambda k: (0, 0))),
        scratch_shapes=[pltpu.VMEM((2, s, _FF_CHUNK), BF16), pltpu.VMEM((s, d), F32)],
        compiler_params=_params(("arbitrary",), VMEM_BIG),
    )(zf, conv_w, conv_w, w_down)


def ffn_act_bwd(zf, conv_w, da, name):
    s = zf.shape[1]
    assert s % _FF_ROWS == 0
    nchunks = s // _FF_ROWS

    def body(z_ref, wg_ref, wu_ref, da_ref, dz_ref, dw_ref, dg_ref, du_ref):
        wg, wu = wg_ref[...], wu_ref[...]

        def first(c, acc):
            rows = pl.ds(pl.multiple_of(c * _FF_ROWS, _FF_ROWS), _FF_ROWS)
            tg, tu = _chunk_taps(z_ref, 0, c), _chunk_taps(z_ref, 1, c)
            g = _conv3_taps(tg, wg)
            u = _conv3_taps(tu, wu)
            dav = da_ref[rows, :].astype(F32)
            sg = _sigmoid(g)
            dg = dav * u * (sg * (1.0 + g * (1.0 - sg)))
            du = dav * (g * sg)
            dg_ref[rows, :] = dg
            du_ref[rows, :] = du
            return tuple(a + _fold8(d * t) for a, (d, t) in zip(acc, [(dg, t) for t in tg] + [(du, t) for t in tu]))

        zero = jnp.zeros((8, FF_UNIT), F32)
        acc = lax.fori_loop(0, nchunks, first, (zero,) * 6)
        sums = [jnp.sum(a, axis=0, keepdims=True) for a in acc]
        dw_ref[0] = jnp.concatenate(sums[:3], axis=0)
        dw_ref[1] = jnp.concatenate(sums[3:], axis=0)

        tail = pl.ds(s, _FF_HALO)
        dg_ref[tail, :] = jnp.zeros((_FF_HALO, FF_UNIT), F32)
        du_ref[tail, :] = jnp.zeros((_FF_HALO, FF_UNIT), F32)
        span = _FF_ROWS + _FF_HALO

        def second(c, carry):
            start = pl.multiple_of(c * _FF_ROWS, _FF_ROWS)
            for half, (d_ref, w) in enumerate(((dg_ref, wg), (du_ref, wu))):
                win = d_ref[pl.ds(start, span), :]
                dz = (w[0:1, :] * pltpu.roll(win, span - 2, axis=0)[:_FF_ROWS]
                      + w[1:2, :] * pltpu.roll(win, span - 1, axis=0)[:_FF_ROWS] + w[2:3, :] * win[:_FF_ROWS])
                dz_ref[half, pl.ds(start, _FF_ROWS), :] = dz.astype(BF16)
            return carry

        lax.fori_loop(0, nchunks, second, 0)

    return pl.pallas_call(
        body, name=name, out_shape=(_sds((2, s, D_FF), BF16), _sds((2, 3, D_FF))), grid=(_FF_BLOCKS,),
        in_specs=[_ff_spec(s)] + _ff_wspecs() + [pl.BlockSpec((s, FF_UNIT), lambda j: (0, j))],
        out_specs=(_ff_spec(s), pl.BlockSpec((2, 3, FF_UNIT), lambda j: (0, 0, j))),
        scratch_shapes=[pltpu.VMEM((s + _FF_HALO, FF_UNIT), F32), pltpu.VMEM((s + _FF_HALO, FF_UNIT), F32)],
        compiler_params=_params(("parallel",), VMEM_BIG),
    )(zf, conv_w, conv_w, da)


def _rope(v, cs, s1, s2):
    return v * cs + pltpu.roll(v, LANES - C_ROPE // 2, axis=1) * s1 + pltpu.roll(v, C_ROPE // 2, axis=1) * s2


def _rope_t(dv, cs, s1, s2):
    return dv * cs + pltpu.roll(dv * s1, C_ROPE // 2, axis=1) + pltpu.roll(dv * s2, LANES - C_ROPE // 2, axis=1)


def _kpe_mask(shape):
    lane = lax.broadcasted_iota(jnp.int32, shape, 1)
    return (lane >= C_NOPE) & (lane < C_NOPE + C_ROPE)


def _rms(v, g):
    r = lax.rsqrt(jnp.mean(v * v, axis=-1, keepdims=True) + EPS)
    return v * r, r


def _rms_bwd(dn, xh, r, g):
    dxh = dn * g
    return r * (dxh - xh * jnp.mean(dxh * xh, axis=-1, keepdims=True)), jnp.sum(dn * xh, axis=0, keepdims=True)


_ZQ = C_Q_RANK + C_KV_RANK + HEAD_PAD
_HW = C_HEADS * HEAD_PAD


_MLA_ROWS = 256


def _mla_tiles(s):
    tm = _tile(s, 2 * _MLA_ROWS, 8)
    sub = min(tm, _MLA_ROWS)
    return tm, [slice(r * sub, (r + 1) * sub) for r in range(tm // sub)]


def mla_pre_fwd(z, gq, gkv, wq, wk, wv, cs, s1, s2):
    s = z.shape[0]
    tm, streams = _mla_tiles(s)

    def body(z_ref, gq_ref, gkv_ref, wq_ref, wk_ref, wv_ref, cs_ref, s1_ref, s2_ref, q_ref, k_ref, v_ref):
        for rs in streams:
            zv = z_ref[rs, :].astype(F32)
            cst, s1t, s2t = cs_ref[rs, :], s1_ref[rs, :], s2_ref[rs, :]
            qh, _ = _rms(zv[:, :C_Q_RANK], None)
            qn = (qh * gq_ref[...]).astype(BF16)
            q = _dot(qn, wq_ref[...])
            kh, _ = _rms(zv[:, C_Q_RANK:C_Q_RANK + C_KV_RANK], None)
            kvn = (kh * gkv_ref[...]).astype(BF16)
            k = _dot(kvn, wk_ref[...])
            v_ref[rs, :] = _dot(kvn, wv_ref[...]).astype(BF16)
            kpe = _rope(zv[:, C_Q_RANK + C_KV_RANK:], cst, s1t, s2t)
            for h in range(C_HEADS):
                sl = slice(h * HEAD_PAD, (h + 1) * HEAD_PAD)
                q_ref[rs, sl] = _rope(q[:, sl], cst, s1t, s2t).astype(BF16)
                k_ref[rs, sl] = (k[:, sl] + kpe).astype(BF16)

    full = lambda r, c: pl.BlockSpec((r, c), lambda i: (0, 0))
    hw = _sds((s, _HW), BF16)
    return pl.pallas_call(
        body, name="mla_pre_fwd", out_shape=(hw, hw, hw), grid=(s // tm,),
        in_specs=[_rows(tm, _ZQ), _vec(C_Q_RANK), _vec(C_KV_RANK), full(C_Q_RANK, _HW), full(C_KV_RANK, _HW),
                  full(C_KV_RANK, _HW), _rows(tm, LANES), _rows(tm, LANES), _rows(tm, LANES)],
        out_specs=(_rows(tm, _HW), _rows(tm, _HW), _rows(tm, _HW)),
        compiler_params=_params(("parallel",), VMEM_BIG),
    )(z, gq, gkv, wq, wk, wv, cs, s1, s2)


def mla_pre_bwd(z, gq, gkv, wq, wk, wv, cs, s1, s2, dq, dk, dv):
    s = z.shape[0]
    tm, streams = _mla_tiles(s)

    def body(z_ref, gq_ref, gkv_ref, wq_ref, wk_ref, wv_ref, cs_ref, s1_ref, s2_ref, dq_ref, dk_ref, dv_ref,
             dz_ref, dwq_ref, dwk_ref, dwv_ref, dgq_ref, dgkv_ref):
        @pl.when(pl.program_id(0) == 0)
        def _():
            dwq_ref[...] = jnp.zeros_like(dwq_ref)
            dwk_ref[...] = jnp.zeros_like(dwk_ref)
            dwv_ref[...] = jnp.zeros_like(dwv_ref)
            dgq_ref[...] = jnp.zeros_like(dgq_ref)
            dgkv_ref[...] = jnp.zeros_like(dgkv_ref)

        gqv, gkvv = gq_ref[...], gkv_ref[...]
        for rs in streams:
            zv = z_ref[rs, :].astype(F32)
            cst, s1t, s2t = cs_ref[rs, :], s1_ref[rs, :], s2_ref[rs, :]
            qh, rq = _rms(zv[:, :C_Q_RANK], None)
            qn = (qh * gqv).astype(BF16)
            kh, rk = _rms(zv[:, C_Q_RANK:C_Q_RANK + C_KV_RANK], None)
            kvn = (kh * gkvv).astype(BF16)

            dqv = dq_ref[rs, :].astype(F32)
            dqp = jnp.concatenate(
                [_rope_t(dqv[:, h * HEAD_PAD:(h + 1) * HEAD_PAD], cst, s1t, s2t) for h in range(C_HEADS)], axis=1
            ).astype(BF16)
            dwq_ref[...] += _dot(qn, dqp, _TN)
            dqn = _dot(dqp, wq_ref[...], _NT)
            dql, dgq = _rms_bwd(dqn, qh, rq, gqv)
            dgq_ref[...] += dgq

            dkv = dk_ref[rs, :]
            dkb = dkv.astype(BF16)
            dvb = dv_ref[rs, :].astype(BF16)
            dwk_ref[...] += _dot(kvn, dkb, _TN)
            dwv_ref[...] += _dot(kvn, dvb, _TN)
            dkvn = _dot(dkb, wk_ref[...], _NT) + _dot(dvb, wv_ref[...], _NT)
            dkl, dgkv = _rms_bwd(dkvn, kh, rk, gkvv)
            dgkv_ref[...] += dgkv

            dkpe = dkv[:, :HEAD_PAD]
            for h in range(1, C_HEADS):
                dkpe = dkpe + dkv[:, h * HEAD_PAD:(h + 1) * HEAD_PAD]
            dkpe = _rope_t(jnp.where(_kpe_mask(dkpe.shape), dkpe, 0.0), cst, s1t, s2t)
            dz_ref[rs, :] = jnp.concatenate([dql, dkl, dkpe], axis=1).astype(BF16)

    full = lambda r, c: pl.BlockSpec((r, c), lambda i: (0, 0))
    return pl.pallas_call(
        body, name="mla_pre_bwd",
        out_shape=(_sds((s, _ZQ), BF16), _sds((C_Q_RANK, _HW)), _sds((C_KV_RANK, _HW)), _sds((C_KV_RANK, _HW)),
                   _sds((1, C_Q_RANK)), _sds((1, C_KV_RANK))),
        grid=(s // tm,),
        in_specs=[_rows(tm, _ZQ), _vec(C_Q_RANK), _vec(C_KV_RANK), full(C_Q_RANK, _HW), full(C_KV_RANK, _HW),
                  full(C_KV_RANK, _HW), _rows(tm, LANES), _rows(tm, LANES), _rows(tm, LANES),
                  _rows(tm, _HW), _rows(tm, _HW), _rows(tm, _HW)],
        out_specs=(_rows(tm, _ZQ), full(C_Q_RANK, _HW), full(C_KV_RANK, _HW), full(C_KV_RANK, _HW),
                   _vec(C_Q_RANK), _vec(C_KV_RANK)),
        compiler_params=_params(("arbitrary",), VMEM_BIG),
    )(z, gq, gkv, wq, wk, wv, cs, s1, s2, dq, dk, dv)


_ATT_SCALE = (C_NOPE + C_ROPE) ** -0.5
_NEG = -1e30


def _att_exp(q, k, row0, ends_here):
    sc = _dot(q, k, _NT) * _ATT_SCALE
    tq, nk = sc.shape
    if ends_here:
        last = sc[:, nk - tq:]
        row = lax.broadcasted_iota(jnp.int32, last.shape, 0)
        col = lax.broadcasted_iota(jnp.int32, last.shape, 1)
        last = jnp.where(col <= row, last, _NEG)
        sc = last if nk == tq else jnp.concatenate([sc[:, :nk - tq], last], axis=1)
    else:
        qpos = row0 + lax.broadcasted_iota(jnp.int32, sc.shape, 0)
        kpos = lax.broadcasted_iota(jnp.int32, sc.shape, 1)
        sc = jnp.where(kpos <= qpos, sc, _NEG)
    e = jnp.exp(sc - jnp.max(sc, axis=-1, keepdims=True))
    return e, 1.0 / jnp.sum(e, axis=-1, keepdims=True)


def _causal_cases(i, nq, tq, fn):
    if nq > 8:
        fn(nq * tq, False)
        return
    for blk in range(nq):
        pl.when(i == blk)(functools.partial(fn, (blk + 1) * tq, True))


_FWD_HEADS_PER_STEP = 4
_BWD_HEADS_PER_STEP = 2


def _head_lanes(heads):
    return [slice(h * HEAD_PAD, (h + 1) * HEAD_PAD) for h in range(heads)]


def attn_fwd(q, k, v):
    s = q.shape[0]
    tq = _tile(s, 256, 8)
    nq = s // tq
    heads = _FWD_HEADS_PER_STEP
    wide = heads * HEAD_PAD

    def body(q_ref, k_ref, v_ref, o_ref):
        i = pl.program_id(1)

        def case(nk, ends_here):
            for hd in _head_lanes(heads):
                e, inv = _att_exp(q_ref[:, hd], k_ref[:nk, hd], i * tq, ends_here)
                o_ref[:, hd] = (_dot(e.astype(BF16), v_ref[:nk, hd]) * inv).astype(BF16)

        _causal_cases(i, nq, tq, case)

    qspec = pl.BlockSpec((tq, wide), lambda h, i: (i, h))
    kspec = pl.BlockSpec((s, wide), lambda h, i: (0, h))
    return pl.pallas_call(
        body, name="attn_fwd", out_shape=_sds((s, _HW + _DW), BF16), grid=(C_HEADS // heads, s // tq),
        in_specs=[qspec, kspec, kspec], out_specs=qspec,
        compiler_params=_params(("parallel", "parallel"), VMEM_BIG),
    )(q, k, v)


def attn_bwd(q, k, v, o, do_all):
    s = q.shape[0]
    tq = _tile(s, 256, 8)
    heads = _BWD_HEADS_PER_STEP
    wide = heads * HEAD_PAD

    def body(q_ref, k_ref, v_ref, o_ref, do_ref, dq_ref, dk_ref, dv_ref):
        i = pl.program_id(1)

        @pl.when(i == 0)
        def _():
            dk_ref[...] = jnp.zeros_like(dk_ref)
            dv_ref[...] = jnp.zeros_like(dv_ref)

        def case(nk, ends_here):
            for hd in _head_lanes(heads):
                qv, kv, vv, dov = q_ref[:, hd], k_ref[:nk, hd], v_ref[:nk, hd], do_ref[:, hd]
                e, inv = _att_exp(qv, kv, i * tq, ends_here)
                p = e * inv
                dp = _dot(dov, vv, _NT)
                delta = jnp.sum(dov.astype(F32) * o_ref[:, hd].astype(F32), axis=-1, keepdims=True)
                ds = (p * (dp - delta) * _ATT_SCALE).astype(BF16)
                dq_ref[:, hd] = _dot(ds, kv).astype(BF16)
                dk_ref[:nk, hd] += _dot(ds, qv, _TN)
                dv_ref[:nk, hd] += _dot(p.astype(BF16), dov, _TN)

        _causal_cases(i, s // tq, tq, case)

    qspec = pl.BlockSpec((tq, wide), lambda h, i: (i, h))
    kspec = pl.BlockSpec((s, wide), lambda h, i: (0, h))
    return pl.pallas_call(
        body, name="attn_bwd", out_shape=(_sds((s, _HW), BF16), _sds((s, _HW)), _sds((s, _HW))),
        grid=(C_HEADS // heads, s // tq),
        in_specs=[qspec, kspec, kspec, qspec, qspec], out_specs=(qspec, kspec, kspec),
        compiler_params=_params(("parallel", "arbitrary"), VMEM_BIG),
    )(q, k, v, o, do_all)


_DW = D_GROUPS * LANES


def _tril_bf16(w):
    r = lax.broadcasted_iota(jnp.int32, w.shape, 0)
    c = lax.broadcasted_iota(jnp.int32, w.shape, 1)
    return jnp.where(c <= r, w, 0.0).astype(BF16)


def _sgu_forward(zu, zv, lg, lb, ws_ref, bs):
    u = _gelu(zu)
    v = _gelu(zv)
    mu = jnp.mean(v, axis=-1, keepdims=True)
    vc = v - mu
    rstd = lax.rsqrt(jnp.mean(vc * vc, axis=-1, keepdims=True) + EPS)
    xh = vc * rstd
    vln = (xh * lg + lb).astype(BF16)
    mixed = []
    for g in range(D_GROUPS):
        wg = _tril_bf16(ws_ref[g])
        mixed.append(_dot(wg, vln[:, g * LANES:(g + 1) * LANES]) + bs[:, g:g + 1])
    return u, xh, rstd, vln, jnp.concatenate(mixed, axis=1)


_SGU_CHUNKS = 4


def sgu_fwd(z, lg, lb, ws, bs_t, ycat):
    s = z.shape[0]
    rows = _SGU_CHUNKS * D_CHUNK

    def body(zu_ref, zv_ref, lg_ref, lb_ref, ws_ref, bs_ref, ycat_ref, o_ref):
        del ycat_ref
        for c in range(_SGU_CHUNKS):
            rs = slice(c * D_CHUNK, (c + 1) * D_CHUNK)
            u, _, _, _, mixed = _sgu_forward(zu_ref[rs, :].astype(F32), zv_ref[rs, :].astype(F32), lg_ref[...],
                                             lb_ref[...], ws_ref, bs_ref[...])
            o_ref[rs, :] = (u * mixed).astype(BF16)

    return pl.pallas_call(
        body, name="sgu_fwd", out_shape=_sds(ycat.shape, BF16), grid=(s // rows,),
        in_specs=[pl.BlockSpec((rows, _DW), lambda n: (n, 1)), pl.BlockSpec((rows, _DW), lambda n: (n, 2)),
                  _vec(_DW), _vec(_DW), pl.BlockSpec((D_GROUPS, D_CHUNK, D_CHUNK), lambda n: (0, 0, 0)),
                  pl.BlockSpec((D_CHUNK, LANES), lambda n: (0, 0)), pl.BlockSpec(memory_space=pl.ANY)],
        out_specs=pl.BlockSpec((rows, _DW), lambda n: (n, _HW // _DW)),
        input_output_aliases={6: 0},
        compiler_params=_params(("parallel",)),
    )(z, z, lg, lb, ws, bs_t, ycat)


def sgu_bwd(z, lg, lb, ws, bs_t, dycat, dy_col):
    s = z.shape[0]
    rows = _SGU_CHUNKS * D_CHUNK

    def body(zu_ref, zv_ref, lg_ref, lb_ref, ws_ref, bs_ref, dy_ref, dzu_ref, dzv_ref, dws_ref, dbs_ref, dlg_ref,
             dlb_ref):
        @pl.when(pl.program_id(0) == 0)
        def _():
            dws_ref[...] = jnp.zeros_like(dws_ref)
            dbs_ref[...] = jnp.zeros_like(dbs_ref)
            dlg_ref[...] = jnp.zeros_like(dlg_ref)
            dlb_ref[...] = jnp.zeros_like(dlb_ref)

        lg = lg_ref[...]
        lane = lax.broadcasted_iota(jnp.int32, (D_CHUNK, LANES), 1)
        row = lax.broadcasted_iota(jnp.int32, (D_CHUNK, D_CHUNK), 0)
        colm = lax.broadcasted_iota(jnp.int32, (D_CHUNK, D_CHUNK), 1)
        for c in range(_SGU_CHUNKS):
            rs = slice(c * D_CHUNK, (c + 1) * D_CHUNK)
            zu, zv = zu_ref[rs, :].astype(F32), zv_ref[rs, :].astype(F32)
            u, xh, rstd, vln, mixed = _sgu_forward(zu, zv, lg, lb_ref[...], ws_ref, bs_ref[...])
            dy = dy_ref[rs, :].astype(F32)
            dzu_ref[rs, :] = (dy * mixed * _gelu_grad(zu)).astype(BF16)
            dmix = dy * u
            dvln = []
            dbs = jnp.zeros((D_CHUNK, LANES), F32)
            for g in range(D_GROUPS):
                sl = slice(g * LANES, (g + 1) * LANES)
                dmg = dmix[:, sl]
                dbs = dbs + jnp.where(lane == g, jnp.sum(dmg, axis=-1, keepdims=True), 0.0)
                dmb = dmg.astype(BF16)
                dws_ref[g] += jnp.where(colm <= row, _dot(dmb, vln[:, sl], _NT), 0.0)
                dvln.append(_dot(_tril_bf16(ws_ref[g]), dmb, _TN))
            dbs_ref[...] += dbs
            dvln = jnp.concatenate(dvln, axis=1)
            dlg_ref[...] += jnp.sum(dvln * xh, axis=0, keepdims=True)
            dlb_ref[...] += jnp.sum(dvln, axis=0, keepdims=True)
            dxh = dvln * lg
            dvv = rstd * (dxh - jnp.mean(dxh, axis=-1, keepdims=True)
                          - xh * jnp.mean(dxh * xh, axis=-1, keepdims=True))
            dzv_ref[rs, :] = (dvv * _gelu_grad(zv)).astype(BF16)

    wsspec = pl.BlockSpec((D_GROUPS, D_CHUNK, D_CHUNK), lambda n: (0, 0, 0))
    chunk = lambda cidx: pl.BlockSpec((rows, _DW), lambda n: (n, cidx))
    return pl.pallas_call(
        body, name="sgu_bwd",
        out_shape=(_sds((s, _DW), BF16), _sds((s, _DW), BF16), _sds((D_GROUPS, D_CHUNK, D_CHUNK)),
                   _sds((D_CHUNK, LANES)), _sds((1, _DW)), _sds((1, _DW))),
        grid=(s // rows,),
        in_specs=[chunk(1), chunk(2), _vec(_DW), _vec(_DW), wsspec, pl.BlockSpec((D_CHUNK, LANES), lambda n: (0, 0)),
                  chunk(dy_col)],
        out_specs=(chunk(0), chunk(0), wsspec, pl.BlockSpec((D_CHUNK, LANES), lambda n: (0, 0)), _vec(_DW), _vec(_DW)),
        compiler_params=_params(("arbitrary",)),
    )(z, z, lg, lb, ws, bs_t, dycat)


def ada_mod(c_all, ada_w, ada_b):
    nl, d, n = ada_w.shape
    nb = c_all.shape[0]
    tn = _tile(n, 512)

    def body(c_ref, w_ref, b_ref, o_ref):
        cv = c_ref[...]
        ca = (cv * _sigmoid(cv)).astype(BF16)
        o_ref[...] = _dot(ca, w_ref[...].astype(BF16)) + b_ref[...]

    return pl.pallas_call(
        body, name="ada_mod", out_shape=_sds((nl, nb, n)), grid=(nl, n // tn),
        in_specs=[pl.BlockSpec((nb, d), lambda l, j: (0, 0)), pl.BlockSpec((None, d, tn), lambda l, j: (l, 0, j)),
                  pl.BlockSpec((None, 1, tn), lambda l, j: (l, 0, j))],
        out_specs=pl.BlockSpec((None, nb, tn), lambda l, j: (l, 0, j)),
        compiler_params=_params(("parallel", "parallel")),
    )(c_all, ada_w, ada_b.reshape(nl, 1, n))


_ADAM_BLOCK = 512 * 1024


def _adam_rows(rows, cols):
    if rows * cols <= _ADAM_BLOCK or rows % 8:
        return rows
    return _tile(rows, max(8, _ADAM_BLOCK // cols), 8)


def _adam_update(w, gv, m, v):
    inv_bc1 = 1.0 / (1.0 - ADAM_B1 ** ADAM_STEP)
    inv_bc2 = 1.0 / (1.0 - ADAM_B2 ** ADAM_STEP)
    nm = ADAM_B1 * m + (1.0 - ADAM_B1) * gv
    nv = ADAM_B2 * v + (1.0 - ADAM_B2) * (gv * gv)
    return -ADAM_LR * ((nm * inv_bc1) / (jnp.sqrt(nv * inv_bc2) + ADAM_EPS) + ADAM_WD * w), nm, nv


def adamw(w, g, m, v, name):
    shape = w.shape
    cols = shape[-1]
    rows = w.size // cols
    tr = _adam_rows(rows, cols)

    def body(w_ref, g_ref, m_ref, v_ref, go_ref, d_ref, nm_ref, nv_ref):
        gv = g_ref[...]
        go_ref[...] = gv
        d_ref[...], nm_ref[...], nv_ref[...] = _adam_update(w_ref[...], gv, m_ref[...], v_ref[...])

    spec = pl.BlockSpec((tr, cols), lambda i: (i, 0))
    out = _sds((rows, cols))
    r2 = lambda t: t.reshape(rows, cols)
    res = pl.pallas_call(
        body, name=name, out_shape=(out,) * 4, grid=(rows // tr,),
        in_specs=[spec] * 4, out_specs=(spec,) * 4, compiler_params=_params(("parallel",), VMEM_BIG),
    )(r2(w), r2(g), r2(m), r2(v))
    return tuple(t.reshape(shape) for t in res)


def adamw_ada(w, c_all, dmod, m, v):
    nl, d, n = w.shape
    tr = _adam_rows(d, n)
    pad = 16 - c_all.shape[0]
    c16 = jnp.pad(c_all, ((0, pad), (0, 0)))
    dm16 = jnp.pad(dmod, ((0, 0), (0, pad), (0, 0)))

    def body(w_ref, c_ref, dm_ref, m_ref, v_ref, g_ref, d_ref, nm_ref, nv_ref):
        cv = c_ref[...]
        gv = _dot((cv * _sigmoid(cv)).astype(BF16), dm_ref[...].astype(BF16), _TN)
        g_ref[...] = gv
        d_ref[...], nm_ref[...], nv_ref[...] = _adam_update(w_ref[...], gv, m_ref[...], v_ref[...])

    spec = pl.BlockSpec((None, tr, n), lambda l, i: (l, i, 0))
    out = _sds((nl, d, n))
    return pl.pallas_call(
        body, name="adamw_ada_w", out_shape=(out, out, out, out), grid=(nl, d // tr),
        in_specs=[spec, pl.BlockSpec((16, tr), lambda l, i: (0, i)), pl.BlockSpec((None, 16, n), lambda l, i: (l, 0, 0)),
                  spec, spec],
        out_specs=(spec,) * 4, compiler_params=_params(("parallel", "parallel"), VMEM_BIG),
    )(w, c16, dm16, m, v)


def adamw_small(ws, gs, ms, vs):
    n = len(ws)
    flat = lambda t: t.reshape(-1, t.shape[-1])

    def body(*refs):
        ins, outs = refs[:4 * n], refs[4 * n:]
        for i in range(n):
            w_ref, g_ref, m_ref, v_ref = ins[4 * i:4 * i + 4]
            outs[3 * i][...], outs[3 * i + 1][...], outs[3 * i + 2][...] = _adam_update(
                w_ref[...], g_ref[...], m_ref[...], v_ref[...])

    operands = [flat(t) for quad in zip(ws, gs, ms, vs) for t in quad]
    res = pl.pallas_call(
        body, name="adamw_small", out_shape=tuple(_sds(flat(w).shape) for w in ws for _ in range(3)),
    )(*operands)
    return [(g, res[3 * i].reshape(w.shape), res[3 * i + 1].reshape(w.shape), res[3 * i + 2].reshape(w.shape))
            for i, (w, g) in enumerate(zip(ws, gs))]


def adamw_layers(w, g0, g1, m, v, name):
    _, rows, cols = w.shape
    tr = _adam_rows(rows, cols)

    def body(w_ref, g0_ref, g1_ref, m_ref, v_ref, g_ref, d_ref, nm_ref, nv_ref):
        gv = jnp.where(pl.program_id(0) == 0, g0_ref[...], g1_ref[...])
        g_ref[...] = gv
        d_ref[...], nm_ref[...], nv_ref[...] = _adam_update(w_ref[...], gv, m_ref[...], v_ref[...])

    spec = pl.BlockSpec((None, tr, cols), lambda l, i: (l, i, 0))
    gspec = pl.BlockSpec((tr, cols), lambda l, i: (i, 0))
    out = _sds((2, rows, cols))
    return pl.pallas_call(
        body, name=name, out_shape=(out, out, out, out), grid=(2, rows // tr),
        in_specs=[spec, gspec, gspec, spec, spec], out_specs=(spec,) * 4,
        compiler_params=_params(("parallel", "parallel"), VMEM_BIG),
    )(w, g0, g1, m, v)


def sum8(gathered):
    _, r, _ = gathered.shape
    tr = _tile(r, 512, 8)

    def body(g_ref, o_ref):
        acc = g_ref[0]
        for dev in range(1, N_DEV):
            acc = acc + g_ref[dev]
        o_ref[...] = acc

    return pl.pallas_call(
        body, name="sum8", out_shape=_sds((r, LANES)), grid=(r // tr,),
        in_specs=[pl.BlockSpec((N_DEV, tr, LANES), lambda i: (0, i, 0))], out_specs=pl.BlockSpec((tr, LANES), lambda i: (i, 0)),
        compiler_params=_params(("parallel",)),
    )(gathered)


_SUM_STEPS = 2


def pair_sums(gs, recvs, core, name):
    n = len(gs)

    def body(c_ref, *refs):
        del c_ref
        for i in range(n):
            a_ref, b_ref, o_ref = refs[2 * i], refs[2 * i + 1], refs[2 * n + i]
            o_ref[...] = (a_ref[...].astype(F32) + b_ref[...].astype(F32)).astype(BF16)

    in_specs, out_specs = [], []
    for g in gs:
        half = (None, g.shape[1] // 2, g.shape[2])
        in_specs.append(pl.BlockSpec(half, lambda k, c: (k, c[0], 0)))
        in_specs.append(pl.BlockSpec(half, lambda k, c: (k, 0, 0)))
        out_specs.append(pl.BlockSpec(half, lambda k, c: (k, 0, 0)))
    grid_spec = pltpu.PrefetchScalarGridSpec(num_scalar_prefetch=1, grid=(N_CHIPS,), in_specs=in_specs,
                                             out_specs=tuple(out_specs))
    return list(pl.pallas_call(
        body, name=name, out_shape=tuple(_sds((N_CHIPS, g.shape[1] // 2, g.shape[2]), BF16) for g in gs),
        grid_spec=grid_spec, compiler_params=_params(("parallel",), VMEM_BIG),
    )(core.reshape(1).astype(jnp.int32), *[t for pair in zip(gs, recvs) for t in pair]))


def chip_sums(pairs, recvs, chip, core, name):
    n = len(pairs)
    trs = [p.shape[1] // _SUM_STEPS for p in pairs]

    def body(p_ref, *refs):
        del p_ref
        for i in range(n):
            own_ref, r_ref, o_ref = refs[2 * i], refs[2 * i + 1], refs[2 * n + i]
            acc = own_ref[...].astype(F32)
            for j in range(N_CHIPS - 1):
                acc = acc + r_ref[j].astype(F32)
            o_ref[...] = acc

    in_specs, out_specs = [], []
    for p, tr in zip(pairs, trs):
        cols = p.shape[2]
        in_specs.append(pl.BlockSpec((None, tr, cols), lambda s, q: (q[0], s, 0)))
        in_specs.append(pl.BlockSpec((N_CHIPS - 1, tr, cols), lambda s, q: (0, s, 0)))
        out_specs.append(pl.BlockSpec((None, tr, cols), lambda s, q: (q[1], s, 0)))
    grid_spec = pltpu.PrefetchScalarGridSpec(num_scalar_prefetch=1, grid=(_SUM_STEPS,), in_specs=in_specs,
                                             out_specs=tuple(out_specs))
    return list(pl.pallas_call(
        body, name=name, out_shape=tuple(_sds((2,) + p.shape[1:]) for p in pairs), grid_spec=grid_spec,
        compiler_params=_params(("parallel",)),
    )(jnp.stack([chip, core]).astype(jnp.int32), *[t for pair in zip(pairs, recvs) for t in pair]))


def _place():
    return lax.axis_index("x"), lax.axis_index("y"), lax.axis_index("c")


def _other_chips(x, y):
    return [(x, 1 - y), (1 - x, y), (1 - x, 1 - y)]


_HBM = pl.BlockSpec(memory_space=pltpu.HBM)


def all_gather8(v, name, after=()):
    m, n = v.shape

    def body(x_ref, *refs):
        out_ref, send_sems, recv_sems, local_sem = refs[len(after):]
        x, y, c = _place()
        me, sibling = (x, y, c), (x, y, 1 - c)
        chips = _other_chips(x, y)

        def rows(px, py, pc):
            return out_ref.at[pl.ds((4 * px + 2 * py + pc) * m, m), :]

        def copy(k, block, to, src=None):
            return pltpu.make_async_remote_copy(
                src_ref=rows(*block) if src is None else src, dst_ref=rows(*block),
                send_sem=send_sems.at[k], recv_sem=recv_sems.at[k], device_id=to, device_id_type=MESH)

        mine = pltpu.make_async_copy(x_ref, rows(*me), local_sem)
        mine.start()
        first = [copy(0, me, sibling, src=x_ref)]
        first += [copy(1 + j, me, (*chip, c), src=x_ref) for j, chip in enumerate(chips)]
        for cp in first:
            cp.start()
        passed = [copy(4 + j, (*chip, c), sibling) for j, chip in enumerate(chips)]
        for j, chip in enumerate(chips):
            copy(1 + j, (*chip, c), me).wait_recv()
            passed[j].start()
        copy(0, sibling, me).wait_recv()
        for j, chip in enumerate(chips):
            copy(4 + j, (*chip, 1 - c), me).wait_recv()
        for cp in first + passed:
            cp.wait_send()
        mine.wait()

    return pl.pallas_call(
        body, name=name, out_shape=_sds((N_DEV * m, n), v.dtype),
        in_specs=[pl.BlockSpec(memory_space=pltpu.VMEM)] + [pl.BlockSpec(memory_space=pl.ANY)] * len(after),
        out_specs=pl.BlockSpec(memory_space=pltpu.VMEM),
        scratch_shapes=[pltpu.SemaphoreType.DMA((7,)), pltpu.SemaphoreType.DMA((7,)), pltpu.SemaphoreType.DMA],
        compiler_params=_params(None, VMEM_BIG),
    )(v, *after)


def _comm_call(body, name, ins, out_shapes, nsem, aliases=None):
    return pl.pallas_call(
        body, name=name, out_shape=tuple(out_shapes), in_specs=[_HBM] * len(ins), out_specs=tuple([_HBM] * len(out_shapes)),
        scratch_shapes=[pltpu.SemaphoreType.DMA((nsem,)), pltpu.SemaphoreType.DMA((nsem,))],
        input_output_aliases=aliases or {},
    )(*ins)


def _remote(src, dst, send_sems, recv_sems, k, to):
    return pltpu.make_async_remote_copy(src_ref=src, dst_ref=dst, send_sem=send_sems.at[k], recv_sem=recv_sems.at[k],
                                        device_id=to, device_id_type=MESH)


def _half(core, rh):
    return pl.ds(pl.multiple_of(core * rh, 16), rh)


def swap_halves(gs, name):
    n = len(gs)

    def body(*refs):
        ins, outs, (send_sems, recv_sems) = refs[:n], refs[n:2 * n], refs[2 * n:]
        x, y, c = _place()
        copies = []
        for i in range(n):
            theirs = _half(1 - c, ins[i].shape[1] // 2)
            cp = _remote(ins[i].at[:, theirs], outs[i], send_sems, recv_sems, i, (x, y, 1 - c))
            cp.start()
            copies.append(cp)
        for cp in copies:
            cp.wait()

    return _comm_call(body, name, gs, [_sds((g.shape[0], g.shape[1] // 2, g.shape[2]), g.dtype) for g in gs], n)


def join_halves(bufs, name):
    n = len(bufs)

    def body(*refs):
        ins, outs, (send_sems, recv_sems) = refs[:n], refs[n:2 * n], refs[2 * n:]
        x, y, c = _place()
        copies = []
        for i in range(n):
            cp = _remote(ins[i].at[c], outs[i].at[c], send_sems, recv_sems, i, (x, y, 1 - c))
            cp.start()
            copies.append(cp)
        for i in range(n):
            theirs = outs[i].at[1 - c]
            _remote(theirs, theirs, send_sems, recv_sems, i, (x, y, 1 - c)).wait_recv()
        for cp in copies:
            cp.wait_send()

    return _comm_call(body, name, bufs, [_sds(b.shape, b.dtype) for b in bufs], n, {i: i for i in range(n)})


def forward_halves(lands, name):
    n = len(lands)

    def body(*refs):
        ins, outs, (send_sems, recv_sems) = refs[:n], refs[n:2 * n], refs[2 * n:]
        x, y, c = _place()
        sibling = (x, y, 1 - c)
        chips = _other_chips(x, y)
        copies = []
        for i in range(n):
            mine = _half(c, ins[i].shape[1] // 2)
            for j, (px, py) in enumerate(chips):
                cp = _remote(ins[i].at[2 * px + py, mine], outs[i].at[2 * px + py, mine], send_sems, recv_sems, 3 * i + j, sibling)
                cp.start()
                copies.append(cp)
        for i in range(n):
            theirs = _half(1 - c, ins[i].shape[1] // 2)
            for j, (px, py) in enumerate(chips):
                landed = outs[i].at[2 * px + py, theirs]
                _remote(landed, landed, send_sems, recv_sems, 3 * i + j, sibling).wait_recv()
        for cp in copies:
            cp.wait_send()

    return _comm_call(body, name, lands, [_sds(b.shape, b.dtype) for b in lands], 3 * n, {i: i for i in range(n)})


_SEM = pl.BlockSpec(memory_space=pltpu.SEMAPHORE)
_EFFECT = pltpu.SideEffectType.DATAFLOW_SIDE_EFFECTING


def _gather_copies(srcs, lands, send_sems, recv_sems):
    x, y, c = _place()
    copies = []
    for i in range(len(srcs)):
        mine = _half(c, srcs[i].shape[0] // 2)
        for j, chip in enumerate(_other_chips(x, y)):
            copies.append(_remote(srcs[i].at[mine], lands[i].at[2 * x + y, mine], send_sems, recv_sems, 3 * i + j, (*chip, c)))
    return copies


def _exchange_copies(srcs, lands, send_sems, recv_sems):
    x, y, c = _place()
    copies = []
    for i in range(len(srcs)):
        for j, (px, py) in enumerate(_other_chips(x, y)):
            copies.append(_remote(srcs[i].at[2 * px + py], lands[i].at[j], send_sems, recv_sems, 3 * i + j, (px, py, c)))
    return copies


def _everyone_copies(srcs, lands, send_sems, recv_sems):
    x, y, c = _place()
    flip = lambda v, b: 1 - v if b else v
    dst = lands[0].at[4 * x + 2 * y + c]
    return [_remote(srcs[0], dst, send_sems, recv_sems, j - 1, (flip(x, j & 4), flip(y, j & 2), flip(c, j & 1)))
            for j in range(1, N_DEV)]


GATHER = (_gather_copies, 3)
EXCHANGE = (_exchange_copies, 3)
EVERYONE = (_everyone_copies, N_DEV - 1)


def split_start(name, plan, srcs, land_shapes, after=()):
    copies_fn, per_source = plan
    n, m, k = len(srcs), len(land_shapes), len(after)
    ncopies = per_source * n

    def body(*refs):
        src_refs, land_refs = refs[:n], refs[n:n + m]
        send_sems, recv_sems = refs[n + m + k], refs[n + m + k + 1]
        token = refs[-1]
        for cp in copies_fn(src_refs, land_refs, send_sems, recv_sems):
            cp.start()
        token[...] = jnp.zeros_like(token)

    hbm = lambda s: pltpu.HBM(tuple(s.shape), s.dtype)
    outs = pl.pallas_call(
        body, name=name,
        out_shape=(pltpu.SemaphoreType.DMA((ncopies,)), pltpu.SemaphoreType.DMA((ncopies,)), *[hbm(s) for s in srcs],
                   *[hbm(s) for s in land_shapes], _sds((8, LANES))),
        in_specs=[_HBM] * (n + m) + [pl.BlockSpec(memory_space=pl.ANY)] * k,
        out_specs=(_SEM, _SEM, *([_HBM] * (n + m)), pl.BlockSpec(memory_space=pltpu.VMEM)),
        input_output_aliases={i: 2 + i for i in range(n + m)},
        compiler_params=pltpu.CompilerParams(has_side_effects=_EFFECT),
    )(*[pltpu.with_memory_space_constraint(s, pltpu.HBM) for s in srcs],
      *[pltpu.with_memory_space_constraint(lax.empty(tuple(s.shape), s.dtype), pltpu.HBM) for s in land_shapes], *after)
    handle = (outs[0], outs[1], list(outs[2:2 + n]), list(outs[2 + n:2 + n + m]))
    return handle, outs[-1][0, 0]


def split_wait(name, plan, handle, after):
    copies_fn, _ = plan
    send_sems, recv_sems, srcs, lands = handle
    n, m = len(srcs), len(lands)
    after = list(after) if isinstance(after, (list, tuple)) else [after]

    def body(*refs):
        src_refs, land_refs = refs[:n], refs[n:n + m]
        for cp in copies_fn(src_refs, land_refs, refs[n + m], refs[n + m + 1]):
            cp.wait_send()
            cp.wait_recv()

    hbm = lambda s: pltpu.HBM(tuple(s.shape), s.dtype)
    outs = pl.pallas_call(
        body, name=name, out_shape=tuple(hbm(s) for s in srcs + lands),
        in_specs=[_HBM] * (n + m) + [_SEM, _SEM] + [pl.BlockSpec(memory_space=pl.ANY)] * len(after),
        out_specs=tuple([_HBM] * (n + m)), input_output_aliases={i: i for i in range(n + m)},
        compiler_params=pltpu.CompilerParams(has_side_effects=_EFFECT),
    )(*srcs, *lands, send_sems, recv_sems, *after)
    return list(outs[:n]), list(outs[n:])


def chip_major(w, groups=N_CHIPS):
    r, c = w.shape
    return w.reshape(r, groups, c // groups).transpose(1, 0, 2)


def from_chip_major(w):
    g, r, c = w.shape
    return w.transpose(1, 0, 2).reshape(r, g * c)


def _cd_in_pad(w):
    a = C_Q_RANK + C_KV_RANK
    z = lambda n: jnp.zeros((w.shape[0], n), w.dtype)
    return jnp.concatenate([w[:, :a], z(C_NOPE), w[:, a:a + C_ROPE], z(HEAD_PAD - C_NOPE - C_ROPE), w[:, a + C_ROPE:]], axis=1)


def _cd_in_unpad(w):
    a = C_Q_RANK + C_KV_RANK
    return jnp.concatenate([w[:, :a], w[:, a + C_NOPE:a + C_NOPE + C_ROPE], w[:, a + HEAD_PAD:]], axis=1)


def _pad_heads(w, width):
    r = w.shape[0]
    w = w.reshape(r, C_HEADS, width)
    return jnp.pad(w, ((0, 0), (0, 0), (0, HEAD_PAD - width))).reshape(r, _HW)


def _unpad_heads(w, width):
    r = w.shape[0]
    return w.reshape(r, C_HEADS, HEAD_PAD)[:, :, :width].reshape(r, C_HEADS * width)


def prepare_weights(p):
    q = dict(p)
    q["cd_w_in"] = _cd_in_pad(p["cd_w_in"])
    q["c_w_uq"] = _pad_heads(p["c_w_uq"], C_NOPE + C_ROPE)
    ukv = p["c_w_ukv"].reshape(C_KV_RANK, C_HEADS, C_NOPE + C_V)
    q["c_w_uk"] = _pad_heads(ukv[:, :, :C_NOPE].reshape(C_KV_RANK, -1), C_NOPE)
    q["c_w_uv"] = _pad_heads(ukv[:, :, C_NOPE:].reshape(C_KV_RANK, -1), C_V)
    wo = p["cd_w_out"]
    att_rows = jnp.pad(wo[:C_HEADS * C_V].reshape(C_HEADS, C_V, D_MODEL), ((0, 0), (0, HEAD_PAD - C_V), (0, 0)))
    q["cd_w_out"] = jnp.concatenate([att_rows.reshape(_HW, D_MODEL), wo[C_HEADS * C_V:]], axis=0)
    return q


def unprepare_grads(g):
    q = dict(g)
    q["cd_w_in"] = _cd_in_unpad(g["cd_w_in"])
    q["c_w_uq"] = _unpad_heads(g["c_w_uq"], C_NOPE + C_ROPE)
    uk = g.pop("c_w_uk").reshape(C_KV_RANK, C_HEADS, HEAD_PAD)[:, :, :C_NOPE]
    uv = g.pop("c_w_uv").reshape(C_KV_RANK, C_HEADS, HEAD_PAD)[:, :, :C_V]
    q.pop("c_w_uk", None)
    q.pop("c_w_uv", None)
    q["c_w_ukv"] = jnp.concatenate([uk, uv], axis=-1).reshape(C_KV_RANK, C_HEADS * (C_NOPE + C_V))
    wo = g["cd_w_out"]
    att = wo[:_HW].reshape(C_HEADS, HEAD_PAD, D_MODEL)[:, :C_V].reshape(C_HEADS * C_V, D_MODEL)
    q["cd_w_out"] = jnp.concatenate([att, wo[_HW:]], axis=0)
    return q


def rope_tables(positions):
    half = C_ROPE // 2
    inv_freq = ROPE_THETA ** (-jnp.arange(half, dtype=F32) / half)
    ang = positions.astype(F32)[:, None] * inv_freq
    cos, sin = jnp.cos(ang), jnp.sin(ang)
    s = positions.shape[0]
    z = lambda n: jnp.zeros((s, n), F32)
    cs = jnp.concatenate([jnp.ones((s, C_NOPE), F32), cos, cos, z(HEAD_PAD - C_NOPE - C_ROPE)], axis=1)
    s1 = jnp.concatenate([z(C_NOPE), -sin, z(HEAD_PAD - C_NOPE - half)], axis=1)
    s2 = jnp.concatenate([z(C_NOPE + half), sin, z(HEAD_PAD - C_NOPE - C_ROPE)], axis=1)
    return cs, s1, s2


_UP_COLS = 2 * D_FF // N_CHIPS


def ffn_fwd(h2, w, l, late_down=None):
    zf = matmul(h2, w["ffn_w_up"][l], "nn", BF16, f"ffn_up{l}", gb=N_CHIPS, go=2, tn=_UP_COLS)
    if late_down is not None:
        late_down(zf)
    a, f = ffn_act_down(zf, w["ffn_conv_w"][l], w["ffn_w_down"][l], f"ffn_act_down{l}")
    return f, (zf, a)


def ffn_bwd(df, h2, saved, w, l):
    zf, a = saved
    da = matmul(df, w["ffn_w_down"][l], "nt", BF16, f"ffn_down_dx{l}", tn=D_FF // 2)
    d_down = matmul(a, df, "tn", BF16, f"ffn_down_dw{l}", tm=D_FF // 2)
    dzf, d_conv = ffn_act_bwd(zf, w["ffn_conv_w"][l], da, f"ffn_act_bwd{l}")
    dh2 = matmul(dzf, w["ffn_w_up"][l], "nt", BF16, f"ffn_up_dx{l}", ga=2, gb=N_CHIPS, tk=_UP_COLS, tn=D_MODEL)
    d_up = matmul(h2, dzf, "tn", BF16, f"ffn_up_dw{l}", gb=2, go=N_CHIPS, tn=_UP_COLS)
    d_conv = d_conv.transpose(1, 0, 2).reshape(3, 2 * D_FF)
    return dh2, dict(ffn_w_down=d_down, ffn_conv_w=d_conv, ffn_w_up=d_up)


def mixer0_fwd(h, w):
    z = matmul(h, w["ab_w_in"], "nn", BF16, "ab_in", gb=N_CHIPS)
    ycat = pool_fwd(z, w["b_mix_w"], w["b_scale"], gconv_fwd(z, w["a_conv_w"]))
    y = matmul(ycat, w["ab_w_out"], "nn", BF16, "ab_out", tn=D_MODEL)
    return y, (z, ycat)


def mixer0_bwd(dy, h, saved, w):
    z, ycat = saved
    grads = {}
    dycat = matmul(dy, w["ab_w_out"], "nt", BF16, "ab_out_dx")
    grads["ab_w_out"] = matmul(ycat, dy, "tn", BF16, "ab_out_dw")
    db, dc, da, d_conv = gconv_bwd(z, w["a_conv_w"], dycat)
    dp, d_mix, d_scale = pool_bwd(z, w["b_mix_w"], w["b_scale"], dycat)
    dz = jnp.concatenate([db, dc, da, dp], axis=1)
    dh = matmul(dz, w["ab_w_in"], "nt", BF16, "ab_in_dx", gb=N_CHIPS, tn=D_MODEL)
    grads["ab_w_in"] = matmul(h, dz, "tn", BF16, "ab_in_dw", go=N_CHIPS)
    grads.update(a_conv_w=d_conv, b_mix_w=d_mix, b_scale=d_scale)
    return dh, grads


def mixer1_fwd(h, ropes, w):
    cs, s1, s2 = ropes
    z = matmul(h, w["cd_w_in"], "nn", BF16, "cd_in")
    bs_t = jnp.pad(w["d_b_s"].T, ((0, 0), (0, LANES - D_GROUPS)))
    qh, kh, vh = mla_pre_fwd(z, w["c_q_norm_g"], w["c_kv_norm_g"], w["c_w_uq"], w["c_w_uk"], w["c_w_uv"], cs, s1, s2)
    ycat = sgu_fwd(z, w["d_ln_g"], w["d_ln_b"], w["d_w_s"], bs_t, attn_fwd(qh, kh, vh))
    y = matmul(ycat, w["cd_w_out"], "nn", BF16, "cd_out", tn=D_MODEL)
    return y, (z, bs_t, qh, kh, vh, ycat)


def mixer1_bwd(dy, h, saved, ropes, w):
    cs, s1, s2 = ropes
    z, bs_t, qh, kh, vh, ycat = saved
    grads = {}
    dycat = matmul(dy, w["cd_w_out"], "nt", BF16, "cd_out_dx")
    grads["cd_w_out"] = matmul(ycat, dy, "tn", BF16, "cd_out_dw")
    dqh, dkh, dvh = attn_bwd(qh, kh, vh, ycat, dycat)
    dzq, d_uq, d_uk, d_uv, d_gq, d_gkv = mla_pre_bwd(
        z, w["c_q_norm_g"], w["c_kv_norm_g"], w["c_w_uq"], w["c_w_uk"], w["c_w_uv"], cs, s1, s2, dqh, dkh, dvh)
    dzu, dzv, d_ws, d_bs, d_lg, d_lb = sgu_bwd(z, w["d_ln_g"], w["d_ln_b"], w["d_w_s"], bs_t, dycat, _HW // _DW)
    dz = jnp.concatenate([dzq, dzu, dzv], axis=1)
    dh = matmul(dz, w["cd_w_in"], "nt", BF16, "cd_in_dx", tn=D_MODEL)
    grads["cd_w_in"] = matmul(h, dz, "tn", BF16, "cd_in_dw")
    grads.update(c_w_uq=d_uq, c_w_uk=d_uk, c_w_uv=d_uv, c_q_norm_g=d_gq, c_kv_norm_g=d_gkv, d_w_s=d_ws,
                 d_b_s=d_bs[:, :D_GROUPS].T, d_ln_g=d_lg, d_ln_b=d_lb)
    return dh, grads


class StepHooks:
    def weights(self, stage, after):
        pass

    def gradients(self, stage, grads, after):
        return 0.0


def run_step(x, tgt, mod, ropes, w, hooks):
    sh1, sc1, g1, sh2, sc2, g2 = range(N_MOD)
    mods = mod.reshape(2, 1, N_MOD * D_MODEL)
    n1 = w["norm1_g"].reshape(2, 1, D_MODEL)
    n2 = w["norm2_g"].reshape(2, 1, D_MODEL)
    final_g = Vec(w["final_norm_g"].reshape(1, 1, D_MODEL), 0, 0)

    hooks.weights("mix0", mod)
    h0 = modnorm_fwd(x, Vec(n1, 0, 0), Vec(mods, 0, sc1), Vec(mods, 0, sh1), "modnorm_0")
    y0, mix0 = mixer0_fwd(h0, w)
    x1, h1 = resid_modnorm_fwd(x, y0, Vec(mods, 0, g1), Vec(n2, 0, 0), Vec(mods, 0, sc2), Vec(mods, 0, sh2), "resid_modnorm_1")
    hooks.weights("up0", x1)
    f0, ffn0 = ffn_fwd(h1, w, 0, lambda act: hooks.weights("down0", act))
    x2, h2 = resid_modnorm_fwd(x1, f0, Vec(mods, 0, g2), Vec(n1, 1, 0), Vec(mods, 1, sc1), Vec(mods, 1, sh1), "resid_modnorm_2")
    hooks.weights("mix1", x2)
    y1, mix1 = mixer1_fwd(h2, ropes, w)
    x3, h3 = resid_modnorm_fwd(x2, y1, Vec(mods, 1, g1), Vec(n2, 1, 0), Vec(mods, 1, sc2), Vec(mods, 1, sh2), "resid_modnorm_3")
    hooks.weights("ffn1", x3)
    f1, ffn1 = ffn_fwd(h3, w, 1)
    dres, d_final, loss, df1, dg2b = final_fused(x3, f1, Vec(mods, 1, g2), final_g, tgt)

    dh3, gf1 = ffn_bwd(df1, h3, ffn1, w, 1)
    late = mods + hooks.gradients("ffn1", gf1, dh3)
    dres, dsh2b, dsc2b, dn2b, dy1, dg1b = norm_gate_bwd(
        x3, dh3, Vec(n2, 1, 0), Vec(late, 1, sc2), dres, y1, Vec(late, 1, g1), "norm_gate_bwd_3")
    dh2, gm1 = mixer1_bwd(dy1, h2, mix1, ropes, w)
    late = mods + hooks.gradients("mix1", gm1, dh2)
    dres, dsh1b, dsc1b, dn1b, df0, dg2a = norm_gate_bwd(
        x2, dh2, Vec(n1, 1, 0), Vec(late, 1, sc1), dres, f0, Vec(late, 0, g2), "norm_gate_bwd_2")
    dh1, gf0 = ffn_bwd(df0, h1, ffn0, w, 0)
    late = mods + hooks.gradients("ffn0", gf0, dh1)
    dres, dsh2a, dsc2a, dn2a, dy0, dg1a = norm_gate_bwd(
        x1, dh1, Vec(n2, 0, 0), Vec(late, 0, sc2), dres, y0, Vec(late, 0, g1), "norm_gate_bwd_1")
    dh0, gm0 = mixer0_bwd(dy0, h0, mix0, w)
    late = mods + hooks.gradients("mix0", gm0, dh0)
    grad_x, dsh1a, dsc1a, dn1a = norm_bwd(x, dh0, Vec(n1, 0, 0), Vec(late, 0, sc1), dres, "norm_bwd_0")

    dmod = jnp.concatenate([jnp.concatenate([dsh1a, dsc1a, dg1a, dsh2a, dsc2a, dg2a], axis=1),
                            jnp.concatenate([dsh1b, dsc1b, dg1b, dsh2b, dsc2b, dg2b], axis=1)], axis=0)
    norms = dict(norm1_g=jnp.concatenate([dn1a, dn1b], axis=0), norm2_g=jnp.concatenate([dn2a, dn2b], axis=0),
                 final_norm_g=d_final)
    return loss, grad_x, dmod, dict(mix0=gm0, ffn0=gf0, mix1=gm1, ffn1=gf1, norms=norms)


def merge_grads(by_stage):
    grads = {**by_stage["mix0"], **by_stage["mix1"], **by_stage["norms"]}
    for k in ("ffn_w_down", "ffn_w_up"):
        grads[k] = [by_stage["ffn0"][k], by_stage["ffn1"][k]]
    grads["ffn_conv_w"] = jnp.stack([by_stage["ffn0"]["ffn_conv_w"], by_stage["ffn1"]["ffn_conv_w"]])
    return grads


_WEIGHTS = ("ada_w", "ada_b", "norm1_g", "norm2_g", "ab_w_in", "a_conv_w", "b_mix_w", "b_scale", "ab_w_out", "cd_w_in",
            "c_q_norm_g", "c_w_uq", "c_kv_norm_g", "c_w_ukv", "d_ln_g", "d_ln_b", "d_w_s", "d_b_s", "cd_w_out",
            "ffn_w_up", "ffn_conv_w", "ffn_w_down", "final_norm_g")
_INPUTS = ("x", "c", "positions") + _WEIGHTS + ("loss_target",) + tuple("m_" + n for n in _WEIGHTS) + tuple(
    "v_" + n for n in _WEIGHTS)

def _pack_rows(parts, rows, dtype):
    flat = jnp.concatenate([p.reshape(-1).astype(dtype) for p in parts])
    return jnp.pad(flat, (0, rows * LANES - flat.shape[0])).reshape(rows, LANES)


def _rows_major(w):
    r, c = w.shape
    return w.reshape(N_CHIPS, r // N_CHIPS, c)


def start_gather(shards, tag, after=()):
    lands = [_sds((N_CHIPS,) + s.shape, s.dtype) for s in shards]
    return split_start("gather_start_" + tag, GATHER, shards, lands, after)


def finish_gather(handle, chip, tag, after):
    shards, lands = split_wait("gather_wait_" + tag, GATHER, handle, after)
    lands = forward_halves(lands, "gather_forward_" + tag)
    return [lax.dynamic_update_index_in_dim(o, s, chip, 0) for o, s in zip(lands, shards)]


def start_reduce(gs, core, tag):
    recv = swap_halves(gs, "swap_halves_" + tag)
    pairs = pair_sums(gs, recv, core, "pair_sums_" + tag)
    lands = [_sds((N_CHIPS - 1,) + p.shape[1:], p.dtype) for p in pairs]
    return split_start("exchange_start_" + tag, EXCHANGE, pairs, lands)


def finish_reduce(handle, chip, core, tag, after):
    pairs, others = split_wait("exchange_wait_" + tag, EXCHANGE, handle, after)
    halves = chip_sums(pairs, others, chip, core, "chip_sums_" + tag)
    full = join_halves(halves, "join_halves_" + tag)
    return [f.reshape(f.shape[1] * 2, f.shape[2]) for f in full]


_SMALL_SHARDED = (("a_conv_w", (3, 128), 1), ("c_q_norm_g", (1, 64), 1), ("d_ln_g", (1, 128), 1), ("d_ln_b", (1, 128), 1),
                  ("ffn_conv_w", (2, 3, 2 * D_FF // N_CHIPS), 2))
_SMALL_GRADS = (("norm1_g", (2, D_MODEL)), ("norm2_g", (2, D_MODEL)), ("b_mix_w", (4, 128, 128)), ("b_scale", (1, 512)),
                ("c_kv_norm_g", (1, 128)), ("d_w_s", (4, 128, 128)), ("d_b_s", (4, 128)), ("final_norm_g", (1, D_MODEL)),
                ("a_conv_w", (3, 512)), ("c_q_norm_g", (1, 256)), ("d_ln_g", (1, 512)), ("d_ln_b", (1, 512)),
                ("ffn_conv_w", (2, 3, 2 * D_FF)))


def _size(shape):
    n = 1
    for d in shape:
        n *= d
    return n


def kernel(x, c, positions, ada_w, ada_b, norm1_g, norm2_g, ab_w_in, a_conv_w, b_mix_w, b_scale, ab_w_out, cd_w_in, c_q_norm_g, c_w_uq, c_kv_norm_g, c_w_ukv, d_ln_g, d_ln_b, d_w_s, d_b_s, cd_w_out, ffn_w_up, ffn_conv_w, ffn_w_down, final_norm_g, loss_target, m_ada_w, m_ada_b, m_norm1_g, m_norm2_g, m_ab_w_in, m_a_conv_w, m_b_mix_w, m_b_scale, m_ab_w_out, m_cd_w_in, m_c_q_norm_g, m_c_w_uq, m_c_kv_norm_g, m_c_w_ukv, m_d_ln_g, m_d_ln_b, m_d_w_s, m_d_b_s, m_cd_w_out, m_ffn_w_up, m_ffn_conv_w, m_ffn_w_down, m_final_norm_g, v_ada_w, v_ada_b, v_norm1_g, v_norm2_g, v_ab_w_in, v_a_conv_w, v_b_mix_w, v_b_scale, v_ab_w_out, v_cd_w_in, v_c_q_norm_g, v_c_w_uq, v_c_kv_norm_g, v_c_w_ukv, v_d_ln_g, v_d_ln_b, v_d_w_s, v_d_b_s, v_cd_w_out, v_ffn_w_up, v_ffn_conv_w, v_ffn_w_down, v_final_norm_g):
    args = (x, c, positions, ada_w, ada_b, norm1_g, norm2_g, ab_w_in, a_conv_w, b_mix_w, b_scale, ab_w_out, cd_w_in, c_q_norm_g, c_w_uq, c_kv_norm_g, c_w_ukv, d_ln_g, d_ln_b, d_w_s, d_b_s, cd_w_out, ffn_w_up, ffn_conv_w, ffn_w_down, final_norm_g, loss_target, m_ada_w, m_ada_b, m_norm1_g, m_norm2_g, m_ab_w_in, m_a_conv_w, m_b_mix_w, m_b_scale, m_ab_w_out, m_cd_w_in, m_c_q_norm_g, m_c_w_uq, m_c_kv_norm_g, m_c_w_ukv, m_d_ln_g, m_d_ln_b, m_d_w_s, m_d_b_s, m_cd_w_out, m_ffn_w_up, m_ffn_conv_w, m_ffn_w_down, m_final_norm_g, v_ada_w, v_ada_b, v_norm1_g, v_norm2_g, v_ab_w_in, v_a_conv_w, v_b_mix_w, v_b_scale, v_ab_w_out, v_cd_w_in, v_c_q_norm_g, v_c_w_uq, v_c_kv_norm_g, v_c_w_ukv, v_d_ln_g, v_d_ln_b, v_d_w_s, v_d_b_s, v_cd_w_out, v_ffn_w_up, v_ffn_conv_w, v_ffn_w_down, v_final_norm_g)
    a = dict(zip(_INPUTS, args, strict=True))
    xi, yi, ci = _place()
    chip = 2 * xi + yi
    dev = 4 * xi + 2 * yi + ci
    x = a["x"][0]
    tgt = a["loss_target"][0]

    bf = lambda t: t.astype(BF16)
    mix0_handle, tok = start_gather([bf(a["ab_w_in"][0]), bf(a["ab_w_out"][0])], "mix0")
    up0_16, down0_16, up1_16, down1_16 = [bf(a[n][l]) for l in (0, 1) for n in ("ffn_w_up", "ffn_w_down")]
    mix1_16 = [bf(a[n][0]) for n in ("cd_w_in", "c_w_uq", "c_w_ukv", "cd_w_out")]

    small_parts = [a["c"] + tok] + [a[n] for n, _, _ in _SMALL_SHARDED]
    rows1 = -(-sum(p.size for p in small_parts) // LANES // 8) * 8
    g1 = all_gather8(_pack_rows(small_parts, rows1, F32), "gather_small",
                     [up0_16, down0_16, up1_16, down1_16, mix1_16[0], mix1_16[3]]).reshape(N_DEV, rows1 * LANES)
    c_all = g1[:, :D_MODEL]
    per_chip = g1[0::2]
    small_full = {}
    off = D_MODEL
    for n, shp, axis in _SMALL_SHARDED:
        piece = per_chip[:, off:off + _size(shp)].reshape((N_CHIPS,) + shp)
        small_full[n] = jnp.concatenate([piece[k] for k in range(N_CHIPS)], axis=axis)
        off += _size(shp)

    merge = lambda t: t.reshape(t.shape[0] * t.shape[1], t.shape[2])
    w = dict(norm1_g=a["norm1_g"], norm2_g=a["norm2_g"], b_mix_w=a["b_mix_w"][0], b_scale=a["b_scale"],
             c_kv_norm_g=a["c_kv_norm_g"], d_w_s=a["d_w_s"][0], d_b_s=a["d_b_s"][0],
             final_norm_g=a["final_norm_g"].reshape(1, D_MODEL), **small_full)

    ncol = N_MOD * D_MODEL // N_CHIPS
    ada_b_mine = lax.dynamic_slice_in_dim(a["ada_b"], chip * ncol, ncol, axis=1)
    mod_cols = ada_mod(c_all, a["ada_w"], ada_b_mine)
    g2_rows = all_gather8(mod_cols.reshape(-1, LANES), "gather_mod")
    g2 = g2_rows.reshape(N_DEV, 2, N_DEV, ncol)
    mod = lax.dynamic_index_in_dim(g2[0::2], dev, axis=2, keepdims=False)
    mod = mod.transpose(1, 0, 2).reshape(2, N_MOD * D_MODEL)

    late = [g2_rows]
    up0_handle, tok_a = start_gather([up0_16], "up0", late)
    down0_handle, tok_b = start_gather([down0_16], "down0", late)
    mix1_handle, tok_c = start_gather(mix1_16, "mix1", late)
    ffn1_handle, tok_d = start_gather([up1_16, down1_16], "ffn1", late)
    mod = mod + (tok_a + tok_b + tok_c + tok_d)

    ropes = rope_tables(a["positions"][0])
    cm16 = lambda t: chip_major(t).astype(BF16)
    w.update(ffn_w_up=[None, None], ffn_w_down=[None, None])
    handles = dict(mix0=mix0_handle, up0=up0_handle, down0=down0_handle, mix1=mix1_handle, ffn1=ffn1_handle)
    reducing, reduced = {}, {}

    class Hooks(StepHooks):
        def weights(self, stage, after):
            got = finish_gather(handles[stage], chip, stage, after)
            if stage == "mix0":
                w.update(ab_w_in=got[0], ab_w_out=merge(got[1]))
            elif stage == "up0":
                w["ffn_w_up"][0] = got[0]
            elif stage == "down0":
                w["ffn_w_down"][0] = merge(got[0])
            elif stage == "mix1":
                cd_in, uq, ukv, cd_out = got
                w.update(prepare_weights(dict(cd_w_in=from_chip_major(cd_in), c_w_uq=from_chip_major(uq),
                                              c_w_ukv=from_chip_major(ukv), cd_w_out=merge(cd_out))))
            else:
                w["ffn_w_up"][1], w["ffn_w_down"][1] = got[0], merge(got[1])

        def gradients(self, stage, grads, after):
            if stage in ("ffn0", "ffn1"):
                parts = [grads["ffn_w_up"], _rows_major(grads["ffn_w_down"])]
            elif stage == "mix1":
                grads.update(unprepare_grads(grads))
                parts = [cm16(grads["cd_w_in"]), cm16(grads["c_w_uq"]), cm16(grads["c_w_ukv"]),
                         _rows_major(grads["cd_w_out"]).astype(BF16)]
            else:
                parts = [grads["ab_w_in"], _rows_major(grads["ab_w_out"])]
            reducing[stage], tok = start_reduce(parts, ci, stage)
            before = {"mix1": "ffn1", "ffn0": "mix1", "mix0": "ffn0"}.get(stage)
            if before is not None:
                reduced[before] = finish_reduce(reducing[before], chip, ci, before, after)
            return tok

    loss, grad_x, dmod, by_stage = run_step(x, tgt, mod, ropes, w, Hooks())
    grads = merge_grads(by_stage)

    parts3 = [dmod] + [grads[n] for n, _ in _SMALL_GRADS] + [loss[0, 0]]
    rows3 = -(-sum(p.size for p in parts3) // LANES // 8) * 8
    small_handle, _ = split_start("small_grads_start", EVERYONE, [_pack_rows(parts3, rows3, F32)],
                                  [_sds((N_DEV, rows3, LANES))])
    red_up1, red_down1 = reduced["ffn1"]
    red_cd_in, red_uq, red_ukv, red_cd_out = reduced["mix1"]
    red_up0, red_down0 = reduced["ffn0"]
    out_grads = dict(cd_w_in=red_cd_in, c_w_uq=red_uq, c_w_ukv=red_ukv, cd_w_out=red_cd_out)
    per_layer = dict(ffn_w_up=(red_up0, red_up1), ffn_w_down=(red_down0, red_down1))
    updates = {}

    def update(n):
        if n in per_layer:
            updates[n] = adamw_layers(a[n], *per_layer[n], a["m_" + n], a["v_" + n], "adamw_" + n)
        else:
            updates[n] = adamw(a[n], out_grads[n].reshape(a[n].shape), a["m_" + n], a["v_" + n], "adamw_" + n)

    early =("ffn_w_up", "ffn_w_down", "cd_w_in", "c_w_uq", "c_w_ukv", "cd_w_out")
    for n in early:
        update(n)
    (mine,), (landed,) = split_wait("small_grads_wait", EVERYONE, small_handle, [updates[n][1] for n in early])
    g3 = lax.dynamic_update_index_in_dim(landed, mine, dev, 0)
    summed = sum8(g3).reshape(-1)
    nmod = 2 * N_MOD * D_MODEL
    out_grads["ada_b"] = summed[:nmod].reshape(2, N_MOD * D_MODEL)
    off = nmod
    for n, shp in _SMALL_GRADS:
        out_grads[n] = summed[off:off + _size(shp)].reshape(shp)
        off += _size(shp)
    loss = summed[off]
    for n, shp, axis in _SMALL_SHARDED:
        width = out_grads[n].shape[-1] // N_CHIPS
        out_grads[n] = lax.dynamic_slice_in_dim(out_grads[n], chip * width, width, axis=out_grads[n].ndim - 1)
    dmod_all = g3.reshape(N_DEV, rows3 * LANES)[:, :nmod].reshape(N_DEV, 2, N_MOD * D_MODEL)
    dmod_mine = lax.dynamic_slice_in_dim(dmod_all, chip * ncol, ncol, axis=2).transpose(1, 0, 2)
    updates["ada_w"] = adamw_ada(a["ada_w"], c_all, dmod_mine, a["m_ada_w"], a["v_ada_w"])

    red_in0, red_out0 = finish_reduce(reducing["mix0"], chip, ci, "mix0", updates["ada_w"][1])
    out_grads.update(ab_w_in=red_in0, ab_w_out=red_out0)

    for n in ("ab_w_in", "ab_w_out"):
        update(n)
    small = [n for n in _WEIGHTS if n not in updates]
    for n, res in zip(small, adamw_small([a[n] for n in small], [out_grads[n].reshape(a[n].shape) for n in small],
                                         [a["m_" + n] for n in small], [a["v_" + n] for n in small])):
        updates[n] = res
    return (loss, grad_x[None], *[updates[n][i] for i in range(4) for n in _WEIGHTS])
```

```python
import functools
from typing import NamedTuple

import jax
import jax.numpy as jnp
from jax import lax
from jax.experimental import pallas as pl
from jax.experimental.pallas import tpu as pltpu

F32 = jnp.float32
BF16 = jnp.bfloat16
EPS = 1e-6
D_MODEL = 1024
N_MOD = 6
A_WIDTH = 512
B_GROUPS = 4
C_HEADS = 8
C_NOPE = 64
C_ROPE = 32
C_V = 64
C_Q_RANK = 256
C_KV_RANK = 128
HEAD_PAD = 128
ROPE_THETA = 10000.0
D_GROUPS = 4
D_CHUNK = 128
D_FF = 2816
FF_UNIT = 128
ADAM_LR = 0.001
ADAM_B1 = 0.9
ADAM_B2 = 0.999
ADAM_EPS = 1e-08
ADAM_WD = 0.01
ADAM_STEP = 10
N_CHIPS = 4
N_DEV = 8
LANES = 128
VMEM_BIG = 56 * 1024 * 1024
MESH = pl.DeviceIdType.MESH


def _sds(shape, dtype=F32):
    return jax.ShapeDtypeStruct(tuple(shape), dtype)


def _tile(n, cap, mult=128):
    if n <= cap:
        return n
    best = None
    for t in range(mult, cap + 1, mult):
        if n % t == 0:
            best = t
    assert best is not None, (n, cap, mult)
    return best


def _params(dims=None, vmem=None):
    return pltpu.CompilerParams(dimension_semantics=dims, vmem_limit_bytes=vmem)


def _shift_down(v, k):
    r = pltpu.roll(v, k, axis=0)
    t = lax.broadcasted_iota(jnp.int32, v.shape, 0)
    return jnp.where(t >= k, r, 0.0)


def _shift_up(v, k):
    n = v.shape[0]
    r = pltpu.roll(v, n - k, axis=0)
    t = lax.broadcasted_iota(jnp.int32, v.shape, 0)
    return jnp.where(t < n - k, r, 0.0)


def _sigmoid(v):
    return 1.0 / (1.0 + jnp.exp(-v))


_GELU_C = 0.7978845608028654
_GELU_A = 0.044715


def _gelu(v):
    return 0.5 * v * (1.0 + jnp.tanh(_GELU_C * (v + _GELU_A * v * v * v)))


def _gelu_grad(v):
    th = jnp.tanh(_GELU_C * (v + _GELU_A * v * v * v))
    return 0.5 * (1.0 + th) + 0.5 * v * (1.0 - th * th) * _GELU_C * (1.0 + 3.0 * _GELU_A * v * v)


_NN = (((1,), (0,)), ((), ()))
_NT = (((1,), (1,)), ((), ()))
_TN = (((0,), (0,)), ((), ()))


def _dot(a, b, dims=_NN):
    return lax.dot_general(a, b, dims, preferred_element_type=F32)


def _logical(t, groups):
    return (t.shape[-2], t.shape[-1] * groups)


def _block(tr, tc, groups, cols, where):
    if groups == 1:
        return pl.BlockSpec((tr, tc), where)
    per = cols // groups // tc

    def index(i, j, s):
        r, c = where(i, j, s)
        return (c // per, r, c % per)

    return pl.BlockSpec((None, tr, tc), index)


def matmul(a, b, mode, out_dtype, name, ga=1, gb=1, go=1, tm=None, tn=None, tk=None):
    (ar, ac), (br, bc) = _logical(a, ga), _logical(b, gb)
    if mode == "nn":
        m, k, n = ar, ac, bc
        a_col, b_col = "k", "n"
    elif mode == "nt":
        m, k, n = ar, ac, br
        a_col, b_col = "k", "k"
    else:
        k, m, n = ar, ac, bc
        a_col, b_col = "m", "n"
    limit = {"m": m, "n": n // go, "k": k}
    limit[a_col] = min(limit[a_col], ac // ga)
    limit[b_col] = min(limit[b_col], bc // gb)
    tm = tm or _tile(limit["m"], 2048, 128 if mode == "tn" else 16)
    tn = tn or _tile(limit["n"], 512)
    tk = tk or _tile(limit["k"], 2048, 16 if mode == "tn" else 128)
    nk = k // tk
    if mode == "nn":
        a_spec = _block(tm, tk, ga, ac, lambda i, j, s: (i, s))
        b_spec = _block(tk, tn, gb, bc, lambda i, j, s: (s, j))
        dims = _NN
    elif mode == "nt":
        a_spec = _block(tm, tk, ga, ac, lambda i, j, s: (i, s))
        b_spec = _block(tn, tk, gb, bc, lambda i, j, s: (j, s))
        dims = _NT
    else:
        a_spec = _block(tk, tm, ga, ac, lambda i, j, s: (s, i))
        b_spec = _block(tk, tn, gb, bc, lambda i, j, s: (s, j))
        dims = _TN
    o_spec = _block(tm, tn, go, n, lambda i, j, s: (i, j))
    out_shape = _sds((m, n), out_dtype) if go == 1 else _sds((go, m, n // go), out_dtype)

    def body(a_ref, b_ref, o_ref, acc_ref):
        s = pl.program_id(2)

        @pl.when(s == 0)
        def _():
            acc_ref[...] = jnp.zeros_like(acc_ref)

        acc_ref[...] += _dot(a_ref[...], b_ref[...], dims)

        @pl.when(s == nk - 1)
        def _():
            o_ref[...] = acc_ref[...].astype(o_ref.dtype)

    return pl.pallas_call(
        body, name=name, out_shape=out_shape, grid=(m // tm, n // tn, nk),
        in_specs=[a_spec, b_spec], out_specs=o_spec,
        scratch_shapes=[pltpu.VMEM((tm, tn), F32)],
        compiler_params=_params(("parallel", "parallel", "arbitrary"), VMEM_BIG),
    )(a, b)


def _rows(tm, n):
    return pl.BlockSpec((tm, n), lambda i: (i, 0))


def _vec(n):
    return pl.BlockSpec((1, n), lambda i: (0, 0))


class Vec(NamedTuple):
    array: jax.Array
    row: int
    col: int


def _vec_in(v, d):
    return pl.BlockSpec((None, 1, d), lambda i: (v.row, 0, v.col))


def modnorm_fwd(x, g, sc, sh, name):
    s, d = x.shape
    tm = _tile(s, 256, 8)

    def body(x_ref, g_ref, sc_ref, sh_ref, o_ref):
        xv = x_ref[...]
        r = lax.rsqrt(jnp.mean(xv * xv, axis=-1, keepdims=True) + EPS)
        o_ref[...] = ((xv * r) * g_ref[...] * (1.0 + sc_ref[...]) + sh_ref[...]).astype(BF16)

    return pl.pallas_call(
        body, name=name, out_shape=_sds((s, d), BF16), grid=(s // tm,),
        in_specs=[_rows(tm, d), _vec_in(g, d), _vec_in(sc, d), _vec_in(sh, d)], out_specs=_rows(tm, d),
        compiler_params=_params(("parallel",)),
    )(x, g.array, sc.array, sh.array)


def norm_bwd(x, dh, g, sc, dres, name):
    s, d = x.shape
    tm, streams = _row_streams(s)
    nsteps = s // tm

    def body(x_ref, dh_ref, g_ref, sc_ref, dr_ref, dx_ref, dsh_ref, dsc_ref, dg_ref, a2_ref):
        i = pl.program_id(0)

        @pl.when(i == 0)
        def _():
            dsh_ref[...] = jnp.zeros_like(dsh_ref)
            a2_ref[...] = jnp.zeros_like(a2_ref)

        for rs in streams:
            xv = x_ref[rs, :]
            dh = dh_ref[rs, :].astype(F32)
            r = lax.rsqrt(jnp.mean(xv * xv, axis=-1, keepdims=True) + EPS)
            xh = xv * r
            dsh_ref[...] += jnp.sum(dh, axis=0, keepdims=True)
            a2_ref[...] += jnp.sum(dh * xh, axis=0, keepdims=True)
            dxh = dh * (g_ref[...] * (1.0 + sc_ref[...]))
            dx = r * (dxh - xh * jnp.mean(dxh * xh, axis=-1, keepdims=True))
            dx_ref[rs, :] = dr_ref[rs, :] + dx

        @pl.when(i == nsteps - 1)
        def _():
            dsc_ref[...] = a2_ref[...] * g_ref[...]
            dg_ref[...] = a2_ref[...] * (1.0 + sc_ref[...])

    return pl.pallas_call(
        body, name=name, out_shape=(_sds((s, d)), _sds((1, d)), _sds((1, d)), _sds((1, d))), grid=(nsteps,),
        in_specs=[_rows(tm, d), _rows(tm, d), _vec_in(g, d), _vec_in(sc, d), _rows(tm, d)],
        out_specs=(_rows(tm, d), _vec(d), _vec(d), _vec(d)),
        scratch_shapes=[pltpu.VMEM((1, d), F32)],
        compiler_params=_params(("arbitrary",), VMEM_BIG),
    )(x, dh, g.array, sc.array, dres)


_ROW_STREAM = 256


def _row_streams(s):
    tm = _tile(s, 2 * _ROW_STREAM, 8)
    sub = min(tm, _ROW_STREAM)
    return tm, [slice(r * sub, (r + 1) * sub) for r in range(tm // sub)]


def resid_modnorm_fwd(x, y, gate, g, sc, sh, name):
    s, d = x.shape
    tm, streams = _row_streams(s)

    def body(x_ref, y_ref, gate_ref, g_ref, sc_ref, sh_ref, xo_ref, h_ref):
        for rs in streams:
            xv = x_ref[rs, :] + gate_ref[...] * y_ref[rs, :].astype(F32)
            xo_ref[rs, :] = xv
            r = lax.rsqrt(jnp.mean(xv * xv, axis=-1, keepdims=True) + EPS)
            h_ref[rs, :] = ((xv * r) * g_ref[...] * (1.0 + sc_ref[...]) + sh_ref[...]).astype(BF16)

    return pl.pallas_call(
        body, name=name, out_shape=(_sds((s, d)), _sds((s, d), BF16)), grid=(s // tm,),
        in_specs=[_rows(tm, d), _rows(tm, d), _vec_in(gate, d), _vec_in(g, d), _vec_in(sc, d), _vec_in(sh, d)],
        out_specs=(_rows(tm, d), _rows(tm, d)),
        compiler_params=_params(("parallel",), VMEM_BIG),
    )(x, y, gate.array, g.array, sc.array, sh.array)


def norm_gate_bwd(x, dh, g, sc, dres, y, gate, name):
    s, d = x.shape
    tm, streams = _row_streams(s)
    nsteps = s // tm

    def body(x_ref, dh_ref, g_ref, sc_ref, dr_ref, y_ref, gate_ref, dx_ref, dsh_ref, dsc_ref, dg_ref, dy_ref,
             dgate_ref, a2_ref):
        i = pl.program_id(0)

        @pl.when(i == 0)
        def _():
            dsh_ref[...] = jnp.zeros_like(dsh_ref)
            a2_ref[...] = jnp.zeros_like(a2_ref)
            dgate_ref[...] = jnp.zeros_like(dgate_ref)

        for rs in streams:
            xv = x_ref[rs, :]
            dh = dh_ref[rs, :].astype(F32)
            r = lax.rsqrt(jnp.mean(xv * xv, axis=-1, keepdims=True) + EPS)
            xh = xv * r
            dsh_ref[...] += jnp.sum(dh, axis=0, keepdims=True)
            a2_ref[...] += jnp.sum(dh * xh, axis=0, keepdims=True)
            dxh = dh * (g_ref[...] * (1.0 + sc_ref[...]))
            dr = dr_ref[rs, :] + r * (dxh - xh * jnp.mean(dxh * xh, axis=-1, keepdims=True))
            dx_ref[rs, :] = dr
            dy_ref[rs, :] = (dr * gate_ref[...]).astype(BF16)
            dgate_ref[...] += jnp.sum(dr * y_ref[rs, :].astype(F32), axis=0, keepdims=True)

        @pl.when(i == nsteps - 1)
        def _():
            dsc_ref[...] = a2_ref[...] * g_ref[...]
            dg_ref[...] = a2_ref[...] * (1.0 + sc_ref[...])

    vec = _sds((1, d))
    return pl.pallas_call(
        body, name=name, out_shape=(_sds((s, d)), vec, vec, vec, _sds((s, d), BF16), vec), grid=(nsteps,),
        in_specs=[_rows(tm, d), _rows(tm, d), _vec_in(g, d), _vec_in(sc, d), _rows(tm, d), _rows(tm, d), _vec_in(gate, d)],
        out_specs=(_rows(tm, d), _vec(d), _vec(d), _vec(d), _rows(tm, d), _vec(d)),
        scratch_shapes=[pltpu.VMEM((1, d), F32)],
        compiler_params=_params(("arbitrary",), VMEM_BIG),
    )(x, dh, g.array, sc.array, dres, y, gate.array)


def final_fused(x, f, gate, g, tgt):
    s, d = x.shape
    tm, streams = _row_streams(s)

    def body(x_ref, f_ref, gate_ref, g_ref, t_ref, dx_ref, dg_ref, loss_ref, df_ref, dgate_ref):
        @pl.when(pl.program_id(0) == 0)
        def _():
            dg_ref[...] = jnp.zeros_like(dg_ref)
            loss_ref[...] = jnp.zeros_like(loss_ref)
            dgate_ref[...] = jnp.zeros_like(dgate_ref)

        gatev, gv = gate_ref[...], g_ref[...]
        for rs in streams:
            fv = f_ref[rs, :].astype(F32)
            xv = x_ref[rs, :] + gatev * fv
            r = lax.rsqrt(jnp.mean(xv * xv, axis=-1, keepdims=True) + EPS)
            xh = xv * r
            e = xh * gv - t_ref[rs, :]
            row = jnp.sum(e * e, axis=-1, keepdims=True) * (0.5 / d)
            loss_ref[...] += jnp.sum(row, axis=0, keepdims=True)
            dy = e * (1.0 / d)
            dg_ref[...] += jnp.sum(dy * xh, axis=0, keepdims=True)
            dxh = dy * gv
            dx = r * (dxh - xh * jnp.mean(dxh * xh, axis=-1, keepdims=True))
            dx_ref[rs, :] = dx
            df_ref[rs, :] = (dx * gatev).astype(BF16)
            dgate_ref[...] += jnp.sum(dx * fv, axis=0, keepdims=True)

    vec = _sds((1, d))
    return pl.pallas_call(
        body, name="final_fused", out_shape=(_sds((s, d)), vec, _sds((1, LANES)), _sds((s, d), BF16), vec),
        grid=(s // tm,),
        in_specs=[_rows(tm, d), _rows(tm, d), _vec_in(gate, d), _vec_in(g, d), _rows(tm, d)],
        out_specs=(_rows(tm, d), _vec(d), _vec(LANES), _rows(tm, d), _vec(d)),
        compiler_params=_params(("arbitrary",), VMEM_BIG),
    )(x, f, gate.array, g.array, tgt)


def _taps(v):
    return _shift_down(v, 2), _shift_down(v, 1), v


def _conv3_taps(taps, w):
    return w[0:1, :] * taps[0] + w[1:2, :] * taps[1] + w[2:3, :] * taps[2]


def _conv3(v, w):
    return _conv3_taps(_taps(v), w)


def _conv3_t(dv, w):
    return w[0:1, :] * _shift_up(dv, 2) + w[1:2, :] * _shift_up(dv, 1) + w[2:3, :] * dv


def _conv3_dw_taps(dv, taps):
    return jnp.concatenate([jnp.sum(dv * t, axis=0, keepdims=True) for t in taps], axis=0)


def _conv3_dw(dv, v):
    return _conv3_dw_taps(dv, _taps(v))


def gconv_fwd(z, conv_w):
    s = z.shape[0]
    nb = A_WIDTH // LANES

    def body(b_ref, c_ref, a_ref, w_ref, o_ref):
        b, c, a = b_ref[...].astype(F32), c_ref[...].astype(F32), a_ref[...].astype(F32)
        o_ref[...] = (b * _conv3(c * a, w_ref[...])).astype(BF16)

    col = lambda off: pl.BlockSpec((s, LANES), lambda j: (0, off + j))
    return pl.pallas_call(
        body, name="gconv_fwd", out_shape=_sds((s, A_WIDTH + _B_WIDTH), BF16), grid=(nb,),
        in_specs=[col(0), col(nb), col(2 * nb), pl.BlockSpec((3, LANES), lambda j: (0, j))],
        out_specs=pl.BlockSpec((s, LANES), lambda j: (0, j)),
        compiler_params=_params(("parallel",), VMEM_BIG),
    )(z, z, z, conv_w)


def gconv_bwd(z, conv_w, dycat):
    s = z.shape[0]
    nb = A_WIDTH // LANES

    def body(b_ref, c_ref, a_ref, w_ref, dy_ref, db_ref, dc_ref, da_ref, dw_ref):
        c, a, w, dy = c_ref[...].astype(F32), a_ref[...].astype(F32), w_ref[...], dy_ref[...].astype(F32)
        ca = c * a
        db_ref[...] = (dy * _conv3(ca, w)).astype(BF16)
        dconv = dy * b_ref[...].astype(F32)
        dw_ref[...] = _conv3_dw(dconv, ca)
        dca = _conv3_t(dconv, w)
        dc_ref[...] = (dca * a).astype(BF16)
        da_ref[...] = (dca * c).astype(BF16)

    col = lambda off: pl.BlockSpec((s, LANES), lambda j: (0, off + j))
    wspec = pl.BlockSpec((3, LANES), lambda j: (0, j))
    part = _sds((s, A_WIDTH), BF16)
    return pl.pallas_call(
        body, name="gconv_bwd", out_shape=(part, part, part, _sds((3, A_WIDTH))), grid=(nb,),
        in_specs=[col(0), col(nb), col(2 * nb), wspec, col(0)],
        out_specs=(col(0), col(0), col(0), wspec),
        compiler_params=_params(("parallel",), VMEM_BIG),
    )(z, z, z, conv_w, dycat)


def _pool_counts(s, w):
    t = lax.broadcasted_iota(jnp.int32, (s, 1), 0)
    return jnp.minimum(t + 1, w).astype(F32)


def _pooled(p, levels):
    acc = p
    for lv in range(levels):
        acc = acc + _shift_down(acc, 2 ** lv)
    return acc / _pool_counts(p.shape[0], 2 ** levels) - p


_B_WIDTH = B_GROUPS * LANES


def pool_fwd(z, mix_w, scale, ycat):
    s = z.shape[0]

    def body(p_ref, m_ref, sc_ref, ycat_ref, o_ref):
        del ycat_ref
        for g in range(B_GROUPS):
            cols = slice(g * LANES, (g + 1) * LANES)
            pooled = _pooled(p_ref[:, cols].astype(F32), g + 1)
            y = _dot(pooled.astype(BF16), m_ref[g].astype(BF16))
            o_ref[:, cols] = (y * sc_ref[:, cols]).astype(BF16)

    return pl.pallas_call(
        body, name="pool_fwd", out_shape=_sds(ycat.shape, BF16), grid=(1,),
        in_specs=[pl.BlockSpec((s, _B_WIDTH), lambda i: (0, 3 * A_WIDTH // _B_WIDTH)),
                  pl.BlockSpec((B_GROUPS, LANES, LANES), lambda i: (0, 0, 0)), pl.BlockSpec((1, _B_WIDTH), lambda i: (0, 0)),
                  pl.BlockSpec(memory_space=pl.ANY)],
        out_specs=pl.BlockSpec((s, _B_WIDTH), lambda i: (0, A_WIDTH // _B_WIDTH)),
        input_output_aliases={3: 0},
        compiler_params=_params(("arbitrary",), VMEM_BIG),
    )(z, mix_w, scale, ycat)


def pool_bwd(z, mix_w, scale, dycat):
    s = z.shape[0]

    def body(p_ref, m_ref, sc_ref, dy_ref, dp_ref, dm_ref, dsc_ref):
        for g in range(B_GROUPS):
            cols = slice(g * LANES, (g + 1) * LANES)
            pooled = _pooled(p_ref[:, cols].astype(F32), g + 1)
            mw = m_ref[g].astype(BF16)
            pb = pooled.astype(BF16)
            dy = dy_ref[:, cols].astype(F32)
            dsc_ref[:, cols] = jnp.sum(dy * _dot(pb, mw), axis=0, keepdims=True)
            dmix = (dy * sc_ref[:, cols]).astype(BF16)
            dm_ref[g] = _dot(pb, dmix, _TN)
            dpool = _dot(dmix, mw, _NT)
            acc = dpool / _pool_counts(s, 2 ** (g + 1))
            for lv in range(g + 1):
                acc = acc + _shift_up(acc, 2 ** lv)
            dp_ref[:, cols] = (acc - dpool).astype(BF16)

    wide = lambda c: pl.BlockSpec((s, _B_WIDTH), lambda i: (0, c))
    mspec = pl.BlockSpec((B_GROUPS, LANES, LANES), lambda i: (0, 0, 0))
    vspec = pl.BlockSpec((1, _B_WIDTH), lambda i: (0, 0))
    return pl.pallas_call(
        body, name="pool_bwd", out_shape=(_sds((s, _B_WIDTH), BF16), _sds((B_GROUPS, LANES, LANES)), _sds((1, _B_WIDTH))),
        grid=(1,), in_specs=[wide(3 * A_WIDTH // _B_WIDTH), mspec, vspec, wide(A_WIDTH // _B_WIDTH)],
        out_specs=(wide(0), mspec, vspec),
        compiler_params=_params(("arbitrary",), VMEM_BIG),
    )(z, mix_w, scale, dycat)


_FF_BLOCKS = D_FF // FF_UNIT


def _ff_spec(s):
    return pl.BlockSpec((2, s, FF_UNIT), lambda j: (0, 0, j))


def _ff_wspecs():
    return [pl.BlockSpec((3, FF_UNIT), lambda j: (0, j)), pl.BlockSpec((3, FF_UNIT), lambda j: (0, _FF_BLOCKS + j))]


_FF_ROWS = 128
_FF_HALO = 16


def _chunk_taps(z_ref, half, c):
    start = pl.multiple_of(c * _FF_ROWS, _FF_ROWS)
    before = pl.multiple_of(jnp.maximum(c * _FF_ROWS - _FF_HALO, 0), _FF_HALO)
    halo = z_ref[half, pl.ds(before, _FF_HALO), :].astype(F32)
    halo = jnp.where(c > 0, halo, 0.0)
    win = jnp.concatenate([halo, z_ref[half, pl.ds(start, _FF_ROWS), :].astype(F32)], axis=0)
    return tuple(pltpu.roll(win, k, axis=0)[_FF_HALO:] for k in (2, 1)) + (win[_FF_HALO:],)


def _fold8(v):
    acc = v[0:8]
    for r in range(8, v.shape[0], 8):
        acc = acc + v[r:r + 8]
    return acc


_FF_CHUNK = 256


def ffn_act_down(zf, conv_w, w_down, name):
    s, d = zf.shape[1], w_down.shape[1]
    nk = D_FF // _FF_CHUNK
    chunk = lambda k: jnp.minimum(k, nk - 1)

    def body(z_ref, wg_ref, wu_ref, wd_ref, a_ref, f_ref, held_ref, acc_ref):
        k = pl.program_id(0)

        @pl.when(k == 0)
        def _():
            held_ref[...] = jnp.zeros_like(held_ref)
            acc_ref[...] = jnp.zeros_like(acc_ref)

        acc_ref[...] += _dot(held_ref[(k + 1) % 2], wd_ref[...])
        g = _conv3(z_ref[0].astype(F32), wg_ref[...])
        u = _conv3(z_ref[1].astype(F32), wu_ref[...])
        act = (g * _sigmoid(g) * u).astype(BF16)
        a_ref[...] = act
        held_ref[k % 2] = act

        @pl.when(k == nk)
        def _():
            f_ref[...] = acc_ref[...].astype(BF16)

    return pl.pallas_call(
        body, name=name, out_shape=(_sds((s, D_FF), BF16), _sds((s, d), BF16)), grid=(nk + 1,),
        in_specs=[pl.BlockSpec((2, s, _FF_CHUNK), lambda k: (0, 0, chunk(k))),
                  pl.BlockSpec((3, _FF_CHUNK), lambda k: (0, chunk(k))),
                  pl.BlockSpec((3, _FF_CHUNK), lambda k: (0, nk + chunk(k))),
                  pl.BlockSpec((_FF_CHUNK, d), lambda k: (jnp.maximum(k - 1, 0), 0))],
        out_specs=(pl.BlockSpec((s, _FF_CHUNK), lambda k: (0, chunk(k))), pl.BlockSpec((s, d), lambda k: (0, 0))),
        scratch_shapes=[pltpu.VMEM((2, s, _FF_CHUNK), BF16), pltpu.VMEM((s, d), F32)],
        compiler_params=_params(("arbitrary",), VMEM_BIG),
    )(zf, conv_w, conv_w, w_down)


def ffn_act_bwd(zf, conv_w, da, name):
    s = zf.shape[1]
    assert s % _FF_ROWS == 0
    nchunks = s // _FF_ROWS

    def body(z_ref, wg_ref, wu_ref, da_ref, dz_ref, dw_ref, dg_ref, du_ref):
        wg, wu = wg_ref[...], wu_ref[...]

        def first(c, acc):
            rows = pl.ds(pl.multiple_of(c * _FF_ROWS, _FF_ROWS), _FF_ROWS)
            tg, tu = _chunk_taps(z_ref, 0, c), _chunk_taps(z_ref, 1, c)
            g = _conv3_taps(tg, wg)
            u = _conv3_taps(tu, wu)
            dav = da_ref[rows, :].astype(F32)
            sg = _sigmoid(g)
            dg = dav * u * (sg * (1.0 + g * (1.0 - sg)))
            du = dav * (g * sg)
            dg_ref[rows, :] = dg
            du_ref[rows, :] = du
            return tuple(a + _fold8(d * t) for a, (d, t) in zip(acc, [(dg, t) for t in tg] + [(du, t) for t in tu]))

        zero = jnp.zeros((8, FF_UNIT), F32)
        acc = lax.fori_loop(0, nchunks, first, (zero,) * 6)
        sums = [jnp.sum(a, axis=0, keepdims=True) for a in acc]
        dw_ref[0] = jnp.concatenate(sums[:3], axis=0)
        dw_ref[1] = jnp.concatenate(sums[3:], axis=0)

        tail = pl.ds(s, _FF_HALO)
        dg_ref[tail, :] = jnp.zeros((_FF_HALO, FF_UNIT), F32)
        du_ref[tail, :] = jnp.zeros((_FF_HALO, FF_UNIT), F32)
        span = _FF_ROWS + _FF_HALO

        def second(c, carry):
            start = pl.multiple_of(c * _FF_ROWS, _FF_ROWS)
            for half, (d_ref, w) in enumerate(((dg_ref, wg), (du_ref, wu))):
                win = d_ref[pl.ds(start, span), :]
                dz = (w[0:1, :] * pltpu.roll(win, span - 2, axis=0)[:_FF_ROWS]
                      + w[1:2, :] * pltpu.roll(win, span - 1, axis=0)[:_FF_ROWS] + w[2:3, :] * win[:_FF_ROWS])
                dz_ref[half, pl.ds(start, _FF_ROWS), :] = dz.astype(BF16)
            return carry

        lax.fori_loop(0, nchunks, second, 0)

    return pl.pallas_call(
        body, name=name, out_shape=(_sds((2, s, D_FF), BF16), _sds((2, 3, D_FF))), grid=(_FF_BLOCKS,),
        in_specs=[_ff_spec(s)] + _ff_wspecs() + [pl.BlockSpec((s, FF_UNIT), lambda j: (0, j))],
        out_specs=(_ff_spec(s), pl.BlockSpec((2, 3, FF_UNIT), lambda j: (0, 0, j))),
        scratch_shapes=[pltpu.VMEM((s + _FF_HALO, FF_UNIT), F32), pltpu.VMEM((s + _FF_HALO, FF_UNIT), F32)],
        compiler_params=_params(("parallel",), VMEM_BIG),
    )(zf, conv_w, conv_w, da)


def _rope(v, cs, s1, s2):
    return v * cs + pltpu.roll(v, LANES - C_ROPE // 2, axis=1) * s1 + pltpu.roll(v, C_ROPE // 2, axis=1) * s2


def _rope_t(dv, cs, s1, s2):
    return dv * cs + pltpu.roll(dv * s1, C_ROPE // 2, axis=1) + pltpu.roll(dv * s2, LANES - C_ROPE // 2, axis=1)


def _kpe_mask(shape):
    lane = lax.broadcasted_iota(jnp.int32, shape, 1)
    return (lane >= C_NOPE) & (lane < C_NOPE + C_ROPE)


def _rms(v, g):
    r = lax.rsqrt(jnp.mean(v * v, axis=-1, keepdims=True) + EPS)
    return v * r, r


def _rms_bwd(dn, xh, r, g):
    dxh = dn * g
    return r * (dxh - xh * jnp.mean(dxh * xh, axis=-1, keepdims=True)), jnp.sum(dn * xh, axis=0, keepdims=True)


_ZQ = C_Q_RANK + C_KV_RANK + HEAD_PAD
_HW = C_HEADS * HEAD_PAD


_MLA_ROWS = 256


def _mla_tiles(s):
    tm = _tile(s, 2 * _MLA_ROWS, 8)
    sub = min(tm, _MLA_ROWS)
    return tm, [slice(r * sub, (r + 1) * sub) for r in range(tm // sub)]


def mla_pre_fwd(z, gq, gkv, wq, wk, wv, cs, s1, s2):
    s = z.shape[0]
    tm, streams = _mla_tiles(s)

    def body(z_ref, gq_ref, gkv_ref, wq_ref, wk_ref, wv_ref, cs_ref, s1_ref, s2_ref, q_ref, k_ref, v_ref):
        for rs in streams:
            zv = z_ref[rs, :].astype(F32)
            cst, s1t, s2t = cs_ref[rs, :], s1_ref[rs, :], s2_ref[rs, :]
            qh, _ = _rms(zv[:, :C_Q_RANK], None)
            qn = (qh * gq_ref[...]).astype(BF16)
            q = _dot(qn, wq_ref[...])
            kh, _ = _rms(zv[:, C_Q_RANK:C_Q_RANK + C_KV_RANK], None)
            kvn = (kh * gkv_ref[...]).astype(BF16)
            k = _dot(kvn, wk_ref[...])
            v_ref[rs, :] = _dot(kvn, wv_ref[...]).astype(BF16)
            kpe = _rope(zv[:, C_Q_RANK + C_KV_RANK:], cst, s1t, s2t)
            for h in range(C_HEADS):
                sl = slice(h * HEAD_PAD, (h + 1) * HEAD_PAD)
                q_ref[rs, sl] = _rope(q[:, sl], cst, s1t, s2t).astype(BF16)
                k_ref[rs, sl] = (k[:, sl] + kpe).astype(BF16)

    full = lambda r, c: pl.BlockSpec((r, c), lambda i: (0, 0))
    hw = _sds((s, _HW), BF16)
    return pl.pallas_call(
        body, name="mla_pre_fwd", out_shape=(hw, hw, hw), grid=(s // tm,),
        in_specs=[_rows(tm, _ZQ), _vec(C_Q_RANK), _vec(C_KV_RANK), full(C_Q_RANK, _HW), full(C_KV_RANK, _HW),
                  full(C_KV_RANK, _HW), _rows(tm, LANES), _rows(tm, LANES), _rows(tm, LANES)],
        out_specs=(_rows(tm, _HW), _rows(tm, _HW), _rows(tm, _HW)),
        compiler_params=_params(("parallel",), VMEM_BIG),
    )(z, gq, gkv, wq, wk, wv, cs, s1, s2)


def mla_pre_bwd(z, gq, gkv, wq, wk, wv, cs, s1, s2, dq, dk, dv):
    s = z.shape[0]
    tm, streams = _mla_tiles(s)

    def body(z_ref, gq_ref, gkv_ref, wq_ref, wk_ref, wv_ref, cs_ref, s1_ref, s2_ref, dq_ref, dk_ref, dv_ref,
             dz_ref, dwq_ref, dwk_ref, dwv_ref, dgq_ref, dgkv_ref):
        @pl.when(pl.program_id(0) == 0)
        def _():
            dwq_ref[...] = jnp.zeros_like(dwq_ref)
            dwk_ref[...] = jnp.zeros_like(dwk_ref)
            dwv_ref[...] = jnp.zeros_like(dwv_ref)
            dgq_ref[...] = jnp.zeros_like(dgq_ref)
            dgkv_ref[...] = jnp.zeros_like(dgkv_ref)

        gqv, gkvv = gq_ref[...], gkv_ref[...]
        for rs in streams:
            zv = z_ref[rs, :].astype(F32)
            cst, s1t, s2t = cs_ref[rs, :], s1_ref[rs, :], s2_ref[rs, :]
            qh, rq = _rms(zv[:, :C_Q_RANK], None)
            qn = (qh * gqv).astype(BF16)
            kh, rk = _rms(zv[:, C_Q_RANK:C_Q_RANK + C_KV_RANK], None)
            kvn = (kh * gkvv).astype(BF16)

            dqv = dq_ref[rs, :].astype(F32)
            dqp = jnp.concatenate(
                [_rope_t(dqv[:, h * HEAD_PAD:(h + 1) * HEAD_PAD], cst, s1t, s2t) for h in range(C_HEADS)], axis=1
            ).astype(BF16)
            dwq_ref[...] += _dot(qn, dqp, _TN)
            dqn = _dot(dqp, wq_ref[...], _NT)
            dql, dgq = _rms_bwd(dqn, qh, rq, gqv)
            dgq_ref[...] += dgq

            dkv = dk_ref[rs, :]
            dkb = dkv.astype(BF16)
            dvb = dv_ref[rs, :].astype(BF16)
            dwk_ref[...] += _dot(kvn, dkb, _TN)
            dwv_ref[...] += _dot(kvn, dvb, _TN)
            dkvn = _dot(dkb, wk_ref[...], _NT) + _dot(dvb, wv_ref[...], _NT)
            dkl, dgkv = _rms_bwd(dkvn, kh, rk, gkvv)
            dgkv_ref[...] += dgkv

            dkpe = dkv[:, :HEAD_PAD]
            for h in range(1, C_HEADS):
                dkpe = dkpe + dkv[:, h * HEAD_PAD:(h + 1) * HEAD_PAD]
            dkpe = _rope_t(jnp.where(_kpe_mask(dkpe.shape), dkpe, 0.0), cst, s1t, s2t)
            dz_ref[rs, :] = jnp.concatenate([dql, dkl, dkpe], axis=1).astype(BF16)

    full = lambda r, c: pl.BlockSpec((r, c), lambda i: (0, 0))
    return pl.pallas_call(
        body, name="mla_pre_bwd",
        out_shape=(_sds((s, _ZQ), BF16), _sds((C_Q_RANK, _HW)), _sds((C_KV_RANK, _HW)), _sds((C_KV_RANK, _HW)),
                   _sds((1, C_Q_RANK)), _sds((1, C_KV_RANK))),
        grid=(s // tm,),
        in_specs=[_rows(tm, _ZQ), _vec(C_Q_RANK), _vec(C_KV_RANK), full(C_Q_RANK, _HW), full(C_KV_RANK, _HW),
                  full(C_KV_RANK, _HW), _rows(tm, LANES), _rows(tm, LANES), _rows(tm, LANES),
                  _rows(tm, _HW), _rows(tm, _HW), _rows(tm, _HW)],
        out_specs=(_rows(tm, _ZQ), full(C_Q_RANK, _HW), full(C_KV_RANK, _HW), full(C_KV_RANK, _HW),
                   _vec(C_Q_RANK), _vec(C_KV_RANK)),
        compiler_params=_params(("arbitrary",), VMEM_BIG),
    )(z, gq, gkv, wq, wk, wv, cs, s1, s2, dq, dk, dv)


_ATT_SCALE = (C_NOPE + C_ROPE) ** -0.5
_NEG = -1e30


def _att_exp(q, k, row0, ends_here):
    sc = _dot(q, k, _NT) * _ATT_SCALE
    tq, nk = sc.shape
    if ends_here:
        last = sc[:, nk - tq:]
        row = lax.broadcasted_iota(jnp.int32, last.shape, 0)
        col = lax.broadcasted_iota(jnp.int32, last.shape, 1)
        last = jnp.where(col <= row, last, _NEG)
        sc = last if nk == tq else jnp.concatenate([sc[:, :nk - tq], last], axis=1)
    else:
        qpos = row0 + lax.broadcasted_iota(jnp.int32, sc.shape, 0)
        kpos = lax.broadcasted_iota(jnp.int32, sc.shape, 1)
        sc = jnp.where(kpos <= qpos, sc, _NEG)
    e = jnp.exp(sc - jnp.max(sc, axis=-1, keepdims=True))
    return e, 1.0 / jnp.sum(e, axis=-1, keepdims=True)


def _causal_cases(i, nq, tq, fn):
    if nq > 8:
        fn(nq * tq, False)
        return
    for blk in range(nq):
        pl.when(i == blk)(functools.partial(fn, (blk + 1) * tq, True))


_FWD_HEADS_PER_STEP = 4
_BWD_HEADS_PER_STEP = 2


def _head_lanes(heads):
    return [slice(h * HEAD_PAD, (h + 1) * HEAD_PAD) for h in range(heads)]


def attn_fwd(q, k, v):
    s = q.shape[0]
    tq = _tile(s, 256, 8)
    nq = s // tq
    heads = _FWD_HEADS_PER_STEP
    wide = heads * HEAD_PAD

    def body(q_ref, k_ref, v_ref, o_ref):
        i = pl.program_id(1)

        def case(nk, ends_here):
            for hd in _head_lanes(heads):
                e, inv = _att_exp(q_ref[:, hd], k_ref[:nk, hd], i * tq, ends_here)
                o_ref[:, hd] = (_dot(e.astype(BF16), v_ref[:nk, hd]) * inv).astype(BF16)

        _causal_cases(i, nq, tq, case)

    qspec = pl.BlockSpec((tq, wide), lambda h, i: (i, h))
    kspec = pl.BlockSpec((s, wide), lambda h, i: (0, h))
    return pl.pallas_call(
        body, name="attn_fwd", out_shape=_sds((s, _HW + _DW), BF16), grid=(C_HEADS // heads, s // tq),
        in_specs=[qspec, kspec, kspec], out_specs=qspec,
        compiler_params=_params(("parallel", "parallel"), VMEM_BIG),
    )(q, k, v)


def attn_bwd(q, k, v, o, do_all):
    s = q.shape[0]
    tq = _tile(s, 256, 8)
    heads = _BWD_HEADS_PER_STEP
    wide = heads * HEAD_PAD

    def body(q_ref, k_ref, v_ref, o_ref, do_ref, dq_ref, dk_ref, dv_ref):
        i = pl.program_id(1)

        @pl.when(i == 0)
        def _():
            dk_ref[...] = jnp.zeros_like(dk_ref)
            dv_ref[...] = jnp.zeros_like(dv_ref)

        def case(nk, ends_here):
            for hd in _head_lanes(heads):
                qv, kv, vv, dov = q_ref[:, hd], k_ref[:nk, hd], v_ref[:nk, hd], do_ref[:, hd]
                e, inv = _att_exp(qv, kv, i * tq, ends_here)
                p = e * inv
                dp = _dot(dov, vv, _NT)
                delta = jnp.sum(dov.astype(F32) * o_ref[:, hd].astype(F32), axis=-1, keepdims=True)
                ds = (p * (dp - delta) * _ATT_SCALE).astype(BF16)
                dq_ref[:, hd] = _dot(ds, kv).astype(BF16)
                dk_ref[:nk, hd] += _dot(ds, qv, _TN)
                dv_ref[:nk, hd] += _dot(p.astype(BF16), dov, _TN)

        _causal_cases(i, s // tq, tq, case)

    qspec = pl.BlockSpec((tq, wide), lambda h, i: (i, h))
    kspec = pl.BlockSpec((s, wide), lambda h, i: (0, h))
    return pl.pallas_call(
        body, name="attn_bwd", out_shape=(_sds((s, _HW), BF16), _sds((s, _HW)), _sds((s, _HW))),
        grid=(C_HEADS // heads, s // tq),
        in_specs=[qspec, kspec, kspec, qspec, qspec], out_specs=(qspec, kspec, kspec),
        compiler_params=_params(("parallel", "arbitrary"), VMEM_BIG),
    )(q, k, v, o, do_all)


_DW = D_GROUPS * LANES


def _tril_bf16(w):
    r = lax.broadcasted_iota(jnp.int32, w.shape, 0)
    c = lax.broadcasted_iota(jnp.int32, w.shape, 1)
    return jnp.where(c <= r, w, 0.0).astype(BF16)


def _sgu_forward(zu, zv, lg, lb, ws_ref, bs):
    u = _gelu(zu)
    v = _gelu(zv)
    mu = jnp.mean(v, axis=-1, keepdims=True)
    vc = v - mu
    rstd = lax.rsqrt(jnp.mean(vc * vc, axis=-1, keepdims=True) + EPS)
    xh = vc * rstd
    vln = (xh * lg + lb).astype(BF16)
    mixed = []
    for g in range(D_GROUPS):
        wg = _tril_bf16(ws_ref[g])
        mixed.append(_dot(wg, vln[:, g * LANES:(g + 1) * LANES]) + bs[:, g:g + 1])
    return u, xh, rstd, vln, jnp.concatenate(mixed, axis=1)


_SGU_CHUNKS = 4


def sgu_fwd(z, lg, lb, ws, bs_t, ycat):
    s = z.shape[0]
    rows = _SGU_CHUNKS * D_CHUNK

    def body(zu_ref, zv_ref, lg_ref, lb_ref, ws_ref, bs_ref, ycat_ref, o_ref):
        del ycat_ref
        for c in range(_SGU_CHUNKS):
            rs = slice(c * D_CHUNK, (c + 1) * D_CHUNK)
            u, _, _, _, mixed = _sgu_forward(zu_ref[rs, :].astype(F32), zv_ref[rs, :].astype(F32), lg_ref[...],
                                             lb_ref[...], ws_ref, bs_ref[...])
            o_ref[rs, :] = (u * mixed).astype(BF16)

    return pl.pallas_call(
        body, name="sgu_fwd", out_shape=_sds(ycat.shape, BF16), grid=(s // rows,),
        in_specs=[pl.BlockSpec((rows, _DW), lambda n: (n, 1)), pl.BlockSpec((rows, _DW), lambda n: (n, 2)),
                  _vec(_DW), _vec(_DW), pl.BlockSpec((D_GROUPS, D_CHUNK, D_CHUNK), lambda n: (0, 0, 0)),
                  pl.BlockSpec((D_CHUNK, LANES), lambda n: (0, 0)), pl.BlockSpec(memory_space=pl.ANY)],
        out_specs=pl.BlockSpec((rows, _DW), lambda n: (n, _HW // _DW)),
        input_output_aliases={6: 0},
        compiler_params=_params(("parallel",)),
    )(z, z, lg, lb, ws, bs_t, ycat)


def sgu_bwd(z, lg, lb, ws, bs_t, dycat, dy_col):
    s = z.shape[0]
    rows = _SGU_CHUNKS * D_CHUNK

    def body(zu_ref, zv_ref, lg_ref, lb_ref, ws_ref, bs_ref, dy_ref, dzu_ref, dzv_ref, dws_ref, dbs_ref, dlg_ref,
             dlb_ref):
        @pl.when(pl.program_id(0) == 0)
        def _():
            dws_ref[...] = jnp.zeros_like(dws_ref)
            dbs_ref[...] = jnp.zeros_like(dbs_ref)
            dlg_ref[...] = jnp.zeros_like(dlg_ref)
            dlb_ref[...] = jnp.zeros_like(dlb_ref)

        lg = lg_ref[...]
        lane = lax.broadcasted_iota(jnp.int32, (D_CHUNK, LANES), 1)
        row = lax.broadcasted_iota(jnp.int32, (D_CHUNK, D_CHUNK), 0)
        colm = lax.broadcasted_iota(jnp.int32, (D_CHUNK, D_CHUNK), 1)
        for c in range(_SGU_CHUNKS):
            rs = slice(c * D_CHUNK, (c + 1) * D_CHUNK)
            zu, zv = zu_ref[rs, :].astype(F32), zv_ref[rs, :].astype(F32)
            u, xh, rstd, vln, mixed = _sgu_forward(zu, zv, lg, lb_ref[...], ws_ref, bs_ref[...])
            dy = dy_ref[rs, :].astype(F32)
            dzu_ref[rs, :] = (dy * mixed * _gelu_grad(zu)).astype(BF16)
            dmix = dy * u
            dvln = []
            dbs = jnp.zeros((D_CHUNK, LANES), F32)
            for g in range(D_GROUPS):
                sl = slice(g * LANES, (g + 1) * LANES)
                dmg = dmix[:, sl]
                dbs = dbs + jnp.where(lane == g, jnp.sum(dmg, axis=-1, keepdims=True), 0.0)
                dmb = dmg.astype(BF16)
                dws_ref[g] += jnp.where(colm <= row, _dot(dmb, vln[:, sl], _NT), 0.0)
                dvln.append(_dot(_tril_bf16(ws_ref[g]), dmb, _TN))
            dbs_ref[...] += dbs
            dvln = jnp.concatenate(dvln, axis=1)
            dlg_ref[...] += jnp.sum(dvln * xh, axis=0, keepdims=True)
            dlb_ref[...] += jnp.sum(dvln, axis=0, keepdims=True)
            dxh = dvln * lg
            dvv = rstd * (dxh - jnp.mean(dxh, axis=-1, keepdims=True)
                          - xh * jnp.mean(dxh * xh, axis=-1, keepdims=True))
            dzv_ref[rs, :] = (dvv * _gelu_grad(zv)).astype(BF16)

    wsspec = pl.BlockSpec((D_GROUPS, D_CHUNK, D_CHUNK), lambda n: (0, 0, 0))
    chunk = lambda cidx: pl.BlockSpec((rows, _DW), lambda n: (n, cidx))
    return pl.pallas_call(
        body, name="sgu_bwd",
        out_shape=(_sds((s, _DW), BF16), _sds((s, _DW), BF16), _sds((D_GROUPS, D_CHUNK, D_CHUNK)),
                   _sds((D_CHUNK, LANES)), _sds((1, _DW)), _sds((1, _DW))),
        grid=(s // rows,),
        in_specs=[chunk(1), chunk(2), _vec(_DW), _vec(_DW), wsspec, pl.BlockSpec((D_CHUNK, LANES), lambda n: (0, 0)),
                  chunk(dy_col)],
        out_specs=(chunk(0), chunk(0), wsspec, pl.BlockSpec((D_CHUNK, LANES), lambda n: (0, 0)), _vec(_DW), _vec(_DW)),
        compiler_params=_params(("arbitrary",)),
    )(z, z, lg, lb, ws, bs_t, dycat)


def ada_mod(c_all, ada_w, ada_b):
    nl, d, n = ada_w.shape
    nb = c_all.shape[0]
    tn = _tile(n, 512)

    def body(c_ref, w_ref, b_ref, o_ref):
        cv = c_ref[...]
        ca = (cv * _sigmoid(cv)).astype(BF16)
        o_ref[...] = _dot(ca, w_ref[...].astype(BF16)) + b_ref[...]

    return pl.pallas_call(
        body, name="ada_mod", out_shape=_sds((nl, nb, n)), grid=(nl, n // tn),
        in_specs=[pl.BlockSpec((nb, d), lambda l, j: (0, 0)), pl.BlockSpec((None, d, tn), lambda l, j: (l, 0, j)),
                  pl.BlockSpec((None, 1, tn), lambda l, j: (l, 0, j))],
        out_specs=pl.BlockSpec((None, nb, tn), lambda l, j: (l, 0, j)),
        compiler_params=_params(("parallel", "parallel")),
    )(c_all, ada_w, ada_b.reshape(nl, 1, n))


_ADAM_BLOCK = 512 * 1024


def _adam_rows(rows, cols):
    if rows * cols <= _ADAM_BLOCK or rows % 8:
        return rows
    return _tile(rows, max(8, _ADAM_BLOCK // cols), 8)


def _adam_update(w, gv, m, v):
    inv_bc1 = 1.0 / (1.0 - ADAM_B1 ** ADAM_STEP)
    inv_bc2 = 1.0 / (1.0 - ADAM_B2 ** ADAM_STEP)
    nm = ADAM_B1 * m + (1.0 - ADAM_B1) * gv
    nv = ADAM_B2 * v + (1.0 - ADAM_B2) * (gv * gv)
    return -ADAM_LR * ((nm * inv_bc1) / (jnp.sqrt(nv * inv_bc2) + ADAM_EPS) + ADAM_WD * w), nm, nv


def adamw(w, g, m, v, name):
    shape = w.shape
    cols = shape[-1]
    rows = w.size // cols
    tr = _adam_rows(rows, cols)

    def body(w_ref, g_ref, m_ref, v_ref, go_ref, d_ref, nm_ref, nv_ref):
        gv = g_ref[...]
        go_ref[...] = gv
        d_ref[...], nm_ref[...], nv_ref[...] = _adam_update(w_ref[...], gv, m_ref[...], v_ref[...])

    spec = pl.BlockSpec((tr, cols), lambda i: (i, 0))
    out = _sds((rows, cols))
    r2 = lambda t: t.reshape(rows, cols)
    res = pl.pallas_call(
        body, name=name, out_shape=(out,) * 4, grid=(rows // tr,),
        in_specs=[spec] * 4, out_specs=(spec,) * 4, compiler_params=_params(("parallel",), VMEM_BIG),
    )(r2(w), r2(g), r2(m), r2(v))
    return tuple(t.reshape(shape) for t in res)


def adamw_ada(w, c_all, dmod, m, v):
    nl, d, n = w.shape
    tr = _adam_rows(d, n)
    pad = 16 - c_all.shape[0]
    c16 = jnp.pad(c_all, ((0, pad), (0, 0)))
    dm16 = jnp.pad(dmod, ((0, 0), (0, pad), (0, 0)))

    def body(w_ref, c_ref, dm_ref, m_ref, v_ref, g_ref, d_ref, nm_ref, nv_ref):
        cv = c_ref[...]
        gv = _dot((cv * _sigmoid(cv)).astype(BF16), dm_ref[...].astype(BF16), _TN)
        g_ref[...] = gv
        d_ref[...], nm_ref[...], nv_ref[...] = _adam_update(w_ref[...], gv, m_ref[...], v_ref[...])

    spec = pl.BlockSpec((None, tr, n), lambda l, i: (l, i, 0))
    out = _sds((nl, d, n))
    return pl.pallas_call(
        body, name="adamw_ada_w", out_shape=(out, out, out, out), grid=(nl, d // tr),
        in_specs=[spec, pl.BlockSpec((16, tr), lambda l, i: (0, i)), pl.BlockSpec((None, 16, n), lambda l, i: (l, 0, 0)),
                  spec, spec],
        out_specs=(spec,) * 4, compiler_params=_params(("parallel", "parallel"), VMEM_BIG),
    )(w, c16, dm16, m, v)


def adamw_small(ws, gs, ms, vs):
    n = len(ws)
    flat = lambda t: t.reshape(-1, t.shape[-1])

    def body(*refs):
        ins, outs = refs[:4 * n], refs[4 * n:]
        for i in range(n):
            w_ref, g_ref, m_ref, v_ref = ins[4 * i:4 * i + 4]
            outs[3 * i][...], outs[3 * i + 1][...], outs[3 * i + 2][...] = _adam_update(
                w_ref[...], g_ref[...], m_ref[...], v_ref[...])

    operands = [flat(t) for quad in zip(ws, gs, ms, vs) for t in quad]
    res = pl.pallas_call(
        body, name="adamw_small", out_shape=tuple(_sds(flat(w).shape) for w in ws for _ in range(3)),
    )(*operands)
    return [(g, res[3 * i].reshape(w.shape), res[3 * i + 1].reshape(w.shape), res[3 * i + 2].reshape(w.shape))
            for i, (w, g) in enumerate(zip(ws, gs))]


def adamw_layers(w, g0, g1, m, v, name):
    _, rows, cols = w.shape
    tr = _adam_rows(rows, cols)

    def body(w_ref, g0_ref, g1_ref, m_ref, v_ref, g_ref, d_ref, nm_ref, nv_ref):
        gv = jnp.where(pl.program_id(0) == 0, g0_ref[...], g1_ref[...])
        g_ref[...] = gv
        d_ref[...], nm_ref[...], nv_ref[...] = _adam_update(w_ref[...], gv, m_ref[...], v_ref[...])

    spec = pl.BlockSpec((None, tr, cols), lambda l, i: (l, i, 0))
    gspec = pl.BlockSpec((tr, cols), lambda l, i: (i, 0))
    out = _sds((2, rows, cols))
    return pl.pallas_call(
        body, name=name, out_shape=(out, out, out, out), grid=(2, rows // tr),
        in_specs=[spec, gspec, gspec, spec, spec], out_specs=(spec,) * 4,
        compiler_params=_params(("parallel", "parallel"), VMEM_BIG),
    )(w, g0, g1, m, v)


def sum8(gathered):
    _, r, _ = gathered.shape
    tr = _tile(r, 512, 8)

    def body(g_ref, o_ref):
        acc = g_ref[0]
        for dev in range(1, N_DEV):
            acc = acc + g_ref[dev]
        o_ref[...] = acc

    return pl.pallas_call(
        body, name="sum8", out_shape=_sds((r, LANES)), grid=(r // tr,),
        in_specs=[pl.BlockSpec((N_DEV, tr, LANES), lambda i: (0, i, 0))], out_specs=pl.BlockSpec((tr, LANES), lambda i: (i, 0)),
        compiler_params=_params(("parallel",)),
    )(gathered)


_SUM_STEPS = 2


def pair_sums(gs, recvs, core, name):
    n = len(gs)

    def body(c_ref, *refs):
        del c_ref
        for i in range(n):
            a_ref, b_ref, o_ref = refs[2 * i], refs[2 * i + 1], refs[2 * n + i]
            o_ref[...] = (a_ref[...].astype(F32) + b_ref[...].astype(F32)).astype(BF16)

    in_specs, out_specs = [], []
    for g in gs:
        half = (None, g.shape[1] // 2, g.shape[2])
        in_specs.append(pl.BlockSpec(half, lambda k, c: (k, c[0], 0)))
        in_specs.append(pl.BlockSpec(half, lambda k, c: (k, 0, 0)))
        out_specs.append(pl.BlockSpec(half, lambda k, c: (k, 0, 0)))
    grid_spec = pltpu.PrefetchScalarGridSpec(num_scalar_prefetch=1, grid=(N_CHIPS,), in_specs=in_specs,
                                             out_specs=tuple(out_specs))
    return list(pl.pallas_call(
        body, name=name, out_shape=tuple(_sds((N_CHIPS, g.shape[1] // 2, g.shape[2]), BF16) for g in gs),
        grid_spec=grid_spec, compiler_params=_params(("parallel",), VMEM_BIG),
    )(core.reshape(1).astype(jnp.int32), *[t for pair in zip(gs, recvs) for t in pair]))


def chip_sums(pairs, recvs, chip, core, name):
    n = len(pairs)
    trs = [p.shape[1] // _SUM_STEPS for p in pairs]

    def body(p_ref, *refs):
        del p_ref
        for i in range(n):
            own_ref, r_ref, o_ref = refs[2 * i], refs[2 * i + 1], refs[2 * n + i]
            acc = own_ref[...].astype(F32)
            for j in range(N_CHIPS - 1):
                acc = acc + r_ref[j].astype(F32)
            o_ref[...] = acc

    in_specs, out_specs = [], []
    for p, tr in zip(pairs, trs):
        cols = p.shape[2]
        in_specs.append(pl.BlockSpec((None, tr, cols), lambda s, q: (q[0], s, 0)))
        in_specs.append(pl.BlockSpec((N_CHIPS - 1, tr, cols), lambda s, q: (0, s, 0)))
        out_specs.append(pl.BlockSpec((None, tr, cols), lambda s, q: (q[1], s, 0)))
    grid_spec = pltpu.PrefetchScalarGridSpec(num_scalar_prefetch=1, grid=(_SUM_STEPS,), in_specs=in_specs,
                                             out_specs=tuple(out_specs))
    return list(pl.pallas_call(
        body, name=name, out_shape=tuple(_sds((2,) + p.shape[1:]) for p in pairs), grid_spec=grid_spec,
        compiler_params=_params(("parallel",)),
    )(jnp.stack([chip, core]).astype(jnp.int32), *[t for pair in zip(pairs, recvs) for t in pair]))


def _place():
    return lax.axis_index("x"), lax.axis_index("y"), lax.axis_index("c")


def _other_chips(x, y):
    return [(x, 1 - y), (1 - x, y), (1 - x, 1 - y)]


_HBM = pl.BlockSpec(memory_space=pltpu.HBM)


def all_gather8(v, name, after=()):
    m, n = v.shape

    def body(x_ref, *refs):
        out_ref, send_sems, recv_sems, local_sem = refs[len(after):]
        x, y, c = _place()
        me, sibling = (x, y, c), (x, y, 1 - c)
        chips = _other_chips(x, y)

        def rows(px, py, pc):
            return out_ref.at[pl.ds((4 * px + 2 * py + pc) * m, m), :]

        def copy(k, block, to, src=None):
            return pltpu.make_async_remote_copy(
                src_ref=rows(*block) if src is None else src, dst_ref=rows(*block),
                send_sem=send_sems.at[k], recv_sem=recv_sems.at[k], device_id=to, device_id_type=MESH)

        mine = pltpu.make_async_copy(x_ref, rows(*me), local_sem)
        mine.start()
        first = [copy(0, me, sibling, src=x_ref)]
        first += [copy(1 + j, me, (*chip, c), src=x_ref) for j, chip in enumerate(chips)]
        for cp in first:
            cp.start()
        passed = [copy(4 + j, (*chip, c), sibling) for j, chip in enumerate(chips)]
        for j, chip in enumerate(chips):
            copy(1 + j, (*chip, c), me).wait_recv()
            passed[j].start()
        copy(0, sibling, me).wait_recv()
        for j, chip in enumerate(chips):
            copy(4 + j, (*chip, 1 - c), me).wait_recv()
        for cp in first + passed:
            cp.wait_send()
        mine.wait()

    return pl.pallas_call(
        body, name=name, out_shape=_sds((N_DEV * m, n), v.dtype),
        in_specs=[pl.BlockSpec(memory_space=pltpu.VMEM)] + [pl.BlockSpec(memory_space=pl.ANY)] * len(after),
        out_specs=pl.BlockSpec(memory_space=pltpu.VMEM),
        scratch_shapes=[pltpu.SemaphoreType.DMA((7,)), pltpu.SemaphoreType.DMA((7,)), pltpu.SemaphoreType.DMA],
        compiler_params=_params(None, VMEM_BIG),
    )(v, *after)


def _comm_call(body, name, ins, out_shapes, nsem, aliases=None):
    return pl.pallas_call(
        body, name=name, out_shape=tuple(out_shapes), in_specs=[_HBM] * len(ins), out_specs=tuple([_HBM] * len(out_shapes)),
        scratch_shapes=[pltpu.SemaphoreType.DMA((nsem,)), pltpu.SemaphoreType.DMA((nsem,))],
        input_output_aliases=aliases or {},
    )(*ins)


def _remote(src, dst, send_sems, recv_sems, k, to):
    return pltpu.make_async_remote_copy(src_ref=src, dst_ref=dst, send_sem=send_sems.at[k], recv_sem=recv_sems.at[k],
                                        device_id=to, device_id_type=MESH)


def _half(core, rh):
    return pl.ds(pl.multiple_of(core * rh, 16), rh)


def swap_halves(gs, name):
    n = len(gs)

    def body(*refs):
        ins, outs, (send_sems, recv_sems) = refs[:n], refs[n:2 * n], refs[2 * n:]
        x, y, c = _place()
        copies = []
        for i in range(n):
            theirs = _half(1 - c, ins[i].shape[1] // 2)
            cp = _remote(ins[i].at[:, theirs], outs[i], send_sems, recv_sems, i, (x, y, 1 - c))
            cp.start()
            copies.append(cp)
        for cp in copies:
            cp.wait()

    return _comm_call(body, name, gs, [_sds((g.shape[0], g.shape[1] // 2, g.shape[2]), g.dtype) for g in gs], n)


def join_halves(bufs, name):
    n = len(bufs)

    def body(*refs):
        ins, outs, (send_sems, recv_sems) = refs[:n], refs[n:2 * n], refs[2 * n:]
        x, y, c = _place()
        copies = []
        for i in range(n):
            cp = _remote(ins[i].at[c], outs[i].at[c], send_sems, recv_sems, i, (x, y, 1 - c))
            cp.start()
            copies.append(cp)
        for i in range(n):
            theirs = outs[i].at[1 - c]
            _remote(theirs, theirs, send_sems, recv_sems, i, (x, y, 1 - c)).wait_recv()
        for cp in copies:
            cp.wait_send()

    return _comm_call(body, name, bufs, [_sds(b.shape, b.dtype) for b in bufs], n, {i: i for i in range(n)})


def forward_halves(lands, name):
    n = len(lands)

    def body(*refs):
        ins, outs, (send_sems, recv_sems) = refs[:n], refs[n:2 * n], refs[2 * n:]
        x, y, c = _place()
        sibling = (x, y, 1 - c)
        chips = _other_chips(x, y)
        copies = []
        for i in range(n):
            mine = _half(c, ins[i].shape[1] // 2)
            for j, (px, py) in enumerate(chips):
                cp = _remote(ins[i].at[2 * px + py, mine], outs[i].at[2 * px + py, mine], send_sems, recv_sems, 3 * i + j, sibling)
                cp.start()
                copies.append(cp)
        for i in range(n):
            theirs = _half(1 - c, ins[i].shape[1] // 2)
            for j, (px, py) in enumerate(chips):
                landed = outs[i].at[2 * px + py, theirs]
                _remote(landed, landed, send_sems, recv_sems, 3 * i + j, sibling).wait_recv()
        for cp in copies:
            cp.wait_send()

    return _comm_call(body, name, lands, [_sds(b.shape, b.dtype) for b in lands], 3 * n, {i: i for i in range(n)})


_SEM = pl.BlockSpec(memory_space=pltpu.SEMAPHORE)
_EFFECT = pltpu.SideEffectType.DATAFLOW_SIDE_EFFECTING


def _gather_copies(srcs, lands, send_sems, recv_sems):
    x, y, c = _place()
    copies = []
    for i in range(len(srcs)):
        mine = _half(c, srcs[i].shape[0] // 2)
        for j, chip in enumerate(_other_chips(x, y)):
            copies.append(_remote(srcs[i].at[mine], lands[i].at[2 * x + y, mine], send_sems, recv_sems, 3 * i + j, (*chip, c)))
    return copies


def _exchange_copies(srcs, lands, send_sems, recv_sems):
    x, y, c = _place()
    copies = []
    for i in range(len(srcs)):
        for j, (px, py) in enumerate(_other_chips(x, y)):
            copies.append(_remote(srcs[i].at[2 * px + py], lands[i].at[j], send_sems, recv_sems, 3 * i + j, (px, py, c)))
    return copies


def _everyone_copies(srcs, lands, send_sems, recv_sems):
    x, y, c = _place()
    flip = lambda v, b: 1 - v if b else v
    dst = lands[0].at[4 * x + 2 * y + c]
    return [_remote(srcs[0], dst, send_sems, recv_sems, j - 1, (flip(x, j & 4), flip(y, j & 2), flip(c, j & 1)))
            for j in range(1, N_DEV)]


GATHER = (_gather_copies, 3)
EXCHANGE = (_exchange_copies, 3)
EVERYONE = (_everyone_copies, N_DEV - 1)


def split_start(name, plan, srcs, land_shapes, after=()):
    copies_fn, per_source = plan
    n, m, k = len(srcs), len(land_shapes), len(after)
    ncopies = per_source * n

    def body(*refs):
        src_refs, land_refs = refs[:n], refs[n:n + m]
        send_sems, recv_sems = refs[n + m + k], refs[n + m + k + 1]
        token = refs[-1]
        for cp in copies_fn(src_refs, land_refs, send_sems, recv_sems):
            cp.start()
        token[...] = jnp.zeros_like(token)

    hbm = lambda s: pltpu.HBM(tuple(s.shape), s.dtype)
    outs = pl.pallas_call(
        body, name=name,
        out_shape=(pltpu.SemaphoreType.DMA((ncopies,)), pltpu.SemaphoreType.DMA((ncopies,)), *[hbm(s) for s in srcs],
                   *[hbm(s) for s in land_shapes], _sds((8, LANES))),
        in_specs=[_HBM] * (n + m) + [pl.BlockSpec(memory_space=pl.ANY)] * k,
        out_specs=(_SEM, _SEM, *([_HBM] * (n + m)), pl.BlockSpec(memory_space=pltpu.VMEM)),
        input_output_aliases={i: 2 + i for i in range(n + m)},
        compiler_params=pltpu.CompilerParams(has_side_effects=_EFFECT),
    )(*[pltpu.with_memory_space_constraint(s, pltpu.HBM) for s in srcs],
      *[pltpu.with_memory_space_constraint(lax.empty(tuple(s.shape), s.dtype), pltpu.HBM) for s in land_shapes], *after)
    handle = (outs[0], outs[1], list(outs[2:2 + n]), list(outs[2 + n:2 + n + m]))
    return handle, outs[-1][0, 0]


def split_wait(name, plan, handle, after):
    copies_fn, _ = plan
    send_sems, recv_sems, srcs, lands = handle
    n, m = len(srcs), len(lands)
    after = list(after) if isinstance(after, (list, tuple)) else [after]

    def body(*refs):
        src_refs, land_refs = refs[:n], refs[n:n + m]
        for cp in copies_fn(src_refs, land_refs, refs[n + m], refs[n + m + 1]):
            cp.wait_send()
            cp.wait_recv()

    hbm = lambda s: pltpu.HBM(tuple(s.shape), s.dtype)
    outs = pl.pallas_call(
        body, name=name, out_shape=tuple(hbm(s) for s in srcs + lands),
        in_specs=[_HBM] * (n + m) + [_SEM, _SEM] + [pl.BlockSpec(memory_space=pl.ANY)] * len(after),
        out_specs=tuple([_HBM] * (n + m)), input_output_aliases={i: i for i in range(n + m)},
        compiler_params=pltpu.CompilerParams(has_side_effects=_EFFECT),
    )(*srcs, *lands, send_sems, recv_sems, *after)
    return list(outs[:n]), list(outs[n:])


def chip_major(w, groups=N_CHIPS):
    r, c = w.shape
    return w.reshape(r, groups, c // groups).transpose(1, 0, 2)


def from_chip_major(w):
    g, r, c = w.shape
    return w.transpose(1, 0, 2).reshape(r, g * c)


def _cd_in_pad(w):
    a = C_Q_RANK + C_KV_RANK
    z = lambda n: jnp.zeros((w.shape[0], n), w.dtype)
    return jnp.concatenate([w[:, :a], z(C_NOPE), w[:, a:a + C_ROPE], z(HEAD_PAD - C_NOPE - C_ROPE), w[:, a + C_ROPE:]], axis=1)


def _cd_in_unpad(w):
    a = C_Q_RANK + C_KV_RANK
    return jnp.concatenate([w[:, :a], w[:, a + C_NOPE:a + C_NOPE + C_ROPE], w[:, a + HEAD_PAD:]], axis=1)


def _pad_heads(w, width):
    r = w.shape[0]
    w = w.reshape(r, C_HEADS, width)
    return jnp.pad(w, ((0, 0), (0, 0), (0, HEAD_PAD - width))).reshape(r, _HW)


def _unpad_heads(w, width):
    r = w.shape[0]
    return w.reshape(r, C_HEADS, HEAD_PAD)[:, :, :width].reshape(r, C_HEADS * width)


def prepare_weights(p):
    q = dict(p)
    q["cd_w_in"] = _cd_in_pad(p["cd_w_in"])
    q["c_w_uq"] = _pad_heads(p["c_w_uq"], C_NOPE + C_ROPE)
    ukv = p["c_w_ukv"].reshape(C_KV_RANK, C_HEADS, C_NOPE + C_V)
    q["c_w_uk"] = _pad_heads(ukv[:, :, :C_NOPE].reshape(C_KV_RANK, -1), C_NOPE)
    q["c_w_uv"] = _pad_heads(ukv[:, :, C_NOPE:].reshape(C_KV_RANK, -1), C_V)
    wo = p["cd_w_out"]
    att_rows = jnp.pad(wo[:C_HEADS * C_V].reshape(C_HEADS, C_V, D_MODEL), ((0, 0), (0, HEAD_PAD - C_V), (0, 0)))
    q["cd_w_out"] = jnp.concatenate([att_rows.reshape(_HW, D_MODEL), wo[C_HEADS * C_V:]], axis=0)
    return q


def unprepare_grads(g):
    q = dict(g)
    q["cd_w_in"] = _cd_in_unpad(g["cd_w_in"])
    q["c_w_uq"] = _unpad_heads(g["c_w_uq"], C_NOPE + C_ROPE)
    uk = g.pop("c_w_uk").reshape(C_KV_RANK, C_HEADS, HEAD_PAD)[:, :, :C_NOPE]
    uv = g.pop("c_w_uv").reshape(C_KV_RANK, C_HEADS, HEAD_PAD)[:, :, :C_V]
    q.pop("c_w_uk", None)
    q.pop("c_w_uv", None)
    q["c_w_ukv"] = jnp.concatenate([uk, uv], axis=-1).reshape(C_KV_RANK, C_HEADS * (C_NOPE + C_V))
    wo = g["cd_w_out"]
    att = wo[:_HW].reshape(C_HEADS, HEAD_PAD, D_MODEL)[:, :C_V].reshape(C_HEADS * C_V, D_MODEL)
    q["cd_w_out"] = jnp.concatenate([att, wo[_HW:]], axis=0)
    return q


def rope_tables(positions):
    half = C_ROPE // 2
    inv_freq = ROPE_THETA ** (-jnp.arange(half, dtype=F32) / half)
    ang = positions.astype(F32)[:, None] * inv_freq
    cos, sin = jnp.cos(ang), jnp.sin(ang)
    s = positions.shape[0]
    z = lambda n: jnp.zeros((s, n), F32)
    cs = jnp.concatenate([jnp.ones((s, C_NOPE), F32), cos, cos, z(HEAD_PAD - C_NOPE - C_ROPE)], axis=1)
    s1 = jnp.concatenate([z(C_NOPE), -sin, z(HEAD_PAD - C_NOPE - half)], axis=1)
    s2 = jnp.concatenate([z(C_NOPE + half), sin, z(HEAD_PAD - C_NOPE - C_ROPE)], axis=1)
    return cs, s1, s2


_UP_COLS = 2 * D_FF // N_CHIPS


def ffn_fwd(h2, w, l, late_down=None):
    zf = matmul(h2, w["ffn_w_up"][l], "nn", BF16, f"ffn_up{l}", gb=N_CHIPS, go=2, tn=_UP_COLS)
    if late_down is not None:
        late_down(zf)
    a, f = ffn_act_down(zf, w["ffn_conv_w"][l], w["ffn_w_down"][l], f"ffn_act_down{l}")
    return f, (zf, a)


def ffn_bwd(df, h2, saved, w, l):
    zf, a = saved
    da = matmul(df, w["ffn_w_down"][l], "nt", BF16, f"ffn_down_dx{l}", tn=D_FF // 2)
    d_down = matmul(a, df, "tn", BF16, f"ffn_down_dw{l}", tm=D_FF // 2)
    dzf, d_conv = ffn_act_bwd(zf, w["ffn_conv_w"][l], da, f"ffn_act_bwd{l}")
    dh2 = matmul(dzf, w["ffn_w_up"][l], "nt", BF16, f"ffn_up_dx{l}", ga=2, gb=N_CHIPS, tk=_UP_COLS, tn=D_MODEL)
    d_up = matmul(h2, dzf, "tn", BF16, f"ffn_up_dw{l}", gb=2, go=N_CHIPS, tn=_UP_COLS)
    d_conv = d_conv.transpose(1, 0, 2).reshape(3, 2 * D_FF)
    return dh2, dict(ffn_w_down=d_down, ffn_conv_w=d_conv, ffn_w_up=d_up)


def mixer0_fwd(h, w):
    z = matmul(h, w["ab_w_in"], "nn", BF16, "ab_in", gb=N_CHIPS)
    ycat = pool_fwd(z, w["b_mix_w"], w["b_scale"], gconv_fwd(z, w["a_conv_w"]))
    y = matmul(ycat, w["ab_w_out"], "nn", BF16, "ab_out", tn=D_MODEL)
    return y, (z, ycat)


def mixer0_bwd(dy, h, saved, w):
    z, ycat = saved
    grads = {}
    dycat = matmul(dy, w["ab_w_out"], "nt", BF16, "ab_out_dx")
    grads["ab_w_out"] = matmul(ycat, dy, "tn", BF16, "ab_out_dw")
    db, dc, da, d_conv = gconv_bwd(z, w["a_conv_w"], dycat)
    dp, d_mix, d_scale = pool_bwd(z, w["b_mix_w"], w["b_scale"], dycat)
    dz = jnp.concatenate([db, dc, da, dp], axis=1)
    dh = matmul(dz, w["ab_w_in"], "nt", BF16, "ab_in_dx", gb=N_CHIPS, tn=D_MODEL)
    grads["ab_w_in"] = matmul(h, dz, "tn", BF16, "ab_in_dw", go=N_CHIPS)
    grads.update(a_conv_w=d_conv, b_mix_w=d_mix, b_scale=d_scale)
    return dh, grads


def mixer1_fwd(h, ropes, w):
    cs, s1, s2 = ropes
    z = matmul(h, w["cd_w_in"], "nn", BF16, "cd_in")
    bs_t = jnp.pad(w["d_b_s"].T, ((0, 0), (0, LANES - D_GROUPS)))
    qh, kh, vh = mla_pre_fwd(z, w["c_q_norm_g"], w["c_kv_norm_g"], w["c_w_uq"], w["c_w_uk"], w["c_w_uv"], cs, s1, s2)
    ycat = sgu_fwd(z, w["d_ln_g"], w["d_ln_b"], w["d_w_s"], bs_t, attn_fwd(qh, kh, vh))
    y = matmul(ycat, w["cd_w_out"], "nn", BF16, "cd_out", tn=D_MODEL)
    return y, (z, bs_t, qh, kh, vh, ycat)


def mixer1_bwd(dy, h, saved, ropes, w):
    cs, s1, s2 = ropes
    z, bs_t, qh, kh, vh, ycat = saved
    grads = {}
    dycat = matmul(dy, w["cd_w_out"], "nt", BF16, "cd_out_dx")
    grads["cd_w_out"] = matmul(ycat, dy, "tn", BF16, "cd_out_dw")
    dqh, dkh, dvh = attn_bwd(qh, kh, vh, ycat, dycat)
    dzq, d_uq, d_uk, d_uv, d_gq, d_gkv = mla_pre_bwd(
        z, w["c_q_norm_g"], w["c_kv_norm_g"], w["c_w_uq"], w["c_w_uk"], w["c_w_uv"], cs, s1, s2, dqh, dkh, dvh)
    dzu, dzv, d_ws, d_bs, d_lg, d_lb = sgu_bwd(z, w["d_ln_g"], w["d_ln_b"], w["d_w_s"], bs_t, dycat, _HW // _DW)
    dz = jnp.concatenate([dzq, dzu, dzv], axis=1)
    dh = matmul(dz, w["cd_w_in"], "nt", BF16, "cd_in_dx", tn=D_MODEL)
    grads["cd_w_in"] = matmul(h, dz, "tn", BF16, "cd_in_dw")
    grads.update(c_w_uq=d_uq, c_w_uk=d_uk, c_w_uv=d_uv, c_q_norm_g=d_gq, c_kv_norm_g=d_gkv, d_w_s=d_ws,
                 d_b_s=d_bs[:, :D_GROUPS].T, d_ln_g=d_lg, d_ln_b=d_lb)
    return dh, grads


class StepHooks:
    def weights(self, stage, after):
        pass

    def gradients(self, stage, grads, after):
        return 0.0


def run_step(x, tgt, mod, ropes, w, hooks):
    sh1, sc1, g1, sh2, sc2, g2 = range(N_MOD)
    mods = mod.reshape(2, 1, N_MOD * D_MODEL)
    n1 = w["norm1_g"].reshape(2, 1, D_MODEL)
    n2 = w["norm2_g"].reshape(2, 1, D_MODEL)
    final_g = Vec(w["final_norm_g"].reshape(1, 1, D_MODEL), 0, 0)

    hooks.weights("mix0", mod)
    h0 = modnorm_fwd(x, Vec(n1, 0, 0), Vec(mods, 0, sc1), Vec(mods, 0, sh1), "modnorm_0")
    y0, mix0 = mixer0_fwd(h0, w)
    x1, h1 = resid_modnorm_fwd(x, y0, Vec(mods, 0, g1), Vec(n2, 0, 0), Vec(mods, 0, sc2), Vec(mods, 0, sh2), "resid_modnorm_1")
    hooks.weights("up0", x1)
    f0, ffn0 = ffn_fwd(h1, w, 0, lambda act: hooks.weights("down0", act))
    x2, h2 = resid_modnorm_fwd(x1, f0, Vec(mods, 0, g2), Vec(n1, 1, 0), Vec(mods, 1, sc1), Vec(mods, 1, sh1), "resid_modnorm_2")
    hooks.weights("mix1", x2)
    y1, mix1 = mixer1_fwd(h2, ropes, w)
    x3, h3 = resid_modnorm_fwd(x2, y1, Vec(mods, 1, g1), Vec(n2, 1, 0), Vec(mods, 1, sc2), Vec(mods, 1, sh2), "resid_modnorm_3")
    hooks.weights("ffn1", x3)
    f1, ffn1 = ffn_fwd(h3, w, 1)
    dres, d_final, loss, df1, dg2b = final_fused(x3, f1, Vec(mods, 1, g2), final_g, tgt)

    dh3, gf1 = ffn_bwd(df1, h3, ffn1, w, 1)
    late = mods + hooks.gradients("ffn1", gf1, dh3)
    dres, dsh2b, dsc2b, dn2b, dy1, dg1b = norm_gate_bwd(
        x3, dh3, Vec(n2, 1, 0), Vec(late, 1, sc2), dres, y1, Vec(late, 1, g1), "norm_gate_bwd_3")
    dh2, gm1 = mixer1_bwd(dy1, h2, mix1, ropes, w)
    late = mods + hooks.gradients("mix1", gm1, dh2)
    dres, dsh1b, dsc1b, dn1b, df0, dg2a = norm_gate_bwd(
        x2, dh2, Vec(n1, 1, 0), Vec(late, 1, sc1), dres, f0, Vec(late, 0, g2), "norm_gate_bwd_2")
    dh1, gf0 = ffn_bwd(df0, h1, ffn0, w, 0)
    late = mods + hooks.gradients("ffn0", gf0, dh1)
    dres, dsh2a, dsc2a, dn2a, dy0, dg1a = norm_gate_bwd(
        x1, dh1, Vec(n2, 0, 0), Vec(late, 0, sc2), dres, y0, Vec(late, 0, g1), "norm_gate_bwd_1")
    dh0, gm0 = mixer0_bwd(dy0, h0, mix0, w)
    late = mods + hooks.gradients("mix0", gm0, dh0)
    grad_x, dsh1a, dsc1a, dn1a = norm_bwd(x, dh0, Vec(n1, 0, 0), Vec(late, 0, sc1), dres, "norm_bwd_0")

    dmod = jnp.concatenate([jnp.concatenate([dsh1a, dsc1a, dg1a, dsh2a, dsc2a, dg2a], axis=1),
                            jnp.concatenate([dsh1b, dsc1b, dg1b, dsh2b, dsc2b, dg2b], axis=1)], axis=0)
    norms = dict(norm1_g=jnp.concatenate([dn1a, dn1b], axis=0), norm2_g=jnp.concatenate([dn2a, dn2b], axis=0),
                 final_norm_g=d_final)
    return loss, grad_x, dmod, dict(mix0=gm0, ffn0=gf0, mix1=gm1, ffn1=gf1, norms=norms)


def merge_grads(by_stage):
    grads = {**by_stage["mix0"], **by_stage["mix1"], **by_stage["norms"]}
    for k in ("ffn_w_down", "ffn_w_up"):
        grads[k] = [by_stage["ffn0"][k], by_stage["ffn1"][k]]
    grads["ffn_conv_w"] = jnp.stack([by_stage["ffn0"]["ffn_conv_w"], by_stage["ffn1"]["ffn_conv_w"]])
    return grads


_WEIGHTS = ("ada_w", "ada_b", "norm1_g", "norm2_g", "ab_w_in", "a_conv_w", "b_mix_w", "b_scale", "ab_w_out", "cd_w_in",
            "c_q_norm_g", "c_w_uq", "c_kv_norm_g", "c_w_ukv", "d_ln_g", "d_ln_b", "d_w_s", "d_b_s", "cd_w_out",
            "ffn_w_up", "ffn_conv_w", "ffn_w_down", "final_norm_g")
_INPUTS = ("x", "c", "positions") + _WEIGHTS + ("loss_target",) + tuple("m_" + n for n in _WEIGHTS) + tuple(
    "v_" + n for n in _WEIGHTS)

def _pack_rows(parts, rows, dtype):
    flat = jnp.concatenate([p.reshape(-1).astype(dtype) for p in parts])
    return jnp.pad(flat, (0, rows * LANES - flat.shape[0])).reshape(rows, LANES)


def _rows_major(w):
    r, c = w.shape
    return w.reshape(N_CHIPS, r // N_CHIPS, c)


def start_gather(shards, tag, after=()):
    lands = [_sds((N_CHIPS,) + s.shape, s.dtype) for s in shards]
    return split_start("gather_start_" + tag, GATHER, shards, lands, after)


def finish_gather(handle, chip, tag, after):
    shards, lands = split_wait("gather_wait_" + tag, GATHER, handle, after)
    lands = forward_halves(lands, "gather_forward_" + tag)
    return [lax.dynamic_update_index_in_dim(o, s, chip, 0) for o, s in zip(lands, shards)]


def start_reduce(gs, core, tag):
    recv = swap_halves(gs, "swap_halves_" + tag)
    pairs = pair_sums(gs, recv, core, "pair_sums_" + tag)
    lands = [_sds((N_CHIPS - 1,) + p.shape[1:], p.dtype) for p in pairs]
    return split_start("exchange_start_" + tag, EXCHANGE, pairs, lands)


def finish_reduce(handle, chip, core, tag, after):
    pairs, others = split_wait("exchange_wait_" + tag, EXCHANGE, handle, after)
    halves = chip_sums(pairs, others, chip, core, "chip_sums_" + tag)
    full = join_halves(halves, "join_halves_" + tag)
    return [f.reshape(f.shape[1] * 2, f.shape[2]) for f in full]


_SMALL_SHARDED = (("a_conv_w", (3, 128), 1), ("c_q_norm_g", (1, 64), 1), ("d_ln_g", (1, 128), 1), ("d_ln_b", (1, 128), 1),
                  ("ffn_conv_w", (2, 3, 2 * D_FF // N_CHIPS), 2))
_SMALL_GRADS = (("norm1_g", (2, D_MODEL)), ("norm2_g", (2, D_MODEL)), ("b_mix_w", (4, 128, 128)), ("b_scale", (1, 512)),
                ("c_kv_norm_g", (1, 128)), ("d_w_s", (4, 128, 128)), ("d_b_s", (4, 128)), ("final_norm_g", (1, D_MODEL)),
                ("a_conv_w", (3, 512)), ("c_q_norm_g", (1, 256)), ("d_ln_g", (1, 512)), ("d_ln_b", (1, 512)),
                ("ffn_conv_w", (2, 3, 2 * D_FF)))


def _size(shape):
    n = 1
    for d in shape:
        n *= d
    return n


def kernel(x, c, positions, ada_w, ada_b, norm1_g, norm2_g, ab_w_in, a_conv_w, b_mix_w, b_scale, ab_w_out, cd_w_in, c_q_norm_g, c_w_uq, c_kv_norm_g, c_w_ukv, d_ln_g, d_ln_b, d_w_s, d_b_s, cd_w_out, ffn_w_up, ffn_conv_w, ffn_w_down, final_norm_g, loss_target, m_ada_w, m_ada_b, m_norm1_g, m_norm2_g, m_ab_w_in, m_a_conv_w, m_b_mix_w, m_b_scale, m_ab_w_out, m_cd_w_in, m_c_q_norm_g, m_c_w_uq, m_c_kv_norm_g, m_c_w_ukv, m_d_ln_g, m_d_ln_b, m_d_w_s, m_d_b_s, m_cd_w_out, m_ffn_w_up, m_ffn_conv_w, m_ffn_w_down, m_final_norm_g, v_ada_w, v_ada_b, v_norm1_g, v_norm2_g, v_ab_w_in, v_a_conv_w, v_b_mix_w, v_b_scale, v_ab_w_out, v_cd_w_in, v_c_q_norm_g, v_c_w_uq, v_c_kv_norm_g, v_c_w_ukv, v_d_ln_g, v_d_ln_b, v_d_w_s, v_d_b_s, v_cd_w_out, v_ffn_w_up, v_ffn_conv_w, v_ffn_w_down, v_final_norm_g):
    args = (x, c, positions, ada_w, ada_b, norm1_g, norm2_g, ab_w_in, a_conv_w, b_mix_w, b_scale, ab_w_out, cd_w_in, c_q_norm_g, c_w_uq, c_kv_norm_g, c_w_ukv, d_ln_g, d_ln_b, d_w_s, d_b_s, cd_w_out, ffn_w_up, ffn_conv_w, ffn_w_down, final_norm_g, loss_target, m_ada_w, m_ada_b, m_norm1_g, m_norm2_g, m_ab_w_in, m_a_conv_w, m_b_mix_w, m_b_scale, m_ab_w_out, m_cd_w_in, m_c_q_norm_g, m_c_w_uq, m_c_kv_norm_g, m_c_w_ukv, m_d_ln_g, m_d_ln_b, m_d_w_s, m_d_b_s, m_cd_w_out, m_ffn_w_up, m_ffn_conv_w, m_ffn_w_down, m_final_norm_g, v_ada_w, v_ada_b, v_norm1_g, v_norm2_g, v_ab_w_in, v_a_conv_w, v_b_mix_w, v_b_scale, v_ab_w_out, v_cd_w_in, v_c_q_norm_g, v_c_w_uq, v_c_kv_norm_g, v_c_w_ukv, v_d_ln_g, v_d_ln_b, v_d_w_s, v_d_b_s, v_cd_w_out, v_ffn_w_up, v_ffn_conv_w, v_ffn_w_down, v_final_norm_g)
    a = dict(zip(_INPUTS, args, strict=True))
    xi, yi, ci = _place()
    chip = 2 * xi + yi
    dev = 4 * xi + 2 * yi + ci
    x = a["x"][0]
    tgt = a["loss_target"][0]

    bf = lambda t: t.astype(BF16)
    mix0_handle, tok = start_gather([bf(a["ab_w_in"][0]), bf(a["ab_w_out"][0])], "mix0")
    up0_16, down0_16, up1_16, down1_16 = [bf(a[n][l]) for l in (0, 1) for n in ("ffn_w_up", "ffn_w_down")]
    mix1_16 = [bf(a[n][0]) for n in ("cd_w_in", "c_w_uq", "c_w_ukv", "cd_w_out")]

    small_parts = [a["c"] + tok] + [a[n] for n, _, _ in _SMALL_SHARDED]
    rows1 = -(-sum(p.size for p in small_parts) // LANES // 8) * 8
    g1 = all_gather8(_pack_rows(small_parts, rows1, F32), "gather_small",
                     [up0_16, down0_16, up1_16, down1_16, mix1_16[0], mix1_16[3]]).reshape(N_DEV, rows1 * LANES)
    c_all = g1[:, :D_MODEL]
    per_chip = g1[0::2]
    small_full = {}
    off = D_MODEL
    for n, shp, axis in _SMALL_SHARDED:
        piece = per_chip[:, off:off + _size(shp)].reshape((N_CHIPS,) + shp)
        small_full[n] = jnp.concatenate([piece[k] for k in range(N_CHIPS)], axis=axis)
        off += _size(shp)

    merge = lambda t: t.reshape(t.shape[0] * t.shape[1], t.shape[2])
    w = dict(norm1_g=a["norm1_g"], norm2_g=a["norm2_g"], b_mix_w=a["b_mix_w"][0], b_scale=a["b_scale"],
             c_kv_norm_g=a["c_kv_norm_g"], d_w_s=a["d_w_s"][0], d_b_s=a["d_b_s"][0],
             final_norm_g=a["final_norm_g"].reshape(1, D_MODEL), **small_full)

    ncol = N_MOD * D_MODEL // N_CHIPS
    ada_b_mine = lax.dynamic_slice_in_dim(a["ada_b"], chip * ncol, ncol, axis=1)
    mod_cols = ada_mod(c_all, a["ada_w"], ada_b_mine)
    g2_rows = all_gather8(mod_cols.reshape(-1, LANES), "gather_mod")
    g2 = g2_rows.reshape(N_DEV, 2, N_DEV, ncol)
    mod = lax.dynamic_index_in_dim(g2[0::2], dev, axis=2, keepdims=False)
    mod = mod.transpose(1, 0, 2).reshape(2, N_MOD * D_MODEL)

    late = [g2_rows]
    up0_handle, tok_a = start_gather([up0_16], "up0", late)
    down0_handle, tok_b = start_gather([down0_16], "down0", late)
    mix1_handle, tok_c = start_gather(mix1_16, "mix1", late)
    ffn1_handle, tok_d = start_gather([up1_16, down1_16], "ffn1", late)
    mod = mod + (tok_a + tok_b + tok_c + tok_d)

    ropes = rope_tables(a["positions"][0])
    cm16 = lambda t: chip_major(t).astype(BF16)
    w.update(ffn_w_up=[None, None], ffn_w_down=[None, None])
    handles = dict(mix0=mix0_handle, up0=up0_handle, down0=down0_handle, mix1=mix1_handle, ffn1=ffn1_handle)
    reducing, reduced = {}, {}

    class Hooks(StepHooks):
        def weights(self, stage, after):
            got = finish_gather(handles[stage], chip, stage, after)
            if stage == "mix0":
                w.update(ab_w_in=got[0], ab_w_out=merge(got[1]))
            elif stage == "up0":
                w["ffn_w_up"][0] = got[0]
            elif stage == "down0":
                w["ffn_w_down"][0] = merge(got[0])
            elif stage == "mix1":
                cd_in, uq, ukv, cd_out = got
                w.update(prepare_weights(dict(cd_w_in=from_chip_major(cd_in), c_w_uq=from_chip_major(uq),
                                              c_w_ukv=from_chip_major(ukv), cd_w_out=merge(cd_out))))
            else:
                w["ffn_w_up"][1], w["ffn_w_down"][1] = got[0], merge(got[1])

        def gradients(self, stage, grads, after):
            if stage in ("ffn0", "ffn1"):
                parts = [grads["ffn_w_up"], _rows_major(grads["ffn_w_down"])]
            elif stage == "mix1":
                grads.update(unprepare_grads(grads))
                parts = [cm16(grads["cd_w_in"]), cm16(grads["c_w_uq"]), cm16(grads["c_w_ukv"]),
                         _rows_major(grads["cd_w_out"]).astype(BF16)]
            else:
                parts = [grads["ab_w_in"], _rows_major(grads["ab_w_out"])]
            reducing[stage], tok = start_reduce(parts, ci, stage)
            before = {"mix1": "ffn1", "ffn0": "mix1", "mix0": "ffn0"}.get(stage)
            if before is not None:
                reduced[before] = finish_reduce(reducing[before], chip, ci, before, after)
            return tok

    loss, grad_x, dmod, by_stage = run_step(x, tgt, mod, ropes, w, Hooks())
    grads = merge_grads(by_stage)

    parts3 = [dmod] + [grads[n] for n, _ in _SMALL_GRADS] + [loss[0, 0]]
    rows3 = -(-sum(p.size for p in parts3) // LANES // 8) * 8
    small_handle, _ = split_start("small_grads_start", EVERYONE, [_pack_rows(parts3, rows3, F32)],
                                  [_sds((N_DEV, rows3, LANES))])
    red_up1, red_down1 = reduced["ffn1"]
    red_cd_in, red_uq, red_ukv, red_cd_out = reduced["mix1"]
    red_up0, red_down0 = reduced["ffn0"]
    out_grads = dict(cd_w_in=red_cd_in, c_w_uq=red_uq, c_w_ukv=red_ukv, cd_w_out=red_cd_out)
    per_layer = dict(ffn_w_up=(red_up0, red_up1), ffn_w_down=(red_down0, red_down1))
    updates = {}

    def update(n):
        if n in per_layer:
            updates[n] = adamw_layers(a[n], *per_layer[n], a["m_" + n], a["v_" + n], "adamw_" + n)
        else:
            updates[n] = adamw(a[n], out_grads[n].reshape(a[n].shape), a["m_" + n], a["v_" + n], "adamw_" + n)

    early =("ffn_w_up", "ffn_w_down", "cd_w_in", "c_w_uq", "c_w_ukv", "cd_w_out")
    for n in early:
        update(n)
    (mine,), (landed,) = split_wait("small_grads_wait", EVERYONE, small_handle, [updates[n][1] for n in early])
    g3 = lax.dynamic_update_index_in_dim(landed, mine, dev, 0)
    summed = sum8(g3).reshape(-1)
    nmod = 2 * N_MOD * D_MODEL
    out_grads["ada_b"] = summed[:nmod].reshape(2, N_MOD * D_MODEL)
    off = nmod
    for n, shp in _SMALL_GRADS:
        out_grads[n] = summed[off:off + _size(shp)].reshape(shp)
        off += _size(shp)
    loss = summed[off]
    for n, shp, axis in _SMALL_SHARDED:
        width = out_grads[n].shape[-1] // N_CHIPS
        out_grads[n] = lax.dynamic_slice_in_dim(out_grads[n], chip * width, width, axis=out_grads[n].ndim - 1)
    dmod_all = g3.reshape(N_DEV, rows3 * LANES)[:, :nmod].reshape(N_DEV, 2, N_MOD * D_MODEL)
    dmod_mine = lax.dynamic_slice_in_dim(dmod_all, chip * ncol, ncol, axis=2).transpose(1, 0, 2)
    updates["ada_w"] = adamw_ada(a["ada_w"], c_all, dmod_mine, a["m_ada_w"], a["v_ada_w"])

    red_in0, red_out0 = finish_reduce(reducing["mix0"], chip, ci, "mix0", updates["ada_w"][1])
    out_grads.update(ab_w_in=red_in0, ab_w_out=red_out0)

    for n in ("ab_w_in", "ab_w_out"):
        update(n)
    small = [n for n in _WEIGHTS if n not in updates]
    for n, res in zip(small, adamw_small([a[n] for n in small], [out_grads[n].reshape(a[n].shape) for n in small],
                                         [a["m_" + n] for n in small], [a["v_" + n] for n in small])):
        updates[n] = res
    return (loss, grad_x[None], *[updates[n][i] for i in range(4) for n in _WEIGHTS])
```

```python
import functools
from typing import NamedTuple

import jax
import jax.numpy as jnp
from jax import lax
from jax.experimental import pallas as pl
from jax.experimental.pallas import tpu as pltpu

F32 = jnp.float32
BF16 = jnp.bfloat16
EPS = 1e-6
D_MODEL = 1024
N_MOD = 6
A_WIDTH = 512
B_GROUPS = 4
C_HEADS = 8
C_NOPE = 64
C_ROPE = 32
C_V = 64
C_Q_RANK = 256
C_KV_RANK = 128
HEAD_PAD = 128
ROPE_THETA = 10000.0
D_GROUPS = 4
D_CHUNK = 128
D_FF = 2816
FF_UNIT = 128
ADAM_LR = 0.001
ADAM_B1 = 0.9
ADAM_B2 = 0.999
ADAM_EPS = 1e-08
ADAM_WD = 0.01
ADAM_STEP = 10
N_CHIPS = 4
N_DEV = 8
LANES = 128
VMEM_BIG = 56 * 1024 * 1024
MESH = pl.DeviceIdType.MESH


def _sds(shape, dtype=F32):
    return jax.ShapeDtypeStruct(tuple(shape), dtype)


def _tile(n, cap, mult=128):
    if n <= cap:
        return n
    best = None
    for t in range(mult, cap + 1, mult):
        if n % t == 0:
            best = t
    assert best is not None, (n, cap, mult)
    return best


def _params(dims=None, vmem=None):
    return pltpu.CompilerParams(dimension_semantics=dims, vmem_limit_bytes=vmem)


def _shift_down(v, k):
    r = pltpu.roll(v, k, axis=0)
    t = lax.broadcasted_iota(jnp.int32, v.shape, 0)
    return jnp.where(t >= k, r, 0.0)


def _shift_up(v, k):
    n = v.shape[0]
    r = pltpu.roll(v, n - k, axis=0)
    t = lax.broadcasted_iota(jnp.int32, v.shape, 0)
    return jnp.where(t < n - k, r, 0.0)


def _sigmoid(v):
    return 1.0 / (1.0 + jnp.exp(-v))


_GELU_C = 0.7978845608028654
_GELU_A = 0.044715


def _gelu(v):
    return 0.5 * v * (1.0 + jnp.tanh(_GELU_C * (v + _GELU_A * v * v * v)))


def _gelu_grad(v):
    th = jnp.tanh(_GELU_C * (v + _GELU_A * v * v * v))
    return 0.5 * (1.0 + th) + 0.5 * v * (1.0 - th * th) * _GELU_C * (1.0 + 3.0 * _GELU_A * v * v)


_NN = (((1,), (0,)), ((), ()))
_NT = (((1,), (1,)), ((), ()))
_TN = (((0,), (0,)), ((), ()))


def _dot(a, b, dims=_NN):
    return lax.dot_general(a, b, dims, preferred_element_type=F32)


def _logical(t, groups):
    return (t.shape[-2], t.shape[-1] * groups)


def _block(tr, tc, groups, cols, where):
    if groups == 1:
        return pl.BlockSpec((tr, tc), where)
    per = cols // groups // tc

    def index(i, j, s):
        r, c = where(i, j, s)
        return (c // per, r, c % per)

    return pl.BlockSpec((None, tr, tc), index)


def matmul(a, b, mode, out_dtype, name, ga=1, gb=1, go=1, tm=None, tn=None, tk=None):
    (ar, ac), (br, bc) = _logical(a, ga), _logical(b, gb)
    if mode == "nn":
        m, k, n = ar, ac, bc
        a_col, b_col = "k", "n"
    elif mode == "nt":
        m, k, n = ar, ac, br
        a_col, b_col = "k", "k"
    else:
        k, m, n = ar, ac, bc
        a_col, b_col = "m", "n"
    limit = {"m": m, "n": n // go, "k": k}
    limit[a_col] = min(limit[a_col], ac // ga)
    limit[b_col] = min(limit[b_col], bc // gb)
    tm = tm or _tile(limit["m"], 2048, 128 if mode == "tn" else 16)
    tn = tn or _tile(limit["n"], 512)
    tk = tk or _tile(limit["k"], 2048, 16 if mode == "tn" else 128)
    nk = k // tk
    if mode == "nn":
        a_spec = _block(tm, tk, ga, ac, lambda i, j, s: (i, s))
        b_spec = _block(tk, tn, gb, bc, lambda i, j, s: (s, j))
        dims = _NN
    elif mode == "nt":
        a_spec = _block(tm, tk, ga, ac, lambda i, j, s: (i, s))
        b_spec = _block(tn, tk, gb, bc, lambda i, j, s: (j, s))
        dims = _NT
    else:
        a_spec = _block(tk, tm, ga, ac, lambda i, j, s: (s, i))
        b_spec = _block(tk, tn, gb, bc, lambda i, j, s: (s, j))
        dims = _TN
    o_spec = _block(tm, tn, go, n, lambda i, j, s: (i, j))
    out_shape = _sds((m, n), out_dtype) if go == 1 else _sds((go, m, n // go), out_dtype)

    def body(a_ref, b_ref, o_ref, acc_ref):
        s = pl.program_id(2)

        @pl.when(s == 0)
        def _():
            acc_ref[...] = jnp.zeros_like(acc_ref)

        acc_ref[...] += _dot(a_ref[...], b_ref[...], dims)

        @pl.when(s == nk - 1)
        def _():
            o_ref[...] = acc_ref[...].astype(o_ref.dtype)

    return pl.pallas_call(
        body, name=name, out_shape=out_shape, grid=(m // tm, n // tn, nk),
        in_specs=[a_spec, b_spec], out_specs=o_spec,
        scratch_shapes=[pltpu.VMEM((tm, tn), F32)],
        compiler_params=_params(("parallel", "parallel", "arbitrary"), VMEM_BIG),
    )(a, b)


def _rows(tm, n):
    return pl.BlockSpec((tm, n), lambda i: (i, 0))


def _vec(n):
    return pl.BlockSpec((1, n), lambda i: (0, 0))


class Vec(NamedTuple):
    array: jax.Array
    row: int
    col: int


def _vec_in(v, d):
    return pl.BlockSpec((None, 1, d), lambda i: (v.row, 0, v.col))


def modnorm_fwd(x, g, sc, sh, name):
    s, d = x.shape
    tm = _tile(s, 256, 8)

    def body(x_ref, g_ref, sc_ref, sh_ref, o_ref):
        xv = x_ref[...]
        r = lax.rsqrt(jnp.mean(xv * xv, axis=-1, keepdims=True) + EPS)
        o_ref[...] = ((xv * r) * g_ref[...] * (1.0 + sc_ref[...]) + sh_ref[...]).astype(BF16)

    return pl.pallas_call(
        body, name=name, out_shape=_sds((s, d), BF16), grid=(s // tm,),
        in_specs=[_rows(tm, d), _vec_in(g, d), _vec_in(sc, d), _vec_in(sh, d)], out_specs=_rows(tm, d),
        compiler_params=_params(("parallel",)),
    )(x, g.array, sc.array, sh.array)


def norm_bwd(x, dh, g, sc, dres, name):
    s, d = x.shape
    tm, streams = _row_streams(s)
    nsteps = s // tm

    def body(x_ref, dh_ref, g_ref, sc_ref, dr_ref, dx_ref, dsh_ref, dsc_ref, dg_ref, a2_ref):
        i = pl.program_id(0)

        @pl.when(i == 0)
        def _():
            dsh_ref[...] = jnp.zeros_like(dsh_ref)
            a2_ref[...] = jnp.zeros_like(a2_ref)

        for rs in streams:
            xv = x_ref[rs, :]
            dh = dh_ref[rs, :].astype(F32)
            r = lax.rsqrt(jnp.mean(xv * xv, axis=-1, keepdims=True) + EPS)
            xh = xv * r
            dsh_ref[...] += jnp.sum(dh, axis=0, keepdims=True)
            a2_ref[...] += jnp.sum(dh * xh, axis=0, keepdims=True)
            dxh = dh * (g_ref[...] * (1.0 + sc_ref[...]))
            dx = r * (dxh - xh * jnp.mean(dxh * xh, axis=-1, keepdims=True))
            dx_ref[rs, :] = dr_ref[rs, :] + dx

        @pl.when(i == nsteps - 1)
        def _():
            dsc_ref[...] = a2_ref[...] * g_ref[...]
            dg_ref[...] = a2_ref[...] * (1.0 + sc_ref[...])

    return pl.pallas_call(
        body, name=name, out_shape=(_sds((s, d)), _sds((1, d)), _sds((1, d)), _sds((1, d))), grid=(nsteps,),
        in_specs=[_rows(tm, d), _rows(tm, d), _vec_in(g, d), _vec_in(sc, d), _rows(tm, d)],
        out_specs=(_rows(tm, d), _vec(d), _vec(d), _vec(d)),
        scratch_shapes=[pltpu.VMEM((1, d), F32)],
        compiler_params=_params(("arbitrary",), VMEM_BIG),
    )(x, dh, g.array, sc.array, dres)


_ROW_STREAM = 256


def _row_streams(s):
    tm = _tile(s, 2 * _ROW_STREAM, 8)
    sub = min(tm, _ROW_STREAM)
    return tm, [slice(r * sub, (r + 1) * sub) for r in range(tm // sub)]


def resid_modnorm_fwd(x, y, gate, g, sc, sh, name):
    s, d = x.shape
    tm, streams = _row_streams(s)

    def body(x_ref, y_ref, gate_ref, g_ref, sc_ref, sh_ref, xo_ref, h_ref):
        for rs in streams:
            xv = x_ref[rs, :] + gate_ref[...] * y_ref[rs, :].astype(F32)
            xo_ref[rs, :] = xv
            r = lax.rsqrt(jnp.mean(xv * xv, axis=-1, keepdims=True) + EPS)
            h_ref[rs, :] = ((xv * r) * g_ref[...] * (1.0 + sc_ref[...]) + sh_ref[...]).astype(BF16)

    return pl.pallas_call(
        body, name=name, out_shape=(_sds((s, d)), _sds((s, d), BF16)), grid=(s // tm,),
        in_specs=[_rows(tm, d), _rows(tm, d), _vec_in(gate, d), _vec_in(g, d), _vec_in(sc, d), _vec_in(sh, d)],
        out_specs=(_rows(tm, d), _rows(tm, d)),
        compiler_params=_params(("parallel",), VMEM_BIG),
    )(x, y, gate.array, g.array, sc.array, sh.array)


def norm_gate_bwd(x, dh, g, sc, dres, y, gate, name):
    s, d = x.shape
    tm, streams = _row_streams(s)
    nsteps = s // tm

    def body(x_ref, dh_ref, g_ref, sc_ref, dr_ref, y_ref, gate_ref, dx_ref, dsh_ref, dsc_ref, dg_ref, dy_ref,
             dgate_ref, a2_ref):
        i = pl.program_id(0)

        @pl.when(i == 0)
        def _():
            dsh_ref[...] = jnp.zeros_like(dsh_ref)
            a2_ref[...] = jnp.zeros_like(a2_ref)
            dgate_ref[...] = jnp.zeros_like(dgate_ref)

        for rs in streams:
            xv = x_ref[rs, :]
            dh = dh_ref[rs, :].astype(F32)
            r = lax.rsqrt(jnp.mean(xv * xv, axis=-1, keepdims=True) + EPS)
            xh = xv * r
            dsh_ref[...] += jnp.sum(dh, axis=0, keepdims=True)
            a2_ref[...] += jnp.sum(dh * xh, axis=0, keepdims=True)
            dxh = dh * (g_ref[...] * (1.0 + sc_ref[...]))
            dr = dr_ref[rs, :] + r * (dxh - xh * jnp.mean(dxh * xh, axis=-1, keepdims=True))
            dx_ref[rs, :] = dr
            dy_ref[rs, :] = (dr * gate_ref[...]).astype(BF16)
            dgate_ref[...] += jnp.sum(dr * y_ref[rs, :].astype(F32), axis=0, keepdims=True)

        @pl.when(i == nsteps - 1)
        def _():
            dsc_ref[...] = a2_ref[...] * g_ref[...]
            dg_ref[...] = a2_ref[...] * (1.0 + sc_ref[...])

    vec = _sds((1, d))
    return pl.pallas_call(
        body, name=name, out_shape=(_sds((s, d)), vec, vec, vec, _sds((s, d), BF16), vec), grid=(nsteps,),
        in_specs=[_rows(tm, d), _rows(tm, d), _vec_in(g, d), _vec_in(sc, d), _rows(tm, d), _rows(tm, d), _vec_in(gate, d)],
        out_specs=(_rows(tm, d), _vec(d), _vec(d), _vec(d), _rows(tm, d), _vec(d)),
        scratch_shapes=[pltpu.VMEM((1, d), F32)],
        compiler_params=_params(("arbitrary",), VMEM_BIG),
    )(x, dh, g.array, sc.array, dres, y, gate.array)


def final_fused(x, f, gate, g, tgt):
    s, d = x.shape
    tm, streams = _row_streams(s)

    def body(x_ref, f_ref, gate_ref, g_ref, t_ref, dx_ref, dg_ref, loss_ref, df_ref, dgate_ref):
        @pl.when(pl.program_id(0) == 0)
        def _():
            dg_ref[...] = jnp.zeros_like(dg_ref)
            loss_ref[...] = jnp.zeros_like(loss_ref)
            dgate_ref[...] = jnp.zeros_like(dgate_ref)

        gatev, gv = gate_ref[...], g_ref[...]
        for rs in streams:
            fv = f_ref[rs, :].astype(F32)
            xv = x_ref[rs, :] + gatev * fv
            r = lax.rsqrt(jnp.mean(xv * xv, axis=-1, keepdims=True) + EPS)
            xh = xv * r
            e = xh * gv - t_ref[rs, :]
            row = jnp.sum(e * e, axis=-1, keepdims=True) * (0.5 / d)
            loss_ref[...] += jnp.sum(row, axis=0, keepdims=True)
            dy = e * (1.0 / d)
            dg_ref[...] += jnp.sum(dy * xh, axis=0, keepdims=True)
            dxh = dy * gv
            dx = r * (dxh - xh * jnp.mean(dxh * xh, axis=-1, keepdims=True))
            dx_ref[rs, :] = dx
            df_ref[rs, :] = (dx * gatev).astype(BF16)
            dgate_ref[...] += jnp.sum(dx * fv, axis=0, keepdims=True)

    vec = _sds((1, d))
    return pl.pallas_call(
        body, name="final_fused", out_shape=(_sds((s, d)), vec, _sds((1, LANES)), _sds((s, d), BF16), vec),
        grid=(s // tm,),
        in_specs=[_rows(tm, d), _rows(tm, d), _vec_in(gate, d), _vec_in(g, d), _rows(tm, d)],
        out_specs=(_rows(tm, d), _vec(d), _vec(LANES), _rows(tm, d), _vec(d)),
        compiler_params=_params(("arbitrary",), VMEM_BIG),
    )(x, f, gate.array, g.array, tgt)


def _taps(v):
    return _shift_down(v, 2), _shift_down(v, 1), v


def _conv3_taps(taps, w):
    return w[0:1, :] * taps[0] + w[1:2, :] * taps[1] + w[2:3, :] * taps[2]


def _conv3(v, w):
    return _conv3_taps(_taps(v), w)


def _conv3_t(dv, w):
    return w[0:1, :] * _shift_up(dv, 2) + w[1:2, :] * _shift_up(dv, 1) + w[2:3, :] * dv


def _conv3_dw_taps(dv, taps):
    return jnp.concatenate([jnp.sum(dv * t, axis=0, keepdims=True) for t in taps], axis=0)


def _conv3_dw(dv, v):
    return _conv3_dw_taps(dv, _taps(v))


def gconv_fwd(z, conv_w):
    s = z.shape[0]
    nb = A_WIDTH // LANES

    def body(b_ref, c_ref, a_ref, w_ref, o_ref):
        b, c, a = b_ref[...].astype(F32), c_ref[...].astype(F32), a_ref[...].astype(F32)
        o_ref[...] = (b * _conv3(c * a, w_ref[...])).astype(BF16)

    col = lambda off: pl.BlockSpec((s, LANES), lambda j: (0, off + j))
    return pl.pallas_call(
        body, name="gconv_fwd", out_shape=_sds((s, A_WIDTH + _B_WIDTH), BF16), grid=(nb,),
        in_specs=[col(0), col(nb), col(2 * nb), pl.BlockSpec((3, LANES), lambda j: (0, j))],
        out_specs=pl.BlockSpec((s, LANES), lambda j: (0, j)),
        compiler_params=_params(("parallel",), VMEM_BIG),
    )(z, z, z, conv_w)


def gconv_bwd(z, conv_w, dycat):
    s = z.shape[0]
    nb = A_WIDTH // LANES

    def body(b_ref, c_ref, a_ref, w_ref, dy_ref, db_ref, dc_ref, da_ref, dw_ref):
        c, a, w, dy = c_ref[...].astype(F32), a_ref[...].astype(F32), w_ref[...], dy_ref[...].astype(F32)
        ca = c * a
        db_ref[...] = (dy * _conv3(ca, w)).astype(BF16)
        dconv = dy * b_ref[...].astype(F32)
        dw_ref[...] = _conv3_dw(dconv, ca)
        dca = _conv3_t(dconv, w)
        dc_ref[...] = (dca * a).astype(BF16)
        da_ref[...] = (dca * c).astype(BF16)

    col = lambda off: pl.BlockSpec((s, LANES), lambda j: (0, off + j))
    wspec = pl.BlockSpec((3, LANES), lambda j: (0, j))
    part = _sds((s, A_WIDTH), BF16)
    return pl.pallas_call(
        body, name="gconv_bwd", out_shape=(part, part, part, _sds((3, A_WIDTH))), grid=(nb,),
        in_specs=[col(0), col(nb), col(2 * nb), wspec, col(0)],
        out_specs=(col(0), col(0), col(0), wspec),
        compiler_params=_params(("parallel",), VMEM_BIG),
    )(z, z, z, conv_w, dycat)


def _pool_counts(s, w):
    t = lax.broadcasted_iota(jnp.int32, (s, 1), 0)
    return jnp.minimum(t + 1, w).astype(F32)


def _pooled(p, levels):
    acc = p
    for lv in range(levels):
        acc = acc + _shift_down(acc, 2 ** lv)
    return acc / _pool_counts(p.shape[0], 2 ** levels) - p


_B_WIDTH = B_GROUPS * LANES


def pool_fwd(z, mix_w, scale, ycat):
    s = z.shape[0]

    def body(p_ref, m_ref, sc_ref, ycat_ref, o_ref):
        del ycat_ref
        for g in range(B_GROUPS):
            cols = slice(g * LANES, (g + 1) * LANES)
            pooled = _pooled(p_ref[:, cols].astype(F32), g + 1)
            y = _dot(pooled.astype(BF16), m_ref[g].astype(BF16))
            o_ref[:, cols] = (y * sc_ref[:, cols]).astype(BF16)

    return pl.pallas_call(
        body, name="pool_fwd", out_shape=_sds(ycat.shape, BF16), grid=(1,),
        in_specs=[pl.BlockSpec((s, _B_WIDTH), lambda i: (0, 3 * A_WIDTH // _B_WIDTH)),
                  pl.BlockSpec((B_GROUPS, LANES, LANES), lambda i: (0, 0, 0)), pl.BlockSpec((1, _B_WIDTH), lambda i: (0, 0)),
                  pl.BlockSpec(memory_space=pl.ANY)],
        out_specs=pl.BlockSpec((s, _B_WIDTH), lambda i: (0, A_WIDTH // _B_WIDTH)),
        input_output_aliases={3: 0},
        compiler_params=_params(("arbitrary",), VMEM_BIG),
    )(z, mix_w, scale, ycat)


def pool_bwd(z, mix_w, scale, dycat):
    s = z.shape[0]

    def body(p_ref, m_ref, sc_ref, dy_ref, dp_ref, dm_ref, dsc_ref):
        for g in range(B_GROUPS):
            cols = slice(g * LANES, (g + 1) * LANES)
            pooled = _pooled(p_ref[:, cols].astype(F32), g + 1)
            mw = m_ref[g].astype(BF16)
            pb = pooled.astype(BF16)
            dy = dy_ref[:, cols].astype(F32)
            dsc_ref[:, cols] = jnp.sum(dy * _dot(pb, mw), axis=0, keepdims=True)
            dmix = (dy * sc_ref[:, cols]).astype(BF16)
            dm_ref[g] = _dot(pb, dmix, _TN)
            dpool = _dot(dmix, mw, _NT)
            acc = dpool / _pool_counts(s, 2 ** (g + 1))
            for lv in range(g + 1):
                acc = acc + _shift_up(acc, 2 ** lv)
            dp_ref[:, cols] = (acc - dpool).astype(BF16)

    wide = lambda c: pl.BlockSpec((s, _B_WIDTH), lambda i: (0, c))
    mspec = pl.BlockSpec((B_GROUPS, LANES, LANES), lambda i: (0, 0, 0))
    vspec = pl.BlockSpec((1, _B_WIDTH), lambda i: (0, 0))
    return pl.pallas_call(
        body, name="pool_bwd", out_shape=(_sds((s, _B_WIDTH), BF16), _sds((B_GROUPS, LANES, LANES)), _sds((1, _B_WIDTH))),
        grid=(1,), in_specs=[wide(3 * A_WIDTH // _B_WIDTH), mspec, vspec, wide(A_WIDTH // _B_WIDTH)],
        out_specs=(wide(0), mspec, vspec),
        compiler_params=_params(("arbitrary",), VMEM_BIG),
    )(z, mix_w, scale, dycat)


_FF_BLOCKS = D_FF // FF_UNIT


def _ff_spec(s):
    return pl.BlockSpec((2, s, FF_UNIT), lambda j: (0, 0, j))


def _ff_wspecs():
    return [pl.BlockSpec((3, FF_UNIT), lambda j: (0, j)), pl.BlockSpec((3, FF_UNIT), lambda j: (0, _FF_BLOCKS + j))]


_FF_ROWS = 256
_FF_HALO = 16


def _chunk_taps(z_ref, half, c):
    start = pl.multiple_of(c * _FF_ROWS, _FF_ROWS)
    before = pl.multiple_of(jnp.maximum(c * _FF_ROWS - _FF_HALO, 0), _FF_HALO)
    halo = z_ref[half, pl.ds(before, _FF_HALO), :].astype(F32)
    halo = jnp.where(c > 0, halo, 0.0)
    win = jnp.concatenate([halo, z_ref[half, pl.ds(start, _FF_ROWS), :].astype(F32)], axis=0)
    return tuple(pltpu.roll(win, k, axis=0)[_FF_HALO:] for k in (2, 1)) + (win[_FF_HALO:],)


def _fold8(v):
    acc = v[0:8]
    for r in range(8, v.shape[0], 8):
        acc = acc + v[r:r + 8]
    return acc


_FF_CHUNK = 256


def ffn_act_down(zf, conv_w, w_down, name):
    s, d = zf.shape[1], w_down.shape[1]
    nk = D_FF // _FF_CHUNK
    chunk = lambda k: jnp.minimum(k, nk - 1)

    def body(z_ref, wg_ref, wu_ref, wd_ref, a_ref, f_ref, held_ref, acc_ref):
        k = pl.program_id(0)

        @pl.when(k == 0)
        def _():
            held_ref[...] = jnp.zeros_like(held_ref)
            acc_ref[...] = jnp.zeros_like(acc_ref)

        acc_ref[...] += _dot(held_ref[(k + 1) % 2], wd_ref[...])
        g = _conv3(z_ref[0].astype(F32), wg_ref[...])
        u = _conv3(z_ref[1].astype(F32), wu_ref[...])
        act = (g * _sigmoid(g) * u).astype(BF16)
        a_ref[...] = act
        held_ref[k % 2] = act

        @pl.when(k == nk)
        def _():
            f_ref[...] = acc_ref[...].astype(BF16)

    return pl.pallas_call(
        body, name=name, out_shape=(_sds((s, D_FF), BF16), _sds((s, d), BF16)), grid=(nk + 1,),
        in_specs=[pl.BlockSpec((2, s, _FF_CHUNK), lambda k: (0, 0, chunk(k))),
                  pl.BlockSpec((3, _FF_CHUNK), lambda k: (0, chunk(k))),
                  pl.BlockSpec((3, _FF_CHUNK), lambda k: (0, nk + chunk(k))),
                  pl.BlockSpec((_FF_CHUNK, d), lambda k: (jnp.maximum(k - 1, 0), 0))],
        out_specs=(pl.BlockSpec((s, _FF_CHUNK), lambda k: (0, chunk(k))), pl.BlockSpec((s, d), lambda k: (0, 0))),
        scratch_shapes=[pltpu.VMEM((2, s, _FF_CHUNK), BF16), pltpu.VMEM((s, d), F32)],
        compiler_params=_params(("arbitrary",), VMEM_BIG),
    )(zf, conv_w, conv_w, w_down)


def ffn_act_bwd(zf, conv_w, da, name):
    s = zf.shape[1]
    assert s % _FF_ROWS == 0
    nchunks = s // _FF_ROWS

    def body(z_ref, wg_ref, wu_ref, da_ref, dz_ref, dw_ref, dg_ref, du_ref):
        wg, wu = wg_ref[...], wu_ref[...]

        def first(c, acc):
            rows = pl.ds(pl.multiple_of(c * _FF_ROWS, _FF_ROWS), _FF_ROWS)
            tg, tu = _chunk_taps(z_ref, 0, c), _chunk_taps(z_ref, 1, c)
            g = _conv3_taps(tg, wg)
            u = _conv3_taps(tu, wu)
            dav = da_ref[rows, :].astype(F32)
            sg = _sigmoid(g)
            dg = dav * u * (sg * (1.0 + g * (1.0 - sg)))
            du = dav * (g * sg)
            dg_ref[rows, :] = dg
            du_ref[rows, :] = du
            return tuple(a + _fold8(d * t) for a, (d, t) in zip(acc, [(dg, t) for t in tg] + [(du, t) for t in tu]))

        zero = jnp.zeros((8, FF_UNIT), F32)
        acc = lax.fori_loop(0, nchunks, first, (zero,) * 6)
        sums = [jnp.sum(a, axis=0, keepdims=True) for a in acc]
        dw_ref[0] = jnp.concatenate(sums[:3], axis=0)
        dw_ref[1] = jnp.concatenate(sums[3:], axis=0)

        tail = pl.ds(s, _FF_HALO)
        dg_ref[tail, :] = jnp.zeros((_FF_HALO, FF_UNIT), F32)
        du_ref[tail, :] = jnp.zeros((_FF_HALO, FF_UNIT), F32)
        span = _FF_ROWS + _FF_HALO

        def second(c, carry):
            start = pl.multiple_of(c * _FF_ROWS, _FF_ROWS)
            for half, (d_ref, w) in enumerate(((dg_ref, wg), (du_ref, wu))):
                win = d_ref[pl.ds(start, span), :]
                dz = (w[0:1, :] * pltpu.roll(win, span - 2, axis=0)[:_FF_ROWS]
                      + w[1:2, :] * pltpu.roll(win, span - 1, axis=0)[:_FF_ROWS] + w[2:3, :] * win[:_FF_ROWS])
                dz_ref[half, pl.ds(start, _FF_ROWS), :] = dz.astype(BF16)
            return carry

        lax.fori_loop(0, nchunks, second, 0)

    return pl.pallas_call(
        body, name=name, out_shape=(_sds((2, s, D_FF), BF16), _sds((2, 3, D_FF))), grid=(_FF_BLOCKS,),
        in_specs=[_ff_spec(s)] + _ff_wspecs() + [pl.BlockSpec((s, FF_UNIT), lambda j: (0, j))],
        out_specs=(_ff_spec(s), pl.BlockSpec((2, 3, FF_UNIT), lambda j: (0, 0, j))),
        scratch_shapes=[pltpu.VMEM((s + _FF_HALO, FF_UNIT), F32), pltpu.VMEM((s + _FF_HALO, FF_UNIT), F32)],
        compiler_params=_params(("parallel",), VMEM_BIG),
    )(zf, conv_w, conv_w, da)


def _rope(v, cs, s1, s2):
    return v * cs + pltpu.roll(v, LANES - C_ROPE // 2, axis=1) * s1 + pltpu.roll(v, C_ROPE // 2, axis=1) * s2


def _rope_t(dv, cs, s1, s2):
    return dv * cs + pltpu.roll(dv * s1, C_ROPE // 2, axis=1) + pltpu.roll(dv * s2, LANES - C_ROPE // 2, axis=1)


def _kpe_mask(shape):
    lane = lax.broadcasted_iota(jnp.int32, shape, 1)
    return (lane >= C_NOPE) & (lane < C_NOPE + C_ROPE)


def _rms(v, g):
    r = lax.rsqrt(jnp.mean(v * v, axis=-1, keepdims=True) + EPS)
    return v * r, r


def _rms_bwd(dn, xh, r, g):
    dxh = dn * g
    return r * (dxh - xh * jnp.mean(dxh * xh, axis=-1, keepdims=True)), jnp.sum(dn * xh, axis=0, keepdims=True)


_ZQ = C_Q_RANK + C_KV_RANK + HEAD_PAD
_HW = C_HEADS * HEAD_PAD


_MLA_ROWS = 256


def _mla_tiles(s):
    tm = _tile(s, 2 * _MLA_ROWS, 8)
    sub = min(tm, _MLA_ROWS)
    return tm, [slice(r * sub, (r + 1) * sub) for r in range(tm // sub)]


def mla_pre_fwd(z, gq, gkv, wq, wk, wv, cs, s1, s2):
    s = z.shape[0]
    tm, streams = _mla_tiles(s)

    def body(z_ref, gq_ref, gkv_ref, wq_ref, wk_ref, wv_ref, cs_ref, s1_ref, s2_ref, q_ref, k_ref, v_ref):
        for rs in streams:
            zv = z_ref[rs, :].astype(F32)
            cst, s1t, s2t = cs_ref[rs, :], s1_ref[rs, :], s2_ref[rs, :]
            qh, _ = _rms(zv[:, :C_Q_RANK], None)
            qn = (qh * gq_ref[...]).astype(BF16)
            q = _dot(qn, wq_ref[...])
            kh, _ = _rms(zv[:, C_Q_RANK:C_Q_RANK + C_KV_RANK], None)
            kvn = (kh * gkv_ref[...]).astype(BF16)
            k = _dot(kvn, wk_ref[...])
            v_ref[rs, :] = _dot(kvn, wv_ref[...]).astype(BF16)
            kpe = _rope(zv[:, C_Q_RANK + C_KV_RANK:], cst, s1t, s2t)
            for h in range(C_HEADS):
                sl = slice(h * HEAD_PAD, (h + 1) * HEAD_PAD)
                q_ref[rs, sl] = _rope(q[:, sl], cst, s1t, s2t).astype(BF16)
                k_ref[rs, sl] = (k[:, sl] + kpe).astype(BF16)

    full = lambda r, c: pl.BlockSpec((r, c), lambda i: (0, 0))
    hw = _sds((s, _HW), BF16)
    return pl.pallas_call(
        body, name="mla_pre_fwd", out_shape=(hw, hw, hw), grid=(s // tm,),
        in_specs=[_rows(tm, _ZQ), _vec(C_Q_RANK), _vec(C_KV_RANK), full(C_Q_RANK, _HW), full(C_KV_RANK, _HW),
                  full(C_KV_RANK, _HW), _rows(tm, LANES), _rows(tm, LANES), _rows(tm, LANES)],
        out_specs=(_rows(tm, _HW), _rows(tm, _HW), _rows(tm, _HW)),
        compiler_params=_params(("parallel",), VMEM_BIG),
    )(z, gq, gkv, wq, wk, wv, cs, s1, s2)


def mla_pre_bwd(z, gq, gkv, wq, wk, wv, cs, s1, s2, dq, dk, dv):
    s = z.shape[0]
    tm, streams = _mla_tiles(s)

    def body(z_ref, gq_ref, gkv_ref, wq_ref, wk_ref, wv_ref, cs_ref, s1_ref, s2_ref, dq_ref, dk_ref, dv_ref,
             dz_ref, dwq_ref, dwk_ref, dwv_ref, dgq_ref, dgkv_ref):
        @pl.when(pl.program_id(0) == 0)
        def _():
            dwq_ref[...] = jnp.zeros_like(dwq_ref)
            dwk_ref[...] = jnp.zeros_like(dwk_ref)
            dwv_ref[...] = jnp.zeros_like(dwv_ref)
            dgq_ref[...] = jnp.zeros_like(dgq_ref)
            dgkv_ref[...] = jnp.zeros_like(dgkv_ref)

        gqv, gkvv = gq_ref[...], gkv_ref[...]
        for rs in streams:
            zv = z_ref[rs, :].astype(F32)
            cst, s1t, s2t = cs_ref[rs, :], s1_ref[rs, :], s2_ref[rs, :]
            qh, rq = _rms(zv[:, :C_Q_RANK], None)
            qn = (qh * gqv).astype(BF16)
            kh, rk = _rms(zv[:, C_Q_RANK:C_Q_RANK + C_KV_RANK], None)
            kvn = (kh * gkvv).astype(BF16)

            dqv = dq_ref[rs, :].astype(F32)
            dqp = jnp.concatenate(
                [_rope_t(dqv[:, h * HEAD_PAD:(h + 1) * HEAD_PAD], cst, s1t, s2t) for h in range(C_HEADS)], axis=1
            ).astype(BF16)
            dwq_ref[...] += _dot(qn, dqp, _TN)
            dqn = _dot(dqp, wq_ref[...], _NT)
            dql, dgq = _rms_bwd(dqn, qh, rq, gqv)
            dgq_ref[...] += dgq

            dkv = dk_ref[rs, :]
            dkb = dkv.astype(BF16)
            dvb = dv_ref[rs, :].astype(BF16)
            dwk_ref[...] += _dot(kvn, dkb, _TN)
            dwv_ref[...] += _dot(kvn, dvb, _TN)
            dkvn = _dot(dkb, wk_ref[...], _NT) + _dot(dvb, wv_ref[...], _NT)
            dkl, dgkv = _rms_bwd(dkvn, kh, rk, gkvv)
            dgkv_ref[...] += dgkv

            dkpe = dkv[:, :HEAD_PAD]
            for h in range(1, C_HEADS):
                dkpe = dkpe + dkv[:, h * HEAD_PAD:(h + 1) * HEAD_PAD]
            dkpe = _rope_t(jnp.where(_kpe_mask(dkpe.shape), dkpe, 0.0), cst, s1t, s2t)
            dz_ref[rs, :] = jnp.concatenate([dql, dkl, dkpe], axis=1).astype(BF16)

    full = lambda r, c: pl.BlockSpec((r, c), lambda i: (0, 0))
    return pl.pallas_call(
        body, name="mla_pre_bwd",
        out_shape=(_sds((s, _ZQ), BF16), _sds((C_Q_RANK, _HW)), _sds((C_KV_RANK, _HW)), _sds((C_KV_RANK, _HW)),
                   _sds((1, C_Q_RANK)), _sds((1, C_KV_RANK))),
        grid=(s // tm,),
        in_specs=[_rows(tm, _ZQ), _vec(C_Q_RANK), _vec(C_KV_RANK), full(C_Q_RANK, _HW), full(C_KV_RANK, _HW),
                  full(C_KV_RANK, _HW), _rows(tm, LANES), _rows(tm, LANES), _rows(tm, LANES),
                  _rows(tm, _HW), _rows(tm, _HW), _rows(tm, _HW)],
        out_specs=(_rows(tm, _ZQ), full(C_Q_RANK, _HW), full(C_KV_RANK, _HW), full(C_KV_RANK, _HW),
                   _vec(C_Q_RANK), _vec(C_KV_RANK)),
        compiler_params=_params(("arbitrary",), VMEM_BIG),
    )(z, gq, gkv, wq, wk, wv, cs, s1, s2, dq, dk, dv)


_ATT_SCALE = (C_NOPE + C_ROPE) ** -0.5
_NEG = -1e30


def _att_exp(q, k, row0, ends_here):
    sc = _dot(q, k, _NT) * _ATT_SCALE
    tq, nk = sc.shape
    if ends_here:
        last = sc[:, nk - tq:]
        row = lax.broadcasted_iota(jnp.int32, last.shape, 0)
        col = lax.broadcasted_iota(jnp.int32, last.shape, 1)
        last = jnp.where(col <= row, last, _NEG)
        sc = last if nk == tq else jnp.concatenate([sc[:, :nk - tq], last], axis=1)
    else:
        qpos = row0 + lax.broadcasted_iota(jnp.int32, sc.shape, 0)
        kpos = lax.broadcasted_iota(jnp.int32, sc.shape, 1)
        sc = jnp.where(kpos <= qpos, sc, _NEG)
    e = jnp.exp(sc - jnp.max(sc, axis=-1, keepdims=True))
    return e, 1.0 / jnp.sum(e, axis=-1, keepdims=True)


def _causal_cases(i, nq, tq, fn):
    if nq > 8:
        fn(nq * tq, False)
        return
    for blk in range(nq):
        pl.when(i == blk)(functools.partial(fn, (blk + 1) * tq, True))


_FWD_HEADS_PER_STEP = 4
_BWD_HEADS_PER_STEP = 2


def _head_lanes(heads):
    return [slice(h * HEAD_PAD, (h + 1) * HEAD_PAD) for h in range(heads)]


def attn_fwd(q, k, v):
    s = q.shape[0]
    tq = _tile(s, 256, 8)
    nq = s // tq
    heads = _FWD_HEADS_PER_STEP
    wide = heads * HEAD_PAD

    def body(q_ref, k_ref, v_ref, o_ref):
        i = pl.program_id(1)

        def case(nk, ends_here):
            for hd in _head_lanes(heads):
                e, inv = _att_exp(q_ref[:, hd], k_ref[:nk, hd], i * tq, ends_here)
                o_ref[:, hd] = (_dot(e.astype(BF16), v_ref[:nk, hd]) * inv).astype(BF16)

        _causal_cases(i, nq, tq, case)

    qspec = pl.BlockSpec((tq, wide), lambda h, i: (i, h))
    kspec = pl.BlockSpec((s, wide), lambda h, i: (0, h))
    return pl.pallas_call(
        body, name="attn_fwd", out_shape=_sds((s, _HW + _DW), BF16), grid=(C_HEADS // heads, s // tq),
        in_specs=[qspec, kspec, kspec], out_specs=qspec,
        compiler_params=_params(("parallel", "parallel"), VMEM_BIG),
    )(q, k, v)


def attn_bwd(q, k, v, o, do_all):
    s = q.shape[0]
    tq = _tile(s, 256, 8)
    heads = _BWD_HEADS_PER_STEP
    wide = heads * HEAD_PAD

    def body(q_ref, k_ref, v_ref, o_ref, do_ref, dq_ref, dk_ref, dv_ref):
        i = pl.program_id(1)

        @pl.when(i == 0)
        def _():
            dk_ref[...] = jnp.zeros_like(dk_ref)
            dv_ref[...] = jnp.zeros_like(dv_ref)

        def case(nk, ends_here):
            for hd in _head_lanes(heads):
                qv, kv, vv, dov = q_ref[:, hd], k_ref[:nk, hd], v_ref[:nk, hd], do_ref[:, hd]
                e, inv = _att_exp(qv, kv, i * tq, ends_here)
                p = e * inv
                dp = _dot(dov, vv, _NT)
                delta = jnp.sum(dov.astype(F32) * o_ref[:, hd].astype(F32), axis=-1, keepdims=True)
                ds = (p * (dp - delta) * _ATT_SCALE).astype(BF16)
                dq_ref[:, hd] = _dot(ds, kv).astype(BF16)
                dk_ref[:nk, hd] += _dot(ds, qv, _TN)
                dv_ref[:nk, hd] += _dot(p.astype(BF16), dov, _TN)

        _causal_cases(i, s // tq, tq, case)

    qspec = pl.BlockSpec((tq, wide), lambda h, i: (i, h))
    kspec = pl.BlockSpec((s, wide), lambda h, i: (0, h))
    return pl.pallas_call(
        body, name="attn_bwd", out_shape=(_sds((s, _HW), BF16), _sds((s, _HW)), _sds((s, _HW))),
        grid=(C_HEADS // heads, s // tq),
        in_specs=[qspec, kspec, kspec, qspec, qspec], out_specs=(qspec, kspec, kspec),
        compiler_params=_params(("parallel", "arbitrary"), VMEM_BIG),
    )(q, k, v, o, do_all)


_DW = D_GROUPS * LANES


def _tril_bf16(w):
    r = lax.broadcasted_iota(jnp.int32, w.shape, 0)
    c = lax.broadcasted_iota(jnp.int32, w.shape, 1)
    return jnp.where(c <= r, w, 0.0).astype(BF16)


def _sgu_forward(zu, zv, lg, lb, ws_ref, bs):
    u = _gelu(zu)
    v = _gelu(zv)
    mu = jnp.mean(v, axis=-1, keepdims=True)
    vc = v - mu
    rstd = lax.rsqrt(jnp.mean(vc * vc, axis=-1, keepdims=True) + EPS)
    xh = vc * rstd
    vln = (xh * lg + lb).astype(BF16)
    mixed = []
    for g in range(D_GROUPS):
        wg = _tril_bf16(ws_ref[g])
        mixed.append(_dot(wg, vln[:, g * LANES:(g + 1) * LANES]) + bs[:, g:g + 1])
    return u, xh, rstd, vln, jnp.concatenate(mixed, axis=1)


_SGU_CHUNKS = 4


def sgu_fwd(z, lg, lb, ws, bs_t, ycat):
    s = z.shape[0]
    rows = _SGU_CHUNKS * D_CHUNK

    def body(zu_ref, zv_ref, lg_ref, lb_ref, ws_ref, bs_ref, ycat_ref, o_ref):
        del ycat_ref
        for c in range(_SGU_CHUNKS):
            rs = slice(c * D_CHUNK, (c + 1) * D_CHUNK)
            u, _, _, _, mixed = _sgu_forward(zu_ref[rs, :].astype(F32), zv_ref[rs, :].astype(F32), lg_ref[...],
                                             lb_ref[...], ws_ref, bs_ref[...])
            o_ref[rs, :] = (u * mixed).astype(BF16)

    return pl.pallas_call(
        body, name="sgu_fwd", out_shape=_sds(ycat.shape, BF16), grid=(s // rows,),
        in_specs=[pl.BlockSpec((rows, _DW), lambda n: (n, 1)), pl.BlockSpec((rows, _DW), lambda n: (n, 2)),
                  _vec(_DW), _vec(_DW), pl.BlockSpec((D_GROUPS, D_CHUNK, D_CHUNK), lambda n: (0, 0, 0)),
                  pl.BlockSpec((D_CHUNK, LANES), lambda n: (0, 0)), pl.BlockSpec(memory_space=pl.ANY)],
        out_specs=pl.BlockSpec((rows, _DW), lambda n: (n, _HW // _DW)),
        input_output_aliases={6: 0},
        compiler_params=_params(("parallel",)),
    )(z, z, lg, lb, ws, bs_t, ycat)


def sgu_bwd(z, lg, lb, ws, bs_t, dycat, dy_col):
    s = z.shape[0]
    rows = _SGU_CHUNKS * D_CHUNK

    def body(zu_ref, zv_ref, lg_ref, lb_ref, ws_ref, bs_ref, dy_ref, dzu_ref, dzv_ref, dws_ref, dbs_ref, dlg_ref,
             dlb_ref):
        @pl.when(pl.program_id(0) == 0)
        def _():
            dws_ref[...] = jnp.zeros_like(dws_ref)
            dbs_ref[...] = jnp.zeros_like(dbs_ref)
            dlg_ref[...] = jnp.zeros_like(dlg_ref)
            dlb_ref[...] = jnp.zeros_like(dlb_ref)

        lg = lg_ref[...]
        lane = lax.broadcasted_iota(jnp.int32, (D_CHUNK, LANES), 1)
        row = lax.broadcasted_iota(jnp.int32, (D_CHUNK, D_CHUNK), 0)
        colm = lax.broadcasted_iota(jnp.int32, (D_CHUNK, D_CHUNK), 1)
        for c in range(_SGU_CHUNKS):
            rs = slice(c * D_CHUNK, (c + 1) * D_CHUNK)
            zu, zv = zu_ref[rs, :].astype(F32), zv_ref[rs, :].astype(F32)
            u, xh, rstd, vln, mixed = _sgu_forward(zu, zv, lg, lb_ref[...], ws_ref, bs_ref[...])
            dy = dy_ref[rs, :].astype(F32)
            dzu_ref[rs, :] = (dy * mixed * _gelu_grad(zu)).astype(BF16)
            dmix = dy * u
            dvln = []
            dbs = jnp.zeros((D_CHUNK, LANES), F32)
            for g in range(D_GROUPS):
                sl = slice(g * LANES, (g + 1) * LANES)
                dmg = dmix[:, sl]
                dbs = dbs + jnp.where(lane == g, jnp.sum(dmg, axis=-1, keepdims=True), 0.0)
                dmb = dmg.astype(BF16)
                dws_ref[g] += jnp.where(colm <= row, _dot(dmb, vln[:, sl], _NT), 0.0)
                dvln.append(_dot(_tril_bf16(ws_ref[g]), dmb, _TN))
            dbs_ref[...] += dbs
            dvln = jnp.concatenate(dvln, axis=1)
            dlg_ref[...] += jnp.sum(dvln * xh, axis=0, keepdims=True)
            dlb_ref[...] += jnp.sum(dvln, axis=0, keepdims=True)
            dxh = dvln * lg
            dvv = rstd * (dxh - jnp.mean(dxh, axis=-1, keepdims=True)
                          - xh * jnp.mean(dxh * xh, axis=-1, keepdims=True))
            dzv_ref[rs, :] = (dvv * _gelu_grad(zv)).astype(BF16)

    wsspec = pl.BlockSpec((D_GROUPS, D_CHUNK, D_CHUNK), lambda n: (0, 0, 0))
    chunk = lambda cidx: pl.BlockSpec((rows, _DW), lambda n: (n, cidx))
    return pl.pallas_call(
        body, name="sgu_bwd",
        out_shape=(_sds((s, _DW), BF16), _sds((s, _DW), BF16), _sds((D_GROUPS, D_CHUNK, D_CHUNK)),
                   _sds((D_CHUNK, LANES)), _sds((1, _DW)), _sds((1, _DW))),
        grid=(s // rows,),
        in_specs=[chunk(1), chunk(2), _vec(_DW), _vec(_DW), wsspec, pl.BlockSpec((D_CHUNK, LANES), lambda n: (0, 0)),
                  chunk(dy_col)],
        out_specs=(chunk(0), chunk(0), wsspec, pl.BlockSpec((D_CHUNK, LANES), lambda n: (0, 0)), _vec(_DW), _vec(_DW)),
        compiler_params=_params(("arbitrary",)),
    )(z, z, lg, lb, ws, bs_t, dycat)


def ada_mod(c_all, ada_w, ada_b):
    nl, d, n = ada_w.shape
    nb = c_all.shape[0]
    tn = _tile(n, 512)

    def body(c_ref, w_ref, b_ref, o_ref):
        cv = c_ref[...]
        ca = (cv * _sigmoid(cv)).astype(BF16)
        o_ref[...] = _dot(ca, w_ref[...].astype(BF16)) + b_ref[...]

    return pl.pallas_call(
        body, name="ada_mod", out_shape=_sds((nl, nb, n)), grid=(nl, n // tn),
        in_specs=[pl.BlockSpec((nb, d), lambda l, j: (0, 0)), pl.BlockSpec((None, d, tn), lambda l, j: (l, 0, j)),
                  pl.BlockSpec((None, 1, tn), lambda l, j: (l, 0, j))],
        out_specs=pl.BlockSpec((None, nb, tn), lambda l, j: (l, 0, j)),
        compiler_params=_params(("parallel", "parallel")),
    )(c_all, ada_w, ada_b.reshape(nl, 1, n))


_ADAM_BLOCK = 512 * 1024


def _adam_rows(rows, cols):
    if rows * cols <= _ADAM_BLOCK or rows % 8:
        return rows
    return _tile(rows, max(8, _ADAM_BLOCK // cols), 8)


def _adam_update(w, gv, m, v):
    inv_bc1 = 1.0 / (1.0 - ADAM_B1 ** ADAM_STEP)
    inv_bc2 = 1.0 / (1.0 - ADAM_B2 ** ADAM_STEP)
    nm = ADAM_B1 * m + (1.0 - ADAM_B1) * gv
    nv = ADAM_B2 * v + (1.0 - ADAM_B2) * (gv * gv)
    return -ADAM_LR * ((nm * inv_bc1) / (jnp.sqrt(nv * inv_bc2) + ADAM_EPS) + ADAM_WD * w), nm, nv


def adamw(w, g, m, v, name):
    shape = w.shape
    cols = shape[-1]
    rows = w.size // cols
    tr = _adam_rows(rows, cols)

    def body(w_ref, g_ref, m_ref, v_ref, go_ref, d_ref, nm_ref, nv_ref):
        gv = g_ref[...]
        go_ref[...] = gv
        d_ref[...], nm_ref[...], nv_ref[...] = _adam_update(w_ref[...], gv, m_ref[...], v_ref[...])

    spec = pl.BlockSpec((tr, cols), lambda i: (i, 0))
    out = _sds((rows, cols))
    r2 = lambda t: t.reshape(rows, cols)
    res = pl.pallas_call(
        body, name=name, out_shape=(out,) * 4, grid=(rows // tr,),
        in_specs=[spec] * 4, out_specs=(spec,) * 4, compiler_params=_params(("parallel",), VMEM_BIG),
    )(r2(w), r2(g), r2(m), r2(v))
    return tuple(t.reshape(shape) for t in res)


def adamw_ada(w, c_all, dmod, m, v):
    nl, d, n = w.shape
    tr = _adam_rows(d, n)
    pad = 16 - c_all.shape[0]
    c16 = jnp.pad(c_all, ((0, pad), (0, 0)))
    dm16 = jnp.pad(dmod, ((0, 0), (0, pad), (0, 0)))

    def body(w_ref, c_ref, dm_ref, m_ref, v_ref, g_ref, d_ref, nm_ref, nv_ref):
        cv = c_ref[...]
        gv = _dot((cv * _sigmoid(cv)).astype(BF16), dm_ref[...].astype(BF16), _TN)
        g_ref[...] = gv
        d_ref[...], nm_ref[...], nv_ref[...] = _adam_update(w_ref[...], gv, m_ref[...], v_ref[...])

    spec = pl.BlockSpec((None, tr, n), lambda l, i: (l, i, 0))
    out = _sds((nl, d, n))
    return pl.pallas_call(
        body, name="adamw_ada_w", out_shape=(out, out, out, out), grid=(nl, d // tr),
        in_specs=[spec, pl.BlockSpec((16, tr), lambda l, i: (0, i)), pl.BlockSpec((None, 16, n), lambda l, i: (l, 0, 0)),
                  spec, spec],
        out_specs=(spec,) * 4, compiler_params=_params(("parallel", "parallel"), VMEM_BIG),
    )(w, c16, dm16, m, v)


def adamw_small(ws, gs, ms, vs):
    n = len(ws)
    flat = lambda t: t.reshape(-1, t.shape[-1])

    def body(*refs):
        ins, outs = refs[:4 * n], refs[4 * n:]
        for i in range(n):
            w_ref, g_ref, m_ref, v_ref = ins[4 * i:4 * i + 4]
            outs[3 * i][...], outs[3 * i + 1][...], outs[3 * i + 2][...] = _adam_update(
                w_ref[...], g_ref[...], m_ref[...], v_ref[...])

    operands = [flat(t) for quad in zip(ws, gs, ms, vs) for t in quad]
    res = pl.pallas_call(
        body, name="adamw_small", out_shape=tuple(_sds(flat(w).shape) for w in ws for _ in range(3)),
    )(*operands)
    return [(g, res[3 * i].reshape(w.shape), res[3 * i + 1].reshape(w.shape), res[3 * i + 2].reshape(w.shape))
            for i, (w, g) in enumerate(zip(ws, gs))]


def adamw_layers(w, g0, g1, m, v, name):
    _, rows, cols = w.shape
    tr = _adam_rows(rows, cols)

    def body(w_ref, g0_ref, g1_ref, m_ref, v_ref, g_ref, d_ref, nm_ref, nv_ref):
        gv = jnp.where(pl.program_id(0) == 0, g0_ref[...], g1_ref[...])
        g_ref[...] = gv
        d_ref[...], nm_ref[...], nv_ref[...] = _adam_update(w_ref[...], gv, m_ref[...], v_ref[...])

    spec = pl.BlockSpec((None, tr, cols), lambda l, i: (l, i, 0))
    gspec = pl.BlockSpec((tr, cols), lambda l, i: (i, 0))
    out = _sds((2, rows, cols))
    return pl.pallas_call(
        body, name=name, out_shape=(out, out, out, out), grid=(2, rows // tr),
        in_specs=[spec, gspec, gspec, spec, spec], out_specs=(spec,) * 4,
        compiler_params=_params(("parallel", "parallel"), VMEM_BIG),
    )(w, g0, g1, m, v)


def sum8(gathered):
    _, r, _ = gathered.shape
    tr = _tile(r, 512, 8)

    def body(g_ref, o_ref):
        acc = g_ref[0]
        for dev in range(1, N_DEV):
            acc = acc + g_ref[dev]
        o_ref[...] = acc

    return pl.pallas_call(
        body, name="sum8", out_shape=_sds((r, LANES)), grid=(r // tr,),
        in_specs=[pl.BlockSpec((N_DEV, tr, LANES), lambda i: (0, i, 0))], out_specs=pl.BlockSpec((tr, LANES), lambda i: (i, 0)),
        compiler_params=_params(("parallel",)),
    )(gathered)


_SUM_STEPS = 2


def pair_sums(gs, recvs, core, name):
    n = len(gs)

    def body(c_ref, *refs):
        del c_ref
        for i in range(n):
            a_ref, b_ref, o_ref = refs[2 * i], refs[2 * i + 1], refs[2 * n + i]
            o_ref[...] = (a_ref[...].astype(F32) + b_ref[...].astype(F32)).astype(BF16)

    in_specs, out_specs = [], []
    for g in gs:
        half = (None, g.shape[1] // 2, g.shape[2])
        in_specs.append(pl.BlockSpec(half, lambda k, c: (k, c[0], 0)))
        in_specs.append(pl.BlockSpec(half, lambda k, c: (k, 0, 0)))
        out_specs.append(pl.BlockSpec(half, lambda k, c: (k, 0, 0)))
    grid_spec = pltpu.PrefetchScalarGridSpec(num_scalar_prefetch=1, grid=(N_CHIPS,), in_specs=in_specs,
                                             out_specs=tuple(out_specs))
    return list(pl.pallas_call(
        body, name=name, out_shape=tuple(_sds((N_CHIPS, g.shape[1] // 2, g.shape[2]), BF16) for g in gs),
        grid_spec=grid_spec, compiler_params=_params(("parallel",), VMEM_BIG),
    )(core.reshape(1).astype(jnp.int32), *[t for pair in zip(gs, recvs) for t in pair]))


def chip_sums(pairs, recvs, chip, core, name):
    n = len(pairs)
    trs = [p.shape[1] // _SUM_STEPS for p in pairs]

    def body(p_ref, *refs):
        del p_ref
        for i in range(n):
            own_ref, r_ref, o_ref = refs[2 * i], refs[2 * i + 1], refs[2 * n + i]
            acc = own_ref[...].astype(F32)
            for j in range(N_CHIPS - 1):
                acc = acc + r_ref[j].astype(F32)
            o_ref[...] = acc

    in_specs, out_specs = [], []
    for p, tr in zip(pairs, trs):
        cols = p.shape[2]
        in_specs.append(pl.BlockSpec((None, tr, cols), lambda s, q: (q[0], s, 0)))
        in_specs.append(pl.BlockSpec((N_CHIPS - 1, tr, cols), lambda s, q: (0, s, 0)))
        out_specs.append(pl.BlockSpec((None, tr, cols), lambda s, q: (q[1], s, 0)))
    grid_spec = pltpu.PrefetchScalarGridSpec(num_scalar_prefetch=1, grid=(_SUM_STEPS,), in_specs=in_specs,
                                             out_specs=tuple(out_specs))
    return list(pl.pallas_call(
        body, name=name, out_shape=tuple(_sds((2,) + p.shape[1:]) for p in pairs), grid_spec=grid_spec,
        compiler_params=_params(("parallel",)),
    )(jnp.stack([chip, core]).astype(jnp.int32), *[t for pair in zip(pairs, recvs) for t in pair]))


def _place():
    return lax.axis_index("x"), lax.axis_index("y"), lax.axis_index("c")


def _other_chips(x, y):
    return [(x, 1 - y), (1 - x, y), (1 - x, 1 - y)]


_HBM = pl.BlockSpec(memory_space=pltpu.HBM)


def all_gather8(v, name, after=()):
    m, n = v.shape

    def body(x_ref, *refs):
        out_ref, send_sems, recv_sems, local_sem = refs[len(after):]
        x, y, c = _place()
        me, sibling = (x, y, c), (x, y, 1 - c)
        chips = _other_chips(x, y)

        def rows(px, py, pc):
            return out_ref.at[pl.ds((4 * px + 2 * py + pc) * m, m), :]

        def copy(k, block, to, src=None):
            return pltpu.make_async_remote_copy(
                src_ref=rows(*block) if src is None else src, dst_ref=rows(*block),
                send_sem=send_sems.at[k], recv_sem=recv_sems.at[k], device_id=to, device_id_type=MESH)

        mine = pltpu.make_async_copy(x_ref, rows(*me), local_sem)
        mine.start()
        first = [copy(0, me, sibling, src=x_ref)]
        first += [copy(1 + j, me, (*chip, c), src=x_ref) for j, chip in enumerate(chips)]
        for cp in first:
            cp.start()
        passed = [copy(4 + j, (*chip, c), sibling) for j, chip in enumerate(chips)]
        for j, chip in enumerate(chips):
            copy(1 + j, (*chip, c), me).wait_recv()
            passed[j].start()
        copy(0, sibling, me).wait_recv()
        for j, chip in enumerate(chips):
            copy(4 + j, (*chip, 1 - c), me).wait_recv()
        for cp in first + passed:
            cp.wait_send()
        mine.wait()

    return pl.pallas_call(
        body, name=name, out_shape=_sds((N_DEV * m, n), v.dtype),
        in_specs=[pl.BlockSpec(memory_space=pltpu.VMEM)] + [pl.BlockSpec(memory_space=pl.ANY)] * len(after),
        out_specs=pl.BlockSpec(memory_space=pltpu.VMEM),
        scratch_shapes=[pltpu.SemaphoreType.DMA((7,)), pltpu.SemaphoreType.DMA((7,)), pltpu.SemaphoreType.DMA],
        compiler_params=_params(None, VMEM_BIG),
    )(v, *after)


def _comm_call(body, name, ins, out_shapes, nsem, aliases=None):
    return pl.pallas_call(
        body, name=name, out_shape=tuple(out_shapes), in_specs=[_HBM] * len(ins), out_specs=tuple([_HBM] * len(out_shapes)),
        scratch_shapes=[pltpu.SemaphoreType.DMA((nsem,)), pltpu.SemaphoreType.DMA((nsem,))],
        input_output_aliases=aliases or {},
    )(*ins)


def _remote(src, dst, send_sems, recv_sems, k, to):
    return pltpu.make_async_remote_copy(src_ref=src, dst_ref=dst, send_sem=send_sems.at[k], recv_sem=recv_sems.at[k],
                                        device_id=to, device_id_type=MESH)


def _half(core, rh):
    return pl.ds(pl.multiple_of(core * rh, 16), rh)


def swap_halves(gs, name):
    n = len(gs)

    def body(*refs):
        ins, outs, (send_sems, recv_sems) = refs[:n], refs[n:2 * n], refs[2 * n:]
        x, y, c = _place()
        copies = []
        for i in range(n):
            theirs = _half(1 - c, ins[i].shape[1] // 2)
            cp = _remote(ins[i].at[:, theirs], outs[i], send_sems, recv_sems, i, (x, y, 1 - c))
            cp.start()
            copies.append(cp)
        for cp in copies:
            cp.wait()

    return _comm_call(body, name, gs, [_sds((g.shape[0], g.shape[1] // 2, g.shape[2]), g.dtype) for g in gs], n)


def join_halves(bufs, name):
    n = len(bufs)

    def body(*refs):
        ins, outs, (send_sems, recv_sems) = refs[:n], refs[n:2 * n], refs[2 * n:]
        x, y, c = _place()
        copies = []
        for i in range(n):
            cp = _remote(ins[i].at[c], outs[i].at[c], send_sems, recv_sems, i, (x, y, 1 - c))
            cp.start()
            copies.append(cp)
        for i in range(n):
            theirs = outs[i].at[1 - c]
            _remote(theirs, theirs, send_sems, recv_sems, i, (x, y, 1 - c)).wait_recv()
        for cp in copies:
            cp.wait_send()

    return _comm_call(body, name, bufs, [_sds(b.shape, b.dtype) for b in bufs], n, {i: i for i in range(n)})


def forward_halves(lands, name):
    n = len(lands)

    def body(*refs):
        ins, outs, (send_sems, recv_sems) = refs[:n], refs[n:2 * n], refs[2 * n:]
        x, y, c = _place()
        sibling = (x, y, 1 - c)
        chips = _other_chips(x, y)
        copies = []
        for i in range(n):
            mine = _half(c, ins[i].shape[1] // 2)
            for j, (px, py) in enumerate(chips):
                cp = _remote(ins[i].at[2 * px + py, mine], outs[i].at[2 * px + py, mine], send_sems, recv_sems, 3 * i + j, sibling)
                cp.start()
                copies.append(cp)
        for i in range(n):
            theirs = _half(1 - c, ins[i].shape[1] // 2)
            for j, (px, py) in enumerate(chips):
                landed = outs[i].at[2 * px + py, theirs]
                _remote(landed, landed, send_sems, recv_sems, 3 * i + j, sibling).wait_recv()
        for cp in copies:
            cp.wait_send()

    return _comm_call(body, name, lands, [_sds(b.shape, b.dtype) for b in lands], 3 * n, {i: i for i in range(n)})


_SEM = pl.BlockSpec(memory_space=pltpu.SEMAPHORE)
_EFFECT = pltpu.SideEffectType.DATAFLOW_SIDE_EFFECTING


def _gather_copies(srcs, lands, send_sems, recv_sems):
    x, y, c = _place()
    copies = []
    for i in range(len(srcs)):
        mine = _half(c, srcs[i].shape[0] // 2)
        for j, chip in enumerate(_other_chips(x, y)):
            copies.append(_remote(srcs[i].at[mine], lands[i].at[2 * x + y, mine], send_sems, recv_sems, 3 * i + j, (*chip, c)))
    return copies


def _exchange_copies(srcs, lands, send_sems, recv_sems):
    x, y, c = _place()
    copies = []
    for i in range(len(srcs)):
        for j, (px, py) in enumerate(_other_chips(x, y)):
            copies.append(_remote(srcs[i].at[2 * px + py], lands[i].at[j], send_sems, recv_sems, 3 * i + j, (px, py, c)))
    return copies


def _everyone_copies(srcs, lands, send_sems, recv_sems):
    x, y, c = _place()
    flip = lambda v, b: 1 - v if b else v
    dst = lands[0].at[4 * x + 2 * y + c]
    return [_remote(srcs[0], dst, send_sems, recv_sems, j - 1, (flip(x, j & 4), flip(y, j & 2), flip(c, j & 1)))
            for j in range(1, N_DEV)]


GATHER = (_gather_copies, 3)
EXCHANGE = (_exchange_copies, 3)
EVERYONE = (_everyone_copies, N_DEV - 1)


def split_start(name, plan, srcs, land_shapes, after=()):
    copies_fn, per_source = plan
    n, m, k = len(srcs), len(land_shapes), len(after)
    ncopies = per_source * n

    def body(*refs):
        src_refs, land_refs = refs[:n], refs[n:n + m]
        send_sems, recv_sems = refs[n + m + k], refs[n + m + k + 1]
        token = refs[-1]
        for cp in copies_fn(src_refs, land_refs, send_sems, recv_sems):
            cp.start()
        token[...] = jnp.zeros_like(token)

    hbm = lambda s: pltpu.HBM(tuple(s.shape), s.dtype)
    outs = pl.pallas_call(
        body, name=name,
        out_shape=(pltpu.SemaphoreType.DMA((ncopies,)), pltpu.SemaphoreType.DMA((ncopies,)), *[hbm(s) for s in srcs],
                   *[hbm(s) for s in land_shapes], _sds((8, LANES))),
        in_specs=[_HBM] * (n + m) + [pl.BlockSpec(memory_space=pl.ANY)] * k,
        out_specs=(_SEM, _SEM, *([_HBM] * (n + m)), pl.BlockSpec(memory_space=pltpu.VMEM)),
        input_output_aliases={i: 2 + i for i in range(n + m)},
        compiler_params=pltpu.CompilerParams(has_side_effects=_EFFECT),
    )(*[pltpu.with_memory_space_constraint(s, pltpu.HBM) for s in srcs],
      *[pltpu.with_memory_space_constraint(lax.empty(tuple(s.shape), s.dtype), pltpu.HBM) for s in land_shapes], *after)
    handle = (outs[0], outs[1], list(outs[2:2 + n]), list(outs[2 + n:2 + n + m]))
    return handle, outs[-1][0, 0]


def split_wait(name, plan, handle, after):
    copies_fn, _ = plan
    send_sems, recv_sems, srcs, lands = handle
    n, m = len(srcs), len(lands)
    after = list(after) if isinstance(after, (list, tuple)) else [after]

    def body(*refs):
        src_refs, land_refs = refs[:n], refs[n:n + m]
        for cp in copies_fn(src_refs, land_refs, refs[n + m], refs[n + m + 1]):
            cp.wait_send()
            cp.wait_recv()

    hbm = lambda s: pltpu.HBM(tuple(s.shape), s.dtype)
    outs = pl.pallas_call(
        body, name=name, out_shape=tuple(hbm(s) for s in srcs + lands),
        in_specs=[_HBM] * (n + m) + [_SEM, _SEM] + [pl.BlockSpec(memory_space=pl.ANY)] * len(after),
        out_specs=tuple([_HBM] * (n + m)), input_output_aliases={i: i for i in range(n + m)},
        compiler_params=pltpu.CompilerParams(has_side_effects=_EFFECT),
    )(*srcs, *lands, send_sems, recv_sems, *after)
    return list(outs[:n]), list(outs[n:])


def chip_major(w, groups=N_CHIPS):
    r, c = w.shape
    return w.reshape(r, groups, c // groups).transpose(1, 0, 2)


def from_chip_major(w):
    g, r, c = w.shape
    return w.transpose(1, 0, 2).reshape(r, g * c)


def _cd_in_pad(w):
    a = C_Q_RANK + C_KV_RANK
    z = lambda n: jnp.zeros((w.shape[0], n), w.dtype)
    return jnp.concatenate([w[:, :a], z(C_NOPE), w[:, a:a + C_ROPE], z(HEAD_PAD - C_NOPE - C_ROPE), w[:, a + C_ROPE:]], axis=1)


def _cd_in_unpad(w):
    a = C_Q_RANK + C_KV_RANK
    return jnp.concatenate([w[:, :a], w[:, a + C_NOPE:a + C_NOPE + C_ROPE], w[:, a + HEAD_PAD:]], axis=1)


def _pad_heads(w, width):
    r = w.shape[0]
    w = w.reshape(r, C_HEADS, width)
    return jnp.pad(w, ((0, 0), (0, 0), (0, HEAD_PAD - width))).reshape(r, _HW)


def _unpad_heads(w, width):
    r = w.shape[0]
    return w.reshape(r, C_HEADS, HEAD_PAD)[:, :, :width].reshape(r, C_HEADS * width)


def prepare_weights(p):
    q = dict(p)
    q["cd_w_in"] = _cd_in_pad(p["cd_w_in"])
    q["c_w_uq"] = _pad_heads(p["c_w_uq"], C_NOPE + C_ROPE)
    ukv = p["c_w_ukv"].reshape(C_KV_RANK, C_HEADS, C_NOPE + C_V)
    q["c_w_uk"] = _pad_heads(ukv[:, :, :C_NOPE].reshape(C_KV_RANK, -1), C_NOPE)
    q["c_w_uv"] = _pad_heads(ukv[:, :, C_NOPE:].reshape(C_KV_RANK, -1), C_V)
    wo = p["cd_w_out"]
    att_rows = jnp.pad(wo[:C_HEADS * C_V].reshape(C_HEADS, C_V, D_MODEL), ((0, 0), (0, HEAD_PAD - C_V), (0, 0)))
    q["cd_w_out"] = jnp.concatenate([att_rows.reshape(_HW, D_MODEL), wo[C_HEADS * C_V:]], axis=0)
    return q


def unprepare_grads(g):
    q = dict(g)
    q["cd_w_in"] = _cd_in_unpad(g["cd_w_in"])
    q["c_w_uq"] = _unpad_heads(g["c_w_uq"], C_NOPE + C_ROPE)
    uk = g.pop("c_w_uk").reshape(C_KV_RANK, C_HEADS, HEAD_PAD)[:, :, :C_NOPE]
    uv = g.pop("c_w_uv").reshape(C_KV_RANK, C_HEADS, HEAD_PAD)[:, :, :C_V]
    q.pop("c_w_uk", None)
    q.pop("c_w_uv", None)
    q["c_w_ukv"] = jnp.concatenate([uk, uv], axis=-1).reshape(C_KV_RANK, C_HEADS * (C_NOPE + C_V))
    wo = g["cd_w_out"]
    att = wo[:_HW].reshape(C_HEADS, HEAD_PAD, D_MODEL)[:, :C_V].reshape(C_HEADS * C_V, D_MODEL)
    q["cd_w_out"] = jnp.concatenate([att, wo[_HW:]], axis=0)
    return q


def rope_tables(positions):
    half = C_ROPE // 2
    inv_freq = ROPE_THETA ** (-jnp.arange(half, dtype=F32) / half)
    ang = positions.astype(F32)[:, None] * inv_freq
    cos, sin = jnp.cos(ang), jnp.sin(ang)
    s = positions.shape[0]
    z = lambda n: jnp.zeros((s, n), F32)
    cs = jnp.concatenate([jnp.ones((s, C_NOPE), F32), cos, cos, z(HEAD_PAD - C_NOPE - C_ROPE)], axis=1)
    s1 = jnp.concatenate([z(C_NOPE), -sin, z(HEAD_PAD - C_NOPE - half)], axis=1)
    s2 = jnp.concatenate([z(C_NOPE + half), sin, z(HEAD_PAD - C_NOPE - C_ROPE)], axis=1)
    return cs, s1, s2


_UP_COLS = 2 * D_FF // N_CHIPS


def ffn_fwd(h2, w, l, late_down=None):
    zf = matmul(h2, w["ffn_w_up"][l], "nn", BF16, f"ffn_up{l}", gb=N_CHIPS, go=2, tn=_UP_COLS)
    if late_down is not None:
        late_down(zf)
    a, f = ffn_act_down(zf, w["ffn_conv_w"][l], w["ffn_w_down"][l], f"ffn_act_down{l}")
    return f, (zf, a)


def ffn_bwd(df, h2, saved, w, l):
    zf, a = saved
    da = matmul(df, w["ffn_w_down"][l], "nt", BF16, f"ffn_down_dx{l}", tn=D_FF // 2)
    d_down = matmul(a, df, "tn", BF16, f"ffn_down_dw{l}", tm=D_FF // 2)
    dzf, d_conv = ffn_act_bwd(zf, w["ffn_conv_w"][l], da, f"ffn_act_bwd{l}")
    dh2 = matmul(dzf, w["ffn_w_up"][l], "nt", BF16, f"ffn_up_dx{l}", ga=2, gb=N_CHIPS, tk=_UP_COLS, tn=D_MODEL)
    d_up = matmul(h2, dzf, "tn", BF16, f"ffn_up_dw{l}", gb=2, go=N_CHIPS, tn=_UP_COLS)
    d_conv = d_conv.transpose(1, 0, 2).reshape(3, 2 * D_FF)
    return dh2, dict(ffn_w_down=d_down, ffn_conv_w=d_conv, ffn_w_up=d_up)


def mixer0_fwd(h, w):
    z = matmul(h, w["ab_w_in"], "nn", BF16, "ab_in", gb=N_CHIPS)
    ycat = pool_fwd(z, w["b_mix_w"], w["b_scale"], gconv_fwd(z, w["a_conv_w"]))
    y = matmul(ycat, w["ab_w_out"], "nn", BF16, "ab_out", tn=D_MODEL)
    return y, (z, ycat)


def mixer0_bwd(dy, h, saved, w):
    z, ycat = saved
    grads = {}
    dycat = matmul(dy, w["ab_w_out"], "nt", BF16, "ab_out_dx")
    grads["ab_w_out"] = matmul(ycat, dy, "tn", BF16, "ab_out_dw")
    db, dc, da, d_conv = gconv_bwd(z, w["a_conv_w"], dycat)
    dp, d_mix, d_scale = pool_bwd(z, w["b_mix_w"], w["b_scale"], dycat)
    dz = jnp.concatenate([db, dc, da, dp], axis=1)
    dh = matmul(dz, w["ab_w_in"], "nt", BF16, "ab_in_dx", gb=N_CHIPS, tn=D_MODEL)
    grads["ab_w_in"] = matmul(h, dz, "tn", BF16, "ab_in_dw", go=N_CHIPS)
    grads.update(a_conv_w=d_conv, b_mix_w=d_mix, b_scale=d_scale)
    return dh, grads


def mixer1_fwd(h, ropes, w):
    cs, s1, s2 = ropes
    z = matmul(h, w["cd_w_in"], "nn", BF16, "cd_in")
    bs_t = jnp.pad(w["d_b_s"].T, ((0, 0), (0, LANES - D_GROUPS)))
    qh, kh, vh = mla_pre_fwd(z, w["c_q_norm_g"], w["c_kv_norm_g"], w["c_w_uq"], w["c_w_uk"], w["c_w_uv"], cs, s1, s2)
    ycat = sgu_fwd(z, w["d_ln_g"], w["d_ln_b"], w["d_w_s"], bs_t, attn_fwd(qh, kh, vh))
    y = matmul(ycat, w["cd_w_out"], "nn", BF16, "cd_out", tn=D_MODEL)
    return y, (z, bs_t, qh, kh, vh, ycat)


def mixer1_bwd(dy, h, saved, ropes, w):
    cs, s1, s2 = ropes
    z, bs_t, qh, kh, vh, ycat = saved
    grads = {}
    dycat = matmul(dy, w["cd_w_out"], "nt", BF16, "cd_out_dx")
    grads["cd_w_out"] = matmul(ycat, dy, "tn", BF16, "cd_out_dw")
    dqh, dkh, dvh = attn_bwd(qh, kh, vh, ycat, dycat)
    dzq, d_uq, d_uk, d_uv, d_gq, d_gkv = mla_pre_bwd(
        z, w["c_q_norm_g"], w["c_kv_norm_g"], w["c_w_uq"], w["c_w_uk"], w["c_w_uv"], cs, s1, s2, dqh, dkh, dvh)
    dzu, dzv, d_ws, d_bs, d_lg, d_lb = sgu_bwd(z, w["d_ln_g"], w["d_ln_b"], w["d_w_s"], bs_t, dycat, _HW // _DW)
    dz = jnp.concatenate([dzq, dzu, dzv], axis=1)
    dh = matmul(dz, w["cd_w_in"], "nt", BF16, "cd_in_dx", tn=D_MODEL)
    grads["cd_w_in"] = matmul(h, dz, "tn", BF16, "cd_in_dw")
    grads.update(c_w_uq=d_uq, c_w_uk=d_uk, c_w_uv=d_uv, c_q_norm_g=d_gq, c_kv_norm_g=d_gkv, d_w_s=d_ws,
                 d_b_s=d_bs[:, :D_GROUPS].T, d_ln_g=d_lg, d_ln_b=d_lb)
    return dh, grads


class StepHooks:
    def weights(self, stage, after):
        pass

    def gradients(self, stage, grads, after):
        return 0.0


def run_step(x, tgt, mod, ropes, w, hooks):
    sh1, sc1, g1, sh2, sc2, g2 = range(N_MOD)
    mods = mod.reshape(2, 1, N_MOD * D_MODEL)
    n1 = w["norm1_g"].reshape(2, 1, D_MODEL)
    n2 = w["norm2_g"].reshape(2, 1, D_MODEL)
    final_g = Vec(w["final_norm_g"].reshape(1, 1, D_MODEL), 0, 0)

    hooks.weights("mix0", mod)
    h0 = modnorm_fwd(x, Vec(n1, 0, 0), Vec(mods, 0, sc1), Vec(mods, 0, sh1), "modnorm_0")
    y0, mix0 = mixer0_fwd(h0, w)
    x1, h1 = resid_modnorm_fwd(x, y0, Vec(mods, 0, g1), Vec(n2, 0, 0), Vec(mods, 0, sc2), Vec(mods, 0, sh2), "resid_modnorm_1")
    hooks.weights("up0", x1)
    f0, ffn0 = ffn_fwd(h1, w, 0, lambda act: hooks.weights("down0", act))
    x2, h2 = resid_modnorm_fwd(x1, f0, Vec(mods, 0, g2), Vec(n1, 1, 0), Vec(mods, 1, sc1), Vec(mods, 1, sh1), "resid_modnorm_2")
    hooks.weights("mix1", x2)
    y1, mix1 = mixer1_fwd(h2, ropes, w)
    x3, h3 = resid_modnorm_fwd(x2, y1, Vec(mods, 1, g1), Vec(n2, 1, 0), Vec(mods, 1, sc2), Vec(mods, 1, sh2), "resid_modnorm_3")
    hooks.weights("ffn1", x3)
    f1, ffn1 = ffn_fwd(h3, w, 1)
    dres, d_final, loss, df1, dg2b = final_fused(x3, f1, Vec(mods, 1, g2), final_g, tgt)

    dh3, gf1 = ffn_bwd(df1, h3, ffn1, w, 1)
    late = mods + hooks.gradients("ffn1", gf1, dh3)
    dres, dsh2b, dsc2b, dn2b, dy1, dg1b = norm_gate_bwd(
        x3, dh3, Vec(n2, 1, 0), Vec(late, 1, sc2), dres, y1, Vec(late, 1, g1), "norm_gate_bwd_3")
    dh2, gm1 = mixer1_bwd(dy1, h2, mix1, ropes, w)
    late = mods + hooks.gradients("mix1", gm1, dh2)
    dres, dsh1b, dsc1b, dn1b, df0, dg2a = norm_gate_bwd(
        x2, dh2, Vec(n1, 1, 0), Vec(late, 1, sc1), dres, f0, Vec(late, 0, g2), "norm_gate_bwd_2")
    dh1, gf0 = ffn_bwd(df0, h1, ffn0, w, 0)
    late = mods + hooks.gradients("ffn0", gf0, dh1)
    dres, dsh2a, dsc2a, dn2a, dy0, dg1a = norm_gate_bwd(
        x1, dh1, Vec(n2, 0, 0), Vec(late, 0, sc2), dres, y0, Vec(late, 0, g1), "norm_gate_bwd_1")
    dh0, gm0 = mixer0_bwd(dy0, h0, mix0, w)
    late = mods + hooks.gradients("mix0", gm0, dh0)
    grad_x, dsh1a, dsc1a, dn1a = norm_bwd(x, dh0, Vec(n1, 0, 0), Vec(late, 0, sc1), dres, "norm_bwd_0")

    dmod = jnp.concatenate([jnp.concatenate([dsh1a, dsc1a, dg1a, dsh2a, dsc2a, dg2a], axis=1),
                            jnp.concatenate([dsh1b, dsc1b, dg1b, dsh2b, dsc2b, dg2b], axis=1)], axis=0)
    norms = dict(norm1_g=jnp.concatenate([dn1a, dn1b], axis=0), norm2_g=jnp.concatenate([dn2a, dn2b], axis=0),
                 final_norm_g=d_final)
    return loss, grad_x, dmod, dict(mix0=gm0, ffn0=gf0, mix1=gm1, ffn1=gf1, norms=norms)


def merge_grads(by_stage):
    grads = {**by_stage["mix0"], **by_stage["mix1"], **by_stage["norms"]}
    for k in ("ffn_w_down", "ffn_w_up"):
        grads[k] = [by_stage["ffn0"][k], by_stage["ffn1"][k]]
    grads["ffn_conv_w"] = jnp.stack([by_stage["ffn0"]["ffn_conv_w"], by_stage["ffn1"]["ffn_conv_w"]])
    return grads


_WEIGHTS = ("ada_w", "ada_b", "norm1_g", "norm2_g", "ab_w_in", "a_conv_w", "b_mix_w", "b_scale", "ab_w_out", "cd_w_in",
            "c_q_norm_g", "c_w_uq", "c_kv_norm_g", "c_w_ukv", "d_ln_g", "d_ln_b", "d_w_s", "d_b_s", "cd_w_out",
            "ffn_w_up", "ffn_conv_w", "ffn_w_down", "final_norm_g")
_INPUTS = ("x", "c", "positions") + _WEIGHTS + ("loss_target",) + tuple("m_" + n for n in _WEIGHTS) + tuple(
    "v_" + n for n in _WEIGHTS)

def _pack_rows(parts, rows, dtype):
    flat = jnp.concatenate([p.reshape(-1).astype(dtype) for p in parts])
    return jnp.pad(flat, (0, rows * LANES - flat.shape[0])).reshape(rows, LANES)


def _rows_major(w):
    r, c = w.shape
    return w.reshape(N_CHIPS, r // N_CHIPS, c)


def start_gather(shards, tag, after=()):
    lands = [_sds((N_CHIPS,) + s.shape, s.dtype) for s in shards]
    return split_start("gather_start_" + tag, GATHER, shards, lands, after)


def finish_gather(handle, chip, tag, after):
    shards, lands = split_wait("gather_wait_" + tag, GATHER, handle, after)
    lands = forward_halves(lands, "gather_forward_" + tag)
    return [lax.dynamic_update_index_in_dim(o, s, chip, 0) for o, s in zip(lands, shards)]


def start_reduce(gs, core, tag):
    recv = swap_halves(gs, "swap_halves_" + tag)
    pairs = pair_sums(gs, recv, core, "pair_sums_" + tag)
    lands = [_sds((N_CHIPS - 1,) + p.shape[1:], p.dtype) for p in pairs]
    return split_start("exchange_start_" + tag, EXCHANGE, pairs, lands)


def finish_reduce(handle, chip, core, tag, after):
    pairs, others = split_wait("exchange_wait_" + tag, EXCHANGE, handle, after)
    halves = chip_sums(pairs, others, chip, core, "chip_sums_" + tag)
    full = join_halves(halves, "join_halves_" + tag)
    return [f.reshape(f.shape[1] * 2, f.shape[2]) for f in full]


_SMALL_SHARDED = (("a_conv_w", (3, 128), 1), ("c_q_norm_g", (1, 64), 1), ("d_ln_g", (1, 128), 1), ("d_ln_b", (1, 128), 1),
                  ("ffn_conv_w", (2, 3, 2 * D_FF // N_CHIPS), 2))
_SMALL_GRADS = (("norm1_g", (2, D_MODEL)), ("norm2_g", (2, D_MODEL)), ("b_mix_w", (4, 128, 128)), ("b_scale", (1, 512)),
                ("c_kv_norm_g", (1, 128)), ("d_w_s", (4, 128, 128)), ("d_b_s", (4, 128)), ("final_norm_g", (1, D_MODEL)),
                ("a_conv_w", (3, 512)), ("c_q_norm_g", (1, 256)), ("d_ln_g", (1, 512)), ("d_ln_b", (1, 512)),
                ("ffn_conv_w", (2, 3, 2 * D_FF)))


def _size(shape):
    n = 1
    for d in shape:
        n *= d
    return n


def kernel(x, c, positions, ada_w, ada_b, norm1_g, norm2_g, ab_w_in, a_conv_w, b_mix_w, b_scale, ab_w_out, cd_w_in, c_q_norm_g, c_w_uq, c_kv_norm_g, c_w_ukv, d_ln_g, d_ln_b, d_w_s, d_b_s, cd_w_out, ffn_w_up, ffn_conv_w, ffn_w_down, final_norm_g, loss_target, m_ada_w, m_ada_b, m_norm1_g, m_norm2_g, m_ab_w_in, m_a_conv_w, m_b_mix_w, m_b_scale, m_ab_w_out, m_cd_w_in, m_c_q_norm_g, m_c_w_uq, m_c_kv_norm_g, m_c_w_ukv, m_d_ln_g, m_d_ln_b, m_d_w_s, m_d_b_s, m_cd_w_out, m_ffn_w_up, m_ffn_conv_w, m_ffn_w_down, m_final_norm_g, v_ada_w, v_ada_b, v_norm1_g, v_norm2_g, v_ab_w_in, v_a_conv_w, v_b_mix_w, v_b_scale, v_ab_w_out, v_cd_w_in, v_c_q_norm_g, v_c_w_uq, v_c_kv_norm_g, v_c_w_ukv, v_d_ln_g, v_d_ln_b, v_d_w_s, v_d_b_s, v_cd_w_out, v_ffn_w_up, v_ffn_conv_w, v_ffn_w_down, v_final_norm_g):
    args = (x, c, positions, ada_w, ada_b, norm1_g, norm2_g, ab_w_in, a_conv_w, b_mix_w, b_scale, ab_w_out, cd_w_in, c_q_norm_g, c_w_uq, c_kv_norm_g, c_w_ukv, d_ln_g, d_ln_b, d_w_s, d_b_s, cd_w_out, ffn_w_up, ffn_conv_w, ffn_w_down, final_norm_g, loss_target, m_ada_w, m_ada_b, m_norm1_g, m_norm2_g, m_ab_w_in, m_a_conv_w, m_b_mix_w, m_b_scale, m_ab_w_out, m_cd_w_in, m_c_q_norm_g, m_c_w_uq, m_c_kv_norm_g, m_c_w_ukv, m_d_ln_g, m_d_ln_b, m_d_w_s, m_d_b_s, m_cd_w_out, m_ffn_w_up, m_ffn_conv_w, m_ffn_w_down, m_final_norm_g, v_ada_w, v_ada_b, v_norm1_g, v_norm2_g, v_ab_w_in, v_a_conv_w, v_b_mix_w, v_b_scale, v_ab_w_out, v_cd_w_in, v_c_q_norm_g, v_c_w_uq, v_c_kv_norm_g, v_c_w_ukv, v_d_ln_g, v_d_ln_b, v_d_w_s, v_d_b_s, v_cd_w_out, v_ffn_w_up, v_ffn_conv_w, v_ffn_w_down, v_final_norm_g)
    a = dict(zip(_INPUTS, args, strict=True))
    xi, yi, ci = _place()
    chip = 2 * xi + yi
    dev = 4 * xi + 2 * yi + ci
    x = a["x"][0]
    tgt = a["loss_target"][0]

    bf = lambda t: t.astype(BF16)
    mix0_handle, tok = start_gather([bf(a["ab_w_in"][0]), bf(a["ab_w_out"][0])], "mix0")
    up0_16, down0_16, up1_16, down1_16 = [bf(a[n][l]) for l in (0, 1) for n in ("ffn_w_up", "ffn_w_down")]
    mix1_16 = [bf(a[n][0]) for n in ("cd_w_in", "c_w_uq", "c_w_ukv", "cd_w_out")]

    small_parts = [a["c"] + tok] + [a[n] for n, _, _ in _SMALL_SHARDED]
    rows1 = -(-sum(p.size for p in small_parts) // LANES // 8) * 8
    g1 = all_gather8(_pack_rows(small_parts, rows1, F32), "gather_small",
                     [up0_16, down0_16, up1_16, down1_16, mix1_16[0], mix1_16[3]]).reshape(N_DEV, rows1 * LANES)
    c_all = g1[:, :D_MODEL]
    per_chip = g1[0::2]
    small_full = {}
    off = D_MODEL
    for n, shp, axis in _SMALL_SHARDED:
        piece = per_chip[:, off:off + _size(shp)].reshape((N_CHIPS,) + shp)
        small_full[n] = jnp.concatenate([piece[k] for k in range(N_CHIPS)], axis=axis)
        off += _size(shp)

    merge = lambda t: t.reshape(t.shape[0] * t.shape[1], t.shape[2])
    w = dict(norm1_g=a["norm1_g"], norm2_g=a["norm2_g"], b_mix_w=a["b_mix_w"][0], b_scale=a["b_scale"],
             c_kv_norm_g=a["c_kv_norm_g"], d_w_s=a["d_w_s"][0], d_b_s=a["d_b_s"][0],
             final_norm_g=a["final_norm_g"].reshape(1, D_MODEL), **small_full)

    ncol = N_MOD * D_MODEL // N_CHIPS
    ada_b_mine = lax.dynamic_slice_in_dim(a["ada_b"], chip * ncol, ncol, axis=1)
    mod_cols = ada_mod(c_all, a["ada_w"], ada_b_mine)
    g2_rows = all_gather8(mod_cols.reshape(-1, LANES), "gather_mod")
    g2 = g2_rows.reshape(N_DEV, 2, N_DEV, ncol)
    mod = lax.dynamic_index_in_dim(g2[0::2], dev, axis=2, keepdims=False)
    mod = mod.transpose(1, 0, 2).reshape(2, N_MOD * D_MODEL)

    late = [g2_rows]
    up0_handle, tok_a = start_gather([up0_16], "up0", late)
    down0_handle, tok_b = start_gather([down0_16], "down0", late)
    mix1_handle, tok_c = start_gather(mix1_16, "mix1", late)
    ffn1_handle, tok_d = start_gather([up1_16, down1_16], "ffn1", late)
    mod = mod + (tok_a + tok_b + tok_c + tok_d)

    ropes = rope_tables(a["positions"][0])
    cm16 = lambda t: chip_major(t).astype(BF16)
    w.update(ffn_w_up=[None, None], ffn_w_down=[None, None])
    handles = dict(mix0=mix0_handle, up0=up0_handle, down0=down0_handle, mix1=mix1_handle, ffn1=ffn1_handle)
    reducing, reduced = {}, {}

    class Hooks(StepHooks):
        def weights(self, stage, after):
            got = finish_gather(handles[stage], chip, stage, after)
            if stage == "mix0":
                w.update(ab_w_in=got[0], ab_w_out=merge(got[1]))
            elif stage == "up0":
                w["ffn_w_up"][0] = got[0]
            elif stage == "down0":
                w["ffn_w_down"][0] = merge(got[0])
            elif stage == "mix1":
                cd_in, uq, ukv, cd_out = got
                w.update(prepare_weights(dict(cd_w_in=from_chip_major(cd_in), c_w_uq=from_chip_major(uq),
                                              c_w_ukv=from_chip_major(ukv), cd_w_out=merge(cd_out))))
            else:
                w["ffn_w_up"][1], w["ffn_w_down"][1] = got[0], merge(got[1])

        def gradients(self, stage, grads, after):
            if stage in ("ffn0", "ffn1"):
                parts = [grads["ffn_w_up"], _rows_major(grads["ffn_w_down"])]
            elif stage == "mix1":
                grads.update(unprepare_grads(grads))
                parts = [cm16(grads["cd_w_in"]), cm16(grads["c_w_uq"]), cm16(grads["c_w_ukv"]),
                         _rows_major(grads["cd_w_out"]).astype(BF16)]
            else:
                parts = [grads["ab_w_in"], _rows_major(grads["ab_w_out"])]
            reducing[stage], tok = start_reduce(parts, ci, stage)
            before = {"mix1": "ffn1", "ffn0": "mix1", "mix0": "ffn0"}.get(stage)
            if before is not None:
                reduced[before] = finish_reduce(reducing[before], chip, ci, before, after)
            return tok

    loss, grad_x, dmod, by_stage = run_step(x, tgt, mod, ropes, w, Hooks())
    grads = merge_grads(by_stage)

    parts3 = [dmod] + [grads[n] for n, _ in _SMALL_GRADS] + [loss[0, 0]]
    rows3 = -(-sum(p.size for p in parts3) // LANES // 8) * 8
    small_handle, _ = split_start("small_grads_start", EVERYONE, [_pack_rows(parts3, rows3, F32)],
                                  [_sds((N_DEV, rows3, LANES))])
    red_up1, red_down1 = reduced["ffn1"]
    red_cd_in, red_uq, red_ukv, red_cd_out = reduced["mix1"]
    red_up0, red_down0 = reduced["ffn0"]
    out_grads = dict(cd_w_in=red_cd_in, c_w_uq=red_uq, c_w_ukv=red_ukv, cd_w_out=red_cd_out)
    per_layer = dict(ffn_w_up=(red_up0, red_up1), ffn_w_down=(red_down0, red_down1))
    updates = {}

    def update(n):
        if n in per_layer:
            updates[n] = adamw_layers(a[n], *per_layer[n], a["m_" + n], a["v_" + n], "adamw_" + n)
        else:
            updates[n] = adamw(a[n], out_grads[n].reshape(a[n].shape), a["m_" + n], a["v_" + n], "adamw_" + n)

    early =("ffn_w_up", "ffn_w_down", "cd_w_in", "c_w_uq", "c_w_ukv", "cd_w_out")
    for n in early:
        update(n)
    (mine,), (landed,) = split_wait("small_grads_wait", EVERYONE, small_handle, [updates[n][1] for n in early])
    g3 = lax.dynamic_update_index_in_dim(landed, mine, dev, 0)
    summed = sum8(g3).reshape(-1)
    nmod = 2 * N_MOD * D_MODEL
    out_grads["ada_b"] = summed[:nmod].reshape(2, N_MOD * D_MODEL)
    off = nmod
    for n, shp in _SMALL_GRADS:
        out_grads[n] = summed[off:off + _size(shp)].reshape(shp)
        off += _size(shp)
    loss = summed[off]
    for n, shp, axis in _SMALL_SHARDED:
        width = out_grads[n].shape[-1] // N_CHIPS
        out_grads[n] = lax.dynamic_slice_in_dim(out_grads[n], chip * width, width, axis=out_grads[n].ndim - 1)
    dmod_all = g3.reshape(N_DEV, rows3 * LANES)[:, :nmod].reshape(N_DEV, 2, N_MOD * D_MODEL)
    dmod_mine = lax.dynamic_slice_in_dim(dmod_all, chip * ncol, ncol, axis=2).transpose(1, 0, 2)
    updates["ada_w"] = adamw_ada(a["ada_w"], c_all, dmod_mine, a["m_ada_w"], a["v_ada_w"])

    red_in0, red_out0 = finish_reduce(reducing["mix0"], chip, ci, "mix0", updates["ada_w"][1])
    out_grads.update(ab_w_in=red_in0, ab_w_out=red_out0)

    for n in ("ab_w_in", "ab_w_out"):
        update(n)
    small = [n for n in _WEIGHTS if n not in updates]
    for n, res in zip(small, adamw_small([a[n] for n in small], [out_grads[n].reshape(a[n].shape) for n in small],
                                         [a["m_" + n] for n in small], [a["v_" + n] for n in small])):
        updates[n] = res
    return (loss, grad_x[None], *[updates[n][i] for i in range(4) for n in _WEIGHTS])
```

```python
import functools
from typing import NamedTuple

import jax
import jax.numpy as jnp
from jax import lax
from jax.experimental import pallas as pl
from jax.experimental.pallas import tpu as pltpu

F32 = jnp.float32
BF16 = jnp.bfloat16
EPS = 1e-6
D_MODEL = 1024
N_MOD = 6
A_WIDTH = 512
B_GROUPS = 4
C_HEADS = 8
C_NOPE = 64
C_ROPE = 32
C_V = 64
C_Q_RANK = 256
C_KV_RANK = 128
HEAD_PAD = 128
ROPE_THETA = 10000.0
D_GROUPS = 4
D_CHUNK = 128
D_FF = 2816
FF_UNIT = 128
ADAM_LR = 0.001
ADAM_B1 = 0.9
ADAM_B2 = 0.999
ADAM_EPS = 1e-08
ADAM_WD = 0.01
ADAM_STEP = 10
N_CHIPS = 4
N_DEV = 8
LANES = 128
VMEM_BIG = 56 * 1024 * 1024
MESH = pl.DeviceIdType.MESH


def _sds(shape, dtype=F32):
    return jax.ShapeDtypeStruct(tuple(shape), dtype)


def _tile(n, cap, mult=128):
    if n <= cap:
        return n
    best = None
    for t in range(mult, cap + 1, mult):
        if n % t == 0:
            best = t
    assert best is not None, (n, cap, mult)
    return best


def _params(dims=None, vmem=None):
    return pltpu.CompilerParams(dimension_semantics=dims, vmem_limit_bytes=vmem)


def _shift_down(v, k):
    r = pltpu.roll(v, k, axis=0)
    t = lax.broadcasted_iota(jnp.int32, v.shape, 0)
    return jnp.where(t >= k, r, 0.0)


def _shift_up(v, k):
    n = v.shape[0]
    r = pltpu.roll(v, n - k, axis=0)
    t = lax.broadcasted_iota(jnp.int32, v.shape, 0)
    return jnp.where(t < n - k, r, 0.0)


def _sigmoid(v):
    return 1.0 / (1.0 + jnp.exp(-v))


_GELU_C = 0.7978845608028654
_GELU_A = 0.044715


def _gelu(v):
    return 0.5 * v * (1.0 + jnp.tanh(_GELU_C * (v + _GELU_A * v * v * v)))


def _gelu_grad(v):
    th = jnp.tanh(_GELU_C * (v + _GELU_A * v * v * v))
    return 0.5 * (1.0 + th) + 0.5 * v * (1.0 - th * th) * _GELU_C * (1.0 + 3.0 * _GELU_A * v * v)


_NN = (((1,), (0,)), ((), ()))
_NT = (((1,), (1,)), ((), ()))
_TN = (((0,), (0,)), ((), ()))


def _dot(a, b, dims=_NN):
    return lax.dot_general(a, b, dims, preferred_element_type=F32)


def _logical(t, groups):
    return (t.shape[-2], t.shape[-1] * groups)


def _block(tr, tc, groups, cols, where):
    if groups == 1:
        return pl.BlockSpec((tr, tc), where)
    per = cols // groups // tc

    def index(i, j, s):
        r, c = where(i, j, s)
        return (c // per, r, c % per)

    return pl.BlockSpec((None, tr, tc), index)


def matmul(a, b, mode, out_dtype, name, ga=1, gb=1, go=1, tm=None, tn=None, tk=None):
    (ar, ac), (br, bc) = _logical(a, ga), _logical(b, gb)
    if mode == "nn":
        m, k, n = ar, ac, bc
        a_col, b_col = "k", "n"
    elif mode == "nt":
        m, k, n = ar, ac, br
        a_col, b_col = "k", "k"
    else:
        k, m, n = ar, ac, bc
        a_col, b_col = "m", "n"
    limit = {"m": m, "n": n // go, "k": k}
    limit[a_col] = min(limit[a_col], ac // ga)
    limit[b_col] = min(limit[b_col], bc // gb)
    tm = tm or _tile(limit["m"], 2048, 128 if mode == "tn" else 16)
    tn = tn or _tile(limit["n"], 512)
    tk = tk or _tile(limit["k"], 2048, 16 if mode == "tn" else 128)
    nk = k // tk
    if mode == "nn":
        a_spec = _block(tm, tk, ga, ac, lambda i, j, s: (i, s))
        b_spec = _block(tk, tn, gb, bc, lambda i, j, s: (s, j))
        dims = _NN
    elif mode == "nt":
        a_spec = _block(tm, tk, ga, ac, lambda i, j, s: (i, s))
        b_spec = _block(tn, tk, gb, bc, lambda i, j, s: (j, s))
        dims = _NT
    else:
        a_spec = _block(tk, tm, ga, ac, lambda i, j, s: (s, i))
        b_spec = _block(tk, tn, gb, bc, lambda i, j, s: (s, j))
        dims = _TN
    o_spec = _block(tm, tn, go, n, lambda i, j, s: (i, j))
    out_shape = _sds((m, n), out_dtype) if go == 1 else _sds((go, m, n // go), out_dtype)

    def body(a_ref, b_ref, o_ref, acc_ref):
        s = pl.program_id(2)

        @pl.when(s == 0)
        def _():
            acc_ref[...] = jnp.zeros_like(acc_ref)

        acc_ref[...] += _dot(a_ref[...], b_ref[...], dims)

        @pl.when(s == nk - 1)
        def _():
            o_ref[...] = acc_ref[...].astype(o_ref.dtype)

    return pl.pallas_call(
        body, name=name, out_shape=out_shape, grid=(m // tm, n // tn, nk),
        in_specs=[a_spec, b_spec], out_specs=o_spec,
        scratch_shapes=[pltpu.VMEM((tm, tn), F32)],
        compiler_params=_params(("parallel", "parallel", "arbitrary"), VMEM_BIG),
    )(a, b)


def _rows(tm, n):
    return pl.BlockSpec((tm, n), lambda i: (i, 0))


def _vec(n):
    return pl.BlockSpec((1, n), lambda i: (0, 0))


class Vec(NamedTuple):
    array: jax.Array
    row: int
    col: int


def _vec_in(v, d):
    return pl.BlockSpec((None, 1, d), lambda i: (v.row, 0, v.col))


def modnorm_fwd(x, g, sc, sh, name):
    s, d = x.shape
    tm = _tile(s, 256, 8)

    def body(x_ref, g_ref, sc_ref, sh_ref, o_ref):
        xv = x_ref[...]
        r = lax.rsqrt(jnp.mean(xv * xv, axis=-1, keepdims=True) + EPS)
        o_ref[...] = ((xv * r) * g_ref[...] * (1.0 + sc_ref[...]) + sh_ref[...]).astype(BF16)

    return pl.pallas_call(
        body, name=name, out_shape=_sds((s, d), BF16), grid=(s // tm,),
        in_specs=[_rows(tm, d), _vec_in(g, d), _vec_in(sc, d), _vec_in(sh, d)], out_specs=_rows(tm, d),
        compiler_params=_params(("parallel",)),
    )(x, g.array, sc.array, sh.array)


def norm_bwd(x, dh, g, sc, dres, name):
    s, d = x.shape
    tm, streams = _row_streams(s)
    nsteps = s // tm

    def body(x_ref, dh_ref, g_ref, sc_ref, dr_ref, dx_ref, dsh_ref, dsc_ref, dg_ref, a2_ref):
        i = pl.program_id(0)

        @pl.when(i == 0)
        def _():
            dsh_ref[...] = jnp.zeros_like(dsh_ref)
            a2_ref[...] = jnp.zeros_like(a2_ref)

        for rs in streams:
            xv = x_ref[rs, :]
            dh = dh_ref[rs, :].astype(F32)
            r = lax.rsqrt(jnp.mean(xv * xv, axis=-1, keepdims=True) + EPS)
            xh = xv * r
            dsh_ref[...] += jnp.sum(dh, axis=0, keepdims=True)
            a2_ref[...] += jnp.sum(dh * xh, axis=0, keepdims=True)
            dxh = dh * (g_ref[...] * (1.0 + sc_ref[...]))
            dx = r * (dxh - xh * jnp.mean(dxh * xh, axis=-1, keepdims=True))
            dx_ref[rs, :] = dr_ref[rs, :] + dx

        @pl.when(i == nsteps - 1)
        def _():
            dsc_ref[...] = a2_ref[...] * g_ref[...]
            dg_ref[...] = a2_ref[...] * (1.0 + sc_ref[...])

    return pl.pallas_call(
        body, name=name, out_shape=(_sds((s, d)), _sds((1, d)), _sds((1, d)), _sds((1, d))), grid=(nsteps,),
        in_specs=[_rows(tm, d), _rows(tm, d), _vec_in(g, d), _vec_in(sc, d), _rows(tm, d)],
        out_specs=(_rows(tm, d), _vec(d), _vec(d), _vec(d)),
        scratch_shapes=[pltpu.VMEM((1, d), F32)],
        compiler_params=_params(("arbitrary",), VMEM_BIG),
    )(x, dh, g.array, sc.array, dres)


_ROW_STREAM = 256


def _row_streams(s):
    tm = _tile(s, 2 * _ROW_STREAM, 8)
    sub = min(tm, _ROW_STREAM)
    return tm, [slice(r * sub, (r + 1) * sub) for r in range(tm // sub)]


def resid_modnorm_fwd(x, y, gate, g, sc, sh, name):
    s, d = x.shape
    tm, streams = _row_streams(s)

    def body(x_ref, y_ref, gate_ref, g_ref, sc_ref, sh_ref, xo_ref, h_ref):
        for rs in streams:
            xv = x_ref[rs, :] + gate_ref[...] * y_ref[rs, :].astype(F32)
            xo_ref[rs, :] = xv
            r = lax.rsqrt(jnp.mean(xv * xv, axis=-1, keepdims=True) + EPS)
            h_ref[rs, :] = ((xv * r) * g_ref[...] * (1.0 + sc_ref[...]) + sh_ref[...]).astype(BF16)

    return pl.pallas_call(
        body, name=name, out_shape=(_sds((s, d)), _sds((s, d), BF16)), grid=(s // tm,),
        in_specs=[_rows(tm, d), _rows(tm, d), _vec_in(gate, d), _vec_in(g, d), _vec_in(sc, d), _vec_in(sh, d)],
        out_specs=(_rows(tm, d), _rows(tm, d)),
        compiler_params=_params(("parallel",), VMEM_BIG),
    )(x, y, gate.array, g.array, sc.array, sh.array)


def norm_gate_bwd(x, dh, g, sc, dres, y, gate, name):
    s, d = x.shape
    tm, streams = _row_streams(s)
    nsteps = s // tm

    def body(x_ref, dh_ref, g_ref, sc_ref, dr_ref, y_ref, gate_ref, dx_ref, dsh_ref, dsc_ref, dg_ref, dy_ref,
             dgate_ref, a2_ref):
        i = pl.program_id(0)

        @pl.when(i == 0)
        def _():
            dsh_ref[...] = jnp.zeros_like(dsh_ref)
            a2_ref[...] = jnp.zeros_like(a2_ref)
            dgate_ref[...] = jnp.zeros_like(dgate_ref)

        for rs in streams:
            xv = x_ref[rs, :]
            dh = dh_ref[rs, :].astype(F32)
            r = lax.rsqrt(jnp.mean(xv * xv, axis=-1, keepdims=True) + EPS)
            xh = xv * r
            dsh_ref[...] += jnp.sum(dh, axis=0, keepdims=True)
            a2_ref[...] += jnp.sum(dh * xh, axis=0, keepdims=True)
            dxh = dh * (g_ref[...] * (1.0 + sc_ref[...]))
            dr = dr_ref[rs, :] + r * (dxh - xh * jnp.mean(dxh * xh, axis=-1, keepdims=True))
            dx_ref[rs, :] = dr
            dy_ref[rs, :] = (dr * gate_ref[...]).astype(BF16)
            dgate_ref[...] += jnp.sum(dr * y_ref[rs, :].astype(F32), axis=0, keepdims=True)

        @pl.when(i == nsteps - 1)
        def _():
            dsc_ref[...] = a2_ref[...] * g_ref[...]
            dg_ref[...] = a2_ref[...] * (1.0 + sc_ref[...])

    vec = _sds((1, d))
    return pl.pallas_call(
        body, name=name, out_shape=(_sds((s, d)), vec, vec, vec, _sds((s, d), BF16), vec), grid=(nsteps,),
        in_specs=[_rows(tm, d), _rows(tm, d), _vec_in(g, d), _vec_in(sc, d), _rows(tm, d), _rows(tm, d), _vec_in(gate, d)],
        out_specs=(_rows(tm, d), _vec(d), _vec(d), _vec(d), _rows(tm, d), _vec(d)),
        scratch_shapes=[pltpu.VMEM((1, d), F32)],
        compiler_params=_params(("arbitrary",), VMEM_BIG),
    )(x, dh, g.array, sc.array, dres, y, gate.array)


def final_fused(x, f, gate, g, tgt):
    s, d = x.shape
    tm, streams = _row_streams(s)

    def body(x_ref, f_ref, gate_ref, g_ref, t_ref, dx_ref, dg_ref, loss_ref, df_ref, dgate_ref):
        @pl.when(pl.program_id(0) == 0)
        def _():
            dg_ref[...] = jnp.zeros_like(dg_ref)
            loss_ref[...] = jnp.zeros_like(loss_ref)
            dgate_ref[...] = jnp.zeros_like(dgate_ref)

        gatev, gv = gate_ref[...], g_ref[...]
        for rs in streams:
            fv = f_ref[rs, :].astype(F32)
            xv = x_ref[rs, :] + gatev * fv
            r = lax.rsqrt(jnp.mean(xv * xv, axis=-1, keepdims=True) + EPS)
            xh = xv * r
            e = xh * gv - t_ref[rs, :]
            row = jnp.sum(e * e, axis=-1, keepdims=True) * (0.5 / d)
            loss_ref[...] += jnp.sum(row, axis=0, keepdims=True)
            dy = e * (1.0 / d)
            dg_ref[...] += jnp.sum(dy * xh, axis=0, keepdims=True)
            dxh = dy * gv
            dx = r * (dxh - xh * jnp.mean(dxh * xh, axis=-1, keepdims=True))
            dx_ref[rs, :] = dx
            df_ref[rs, :] = (dx * gatev).astype(BF16)
            dgate_ref[...] += jnp.sum(dx * fv, axis=0, keepdims=True)

    vec = _sds((1, d))
    return pl.pallas_call(
        body, name="final_fused", out_shape=(_sds((s, d)), vec, _sds((1, LANES)), _sds((s, d), BF16), vec),
        grid=(s // tm,),
        in_specs=[_rows(tm, d), _rows(tm, d), _vec_in(gate, d), _vec_in(g, d), _rows(tm, d)],
        out_specs=(_rows(tm, d), _vec(d), _vec(LANES), _rows(tm, d), _vec(d)),
        compiler_params=_params(("arbitrary",), VMEM_BIG),
    )(x, f, gate.array, g.array, tgt)


def _taps(v):
    return _shift_down(v, 2), _shift_down(v, 1), v


def _conv3_taps(taps, w):
    return w[0:1, :] * taps[0] + w[1:2, :] * taps[1] + w[2:3, :] * taps[2]


def _conv3(v, w):
    return _conv3_taps(_taps(v), w)


def _conv3_t(dv, w):
    return w[0:1, :] * _shift_up(dv, 2) + w[1:2, :] * _shift_up(dv, 1) + w[2:3, :] * dv


def _conv3_dw_taps(dv, taps):
    return jnp.concatenate([jnp.sum(dv * t, axis=0, keepdims=True) for t in taps], axis=0)


def _conv3_dw(dv, v):
    return _conv3_dw_taps(dv, _taps(v))


def gconv_fwd(z, conv_w):
    s = z.shape[0]
    nb = A_WIDTH // LANES

    def body(b_ref, c_ref, a_ref, w_ref, o_ref):
        b, c, a = b_ref[...].astype(F32), c_ref[...].astype(F32), a_ref[...].astype(F32)
        o_ref[...] = (b * _conv3(c * a, w_ref[...])).astype(BF16)

    col = lambda off: pl.BlockSpec((s, LANES), lambda j: (0, off + j))
    return pl.pallas_call(
        body, name="gconv_fwd", out_shape=_sds((s, A_WIDTH + _B_WIDTH), BF16), grid=(nb,),
        in_specs=[col(0), col(nb), col(2 * nb), pl.BlockSpec((3, LANES), lambda j: (0, j))],
        out_specs=pl.BlockSpec((s, LANES), lambda j: (0, j)),
        compiler_params=_params(("parallel",), VMEM_BIG),
    )(z, z, z, conv_w)


def gconv_bwd(z, conv_w, dycat):
    s = z.shape[0]
    nb = A_WIDTH // LANES

    def body(b_ref, c_ref, a_ref, w_ref, dy_ref, db_ref, dc_ref, da_ref, dw_ref):
        c, a, w, dy = c_ref[...].astype(F32), a_ref[...].astype(F32), w_ref[...], dy_ref[...].astype(F32)
        ca = c * a
        db_ref[...] = (dy * _conv3(ca, w)).astype(BF16)
        dconv = dy * b_ref[...].astype(F32)
        dw_ref[...] = _conv3_dw(dconv, ca)
        dca = _conv3_t(dconv, w)
        dc_ref[...] = (dca * a).astype(BF16)
        da_ref[...] = (dca * c).astype(BF16)

    col = lambda off: pl.BlockSpec((s, LANES), lambda j: (0, off + j))
    wspec = pl.BlockSpec((3, LANES), lambda j: (0, j))
    part = _sds((s, A_WIDTH), BF16)
    return pl.pallas_call(
        body, name="gconv_bwd", out_shape=(part, part, part, _sds((3, A_WIDTH))), grid=(nb,),
        in_specs=[col(0), col(nb), col(2 * nb), wspec, col(0)],
        out_specs=(col(0), col(0), col(0), wspec),
        compiler_params=_params(("parallel",), VMEM_BIG),
    )(z, z, z, conv_w, dycat)


def _pool_counts(s, w):
    t = lax.broadcasted_iota(jnp.int32, (s, 1), 0)
    return jnp.minimum(t + 1, w).astype(F32)


def _pooled(p, levels):
    acc = p
    for lv in range(levels):
        acc = acc + _shift_down(acc, 2 ** lv)
    return acc / _pool_counts(p.shape[0], 2 ** levels) - p


_B_WIDTH = B_GROUPS * LANES


def pool_fwd(z, mix_w, scale, ycat):
    s = z.shape[0]

    def body(p_ref, m_ref, sc_ref, ycat_ref, o_ref):
        del ycat_ref
        for g in range(B_GROUPS):
            cols = slice(g * LANES, (g + 1) * LANES)
            pooled = _pooled(p_ref[:, cols].astype(F32), g + 1)
            y = _dot(pooled.astype(BF16), m_ref[g].astype(BF16))
            o_ref[:, cols] = (y * sc_ref[:, cols]).astype(BF16)

    return pl.pallas_call(
        body, name="pool_fwd", out_shape=_sds(ycat.shape, BF16), grid=(1,),
        in_specs=[pl.BlockSpec((s, _B_WIDTH), lambda i: (0, 3 * A_WIDTH // _B_WIDTH)),
                  pl.BlockSpec((B_GROUPS, LANES, LANES), lambda i: (0, 0, 0)), pl.BlockSpec((1, _B_WIDTH), lambda i: (0, 0)),
                  pl.BlockSpec(memory_space=pl.ANY)],
        out_specs=pl.BlockSpec((s, _B_WIDTH), lambda i: (0, A_WIDTH // _B_WIDTH)),
        input_output_aliases={3: 0},
        compiler_params=_params(("arbitrary",), VMEM_BIG),
    )(z, mix_w, scale, ycat)


def pool_bwd(z, mix_w, scale, dycat):
    s = z.shape[0]

    def body(p_ref, m_ref, sc_ref, dy_ref, dp_ref, dm_ref, dsc_ref):
        for g in range(B_GROUPS):
            cols = slice(g * LANES, (g + 1) * LANES)
            pooled = _pooled(p_ref[:, cols].astype(F32), g + 1)
            mw = m_ref[g].astype(BF16)
            pb = pooled.astype(BF16)
            dy = dy_ref[:, cols].astype(F32)
            dsc_ref[:, cols] = jnp.sum(dy * _dot(pb, mw), axis=0, keepdims=True)
            dmix = (dy * sc_ref[:, cols]).astype(BF16)
            dm_ref[g] = _dot(pb, dmix, _TN)
            dpool = _dot(dmix, mw, _NT)
            acc = dpool / _pool_counts(s, 2 ** (g + 1))
            for lv in range(g + 1):
                acc = acc + _shift_up(acc, 2 ** lv)
            dp_ref[:, cols] = (acc - dpool).astype(BF16)

    wide = lambda c: pl.BlockSpec((s, _B_WIDTH), lambda i: (0, c))
    mspec = pl.BlockSpec((B_GROUPS, LANES, LANES), lambda i: (0, 0, 0))
    vspec = pl.BlockSpec((1, _B_WIDTH), lambda i: (0, 0))
    return pl.pallas_call(
        body, name="pool_bwd", out_shape=(_sds((s, _B_WIDTH), BF16), _sds((B_GROUPS, LANES, LANES)), _sds((1, _B_WIDTH))),
        grid=(1,), in_specs=[wide(3 * A_WIDTH // _B_WIDTH), mspec, vspec, wide(A_WIDTH // _B_WIDTH)],
        out_specs=(wide(0), mspec, vspec),
        compiler_params=_params(("arbitrary",), VMEM_BIG),
    )(z, mix_w, scale, dycat)


_FF_BLOCKS = D_FF // FF_UNIT


def _ff_spec(s):
    return pl.BlockSpec((2, s, FF_UNIT), lambda j: (0, 0, j))


def _ff_wspecs():
    return [pl.BlockSpec((3, FF_UNIT), lambda j: (0, j)), pl.BlockSpec((3, FF_UNIT), lambda j: (0, _FF_BLOCKS + j))]


_FF_ROWS = 512
_FF_HALO = 16


def _chunk_taps(z_ref, half, c):
    start = pl.multiple_of(c * _FF_ROWS, _FF_ROWS)
    before = pl.multiple_of(jnp.maximum(c * _FF_ROWS - _FF_HALO, 0), _FF_HALO)
    halo = z_ref[half, pl.ds(before, _FF_HALO), :].astype(F32)
    halo = jnp.where(c > 0, halo, 0.0)
    win = jnp.concatenate([halo, z_ref[half, pl.ds(start, _FF_ROWS), :].astype(F32)], axis=0)
    return tuple(pltpu.roll(win, k, axis=0)[_FF_HALO:] for k in (2, 1)) + (win[_FF_HALO:],)


def _fold8(v):
    acc = v[0:8]
    for r in range(8, v.shape[0], 8):
        acc = acc + v[r:r + 8]
    return acc


_FF_CHUNK = 256


def ffn_act_down(zf, conv_w, w_down, name):
    s, d = zf.shape[1], w_down.shape[1]
    nk = D_FF // _FF_CHUNK
    chunk = lambda k: jnp.minimum(k, nk - 1)

    def body(z_ref, wg_ref, wu_ref, wd_ref, a_ref, f_ref, held_ref, acc_ref):
        k = pl.program_id(0)

        @pl.when(k == 0)
        def _():
            held_ref[...] = jnp.zeros_like(held_ref)
            acc_ref[...] = jnp.zeros_like(acc_ref)

        acc_ref[...] += _dot(held_ref[(k + 1) % 2], wd_ref[...])
        g = _conv3(z_ref[0].astype(F32), wg_ref[...])
        u = _conv3(z_ref[1].astype(F32), wu_ref[...])
        act = (g * _sigmoid(g) * u).astype(BF16)
        a_ref[...] = act
        held_ref[k % 2] = act

        @pl.when(k == nk)
        def _():
            f_ref[...] = acc_ref[...].astype(BF16)

    return pl.pallas_call(
        body, name=name, out_shape=(_sds((s, D_FF), BF16), _sds((s, d), BF16)), grid=(nk + 1,),
        in_specs=[pl.BlockSpec((2, s, _FF_CHUNK), lambda k: (0, 0, chunk(k))),
                  pl.BlockSpec((3, _FF_CHUNK), lambda k: (0, chunk(k))),
                  pl.BlockSpec((3, _FF_CHUNK), lambda k: (0, nk + chunk(k))),
                  pl.BlockSpec((_FF_CHUNK, d), lambda k: (jnp.maximum(k - 1, 0), 0))],
        out_specs=(pl.BlockSpec((s, _FF_CHUNK), lambda k: (0, chunk(k))), pl.BlockSpec((s, d), lambda k: (0, 0))),
        scratch_shapes=[pltpu.VMEM((2, s, _FF_CHUNK), BF16), pltpu.VMEM((s, d), F32)],
        compiler_params=_params(("arbitrary",), VMEM_BIG),
    )(zf, conv_w, conv_w, w_down)


def ffn_act_bwd(zf, conv_w, da, name):
    s = zf.shape[1]
    assert s % _FF_ROWS == 0
    nchunks = s // _FF_ROWS

    def body(z_ref, wg_ref, wu_ref, da_ref, dz_ref, dw_ref, dg_ref, du_ref):
        wg, wu = wg_ref[...], wu_ref[...]

        def first(c, acc):
            rows = pl.ds(pl.multiple_of(c * _FF_ROWS, _FF_ROWS), _FF_ROWS)
            tg, tu = _chunk_taps(z_ref, 0, c), _chunk_taps(z_ref, 1, c)
            g = _conv3_taps(tg, wg)
            u = _conv3_taps(tu, wu)
            dav = da_ref[rows, :].astype(F32)
            sg = _sigmoid(g)
            dg = dav * u * (sg * (1.0 + g * (1.0 - sg)))
            du = dav * (g * sg)
            dg_ref[rows, :] = dg
            du_ref[rows, :] = du
            return tuple(a + _fold8(d * t) for a, (d, t) in zip(acc, [(dg, t) for t in tg] + [(du, t) for t in tu]))

        zero = jnp.zeros((8, FF_UNIT), F32)
        acc = lax.fori_loop(0, nchunks, first, (zero,) * 6)
        sums = [jnp.sum(a, axis=0, keepdims=True) for a in acc]
        dw_ref[0] = jnp.concatenate(sums[:3], axis=0)
        dw_ref[1] = jnp.concatenate(sums[3:], axis=0)

        tail = pl.ds(s, _FF_HALO)
        dg_ref[tail, :] = jnp.zeros((_FF_HALO, FF_UNIT), F32)
        du_ref[tail, :] = jnp.zeros((_FF_HALO, FF_UNIT), F32)
        span = _FF_ROWS + _FF_HALO

        def second(c, carry):
            start = pl.multiple_of(c * _FF_ROWS, _FF_ROWS)
            for half, (d_ref, w) in enumerate(((dg_ref, wg), (du_ref, wu))):
                win = d_ref[pl.ds(start, span), :]
                dz = (w[0:1, :] * pltpu.roll(win, span - 2, axis=0)[:_FF_ROWS]
                      + w[1:2, :] * pltpu.roll(win, span - 1, axis=0)[:_FF_ROWS] + w[2:3, :] * win[:_FF_ROWS])
                dz_ref[half, pl.ds(start, _FF_ROWS), :] = dz.astype(BF16)
            return carry

        lax.fori_loop(0, nchunks, second, 0)

    return pl.pallas_call(
        body, name=name, out_shape=(_sds((2, s, D_FF), BF16), _sds((2, 3, D_FF))), grid=(_FF_BLOCKS,),
        in_specs=[_ff_spec(s)] + _ff_wspecs() + [pl.BlockSpec((s, FF_UNIT), lambda j: (0, j))],
        out_specs=(_ff_spec(s), pl.BlockSpec((2, 3, FF_UNIT), lambda j: (0, 0, j))),
        scratch_shapes=[pltpu.VMEM((s + _FF_HALO, FF_UNIT), F32), pltpu.VMEM((s + _FF_HALO, FF_UNIT), F32)],
        compiler_params=_params(("parallel",), VMEM_BIG),
    )(zf, conv_w, conv_w, da)


def _rope(v, cs, s1, s2):
    return v * cs + pltpu.roll(v, LANES - C_ROPE // 2, axis=1) * s1 + pltpu.roll(v, C_ROPE // 2, axis=1) * s2


def _rope_t(dv, cs, s1, s2):
    return dv * cs + pltpu.roll(dv * s1, C_ROPE // 2, axis=1) + pltpu.roll(dv * s2, LANES - C_ROPE // 2, axis=1)


def _kpe_mask(shape):
    lane = lax.broadcasted_iota(jnp.int32, shape, 1)
    return (lane >= C_NOPE) & (lane < C_NOPE + C_ROPE)


def _rms(v, g):
    r = lax.rsqrt(jnp.mean(v * v, axis=-1, keepdims=True) + EPS)
    return v * r, r


def _rms_bwd(dn, xh, r, g):
    dxh = dn * g
    return r * (dxh - xh * jnp.mean(dxh * xh, axis=-1, keepdims=True)), jnp.sum(dn * xh, axis=0, keepdims=True)


_ZQ = C_Q_RANK + C_KV_RANK + HEAD_PAD
_HW = C_HEADS * HEAD_PAD


_MLA_ROWS = 256


def _mla_tiles(s):
    tm = _tile(s, 2 * _MLA_ROWS, 8)
    sub = min(tm, _MLA_ROWS)
    return tm, [slice(r * sub, (r + 1) * sub) for r in range(tm // sub)]


def mla_pre_fwd(z, gq, gkv, wq, wk, wv, cs, s1, s2):
    s = z.shape[0]
    tm, streams = _mla_tiles(s)

    def body(z_ref, gq_ref, gkv_ref, wq_ref, wk_ref, wv_ref, cs_ref, s1_ref, s2_ref, q_ref, k_ref, v_ref):
        for rs in streams:
            zv = z_ref[rs, :].astype(F32)
            cst, s1t, s2t = cs_ref[rs, :], s1_ref[rs, :], s2_ref[rs, :]
            qh, _ = _rms(zv[:, :C_Q_RANK], None)
            qn = (qh * gq_ref[...]).astype(BF16)
            q = _dot(qn, wq_ref[...])
            kh, _ = _rms(zv[:, C_Q_RANK:C_Q_RANK + C_KV_RANK], None)
            kvn = (kh * gkv_ref[...]).astype(BF16)
            k = _dot(kvn, wk_ref[...])
            v_ref[rs, :] = _dot(kvn, wv_ref[...]).astype(BF16)
            kpe = _rope(zv[:, C_Q_RANK + C_KV_RANK:], cst, s1t, s2t)
            for h in range(C_HEADS):
                sl = slice(h * HEAD_PAD, (h + 1) * HEAD_PAD)
                q_ref[rs, sl] = _rope(q[:, sl], cst, s1t, s2t).astype(BF16)
                k_ref[rs, sl] = (k[:, sl] + kpe).astype(BF16)

    full = lambda r, c: pl.BlockSpec((r, c), lambda i: (0, 0))
    hw = _sds((s, _HW), BF16)
    return pl.pallas_call(
        body, name="mla_pre_fwd", out_shape=(hw, hw, hw), grid=(s // tm,),
        in_specs=[_rows(tm, _ZQ), _vec(C_Q_RANK), _vec(C_KV_RANK), full(C_Q_RANK, _HW), full(C_KV_RANK, _HW),
                  full(C_KV_RANK, _HW), _rows(tm, LANES), _rows(tm, LANES), _rows(tm, LANES)],
        out_specs=(_rows(tm, _HW), _rows(tm, _HW), _rows(tm, _HW)),
        compiler_params=_params(("parallel",), VMEM_BIG),
    )(z, gq, gkv, wq, wk, wv, cs, s1, s2)


def mla_pre_bwd(z, gq, gkv, wq, wk, wv, cs, s1, s2, dq, dk, dv):
    s = z.shape[0]
    tm, streams = _mla_tiles(s)

    def body(z_ref, gq_ref, gkv_ref, wq_ref, wk_ref, wv_ref, cs_ref, s1_ref, s2_ref, dq_ref, dk_ref, dv_ref,
             dz_ref, dwq_ref, dwk_ref, dwv_ref, dgq_ref, dgkv_ref):
        @pl.when(pl.program_id(0) == 0)
        def _():
            dwq_ref[...] = jnp.zeros_like(dwq_ref)
            dwk_ref[...] = jnp.zeros_like(dwk_ref)
            dwv_ref[...] = jnp.zeros_like(dwv_ref)
            dgq_ref[...] = jnp.zeros_like(dgq_ref)
            dgkv_ref[...] = jnp.zeros_like(dgkv_ref)

        gqv, gkvv = gq_ref[...], gkv_ref[...]
        for rs in streams:
            zv = z_ref[rs, :].astype(F32)
            cst, s1t, s2t = cs_ref[rs, :], s1_ref[rs, :], s2_ref[rs, :]
            qh, rq = _rms(zv[:, :C_Q_RANK], None)
            qn = (qh * gqv).astype(BF16)
            kh, rk = _rms(zv[:, C_Q_RANK:C_Q_RANK + C_KV_RANK], None)
            kvn = (kh * gkvv).astype(BF16)

            dqv = dq_ref[rs, :].astype(F32)
            dqp = jnp.concatenate(
                [_rope_t(dqv[:, h * HEAD_PAD:(h + 1) * HEAD_PAD], cst, s1t, s2t) for h in range(C_HEADS)], axis=1
            ).astype(BF16)
            dwq_ref[...] += _dot(qn, dqp, _TN)
            dqn = _dot(dqp, wq_ref[...], _NT)
            dql, dgq = _rms_bwd(dqn, qh, rq, gqv)
            dgq_ref[...] += dgq

            dkv = dk_ref[rs, :]
            dkb = dkv.astype(BF16)
            dvb = dv_ref[rs, :].astype(BF16)
            dwk_ref[...] += _dot(kvn, dkb, _TN)
            dwv_ref[...] += _dot(kvn, dvb, _TN)
            dkvn = _dot(dkb, wk_ref[...], _NT) + _dot(dvb, wv_ref[...], _NT)
            dkl, dgkv = _rms_bwd(dkvn, kh, rk, gkvv)
            dgkv_ref[...] += dgkv

            dkpe = dkv[:, :HEAD_PAD]
            for h in range(1, C_HEADS):
                dkpe = dkpe + dkv[:, h * HEAD_PAD:(h + 1) * HEAD_PAD]
            dkpe = _rope_t(jnp.where(_kpe_mask(dkpe.shape), dkpe, 0.0), cst, s1t, s2t)
            dz_ref[rs, :] = jnp.concatenate([dql, dkl, dkpe], axis=1).astype(BF16)

    full = lambda r, c: pl.BlockSpec((r, c), lambda i: (0, 0))
    return pl.pallas_call(
        body, name="mla_pre_bwd",
        out_shape=(_sds((s, _ZQ), BF16), _sds((C_Q_RANK, _HW)), _sds((C_KV_RANK, _HW)), _sds((C_KV_RANK, _HW)),
                   _sds((1, C_Q_RANK)), _sds((1, C_KV_RANK))),
        grid=(s // tm,),
        in_specs=[_rows(tm, _ZQ), _vec(C_Q_RANK), _vec(C_KV_RANK), full(C_Q_RANK, _HW), full(C_KV_RANK, _HW),
                  full(C_KV_RANK, _HW), _rows(tm, LANES), _rows(tm, LANES), _rows(tm, LANES),
                  _rows(tm, _HW), _rows(tm, _HW), _rows(tm, _HW)],
        out_specs=(_rows(tm, _ZQ), full(C_Q_RANK, _HW), full(C_KV_RANK, _HW), full(C_KV_RANK, _HW),
                   _vec(C_Q_RANK), _vec(C_KV_RANK)),
        compiler_params=_params(("arbitrary",), VMEM_BIG),
    )(z, gq, gkv, wq, wk, wv, cs, s1, s2, dq, dk, dv)


_ATT_SCALE = (C_NOPE + C_ROPE) ** -0.5
_NEG = -1e30


def _att_exp(q, k, row0, ends_here):
    sc = _dot(q, k, _NT) * _ATT_SCALE
    tq, nk = sc.shape
    if ends_here:
        last = sc[:, nk - tq:]
        row = lax.broadcasted_iota(jnp.int32, last.shape, 0)
        col = lax.broadcasted_iota(jnp.int32, last.shape, 1)
        last = jnp.where(col <= row, last, _NEG)
        sc = last if nk == tq else jnp.concatenate([sc[:, :nk - tq], last], axis=1)
    else:
        qpos = row0 + lax.broadcasted_iota(jnp.int32, sc.shape, 0)
        kpos = lax.broadcasted_iota(jnp.int32, sc.shape, 1)
        sc = jnp.where(kpos <= qpos, sc, _NEG)
    e = jnp.exp(sc - jnp.max(sc, axis=-1, keepdims=True))
    return e, 1.0 / jnp.sum(e, axis=-1, keepdims=True)


def _causal_cases(i, nq, tq, fn):
    if nq > 8:
        fn(nq * tq, False)
        return
    for blk in range(nq):
        pl.when(i == blk)(functools.partial(fn, (blk + 1) * tq, True))


_FWD_HEADS_PER_STEP = 4
_BWD_HEADS_PER_STEP = 2


def _head_lanes(heads):
    return [slice(h * HEAD_PAD, (h + 1) * HEAD_PAD) for h in range(heads)]


def attn_fwd(q, k, v):
    s = q.shape[0]
    tq = _tile(s, 256, 8)
    nq = s // tq
    heads = _FWD_HEADS_PER_STEP
    wide = heads * HEAD_PAD

    def body(q_ref, k_ref, v_ref, o_ref):
        i = pl.program_id(1)

        def case(nk, ends_here):
            for hd in _head_lanes(heads):
                e, inv = _att_exp(q_ref[:, hd], k_ref[:nk, hd], i * tq, ends_here)
                o_ref[:, hd] = (_dot(e.astype(BF16), v_ref[:nk, hd]) * inv).astype(BF16)

        _causal_cases(i, nq, tq, case)

    qspec = pl.BlockSpec((tq, wide), lambda h, i: (i, h))
    kspec = pl.BlockSpec((s, wide), lambda h, i: (0, h))
    return pl.pallas_call(
        body, name="attn_fwd", out_shape=_sds((s, _HW + _DW), BF16), grid=(C_HEADS // heads, s // tq),
        in_specs=[qspec, kspec, kspec], out_specs=qspec,
        compiler_params=_params(("parallel", "parallel"), VMEM_BIG),
    )(q, k, v)


def attn_bwd(q, k, v, o, do_all):
    s = q.shape[0]
    tq = _tile(s, 256, 8)
    heads = _BWD_HEADS_PER_STEP
    wide = heads * HEAD_PAD

    def body(q_ref, k_ref, v_ref, o_ref, do_ref, dq_ref, dk_ref, dv_ref):
        i = pl.program_id(1)

        @pl.when(i == 0)
        def _():
            dk_ref[...] = jnp.zeros_like(dk_ref)
            dv_ref[...] = jnp.zeros_like(dv_ref)

        def case(nk, ends_here):
            for hd in _head_lanes(heads):
                qv, kv, vv, dov = q_ref[:, hd], k_ref[:nk, hd], v_ref[:nk, hd], do_ref[:, hd]
                e, inv = _att_exp(qv, kv, i * tq, ends_here)
                p = e * inv
                dp = _dot(dov, vv, _NT)
                delta = jnp.sum(dov.astype(F32) * o_ref[:, hd].astype(F32), axis=-1, keepdims=True)
                ds = (p * (dp - delta) * _ATT_SCALE).astype(BF16)
                dq_ref[:, hd] = _dot(ds, kv).astype(BF16)
                dk_ref[:nk, hd] += _dot(ds, qv, _TN)
                dv_ref[:nk, hd] += _dot(p.astype(BF16), dov, _TN)

        _causal_cases(i, s // tq, tq, case)

    qspec = pl.BlockSpec((tq, wide), lambda h, i: (i, h))
    kspec = pl.BlockSpec((s, wide), lambda h, i: (0, h))
    return pl.pallas_call(
        body, name="attn_bwd", out_shape=(_sds((s, _HW), BF16), _sds((s, _HW)), _sds((s, _HW))),
        grid=(C_HEADS // heads, s // tq),
        in_specs=[qspec, kspec, kspec, qspec, qspec], out_specs=(qspec, kspec, kspec),
        compiler_params=_params(("parallel", "arbitrary"), VMEM_BIG),
    )(q, k, v, o, do_all)


_DW = D_GROUPS * LANES


def _tril_bf16(w):
    r = lax.broadcasted_iota(jnp.int32, w.shape, 0)
    c = lax.broadcasted_iota(jnp.int32, w.shape, 1)
    return jnp.where(c <= r, w, 0.0).astype(BF16)


def _sgu_forward(zu, zv, lg, lb, ws_ref, bs):
    u = _gelu(zu)
    v = _gelu(zv)
    mu = jnp.mean(v, axis=-1, keepdims=True)
    vc = v - mu
    rstd = lax.rsqrt(jnp.mean(vc * vc, axis=-1, keepdims=True) + EPS)
    xh = vc * rstd
    vln = (xh * lg + lb).astype(BF16)
    mixed = []
    for g in range(D_GROUPS):
        wg = _tril_bf16(ws_ref[g])
        mixed.append(_dot(wg, vln[:, g * LANES:(g + 1) * LANES]) + bs[:, g:g + 1])
    return u, xh, rstd, vln, jnp.concatenate(mixed, axis=1)


_SGU_CHUNKS = 4


def sgu_fwd(z, lg, lb, ws, bs_t, ycat):
    s = z.shape[0]
    rows = _SGU_CHUNKS * D_CHUNK

    def body(zu_ref, zv_ref, lg_ref, lb_ref, ws_ref, bs_ref, ycat_ref, o_ref):
        del ycat_ref
        for c in range(_SGU_CHUNKS):
            rs = slice(c * D_CHUNK, (c + 1) * D_CHUNK)
            u, _, _, _, mixed = _sgu_forward(zu_ref[rs, :].astype(F32), zv_ref[rs, :].astype(F32), lg_ref[...],
                                             lb_ref[...], ws_ref, bs_ref[...])
            o_ref[rs, :] = (u * mixed).astype(BF16)

    return pl.pallas_call(
        body, name="sgu_fwd", out_shape=_sds(ycat.shape, BF16), grid=(s // rows,),
        in_specs=[pl.BlockSpec((rows, _DW), lambda n: (n, 1)), pl.BlockSpec((rows, _DW), lambda n: (n, 2)),
                  _vec(_DW), _vec(_DW), pl.BlockSpec((D_GROUPS, D_CHUNK, D_CHUNK), lambda n: (0, 0, 0)),
                  pl.BlockSpec((D_CHUNK, LANES), lambda n: (0, 0)), pl.BlockSpec(memory_space=pl.ANY)],
        out_specs=pl.BlockSpec((rows, _DW), lambda n: (n, _HW // _DW)),
        input_output_aliases={6: 0},
        compiler_params=_params(("parallel",)),
    )(z, z, lg, lb, ws, bs_t, ycat)


def sgu_bwd(z, lg, lb, ws, bs_t, dycat, dy_col):
    s = z.shape[0]
    rows = _SGU_CHUNKS * D_CHUNK

    def body(zu_ref, zv_ref, lg_ref, lb_ref, ws_ref, bs_ref, dy_ref, dzu_ref, dzv_ref, dws_ref, dbs_ref, dlg_ref,
             dlb_ref):
        @pl.when(pl.program_id(0) == 0)
        def _():
            dws_ref[...] = jnp.zeros_like(dws_ref)
            dbs_ref[...] = jnp.zeros_like(dbs_ref)
            dlg_ref[...] = jnp.zeros_like(dlg_ref)
            dlb_ref[...] = jnp.zeros_like(dlb_ref)

        lg = lg_ref[...]
        lane = lax.broadcasted_iota(jnp.int32, (D_CHUNK, LANES), 1)
        row = lax.broadcasted_iota(jnp.int32, (D_CHUNK, D_CHUNK), 0)
        colm = lax.broadcasted_iota(jnp.int32, (D_CHUNK, D_CHUNK), 1)
        for c in range(_SGU_CHUNKS):
            rs = slice(c * D_CHUNK, (c + 1) * D_CHUNK)
            zu, zv = zu_ref[rs, :].astype(F32), zv_ref[rs, :].astype(F32)
            u, xh, rstd, vln, mixed = _sgu_forward(zu, zv, lg, lb_ref[...], ws_ref, bs_ref[...])
            dy = dy_ref[rs, :].astype(F32)
            dzu_ref[rs, :] = (dy * mixed * _gelu_grad(zu)).astype(BF16)
            dmix = dy * u
            dvln = []
            dbs = jnp.zeros((D_CHUNK, LANES), F32)
            for g in range(D_GROUPS):
                sl = slice(g * LANES, (g + 1) * LANES)
                dmg = dmix[:, sl]
                dbs = dbs + jnp.where(lane == g, jnp.sum(dmg, axis=-1, keepdims=True), 0.0)
                dmb = dmg.astype(BF16)
                dws_ref[g] += jnp.where(colm <= row, _dot(dmb, vln[:, sl], _NT), 0.0)
                dvln.append(_dot(_tril_bf16(ws_ref[g]), dmb, _TN))
            dbs_ref[...] += dbs
            dvln = jnp.concatenate(dvln, axis=1)
            dlg_ref[...] += jnp.sum(dvln * xh, axis=0, keepdims=True)
            dlb_ref[...] += jnp.sum(dvln, axis=0, keepdims=True)
            dxh = dvln * lg
            dvv = rstd * (dxh - jnp.mean(dxh, axis=-1, keepdims=True)
                          - xh * jnp.mean(dxh * xh, axis=-1, keepdims=True))
            dzv_ref[rs, :] = (dvv * _gelu_grad(zv)).astype(BF16)

    wsspec = pl.BlockSpec((D_GROUPS, D_CHUNK, D_CHUNK), lambda n: (0, 0, 0))
    chunk = lambda cidx: pl.BlockSpec((rows, _DW), lambda n: (n, cidx))
    return pl.pallas_call(
        body, name="sgu_bwd",
        out_shape=(_sds((s, _DW), BF16), _sds((s, _DW), BF16), _sds((D_GROUPS, D_CHUNK, D_CHUNK)),
                   _sds((D_CHUNK, LANES)), _sds((1, _DW)), _sds((1, _DW))),
        grid=(s // rows,),
        in_specs=[chunk(1), chunk(2), _vec(_DW), _vec(_DW), wsspec, pl.BlockSpec((D_CHUNK, LANES), lambda n: (0, 0)),
                  chunk(dy_col)],
        out_specs=(chunk(0), chunk(0), wsspec, pl.BlockSpec((D_CHUNK, LANES), lambda n: (0, 0)), _vec(_DW), _vec(_DW)),
        compiler_params=_params(("arbitrary",)),
    )(z, z, lg, lb, ws, bs_t, dycat)


def ada_mod(c_all, ada_w, ada_b):
    nl, d, n = ada_w.shape
    nb = c_all.shape[0]
    tn = _tile(n, 512)

    def body(c_ref, w_ref, b_ref, o_ref):
        cv = c_ref[...]
        ca = (cv * _sigmoid(cv)).astype(BF16)
        o_ref[...] = _dot(ca, w_ref[...].astype(BF16)) + b_ref[...]

    return pl.pallas_call(
        body, name="ada_mod", out_shape=_sds((nl, nb, n)), grid=(nl, n // tn),
        in_specs=[pl.BlockSpec((nb, d), lambda l, j: (0, 0)), pl.BlockSpec((None, d, tn), lambda l, j: (l, 0, j)),
                  pl.BlockSpec((None, 1, tn), lambda l, j: (l, 0, j))],
        out_specs=pl.BlockSpec((None, nb, tn), lambda l, j: (l, 0, j)),
        compiler_params=_params(("parallel", "parallel")),
    )(c_all, ada_w, ada_b.reshape(nl, 1, n))


_ADAM_BLOCK = 512 * 1024


def _adam_rows(rows, cols):
    if rows * cols <= _ADAM_BLOCK or rows % 8:
        return rows
    return _tile(rows, max(8, _ADAM_BLOCK // cols), 8)


def _adam_update(w, gv, m, v):
    inv_bc1 = 1.0 / (1.0 - ADAM_B1 ** ADAM_STEP)
    inv_bc2 = 1.0 / (1.0 - ADAM_B2 ** ADAM_STEP)
    nm = ADAM_B1 * m + (1.0 - ADAM_B1) * gv
    nv = ADAM_B2 * v + (1.0 - ADAM_B2) * (gv * gv)
    return -ADAM_LR * ((nm * inv_bc1) / (jnp.sqrt(nv * inv_bc2) + ADAM_EPS) + ADAM_WD * w), nm, nv


def adamw(w, g, m, v, name):
    shape = w.shape
    cols = shape[-1]
    rows = w.size // cols
    tr = _adam_rows(rows, cols)

    def body(w_ref, g_ref, m_ref, v_ref, go_ref, d_ref, nm_ref, nv_ref):
        gv = g_ref[...]
        go_ref[...] = gv
        d_ref[...], nm_ref[...], nv_ref[...] = _adam_update(w_ref[...], gv, m_ref[...], v_ref[...])

    spec = pl.BlockSpec((tr, cols), lambda i: (i, 0))
    out = _sds((rows, cols))
    r2 = lambda t: t.reshape(rows, cols)
    res = pl.pallas_call(
        body, name=name, out_shape=(out,) * 4, grid=(rows // tr,),
        in_specs=[spec] * 4, out_specs=(spec,) * 4, compiler_params=_params(("parallel",), VMEM_BIG),
    )(r2(w), r2(g), r2(m), r2(v))
    return tuple(t.reshape(shape) for t in res)


def adamw_ada(w, c_all, dmod, m, v):
    nl, d, n = w.shape
    tr = _adam_rows(d, n)
    pad = 16 - c_all.shape[0]
    c16 = jnp.pad(c_all, ((0, pad), (0, 0)))
    dm16 = jnp.pad(dmod, ((0, 0), (0, pad), (0, 0)))

    def body(w_ref, c_ref, dm_ref, m_ref, v_ref, g_ref, d_ref, nm_ref, nv_ref):
        cv = c_ref[...]
        gv = _dot((cv * _sigmoid(cv)).astype(BF16), dm_ref[...].astype(BF16), _TN)
        g_ref[...] = gv
        d_ref[...], nm_ref[...], nv_ref[...] = _adam_update(w_ref[...], gv, m_ref[...], v_ref[...])

    spec = pl.BlockSpec((None, tr, n), lambda l, i: (l, i, 0))
    out = _sds((nl, d, n))
    return pl.pallas_call(
        body, name="adamw_ada_w", out_shape=(out, out, out, out), grid=(nl, d // tr),
        in_specs=[spec, pl.BlockSpec((16, tr), lambda l, i: (0, i)), pl.BlockSpec((None, 16, n), lambda l, i: (l, 0, 0)),
                  spec, spec],
        out_specs=(spec,) * 4, compiler_params=_params(("parallel", "parallel"), VMEM_BIG),
    )(w, c16, dm16, m, v)


def adamw_small(ws, gs, ms, vs):
    n = len(ws)
    flat = lambda t: t.reshape(-1, t.shape[-1])

    def body(*refs):
        ins, outs = refs[:4 * n], refs[4 * n:]
        for i in range(n):
            w_ref, g_ref, m_ref, v_ref = ins[4 * i:4 * i + 4]
            outs[3 * i][...], outs[3 * i + 1][...], outs[3 * i + 2][...] = _adam_update(
                w_ref[...], g_ref[...], m_ref[...], v_ref[...])

    operands = [flat(t) for quad in zip(ws, gs, ms, vs) for t in quad]
    res = pl.pallas_call(
        body, name="adamw_small", out_shape=tuple(_sds(flat(w).shape) for w in ws for _ in range(3)),
    )(*operands)
    return [(g, res[3 * i].reshape(w.shape), res[3 * i + 1].reshape(w.shape), res[3 * i + 2].reshape(w.shape))
            for i, (w, g) in enumerate(zip(ws, gs))]


def adamw_layers(w, g0, g1, m, v, name):
    _, rows, cols = w.shape
    tr = _adam_rows(rows, cols)

    def body(w_ref, g0_ref, g1_ref, m_ref, v_ref, g_ref, d_ref, nm_ref, nv_ref):
        gv = jnp.where(pl.program_id(0) == 0, g0_ref[...], g1_ref[...])
        g_ref[...] = gv
        d_ref[...], nm_ref[...], nv_ref[...] = _adam_update(w_ref[...], gv, m_ref[...], v_ref[...])

    spec = pl.BlockSpec((None, tr, cols), lambda l, i: (l, i, 0))
    gspec = pl.BlockSpec((tr, cols), lambda l, i: (i, 0))
    out = _sds((2, rows, cols))
    return pl.pallas_call(
        body, name=name, out_shape=(out, out, out, out), grid=(2, rows // tr),
        in_specs=[spec, gspec, gspec, spec, spec], out_specs=(spec,) * 4,
        compiler_params=_params(("parallel", "parallel"), VMEM_BIG),
    )(w, g0, g1, m, v)


def sum8(gathered):
    _, r, _ = gathered.shape
    tr = _tile(r, 512, 8)

    def body(g_ref, o_ref):
        acc = g_ref[0]
        for dev in range(1, N_DEV):
            acc = acc + g_ref[dev]
        o_ref[...] = acc

    return pl.pallas_call(
        body, name="sum8", out_shape=_sds((r, LANES)), grid=(r // tr,),
        in_specs=[pl.BlockSpec((N_DEV, tr, LANES), lambda i: (0, i, 0))], out_specs=pl.BlockSpec((tr, LANES), lambda i: (i, 0)),
        compiler_params=_params(("parallel",)),
    )(gathered)


_SUM_STEPS = 2


def pair_sums(gs, recvs, core, name):
    n = len(gs)

    def body(c_ref, *refs):
        del c_ref
        for i in range(n):
            a_ref, b_ref, o_ref = refs[2 * i], refs[2 * i + 1], refs[2 * n + i]
            o_ref[...] = (a_ref[...].astype(F32) + b_ref[...].astype(F32)).astype(BF16)

    in_specs, out_specs = [], []
    for g in gs:
        half = (None, g.shape[1] // 2, g.shape[2])
        in_specs.append(pl.BlockSpec(half, lambda k, c: (k, c[0], 0)))
        in_specs.append(pl.BlockSpec(half, lambda k, c: (k, 0, 0)))
        out_specs.append(pl.BlockSpec(half, lambda k, c: (k, 0, 0)))
    grid_spec = pltpu.PrefetchScalarGridSpec(num_scalar_prefetch=1, grid=(N_CHIPS,), in_specs=in_specs,
                                             out_specs=tuple(out_specs))
    return list(pl.pallas_call(
        body, name=name, out_shape=tuple(_sds((N_CHIPS, g.shape[1] // 2, g.shape[2]), BF16) for g in gs),
        grid_spec=grid_spec, compiler_params=_params(("parallel",), VMEM_BIG),
    )(core.reshape(1).astype(jnp.int32), *[t for pair in zip(gs, recvs) for t in pair]))


def chip_sums(pairs, recvs, chip, core, name):
    n = len(pairs)
    trs = [p.shape[1] // _SUM_STEPS for p in pairs]

    def body(p_ref, *refs):
        del p_ref
        for i in range(n):
            own_ref, r_ref, o_ref = refs[2 * i], refs[2 * i + 1], refs[2 * n + i]
            acc = own_ref[...].astype(F32)
            for j in range(N_CHIPS - 1):
                acc = acc + r_ref[j].astype(F32)
            o_ref[...] = acc

    in_specs, out_specs = [], []
    for p, tr in zip(pairs, trs):
        cols = p.shape[2]
        in_specs.append(pl.BlockSpec((None, tr, cols), lambda s, q: (q[0], s, 0)))
        in_specs.append(pl.BlockSpec((N_CHIPS - 1, tr, cols), lambda s, q: (0, s, 0)))
        out_specs.append(pl.BlockSpec((None, tr, cols), lambda s, q: (q[1], s, 0)))
    grid_spec = pltpu.PrefetchScalarGridSpec(num_scalar_prefetch=1, grid=(_SUM_STEPS,), in_specs=in_specs,
                                             out_specs=tuple(out_specs))
    return list(pl.pallas_call(
        body, name=name, out_shape=tuple(_sds((2,) + p.shape[1:]) for p in pairs), grid_spec=grid_spec,
        compiler_params=_params(("parallel",)),
    )(jnp.stack([chip, core]).astype(jnp.int32), *[t for pair in zip(pairs, recvs) for t in pair]))


def _place():
    return lax.axis_index("x"), lax.axis_index("y"), lax.axis_index("c")


def _other_chips(x, y):
    return [(x, 1 - y), (1 - x, y), (1 - x, 1 - y)]


_HBM = pl.BlockSpec(memory_space=pltpu.HBM)


def all_gather8(v, name, after=()):
    m, n = v.shape

    def body(x_ref, *refs):
        out_ref, send_sems, recv_sems, local_sem = refs[len(after):]
        x, y, c = _place()
        me, sibling = (x, y, c), (x, y, 1 - c)
        chips = _other_chips(x, y)

        def rows(px, py, pc):
            return out_ref.at[pl.ds((4 * px + 2 * py + pc) * m, m), :]

        def copy(k, block, to, src=None):
            return pltpu.make_async_remote_copy(
                src_ref=rows(*block) if src is None else src, dst_ref=rows(*block),
                send_sem=send_sems.at[k], recv_sem=recv_sems.at[k], device_id=to, device_id_type=MESH)

        mine = pltpu.make_async_copy(x_ref, rows(*me), local_sem)
        mine.start()
        first = [copy(0, me, sibling, src=x_ref)]
        first += [copy(1 + j, me, (*chip, c), src=x_ref) for j, chip in enumerate(chips)]
        for cp in first:
            cp.start()
        passed = [copy(4 + j, (*chip, c), sibling) for j, chip in enumerate(chips)]
        for j, chip in enumerate(chips):
            copy(1 + j, (*chip, c), me).wait_recv()
            passed[j].start()
        copy(0, sibling, me).wait_recv()
        for j, chip in enumerate(chips):
            copy(4 + j, (*chip, 1 - c), me).wait_recv()
        for cp in first + passed:
            cp.wait_send()
        mine.wait()

    return pl.pallas_call(
        body, name=name, out_shape=_sds((N_DEV * m, n), v.dtype),
        in_specs=[pl.BlockSpec(memory_space=pltpu.VMEM)] + [pl.BlockSpec(memory_space=pl.ANY)] * len(after),
        out_specs=pl.BlockSpec(memory_space=pltpu.VMEM),
        scratch_shapes=[pltpu.SemaphoreType.DMA((7,)), pltpu.SemaphoreType.DMA((7,)), pltpu.SemaphoreType.DMA],
        compiler_params=_params(None, VMEM_BIG),
    )(v, *after)


def _comm_call(body, name, ins, out_shapes, nsem, aliases=None):
    return pl.pallas_call(
        body, name=name, out_shape=tuple(out_shapes), in_specs=[_HBM] * len(ins), out_specs=tuple([_HBM] * len(out_shapes)),
        scratch_shapes=[pltpu.SemaphoreType.DMA((nsem,)), pltpu.SemaphoreType.DMA((nsem,))],
        input_output_aliases=aliases or {},
    )(*ins)


def _remote(src, dst, send_sems, recv_sems, k, to):
    return pltpu.make_async_remote_copy(src_ref=src, dst_ref=dst, send_sem=send_sems.at[k], recv_sem=recv_sems.at[k],
                                        device_id=to, device_id_type=MESH)


def _half(core, rh):
    return pl.ds(pl.multiple_of(core * rh, 16), rh)


def swap_halves(gs, name):
    n = len(gs)

    def body(*refs):
        ins, outs, (send_sems, recv_sems) = refs[:n], refs[n:2 * n], refs[2 * n:]
        x, y, c = _place()
        copies = []
        for i in range(n):
            theirs = _half(1 - c, ins[i].shape[1] // 2)
            cp = _remote(ins[i].at[:, theirs], outs[i], send_sems, recv_sems, i, (x, y, 1 - c))
            cp.start()
            copies.append(cp)
        for cp in copies:
            cp.wait()

    return _comm_call(body, name, gs, [_sds((g.shape[0], g.shape[1] // 2, g.shape[2]), g.dtype) for g in gs], n)


def join_halves(bufs, name):
    n = len(bufs)

    def body(*refs):
        ins, outs, (send_sems, recv_sems) = refs[:n], refs[n:2 * n], refs[2 * n:]
        x, y, c = _place()
        copies = []
        for i in range(n):
            cp = _remote(ins[i].at[c], outs[i].at[c], send_sems, recv_sems, i, (x, y, 1 - c))
            cp.start()
            copies.append(cp)
        for i in range(n):
            theirs = outs[i].at[1 - c]
            _remote(theirs, theirs, send_sems, recv_sems, i, (x, y, 1 - c)).wait_recv()
        for cp in copies:
            cp.wait_send()

    return _comm_call(body, name, bufs, [_sds(b.shape, b.dtype) for b in bufs], n, {i: i for i in range(n)})


def forward_halves(lands, name):
    n = len(lands)

    def body(*refs):
        ins, outs, (send_sems, recv_sems) = refs[:n], refs[n:2 * n], refs[2 * n:]
        x, y, c = _place()
        sibling = (x, y, 1 - c)
        chips = _other_chips(x, y)
        copies = []
        for i in range(n):
            mine = _half(c, ins[i].shape[1] // 2)
            for j, (px, py) in enumerate(chips):
                cp = _remote(ins[i].at[2 * px + py, mine], outs[i].at[2 * px + py, mine], send_sems, recv_sems, 3 * i + j, sibling)
                cp.start()
                copies.append(cp)
        for i in range(n):
            theirs = _half(1 - c, ins[i].shape[1] // 2)
            for j, (px, py) in enumerate(chips):
                landed = outs[i].at[2 * px + py, theirs]
                _remote(landed, landed, send_sems, recv_sems, 3 * i + j, sibling).wait_recv()
        for cp in copies:
            cp.wait_send()

    return _comm_call(body, name, lands, [_sds(b.shape, b.dtype) for b in lands], 3 * n, {i: i for i in range(n)})


_SEM = pl.BlockSpec(memory_space=pltpu.SEMAPHORE)
_EFFECT = pltpu.SideEffectType.DATAFLOW_SIDE_EFFECTING


def _gather_copies(srcs, lands, send_sems, recv_sems):
    x, y, c = _place()
    copies = []
    for i in range(len(srcs)):
        mine = _half(c, srcs[i].shape[0] // 2)
        for j, chip in enumerate(_other_chips(x, y)):
            copies.append(_remote(srcs[i].at[mine], lands[i].at[2 * x + y, mine], send_sems, recv_sems, 3 * i + j, (*chip, c)))
    return copies


def _exchange_copies(srcs, lands, send_sems, recv_sems):
    x, y, c = _place()
    copies = []
    for i in range(len(srcs)):
        for j, (px, py) in enumerate(_other_chips(x, y)):
            copies.append(_remote(srcs[i].at[2 * px + py], lands[i].at[j], send_sems, recv_sems, 3 * i + j, (px, py, c)))
    return copies


def _everyone_copies(srcs, lands, send_sems, recv_sems):
    x, y, c = _place()
    flip = lambda v, b: 1 - v if b else v
    dst = lands[0].at[4 * x + 2 * y + c]
    return [_remote(srcs[0], dst, send_sems, recv_sems, j - 1, (flip(x, j & 4), flip(y, j & 2), flip(c, j & 1)))
            for j in range(1, N_DEV)]


GATHER = (_gather_copies, 3)
EXCHANGE = (_exchange_copies, 3)
EVERYONE = (_everyone_copies, N_DEV - 1)


def split_start(name, plan, srcs, land_shapes, after=()):
    copies_fn, per_source = plan
    n, m, k = len(srcs), len(land_shapes), len(after)
    ncopies = per_source * n

    def body(*refs):
        src_refs, land_refs = refs[:n], refs[n:n + m]
        send_sems, recv_sems = refs[n + m + k], refs[n + m + k + 1]
        token = refs[-1]
        for cp in copies_fn(src_refs, land_refs, send_sems, recv_sems):
            cp.start()
        token[...] = jnp.zeros_like(token)

    hbm = lambda s: pltpu.HBM(tuple(s.shape), s.dtype)
    outs = pl.pallas_call(
        body, name=name,
        out_shape=(pltpu.SemaphoreType.DMA((ncopies,)), pltpu.SemaphoreType.DMA((ncopies,)), *[hbm(s) for s in srcs],
                   *[hbm(s) for s in land_shapes], _sds((8, LANES))),
        in_specs=[_HBM] * (n + m) + [pl.BlockSpec(memory_space=pl.ANY)] * k,
        out_specs=(_SEM, _SEM, *([_HBM] * (n + m)), pl.BlockSpec(memory_space=pltpu.VMEM)),
        input_output_aliases={i: 2 + i for i in range(n + m)},
        compiler_params=pltpu.CompilerParams(has_side_effects=_EFFECT),
    )(*[pltpu.with_memory_space_constraint(s, pltpu.HBM) for s in srcs],
      *[pltpu.with_memory_space_constraint(lax.empty(tuple(s.shape), s.dtype), pltpu.HBM) for s in land_shapes], *after)
    handle = (outs[0], outs[1], list(outs[2:2 + n]), list(outs[2 + n:2 + n + m]))
    return handle, outs[-1][0, 0]


def split_wait(name, plan, handle, after):
    copies_fn, _ = plan
    send_sems, recv_sems, srcs, lands = handle
    n, m = len(srcs), len(lands)
    after = list(after) if isinstance(after, (list, tuple)) else [after]

    def body(*refs):
        src_refs, land_refs = refs[:n], refs[n:n + m]
        for cp in copies_fn(src_refs, land_refs, refs[n + m], refs[n + m + 1]):
            cp.wait_send()
            cp.wait_recv()

    hbm = lambda s: pltpu.HBM(tuple(s.shape), s.dtype)
    outs = pl.pallas_call(
        body, name=name, out_shape=tuple(hbm(s) for s in srcs + lands),
        in_specs=[_HBM] * (n + m) + [_SEM, _SEM] + [pl.BlockSpec(memory_space=pl.ANY)] * len(after),
        out_specs=tuple([_HBM] * (n + m)), input_output_aliases={i: i for i in range(n + m)},
        compiler_params=pltpu.CompilerParams(has_side_effects=_EFFECT),
    )(*srcs, *lands, send_sems, recv_sems, *after)
    return list(outs[:n]), list(outs[n:])


def chip_major(w, groups=N_CHIPS):
    r, c = w.shape
    return w.reshape(r, groups, c // groups).transpose(1, 0, 2)


def from_chip_major(w):
    g, r, c = w.shape
    return w.transpose(1, 0, 2).reshape(r, g * c)


def _cd_in_pad(w):
    a = C_Q_RANK + C_KV_RANK
    z = lambda n: jnp.zeros((w.shape[0], n), w.dtype)
    return jnp.concatenate([w[:, :a], z(C_NOPE), w[:, a:a + C_ROPE], z(HEAD_PAD - C_NOPE - C_ROPE), w[:, a + C_ROPE:]], axis=1)


def _cd_in_unpad(w):
    a = C_Q_RANK + C_KV_RANK
    return jnp.concatenate([w[:, :a], w[:, a + C_NOPE:a + C_NOPE + C_ROPE], w[:, a + HEAD_PAD:]], axis=1)


def _pad_heads(w, width):
    r = w.shape[0]
    w = w.reshape(r, C_HEADS, width)
    return jnp.pad(w, ((0, 0), (0, 0), (0, HEAD_PAD - width))).reshape(r, _HW)


def _unpad_heads(w, width):
    r = w.shape[0]
    return w.reshape(r, C_HEADS, HEAD_PAD)[:, :, :width].reshape(r, C_HEADS * width)


def prepare_weights(p):
    q = dict(p)
    q["cd_w_in"] = _cd_in_pad(p["cd_w_in"])
    q["c_w_uq"] = _pad_heads(p["c_w_uq"], C_NOPE + C_ROPE)
    ukv = p["c_w_ukv"].reshape(C_KV_RANK, C_HEADS, C_NOPE + C_V)
    q["c_w_uk"] = _pad_heads(ukv[:, :, :C_NOPE].reshape(C_KV_RANK, -1), C_NOPE)
    q["c_w_uv"] = _pad_heads(ukv[:, :, C_NOPE:].reshape(C_KV_RANK, -1), C_V)
    wo = p["cd_w_out"]
    att_rows = jnp.pad(wo[:C_HEADS * C_V].reshape(C_HEADS, C_V, D_MODEL), ((0, 0), (0, HEAD_PAD - C_V), (0, 0)))
    q["cd_w_out"] = jnp.concatenate([att_rows.reshape(_HW, D_MODEL), wo[C_HEADS * C_V:]], axis=0)
    return q


def unprepare_grads(g):
    q = dict(g)
    q["cd_w_in"] = _cd_in_unpad(g["cd_w_in"])
    q["c_w_uq"] = _unpad_heads(g["c_w_uq"], C_NOPE + C_ROPE)
    uk = g.pop("c_w_uk").reshape(C_KV_RANK, C_HEADS, HEAD_PAD)[:, :, :C_NOPE]
    uv = g.pop("c_w_uv").reshape(C_KV_RANK, C_HEADS, HEAD_PAD)[:, :, :C_V]
    q.pop("c_w_uk", None)
    q.pop("c_w_uv", None)
    q["c_w_ukv"] = jnp.concatenate([uk, uv], axis=-1).reshape(C_KV_RANK, C_HEADS * (C_NOPE + C_V))
    wo = g["cd_w_out"]
    att = wo[:_HW].reshape(C_HEADS, HEAD_PAD, D_MODEL)[:, :C_V].reshape(C_HEADS * C_V, D_MODEL)
    q["cd_w_out"] = jnp.concatenate([att, wo[_HW:]], axis=0)
    return q


def rope_tables(positions):
    half = C_ROPE // 2
    inv_freq = ROPE_THETA ** (-jnp.arange(half, dtype=F32) / half)
    ang = positions.astype(F32)[:, None] * inv_freq
    cos, sin = jnp.cos(ang), jnp.sin(ang)
    s = positions.shape[0]
    z = lambda n: jnp.zeros((s, n), F32)
    cs = jnp.concatenate([jnp.ones((s, C_NOPE), F32), cos, cos, z(HEAD_PAD - C_NOPE - C_ROPE)], axis=1)
    s1 = jnp.concatenate([z(C_NOPE), -sin, z(HEAD_PAD - C_NOPE - half)], axis=1)
    s2 = jnp.concatenate([z(C_NOPE + half), sin, z(HEAD_PAD - C_NOPE - C_ROPE)], axis=1)
    return cs, s1, s2


_UP_COLS = 2 * D_FF // N_CHIPS


def ffn_fwd(h2, w, l, late_down=None):
    zf = matmul(h2, w["ffn_w_up"][l], "nn", BF16, f"ffn_up{l}", gb=N_CHIPS, go=2, tn=_UP_COLS)
    if late_down is not None:
        late_down(zf)
    a, f = ffn_act_down(zf, w["ffn_conv_w"][l], w["ffn_w_down"][l], f"ffn_act_down{l}")
    return f, (zf, a)


def ffn_bwd(df, h2, saved, w, l):
    zf, a = saved
    da = matmul(df, w["ffn_w_down"][l], "nt", BF16, f"ffn_down_dx{l}", tn=D_FF // 2)
    d_down = matmul(a, df, "tn", BF16, f"ffn_down_dw{l}", tm=D_FF // 2)
    dzf, d_conv = ffn_act_bwd(zf, w["ffn_conv_w"][l], da, f"ffn_act_bwd{l}")
    dh2 = matmul(dzf, w["ffn_w_up"][l], "nt", BF16, f"ffn_up_dx{l}", ga=2, gb=N_CHIPS, tk=_UP_COLS, tn=D_MODEL)
    d_up = matmul(h2, dzf, "tn", BF16, f"ffn_up_dw{l}", gb=2, go=N_CHIPS, tn=_UP_COLS)
    d_conv = d_conv.transpose(1, 0, 2).reshape(3, 2 * D_FF)
    return dh2, dict(ffn_w_down=d_down, ffn_conv_w=d_conv, ffn_w_up=d_up)


def mixer0_fwd(h, w):
    z = matmul(h, w["ab_w_in"], "nn", BF16, "ab_in", gb=N_CHIPS)
    ycat = pool_fwd(z, w["b_mix_w"], w["b_scale"], gconv_fwd(z, w["a_conv_w"]))
    y = matmul(ycat, w["ab_w_out"], "nn", BF16, "ab_out", tn=D_MODEL)
    return y, (z, ycat)


def mixer0_bwd(dy, h, saved, w):
    z, ycat = saved
    grads = {}
    dycat = matmul(dy, w["ab_w_out"], "nt", BF16, "ab_out_dx")
    grads["ab_w_out"] = matmul(ycat, dy, "tn", BF16, "ab_out_dw")
    db, dc, da, d_conv = gconv_bwd(z, w["a_conv_w"], dycat)
    dp, d_mix, d_scale = pool_bwd(z, w["b_mix_w"], w["b_scale"], dycat)
    dz = jnp.concatenate([db, dc, da, dp], axis=1)
    dh = matmul(dz, w["ab_w_in"], "nt", BF16, "ab_in_dx", gb=N_CHIPS, tn=D_MODEL)
    grads["ab_w_in"] = matmul(h, dz, "tn", BF16, "ab_in_dw", go=N_CHIPS)
    grads.update(a_conv_w=d_conv, b_mix_w=d_mix, b_scale=d_scale)
    return dh, grads


def mixer1_fwd(h, ropes, w):
    cs, s1, s2 = ropes
    z = matmul(h, w["cd_w_in"], "nn", BF16, "cd_in")
    bs_t = jnp.pad(w["d_b_s"].T, ((0, 0), (0, LANES - D_GROUPS)))
    qh, kh, vh = mla_pre_fwd(z, w["c_q_norm_g"], w["c_kv_norm_g"], w["c_w_uq"], w["c_w_uk"], w["c_w_uv"], cs, s1, s2)
    ycat = sgu_fwd(z, w["d_ln_g"], w["d_ln_b"], w["d_w_s"], bs_t, attn_fwd(qh, kh, vh))
    y = matmul(ycat, w["cd_w_out"], "nn", BF16, "cd_out", tn=D_MODEL)
    return y, (z, bs_t, qh, kh, vh, ycat)


def mixer1_bwd(dy, h, saved, ropes, w):
    cs, s1, s2 = ropes
    z, bs_t, qh, kh, vh, ycat = saved
    grads = {}
    dycat = matmul(dy, w["cd_w_out"], "nt", BF16, "cd_out_dx")
    grads["cd_w_out"] = matmul(ycat, dy, "tn", BF16, "cd_out_dw")
    dqh, dkh, dvh = attn_bwd(qh, kh, vh, ycat, dycat)
    dzq, d_uq, d_uk, d_uv, d_gq, d_gkv = mla_pre_bwd(
        z, w["c_q_norm_g"], w["c_kv_norm_g"], w["c_w_uq"], w["c_w_uk"], w["c_w_uv"], cs, s1, s2, dqh, dkh, dvh)
    dzu, dzv, d_ws, d_bs, d_lg, d_lb = sgu_bwd(z, w["d_ln_g"], w["d_ln_b"], w["d_w_s"], bs_t, dycat, _HW // _DW)
    dz = jnp.concatenate([dzq, dzu, dzv], axis=1)
    dh = matmul(dz, w["cd_w_in"], "nt", BF16, "cd_in_dx", tn=D_MODEL)
    grads["cd_w_in"] = matmul(h, dz, "tn", BF16, "cd_in_dw")
    grads.update(c_w_uq=d_uq, c_w_uk=d_uk, c_w_uv=d_uv, c_q_norm_g=d_gq, c_kv_norm_g=d_gkv, d_w_s=d_ws,
                 d_b_s=d_bs[:, :D_GROUPS].T, d_ln_g=d_lg, d_ln_b=d_lb)
    return dh, grads


class StepHooks:
    def weights(self, stage, after):
        pass

    def gradients(self, stage, grads, after):
        return 0.0


def run_step(x, tgt, mod, ropes, w, hooks):
    sh1, sc1, g1, sh2, sc2, g2 = range(N_MOD)
    mods = mod.reshape(2, 1, N_MOD * D_MODEL)
    n1 = w["norm1_g"].reshape(2, 1, D_MODEL)
    n2 = w["norm2_g"].reshape(2, 1, D_MODEL)
    final_g = Vec(w["final_norm_g"].reshape(1, 1, D_MODEL), 0, 0)

    hooks.weights("mix0", mod)
    h0 = modnorm_fwd(x, Vec(n1, 0, 0), Vec(mods, 0, sc1), Vec(mods, 0, sh1), "modnorm_0")
    y0, mix0 = mixer0_fwd(h0, w)
    x1, h1 = resid_modnorm_fwd(x, y0, Vec(mods, 0, g1), Vec(n2, 0, 0), Vec(mods, 0, sc2), Vec(mods, 0, sh2), "resid_modnorm_1")
    hooks.weights("up0", x1)
    f0, ffn0 = ffn_fwd(h1, w, 0, lambda act: hooks.weights("down0", act))
    x2, h2 = resid_modnorm_fwd(x1, f0, Vec(mods, 0, g2), Vec(n1, 1, 0), Vec(mods, 1, sc1), Vec(mods, 1, sh1), "resid_modnorm_2")
    hooks.weights("mix1", x2)
    y1, mix1 = mixer1_fwd(h2, ropes, w)
    x3, h3 = resid_modnorm_fwd(x2, y1, Vec(mods, 1, g1), Vec(n2, 1, 0), Vec(mods, 1, sc2), Vec(mods, 1, sh2), "resid_modnorm_3")
    hooks.weights("ffn1", x3)
    f1, ffn1 = ffn_fwd(h3, w, 1)
    dres, d_final, loss, df1, dg2b = final_fused(x3, f1, Vec(mods, 1, g2), final_g, tgt)

    dh3, gf1 = ffn_bwd(df1, h3, ffn1, w, 1)
    late = mods + hooks.gradients("ffn1", gf1, dh3)
    dres, dsh2b, dsc2b, dn2b, dy1, dg1b = norm_gate_bwd(
        x3, dh3, Vec(n2, 1, 0), Vec(late, 1, sc2), dres, y1, Vec(late, 1, g1), "norm_gate_bwd_3")
    dh2, gm1 = mixer1_bwd(dy1, h2, mix1, ropes, w)
    late = mods + hooks.gradients("mix1", gm1, dh2)
    dres, dsh1b, dsc1b, dn1b, df0, dg2a = norm_gate_bwd(
        x2, dh2, Vec(n1, 1, 0), Vec(late, 1, sc1), dres, f0, Vec(late, 0, g2), "norm_gate_bwd_2")
    dh1, gf0 = ffn_bwd(df0, h1, ffn0, w, 0)
    late = mods + hooks.gradients("ffn0", gf0, dh1)
    dres, dsh2a, dsc2a, dn2a, dy0, dg1a = norm_gate_bwd(
        x1, dh1, Vec(n2, 0, 0), Vec(late, 0, sc2), dres, y0, Vec(late, 0, g1), "norm_gate_bwd_1")
    dh0, gm0 = mixer0_bwd(dy0, h0, mix0, w)
    late = mods + hooks.gradients("mix0", gm0, dh0)
    grad_x, dsh1a, dsc1a, dn1a = norm_bwd(x, dh0, Vec(n1, 0, 0), Vec(late, 0, sc1), dres, "norm_bwd_0")

    dmod = jnp.concatenate([jnp.concatenate([dsh1a, dsc1a, dg1a, dsh2a, dsc2a, dg2a], axis=1),
                            jnp.concatenate([dsh1b, dsc1b, dg1b, dsh2b, dsc2b, dg2b], axis=1)], axis=0)
    norms = dict(norm1_g=jnp.concatenate([dn1a, dn1b], axis=0), norm2_g=jnp.concatenate([dn2a, dn2b], axis=0),
                 final_norm_g=d_final)
    return loss, grad_x, dmod, dict(mix0=gm0, ffn0=gf0, mix1=gm1, ffn1=gf1, norms=norms)


def merge_grads(by_stage):
    grads = {**by_stage["mix0"], **by_stage["mix1"], **by_stage["norms"]}
    for k in ("ffn_w_down", "ffn_w_up"):
        grads[k] = [by_stage["ffn0"][k], by_stage["ffn1"][k]]
    grads["ffn_conv_w"] = jnp.stack([by_stage["ffn0"]["ffn_conv_w"], by_stage["ffn1"]["ffn_conv_w"]])
    return grads


_WEIGHTS = ("ada_w", "ada_b", "norm1_g", "norm2_g", "ab_w_in", "a_conv_w", "b_mix_w", "b_scale", "ab_w_out", "cd_w_in",
            "c_q_norm_g", "c_w_uq", "c_kv_norm_g", "c_w_ukv", "d_ln_g", "d_ln_b", "d_w_s", "d_b_s", "cd_w_out",
            "ffn_w_up", "ffn_conv_w", "ffn_w_down", "final_norm_g")
_INPUTS = ("x", "c", "positions") + _WEIGHTS + ("loss_target",) + tuple("m_" + n for n in _WEIGHTS) + tuple(
    "v_" + n for n in _WEIGHTS)

def _pack_rows(parts, rows, dtype):
    flat = jnp.concatenate([p.reshape(-1).astype(dtype) for p in parts])
    return jnp.pad(flat, (0, rows * LANES - flat.shape[0])).reshape(rows, LANES)


def _rows_major(w):
    r, c = w.shape
    return w.reshape(N_CHIPS, r // N_CHIPS, c)


def start_gather(shards, tag, after=()):
    lands = [_sds((N_CHIPS,) + s.shape, s.dtype) for s in shards]
    return split_start("gather_start_" + tag, GATHER, shards, lands, after)


def finish_gather(handle, chip, tag, after):
    shards, lands = split_wait("gather_wait_" + tag, GATHER, handle, after)
    lands = forward_halves(lands, "gather_forward_" + tag)
    return [lax.dynamic_update_index_in_dim(o, s, chip, 0) for o, s in zip(lands, shards)]


def start_reduce(gs, core, tag):
    recv = swap_halves(gs, "swap_halves_" + tag)
    pairs = pair_sums(gs, recv, core, "pair_sums_" + tag)
    lands = [_sds((N_CHIPS - 1,) + p.shape[1:], p.dtype) for p in pairs]
    return split_start("exchange_start_" + tag, EXCHANGE, pairs, lands)


def finish_reduce(handle, chip, core, tag, after):
    pairs, others = split_wait("exchange_wait_" + tag, EXCHANGE, handle, after)
    halves = chip_sums(pairs, others, chip, core, "chip_sums_" + tag)
    full = join_halves(halves, "join_halves_" + tag)
    return [f.reshape(f.shape[1] * 2, f.shape[2]) for f in full]


_SMALL_SHARDED = (("a_conv_w", (3, 128), 1), ("c_q_norm_g", (1, 64), 1), ("d_ln_g", (1, 128), 1), ("d_ln_b", (1, 128), 1),
                  ("ffn_conv_w", (2, 3, 2 * D_FF // N_CHIPS), 2))
_SMALL_GRADS = (("norm1_g", (2, D_MODEL)), ("norm2_g", (2, D_MODEL)), ("b_mix_w", (4, 128, 128)), ("b_scale", (1, 512)),
                ("c_kv_norm_g", (1, 128)), ("d_w_s", (4, 128, 128)), ("d_b_s", (4, 128)), ("final_norm_g", (1, D_MODEL)),
                ("a_conv_w", (3, 512)), ("c_q_norm_g", (1, 256)), ("d_ln_g", (1, 512)), ("d_ln_b", (1, 512)),
                ("ffn_conv_w", (2, 3, 2 * D_FF)))


def _size(shape):
    n = 1
    for d in shape:
        n *= d
    return n


def kernel(x, c, positions, ada_w, ada_b, norm1_g, norm2_g, ab_w_in, a_conv_w, b_mix_w, b_scale, ab_w_out, cd_w_in, c_q_norm_g, c_w_uq, c_kv_norm_g, c_w_ukv, d_ln_g, d_ln_b, d_w_s, d_b_s, cd_w_out, ffn_w_up, ffn_conv_w, ffn_w_down, final_norm_g, loss_target, m_ada_w, m_ada_b, m_norm1_g, m_norm2_g, m_ab_w_in, m_a_conv_w, m_b_mix_w, m_b_scale, m_ab_w_out, m_cd_w_in, m_c_q_norm_g, m_c_w_uq, m_c_kv_norm_g, m_c_w_ukv, m_d_ln_g, m_d_ln_b, m_d_w_s, m_d_b_s, m_cd_w_out, m_ffn_w_up, m_ffn_conv_w, m_ffn_w_down, m_final_norm_g, v_ada_w, v_ada_b, v_norm1_g, v_norm2_g, v_ab_w_in, v_a_conv_w, v_b_mix_w, v_b_scale, v_ab_w_out, v_cd_w_in, v_c_q_norm_g, v_c_w_uq, v_c_kv_norm_g, v_c_w_ukv, v_d_ln_g, v_d_ln_b, v_d_w_s, v_d_b_s, v_cd_w_out, v_ffn_w_up, v_ffn_conv_w, v_ffn_w_down, v_final_norm_g):
    args = (x, c, positions, ada_w, ada_b, norm1_g, norm2_g, ab_w_in, a_conv_w, b_mix_w, b_scale, ab_w_out, cd_w_in, c_q_norm_g, c_w_uq, c_kv_norm_g, c_w_ukv, d_ln_g, d_ln_b, d_w_s, d_b_s, cd_w_out, ffn_w_up, ffn_conv_w, ffn_w_down, final_norm_g, loss_target, m_ada_w, m_ada_b, m_norm1_g, m_norm2_g, m_ab_w_in, m_a_conv_w, m_b_mix_w, m_b_scale, m_ab_w_out, m_cd_w_in, m_c_q_norm_g, m_c_w_uq, m_c_kv_norm_g, m_c_w_ukv, m_d_ln_g, m_d_ln_b, m_d_w_s, m_d_b_s, m_cd_w_out, m_ffn_w_up, m_ffn_conv_w, m_ffn_w_down, m_final_norm_g, v_ada_w, v_ada_b, v_norm1_g, v_norm2_g, v_ab_w_in, v_a_conv_w, v_b_mix_w, v_b_scale, v_ab_w_out, v_cd_w_in, v_c_q_norm_g, v_c_w_uq, v_c_kv_norm_g, v_c_w_ukv, v_d_ln_g, v_d_ln_b, v_d_w_s, v_d_b_s, v_cd_w_out, v_ffn_w_up, v_ffn_conv_w, v_ffn_w_down, v_final_norm_g)
    a = dict(zip(_INPUTS, args, strict=True))
    xi, yi, ci = _place()
    chip = 2 * xi + yi
    dev = 4 * xi + 2 * yi + ci
    x = a["x"][0]
    tgt = a["loss_target"][0]

    bf = lambda t: t.astype(BF16)
    mix0_handle, tok = start_gather([bf(a["ab_w_in"][0]), bf(a["ab_w_out"][0])], "mix0")
    up0_16, down0_16, up1_16, down1_16 = [bf(a[n][l]) for l in (0, 1) for n in ("ffn_w_up", "ffn_w_down")]
    mix1_16 = [bf(a[n][0]) for n in ("cd_w_in", "c_w_uq", "c_w_ukv", "cd_w_out")]

    small_parts = [a["c"] + tok] + [a[n] for n, _, _ in _SMALL_SHARDED]
    rows1 = -(-sum(p.size for p in small_parts) // LANES // 8) * 8
    g1 = all_gather8(_pack_rows(small_parts, rows1, F32), "gather_small",
                     [up0_16, down0_16, up1_16, down1_16, mix1_16[0], mix1_16[3]]).reshape(N_DEV, rows1 * LANES)
    c_all = g1[:, :D_MODEL]
    per_chip = g1[0::2]
    small_full = {}
    off = D_MODEL
    for n, shp, axis in _SMALL_SHARDED:
        piece = per_chip[:, off:off + _size(shp)].reshape((N_CHIPS,) + shp)
        small_full[n] = jnp.concatenate([piece[k] for k in range(N_CHIPS)], axis=axis)
        off += _size(shp)

    merge = lambda t: t.reshape(t.shape[0] * t.shape[1], t.shape[2])
    w = dict(norm1_g=a["norm1_g"], norm2_g=a["norm2_g"], b_mix_w=a["b_mix_w"][0], b_scale=a["b_scale"],
             c_kv_norm_g=a["c_kv_norm_g"], d_w_s=a["d_w_s"][0], d_b_s=a["d_b_s"][0],
             final_norm_g=a["final_norm_g"].reshape(1, D_MODEL), **small_full)

    ncol = N_MOD * D_MODEL // N_CHIPS
    ada_b_mine = lax.dynamic_slice_in_dim(a["ada_b"], chip * ncol, ncol, axis=1)
    mod_cols = ada_mod(c_all, a["ada_w"], ada_b_mine)
    g2_rows = all_gather8(mod_cols.reshape(-1, LANES), "gather_mod")
    g2 = g2_rows.reshape(N_DEV, 2, N_DEV, ncol)
    mod = lax.dynamic_index_in_dim(g2[0::2], dev, axis=2, keepdims=False)
    mod = mod.transpose(1, 0, 2).reshape(2, N_MOD * D_MODEL)

    late = [g2_rows]
    up0_handle, tok_a = start_gather([up0_16], "up0", late)
    down0_handle, tok_b = start_gather([down0_16], "down0", late)
    mix1_handle, tok_c = start_gather(mix1_16, "mix1", late)
    ffn1_handle, tok_d = start_gather([up1_16, down1_16], "ffn1", late)
    mod = mod + (tok_a + tok_b + tok_c + tok_d)

    ropes = rope_tables(a["positions"][0])
    cm16 = lambda t: chip_major(t).astype(BF16)
    w.update(ffn_w_up=[None, None], ffn_w_down=[None, None])
    handles = dict(mix0=mix0_handle, up0=up0_handle, down0=down0_handle, mix1=mix1_handle, ffn1=ffn1_handle)
    reducing, reduced = {}, {}

    class Hooks(StepHooks):
        def weights(self, stage, after):
            got = finish_gather(handles[stage], chip, stage, after)
            if stage == "mix0":
                w.update(ab_w_in=got[0], ab_w_out=merge(got[1]))
            elif stage == "up0":
                w["ffn_w_up"][0] = got[0]
            elif stage == "down0":
                w["ffn_w_down"][0] = merge(got[0])
            elif stage == "mix1":
                cd_in, uq, ukv, cd_out = got
                w.update(prepare_weights(dict(cd_w_in=from_chip_major(cd_in), c_w_uq=from_chip_major(uq),
                                              c_w_ukv=from_chip_major(ukv), cd_w_out=merge(cd_out))))
            else:
                w["ffn_w_up"][1], w["ffn_w_down"][1] = got[0], merge(got[1])

        def gradients(self, stage, grads, after):
            if stage in ("ffn0", "ffn1"):
                parts = [grads["ffn_w_up"], _rows_major(grads["ffn_w_down"])]
            elif stage == "mix1":
                grads.update(unprepare_grads(grads))
                parts = [cm16(grads["cd_w_in"]), cm16(grads["c_w_uq"]), cm16(grads["c_w_ukv"]),
                         _rows_major(grads["cd_w_out"]).astype(BF16)]
            else:
                parts = [grads["ab_w_in"], _rows_major(grads["ab_w_out"])]
            reducing[stage], tok = start_reduce(parts, ci, stage)
            before = {"mix1": "ffn1", "ffn0": "mix1", "mix0": "ffn0"}.get(stage)
            if before is not None:
                reduced[before] = finish_reduce(reducing[before], chip, ci, before, after)
            return tok

    loss, grad_x, dmod, by_stage = run_step(x, tgt, mod, ropes, w, Hooks())
    grads = merge_grads(by_stage)

    parts3 = [dmod] + [grads[n] for n, _ in _SMALL_GRADS] + [loss[0, 0]]
    rows3 = -(-sum(p.size for p in parts3) // LANES // 8) * 8
    small_handle, _ = split_start("small_grads_start", EVERYONE, [_pack_rows(parts3, rows3, F32)],
                                  [_sds((N_DEV, rows3, LANES))])
    red_up1, red_down1 = reduced["ffn1"]
    red_cd_in, red_uq, red_ukv, red_cd_out = reduced["mix1"]
    red_up0, red_down0 = reduced["ffn0"]
    out_grads = dict(cd_w_in=red_cd_in, c_w_uq=red_uq, c_w_ukv=red_ukv, cd_w_out=red_cd_out)
    per_layer = dict(ffn_w_up=(red_up0, red_up1), ffn_w_down=(red_down0, red_down1))
    updates = {}

    def update(n):
        if n in per_layer:
            updates[n] = adamw_layers(a[n], *per_layer[n], a["m_" + n], a["v_" + n], "adamw_" + n)
        else:
            updates[n] = adamw(a[n], out_grads[n].reshape(a[n].shape), a["m_" + n], a["v_" + n], "adamw_" + n)

    early =("ffn_w_up", "ffn_w_down", "cd_w_in", "c_w_uq", "c_w_ukv", "cd_w_out")
    for n in early:
        update(n)
    (mine,), (landed,) = split_wait("small_grads_wait", EVERYONE, small_handle, [updates[n][1] for n in early])
    g3 = lax.dynamic_update_index_in_dim(landed, mine, dev, 0)
    summed = sum8(g3).reshape(-1)
    nmod = 2 * N_MOD * D_MODEL
    out_grads["ada_b"] = summed[:nmod].reshape(2, N_MOD * D_MODEL)
    off = nmod
    for n, shp in _SMALL_GRADS:
        out_grads[n] = summed[off:off + _size(shp)].reshape(shp)
        off += _size(shp)
    loss = summed[off]
    for n, shp, axis in _SMALL_SHARDED:
        width = out_grads[n].shape[-1] // N_CHIPS
        out_grads[n] = lax.dynamic_slice_in_dim(out_grads[n], chip * width, width, axis=out_grads[n].ndim - 1)
    dmod_all = g3.reshape(N_DEV, rows3 * LANES)[:, :nmod].reshape(N_DEV, 2, N_MOD * D_MODEL)
    dmod_mine = lax.dynamic_slice_in_dim(dmod_all, chip * ncol, ncol, axis=2).transpose(1, 0, 2)
    updates["ada_w"] = adamw_ada(a["ada_w"], c_all, dmod_mine, a["m_ada_w"], a["v_ada_w"])

    red_in0, red_out0 = finish_reduce(reducing["mix0"], chip, ci, "mix0", updates["ada_w"][1])
    out_grads.update(ab_w_in=red_in0, ab_w_out=red_out0)

    for n in ("ab_w_in", "ab_w_out"):
        update(n)
    small = [n for n in _WEIGHTS if n not in updates]
    for n, res in zip(small, adamw_small([a[n] for n in small], [out_grads[n].reshape(a[n].shape) for n in small],
                                         [a["m_" + n] for n in small], [a["v_" + n] for n in small])):
        updates[n] = res
    return (loss, grad_x[None], *[updates[n][i] for i in range(4) for n in _WEIGHTS])
```
